```python
import math
import jax, jax.numpy as jnp
from jax import lax
import numpy as np

D_MODEL = 1024
BATCH = 8
SEQ = 2048
DEPTH = 4

CHUNK = 64
N_MIXERS = 3
ALPHA = (2.0 * DEPTH) ** 0.25
BETA = (8.0 * DEPTH) ** -0.25
LN_EPS = 1e-5
RMS_EPS = 1e-6
SC_WIDTH = 3
MLA_HEADS = 8
QK_NOPE = 128
QK_ROPE = 64
V_HEAD = 128
Q_LORA = 3 * D_MODEL // 8
KV_LORA = D_MODEL // 4
ROPE_THETA = 10000.0
Q_BLOCK = 128
CONF_WIDTH = 31
D_FF = 4 * D_MODEL
N_A = len(range(0, DEPTH, N_MIXERS))
N_B = len(range(1, DEPTH, N_MIXERS))
N_C = len(range(2, DEPTH, N_MIXERS))

kernel_name = "hybrid_chunk_causal_deepnorm_trunk"


def _layer_norm(x, g, b):
    xf = x.astype(jnp.float32)
    mu = jnp.mean(xf, axis=-1, keepdims=True)
    var = jnp.mean(jnp.square(xf - mu), axis=-1, keepdims=True)
    y = (xf - mu) * lax.rsqrt(var + LN_EPS) * g.astype(jnp.float32) + b.astype(jnp.float32)
    return y.astype(x.dtype)


def _rms_norm(x, g):
    xf = x.astype(jnp.float32)
    y = xf * lax.rsqrt(jnp.mean(jnp.square(xf), axis=-1, keepdims=True) + RMS_EPS) * g.astype(jnp.float32)
    return y.astype(x.dtype)


def _causal_dwconv(x, w):
    k_width, c = w.shape
    return lax.conv_general_dilated(
        x, w[:, None, :].astype(x.dtype), window_strides=(1,), padding=[(k_width - 1, 0)],
        dimension_numbers=("NWC", "WIO", "NWC"), feature_group_count=c)


def _rope(x, cos, sin):
    x1, x2 = jnp.split(x, 2, axis=-1)
    c = cos[None, :, None, :].astype(x.dtype)
    s = sin[None, :, None, :].astype(x.dtype)
    return jnp.concatenate([x1 * c - x2 * s, x1 * s + x2 * c], axis=-1)


def _short_conv_mixer(x, w_in, conv_w, w_out):
    b_gate, c_gate, h = jnp.split(x @ w_in, 3, axis=-1)
    return (b_gate * _causal_dwconv(c_gate * h, conv_w)) @ w_out


def _mla_mixer(x, w_dq, g_q, w_uq, w_dkv, g_kv, w_uk, w_uv, w_o):
    bsz, seq, _ = x.shape
    pos = jnp.arange(seq, dtype=jnp.float32)
    inv_freq = ROPE_THETA ** (-jnp.arange(0, QK_ROPE, 2, dtype=jnp.float32) / QK_ROPE)
    ang = pos[:, None] * inv_freq[None, :]
    cos, sin = jnp.cos(ang), jnp.sin(ang)
    cq = _rms_norm(x @ w_dq, g_q)
    q = (cq @ w_uq).reshape(bsz, seq, MLA_HEADS, QK_NOPE + QK_ROPE)
    q_nope, q_pe = q[..., :QK_NOPE], _rope(q[..., QK_NOPE:], cos, sin)
    ckv_full = x @ w_dkv
    ckv = _rms_norm(ckv_full[..., :KV_LORA], g_kv)
    k_pe = _rope(ckv_full[..., None, KV_LORA:], cos, sin)[:, :, 0, :]
    k_nope = jnp.einsum("bsc,chd->bshd", ckv, w_uk)
    v = jnp.einsum("bsc,chd->bshd", ckv, w_uv)
    scale = (QK_NOPE + QK_ROPE) ** -0.5
    n_blk = seq // Q_BLOCK
    qn_b = q_nope.reshape(bsz, n_blk, Q_BLOCK, MLA_HEADS, QK_NOPE).transpose(1, 0, 2, 3, 4)
    qp_b = q_pe.reshape(bsz, n_blk, Q_BLOCK, MLA_HEADS, QK_ROPE).transpose(1, 0, 2, 3, 4)
    k_chunk = jnp.arange(seq) // CHUNK

    def block(args):
        qn, qp, blk = args
        s = (jnp.einsum("bqhd,bkhd->bhqk", qn, k_nope)
             + jnp.einsum("bqhd,bkd->bhqk", qp, k_pe)).astype(jnp.float32) * scale
        q_chunk = (blk * Q_BLOCK + jnp.arange(Q_BLOCK)) // CHUNK
        allowed = k_chunk[None, :] <= q_chunk[:, None]
        p = jax.nn.softmax(jnp.where(allowed[None, None], s, -jnp.inf), axis=-1).astype(v.dtype)
        return jnp.einsum("bhqk,bkhd->bqhd", p, v)

    o = lax.map(block, (qn_b, qp_b, jnp.arange(n_blk)))
    o = o.transpose(1, 0, 2, 3, 4).reshape(bsz, seq, MLA_HEADS * V_HEAD)
    return o @ w_o


def _conformer_conv_mixer(x, w_pw1, b_pw1, dw_w, dw_b, norm_g, norm_b, w_pw2, b_pw2):
    a, gate = jnp.split(x @ w_pw1 + b_pw1, 2, axis=-1)
    h = a * jax.nn.sigmoid(gate)
    h = _causal_dwconv(h, dw_w) + dw_b
    h = jax.nn.silu(_layer_norm(h, norm_g, norm_b))
    return h @ w_pw2 + b_pw2


def _sq_relu_mlp(x, w1, w2):
    return jnp.square(jax.nn.relu(x @ w1)) @ w2


def _normal(k, shape, fan_in, scale=1.0):
    return jax.random.normal(k, shape, jnp.float32) * (scale * fan_in ** -0.5)


def _fwd_setup_inputs(seed: int = 0) -> dict:
    key = jax.random.key(seed)
    ks = iter(jax.random.split(key, 40))
    D = D_MODEL
    gain = lambda shape: 1.0 + 0.02 * jax.random.normal(next(ks), shape, jnp.float32)
    bias = lambda shape: 0.02 * jax.random.normal(next(ks), shape, jnp.float32)
    return {
        "x": jax.random.normal(next(ks), (BATCH, SEQ, D), jnp.float32),
        "sc_w_in": _normal(next(ks), (N_A, D, 3 * D), D),
        "sc_conv_w": _normal(next(ks), (N_A, SC_WIDTH, D), SC_WIDTH),
        "sc_w_out": _normal(next(ks), (N_A, D, D), D, BETA),
        "mla_w_dq": _normal(next(ks), (N_B, D, Q_LORA), D),
        "mla_g_q": gain((N_B, Q_LORA)),
        "mla_w_uq": _normal(next(ks), (N_B, Q_LORA, MLA_HEADS * (QK_NOPE + QK_ROPE)), Q_LORA),
        "mla_w_dkv": _normal(next(ks), (N_B, D, KV_LORA + QK_ROPE), D),
        "mla_g_kv": gain((N_B, KV_LORA)),
        "mla_w_uk": _normal(next(ks), (N_B, KV_LORA, MLA_HEADS, QK_NOPE), KV_LORA),
        "mla_w_uv": _normal(next(ks), (N_B, KV_LORA, MLA_HEADS, V_HEAD), KV_LORA, BETA),
        "mla_w_o": _normal(next(ks), (N_B, MLA_HEADS * V_HEAD, D), MLA_HEADS * V_HEAD, BETA),
        "cf_w_pw1": _normal(next(ks), (N_C, D, 2 * D), D),
        "cf_b_pw1": bias((N_C, 2 * D)),
        "cf_dw_w": _normal(next(ks), (N_C, CONF_WIDTH, D), CONF_WIDTH),
        "cf_dw_b": bias((N_C, D)),
        "cf_norm_g": gain((N_C, D)),
        "cf_norm_b": bias((N_C, D)),
        "cf_w_pw2": _normal(next(ks), (N_C, D, D), D, BETA),
        "cf_b_pw2": bias((N_C, D)),
        "ff_w1": _normal(next(ks), (DEPTH, D, D_FF), D, BETA),
        "ff_w2": _normal(next(ks), (DEPTH, D_FF, D), D_FF, BETA),
        "ln_mix_g": gain((DEPTH, D)),
        "ln_mix_b": bias((DEPTH, D)),
        "ln_ff_g": gain((DEPTH, D)),
        "ln_ff_b": bias((DEPTH, D)),
    }


def _fwd_reference(x, sc_w_in, sc_conv_w, sc_w_out,
              mla_w_dq, mla_g_q, mla_w_uq, mla_w_dkv, mla_g_kv, mla_w_uk, mla_w_uv, mla_w_o,
              cf_w_pw1, cf_b_pw1, cf_dw_w, cf_dw_b, cf_norm_g, cf_norm_b, cf_w_pw2, cf_b_pw2,
              ff_w1, ff_w2, ln_mix_g, ln_mix_b, ln_ff_g, ln_ff_b):
    for i in range(DEPTH):
        m, j = i % N_MIXERS, i // N_MIXERS
        if m == 0:
            y = _short_conv_mixer(x, sc_w_in[j], sc_conv_w[j], sc_w_out[j])
        elif m == 1:
            y = _mla_mixer(x, mla_w_dq[j], mla_g_q[j], mla_w_uq[j], mla_w_dkv[j], mla_g_kv[j],
                           mla_w_uk[j], mla_w_uv[j], mla_w_o[j])
        else:
            y = _conformer_conv_mixer(x, cf_w_pw1[j], cf_b_pw1[j], cf_dw_w[j], cf_dw_b[j],
                                      cf_norm_g[j], cf_norm_b[j], cf_w_pw2[j], cf_b_pw2[j])
        x = _layer_norm(ALPHA * x + y, ln_mix_g[i], ln_mix_b[i])
        x = _layer_norm(ALPHA * x + _sq_relu_mlp(x, ff_w1[i], ff_w2[i]), ln_ff_g[i], ln_ff_b[i])
    return x


import jax as _jax
import jax.numpy as _jnp

TWIN_FORMAT = 'train_step'
FWD_PARAMS = ['x', 'sc_w_in', 'sc_conv_w', 'sc_w_out', 'mla_w_dq', 'mla_g_q', 'mla_w_uq', 'mla_w_dkv', 'mla_g_kv', 'mla_w_uk', 'mla_w_uv', 'mla_w_o', 'cf_w_pw1', 'cf_b_pw1', 'cf_dw_w', 'cf_dw_b', 'cf_norm_g', 'cf_norm_b', 'cf_w_pw2', 'cf_b_pw2', 'ff_w1', 'ff_w2', 'ln_mix_g', 'ln_mix_b', 'ln_ff_g', 'ln_ff_b']
TWIN_WEIGHTS = ['sc_w_in', 'sc_conv_w', 'sc_w_out', 'mla_w_dq', 'mla_g_q', 'mla_w_uq', 'mla_w_dkv', 'mla_g_kv', 'mla_w_uk', 'mla_w_uv', 'mla_w_o', 'cf_w_pw1', 'cf_b_pw1', 'cf_dw_w', 'cf_dw_b', 'cf_norm_g', 'cf_norm_b', 'cf_w_pw2', 'cf_b_pw2', 'ff_w1', 'ff_w2', 'ln_mix_g', 'ln_mix_b', 'ln_ff_g', 'ln_ff_b']
TWIN_DIFF_INPUT = 'x'
TWIN_INPUTS = ['x', 'sc_w_in', 'sc_conv_w', 'sc_w_out', 'mla_w_dq', 'mla_g_q', 'mla_w_uq', 'mla_w_dkv', 'mla_g_kv', 'mla_w_uk', 'mla_w_uv', 'mla_w_o', 'cf_w_pw1', 'cf_b_pw1', 'cf_dw_w', 'cf_dw_b', 'cf_norm_g', 'cf_norm_b', 'cf_w_pw2', 'cf_b_pw2', 'ff_w1', 'ff_w2', 'ln_mix_g', 'ln_mix_b', 'ln_ff_g', 'ln_ff_b', 'loss_target', 'm_sc_w_in', 'm_sc_conv_w', 'm_sc_w_out', 'm_mla_w_dq', 'm_mla_g_q', 'm_mla_w_uq', 'm_mla_w_dkv', 'm_mla_g_kv', 'm_mla_w_uk', 'm_mla_w_uv', 'm_mla_w_o', 'm_cf_w_pw1', 'm_cf_b_pw1', 'm_cf_dw_w', 'm_cf_dw_b', 'm_cf_norm_g', 'm_cf_norm_b', 'm_cf_w_pw2', 'm_cf_b_pw2', 'm_ff_w1', 'm_ff_w2', 'm_ln_mix_g', 'm_ln_mix_b', 'm_ln_ff_g', 'm_ln_ff_b', 'v_sc_w_in', 'v_sc_conv_w', 'v_sc_w_out', 'v_mla_w_dq', 'v_mla_g_q', 'v_mla_w_uq', 'v_mla_w_dkv', 'v_mla_g_kv', 'v_mla_w_uk', 'v_mla_w_uv', 'v_mla_w_o', 'v_cf_w_pw1', 'v_cf_b_pw1', 'v_cf_dw_w', 'v_cf_dw_b', 'v_cf_norm_g', 'v_cf_norm_b', 'v_cf_w_pw2', 'v_cf_b_pw2', 'v_ff_w1', 'v_ff_w2', 'v_ln_mix_g', 'v_ln_mix_b', 'v_ln_ff_g', 'v_ln_ff_b']
TWIN_OUTPUTS = ['loss', 'grad_x', 'grad_sc_w_in', 'grad_sc_conv_w', 'grad_sc_w_out', 'grad_mla_w_dq', 'grad_mla_g_q', 'grad_mla_w_uq', 'grad_mla_w_dkv', 'grad_mla_g_kv', 'grad_mla_w_uk', 'grad_mla_w_uv', 'grad_mla_w_o', 'grad_cf_w_pw1', 'grad_cf_b_pw1', 'grad_cf_dw_w', 'grad_cf_dw_b', 'grad_cf_norm_g', 'grad_cf_norm_b', 'grad_cf_w_pw2', 'grad_cf_b_pw2', 'grad_ff_w1', 'grad_ff_w2', 'grad_ln_mix_g', 'grad_ln_mix_b', 'grad_ln_ff_g', 'grad_ln_ff_b', 'delta_sc_w_in', 'delta_sc_conv_w', 'delta_sc_w_out', 'delta_mla_w_dq', 'delta_mla_g_q', 'delta_mla_w_uq', 'delta_mla_w_dkv', 'delta_mla_g_kv', 'delta_mla_w_uk', 'delta_mla_w_uv', 'delta_mla_w_o', 'delta_cf_w_pw1', 'delta_cf_b_pw1', 'delta_cf_dw_w', 'delta_cf_dw_b', 'delta_cf_norm_g', 'delta_cf_norm_b', 'delta_cf_w_pw2', 'delta_cf_b_pw2', 'delta_ff_w1', 'delta_ff_w2', 'delta_ln_mix_g', 'delta_ln_mix_b', 'delta_ln_ff_g', 'delta_ln_ff_b', 'new_m_sc_w_in', 'new_m_sc_conv_w', 'new_m_sc_w_out', 'new_m_mla_w_dq', 'new_m_mla_g_q', 'new_m_mla_w_uq', 'new_m_mla_w_dkv', 'new_m_mla_g_kv', 'new_m_mla_w_uk', 'new_m_mla_w_uv', 'new_m_mla_w_o', 'new_m_cf_w_pw1', 'new_m_cf_b_pw1', 'new_m_cf_dw_w', 'new_m_cf_dw_b', 'new_m_cf_norm_g', 'new_m_cf_norm_b', 'new_m_cf_w_pw2', 'new_m_cf_b_pw2', 'new_m_ff_w1', 'new_m_ff_w2', 'new_m_ln_mix_g', 'new_m_ln_mix_b', 'new_m_ln_ff_g', 'new_m_ln_ff_b', 'new_v_sc_w_in', 'new_v_sc_conv_w', 'new_v_sc_w_out', 'new_v_mla_w_dq', 'new_v_mla_g_q', 'new_v_mla_w_uq', 'new_v_mla_w_dkv', 'new_v_mla_g_kv', 'new_v_mla_w_uk', 'new_v_mla_w_uv', 'new_v_mla_w_o', 'new_v_cf_w_pw1', 'new_v_cf_b_pw1', 'new_v_cf_dw_w', 'new_v_cf_dw_b', 'new_v_cf_norm_g', 'new_v_cf_norm_b', 'new_v_cf_w_pw2', 'new_v_cf_b_pw2', 'new_v_ff_w1', 'new_v_ff_w2', 'new_v_ln_mix_g', 'new_v_ln_mix_b', 'new_v_ln_ff_g', 'new_v_ln_ff_b']
TWIN_LEAF_KINDS = {'loss': 'loss', 'grad_x': 'grad_x', 'grad_sc_w_in': 'grad_w', 'grad_sc_conv_w': 'grad_w', 'grad_sc_w_out': 'grad_w', 'grad_mla_w_dq': 'grad_w', 'grad_mla_g_q': 'grad_w', 'grad_mla_w_uq': 'grad_w', 'grad_mla_w_dkv': 'grad_w', 'grad_mla_g_kv': 'grad_w', 'grad_mla_w_uk': 'grad_w', 'grad_mla_w_uv': 'grad_w', 'grad_mla_w_o': 'grad_w', 'grad_cf_w_pw1': 'grad_w', 'grad_cf_b_pw1': 'grad_w', 'grad_cf_dw_w': 'grad_w', 'grad_cf_dw_b': 'grad_w', 'grad_cf_norm_g': 'grad_w', 'grad_cf_norm_b': 'grad_w', 'grad_cf_w_pw2': 'grad_w', 'grad_cf_b_pw2': 'grad_w', 'grad_ff_w1': 'grad_w', 'grad_ff_w2': 'grad_w', 'grad_ln_mix_g': 'grad_w', 'grad_ln_mix_b': 'grad_w', 'grad_ln_ff_g': 'grad_w', 'grad_ln_ff_b': 'grad_w', 'delta_sc_w_in': 'delta_w', 'delta_sc_conv_w': 'delta_w', 'delta_sc_w_out': 'delta_w', 'delta_mla_w_dq': 'delta_w', 'delta_mla_g_q': 'delta_w', 'delta_mla_w_uq': 'delta_w', 'delta_mla_w_dkv': 'delta_w', 'delta_mla_g_kv': 'delta_w', 'delta_mla_w_uk': 'delta_w', 'delta_mla_w_uv': 'delta_w', 'delta_mla_w_o': 'delta_w', 'delta_cf_w_pw1': 'delta_w', 'delta_cf_b_pw1': 'delta_w', 'delta_cf_dw_w': 'delta_w', 'delta_cf_dw_b': 'delta_w', 'delta_cf_norm_g': 'delta_w', 'delta_cf_norm_b': 'delta_w', 'delta_cf_w_pw2': 'delta_w', 'delta_cf_b_pw2': 'delta_w', 'delta_ff_w1': 'delta_w', 'delta_ff_w2': 'delta_w', 'delta_ln_mix_g': 'delta_w', 'delta_ln_mix_b': 'delta_w', 'delta_ln_ff_g': 'delta_w', 'delta_ln_ff_b': 'delta_w', 'new_m_sc_w_in': 'new_m', 'new_m_sc_conv_w': 'new_m', 'new_m_sc_w_out': 'new_m', 'new_m_mla_w_dq': 'new_m', 'new_m_mla_g_q': 'new_m', 'new_m_mla_w_uq': 'new_m', 'new_m_mla_w_dkv': 'new_m', 'new_m_mla_g_kv': 'new_m', 'new_m_mla_w_uk': 'new_m', 'new_m_mla_w_uv': 'new_m', 'new_m_mla_w_o': 'new_m', 'new_m_cf_w_pw1': 'new_m', 'new_m_cf_b_pw1': 'new_m', 'new_m_cf_dw_w': 'new_m', 'new_m_cf_dw_b': 'new_m', 'new_m_cf_norm_g': 'new_m', 'new_m_cf_norm_b': 'new_m', 'new_m_cf_w_pw2': 'new_m', 'new_m_cf_b_pw2': 'new_m', 'new_m_ff_w1': 'new_m', 'new_m_ff_w2': 'new_m', 'new_m_ln_mix_g': 'new_m', 'new_m_ln_mix_b': 'new_m', 'new_m_ln_ff_g': 'new_m', 'new_m_ln_ff_b': 'new_m', 'new_v_sc_w_in': 'new_v', 'new_v_sc_conv_w': 'new_v', 'new_v_sc_w_out': 'new_v', 'new_v_mla_w_dq': 'new_v', 'new_v_mla_g_q': 'new_v', 'new_v_mla_w_uq': 'new_v', 'new_v_mla_w_dkv': 'new_v', 'new_v_mla_g_kv': 'new_v', 'new_v_mla_w_uk': 'new_v', 'new_v_mla_w_uv': 'new_v', 'new_v_mla_w_o': 'new_v', 'new_v_cf_w_pw1': 'new_v', 'new_v_cf_b_pw1': 'new_v', 'new_v_cf_dw_w': 'new_v', 'new_v_cf_dw_b': 'new_v', 'new_v_cf_norm_g': 'new_v', 'new_v_cf_norm_b': 'new_v', 'new_v_cf_w_pw2': 'new_v', 'new_v_cf_b_pw2': 'new_v', 'new_v_ff_w1': 'new_v', 'new_v_ff_w2': 'new_v', 'new_v_ln_mix_g': 'new_v', 'new_v_ln_mix_b': 'new_v', 'new_v_ln_ff_g': 'new_v', 'new_v_ln_ff_b': 'new_v'}


def _forward(args):
    return _fwd_reference(*[args[k] for k in FWD_PARAMS])


def _output_shape():
    out = _jax.eval_shape(lambda: _forward(_fwd_setup_inputs(0)))
    return out.shape, out.dtype

N_MICROBATCH = 1
ADAM_LR = 0.001
ADAM_B1 = 0.9
ADAM_B2 = 0.999
ADAM_EPS = 1e-08
ADAM_WD = 0.01
ADAM_STEP = 10
PER_EXAMPLE_BATCH_AXIS = {'x': 0, 'loss_target': 0}
SHARED_INPUTS = []
_WEIGHT_DTYPES = {'sc_w_in': _jnp.float32, 'sc_conv_w': _jnp.float32, 'sc_w_out': _jnp.float32, 'mla_w_dq': _jnp.float32, 'mla_g_q': _jnp.float32, 'mla_w_uq': _jnp.float32, 'mla_w_dkv': _jnp.float32, 'mla_g_kv': _jnp.float32, 'mla_w_uk': _jnp.float32, 'mla_w_uv': _jnp.float32, 'mla_w_o': _jnp.float32, 'cf_w_pw1': _jnp.float32, 'cf_b_pw1': _jnp.float32, 'cf_dw_w': _jnp.float32, 'cf_dw_b': _jnp.float32, 'cf_norm_g': _jnp.float32, 'cf_norm_b': _jnp.float32, 'cf_w_pw2': _jnp.float32, 'cf_b_pw2': _jnp.float32, 'ff_w1': _jnp.float32, 'ff_w2': _jnp.float32, 'ln_mix_g': _jnp.float32, 'ln_mix_b': _jnp.float32, 'ln_ff_g': _jnp.float32, 'ln_ff_b': _jnp.float32}
MOMENT_SCALE = {'sc_w_in': 3.159703e-02, 'sc_conv_w': 3.123266e-02, 'sc_w_out': 7.531747e-02, 'mla_w_dq': 3.359845e-03, 'mla_g_q': 3.206365e-03, 'mla_w_uq': 1.680839e-03, 'mla_w_dkv': 5.228956e-03, 'mla_g_kv': 6.253675e-03, 'mla_w_uk': 1.732986e-03, 'mla_w_uv': 5.499622e-03, 'mla_w_o': 5.478780e-03, 'cf_w_pw1': 1.545410e-02, 'cf_b_pw1': 2.366507e-02, 'cf_dw_w': 2.044323e-02, 'cf_dw_b': 5.705410e-02, 'cf_norm_g': 2.885778e-02, 'cf_norm_b': 3.365180e-02, 'cf_w_pw2': 5.418534e-02, 'cf_b_pw2': 1.649252e-01, 'ff_w1': 9.744360e-03, 'ff_w2': 2.064133e-02, 'ln_mix_g': 5.723602e-01, 'ln_mix_b': 2.816116e-01, 'ln_ff_g': 8.049765e+00, 'ln_ff_b': 6.788066e-01}


def _to_microbatches(a, axis):
    t = _jnp.moveaxis(a, axis, 0)
    t = t.reshape((N_MICROBATCH, t.shape[0] // N_MICROBATCH) + t.shape[1:])
    return _jnp.moveaxis(t, 1, axis + 1)


def setup_inputs(seed: int = 0) -> dict:
    inp = _fwd_setup_inputs(seed)
    key = _jax.random.fold_in(_jax.random.key(seed), 7919)
    shape, _ = _output_shape()
    out = dict(inp)
    out["loss_target"] = _jax.random.normal(_jax.random.fold_in(key, 0), shape, _jnp.float32)
    for i, name in enumerate(TWIN_WEIGHTS):
        w = inp[name].astype(_jnp.float32)
        if MOMENT_SCALE is None:
            s = _jnp.sqrt(_jnp.mean(_jnp.square(w)) + 1e-30)
        else:
            s = MOMENT_SCALE[name]
        km, kv = _jax.random.split(_jax.random.fold_in(key, i + 1))
        out[name] = w
        out["m_" + name] = s * _jax.random.normal(km, w.shape, _jnp.float32)
        out["v_" + name] = (s * s) * _jax.random.uniform(kv, w.shape, _jnp.float32, 0.5, 1.5)
    if N_MICROBATCH > 1:
        for name, axis in PER_EXAMPLE_BATCH_AXIS.items():
            out[name] = _to_microbatches(out[name], axis)
    return {'x': out['x'], 'sc_w_in': out['sc_w_in'], 'sc_conv_w': out['sc_conv_w'], 'sc_w_out': out['sc_w_out'], 'mla_w_dq': out['mla_w_dq'], 'mla_g_q': out['mla_g_q'], 'mla_w_uq': out['mla_w_uq'], 'mla_w_dkv': out['mla_w_dkv'], 'mla_g_kv': out['mla_g_kv'], 'mla_w_uk': out['mla_w_uk'], 'mla_w_uv': out['mla_w_uv'], 'mla_w_o': out['mla_w_o'], 'cf_w_pw1': out['cf_w_pw1'], 'cf_b_pw1': out['cf_b_pw1'], 'cf_dw_w': out['cf_dw_w'], 'cf_dw_b': out['cf_dw_b'], 'cf_norm_g': out['cf_norm_g'], 'cf_norm_b': out['cf_norm_b'], 'cf_w_pw2': out['cf_w_pw2'], 'cf_b_pw2': out['cf_b_pw2'], 'ff_w1': out['ff_w1'], 'ff_w2': out['ff_w2'], 'ln_mix_g': out['ln_mix_g'], 'ln_mix_b': out['ln_mix_b'], 'ln_ff_g': out['ln_ff_g'], 'ln_ff_b': out['ln_ff_b'], 'loss_target': out['loss_target'], 'm_sc_w_in': out['m_sc_w_in'], 'm_sc_conv_w': out['m_sc_conv_w'], 'm_sc_w_out': out['m_sc_w_out'], 'm_mla_w_dq': out['m_mla_w_dq'], 'm_mla_g_q': out['m_mla_g_q'], 'm_mla_w_uq': out['m_mla_w_uq'], 'm_mla_w_dkv': out['m_mla_w_dkv'], 'm_mla_g_kv': out['m_mla_g_kv'], 'm_mla_w_uk': out['m_mla_w_uk'], 'm_mla_w_uv': out['m_mla_w_uv'], 'm_mla_w_o': out['m_mla_w_o'], 'm_cf_w_pw1': out['m_cf_w_pw1'], 'm_cf_b_pw1': out['m_cf_b_pw1'], 'm_cf_dw_w': out['m_cf_dw_w'], 'm_cf_dw_b': out['m_cf_dw_b'], 'm_cf_norm_g': out['m_cf_norm_g'], 'm_cf_norm_b': out['m_cf_norm_b'], 'm_cf_w_pw2': out['m_cf_w_pw2'], 'm_cf_b_pw2': out['m_cf_b_pw2'], 'm_ff_w1': out['m_ff_w1'], 'm_ff_w2': out['m_ff_w2'], 'm_ln_mix_g': out['m_ln_mix_g'], 'm_ln_mix_b': out['m_ln_mix_b'], 'm_ln_ff_g': out['m_ln_ff_g'], 'm_ln_ff_b': out['m_ln_ff_b'], 'v_sc_w_in': out['v_sc_w_in'], 'v_sc_conv_w': out['v_sc_conv_w'], 'v_sc_w_out': out['v_sc_w_out'], 'v_mla_w_dq': out['v_mla_w_dq'], 'v_mla_g_q': out['v_mla_g_q'], 'v_mla_w_uq': out['v_mla_w_uq'], 'v_mla_w_dkv': out['v_mla_w_dkv'], 'v_mla_g_kv': out['v_mla_g_kv'], 'v_mla_w_uk': out['v_mla_w_uk'], 'v_mla_w_uv': out['v_mla_w_uv'], 'v_mla_w_o': out['v_mla_w_o'], 'v_cf_w_pw1': out['v_cf_w_pw1'], 'v_cf_b_pw1': out['v_cf_b_pw1'], 'v_cf_dw_w': out['v_cf_dw_w'], 'v_cf_dw_b': out['v_cf_dw_b'], 'v_cf_norm_g': out['v_cf_norm_g'], 'v_cf_norm_b': out['v_cf_norm_b'], 'v_cf_w_pw2': out['v_cf_w_pw2'], 'v_cf_b_pw2': out['v_cf_b_pw2'], 'v_ff_w1': out['v_ff_w1'], 'v_ff_w2': out['v_ff_w2'], 'v_ln_mix_g': out['v_ln_mix_g'], 'v_ln_mix_b': out['v_ln_mix_b'], 'v_ln_ff_g': out['v_ln_ff_g'], 'v_ln_ff_b': out['v_ln_ff_b']}


def _loss(weights, diff, rest, loss_target):
    with _jax.named_scope("forward"):
        args = {**rest, TWIN_DIFF_INPUT: diff, **{k: w.astype(_WEIGHT_DTYPES[k]) for k, w in weights.items()}}
        y = _forward(args)
    with _jax.named_scope("loss_head"):
        err = _jnp.square(y.astype(_jnp.float32) - loss_target)
        return 0.5 * _jnp.sum(_jnp.mean(err, axis=-1)) if err.ndim else 0.5 * err


def _adamw(w, g, m, v):
    m = ADAM_B1 * m + (1.0 - ADAM_B1) * g
    v = ADAM_B2 * v + (1.0 - ADAM_B2) * _jnp.square(g)
    m_hat = m / (1.0 - ADAM_B1 ** ADAM_STEP)
    v_hat = v / (1.0 - ADAM_B2 ** ADAM_STEP)
    delta = -ADAM_LR * (m_hat / (_jnp.sqrt(v_hat) + ADAM_EPS) + ADAM_WD * w)
    return delta, m, v


def reference(x, sc_w_in, sc_conv_w, sc_w_out, mla_w_dq, mla_g_q, mla_w_uq, mla_w_dkv, mla_g_kv, mla_w_uk, mla_w_uv, mla_w_o, cf_w_pw1, cf_b_pw1, cf_dw_w, cf_dw_b, cf_norm_g, cf_norm_b, cf_w_pw2, cf_b_pw2, ff_w1, ff_w2, ln_mix_g, ln_mix_b, ln_ff_g, ln_ff_b, loss_target, m_sc_w_in, m_sc_conv_w, m_sc_w_out, m_mla_w_dq, m_mla_g_q, m_mla_w_uq, m_mla_w_dkv, m_mla_g_kv, m_mla_w_uk, m_mla_w_uv, m_mla_w_o, m_cf_w_pw1, m_cf_b_pw1, m_cf_dw_w, m_cf_dw_b, m_cf_norm_g, m_cf_norm_b, m_cf_w_pw2, m_cf_b_pw2, m_ff_w1, m_ff_w2, m_ln_mix_g, m_ln_mix_b, m_ln_ff_g, m_ln_ff_b, v_sc_w_in, v_sc_conv_w, v_sc_w_out, v_mla_w_dq, v_mla_g_q, v_mla_w_uq, v_mla_w_dkv, v_mla_g_kv, v_mla_w_uk, v_mla_w_uv, v_mla_w_o, v_cf_w_pw1, v_cf_b_pw1, v_cf_dw_w, v_cf_dw_b, v_cf_norm_g, v_cf_norm_b, v_cf_w_pw2, v_cf_b_pw2, v_ff_w1, v_ff_w2, v_ln_mix_g, v_ln_mix_b, v_ln_ff_g, v_ln_ff_b):
    given = dict(x=x, sc_w_in=sc_w_in, sc_conv_w=sc_conv_w, sc_w_out=sc_w_out, mla_w_dq=mla_w_dq, mla_g_q=mla_g_q, mla_w_uq=mla_w_uq, mla_w_dkv=mla_w_dkv, mla_g_kv=mla_g_kv, mla_w_uk=mla_w_uk, mla_w_uv=mla_w_uv, mla_w_o=mla_w_o, cf_w_pw1=cf_w_pw1, cf_b_pw1=cf_b_pw1, cf_dw_w=cf_dw_w, cf_dw_b=cf_dw_b, cf_norm_g=cf_norm_g, cf_norm_b=cf_norm_b, cf_w_pw2=cf_w_pw2, cf_b_pw2=cf_b_pw2, ff_w1=ff_w1, ff_w2=ff_w2, ln_mix_g=ln_mix_g, ln_mix_b=ln_mix_b, ln_ff_g=ln_ff_g, ln_ff_b=ln_ff_b, loss_target=loss_target, m_sc_w_in=m_sc_w_in, m_sc_conv_w=m_sc_conv_w, m_sc_w_out=m_sc_w_out, m_mla_w_dq=m_mla_w_dq, m_mla_g_q=m_mla_g_q, m_mla_w_uq=m_mla_w_uq, m_mla_w_dkv=m_mla_w_dkv, m_mla_g_kv=m_mla_g_kv, m_mla_w_uk=m_mla_w_uk, m_mla_w_uv=m_mla_w_uv, m_mla_w_o=m_mla_w_o, m_cf_w_pw1=m_cf_w_pw1, m_cf_b_pw1=m_cf_b_pw1, m_cf_dw_w=m_cf_dw_w, m_cf_dw_b=m_cf_dw_b, m_cf_norm_g=m_cf_norm_g, m_cf_norm_b=m_cf_norm_b, m_cf_w_pw2=m_cf_w_pw2, m_cf_b_pw2=m_cf_b_pw2, m_ff_w1=m_ff_w1, m_ff_w2=m_ff_w2, m_ln_mix_g=m_ln_mix_g, m_ln_mix_b=m_ln_mix_b, m_ln_ff_g=m_ln_ff_g, m_ln_ff_b=m_ln_ff_b, v_sc_w_in=v_sc_w_in, v_sc_conv_w=v_sc_conv_w, v_sc_w_out=v_sc_w_out, v_mla_w_dq=v_mla_w_dq, v_mla_g_q=v_mla_g_q, v_mla_w_uq=v_mla_w_uq, v_mla_w_dkv=v_mla_w_dkv, v_mla_g_kv=v_mla_g_kv, v_mla_w_uk=v_mla_w_uk, v_mla_w_uv=v_mla_w_uv, v_mla_w_o=v_mla_w_o, v_cf_w_pw1=v_cf_w_pw1, v_cf_b_pw1=v_cf_b_pw1, v_cf_dw_w=v_cf_dw_w, v_cf_dw_b=v_cf_dw_b, v_cf_norm_g=v_cf_norm_g, v_cf_norm_b=v_cf_norm_b, v_cf_w_pw2=v_cf_w_pw2, v_cf_b_pw2=v_cf_b_pw2, v_ff_w1=v_ff_w1, v_ff_w2=v_ff_w2, v_ln_mix_g=v_ln_mix_g, v_ln_mix_b=v_ln_mix_b, v_ln_ff_g=v_ln_ff_g, v_ln_ff_b=v_ln_ff_b)
    weights = {n: given[n] for n in TWIN_WEIGHTS}
    shared = {n: given[n] for n in SHARED_INPUTS}
    per_example = {n: given[n] for n in ['x']}
    grad_fn = _jax.value_and_grad(_loss, argnums=(0, 1))

    def one_microbatch(ex, loss_target):
        ex = dict(ex)
        diff = ex.pop(TWIN_DIFF_INPUT)
        return grad_fn(weights, diff, {**shared, **ex}, loss_target)

    if N_MICROBATCH == 1:
        loss, (grad_w, grad_x) = one_microbatch(per_example, given["loss_target"])
    else:
        def body(carry, xs):
            loss_sum, grad_sum = carry
            l_k, (gw_k, gx_k) = one_microbatch(xs[0], xs[1])
            with _jax.named_scope("update"):
                return (loss_sum + l_k, _jax.tree.map(_jnp.add, grad_sum, gw_k)), gx_k

        init = (_jnp.zeros((), _jnp.float32), _jax.tree.map(_jnp.zeros_like, weights))
        (loss, grad_w), grad_x = _jax.lax.scan(body, init, (per_example, given["loss_target"]))
    with _jax.named_scope("update"):
        delta_w, new_m, new_v = {}, {}, {}
        for n in TWIN_WEIGHTS:
            delta_w[n], new_m[n], new_v[n] = _adamw(weights[n], grad_w[n], given["m_" + n], given["v_" + n])
    return (loss, grad_x, *[grad_w[n] for n in TWIN_WEIGHTS], *[delta_w[n] for n in TWIN_WEIGHTS],
            *[new_m[n] for n in TWIN_WEIGHTS], *[new_v[n] for n in TWIN_WEIGHTS])
```

```python
import jax
import jax.numpy as jnp
from jax import lax
from jax.experimental import pallas as pl
from jax.experimental.pallas import tpu as pltpu

F32 = jnp.float32
BF16 = jnp.bfloat16
MESH = pl.DeviceIdType.MESH

DEPTH = 4
ALPHA = (2.0 * DEPTH) ** 0.25
LN_EPS = 1e-5
RMS_EPS = 1e-6
CHUNK_SHIFT = 6
N_HEADS = 8
QK_NOPE = 128
QK_ROPE = 64
V_HEAD = 128
HEAD_PAD = 256
Q_LORA = 384
KV_LORA = 256
ROPE_THETA = 10000.0
SC_WIDTH = 3
CONF_WIDTH = 31
N_CHIPS = 4
ATTN_SCALE = (QK_NOPE + QK_ROPE) ** -0.5

ADAM_LR = 0.001
ADAM_B1 = 0.9
ADAM_B2 = 0.999
ADAM_EPS = 1e-08
ADAM_WD = 0.01
ADAM_STEP = 10

VMEM_LIMIT = 56 * 2**20

NN = (((1,), (0,)), ((), ()))
NT = (((1,), (1,)), ((), ()))
TN = (((0,), (0,)), ((), ()))


def _params(sem=None):
    return pltpu.CompilerParams(dimension_semantics=sem, vmem_limit_bytes=VMEM_LIMIT)


class Stk:
    def __init__(self, kind, k, n, arr=None, layers=None, layer=None):
        self.kind, self.k, self.n, self.layers, self.layer = kind, k, n, layers, layer
        self.plain = kind == "row" and layers is None
        self.kloc = k // N_CHIPS if kind == "row" else k
        self.nloc = n // N_CHIPS if kind == "col" else n
        if arr is not None and self.plain:
            arr = arr.reshape(k, n)
        self.arr = arr

    @property
    def shape(self):
        if self.plain:
            return (self.k, self.n)
        lead = (N_CHIPS,) if self.layers is None else (N_CHIPS, self.layers)
        return lead + (self.kloc, self.nloc)

    def spec(self, bk, bn, f):
        if self.plain:
            return pl.BlockSpec((bk, bn), f)
        assert self.kloc % bk == 0 and self.nloc % bn == 0, (self.kloc, bk, self.nloc, bn)
        pk, pn = self.kloc // bk, self.nloc // bn
        kind, layer = self.kind, self.layer

        def imap(*g):
            kb, nb = f(*g)
            if kind == "row":
                q, kb, nb = kb // pk, kb % pk, nb
            else:
                q, kb, nb = nb // pn, kb, nb % pn
            return (q, kb, nb) if layer is None else (q, layer, kb, nb)

        block = (None, bk, bn) if layer is None else (None, None, bk, bn)
        return pl.BlockSpec(block, imap)


def _mm(name, mode, a, b, grid, a_spec, b_spec, acc_shape, extras, extra_specs, out_shapes, out_specs, epi):
    nk = grid[2]
    ne = len(extras)

    def body(*refs):
        a_ref, b_ref = refs[0], refs[1]
        e_refs = refs[2:2 + ne]
        o_refs = refs[2 + ne:-1]
        acc = refs[-1]
        k = pl.program_id(2)

        @pl.when(k == 0)
        def _():
            acc[...] = jnp.zeros_like(acc)

        acc[...] += lax.dot_general(a_ref[...], b_ref[...], mode, preferred_element_type=F32)

        @pl.when(k == nk - 1)
        def _():
            epi(acc[...], e_refs, o_refs)

    return pl.pallas_call(
        body, grid=grid, in_specs=[a_spec, b_spec, *extra_specs], out_specs=out_specs, out_shape=out_shapes,
        scratch_shapes=[pltpu.VMEM(acc_shape, F32)],
        compiler_params=_params(("parallel", "parallel", "arbitrary")), name=name)(a, b, *extras)


def _tile(n, t):
    t = min(n, t)
    while n % t:
        t -= 8
    assert t > 0, (n, t)
    return t


def mm_nn(name, a, w, tm, tn, tk, epi, out_shapes, out_specs, extras=(), extra_specs=(), a_spec=None):
    m = a.shape[0]
    tm, tn, tk = _tile(m, tm), _tile(w.n, tn), _tile(w.k, tk)
    grid = (m // tm, w.n // tn, w.k // tk)
    a_spec = a_spec or pl.BlockSpec((tm, tk), lambda i, j, k: (i, k))
    b_spec = w.spec(tk, tn, lambda i, j, k: (k, j))
    return _mm(name, NN, a, w.arr, grid, a_spec, b_spec, (tm, tn), extras, extra_specs, out_shapes, out_specs, epi)


def mm_nt(name, a, w, m, tm, tn, tk, epi, out_shapes, out_specs, extras=(), extra_specs=(), a_spec=None):
    tm, tn, tk = _tile(m, tm), _tile(w.k, tn), _tile(w.n, tk)
    grid = (m // tm, w.k // tn, w.n // tk)
    a_spec = a_spec or pl.BlockSpec((tm, tk), lambda i, j, k: (i, k))
    b_spec = w.spec(tn, tk, lambda i, j, k: (j, k))
    return _mm(name, NT, a, w.arr, grid, a_spec, b_spec, (tm, tn), extras, extra_specs, out_shapes, out_specs, epi)


def mm_tn(name, a, b, dw, s, tm, tn, tk, a_spec=None, b_spec=None):
    tm, tn, tk = _tile(dw.k, tm), _tile(dw.n, tn), _tile(s, tk)
    grid = (dw.k // tm, dw.n // tn, s // tk)
    a_spec = a_spec or pl.BlockSpec((tk, tm), lambda i, j, k: (k, i))
    b_spec = b_spec or pl.BlockSpec((tk, tn), lambda i, j, k: (k, j))

    def epi(acc, e, o):
        o[0][...] = acc.astype(BF16)

    out = _mm(name, TN, a, b, grid, a_spec, b_spec, (tm, tn), (), (), [jax.ShapeDtypeStruct(dw.shape, BF16)],
              [dw.spec(tm, tn, lambda i, j, k: (i, j))], epi)[0]
    return out.reshape(N_CHIPS, dw.k // N_CHIPS, dw.n) if dw.plain else out


def _sds(shape, dtype):
    return jax.ShapeDtypeStruct(shape, dtype)


def _ij(tm, tn):
    return pl.BlockSpec((tm, tn), lambda i, j, k: (i, j))


def _i0(tm, c):
    return pl.BlockSpec((tm, c), lambda i, j, k: (i, 0))


def _0j(r, tn):
    return pl.BlockSpec((r, tn), lambda i, j, k: (0, j))


def _layer_norm_rows(r, g, b):
    mu = jnp.mean(r, axis=-1, keepdims=True)
    d = r - mu
    var = jnp.mean(d * d, axis=-1, keepdims=True)
    rstd = lax.rsqrt(var + LN_EPS)
    xh = d * rstd
    return xh * g + b, xh, rstd


def mm_residual_ln(name, a, w, x, g, b, bias=None, tm=256, tk=512):
    s, d = x.shape
    tm = _tile(s, tm)
    extras = [x, g, b] + ([bias] if bias is not None else [])
    especs = [_i0(tm, d), _0j(1, d), _0j(1, d)] + ([_0j(1, d)] if bias is not None else [])

    def epi(acc, e, o):
        r = ALPHA * e[0][...] + acc
        if bias is not None:
            r = r + e[3][...]
        y, xh, rstd = _layer_norm_rows(r, e[1][...], e[2][...])
        o[0][...] = y
        o[1][...] = y.astype(BF16)
        o[2][...] = xh
        o[3][...] = rstd

    return mm_nn(name, a, w, tm, d, tk, epi,
                 [_sds((s, d), F32), _sds((s, d), BF16), _sds((s, d), F32), _sds((s, 1), F32)],
                 [_i0(tm, d), _i0(tm, d), _i0(tm, d), _i0(tm, 1)], extras, especs)


def mm_plain_nn(name, a, w, out_dtype, tm=512, tn=512, tk=512, bias=None):
    m = a.shape[0]
    tm, tn = _tile(m, tm), _tile(w.n, tn)
    if w.kind == "col":
        tn = _tile(w.nloc, tn)

    def epi(acc, e, o):
        if bias is not None:
            acc = acc + e[0][...]
        o[0][...] = acc.astype(out_dtype)

    extras, especs = ([bias], [_0j(1, tn)]) if bias is not None else ((), ())
    return mm_nn(name, a, w, tm, tn, tk, epi, [_sds((m, w.n), out_dtype)], [_ij(tm, tn)], extras, especs)[0]


def mm_plain_nt(name, a, w, out_dtype, tm=512, tn=512, tk=512, add=None, add_scale=1.0, a_spec_fn=None):
    m = a.shape[0] if a_spec_fn is None else a_spec_fn[0]
    tm, tn = _tile(m, tm), _tile(w.k, tn)
    tk = _tile(w.n, tk)
    if w.kind == "col":
        tk = _tile(w.nloc, tk)
    if w.kind == "row" and not w.plain:
        tn = _tile(w.kloc, tn)

    def epi(acc, e, o):
        if add is not None:
            acc = acc + add_scale * e[0][...].astype(F32)
        o[0][...] = acc.astype(out_dtype)

    extras, especs = ([add], [_ij(tm, tn)]) if add is not None else ((), ())
    a_spec = None if a_spec_fn is None else a_spec_fn[1](tm, tk)
    return mm_nt(name, a, w, m, tm, tn, tk, epi, [_sds((m, w.k), out_dtype)], [_ij(tm, tn)], extras, especs,
                 a_spec=a_spec)[0]


def _rows(tm, c):
    return pl.BlockSpec((tm, c), lambda i: (i, 0))


def _fix(shape):
    nd = len(shape)
    return pl.BlockSpec(shape, lambda i: (0,) * nd)


def _accumulate(ref, val):
    @pl.when(pl.program_id(0) == 0)
    def _():
        ref[...] = jnp.zeros_like(ref)

    ref[...] += val


def ln_backward(name, dy, xhat, rstd, g, tm=256):
    s, d = dy.shape
    tm = _tile(s, tm)

    def body(dy_ref, xh_ref, rstd_ref, g_ref, dr_ref, drb_ref, dg_ref, db_ref, ds_ref):
        dyv, xh = dy_ref[...], xh_ref[...]
        dxh = dyv * g_ref[...]
        m1 = jnp.mean(dxh, axis=-1, keepdims=True)
        m2 = jnp.mean(dxh * xh, axis=-1, keepdims=True)
        dr = rstd_ref[...] * (dxh - m1 - xh * m2)
        dr_ref[...] = dr
        drb_ref[...] = dr.astype(BF16)
        _accumulate(dg_ref, jnp.sum(dyv * xh, axis=0, keepdims=True))
        _accumulate(db_ref, jnp.sum(dyv, axis=0, keepdims=True))
        _accumulate(ds_ref, jnp.sum(dr, axis=0, keepdims=True))

    return pl.pallas_call(
        body, grid=(s // tm,),
        in_specs=[_rows(tm, d), _rows(tm, d), _rows(tm, 1), _fix((1, d))],
        out_specs=[_rows(tm, d), _rows(tm, d), _fix((1, d)), _fix((1, d)), _fix((1, d))],
        out_shape=[_sds((s, d), F32), _sds((s, d), BF16), _sds((1, d), F32), _sds((1, d), F32), _sds((1, d), F32)],
        compiler_params=_params(("arbitrary",)), name=name)(dy, xhat, rstd, g)


def loss_head(y, target, tm=256):
    s, d = y.shape
    tm = _tile(s, tm)

    def body(y_ref, t_ref, dy_ref, loss_ref):
        e = y_ref[...] - t_ref[...]
        dy_ref[...] = e * (1.0 / d)
        part = 0.5 * jnp.sum(jnp.mean(e * e, axis=-1, keepdims=True), axis=0, keepdims=True)
        _accumulate(loss_ref, jnp.broadcast_to(part, (1, 128)))

    return pl.pallas_call(
        body, grid=(s // tm,), in_specs=[_rows(tm, d), _rows(tm, d)],
        out_specs=[_rows(tm, d), _fix((1, 128))], out_shape=[_sds((s, d), F32), _sds((1, 128), F32)],
        compiler_params=_params(("arbitrary",)), name="loss_head")(y, target)


def _cols(s, tc, off=0):
    return pl.BlockSpec((s, tc), lambda i: (0, i + off))


def _shift_down(z, sft, rows):
    return jnp.where(rows >= sft, pltpu.roll(z, sft, 0), 0.0)


def _shift_up(z, sft, rows, s):
    return jnp.where(rows < s - sft, pltpu.roll(z, (s - sft) % s, 0), 0.0)


def short_conv_gate(u, conv_w, tc=256):
    s, d3 = u.shape
    d = d3 // 3
    nb = d // tc

    def body(b_ref, c_ref, h_ref, w_ref, o_ref):
        rows = lax.broadcasted_iota(jnp.int32, (s, tc), 0)
        z = c_ref[...] * h_ref[...]
        cz = jnp.zeros((s, tc), F32)
        for k in range(SC_WIDTH):
            sft = SC_WIDTH - 1 - k
            cz = cz + w_ref[pl.ds(k, 1), :] * (_shift_down(z, sft, rows) if sft else z)
        o_ref[...] = (b_ref[...] * cz).astype(BF16)

    return pl.pallas_call(
        body, grid=(nb,),
        in_specs=[_cols(s, tc), _cols(s, tc, nb), _cols(s, tc, 2 * nb), _cols(SC_WIDTH, tc)],
        out_specs=_cols(s, tc), out_shape=_sds((s, d), BF16),
        compiler_params=_params(("parallel",)), name="short_conv_gate")(u, u, u, conv_w)


def short_conv_gate_bwd(u, conv_w, dg, tc=256):
    s, d3 = u.shape
    d = d3 // 3
    nb = d // tc

    def body(b_ref, c_ref, h_ref, w_ref, dg_ref, du_ref, dw_ref):
        rows = lax.broadcasted_iota(jnp.int32, (s, tc), 0)
        c, h, dgv = c_ref[...], h_ref[...], dg_ref[...]
        z = c * h
        dcz = dgv * b_ref[...]
        cz = jnp.zeros((s, tc), F32)
        dz = jnp.zeros((s, tc), F32)
        for k in range(SC_WIDTH):
            sft = SC_WIDTH - 1 - k
            zs = _shift_down(z, sft, rows) if sft else z
            wk = w_ref[pl.ds(k, 1), :]
            cz = cz + wk * zs
            dz = dz + wk * (_shift_up(dcz, sft, rows, s) if sft else dcz)
            dw_ref[pl.ds(k, 1), :] = jnp.sum(dcz * zs, axis=0, keepdims=True)
        du_ref[0] = (dgv * cz).astype(BF16)
        du_ref[1] = (dz * h).astype(BF16)
        du_ref[2] = (dz * c).astype(BF16)

    return pl.pallas_call(
        body, grid=(nb,),
        in_specs=[_cols(s, tc), _cols(s, tc, nb), _cols(s, tc, 2 * nb), _cols(SC_WIDTH, tc), _cols(s, tc)],
        out_specs=[pl.BlockSpec((3, s, tc), lambda i: (0, 0, i)), _cols(SC_WIDTH, tc)],
        out_shape=[_sds((3, s, d), BF16), _sds((SC_WIDTH, d), F32)],
        compiler_params=_params(("parallel",)), name="short_conv_gate_bwd")(u, u, u, conv_w, dg)


def conformer_glu_conv(u, dw_w, dw_b, tc=256):
    s, d2 = u.shape
    d = d2 // 2
    nb = d // tc

    def body(a_ref, g_ref, w_ref, b_ref, o_ref):
        rows = lax.broadcasted_iota(jnp.int32, (s, tc), 0)
        h = a_ref[...] * jax.nn.sigmoid(g_ref[...])

        def tap(k, acc):
            sft = CONF_WIDTH - 1 - k
            return acc + w_ref[pl.ds(k, 1), :] * _shift_down(h, sft, rows)

        o_ref[...] = lax.fori_loop(0, CONF_WIDTH, tap, jnp.zeros((s, tc), F32)) + b_ref[...]

    return pl.pallas_call(
        body, grid=(nb,),
        in_specs=[_cols(s, tc), _cols(s, tc, nb), _cols(CONF_WIDTH, tc), _cols(1, tc)],
        out_specs=_cols(s, tc), out_shape=_sds((s, d), F32),
        compiler_params=_params(("parallel",)), name="conformer_glu_conv")(u, u, dw_w, dw_b)


def conformer_glu_conv_bwd(u, dw_w, dhc, tc=256):
    s, d2 = u.shape
    d = d2 // 2
    nb = d // tc

    def body(a_ref, g_ref, w_ref, dhc_ref, du_ref, dbias_ref, dw_ref, db_ref):
        rows = lax.broadcasted_iota(jnp.int32, (s, tc), 0)
        a = a_ref[...]
        sg = jax.nn.sigmoid(g_ref[...])
        h = a * sg
        dhcv = dhc_ref[...]

        def tap(k, dh):
            sft = CONF_WIDTH - 1 - k
            dw_ref[pl.ds(k, 1), :] = jnp.sum(dhcv * _shift_down(h, sft, rows), axis=0, keepdims=True)
            return dh + w_ref[pl.ds(k, 1), :] * _shift_up(dhcv, sft, rows, s)

        dh = lax.fori_loop(0, CONF_WIDTH, tap, jnp.zeros((s, tc), F32))
        da = dh * sg
        dgate = dh * a * sg * (1.0 - sg)
        du_ref[0] = da.astype(BF16)
        du_ref[1] = dgate.astype(BF16)
        dbias_ref[pl.ds(0, 1), :] = jnp.sum(da, axis=0, keepdims=True)
        dbias_ref[pl.ds(1, 1), :] = jnp.sum(dgate, axis=0, keepdims=True)
        db_ref[...] = jnp.sum(dhcv, axis=0, keepdims=True)

    return pl.pallas_call(
        body, grid=(nb,),
        in_specs=[_cols(s, tc), _cols(s, tc, nb), _cols(CONF_WIDTH, tc), _cols(s, tc)],
        out_specs=[pl.BlockSpec((2, s, tc), lambda i: (0, 0, i)), _cols(2, tc), _cols(CONF_WIDTH, tc), _cols(1, tc)],
        out_shape=[_sds((2, s, d), BF16), _sds((2, d), F32), _sds((CONF_WIDTH, d), F32), _sds((1, d), F32)],
        compiler_params=_params(("parallel",)), name="conformer_glu_conv_bwd")(u, u, dw_w, dhc)


def conformer_norm_swish(hc, g, b, tm=256):
    s, d = hc.shape
    tm = _tile(s, tm)

    def body(h_ref, g_ref, b_ref, o_ref):
        n, _, _ = _layer_norm_rows(h_ref[...], g_ref[...], b_ref[...])
        o_ref[...] = (n * jax.nn.sigmoid(n)).astype(BF16)

    return pl.pallas_call(
        body, grid=(s // tm,), in_specs=[_rows(tm, d), _fix((1, d)), _fix((1, d))], out_specs=_rows(tm, d),
        out_shape=_sds((s, d), BF16), compiler_params=_params(("parallel",)), name="conformer_norm_swish")(hc, g, b)


def conformer_norm_swish_bwd(hc, g, b, ds, tm=256):
    s, d = hc.shape
    tm = _tile(s, tm)

    def body(h_ref, g_ref, b_ref, ds_ref, dh_ref, dg_ref, db_ref):
        n, nh, rstd = _layer_norm_rows(h_ref[...], g_ref[...], b_ref[...])
        sg = jax.nn.sigmoid(n)
        dn = ds_ref[...] * (sg * (1.0 + n * (1.0 - sg)))
        dnh = dn * g_ref[...]
        m1 = jnp.mean(dnh, axis=-1, keepdims=True)
        m2 = jnp.mean(dnh * nh, axis=-1, keepdims=True)
        dh_ref[...] = rstd * (dnh - m1 - nh * m2)
        _accumulate(dg_ref, jnp.sum(dn * nh, axis=0, keepdims=True))
        _accumulate(db_ref, jnp.sum(dn, axis=0, keepdims=True))

    return pl.pallas_call(
        body, grid=(s // tm,), in_specs=[_rows(tm, d), _fix((1, d)), _fix((1, d)), _rows(tm, d)],
        out_specs=[_rows(tm, d), _fix((1, d)), _fix((1, d))],
        out_shape=[_sds((s, d), F32), _sds((1, d), F32), _sds((1, d), F32)],
        compiler_params=_params(("arbitrary",)), name="conformer_norm_swish_bwd")(hc, g, b, ds)


def _swap_halves(x):
    lane = lax.broadcasted_iota(jnp.int32, x.shape, 1)
    return jnp.where(lane < QK_ROPE // 2, pltpu.roll(x, 128 - QK_ROPE // 2, 1), pltpu.roll(x, QK_ROPE // 2, 1))


def _rope(x, cf, sf):
    return x * cf + _swap_halves(x) * sf


def _unrope(dx, cf, sf):
    return dx * cf - _swap_halves(dx) * sf


def _rms_rows(x, g):
    r = lax.rsqrt(jnp.mean(x * x, axis=-1, keepdims=True) + RMS_EPS)
    return x * r, r


def mla_latents(t, g_q, g_kv, cf, sf, tm=256):
    s = t.shape[0]
    tm = _tile(s, tm)

    def body(t_ref, gq_ref, gkv_ref, cf_ref, sf_ref, cq_ref, ckv_ref, kpe_ref):
        xq, _ = _rms_rows(t_ref[:, 0:Q_LORA], gq_ref[...])
        cq_ref[...] = (xq * gq_ref[...]).astype(BF16)
        xkv, _ = _rms_rows(t_ref[:, Q_LORA:Q_LORA + KV_LORA], gkv_ref[...])
        ckv_ref[...] = (xkv * gkv_ref[...]).astype(BF16)
        kpe_ref[...] = _rope(t_ref[:, Q_LORA + KV_LORA:], cf_ref[...], sf_ref[...]).astype(BF16)

    w = Q_LORA + KV_LORA + 128
    return pl.pallas_call(
        body, grid=(s // tm,),
        in_specs=[_rows(tm, w), _fix((1, Q_LORA)), _fix((1, KV_LORA)), _rows(tm, 128), _rows(tm, 128)],
        out_specs=[_rows(tm, Q_LORA), _rows(tm, KV_LORA), _rows(tm, 128)],
        out_shape=[_sds((s, Q_LORA), BF16), _sds((s, KV_LORA), BF16), _sds((s, 128), BF16)],
        compiler_params=_params(("parallel",)), name="mla_latents")(t, g_q, g_kv, cf, sf)


def mla_latents_bwd(t, g_q, g_kv, cf, sf, dcq, dckv, dkpe, tm=256):
    s = t.shape[0]
    tm = _tile(s, tm)
    w = Q_LORA + KV_LORA + 128

    def rms_bwd(x, g, dy):
        xh, r = _rms_rows(x, g)
        dxh = dy * g
        return r * (dxh - xh * jnp.mean(dxh * xh, axis=-1, keepdims=True)), jnp.sum(dy * xh, axis=0, keepdims=True)

    def body(t_ref, gq_ref, gkv_ref, cf_ref, sf_ref, dcq_ref, dckv_ref, dkpe_ref, dt_ref, dgq_ref, dgkv_ref):
        dxq, dgq = rms_bwd(t_ref[:, 0:Q_LORA], gq_ref[...], dcq_ref[...])
        dxkv, dgkv = rms_bwd(t_ref[:, Q_LORA:Q_LORA + KV_LORA], gkv_ref[...], dckv_ref[...])
        dt_ref[:, 0:Q_LORA] = dxq.astype(BF16)
        dt_ref[:, Q_LORA:Q_LORA + KV_LORA] = dxkv.astype(BF16)
        dt_ref[:, Q_LORA + KV_LORA:] = _unrope(dkpe_ref[...], cf_ref[...], sf_ref[...]).astype(BF16)
        _accumulate(dgq_ref, dgq)
        _accumulate(dgkv_ref, dgkv)

    return pl.pallas_call(
        body, grid=(s // tm,),
        in_specs=[_rows(tm, w), _fix((1, Q_LORA)), _fix((1, KV_LORA)), _rows(tm, 128), _rows(tm, 128),
                  _rows(tm, Q_LORA), _rows(tm, KV_LORA), _rows(tm, 128)],
        out_specs=[_rows(tm, w), _fix((1, Q_LORA)), _fix((1, KV_LORA))],
        out_shape=[_sds((s, w), BF16), _sds((1, Q_LORA), F32), _sds((1, KV_LORA), F32)],
        compiler_params=_params(("arbitrary",)), name="mla_latents_bwd")(t, g_q, g_kv, cf, sf, dcq, dckv, dkpe)


def mla_queries(cq, w_uq, cf, sf, tm=512):
    s = cq.shape[0]
    tm = _tile(s, tm)

    def epi(acc, e, o):
        o[0][:, 0:QK_NOPE] = acc[:, 0:QK_NOPE].astype(BF16)
        o[0][:, QK_NOPE:] = _rope(acc[:, QK_NOPE:], e[0][...], e[1][...]).astype(BF16)

    return mm_nn("mla_queries", cq, w_uq, tm, HEAD_PAD, Q_LORA, epi, [_sds((s, N_HEADS * HEAD_PAD), BF16)],
                 [_ij(tm, HEAD_PAD)], [cf, sf], [_i0(tm, 128), _i0(tm, 128)])[0]


def mla_keys(ckv, w_uk, kpe, tm=512):
    s = ckv.shape[0]
    tm = _tile(s, tm)

    def epi(acc, e, o):
        o[0][:, 0:QK_NOPE] = acc.astype(BF16)
        o[0][:, QK_NOPE:] = e[0][...]

    return mm_nn("mla_keys", ckv, w_uk, tm, QK_NOPE, KV_LORA, epi, [_sds((s, N_HEADS * HEAD_PAD), BF16)],
                 [_ij(tm, HEAD_PAD)], [kpe], [_i0(tm, 128)])[0]


def _masked_scores(q, k, qi, tq, kv):
    sc = lax.dot_general(q, k, NT, preferred_element_type=F32) * ATTN_SCALE
    row = lax.broadcasted_iota(jnp.int32, (tq, kv), 0) + qi * tq
    col = lax.broadcasted_iota(jnp.int32, (tq, kv), 1)
    ok = lax.shift_right_logical(col, CHUNK_SHIFT) <= lax.shift_right_logical(row, CHUNK_SHIFT)
    return jnp.where(ok, sc, -1e30)


def attention(q, k, v, tq=256):
    s = q.shape[0]
    tq = _tile(s, tq)
    nq = s // tq

    def body(q_ref, k_ref, v_ref, o_ref):
        for qi in range(nq):
            kv = (qi + 1) * tq
            sc = _masked_scores(q_ref[pl.ds(qi * tq, tq), :], k_ref[pl.ds(0, kv), :], qi, tq, kv)
            p = jnp.exp(sc - jnp.max(sc, axis=-1, keepdims=True))
            o = lax.dot_general(p.astype(BF16), v_ref[pl.ds(0, kv), :], NN, preferred_element_type=F32)
            o_ref[pl.ds(qi * tq, tq), :] = (o / jnp.sum(p, axis=-1, keepdims=True)).astype(BF16)

    hq = pl.BlockSpec((s, HEAD_PAD), lambda h: (0, h))
    hv = pl.BlockSpec((s, V_HEAD), lambda h: (0, h))
    return pl.pallas_call(
        body, grid=(N_HEADS,), in_specs=[hq, hq, hv], out_specs=hv, out_shape=_sds((s, N_HEADS * V_HEAD), BF16),
        compiler_params=_params(("parallel",)), name="attention")(q, k, v)


def attention_bwd(q, k, v, do, tq=256):
    s = q.shape[0]
    tq = _tile(s, tq)
    nq = s // tq

    def body(q_ref, k_ref, v_ref, do_ref, dq_ref, dk_ref, dv_ref, dk_acc, dv_acc):
        dk_acc[...] = jnp.zeros_like(dk_acc)
        dv_acc[...] = jnp.zeros_like(dv_acc)
        for qi in range(nq):
            kv = (qi + 1) * tq
            qt = q_ref[pl.ds(qi * tq, tq), :]
            kt = k_ref[pl.ds(0, kv), :]
            dot = do_ref[pl.ds(qi * tq, tq), :]
            sc = _masked_scores(qt, kt, qi, tq, kv)
            p = jnp.exp(sc - jnp.max(sc, axis=-1, keepdims=True))
            p = p / jnp.sum(p, axis=-1, keepdims=True)
            dp = lax.dot_general(dot, v_ref[pl.ds(0, kv), :], NT, preferred_element_type=F32)
            delta = jnp.sum(p * dp, axis=-1, keepdims=True)
            ds = (p * (dp - delta) * ATTN_SCALE).astype(BF16)
            dq_ref[pl.ds(qi * tq, tq), :] = lax.dot_general(ds, kt, NN, preferred_element_type=F32).astype(BF16)
            dk_acc[pl.ds(0, kv), :] += lax.dot_general(ds, qt, TN, preferred_element_type=F32)
            dv_acc[pl.ds(0, kv), :] += lax.dot_general(p.astype(BF16), dot, TN, preferred_element_type=F32)
        dk_ref[...] = dk_acc[...].astype(BF16)
        dv_ref[...] = dv_acc[...].astype(BF16)

    hq = pl.BlockSpec((s, HEAD_PAD), lambda h: (0, h))
    hv = pl.BlockSpec((s, V_HEAD), lambda h: (0, h))
    return pl.pallas_call(
        body, grid=(N_HEADS,), in_specs=[hq, hq, hv, hv], out_specs=[hq, hq, hv],
        out_shape=[_sds((s, N_HEADS * HEAD_PAD), BF16), _sds((s, N_HEADS * HEAD_PAD), BF16),
                   _sds((s, N_HEADS * V_HEAD), BF16)],
        scratch_shapes=[pltpu.VMEM((s, HEAD_PAD), F32), pltpu.VMEM((s, V_HEAD), F32)],
        compiler_params=_params(("parallel",)), name="attention_bwd")(q, k, v, do)


def mla_unrope_grads(dq, dk, cf, sf, tm=256):
    s = dq.shape[0]
    tm = _tile(s, tm)

    def body(dq_ref, dk_ref, cf_ref, sf_ref, dql_ref, dkn_ref, dkpe_ref):
        cfv, sfv = cf_ref[...], sf_ref[...]
        dkpe = jnp.zeros((tm, 128), F32)
        for h in range(N_HEADS):
            lo = h * HEAD_PAD
            dql_ref[:, lo:lo + QK_NOPE] = dq_ref[:, lo:lo + QK_NOPE]
            dql_ref[:, lo + QK_NOPE:lo + HEAD_PAD] = _unrope(
                dq_ref[:, lo + QK_NOPE:lo + HEAD_PAD].astype(F32), cfv, sfv).astype(BF16)
            dkn_ref[:, h * QK_NOPE:(h + 1) * QK_NOPE] = dk_ref[:, lo:lo + QK_NOPE]
            dkpe = dkpe + dk_ref[:, lo + QK_NOPE:lo + HEAD_PAD].astype(F32)
        dkpe_ref[...] = dkpe

    wq = N_HEADS * HEAD_PAD
    return pl.pallas_call(
        body, grid=(s // tm,), in_specs=[_rows(tm, wq), _rows(tm, wq), _rows(tm, 128), _rows(tm, 128)],
        out_specs=[_rows(tm, wq), _rows(tm, N_HEADS * QK_NOPE), _rows(tm, 128)],
        out_shape=[_sds((s, wq), BF16), _sds((s, N_HEADS * QK_NOPE), BF16), _sds((s, 128), F32)],
        compiler_params=_params(("parallel",)), name="mla_unrope_grads")(dq, dk, cf, sf)


ANY = pl.BlockSpec(memory_space=pl.ANY)


def _place():
    x, y, c = lax.axis_index("x"), lax.axis_index("y"), lax.axis_index("c")
    chips = [(1 - x, y), (x, 1 - y), (1 - x, 1 - y)]
    return x, y, c, chips


def _half(ref, hc, axis=0):
    n = ref.shape[axis] // 2
    idx = (slice(None),) * axis + (pl.ds(hc * n, n),)
    return ref.at[idx]


def gather_shards(tensors):
    nt = len(tensors)

    def body(*refs):
        a, g = refs[:nt], refs[nt:2 * nt]
        send, recv, lsem = refs[2 * nt:]
        x, y, c, chips = _place()
        q = 2 * x + y
        sib = (x, y, 1 - c)
        locs = [pltpu.make_async_copy(a[t], g[t].at[q], lsem.at[t]) for t in range(nt)]
        for cp in locs:
            cp.start()

        def slot(t, chip, hc):
            return _half(g[t].at[2 * chip[0] + chip[1]], hc)

        def rc(t, k, src, dst, to):
            return pltpu.make_async_remote_copy(src_ref=src, dst_ref=dst, send_sem=send.at[t, k], recv_sem=recv.at[t, k],
                                                device_id=to, device_id_type=MESH)

        sent = []
        for t in range(nt):
            for j, chip in enumerate(chips):
                cp = rc(t, j, _half(a[t], c), slot(t, (x, y), c), (*chip, c))
                cp.start()
                sent.append(cp)
        for t in range(nt):
            for j, chip in enumerate(chips):
                landed = slot(t, chip, c)
                rc(t, j, landed, landed, (*chip, c)).wait_recv()
                cp = rc(t, 3 + j, landed, landed, sib)
                cp.start()
                sent.append(cp)
        for t in range(nt):
            for j, chip in enumerate(chips):
                other = slot(t, chip, 1 - c)
                rc(t, 3 + j, other, other, sib).wait_recv()
        for cp in sent:
            cp.wait_send()
        for cp in locs:
            cp.wait()

    return pl.pallas_call(
        body, in_specs=[ANY] * nt, out_specs=[ANY] * nt,
        out_shape=[_sds((N_CHIPS,) + a.shape, a.dtype) for a in tensors],
        scratch_shapes=[pltpu.SemaphoreType.DMA((nt, 6)), pltpu.SemaphoreType.DMA((nt, 6)), pltpu.SemaphoreType.DMA((nt,))],
        name="gather_shards")(*tensors)


def pair_exchange(grads):
    nt = len(grads)

    def body(*refs):
        g, mine, theirs = refs[:nt], refs[nt:2 * nt], refs[2 * nt:3 * nt]
        send, recv, lsem = refs[3 * nt:]
        x, y, c, _ = _place()
        sib = (x, y, 1 - c)
        cps = []
        for t in range(nt):
            loc = pltpu.make_async_copy(_half(g[t], c, 1), mine[t], lsem.at[t])
            loc.start()
            rem = pltpu.make_async_remote_copy(src_ref=_half(g[t], 1 - c, 1), dst_ref=theirs[t], send_sem=send.at[t],
                                               recv_sem=recv.at[t], device_id=sib, device_id_type=MESH)
            rem.start()
            cps.append((loc, rem))
        for loc, rem in cps:
            rem.wait()
            loc.wait()

    half = [_sds((N_CHIPS, a.shape[1] // 2, a.shape[2]), a.dtype) for a in grads]
    return pl.pallas_call(
        body, in_specs=[ANY] * nt, out_specs=[ANY] * (2 * nt), out_shape=half + half,
        scratch_shapes=[pltpu.SemaphoreType.DMA((nt,)), pltpu.SemaphoreType.DMA((nt,)), pltpu.SemaphoreType.DMA((nt,))],
        name="pair_exchange")(*grads)


def chip_exchange(parts):
    nt = len(parts)

    def body(*refs):
        a, r = refs[:nt], refs[nt:2 * nt]
        send, recv, lsem = refs[2 * nt:]
        x, y, c, chips = _place()
        q = 2 * x + y
        cps = []
        for t in range(nt):
            loc = pltpu.make_async_copy(a[t].at[q], r[t].at[q], lsem.at[t])
            loc.start()
            cps.append(loc)
            for j, chip in enumerate(chips):
                rem = pltpu.make_async_remote_copy(
                    src_ref=a[t].at[2 * chip[0] + chip[1]], dst_ref=r[t].at[q], send_sem=send.at[t, j],
                    recv_sem=recv.at[t, j], device_id=(*chip, c), device_id_type=MESH)
                rem.start()
                cps.append(rem)
        for cp in cps:
            cp.wait()

    return pl.pallas_call(
        body, in_specs=[ANY] * nt, out_specs=[ANY] * nt, out_shape=[_sds(a.shape, a.dtype) for a in parts],
        scratch_shapes=[pltpu.SemaphoreType.DMA((nt, 3)), pltpu.SemaphoreType.DMA((nt, 3)), pltpu.SemaphoreType.DMA((nt,))],
        name="chip_exchange")(*parts)


def pair_share(halves, out_shapes, places):
    nt = len(halves)
    no = len(out_shapes)

    def body(*refs):
        h, out = refs[:nt], refs[nt:nt + no]
        send, recv, lsem = refs[nt + no:]
        x, y, c, _ = _place()
        sib = (x, y, 1 - c)
        cps = []
        for t in range(nt):
            oi, layer = places[t]
            full = out[oi] if layer is None else out[oi].at[layer]
            dst = _half(full, c)
            loc = pltpu.make_async_copy(h[t], dst, lsem.at[t])
            loc.start()
            rem = pltpu.make_async_remote_copy(src_ref=h[t], dst_ref=dst, send_sem=send.at[t], recv_sem=recv.at[t],
                                               device_id=sib, device_id_type=MESH)
            rem.start()
            cps.append((loc, rem, _half(full, 1 - c)))
        for t, (loc, rem, other) in enumerate(cps):
            loc.wait()
            rem.wait_send()
            pltpu.make_async_remote_copy(src_ref=other, dst_ref=other, send_sem=send.at[t], recv_sem=recv.at[t],
                                         device_id=sib, device_id_type=MESH).wait_recv()

    return pl.pallas_call(
        body, in_specs=[ANY] * nt, out_specs=[ANY] * no, out_shape=[_sds(sh, F32) for sh in out_shapes],
        scratch_shapes=[pltpu.SemaphoreType.DMA((nt,)), pltpu.SemaphoreType.DMA((nt,)), pltpu.SemaphoreType.DMA((nt,))],
        name="pair_share")(*halves)


def all_reduce_small(part):
    r, cdim = part.shape

    def body(p_ref, o_ref, buf, send, recv):
        x, y, c, _ = _place()
        me = 4 * x + 2 * y + c
        buf[me] = p_ref[...]
        cps = []
        for k in range(1, 8):
            to = (x ^ (k >> 2), y ^ ((k >> 1) & 1), c ^ (k & 1))
            cp = pltpu.make_async_remote_copy(src_ref=p_ref, dst_ref=buf.at[me], send_sem=send.at[k - 1],
                                              recv_sem=recv.at[k - 1], device_id=to, device_id_type=MESH)
            cp.start()
            cps.append(cp)
        for k in range(1, 8):
            frm = 4 * (x ^ (k >> 2)) + 2 * (y ^ ((k >> 1) & 1)) + (c ^ (k & 1))
            pltpu.make_async_remote_copy(src_ref=p_ref, dst_ref=buf.at[frm], send_sem=send.at[k - 1],
                                         recv_sem=recv.at[k - 1], device_id=(x, y, c), device_id_type=MESH).wait_recv()
        for cp in cps:
            cp.wait_send()
        acc = buf[0]
        for d in range(1, 8):
            acc = acc + buf[d]
        o_ref[...] = acc

    vm = pl.BlockSpec(memory_space=pltpu.VMEM)
    return pl.pallas_call(
        body, in_specs=[vm], out_specs=vm, out_shape=_sds((r, cdim), F32),
        scratch_shapes=[pltpu.VMEM((8, r, cdim), F32), pltpu.SemaphoreType.DMA((7,)), pltpu.SemaphoreType.DMA((7,))],
        name="all_reduce_small")(part)


def pair_sum(mine, theirs, tm=512):
    shape = mine.shape
    r, c = shape[0] * shape[1], shape[2]
    tm = _tile(r, tm)

    def body(a_ref, b_ref, o_ref):
        o_ref[...] = (a_ref[...].astype(F32) + b_ref[...].astype(F32)).astype(BF16)

    out = pl.pallas_call(
        body, grid=(r // tm,), in_specs=[_rows(tm, c), _rows(tm, c)], out_specs=_rows(tm, c),
        out_shape=_sds((r, c), BF16), compiler_params=_params(("parallel",)), name="pair_sum")(
            mine.reshape(r, c), theirs.reshape(r, c))
    return out.reshape(shape)


def chip_sum(parts, tm=256):
    _, r, c = parts.shape
    tm = _tile(r, tm)

    def body(p_ref, o_ref):
        acc = p_ref[0].astype(F32)
        for q in range(1, N_CHIPS):
            acc = acc + p_ref[q].astype(F32)
        o_ref[...] = acc

    return pl.pallas_call(
        body, grid=(r // tm,), in_specs=[pl.BlockSpec((N_CHIPS, tm, c), lambda i: (0, i, 0))], out_specs=_rows(tm, c),
        out_shape=_sds((r, c), F32), compiler_params=_params(("parallel",)), name="chip_sum")(parts)


def adamw(w, g, m, v, tm=256):
    shape = w.shape
    c = shape[-1]
    r = w.size // c
    tm = _tile(r, tm)
    bc1 = 1.0 - ADAM_B1 ** ADAM_STEP
    bc2 = 1.0 - ADAM_B2 ** ADAM_STEP

    def body(w_ref, g_ref, m_ref, v_ref, d_ref, nm_ref, nv_ref):
        gv = g_ref[...]
        nm = ADAM_B1 * m_ref[...] + (1.0 - ADAM_B1) * gv
        nv = ADAM_B2 * v_ref[...] + (1.0 - ADAM_B2) * (gv * gv)
        d_ref[...] = -ADAM_LR * ((nm / bc1) / (jnp.sqrt(nv / bc2) + ADAM_EPS) + ADAM_WD * w_ref[...])
        nm_ref[...] = nm
        nv_ref[...] = nv

    outs = pl.pallas_call(
        body, grid=(r // tm,), in_specs=[_rows(tm, c)] * 4, out_specs=[_rows(tm, c)] * 3,
        out_shape=[_sds((r, c), F32)] * 3, compiler_params=_params(("parallel",)), name="adamw")(
            w.reshape(r, c), g.reshape(r, c), m.reshape(r, c), v.reshape(r, c))
    return [o.reshape(shape) for o in outs]


WEIGHTS = ['sc_w_in', 'sc_conv_w', 'sc_w_out', 'mla_w_dq', 'mla_g_q', 'mla_w_uq', 'mla_w_dkv', 'mla_g_kv', 'mla_w_uk',
           'mla_w_uv', 'mla_w_o', 'cf_w_pw1', 'cf_b_pw1', 'cf_dw_w', 'cf_dw_b', 'cf_norm_g', 'cf_norm_b', 'cf_w_pw2',
           'cf_b_pw2', 'ff_w1', 'ff_w2', 'ln_mix_g', 'ln_mix_b', 'ln_ff_g', 'ln_ff_b']
ARGS = ['x'] + WEIGHTS + ['loss_target'] + ['m_' + n for n in WEIGHTS] + ['v_' + n for n in WEIGHTS]


def _mlp_forward(i, x, xb, w1, w2, g, b):
    s = x.shape[0]
    d_ff = w1.n
    tm, tn = _tile(s, 512), 512

    def epi(acc, e, o):
        o[0][...] = acc.astype(BF16)
        r = jnp.maximum(acc, 0.0)
        o[1][...] = (r * r).astype(BF16)

    hb, ab = mm_nn(f"mlp{i}_up", xb, w1, tm, tn, 512, epi, [_sds((s, d_ff), BF16)] * 2, [_ij(tm, tn)] * 2)
    y, yb, xh, rstd = mm_residual_ln(f"mlp{i}_down_ln", ab, w2, x, g, b)
    return (y, yb), dict(xb=xb, hb=hb, ab=ab, xh=xh, rstd=rstd, g=g)


def _mlp_backward(i, dy, sv, w1, w2, dw1, dw2):
    s = dy.shape[0]
    dr, drb, dg, db, _ = ln_backward(f"mlp{i}_ln_bwd", dy, sv["xh"], sv["rstd"], sv["g"])
    tm, tn = _tile(s, 512), 512

    def epi(acc, e, o):
        o[0][...] = (acc * (2.0 * jnp.maximum(e[0][...].astype(F32), 0.0))).astype(BF16)

    dhb = mm_nt(f"mlp{i}_down_bwd", drb, w2, s, tm, tn, 512, epi, [_sds((s, w2.k), BF16)], [_ij(tm, tn)],
                [sv["hb"]], [_ij(tm, tn)])[0]
    g_w2 = mm_tn(f"mlp{i}_dw2", sv["ab"], drb, dw2, s, 512, 512, 512)
    g_w1 = mm_tn(f"mlp{i}_dw1", sv["xb"], dhb, dw1, s, 512, 512, 512)
    dx = mm_plain_nt(f"mlp{i}_up_bwd", dhb, w1, F32, add=dr, add_scale=ALPHA)
    return dx, g_w1, g_w2, dg, db


def kernel(x, sc_w_in, sc_conv_w, sc_w_out, mla_w_dq, mla_g_q, mla_w_uq, mla_w_dkv, mla_g_kv, mla_w_uk, mla_w_uv, mla_w_o, cf_w_pw1, cf_b_pw1, cf_dw_w, cf_dw_b, cf_norm_g, cf_norm_b, cf_w_pw2, cf_b_pw2, ff_w1, ff_w2, ln_mix_g, ln_mix_b, ln_ff_g, ln_ff_b, loss_target, m_sc_w_in, m_sc_conv_w, m_sc_w_out, m_mla_w_dq, m_mla_g_q, m_mla_w_uq, m_mla_w_dkv, m_mla_g_kv, m_mla_w_uk, m_mla_w_uv, m_mla_w_o, m_cf_w_pw1, m_cf_b_pw1, m_cf_dw_w, m_cf_dw_b, m_cf_norm_g, m_cf_norm_b, m_cf_w_pw2, m_cf_b_pw2, m_ff_w1, m_ff_w2, m_ln_mix_g, m_ln_mix_b, m_ln_ff_g, m_ln_ff_b, v_sc_w_in, v_sc_conv_w, v_sc_w_out, v_mla_w_dq, v_mla_g_q, v_mla_w_uq, v_mla_w_dkv, v_mla_g_kv, v_mla_w_uk, v_mla_w_uv, v_mla_w_o, v_cf_w_pw1, v_cf_b_pw1, v_cf_dw_w, v_cf_dw_b, v_cf_norm_g, v_cf_norm_b, v_cf_w_pw2, v_cf_b_pw2, v_ff_w1, v_ff_w2, v_ln_mix_g, v_ln_mix_b, v_ln_ff_g, v_ln_ff_b):
    given = dict(zip(ARGS, (x, sc_w_in, sc_conv_w, sc_w_out, mla_w_dq, mla_g_q, mla_w_uq, mla_w_dkv, mla_g_kv, mla_w_uk, mla_w_uv, mla_w_o, cf_w_pw1, cf_b_pw1, cf_dw_w, cf_dw_b, cf_norm_g, cf_norm_b, cf_w_pw2, cf_b_pw2, ff_w1, ff_w2, ln_mix_g, ln_mix_b, ln_ff_g, ln_ff_b, loss_target, m_sc_w_in, m_sc_conv_w, m_sc_w_out, m_mla_w_dq, m_mla_g_q, m_mla_w_uq, m_mla_w_dkv, m_mla_g_kv, m_mla_w_uk, m_mla_w_uv, m_mla_w_o, m_cf_w_pw1, m_cf_b_pw1, m_cf_dw_w, m_cf_dw_b, m_cf_norm_g, m_cf_norm_b, m_cf_w_pw2, m_cf_b_pw2, m_ff_w1, m_ff_w2, m_ln_mix_g, m_ln_mix_b, m_ln_ff_g, m_ln_ff_b, v_sc_w_in, v_sc_conv_w, v_sc_w_out, v_mla_w_dq, v_mla_g_q, v_mla_w_uq, v_mla_w_dkv, v_mla_g_kv, v_mla_w_uk, v_mla_w_uv, v_mla_w_o, v_cf_w_pw1, v_cf_b_pw1, v_cf_dw_w, v_cf_dw_b, v_cf_norm_g, v_cf_norm_b, v_cf_w_pw2, v_cf_b_pw2, v_ff_w1, v_ff_w2, v_ln_mix_g, v_ln_mix_b, v_ln_ff_g, v_ln_ff_b)))
    s, d = x.shape[1], x.shape[2]
    d_ff = 4 * d
    dq4 = d // N_CHIPS
    xq = lax.axis_index("x") * 2 + lax.axis_index("y")

    w_dkv_pad = jnp.pad(mla_w_dkv[0], ((0, 0), (0, 128 - QK_ROPE)))
    w_uq_pad = jnp.pad(mla_w_uq[0].reshape(Q_LORA, 2, QK_NOPE + QK_ROPE), ((0, 0), (0, 0), (0, HEAD_PAD - QK_NOPE - QK_ROPE)))
    small = jnp.concatenate([
        sc_conv_w.reshape(2 * SC_WIDTH, dq4), cf_b_pw1.reshape(2, dq4), cf_dw_w[0], cf_dw_b, cf_norm_g, cf_norm_b,
        cf_b_pw2, jnp.zeros((5, dq4), F32)], axis=0)
    local = [
        sc_w_in.astype(BF16), sc_w_out.astype(BF16),
        jnp.concatenate([mla_w_dq[0], w_dkv_pad], axis=1).astype(BF16),
        w_uq_pad.reshape(Q_LORA, 2 * HEAD_PAD).astype(BF16),
        mla_w_uk.reshape(KV_LORA // N_CHIPS, N_HEADS * QK_NOPE).astype(BF16),
        mla_w_uv.reshape(KV_LORA // N_CHIPS, N_HEADS * V_HEAD).astype(BF16),
        mla_w_o[0].astype(BF16), cf_w_pw1[0].astype(BF16), cf_w_pw2[0].astype(BF16),
        ff_w1.astype(BF16), ff_w2.astype(BF16), small]
    (g_in, g_out, g_dqkv, g_uq, g_uk, g_uv, g_o, g_pw1, g_pw2, g_w1, g_w2, g_small) = gather_shards(local)

    wd_t = Q_LORA + KV_LORA + 128
    w_in = [Stk("col", d, 3 * d, g_in, 2, j) for j in range(2)]
    w_out = [Stk("row", d, d, g_out, 2, j) for j in range(2)]
    w_dqkv = Stk("row", d, wd_t, g_dqkv)
    w_uq = Stk("col", Q_LORA, N_HEADS * HEAD_PAD, g_uq)
    w_uk = Stk("row", KV_LORA, N_HEADS * QK_NOPE, g_uk)
    w_uv = Stk("row", KV_LORA, N_HEADS * V_HEAD, g_uv)
    w_o = Stk("row", d, d, g_o)
    w_pw1 = Stk("col", d, 2 * d, g_pw1)
    w_pw2 = Stk("row", d, d, g_pw2)
    w_1 = [Stk("col", d, d_ff, g_w1, DEPTH, i) for i in range(DEPTH)]
    w_2 = [Stk("row", d_ff, d, g_w2, DEPTH, i) for i in range(DEPTH)]

    def wide(rows):
        return jnp.swapaxes(rows, 0, 1).reshape(rows.shape[1], d)

    conv_w = wide(g_small[:, 0:6]).reshape(2, SC_WIDTH, d)
    b_pw1 = g_small[:, 6:8].reshape(1, 2 * d)
    dw_w = wide(g_small[:, 8:39])
    dw_b, norm_g, norm_b, b_pw2 = (wide(g_small[:, 39 + k:40 + k]) for k in range(4))

    pos = jnp.arange(s, dtype=F32)
    inv_freq = ROPE_THETA ** (-jnp.arange(0, QK_ROPE, 2, dtype=F32) / QK_ROPE)
    ang = pos[:, None] * inv_freq[None, :]
    cos, sin, zero = jnp.cos(ang), jnp.sin(ang), jnp.zeros((s, 128 - QK_ROPE), F32)
    cf = jnp.concatenate([cos, cos, zero], axis=1)
    sf = jnp.concatenate([-sin, sin, zero], axis=1)

    def row(a, i):
        return a[i:i + 1]

    xs = x.reshape(s, d)
    cur = (xs, xs.astype(BF16))
    tape = []
    for i in range(DEPTH):
        mixer, j = i % 3, i // 3
        xf, xb = cur
        lg, lb = row(ln_mix_g, i), row(ln_mix_b, i)
        if mixer == 0:
            u = mm_plain_nn(f"sc{j}_in", xb, w_in[j], F32, tn=256)
            gb = short_conv_gate(u, conv_w[j])
            y, yb, xh, rstd = mm_residual_ln(f"sc{j}_out_ln", gb, w_out[j], xf, lg, lb, tk=256)
            sv = dict(xb=xb, u=u, gb=gb)
        elif mixer == 1:
            t = mm_plain_nn("mla_down", xb, w_dqkv, F32, tn=wd_t // 2)
            cq, ckv, kpe = mla_latents(t, mla_g_q, mla_g_kv, cf, sf)
            qh = mla_queries(cq, w_uq, cf, sf)
            kh = mla_keys(ckv, w_uk, kpe)
            vh = mm_plain_nn("mla_values", ckv, w_uv, BF16, tk=KV_LORA)
            oh = attention(qh, kh, vh)
            y, yb, xh, rstd = mm_residual_ln("mla_out_ln", oh, w_o, xf, lg, lb)
            sv = dict(xb=xb, t=t, cq=cq, ckv=ckv, qh=qh, kh=kh, vh=vh, oh=oh)
        else:
            u = mm_plain_nn("cf_pw1", xb, w_pw1, F32, bias=b_pw1)
            hc = conformer_glu_conv(u, dw_w, dw_b)
            sb = conformer_norm_swish(hc, norm_g, norm_b)
            y, yb, xh, rstd = mm_residual_ln("cf_pw2_ln", sb, w_pw2, xf, lg, lb, bias=b_pw2)
            sv = dict(xb=xb, u=u, hc=hc, sb=sb)
        sv.update(xh=xh, rstd=rstd, g=lg)
        cur, sv_mlp = _mlp_forward(i, y, yb, w_1[i], w_2[i], row(ln_ff_g, i), row(ln_ff_b, i))
        tape.append((sv, sv_mlp))

    dy, loss_part = loss_head(cur[0], loss_target.reshape(s, d))
    loss = lax.psum(loss_part[0, 0], ("x", "y", "c"))

    grads = {}
    smalls = {}
    g_ln = {n: [None] * DEPTH for n in ("ln_mix_g", "ln_mix_b", "ln_ff_g", "ln_ff_b")}
    conv_grads = [None, None]
    for i in reversed(range(DEPTH)):
        mixer, j = i % 3, i // 3
        sv, sv_mlp = tape[i]
        dy, grads[f"w1_{i}"], grads[f"w2_{i}"], g_ln["ln_ff_g"][i], g_ln["ln_ff_b"][i] = _mlp_backward(
            i, dy, sv_mlp, w_1[i], w_2[i], Stk("col", d, d_ff), Stk("row", d_ff, d))
        dr, drb, g_ln["ln_mix_g"][i], g_ln["ln_mix_b"][i], dr_sum = ln_backward(
            f"mix{i}_ln_bwd", dy, sv["xh"], sv["rstd"], sv["g"])
        if mixer == 0:
            dgate = mm_plain_nt(f"sc{j}_out_bwd", drb, w_out[j], F32, tn=256)
            grads[f"out_{j}"] = mm_tn(f"sc{j}_dw_out", sv["gb"], drb, Stk("row", d, d), s, 512, 512, 512)
            du, conv_grads[j] = short_conv_gate_bwd(sv["u"], conv_w[j], dgate)
            nb = d // 256
            grads[f"in_{j}"] = mm_tn(
                f"sc{j}_dw_in", sv["xb"], du, Stk("col", d, 3 * d), s, 512, 256, 512,
                b_spec=pl.BlockSpec((None, _tile(s, 512), 256), lambda i_, j_, k_: (j_ // nb, k_, j_ % nb)))
            dy = mm_plain_nt(
                f"sc{j}_in_bwd", du, w_in[j], F32, tk=256, add=dr, add_scale=ALPHA,
                a_spec_fn=(s, lambda tm, tk: pl.BlockSpec((None, tm, tk), lambda i_, j_, k_: (k_ // nb, i_, k_ % nb))))
        elif mixer == 1:
            do = mm_plain_nt("mla_out_bwd", drb, w_o, BF16)
            grads["o"] = mm_tn("mla_dw_o", sv["oh"], drb, Stk("row", d, d), s, 512, 512, 512)
            dqh, dkh, dvh = attention_bwd(sv["qh"], sv["kh"], sv["vh"], do)
            dql, dkn, dkpe = mla_unrope_grads(dqh, dkh, cf, sf)
            grads["uq"] = mm_tn("mla_dw_uq", sv["cq"], dql, Stk("col", Q_LORA, N_HEADS * HEAD_PAD), s, Q_LORA, 512, 512)
            dcq = mm_plain_nt("mla_uq_bwd", dql, w_uq, F32, tn=Q_LORA)
            grads["uk"] = mm_tn("mla_dw_uk", sv["ckv"], dkn, Stk("row", KV_LORA, N_HEADS * QK_NOPE), s, KV_LORA, 512, 512)
            grads["uv"] = mm_tn("mla_dw_uv", sv["ckv"], dvh, Stk("row", KV_LORA, N_HEADS * V_HEAD), s, KV_LORA, 512, 512)
            dckv = mm_plain_nt("mla_uk_bwd", dkn, w_uk, F32, tn=KV_LORA)
            dckv = mm_plain_nt("mla_uv_bwd", dvh, w_uv, F32, tn=KV_LORA, add=dckv)
            dt, smalls["g_q"], smalls["g_kv"] = mla_latents_bwd(sv["t"], mla_g_q, mla_g_kv, cf, sf, dcq, dckv, dkpe)
            grads["dqkv"] = mm_tn("mla_dw_down", sv["xb"], dt, Stk("row", d, wd_t), s, 512, wd_t // 2, 512)
            dy = mm_plain_nt("mla_down_bwd", dt, w_dqkv, F32, tk=wd_t // 2, add=dr, add_scale=ALPHA)
        else:
            dsw = mm_plain_nt("cf_pw2_bwd", drb, w_pw2, F32)
            grads["pw2"] = mm_tn("cf_dw_pw2", sv["sb"], drb, Stk("row", d, d), s, 512, 512, 512)
            smalls["b_pw2"] = dr_sum
            dhc, smalls["norm_g"], smalls["norm_b"] = conformer_norm_swish_bwd(sv["hc"], norm_g, norm_b, dsw)
            du, smalls["b_pw1"], smalls["dw_w"], smalls["dw_b"] = conformer_glu_conv_bwd(sv["u"], dw_w, dhc)
            nb = d // 512
            grads["pw1"] = mm_tn(
                "cf_dw_pw1", sv["xb"], du, Stk("col", d, 2 * d), s, 512, 512, 512,
                b_spec=pl.BlockSpec((None, _tile(s, 512), 512), lambda i_, j_, k_: (j_ // nb, k_, j_ % nb)))
            dy = mm_plain_nt(
                "cf_pw1_bwd", du, w_pw1, F32, add=dr, add_scale=ALPHA,
                a_spec_fn=(s, lambda tm, tk: pl.BlockSpec((None, tm, tk), lambda i_, j_, k_: (k_ // nb, i_, k_ % nb))))
    grad_x = dy.reshape(1, s, d)

    order = ([f"in_{j}" for j in range(2)] + [f"out_{j}" for j in range(2)] + ["dqkv", "uq", "uk", "uv", "o", "pw1", "pw2"]
             + [f"w1_{i}" for i in range(DEPTH)] + [f"w2_{i}" for i in range(DEPTH)])
    mine, theirs = (lambda r: (r[:len(order)], r[len(order):]))(pair_exchange([grads[n] for n in order]))
    pairs = [pair_sum(a, b) for a, b in zip(mine, theirs)]
    landed = chip_exchange(pairs)
    halves = [chip_sum(p) for p in landed]
    out_shapes = [(2, d, 3 * dq4), (2, dq4, d), (dq4, wd_t), (Q_LORA, 2 * HEAD_PAD), (KV_LORA // N_CHIPS, d),
                  (KV_LORA // N_CHIPS, d), (dq4, d), (d, 2 * dq4), (dq4, d), (DEPTH, d, d), (DEPTH, d, d)]
    places = ([(0, 0), (0, 1), (1, 0), (1, 1)] + [(k, None) for k in range(2, 9)]
              + [(9, i) for i in range(DEPTH)] + [(10, i) for i in range(DEPTH)])
    (gf_in, gf_out, gf_dqkv, gf_uq, gf_uk, gf_uv, gf_o, gf_pw1, gf_pw2, gf_w1, gf_w2) = pair_share(halves, out_shapes, places)

    pad_row = lambda a: jnp.pad(a, ((0, 0), (0, d - a.shape[1])))
    small_part = jnp.concatenate(
        [jnp.concatenate(g_ln[n], axis=0) for n in ("ln_mix_g", "ln_mix_b", "ln_ff_g", "ln_ff_b")]
        + [pad_row(smalls["g_q"]), pad_row(smalls["g_kv"]), conv_grads[0], conv_grads[1],
           smalls["b_pw1"].reshape(2, d), smalls["dw_w"], smalls["dw_b"], smalls["norm_g"], smalls["norm_b"],
           smalls["b_pw2"], jnp.zeros((3, d), F32)], axis=0)
    red = all_reduce_small(small_part)

    def shard(rows):
        return lax.dynamic_slice_in_dim(rows, xq * dq4, dq4, axis=1)

    gw = {
        "sc_w_in": gf_in, "sc_w_out": gf_out,
        "mla_w_dq": gf_dqkv[None, :, 0:Q_LORA], "mla_w_dkv": gf_dqkv[None, :, Q_LORA:Q_LORA + KV_LORA + QK_ROPE],
        "mla_w_uq": gf_uq.reshape(Q_LORA, 2, HEAD_PAD)[:, :, 0:QK_NOPE + QK_ROPE].reshape(1, Q_LORA, 2 * (QK_NOPE + QK_ROPE)),
        "mla_w_uk": gf_uk.reshape(mla_w_uk.shape), "mla_w_uv": gf_uv.reshape(mla_w_uv.shape),
        "mla_w_o": gf_o[None], "cf_w_pw1": gf_pw1[None], "cf_w_pw2": gf_pw2[None], "ff_w1": gf_w1, "ff_w2": gf_w2,
        "ln_mix_g": red[0:4], "ln_mix_b": red[4:8], "ln_ff_g": red[8:12], "ln_ff_b": red[12:16],
        "mla_g_q": red[16:17, 0:Q_LORA], "mla_g_kv": red[17:18, 0:KV_LORA],
        "sc_conv_w": shard(red[18:24]).reshape(2, SC_WIDTH, dq4),
        "cf_b_pw1": lax.dynamic_slice_in_dim(red[24:26].reshape(1, 2 * d), xq * 2 * dq4, 2 * dq4, axis=1),
        "cf_dw_w": shard(red[26:57])[None], "cf_dw_b": shard(red[57:58]), "cf_norm_g": shard(red[58:59]),
        "cf_norm_b": shard(red[59:60]), "cf_b_pw2": shard(red[60:61]),
    }

    big = ["sc_w_in", "sc_w_out", "mla_w_dq", "mla_w_uq", "mla_w_dkv", "mla_w_uk", "mla_w_uv", "mla_w_o", "cf_w_pw1",
           "cf_w_pw2", "ff_w1", "ff_w2"]
    upd = {}
    for n in big:
        w2d = given[n].reshape(-1, given[n].shape[-1]) if n not in ("mla_w_uk", "mla_w_uv") else given[n].reshape(-1, d)
        sh = w2d.shape
        res = adamw(w2d, gw[n].reshape(sh), given["m_" + n].reshape(sh), given["v_" + n].reshape(sh))
        upd[n] = [r.reshape(given[n].shape) for r in res]

    def pack(names, width, get):
        return jnp.concatenate([get(n).reshape(-1, width) for n in names], axis=0)

    def unpack(names, packed):
        out, at = {}, 0
        for n in names:
            rows = given[n].size // packed.shape[1]
            out[n] = packed[at:at + rows].reshape(given[n].shape)
            at += rows
        return out

    rep = ["ln_mix_g", "ln_mix_b", "ln_ff_g", "ln_ff_b"]
    shd = ["sc_conv_w", "cf_b_pw1", "cf_dw_w", "cf_dw_b", "cf_norm_g", "cf_norm_b", "cf_b_pw2"]
    for names, width in ((rep, d), (shd, dq4), (["mla_g_q"], Q_LORA), (["mla_g_kv"], KV_LORA)):
        res = adamw(pack(names, width, lambda n: given[n]), pack(names, width, lambda n: gw[n]),
                    pack(names, width, lambda n: given["m_" + n]), pack(names, width, lambda n: given["v_" + n]), tm=4096)
        parts = [unpack(names, r) for r in res]
        for n in names:
            upd[n] = [p[n] for p in parts]

    return (loss, grad_x, *[gw[n].reshape(given[n].shape) for n in WEIGHTS], *[upd[n][0] for n in WEIGHTS],
            *[upd[n][1] for n in WEIGHTS], *[upd[n][2] for n in WEIGHTS])
```

```python
import jax
import jax.numpy as jnp
from jax import lax
from jax.experimental import pallas as pl
from jax.experimental.pallas import tpu as pltpu

F32 = jnp.float32
BF16 = jnp.bfloat16
MESH = pl.DeviceIdType.MESH

DEPTH = 4
ALPHA = (2.0 * DEPTH) ** 0.25
LN_EPS = 1e-5
RMS_EPS = 1e-6
CHUNK_SHIFT = 6
N_HEADS = 8
QK_NOPE = 128
QK_ROPE = 64
V_HEAD = 128
HEAD_PAD = 256
Q_LORA = 384
KV_LORA = 256
ROPE_THETA = 10000.0
SC_WIDTH = 3
CONF_WIDTH = 31
CONV_PAD = 32
CONV_CHUNK = 64
N_CHIPS = 4
ATTN_SCALE = (QK_NOPE + QK_ROPE) ** -0.5

ADAM_LR = 0.001
ADAM_B1 = 0.9
ADAM_B2 = 0.999
ADAM_EPS = 1e-08
ADAM_WD = 0.01
ADAM_STEP = 10

VMEM_LIMIT = 56 * 2**20

NN = (((1,), (0,)), ((), ()))
NT = (((1,), (1,)), ((), ()))
TN = (((0,), (0,)), ((), ()))


def _params(sem=None):
    return pltpu.CompilerParams(dimension_semantics=sem, vmem_limit_bytes=VMEM_LIMIT)


class Stk:
    def __init__(self, kind, k, n, arr=None, layers=None, layer=None):
        self.kind, self.k, self.n, self.layers, self.layer = kind, k, n, layers, layer
        self.plain = kind == "row" and layers is None
        self.kloc = k // N_CHIPS if kind == "row" else k
        self.nloc = n // N_CHIPS if kind == "col" else n
        if arr is not None and self.plain:
            arr = arr.reshape(k, n)
        self.arr = arr

    @property
    def shape(self):
        if self.plain:
            return (self.k, self.n)
        lead = (N_CHIPS,) if self.layers is None else (N_CHIPS, self.layers)
        return lead + (self.kloc, self.nloc)

    def spec(self, bk, bn, f):
        if self.plain:
            return pl.BlockSpec((bk, bn), f)
        assert self.kloc % bk == 0 and self.nloc % bn == 0, (self.kloc, bk, self.nloc, bn)
        pk, pn = self.kloc // bk, self.nloc // bn
        kind, layer = self.kind, self.layer

        def imap(*g):
            kb, nb = f(*g)
            if kind == "row":
                q, kb, nb = kb // pk, kb % pk, nb
            else:
                q, kb, nb = nb // pn, kb, nb % pn
            return (q, kb, nb) if layer is None else (q, layer, kb, nb)

        block = (None, bk, bn) if layer is None else (None, None, bk, bn)
        return pl.BlockSpec(block, imap)


def _mm(name, mode, a, b, grid, a_spec, b_spec, acc_shape, extras, extra_specs, out_shapes, out_specs, epi, a_fn=None):
    nk = grid[2]
    ne = len(extras)

    def body(*refs):
        a_ref, b_ref = refs[0], refs[1]
        e_refs = refs[2:2 + ne]
        av = a_ref[...] if a_fn is None else a_fn(a_ref[...])
        part = lax.dot_general(av, b_ref[...], mode, preferred_element_type=F32)
        if nk == 1:
            epi(part, e_refs, refs[2 + ne:])
            return
        o_refs = refs[2 + ne:-1]
        acc = refs[-1]
        k = pl.program_id(2)

        @pl.when(k == 0)
        def _():
            acc[...] = part

        @pl.when(k > 0)
        def _():
            acc[...] += part

        @pl.when(k == nk - 1)
        def _():
            epi(acc[...], e_refs, o_refs)

    return pl.pallas_call(
        body, grid=grid, in_specs=[a_spec, b_spec, *extra_specs], out_specs=out_specs, out_shape=out_shapes,
        scratch_shapes=[pltpu.VMEM(acc_shape, F32)] if nk > 1 else [],
        compiler_params=_params(("parallel", "parallel", "arbitrary")), name=name)(a, b, *extras)


def _tile(n, t):
    t = min(n, t)
    while n % t:
        t -= 8
    assert t > 0, (n, t)
    return t


def mm_nn(name, a, w, tm, tn, tk, epi, out_shapes, out_specs, extras=(), extra_specs=(), a_spec=None, a_fn=None):
    m = a.shape[0]
    tm, tn, tk = _tile(m, tm), _tile(w.n, tn), _tile(w.k, tk)
    grid = (m // tm, w.n // tn, w.k // tk)
    a_spec = a_spec or pl.BlockSpec((tm, tk), lambda i, j, k: (i, k))
    b_spec = w.spec(tk, tn, lambda i, j, k: (k, j))
    return _mm(name, NN, a, w.arr, grid, a_spec, b_spec, (tm, tn), extras, extra_specs, out_shapes, out_specs, epi, a_fn)


def mm_nt(name, a, w, m, tm, tn, tk, epi, out_shapes, out_specs, extras=(), extra_specs=(), a_spec=None):
    tm, tn, tk = _tile(m, tm), _tile(w.k, tn), _tile(w.n, tk)
    grid = (m // tm, w.k // tn, w.n // tk)
    a_spec = a_spec or pl.BlockSpec((tm, tk), lambda i, j, k: (i, k))
    b_spec = w.spec(tn, tk, lambda i, j, k: (j, k))
    return _mm(name, NT, a, w.arr, grid, a_spec, b_spec, (tm, tn), extras, extra_specs, out_shapes, out_specs, epi)


def mm_tn(name, a, b, dw, s, tm=512, tn=512, tk=4096, a_spec=None, b_spec=None, a_fn=None):
    tm, tn, tk = _tile(dw.k, tm), _tile(dw.n, tn), _tile(s, tk)
    grid = (dw.k // tm, dw.n // tn, s // tk)
    a_spec = a_spec or pl.BlockSpec((tk, tm), lambda i, j, k: (k, i))
    b_spec = b_spec or pl.BlockSpec((tk, tn), lambda i, j, k: (k, j))

    def epi(acc, e, o):
        o[0][...] = acc.astype(BF16)

    out = _mm(name, TN, a, b, grid, a_spec, b_spec, (tm, tn), (), (), [jax.ShapeDtypeStruct(dw.shape, BF16)],
              [dw.spec(tm, tn, lambda i, j, k: (i, j))], epi, a_fn)[0]
    return out.reshape(N_CHIPS, dw.k // N_CHIPS, dw.n) if dw.plain else out


def _sds(shape, dtype):
    return jax.ShapeDtypeStruct(shape, dtype)


def _ij(tm, tn):
    return pl.BlockSpec((tm, tn), lambda i, j, k: (i, j))


def _i0(tm, c):
    return pl.BlockSpec((tm, c), lambda i, j, k: (i, 0))


def _0j(r, tn):
    return pl.BlockSpec((r, tn), lambda i, j, k: (0, j))


def _layer_norm_rows(r, g, b):
    mu = jnp.mean(r, axis=-1, keepdims=True)
    d = r - mu
    var = jnp.mean(d * d, axis=-1, keepdims=True)
    rstd = lax.rsqrt(var + LN_EPS)
    xh = d * rstd
    return xh * g + b, xh, rstd


def mm_residual_ln(name, a, w, x, g, b, bias=None, tm=512, tk=1024, a_fn=None):
    s, d = x.shape
    tm = _tile(s, tm)
    extras = [x, g, b] + ([bias] if bias is not None else [])
    especs = [_i0(tm, d), _0j(1, d), _0j(1, d)] + ([_0j(1, d)] if bias is not None else [])

    def epi(acc, e, o):
        r = ALPHA * e[0][...] + acc
        if bias is not None:
            r = r + e[3][...]
        y, xh, rstd = _layer_norm_rows(r, e[1][...], e[2][...])
        o[0][...] = y
        o[1][...] = y.astype(BF16)
        o[2][...] = xh
        o[3][...] = rstd

    return mm_nn(name, a, w, tm, d, tk, epi,
                 [_sds((s, d), F32), _sds((s, d), BF16), _sds((s, d), F32), _sds((s, 1), F32)],
                 [_i0(tm, d), _i0(tm, d), _i0(tm, d), _i0(tm, 1)], extras, especs, a_fn=a_fn)


def mm_plain_nn(name, a, w, out_dtype, tm=1024, tn=512, tk=1024, bias=None):
    m = a.shape[0]
    tm, tn = _tile(m, tm), _tile(w.n, tn)
    if w.kind == "col":
        tn = _tile(w.nloc, tn)

    def epi(acc, e, o):
        if bias is not None:
            acc = acc + e[0][...]
        o[0][...] = acc.astype(out_dtype)

    extras, especs = ([bias], [_0j(1, tn)]) if bias is not None else ((), ())
    return mm_nn(name, a, w, tm, tn, tk, epi, [_sds((m, w.n), out_dtype)], [_ij(tm, tn)], extras, especs)[0]


def mm_plain_nt(name, a, w, out_dtype, tm=1024, tn=512, tk=1024, add=None, add_scale=1.0, a_spec_fn=None):
    m = a.shape[0] if a_spec_fn is None else a_spec_fn[0]
    tm, tn = _tile(m, tm), _tile(w.k, tn)
    tk = _tile(w.n, tk)
    if w.kind == "col":
        tk = _tile(w.nloc, tk)
    if w.kind == "row" and not w.plain:
        tn = _tile(w.kloc, tn)

    def epi(acc, e, o):
        if add is not None:
            acc = acc + add_scale * e[0][...].astype(F32)
        o[0][...] = acc.astype(out_dtype)

    extras, especs = ([add], [_ij(tm, tn)]) if add is not None else ((), ())
    a_spec = None if a_spec_fn is None else a_spec_fn[1](tm, tk)
    return mm_nt(name, a, w, m, tm, tn, tk, epi, [_sds((m, w.k), out_dtype)], [_ij(tm, tn)], extras, especs,
                 a_spec=a_spec)[0]


def _rows(tm, c):
    return pl.BlockSpec((tm, c), lambda i: (i, 0))


def _fix(shape):
    nd = len(shape)
    return pl.BlockSpec(shape, lambda i: (0,) * nd)


def _accumulate(ref, val):
    @pl.when(pl.program_id(0) == 0)
    def _():
        ref[...] = jnp.zeros_like(ref)

    ref[...] += val


def ln_backward(name, dy, xhat, rstd, g, tm=256):
    s, d = dy.shape
    tm = _tile(s, tm)

    def body(dy_ref, xh_ref, rstd_ref, g_ref, dr_ref, drb_ref, dg_ref, db_ref, ds_ref):
        dyv, xh = dy_ref[...], xh_ref[...]
        dxh = dyv * g_ref[...]
        m1 = jnp.mean(dxh, axis=-1, keepdims=True)
        m2 = jnp.mean(dxh * xh, axis=-1, keepdims=True)
        dr = rstd_ref[...] * (dxh - m1 - xh * m2)
        dr_ref[...] = dr
        drb_ref[...] = dr.astype(BF16)
        _accumulate(dg_ref, jnp.sum(dyv * xh, axis=0, keepdims=True))
        _accumulate(db_ref, jnp.sum(dyv, axis=0, keepdims=True))
        _accumulate(ds_ref, jnp.sum(dr, axis=0, keepdims=True))

    return pl.pallas_call(
        body, grid=(s // tm,),
        in_specs=[_rows(tm, d), _rows(tm, d), _rows(tm, 1), _fix((1, d))],
        out_specs=[_rows(tm, d), _rows(tm, d), _fix((1, d)), _fix((1, d)), _fix((1, d))],
        out_shape=[_sds((s, d), F32), _sds((s, d), BF16), _sds((1, d), F32), _sds((1, d), F32), _sds((1, d), F32)],
        compiler_params=_params(("arbitrary",)), name=name)(dy, xhat, rstd, g)


def loss_head(y, target, tm=256):
    s, d = y.shape
    tm = _tile(s, tm)

    def body(y_ref, t_ref, dy_ref, loss_ref):
        e = y_ref[...] - t_ref[...]
        dy_ref[...] = e * (1.0 / d)
        part = 0.5 * jnp.sum(jnp.mean(e * e, axis=-1, keepdims=True), axis=0, keepdims=True)
        _accumulate(loss_ref, jnp.broadcast_to(part, (1, 128)))

    return pl.pallas_call(
        body, grid=(s // tm,), in_specs=[_rows(tm, d), _rows(tm, d)],
        out_specs=[_rows(tm, d), _fix((1, 128))], out_shape=[_sds((s, d), F32), _sds((1, 128), F32)],
        compiler_params=_params(("arbitrary",)), name="loss_head")(y, target)


def _cols(s, tc, off=0):
    return pl.BlockSpec((s, tc), lambda i: (0, i + off))


def _shift_down(z, sft, rows):
    return jnp.where(rows >= sft, pltpu.roll(z, sft, 0), 0.0)


def _shift_up(z, sft, rows, s):
    return jnp.where(rows < s - sft, pltpu.roll(z, (s - sft) % s, 0), 0.0)


def short_conv_gate(u, conv_w, tc=256):
    s, d3 = u.shape
    d = d3 // 3
    nb = d // tc

    def body(b_ref, c_ref, h_ref, w_ref, o_ref):
        rows = lax.broadcasted_iota(jnp.int32, (s, tc), 0)
        z = c_ref[...] * h_ref[...]
        cz = jnp.zeros((s, tc), F32)
        for k in range(SC_WIDTH):
            sft = SC_WIDTH - 1 - k
            cz = cz + w_ref[pl.ds(k, 1), :] * (_shift_down(z, sft, rows) if sft else z)
        o_ref[...] = (b_ref[...] * cz).astype(BF16)

    return pl.pallas_call(
        body, grid=(nb,),
        in_specs=[_cols(s, tc), _cols(s, tc, nb), _cols(s, tc, 2 * nb), _cols(SC_WIDTH, tc)],
        out_specs=_cols(s, tc), out_shape=_sds((s, d), BF16),
        compiler_params=_params(("parallel",)), name="short_conv_gate")(u, u, u, conv_w)


def short_conv_gate_bwd(u, conv_w, dg, tc=256):
    s, d3 = u.shape
    d = d3 // 3
    nb = d // tc

    def body(b_ref, c_ref, h_ref, w_ref, dg_ref, du_ref, dw_ref):
        rows = lax.broadcasted_iota(jnp.int32, (s, tc), 0)
        c, h, dgv = c_ref[...], h_ref[...], dg_ref[...]
        z = c * h
        dcz = dgv * b_ref[...]
        cz = jnp.zeros((s, tc), F32)
        dz = jnp.zeros((s, tc), F32)
        for k in range(SC_WIDTH):
            sft = SC_WIDTH - 1 - k
            zs = _shift_down(z, sft, rows) if sft else z
            wk = w_ref[pl.ds(k, 1), :]
            cz = cz + wk * zs
            dz = dz + wk * (_shift_up(dcz, sft, rows, s) if sft else dcz)
            dw_ref[pl.ds(k, 1), :] = jnp.sum(dcz * zs, axis=0, keepdims=True)
        du_ref[0] = (dgv * cz).astype(BF16)
        du_ref[1] = (dz * h).astype(BF16)
        du_ref[2] = (dz * c).astype(BF16)

    return pl.pallas_call(
        body, grid=(nb,),
        in_specs=[_cols(s, tc), _cols(s, tc, nb), _cols(s, tc, 2 * nb), _cols(SC_WIDTH, tc), _cols(s, tc)],
        out_specs=[pl.BlockSpec((3, s, tc), lambda i: (0, 0, i)), _cols(SC_WIDTH, tc)],
        out_shape=[_sds((3, s, d), BF16), _sds((SC_WIDTH, d), F32)],
        compiler_params=_params(("parallel",)), name="short_conv_gate_bwd")(u, u, u, conv_w, dg)


def _store_shifted_down(ref, z, rows):
    s, tc = z.shape
    for b in range(8):
        ref[b, pl.ds(0, CONV_PAD), :] = jnp.zeros((CONV_PAD, tc), F32)
        ref[b, pl.ds(CONV_PAD, s), :] = z if b == 0 else _shift_down(z, b, rows)


def _store_shifted_up(ref, z, rows):
    s, tc = z.shape
    for b in range(8):
        ref[b, pl.ds(0, s), :] = z if b == 0 else _shift_up(z, b, rows, s)
        ref[b, pl.ds(s, CONV_PAD), :] = jnp.zeros((CONV_PAD, tc), F32)


def conformer_glu_conv(u, dw_w, dw_b, tc=128):
    s, d2 = u.shape
    d = d2 // 2
    nb = d // tc

    ch = min(CONV_CHUNK, s)

    def body(a_ref, g_ref, w_ref, b_ref, o_ref, down):
        rows = lax.broadcasted_iota(jnp.int32, (s, tc), 0)
        _store_shifted_down(down, a_ref[...] * jax.nn.sigmoid(g_ref[...]), rows)

        def chunk(ci, carry):
            r0 = pl.multiple_of(ci * ch, ch)
            acc = jnp.broadcast_to(b_ref[...], (ch, tc))
            for k in range(CONF_WIDTH):
                sft = CONF_WIDTH - 1 - k
                acc = acc + w_ref[pl.ds(k, 1), :] * down[sft % 8, pl.ds(CONV_PAD + r0 - (sft // 8) * 8, ch), :]
            o_ref[pl.ds(r0, ch), :] = acc
            return carry

        lax.fori_loop(0, s // ch, chunk, 0)

    return pl.pallas_call(
        body, grid=(nb,),
        in_specs=[_cols(s, tc), _cols(s, tc, nb), _cols(CONF_WIDTH, tc), _cols(1, tc)],
        out_specs=_cols(s, tc), out_shape=_sds((s, d), F32),
        scratch_shapes=[pltpu.VMEM((8, CONV_PAD + s, tc), F32)],
        compiler_params=_params(("parallel",)), name="conformer_glu_conv")(u, u, dw_w, dw_b)


def conformer_glu_conv_bwd(u, dw_w, dhc, tc=128):
    s, d2 = u.shape
    d = d2 // 2
    nb = d // tc
    ch = min(CONV_CHUNK, s)

    def body(a_ref, g_ref, w_ref, dhc_ref, du_ref, dbias_ref, dw_ref, db_ref, down, up, dw_acc, dh_buf):
        rows = lax.broadcasted_iota(jnp.int32, (s, tc), 0)
        a = a_ref[...]
        sg = jax.nn.sigmoid(g_ref[...])
        dhcv = dhc_ref[...]
        _store_shifted_down(down, a * sg, rows)
        _store_shifted_up(up, dhcv, rows)
        dw_acc[...] = jnp.zeros_like(dw_acc)

        def chunk(ci, carry):
            r0 = pl.multiple_of(ci * ch, ch)
            dc = dhc_ref[pl.ds(r0, ch), :]
            dh = jnp.zeros((ch, tc), F32)
            for k in range(CONF_WIDTH):
                sft = CONF_WIDTH - 1 - k
                a8, b = (sft // 8) * 8, sft % 8
                dh = dh + w_ref[pl.ds(k, 1), :] * up[b, pl.ds(r0 + a8, ch), :]
                prod = dc * down[b, pl.ds(CONV_PAD + r0 - a8, ch), :]
                dw_acc[k] += jnp.sum(prod.reshape(ch // 8, 8, tc), axis=0)
            dh_buf[pl.ds(r0, ch), :] = dh
            return carry

        lax.fori_loop(0, s // ch, chunk, 0)
        dh = dh_buf[...]
        da = dh * sg
        dgate = dh * a * sg * (1.0 - sg)
        du_ref[0] = da.astype(BF16)
        du_ref[1] = dgate.astype(BF16)
        dbias_ref[pl.ds(0, 1), :] = jnp.sum(da, axis=0, keepdims=True)
        dbias_ref[pl.ds(1, 1), :] = jnp.sum(dgate, axis=0, keepdims=True)
        db_ref[...] = jnp.sum(dhcv, axis=0, keepdims=True)
        for k in range(CONF_WIDTH):
            dw_ref[pl.ds(k, 1), :] = jnp.sum(dw_acc[k], axis=0, keepdims=True)

    return pl.pallas_call(
        body, grid=(nb,),
        in_specs=[_cols(s, tc), _cols(s, tc, nb), _cols(CONF_WIDTH, tc), _cols(s, tc)],
        out_specs=[pl.BlockSpec((2, s, tc), lambda i: (0, 0, i)), _cols(2, tc), _cols(CONF_WIDTH, tc), _cols(1, tc)],
        out_shape=[_sds((2, s, d), BF16), _sds((2, d), F32), _sds((CONF_WIDTH, d), F32), _sds((1, d), F32)],
        scratch_shapes=[pltpu.VMEM((8, CONV_PAD + s, tc), F32), pltpu.VMEM((8, CONV_PAD + s, tc), F32),
                        pltpu.VMEM((CONF_WIDTH + 1, 8, tc), F32), pltpu.VMEM((s, tc), F32)],
        compiler_params=_params(("parallel",)), name="conformer_glu_conv_bwd")(u, u, dw_w, dhc)


def conformer_norm_swish(hc, g, b, tm=256):
    s, d = hc.shape
    tm = _tile(s, tm)

    def body(h_ref, g_ref, b_ref, o_ref):
        n, _, _ = _layer_norm_rows(h_ref[...], g_ref[...], b_ref[...])
        o_ref[...] = (n * jax.nn.sigmoid(n)).astype(BF16)

    return pl.pallas_call(
        body, grid=(s // tm,), in_specs=[_rows(tm, d), _fix((1, d)), _fix((1, d))], out_specs=_rows(tm, d),
        out_shape=_sds((s, d), BF16), compiler_params=_params(("parallel",)), name="conformer_norm_swish")(hc, g, b)


def conformer_norm_swish_bwd(hc, g, b, ds, tm=256):
    s, d = hc.shape
    tm = _tile(s, tm)

    def body(h_ref, g_ref, b_ref, ds_ref, dh_ref, dg_ref, db_ref):
        n, nh, rstd = _layer_norm_rows(h_ref[...], g_ref[...], b_ref[...])
        sg = jax.nn.sigmoid(n)
        dn = ds_ref[...] * (sg * (1.0 + n * (1.0 - sg)))
        dnh = dn * g_ref[...]
        m1 = jnp.mean(dnh, axis=-1, keepdims=True)
        m2 = jnp.mean(dnh * nh, axis=-1, keepdims=True)
        dh_ref[...] = rstd * (dnh - m1 - nh * m2)
        _accumulate(dg_ref, jnp.sum(dn * nh, axis=0, keepdims=True))
        _accumulate(db_ref, jnp.sum(dn, axis=0, keepdims=True))

    return pl.pallas_call(
        body, grid=(s // tm,), in_specs=[_rows(tm, d), _fix((1, d)), _fix((1, d)), _rows(tm, d)],
        out_specs=[_rows(tm, d), _fix((1, d)), _fix((1, d))],
        out_shape=[_sds((s, d), F32), _sds((1, d), F32), _sds((1, d), F32)],
        compiler_params=_params(("arbitrary",)), name="conformer_norm_swish_bwd")(hc, g, b, ds)


def _swap_halves(x):
    lane = lax.broadcasted_iota(jnp.int32, x.shape, 1)
    return jnp.where(lane < QK_ROPE // 2, pltpu.roll(x, 128 - QK_ROPE // 2, 1), pltpu.roll(x, QK_ROPE // 2, 1))


def _rope(x, cf, sf):
    return x * cf + _swap_halves(x) * sf


def _unrope(dx, cf, sf):
    return dx * cf - _swap_halves(dx) * sf


def _rms_rows(x, g):
    r = lax.rsqrt(jnp.mean(x * x, axis=-1, keepdims=True) + RMS_EPS)
    return x * r, r


def mla_latents(t, g_q, g_kv, cf, sf, tm=256):
    s = t.shape[0]
    tm = _tile(s, tm)

    def body(t_ref, gq_ref, gkv_ref, cf_ref, sf_ref, cq_ref, ckv_ref, kpe_ref):
        xq, _ = _rms_rows(t_ref[:, 0:Q_LORA], gq_ref[...])
        cq_ref[...] = (xq * gq_ref[...]).astype(BF16)
        xkv, _ = _rms_rows(t_ref[:, Q_LORA:Q_LORA + KV_LORA], gkv_ref[...])
        ckv_ref[...] = (xkv * gkv_ref[...]).astype(BF16)
        kpe_ref[...] = _rope(t_ref[:, Q_LORA + KV_LORA:], cf_ref[...], sf_ref[...]).astype(BF16)

    w = Q_LORA + KV_LORA + 128
    return pl.pallas_call(
        body, grid=(s // tm,),
        in_specs=[_rows(tm, w), _fix((1, Q_LORA)), _fix((1, KV_LORA)), _rows(tm, 128), _rows(tm, 128)],
        out_specs=[_rows(tm, Q_LORA), _rows(tm, KV_LORA), _rows(tm, 128)],
        out_shape=[_sds((s, Q_LORA), BF16), _sds((s, KV_LORA), BF16), _sds((s, 128), BF16)],
        compiler_params=_params(("parallel",)), name="mla_latents")(t, g_q, g_kv, cf, sf)


def mla_latents_bwd(t, g_q, g_kv, cf, sf, dcq, dckv, dkpe, tm=256):
    s = t.shape[0]
    tm = _tile(s, tm)
    w = Q_LORA + KV_LORA + 128

    def rms_bwd(x, g, dy):
        xh, r = _rms_rows(x, g)
        dxh = dy * g
        return r * (dxh - xh * jnp.mean(dxh * xh, axis=-1, keepdims=True)), jnp.sum(dy * xh, axis=0, keepdims=True)

    def body(t_ref, gq_ref, gkv_ref, cf_ref, sf_ref, dcq_ref, dckv_ref, dkpe_ref, dt_ref, dgq_ref, dgkv_ref):
        dxq, dgq = rms_bwd(t_ref[:, 0:Q_LORA], gq_ref[...], dcq_ref[...])
        dxkv, dgkv = rms_bwd(t_ref[:, Q_LORA:Q_LORA + KV_LORA], gkv_ref[...], dckv_ref[...])
        dt_ref[:, 0:Q_LORA] = dxq.astype(BF16)
        dt_ref[:, Q_LORA:Q_LORA + KV_LORA] = dxkv.astype(BF16)
        dt_ref[:, Q_LORA + KV_LORA:] = _unrope(dkpe_ref[...], cf_ref[...], sf_ref[...]).astype(BF16)
        _accumulate(dgq_ref, dgq)
        _accumulate(dgkv_ref, dgkv)

    return pl.pallas_call(
        body, grid=(s // tm,),
        in_specs=[_rows(tm, w), _fix((1, Q_LORA)), _fix((1, KV_LORA)), _rows(tm, 128), _rows(tm, 128),
                  _rows(tm, Q_LORA), _rows(tm, KV_LORA), _rows(tm, 128)],
        out_specs=[_rows(tm, w), _fix((1, Q_LORA)), _fix((1, KV_LORA))],
        out_shape=[_sds((s, w), BF16), _sds((1, Q_LORA), F32), _sds((1, KV_LORA), F32)],
        compiler_params=_params(("arbitrary",)), name="mla_latents_bwd")(t, g_q, g_kv, cf, sf, dcq, dckv, dkpe)


def mla_queries(cq, w_uq, cf, sf, tm=512):
    s = cq.shape[0]
    tm = _tile(s, tm)

    def epi(acc, e, o):
        o[0][:, 0:QK_NOPE] = acc[:, 0:QK_NOPE].astype(BF16)
        o[0][:, QK_NOPE:] = _rope(acc[:, QK_NOPE:], e[0][...], e[1][...]).astype(BF16)

    return mm_nn("mla_queries", cq, w_uq, tm, HEAD_PAD, Q_LORA, epi, [_sds((s, N_HEADS * HEAD_PAD), BF16)],
                 [_ij(tm, HEAD_PAD)], [cf, sf], [_i0(tm, 128), _i0(tm, 128)])[0]


def mla_keys(ckv, w_uk, kpe, tm=512):
    s = ckv.shape[0]
    tm = _tile(s, tm)

    def epi(acc, e, o):
        o[0][:, 0:QK_NOPE] = acc.astype(BF16)
        o[0][:, QK_NOPE:] = e[0][...]

    return mm_nn("mla_keys", ckv, w_uk, tm, QK_NOPE, KV_LORA, epi, [_sds((s, N_HEADS * HEAD_PAD), BF16)],
                 [_ij(tm, HEAD_PAD)], [kpe], [_i0(tm, 128)])[0]


def _masked_scores(q, k, qi, tq, kv):
    sc = lax.dot_general(q, k, NT, preferred_element_type=F32) * ATTN_SCALE
    row = lax.broadcasted_iota(jnp.int32, (tq, kv), 0) + qi * tq
    col = lax.broadcasted_iota(jnp.int32, (tq, kv), 1)
    ok = lax.shift_right_logical(col, CHUNK_SHIFT) <= lax.shift_right_logical(row, CHUNK_SHIFT)
    return jnp.where(ok, sc, -1e30)


def attention(q, k, v, tq=256):
    s = q.shape[0]
    tq = _tile(s, tq)
    nq = s // tq

    def body(q_ref, k_ref, v_ref, o_ref):
        for qi in range(nq):
            kv = (qi + 1) * tq
            sc = _masked_scores(q_ref[pl.ds(qi * tq, tq), :], k_ref[pl.ds(0, kv), :], qi, tq, kv)
            p = jnp.exp(sc - jnp.max(sc, axis=-1, keepdims=True))
            o = lax.dot_general(p.astype(BF16), v_ref[pl.ds(0, kv), :], NN, preferred_element_type=F32)
            o_ref[pl.ds(qi * tq, tq), :] = (o / jnp.sum(p, axis=-1, keepdims=True)).astype(BF16)

    hq = pl.BlockSpec((s, HEAD_PAD), lambda h: (0, h))
    hv = pl.BlockSpec((s, V_HEAD), lambda h: (0, h))
    return pl.pallas_call(
        body, grid=(N_HEADS,), in_specs=[hq, hq, hv], out_specs=hv, out_shape=_sds((s, N_HEADS * V_HEAD), BF16),
        compiler_params=_params(("parallel",)), name="attention")(q, k, v)


def attention_bwd(q, k, v, do, tq=256):
    s = q.shape[0]
    tq = _tile(s, tq)
    nq = s // tq

    def body(q_ref, k_ref, v_ref, do_ref, dq_ref, dk_ref, dv_ref, dk_acc, dv_acc):
        dk_acc[...] = jnp.zeros_like(dk_acc)
        dv_acc[...] = jnp.zeros_like(dv_acc)
        for qi in range(nq):
            kv = (qi + 1) * tq
            qt = q_ref[pl.ds(qi * tq, tq), :]
            kt = k_ref[pl.ds(0, kv), :]
            dot = do_ref[pl.ds(qi * tq, tq), :]
            sc = _masked_scores(qt, kt, qi, tq, kv)
            p = jnp.exp(sc - jnp.max(sc, axis=-1, keepdims=True))
            p = p / jnp.sum(p, axis=-1, keepdims=True)
            dp = lax.dot_general(dot, v_ref[pl.ds(0, kv), :], NT, preferred_element_type=F32)
            delta = jnp.sum(p * dp, axis=-1, keepdims=True)
            ds = (p * (dp - delta) * ATTN_SCALE).astype(BF16)
            dq_ref[pl.ds(qi * tq, tq), :] = lax.dot_general(ds, kt, NN, preferred_element_type=F32).astype(BF16)
            dk_acc[pl.ds(0, kv), :] += lax.dot_general(ds, qt, TN, preferred_element_type=F32)
            dv_acc[pl.ds(0, kv), :] += lax.dot_general(p.astype(BF16), dot, TN, preferred_element_type=F32)
        dk_ref[...] = dk_acc[...].astype(BF16)
        dv_ref[...] = dv_acc[...].astype(BF16)

    hq = pl.BlockSpec((s, HEAD_PAD), lambda h: (0, h))
    hv = pl.BlockSpec((s, V_HEAD), lambda h: (0, h))
    return pl.pallas_call(
        body, grid=(N_HEADS,), in_specs=[hq, hq, hv, hv], out_specs=[hq, hq, hv],
        out_shape=[_sds((s, N_HEADS * HEAD_PAD), BF16), _sds((s, N_HEADS * HEAD_PAD), BF16),
                   _sds((s, N_HEADS * V_HEAD), BF16)],
        scratch_shapes=[pltpu.VMEM((s, HEAD_PAD), F32), pltpu.VMEM((s, V_HEAD), F32)],
        compiler_params=_params(("parallel",)), name="attention_bwd")(q, k, v, do)


def mla_unrope_grads(dq, dk, cf, sf, tm=256):
    s = dq.shape[0]
    tm = _tile(s, tm)

    def body(dq_ref, dk_ref, cf_ref, sf_ref, dql_ref, dkn_ref, dkpe_ref):
        cfv, sfv = cf_ref[...], sf_ref[...]
        dkpe = jnp.zeros((tm, 128), F32)
        for h in range(N_HEADS):
            lo = h * HEAD_PAD
            dql_ref[:, lo:lo + QK_NOPE] = dq_ref[:, lo:lo + QK_NOPE]
            dql_ref[:, lo + QK_NOPE:lo + HEAD_PAD] = _unrope(
                dq_ref[:, lo + QK_NOPE:lo + HEAD_PAD].astype(F32), cfv, sfv).astype(BF16)
            dkn_ref[:, h * QK_NOPE:(h + 1) * QK_NOPE] = dk_ref[:, lo:lo + QK_NOPE]
            dkpe = dkpe + dk_ref[:, lo + QK_NOPE:lo + HEAD_PAD].astype(F32)
        dkpe_ref[...] = dkpe

    wq = N_HEADS * HEAD_PAD
    return pl.pallas_call(
        body, grid=(s // tm,), in_specs=[_rows(tm, wq), _rows(tm, wq), _rows(tm, 128), _rows(tm, 128)],
        out_specs=[_rows(tm, wq), _rows(tm, N_HEADS * QK_NOPE), _rows(tm, 128)],
        out_shape=[_sds((s, wq), BF16), _sds((s, N_HEADS * QK_NOPE), BF16), _sds((s, 128), F32)],
        compiler_params=_params(("parallel",)), name="mla_unrope_grads")(dq, dk, cf, sf)


ANY = pl.BlockSpec(memory_space=pl.ANY)


def _place():
    x, y, c = lax.axis_index("x"), lax.axis_index("y"), lax.axis_index("c")
    chips = [(1 - x, y), (x, 1 - y), (1 - x, 1 - y)]
    return x, y, c, chips


def _half(ref, hc, axis=0):
    n = ref.shape[axis] // 2
    idx = (slice(None),) * axis + (pl.ds(hc * n, n),)
    return ref.at[idx]


def gather_shards(tensors):
    nt = len(tensors)

    def body(*refs):
        a, g = refs[:nt], refs[nt:2 * nt]
        send, recv = refs[2 * nt:]
        x, y, c, chips = _place()
        q = 2 * x + y
        sib = (x, y, 1 - c)

        def slot(t, chip, hc):
            return _half(g[t].at[2 * chip[0] + chip[1]], hc)

        def rc(t, k, src, dst, to):
            return pltpu.make_async_remote_copy(src_ref=src, dst_ref=dst, send_sem=send.at[t, k], recv_sem=recv.at[t, k],
                                                device_id=to, device_id_type=MESH)

        sent = []
        for t in range(nt):
            cp = rc(t, 6, a[t], g[t].at[q], sib)
            cp.start()
            sent.append(cp)
            for j, chip in enumerate(chips):
                cp = rc(t, j, _half(a[t], c), slot(t, (x, y), c), (*chip, c))
                cp.start()
                sent.append(cp)
        for t in range(nt):
            for j, chip in enumerate(chips):
                landed = slot(t, chip, c)
                rc(t, j, landed, landed, (*chip, c)).wait_recv()
                cp = rc(t, 3 + j, landed, landed, sib)
                cp.start()
                sent.append(cp)
        for t in range(nt):
            for j, chip in enumerate(chips):
                other = slot(t, chip, 1 - c)
                rc(t, 3 + j, other, other, sib).wait_recv()
            own = g[t].at[q]
            rc(t, 6, own, own, sib).wait_recv()
        for cp in sent:
            cp.wait_send()

    return pl.pallas_call(
        body, in_specs=[ANY] * nt, out_specs=[ANY] * nt,
        out_shape=[_sds((N_CHIPS,) + a.shape, a.dtype) for a in tensors],
        scratch_shapes=[pltpu.SemaphoreType.DMA((nt, 7)), pltpu.SemaphoreType.DMA((nt, 7))],
        name="gather_shards")(*tensors)


def pair_exchange(grads):
    nt = len(grads)

    def body(*refs):
        g, theirs = refs[:nt], refs[nt:2 * nt]
        send, recv = refs[2 * nt:]
        x, y, c, _ = _place()
        cps = []
        for t in range(nt):
            cp = pltpu.make_async_remote_copy(src_ref=_half(g[t], 1 - c, 1), dst_ref=theirs[t], send_sem=send.at[t],
                                              recv_sem=recv.at[t], device_id=(x, y, 1 - c), device_id_type=MESH)
            cp.start()
            cps.append(cp)
        for cp in cps:
            cp.wait()

    return pl.pallas_call(
        body, in_specs=[ANY] * nt, out_specs=[ANY] * nt,
        out_shape=[_sds((N_CHIPS, a.shape[1] // 2, a.shape[2]), a.dtype) for a in grads],
        scratch_shapes=[pltpu.SemaphoreType.DMA((nt,)), pltpu.SemaphoreType.DMA((nt,))],
        name="pair_exchange")(*grads)


def chip_exchange(parts):
    nt = len(parts)

    def body(*refs):
        a, r = refs[:nt], refs[nt:2 * nt]
        send, recv = refs[2 * nt:]
        x, y, c, chips = _place()
        cps = []
        for t in range(nt):
            for j, chip in enumerate(chips):
                cp = pltpu.make_async_remote_copy(
                    src_ref=a[t].at[2 * chip[0] + chip[1]], dst_ref=r[t].at[j], send_sem=send.at[t, j],
                    recv_sem=recv.at[t, j], device_id=(*chip, c), device_id_type=MESH)
                cp.start()
                cps.append(cp)
        for cp in cps:
            cp.wait()

    return pl.pallas_call(
        body, in_specs=[ANY] * nt, out_specs=[ANY] * nt,
        out_shape=[_sds((N_CHIPS - 1,) + a.shape[1:], a.dtype) for a in parts],
        scratch_shapes=[pltpu.SemaphoreType.DMA((nt, 3)), pltpu.SemaphoreType.DMA((nt, 3))],
        name="chip_exchange")(*parts)


def pair_share(halves):
    nt = len(halves)

    def body(*refs):
        h, other = refs[:nt], refs[nt:2 * nt]
        send, recv = refs[2 * nt:]
        x, y, c, _ = _place()
        cps = []
        for t in range(nt):
            cp = pltpu.make_async_remote_copy(src_ref=h[t], dst_ref=other[t], send_sem=send.at[t], recv_sem=recv.at[t],
                                              device_id=(x, y, 1 - c), device_id_type=MESH)
            cp.start()
            cps.append(cp)
        for cp in cps:
            cp.wait()

    return pl.pallas_call(
        body, in_specs=[ANY] * nt, out_specs=[ANY] * nt, out_shape=[_sds(a.shape, a.dtype) for a in halves],
        scratch_shapes=[pltpu.SemaphoreType.DMA((nt,)), pltpu.SemaphoreType.DMA((nt,))],
        name="pair_share")(*halves)


def all_reduce_small(part):
    r, cdim = part.shape

    def body(p_ref, o_ref, buf, send, recv):
        x, y, c, _ = _place()
        me = 4 * x + 2 * y + c
        buf[me] = p_ref[...]
        cps = []
        for k in range(1, 8):
            to = (x ^ (k >> 2), y ^ ((k >> 1) & 1), c ^ (k & 1))
            cp = pltpu.make_async_remote_copy(src_ref=p_ref, dst_ref=buf.at[me], send_sem=send.at[k - 1],
                                              recv_sem=recv.at[k - 1], device_id=to, device_id_type=MESH)
            cp.start()
            cps.append(cp)
        for k in range(1, 8):
            frm = 4 * (x ^ (k >> 2)) + 2 * (y ^ ((k >> 1) & 1)) + (c ^ (k & 1))
            pltpu.make_async_remote_copy(src_ref=p_ref, dst_ref=buf.at[frm], send_sem=send.at[k - 1],
                                         recv_sem=recv.at[k - 1], device_id=(x, y, c), device_id_type=MESH).wait_recv()
        for cp in cps:
            cp.wait_send()
        acc = buf[0]
        for d in range(1, 8):
            acc = acc + buf[d]
        o_ref[...] = acc

    vm = pl.BlockSpec(memory_space=pltpu.VMEM)
    return pl.pallas_call(
        body, in_specs=[vm], out_specs=vm, out_shape=_sds((r, cdim), F32),
        scratch_shapes=[pltpu.VMEM((8, r, cdim), F32), pltpu.SemaphoreType.DMA((7,)), pltpu.SemaphoreType.DMA((7,))],
        name="all_reduce_small")(part)


def pair_sum(g, theirs, core, tm=256):
    _, r, c = g.shape
    tm = _tile(r // 2, tm)
    nh = r // 2 // tm

    def body(core_ref, a_ref, b_ref, o_ref):
        o_ref[...] = (a_ref[...].astype(F32) + b_ref[...].astype(F32)).astype(BF16)

    blk = (N_CHIPS, tm, c)
    return pl.pallas_call(
        body, grid_spec=pltpu.PrefetchScalarGridSpec(
            num_scalar_prefetch=1, grid=(nh,),
            in_specs=[pl.BlockSpec(blk, lambda i, cr: (0, cr[0] * nh + i, 0)), pl.BlockSpec(blk, lambda i, cr: (0, i, 0))],
            out_specs=pl.BlockSpec(blk, lambda i, cr: (0, i, 0))),
        out_shape=_sds(theirs.shape, BF16), compiler_params=_params(("parallel",)), name="pair_sum")(core, g, theirs)


def chip_sum(own, landed, chip, stack, layer, layers, tm=256):
    _, r, c = own.shape
    tm = _tile(r, tm)

    def body(chip_ref, own_ref, l_ref, *rest):
        acc = own_ref[...].astype(F32)
        for j in range(N_CHIPS - 1):
            acc = acc + l_ref[j].astype(F32)
        rest[-1][...] = acc

    in_specs = [pl.BlockSpec((None, tm, c), lambda i, qr: (qr[0], i, 0)),
                pl.BlockSpec((N_CHIPS - 1, tm, c), lambda i, qr: (0, i, 0))]
    args = [chip, own, landed]
    if stack is not None:
        in_specs.append(ANY)
        args.append(stack)
    return pl.pallas_call(
        body, grid_spec=pltpu.PrefetchScalarGridSpec(
            num_scalar_prefetch=1, grid=(r // tm,), in_specs=in_specs,
            out_specs=pl.BlockSpec((None, tm, c), lambda i, qr: (layer, i, 0))),
        out_shape=_sds((layers, r, c), F32), input_output_aliases={3: 0} if stack is not None else {},
        compiler_params=_params(("parallel",)), name="chip_sum")(*args)


def adamw_joined(w, m, v, g_mine, g_theirs, core, tm=256):
    nl, r, c = w.shape
    tm = _tile(r // 2, tm)
    nh = r // 2 // tm
    bc1 = 1.0 - ADAM_B1 ** ADAM_STEP
    bc2 = 1.0 - ADAM_B2 ** ADAM_STEP

    def body(core_ref, w_ref, m_ref, v_ref, gm_ref, gt_ref, g_ref, d_ref, nm_ref, nv_ref):
        mine = (pl.program_id(1) // nh) == core_ref[0]
        gv = jnp.where(mine, gm_ref[...], gt_ref[...])
        nm = ADAM_B1 * m_ref[...] + (1.0 - ADAM_B1) * gv
        nv = ADAM_B2 * v_ref[...] + (1.0 - ADAM_B2) * (gv * gv)
        g_ref[...] = gv
        d_ref[...] = -ADAM_LR * ((nm / bc1) / (jnp.sqrt(nv / bc2) + ADAM_EPS) + ADAM_WD * w_ref[...])
        nm_ref[...] = nm
        nv_ref[...] = nv

    full = pl.BlockSpec((None, tm, c), lambda l, i, cr: (l, i, 0))
    half = pl.BlockSpec((None, tm, c), lambda l, i, cr: (l, i % nh, 0))
    return pl.pallas_call(
        body, grid_spec=pltpu.PrefetchScalarGridSpec(
            num_scalar_prefetch=1, grid=(nl, r // tm), in_specs=[full, full, full, half, half], out_specs=[full] * 4),
        out_shape=[_sds((nl, r, c), F32)] * 4, compiler_params=_params(("parallel", "parallel")),
        name="adamw_joined")(core, w, m, v, g_mine, g_theirs)


def adamw(w, g, m, v, tm=256):
    shape = w.shape
    c = shape[-1]
    r = w.size // c
    tm = _tile(r, tm)
    bc1 = 1.0 - ADAM_B1 ** ADAM_STEP
    bc2 = 1.0 - ADAM_B2 ** ADAM_STEP

    def body(w_ref, g_ref, m_ref, v_ref, d_ref, nm_ref, nv_ref):
        gv = g_ref[...]
        nm = ADAM_B1 * m_ref[...] + (1.0 - ADAM_B1) * gv
        nv = ADAM_B2 * v_ref[...] + (1.0 - ADAM_B2) * (gv * gv)
        d_ref[...] = -ADAM_LR * ((nm / bc1) / (jnp.sqrt(nv / bc2) + ADAM_EPS) + ADAM_WD * w_ref[...])
        nm_ref[...] = nm
        nv_ref[...] = nv

    outs = pl.pallas_call(
        body, grid=(r // tm,), in_specs=[_rows(tm, c)] * 4, out_specs=[_rows(tm, c)] * 3,
        out_shape=[_sds((r, c), F32)] * 3, compiler_params=_params(("parallel",)), name="adamw")(
            w.reshape(r, c), g.reshape(r, c), m.reshape(r, c), v.reshape(r, c))
    return [o.reshape(shape) for o in outs]


WEIGHTS = ['sc_w_in', 'sc_conv_w', 'sc_w_out', 'mla_w_dq', 'mla_g_q', 'mla_w_uq', 'mla_w_dkv', 'mla_g_kv', 'mla_w_uk',
           'mla_w_uv', 'mla_w_o', 'cf_w_pw1', 'cf_b_pw1', 'cf_dw_w', 'cf_dw_b', 'cf_norm_g', 'cf_norm_b', 'cf_w_pw2',
           'cf_b_pw2', 'ff_w1', 'ff_w2', 'ln_mix_g', 'ln_mix_b', 'ln_ff_g', 'ln_ff_b']
ARGS = ['x'] + WEIGHTS + ['loss_target'] + ['m_' + n for n in WEIGHTS] + ['v_' + n for n in WEIGHTS]


def _sq_relu(h):
    r = jnp.maximum(h.astype(F32), 0.0)
    return (r * r).astype(BF16)


def _mlp_forward(i, x, xb, w1, w2, g, b):
    hb = mm_plain_nn(f"mlp{i}_up", xb, w1, BF16)
    y, yb, xh, rstd = mm_residual_ln(f"mlp{i}_down_ln", hb, w2, x, g, b, a_fn=_sq_relu)
    return (y, yb), dict(xb=xb, hb=hb, xh=xh, rstd=rstd, g=g)


def _mlp_backward(i, dy, sv, w1, w2, dw1, dw2):
    s = dy.shape[0]
    dr, drb, dg, db, _ = ln_backward(f"mlp{i}_ln_bwd", dy, sv["xh"], sv["rstd"], sv["g"])
    tm, tn = _tile(s, 1024), 512

    def epi(acc, e, o):
        o[0][...] = (acc * (2.0 * jnp.maximum(e[0][...].astype(F32), 0.0))).astype(BF16)

    dhb = mm_nt(f"mlp{i}_down_bwd", drb, w2, s, tm, tn, 1024, epi, [_sds((s, w2.k), BF16)], [_ij(tm, tn)],
                [sv["hb"]], [_ij(tm, tn)])[0]
    g_w2 = mm_tn(f"mlp{i}_dw2", sv["hb"], drb, dw2, s, 512, 1024, a_fn=_sq_relu)
    g_w1 = mm_tn(f"mlp{i}_dw1", sv["xb"], dhb, dw1, s, 1024, 512)
    dx = mm_plain_nt(f"mlp{i}_up_bwd", dhb, w1, F32, tn=1024, add=dr, add_scale=ALPHA)
    return dx, g_w1, g_w2, dg, db


def kernel(x, sc_w_in, sc_conv_w, sc_w_out, mla_w_dq, mla_g_q, mla_w_uq, mla_w_dkv, mla_g_kv, mla_w_uk, mla_w_uv, mla_w_o, cf_w_pw1, cf_b_pw1, cf_dw_w, cf_dw_b, cf_norm_g, cf_norm_b, cf_w_pw2, cf_b_pw2, ff_w1, ff_w2, ln_mix_g, ln_mix_b, ln_ff_g, ln_ff_b, loss_target, m_sc_w_in, m_sc_conv_w, m_sc_w_out, m_mla_w_dq, m_mla_g_q, m_mla_w_uq, m_mla_w_dkv, m_mla_g_kv, m_mla_w_uk, m_mla_w_uv, m_mla_w_o, m_cf_w_pw1, m_cf_b_pw1, m_cf_dw_w, m_cf_dw_b, m_cf_norm_g, m_cf_norm_b, m_cf_w_pw2, m_cf_b_pw2, m_ff_w1, m_ff_w2, m_ln_mix_g, m_ln_mix_b, m_ln_ff_g, m_ln_ff_b, v_sc_w_in, v_sc_conv_w, v_sc_w_out, v_mla_w_dq, v_mla_g_q, v_mla_w_uq, v_mla_w_dkv, v_mla_g_kv, v_mla_w_uk, v_mla_w_uv, v_mla_w_o, v_cf_w_pw1, v_cf_b_pw1, v_cf_dw_w, v_cf_dw_b, v_cf_norm_g, v_cf_norm_b, v_cf_w_pw2, v_cf_b_pw2, v_ff_w1, v_ff_w2, v_ln_mix_g, v_ln_mix_b, v_ln_ff_g, v_ln_ff_b):
    given = dict(zip(ARGS, (x, sc_w_in, sc_conv_w, sc_w_out, mla_w_dq, mla_g_q, mla_w_uq, mla_w_dkv, mla_g_kv, mla_w_uk, mla_w_uv, mla_w_o, cf_w_pw1, cf_b_pw1, cf_dw_w, cf_dw_b, cf_norm_g, cf_norm_b, cf_w_pw2, cf_b_pw2, ff_w1, ff_w2, ln_mix_g, ln_mix_b, ln_ff_g, ln_ff_b, loss_target, m_sc_w_in, m_sc_conv_w, m_sc_w_out, m_mla_w_dq, m_mla_g_q, m_mla_w_uq, m_mla_w_dkv, m_mla_g_kv, m_mla_w_uk, m_mla_w_uv, m_mla_w_o, m_cf_w_pw1, m_cf_b_pw1, m_cf_dw_w, m_cf_dw_b, m_cf_norm_g, m_cf_norm_b, m_cf_w_pw2, m_cf_b_pw2, m_ff_w1, m_ff_w2, m_ln_mix_g, m_ln_mix_b, m_ln_ff_g, m_ln_ff_b, v_sc_w_in, v_sc_conv_w, v_sc_w_out, v_mla_w_dq, v_mla_g_q, v_mla_w_uq, v_mla_w_dkv, v_mla_g_kv, v_mla_w_uk, v_mla_w_uv, v_mla_w_o, v_cf_w_pw1, v_cf_b_pw1, v_cf_dw_w, v_cf_dw_b, v_cf_norm_g, v_cf_norm_b, v_cf_w_pw2, v_cf_b_pw2, v_ff_w1, v_ff_w2, v_ln_mix_g, v_ln_mix_b, v_ln_ff_g, v_ln_ff_b)))
    s, d = x.shape[1], x.shape[2]
    d_ff = 4 * d
    dq4 = d // N_CHIPS
    xq = lax.axis_index("x") * 2 + lax.axis_index("y")

    w_dkv_pad = jnp.pad(mla_w_dkv[0], ((0, 0), (0, 128 - QK_ROPE)))
    w_uq_pad = jnp.pad(mla_w_uq[0].reshape(Q_LORA, 2, QK_NOPE + QK_ROPE), ((0, 0), (0, 0), (0, HEAD_PAD - QK_NOPE - QK_ROPE)))
    small = jnp.concatenate([
        sc_conv_w.reshape(2 * SC_WIDTH, dq4), cf_b_pw1.reshape(2, dq4), cf_dw_w[0], cf_dw_b, cf_norm_g, cf_norm_b,
        cf_b_pw2, jnp.zeros((5, dq4), F32)], axis=0)
    local = [
        sc_w_in.astype(BF16), sc_w_out.astype(BF16),
        jnp.concatenate([mla_w_dq[0], w_dkv_pad], axis=1).astype(BF16),
        w_uq_pad.reshape(Q_LORA, 2 * HEAD_PAD).astype(BF16),
        mla_w_uk.reshape(KV_LORA // N_CHIPS, N_HEADS * QK_NOPE).astype(BF16),
        mla_w_uv.reshape(KV_LORA // N_CHIPS, N_HEADS * V_HEAD).astype(BF16),
        mla_w_o[0].astype(BF16), cf_w_pw1[0].astype(BF16), cf_w_pw2[0].astype(BF16),
        ff_w1.astype(BF16), ff_w2.astype(BF16), small]
    (g_in, g_out, g_dqkv, g_uq, g_uk, g_uv, g_o, g_pw1, g_pw2, g_w1, g_w2, g_small) = gather_shards(local)

    wd_t = Q_LORA + KV_LORA + 128
    w_in = [Stk("col", d, 3 * d, g_in, 2, j) for j in range(2)]
    w_out = [Stk("row", d, d, g_out, 2, j) for j in range(2)]
    w_dqkv = Stk("row", d, wd_t, g_dqkv)
    w_uq = Stk("col", Q_LORA, N_HEADS * HEAD_PAD, g_uq)
    w_uk = Stk("row", KV_LORA, N_HEADS * QK_NOPE, g_uk)
    w_uv = Stk("row", KV_LORA, N_HEADS * V_HEAD, g_uv)
    w_o = Stk("row", d, d, g_o)
    w_pw1 = Stk("col", d, 2 * d, g_pw1)
    w_pw2 = Stk("row", d, d, g_pw2)
    w_1 = [Stk("col", d, d_ff, g_w1, DEPTH, i) for i in range(DEPTH)]
    w_2 = [Stk("row", d_ff, d, g_w2, DEPTH, i) for i in range(DEPTH)]

    def wide(rows):
        return jnp.swapaxes(rows, 0, 1).reshape(rows.shape[1], d)

    conv_w = wide(g_small[:, 0:6]).reshape(2, SC_WIDTH, d)
    b_pw1 = g_small[:, 6:8].reshape(1, 2 * d)
    dw_w = wide(g_small[:, 8:39])
    dw_b, norm_g, norm_b, b_pw2 = (wide(g_small[:, 39 + k:40 + k]) for k in range(4))

    pos = jnp.arange(s, dtype=F32)
    inv_freq = ROPE_THETA ** (-jnp.arange(0, QK_ROPE, 2, dtype=F32) / QK_ROPE)
    ang = pos[:, None] * inv_freq[None, :]
    cos, sin, zero = jnp.cos(ang), jnp.sin(ang), jnp.zeros((s, 128 - QK_ROPE), F32)
    cf = jnp.concatenate([cos, cos, zero], axis=1)
    sf = jnp.concatenate([-sin, sin, zero], axis=1)

    def row(a, i):
        return a[i:i + 1]

    xs = x.reshape(s, d)
    cur = (xs, xs.astype(BF16))
    tape = []
    for i in range(DEPTH):
        mixer, j = i % 3, i // 3
        xf, xb = cur
        lg, lb = row(ln_mix_g, i), row(ln_mix_b, i)
        if mixer == 0:
            u = mm_plain_nn(f"sc{j}_in", xb, w_in[j], F32, tn=3 * dq4)
            gb = short_conv_gate(u, conv_w[j])
            y, yb, xh, rstd = mm_residual_ln(f"sc{j}_out_ln", gb, w_out[j], xf, lg, lb, tk=dq4)
            sv = dict(xb=xb, u=u, gb=gb)
        elif mixer == 1:
            t = mm_plain_nn("mla_down", xb, w_dqkv, F32, tn=wd_t // 2)
            cq, ckv, kpe = mla_latents(t, mla_g_q, mla_g_kv, cf, sf)
            qh = mla_queries(cq, w_uq, cf, sf)
            kh = mla_keys(ckv, w_uk, kpe)
            vh = mm_plain_nn("mla_values", ckv, w_uv, BF16, tk=KV_LORA)
            oh = attention(qh, kh, vh)
            y, yb, xh, rstd = mm_residual_ln("mla_out_ln", oh, w_o, xf, lg, lb)
            sv = dict(xb=xb, t=t, cq=cq, ckv=ckv, qh=qh, kh=kh, vh=vh, oh=oh)
        else:
            u = mm_plain_nn("cf_pw1", xb, w_pw1, F32, bias=b_pw1)
            hc = conformer_glu_conv(u, dw_w, dw_b)
            sb = conformer_norm_swish(hc, norm_g, norm_b)
            y, yb, xh, rstd = mm_residual_ln("cf_pw2_ln", sb, w_pw2, xf, lg, lb, bias=b_pw2)
            sv = dict(xb=xb, u=u, hc=hc, sb=sb)
        sv.update(xh=xh, rstd=rstd, g=lg)
        cur, sv_mlp = _mlp_forward(i, y, yb, w_1[i], w_2[i], row(ln_ff_g, i), row(ln_ff_b, i))
        tape.append((sv, sv_mlp))

    dy, loss_part = loss_head(cur[0], loss_target.reshape(s, d))
    loss = lax.psum(loss_part[0, 0], ("x", "y", "c"))

    grads = {}
    smalls = {}
    g_ln = {n: [None] * DEPTH for n in ("ln_mix_g", "ln_mix_b", "ln_ff_g", "ln_ff_b")}
    conv_grads = [None, None]
    for i in reversed(range(DEPTH)):
        mixer, j = i % 3, i // 3
        sv, sv_mlp = tape[i]
        dy, grads[f"w1_{i}"], grads[f"w2_{i}"], g_ln["ln_ff_g"][i], g_ln["ln_ff_b"][i] = _mlp_backward(
            i, dy, sv_mlp, w_1[i], w_2[i], Stk("col", d, d_ff), Stk("row", d_ff, d))
        dr, drb, g_ln["ln_mix_g"][i], g_ln["ln_mix_b"][i], dr_sum = ln_backward(
            f"mix{i}_ln_bwd", dy, sv["xh"], sv["rstd"], sv["g"])
        if mixer == 0:
            dgate = mm_plain_nt(f"sc{j}_out_bwd", drb, w_out[j], F32, tn=dq4)
            grads[f"out_{j}"] = mm_tn(f"sc{j}_dw_out", sv["gb"], drb, Stk("row", d, d), s, 512, 1024)
            du, conv_grads[j] = short_conv_gate_bwd(sv["u"], conv_w[j], dgate)
            nb = d // 256
            grads[f"in_{j}"] = mm_tn(
                f"sc{j}_dw_in", sv["xb"], du, Stk("col", d, 3 * d), s, 1024, 256,
                b_spec=pl.BlockSpec((None, s, 256), lambda i_, j_, k_: (j_ // nb, k_, j_ % nb)))
            dy = mm_plain_nt(
                f"sc{j}_in_bwd", du, w_in[j], F32, tk=256, add=dr, add_scale=ALPHA,
                a_spec_fn=(s, lambda tm, tk: pl.BlockSpec((None, tm, tk), lambda i_, j_, k_: (k_ // nb, i_, k_ % nb))))
        elif mixer == 1:
            do = mm_plain_nt("mla_out_bwd", drb, w_o, BF16)
            grads["o"] = mm_tn("mla_dw_o", sv["oh"], drb, Stk("row", d, d), s, 512, 1024)
            dqh, dkh, dvh = attention_bwd(sv["qh"], sv["kh"], sv["vh"], do)
            dql, dkn, dkpe = mla_unrope_grads(dqh, dkh, cf, sf)
            grads["uq"] = mm_tn("mla_dw_uq", sv["cq"], dql, Stk("col", Q_LORA, N_HEADS * HEAD_PAD), s, Q_LORA, 512)
            dcq = mm_plain_nt("mla_uq_bwd", dql, w_uq, F32, tn=Q_LORA)
            grads["uk"] = mm_tn("mla_dw_uk", sv["ckv"], dkn, Stk("row", KV_LORA, N_HEADS * QK_NOPE), s, KV_LORA, 1024)
            grads["uv"] = mm_tn("mla_dw_uv", sv["ckv"], dvh, Stk("row", KV_LORA, N_HEADS * V_HEAD), s, KV_LORA, 1024)
            dckv = mm_plain_nt("mla_uk_bwd", dkn, w_uk, F32, tn=KV_LORA)
            dckv = mm_plain_nt("mla_uv_bwd", dvh, w_uv, F32, tn=KV_LORA, add=dckv)
            dt, smalls["g_q"], smalls["g_kv"] = mla_latents_bwd(sv["t"], mla_g_q, mla_g_kv, cf, sf, dcq, dckv, dkpe)
            grads["dqkv"] = mm_tn("mla_dw_down", sv["xb"], dt, Stk("row", d, wd_t), s, 512, wd_t)
            dy = mm_plain_nt("mla_down_bwd", dt, w_dqkv, F32, tk=wd_t, add=dr, add_scale=ALPHA)
        else:
            dsw = mm_plain_nt("cf_pw2_bwd", drb, w_pw2, F32)
            grads["pw2"] = mm_tn("cf_dw_pw2", sv["sb"], drb, Stk("row", d, d), s, 512, 1024)
            smalls["b_pw2"] = dr_sum
            dhc, smalls["norm_g"], smalls["norm_b"] = conformer_norm_swish_bwd(sv["hc"], norm_g, norm_b, dsw)
            du, smalls["b_pw1"], smalls["dw_w"], smalls["dw_b"] = conformer_glu_conv_bwd(sv["u"], dw_w, dhc)
            nb = d // 512
            grads["pw1"] = mm_tn(
                "cf_dw_pw1", sv["xb"], du, Stk("col", d, 2 * d), s, 1024, 512,
                b_spec=pl.BlockSpec((None, s, 512), lambda i_, j_, k_: (j_ // nb, k_, j_ % nb)))
            dy = mm_plain_nt(
                "cf_pw1_bwd", du, w_pw1, F32, add=dr, add_scale=ALPHA,
                a_spec_fn=(s, lambda tm, tk: pl.BlockSpec((None, tm, tk), lambda i_, j_, k_: (k_ // nb, i_, k_ % nb))))
    grad_x = dy.reshape(1, s, d)

    order = ([f"in_{j}" for j in range(2)] + [f"out_{j}" for j in range(2)] + ["dqkv", "uq", "uk", "uv", "o", "pw1", "pw2"]
             + [f"w1_{i}" for i in range(DEPTH)] + [f"w2_{i}" for i in range(DEPTH)])
    core = lax.axis_index("c").astype(jnp.int32).reshape(1)
    chip = xq.astype(jnp.int32).reshape(1)
    theirs = pair_exchange([grads[n] for n in order])
    pairs = [pair_sum(grads[n], th, core) for n, th in zip(order, theirs)]
    landed = chip_exchange(pairs)
    groups = [["in_0", "in_1"], ["out_0", "out_1"], ["dqkv"], ["uq"], ["uk"], ["uv"], ["o"], ["pw1"], ["pw2"],
              [f"w1_{i}" for i in range(DEPTH)], [f"w2_{i}" for i in range(DEPTH)]]
    mine = []
    for members in groups:
        stack = None
        for layer, n in enumerate(members):
            k = order.index(n)
            stack = chip_sum(pairs[k], landed[k], chip, stack, layer, len(members))
        mine.append(stack)
    other = pair_share(mine)

    def padded(get):
        dqkv = jnp.concatenate([get("mla_w_dq")[0], jnp.pad(get("mla_w_dkv")[0], ((0, 0), (0, 128 - QK_ROPE)))], axis=1)
        uq = jnp.pad(get("mla_w_uq")[0].reshape(Q_LORA, 2, QK_NOPE + QK_ROPE),
                     ((0, 0), (0, 0), (0, HEAD_PAD - QK_NOPE - QK_ROPE))).reshape(Q_LORA, 2 * HEAD_PAD)
        return [get("sc_w_in"), get("sc_w_out"), dqkv[None], uq[None],
                get("mla_w_uk").reshape(1, KV_LORA // N_CHIPS, d), get("mla_w_uv").reshape(1, KV_LORA // N_CHIPS, d),
                get("mla_w_o"), get("cf_w_pw1"), get("cf_w_pw2"), get("ff_w1"), get("ff_w2")]

    w_l, m_l, v_l = (padded(lambda n, p=p: given[p + n]) for p in ("", "m_", "v_"))
    res = [adamw_joined(w_l[k], m_l[k], v_l[k], mine[k], other[k], core) for k in range(len(groups))]

    def unpadded(k):
        r_in, r_out, r_dqkv, r_uq, r_uk, r_uv, r_o, r_pw1, r_pw2, r_w1, r_w2 = (r[k] for r in res)
        return {
            "sc_w_in": r_in, "sc_w_out": r_out, "mla_w_dq": r_dqkv[:, :, 0:Q_LORA],
            "mla_w_dkv": r_dqkv[:, :, Q_LORA:Q_LORA + KV_LORA + QK_ROPE],
            "mla_w_uq": r_uq.reshape(1, Q_LORA, 2, HEAD_PAD)[:, :, :, 0:QK_NOPE + QK_ROPE].reshape(mla_w_uq.shape),
            "mla_w_uk": r_uk.reshape(mla_w_uk.shape), "mla_w_uv": r_uv.reshape(mla_w_uv.shape),
            "mla_w_o": r_o, "cf_w_pw1": r_pw1, "cf_w_pw2": r_pw2, "ff_w1": r_w1, "ff_w2": r_w2}

    big_g, big_d, big_m, big_v = (unpadded(k) for k in range(4))

    pad_row = lambda a: jnp.pad(a, ((0, 0), (0, d - a.shape[1])))
    small_part = jnp.concatenate(
        [jnp.concatenate(g_ln[n], axis=0) for n in ("ln_mix_g", "ln_mix_b", "ln_ff_g", "ln_ff_b")]
        + [pad_row(smalls["g_q"]), pad_row(smalls["g_kv"]), conv_grads[0], conv_grads[1],
           smalls["b_pw1"].reshape(2, d), smalls["dw_w"], smalls["dw_b"], smalls["norm_g"], smalls["norm_b"],
           smalls["b_pw2"], jnp.zeros((3, d), F32)], axis=0)
    red = all_reduce_small(small_part)

    def shard(rows):
        return lax.dynamic_slice_in_dim(rows, xq * dq4, dq4, axis=1)

    gw = {
        **big_g,
        "ln_mix_g": red[0:4], "ln_mix_b": red[4:8], "ln_ff_g": red[8:12], "ln_ff_b": red[12:16],
        "mla_g_q": red[16:17, 0:Q_LORA], "mla_g_kv": red[17:18, 0:KV_LORA],
        "sc_conv_w": shard(red[18:24]).reshape(2, SC_WIDTH, dq4),
        "cf_b_pw1": lax.dynamic_slice_in_dim(red[24:26].reshape(1, 2 * d), xq * 2 * dq4, 2 * dq4, axis=1),
        "cf_dw_w": shard(red[26:57])[None], "cf_dw_b": shard(red[57:58]), "cf_norm_g": shard(red[58:59]),
        "cf_norm_b": shard(red[59:60]), "cf_b_pw2": shard(red[60:61]),
    }

    upd = {n: [big_d[n], big_m[n], big_v[n]] for n in big_g}

    def pack(names, width, get):
        return jnp.concatenate([get(n).reshape(-1, width) for n in names], axis=0)

    def unpack(names, packed):
        out, at = {}, 0
        for n in names:
            rows = given[n].size // packed.shape[1]
            out[n] = packed[at:at + rows].reshape(given[n].shape)
            at += rows
        return out

    rep = ["ln_mix_g", "ln_mix_b", "ln_ff_g", "ln_ff_b"]
    shd = ["sc_conv_w", "cf_b_pw1", "cf_dw_w", "cf_dw_b", "cf_norm_g", "cf_norm_b", "cf_b_pw2"]
    for names, width in ((rep, d), (shd, dq4), (["mla_g_q"], Q_LORA), (["mla_g_kv"], KV_LORA)):
        res = adamw(pack(names, width, lambda n: given[n]), pack(names, width, lambda n: gw[n]),
                    pack(names, width, lambda n: given["m_" + n]), pack(names, width, lambda n: given["v_" + n]), tm=4096)
        parts = [unpack(names, r) for r in res]
        for n in names:
            upd[n] = [p[n] for p in parts]

    return (loss, grad_x, *[gw[n].reshape(given[n].shape) for n in WEIGHTS], *[upd[n][0] for n in WEIGHTS],
            *[upd[n][1] for n in WEIGHTS], *[upd[n][2] for n in WEIGHTS])
```

```python
import jax
import jax.numpy as jnp
from jax import lax
from jax.experimental import pallas as pl
from jax.experimental.pallas import tpu as pltpu
from jax.experimental.pallas import tpu_sc as plsc

F32 = jnp.float32
BF16 = jnp.bfloat16
MESH = pl.DeviceIdType.MESH

DEPTH = 4
ALPHA = (2.0 * DEPTH) ** 0.25
LN_EPS = 1e-5
RMS_EPS = 1e-6
CHUNK_SHIFT = 6
N_HEADS = 8
QK_NOPE = 128
QK_ROPE = 64
V_HEAD = 128
HEAD_PAD = 256
Q_LORA = 384
KV_LORA = 256
ROPE_THETA = 10000.0
SC_WIDTH = 3
CONF_WIDTH = 31
CONV_PAD = 32
CONV_CHUNK = 64
N_CHIPS = 4
ATTN_SCALE = (QK_NOPE + QK_ROPE) ** -0.5

ADAM_LR = 0.001
ADAM_B1 = 0.9
ADAM_B2 = 0.999
ADAM_EPS = 1e-08
ADAM_WD = 0.01
ADAM_STEP = 10

VMEM_LIMIT = 56 * 2**20

NN = (((1,), (0,)), ((), ()))
NT = (((1,), (1,)), ((), ()))
TN = (((0,), (0,)), ((), ()))


def _params(sem=None):
    return pltpu.CompilerParams(dimension_semantics=sem, vmem_limit_bytes=VMEM_LIMIT)


class Stk:
    def __init__(self, kind, k, n, arr=None, layers=None, layer=None):
        self.kind, self.k, self.n, self.layers, self.layer = kind, k, n, layers, layer
        self.plain = kind == "row" and layers is None
        self.kloc = k // N_CHIPS if kind == "row" else k
        self.nloc = n // N_CHIPS if kind == "col" else n
        if arr is not None and self.plain:
            arr = arr.reshape(k, n)
        self.arr = arr

    @property
    def shape(self):
        if self.plain:
            return (self.k, self.n)
        lead = (N_CHIPS,) if self.layers is None else (N_CHIPS, self.layers)
        return lead + (self.kloc, self.nloc)

    def spec(self, bk, bn, f):
        if self.plain:
            return pl.BlockSpec((bk, bn), f)
        assert self.kloc % bk == 0 and self.nloc % bn == 0, (self.kloc, bk, self.nloc, bn)
        pk, pn = self.kloc // bk, self.nloc // bn
        kind, layer = self.kind, self.layer

        def imap(*g):
            kb, nb = f(*g)
            if kind == "row":
                q, kb, nb = kb // pk, kb % pk, nb
            else:
                q, kb, nb = nb // pn, kb, nb % pn
            return (q, kb, nb) if layer is None else (q, layer, kb, nb)

        block = (None, bk, bn) if layer is None else (None, None, bk, bn)
        return pl.BlockSpec(block, imap)


def _mm(name, mode, a, b, grid, a_spec, b_spec, acc_shape, extras, extra_specs, out_shapes, out_specs, epi, a_fn=None):
    nk = grid[2]
    ne = len(extras)

    def body(*refs):
        a_ref, b_ref = refs[0], refs[1]
        e_refs = refs[2:2 + ne]
        av = a_ref[...] if a_fn is None else a_fn(a_ref[...])
        part = lax.dot_general(av, b_ref[...], mode, preferred_element_type=F32)
        if nk == 1:
            epi(part, e_refs, refs[2 + ne:])
            return
        o_refs = refs[2 + ne:-1]
        acc = refs[-1]
        k = pl.program_id(2)

        @pl.when(k == 0)
        def _():
            acc[...] = part

        @pl.when(k > 0)
        def _():
            acc[...] += part

        @pl.when(k == nk - 1)
        def _():
            epi(acc[...], e_refs, o_refs)

    return pl.pallas_call(
        body, grid=grid, in_specs=[a_spec, b_spec, *extra_specs], out_specs=out_specs, out_shape=out_shapes,
        scratch_shapes=[pltpu.VMEM(acc_shape, F32)] if nk > 1 else [],
        compiler_params=_params(("parallel", "parallel", "arbitrary")), name=name)(a, b, *extras)


def _tile(n, t):
    t = min(n, t)
    while n % t:
        t -= 8
    assert t > 0, (n, t)
    return t


def mm_nn(name, a, w, tm, tn, tk, epi, out_shapes, out_specs, extras=(), extra_specs=(), a_spec=None, a_fn=None):
    m = a.shape[0]
    tm, tn, tk = _tile(m, tm), _tile(w.n, tn), _tile(w.k, tk)
    grid = (m // tm, w.n // tn, w.k // tk)
    a_spec = a_spec or pl.BlockSpec((tm, tk), lambda i, j, k: (i, k))
    b_spec = w.spec(tk, tn, lambda i, j, k: (k, j))
    return _mm(name, NN, a, w.arr, grid, a_spec, b_spec, (tm, tn), extras, extra_specs, out_shapes, out_specs, epi, a_fn)


def mm_nt(name, a, w, m, tm, tn, tk, epi, out_shapes, out_specs, extras=(), extra_specs=(), a_spec=None):
    tm, tn, tk = _tile(m, tm), _tile(w.k, tn), _tile(w.n, tk)
    grid = (m // tm, w.k // tn, w.n // tk)
    a_spec = a_spec or pl.BlockSpec((tm, tk), lambda i, j, k: (i, k))
    b_spec = w.spec(tn, tk, lambda i, j, k: (j, k))
    return _mm(name, NT, a, w.arr, grid, a_spec, b_spec, (tm, tn), extras, extra_specs, out_shapes, out_specs, epi)


def mm_tn(name, a, b, dw, s, tm=512, tn=512, tk=4096, a_spec=None, b_spec=None, a_fn=None):
    tm, tn, tk = _tile(dw.k, tm), _tile(dw.n, tn), _tile(s, tk)
    grid = (dw.k // tm, dw.n // tn, s // tk)
    a_spec = a_spec or pl.BlockSpec((tk, tm), lambda i, j, k: (k, i))
    b_spec = b_spec or pl.BlockSpec((tk, tn), lambda i, j, k: (k, j))

    def epi(acc, e, o):
        o[0][...] = acc.astype(BF16)

    out = _mm(name, TN, a, b, grid, a_spec, b_spec, (tm, tn), (), (), [jax.ShapeDtypeStruct(dw.shape, BF16)],
              [dw.spec(tm, tn, lambda i, j, k: (i, j))], epi, a_fn)[0]
    return out.reshape(N_CHIPS, dw.k // N_CHIPS, dw.n) if dw.plain else out


def _sds(shape, dtype):
    return jax.ShapeDtypeStruct(shape, dtype)


def _ij(tm, tn):
    return pl.BlockSpec((tm, tn), lambda i, j, k: (i, j))


def _i0(tm, c):
    return pl.BlockSpec((tm, c), lambda i, j, k: (i, 0))


def _0j(r, tn):
    return pl.BlockSpec((r, tn), lambda i, j, k: (0, j))


def _layer_norm_rows(r, g, b):
    mu = jnp.mean(r, axis=-1, keepdims=True)
    d = r - mu
    var = jnp.mean(d * d, axis=-1, keepdims=True)
    rstd = lax.rsqrt(var + LN_EPS)
    xh = d * rstd
    return xh * g + b, xh, rstd


def mm_residual_ln(name, a, w, x, g, b, bias=None, tm=512, tk=1024, a_fn=None):
    s, d = x.shape
    tm = _tile(s, tm)
    extras = [x, g, b] + ([bias] if bias is not None else [])
    especs = [_i0(tm, d), _0j(1, d), _0j(1, d)] + ([_0j(1, d)] if bias is not None else [])

    def epi(acc, e, o):
        r = ALPHA * e[0][...] + acc
        if bias is not None:
            r = r + e[3][...]
        y, xh, rstd = _layer_norm_rows(r, e[1][...], e[2][...])
        o[0][...] = y
        o[1][...] = y.astype(BF16)
        o[2][...] = xh
        o[3][...] = rstd

    return mm_nn(name, a, w, tm, d, tk, epi,
                 [_sds((s, d), F32), _sds((s, d), BF16), _sds((s, d), F32), _sds((s, 1), F32)],
                 [_i0(tm, d), _i0(tm, d), _i0(tm, d), _i0(tm, 1)], extras, especs, a_fn=a_fn)


def mm_plain_nn(name, a, w, out_dtype, tm=1024, tn=512, tk=1024, bias=None):
    m = a.shape[0]
    tm, tn = _tile(m, tm), _tile(w.n, tn)
    if w.kind == "col":
        tn = _tile(w.nloc, tn)

    def epi(acc, e, o):
        if bias is not None:
            acc = acc + e[0][...]
        o[0][...] = acc.astype(out_dtype)

    extras, especs = ([bias], [_0j(1, tn)]) if bias is not None else ((), ())
    return mm_nn(name, a, w, tm, tn, tk, epi, [_sds((m, w.n), out_dtype)], [_ij(tm, tn)], extras, especs)[0]


def mm_plain_nt(name, a, w, out_dtype, tm=1024, tn=512, tk=1024, add=None, add_scale=1.0, a_spec_fn=None):
    m = a.shape[0] if a_spec_fn is None else a_spec_fn[0]
    tm, tn = _tile(m, tm), _tile(w.k, tn)
    tk = _tile(w.n, tk)
    if w.kind == "col":
        tk = _tile(w.nloc, tk)
    if w.kind == "row" and not w.plain:
        tn = _tile(w.kloc, tn)

    def epi(acc, e, o):
        if add is not None:
            acc = acc + add_scale * e[0][...].astype(F32)
        o[0][...] = acc.astype(out_dtype)

    extras, especs = ([add], [_ij(tm, tn)]) if add is not None else ((), ())
    a_spec = None if a_spec_fn is None else a_spec_fn[1](tm, tk)
    return mm_nt(name, a, w, m, tm, tn, tk, epi, [_sds((m, w.k), out_dtype)], [_ij(tm, tn)], extras, especs,
                 a_spec=a_spec)[0]


def _rows(tm, c):
    return pl.BlockSpec((tm, c), lambda i: (i, 0))


def _fix(shape):
    nd = len(shape)
    return pl.BlockSpec(shape, lambda i: (0,) * nd)


def _accumulate(ref, val):
    @pl.when(pl.program_id(0) == 0)
    def _():
        ref[...] = jnp.zeros_like(ref)

    ref[...] += val


def ln_backward(name, dy, xhat, rstd, g, tm=256):
    s, d = dy.shape
    tm = _tile(s, tm)

    def body(dy_ref, xh_ref, rstd_ref, g_ref, dr_ref, drb_ref, dg_ref, db_ref, ds_ref):
        dyv, xh = dy_ref[...], xh_ref[...]
        dxh = dyv * g_ref[...]
        m1 = jnp.mean(dxh, axis=-1, keepdims=True)
        m2 = jnp.mean(dxh * xh, axis=-1, keepdims=True)
        dr = rstd_ref[...] * (dxh - m1 - xh * m2)
        dr_ref[...] = dr
        drb_ref[...] = dr.astype(BF16)
        _accumulate(dg_ref, jnp.sum(dyv * xh, axis=0, keepdims=True))
        _accumulate(db_ref, jnp.sum(dyv, axis=0, keepdims=True))
        _accumulate(ds_ref, jnp.sum(dr, axis=0, keepdims=True))

    return pl.pallas_call(
        body, grid=(s // tm,),
        in_specs=[_rows(tm, d), _rows(tm, d), _rows(tm, 1), _fix((1, d))],
        out_specs=[_rows(tm, d), _rows(tm, d), _fix((1, d)), _fix((1, d)), _fix((1, d))],
        out_shape=[_sds((s, d), F32), _sds((s, d), BF16), _sds((1, d), F32), _sds((1, d), F32), _sds((1, d), F32)],
        compiler_params=_params(("arbitrary",)), name=name)(dy, xhat, rstd, g)


def loss_head(y, target, tm=256):
    s, d = y.shape
    tm = _tile(s, tm)

    def body(y_ref, t_ref, dy_ref, loss_ref):
        e = y_ref[...] - t_ref[...]
        dy_ref[...] = e * (1.0 / d)
        part = 0.5 * jnp.sum(jnp.mean(e * e, axis=-1, keepdims=True), axis=0, keepdims=True)
        _accumulate(loss_ref, jnp.broadcast_to(part, (1, 128)))

    return pl.pallas_call(
        body, grid=(s // tm,), in_specs=[_rows(tm, d), _rows(tm, d)],
        out_specs=[_rows(tm, d), _fix((1, 128))], out_shape=[_sds((s, d), F32), _sds((1, 128), F32)],
        compiler_params=_params(("arbitrary",)), name="loss_head")(y, target)


def _cols(s, tc, off=0):
    return pl.BlockSpec((s, tc), lambda i: (0, i + off))


def _shift_down(z, sft, rows):
    return jnp.where(rows >= sft, pltpu.roll(z, sft, 0), 0.0)


def _shift_up(z, sft, rows, s):
    return jnp.where(rows < s - sft, pltpu.roll(z, (s - sft) % s, 0), 0.0)


def short_conv_gate(u, conv_w, tc=256):
    s, d3 = u.shape
    d = d3 // 3
    nb = d // tc

    def body(b_ref, c_ref, h_ref, w_ref, o_ref):
        rows = lax.broadcasted_iota(jnp.int32, (s, tc), 0)
        z = c_ref[...] * h_ref[...]
        cz = jnp.zeros((s, tc), F32)
        for k in range(SC_WIDTH):
            sft = SC_WIDTH - 1 - k
            cz = cz + w_ref[pl.ds(k, 1), :] * (_shift_down(z, sft, rows) if sft else z)
        o_ref[...] = (b_ref[...] * cz).astype(BF16)

    return pl.pallas_call(
        body, grid=(nb,),
        in_specs=[_cols(s, tc), _cols(s, tc, nb), _cols(s, tc, 2 * nb), _cols(SC_WIDTH, tc)],
        out_specs=_cols(s, tc), out_shape=_sds((s, d), BF16),
        compiler_params=_params(("parallel",)), name="short_conv_gate")(u, u, u, conv_w)


def short_conv_gate_bwd(u, conv_w, dg, tc=256):
    s, d3 = u.shape
    d = d3 // 3
    nb = d // tc

    def body(b_ref, c_ref, h_ref, w_ref, dg_ref, du_ref, dw_ref):
        rows = lax.broadcasted_iota(jnp.int32, (s, tc), 0)
        c, h, dgv = c_ref[...], h_ref[...], dg_ref[...]
        z = c * h
        dcz = dgv * b_ref[...]
        cz = jnp.zeros((s, tc), F32)
        dz = jnp.zeros((s, tc), F32)
        for k in range(SC_WIDTH):
            sft = SC_WIDTH - 1 - k
            zs = _shift_down(z, sft, rows) if sft else z
            wk = w_ref[pl.ds(k, 1), :]
            cz = cz + wk * zs
            dz = dz + wk * (_shift_up(dcz, sft, rows, s) if sft else dcz)
            dw_ref[pl.ds(k, 1), :] = jnp.sum(dcz * zs, axis=0, keepdims=True)
        du_ref[0] = (dgv * cz).astype(BF16)
        du_ref[1] = (dz * h).astype(BF16)
        du_ref[2] = (dz * c).astype(BF16)

    return pl.pallas_call(
        body, grid=(nb,),
        in_specs=[_cols(s, tc), _cols(s, tc, nb), _cols(s, tc, 2 * nb), _cols(SC_WIDTH, tc), _cols(s, tc)],
        out_specs=[pl.BlockSpec((3, s, tc), lambda i: (0, 0, i)), _cols(SC_WIDTH, tc)],
        out_shape=[_sds((3, s, d), BF16), _sds((SC_WIDTH, d), F32)],
        compiler_params=_params(("parallel",)), name="short_conv_gate_bwd")(u, u, u, conv_w, dg)


def _store_shifted_down(ref, z, rows):
    s, tc = z.shape
    for b in range(8):
        ref[b, pl.ds(0, CONV_PAD), :] = jnp.zeros((CONV_PAD, tc), F32)
        ref[b, pl.ds(CONV_PAD, s), :] = z if b == 0 else _shift_down(z, b, rows)


def _store_shifted_up(ref, z, rows):
    s, tc = z.shape
    for b in range(8):
        ref[b, pl.ds(0, s), :] = z if b == 0 else _shift_up(z, b, rows, s)
        ref[b, pl.ds(s, CONV_PAD), :] = jnp.zeros((CONV_PAD, tc), F32)


def conformer_glu_conv(u, dw_w, dw_b, tc=128):
    s, d2 = u.shape
    d = d2 // 2
    nb = d // tc

    ch = min(CONV_CHUNK, s)

    def body(a_ref, g_ref, w_ref, b_ref, o_ref, down):
        rows = lax.broadcasted_iota(jnp.int32, (s, tc), 0)
        _store_shifted_down(down, a_ref[...] * jax.nn.sigmoid(g_ref[...]), rows)

        def chunk(ci, carry):
            r0 = pl.multiple_of(ci * ch, ch)
            acc = jnp.broadcast_to(b_ref[...], (ch, tc))
            for k in range(CONF_WIDTH):
                sft = CONF_WIDTH - 1 - k
                acc = acc + w_ref[pl.ds(k, 1), :] * down[sft % 8, pl.ds(CONV_PAD + r0 - (sft // 8) * 8, ch), :]
            o_ref[pl.ds(r0, ch), :] = acc
            return carry

        lax.fori_loop(0, s // ch, chunk, 0)

    return pl.pallas_call(
        body, grid=(nb,),
        in_specs=[_cols(s, tc), _cols(s, tc, nb), _cols(CONF_WIDTH, tc), _cols(1, tc)],
        out_specs=_cols(s, tc), out_shape=_sds((s, d), F32),
        scratch_shapes=[pltpu.VMEM((8, CONV_PAD + s, tc), F32)],
        compiler_params=_params(("parallel",)), name="conformer_glu_conv")(u, u, dw_w, dw_b)


def conformer_glu_conv_bwd(u, dw_w, dhc, tc=128):
    s, d2 = u.shape
    d = d2 // 2
    nb = d // tc
    ch = min(CONV_CHUNK, s)

    def body(a_ref, g_ref, w_ref, dhc_ref, du_ref, dbias_ref, dw_ref, db_ref, down, up, dw_acc, dh_buf):
        rows = lax.broadcasted_iota(jnp.int32, (s, tc), 0)
        a = a_ref[...]
        sg = jax.nn.sigmoid(g_ref[...])
        dhcv = dhc_ref[...]
        _store_shifted_down(down, a * sg, rows)
        _store_shifted_up(up, dhcv, rows)
        dw_acc[...] = jnp.zeros_like(dw_acc)

        def chunk(ci, carry):
            r0 = pl.multiple_of(ci * ch, ch)
            dc = dhc_ref[pl.ds(r0, ch), :]
            dh = jnp.zeros((ch, tc), F32)
            for k in range(CONF_WIDTH):
                sft = CONF_WIDTH - 1 - k
                a8, b = (sft // 8) * 8, sft % 8
                dh = dh + w_ref[pl.ds(k, 1), :] * up[b, pl.ds(r0 + a8, ch), :]
                prod = dc * down[b, pl.ds(CONV_PAD + r0 - a8, ch), :]
                dw_acc[k] += jnp.sum(prod.reshape(ch // 8, 8, tc), axis=0)
            dh_buf[pl.ds(r0, ch), :] = dh
            return carry

        lax.fori_loop(0, s // ch, chunk, 0)
        dh = dh_buf[...]
        da = dh * sg
        dgate = dh * a * sg * (1.0 - sg)
        du_ref[0] = da.astype(BF16)
        du_ref[1] = dgate.astype(BF16)
        dbias_ref[pl.ds(0, 1), :] = jnp.sum(da, axis=0, keepdims=True)
        dbias_ref[pl.ds(1, 1), :] = jnp.sum(dgate, axis=0, keepdims=True)
        db_ref[...] = jnp.sum(dhcv, axis=0, keepdims=True)
        for k in range(CONF_WIDTH):
            dw_ref[pl.ds(k, 1), :] = jnp.sum(dw_acc[k], axis=0, keepdims=True)

    return pl.pallas_call(
        body, grid=(nb,),
        in_specs=[_cols(s, tc), _cols(s, tc, nb), _cols(CONF_WIDTH, tc), _cols(s, tc)],
        out_specs=[pl.BlockSpec((2, s, tc), lambda i: (0, 0, i)), _cols(2, tc), _cols(CONF_WIDTH, tc), _cols(1, tc)],
        out_shape=[_sds((2, s, d), BF16), _sds((2, d), F32), _sds((CONF_WIDTH, d), F32), _sds((1, d), F32)],
        scratch_shapes=[pltpu.VMEM((8, CONV_PAD + s, tc), F32), pltpu.VMEM((8, CONV_PAD + s, tc), F32),
                        pltpu.VMEM((CONF_WIDTH + 1, 8, tc), F32), pltpu.VMEM((s, tc), F32)],
        compiler_params=_params(("parallel",)), name="conformer_glu_conv_bwd")(u, u, dw_w, dhc)


def conformer_norm_swish(hc, g, b, tm=256):
    s, d = hc.shape
    tm = _tile(s, tm)

    def body(h_ref, g_ref, b_ref, o_ref):
        n, _, _ = _layer_norm_rows(h_ref[...], g_ref[...], b_ref[...])
        o_ref[...] = (n * jax.nn.sigmoid(n)).astype(BF16)

    return pl.pallas_call(
        body, grid=(s // tm,), in_specs=[_rows(tm, d), _fix((1, d)), _fix((1, d))], out_specs=_rows(tm, d),
        out_shape=_sds((s, d), BF16), compiler_params=_params(("parallel",)), name="conformer_norm_swish")(hc, g, b)


def conformer_norm_swish_bwd(hc, g, b, ds, tm=256):
    s, d = hc.shape
    tm = _tile(s, tm)

    def body(h_ref, g_ref, b_ref, ds_ref, dh_ref, dg_ref, db_ref):
        n, nh, rstd = _layer_norm_rows(h_ref[...], g_ref[...], b_ref[...])
        sg = jax.nn.sigmoid(n)
        dn = ds_ref[...] * (sg * (1.0 + n * (1.0 - sg)))
        dnh = dn * g_ref[...]
        m1 = jnp.mean(dnh, axis=-1, keepdims=True)
        m2 = jnp.mean(dnh * nh, axis=-1, keepdims=True)
        dh_ref[...] = rstd * (dnh - m1 - nh * m2)
        _accumulate(dg_ref, jnp.sum(dn * nh, axis=0, keepdims=True))
        _accumulate(db_ref, jnp.sum(dn, axis=0, keepdims=True))

    return pl.pallas_call(
        body, grid=(s // tm,), in_specs=[_rows(tm, d), _fix((1, d)), _fix((1, d)), _rows(tm, d)],
        out_specs=[_rows(tm, d), _fix((1, d)), _fix((1, d))],
        out_shape=[_sds((s, d), F32), _sds((1, d), F32), _sds((1, d), F32)],
        compiler_params=_params(("arbitrary",)), name="conformer_norm_swish_bwd")(hc, g, b, ds)


def _swap_halves(x):
    lane = lax.broadcasted_iota(jnp.int32, x.shape, 1)
    return jnp.where(lane < QK_ROPE // 2, pltpu.roll(x, 128 - QK_ROPE // 2, 1), pltpu.roll(x, QK_ROPE // 2, 1))


def _rope(x, cf, sf):
    return x * cf + _swap_halves(x) * sf


def _unrope(dx, cf, sf):
    return dx * cf - _swap_halves(dx) * sf


def _rms_rows(x, g):
    r = lax.rsqrt(jnp.mean(x * x, axis=-1, keepdims=True) + RMS_EPS)
    return x * r, r


def mla_latents(t, g_q, g_kv, cf, sf, tm=256):
    s = t.shape[0]
    tm = _tile(s, tm)

    def body(t_ref, gq_ref, gkv_ref, cf_ref, sf_ref, cq_ref, ckv_ref, kpe_ref):
        xq, _ = _rms_rows(t_ref[:, 0:Q_LORA], gq_ref[...])
        cq_ref[...] = (xq * gq_ref[...]).astype(BF16)
        xkv, _ = _rms_rows(t_ref[:, Q_LORA:Q_LORA + KV_LORA], gkv_ref[...])
        ckv_ref[...] = (xkv * gkv_ref[...]).astype(BF16)
        kpe_ref[...] = _rope(t_ref[:, Q_LORA + KV_LORA:], cf_ref[...], sf_ref[...]).astype(BF16)

    w = Q_LORA + KV_LORA + 128
    return pl.pallas_call(
        body, grid=(s // tm,),
        in_specs=[_rows(tm, w), _fix((1, Q_LORA)), _fix((1, KV_LORA)), _rows(tm, 128), _rows(tm, 128)],
        out_specs=[_rows(tm, Q_LORA), _rows(tm, KV_LORA), _rows(tm, 128)],
        out_shape=[_sds((s, Q_LORA), BF16), _sds((s, KV_LORA), BF16), _sds((s, 128), BF16)],
        compiler_params=_params(("parallel",)), name="mla_latents")(t, g_q, g_kv, cf, sf)


def mla_latents_bwd(t, g_q, g_kv, cf, sf, dcq, dckv, dkpe, tm=256):
    s = t.shape[0]
    tm = _tile(s, tm)
    w = Q_LORA + KV_LORA + 128

    def rms_bwd(x, g, dy):
        xh, r = _rms_rows(x, g)
        dxh = dy * g
        return r * (dxh - xh * jnp.mean(dxh * xh, axis=-1, keepdims=True)), jnp.sum(dy * xh, axis=0, keepdims=True)

    def body(t_ref, gq_ref, gkv_ref, cf_ref, sf_ref, dcq_ref, dckv_ref, dkpe_ref, dt_ref, dgq_ref, dgkv_ref):
        dxq, dgq = rms_bwd(t_ref[:, 0:Q_LORA], gq_ref[...], dcq_ref[...])
        dxkv, dgkv = rms_bwd(t_ref[:, Q_LORA:Q_LORA + KV_LORA], gkv_ref[...], dckv_ref[...])
        dt_ref[:, 0:Q_LORA] = dxq.astype(BF16)
        dt_ref[:, Q_LORA:Q_LORA + KV_LORA] = dxkv.astype(BF16)
        dt_ref[:, Q_LORA + KV_LORA:] = _unrope(dkpe_ref[...], cf_ref[...], sf_ref[...]).astype(BF16)
        _accumulate(dgq_ref, dgq)
        _accumulate(dgkv_ref, dgkv)

    return pl.pallas_call(
        body, grid=(s // tm,),
        in_specs=[_rows(tm, w), _fix((1, Q_LORA)), _fix((1, KV_LORA)), _rows(tm, 128), _rows(tm, 128),
                  _rows(tm, Q_LORA), _rows(tm, KV_LORA), _rows(tm, 128)],
        out_specs=[_rows(tm, w), _fix((1, Q_LORA)), _fix((1, KV_LORA))],
        out_shape=[_sds((s, w), BF16), _sds((1, Q_LORA), F32), _sds((1, KV_LORA), F32)],
        compiler_params=_params(("arbitrary",)), name="mla_latents_bwd")(t, g_q, g_kv, cf, sf, dcq, dckv, dkpe)


def mla_queries(cq, w_uq, cf, sf, tm=512):
    s = cq.shape[0]
    tm = _tile(s, tm)

    def epi(acc, e, o):
        o[0][:, 0:QK_NOPE] = acc[:, 0:QK_NOPE].astype(BF16)
        o[0][:, QK_NOPE:] = _rope(acc[:, QK_NOPE:], e[0][...], e[1][...]).astype(BF16)

    return mm_nn("mla_queries", cq, w_uq, tm, HEAD_PAD, Q_LORA, epi, [_sds((s, N_HEADS * HEAD_PAD), BF16)],
                 [_ij(tm, HEAD_PAD)], [cf, sf], [_i0(tm, 128), _i0(tm, 128)])[0]


def mla_keys(ckv, w_uk, kpe, tm=512):
    s = ckv.shape[0]
    tm = _tile(s, tm)

    def epi(acc, e, o):
        o[0][:, 0:QK_NOPE] = acc.astype(BF16)
        o[0][:, QK_NOPE:] = e[0][...]

    return mm_nn("mla_keys", ckv, w_uk, tm, QK_NOPE, KV_LORA, epi, [_sds((s, N_HEADS * HEAD_PAD), BF16)],
                 [_ij(tm, HEAD_PAD)], [kpe], [_i0(tm, 128)])[0]


def _masked_scores(q, k, qi, tq, kv):
    sc = lax.dot_general(q, k, NT, preferred_element_type=F32) * ATTN_SCALE
    row = lax.broadcasted_iota(jnp.int32, (tq, kv), 0) + qi * tq
    col = lax.broadcasted_iota(jnp.int32, (tq, kv), 1)
    ok = lax.shift_right_logical(col, CHUNK_SHIFT) <= lax.shift_right_logical(row, CHUNK_SHIFT)
    return jnp.where(ok, sc, -1e30)


def attention(q, k, v, tq=256):
    s = q.shape[0]
    tq = _tile(s, tq)
    nq = s // tq

    def body(q_ref, k_ref, v_ref, o_ref):
        for qi in range(nq):
            kv = (qi + 1) * tq
            sc = _masked_scores(q_ref[pl.ds(qi * tq, tq), :], k_ref[pl.ds(0, kv), :], qi, tq, kv)
            p = jnp.exp(sc - jnp.max(sc, axis=-1, keepdims=True))
            o = lax.dot_general(p.astype(BF16), v_ref[pl.ds(0, kv), :], NN, preferred_element_type=F32)
            o_ref[pl.ds(qi * tq, tq), :] = (o / jnp.sum(p, axis=-1, keepdims=True)).astype(BF16)

    hq = pl.BlockSpec((s, HEAD_PAD), lambda h: (0, h))
    hv = pl.BlockSpec((s, V_HEAD), lambda h: (0, h))
    return pl.pallas_call(
        body, grid=(N_HEADS,), in_specs=[hq, hq, hv], out_specs=hv, out_shape=_sds((s, N_HEADS * V_HEAD), BF16),
        compiler_params=_params(("parallel",)), name="attention")(q, k, v)


def attention_bwd(q, k, v, do, tq=256):
    s = q.shape[0]
    tq = _tile(s, tq)
    nq = s // tq

    def body(q_ref, k_ref, v_ref, do_ref, dq_ref, dk_ref, dv_ref, dk_acc, dv_acc):
        dk_acc[...] = jnp.zeros_like(dk_acc)
        dv_acc[...] = jnp.zeros_like(dv_acc)
        for qi in range(nq):
            kv = (qi + 1) * tq
            qt = q_ref[pl.ds(qi * tq, tq), :]
            kt = k_ref[pl.ds(0, kv), :]
            dot = do_ref[pl.ds(qi * tq, tq), :]
            sc = _masked_scores(qt, kt, qi, tq, kv)
            p = jnp.exp(sc - jnp.max(sc, axis=-1, keepdims=True))
            p = p / jnp.sum(p, axis=-1, keepdims=True)
            dp = lax.dot_general(dot, v_ref[pl.ds(0, kv), :], NT, preferred_element_type=F32)
            delta = jnp.sum(p * dp, axis=-1, keepdims=True)
            ds = (p * (dp - delta) * ATTN_SCALE).astype(BF16)
            dq_ref[pl.ds(qi * tq, tq), :] = lax.dot_general(ds, kt, NN, preferred_element_type=F32).astype(BF16)
            dk_acc[pl.ds(0, kv), :] += lax.dot_general(ds, qt, TN, preferred_element_type=F32)
            dv_acc[pl.ds(0, kv), :] += lax.dot_general(p.astype(BF16), dot, TN, preferred_element_type=F32)
        dk_ref[...] = dk_acc[...].astype(BF16)
        dv_ref[...] = dv_acc[...].astype(BF16)

    hq = pl.BlockSpec((s, HEAD_PAD), lambda h: (0, h))
    hv = pl.BlockSpec((s, V_HEAD), lambda h: (0, h))
    return pl.pallas_call(
        body, grid=(N_HEADS,), in_specs=[hq, hq, hv, hv], out_specs=[hq, hq, hv],
        out_shape=[_sds((s, N_HEADS * HEAD_PAD), BF16), _sds((s, N_HEADS * HEAD_PAD), BF16),
                   _sds((s, N_HEADS * V_HEAD), BF16)],
        scratch_shapes=[pltpu.VMEM((s, HEAD_PAD), F32), pltpu.VMEM((s, V_HEAD), F32)],
        compiler_params=_params(("parallel",)), name="attention_bwd")(q, k, v, do)


def mla_unrope_grads(dq, dk, cf, sf, tm=256):
    s = dq.shape[0]
    tm = _tile(s, tm)

    def body(dq_ref, dk_ref, cf_ref, sf_ref, dql_ref, dkn_ref, dkpe_ref):
        cfv, sfv = cf_ref[...], sf_ref[...]
        dkpe = jnp.zeros((tm, 128), F32)
        for h in range(N_HEADS):
            lo = h * HEAD_PAD
            dql_ref[:, lo:lo + QK_NOPE] = dq_ref[:, lo:lo + QK_NOPE]
            dql_ref[:, lo + QK_NOPE:lo + HEAD_PAD] = _unrope(
                dq_ref[:, lo + QK_NOPE:lo + HEAD_PAD].astype(F32), cfv, sfv).astype(BF16)
            dkn_ref[:, h * QK_NOPE:(h + 1) * QK_NOPE] = dk_ref[:, lo:lo + QK_NOPE]
            dkpe = dkpe + dk_ref[:, lo + QK_NOPE:lo + HEAD_PAD].astype(F32)
        dkpe_ref[...] = dkpe

    wq = N_HEADS * HEAD_PAD
    return pl.pallas_call(
        body, grid=(s // tm,), in_specs=[_rows(tm, wq), _rows(tm, wq), _rows(tm, 128), _rows(tm, 128)],
        out_specs=[_rows(tm, wq), _rows(tm, N_HEADS * QK_NOPE), _rows(tm, 128)],
        out_shape=[_sds((s, wq), BF16), _sds((s, N_HEADS * QK_NOPE), BF16), _sds((s, 128), F32)],
        compiler_params=_params(("parallel",)), name="mla_unrope_grads")(dq, dk, cf, sf)


ANY = pl.BlockSpec(memory_space=pl.ANY)
GATHER_ID = 1
CHIP_EXCHANGE_ID = 2


def _handshake(peers):
    barrier = pltpu.get_barrier_semaphore()
    for peer in peers:
        pl.semaphore_signal(barrier, inc=1, device_id=peer, device_id_type=MESH)
    pl.semaphore_wait(barrier, len(peers))


def _place():
    x, y, c = lax.axis_index("x"), lax.axis_index("y"), lax.axis_index("c")
    chips = [(1 - x, y), (x, 1 - y), (1 - x, 1 - y)]
    return x, y, c, chips


def _half(ref, hc, axis=0):
    n = ref.shape[axis] // 2
    idx = (slice(None),) * axis + (pl.ds(hc * n, n),)
    return ref.at[idx]


def gather_shards(name, tensors):
    nt = len(tensors)

    def body(*refs):
        a, g = refs[:nt], refs[nt:2 * nt]
        send, recv = refs[2 * nt:]
        x, y, c, chips = _place()
        q = 2 * x + y
        sib = (x, y, 1 - c)
        _handshake([sib] + [(*chip, c) for chip in chips])

        def slot(t, chip, hc):
            return _half(g[t].at[2 * chip[0] + chip[1]], hc)

        def rc(t, k, src, dst, to):
            return pltpu.make_async_remote_copy(src_ref=src, dst_ref=dst, send_sem=send.at[t, k], recv_sem=recv.at[t, k],
                                                device_id=to, device_id_type=MESH)

        sent = []
        for t in range(nt):
            cp = rc(t, 6, a[t], g[t].at[q], sib)
            cp.start()
            sent.append(cp)
            for j, chip in enumerate(chips):
                cp = rc(t, j, _half(a[t], c), slot(t, (x, y), c), (*chip, c))
                cp.start()
                sent.append(cp)
        for t in range(nt):
            for j, chip in enumerate(chips):
                landed = slot(t, chip, c)
                rc(t, j, landed, landed, (*chip, c)).wait_recv()
                cp = rc(t, 3 + j, landed, landed, sib)
                cp.start()
                sent.append(cp)
        for t in range(nt):
            for j, chip in enumerate(chips):
                other = slot(t, chip, 1 - c)
                rc(t, 3 + j, other, other, sib).wait_recv()
            own = g[t].at[q]
            rc(t, 6, own, own, sib).wait_recv()
        for cp in sent:
            cp.wait_send()

    return pl.kernel(
        body, name=name, out_type=[_sds((N_CHIPS,) + a.shape, a.dtype) for a in tensors],
        mesh=plsc.ScalarSubcoreMesh(axis_name="sequencer", num_cores=1),
        scratch_types=[pltpu.SemaphoreType.DMA((nt, 7)), pltpu.SemaphoreType.DMA((nt, 7))],
        compiler_params=pltpu.CompilerParams(collective_id=GATHER_ID))(*tensors)


def pair_exchange(name, grads):
    nt = len(grads)

    def body(*refs):
        g, theirs = refs[:nt], refs[nt:2 * nt]
        send, recv = refs[2 * nt:]
        x, y, c, _ = _place()
        cps = []
        for t in range(nt):
            cp = pltpu.make_async_remote_copy(src_ref=_half(g[t], 1 - c, 1), dst_ref=theirs[t], send_sem=send.at[t],
                                              recv_sem=recv.at[t], device_id=(x, y, 1 - c), device_id_type=MESH)
            cp.start()
            cps.append(cp)
        for cp in cps:
            cp.wait()

    return pl.pallas_call(
        body, in_specs=[ANY] * nt, out_specs=[ANY] * nt,
        out_shape=[_sds((N_CHIPS, a.shape[1] // 2, a.shape[2]), a.dtype) for a in grads],
        scratch_shapes=[pltpu.SemaphoreType.DMA((nt,)), pltpu.SemaphoreType.DMA((nt,))],
        name=name)(*grads)


def chip_exchange(name, parts):
    nt = len(parts)

    def body(*refs):
        a, r = refs[:nt], refs[nt:2 * nt]
        send, recv = refs[2 * nt:]
        x, y, c, chips = _place()
        _handshake([(*chip, c) for chip in chips])
        cps = []
        for t in range(nt):
            for j, chip in enumerate(chips):
                cp = pltpu.make_async_remote_copy(
                    src_ref=a[t].at[2 * chip[0] + chip[1]], dst_ref=r[t].at[j], send_sem=send.at[t, j],
                    recv_sem=recv.at[t, j], device_id=(*chip, c), device_id_type=MESH)
                cp.start()
                cps.append(cp)
        for cp in cps:
            cp.wait()

    return pl.kernel(
        body, name=name, out_type=[_sds((N_CHIPS - 1,) + a.shape[1:], a.dtype) for a in parts],
        mesh=plsc.ScalarSubcoreMesh(axis_name="sequencer", num_cores=1),
        scratch_types=[pltpu.SemaphoreType.DMA((nt, 3)), pltpu.SemaphoreType.DMA((nt, 3))],
        compiler_params=pltpu.CompilerParams(collective_id=CHIP_EXCHANGE_ID))(*parts)


def pair_share(halves):
    nt = len(halves)

    def body(*refs):
        h, other = refs[:nt], refs[nt:2 * nt]
        send, recv = refs[2 * nt:]
        x, y, c, _ = _place()
        cps = []
        for t in range(nt):
            cp = pltpu.make_async_remote_copy(src_ref=h[t], dst_ref=other[t], send_sem=send.at[t], recv_sem=recv.at[t],
                                              device_id=(x, y, 1 - c), device_id_type=MESH)
            cp.start()
            cps.append(cp)
        for cp in cps:
            cp.wait()

    return pl.pallas_call(
        body, in_specs=[ANY] * nt, out_specs=[ANY] * nt, out_shape=[_sds(a.shape, a.dtype) for a in halves],
        scratch_shapes=[pltpu.SemaphoreType.DMA((nt,)), pltpu.SemaphoreType.DMA((nt,))],
        name="pair_share")(*halves)


def all_reduce_small(parts, rows):
    cdim = parts[0].shape[1]
    n = len(parts)

    def body(*refs):
        p, o_ref = refs[:n], refs[n]
        mine, buf, send, recv = refs[n + 1:]
        x, y, c, _ = _place()
        me = 4 * x + 2 * y + c
        at = 0
        for ref in p:
            mine[pl.ds(at, ref.shape[0]), :] = ref[...]
            at += ref.shape[0]
        mine[pl.ds(at, rows - at), :] = jnp.zeros((rows - at, cdim), F32)
        buf[me] = mine[...]
        cps = []
        for k in range(1, 8):
            to = (x ^ (k >> 2), y ^ ((k >> 1) & 1), c ^ (k & 1))
            cp = pltpu.make_async_remote_copy(src_ref=mine, dst_ref=buf.at[me], send_sem=send.at[k - 1],
                                              recv_sem=recv.at[k - 1], device_id=to, device_id_type=MESH)
            cp.start()
            cps.append(cp)
        for k in range(1, 8):
            frm = 4 * (x ^ (k >> 2)) + 2 * (y ^ ((k >> 1) & 1)) + (c ^ (k & 1))
            pltpu.make_async_remote_copy(src_ref=mine, dst_ref=buf.at[frm], send_sem=send.at[k - 1],
                                         recv_sem=recv.at[k - 1], device_id=(x, y, c), device_id_type=MESH).wait_recv()
        for cp in cps:
            cp.wait_send()
        acc = buf[0]
        for d in range(1, 8):
            acc = acc + buf[d]
        o_ref[...] = acc

    vm = pl.BlockSpec(memory_space=pltpu.VMEM)
    return pl.pallas_call(
        body, in_specs=[vm] * n, out_specs=vm, out_shape=_sds((rows, cdim), F32),
        scratch_shapes=[pltpu.VMEM((rows, cdim), F32), pltpu.VMEM((8, rows, cdim), F32),
                        pltpu.SemaphoreType.DMA((7,)), pltpu.SemaphoreType.DMA((7,))],
        name="all_reduce_small")(*parts)


def pair_sum(g, theirs, core, tm=256):
    _, r, c = g.shape
    tm = _tile(r // 2, tm)
    nh = r // 2 // tm

    def body(core_ref, a_ref, b_ref, o_ref):
        o_ref[...] = (a_ref[...].astype(F32) + b_ref[...].astype(F32)).astype(BF16)

    blk = (N_CHIPS, tm, c)
    return pl.pallas_call(
        body, grid_spec=pltpu.PrefetchScalarGridSpec(
            num_scalar_prefetch=1, grid=(nh,),
            in_specs=[pl.BlockSpec(blk, lambda i, cr: (0, cr[0] * nh + i, 0)), pl.BlockSpec(blk, lambda i, cr: (0, i, 0))],
            out_specs=pl.BlockSpec(blk, lambda i, cr: (0, i, 0))),
        out_shape=_sds(theirs.shape, BF16), compiler_params=_params(("parallel",)), name="pair_sum")(core, g, theirs)


def chip_sum(own, landed, chip, stack, layer, layers, tm=256):
    _, r, c = own.shape
    tm = _tile(r, tm)

    def body(chip_ref, own_ref, l_ref, *rest):
        acc = own_ref[...].astype(F32)
        for j in range(N_CHIPS - 1):
            acc = acc + l_ref[j].astype(F32)
        rest[-1][...] = acc

    in_specs = [pl.BlockSpec((None, tm, c), lambda i, qr: (qr[0], i, 0)),
                pl.BlockSpec((N_CHIPS - 1, tm, c), lambda i, qr: (0, i, 0))]
    args = [chip, own, landed]
    if stack is not None:
        in_specs.append(ANY)
        args.append(stack)
    return pl.pallas_call(
        body, grid_spec=pltpu.PrefetchScalarGridSpec(
            num_scalar_prefetch=1, grid=(r // tm,), in_specs=in_specs,
            out_specs=pl.BlockSpec((None, tm, c), lambda i, qr: (layer, i, 0))),
        out_shape=_sds((layers, r, c), F32), input_output_aliases={3: 0} if stack is not None else {},
        compiler_params=_params(("parallel",)), name="chip_sum")(*args)


def adamw_joined(w, m, v, g_mine, g_theirs, core, tm=256):
    nl, r, c = w.shape
    tm = _tile(r // 2, tm)
    nh = r // 2 // tm
    bc1 = 1.0 - ADAM_B1 ** ADAM_STEP
    bc2 = 1.0 - ADAM_B2 ** ADAM_STEP

    def body(core_ref, w_ref, m_ref, v_ref, gm_ref, gt_ref, g_ref, d_ref, nm_ref, nv_ref):
        mine = (pl.program_id(1) // nh) == core_ref[0]
        gv = jnp.where(mine, gm_ref[...], gt_ref[...])
        nm = ADAM_B1 * m_ref[...] + (1.0 - ADAM_B1) * gv
        nv = ADAM_B2 * v_ref[...] + (1.0 - ADAM_B2) * (gv * gv)
        g_ref[...] = gv
        d_ref[...] = -ADAM_LR * ((nm / bc1) / (jnp.sqrt(nv / bc2) + ADAM_EPS) + ADAM_WD * w_ref[...])
        nm_ref[...] = nm
        nv_ref[...] = nv

    full = pl.BlockSpec((None, tm, c), lambda l, i, cr: (l, i, 0))
    half = pl.BlockSpec((None, tm, c), lambda l, i, cr: (l, i % nh, 0))
    return pl.pallas_call(
        body, grid_spec=pltpu.PrefetchScalarGridSpec(
            num_scalar_prefetch=1, grid=(nl, r // tm), in_specs=[full, full, full, half, half], out_specs=[full] * 4),
        out_shape=[_sds((nl, r, c), F32)] * 4, compiler_params=_params(("parallel", "parallel")),
        name="adamw_joined")(core, w, m, v, g_mine, g_theirs)


def adamw(w, g, m, v, tm=256):
    shape = w.shape
    c = shape[-1]
    r = w.size // c
    tm = _tile(r, tm)
    bc1 = 1.0 - ADAM_B1 ** ADAM_STEP
    bc2 = 1.0 - ADAM_B2 ** ADAM_STEP

    def body(w_ref, g_ref, m_ref, v_ref, d_ref, nm_ref, nv_ref):
        gv = g_ref[...]
        nm = ADAM_B1 * m_ref[...] + (1.0 - ADAM_B1) * gv
        nv = ADAM_B2 * v_ref[...] + (1.0 - ADAM_B2) * (gv * gv)
        d_ref[...] = -ADAM_LR * ((nm / bc1) / (jnp.sqrt(nv / bc2) + ADAM_EPS) + ADAM_WD * w_ref[...])
        nm_ref[...] = nm
        nv_ref[...] = nv

    outs = pl.pallas_call(
        body, grid=(r // tm,), in_specs=[_rows(tm, c)] * 4, out_specs=[_rows(tm, c)] * 3,
        out_shape=[_sds((r, c), F32)] * 3, compiler_params=_params(("parallel",)), name="adamw")(
            w.reshape(r, c), g.reshape(r, c), m.reshape(r, c), v.reshape(r, c))
    return [o.reshape(shape) for o in outs]


WEIGHTS = ['sc_w_in', 'sc_conv_w', 'sc_w_out', 'mla_w_dq', 'mla_g_q', 'mla_w_uq', 'mla_w_dkv', 'mla_g_kv', 'mla_w_uk',
           'mla_w_uv', 'mla_w_o', 'cf_w_pw1', 'cf_b_pw1', 'cf_dw_w', 'cf_dw_b', 'cf_norm_g', 'cf_norm_b', 'cf_w_pw2',
           'cf_b_pw2', 'ff_w1', 'ff_w2', 'ln_mix_g', 'ln_mix_b', 'ln_ff_g', 'ln_ff_b']
ARGS = ['x'] + WEIGHTS + ['loss_target'] + ['m_' + n for n in WEIGHTS] + ['v_' + n for n in WEIGHTS]


def _sq_relu(h):
    r = jnp.maximum(h.astype(F32), 0.0)
    return (r * r).astype(BF16)


def _mlp_forward(i, x, xb, w1, w2, g, b):
    hb = mm_plain_nn(f"mlp{i}_up", xb, w1, BF16)
    y, yb, xh, rstd = mm_residual_ln(f"mlp{i}_down_ln", hb, w2, x, g, b, a_fn=_sq_relu)
    return (y, yb), dict(xb=xb, hb=hb, xh=xh, rstd=rstd, g=g)


def _mlp_backward(i, dy, sv, w1, w2, dw1, dw2):
    s = dy.shape[0]
    dr, drb, dg, db, _ = ln_backward(f"mlp{i}_ln_bwd", dy, sv["xh"], sv["rstd"], sv["g"])
    tm, tn = _tile(s, 1024), 512

    def epi(acc, e, o):
        o[0][...] = (acc * (2.0 * jnp.maximum(e[0][...].astype(F32), 0.0))).astype(BF16)

    dhb = mm_nt(f"mlp{i}_down_bwd", drb, w2, s, tm, tn, 1024, epi, [_sds((s, w2.k), BF16)], [_ij(tm, tn)],
                [sv["hb"]], [_ij(tm, tn)])[0]
    g_w2 = mm_tn(f"mlp{i}_dw2", sv["hb"], drb, dw2, s, 512, 1024, a_fn=_sq_relu)
    g_w1 = mm_tn(f"mlp{i}_dw1", sv["xb"], dhb, dw1, s, 1024, 512)
    dx = mm_plain_nt(f"mlp{i}_up_bwd", dhb, w1, F32, tn=1024, add=dr, add_scale=ALPHA)
    return dx, g_w1, g_w2, dg, db


def kernel(x, sc_w_in, sc_conv_w, sc_w_out, mla_w_dq, mla_g_q, mla_w_uq, mla_w_dkv, mla_g_kv, mla_w_uk, mla_w_uv, mla_w_o, cf_w_pw1, cf_b_pw1, cf_dw_w, cf_dw_b, cf_norm_g, cf_norm_b, cf_w_pw2, cf_b_pw2, ff_w1, ff_w2, ln_mix_g, ln_mix_b, ln_ff_g, ln_ff_b, loss_target, m_sc_w_in, m_sc_conv_w, m_sc_w_out, m_mla_w_dq, m_mla_g_q, m_mla_w_uq, m_mla_w_dkv, m_mla_g_kv, m_mla_w_uk, m_mla_w_uv, m_mla_w_o, m_cf_w_pw1, m_cf_b_pw1, m_cf_dw_w, m_cf_dw_b, m_cf_norm_g, m_cf_norm_b, m_cf_w_pw2, m_cf_b_pw2, m_ff_w1, m_ff_w2, m_ln_mix_g, m_ln_mix_b, m_ln_ff_g, m_ln_ff_b, v_sc_w_in, v_sc_conv_w, v_sc_w_out, v_mla_w_dq, v_mla_g_q, v_mla_w_uq, v_mla_w_dkv, v_mla_g_kv, v_mla_w_uk, v_mla_w_uv, v_mla_w_o, v_cf_w_pw1, v_cf_b_pw1, v_cf_dw_w, v_cf_dw_b, v_cf_norm_g, v_cf_norm_b, v_cf_w_pw2, v_cf_b_pw2, v_ff_w1, v_ff_w2, v_ln_mix_g, v_ln_mix_b, v_ln_ff_g, v_ln_ff_b):
    given = dict(zip(ARGS, (x, sc_w_in, sc_conv_w, sc_w_out, mla_w_dq, mla_g_q, mla_w_uq, mla_w_dkv, mla_g_kv, mla_w_uk, mla_w_uv, mla_w_o, cf_w_pw1, cf_b_pw1, cf_dw_w, cf_dw_b, cf_norm_g, cf_norm_b, cf_w_pw2, cf_b_pw2, ff_w1, ff_w2, ln_mix_g, ln_mix_b, ln_ff_g, ln_ff_b, loss_target, m_sc_w_in, m_sc_conv_w, m_sc_w_out, m_mla_w_dq, m_mla_g_q, m_mla_w_uq, m_mla_w_dkv, m_mla_g_kv, m_mla_w_uk, m_mla_w_uv, m_mla_w_o, m_cf_w_pw1, m_cf_b_pw1, m_cf_dw_w, m_cf_dw_b, m_cf_norm_g, m_cf_norm_b, m_cf_w_pw2, m_cf_b_pw2, m_ff_w1, m_ff_w2, m_ln_mix_g, m_ln_mix_b, m_ln_ff_g, m_ln_ff_b, v_sc_w_in, v_sc_conv_w, v_sc_w_out, v_mla_w_dq, v_mla_g_q, v_mla_w_uq, v_mla_w_dkv, v_mla_g_kv, v_mla_w_uk, v_mla_w_uv, v_mla_w_o, v_cf_w_pw1, v_cf_b_pw1, v_cf_dw_w, v_cf_dw_b, v_cf_norm_g, v_cf_norm_b, v_cf_w_pw2, v_cf_b_pw2, v_ff_w1, v_ff_w2, v_ln_mix_g, v_ln_mix_b, v_ln_ff_g, v_ln_ff_b)))
    s, d = x.shape[1], x.shape[2]
    d_ff = 4 * d
    dq4 = d // N_CHIPS
    xq = lax.axis_index("x") * 2 + lax.axis_index("y")

    w_dkv_pad = jnp.pad(mla_w_dkv[0], ((0, 0), (0, 128 - QK_ROPE)))
    w_uq_pad = jnp.pad(mla_w_uq[0].reshape(Q_LORA, 2, QK_NOPE + QK_ROPE), ((0, 0), (0, 0), (0, HEAD_PAD - QK_NOPE - QK_ROPE)))
    small = jnp.concatenate([
        sc_conv_w.reshape(2 * SC_WIDTH, dq4), cf_b_pw1.reshape(2, dq4), cf_dw_w[0], cf_dw_b, cf_norm_g, cf_norm_b,
        cf_b_pw2, jnp.zeros((5, dq4), F32)], axis=0)
    mlp_w = lambda i: [ff_w1[i].astype(BF16), ff_w2[i].astype(BF16)]
    g_in, g_out, g_w1, g_w2 = [None] * 2, [None] * 2, [None] * DEPTH, [None] * DEPTH
    g_in[0], g_out[0], g_w1[0], g_w2[0], g_small = gather_shards(
        "gather_layer0", [sc_w_in[0].astype(BF16), sc_w_out[0].astype(BF16)] + mlp_w(0) + [small])
    g_dqkv, g_uq, g_uk, g_uv, g_o, g_w1[1], g_w2[1] = gather_shards("gather_layer1", [
        jnp.concatenate([mla_w_dq[0], w_dkv_pad], axis=1).astype(BF16),
        w_uq_pad.reshape(Q_LORA, 2 * HEAD_PAD).astype(BF16),
        mla_w_uk.reshape(KV_LORA // N_CHIPS, N_HEADS * QK_NOPE).astype(BF16),
        mla_w_uv.reshape(KV_LORA // N_CHIPS, N_HEADS * V_HEAD).astype(BF16), mla_w_o[0].astype(BF16)] + mlp_w(1))
    g_pw1, g_pw2, g_w1[2], g_w2[2] = gather_shards(
        "gather_layer2", [cf_w_pw1[0].astype(BF16), cf_w_pw2[0].astype(BF16)] + mlp_w(2))
    g_in[1], g_out[1], g_w1[3], g_w2[3] = gather_shards(
        "gather_layer3", [sc_w_in[1].astype(BF16), sc_w_out[1].astype(BF16)] + mlp_w(3))

    wd_t = Q_LORA + KV_LORA + 128
    w_in = [Stk("col", d, 3 * d, g_in[j]) for j in range(2)]
    w_out = [Stk("row", d, d, g_out[j]) for j in range(2)]
    w_dqkv = Stk("row", d, wd_t, g_dqkv)
    w_uq = Stk("col", Q_LORA, N_HEADS * HEAD_PAD, g_uq)
    w_uk = Stk("row", KV_LORA, N_HEADS * QK_NOPE, g_uk)
    w_uv = Stk("row", KV_LORA, N_HEADS * V_HEAD, g_uv)
    w_o = Stk("row", d, d, g_o)
    w_pw1 = Stk("col", d, 2 * d, g_pw1)
    w_pw2 = Stk("row", d, d, g_pw2)
    w_1 = [Stk("col", d, d_ff, g_w1[i]) for i in range(DEPTH)]
    w_2 = [Stk("row", d_ff, d, g_w2[i]) for i in range(DEPTH)]

    def wide(rows):
        return jnp.swapaxes(rows, 0, 1).reshape(rows.shape[1], d)

    conv_w = wide(g_small[:, 0:6]).reshape(2, SC_WIDTH, d)
    b_pw1 = g_small[:, 6:8].reshape(1, 2 * d)
    dw_w = wide(g_small[:, 8:39])
    dw_b, norm_g, norm_b, b_pw2 = (wide(g_small[:, 39 + k:40 + k]) for k in range(4))

    pos = jnp.arange(s, dtype=F32)
    inv_freq = ROPE_THETA ** (-jnp.arange(0, QK_ROPE, 2, dtype=F32) / QK_ROPE)
    ang = pos[:, None] * inv_freq[None, :]
    cos, sin, zero = jnp.cos(ang), jnp.sin(ang), jnp.zeros((s, 128 - QK_ROPE), F32)
    cf = jnp.concatenate([cos, cos, zero], axis=1)
    sf = jnp.concatenate([-sin, sin, zero], axis=1)

    def row(a, i):
        return a[i:i + 1]

    xs = x.reshape(s, d)
    cur = (xs, xs.astype(BF16))
    tape = []
    for i in range(DEPTH):
        mixer, j = i % 3, i // 3
        xf, xb = cur
        lg, lb = row(ln_mix_g, i), row(ln_mix_b, i)
        if mixer == 0:
            u = mm_plain_nn(f"sc{j}_in", xb, w_in[j], F32, tn=3 * dq4)
            gb = short_conv_gate(u, conv_w[j])
            y, yb, xh, rstd = mm_residual_ln(f"sc{j}_out_ln", gb, w_out[j], xf, lg, lb)
            sv = dict(xb=xb, u=u, gb=gb)
        elif mixer == 1:
            t = mm_plain_nn("mla_down", xb, w_dqkv, F32, tn=wd_t // 2)
            cq, ckv, kpe = mla_latents(t, mla_g_q, mla_g_kv, cf, sf)
            qh = mla_queries(cq, w_uq, cf, sf)
            kh = mla_keys(ckv, w_uk, kpe)
            vh = mm_plain_nn("mla_values", ckv, w_uv, BF16, tk=KV_LORA)
            oh = attention(qh, kh, vh)
            y, yb, xh, rstd = mm_residual_ln("mla_out_ln", oh, w_o, xf, lg, lb)
            sv = dict(xb=xb, t=t, cq=cq, ckv=ckv, qh=qh, kh=kh, vh=vh, oh=oh)
        else:
            u = mm_plain_nn("cf_pw1", xb, w_pw1, F32, bias=b_pw1)
            hc = conformer_glu_conv(u, dw_w, dw_b)
            sb = conformer_norm_swish(hc, norm_g, norm_b)
            y, yb, xh, rstd = mm_residual_ln("cf_pw2_ln", sb, w_pw2, xf, lg, lb, bias=b_pw2)
            sv = dict(xb=xb, u=u, hc=hc, sb=sb)
        sv.update(xh=xh, rstd=rstd, g=lg)
        cur, sv_mlp = _mlp_forward(i, y, yb, w_1[i], w_2[i], row(ln_ff_g, i), row(ln_ff_b, i))
        tape.append((sv, sv_mlp))

    dy, loss_part = loss_head(cur[0], loss_target.reshape(s, d))
    loss = lax.psum(loss_part[0, 0], ("x", "y", "c"))

    grads = {}
    smalls = {}
    g_ln = {n: [None] * DEPTH for n in ("ln_mix_g", "ln_mix_b", "ln_ff_g", "ln_ff_b")}
    conv_grads = [None, None]
    core = lax.axis_index("c").astype(jnp.int32).reshape(1)
    chip = xq.astype(jnp.int32).reshape(1)
    mixer_grads = [["in_0", "out_0"], ["dqkv", "uq", "uk", "uv", "o"], ["pw1", "pw2"], ["in_1", "out_1"]]
    pairs, landed = {}, {}

    def start_reduction(i):
        names = mixer_grads[i] + [f"w1_{i}", f"w2_{i}"]
        theirs = pair_exchange(f"pair_exchange_layer{i}", [grads[n] for n in names])
        sums = [pair_sum(grads[n], th, core) for n, th in zip(names, theirs)]
        pairs.update(zip(names, sums))
        landed.update(zip(names, chip_exchange(f"chip_exchange_layer{i}", sums)))

    for i in reversed(range(DEPTH)):
        mixer, j = i % 3, i // 3
        sv, sv_mlp = tape[i]
        dy, grads[f"w1_{i}"], grads[f"w2_{i}"], g_ln["ln_ff_g"][i], g_ln["ln_ff_b"][i] = _mlp_backward(
            i, dy, sv_mlp, w_1[i], w_2[i], Stk("col", d, d_ff), Stk("row", d_ff, d))
        dr, drb, g_ln["ln_mix_g"][i], g_ln["ln_mix_b"][i], dr_sum = ln_backward(
            f"mix{i}_ln_bwd", dy, sv["xh"], sv["rstd"], sv["g"])
        if mixer == 0:
            dgate = mm_plain_nt(f"sc{j}_out_bwd", drb, w_out[j], F32)
            grads[f"out_{j}"] = mm_tn(f"sc{j}_dw_out", sv["gb"], drb, Stk("row", d, d), s, 512, 1024)
            du, conv_grads[j] = short_conv_gate_bwd(sv["u"], conv_w[j], dgate)
            nb = d // 256
            grads[f"in_{j}"] = mm_tn(
                f"sc{j}_dw_in", sv["xb"], du, Stk("col", d, 3 * d), s, 1024, 256,
                b_spec=pl.BlockSpec((None, s, 256), lambda i_, j_, k_: (j_ // nb, k_, j_ % nb)))
            dy = mm_plain_nt(
                f"sc{j}_in_bwd", du, w_in[j], F32, tk=256, add=dr, add_scale=ALPHA,
                a_spec_fn=(s, lambda tm, tk: pl.BlockSpec((None, tm, tk), lambda i_, j_, k_: (k_ // nb, i_, k_ % nb))))
        elif mixer == 1:
            do = mm_plain_nt("mla_out_bwd", drb, w_o, BF16)
            grads["o"] = mm_tn("mla_dw_o", sv["oh"], drb, Stk("row", d, d), s, 512, 1024)
            dqh, dkh, dvh = attention_bwd(sv["qh"], sv["kh"], sv["vh"], do)
            dql, dkn, dkpe = mla_unrope_grads(dqh, dkh, cf, sf)
            grads["uq"] = mm_tn("mla_dw_uq", sv["cq"], dql, Stk("col", Q_LORA, N_HEADS * HEAD_PAD), s, Q_LORA, 512)
            dcq = mm_plain_nt("mla_uq_bwd", dql, w_uq, F32, tn=Q_LORA)
            grads["uk"] = mm_tn("mla_dw_uk", sv["ckv"], dkn, Stk("row", KV_LORA, N_HEADS * QK_NOPE), s, KV_LORA, 1024)
            grads["uv"] = mm_tn("mla_dw_uv", sv["ckv"], dvh, Stk("row", KV_LORA, N_HEADS * V_HEAD), s, KV_LORA, 1024)
            dckv = mm_plain_nt("mla_uk_bwd", dkn, w_uk, F32, tn=KV_LORA)
            dckv = mm_plain_nt("mla_uv_bwd", dvh, w_uv, F32, tn=KV_LORA, add=dckv)
            dt, smalls["g_q"], smalls["g_kv"] = mla_latents_bwd(sv["t"], mla_g_q, mla_g_kv, cf, sf, dcq, dckv, dkpe)
            grads["dqkv"] = mm_tn("mla_dw_down", sv["xb"], dt, Stk("row", d, wd_t), s, 512, wd_t)
            dy = mm_plain_nt("mla_down_bwd", dt, w_dqkv, F32, tk=wd_t, add=dr, add_scale=ALPHA)
        else:
            dsw = mm_plain_nt("cf_pw2_bwd", drb, w_pw2, F32)
            grads["pw2"] = mm_tn("cf_dw_pw2", sv["sb"], drb, Stk("row", d, d), s, 512, 1024)
            smalls["b_pw2"] = dr_sum
            dhc, smalls["norm_g"], smalls["norm_b"] = conformer_norm_swish_bwd(sv["hc"], norm_g, norm_b, dsw)
            du, smalls["b_pw1"], smalls["dw_w"], smalls["dw_b"] = conformer_glu_conv_bwd(sv["u"], dw_w, dhc)
            nb = d // 512
            grads["pw1"] = mm_tn(
                "cf_dw_pw1", sv["xb"], du, Stk("col", d, 2 * d), s, 1024, 512,
                b_spec=pl.BlockSpec((None, s, 512), lambda i_, j_, k_: (j_ // nb, k_, j_ % nb)))
            dy = mm_plain_nt(
                "cf_pw1_bwd", du, w_pw1, F32, add=dr, add_scale=ALPHA,
                a_spec_fn=(s, lambda tm, tk: pl.BlockSpec((None, tm, tk), lambda i_, j_, k_: (k_ // nb, i_, k_ % nb))))
        start_reduction(i)
    grad_x = dy.reshape(1, s, d)

    groups = [["in_0", "in_1"], ["out_0", "out_1"], ["dqkv"], ["uq"], ["uk"], ["uv"], ["o"], ["pw1"], ["pw2"],
              [f"w1_{i}" for i in range(DEPTH)], [f"w2_{i}" for i in range(DEPTH)]]
    mine = []
    for members in groups:
        stack = None
        for layer, n in reversed(list(enumerate(members))):
            stack = chip_sum(pairs[n], landed[n], chip, stack, layer, len(members))
        mine.append(stack)
    other = pair_share(mine)

    def padded(get):
        dqkv = jnp.concatenate([get("mla_w_dq")[0], jnp.pad(get("mla_w_dkv")[0], ((0, 0), (0, 128 - QK_ROPE)))], axis=1)
        uq = jnp.pad(get("mla_w_uq")[0].reshape(Q_LORA, 2, QK_NOPE + QK_ROPE),
                     ((0, 0), (0, 0), (0, HEAD_PAD - QK_NOPE - QK_ROPE))).reshape(Q_LORA, 2 * HEAD_PAD)
        return [get("sc_w_in"), get("sc_w_out"), dqkv[None], uq[None],
                get("mla_w_uk").reshape(1, KV_LORA // N_CHIPS, d), get("mla_w_uv").reshape(1, KV_LORA // N_CHIPS, d),
                get("mla_w_o"), get("cf_w_pw1"), get("cf_w_pw2"), get("ff_w1"), get("ff_w2")]

    w_l, m_l, v_l = (padded(lambda n, p=p: given[p + n]) for p in ("", "m_", "v_"))
    res = [adamw_joined(w_l[k], m_l[k], v_l[k], mine[k], other[k], core) for k in range(len(groups))]

    def unpadded(k):
        r_in, r_out, r_dqkv, r_uq, r_uk, r_uv, r_o, r_pw1, r_pw2, r_w1, r_w2 = (r[k] for r in res)
        return {
            "sc_w_in": r_in, "sc_w_out": r_out, "mla_w_dq": r_dqkv[:, :, 0:Q_LORA],
            "mla_w_dkv": r_dqkv[:, :, Q_LORA:Q_LORA + KV_LORA + QK_ROPE],
            "mla_w_uq": r_uq.reshape(1, Q_LORA, 2, HEAD_PAD)[:, :, :, 0:QK_NOPE + QK_ROPE].reshape(mla_w_uq.shape),
            "mla_w_uk": r_uk.reshape(mla_w_uk.shape), "mla_w_uv": r_uv.reshape(mla_w_uv.shape),
            "mla_w_o": r_o, "cf_w_pw1": r_pw1, "cf_w_pw2": r_pw2, "ff_w1": r_w1, "ff_w2": r_w2}

    big_g, big_d, big_m, big_v = (unpadded(k) for k in range(4))

    pad_row = lambda a: jnp.pad(a, ((0, 0), (0, d - a.shape[1])))
    small_parts = ([g for n in ("ln_mix_g", "ln_mix_b", "ln_ff_g", "ln_ff_b") for g in g_ln[n]]
                   + [pad_row(smalls["g_q"]), pad_row(smalls["g_kv"]), conv_grads[0], conv_grads[1],
                      smalls["b_pw1"].reshape(2, d), smalls["dw_w"], smalls["dw_b"], smalls["norm_g"], smalls["norm_b"],
                      smalls["b_pw2"]])
    red = all_reduce_small(small_parts, 64)

    def shard(rows):
        return lax.dynamic_slice_in_dim(rows, xq * dq4, dq4, axis=1)

    gw = {
        **big_g,
        "ln_mix_g": red[0:4], "ln_mix_b": red[4:8], "ln_ff_g": red[8:12], "ln_ff_b": red[12:16],
        "mla_g_q": red[16:17, 0:Q_LORA], "mla_g_kv": red[17:18, 0:KV_LORA],
        "sc_conv_w": shard(red[18:24]).reshape(2, SC_WIDTH, dq4),
        "cf_b_pw1": lax.dynamic_slice_in_dim(red[24:26].reshape(1, 2 * d), xq * 2 * dq4, 2 * dq4, axis=1),
        "cf_dw_w": shard(red[26:57])[None], "cf_dw_b": shard(red[57:58]), "cf_norm_g": shard(red[58:59]),
        "cf_norm_b": shard(red[59:60]), "cf_b_pw2": shard(red[60:61]),
    }

    upd = {n: [big_d[n], big_m[n], big_v[n]] for n in big_g}

    def pack(names, width, get):
        return jnp.concatenate([get(n).reshape(-1, width) for n in names], axis=0)

    def unpack(names, packed):
        out, at = {}, 0
        for n in names:
            rows = given[n].size // packed.shape[1]
            out[n] = packed[at:at + rows].reshape(given[n].shape)
            at += rows
        return out

    rep = ["ln_mix_g", "ln_mix_b", "ln_ff_g", "ln_ff_b"]
    shd = ["sc_conv_w", "cf_b_pw1", "cf_dw_w", "cf_dw_b", "cf_norm_g", "cf_norm_b", "cf_b_pw2"]
    for names, width in ((rep, d), (shd, dq4), (["mla_g_q"], Q_LORA), (["mla_g_kv"], KV_LORA)):
        res = adamw(pack(names, width, lambda n: given[n]), pack(names, width, lambda n: gw[n]),
                    pack(names, width, lambda n: given["m_" + n]), pack(names, width, lambda n: given["v_" + n]), tm=4096)
        parts = [unpack(names, r) for r in res]
        for n in names:
            upd[n] = [p[n] for p in parts]

    return (loss, grad_x, *[gw[n].reshape(given[n].shape) for n in WEIGHTS], *[upd[n][0] for n in WEIGHTS],
            *[upd[n][1] for n in WEIGHTS], *[upd[n][2] for n in WEIGHTS])
```

```python
import jax
import jax.numpy as jnp
from jax import lax
from jax.experimental import pallas as pl
from jax.experimental.pallas import tpu as pltpu
from jax.experimental.pallas import tpu_sc as plsc

F32 = jnp.float32
BF16 = jnp.bfloat16
MESH = pl.DeviceIdType.MESH

DEPTH = 4
ALPHA = (2.0 * DEPTH) ** 0.25
LN_EPS = 1e-5
RMS_EPS = 1e-6
CHUNK_SHIFT = 6
N_HEADS = 8
QK_NOPE = 128
QK_ROPE = 64
V_HEAD = 128
HEAD_PAD = 256
Q_LORA = 384
KV_LORA = 256
ROPE_THETA = 10000.0
SC_WIDTH = 3
CONF_WIDTH = 31
CONV_PAD = 32
CONV_CHUNK = 64
N_CHIPS = 4
ATTN_SCALE = (QK_NOPE + QK_ROPE) ** -0.5

ADAM_LR = 0.001
ADAM_B1 = 0.9
ADAM_B2 = 0.999
ADAM_EPS = 1e-08
ADAM_WD = 0.01
ADAM_STEP = 10

VMEM_LIMIT = 56 * 2**20

NN = (((1,), (0,)), ((), ()))
NT = (((1,), (1,)), ((), ()))
TN = (((0,), (0,)), ((), ()))


def _params(sem=None):
    return pltpu.CompilerParams(dimension_semantics=sem, vmem_limit_bytes=VMEM_LIMIT)


class Stk:
    def __init__(self, kind, k, n, arr=None, layers=None, layer=None):
        self.kind, self.k, self.n, self.layers, self.layer = kind, k, n, layers, layer
        self.plain = kind == "row" and layers is None
        self.kloc = k // N_CHIPS if kind == "row" else k
        self.nloc = n // N_CHIPS if kind == "col" else n
        if arr is not None and self.plain:
            arr = arr.reshape(k, n)
        self.arr = arr

    @property
    def shape(self):
        if self.plain:
            return (self.k, self.n)
        lead = (N_CHIPS,) if self.layers is None else (N_CHIPS, self.layers)
        return lead + (self.kloc, self.nloc)

    def spec(self, bk, bn, f):
        if self.plain:
            return pl.BlockSpec((bk, bn), f)
        assert self.kloc % bk == 0 and self.nloc % bn == 0, (self.kloc, bk, self.nloc, bn)
        pk, pn = self.kloc // bk, self.nloc // bn
        kind, layer = self.kind, self.layer

        def imap(*g):
            kb, nb = f(*g)
            if kind == "row":
                q, kb, nb = kb // pk, kb % pk, nb
            else:
                q, kb, nb = nb // pn, kb, nb % pn
            return (q, kb, nb) if layer is None else (q, layer, kb, nb)

        block = (None, bk, bn) if layer is None else (None, None, bk, bn)
        return pl.BlockSpec(block, imap)


def _mm(name, mode, a, b, grid, a_spec, b_spec, acc_shape, extras, extra_specs, out_shapes, out_specs, epi, a_fn=None):
    nk = grid[2]
    ne = len(extras)

    def body(*refs):
        a_ref, b_ref = refs[0], refs[1]
        e_refs = refs[2:2 + ne]
        av = a_ref[...] if a_fn is None else a_fn(a_ref[...])
        part = lax.dot_general(av, b_ref[...], mode, preferred_element_type=F32)
        if nk == 1:
            epi(part, e_refs, refs[2 + ne:])
            return
        o_refs = refs[2 + ne:-1]
        acc = refs[-1]
        k = pl.program_id(2)

        @pl.when(k == 0)
        def _():
            acc[...] = part

        @pl.when(k > 0)
        def _():
            acc[...] += part

        @pl.when(k == nk - 1)
        def _():
            epi(acc[...], e_refs, o_refs)

    return pl.pallas_call(
        body, grid=grid, in_specs=[a_spec, b_spec, *extra_specs], out_specs=out_specs, out_shape=out_shapes,
        scratch_shapes=[pltpu.VMEM(acc_shape, F32)] if nk > 1 else [],
        compiler_params=_params(("parallel", "parallel", "arbitrary")), name=name)(a, b, *extras)


def _tile(n, t):
    t = min(n, t)
    while n % t:
        t -= 8
    assert t > 0, (n, t)
    return t


def mm_nn(name, a, w, tm, tn, tk, epi, out_shapes, out_specs, extras=(), extra_specs=(), a_spec=None, a_fn=None):
    m = a.shape[0]
    tm, tn, tk = _tile(m, tm), _tile(w.n, tn), _tile(w.k, tk)
    grid = (m // tm, w.n // tn, w.k // tk)
    a_spec = a_spec or pl.BlockSpec((tm, tk), lambda i, j, k: (i, k))
    b_spec = w.spec(tk, tn, lambda i, j, k: (k, j))
    return _mm(name, NN, a, w.arr, grid, a_spec, b_spec, (tm, tn), extras, extra_specs, out_shapes, out_specs, epi, a_fn)


def mm_nt(name, a, w, m, tm, tn, tk, epi, out_shapes, out_specs, extras=(), extra_specs=(), a_spec=None):
    tm, tn, tk = _tile(m, tm), _tile(w.k, tn), _tile(w.n, tk)
    grid = (m // tm, w.k // tn, w.n // tk)
    a_spec = a_spec or pl.BlockSpec((tm, tk), lambda i, j, k: (i, k))
    b_spec = w.spec(tn, tk, lambda i, j, k: (j, k))
    return _mm(name, NT, a, w.arr, grid, a_spec, b_spec, (tm, tn), extras, extra_specs, out_shapes, out_specs, epi)


def mm_tn(name, a, b, dw, s, tm=512, tn=512, tk=4096, a_spec=None, b_spec=None, a_fn=None):
    tm, tn, tk = _tile(dw.k, tm), _tile(dw.n, tn), _tile(s, tk)
    grid = (dw.k // tm, dw.n // tn, s // tk)
    a_spec = a_spec or pl.BlockSpec((tk, tm), lambda i, j, k: (k, i))
    b_spec = b_spec or pl.BlockSpec((tk, tn), lambda i, j, k: (k, j))

    def epi(acc, e, o):
        o[0][...] = acc.astype(BF16)

    out = _mm(name, TN, a, b, grid, a_spec, b_spec, (tm, tn), (), (), [jax.ShapeDtypeStruct(dw.shape, BF16)],
              [dw.spec(tm, tn, lambda i, j, k: (i, j))], epi, a_fn)[0]
    return out.reshape(N_CHIPS, dw.k // N_CHIPS, dw.n) if dw.plain else out


def _sds(shape, dtype):
    return jax.ShapeDtypeStruct(shape, dtype)


def _ij(tm, tn):
    return pl.BlockSpec((tm, tn), lambda i, j, k: (i, j))


def _i0(tm, c):
    return pl.BlockSpec((tm, c), lambda i, j, k: (i, 0))


def _0j(r, tn):
    return pl.BlockSpec((r, tn), lambda i, j, k: (0, j))


def _layer_norm_rows(r, g, b):
    mu = jnp.mean(r, axis=-1, keepdims=True)
    d = r - mu
    var = jnp.mean(d * d, axis=-1, keepdims=True)
    rstd = lax.rsqrt(var + LN_EPS)
    xh = d * rstd
    return xh * g + b, xh, rstd


def mm_residual_ln(name, a, w, x, g, b, bias=None, tm=512, tk=1024, a_fn=None):
    s, d = x.shape
    tm = _tile(s, tm)
    extras = [x, g, b] + ([bias] if bias is not None else [])
    especs = [_i0(tm, d), _0j(1, d), _0j(1, d)] + ([_0j(1, d)] if bias is not None else [])

    def epi(acc, e, o):
        r = ALPHA * e[0][...] + acc
        if bias is not None:
            r = r + e[3][...]
        y, xh, rstd = _layer_norm_rows(r, e[1][...], e[2][...])
        o[0][...] = y
        o[1][...] = y.astype(BF16)
        o[2][...] = xh
        o[3][...] = rstd

    return mm_nn(name, a, w, tm, d, tk, epi,
                 [_sds((s, d), F32), _sds((s, d), BF16), _sds((s, d), F32), _sds((s, 1), F32)],
                 [_i0(tm, d), _i0(tm, d), _i0(tm, d), _i0(tm, 1)], extras, especs, a_fn=a_fn)


def mm_plain_nn(name, a, w, out_dtype, tm=1024, tn=512, tk=1024, bias=None):
    m = a.shape[0]
    tm, tn = _tile(m, tm), _tile(w.n, tn)
    if w.kind == "col":
        tn = _tile(w.nloc, tn)

    def epi(acc, e, o):
        if bias is not None:
            acc = acc + e[0][...]
        o[0][...] = acc.astype(out_dtype)

    extras, especs = ([bias], [_0j(1, tn)]) if bias is not None else ((), ())
    return mm_nn(name, a, w, tm, tn, tk, epi, [_sds((m, w.n), out_dtype)], [_ij(tm, tn)], extras, especs)[0]


def mm_plain_nt(name, a, w, out_dtype, tm=1024, tn=512, tk=1024, add=None, add_scale=1.0, a_spec_fn=None):
    m = a.shape[0] if a_spec_fn is None else a_spec_fn[0]
    tm, tn = _tile(m, tm), _tile(w.k, tn)
    tk = _tile(w.n, tk)
    if w.kind == "col":
        tk = _tile(w.nloc, tk)
    if w.kind == "row" and not w.plain:
        tn = _tile(w.kloc, tn)

    def epi(acc, e, o):
        if add is not None:
            acc = acc + add_scale * e[0][...].astype(F32)
        o[0][...] = acc.astype(out_dtype)

    extras, especs = ([add], [_ij(tm, tn)]) if add is not None else ((), ())
    a_spec = None if a_spec_fn is None else a_spec_fn[1](tm, tk)
    return mm_nt(name, a, w, m, tm, tn, tk, epi, [_sds((m, w.k), out_dtype)], [_ij(tm, tn)], extras, especs,
                 a_spec=a_spec)[0]


def _rows(tm, c):
    return pl.BlockSpec((tm, c), lambda i: (i, 0))


def _fix(shape):
    nd = len(shape)
    return pl.BlockSpec(shape, lambda i: (0,) * nd)


def _accumulate(ref, val):
    @pl.when(pl.program_id(0) == 0)
    def _():
        ref[...] = jnp.zeros_like(ref)

    ref[...] += val


def ln_backward(name, dy, xhat, rstd, g, tm=256):
    s, d = dy.shape
    tm = _tile(s, tm)

    def body(dy_ref, xh_ref, rstd_ref, g_ref, dr_ref, drb_ref, dg_ref, db_ref, ds_ref):
        dyv, xh = dy_ref[...], xh_ref[...]
        dxh = dyv * g_ref[...]
        m1 = jnp.mean(dxh, axis=-1, keepdims=True)
        m2 = jnp.mean(dxh * xh, axis=-1, keepdims=True)
        dr = rstd_ref[...] * (dxh - m1 - xh * m2)
        dr_ref[...] = dr
        drb_ref[...] = dr.astype(BF16)
        _accumulate(dg_ref, jnp.sum(dyv * xh, axis=0, keepdims=True))
        _accumulate(db_ref, jnp.sum(dyv, axis=0, keepdims=True))
        _accumulate(ds_ref, jnp.sum(dr, axis=0, keepdims=True))

    return pl.pallas_call(
        body, grid=(s // tm,),
        in_specs=[_rows(tm, d), _rows(tm, d), _rows(tm, 1), _fix((1, d))],
        out_specs=[_rows(tm, d), _rows(tm, d), _fix((1, d)), _fix((1, d)), _fix((1, d))],
        out_shape=[_sds((s, d), F32), _sds((s, d), BF16), _sds((1, d), F32), _sds((1, d), F32), _sds((1, d), F32)],
        compiler_params=_params(("arbitrary",)), name=name)(dy, xhat, rstd, g)


def loss_head(y, target, tm=256):
    s, d = y.shape
    tm = _tile(s, tm)

    def body(y_ref, t_ref, dy_ref, loss_ref):
        e = y_ref[...] - t_ref[...]
        dy_ref[...] = e * (1.0 / d)
        part = 0.5 * jnp.sum(jnp.mean(e * e, axis=-1, keepdims=True), axis=0, keepdims=True)
        _accumulate(loss_ref, jnp.broadcast_to(part, (1, d)))

    return pl.pallas_call(
        body, grid=(s // tm,), in_specs=[_rows(tm, d), _rows(tm, d)],
        out_specs=[_rows(tm, d), _fix((1, d))], out_shape=[_sds((s, d), F32), _sds((1, d), F32)],
        compiler_params=_params(("arbitrary",)), name="loss_head")(y, target)


def _cols(s, tc, off=0):
    return pl.BlockSpec((s, tc), lambda i: (0, i + off))


def _shift_down(z, sft, rows):
    return jnp.where(rows >= sft, pltpu.roll(z, sft, 0), 0.0)


def _shift_up(z, sft, rows, s):
    return jnp.where(rows < s - sft, pltpu.roll(z, (s - sft) % s, 0), 0.0)


def short_conv_gate(u, conv_w, tc=256):
    s, d3 = u.shape
    d = d3 // 3
    nb = d // tc

    def body(b_ref, c_ref, h_ref, w_ref, o_ref):
        rows = lax.broadcasted_iota(jnp.int32, (s, tc), 0)
        z = c_ref[...] * h_ref[...]
        cz = jnp.zeros((s, tc), F32)
        for k in range(SC_WIDTH):
            sft = SC_WIDTH - 1 - k
            cz = cz + w_ref[pl.ds(k, 1), :] * (_shift_down(z, sft, rows) if sft else z)
        o_ref[...] = (b_ref[...] * cz).astype(BF16)

    return pl.pallas_call(
        body, grid=(nb,),
        in_specs=[_cols(s, tc), _cols(s, tc, nb), _cols(s, tc, 2 * nb), _cols(SC_WIDTH, tc)],
        out_specs=_cols(s, tc), out_shape=_sds((s, d), BF16),
        compiler_params=_params(("parallel",)), name="short_conv_gate")(u, u, u, conv_w)


def short_conv_gate_bwd(u, conv_w, dg, tc=256):
    s, d3 = u.shape
    d = d3 // 3
    nb = d // tc

    def body(b_ref, c_ref, h_ref, w_ref, dg_ref, du_ref, dw_ref):
        rows = lax.broadcasted_iota(jnp.int32, (s, tc), 0)
        c, h, dgv = c_ref[...], h_ref[...], dg_ref[...]
        z = c * h
        dcz = dgv * b_ref[...]
        cz = jnp.zeros((s, tc), F32)
        dz = jnp.zeros((s, tc), F32)
        for k in range(SC_WIDTH):
            sft = SC_WIDTH - 1 - k
            zs = _shift_down(z, sft, rows) if sft else z
            wk = w_ref[pl.ds(k, 1), :]
            cz = cz + wk * zs
            dz = dz + wk * (_shift_up(dcz, sft, rows, s) if sft else dcz)
            dw_ref[pl.ds(k, 1), :] = jnp.sum(dcz * zs, axis=0, keepdims=True)
        du_ref[0] = (dgv * cz).astype(BF16)
        du_ref[1] = (dz * h).astype(BF16)
        du_ref[2] = (dz * c).astype(BF16)

    return pl.pallas_call(
        body, grid=(nb,),
        in_specs=[_cols(s, tc), _cols(s, tc, nb), _cols(s, tc, 2 * nb), _cols(SC_WIDTH, tc), _cols(s, tc)],
        out_specs=[pl.BlockSpec((3, s, tc), lambda i: (0, 0, i)), _cols(SC_WIDTH, tc)],
        out_shape=[_sds((3, s, d), BF16), _sds((SC_WIDTH, d), F32)],
        compiler_params=_params(("parallel",)), name="short_conv_gate_bwd")(u, u, u, conv_w, dg)


def _store_shifted_down(ref, z, rows):
    s, tc = z.shape
    for b in range(8):
        ref[b, pl.ds(0, CONV_PAD), :] = jnp.zeros((CONV_PAD, tc), F32)
        ref[b, pl.ds(CONV_PAD, s), :] = z if b == 0 else _shift_down(z, b, rows)


def _store_shifted_up(ref, z, rows):
    s, tc = z.shape
    for b in range(8):
        ref[b, pl.ds(0, s), :] = z if b == 0 else _shift_up(z, b, rows, s)
        ref[b, pl.ds(s, CONV_PAD), :] = jnp.zeros((CONV_PAD, tc), F32)


def conformer_glu_conv(u, dw_w, dw_b, tc=128):
    s, d2 = u.shape
    d = d2 // 2
    nb = d // tc

    ch = min(CONV_CHUNK, s)

    def body(a_ref, g_ref, w_ref, b_ref, o_ref, down):
        rows = lax.broadcasted_iota(jnp.int32, (s, tc), 0)
        _store_shifted_down(down, a_ref[...] * jax.nn.sigmoid(g_ref[...]), rows)

        def chunk(ci, carry):
            r0 = pl.multiple_of(ci * ch, ch)
            acc = jnp.broadcast_to(b_ref[...], (ch, tc))
            for k in range(CONF_WIDTH):
                sft = CONF_WIDTH - 1 - k
                acc = acc + w_ref[pl.ds(k, 1), :] * down[sft % 8, pl.ds(CONV_PAD + r0 - (sft // 8) * 8, ch), :]
            o_ref[pl.ds(r0, ch), :] = acc
            return carry

        lax.fori_loop(0, s // ch, chunk, 0)

    return pl.pallas_call(
        body, grid=(nb,),
        in_specs=[_cols(s, tc), _cols(s, tc, nb), _cols(CONF_WIDTH, tc), _cols(1, tc)],
        out_specs=_cols(s, tc), out_shape=_sds((s, d), F32),
        scratch_shapes=[pltpu.VMEM((8, CONV_PAD + s, tc), F32)],
        compiler_params=_params(("parallel",)), name="conformer_glu_conv")(u, u, dw_w, dw_b)


def conformer_glu_conv_bwd(u, dw_w, dhc, tc=128):
    s, d2 = u.shape
    d = d2 // 2
    nb = d // tc
    ch = min(CONV_CHUNK, s)

    def body(a_ref, g_ref, w_ref, dhc_ref, du_ref, dbias_ref, dw_ref, db_ref, down, up, dw_acc, dh_buf):
        rows = lax.broadcasted_iota(jnp.int32, (s, tc), 0)
        a = a_ref[...]
        sg = jax.nn.sigmoid(g_ref[...])
        dhcv = dhc_ref[...]
        _store_shifted_down(down, a * sg, rows)
        _store_shifted_up(up, dhcv, rows)
        dw_acc[...] = jnp.zeros_like(dw_acc)

        def chunk(ci, carry):
            r0 = pl.multiple_of(ci * ch, ch)
            dc = dhc_ref[pl.ds(r0, ch), :]
            dh = jnp.zeros((ch, tc), F32)
            for k in range(CONF_WIDTH):
                sft = CONF_WIDTH - 1 - k
                a8, b = (sft // 8) * 8, sft % 8
                dh = dh + w_ref[pl.ds(k, 1), :] * up[b, pl.ds(r0 + a8, ch), :]
                prod = dc * down[b, pl.ds(CONV_PAD + r0 - a8, ch), :]
                dw_acc[k] += jnp.sum(prod.reshape(ch // 8, 8, tc), axis=0)
            dh_buf[pl.ds(r0, ch), :] = dh
            return carry

        lax.fori_loop(0, s // ch, chunk, 0)
        dh = dh_buf[...]
        da = dh * sg
        dgate = dh * a * sg * (1.0 - sg)
        du_ref[0] = da.astype(BF16)
        du_ref[1] = dgate.astype(BF16)
        dbias_ref[pl.ds(0, 1), :] = jnp.sum(da, axis=0, keepdims=True)
        dbias_ref[pl.ds(1, 1), :] = jnp.sum(dgate, axis=0, keepdims=True)
        db_ref[...] = jnp.sum(dhcv, axis=0, keepdims=True)
        for k in range(CONF_WIDTH):
            dw_ref[pl.ds(k, 1), :] = jnp.sum(dw_acc[k], axis=0, keepdims=True)

    return pl.pallas_call(
        body, grid=(nb,),
        in_specs=[_cols(s, tc), _cols(s, tc, nb), _cols(CONF_WIDTH, tc), _cols(s, tc)],
        out_specs=[pl.BlockSpec((2, s, tc), lambda i: (0, 0, i)), _cols(2, tc), _cols(CONF_WIDTH, tc), _cols(1, tc)],
        out_shape=[_sds((2, s, d), BF16), _sds((2, d), F32), _sds((CONF_WIDTH, d), F32), _sds((1, d), F32)],
        scratch_shapes=[pltpu.VMEM((8, CONV_PAD + s, tc), F32), pltpu.VMEM((8, CONV_PAD + s, tc), F32),
                        pltpu.VMEM((CONF_WIDTH + 1, 8, tc), F32), pltpu.VMEM((s, tc), F32)],
        compiler_params=_params(("parallel",)), name="conformer_glu_conv_bwd")(u, u, dw_w, dhc)


def conformer_norm_swish(hc, g, b, tm=256):
    s, d = hc.shape
    tm = _tile(s, tm)

    def body(h_ref, g_ref, b_ref, o_ref):
        n, _, _ = _layer_norm_rows(h_ref[...], g_ref[...], b_ref[...])
        o_ref[...] = (n * jax.nn.sigmoid(n)).astype(BF16)

    return pl.pallas_call(
        body, grid=(s // tm,), in_specs=[_rows(tm, d), _fix((1, d)), _fix((1, d))], out_specs=_rows(tm, d),
        out_shape=_sds((s, d), BF16), compiler_params=_params(("parallel",)), name="conformer_norm_swish")(hc, g, b)


def conformer_norm_swish_bwd(hc, g, b, ds, tm=256):
    s, d = hc.shape
    tm = _tile(s, tm)

    def body(h_ref, g_ref, b_ref, ds_ref, dh_ref, dg_ref, db_ref):
        n, nh, rstd = _layer_norm_rows(h_ref[...], g_ref[...], b_ref[...])
        sg = jax.nn.sigmoid(n)
        dn = ds_ref[...] * (sg * (1.0 + n * (1.0 - sg)))
        dnh = dn * g_ref[...]
        m1 = jnp.mean(dnh, axis=-1, keepdims=True)
        m2 = jnp.mean(dnh * nh, axis=-1, keepdims=True)
        dh_ref[...] = rstd * (dnh - m1 - nh * m2)
        _accumulate(dg_ref, jnp.sum(dn * nh, axis=0, keepdims=True))
        _accumulate(db_ref, jnp.sum(dn, axis=0, keepdims=True))

    return pl.pallas_call(
        body, grid=(s // tm,), in_specs=[_rows(tm, d), _fix((1, d)), _fix((1, d)), _rows(tm, d)],
        out_specs=[_rows(tm, d), _fix((1, d)), _fix((1, d))],
        out_shape=[_sds((s, d), F32), _sds((1, d), F32), _sds((1, d), F32)],
        compiler_params=_params(("arbitrary",)), name="conformer_norm_swish_bwd")(hc, g, b, ds)


def _swap_halves(x):
    lane = lax.broadcasted_iota(jnp.int32, x.shape, 1)
    return jnp.where(lane < QK_ROPE // 2, pltpu.roll(x, 128 - QK_ROPE // 2, 1), pltpu.roll(x, QK_ROPE // 2, 1))


def _rope(x, cf, sf):
    return x * cf + _swap_halves(x) * sf


def _unrope(dx, cf, sf):
    return dx * cf - _swap_halves(dx) * sf


def _rms_rows(x, g):
    r = lax.rsqrt(jnp.mean(x * x, axis=-1, keepdims=True) + RMS_EPS)
    return x * r, r


def mla_latents(t, g_q, g_kv, cf, sf, tm=256):
    s = t.shape[0]
    tm = _tile(s, tm)

    def body(t_ref, gq_ref, gkv_ref, cf_ref, sf_ref, cq_ref, ckv_ref, kpe_ref):
        xq, _ = _rms_rows(t_ref[:, 0:Q_LORA], gq_ref[...])
        cq_ref[...] = (xq * gq_ref[...]).astype(BF16)
        xkv, _ = _rms_rows(t_ref[:, Q_LORA:Q_LORA + KV_LORA], gkv_ref[...])
        ckv_ref[...] = (xkv * gkv_ref[...]).astype(BF16)
        kpe_ref[...] = _rope(t_ref[:, Q_LORA + KV_LORA:], cf_ref[...], sf_ref[...]).astype(BF16)

    w = Q_LORA + KV_LORA + 128
    return pl.pallas_call(
        body, grid=(s // tm,),
        in_specs=[_rows(tm, w), _fix((1, Q_LORA)), _fix((1, KV_LORA)), _rows(tm, 128), _rows(tm, 128)],
        out_specs=[_rows(tm, Q_LORA), _rows(tm, KV_LORA), _rows(tm, 128)],
        out_shape=[_sds((s, Q_LORA), BF16), _sds((s, KV_LORA), BF16), _sds((s, 128), BF16)],
        compiler_params=_params(("parallel",)), name="mla_latents")(t, g_q, g_kv, cf, sf)


def mla_latents_bwd(t, g_q, g_kv, cf, sf, dcq, dckv, dkpe, tm=256):
    s = t.shape[0]
    tm = _tile(s, tm)
    w = Q_LORA + KV_LORA + 128

    def rms_bwd(x, g, dy):
        xh, r = _rms_rows(x, g)
        dxh = dy * g
        return r * (dxh - xh * jnp.mean(dxh * xh, axis=-1, keepdims=True)), jnp.sum(dy * xh, axis=0, keepdims=True)

    def body(t_ref, gq_ref, gkv_ref, cf_ref, sf_ref, dcq_ref, dckv_ref, dkpe_ref, dt_ref, dgq_ref, dgkv_ref):
        dxq, dgq = rms_bwd(t_ref[:, 0:Q_LORA], gq_ref[...], dcq_ref[...])
        dxkv, dgkv = rms_bwd(t_ref[:, Q_LORA:Q_LORA + KV_LORA], gkv_ref[...], dckv_ref[...])
        dt_ref[:, 0:Q_LORA] = dxq.astype(BF16)
        dt_ref[:, Q_LORA:Q_LORA + KV_LORA] = dxkv.astype(BF16)
        dt_ref[:, Q_LORA + KV_LORA:] = _unrope(dkpe_ref[...], cf_ref[...], sf_ref[...]).astype(BF16)
        _accumulate(dgq_ref, dgq)
        _accumulate(dgkv_ref, dgkv)

    return pl.pallas_call(
        body, grid=(s // tm,),
        in_specs=[_rows(tm, w), _fix((1, Q_LORA)), _fix((1, KV_LORA)), _rows(tm, 128), _rows(tm, 128),
                  _rows(tm, Q_LORA), _rows(tm, KV_LORA), _rows(tm, 128)],
        out_specs=[_rows(tm, w), _fix((1, Q_LORA)), _fix((1, KV_LORA))],
        out_shape=[_sds((s, w), BF16), _sds((1, Q_LORA), F32), _sds((1, KV_LORA), F32)],
        compiler_params=_params(("arbitrary",)), name="mla_latents_bwd")(t, g_q, g_kv, cf, sf, dcq, dckv, dkpe)


def mla_queries(cq, w_uq, cf, sf, tm=512):
    s = cq.shape[0]
    tm = _tile(s, tm)

    def epi(acc, e, o):
        o[0][:, 0:QK_NOPE] = acc[:, 0:QK_NOPE].astype(BF16)
        o[0][:, QK_NOPE:] = _rope(acc[:, QK_NOPE:], e[0][...], e[1][...]).astype(BF16)

    return mm_nn("mla_queries", cq, w_uq, tm, HEAD_PAD, Q_LORA, epi, [_sds((s, N_HEADS * HEAD_PAD), BF16)],
                 [_ij(tm, HEAD_PAD)], [cf, sf], [_i0(tm, 128), _i0(tm, 128)])[0]


def mla_keys(ckv, w_uk, kpe, tm=512):
    s = ckv.shape[0]
    tm = _tile(s, tm)

    def epi(acc, e, o):
        o[0][:, 0:QK_NOPE] = acc.astype(BF16)
        o[0][:, QK_NOPE:] = e[0][...]

    return mm_nn("mla_keys", ckv, w_uk, tm, QK_NOPE, KV_LORA, epi, [_sds((s, N_HEADS * HEAD_PAD), BF16)],
                 [_ij(tm, HEAD_PAD)], [kpe], [_i0(tm, 128)])[0]


def _masked_scores(q, k, qi, tq, kv):
    sc = lax.dot_general(q, k, NT, preferred_element_type=F32) * ATTN_SCALE
    row = lax.broadcasted_iota(jnp.int32, (tq, kv), 0) + qi * tq
    col = lax.broadcasted_iota(jnp.int32, (tq, kv), 1)
    ok = lax.shift_right_logical(col, CHUNK_SHIFT) <= lax.shift_right_logical(row, CHUNK_SHIFT)
    return jnp.where(ok, sc, -1e30)


def attention(q, k, v, tq=256):
    s = q.shape[0]
    tq = _tile(s, tq)
    nq = s // tq

    def body(q_ref, k_ref, v_ref, o_ref):
        for qi in range(nq):
            kv = (qi + 1) * tq
            sc = _masked_scores(q_ref[pl.ds(qi * tq, tq), :], k_ref[pl.ds(0, kv), :], qi, tq, kv)
            p = jnp.exp(sc - jnp.max(sc, axis=-1, keepdims=True))
            o = lax.dot_general(p.astype(BF16), v_ref[pl.ds(0, kv), :], NN, preferred_element_type=F32)
            o_ref[pl.ds(qi * tq, tq), :] = (o / jnp.sum(p, axis=-1, keepdims=True)).astype(BF16)

    hq = pl.BlockSpec((s, HEAD_PAD), lambda h: (0, h))
    hv = pl.BlockSpec((s, V_HEAD), lambda h: (0, h))
    return pl.pallas_call(
        body, grid=(N_HEADS,), in_specs=[hq, hq, hv], out_specs=hv, out_shape=_sds((s, N_HEADS * V_HEAD), BF16),
        compiler_params=_params(("parallel",)), name="attention")(q, k, v)


def attention_bwd(q, k, v, do, tq=256):
    s = q.shape[0]
    tq = _tile(s, tq)
    nq = s // tq

    def body(q_ref, k_ref, v_ref, do_ref, dq_ref, dk_ref, dv_ref, dk_acc, dv_acc):
        dk_acc[...] = jnp.zeros_like(dk_acc)
        dv_acc[...] = jnp.zeros_like(dv_acc)
        for qi in range(nq):
            kv = (qi + 1) * tq
            qt = q_ref[pl.ds(qi * tq, tq), :]
            kt = k_ref[pl.ds(0, kv), :]
            dot = do_ref[pl.ds(qi * tq, tq), :]
            sc = _masked_scores(qt, kt, qi, tq, kv)
            p = jnp.exp(sc - jnp.max(sc, axis=-1, keepdims=True))
            p = p / jnp.sum(p, axis=-1, keepdims=True)
            dp = lax.dot_general(dot, v_ref[pl.ds(0, kv), :], NT, preferred_element_type=F32)
            delta = jnp.sum(p * dp, axis=-1, keepdims=True)
            ds = (p * (dp - delta) * ATTN_SCALE).astype(BF16)
            dq_ref[pl.ds(qi * tq, tq), :] = lax.dot_general(ds, kt, NN, preferred_element_type=F32).astype(BF16)
            dk_acc[pl.ds(0, kv), :] += lax.dot_general(ds, qt, TN, preferred_element_type=F32)
            dv_acc[pl.ds(0, kv), :] += lax.dot_general(p.astype(BF16), dot, TN, preferred_element_type=F32)
        dk_ref[...] = dk_acc[...].astype(BF16)
        dv_ref[...] = dv_acc[...].astype(BF16)

    hq = pl.BlockSpec((s, HEAD_PAD), lambda h: (0, h))
    hv = pl.BlockSpec((s, V_HEAD), lambda h: (0, h))
    return pl.pallas_call(
        body, grid=(N_HEADS,), in_specs=[hq, hq, hv, hv], out_specs=[hq, hq, hv],
        out_shape=[_sds((s, N_HEADS * HEAD_PAD), BF16), _sds((s, N_HEADS * HEAD_PAD), BF16),
                   _sds((s, N_HEADS * V_HEAD), BF16)],
        scratch_shapes=[pltpu.VMEM((s, HEAD_PAD), F32), pltpu.VMEM((s, V_HEAD), F32)],
        compiler_params=_params(("parallel",)), name="attention_bwd")(q, k, v, do)


def mla_unrope_grads(dq, dk, cf, sf, tm=256):
    s = dq.shape[0]
    tm = _tile(s, tm)

    def body(dq_ref, dk_ref, cf_ref, sf_ref, dql_ref, dkn_ref, dkpe_ref):
        cfv, sfv = cf_ref[...], sf_ref[...]
        dkpe = jnp.zeros((tm, 128), F32)
        for h in range(N_HEADS):
            lo = h * HEAD_PAD
            dql_ref[:, lo:lo + QK_NOPE] = dq_ref[:, lo:lo + QK_NOPE]
            dql_ref[:, lo + QK_NOPE:lo + HEAD_PAD] = _unrope(
                dq_ref[:, lo + QK_NOPE:lo + HEAD_PAD].astype(F32), cfv, sfv).astype(BF16)
            dkn_ref[:, h * QK_NOPE:(h + 1) * QK_NOPE] = dk_ref[:, lo:lo + QK_NOPE]
            dkpe = dkpe + dk_ref[:, lo + QK_NOPE:lo + HEAD_PAD].astype(F32)
        dkpe_ref[...] = dkpe

    wq = N_HEADS * HEAD_PAD
    return pl.pallas_call(
        body, grid=(s // tm,), in_specs=[_rows(tm, wq), _rows(tm, wq), _rows(tm, 128), _rows(tm, 128)],
        out_specs=[_rows(tm, wq), _rows(tm, N_HEADS * QK_NOPE), _rows(tm, 128)],
        out_shape=[_sds((s, wq), BF16), _sds((s, N_HEADS * QK_NOPE), BF16), _sds((s, 128), F32)],
        compiler_params=_params(("parallel",)), name="mla_unrope_grads")(dq, dk, cf, sf)


ANY = pl.BlockSpec(memory_space=pl.ANY)
GATHER_ID = 1
CHIP_EXCHANGE_ID = 2
PAIR_ID = 3
ALL_ID = 4


def _handshake(peers):
    barrier = pltpu.get_barrier_semaphore()
    for peer in peers:
        pl.semaphore_signal(barrier, inc=1, device_id=peer, device_id_type=MESH)
    pl.semaphore_wait(barrier, len(peers))


def _place():
    x, y, c = lax.axis_index("x"), lax.axis_index("y"), lax.axis_index("c")
    chips = [(1 - x, y), (x, 1 - y), (1 - x, 1 - y)]
    return x, y, c, chips


def _half(ref, hc, axis=0):
    n = ref.shape[axis] // 2
    idx = (slice(None),) * axis + (pl.ds(hc * n, n),)
    return ref.at[idx]


def gather_shards(name, tensors):
    nt = len(tensors)

    def body(*refs):
        a, g = refs[:nt], refs[nt:2 * nt]
        send, recv = refs[2 * nt:]
        x, y, c, chips = _place()
        q = 2 * x + y
        sib = (x, y, 1 - c)
        _handshake([sib] + [(*chip, c) for chip in chips])

        def slot(t, chip, hc):
            return _half(g[t].at[2 * chip[0] + chip[1]], hc)

        def rc(t, k, src, dst, to):
            return pltpu.make_async_remote_copy(src_ref=src, dst_ref=dst, send_sem=send.at[t, k], recv_sem=recv.at[t, k],
                                                device_id=to, device_id_type=MESH)

        sent = []
        for t in range(nt):
            cp = rc(t, 6, a[t], g[t].at[q], sib)
            cp.start()
            sent.append(cp)
            for j, chip in enumerate(chips):
                cp = rc(t, j, _half(a[t], c), slot(t, (x, y), c), (*chip, c))
                cp.start()
                sent.append(cp)
        for t in range(nt):
            for j, chip in enumerate(chips):
                landed = slot(t, chip, c)
                rc(t, j, landed, landed, (*chip, c)).wait_recv()
                cp = rc(t, 3 + j, landed, landed, sib)
                cp.start()
                sent.append(cp)
        for t in range(nt):
            for j, chip in enumerate(chips):
                other = slot(t, chip, 1 - c)
                rc(t, 3 + j, other, other, sib).wait_recv()
            own = g[t].at[q]
            rc(t, 6, own, own, sib).wait_recv()
        for cp in sent:
            cp.wait_send()

    return pl.kernel(
        body, name=name, out_type=[_sds((N_CHIPS,) + a.shape, a.dtype) for a in tensors],
        mesh=plsc.ScalarSubcoreMesh(axis_name="sequencer", num_cores=1),
        scratch_types=[pltpu.SemaphoreType.DMA((nt, 7)), pltpu.SemaphoreType.DMA((nt, 7))],
        compiler_params=pltpu.CompilerParams(collective_id=GATHER_ID))(*tensors)


def pair_exchange(name, grads):
    nt = len(grads)

    def body(*refs):
        g, theirs = refs[:nt], refs[nt:2 * nt]
        send, recv = refs[2 * nt:]
        x, y, c, _ = _place()
        _handshake([(x, y, 1 - c)])
        cps = []
        for t in range(nt):
            cp = pltpu.make_async_remote_copy(src_ref=_half(g[t], 1 - c, 1), dst_ref=theirs[t], send_sem=send.at[t],
                                              recv_sem=recv.at[t], device_id=(x, y, 1 - c), device_id_type=MESH)
            cp.start()
            cps.append(cp)
        for cp in cps:
            cp.wait()

    return pl.kernel(
        body, name=name, out_type=[_sds((N_CHIPS, a.shape[1] // 2, a.shape[2]), a.dtype) for a in grads],
        mesh=plsc.ScalarSubcoreMesh(axis_name="sequencer", num_cores=1),
        scratch_types=[pltpu.SemaphoreType.DMA((nt,)), pltpu.SemaphoreType.DMA((nt,))],
        compiler_params=pltpu.CompilerParams(collective_id=PAIR_ID))(*grads)


def chip_exchange(name, parts):
    nt = len(parts)

    def body(*refs):
        a, r = refs[:nt], refs[nt:2 * nt]
        send, recv = refs[2 * nt:]
        x, y, c, chips = _place()
        _handshake([(*chip, c) for chip in chips])
        cps = []
        for t in range(nt):
            for j, chip in enumerate(chips):
                cp = pltpu.make_async_remote_copy(
                    src_ref=a[t].at[2 * chip[0] + chip[1]], dst_ref=r[t].at[j], send_sem=send.at[t, j],
                    recv_sem=recv.at[t, j], device_id=(*chip, c), device_id_type=MESH)
                cp.start()
                cps.append(cp)
        for cp in cps:
            cp.wait()

    return pl.kernel(
        body, name=name, out_type=[_sds((N_CHIPS - 1,) + a.shape[1:], a.dtype) for a in parts],
        mesh=plsc.ScalarSubcoreMesh(axis_name="sequencer", num_cores=1),
        scratch_types=[pltpu.SemaphoreType.DMA((nt, 3)), pltpu.SemaphoreType.DMA((nt, 3))],
        compiler_params=pltpu.CompilerParams(collective_id=CHIP_EXCHANGE_ID))(*parts)


def pair_share(name, halves):
    nt = len(halves)

    def body(*refs):
        h, other = refs[:nt], refs[nt:2 * nt]
        send, recv = refs[2 * nt:]
        x, y, c, _ = _place()
        _handshake([(x, y, 1 - c)])
        cps = []
        for t in range(nt):
            cp = pltpu.make_async_remote_copy(src_ref=h[t], dst_ref=other[t], send_sem=send.at[t], recv_sem=recv.at[t],
                                              device_id=(x, y, 1 - c), device_id_type=MESH)
            cp.start()
            cps.append(cp)
        for cp in cps:
            cp.wait()

    return pl.kernel(
        body, name=name, out_type=[_sds(a.shape, a.dtype) for a in halves],
        mesh=plsc.ScalarSubcoreMesh(axis_name="sequencer", num_cores=1),
        scratch_types=[pltpu.SemaphoreType.DMA((nt,)), pltpu.SemaphoreType.DMA((nt,))],
        compiler_params=pltpu.CompilerParams(collective_id=PAIR_ID))(*halves)


def all_reduce_small(parts, rows):
    cdim = parts[0].shape[1]
    n = len(parts)
    vm = pl.BlockSpec(memory_space=pltpu.VMEM)

    def pack(*refs):
        p, o_ref = refs[:n], refs[n]
        at = 0
        for ref in p:
            o_ref[pl.ds(at, ref.shape[0]), :] = ref[...]
            at += ref.shape[0]
        o_ref[pl.ds(at, rows - at), :] = jnp.zeros((rows - at, cdim), F32)

    mine = pl.pallas_call(pack, in_specs=[vm] * n, out_specs=vm, out_shape=_sds((rows, cdim), F32), name="small_pack")(*parts)

    def exchange(mine_ref, buf, send, recv, lsem):
        x, y, c, _ = _place()
        me = 4 * x + 2 * y + c
        peers = [(x ^ (k >> 2), y ^ ((k >> 1) & 1), c ^ (k & 1)) for k in range(1, 8)]
        _handshake(peers)
        own = pltpu.make_async_copy(mine_ref, buf.at[me], lsem)
        own.start()
        cps = []
        for k, to in enumerate(peers):
            cp = pltpu.make_async_remote_copy(src_ref=mine_ref, dst_ref=buf.at[me], send_sem=send.at[k], recv_sem=recv.at[k],
                                              device_id=to, device_id_type=MESH)
            cp.start()
            cps.append(cp)
        for k, (px, py, pc) in enumerate(peers):
            pltpu.make_async_remote_copy(src_ref=mine_ref, dst_ref=buf.at[4 * px + 2 * py + pc], send_sem=send.at[k],
                                         recv_sem=recv.at[k], device_id=(x, y, c), device_id_type=MESH).wait_recv()
        for cp in cps:
            cp.wait_send()
        own.wait()

    landed = pl.kernel(
        exchange, name="small_exchange", out_type=_sds((8, rows, cdim), F32),
        mesh=plsc.ScalarSubcoreMesh(axis_name="sequencer", num_cores=1),
        scratch_types=[pltpu.SemaphoreType.DMA((7,)), pltpu.SemaphoreType.DMA((7,)), pltpu.SemaphoreType.DMA],
        compiler_params=pltpu.CompilerParams(collective_id=ALL_ID))(mine)

    def total(buf, o_ref):
        acc = buf[0]
        for d in range(1, 8):
            acc = acc + buf[d]
        o_ref[...] = acc

    return pl.pallas_call(total, in_specs=[vm], out_specs=vm, out_shape=_sds((rows, cdim), F32), name="small_sum")(landed)


def pair_sum(g, theirs, core, tm=256):
    _, r, c = g.shape
    tm = _tile(r // 2, tm)
    nh = r // 2 // tm

    def body(core_ref, a_ref, b_ref, o_ref):
        o_ref[...] = (a_ref[...].astype(F32) + b_ref[...].astype(F32)).astype(BF16)

    blk = (N_CHIPS, tm, c)
    return pl.pallas_call(
        body, grid_spec=pltpu.PrefetchScalarGridSpec(
            num_scalar_prefetch=1, grid=(nh,),
            in_specs=[pl.BlockSpec(blk, lambda i, cr: (0, cr[0] * nh + i, 0)), pl.BlockSpec(blk, lambda i, cr: (0, i, 0))],
            out_specs=pl.BlockSpec(blk, lambda i, cr: (0, i, 0))),
        out_shape=_sds(theirs.shape, BF16), compiler_params=_params(("parallel",)), name="pair_sum")(core, g, theirs)


def chip_sum(own, landed, chip, stack, layer, layers, tm=256):
    _, r, c = own.shape
    tm = _tile(r, tm)

    def body(chip_ref, own_ref, l_ref, *rest):
        acc = own_ref[...].astype(F32)
        for j in range(N_CHIPS - 1):
            acc = acc + l_ref[j].astype(F32)
        rest[-1][...] = acc

    in_specs = [pl.BlockSpec((None, tm, c), lambda i, qr: (qr[0], i, 0)),
                pl.BlockSpec((N_CHIPS - 1, tm, c), lambda i, qr: (0, i, 0))]
    args = [chip, own, landed]
    if stack is not None:
        in_specs.append(ANY)
        args.append(stack)
    return pl.pallas_call(
        body, grid_spec=pltpu.PrefetchScalarGridSpec(
            num_scalar_prefetch=1, grid=(r // tm,), in_specs=in_specs,
            out_specs=pl.BlockSpec((None, tm, c), lambda i, qr: (layer, i, 0))),
        out_shape=_sds((layers, r, c), F32), input_output_aliases={3: 0} if stack is not None else {},
        compiler_params=_params(("parallel",)), name="chip_sum")(*args)


def adamw_joined(w, m, v, g_mine, g_theirs, core, tm=256):
    nl, r, c = w.shape
    tm = _tile(r // 2, tm)
    nh = r // 2 // tm
    bc1 = 1.0 - ADAM_B1 ** ADAM_STEP
    bc2 = 1.0 - ADAM_B2 ** ADAM_STEP

    def body(core_ref, w_ref, m_ref, v_ref, gm_ref, gt_ref, g_ref, d_ref, nm_ref, nv_ref):
        mine = (pl.program_id(1) // nh) == core_ref[0]
        gv = jnp.where(mine, gm_ref[...], gt_ref[...])
        nm = ADAM_B1 * m_ref[...] + (1.0 - ADAM_B1) * gv
        nv = ADAM_B2 * v_ref[...] + (1.0 - ADAM_B2) * (gv * gv)
        g_ref[...] = gv
        d_ref[...] = -ADAM_LR * ((nm / bc1) / (jnp.sqrt(nv / bc2) + ADAM_EPS) + ADAM_WD * w_ref[...])
        nm_ref[...] = nm
        nv_ref[...] = nv

    full = pl.BlockSpec((None, tm, c), lambda l, i, cr: (l, i, 0))
    half = pl.BlockSpec((None, tm, c), lambda l, i, cr: (l, i % nh, 0))
    return pl.pallas_call(
        body, grid_spec=pltpu.PrefetchScalarGridSpec(
            num_scalar_prefetch=1, grid=(nl, r // tm), in_specs=[full, full, full, half, half], out_specs=[full] * 4),
        out_shape=[_sds((nl, r, c), F32)] * 4, compiler_params=_params(("parallel", "parallel")),
        name="adamw_joined")(core, w, m, v, g_mine, g_theirs)


def adamw(w, g, m, v, tm=256):
    shape = w.shape
    c = shape[-1]
    r = w.size // c
    tm = _tile(r, tm)
    bc1 = 1.0 - ADAM_B1 ** ADAM_STEP
    bc2 = 1.0 - ADAM_B2 ** ADAM_STEP

    def body(w_ref, g_ref, m_ref, v_ref, d_ref, nm_ref, nv_ref):
        gv = g_ref[...]
        nm = ADAM_B1 * m_ref[...] + (1.0 - ADAM_B1) * gv
        nv = ADAM_B2 * v_ref[...] + (1.0 - ADAM_B2) * (gv * gv)
        d_ref[...] = -ADAM_LR * ((nm / bc1) / (jnp.sqrt(nv / bc2) + ADAM_EPS) + ADAM_WD * w_ref[...])
        nm_ref[...] = nm
        nv_ref[...] = nv

    outs = pl.pallas_call(
        body, grid=(r // tm,), in_specs=[_rows(tm, c)] * 4, out_specs=[_rows(tm, c)] * 3,
        out_shape=[_sds((r, c), F32)] * 3, compiler_params=_params(("parallel",)), name="adamw")(
            w.reshape(r, c), g.reshape(r, c), m.reshape(r, c), v.reshape(r, c))
    return [o.reshape(shape) for o in outs]


WEIGHTS = ['sc_w_in', 'sc_conv_w', 'sc_w_out', 'mla_w_dq', 'mla_g_q', 'mla_w_uq', 'mla_w_dkv', 'mla_g_kv', 'mla_w_uk',
           'mla_w_uv', 'mla_w_o', 'cf_w_pw1', 'cf_b_pw1', 'cf_dw_w', 'cf_dw_b', 'cf_norm_g', 'cf_norm_b', 'cf_w_pw2',
           'cf_b_pw2', 'ff_w1', 'ff_w2', 'ln_mix_g', 'ln_mix_b', 'ln_ff_g', 'ln_ff_b']
ARGS = ['x'] + WEIGHTS + ['loss_target'] + ['m_' + n for n in WEIGHTS] + ['v_' + n for n in WEIGHTS]


def _sq_relu(h):
    r = jnp.maximum(h.astype(F32), 0.0)
    return (r * r).astype(BF16)


def _mlp_forward(i, x, xb, w1, w2, g, b):
    hb = mm_plain_nn(f"mlp{i}_up", xb, w1, BF16)
    y, yb, xh, rstd = mm_residual_ln(f"mlp{i}_down_ln", hb, w2, x, g, b, a_fn=_sq_relu)
    return (y, yb), dict(xb=xb, hb=hb, xh=xh, rstd=rstd, g=g)


def _mlp_backward(i, dy, sv, w1, w2, dw1, dw2):
    s = dy.shape[0]
    dr, drb, dg, db, _ = ln_backward(f"mlp{i}_ln_bwd", dy, sv["xh"], sv["rstd"], sv["g"])
    tm, tn = _tile(s, 1024), 512

    def epi(acc, e, o):
        o[0][...] = (acc * (2.0 * jnp.maximum(e[0][...].astype(F32), 0.0))).astype(BF16)

    dhb = mm_nt(f"mlp{i}_down_bwd", drb, w2, s, tm, tn, 1024, epi, [_sds((s, w2.k), BF16)], [_ij(tm, tn)],
                [sv["hb"]], [_ij(tm, tn)])[0]
    g_w2 = mm_tn(f"mlp{i}_dw2", sv["hb"], drb, dw2, s, 512, 1024, a_fn=_sq_relu)
    g_w1 = mm_tn(f"mlp{i}_dw1", sv["xb"], dhb, dw1, s, 1024, 512)
    dx = mm_plain_nt(f"mlp{i}_up_bwd", dhb, w1, F32, tn=1024, add=dr, add_scale=ALPHA)
    return dx, g_w1, g_w2, dg, db


def kernel(x, sc_w_in, sc_conv_w, sc_w_out, mla_w_dq, mla_g_q, mla_w_uq, mla_w_dkv, mla_g_kv, mla_w_uk, mla_w_uv, mla_w_o, cf_w_pw1, cf_b_pw1, cf_dw_w, cf_dw_b, cf_norm_g, cf_norm_b, cf_w_pw2, cf_b_pw2, ff_w1, ff_w2, ln_mix_g, ln_mix_b, ln_ff_g, ln_ff_b, loss_target, m_sc_w_in, m_sc_conv_w, m_sc_w_out, m_mla_w_dq, m_mla_g_q, m_mla_w_uq, m_mla_w_dkv, m_mla_g_kv, m_mla_w_uk, m_mla_w_uv, m_mla_w_o, m_cf_w_pw1, m_cf_b_pw1, m_cf_dw_w, m_cf_dw_b, m_cf_norm_g, m_cf_norm_b, m_cf_w_pw2, m_cf_b_pw2, m_ff_w1, m_ff_w2, m_ln_mix_g, m_ln_mix_b, m_ln_ff_g, m_ln_ff_b, v_sc_w_in, v_sc_conv_w, v_sc_w_out, v_mla_w_dq, v_mla_g_q, v_mla_w_uq, v_mla_w_dkv, v_mla_g_kv, v_mla_w_uk, v_mla_w_uv, v_mla_w_o, v_cf_w_pw1, v_cf_b_pw1, v_cf_dw_w, v_cf_dw_b, v_cf_norm_g, v_cf_norm_b, v_cf_w_pw2, v_cf_b_pw2, v_ff_w1, v_ff_w2, v_ln_mix_g, v_ln_mix_b, v_ln_ff_g, v_ln_ff_b):
    given = dict(zip(ARGS, (x, sc_w_in, sc_conv_w, sc_w_out, mla_w_dq, mla_g_q, mla_w_uq, mla_w_dkv, mla_g_kv, mla_w_uk, mla_w_uv, mla_w_o, cf_w_pw1, cf_b_pw1, cf_dw_w, cf_dw_b, cf_norm_g, cf_norm_b, cf_w_pw2, cf_b_pw2, ff_w1, ff_w2, ln_mix_g, ln_mix_b, ln_ff_g, ln_ff_b, loss_target, m_sc_w_in, m_sc_conv_w, m_sc_w_out, m_mla_w_dq, m_mla_g_q, m_mla_w_uq, m_mla_w_dkv, m_mla_g_kv, m_mla_w_uk, m_mla_w_uv, m_mla_w_o, m_cf_w_pw1, m_cf_b_pw1, m_cf_dw_w, m_cf_dw_b, m_cf_norm_g, m_cf_norm_b, m_cf_w_pw2, m_cf_b_pw2, m_ff_w1, m_ff_w2, m_ln_mix_g, m_ln_mix_b, m_ln_ff_g, m_ln_ff_b, v_sc_w_in, v_sc_conv_w, v_sc_w_out, v_mla_w_dq, v_mla_g_q, v_mla_w_uq, v_mla_w_dkv, v_mla_g_kv, v_mla_w_uk, v_mla_w_uv, v_mla_w_o, v_cf_w_pw1, v_cf_b_pw1, v_cf_dw_w, v_cf_dw_b, v_cf_norm_g, v_cf_norm_b, v_cf_w_pw2, v_cf_b_pw2, v_ff_w1, v_ff_w2, v_ln_mix_g, v_ln_mix_b, v_ln_ff_g, v_ln_ff_b)))
    s, d = x.shape[1], x.shape[2]
    d_ff = 4 * d
    dq4 = d // N_CHIPS
    xq = lax.axis_index("x") * 2 + lax.axis_index("y")

    w_dkv_pad = jnp.pad(mla_w_dkv[0], ((0, 0), (0, 128 - QK_ROPE)))
    w_uq_pad = jnp.pad(mla_w_uq[0].reshape(Q_LORA, 2, QK_NOPE + QK_ROPE), ((0, 0), (0, 0), (0, HEAD_PAD - QK_NOPE - QK_ROPE)))
    small = jnp.concatenate([
        sc_conv_w.reshape(2 * SC_WIDTH, dq4), cf_b_pw1.reshape(2, dq4), cf_dw_w[0], cf_dw_b, cf_norm_g, cf_norm_b,
        cf_b_pw2, jnp.zeros((5, dq4), F32)], axis=0)
    mlp_w = lambda i: [ff_w1[i].astype(BF16), ff_w2[i].astype(BF16)]
    g_in, g_out, g_w1, g_w2 = [None] * 2, [None] * 2, [None] * DEPTH, [None] * DEPTH
    g_in[0], g_out[0], g_small = gather_shards("gather_mixer0", [sc_w_in[0].astype(BF16), sc_w_out[0].astype(BF16), small])
    (g_w1[0],) = gather_shards("gather_up0", [ff_w1[0].astype(BF16)])
    (g_w2[0],) = gather_shards("gather_down0", [ff_w2[0].astype(BF16)])
    g_dqkv, g_uq, g_uk, g_uv, g_o, g_w1[1], g_w2[1] = gather_shards("gather_layer1", [
        jnp.concatenate([mla_w_dq[0], w_dkv_pad], axis=1).astype(BF16),
        w_uq_pad.reshape(Q_LORA, 2 * HEAD_PAD).astype(BF16),
        mla_w_uk.reshape(KV_LORA // N_CHIPS, N_HEADS * QK_NOPE).astype(BF16),
        mla_w_uv.reshape(KV_LORA // N_CHIPS, N_HEADS * V_HEAD).astype(BF16), mla_w_o[0].astype(BF16)] + mlp_w(1))
    g_pw1, g_pw2, g_w1[2], g_w2[2] = gather_shards(
        "gather_layer2", [cf_w_pw1[0].astype(BF16), cf_w_pw2[0].astype(BF16)] + mlp_w(2))
    g_in[1], g_out[1], g_w1[3], g_w2[3] = gather_shards(
        "gather_layer3", [sc_w_in[1].astype(BF16), sc_w_out[1].astype(BF16)] + mlp_w(3))

    wd_t = Q_LORA + KV_LORA + 128
    w_in = [Stk("col", d, 3 * d, g_in[j]) for j in range(2)]
    w_out = [Stk("row", d, d, g_out[j]) for j in range(2)]
    w_dqkv = Stk("row", d, wd_t, g_dqkv)
    w_uq = Stk("col", Q_LORA, N_HEADS * HEAD_PAD, g_uq)
    w_uk = Stk("row", KV_LORA, N_HEADS * QK_NOPE, g_uk)
    w_uv = Stk("row", KV_LORA, N_HEADS * V_HEAD, g_uv)
    w_o = Stk("row", d, d, g_o)
    w_pw1 = Stk("col", d, 2 * d, g_pw1)
    w_pw2 = Stk("row", d, d, g_pw2)
    w_1 = [Stk("col", d, d_ff, g_w1[i]) for i in range(DEPTH)]
    w_2 = [Stk("row", d_ff, d, g_w2[i]) for i in range(DEPTH)]

    def wide(rows):
        return jnp.swapaxes(rows, 0, 1).reshape(rows.shape[1], d)

    conv_w = wide(g_small[:, 0:6]).reshape(2, SC_WIDTH, d)
    b_pw1 = g_small[:, 6:8].reshape(1, 2 * d)
    dw_w = wide(g_small[:, 8:39])
    dw_b, norm_g, norm_b, b_pw2 = (wide(g_small[:, 39 + k:40 + k]) for k in range(4))

    pos = jnp.arange(s, dtype=F32)
    inv_freq = ROPE_THETA ** (-jnp.arange(0, QK_ROPE, 2, dtype=F32) / QK_ROPE)
    ang = pos[:, None] * inv_freq[None, :]
    cos, sin, zero = jnp.cos(ang), jnp.sin(ang), jnp.zeros((s, 128 - QK_ROPE), F32)
    cf = jnp.concatenate([cos, cos, zero], axis=1)
    sf = jnp.concatenate([-sin, sin, zero], axis=1)

    def row(a, i):
        return a[i:i + 1]

    xs = x.reshape(s, d)
    cur = (xs, xs.astype(BF16))
    tape = []
    for i in range(DEPTH):
        mixer, j = i % 3, i // 3
        xf, xb = cur
        lg, lb = row(ln_mix_g, i), row(ln_mix_b, i)
        if mixer == 0:
            u = mm_plain_nn(f"sc{j}_in", xb, w_in[j], F32, tn=3 * dq4)
            gb = short_conv_gate(u, conv_w[j])
            y, yb, xh, rstd = mm_residual_ln(f"sc{j}_out_ln", gb, w_out[j], xf, lg, lb)
            sv = dict(xb=xb, u=u, gb=gb)
        elif mixer == 1:
            t = mm_plain_nn("mla_down", xb, w_dqkv, F32, tn=wd_t // 2)
            cq, ckv, kpe = mla_latents(t, mla_g_q, mla_g_kv, cf, sf)
            qh = mla_queries(cq, w_uq, cf, sf)
            kh = mla_keys(ckv, w_uk, kpe)
            vh = mm_plain_nn("mla_values", ckv, w_uv, BF16, tk=KV_LORA)
            oh = attention(qh, kh, vh)
            y, yb, xh, rstd = mm_residual_ln("mla_out_ln", oh, w_o, xf, lg, lb)
            sv = dict(xb=xb, t=t, cq=cq, ckv=ckv, qh=qh, kh=kh, vh=vh, oh=oh)
        else:
            u = mm_plain_nn("cf_pw1", xb, w_pw1, F32, bias=b_pw1)
            hc = conformer_glu_conv(u, dw_w, dw_b)
            sb = conformer_norm_swish(hc, norm_g, norm_b)
            y, yb, xh, rstd = mm_residual_ln("cf_pw2_ln", sb, w_pw2, xf, lg, lb, bias=b_pw2)
            sv = dict(xb=xb, u=u, hc=hc, sb=sb)
        sv.update(xh=xh, rstd=rstd, g=lg)
        cur, sv_mlp = _mlp_forward(i, y, yb, w_1[i], w_2[i], row(ln_ff_g, i), row(ln_ff_b, i))
        tape.append((sv, sv_mlp))

    dy, loss_part = loss_head(cur[0], loss_target.reshape(s, d))

    grads = {}
    smalls = {}
    g_ln = {n: [None] * DEPTH for n in ("ln_mix_g", "ln_mix_b", "ln_ff_g", "ln_ff_b")}
    conv_grads = [None, None]
    core = lax.axis_index("c").astype(jnp.int32).reshape(1)
    chip = xq.astype(jnp.int32).reshape(1)
    mixer_grads = [["in_0", "out_0"], ["dqkv", "uq", "uk", "uv", "o"], ["pw1", "pw2"], ["in_1", "out_1"]]
    pairs, landed = {}, {}

    def pair_stage(i):
        names = mixer_grads[i] + [f"w1_{i}", f"w2_{i}"]
        return i, names, pair_exchange(f"pair_exchange_layer{i}", [grads[n] for n in names])

    def chip_stage(i, names, theirs):
        sums = [pair_sum(grads[n], th, core) for n, th in zip(names, theirs)]
        pairs.update(zip(names, sums))
        landed.update(zip(names, chip_exchange(f"chip_exchange_layer{i}", sums)))

    in_flight = None
    for i in reversed(range(DEPTH)):
        mixer, j = i % 3, i // 3
        sv, sv_mlp = tape[i]
        dy, grads[f"w1_{i}"], grads[f"w2_{i}"], g_ln["ln_ff_g"][i], g_ln["ln_ff_b"][i] = _mlp_backward(
            i, dy, sv_mlp, w_1[i], w_2[i], Stk("col", d, d_ff), Stk("row", d_ff, d))
        if in_flight is not None:
            chip_stage(*in_flight)
        dr, drb, g_ln["ln_mix_g"][i], g_ln["ln_mix_b"][i], dr_sum = ln_backward(
            f"mix{i}_ln_bwd", dy, sv["xh"], sv["rstd"], sv["g"])
        if mixer == 0:
            dgate = mm_plain_nt(f"sc{j}_out_bwd", drb, w_out[j], F32)
            grads[f"out_{j}"] = mm_tn(f"sc{j}_dw_out", sv["gb"], drb, Stk("row", d, d), s, 512, 1024)
            du, conv_grads[j] = short_conv_gate_bwd(sv["u"], conv_w[j], dgate)
            nb = d // 256
            grads[f"in_{j}"] = mm_tn(
                f"sc{j}_dw_in", sv["xb"], du, Stk("col", d, 3 * d), s, 1024, 256,
                b_spec=pl.BlockSpec((None, s, 256), lambda i_, j_, k_: (j_ // nb, k_, j_ % nb)))
            dy = mm_plain_nt(
                f"sc{j}_in_bwd", du, w_in[j], F32, tk=256, add=dr, add_scale=ALPHA,
                a_spec_fn=(s, lambda tm, tk: pl.BlockSpec((None, tm, tk), lambda i_, j_, k_: (k_ // nb, i_, k_ % nb))))
        elif mixer == 1:
            do = mm_plain_nt("mla_out_bwd", drb, w_o, BF16)
            grads["o"] = mm_tn("mla_dw_o", sv["oh"], drb, Stk("row", d, d), s, 512, 1024)
            dqh, dkh, dvh = attention_bwd(sv["qh"], sv["kh"], sv["vh"], do)
            dql, dkn, dkpe = mla_unrope_grads(dqh, dkh, cf, sf)
            grads["uq"] = mm_tn("mla_dw_uq", sv["cq"], dql, Stk("col", Q_LORA, N_HEADS * HEAD_PAD), s, Q_LORA, 512)
            dcq = mm_plain_nt("mla_uq_bwd", dql, w_uq, F32, tn=Q_LORA)
            grads["uk"] = mm_tn("mla_dw_uk", sv["ckv"], dkn, Stk("row", KV_LORA, N_HEADS * QK_NOPE), s, KV_LORA, 1024)
            grads["uv"] = mm_tn("mla_dw_uv", sv["ckv"], dvh, Stk("row", KV_LORA, N_HEADS * V_HEAD), s, KV_LORA, 1024)
            dckv = mm_plain_nt("mla_uk_bwd", dkn, w_uk, F32, tn=KV_LORA)
            dckv = mm_plain_nt("mla_uv_bwd", dvh, w_uv, F32, tn=KV_LORA, add=dckv)
            dt, smalls["g_q"], smalls["g_kv"] = mla_latents_bwd(sv["t"], mla_g_q, mla_g_kv, cf, sf, dcq, dckv, dkpe)
            grads["dqkv"] = mm_tn("mla_dw_down", sv["xb"], dt, Stk("row", d, wd_t), s, 512, wd_t)
            dy = mm_plain_nt("mla_down_bwd", dt, w_dqkv, F32, tk=wd_t, add=dr, add_scale=ALPHA)
        else:
            dsw = mm_plain_nt("cf_pw2_bwd", drb, w_pw2, F32)
            grads["pw2"] = mm_tn("cf_dw_pw2", sv["sb"], drb, Stk("row", d, d), s, 512, 1024)
            smalls["b_pw2"] = dr_sum
            dhc, smalls["norm_g"], smalls["norm_b"] = conformer_norm_swish_bwd(sv["hc"], norm_g, norm_b, dsw)
            du, smalls["b_pw1"], smalls["dw_w"], smalls["dw_b"] = conformer_glu_conv_bwd(sv["u"], dw_w, dhc)
            nb = d // 512
            grads["pw1"] = mm_tn(
                "cf_dw_pw1", sv["xb"], du, Stk("col", d, 2 * d), s, 1024, 512,
                b_spec=pl.BlockSpec((None, s, 512), lambda i_, j_, k_: (j_ // nb, k_, j_ % nb)))
            dy = mm_plain_nt(
                "cf_pw1_bwd", du, w_pw1, F32, add=dr, add_scale=ALPHA,
                a_spec_fn=(s, lambda tm, tk: pl.BlockSpec((None, tm, tk), lambda i_, j_, k_: (k_ // nb, i_, k_ % nb))))
        in_flight = pair_stage(i)
    chip_stage(*in_flight)
    grad_x = dy.reshape(1, s, d)

    groups = [["in_0", "in_1"], ["out_0", "out_1"], ["dqkv"], ["uq"], ["uk"], ["uv"], ["o"], ["pw1"], ["pw2"],
              [f"w1_{i}" for i in range(DEPTH)], [f"w2_{i}" for i in range(DEPTH)]]
    mine = []
    for members in groups:
        stack = None
        for layer, n in reversed(list(enumerate(members))):
            stack = chip_sum(pairs[n], landed[n], chip, stack, layer, len(members))
        mine.append(stack)
    other = (pair_share("pair_share_mixers", mine[:9]) + pair_share("pair_share_up", mine[9:10])
             + pair_share("pair_share_down", mine[10:]))

    def padded(get):
        dqkv = jnp.concatenate([get("mla_w_dq")[0], jnp.pad(get("mla_w_dkv")[0], ((0, 0), (0, 128 - QK_ROPE)))], axis=1)
        uq = jnp.pad(get("mla_w_uq")[0].reshape(Q_LORA, 2, QK_NOPE + QK_ROPE),
                     ((0, 0), (0, 0), (0, HEAD_PAD - QK_NOPE - QK_ROPE))).reshape(Q_LORA, 2 * HEAD_PAD)
        return [get("sc_w_in"), get("sc_w_out"), dqkv[None], uq[None],
                get("mla_w_uk").reshape(1, KV_LORA // N_CHIPS, d), get("mla_w_uv").reshape(1, KV_LORA // N_CHIPS, d),
                get("mla_w_o"), get("cf_w_pw1"), get("cf_w_pw2"), get("ff_w1"), get("ff_w2")]

    w_l, m_l, v_l = (padded(lambda n, p=p: given[p + n]) for p in ("", "m_", "v_"))
    res = [adamw_joined(w_l[k], m_l[k], v_l[k], mine[k], other[k], core) for k in range(len(groups))]

    def unpadded(k):
        r_in, r_out, r_dqkv, r_uq, r_uk, r_uv, r_o, r_pw1, r_pw2, r_w1, r_w2 = (r[k] for r in res)
        return {
            "sc_w_in": r_in, "sc_w_out": r_out, "mla_w_dq": r_dqkv[:, :, 0:Q_LORA],
            "mla_w_dkv": r_dqkv[:, :, Q_LORA:Q_LORA + KV_LORA + QK_ROPE],
            "mla_w_uq": r_uq.reshape(1, Q_LORA, 2, HEAD_PAD)[:, :, :, 0:QK_NOPE + QK_ROPE].reshape(mla_w_uq.shape),
            "mla_w_uk": r_uk.reshape(mla_w_uk.shape), "mla_w_uv": r_uv.reshape(mla_w_uv.shape),
            "mla_w_o": r_o, "cf_w_pw1": r_pw1, "cf_w_pw2": r_pw2, "ff_w1": r_w1, "ff_w2": r_w2}

    big_g, big_d, big_m, big_v = (unpadded(k) for k in range(4))

    pad_row = lambda a: jnp.pad(a, ((0, 0), (0, d - a.shape[1])))
    small_parts = ([g for n in ("ln_mix_g", "ln_mix_b", "ln_ff_g", "ln_ff_b") for g in g_ln[n]]
                   + [pad_row(smalls["g_q"]), pad_row(smalls["g_kv"]), conv_grads[0], conv_grads[1],
                      smalls["b_pw1"].reshape(2, d), smalls["dw_w"], smalls["dw_b"], smalls["norm_g"], smalls["norm_b"],
                      smalls["b_pw2"], loss_part])
    red = all_reduce_small(small_parts, 64)
    loss = red[61, 0]

    def shard(rows):
        return lax.dynamic_slice_in_dim(rows, xq * dq4, dq4, axis=1)

    gw = {
        **big_g,
        "ln_mix_g": red[0:4], "ln_mix_b": red[4:8], "ln_ff_g": red[8:12], "ln_ff_b": red[12:16],
        "mla_g_q": red[16:17, 0:Q_LORA], "mla_g_kv": red[17:18, 0:KV_LORA],
        "sc_conv_w": shard(red[18:24]).reshape(2, SC_WIDTH, dq4),
        "cf_b_pw1": lax.dynamic_slice_in_dim(red[24:26].reshape(1, 2 * d), xq * 2 * dq4, 2 * dq4, axis=1),
        "cf_dw_w": shard(red[26:57])[None], "cf_dw_b": shard(red[57:58]), "cf_norm_g": shard(red[58:59]),
        "cf_norm_b": shard(red[59:60]), "cf_b_pw2": shard(red[60:61]),
    }

    upd = {n: [big_d[n], big_m[n], big_v[n]] for n in big_g}

    def pack(names, width, get):
        return jnp.concatenate([get(n).reshape(-1, width) for n in names], axis=0)

    def unpack(names, packed):
        out, at = {}, 0
        for n in names:
            rows = given[n].size // packed.shape[1]
            out[n] = packed[at:at + rows].reshape(given[n].shape)
            at += rows
        return out

    rep = ["ln_mix_g", "ln_mix_b", "ln_ff_g", "ln_ff_b"]
    shd = ["sc_conv_w", "cf_b_pw1", "cf_dw_w", "cf_dw_b", "cf_norm_g", "cf_norm_b", "cf_b_pw2"]
    for names, width in ((rep, d), (shd, dq4), (["mla_g_q"], Q_LORA), (["mla_g_kv"], KV_LORA)):
        res = adamw(pack(names, width, lambda n: given[n]), pack(names, width, lambda n: gw[n]),
                    pack(names, width, lambda n: given["m_" + n]), pack(names, width, lambda n: given["v_" + n]), tm=4096)
        parts = [unpack(names, r) for r in res]
        for n in names:
            upd[n] = [p[n] for p in parts]

    return (loss, grad_x, *[gw[n].reshape(given[n].shape) for n in WEIGHTS], *[upd[n][0] for n in WEIGHTS],
            *[upd[n][1] for n in WEIGHTS], *[upd[n][2] for n in WEIGHTS])
```

```python
import jax
import jax.numpy as jnp
from jax import lax
from jax.experimental import pallas as pl
from jax.experimental.pallas import tpu as pltpu
from jax.experimental.pallas import tpu_sc as plsc

F32 = jnp.float32
BF16 = jnp.bfloat16
MESH = pl.DeviceIdType.MESH

DEPTH = 4
ALPHA = (2.0 * DEPTH) ** 0.25
LN_EPS = 1e-5
RMS_EPS = 1e-6
CHUNK_SHIFT = 6
N_HEADS = 8
QK_NOPE = 128
QK_ROPE = 64
V_HEAD = 128
HEAD_PAD = 256
Q_LORA = 384
KV_LORA = 256
ROPE_THETA = 10000.0
SC_WIDTH = 3
CONF_WIDTH = 31
CONV_PAD = 32
CONV_CHUNK = 64
N_CHIPS = 4
ATTN_SCALE = (QK_NOPE + QK_ROPE) ** -0.5

ADAM_LR = 0.001
ADAM_B1 = 0.9
ADAM_B2 = 0.999
ADAM_EPS = 1e-08
ADAM_WD = 0.01
ADAM_STEP = 10

VMEM_LIMIT = 56 * 2**20

NN = (((1,), (0,)), ((), ()))
NT = (((1,), (1,)), ((), ()))
TN = (((0,), (0,)), ((), ()))


def _params(sem=None):
    return pltpu.CompilerParams(dimension_semantics=sem, vmem_limit_bytes=VMEM_LIMIT)


class Stk:
    def __init__(self, kind, k, n, arr=None, layers=None, layer=None):
        self.kind, self.k, self.n, self.layers, self.layer = kind, k, n, layers, layer
        self.plain = kind == "row" and layers is None
        self.kloc = k // N_CHIPS if kind == "row" else k
        self.nloc = n // N_CHIPS if kind == "col" else n
        if arr is not None and self.plain:
            arr = arr.reshape(k, n)
        self.arr = arr

    @property
    def shape(self):
        if self.plain:
            return (self.k, self.n)
        lead = (N_CHIPS,) if self.layers is None else (N_CHIPS, self.layers)
        return lead + (self.kloc, self.nloc)

    def spec(self, bk, bn, f):
        if self.plain:
            return pl.BlockSpec((bk, bn), f)
        assert self.kloc % bk == 0 and self.nloc % bn == 0, (self.kloc, bk, self.nloc, bn)
        pk, pn = self.kloc // bk, self.nloc // bn
        kind, layer = self.kind, self.layer

        def imap(*g):
            kb, nb = f(*g)
            if kind == "row":
                q, kb, nb = kb // pk, kb % pk, nb
            else:
                q, kb, nb = nb // pn, kb, nb % pn
            return (q, kb, nb) if layer is None else (q, layer, kb, nb)

        block = (None, bk, bn) if layer is None else (None, None, bk, bn)
        return pl.BlockSpec(block, imap)


def _mm(name, mode, a, b, grid, a_spec, b_spec, acc_shape, extras, extra_specs, out_shapes, out_specs, epi, a_fn=None):
    nk = grid[2]
    ne = len(extras)

    def body(*refs):
        a_ref, b_ref = refs[0], refs[1]
        e_refs = refs[2:2 + ne]
        av = a_ref[...] if a_fn is None else a_fn(a_ref[...])
        part = lax.dot_general(av, b_ref[...], mode, preferred_element_type=F32)
        if nk == 1:
            epi(part, e_refs, refs[2 + ne:])
            return
        o_refs = refs[2 + ne:-1]
        acc = refs[-1]
        k = pl.program_id(2)

        @pl.when(k == 0)
        def _():
            acc[...] = part

        @pl.when(k > 0)
        def _():
            acc[...] += part

        @pl.when(k == nk - 1)
        def _():
            epi(acc[...], e_refs, o_refs)

    return pl.pallas_call(
        body, grid=grid, in_specs=[a_spec, b_spec, *extra_specs], out_specs=out_specs, out_shape=out_shapes,
        scratch_shapes=[pltpu.VMEM(acc_shape, F32)] if nk > 1 else [],
        compiler_params=_params(("parallel", "parallel", "arbitrary")), name=name)(a, b, *extras)


def _tile(n, t):
    t = min(n, t)
    while n % t:
        t -= 8
    assert t > 0, (n, t)
    return t


def mm_nn(name, a, w, tm, tn, tk, epi, out_shapes, out_specs, extras=(), extra_specs=(), a_spec=None, a_fn=None):
    m = a.shape[0]
    tm, tn, tk = _tile(m, tm), _tile(w.n, tn), _tile(w.k, tk)
    grid = (m // tm, w.n // tn, w.k // tk)
    a_spec = a_spec or pl.BlockSpec((tm, tk), lambda i, j, k: (i, k))
    b_spec = w.spec(tk, tn, lambda i, j, k: (k, j))
    return _mm(name, NN, a, w.arr, grid, a_spec, b_spec, (tm, tn), extras, extra_specs, out_shapes, out_specs, epi, a_fn)


def mm_nt(name, a, w, m, tm, tn, tk, epi, out_shapes, out_specs, extras=(), extra_specs=(), a_spec=None):
    tm, tn, tk = _tile(m, tm), _tile(w.k, tn), _tile(w.n, tk)
    grid = (m // tm, w.k // tn, w.n // tk)
    a_spec = a_spec or pl.BlockSpec((tm, tk), lambda i, j, k: (i, k))
    b_spec = w.spec(tn, tk, lambda i, j, k: (j, k))
    return _mm(name, NT, a, w.arr, grid, a_spec, b_spec, (tm, tn), extras, extra_specs, out_shapes, out_specs, epi)


def mm_tn(name, a, b, dw, s, tm=512, tn=512, tk=4096, a_spec=None, b_spec=None, a_fn=None):
    tm, tn, tk = _tile(dw.k, tm), _tile(dw.n, tn), _tile(s, tk)
    grid = (dw.k // tm, dw.n // tn, s // tk)
    a_spec = a_spec or pl.BlockSpec((tk, tm), lambda i, j, k: (k, i))
    b_spec = b_spec or pl.BlockSpec((tk, tn), lambda i, j, k: (k, j))

    def epi(acc, e, o):
        o[0][...] = acc.astype(BF16)

    out = _mm(name, TN, a, b, grid, a_spec, b_spec, (tm, tn), (), (), [jax.ShapeDtypeStruct(dw.shape, BF16)],
              [dw.spec(tm, tn, lambda i, j, k: (i, j))], epi, a_fn)[0]
    return out.reshape(N_CHIPS, dw.k // N_CHIPS, dw.n) if dw.plain else out


def _sds(shape, dtype):
    return jax.ShapeDtypeStruct(shape, dtype)


def _ij(tm, tn):
    return pl.BlockSpec((tm, tn), lambda i, j, k: (i, j))


def _i0(tm, c):
    return pl.BlockSpec((tm, c), lambda i, j, k: (i, 0))


def _0j(r, tn):
    return pl.BlockSpec((r, tn), lambda i, j, k: (0, j))


def _layer_norm_rows(r, g, b):
    mu = jnp.mean(r, axis=-1, keepdims=True)
    d = r - mu
    var = jnp.mean(d * d, axis=-1, keepdims=True)
    rstd = lax.rsqrt(var + LN_EPS)
    xh = d * rstd
    return xh * g + b, xh, rstd


def mm_residual_ln(name, a, w, x, g, b, bias=None, tm=512, tk=1024, a_fn=None):
    s, d = x.shape
    tm = _tile(s, tm)
    extras = [x, g, b] + ([bias] if bias is not None else [])
    especs = [_i0(tm, d), _0j(1, d), _0j(1, d)] + ([_0j(1, d)] if bias is not None else [])

    def epi(acc, e, o):
        r = ALPHA * e[0][...] + acc
        if bias is not None:
            r = r + e[3][...]
        y, xh, rstd = _layer_norm_rows(r, e[1][...], e[2][...])
        o[0][...] = y
        o[1][...] = y.astype(BF16)
        o[2][...] = xh
        o[3][...] = rstd

    return mm_nn(name, a, w, tm, d, tk, epi,
                 [_sds((s, d), F32), _sds((s, d), BF16), _sds((s, d), F32), _sds((s, 1), F32)],
                 [_i0(tm, d), _i0(tm, d), _i0(tm, d), _i0(tm, 1)], extras, especs, a_fn=a_fn)


def mm_plain_nn(name, a, w, out_dtype, tm=1024, tn=512, tk=1024, bias=None):
    m = a.shape[0]
    tm, tn = _tile(m, tm), _tile(w.n, tn)
    if w.kind == "col":
        tn = _tile(w.nloc, tn)

    def epi(acc, e, o):
        if bias is not None:
            acc = acc + e[0][...]
        o[0][...] = acc.astype(out_dtype)

    extras, especs = ([bias], [_0j(1, tn)]) if bias is not None else ((), ())
    return mm_nn(name, a, w, tm, tn, tk, epi, [_sds((m, w.n), out_dtype)], [_ij(tm, tn)], extras, especs)[0]


def mm_plain_nt(name, a, w, out_dtype, tm=1024, tn=512, tk=1024, add=None, add_scale=1.0, a_spec_fn=None):
    m = a.shape[0] if a_spec_fn is None else a_spec_fn[0]
    tm, tn = _tile(m, tm), _tile(w.k, tn)
    tk = _tile(w.n, tk)
    if w.kind == "col":
        tk = _tile(w.nloc, tk)
    if w.kind == "row" and not w.plain:
        tn = _tile(w.kloc, tn)

    def epi(acc, e, o):
        if add is not None:
            acc = acc + add_scale * e[0][...].astype(F32)
        o[0][...] = acc.astype(out_dtype)

    extras, especs = ([add], [_ij(tm, tn)]) if add is not None else ((), ())
    a_spec = None if a_spec_fn is None else a_spec_fn[1](tm, tk)
    return mm_nt(name, a, w, m, tm, tn, tk, epi, [_sds((m, w.k), out_dtype)], [_ij(tm, tn)], extras, especs,
                 a_spec=a_spec)[0]


def _rows(tm, c):
    return pl.BlockSpec((tm, c), lambda i: (i, 0))


def _fix(shape):
    nd = len(shape)
    return pl.BlockSpec(shape, lambda i: (0,) * nd)


def _accumulate(ref, val):
    @pl.when(pl.program_id(0) == 0)
    def _():
        ref[...] = jnp.zeros_like(ref)

    ref[...] += val


def ln_backward(name, dy, xhat, rstd, g, tm=256):
    s, d = dy.shape
    tm = _tile(s, tm)

    def body(dy_ref, xh_ref, rstd_ref, g_ref, dr_ref, drb_ref, dg_ref, db_ref, ds_ref):
        dyv, xh = dy_ref[...], xh_ref[...]
        dxh = dyv * g_ref[...]
        m1 = jnp.mean(dxh, axis=-1, keepdims=True)
        m2 = jnp.mean(dxh * xh, axis=-1, keepdims=True)
        dr = rstd_ref[...] * (dxh - m1 - xh * m2)
        dr_ref[...] = dr
        drb_ref[...] = dr.astype(BF16)
        _accumulate(dg_ref, jnp.sum(dyv * xh, axis=0, keepdims=True))
        _accumulate(db_ref, jnp.sum(dyv, axis=0, keepdims=True))
        _accumulate(ds_ref, jnp.sum(dr, axis=0, keepdims=True))

    return pl.pallas_call(
        body, grid=(s // tm,),
        in_specs=[_rows(tm, d), _rows(tm, d), _rows(tm, 1), _fix((1, d))],
        out_specs=[_rows(tm, d), _rows(tm, d), _fix((1, d)), _fix((1, d)), _fix((1, d))],
        out_shape=[_sds((s, d), F32), _sds((s, d), BF16), _sds((1, d), F32), _sds((1, d), F32), _sds((1, d), F32)],
        compiler_params=_params(("arbitrary",)), name=name)(dy, xhat, rstd, g)


def loss_head(y, target, tm=256):
    s, d = y.shape
    tm = _tile(s, tm)

    def body(y_ref, t_ref, dy_ref, loss_ref):
        e = y_ref[...] - t_ref[...]
        dy_ref[...] = e * (1.0 / d)
        part = 0.5 * jnp.sum(jnp.mean(e * e, axis=-1, keepdims=True), axis=0, keepdims=True)
        _accumulate(loss_ref, jnp.broadcast_to(part, (1, d)))

    return pl.pallas_call(
        body, grid=(s // tm,), in_specs=[_rows(tm, d), _rows(tm, d)],
        out_specs=[_rows(tm, d), _fix((1, d))], out_shape=[_sds((s, d), F32), _sds((1, d), F32)],
        compiler_params=_params(("arbitrary",)), name="loss_head")(y, target)


def _cols(s, tc, off=0):
    return pl.BlockSpec((s, tc), lambda i: (0, i + off))


def _shift_down(z, sft, rows):
    return jnp.where(rows >= sft, pltpu.roll(z, sft, 0), 0.0)


def _shift_up(z, sft, rows, s):
    return jnp.where(rows < s - sft, pltpu.roll(z, (s - sft) % s, 0), 0.0)


def short_conv_gate(u, conv_w, tc=256):
    s, d3 = u.shape
    d = d3 // 3
    nb = d // tc

    def body(b_ref, c_ref, h_ref, w_ref, o_ref):
        rows = lax.broadcasted_iota(jnp.int32, (s, tc), 0)
        z = c_ref[...] * h_ref[...]
        cz = jnp.zeros((s, tc), F32)
        for k in range(SC_WIDTH):
            sft = SC_WIDTH - 1 - k
            cz = cz + w_ref[pl.ds(k, 1), :] * (_shift_down(z, sft, rows) if sft else z)
        o_ref[...] = (b_ref[...] * cz).astype(BF16)

    return pl.pallas_call(
        body, grid=(nb,),
        in_specs=[_cols(s, tc), _cols(s, tc, nb), _cols(s, tc, 2 * nb), _cols(SC_WIDTH, tc)],
        out_specs=_cols(s, tc), out_shape=_sds((s, d), BF16),
        compiler_params=_params(("parallel",)), name="short_conv_gate")(u, u, u, conv_w)


def short_conv_gate_bwd(u, conv_w, dg, tc=256):
    s, d3 = u.shape
    d = d3 // 3
    nb = d // tc

    def body(b_ref, c_ref, h_ref, w_ref, dg_ref, du_ref, dw_ref):
        rows = lax.broadcasted_iota(jnp.int32, (s, tc), 0)
        c, h, dgv = c_ref[...], h_ref[...], dg_ref[...]
        z = c * h
        dcz = dgv * b_ref[...]
        cz = jnp.zeros((s, tc), F32)
        dz = jnp.zeros((s, tc), F32)
        for k in range(SC_WIDTH):
            sft = SC_WIDTH - 1 - k
            zs = _shift_down(z, sft, rows) if sft else z
            wk = w_ref[pl.ds(k, 1), :]
            cz = cz + wk * zs
            dz = dz + wk * (_shift_up(dcz, sft, rows, s) if sft else dcz)
            dw_ref[pl.ds(k, 1), :] = jnp.sum(dcz * zs, axis=0, keepdims=True)
        du_ref[0] = (dgv * cz).astype(BF16)
        du_ref[1] = (dz * h).astype(BF16)
        du_ref[2] = (dz * c).astype(BF16)

    return pl.pallas_call(
        body, grid=(nb,),
        in_specs=[_cols(s, tc), _cols(s, tc, nb), _cols(s, tc, 2 * nb), _cols(SC_WIDTH, tc), _cols(s, tc)],
        out_specs=[pl.BlockSpec((3, s, tc), lambda i: (0, 0, i)), _cols(SC_WIDTH, tc)],
        out_shape=[_sds((3, s, d), BF16), _sds((SC_WIDTH, d), F32)],
        compiler_params=_params(("parallel",)), name="short_conv_gate_bwd")(u, u, u, conv_w, dg)


def _store_shifted_down(ref, z, rows):
    s, tc = z.shape
    for b in range(8):
        ref[b, pl.ds(0, CONV_PAD), :] = jnp.zeros((CONV_PAD, tc), F32)
        ref[b, pl.ds(CONV_PAD, s), :] = z if b == 0 else _shift_down(z, b, rows)


def _store_shifted_up(ref, z, rows):
    s, tc = z.shape
    for b in range(8):
        ref[b, pl.ds(0, s), :] = z if b == 0 else _shift_up(z, b, rows, s)
        ref[b, pl.ds(s, CONV_PAD), :] = jnp.zeros((CONV_PAD, tc), F32)


def conformer_glu_conv(u, dw_w, dw_b, tc=128):
    s, d2 = u.shape
    d = d2 // 2
    nb = d // tc

    ch = min(CONV_CHUNK, s)

    def body(a_ref, g_ref, w_ref, b_ref, o_ref, down):
        rows = lax.broadcasted_iota(jnp.int32, (s, tc), 0)
        _store_shifted_down(down, a_ref[...] * jax.nn.sigmoid(g_ref[...]), rows)

        def chunk(ci, carry):
            r0 = pl.multiple_of(ci * ch, ch)
            acc = jnp.broadcast_to(b_ref[...], (ch, tc))
            for k in range(CONF_WIDTH):
                sft = CONF_WIDTH - 1 - k
                acc = acc + w_ref[pl.ds(k, 1), :] * down[sft % 8, pl.ds(CONV_PAD + r0 - (sft // 8) * 8, ch), :]
            o_ref[pl.ds(r0, ch), :] = acc
            return carry

        lax.fori_loop(0, s // ch, chunk, 0)

    return pl.pallas_call(
        body, grid=(nb,),
        in_specs=[_cols(s, tc), _cols(s, tc, nb), _cols(CONF_WIDTH, tc), _cols(1, tc)],
        out_specs=_cols(s, tc), out_shape=_sds((s, d), F32),
        scratch_shapes=[pltpu.VMEM((8, CONV_PAD + s, tc), F32)],
        compiler_params=_params(("parallel",)), name="conformer_glu_conv")(u, u, dw_w, dw_b)


def conformer_glu_conv_bwd(u, dw_w, dhc, tc=128):
    s, d2 = u.shape
    d = d2 // 2
    nb = d // tc
    ch = min(CONV_CHUNK, s)

    def body(a_ref, g_ref, w_ref, dhc_ref, du_ref, dbias_ref, dw_ref, db_ref, down, up, dw_acc, dh_buf):
        rows = lax.broadcasted_iota(jnp.int32, (s, tc), 0)
        a = a_ref[...]
        sg = jax.nn.sigmoid(g_ref[...])
        dhcv = dhc_ref[...]
        _store_shifted_down(down, a * sg, rows)
        _store_shifted_up(up, dhcv, rows)
        dw_acc[...] = jnp.zeros_like(dw_acc)

        def chunk(ci, carry):
            r0 = pl.multiple_of(ci * ch, ch)
            dc = dhc_ref[pl.ds(r0, ch), :]
            dh = jnp.zeros((ch, tc), F32)
            for k in range(CONF_WIDTH):
                sft = CONF_WIDTH - 1 - k
                a8, b = (sft // 8) * 8, sft % 8
                dh = dh + w_ref[pl.ds(k, 1), :] * up[b, pl.ds(r0 + a8, ch), :]
                prod = dc * down[b, pl.ds(CONV_PAD + r0 - a8, ch), :]
                dw_acc[k] += jnp.sum(prod.reshape(ch // 8, 8, tc), axis=0)
            dh_buf[pl.ds(r0, ch), :] = dh
            return carry

        lax.fori_loop(0, s // ch, chunk, 0)
        dh = dh_buf[...]
        da = dh * sg
        dgate = dh * a * sg * (1.0 - sg)
        du_ref[0] = da.astype(BF16)
        du_ref[1] = dgate.astype(BF16)
        dbias_ref[pl.ds(0, 1), :] = jnp.sum(da, axis=0, keepdims=True)
        dbias_ref[pl.ds(1, 1), :] = jnp.sum(dgate, axis=0, keepdims=True)
        db_ref[...] = jnp.sum(dhcv, axis=0, keepdims=True)
        for k in range(CONF_WIDTH):
            dw_ref[pl.ds(k, 1), :] = jnp.sum(dw_acc[k], axis=0, keepdims=True)

    return pl.pallas_call(
        body, grid=(nb,),
        in_specs=[_cols(s, tc), _cols(s, tc, nb), _cols(CONF_WIDTH, tc), _cols(s, tc)],
        out_specs=[pl.BlockSpec((2, s, tc), lambda i: (0, 0, i)), _cols(2, tc), _cols(CONF_WIDTH, tc), _cols(1, tc)],
        out_shape=[_sds((2, s, d), BF16), _sds((2, d), F32), _sds((CONF_WIDTH, d), F32), _sds((1, d), F32)],
        scratch_shapes=[pltpu.VMEM((8, CONV_PAD + s, tc), F32), pltpu.VMEM((8, CONV_PAD + s, tc), F32),
                        pltpu.VMEM((CONF_WIDTH + 1, 8, tc), F32), pltpu.VMEM((s, tc), F32)],
        compiler_params=_params(("parallel",)), name="conformer_glu_conv_bwd")(u, u, dw_w, dhc)


def conformer_norm_swish(hc, g, b, tm=256):
    s, d = hc.shape
    tm = _tile(s, tm)

    def body(h_ref, g_ref, b_ref, o_ref):
        n, _, _ = _layer_norm_rows(h_ref[...], g_ref[...], b_ref[...])
        o_ref[...] = (n * jax.nn.sigmoid(n)).astype(BF16)

    return pl.pallas_call(
        body, grid=(s // tm,), in_specs=[_rows(tm, d), _fix((1, d)), _fix((1, d))], out_specs=_rows(tm, d),
        out_shape=_sds((s, d), BF16), compiler_params=_params(("parallel",)), name="conformer_norm_swish")(hc, g, b)


def conformer_norm_swish_bwd(hc, g, b, ds, tm=256):
    s, d = hc.shape
    tm = _tile(s, tm)

    def body(h_ref, g_ref, b_ref, ds_ref, dh_ref, dg_ref, db_ref):
        n, nh, rstd = _layer_norm_rows(h_ref[...], g_ref[...], b_ref[...])
        sg = jax.nn.sigmoid(n)
        dn = ds_ref[...] * (sg * (1.0 + n * (1.0 - sg)))
        dnh = dn * g_ref[...]
        m1 = jnp.mean(dnh, axis=-1, keepdims=True)
        m2 = jnp.mean(dnh * nh, axis=-1, keepdims=True)
        dh_ref[...] = rstd * (dnh - m1 - nh * m2)
        _accumulate(dg_ref, jnp.sum(dn * nh, axis=0, keepdims=True))
        _accumulate(db_ref, jnp.sum(dn, axis=0, keepdims=True))

    return pl.pallas_call(
        body, grid=(s // tm,), in_specs=[_rows(tm, d), _fix((1, d)), _fix((1, d)), _rows(tm, d)],
        out_specs=[_rows(tm, d), _fix((1, d)), _fix((1, d))],
        out_shape=[_sds((s, d), F32), _sds((1, d), F32), _sds((1, d), F32)],
        compiler_params=_params(("arbitrary",)), name="conformer_norm_swish_bwd")(hc, g, b, ds)


def _swap_halves(x):
    lane = lax.broadcasted_iota(jnp.int32, x.shape, 1)
    return jnp.where(lane < QK_ROPE // 2, pltpu.roll(x, 128 - QK_ROPE // 2, 1), pltpu.roll(x, QK_ROPE // 2, 1))


def _rope(x, cf, sf):
    return x * cf + _swap_halves(x) * sf


def _unrope(dx, cf, sf):
    return dx * cf - _swap_halves(dx) * sf


def _rms_rows(x, g):
    r = lax.rsqrt(jnp.mean(x * x, axis=-1, keepdims=True) + RMS_EPS)
    return x * r, r


def mla_latents(t, g_q, g_kv, cf, sf, tm=256):
    s = t.shape[0]
    tm = _tile(s, tm)

    def body(t_ref, gq_ref, gkv_ref, cf_ref, sf_ref, cq_ref, ckv_ref, kpe_ref):
        xq, _ = _rms_rows(t_ref[:, 0:Q_LORA], gq_ref[...])
        cq_ref[...] = (xq * gq_ref[...]).astype(BF16)
        xkv, _ = _rms_rows(t_ref[:, Q_LORA:Q_LORA + KV_LORA], gkv_ref[...])
        ckv_ref[...] = (xkv * gkv_ref[...]).astype(BF16)
        kpe_ref[...] = _rope(t_ref[:, Q_LORA + KV_LORA:], cf_ref[...], sf_ref[...]).astype(BF16)

    w = Q_LORA + KV_LORA + 128
    return pl.pallas_call(
        body, grid=(s // tm,),
        in_specs=[_rows(tm, w), _fix((1, Q_LORA)), _fix((1, KV_LORA)), _rows(tm, 128), _rows(tm, 128)],
        out_specs=[_rows(tm, Q_LORA), _rows(tm, KV_LORA), _rows(tm, 128)],
        out_shape=[_sds((s, Q_LORA), BF16), _sds((s, KV_LORA), BF16), _sds((s, 128), BF16)],
        compiler_params=_params(("parallel",)), name="mla_latents")(t, g_q, g_kv, cf, sf)


def mla_latents_bwd(t, g_q, g_kv, cf, sf, dcq, dckv, dkpe, tm=256):
    s = t.shape[0]
    tm = _tile(s, tm)
    w = Q_LORA + KV_LORA + 128

    def rms_bwd(x, g, dy):
        xh, r = _rms_rows(x, g)
        dxh = dy * g
        return r * (dxh - xh * jnp.mean(dxh * xh, axis=-1, keepdims=True)), jnp.sum(dy * xh, axis=0, keepdims=True)

    def body(t_ref, gq_ref, gkv_ref, cf_ref, sf_ref, dcq_ref, dckv_ref, dkpe_ref, dt_ref, dgq_ref, dgkv_ref):
        dxq, dgq = rms_bwd(t_ref[:, 0:Q_LORA], gq_ref[...], dcq_ref[...])
        dxkv, dgkv = rms_bwd(t_ref[:, Q_LORA:Q_LORA + KV_LORA], gkv_ref[...], dckv_ref[...])
        dt_ref[:, 0:Q_LORA] = dxq.astype(BF16)
        dt_ref[:, Q_LORA:Q_LORA + KV_LORA] = dxkv.astype(BF16)
        dt_ref[:, Q_LORA + KV_LORA:] = _unrope(dkpe_ref[...], cf_ref[...], sf_ref[...]).astype(BF16)
        _accumulate(dgq_ref, dgq)
        _accumulate(dgkv_ref, dgkv)

    return pl.pallas_call(
        body, grid=(s // tm,),
        in_specs=[_rows(tm, w), _fix((1, Q_LORA)), _fix((1, KV_LORA)), _rows(tm, 128), _rows(tm, 128),
                  _rows(tm, Q_LORA), _rows(tm, KV_LORA), _rows(tm, 128)],
        out_specs=[_rows(tm, w), _fix((1, Q_LORA)), _fix((1, KV_LORA))],
        out_shape=[_sds((s, w), BF16), _sds((1, Q_LORA), F32), _sds((1, KV_LORA), F32)],
        compiler_params=_params(("arbitrary",)), name="mla_latents_bwd")(t, g_q, g_kv, cf, sf, dcq, dckv, dkpe)


def mla_queries(cq, w_uq, cf, sf, tm=512):
    s = cq.shape[0]
    tm = _tile(s, tm)

    def epi(acc, e, o):
        o[0][:, 0:QK_NOPE] = acc[:, 0:QK_NOPE].astype(BF16)
        o[0][:, QK_NOPE:] = _rope(acc[:, QK_NOPE:], e[0][...], e[1][...]).astype(BF16)

    return mm_nn("mla_queries", cq, w_uq, tm, HEAD_PAD, Q_LORA, epi, [_sds((s, N_HEADS * HEAD_PAD), BF16)],
                 [_ij(tm, HEAD_PAD)], [cf, sf], [_i0(tm, 128), _i0(tm, 128)])[0]


def mla_keys(ckv, w_uk, kpe, tm=512):
    s = ckv.shape[0]
    tm = _tile(s, tm)

    def epi(acc, e, o):
        o[0][:, 0:QK_NOPE] = acc.astype(BF16)
        o[0][:, QK_NOPE:] = e[0][...]

    return mm_nn("mla_keys", ckv, w_uk, tm, QK_NOPE, KV_LORA, epi, [_sds((s, N_HEADS * HEAD_PAD), BF16)],
                 [_ij(tm, HEAD_PAD)], [kpe], [_i0(tm, 128)])[0]


def _masked_scores(q, k, qi, tq, kv):
    sc = lax.dot_general(q, k, NT, preferred_element_type=F32) * ATTN_SCALE
    row = lax.broadcasted_iota(jnp.int32, (tq, kv), 0) + qi * tq
    col = lax.broadcasted_iota(jnp.int32, (tq, kv), 1)
    ok = lax.shift_right_logical(col, CHUNK_SHIFT) <= lax.shift_right_logical(row, CHUNK_SHIFT)
    return jnp.where(ok, sc, -1e30)


def attention(q, k, v, tq=256):
    s = q.shape[0]
    tq = _tile(s, tq)
    nq = s // tq

    def body(q_ref, k_ref, v_ref, o_ref):
        for qi in range(nq):
            kv = (qi + 1) * tq
            sc = _masked_scores(q_ref[pl.ds(qi * tq, tq), :], k_ref[pl.ds(0, kv), :], qi, tq, kv)
            p = jnp.exp(sc - jnp.max(sc, axis=-1, keepdims=True))
            o = lax.dot_general(p.astype(BF16), v_ref[pl.ds(0, kv), :], NN, preferred_element_type=F32)
            o_ref[pl.ds(qi * tq, tq), :] = (o / jnp.sum(p, axis=-1, keepdims=True)).astype(BF16)

    hq = pl.BlockSpec((s, HEAD_PAD), lambda h: (0, h))
    hv = pl.BlockSpec((s, V_HEAD), lambda h: (0, h))
    return pl.pallas_call(
        body, grid=(N_HEADS,), in_specs=[hq, hq, hv], out_specs=hv, out_shape=_sds((s, N_HEADS * V_HEAD), BF16),
        compiler_params=_params(("parallel",)), name="attention")(q, k, v)


def attention_bwd(q, k, v, do, tq=256):
    s = q.shape[0]
    tq = _tile(s, tq)
    nq = s // tq

    def body(q_ref, k_ref, v_ref, do_ref, dq_ref, dk_ref, dv_ref, dk_acc, dv_acc):
        dk_acc[...] = jnp.zeros_like(dk_acc)
        dv_acc[...] = jnp.zeros_like(dv_acc)
        for qi in range(nq):
            kv = (qi + 1) * tq
            qt = q_ref[pl.ds(qi * tq, tq), :]
            kt = k_ref[pl.ds(0, kv), :]
            dot = do_ref[pl.ds(qi * tq, tq), :]
            sc = _masked_scores(qt, kt, qi, tq, kv)
            p = jnp.exp(sc - jnp.max(sc, axis=-1, keepdims=True))
            p = p / jnp.sum(p, axis=-1, keepdims=True)
            dp = lax.dot_general(dot, v_ref[pl.ds(0, kv), :], NT, preferred_element_type=F32)
            delta = jnp.sum(p * dp, axis=-1, keepdims=True)
            ds = (p * (dp - delta) * ATTN_SCALE).astype(BF16)
            dq_ref[pl.ds(qi * tq, tq), :] = lax.dot_general(ds, kt, NN, preferred_element_type=F32).astype(BF16)
            dk_acc[pl.ds(0, kv), :] += lax.dot_general(ds, qt, TN, preferred_element_type=F32)
            dv_acc[pl.ds(0, kv), :] += lax.dot_general(p.astype(BF16), dot, TN, preferred_element_type=F32)
        dk_ref[...] = dk_acc[...].astype(BF16)
        dv_ref[...] = dv_acc[...].astype(BF16)

    hq = pl.BlockSpec((s, HEAD_PAD), lambda h: (0, h))
    hv = pl.BlockSpec((s, V_HEAD), lambda h: (0, h))
    return pl.pallas_call(
        body, grid=(N_HEADS,), in_specs=[hq, hq, hv, hv], out_specs=[hq, hq, hv],
        out_shape=[_sds((s, N_HEADS * HEAD_PAD), BF16), _sds((s, N_HEADS * HEAD_PAD), BF16),
                   _sds((s, N_HEADS * V_HEAD), BF16)],
        scratch_shapes=[pltpu.VMEM((s, HEAD_PAD), F32), pltpu.VMEM((s, V_HEAD), F32)],
        compiler_params=_params(("parallel",)), name="attention_bwd")(q, k, v, do)


def mla_unrope_grads(dq, dk, cf, sf, tm=256):
    s = dq.shape[0]
    tm = _tile(s, tm)

    def body(dq_ref, dk_ref, cf_ref, sf_ref, dql_ref, dkn_ref, dkpe_ref):
        cfv, sfv = cf_ref[...], sf_ref[...]
        dkpe = jnp.zeros((tm, 128), F32)
        for h in range(N_HEADS):
            lo = h * HEAD_PAD
            dql_ref[:, lo:lo + QK_NOPE] = dq_ref[:, lo:lo + QK_NOPE]
            dql_ref[:, lo + QK_NOPE:lo + HEAD_PAD] = _unrope(
                dq_ref[:, lo + QK_NOPE:lo + HEAD_PAD].astype(F32), cfv, sfv).astype(BF16)
            dkn_ref[:, h * QK_NOPE:(h + 1) * QK_NOPE] = dk_ref[:, lo:lo + QK_NOPE]
            dkpe = dkpe + dk_ref[:, lo + QK_NOPE:lo + HEAD_PAD].astype(F32)
        dkpe_ref[...] = dkpe

    wq = N_HEADS * HEAD_PAD
    return pl.pallas_call(
        body, grid=(s // tm,), in_specs=[_rows(tm, wq), _rows(tm, wq), _rows(tm, 128), _rows(tm, 128)],
        out_specs=[_rows(tm, wq), _rows(tm, N_HEADS * QK_NOPE), _rows(tm, 128)],
        out_shape=[_sds((s, wq), BF16), _sds((s, N_HEADS * QK_NOPE), BF16), _sds((s, 128), F32)],
        compiler_params=_params(("parallel",)), name="mla_unrope_grads")(dq, dk, cf, sf)


ANY = pl.BlockSpec(memory_space=pl.ANY)
GATHER_ID = 1
CHIP_EXCHANGE_ID = 2
PAIR_ID = 3
ALL_ID = 4


def _nbytes(a):
    return a.size * a.dtype.itemsize


def _copy_cost(operand_bytes, sent_fraction):
    sent = int(operand_bytes * sent_fraction)
    return pl.CostEstimate(flops=0, transcendentals=0, bytes_accessed=2 * sent, remote_bytes_transferred=sent)


def _handshake(peers):
    barrier = pltpu.get_barrier_semaphore()
    for peer in peers:
        pl.semaphore_signal(barrier, inc=1, device_id=peer, device_id_type=MESH)
    pl.semaphore_wait(barrier, len(peers))


def _place():
    x, y, c = lax.axis_index("x"), lax.axis_index("y"), lax.axis_index("c")
    chips = [(1 - x, y), (x, 1 - y), (1 - x, 1 - y)]
    return x, y, c, chips


def _half(ref, hc, axis=0):
    n = ref.shape[axis] // 2
    idx = (slice(None),) * axis + (pl.ds(hc * n, n),)
    return ref.at[idx]


def gather_shards(name, tensors):
    nt = len(tensors)

    def body(*refs):
        a, g = refs[:nt], refs[nt:2 * nt]
        send, recv = refs[2 * nt:]
        x, y, c, chips = _place()
        q = 2 * x + y
        sib = (x, y, 1 - c)
        _handshake([sib] + [(*chip, c) for chip in chips])

        def slot(t, chip, hc):
            return _half(g[t].at[2 * chip[0] + chip[1]], hc)

        def rc(t, k, src, dst, to):
            return pltpu.make_async_remote_copy(src_ref=src, dst_ref=dst, send_sem=send.at[t, k], recv_sem=recv.at[t, k],
                                                device_id=to, device_id_type=MESH)

        sent = []
        for t in range(nt):
            cp = rc(t, 6, a[t], g[t].at[q], sib)
            cp.start()
            sent.append(cp)
            for j, chip in enumerate(chips):
                cp = rc(t, j, _half(a[t], c), slot(t, (x, y), c), (*chip, c))
                cp.start()
                sent.append(cp)
        for t in range(nt):
            for j, chip in enumerate(chips):
                landed = slot(t, chip, c)
                rc(t, j, landed, landed, (*chip, c)).wait_recv()
                cp = rc(t, 3 + j, landed, landed, sib)
                cp.start()
                sent.append(cp)
        for t in range(nt):
            for j, chip in enumerate(chips):
                other = slot(t, chip, 1 - c)
                rc(t, 3 + j, other, other, sib).wait_recv()
            own = g[t].at[q]
            rc(t, 6, own, own, sib).wait_recv()
        for cp in sent:
            cp.wait_send()

    return pl.kernel(
        body, name=name, out_type=[_sds((N_CHIPS,) + a.shape, a.dtype) for a in tensors],
        mesh=plsc.ScalarSubcoreMesh(axis_name="sequencer", num_cores=1),
        scratch_types=[pltpu.SemaphoreType.DMA((nt, 7)), pltpu.SemaphoreType.DMA((nt, 7))],
        cost_estimate=_copy_cost(sum(_nbytes(a) for a in tensors), 4),
        compiler_params=pltpu.CompilerParams(collective_id=GATHER_ID))(*tensors)


def pair_exchange(name, grads):
    nt = len(grads)

    def body(*refs):
        g, theirs = refs[:nt], refs[nt:2 * nt]
        send, recv = refs[2 * nt:]
        x, y, c, _ = _place()
        cps = []
        for t in range(nt):
            cp = pltpu.make_async_remote_copy(src_ref=_half(g[t], 1 - c, 1), dst_ref=theirs[t], send_sem=send.at[t],
                                              recv_sem=recv.at[t], device_id=(x, y, 1 - c), device_id_type=MESH)
            cp.start()
            cps.append(cp)
        for cp in cps:
            cp.wait()

    return pl.pallas_call(
        body, in_specs=[ANY] * nt, out_specs=[ANY] * nt,
        out_shape=[_sds((N_CHIPS, a.shape[1] // 2, a.shape[2]), a.dtype) for a in grads],
        scratch_shapes=[pltpu.SemaphoreType.DMA((nt,)), pltpu.SemaphoreType.DMA((nt,))],
        name=name)(*grads)


def chip_exchange(name, parts):
    nt = len(parts)

    def body(*refs):
        a, r = refs[:nt], refs[nt:2 * nt]
        send, recv = refs[2 * nt:]
        x, y, c, chips = _place()
        _handshake([(*chip, c) for chip in chips])
        cps = []
        for t in range(nt):
            for j, chip in enumerate(chips):
                cp = pltpu.make_async_remote_copy(
                    src_ref=a[t].at[2 * chip[0] + chip[1]], dst_ref=r[t].at[j], send_sem=send.at[t, j],
                    recv_sem=recv.at[t, j], device_id=(*chip, c), device_id_type=MESH)
                cp.start()
                cps.append(cp)
        for cp in cps:
            cp.wait()

    return pl.kernel(
        body, name=name, out_type=[_sds((N_CHIPS - 1,) + a.shape[1:], a.dtype) for a in parts],
        mesh=plsc.ScalarSubcoreMesh(axis_name="sequencer", num_cores=1),
        scratch_types=[pltpu.SemaphoreType.DMA((nt, 3)), pltpu.SemaphoreType.DMA((nt, 3))],
        cost_estimate=_copy_cost(sum(_nbytes(a) for a in parts), 0.75),
        compiler_params=pltpu.CompilerParams(collective_id=CHIP_EXCHANGE_ID))(*parts)


def pair_share(name, halves):
    nt = len(halves)

    def body(*refs):
        h, other = refs[:nt], refs[nt:2 * nt]
        send, recv = refs[2 * nt:]
        x, y, c, _ = _place()
        _handshake([(x, y, 1 - c)])
        cps = []
        for t in range(nt):
            cp = pltpu.make_async_remote_copy(src_ref=h[t], dst_ref=other[t], send_sem=send.at[t], recv_sem=recv.at[t],
                                              device_id=(x, y, 1 - c), device_id_type=MESH)
            cp.start()
            cps.append(cp)
        for cp in cps:
            cp.wait()

    return pl.kernel(
        body, name=name, out_type=[_sds(a.shape, a.dtype) for a in halves],
        mesh=plsc.ScalarSubcoreMesh(axis_name="sequencer", num_cores=1),
        scratch_types=[pltpu.SemaphoreType.DMA((nt,)), pltpu.SemaphoreType.DMA((nt,))],
        cost_estimate=_copy_cost(sum(_nbytes(a) for a in halves), 1),
        compiler_params=pltpu.CompilerParams(collective_id=PAIR_ID))(*halves)


def all_reduce_small(parts, rows):
    cdim = parts[0].shape[1]
    n = len(parts)
    vm = pl.BlockSpec(memory_space=pltpu.VMEM)

    def pack(*refs):
        p, o_ref = refs[:n], refs[n]
        at = 0
        for ref in p:
            o_ref[pl.ds(at, ref.shape[0]), :] = ref[...]
            at += ref.shape[0]
        o_ref[pl.ds(at, rows - at), :] = jnp.zeros((rows - at, cdim), F32)

    mine = pl.pallas_call(pack, in_specs=[vm] * n, out_specs=vm, out_shape=_sds((rows, cdim), F32), name="small_pack")(*parts)

    def exchange(mine_ref, buf, send, recv, lsem):
        x, y, c, _ = _place()
        me = 4 * x + 2 * y + c
        peers = [(x ^ (k >> 2), y ^ ((k >> 1) & 1), c ^ (k & 1)) for k in range(1, 8)]
        _handshake(peers)
        own = pltpu.make_async_copy(mine_ref, buf.at[me], lsem)
        own.start()
        cps = []
        for k, to in enumerate(peers):
            cp = pltpu.make_async_remote_copy(src_ref=mine_ref, dst_ref=buf.at[me], send_sem=send.at[k], recv_sem=recv.at[k],
                                              device_id=to, device_id_type=MESH)
            cp.start()
            cps.append(cp)
        for k, (px, py, pc) in enumerate(peers):
            pltpu.make_async_remote_copy(src_ref=mine_ref, dst_ref=buf.at[4 * px + 2 * py + pc], send_sem=send.at[k],
                                         recv_sem=recv.at[k], device_id=(x, y, c), device_id_type=MESH).wait_recv()
        for cp in cps:
            cp.wait_send()
        own.wait()

    landed = pl.kernel(
        exchange, name="small_exchange", out_type=_sds((8, rows, cdim), F32),
        mesh=plsc.ScalarSubcoreMesh(axis_name="sequencer", num_cores=1),
        scratch_types=[pltpu.SemaphoreType.DMA((7,)), pltpu.SemaphoreType.DMA((7,)), pltpu.SemaphoreType.DMA],
        cost_estimate=_copy_cost(rows * cdim * 4, 7),
        compiler_params=pltpu.CompilerParams(collective_id=ALL_ID))(mine)

    def total(buf, o_ref):
        acc = buf[0]
        for d in range(1, 8):
            acc = acc + buf[d]
        o_ref[...] = acc

    return pl.pallas_call(total, in_specs=[vm], out_specs=vm, out_shape=_sds((rows, cdim), F32), name="small_sum")(landed)


def pair_sum(g, theirs, core, tm=256):
    _, r, c = g.shape
    tm = _tile(r // 2, tm)
    nh = r // 2 // tm

    def body(core_ref, a_ref, b_ref, o_ref):
        o_ref[...] = (a_ref[...].astype(F32) + b_ref[...].astype(F32)).astype(BF16)

    blk = (N_CHIPS, tm, c)
    return pl.pallas_call(
        body, grid_spec=pltpu.PrefetchScalarGridSpec(
            num_scalar_prefetch=1, grid=(nh,),
            in_specs=[pl.BlockSpec(blk, lambda i, cr: (0, cr[0] * nh + i, 0)), pl.BlockSpec(blk, lambda i, cr: (0, i, 0))],
            out_specs=pl.BlockSpec(blk, lambda i, cr: (0, i, 0))),
        out_shape=_sds(theirs.shape, BF16), compiler_params=_params(("parallel",)), name="pair_sum")(core, g, theirs)


def chip_sum(own, landed, chip, stack, layer, layers, tm=256):
    _, r, c = own.shape
    tm = _tile(r, tm)

    def body(chip_ref, own_ref, l_ref, *rest):
        acc = own_ref[...].astype(F32)
        for j in range(N_CHIPS - 1):
            acc = acc + l_ref[j].astype(F32)
        rest[-1][...] = acc

    in_specs = [pl.BlockSpec((None, tm, c), lambda i, qr: (qr[0], i, 0)),
                pl.BlockSpec((N_CHIPS - 1, tm, c), lambda i, qr: (0, i, 0))]
    args = [chip, own, landed]
    if stack is not None:
        in_specs.append(ANY)
        args.append(stack)
    return pl.pallas_call(
        body, grid_spec=pltpu.PrefetchScalarGridSpec(
            num_scalar_prefetch=1, grid=(r // tm,), in_specs=in_specs,
            out_specs=pl.BlockSpec((None, tm, c), lambda i, qr: (layer, i, 0))),
        out_shape=_sds((layers, r, c), F32), input_output_aliases={3: 0} if stack is not None else {},
        compiler_params=_params(("parallel",)), name="chip_sum")(*args)


def adamw_joined(w, m, v, g_mine, g_theirs, core, tm=256):
    nl, r, c = w.shape
    tm = _tile(r // 2, tm)
    nh = r // 2 // tm
    bc1 = 1.0 - ADAM_B1 ** ADAM_STEP
    bc2 = 1.0 - ADAM_B2 ** ADAM_STEP

    def body(core_ref, w_ref, m_ref, v_ref, gm_ref, gt_ref, g_ref, d_ref, nm_ref, nv_ref):
        mine = (pl.program_id(1) // nh) == core_ref[0]
        gv = jnp.where(mine, gm_ref[...], gt_ref[...])
        nm = ADAM_B1 * m_ref[...] + (1.0 - ADAM_B1) * gv
        nv = ADAM_B2 * v_ref[...] + (1.0 - ADAM_B2) * (gv * gv)
        g_ref[...] = gv
        d_ref[...] = -ADAM_LR * ((nm / bc1) / (jnp.sqrt(nv / bc2) + ADAM_EPS) + ADAM_WD * w_ref[...])
        nm_ref[...] = nm
        nv_ref[...] = nv

    full = pl.BlockSpec((None, tm, c), lambda l, i, cr: (l, i, 0))
    half = pl.BlockSpec((None, tm, c), lambda l, i, cr: (l, i % nh, 0))
    return pl.pallas_call(
        body, grid_spec=pltpu.PrefetchScalarGridSpec(
            num_scalar_prefetch=1, grid=(nl, r // tm), in_specs=[full, full, full, half, half], out_specs=[full] * 4),
        out_shape=[_sds((nl, r, c), F32)] * 4, compiler_params=_params(("parallel", "parallel")),
        name="adamw_joined")(core, w, m, v, g_mine, g_theirs)


def adamw(w, g, m, v, tm=256):
    shape = w.shape
    c = shape[-1]
    r = w.size // c
    tm = _tile(r, tm)
    bc1 = 1.0 - ADAM_B1 ** ADAM_STEP
    bc2 = 1.0 - ADAM_B2 ** ADAM_STEP

    def body(w_ref, g_ref, m_ref, v_ref, d_ref, nm_ref, nv_ref):
        gv = g_ref[...]
        nm = ADAM_B1 * m_ref[...] + (1.0 - ADAM_B1) * gv
        nv = ADAM_B2 * v_ref[...] + (1.0 - ADAM_B2) * (gv * gv)
        d_ref[...] = -ADAM_LR * ((nm / bc1) / (jnp.sqrt(nv / bc2) + ADAM_EPS) + ADAM_WD * w_ref[...])
        nm_ref[...] = nm
        nv_ref[...] = nv

    outs = pl.pallas_call(
        body, grid=(r // tm,), in_specs=[_rows(tm, c)] * 4, out_specs=[_rows(tm, c)] * 3,
        out_shape=[_sds((r, c), F32)] * 3, compiler_params=_params(("parallel",)), name="adamw")(
            w.reshape(r, c), g.reshape(r, c), m.reshape(r, c), v.reshape(r, c))
    return [o.reshape(shape) for o in outs]


WEIGHTS = ['sc_w_in', 'sc_conv_w', 'sc_w_out', 'mla_w_dq', 'mla_g_q', 'mla_w_uq', 'mla_w_dkv', 'mla_g_kv', 'mla_w_uk',
           'mla_w_uv', 'mla_w_o', 'cf_w_pw1', 'cf_b_pw1', 'cf_dw_w', 'cf_dw_b', 'cf_norm_g', 'cf_norm_b', 'cf_w_pw2',
           'cf_b_pw2', 'ff_w1', 'ff_w2', 'ln_mix_g', 'ln_mix_b', 'ln_ff_g', 'ln_ff_b']
ARGS = ['x'] + WEIGHTS + ['loss_target'] + ['m_' + n for n in WEIGHTS] + ['v_' + n for n in WEIGHTS]


def _sq_relu(h):
    r = jnp.maximum(h.astype(F32), 0.0)
    return (r * r).astype(BF16)


def _mlp_forward(i, x, xb, w1, w2, g, b):
    hb = mm_plain_nn(f"mlp{i}_up", xb, w1, BF16)
    y, yb, xh, rstd = mm_residual_ln(f"mlp{i}_down_ln", hb, w2, x, g, b, a_fn=_sq_relu)
    return (y, yb), dict(xb=xb, hb=hb, xh=xh, rstd=rstd, g=g)


def _mlp_backward(i, dy, sv, w1, w2, dw1, dw2, reduce_after):
    s = dy.shape[0]
    dr, drb, dg, db, _ = ln_backward(f"mlp{i}_ln_bwd", dy, sv["xh"], sv["rstd"], sv["g"])
    tm, tn = _tile(s, 1024), 512

    def epi(acc, e, o):
        o[0][...] = (acc * (2.0 * jnp.maximum(e[0][...].astype(F32), 0.0))).astype(BF16)

    dhb = mm_nt(f"mlp{i}_down_bwd", drb, w2, s, tm, tn, 1024, epi, [_sds((s, w2.k), BF16)], [_ij(tm, tn)],
                [sv["hb"]], [_ij(tm, tn)])[0]
    g_w2 = mm_tn(f"mlp{i}_dw2", sv["hb"], drb, dw2, s, 512, 1024, a_fn=_sq_relu)
    g_w1 = mm_tn(f"mlp{i}_dw1", sv["xb"], dhb, dw1, s, 1024, 512)
    dhb = reduce_after(dhb, {f"w1_{i}": g_w1, f"w2_{i}": g_w2})
    dx = mm_plain_nt(f"mlp{i}_up_bwd", dhb, w1, F32, tn=1024, add=dr, add_scale=ALPHA)
    return dx, dg, db


def kernel(x, sc_w_in, sc_conv_w, sc_w_out, mla_w_dq, mla_g_q, mla_w_uq, mla_w_dkv, mla_g_kv, mla_w_uk, mla_w_uv, mla_w_o, cf_w_pw1, cf_b_pw1, cf_dw_w, cf_dw_b, cf_norm_g, cf_norm_b, cf_w_pw2, cf_b_pw2, ff_w1, ff_w2, ln_mix_g, ln_mix_b, ln_ff_g, ln_ff_b, loss_target, m_sc_w_in, m_sc_conv_w, m_sc_w_out, m_mla_w_dq, m_mla_g_q, m_mla_w_uq, m_mla_w_dkv, m_mla_g_kv, m_mla_w_uk, m_mla_w_uv, m_mla_w_o, m_cf_w_pw1, m_cf_b_pw1, m_cf_dw_w, m_cf_dw_b, m_cf_norm_g, m_cf_norm_b, m_cf_w_pw2, m_cf_b_pw2, m_ff_w1, m_ff_w2, m_ln_mix_g, m_ln_mix_b, m_ln_ff_g, m_ln_ff_b, v_sc_w_in, v_sc_conv_w, v_sc_w_out, v_mla_w_dq, v_mla_g_q, v_mla_w_uq, v_mla_w_dkv, v_mla_g_kv, v_mla_w_uk, v_mla_w_uv, v_mla_w_o, v_cf_w_pw1, v_cf_b_pw1, v_cf_dw_w, v_cf_dw_b, v_cf_norm_g, v_cf_norm_b, v_cf_w_pw2, v_cf_b_pw2, v_ff_w1, v_ff_w2, v_ln_mix_g, v_ln_mix_b, v_ln_ff_g, v_ln_ff_b):
    given = dict(zip(ARGS, (x, sc_w_in, sc_conv_w, sc_w_out, mla_w_dq, mla_g_q, mla_w_uq, mla_w_dkv, mla_g_kv, mla_w_uk, mla_w_uv, mla_w_o, cf_w_pw1, cf_b_pw1, cf_dw_w, cf_dw_b, cf_norm_g, cf_norm_b, cf_w_pw2, cf_b_pw2, ff_w1, ff_w2, ln_mix_g, ln_mix_b, ln_ff_g, ln_ff_b, loss_target, m_sc_w_in, m_sc_conv_w, m_sc_w_out, m_mla_w_dq, m_mla_g_q, m_mla_w_uq, m_mla_w_dkv, m_mla_g_kv, m_mla_w_uk, m_mla_w_uv, m_mla_w_o, m_cf_w_pw1, m_cf_b_pw1, m_cf_dw_w, m_cf_dw_b, m_cf_norm_g, m_cf_norm_b, m_cf_w_pw2, m_cf_b_pw2, m_ff_w1, m_ff_w2, m_ln_mix_g, m_ln_mix_b, m_ln_ff_g, m_ln_ff_b, v_sc_w_in, v_sc_conv_w, v_sc_w_out, v_mla_w_dq, v_mla_g_q, v_mla_w_uq, v_mla_w_dkv, v_mla_g_kv, v_mla_w_uk, v_mla_w_uv, v_mla_w_o, v_cf_w_pw1, v_cf_b_pw1, v_cf_dw_w, v_cf_dw_b, v_cf_norm_g, v_cf_norm_b, v_cf_w_pw2, v_cf_b_pw2, v_ff_w1, v_ff_w2, v_ln_mix_g, v_ln_mix_b, v_ln_ff_g, v_ln_ff_b)))
    s, d = x.shape[1], x.shape[2]
    d_ff = 4 * d
    dq4 = d // N_CHIPS
    xq = lax.axis_index("x") * 2 + lax.axis_index("y")

    w_dkv_pad = jnp.pad(mla_w_dkv[0], ((0, 0), (0, 128 - QK_ROPE)))
    w_uq_pad = jnp.pad(mla_w_uq[0].reshape(Q_LORA, 2, QK_NOPE + QK_ROPE), ((0, 0), (0, 0), (0, HEAD_PAD - QK_NOPE - QK_ROPE)))
    small = jnp.concatenate([
        sc_conv_w.reshape(2 * SC_WIDTH, dq4), cf_b_pw1.reshape(2, dq4), cf_dw_w[0], cf_dw_b, cf_norm_g, cf_norm_b,
        cf_b_pw2, jnp.zeros((5, dq4), F32)], axis=0)
    mlp_w = lambda i: [ff_w1[i].astype(BF16), ff_w2[i].astype(BF16)]
    g_in, g_out, g_w1, g_w2 = [None] * 2, [None] * 2, [None] * DEPTH, [None] * DEPTH
    g_in[0], g_out[0], g_small = gather_shards("gather_mixer0", [sc_w_in[0].astype(BF16), sc_w_out[0].astype(BF16), small])
    (g_w1[0],) = gather_shards("gather_up0", [ff_w1[0].astype(BF16)])
    (g_w2[0],) = gather_shards("gather_down0", [ff_w2[0].astype(BF16)])
    g_dqkv, g_uq, g_uk, g_uv, g_o, g_w1[1], g_w2[1] = gather_shards("gather_layer1", [
        jnp.concatenate([mla_w_dq[0], w_dkv_pad], axis=1).astype(BF16),
        w_uq_pad.reshape(Q_LORA, 2 * HEAD_PAD).astype(BF16),
        mla_w_uk.reshape(KV_LORA // N_CHIPS, N_HEADS * QK_NOPE).astype(BF16),
        mla_w_uv.reshape(KV_LORA // N_CHIPS, N_HEADS * V_HEAD).astype(BF16), mla_w_o[0].astype(BF16)] + mlp_w(1))
    g_pw1, g_pw2, g_w1[2], g_w2[2] = gather_shards(
        "gather_layer2", [cf_w_pw1[0].astype(BF16), cf_w_pw2[0].astype(BF16)] + mlp_w(2))
    g_in[1], g_out[1], g_w1[3], g_w2[3] = gather_shards(
        "gather_layer3", [sc_w_in[1].astype(BF16), sc_w_out[1].astype(BF16)] + mlp_w(3))

    wd_t = Q_LORA + KV_LORA + 128
    w_in = [Stk("col", d, 3 * d, g_in[j]) for j in range(2)]
    w_out = [Stk("row", d, d, g_out[j]) for j in range(2)]
    w_dqkv = Stk("row", d, wd_t, g_dqkv)
    w_uq = Stk("col", Q_LORA, N_HEADS * HEAD_PAD, g_uq)
    w_uk = Stk("row", KV_LORA, N_HEADS * QK_NOPE, g_uk)
    w_uv = Stk("row", KV_LORA, N_HEADS * V_HEAD, g_uv)
    w_o = Stk("row", d, d, g_o)
    w_pw1 = Stk("col", d, 2 * d, g_pw1)
    w_pw2 = Stk("row", d, d, g_pw2)
    w_1 = [Stk("col", d, d_ff, g_w1[i]) for i in range(DEPTH)]
    w_2 = [Stk("row", d_ff, d, g_w2[i]) for i in range(DEPTH)]

    def wide(rows):
        return jnp.swapaxes(rows, 0, 1).reshape(rows.shape[1], d)

    conv_w = wide(g_small[:, 0:6]).reshape(2, SC_WIDTH, d)
    b_pw1 = g_small[:, 6:8].reshape(1, 2 * d)
    dw_w = wide(g_small[:, 8:39])
    dw_b, norm_g, norm_b, b_pw2 = (wide(g_small[:, 39 + k:40 + k]) for k in range(4))

    pos = jnp.arange(s, dtype=F32)
    inv_freq = ROPE_THETA ** (-jnp.arange(0, QK_ROPE, 2, dtype=F32) / QK_ROPE)
    ang = pos[:, None] * inv_freq[None, :]
    cos, sin, zero = jnp.cos(ang), jnp.sin(ang), jnp.zeros((s, 128 - QK_ROPE), F32)
    cf = jnp.concatenate([cos, cos, zero], axis=1)
    sf = jnp.concatenate([-sin, sin, zero], axis=1)

    def row(a, i):
        return a[i:i + 1]

    xs = x.reshape(s, d)
    cur = (xs, xs.astype(BF16))
    tape = []
    for i in range(DEPTH):
        mixer, j = i % 3, i // 3
        xf, xb = cur
        lg, lb = row(ln_mix_g, i), row(ln_mix_b, i)
        if mixer == 0:
            u = mm_plain_nn(f"sc{j}_in", xb, w_in[j], F32, tn=3 * dq4)
            gb = short_conv_gate(u, conv_w[j])
            y, yb, xh, rstd = mm_residual_ln(f"sc{j}_out_ln", gb, w_out[j], xf, lg, lb)
            sv = dict(xb=xb, u=u, gb=gb)
        elif mixer == 1:
            t = mm_plain_nn("mla_down", xb, w_dqkv, F32, tn=wd_t // 2)
            cq, ckv, kpe = mla_latents(t, mla_g_q, mla_g_kv, cf, sf)
            qh = mla_queries(cq, w_uq, cf, sf)
            kh = mla_keys(ckv, w_uk, kpe)
            vh = mm_plain_nn("mla_values", ckv, w_uv, BF16, tk=KV_LORA)
            oh = attention(qh, kh, vh)
            y, yb, xh, rstd = mm_residual_ln("mla_out_ln", oh, w_o, xf, lg, lb)
            sv = dict(xb=xb, t=t, cq=cq, ckv=ckv, qh=qh, kh=kh, vh=vh, oh=oh)
        else:
            u = mm_plain_nn("cf_pw1", xb, w_pw1, F32, bias=b_pw1)
            hc = conformer_glu_conv(u, dw_w, dw_b)
            sb = conformer_norm_swish(hc, norm_g, norm_b)
            y, yb, xh, rstd = mm_residual_ln("cf_pw2_ln", sb, w_pw2, xf, lg, lb, bias=b_pw2)
            sv = dict(xb=xb, u=u, hc=hc, sb=sb)
        sv.update(xh=xh, rstd=rstd, g=lg)
        cur, sv_mlp = _mlp_forward(i, y, yb, w_1[i], w_2[i], row(ln_ff_g, i), row(ln_ff_b, i))
        tape.append((sv, sv_mlp))

    dy, loss_part = loss_head(cur[0], loss_target.reshape(s, d))

    grads = {}
    smalls = {}
    g_ln = {n: [None] * DEPTH for n in ("ln_mix_g", "ln_mix_b", "ln_ff_g", "ln_ff_b")}
    conv_grads = [None, None]
    core = lax.axis_index("c").astype(jnp.int32).reshape(1)
    chip = xq.astype(jnp.int32).reshape(1)
    pairs, landed = {}, {}
    ready = []

    def reduce_after(x, new):
        out = lax.optimization_barrier((x, *new.values()))
        grads.update(zip(new, out[1:]))
        ready.extend(new)
        return out[0]

    def reduce_layer(i, x):
        theirs = pair_exchange(f"pair_exchange_layer{i}", [grads[n] for n in ready])
        sums = [pair_sum(grads[n], th, core) for n, th in zip(ready, theirs)]
        pairs.update(zip(ready, sums))
        landed.update(zip(ready, chip_exchange(f"chip_exchange_layer{i}", sums)))
        exchanged.append(list(ready))
        ready.clear()
        return lax.optimization_barrier((x, *sums))[0]

    groups = [["in_0", "in_1"], ["out_0", "out_1"], ["dqkv"], ["uq"], ["uk"], ["uv"], ["o"], ["pw1"], ["pw2"],
              [f"w1_{i}" for i in range(DEPTH)], [f"w2_{i}" for i in range(DEPTH)]]
    stacks = [None] * len(groups)
    exchanged = []

    def sum_layer(x, last=False):
        names = exchanged.pop(0)
        if last:
            out = lax.optimization_barrier((x, *[landed[n] for n in names]))
            landed.update(zip(names, out[1:]))
        new = []
        for n in names:
            k = next(k for k, members in enumerate(groups) if n in members)
            stacks[k] = chip_sum(pairs[n], landed[n], chip, stacks[k], groups[k].index(n), len(groups[k]))
            new.append(stacks[k])
        return out[0] if last else lax.optimization_barrier((x, *new))[0]

    for i in reversed(range(DEPTH)):
        mixer, j = i % 3, i // 3
        sv, sv_mlp = tape[i]
        dy, g_ln["ln_ff_g"][i], g_ln["ln_ff_b"][i] = _mlp_backward(
            i, dy, sv_mlp, w_1[i], w_2[i], Stk("col", d, d_ff), Stk("row", d_ff, d), reduce_after)
        if i == 0:
            dy = reduce_layer("0_mlp", dy)
        dr, drb, g_ln["ln_mix_g"][i], g_ln["ln_mix_b"][i], dr_sum = ln_backward(
            f"mix{i}_ln_bwd", dy, sv["xh"], sv["rstd"], sv["g"])
        if mixer == 0:
            dgate = mm_plain_nt(f"sc{j}_out_bwd", drb, w_out[j], F32)
            dw_out = mm_tn(f"sc{j}_dw_out", sv["gb"], drb, Stk("row", d, d), s, 512, 1024)
            du, conv_grads[j] = short_conv_gate_bwd(sv["u"], conv_w[j], dgate)
            nb = d // 256
            dw_in = mm_tn(
                f"sc{j}_dw_in", sv["xb"], du, Stk("col", d, 3 * d), s, 1024, 256,
                b_spec=pl.BlockSpec((None, s, 256), lambda i_, j_, k_: (j_ // nb, k_, j_ % nb)))
            du = reduce_after(du, {f"in_{j}": dw_in, f"out_{j}": dw_out})
            dy = mm_plain_nt(
                f"sc{j}_in_bwd", du, w_in[j], F32, tk=256, add=dr, add_scale=ALPHA,
                a_spec_fn=(s, lambda tm, tk: pl.BlockSpec((None, tm, tk), lambda i_, j_, k_: (k_ // nb, i_, k_ % nb))))
        elif mixer == 1:
            do = mm_plain_nt("mla_out_bwd", drb, w_o, BF16)
            g_o = mm_tn("mla_dw_o", sv["oh"], drb, Stk("row", d, d), s, 512, 1024)
            dqh, dkh, dvh = attention_bwd(sv["qh"], sv["kh"], sv["vh"], do)
            dql, dkn, dkpe = mla_unrope_grads(dqh, dkh, cf, sf)
            g_uq = mm_tn("mla_dw_uq", sv["cq"], dql, Stk("col", Q_LORA, N_HEADS * HEAD_PAD), s, Q_LORA, 512)
            dcq = mm_plain_nt("mla_uq_bwd", dql, w_uq, F32, tn=Q_LORA)
            g_uk = mm_tn("mla_dw_uk", sv["ckv"], dkn, Stk("row", KV_LORA, N_HEADS * QK_NOPE), s, KV_LORA, 1024)
            g_uv = mm_tn("mla_dw_uv", sv["ckv"], dvh, Stk("row", KV_LORA, N_HEADS * V_HEAD), s, KV_LORA, 1024)
            dckv = mm_plain_nt("mla_uk_bwd", dkn, w_uk, F32, tn=KV_LORA)
            dckv = mm_plain_nt("mla_uv_bwd", dvh, w_uv, F32, tn=KV_LORA, add=dckv)
            dt, smalls["g_q"], smalls["g_kv"] = mla_latents_bwd(sv["t"], mla_g_q, mla_g_kv, cf, sf, dcq, dckv, dkpe)
            g_dqkv = mm_tn("mla_dw_down", sv["xb"], dt, Stk("row", d, wd_t), s, 512, wd_t)
            dt = reduce_after(dt, {"dqkv": g_dqkv, "uq": g_uq, "uk": g_uk, "uv": g_uv, "o": g_o})
            dy = mm_plain_nt("mla_down_bwd", dt, w_dqkv, F32, tk=wd_t, add=dr, add_scale=ALPHA)
        else:
            dsw = mm_plain_nt("cf_pw2_bwd", drb, w_pw2, F32)
            g_pw2 = mm_tn("cf_dw_pw2", sv["sb"], drb, Stk("row", d, d), s, 512, 1024)
            smalls["b_pw2"] = dr_sum
            dhc, smalls["norm_g"], smalls["norm_b"] = conformer_norm_swish_bwd(sv["hc"], norm_g, norm_b, dsw)
            du, smalls["b_pw1"], smalls["dw_w"], smalls["dw_b"] = conformer_glu_conv_bwd(sv["u"], dw_w, dhc)
            nb = d // 512
            g_pw1 = mm_tn(
                "cf_dw_pw1", sv["xb"], du, Stk("col", d, 2 * d), s, 1024, 512,
                b_spec=pl.BlockSpec((None, s, 512), lambda i_, j_, k_: (j_ // nb, k_, j_ % nb)))
            du = reduce_after(du, {"pw1": g_pw1, "pw2": g_pw2})
            dy = mm_plain_nt(
                "cf_pw1_bwd", du, w_pw1, F32, add=dr, add_scale=ALPHA,
                a_spec_fn=(s, lambda tm, tk: pl.BlockSpec((None, tm, tk), lambda i_, j_, k_: (k_ // nb, i_, k_ % nb))))
        if i < DEPTH - 1:
            dy = sum_layer(dy)
        dy = reduce_layer(i, dy)
    dy = sum_layer(sum_layer(dy, last=True), last=True)
    grad_x = dy.reshape(1, s, d)

    mine = stacks
    other = (pair_share("pair_share_mixers", mine[:9]) + pair_share("pair_share_up", mine[9:10])
             + pair_share("pair_share_down", mine[10:]))

    def padded(get):
        dqkv = jnp.concatenate([get("mla_w_dq")[0], jnp.pad(get("mla_w_dkv")[0], ((0, 0), (0, 128 - QK_ROPE)))], axis=1)
        uq = jnp.pad(get("mla_w_uq")[0].reshape(Q_LORA, 2, QK_NOPE + QK_ROPE),
                     ((0, 0), (0, 0), (0, HEAD_PAD - QK_NOPE - QK_ROPE))).reshape(Q_LORA, 2 * HEAD_PAD)
        return [get("sc_w_in"), get("sc_w_out"), dqkv[None], uq[None],
                get("mla_w_uk").reshape(1, KV_LORA // N_CHIPS, d), get("mla_w_uv").reshape(1, KV_LORA // N_CHIPS, d),
                get("mla_w_o"), get("cf_w_pw1"), get("cf_w_pw2"), get("ff_w1"), get("ff_w2")]

    w_l, m_l, v_l = (padded(lambda n, p=p: given[p + n]) for p in ("", "m_", "v_"))
    res = [adamw_joined(w_l[k], m_l[k], v_l[k], mine[k], other[k], core) for k in range(len(groups))]

    def unpadded(k):
        r_in, r_out, r_dqkv, r_uq, r_uk, r_uv, r_o, r_pw1, r_pw2, r_w1, r_w2 = (r[k] for r in res)
        return {
            "sc_w_in": r_in, "sc_w_out": r_out, "mla_w_dq": r_dqkv[:, :, 0:Q_LORA],
            "mla_w_dkv": r_dqkv[:, :, Q_LORA:Q_LORA + KV_LORA + QK_ROPE],
            "mla_w_uq": r_uq.reshape(1, Q_LORA, 2, HEAD_PAD)[:, :, :, 0:QK_NOPE + QK_ROPE].reshape(mla_w_uq.shape),
            "mla_w_uk": r_uk.reshape(mla_w_uk.shape), "mla_w_uv": r_uv.reshape(mla_w_uv.shape),
            "mla_w_o": r_o, "cf_w_pw1": r_pw1, "cf_w_pw2": r_pw2, "ff_w1": r_w1, "ff_w2": r_w2}

    big_g, big_d, big_m, big_v = (unpadded(k) for k in range(4))

    pad_row = lambda a: jnp.pad(a, ((0, 0), (0, d - a.shape[1])))
    small_parts = ([g for n in ("ln_mix_g", "ln_mix_b", "ln_ff_g", "ln_ff_b") for g in g_ln[n]]
                   + [pad_row(smalls["g_q"]), pad_row(smalls["g_kv"]), conv_grads[0], conv_grads[1],
                      smalls["b_pw1"].reshape(2, d), smalls["dw_w"], smalls["dw_b"], smalls["norm_g"], smalls["norm_b"],
                      smalls["b_pw2"], loss_part])
    red = all_reduce_small(small_parts, 64)
    loss = red[61, 0]

    def shard(rows):
        return lax.dynamic_slice_in_dim(rows, xq * dq4, dq4, axis=1)

    gw = {
        **big_g,
        "ln_mix_g": red[0:4], "ln_mix_b": red[4:8], "ln_ff_g": red[8:12], "ln_ff_b": red[12:16],
        "mla_g_q": red[16:17, 0:Q_LORA], "mla_g_kv": red[17:18, 0:KV_LORA],
        "sc_conv_w": shard(red[18:24]).reshape(2, SC_WIDTH, dq4),
        "cf_b_pw1": lax.dynamic_slice_in_dim(red[24:26].reshape(1, 2 * d), xq * 2 * dq4, 2 * dq4, axis=1),
        "cf_dw_w": shard(red[26:57])[None], "cf_dw_b": shard(red[57:58]), "cf_norm_g": shard(red[58:59]),
        "cf_norm_b": shard(red[59:60]), "cf_b_pw2": shard(red[60:61]),
    }

    upd = {n: [big_d[n], big_m[n], big_v[n]] for n in big_g}

    def pack(names, width, get):
        return jnp.concatenate([get(n).reshape(-1, width) for n in names], axis=0)

    def unpack(names, packed):
        out, at = {}, 0
        for n in names:
            rows = given[n].size // packed.shape[1]
            out[n] = packed[at:at + rows].reshape(given[n].shape)
            at += rows
        return out

    rep = ["ln_mix_g", "ln_mix_b", "ln_ff_g", "ln_ff_b"]
    shd = ["sc_conv_w", "cf_b_pw1", "cf_dw_w", "cf_dw_b", "cf_norm_g", "cf_norm_b", "cf_b_pw2"]
    for names, width in ((rep, d), (shd, dq4), (["mla_g_q"], Q_LORA), (["mla_g_kv"], KV_LORA)):
        res = adamw(pack(names, width, lambda n: given[n]), pack(names, width, lambda n: gw[n]),
                    pack(names, width, lambda n: given["m_" + n]), pack(names, width, lambda n: given["v_" + n]), tm=4096)
        parts = [unpack(names, r) for r in res]
        for n in names:
            upd[n] = [p[n] for p in parts]

    return (loss, grad_x, *[gw[n].reshape(given[n].shape) for n in WEIGHTS], *[upd[n][0] for n in WEIGHTS],
            *[upd[n][1] for n in WEIGHTS], *[upd[n][2] for n in WEIGHTS])
```

```python
import jax
import jax.numpy as jnp
from jax import lax
from jax.experimental import pallas as pl
from jax.experimental.pallas import tpu as pltpu
from jax.experimental.pallas import tpu_sc as plsc

F32 = jnp.float32
BF16 = jnp.bfloat16
MESH = pl.DeviceIdType.MESH

DEPTH = 4
ALPHA = (2.0 * DEPTH) ** 0.25
LN_EPS = 1e-5
RMS_EPS = 1e-6
CHUNK_SHIFT = 6
N_HEADS = 8
QK_NOPE = 128
QK_ROPE = 64
V_HEAD = 128
HEAD_PAD = 256
Q_LORA = 384
KV_LORA = 256
ROPE_THETA = 10000.0
SC_WIDTH = 3
CONF_WIDTH = 31
CONV_PAD = 32
CONV_CHUNK = 64
N_CHIPS = 4
ATTN_SCALE = (QK_NOPE + QK_ROPE) ** -0.5

ADAM_LR = 0.001
ADAM_B1 = 0.9
ADAM_B2 = 0.999
ADAM_EPS = 1e-08
ADAM_WD = 0.01
ADAM_STEP = 10

VMEM_LIMIT = 56 * 2**20

NN = (((1,), (0,)), ((), ()))
NT = (((1,), (1,)), ((), ()))
TN = (((0,), (0,)), ((), ()))


def _params(sem=None):
    return pltpu.CompilerParams(dimension_semantics=sem, vmem_limit_bytes=VMEM_LIMIT)


class Stk:
    def __init__(self, kind, k, n, arr=None, layers=None, layer=None):
        self.kind, self.k, self.n, self.layers, self.layer = kind, k, n, layers, layer
        self.plain = (kind == "row" and layers is None) or kind == "full"
        self.kloc = k // N_CHIPS if kind == "row" else k
        self.nloc = n // N_CHIPS if kind == "col" else n
        if arr is not None and self.plain:
            arr = arr.reshape(k, n)
        self.arr = arr

    @property
    def shape(self):
        if self.plain:
            return (self.k, self.n)
        lead = (N_CHIPS,) if self.layers is None else (N_CHIPS, self.layers)
        return lead + (self.kloc, self.nloc)

    def spec(self, bk, bn, f):
        if self.plain:
            return pl.BlockSpec((bk, bn), f)
        assert self.kloc % bk == 0 and self.nloc % bn == 0, (self.kloc, bk, self.nloc, bn)
        pk, pn = self.kloc // bk, self.nloc // bn
        kind, layer = self.kind, self.layer

        def imap(*g):
            kb, nb = f(*g)
            if kind == "row":
                q, kb, nb = kb // pk, kb % pk, nb
            else:
                q, kb, nb = nb // pn, kb, nb % pn
            return (q, kb, nb) if layer is None else (q, layer, kb, nb)

        block = (None, bk, bn) if layer is None else (None, None, bk, bn)
        return pl.BlockSpec(block, imap)


def _mm(name, mode, a, b, grid, a_spec, b_spec, acc_shape, extras, extra_specs, out_shapes, out_specs, epi, a_fn=None):
    nk = grid[2]
    ne = len(extras)

    def body(*refs):
        a_ref, b_ref = refs[0], refs[1]
        e_refs = refs[2:2 + ne]
        av = a_ref[...] if a_fn is None else a_fn(a_ref[...])
        part = lax.dot_general(av, b_ref[...], mode, preferred_element_type=F32)
        if nk == 1:
            epi(part, e_refs, refs[2 + ne:])
            return
        o_refs = refs[2 + ne:-1]
        acc = refs[-1]
        k = pl.program_id(2)

        @pl.when(k == 0)
        def _():
            acc[...] = part

        @pl.when(k > 0)
        def _():
            acc[...] += part

        @pl.when(k == nk - 1)
        def _():
            epi(acc[...], e_refs, o_refs)

    return pl.pallas_call(
        body, grid=grid, in_specs=[a_spec, b_spec, *extra_specs], out_specs=out_specs, out_shape=out_shapes,
        scratch_shapes=[pltpu.VMEM(acc_shape, F32)] if nk > 1 else [],
        compiler_params=_params(("parallel", "parallel", "arbitrary")), name=name)(a, b, *extras)


def _tile(n, t):
    t = min(n, t)
    while n % t:
        t -= 8
    assert t > 0, (n, t)
    return t


def mm_nn(name, a, w, tm, tn, tk, epi, out_shapes, out_specs, extras=(), extra_specs=(), a_spec=None, a_fn=None):
    m = a.shape[0]
    tm, tn, tk = _tile(m, tm), _tile(w.n, tn), _tile(w.k, tk)
    grid = (m // tm, w.n // tn, w.k // tk)
    a_spec = a_spec or pl.BlockSpec((tm, tk), lambda i, j, k: (i, k))
    b_spec = w.spec(tk, tn, lambda i, j, k: (k, j))
    return _mm(name, NN, a, w.arr, grid, a_spec, b_spec, (tm, tn), extras, extra_specs, out_shapes, out_specs, epi, a_fn)


def mm_nt(name, a, w, m, tm, tn, tk, epi, out_shapes, out_specs, extras=(), extra_specs=(), a_spec=None):
    tm, tn, tk = _tile(m, tm), _tile(w.k, tn), _tile(w.n, tk)
    grid = (m // tm, w.k // tn, w.n // tk)
    a_spec = a_spec or pl.BlockSpec((tm, tk), lambda i, j, k: (i, k))
    b_spec = w.spec(tn, tk, lambda i, j, k: (j, k))
    return _mm(name, NT, a, w.arr, grid, a_spec, b_spec, (tm, tn), extras, extra_specs, out_shapes, out_specs, epi)


def mm_tn(name, a, b, dw, s, tm=512, tn=512, tk=4096, a_spec=None, b_spec=None, a_fn=None):
    tm, tn, tk = _tile(dw.k, tm), _tile(dw.n, tn), _tile(s, tk)
    grid = (dw.k // tm, dw.n // tn, s // tk)
    a_spec = a_spec or pl.BlockSpec((tk, tm), lambda i, j, k: (k, i))
    b_spec = b_spec or pl.BlockSpec((tk, tn), lambda i, j, k: (k, j))

    def epi(acc, e, o):
        o[0][...] = acc.astype(BF16)

    out = _mm(name, TN, a, b, grid, a_spec, b_spec, (tm, tn), (), (), [jax.ShapeDtypeStruct(dw.shape, BF16)],
              [dw.spec(tm, tn, lambda i, j, k: (i, j))], epi, a_fn)[0]
    return out.reshape(N_CHIPS, dw.k // N_CHIPS, dw.n) if dw.plain else out


def _sds(shape, dtype):
    return jax.ShapeDtypeStruct(shape, dtype)


def _ij(tm, tn):
    return pl.BlockSpec((tm, tn), lambda i, j, k: (i, j))


def _i0(tm, c):
    return pl.BlockSpec((tm, c), lambda i, j, k: (i, 0))


def _0j(r, tn):
    return pl.BlockSpec((r, tn), lambda i, j, k: (0, j))


def _layer_norm_rows(r, g, b):
    mu = jnp.mean(r, axis=-1, keepdims=True)
    d = r - mu
    var = jnp.mean(d * d, axis=-1, keepdims=True)
    rstd = lax.rsqrt(var + LN_EPS)
    xh = d * rstd
    return xh * g + b, xh, rstd


def mm_residual_ln(name, a, w, x, g, b, bias=None, tm=512, tk=1024, a_fn=None):
    s, d = x.shape
    tm = _tile(s, tm)
    extras = [x, g, b] + ([bias] if bias is not None else [])
    especs = [_i0(tm, d), _0j(1, d), _0j(1, d)] + ([_0j(1, d)] if bias is not None else [])

    def epi(acc, e, o):
        r = ALPHA * e[0][...] + acc
        if bias is not None:
            r = r + e[3][...]
        y, xh, rstd = _layer_norm_rows(r, e[1][...], e[2][...])
        o[0][...] = y
        o[1][...] = y.astype(BF16)
        o[2][...] = xh
        o[3][...] = rstd

    return mm_nn(name, a, w, tm, d, tk, epi,
                 [_sds((s, d), F32), _sds((s, d), BF16), _sds((s, d), F32), _sds((s, 1), F32)],
                 [_i0(tm, d), _i0(tm, d), _i0(tm, d), _i0(tm, 1)], extras, especs, a_fn=a_fn)


def mm_plain_nn(name, a, w, out_dtype, tm=1024, tn=512, tk=1024, bias=None):
    m = a.shape[0]
    tm, tn = _tile(m, tm), _tile(w.n, tn)
    if w.kind == "col":
        tn = _tile(w.nloc, tn)

    def epi(acc, e, o):
        if bias is not None:
            acc = acc + e[0][...]
        o[0][...] = acc.astype(out_dtype)

    extras, especs = ([bias], [_0j(1, tn)]) if bias is not None else ((), ())
    return mm_nn(name, a, w, tm, tn, tk, epi, [_sds((m, w.n), out_dtype)], [_ij(tm, tn)], extras, especs)[0]


def mm_plain_nt(name, a, w, out_dtype, tm=1024, tn=512, tk=1024, add=None, add_scale=1.0, a_spec_fn=None):
    m = a.shape[0] if a_spec_fn is None else a_spec_fn[0]
    tm, tn = _tile(m, tm), _tile(w.k, tn)
    tk = _tile(w.n, tk)
    if w.kind == "col":
        tk = _tile(w.nloc, tk)
    if w.kind == "row" and not w.plain:
        tn = _tile(w.kloc, tn)

    def epi(acc, e, o):
        if add is not None:
            acc = acc + add_scale * e[0][...].astype(F32)
        o[0][...] = acc.astype(out_dtype)

    extras, especs = ([add], [_ij(tm, tn)]) if add is not None else ((), ())
    a_spec = None if a_spec_fn is None else a_spec_fn[1](tm, tk)
    return mm_nt(name, a, w, m, tm, tn, tk, epi, [_sds((m, w.k), out_dtype)], [_ij(tm, tn)], extras, especs,
                 a_spec=a_spec)[0]


def _rows(tm, c):
    return pl.BlockSpec((tm, c), lambda i: (i, 0))


def _fix(shape):
    nd = len(shape)
    return pl.BlockSpec(shape, lambda i: (0,) * nd)


def _accumulate(ref, val):
    @pl.when(pl.program_id(0) == 0)
    def _():
        ref[...] = jnp.zeros_like(ref)

    ref[...] += val


def ln_backward(name, dy, xhat, rstd, g, tm=256):
    s, d = dy.shape
    tm = _tile(s, tm)

    def body(dy_ref, xh_ref, rstd_ref, g_ref, dr_ref, drb_ref, dg_ref, db_ref, ds_ref):
        dyv, xh = dy_ref[...], xh_ref[...]
        dxh = dyv * g_ref[...]
        m1 = jnp.mean(dxh, axis=-1, keepdims=True)
        m2 = jnp.mean(dxh * xh, axis=-1, keepdims=True)
        dr = rstd_ref[...] * (dxh - m1 - xh * m2)
        dr_ref[...] = dr
        drb_ref[...] = dr.astype(BF16)
        _accumulate(dg_ref, jnp.sum(dyv * xh, axis=0, keepdims=True))
        _accumulate(db_ref, jnp.sum(dyv, axis=0, keepdims=True))
        _accumulate(ds_ref, jnp.sum(dr, axis=0, keepdims=True))

    return pl.pallas_call(
        body, grid=(s // tm,),
        in_specs=[_rows(tm, d), _rows(tm, d), _rows(tm, 1), _fix((1, d))],
        out_specs=[_rows(tm, d), _rows(tm, d), _fix((1, d)), _fix((1, d)), _fix((1, d))],
        out_shape=[_sds((s, d), F32), _sds((s, d), BF16), _sds((1, d), F32), _sds((1, d), F32), _sds((1, d), F32)],
        compiler_params=_params(("arbitrary",)), name=name)(dy, xhat, rstd, g)


def loss_head(y, target, tm=256):
    s, d = y.shape
    tm = _tile(s, tm)

    def body(y_ref, t_ref, dy_ref, loss_ref):
        e = y_ref[...] - t_ref[...]
        dy_ref[...] = e * (1.0 / d)
        part = 0.5 * jnp.sum(jnp.mean(e * e, axis=-1, keepdims=True), axis=0, keepdims=True)
        _accumulate(loss_ref, jnp.broadcast_to(part, (1, d)))

    return pl.pallas_call(
        body, grid=(s // tm,), in_specs=[_rows(tm, d), _rows(tm, d)],
        out_specs=[_rows(tm, d), _fix((1, d))], out_shape=[_sds((s, d), F32), _sds((1, d), F32)],
        compiler_params=_params(("arbitrary",)), name="loss_head")(y, target)


def _cols(s, tc, off=0):
    return pl.BlockSpec((s, tc), lambda i: (0, i + off))


def _shift_down(z, sft, rows):
    return jnp.where(rows >= sft, pltpu.roll(z, sft, 0), 0.0)


def _shift_up(z, sft, rows, s):
    return jnp.where(rows < s - sft, pltpu.roll(z, (s - sft) % s, 0), 0.0)


def short_conv_gate(u, conv_w, tc=256):
    s, d3 = u.shape
    d = d3 // 3
    nb = d // tc

    def body(b_ref, c_ref, h_ref, w_ref, o_ref):
        rows = lax.broadcasted_iota(jnp.int32, (s, tc), 0)
        z = c_ref[...] * h_ref[...]
        cz = jnp.zeros((s, tc), F32)
        for k in range(SC_WIDTH):
            sft = SC_WIDTH - 1 - k
            cz = cz + w_ref[pl.ds(k, 1), :] * (_shift_down(z, sft, rows) if sft else z)
        o_ref[...] = (b_ref[...] * cz).astype(BF16)

    return pl.pallas_call(
        body, grid=(nb,),
        in_specs=[_cols(s, tc), _cols(s, tc, nb), _cols(s, tc, 2 * nb), _cols(SC_WIDTH, tc)],
        out_specs=_cols(s, tc), out_shape=_sds((s, d), BF16),
        compiler_params=_params(("parallel",)), name="short_conv_gate")(u, u, u, conv_w)


def short_conv_gate_bwd(u, conv_w, dg, tc=256):
    s, d3 = u.shape
    d = d3 // 3
    nb = d // tc

    def body(b_ref, c_ref, h_ref, w_ref, dg_ref, du_ref, dw_ref):
        rows = lax.broadcasted_iota(jnp.int32, (s, tc), 0)
        c, h, dgv = c_ref[...], h_ref[...], dg_ref[...]
        z = c * h
        dcz = dgv * b_ref[...]
        cz = jnp.zeros((s, tc), F32)
        dz = jnp.zeros((s, tc), F32)
        for k in range(SC_WIDTH):
            sft = SC_WIDTH - 1 - k
            zs = _shift_down(z, sft, rows) if sft else z
            wk = w_ref[pl.ds(k, 1), :]
            cz = cz + wk * zs
            dz = dz + wk * (_shift_up(dcz, sft, rows, s) if sft else dcz)
            dw_ref[pl.ds(k, 1), :] = jnp.sum(dcz * zs, axis=0, keepdims=True)
        du_ref[0] = (dgv * cz).astype(BF16)
        du_ref[1] = (dz * h).astype(BF16)
        du_ref[2] = (dz * c).astype(BF16)

    return pl.pallas_call(
        body, grid=(nb,),
        in_specs=[_cols(s, tc), _cols(s, tc, nb), _cols(s, tc, 2 * nb), _cols(SC_WIDTH, tc), _cols(s, tc)],
        out_specs=[pl.BlockSpec((3, s, tc), lambda i: (0, 0, i)), _cols(SC_WIDTH, tc)],
        out_shape=[_sds((3, s, d), BF16), _sds((SC_WIDTH, d), F32)],
        compiler_params=_params(("parallel",)), name="short_conv_gate_bwd")(u, u, u, conv_w, dg)


def _store_shifted_down(ref, z, rows):
    s, tc = z.shape
    for b in range(8):
        ref[b, pl.ds(0, CONV_PAD), :] = jnp.zeros((CONV_PAD, tc), F32)
        ref[b, pl.ds(CONV_PAD, s), :] = z if b == 0 else _shift_down(z, b, rows)


def _store_shifted_up(ref, z, rows):
    s, tc = z.shape
    for b in range(8):
        ref[b, pl.ds(0, s), :] = z if b == 0 else _shift_up(z, b, rows, s)
        ref[b, pl.ds(s, CONV_PAD), :] = jnp.zeros((CONV_PAD, tc), F32)


def conformer_glu_conv(u, dw_w, dw_b, tc=128):
    s, d2 = u.shape
    d = d2 // 2
    nb = d // tc

    ch = min(CONV_CHUNK, s)

    def body(a_ref, g_ref, w_ref, b_ref, o_ref, down):
        rows = lax.broadcasted_iota(jnp.int32, (s, tc), 0)
        _store_shifted_down(down, a_ref[...] * jax.nn.sigmoid(g_ref[...]), rows)

        def chunk(ci, carry):
            r0 = pl.multiple_of(ci * ch, ch)
            acc = jnp.broadcast_to(b_ref[...], (ch, tc))
            for k in range(CONF_WIDTH):
                sft = CONF_WIDTH - 1 - k
                acc = acc + w_ref[pl.ds(k, 1), :] * down[sft % 8, pl.ds(CONV_PAD + r0 - (sft // 8) * 8, ch), :]
            o_ref[pl.ds(r0, ch), :] = acc
            return carry

        lax.fori_loop(0, s // ch, chunk, 0)

    return pl.pallas_call(
        body, grid=(nb,),
        in_specs=[_cols(s, tc), _cols(s, tc, nb), _cols(CONF_WIDTH, tc), _cols(1, tc)],
        out_specs=_cols(s, tc), out_shape=_sds((s, d), F32),
        scratch_shapes=[pltpu.VMEM((8, CONV_PAD + s, tc), F32)],
        compiler_params=_params(("parallel",)), name="conformer_glu_conv")(u, u, dw_w, dw_b)


def conformer_glu_conv_bwd(u, dw_w, dhc, tc=128):
    s, d2 = u.shape
    d = d2 // 2
    nb = d // tc
    ch = min(CONV_CHUNK, s)

    def body(a_ref, g_ref, w_ref, dhc_ref, du_ref, dbias_ref, dw_ref, db_ref, down, up, dw_acc, dh_buf):
        rows = lax.broadcasted_iota(jnp.int32, (s, tc), 0)
        a = a_ref[...]
        sg = jax.nn.sigmoid(g_ref[...])
        dhcv = dhc_ref[...]
        _store_shifted_down(down, a * sg, rows)
        _store_shifted_up(up, dhcv, rows)
        dw_acc[...] = jnp.zeros_like(dw_acc)

        def chunk(ci, carry):
            r0 = pl.multiple_of(ci * ch, ch)
            dc = dhc_ref[pl.ds(r0, ch), :]
            dh = jnp.zeros((ch, tc), F32)
            for k in range(CONF_WIDTH):
                sft = CONF_WIDTH - 1 - k
                a8, b = (sft // 8) * 8, sft % 8
                dh = dh + w_ref[pl.ds(k, 1), :] * up[b, pl.ds(r0 + a8, ch), :]
                prod = dc * down[b, pl.ds(CONV_PAD + r0 - a8, ch), :]
                dw_acc[k] += jnp.sum(prod.reshape(ch // 8, 8, tc), axis=0)
            dh_buf[pl.ds(r0, ch), :] = dh
            return carry

        lax.fori_loop(0, s // ch, chunk, 0)
        dh = dh_buf[...]
        da = dh * sg
        dgate = dh * a * sg * (1.0 - sg)
        du_ref[0] = da.astype(BF16)
        du_ref[1] = dgate.astype(BF16)
        dbias_ref[pl.ds(0, 1), :] = jnp.sum(da, axis=0, keepdims=True)
        dbias_ref[pl.ds(1, 1), :] = jnp.sum(dgate, axis=0, keepdims=True)
        db_ref[...] = jnp.sum(dhcv, axis=0, keepdims=True)
        for k in range(CONF_WIDTH):
            dw_ref[pl.ds(k, 1), :] = jnp.sum(dw_acc[k], axis=0, keepdims=True)

    return pl.pallas_call(
        body, grid=(nb,),
        in_specs=[_cols(s, tc), _cols(s, tc, nb), _cols(CONF_WIDTH, tc), _cols(s, tc)],
        out_specs=[pl.BlockSpec((2, s, tc), lambda i: (0, 0, i)), _cols(2, tc), _cols(CONF_WIDTH, tc), _cols(1, tc)],
        out_shape=[_sds((2, s, d), BF16), _sds((2, d), F32), _sds((CONF_WIDTH, d), F32), _sds((1, d), F32)],
        scratch_shapes=[pltpu.VMEM((8, CONV_PAD + s, tc), F32), pltpu.VMEM((8, CONV_PAD + s, tc), F32),
                        pltpu.VMEM((CONF_WIDTH + 1, 8, tc), F32), pltpu.VMEM((s, tc), F32)],
        compiler_params=_params(("parallel",)), name="conformer_glu_conv_bwd")(u, u, dw_w, dhc)


def conformer_norm_swish(hc, g, b, tm=256):
    s, d = hc.shape
    tm = _tile(s, tm)

    def body(h_ref, g_ref, b_ref, o_ref):
        n, _, _ = _layer_norm_rows(h_ref[...], g_ref[...], b_ref[...])
        o_ref[...] = (n * jax.nn.sigmoid(n)).astype(BF16)

    return pl.pallas_call(
        body, grid=(s // tm,), in_specs=[_rows(tm, d), _fix((1, d)), _fix((1, d))], out_specs=_rows(tm, d),
        out_shape=_sds((s, d), BF16), compiler_params=_params(("parallel",)), name="conformer_norm_swish")(hc, g, b)


def conformer_norm_swish_bwd(hc, g, b, ds, tm=256):
    s, d = hc.shape
    tm = _tile(s, tm)

    def body(h_ref, g_ref, b_ref, ds_ref, dh_ref, dg_ref, db_ref):
        n, nh, rstd = _layer_norm_rows(h_ref[...], g_ref[...], b_ref[...])
        sg = jax.nn.sigmoid(n)
        dn = ds_ref[...] * (sg * (1.0 + n * (1.0 - sg)))
        dnh = dn * g_ref[...]
        m1 = jnp.mean(dnh, axis=-1, keepdims=True)
        m2 = jnp.mean(dnh * nh, axis=-1, keepdims=True)
        dh_ref[...] = rstd * (dnh - m1 - nh * m2)
        _accumulate(dg_ref, jnp.sum(dn * nh, axis=0, keepdims=True))
        _accumulate(db_ref, jnp.sum(dn, axis=0, keepdims=True))

    return pl.pallas_call(
        body, grid=(s // tm,), in_specs=[_rows(tm, d), _fix((1, d)), _fix((1, d)), _rows(tm, d)],
        out_specs=[_rows(tm, d), _fix((1, d)), _fix((1, d))],
        out_shape=[_sds((s, d), F32), _sds((1, d), F32), _sds((1, d), F32)],
        compiler_params=_params(("arbitrary",)), name="conformer_norm_swish_bwd")(hc, g, b, ds)


def _swap_halves(x):
    lane = lax.broadcasted_iota(jnp.int32, x.shape, 1)
    return jnp.where(lane < QK_ROPE // 2, pltpu.roll(x, 128 - QK_ROPE // 2, 1), pltpu.roll(x, QK_ROPE // 2, 1))


def _rope(x, cf, sf):
    return x * cf + _swap_halves(x) * sf


def _unrope(dx, cf, sf):
    return dx * cf - _swap_halves(dx) * sf


def _rms_rows(x, g):
    r = lax.rsqrt(jnp.mean(x * x, axis=-1, keepdims=True) + RMS_EPS)
    return x * r, r


def mla_latents(t, g_q, g_kv, cf, sf, tm=256):
    s = t.shape[0]
    tm = _tile(s, tm)

    def body(t_ref, gq_ref, gkv_ref, cf_ref, sf_ref, cq_ref, ckv_ref, kpe_ref):
        xq, _ = _rms_rows(t_ref[:, 0:Q_LORA], gq_ref[...])
        cq_ref[...] = (xq * gq_ref[...]).astype(BF16)
        xkv, _ = _rms_rows(t_ref[:, Q_LORA:Q_LORA + KV_LORA], gkv_ref[...])
        ckv_ref[...] = (xkv * gkv_ref[...]).astype(BF16)
        kpe_ref[...] = _rope(t_ref[:, Q_LORA + KV_LORA:], cf_ref[...], sf_ref[...]).astype(BF16)

    w = Q_LORA + KV_LORA + 128
    return pl.pallas_call(
        body, grid=(s // tm,),
        in_specs=[_rows(tm, w), _fix((1, Q_LORA)), _fix((1, KV_LORA)), _rows(tm, 128), _rows(tm, 128)],
        out_specs=[_rows(tm, Q_LORA), _rows(tm, KV_LORA), _rows(tm, 128)],
        out_shape=[_sds((s, Q_LORA), BF16), _sds((s, KV_LORA), BF16), _sds((s, 128), BF16)],
        compiler_params=_params(("parallel",)), name="mla_latents")(t, g_q, g_kv, cf, sf)


def mla_latents_bwd(t, g_q, g_kv, cf, sf, dcq, dckv, dkpe, tm=256):
    s = t.shape[0]
    tm = _tile(s, tm)
    w = Q_LORA + KV_LORA + 128

    def rms_bwd(x, g, dy):
        xh, r = _rms_rows(x, g)
        dxh = dy * g
        return r * (dxh - xh * jnp.mean(dxh * xh, axis=-1, keepdims=True)), jnp.sum(dy * xh, axis=0, keepdims=True)

    def body(t_ref, gq_ref, gkv_ref, cf_ref, sf_ref, dcq_ref, dckv_ref, dkpe_ref, dt_ref, dgq_ref, dgkv_ref):
        dxq, dgq = rms_bwd(t_ref[:, 0:Q_LORA], gq_ref[...], dcq_ref[...])
        dxkv, dgkv = rms_bwd(t_ref[:, Q_LORA:Q_LORA + KV_LORA], gkv_ref[...], dckv_ref[...])
        dt_ref[:, 0:Q_LORA] = dxq.astype(BF16)
        dt_ref[:, Q_LORA:Q_LORA + KV_LORA] = dxkv.astype(BF16)
        dt_ref[:, Q_LORA + KV_LORA:] = _unrope(dkpe_ref[...], cf_ref[...], sf_ref[...]).astype(BF16)
        _accumulate(dgq_ref, dgq)
        _accumulate(dgkv_ref, dgkv)

    return pl.pallas_call(
        body, grid=(s // tm,),
        in_specs=[_rows(tm, w), _fix((1, Q_LORA)), _fix((1, KV_LORA)), _rows(tm, 128), _rows(tm, 128),
                  _rows(tm, Q_LORA), _rows(tm, KV_LORA), _rows(tm, 128)],
        out_specs=[_rows(tm, w), _fix((1, Q_LORA)), _fix((1, KV_LORA))],
        out_shape=[_sds((s, w), BF16), _sds((1, Q_LORA), F32), _sds((1, KV_LORA), F32)],
        compiler_params=_params(("arbitrary",)), name="mla_latents_bwd")(t, g_q, g_kv, cf, sf, dcq, dckv, dkpe)


def mla_queries(cq, w_uq, cf, sf, tm=512):
    s = cq.shape[0]
    tm = _tile(s, tm)

    def epi(acc, e, o):
        o[0][:, 0:QK_NOPE] = acc[:, 0:QK_NOPE].astype(BF16)
        o[0][:, QK_NOPE:] = _rope(acc[:, QK_NOPE:], e[0][...], e[1][...]).astype(BF16)

    return mm_nn("mla_queries", cq, w_uq, tm, HEAD_PAD, Q_LORA, epi, [_sds((s, N_HEADS * HEAD_PAD), BF16)],
                 [_ij(tm, HEAD_PAD)], [cf, sf], [_i0(tm, 128), _i0(tm, 128)])[0]


def mla_keys(ckv, w_uk, kpe, tm=512):
    s = ckv.shape[0]
    tm = _tile(s, tm)

    def epi(acc, e, o):
        o[0][:, 0:QK_NOPE] = acc.astype(BF16)
        o[0][:, QK_NOPE:] = e[0][...]

    return mm_nn("mla_keys", ckv, w_uk, tm, QK_NOPE, KV_LORA, epi, [_sds((s, N_HEADS * HEAD_PAD), BF16)],
                 [_ij(tm, HEAD_PAD)], [kpe], [_i0(tm, 128)])[0]


def _masked_scores(q, k, qi, tq, kv):
    sc = lax.dot_general(q, k, NT, preferred_element_type=F32) * ATTN_SCALE
    row = lax.broadcasted_iota(jnp.int32, (tq, kv), 0) + qi * tq
    col = lax.broadcasted_iota(jnp.int32, (tq, kv), 1)
    ok = lax.shift_right_logical(col, CHUNK_SHIFT) <= lax.shift_right_logical(row, CHUNK_SHIFT)
    return jnp.where(ok, sc, -1e30)


def attention(q, k, v, tq=256):
    s = q.shape[0]
    tq = _tile(s, tq)
    nq = s // tq

    def body(q_ref, k_ref, v_ref, o_ref):
        for qi in range(nq):
            kv = (qi + 1) * tq
            sc = _masked_scores(q_ref[pl.ds(qi * tq, tq), :], k_ref[pl.ds(0, kv), :], qi, tq, kv)
            p = jnp.exp(sc - jnp.max(sc, axis=-1, keepdims=True))
            o = lax.dot_general(p.astype(BF16), v_ref[pl.ds(0, kv), :], NN, preferred_element_type=F32)
            o_ref[pl.ds(qi * tq, tq), :] = (o / jnp.sum(p, axis=-1, keepdims=True)).astype(BF16)

    hq = pl.BlockSpec((s, HEAD_PAD), lambda h: (0, h))
    hv = pl.BlockSpec((s, V_HEAD), lambda h: (0, h))
    return pl.pallas_call(
        body, grid=(N_HEADS,), in_specs=[hq, hq, hv], out_specs=hv, out_shape=_sds((s, N_HEADS * V_HEAD), BF16),
        compiler_params=_params(("parallel",)), name="attention")(q, k, v)


def attention_bwd(q, k, v, do, tq=256):
    s = q.shape[0]
    tq = _tile(s, tq)
    nq = s // tq

    def body(q_ref, k_ref, v_ref, do_ref, dq_ref, dk_ref, dv_ref, dk_acc, dv_acc):
        dk_acc[...] = jnp.zeros_like(dk_acc)
        dv_acc[...] = jnp.zeros_like(dv_acc)
        for qi in range(nq):
            kv = (qi + 1) * tq
            qt = q_ref[pl.ds(qi * tq, tq), :]
            kt = k_ref[pl.ds(0, kv), :]
            dot = do_ref[pl.ds(qi * tq, tq), :]
            sc = _masked_scores(qt, kt, qi, tq, kv)
            p = jnp.exp(sc - jnp.max(sc, axis=-1, keepdims=True))
            p = p / jnp.sum(p, axis=-1, keepdims=True)
            dp = lax.dot_general(dot, v_ref[pl.ds(0, kv), :], NT, preferred_element_type=F32)
            delta = jnp.sum(p * dp, axis=-1, keepdims=True)
            ds = (p * (dp - delta) * ATTN_SCALE).astype(BF16)
            dq_ref[pl.ds(qi * tq, tq), :] = lax.dot_general(ds, kt, NN, preferred_element_type=F32).astype(BF16)
            dk_acc[pl.ds(0, kv), :] += lax.dot_general(ds, qt, TN, preferred_element_type=F32)
            dv_acc[pl.ds(0, kv), :] += lax.dot_general(p.astype(BF16), dot, TN, preferred_element_type=F32)
        dk_ref[...] = dk_acc[...].astype(BF16)
        dv_ref[...] = dv_acc[...].astype(BF16)

    hq = pl.BlockSpec((s, HEAD_PAD), lambda h: (0, h))
    hv = pl.BlockSpec((s, V_HEAD), lambda h: (0, h))
    return pl.pallas_call(
        body, grid=(N_HEADS,), in_specs=[hq, hq, hv, hv], out_specs=[hq, hq, hv],
        out_shape=[_sds((s, N_HEADS * HEAD_PAD), BF16), _sds((s, N_HEADS * HEAD_PAD), BF16),
                   _sds((s, N_HEADS * V_HEAD), BF16)],
        scratch_shapes=[pltpu.VMEM((s, HEAD_PAD), F32), pltpu.VMEM((s, V_HEAD), F32)],
        compiler_params=_params(("parallel",)), name="attention_bwd")(q, k, v, do)


def mla_unrope_grads(dq, dk, cf, sf, tm=256):
    s = dq.shape[0]
    tm = _tile(s, tm)

    def body(dq_ref, dk_ref, cf_ref, sf_ref, dql_ref, dkn_ref, dkpe_ref):
        cfv, sfv = cf_ref[...], sf_ref[...]
        dkpe = jnp.zeros((tm, 128), F32)
        for h in range(N_HEADS):
            lo = h * HEAD_PAD
            dql_ref[:, lo:lo + QK_NOPE] = dq_ref[:, lo:lo + QK_NOPE]
            dql_ref[:, lo + QK_NOPE:lo + HEAD_PAD] = _unrope(
                dq_ref[:, lo + QK_NOPE:lo + HEAD_PAD].astype(F32), cfv, sfv).astype(BF16)
            dkn_ref[:, h * QK_NOPE:(h + 1) * QK_NOPE] = dk_ref[:, lo:lo + QK_NOPE]
            dkpe = dkpe + dk_ref[:, lo + QK_NOPE:lo + HEAD_PAD].astype(F32)
        dkpe_ref[...] = dkpe

    wq = N_HEADS * HEAD_PAD
    return pl.pallas_call(
        body, grid=(s // tm,), in_specs=[_rows(tm, wq), _rows(tm, wq), _rows(tm, 128), _rows(tm, 128)],
        out_specs=[_rows(tm, wq), _rows(tm, N_HEADS * QK_NOPE), _rows(tm, 128)],
        out_shape=[_sds((s, wq), BF16), _sds((s, N_HEADS * QK_NOPE), BF16), _sds((s, 128), F32)],
        compiler_params=_params(("parallel",)), name="mla_unrope_grads")(dq, dk, cf, sf)


ANY = pl.BlockSpec(memory_space=pl.ANY)
GATHER_ID = 1
CHIP_EXCHANGE_ID = 2
PAIR_ID = 3
ALL_ID = 4


def _nbytes(a):
    return a.size * a.dtype.itemsize


def _copy_cost(operand_bytes, sent_fraction):
    sent = int(operand_bytes * sent_fraction)
    return pl.CostEstimate(flops=0, transcendentals=0, bytes_accessed=2 * sent, remote_bytes_transferred=sent)


def _handshake(peers):
    barrier = pltpu.get_barrier_semaphore()
    for peer in peers:
        pl.semaphore_signal(barrier, inc=1, device_id=peer, device_id_type=MESH)
    pl.semaphore_wait(barrier, len(peers))


def _place():
    x, y, c = lax.axis_index("x"), lax.axis_index("y"), lax.axis_index("c")
    chips = [(1 - x, y), (x, 1 - y), (1 - x, 1 - y)]
    return x, y, c, chips


def _half(ref, hc, axis=0):
    n = ref.shape[axis] // 2
    idx = (slice(None),) * axis + (pl.ds(hc * n, n),)
    return ref.at[idx]


def gather_shards(name, tensors, by_columns=()):
    nt = len(tensors)

    def body(*refs):
        a, g = refs[:nt], refs[nt:2 * nt]
        send, recv = refs[2 * nt:]
        x, y, c, chips = _place()
        q = 2 * x + y
        sib = (x, y, 1 - c)
        _handshake([sib] + [(*chip, c) for chip in chips])

        def whole(t, p):
            if t in by_columns:
                n = a[t].shape[1]
                return g[t].at[:, pl.ds(p * n, n)]
            return g[t].at[p]

        def slot(t, chip, hc):
            return _half(whole(t, 2 * chip[0] + chip[1]), hc)

        def rc(t, k, src, dst, to):
            return pltpu.make_async_remote_copy(src_ref=src, dst_ref=dst, send_sem=send.at[t, k], recv_sem=recv.at[t, k],
                                                device_id=to, device_id_type=MESH)

        sent = []
        for t in range(nt):
            cp = rc(t, 6, a[t], whole(t, q), sib)
            cp.start()
            sent.append(cp)
            for j, chip in enumerate(chips):
                cp = rc(t, j, _half(a[t], c), slot(t, (x, y), c), (*chip, c))
                cp.start()
                sent.append(cp)
        for t in range(nt):
            for j, chip in enumerate(chips):
                landed = slot(t, chip, c)
                rc(t, j, landed, landed, (*chip, c)).wait_recv()
                cp = rc(t, 3 + j, landed, landed, sib)
                cp.start()
                sent.append(cp)
        for t in range(nt):
            for j, chip in enumerate(chips):
                other = slot(t, chip, 1 - c)
                rc(t, 3 + j, other, other, sib).wait_recv()
            own = whole(t, q)
            rc(t, 6, own, own, sib).wait_recv()
        for cp in sent:
            cp.wait_send()

    return pl.kernel(
        body, name=name,
        out_type=[_sds((a.shape[0], N_CHIPS * a.shape[1]) if t in by_columns else (N_CHIPS,) + a.shape, a.dtype)
                  for t, a in enumerate(tensors)],
        mesh=plsc.ScalarSubcoreMesh(axis_name="sequencer", num_cores=1),
        scratch_types=[pltpu.SemaphoreType.DMA((nt, 7)), pltpu.SemaphoreType.DMA((nt, 7))],
        cost_estimate=_copy_cost(sum(_nbytes(a) for a in tensors), 4),
        compiler_params=pltpu.CompilerParams(collective_id=GATHER_ID))(*tensors)


def pair_exchange(name, grads):
    nt = len(grads)

    def body(*refs):
        g, theirs = refs[:nt], refs[nt:2 * nt]
        send, recv = refs[2 * nt:]
        x, y, c, _ = _place()
        cps = []
        for t in range(nt):
            cp = pltpu.make_async_remote_copy(src_ref=_half(g[t], 1 - c, 1), dst_ref=theirs[t], send_sem=send.at[t],
                                              recv_sem=recv.at[t], device_id=(x, y, 1 - c), device_id_type=MESH)
            cp.start()
            cps.append(cp)
        for cp in cps:
            cp.wait()

    return pl.pallas_call(
        body, in_specs=[ANY] * nt, out_specs=[ANY] * nt,
        out_shape=[_sds((N_CHIPS, a.shape[1] // 2, a.shape[2]), a.dtype) for a in grads],
        scratch_shapes=[pltpu.SemaphoreType.DMA((nt,)), pltpu.SemaphoreType.DMA((nt,))],
        name=name)(*grads)


def chip_exchange(name, parts):
    nt = len(parts)

    def body(*refs):
        a, r = refs[:nt], refs[nt:2 * nt]
        send, recv = refs[2 * nt:]
        x, y, c, chips = _place()
        _handshake([(*chip, c) for chip in chips])
        cps = []
        for t in range(nt):
            for j, chip in enumerate(chips):
                cp = pltpu.make_async_remote_copy(
                    src_ref=a[t].at[2 * chip[0] + chip[1]], dst_ref=r[t].at[j], send_sem=send.at[t, j],
                    recv_sem=recv.at[t, j], device_id=(*chip, c), device_id_type=MESH)
                cp.start()
                cps.append(cp)
        for cp in cps:
            cp.wait()

    return pl.kernel(
        body, name=name, out_type=[_sds((N_CHIPS - 1,) + a.shape[1:], a.dtype) for a in parts],
        mesh=plsc.ScalarSubcoreMesh(axis_name="sequencer", num_cores=1),
        scratch_types=[pltpu.SemaphoreType.DMA((nt, 3)), pltpu.SemaphoreType.DMA((nt, 3))],
        cost_estimate=_copy_cost(sum(_nbytes(a) for a in parts), 0.75),
        compiler_params=pltpu.CompilerParams(collective_id=CHIP_EXCHANGE_ID))(*parts)


def pair_share(name, halves):
    nt = len(halves)

    def body(*refs):
        h, other = refs[:nt], refs[nt:2 * nt]
        send, recv = refs[2 * nt:]
        x, y, c, _ = _place()
        _handshake([(x, y, 1 - c)])
        cps = []
        for t in range(nt):
            cp = pltpu.make_async_remote_copy(src_ref=h[t], dst_ref=other[t], send_sem=send.at[t], recv_sem=recv.at[t],
                                              device_id=(x, y, 1 - c), device_id_type=MESH)
            cp.start()
            cps.append(cp)
        for cp in cps:
            cp.wait()

    return pl.kernel(
        body, name=name, out_type=[_sds(a.shape, a.dtype) for a in halves],
        mesh=plsc.ScalarSubcoreMesh(axis_name="sequencer", num_cores=1),
        scratch_types=[pltpu.SemaphoreType.DMA((nt,)), pltpu.SemaphoreType.DMA((nt,))],
        cost_estimate=_copy_cost(sum(_nbytes(a) for a in halves), 1),
        compiler_params=pltpu.CompilerParams(collective_id=PAIR_ID))(*halves)


def all_reduce_small(parts, rows):
    cdim = parts[0].shape[1]
    n = len(parts)
    vm = pl.BlockSpec(memory_space=pltpu.VMEM)

    def pack(*refs):
        p, o_ref = refs[:n], refs[n]
        at = 0
        for ref in p:
            o_ref[pl.ds(at, ref.shape[0]), :] = ref[...]
            at += ref.shape[0]
        o_ref[pl.ds(at, rows - at), :] = jnp.zeros((rows - at, cdim), F32)

    mine = pl.pallas_call(pack, in_specs=[vm] * n, out_specs=vm, out_shape=_sds((rows, cdim), F32), name="small_pack")(*parts)

    def exchange(mine_ref, buf, send, recv, lsem):
        x, y, c, _ = _place()
        me = 4 * x + 2 * y + c
        peers = [(x ^ (k >> 2), y ^ ((k >> 1) & 1), c ^ (k & 1)) for k in range(1, 8)]
        _handshake(peers)
        own = pltpu.make_async_copy(mine_ref, buf.at[me], lsem)
        own.start()
        cps = []
        for k, to in enumerate(peers):
            cp = pltpu.make_async_remote_copy(src_ref=mine_ref, dst_ref=buf.at[me], send_sem=send.at[k], recv_sem=recv.at[k],
                                              device_id=to, device_id_type=MESH)
            cp.start()
            cps.append(cp)
        for k, (px, py, pc) in enumerate(peers):
            pltpu.make_async_remote_copy(src_ref=mine_ref, dst_ref=buf.at[4 * px + 2 * py + pc], send_sem=send.at[k],
                                         recv_sem=recv.at[k], device_id=(x, y, c), device_id_type=MESH).wait_recv()
        for cp in cps:
            cp.wait_send()
        own.wait()

    landed = pl.kernel(
        exchange, name="small_exchange", out_type=_sds((8, rows, cdim), F32),
        mesh=plsc.ScalarSubcoreMesh(axis_name="sequencer", num_cores=1),
        scratch_types=[pltpu.SemaphoreType.DMA((7,)), pltpu.SemaphoreType.DMA((7,)), pltpu.SemaphoreType.DMA],
        cost_estimate=_copy_cost(rows * cdim * 4, 7),
        compiler_params=pltpu.CompilerParams(collective_id=ALL_ID))(mine)

    def total(buf, o_ref):
        acc = buf[0]
        for d in range(1, 8):
            acc = acc + buf[d]
        o_ref[...] = acc

    return pl.pallas_call(total, in_specs=[vm], out_specs=vm, out_shape=_sds((rows, cdim), F32), name="small_sum")(landed)


def pair_sum(g, theirs, core, tm=256):
    _, r, c = g.shape
    tm = _tile(r // 2, tm)
    nh = r // 2 // tm

    def body(core_ref, a_ref, b_ref, o_ref):
        o_ref[...] = (a_ref[...].astype(F32) + b_ref[...].astype(F32)).astype(BF16)

    blk = (N_CHIPS, tm, c)
    return pl.pallas_call(
        body, grid_spec=pltpu.PrefetchScalarGridSpec(
            num_scalar_prefetch=1, grid=(nh,),
            in_specs=[pl.BlockSpec(blk, lambda i, cr: (0, cr[0] * nh + i, 0)), pl.BlockSpec(blk, lambda i, cr: (0, i, 0))],
            out_specs=pl.BlockSpec(blk, lambda i, cr: (0, i, 0))),
        out_shape=_sds(theirs.shape, BF16), compiler_params=_params(("parallel",)), name="pair_sum")(core, g, theirs)


def chip_sum(own, landed, chip, stack, layer, layers, tm=256):
    _, r, c = own.shape
    tm = _tile(r, tm)

    def body(chip_ref, own_ref, l_ref, *rest):
        acc = own_ref[...].astype(F32)
        for j in range(N_CHIPS - 1):
            acc = acc + l_ref[j].astype(F32)
        rest[-1][...] = acc

    in_specs = [pl.BlockSpec((None, tm, c), lambda i, qr: (qr[0], i, 0)),
                pl.BlockSpec((N_CHIPS - 1, tm, c), lambda i, qr: (0, i, 0))]
    args = [chip, own, landed]
    if stack is not None:
        in_specs.append(ANY)
        args.append(stack)
    return pl.pallas_call(
        body, grid_spec=pltpu.PrefetchScalarGridSpec(
            num_scalar_prefetch=1, grid=(r // tm,), in_specs=in_specs,
            out_specs=pl.BlockSpec((None, tm, c), lambda i, qr: (layer, i, 0))),
        out_shape=_sds((layers, r, c), F32), input_output_aliases={3: 0} if stack is not None else {},
        compiler_params=_params(("parallel",)), name="chip_sum")(*args)


def adamw_joined(w, m, v, g_mine, g_theirs, core, tm=256):
    nl, r, c = w.shape
    tm = _tile(r // 2, tm)
    nh = r // 2 // tm
    bc1 = 1.0 - ADAM_B1 ** ADAM_STEP
    bc2 = 1.0 - ADAM_B2 ** ADAM_STEP

    def body(core_ref, w_ref, m_ref, v_ref, gm_ref, gt_ref, g_ref, d_ref, nm_ref, nv_ref):
        mine = (pl.program_id(1) // nh) == core_ref[0]
        gv = jnp.where(mine, gm_ref[...], gt_ref[...])
        nm = ADAM_B1 * m_ref[...] + (1.0 - ADAM_B1) * gv
        nv = ADAM_B2 * v_ref[...] + (1.0 - ADAM_B2) * (gv * gv)
        g_ref[...] = gv
        d_ref[...] = -ADAM_LR * ((nm / bc1) / (jnp.sqrt(nv / bc2) + ADAM_EPS) + ADAM_WD * w_ref[...])
        nm_ref[...] = nm
        nv_ref[...] = nv

    full = pl.BlockSpec((None, tm, c), lambda l, i, cr: (l, i, 0))
    half = pl.BlockSpec((None, tm, c), lambda l, i, cr: (l, i % nh, 0))
    return pl.pallas_call(
        body, grid_spec=pltpu.PrefetchScalarGridSpec(
            num_scalar_prefetch=1, grid=(nl, r // tm), in_specs=[full, full, full, half, half], out_specs=[full] * 4),
        out_shape=[_sds((nl, r, c), F32)] * 4, compiler_params=_params(("parallel", "parallel")),
        name="adamw_joined")(core, w, m, v, g_mine, g_theirs)


def adamw(w, g, m, v, tm=256):
    shape = w.shape
    c = shape[-1]
    r = w.size // c
    tm = _tile(r, tm)
    bc1 = 1.0 - ADAM_B1 ** ADAM_STEP
    bc2 = 1.0 - ADAM_B2 ** ADAM_STEP

    def body(w_ref, g_ref, m_ref, v_ref, d_ref, nm_ref, nv_ref):
        gv = g_ref[...]
        nm = ADAM_B1 * m_ref[...] + (1.0 - ADAM_B1) * gv
        nv = ADAM_B2 * v_ref[...] + (1.0 - ADAM_B2) * (gv * gv)
        d_ref[...] = -ADAM_LR * ((nm / bc1) / (jnp.sqrt(nv / bc2) + ADAM_EPS) + ADAM_WD * w_ref[...])
        nm_ref[...] = nm
        nv_ref[...] = nv

    outs = pl.pallas_call(
        body, grid=(r // tm,), in_specs=[_rows(tm, c)] * 4, out_specs=[_rows(tm, c)] * 3,
        out_shape=[_sds((r, c), F32)] * 3, compiler_params=_params(("parallel",)), name="adamw")(
            w.reshape(r, c), g.reshape(r, c), m.reshape(r, c), v.reshape(r, c))
    return [o.reshape(shape) for o in outs]


WEIGHTS = ['sc_w_in', 'sc_conv_w', 'sc_w_out', 'mla_w_dq', 'mla_g_q', 'mla_w_uq', 'mla_w_dkv', 'mla_g_kv', 'mla_w_uk',
           'mla_w_uv', 'mla_w_o', 'cf_w_pw1', 'cf_b_pw1', 'cf_dw_w', 'cf_dw_b', 'cf_norm_g', 'cf_norm_b', 'cf_w_pw2',
           'cf_b_pw2', 'ff_w1', 'ff_w2', 'ln_mix_g', 'ln_mix_b', 'ln_ff_g', 'ln_ff_b']
ARGS = ['x'] + WEIGHTS + ['loss_target'] + ['m_' + n for n in WEIGHTS] + ['v_' + n for n in WEIGHTS]


def _sq_relu(h):
    r = jnp.maximum(h.astype(F32), 0.0)
    return (r * r).astype(BF16)


def _mlp_forward(i, x, xb, w1, w2, g, b):
    hb = mm_plain_nn(f"mlp{i}_up", xb, w1, BF16)
    y, yb, xh, rstd = mm_residual_ln(f"mlp{i}_down_ln", hb, w2, x, g, b, a_fn=_sq_relu)
    return (y, yb), dict(xb=xb, hb=hb, xh=xh, rstd=rstd, g=g)


def _mlp_backward(i, dy, sv, w1, w2, dw1, dw2, reduce_after):
    s = dy.shape[0]
    dr, drb, dg, db, _ = ln_backward(f"mlp{i}_ln_bwd", dy, sv["xh"], sv["rstd"], sv["g"])
    tm, tn = _tile(s, 1024), 512

    def epi(acc, e, o):
        o[0][...] = (acc * (2.0 * jnp.maximum(e[0][...].astype(F32), 0.0))).astype(BF16)

    dhb = mm_nt(f"mlp{i}_down_bwd", drb, w2, s, tm, tn, 1024, epi, [_sds((s, w2.k), BF16)], [_ij(tm, tn)],
                [sv["hb"]], [_ij(tm, tn)])[0]
    g_w2 = mm_tn(f"mlp{i}_dw2", sv["hb"], drb, dw2, s, 512, 1024, a_fn=_sq_relu)
    g_w1 = mm_tn(f"mlp{i}_dw1", sv["xb"], dhb, dw1, s, 1024, 512)
    dhb = reduce_after(dhb, {f"w1_{i}": g_w1, f"w2_{i}": g_w2})
    dx = mm_plain_nt(f"mlp{i}_up_bwd", dhb, w1, F32, tn=1024, add=dr, add_scale=ALPHA)
    return dx, dg, db


def kernel(x, sc_w_in, sc_conv_w, sc_w_out, mla_w_dq, mla_g_q, mla_w_uq, mla_w_dkv, mla_g_kv, mla_w_uk, mla_w_uv, mla_w_o, cf_w_pw1, cf_b_pw1, cf_dw_w, cf_dw_b, cf_norm_g, cf_norm_b, cf_w_pw2, cf_b_pw2, ff_w1, ff_w2, ln_mix_g, ln_mix_b, ln_ff_g, ln_ff_b, loss_target, m_sc_w_in, m_sc_conv_w, m_sc_w_out, m_mla_w_dq, m_mla_g_q, m_mla_w_uq, m_mla_w_dkv, m_mla_g_kv, m_mla_w_uk, m_mla_w_uv, m_mla_w_o, m_cf_w_pw1, m_cf_b_pw1, m_cf_dw_w, m_cf_dw_b, m_cf_norm_g, m_cf_norm_b, m_cf_w_pw2, m_cf_b_pw2, m_ff_w1, m_ff_w2, m_ln_mix_g, m_ln_mix_b, m_ln_ff_g, m_ln_ff_b, v_sc_w_in, v_sc_conv_w, v_sc_w_out, v_mla_w_dq, v_mla_g_q, v_mla_w_uq, v_mla_w_dkv, v_mla_g_kv, v_mla_w_uk, v_mla_w_uv, v_mla_w_o, v_cf_w_pw1, v_cf_b_pw1, v_cf_dw_w, v_cf_dw_b, v_cf_norm_g, v_cf_norm_b, v_cf_w_pw2, v_cf_b_pw2, v_ff_w1, v_ff_w2, v_ln_mix_g, v_ln_mix_b, v_ln_ff_g, v_ln_ff_b):
    given = dict(zip(ARGS, (x, sc_w_in, sc_conv_w, sc_w_out, mla_w_dq, mla_g_q, mla_w_uq, mla_w_dkv, mla_g_kv, mla_w_uk, mla_w_uv, mla_w_o, cf_w_pw1, cf_b_pw1, cf_dw_w, cf_dw_b, cf_norm_g, cf_norm_b, cf_w_pw2, cf_b_pw2, ff_w1, ff_w2, ln_mix_g, ln_mix_b, ln_ff_g, ln_ff_b, loss_target, m_sc_w_in, m_sc_conv_w, m_sc_w_out, m_mla_w_dq, m_mla_g_q, m_mla_w_uq, m_mla_w_dkv, m_mla_g_kv, m_mla_w_uk, m_mla_w_uv, m_mla_w_o, m_cf_w_pw1, m_cf_b_pw1, m_cf_dw_w, m_cf_dw_b, m_cf_norm_g, m_cf_norm_b, m_cf_w_pw2, m_cf_b_pw2, m_ff_w1, m_ff_w2, m_ln_mix_g, m_ln_mix_b, m_ln_ff_g, m_ln_ff_b, v_sc_w_in, v_sc_conv_w, v_sc_w_out, v_mla_w_dq, v_mla_g_q, v_mla_w_uq, v_mla_w_dkv, v_mla_g_kv, v_mla_w_uk, v_mla_w_uv, v_mla_w_o, v_cf_w_pw1, v_cf_b_pw1, v_cf_dw_w, v_cf_dw_b, v_cf_norm_g, v_cf_norm_b, v_cf_w_pw2, v_cf_b_pw2, v_ff_w1, v_ff_w2, v_ln_mix_g, v_ln_mix_b, v_ln_ff_g, v_ln_ff_b)))
    s, d = x.shape[1], x.shape[2]
    d_ff = 4 * d
    dq4 = d // N_CHIPS
    xq = lax.axis_index("x") * 2 + lax.axis_index("y")

    w_dkv_pad = jnp.pad(mla_w_dkv[0], ((0, 0), (0, 128 - QK_ROPE)))
    w_uq_pad = jnp.pad(mla_w_uq[0].reshape(Q_LORA, 2, QK_NOPE + QK_ROPE), ((0, 0), (0, 0), (0, HEAD_PAD - QK_NOPE - QK_ROPE)))
    small = jnp.concatenate([
        sc_conv_w.reshape(2 * SC_WIDTH, dq4), cf_b_pw1.reshape(2, dq4), cf_dw_w[0], cf_dw_b, cf_norm_g, cf_norm_b,
        cf_b_pw2, jnp.zeros((5, dq4), F32)], axis=0)
    mlp_w = lambda i: [ff_w1[i].astype(BF16), ff_w2[i].astype(BF16)]
    g_in, g_out, g_w1, g_w2 = [None] * 2, [None] * 2, [None] * DEPTH, [None] * DEPTH
    g_in[0], g_out[0], g_small = gather_shards(
        "gather_mixer0", [sc_w_in[0].astype(BF16), sc_w_out[0].astype(BF16), small], by_columns=(0,))
    (g_w1[0],) = gather_shards("gather_up0", [ff_w1[0].astype(BF16)], by_columns=(0,))
    (g_w2[0],) = gather_shards("gather_down0", [ff_w2[0].astype(BF16)])
    g_dqkv, g_uq, g_uk, g_uv, g_o = gather_shards("gather_mixer1", [
        jnp.concatenate([mla_w_dq[0], w_dkv_pad], axis=1).astype(BF16),
        w_uq_pad.reshape(Q_LORA, 2 * HEAD_PAD).astype(BF16),
        mla_w_uk.reshape(KV_LORA // N_CHIPS, N_HEADS * QK_NOPE).astype(BF16),
        mla_w_uv.reshape(KV_LORA // N_CHIPS, N_HEADS * V_HEAD).astype(BF16), mla_w_o[0].astype(BF16)], by_columns=(1,))
    g_w1[1], g_w2[1] = gather_shards("gather_mlp1", mlp_w(1), by_columns=(0,))
    g_pw1, g_pw2, g_w1[2], g_w2[2] = gather_shards(
        "gather_layer2", [cf_w_pw1[0].astype(BF16), cf_w_pw2[0].astype(BF16)] + mlp_w(2), by_columns=(0, 2))
    g_in[1], g_out[1], g_w1[3], g_w2[3] = gather_shards(
        "gather_layer3", [sc_w_in[1].astype(BF16), sc_w_out[1].astype(BF16)] + mlp_w(3), by_columns=(0, 2))

    wd_t = Q_LORA + KV_LORA + 128
    w_in = [Stk("full", d, 3 * d, g_in[j]) for j in range(2)]
    w_out = [Stk("row", d, d, g_out[j]) for j in range(2)]
    w_dqkv = Stk("row", d, wd_t, g_dqkv)
    w_uq = Stk("full", Q_LORA, N_HEADS * HEAD_PAD, g_uq)
    w_uk = Stk("row", KV_LORA, N_HEADS * QK_NOPE, g_uk)
    w_uv = Stk("row", KV_LORA, N_HEADS * V_HEAD, g_uv)
    w_o = Stk("row", d, d, g_o)
    w_pw1 = Stk("full", d, 2 * d, g_pw1)
    w_pw2 = Stk("row", d, d, g_pw2)
    w_1 = [Stk("full", d, d_ff, g_w1[i]) for i in range(DEPTH)]
    w_2 = [Stk("row", d_ff, d, g_w2[i]) for i in range(DEPTH)]

    def wide(rows):
        return jnp.swapaxes(rows, 0, 1).reshape(rows.shape[1], d)

    conv_w = wide(g_small[:, 0:6]).reshape(2, SC_WIDTH, d)
    b_pw1 = g_small[:, 6:8].reshape(1, 2 * d)
    dw_w = wide(g_small[:, 8:39])
    dw_b, norm_g, norm_b, b_pw2 = (wide(g_small[:, 39 + k:40 + k]) for k in range(4))

    pos = jnp.arange(s, dtype=F32)
    inv_freq = ROPE_THETA ** (-jnp.arange(0, QK_ROPE, 2, dtype=F32) / QK_ROPE)
    ang = pos[:, None] * inv_freq[None, :]
    cos, sin, zero = jnp.cos(ang), jnp.sin(ang), jnp.zeros((s, 128 - QK_ROPE), F32)
    cf = jnp.concatenate([cos, cos, zero], axis=1)
    sf = jnp.concatenate([-sin, sin, zero], axis=1)

    def row(a, i):
        return a[i:i + 1]

    xs = x.reshape(s, d)
    cur = (xs, xs.astype(BF16))
    tape = []
    for i in range(DEPTH):
        mixer, j = i % 3, i // 3
        xf, xb = cur
        lg, lb = row(ln_mix_g, i), row(ln_mix_b, i)
        if mixer == 0:
            u = mm_plain_nn(f"sc{j}_in", xb, w_in[j], F32, tn=3 * dq4)
            gb = short_conv_gate(u, conv_w[j])
            y, yb, xh, rstd = mm_residual_ln(f"sc{j}_out_ln", gb, w_out[j], xf, lg, lb)
            sv = dict(xb=xb, u=u, gb=gb)
        elif mixer == 1:
            t = mm_plain_nn("mla_down", xb, w_dqkv, F32, tn=wd_t // 2)
            cq, ckv, kpe = mla_latents(t, mla_g_q, mla_g_kv, cf, sf)
            qh = mla_queries(cq, w_uq, cf, sf)
            kh = mla_keys(ckv, w_uk, kpe)
            vh = mm_plain_nn("mla_values", ckv, w_uv, BF16, tk=KV_LORA)
            oh = attention(qh, kh, vh)
            y, yb, xh, rstd = mm_residual_ln("mla_out_ln", oh, w_o, xf, lg, lb)
            sv = dict(xb=xb, t=t, cq=cq, ckv=ckv, qh=qh, kh=kh, vh=vh, oh=oh)
        else:
            u = mm_plain_nn("cf_pw1", xb, w_pw1, F32, bias=b_pw1)
            hc = conformer_glu_conv(u, dw_w, dw_b)
            sb = conformer_norm_swish(hc, norm_g, norm_b)
            y, yb, xh, rstd = mm_residual_ln("cf_pw2_ln", sb, w_pw2, xf, lg, lb, bias=b_pw2)
            sv = dict(xb=xb, u=u, hc=hc, sb=sb)
        sv.update(xh=xh, rstd=rstd, g=lg)
        cur, sv_mlp = _mlp_forward(i, y, yb, w_1[i], w_2[i], row(ln_ff_g, i), row(ln_ff_b, i))
        tape.append((sv, sv_mlp))

    dy, loss_part = loss_head(cur[0], loss_target.reshape(s, d))

    grads = {}
    smalls = {}
    g_ln = {n: [None] * DEPTH for n in ("ln_mix_g", "ln_mix_b", "ln_ff_g", "ln_ff_b")}
    conv_grads = [None, None]
    core = lax.axis_index("c").astype(jnp.int32).reshape(1)
    chip = xq.astype(jnp.int32).reshape(1)
    pairs, landed = {}, {}
    ready = []

    def reduce_after(x, new):
        out = lax.optimization_barrier((x, *new.values()))
        grads.update(zip(new, out[1:]))
        ready.extend(new)
        return out[0]

    def reduce_layer(i, x):
        theirs = pair_exchange(f"pair_exchange_layer{i}", [grads[n] for n in ready])
        sums = [pair_sum(grads[n], th, core) for n, th in zip(ready, theirs)]
        pairs.update(zip(ready, sums))
        landed.update(zip(ready, chip_exchange(f"chip_exchange_layer{i}", sums)))
        exchanged.append(list(ready))
        ready.clear()
        return lax.optimization_barrier((x, *sums))[0]

    groups = [["in_0", "in_1"], ["out_0", "out_1"], ["dqkv"], ["uq"], ["uk"], ["uv"], ["o"], ["pw1"], ["pw2"],
              [f"w1_{i}" for i in range(DEPTH)], [f"w2_{i}" for i in range(DEPTH)]]
    stacks = [None] * len(groups)
    exchanged = []

    def sum_layer(x, last=False):
        names = exchanged.pop(0)
        if last:
            out = lax.optimization_barrier((x, *[landed[n] for n in names]))
            landed.update(zip(names, out[1:]))
        new = []
        for n in names:
            k = next(k for k, members in enumerate(groups) if n in members)
            stacks[k] = chip_sum(pairs[n], landed[n], chip, stacks[k], groups[k].index(n), len(groups[k]))
            new.append(stacks[k])
        return out[0] if last else lax.optimization_barrier((x, *new))[0]

    for i in reversed(range(DEPTH)):
        mixer, j = i % 3, i // 3
        sv, sv_mlp = tape[i]
        dy, g_ln["ln_ff_g"][i], g_ln["ln_ff_b"][i] = _mlp_backward(
            i, dy, sv_mlp, w_1[i], w_2[i], Stk("col", d, d_ff), Stk("row", d_ff, d), reduce_after)
        if i == 0:
            dy = reduce_layer("0_mlp", dy)
        dr, drb, g_ln["ln_mix_g"][i], g_ln["ln_mix_b"][i], dr_sum = ln_backward(
            f"mix{i}_ln_bwd", dy, sv["xh"], sv["rstd"], sv["g"])
        if mixer == 0:
            dgate = mm_plain_nt(f"sc{j}_out_bwd", drb, w_out[j], F32)
            dw_out = mm_tn(f"sc{j}_dw_out", sv["gb"], drb, Stk("row", d, d), s, 512, 1024)
            du, conv_grads[j] = short_conv_gate_bwd(sv["u"], conv_w[j], dgate)
            nb = d // 256
            dw_in = mm_tn(
                f"sc{j}_dw_in", sv["xb"], du, Stk("col", d, 3 * d), s, 1024, 256,
                b_spec=pl.BlockSpec((None, s, 256), lambda i_, j_, k_: (j_ // nb, k_, j_ % nb)))
            du = reduce_after(du, {f"in_{j}": dw_in, f"out_{j}": dw_out})
            dy = mm_plain_nt(
                f"sc{j}_in_bwd", du, w_in[j], F32, tn=1024, tk=d, add=dr, add_scale=ALPHA,
                a_spec_fn=(s, lambda tm, tk: pl.BlockSpec((None, tm, tk), lambda i_, j_, k_: (k_, i_, 0))))
        elif mixer == 1:
            do = mm_plain_nt("mla_out_bwd", drb, w_o, BF16)
            g_o = mm_tn("mla_dw_o", sv["oh"], drb, Stk("row", d, d), s, 512, 1024)
            dqh, dkh, dvh = attention_bwd(sv["qh"], sv["kh"], sv["vh"], do)
            dql, dkn, dkpe = mla_unrope_grads(dqh, dkh, cf, sf)
            g_uq = mm_tn("mla_dw_uq", sv["cq"], dql, Stk("col", Q_LORA, N_HEADS * HEAD_PAD), s, Q_LORA, 512)
            dcq = mm_plain_nt("mla_uq_bwd", dql, w_uq, F32, tn=Q_LORA)
            g_uk = mm_tn("mla_dw_uk", sv["ckv"], dkn, Stk("row", KV_LORA, N_HEADS * QK_NOPE), s, KV_LORA, 1024)
            g_uv = mm_tn("mla_dw_uv", sv["ckv"], dvh, Stk("row", KV_LORA, N_HEADS * V_HEAD), s, KV_LORA, 1024)
            dckv = mm_plain_nt("mla_uk_bwd", dkn, w_uk, F32, tn=KV_LORA)
            dckv = mm_plain_nt("mla_uv_bwd", dvh, w_uv, F32, tn=KV_LORA, add=dckv)
            dt, smalls["g_q"], smalls["g_kv"] = mla_latents_bwd(sv["t"], mla_g_q, mla_g_kv, cf, sf, dcq, dckv, dkpe)
            g_dqkv = mm_tn("mla_dw_down", sv["xb"], dt, Stk("row", d, wd_t), s, 512, wd_t)
            dt = reduce_after(dt, {"dqkv": g_dqkv, "uq": g_uq, "uk": g_uk, "uv": g_uv, "o": g_o})
            dy = mm_plain_nt("mla_down_bwd", dt, w_dqkv, F32, tk=wd_t, add=dr, add_scale=ALPHA)
        else:
            dsw = mm_plain_nt("cf_pw2_bwd", drb, w_pw2, F32)
            g_pw2 = mm_tn("cf_dw_pw2", sv["sb"], drb, Stk("row", d, d), s, 512, 1024)
            smalls["b_pw2"] = dr_sum
            dhc, smalls["norm_g"], smalls["norm_b"] = conformer_norm_swish_bwd(sv["hc"], norm_g, norm_b, dsw)
            du, smalls["b_pw1"], smalls["dw_w"], smalls["dw_b"] = conformer_glu_conv_bwd(sv["u"], dw_w, dhc)
            nb = d // 512
            g_pw1 = mm_tn(
                "cf_dw_pw1", sv["xb"], du, Stk("col", d, 2 * d), s, 1024, 512,
                b_spec=pl.BlockSpec((None, s, 512), lambda i_, j_, k_: (j_ // nb, k_, j_ % nb)))
            du = reduce_after(du, {"pw1": g_pw1, "pw2": g_pw2})
            dy = mm_plain_nt(
                "cf_pw1_bwd", du, w_pw1, F32, tn=1024, tk=d, add=dr, add_scale=ALPHA,
                a_spec_fn=(s, lambda tm, tk: pl.BlockSpec((None, tm, tk), lambda i_, j_, k_: (k_, i_, 0))))
        if i < DEPTH - 1:
            dy = sum_layer(dy)
        dy = reduce_layer(i, dy)
    dy = sum_layer(sum_layer(dy, last=True), last=True)
    grad_x = dy.reshape(1, s, d)

    mine = stacks
    other = (pair_share("pair_share_mixers", mine[:9]) + pair_share("pair_share_up", mine[9:10])
             + pair_share("pair_share_down", mine[10:]))

    def padded(get):
        dqkv = jnp.concatenate([get("mla_w_dq")[0], jnp.pad(get("mla_w_dkv")[0], ((0, 0), (0, 128 - QK_ROPE)))], axis=1)
        uq = jnp.pad(get("mla_w_uq")[0].reshape(Q_LORA, 2, QK_NOPE + QK_ROPE),
                     ((0, 0), (0, 0), (0, HEAD_PAD - QK_NOPE - QK_ROPE))).reshape(Q_LORA, 2 * HEAD_PAD)
        return [get("sc_w_in"), get("sc_w_out"), dqkv[None], uq[None],
                get("mla_w_uk").reshape(1, KV_LORA // N_CHIPS, d), get("mla_w_uv").reshape(1, KV_LORA // N_CHIPS, d),
                get("mla_w_o"), get("cf_w_pw1"), get("cf_w_pw2"), get("ff_w1"), get("ff_w2")]

    w_l, m_l, v_l = (padded(lambda n, p=p: given[p + n]) for p in ("", "m_", "v_"))
    res = [adamw_joined(w_l[k], m_l[k], v_l[k], mine[k], other[k], core) for k in range(len(groups))]

    def unpadded(k):
        r_in, r_out, r_dqkv, r_uq, r_uk, r_uv, r_o, r_pw1, r_pw2, r_w1, r_w2 = (r[k] for r in res)
        return {
            "sc_w_in": r_in, "sc_w_out": r_out, "mla_w_dq": r_dqkv[:, :, 0:Q_LORA],
            "mla_w_dkv": r_dqkv[:, :, Q_LORA:Q_LORA + KV_LORA + QK_ROPE],
            "mla_w_uq": r_uq.reshape(1, Q_LORA, 2, HEAD_PAD)[:, :, :, 0:QK_NOPE + QK_ROPE].reshape(mla_w_uq.shape),
            "mla_w_uk": r_uk.reshape(mla_w_uk.shape), "mla_w_uv": r_uv.reshape(mla_w_uv.shape),
            "mla_w_o": r_o, "cf_w_pw1": r_pw1, "cf_w_pw2": r_pw2, "ff_w1": r_w1, "ff_w2": r_w2}

    big_g, big_d, big_m, big_v = (unpadded(k) for k in range(4))

    pad_row = lambda a: jnp.pad(a, ((0, 0), (0, d - a.shape[1])))
    small_parts = ([g for n in ("ln_mix_g", "ln_mix_b", "ln_ff_g", "ln_ff_b") for g in g_ln[n]]
                   + [pad_row(smalls["g_q"]), pad_row(smalls["g_kv"]), conv_grads[0], conv_grads[1],
                      smalls["b_pw1"].reshape(2, d), smalls["dw_w"], smalls["dw_b"], smalls["norm_g"], smalls["norm_b"],
                      smalls["b_pw2"], loss_part])
    red = all_reduce_small(small_parts, 64)
    loss = red[61, 0]

    def shard(rows):
        return lax.dynamic_slice_in_dim(rows, xq * dq4, dq4, axis=1)

    gw = {
        **big_g,
        "ln_mix_g": red[0:4], "ln_mix_b": red[4:8], "ln_ff_g": red[8:12], "ln_ff_b": red[12:16],
        "mla_g_q": red[16:17, 0:Q_LORA], "mla_g_kv": red[17:18, 0:KV_LORA],
        "sc_conv_w": shard(red[18:24]).reshape(2, SC_WIDTH, dq4),
        "cf_b_pw1": lax.dynamic_slice_in_dim(red[24:26].reshape(1, 2 * d), xq * 2 * dq4, 2 * dq4, axis=1),
        "cf_dw_w": shard(red[26:57])[None], "cf_dw_b": shard(red[57:58]), "cf_norm_g": shard(red[58:59]),
        "cf_norm_b": shard(red[59:60]), "cf_b_pw2": shard(red[60:61]),
    }

    upd = {n: [big_d[n], big_m[n], big_v[n]] for n in big_g}

    def pack(names, width, get):
        return jnp.concatenate([get(n).reshape(-1, width) for n in names], axis=0)

    def unpack(names, packed):
        out, at = {}, 0
        for n in names:
            rows = given[n].size // packed.shape[1]
            out[n] = packed[at:at + rows].reshape(given[n].shape)
            at += rows
        return out

    rep = ["ln_mix_g", "ln_mix_b", "ln_ff_g", "ln_ff_b"]
    shd = ["sc_conv_w", "cf_b_pw1", "cf_dw_w", "cf_dw_b", "cf_norm_g", "cf_norm_b", "cf_b_pw2"]
    for names, width in ((rep, d), (shd, dq4), (["mla_g_q"], Q_LORA), (["mla_g_kv"], KV_LORA)):
        res = adamw(pack(names, width, lambda n: given[n]), pack(names, width, lambda n: gw[n]),
                    pack(names, width, lambda n: given["m_" + n]), pack(names, width, lambda n: given["v_" + n]), tm=4096)
        parts = [unpack(names, r) for r in res]
        for n in names:
            upd[n] = [p[n] for p in parts]

    return (loss, grad_x, *[gw[n].reshape(given[n].shape) for n in WEIGHTS], *[upd[n][0] for n in WEIGHTS],
            *[upd[n][1] for n in WEIGHTS], *[upd[n][2] for n in WEIGHTS])
```

```python
import jax
import jax.numpy as jnp
from jax import lax
from jax.experimental import pallas as pl
from jax.experimental.pallas import tpu as pltpu
from jax.experimental.pallas import tpu_sc as plsc

F32 = jnp.float32
BF16 = jnp.bfloat16
MESH = pl.DeviceIdType.MESH

DEPTH = 4
ALPHA = (2.0 * DEPTH) ** 0.25
LN_EPS = 1e-5
RMS_EPS = 1e-6
CHUNK_SHIFT = 6
N_HEADS = 8
QK_NOPE = 128
QK_ROPE = 64
V_HEAD = 128
HEAD_PAD = 256
Q_LORA = 384
KV_LORA = 256
ROPE_THETA = 10000.0
SC_WIDTH = 3
CONF_WIDTH = 31
CONV_PAD = 32
CONV_CHUNK = 64
N_CHIPS = 4
ATTN_SCALE = (QK_NOPE + QK_ROPE) ** -0.5

ADAM_LR = 0.001
ADAM_B1 = 0.9
ADAM_B2 = 0.999
ADAM_EPS = 1e-08
ADAM_WD = 0.01
ADAM_STEP = 10

VMEM_LIMIT = 56 * 2**20

NN = (((1,), (0,)), ((), ()))
NT = (((1,), (1,)), ((), ()))
TN = (((0,), (0,)), ((), ()))


def _params(sem=None):
    return pltpu.CompilerParams(dimension_semantics=sem, vmem_limit_bytes=VMEM_LIMIT)


class Stk:
    def __init__(self, kind, k, n, arr=None, layers=None, layer=None):
        self.kind, self.k, self.n, self.layers, self.layer = kind, k, n, layers, layer
        self.plain = (kind == "row" and layers is None) or kind == "full"
        self.kloc = k // N_CHIPS if kind == "row" else k
        self.nloc = n // N_CHIPS if kind == "col" else n
        if arr is not None and self.plain:
            arr = arr.reshape(k, n)
        self.arr = arr

    @property
    def shape(self):
        if self.plain:
            return (self.k, self.n)
        lead = (N_CHIPS,) if self.layers is None else (N_CHIPS, self.layers)
        return lead + (self.kloc, self.nloc)

    def spec(self, bk, bn, f):
        if self.plain:
            return pl.BlockSpec((bk, bn), f)
        assert self.kloc % bk == 0 and self.nloc % bn == 0, (self.kloc, bk, self.nloc, bn)
        pk, pn = self.kloc // bk, self.nloc // bn
        kind, layer = self.kind, self.layer

        def imap(*g):
            kb, nb = f(*g)
            if kind == "row":
                q, kb, nb = kb // pk, kb % pk, nb
            else:
                q, kb, nb = nb // pn, kb, nb % pn
            return (q, kb, nb) if layer is None else (q, layer, kb, nb)

        block = (None, bk, bn) if layer is None else (None, None, bk, bn)
        return pl.BlockSpec(block, imap)


def _mm(name, mode, a, b, grid, a_spec, b_spec, acc_shape, extras, extra_specs, out_shapes, out_specs, epi, a_fn=None):
    nk = grid[2]
    ne = len(extras)

    def body(*refs):
        a_ref, b_ref = refs[0], refs[1]
        e_refs = refs[2:2 + ne]
        av = a_ref[...] if a_fn is None else a_fn(a_ref[...])
        part = lax.dot_general(av, b_ref[...], mode, preferred_element_type=F32)
        if nk == 1:
            epi(part, e_refs, refs[2 + ne:])
            return
        o_refs = refs[2 + ne:-1]
        acc = refs[-1]
        k = pl.program_id(2)

        @pl.when(k == 0)
        def _():
            acc[...] = part

        @pl.when(k > 0)
        def _():
            acc[...] += part

        @pl.when(k == nk - 1)
        def _():
            epi(acc[...], e_refs, o_refs)

    return pl.pallas_call(
        body, grid=grid, in_specs=[a_spec, b_spec, *extra_specs], out_specs=out_specs, out_shape=out_shapes,
        scratch_shapes=[pltpu.VMEM(acc_shape, F32)] if nk > 1 else [],
        compiler_params=_params(("parallel", "parallel", "arbitrary")), name=name)(a, b, *extras)


def _tile(n, t):
    t = min(n, t)
    while n % t:
        t -= 8
    assert t > 0, (n, t)
    return t


def mm_nn(name, a, w, tm, tn, tk, epi, out_shapes, out_specs, extras=(), extra_specs=(), a_spec=None, a_fn=None):
    m = a.shape[0]
    tm, tn, tk = _tile(m, tm), _tile(w.n, tn), _tile(w.k, tk)
    grid = (m // tm, w.n // tn, w.k // tk)
    a_spec = a_spec or pl.BlockSpec((tm, tk), lambda i, j, k: (i, k))
    b_spec = w.spec(tk, tn, lambda i, j, k: (k, j))
    return _mm(name, NN, a, w.arr, grid, a_spec, b_spec, (tm, tn), extras, extra_specs, out_shapes, out_specs, epi, a_fn)


def mm_nt(name, a, w, m, tm, tn, tk, epi, out_shapes, out_specs, extras=(), extra_specs=(), a_spec=None):
    tm, tn, tk = _tile(m, tm), _tile(w.k, tn), _tile(w.n, tk)
    grid = (m // tm, w.k // tn, w.n // tk)
    a_spec = a_spec or pl.BlockSpec((tm, tk), lambda i, j, k: (i, k))
    b_spec = w.spec(tn, tk, lambda i, j, k: (j, k))
    return _mm(name, NT, a, w.arr, grid, a_spec, b_spec, (tm, tn), extras, extra_specs, out_shapes, out_specs, epi)


def mm_tn(name, a, b, dw, s, tm=512, tn=512, tk=4096, a_spec=None, b_spec=None, a_fn=None):
    tm, tn, tk = _tile(dw.k, tm), _tile(dw.n, tn), _tile(s, tk)
    grid = (dw.k // tm, dw.n // tn, s // tk)
    a_spec = a_spec or pl.BlockSpec((tk, tm), lambda i, j, k: (k, i))
    b_spec = b_spec or pl.BlockSpec((tk, tn), lambda i, j, k: (k, j))

    def epi(acc, e, o):
        o[0][...] = acc.astype(BF16)

    out = _mm(name, TN, a, b, grid, a_spec, b_spec, (tm, tn), (), (), [jax.ShapeDtypeStruct(dw.shape, BF16)],
              [dw.spec(tm, tn, lambda i, j, k: (i, j))], epi, a_fn)[0]
    return out.reshape(N_CHIPS, dw.k // N_CHIPS, dw.n) if dw.plain else out


def _sds(shape, dtype):
    return jax.ShapeDtypeStruct(shape, dtype)


def _ij(tm, tn):
    return pl.BlockSpec((tm, tn), lambda i, j, k: (i, j))


def _i0(tm, c):
    return pl.BlockSpec((tm, c), lambda i, j, k: (i, 0))


def _0j(r, tn):
    return pl.BlockSpec((r, tn), lambda i, j, k: (0, j))


def _layer_norm_rows(r, g, b):
    mu = jnp.mean(r, axis=-1, keepdims=True)
    d = r - mu
    var = jnp.mean(d * d, axis=-1, keepdims=True)
    rstd = lax.rsqrt(var + LN_EPS)
    xh = d * rstd
    return xh * g + b, xh, rstd


def mm_residual_ln(name, a, w, x, g, b, bias=None, tm=512, tk=1024, a_fn=None):
    s, d = x.shape
    tm = _tile(s, tm)
    extras = [x, g, b] + ([bias] if bias is not None else [])
    especs = [_i0(tm, d), _0j(1, d), _0j(1, d)] + ([_0j(1, d)] if bias is not None else [])

    def epi(acc, e, o):
        r = ALPHA * e[0][...] + acc
        if bias is not None:
            r = r + e[3][...]
        y, xh, rstd = _layer_norm_rows(r, e[1][...], e[2][...])
        o[0][...] = y
        o[1][...] = y.astype(BF16)
        o[2][...] = xh
        o[3][...] = rstd

    return mm_nn(name, a, w, tm, d, tk, epi,
                 [_sds((s, d), F32), _sds((s, d), BF16), _sds((s, d), F32), _sds((s, 1), F32)],
                 [_i0(tm, d), _i0(tm, d), _i0(tm, d), _i0(tm, 1)], extras, especs, a_fn=a_fn)


def mm_plain_nn(name, a, w, out_dtype, tm=1024, tn=512, tk=1024, bias=None):
    m = a.shape[0]
    tm, tn = _tile(m, tm), _tile(w.n, tn)
    if w.kind == "col":
        tn = _tile(w.nloc, tn)

    def epi(acc, e, o):
        if bias is not None:
            acc = acc + e[0][...]
        o[0][...] = acc.astype(out_dtype)

    extras, especs = ([bias], [_0j(1, tn)]) if bias is not None else ((), ())
    return mm_nn(name, a, w, tm, tn, tk, epi, [_sds((m, w.n), out_dtype)], [_ij(tm, tn)], extras, especs)[0]


def mm_plain_nt(name, a, w, out_dtype, tm=1024, tn=512, tk=1024, add=None, add_scale=1.0, a_spec_fn=None):
    m = a.shape[0] if a_spec_fn is None else a_spec_fn[0]
    tm, tn = _tile(m, tm), _tile(w.k, tn)
    tk = _tile(w.n, tk)
    if w.kind == "col":
        tk = _tile(w.nloc, tk)
    if w.kind == "row" and not w.plain:
        tn = _tile(w.kloc, tn)

    def epi(acc, e, o):
        if add is not None:
            acc = acc + add_scale * e[0][...].astype(F32)
        o[0][...] = acc.astype(out_dtype)

    extras, especs = ([add], [_ij(tm, tn)]) if add is not None else ((), ())
    a_spec = None if a_spec_fn is None else a_spec_fn[1](tm, tk)
    return mm_nt(name, a, w, m, tm, tn, tk, epi, [_sds((m, w.k), out_dtype)], [_ij(tm, tn)], extras, especs,
                 a_spec=a_spec)[0]


def _rows(tm, c):
    return pl.BlockSpec((tm, c), lambda i: (i, 0))


def _fix(shape):
    nd = len(shape)
    return pl.BlockSpec(shape, lambda i: (0,) * nd)


def _accumulate(ref, val):
    @pl.when(pl.program_id(0) == 0)
    def _():
        ref[...] = jnp.zeros_like(ref)

    ref[...] += val


def ln_backward(name, dy, xhat, rstd, g, tm=256):
    s, d = dy.shape
    tm = _tile(s, tm)

    def body(dy_ref, xh_ref, rstd_ref, g_ref, dr_ref, drb_ref, dg_ref, db_ref, ds_ref):
        dyv, xh = dy_ref[...], xh_ref[...]
        dxh = dyv * g_ref[...]
        m1 = jnp.mean(dxh, axis=-1, keepdims=True)
        m2 = jnp.mean(dxh * xh, axis=-1, keepdims=True)
        dr = rstd_ref[...] * (dxh - m1 - xh * m2)
        dr_ref[...] = dr
        drb_ref[...] = dr.astype(BF16)
        _accumulate(dg_ref, jnp.sum(dyv * xh, axis=0, keepdims=True))
        _accumulate(db_ref, jnp.sum(dyv, axis=0, keepdims=True))
        _accumulate(ds_ref, jnp.sum(dr, axis=0, keepdims=True))

    return pl.pallas_call(
        body, grid=(s // tm,),
        in_specs=[_rows(tm, d), _rows(tm, d), _rows(tm, 1), _fix((1, d))],
        out_specs=[_rows(tm, d), _rows(tm, d), _fix((1, d)), _fix((1, d)), _fix((1, d))],
        out_shape=[_sds((s, d), F32), _sds((s, d), BF16), _sds((1, d), F32), _sds((1, d), F32), _sds((1, d), F32)],
        compiler_params=_params(("arbitrary",)), name=name)(dy, xhat, rstd, g)


def loss_head(y, target, tm=256):
    s, d = y.shape
    tm = _tile(s, tm)

    def body(y_ref, t_ref, dy_ref, loss_ref):
        e = y_ref[...] - t_ref[...]
        dy_ref[...] = e * (1.0 / d)
        part = 0.5 * jnp.sum(jnp.mean(e * e, axis=-1, keepdims=True), axis=0, keepdims=True)
        _accumulate(loss_ref, jnp.broadcast_to(part, (1, d)))

    return pl.pallas_call(
        body, grid=(s // tm,), in_specs=[_rows(tm, d), _rows(tm, d)],
        out_specs=[_rows(tm, d), _fix((1, d))], out_shape=[_sds((s, d), F32), _sds((1, d), F32)],
        compiler_params=_params(("arbitrary",)), name="loss_head")(y, target)


def _cols(s, tc, off=0):
    return pl.BlockSpec((s, tc), lambda i: (0, i + off))


def _shift_down(z, sft, rows):
    return jnp.where(rows >= sft, pltpu.roll(z, sft, 0), 0.0)


def _shift_up(z, sft, rows, s):
    return jnp.where(rows < s - sft, pltpu.roll(z, (s - sft) % s, 0), 0.0)


def short_conv_gate(u, conv_w, tc=256):
    s, d3 = u.shape
    d = d3 // 3
    nb = d // tc

    def body(b_ref, c_ref, h_ref, w_ref, o_ref):
        rows = lax.broadcasted_iota(jnp.int32, (s, tc), 0)
        z = c_ref[...] * h_ref[...]
        cz = jnp.zeros((s, tc), F32)
        for k in range(SC_WIDTH):
            sft = SC_WIDTH - 1 - k
            cz = cz + w_ref[pl.ds(k, 1), :] * (_shift_down(z, sft, rows) if sft else z)
        o_ref[...] = (b_ref[...] * cz).astype(BF16)

    return pl.pallas_call(
        body, grid=(nb,),
        in_specs=[_cols(s, tc), _cols(s, tc, nb), _cols(s, tc, 2 * nb), _cols(SC_WIDTH, tc)],
        out_specs=_cols(s, tc), out_shape=_sds((s, d), BF16),
        compiler_params=_params(("parallel",)), name="short_conv_gate")(u, u, u, conv_w)


def short_conv_gate_bwd(u, conv_w, dg, tc=256):
    s, d3 = u.shape
    d = d3 // 3
    nb = d // tc

    def body(b_ref, c_ref, h_ref, w_ref, dg_ref, du_ref, dw_ref):
        rows = lax.broadcasted_iota(jnp.int32, (s, tc), 0)
        c, h, dgv = c_ref[...], h_ref[...], dg_ref[...]
        z = c * h
        dcz = dgv * b_ref[...]
        cz = jnp.zeros((s, tc), F32)
        dz = jnp.zeros((s, tc), F32)
        for k in range(SC_WIDTH):
            sft = SC_WIDTH - 1 - k
            zs = _shift_down(z, sft, rows) if sft else z
            wk = w_ref[pl.ds(k, 1), :]
            cz = cz + wk * zs
            dz = dz + wk * (_shift_up(dcz, sft, rows, s) if sft else dcz)
            dw_ref[pl.ds(k, 1), :] = jnp.sum(dcz * zs, axis=0, keepdims=True)
        du_ref[0] = (dgv * cz).astype(BF16)
        du_ref[1] = (dz * h).astype(BF16)
        du_ref[2] = (dz * c).astype(BF16)

    return pl.pallas_call(
        body, grid=(nb,),
        in_specs=[_cols(s, tc), _cols(s, tc, nb), _cols(s, tc, 2 * nb), _cols(SC_WIDTH, tc), _cols(s, tc)],
        out_specs=[pl.BlockSpec((3, s, tc), lambda i: (0, 0, i)), _cols(SC_WIDTH, tc)],
        out_shape=[_sds((3, s, d), BF16), _sds((SC_WIDTH, d), F32)],
        compiler_params=_params(("parallel",)), name="short_conv_gate_bwd")(u, u, u, conv_w, dg)


def _store_shifted_down(ref, z, rows):
    s, tc = z.shape
    for b in range(8):
        ref[b, pl.ds(0, CONV_PAD), :] = jnp.zeros((CONV_PAD, tc), F32)
        ref[b, pl.ds(CONV_PAD, s), :] = z if b == 0 else _shift_down(z, b, rows)


def _store_shifted_up(ref, z, rows):
    s, tc = z.shape
    for b in range(8):
        ref[b, pl.ds(0, s), :] = z if b == 0 else _shift_up(z, b, rows, s)
        ref[b, pl.ds(s, CONV_PAD), :] = jnp.zeros((CONV_PAD, tc), F32)


def conformer_glu_conv(u, dw_w, dw_b, tc=128):
    s, d2 = u.shape
    d = d2 // 2
    nb = d // tc

    ch = min(CONV_CHUNK, s)

    def body(a_ref, g_ref, w_ref, b_ref, o_ref, down):
        rows = lax.broadcasted_iota(jnp.int32, (s, tc), 0)
        _store_shifted_down(down, a_ref[...] * jax.nn.sigmoid(g_ref[...]), rows)

        def chunk(ci, carry):
            r0 = pl.multiple_of(ci * ch, ch)
            acc = jnp.broadcast_to(b_ref[...], (ch, tc))
            for k in range(CONF_WIDTH):
                sft = CONF_WIDTH - 1 - k
                acc = acc + w_ref[pl.ds(k, 1), :] * down[sft % 8, pl.ds(CONV_PAD + r0 - (sft // 8) * 8, ch), :]
            o_ref[pl.ds(r0, ch), :] = acc
            return carry

        lax.fori_loop(0, s // ch, chunk, 0)

    return pl.pallas_call(
        body, grid=(nb,),
        in_specs=[_cols(s, tc), _cols(s, tc, nb), _cols(CONF_WIDTH, tc), _cols(1, tc)],
        out_specs=_cols(s, tc), out_shape=_sds((s, d), F32),
        scratch_shapes=[pltpu.VMEM((8, CONV_PAD + s, tc), F32)],
        compiler_params=_params(("parallel",)), name="conformer_glu_conv")(u, u, dw_w, dw_b)


def conformer_glu_conv_bwd(u, dw_w, dhc, tc=128):
    s, d2 = u.shape
    d = d2 // 2
    nb = d // tc
    ch = min(CONV_CHUNK, s)

    def body(a_ref, g_ref, w_ref, dhc_ref, du_ref, dbias_ref, dw_ref, db_ref, down, up, dw_acc, dh_buf):
        rows = lax.broadcasted_iota(jnp.int32, (s, tc), 0)
        a = a_ref[...]
        sg = jax.nn.sigmoid(g_ref[...])
        dhcv = dhc_ref[...]
        _store_shifted_down(down, a * sg, rows)
        _store_shifted_up(up, dhcv, rows)
        dw_acc[...] = jnp.zeros_like(dw_acc)

        def chunk(ci, carry):
            r0 = pl.multiple_of(ci * ch, ch)
            dc = dhc_ref[pl.ds(r0, ch), :]
            dh = jnp.zeros((ch, tc), F32)
            for k in range(CONF_WIDTH):
                sft = CONF_WIDTH - 1 - k
                a8, b = (sft // 8) * 8, sft % 8
                dh = dh + w_ref[pl.ds(k, 1), :] * up[b, pl.ds(r0 + a8, ch), :]
                prod = dc * down[b, pl.ds(CONV_PAD + r0 - a8, ch), :]
                dw_acc[k] += jnp.sum(prod.reshape(ch // 8, 8, tc), axis=0)
            dh_buf[pl.ds(r0, ch), :] = dh
            return carry

        lax.fori_loop(0, s // ch, chunk, 0)
        dh = dh_buf[...]
        da = dh * sg
        dgate = dh * a * sg * (1.0 - sg)
        du_ref[0] = da.astype(BF16)
        du_ref[1] = dgate.astype(BF16)
        dbias_ref[pl.ds(0, 1), :] = jnp.sum(da, axis=0, keepdims=True)
        dbias_ref[pl.ds(1, 1), :] = jnp.sum(dgate, axis=0, keepdims=True)
        db_ref[...] = jnp.sum(dhcv, axis=0, keepdims=True)
        for k in range(CONF_WIDTH):
            dw_ref[pl.ds(k, 1), :] = jnp.sum(dw_acc[k], axis=0, keepdims=True)

    return pl.pallas_call(
        body, grid=(nb,),
        in_specs=[_cols(s, tc), _cols(s, tc, nb), _cols(CONF_WIDTH, tc), _cols(s, tc)],
        out_specs=[pl.BlockSpec((2, s, tc), lambda i: (0, 0, i)), _cols(2, tc), _cols(CONF_WIDTH, tc), _cols(1, tc)],
        out_shape=[_sds((2, s, d), BF16), _sds((2, d), F32), _sds((CONF_WIDTH, d), F32), _sds((1, d), F32)],
        scratch_shapes=[pltpu.VMEM((8, CONV_PAD + s, tc), F32), pltpu.VMEM((8, CONV_PAD + s, tc), F32),
                        pltpu.VMEM((CONF_WIDTH + 1, 8, tc), F32), pltpu.VMEM((s, tc), F32)],
        compiler_params=_params(("parallel",)), name="conformer_glu_conv_bwd")(u, u, dw_w, dhc)


def conformer_norm_swish(hc, g, b, tm=256):
    s, d = hc.shape
    tm = _tile(s, tm)

    def body(h_ref, g_ref, b_ref, o_ref):
        n, _, _ = _layer_norm_rows(h_ref[...], g_ref[...], b_ref[...])
        o_ref[...] = (n * jax.nn.sigmoid(n)).astype(BF16)

    return pl.pallas_call(
        body, grid=(s // tm,), in_specs=[_rows(tm, d), _fix((1, d)), _fix((1, d))], out_specs=_rows(tm, d),
        out_shape=_sds((s, d), BF16), compiler_params=_params(("parallel",)), name="conformer_norm_swish")(hc, g, b)


def conformer_norm_swish_bwd(hc, g, b, ds, tm=256):
    s, d = hc.shape
    tm = _tile(s, tm)

    def body(h_ref, g_ref, b_ref, ds_ref, dh_ref, dg_ref, db_ref):
        n, nh, rstd = _layer_norm_rows(h_ref[...], g_ref[...], b_ref[...])
        sg = jax.nn.sigmoid(n)
        dn = ds_ref[...] * (sg * (1.0 + n * (1.0 - sg)))
        dnh = dn * g_ref[...]
        m1 = jnp.mean(dnh, axis=-1, keepdims=True)
        m2 = jnp.mean(dnh * nh, axis=-1, keepdims=True)
        dh_ref[...] = rstd * (dnh - m1 - nh * m2)
        _accumulate(dg_ref, jnp.sum(dn * nh, axis=0, keepdims=True))
        _accumulate(db_ref, jnp.sum(dn, axis=0, keepdims=True))

    return pl.pallas_call(
        body, grid=(s // tm,), in_specs=[_rows(tm, d), _fix((1, d)), _fix((1, d)), _rows(tm, d)],
        out_specs=[_rows(tm, d), _fix((1, d)), _fix((1, d))],
        out_shape=[_sds((s, d), F32), _sds((1, d), F32), _sds((1, d), F32)],
        compiler_params=_params(("arbitrary",)), name="conformer_norm_swish_bwd")(hc, g, b, ds)


def _swap_halves(x):
    lane = lax.broadcasted_iota(jnp.int32, x.shape, 1)
    return jnp.where(lane < QK_ROPE // 2, pltpu.roll(x, 128 - QK_ROPE // 2, 1), pltpu.roll(x, QK_ROPE // 2, 1))


def _rope(x, cf, sf):
    return x * cf + _swap_halves(x) * sf


def _unrope(dx, cf, sf):
    return dx * cf - _swap_halves(dx) * sf


def _rms_rows(x, g):
    r = lax.rsqrt(jnp.mean(x * x, axis=-1, keepdims=True) + RMS_EPS)
    return x * r, r


def mla_latents(t, g_q, g_kv, cf, sf, tm=256):
    s = t.shape[0]
    tm = _tile(s, tm)

    def body(t_ref, gq_ref, gkv_ref, cf_ref, sf_ref, cq_ref, ckv_ref, kpe_ref):
        xq, _ = _rms_rows(t_ref[:, 0:Q_LORA], gq_ref[...])
        cq_ref[...] = (xq * gq_ref[...]).astype(BF16)
        xkv, _ = _rms_rows(t_ref[:, Q_LORA:Q_LORA + KV_LORA], gkv_ref[...])
        ckv_ref[...] = (xkv * gkv_ref[...]).astype(BF16)
        kpe_ref[...] = _rope(t_ref[:, Q_LORA + KV_LORA:], cf_ref[...], sf_ref[...]).astype(BF16)

    w = Q_LORA + KV_LORA + 128
    return pl.pallas_call(
        body, grid=(s // tm,),
        in_specs=[_rows(tm, w), _fix((1, Q_LORA)), _fix((1, KV_LORA)), _rows(tm, 128), _rows(tm, 128)],
        out_specs=[_rows(tm, Q_LORA), _rows(tm, KV_LORA), _rows(tm, 128)],
        out_shape=[_sds((s, Q_LORA), BF16), _sds((s, KV_LORA), BF16), _sds((s, 128), BF16)],
        compiler_params=_params(("parallel",)), name="mla_latents")(t, g_q, g_kv, cf, sf)


def mla_latents_bwd(t, g_q, g_kv, cf, sf, dcq, dckv, dkpe, tm=256):
    s = t.shape[0]
    tm = _tile(s, tm)
    w = Q_LORA + KV_LORA + 128

    def rms_bwd(x, g, dy):
        xh, r = _rms_rows(x, g)
        dxh = dy * g
        return r * (dxh - xh * jnp.mean(dxh * xh, axis=-1, keepdims=True)), jnp.sum(dy * xh, axis=0, keepdims=True)

    def body(t_ref, gq_ref, gkv_ref, cf_ref, sf_ref, dcq_ref, dckv_ref, dkpe_ref, dt_ref, dgq_ref, dgkv_ref):
        dxq, dgq = rms_bwd(t_ref[:, 0:Q_LORA], gq_ref[...], dcq_ref[...])
        dxkv, dgkv = rms_bwd(t_ref[:, Q_LORA:Q_LORA + KV_LORA], gkv_ref[...], dckv_ref[...])
        dt_ref[:, 0:Q_LORA] = dxq.astype(BF16)
        dt_ref[:, Q_LORA:Q_LORA + KV_LORA] = dxkv.astype(BF16)
        dt_ref[:, Q_LORA + KV_LORA:] = _unrope(dkpe_ref[...], cf_ref[...], sf_ref[...]).astype(BF16)
        _accumulate(dgq_ref, dgq)
        _accumulate(dgkv_ref, dgkv)

    return pl.pallas_call(
        body, grid=(s // tm,),
        in_specs=[_rows(tm, w), _fix((1, Q_LORA)), _fix((1, KV_LORA)), _rows(tm, 128), _rows(tm, 128),
                  _rows(tm, Q_LORA), _rows(tm, KV_LORA), _rows(tm, 128)],
        out_specs=[_rows(tm, w), _fix((1, Q_LORA)), _fix((1, KV_LORA))],
        out_shape=[_sds((s, w), BF16), _sds((1, Q_LORA), F32), _sds((1, KV_LORA), F32)],
        compiler_params=_params(("arbitrary",)), name="mla_latents_bwd")(t, g_q, g_kv, cf, sf, dcq, dckv, dkpe)


def mla_queries(cq, w_uq, cf, sf, tm=512):
    s = cq.shape[0]
    tm = _tile(s, tm)

    def epi(acc, e, o):
        o[0][:, 0:QK_NOPE] = acc[:, 0:QK_NOPE].astype(BF16)
        o[0][:, QK_NOPE:] = _rope(acc[:, QK_NOPE:], e[0][...], e[1][...]).astype(BF16)

    return mm_nn("mla_queries", cq, w_uq, tm, HEAD_PAD, Q_LORA, epi, [_sds((s, N_HEADS * HEAD_PAD), BF16)],
                 [_ij(tm, HEAD_PAD)], [cf, sf], [_i0(tm, 128), _i0(tm, 128)])[0]


def mla_keys(ckv, w_uk, kpe, tm=512):
    s = ckv.shape[0]
    tm = _tile(s, tm)

    def epi(acc, e, o):
        o[0][:, 0:QK_NOPE] = acc.astype(BF16)
        o[0][:, QK_NOPE:] = e[0][...]

    return mm_nn("mla_keys", ckv, w_uk, tm, QK_NOPE, KV_LORA, epi, [_sds((s, N_HEADS * HEAD_PAD), BF16)],
                 [_ij(tm, HEAD_PAD)], [kpe], [_i0(tm, 128)])[0]


def _masked_scores(q, k, qi, tq, kv):
    sc = lax.dot_general(q, k, NT, preferred_element_type=F32) * ATTN_SCALE
    row = lax.broadcasted_iota(jnp.int32, (tq, kv), 0) + qi * tq
    col = lax.broadcasted_iota(jnp.int32, (tq, kv), 1)
    ok = lax.shift_right_logical(col, CHUNK_SHIFT) <= lax.shift_right_logical(row, CHUNK_SHIFT)
    return jnp.where(ok, sc, -1e30)


def attention(q, k, v, tq=256):
    s = q.shape[0]
    tq = _tile(s, tq)
    nq = s // tq

    def body(q_ref, k_ref, v_ref, o_ref):
        for qi in range(nq):
            kv = (qi + 1) * tq
            sc = _masked_scores(q_ref[pl.ds(qi * tq, tq), :], k_ref[pl.ds(0, kv), :], qi, tq, kv)
            p = jnp.exp(sc - jnp.max(sc, axis=-1, keepdims=True))
            o = lax.dot_general(p.astype(BF16), v_ref[pl.ds(0, kv), :], NN, preferred_element_type=F32)
            o_ref[pl.ds(qi * tq, tq), :] = (o / jnp.sum(p, axis=-1, keepdims=True)).astype(BF16)

    hq = pl.BlockSpec((s, HEAD_PAD), lambda h: (0, h))
    hv = pl.BlockSpec((s, V_HEAD), lambda h: (0, h))
    return pl.pallas_call(
        body, grid=(N_HEADS,), in_specs=[hq, hq, hv], out_specs=hv, out_shape=_sds((s, N_HEADS * V_HEAD), BF16),
        compiler_params=_params(("parallel",)), name="attention")(q, k, v)


def attention_bwd(q, k, v, do, tq=512):
    s = q.shape[0]
    tq = _tile(s, tq)
    nq = s // tq

    def body(q_ref, k_ref, v_ref, do_ref, dq_ref, dk_ref, dv_ref, dk_acc, dv_acc):
        dk_acc[...] = jnp.zeros_like(dk_acc)
        dv_acc[...] = jnp.zeros_like(dv_acc)
        for qi in range(nq):
            kv = (qi + 1) * tq
            qt = q_ref[pl.ds(qi * tq, tq), :]
            kt = k_ref[pl.ds(0, kv), :]
            dot = do_ref[pl.ds(qi * tq, tq), :]
            sc = _masked_scores(qt, kt, qi, tq, kv)
            p = jnp.exp(sc - jnp.max(sc, axis=-1, keepdims=True))
            p = p / jnp.sum(p, axis=-1, keepdims=True)
            dp = lax.dot_general(dot, v_ref[pl.ds(0, kv), :], NT, preferred_element_type=F32)
            delta = jnp.sum(p * dp, axis=-1, keepdims=True)
            ds = (p * (dp - delta) * ATTN_SCALE).astype(BF16)
            dq_ref[pl.ds(qi * tq, tq), :] = lax.dot_general(ds, kt, NN, preferred_element_type=F32).astype(BF16)
            dk_acc[pl.ds(0, kv), :] += lax.dot_general(ds, qt, TN, preferred_element_type=F32)
            dv_acc[pl.ds(0, kv), :] += lax.dot_general(p.astype(BF16), dot, TN, preferred_element_type=F32)
        dk_ref[...] = dk_acc[...].astype(BF16)
        dv_ref[...] = dv_acc[...].astype(BF16)

    hq = pl.BlockSpec((s, HEAD_PAD), lambda h: (0, h))
    hv = pl.BlockSpec((s, V_HEAD), lambda h: (0, h))
    return pl.pallas_call(
        body, grid=(N_HEADS,), in_specs=[hq, hq, hv, hv], out_specs=[hq, hq, hv],
        out_shape=[_sds((s, N_HEADS * HEAD_PAD), BF16), _sds((s, N_HEADS * HEAD_PAD), BF16),
                   _sds((s, N_HEADS * V_HEAD), BF16)],
        scratch_shapes=[pltpu.VMEM((s, HEAD_PAD), F32), pltpu.VMEM((s, V_HEAD), F32)],
        compiler_params=_params(("parallel",)), name="attention_bwd")(q, k, v, do)


def mla_unrope_grads(dq, dk, cf, sf, tm=256):
    s = dq.shape[0]
    tm = _tile(s, tm)

    def body(dq_ref, dk_ref, cf_ref, sf_ref, dql_ref, dkn_ref, dkpe_ref):
        cfv, sfv = cf_ref[...], sf_ref[...]
        dkpe = jnp.zeros((tm, 128), F32)
        for h in range(N_HEADS):
            lo = h * HEAD_PAD
            dql_ref[:, lo:lo + QK_NOPE] = dq_ref[:, lo:lo + QK_NOPE]
            dql_ref[:, lo + QK_NOPE:lo + HEAD_PAD] = _unrope(
                dq_ref[:, lo + QK_NOPE:lo + HEAD_PAD].astype(F32), cfv, sfv).astype(BF16)
            dkn_ref[:, h * QK_NOPE:(h + 1) * QK_NOPE] = dk_ref[:, lo:lo + QK_NOPE]
            dkpe = dkpe + dk_ref[:, lo + QK_NOPE:lo + HEAD_PAD].astype(F32)
        dkpe_ref[...] = dkpe

    wq = N_HEADS * HEAD_PAD
    return pl.pallas_call(
        body, grid=(s // tm,), in_specs=[_rows(tm, wq), _rows(tm, wq), _rows(tm, 128), _rows(tm, 128)],
        out_specs=[_rows(tm, wq), _rows(tm, N_HEADS * QK_NOPE), _rows(tm, 128)],
        out_shape=[_sds((s, wq), BF16), _sds((s, N_HEADS * QK_NOPE), BF16), _sds((s, 128), F32)],
        compiler_params=_params(("parallel",)), name="mla_unrope_grads")(dq, dk, cf, sf)


ANY = pl.BlockSpec(memory_space=pl.ANY)
GATHER_ID = 1
CHIP_EXCHANGE_ID = 2
PAIR_ID = 3
ALL_ID = 4


def _nbytes(a):
    return a.size * a.dtype.itemsize


def _copy_cost(operand_bytes, sent_fraction):
    sent = int(operand_bytes * sent_fraction)
    return pl.CostEstimate(flops=0, transcendentals=0, bytes_accessed=2 * sent, remote_bytes_transferred=sent)


def _handshake(peers):
    barrier = pltpu.get_barrier_semaphore()
    for peer in peers:
        pl.semaphore_signal(barrier, inc=1, device_id=peer, device_id_type=MESH)
    pl.semaphore_wait(barrier, len(peers))


def _place():
    x, y, c = lax.axis_index("x"), lax.axis_index("y"), lax.axis_index("c")
    chips = [(1 - x, y), (x, 1 - y), (1 - x, 1 - y)]
    return x, y, c, chips


def _half(ref, hc, axis=0):
    n = ref.shape[axis] // 2
    idx = (slice(None),) * axis + (pl.ds(hc * n, n),)
    return ref.at[idx]


def gather_shards(name, tensors, by_columns=()):
    nt = len(tensors)

    def body(*refs):
        a, g = refs[:nt], refs[nt:2 * nt]
        send, recv = refs[2 * nt:]
        x, y, c, chips = _place()
        q = 2 * x + y
        sib = (x, y, 1 - c)
        _handshake([sib] + [(*chip, c) for chip in chips])

        def whole(t, p):
            if t in by_columns:
                n = a[t].shape[1]
                return g[t].at[:, pl.ds(p * n, n)]
            return g[t].at[p]

        def slot(t, chip, hc):
            return _half(whole(t, 2 * chip[0] + chip[1]), hc)

        def rc(t, k, src, dst, to):
            return pltpu.make_async_remote_copy(src_ref=src, dst_ref=dst, send_sem=send.at[t, k], recv_sem=recv.at[t, k],
                                                device_id=to, device_id_type=MESH)

        sent = []
        for t in range(nt):
            cp = rc(t, 6, a[t], whole(t, q), sib)
            cp.start()
            sent.append(cp)
            for j, chip in enumerate(chips):
                cp = rc(t, j, _half(a[t], c), slot(t, (x, y), c), (*chip, c))
                cp.start()
                sent.append(cp)
        for t in range(nt):
            for j, chip in enumerate(chips):
                landed = slot(t, chip, c)
                rc(t, j, landed, landed, (*chip, c)).wait_recv()
                cp = rc(t, 3 + j, landed, landed, sib)
                cp.start()
                sent.append(cp)
        for t in range(nt):
            for j, chip in enumerate(chips):
                other = slot(t, chip, 1 - c)
                rc(t, 3 + j, other, other, sib).wait_recv()
            own = whole(t, q)
            rc(t, 6, own, own, sib).wait_recv()
        for cp in sent:
            cp.wait_send()

    return pl.kernel(
        body, name=name,
        out_type=[_sds((a.shape[0], N_CHIPS * a.shape[1]) if t in by_columns else (N_CHIPS,) + a.shape, a.dtype)
                  for t, a in enumerate(tensors)],
        mesh=plsc.ScalarSubcoreMesh(axis_name="sequencer", num_cores=1),
        scratch_types=[pltpu.SemaphoreType.DMA((nt, 7)), pltpu.SemaphoreType.DMA((nt, 7))],
        cost_estimate=_copy_cost(sum(_nbytes(a) for a in tensors), 4),
        compiler_params=pltpu.CompilerParams(collective_id=GATHER_ID))(*tensors)


def pair_exchange(name, grads, on_sequencer):
    nt = len(grads)

    def body(*refs):
        g, theirs = refs[:nt], refs[nt:2 * nt]
        send, recv = refs[2 * nt:]
        x, y, c, _ = _place()
        if on_sequencer:
            _handshake([(x, y, 1 - c)])
        cps = []
        for t in range(nt):
            cp = pltpu.make_async_remote_copy(src_ref=_half(g[t], 1 - c, 1), dst_ref=theirs[t], send_sem=send.at[t],
                                              recv_sem=recv.at[t], device_id=(x, y, 1 - c), device_id_type=MESH)
            cp.start()
            cps.append(cp)
        for cp in cps:
            cp.wait()

    if not on_sequencer:
        return pl.pallas_call(
            body, in_specs=[ANY] * nt, out_specs=[ANY] * nt,
            out_shape=[_sds((N_CHIPS, a.shape[1] // 2, a.shape[2]), a.dtype) for a in grads],
            scratch_shapes=[pltpu.SemaphoreType.DMA((nt,)), pltpu.SemaphoreType.DMA((nt,))],
            name=name)(*grads)
    return pl.kernel(
        body, name=name, out_type=[_sds((N_CHIPS, a.shape[1] // 2, a.shape[2]), a.dtype) for a in grads],
        mesh=plsc.ScalarSubcoreMesh(axis_name="sequencer", num_cores=1),
        scratch_types=[pltpu.SemaphoreType.DMA((nt,)), pltpu.SemaphoreType.DMA((nt,))],
        cost_estimate=_copy_cost(sum(_nbytes(a) for a in grads), 0.5),
        compiler_params=pltpu.CompilerParams(collective_id=PAIR_ID))(*grads)


def chip_exchange(name, parts):
    nt = len(parts)

    def body(*refs):
        a, r = refs[:nt], refs[nt:2 * nt]
        send, recv = refs[2 * nt:]
        x, y, c, chips = _place()
        _handshake([(*chip, c) for chip in chips])
        cps = []
        for t in range(nt):
            for j, chip in enumerate(chips):
                cp = pltpu.make_async_remote_copy(
                    src_ref=a[t].at[2 * chip[0] + chip[1]], dst_ref=r[t].at[j], send_sem=send.at[t, j],
                    recv_sem=recv.at[t, j], device_id=(*chip, c), device_id_type=MESH)
                cp.start()
                cps.append(cp)
        for cp in cps:
            cp.wait()

    return pl.kernel(
        body, name=name, out_type=[_sds((N_CHIPS - 1,) + a.shape[1:], a.dtype) for a in parts],
        mesh=plsc.ScalarSubcoreMesh(axis_name="sequencer", num_cores=1),
        scratch_types=[pltpu.SemaphoreType.DMA((nt, 3)), pltpu.SemaphoreType.DMA((nt, 3))],
        cost_estimate=_copy_cost(sum(_nbytes(a) for a in parts), 0.75),
        compiler_params=pltpu.CompilerParams(collective_id=CHIP_EXCHANGE_ID))(*parts)


def pair_share(name, halves):
    nt = len(halves)

    def body(*refs):
        h, other = refs[:nt], refs[nt:2 * nt]
        send, recv = refs[2 * nt:]
        x, y, c, _ = _place()
        _handshake([(x, y, 1 - c)])
        cps = []
        for t in range(nt):
            cp = pltpu.make_async_remote_copy(src_ref=h[t], dst_ref=other[t], send_sem=send.at[t], recv_sem=recv.at[t],
                                              device_id=(x, y, 1 - c), device_id_type=MESH)
            cp.start()
            cps.append(cp)
        for cp in cps:
            cp.wait()

    return pl.kernel(
        body, name=name, out_type=[_sds(a.shape, a.dtype) for a in halves],
        mesh=plsc.ScalarSubcoreMesh(axis_name="sequencer", num_cores=1),
        scratch_types=[pltpu.SemaphoreType.DMA((nt,)), pltpu.SemaphoreType.DMA((nt,))],
        cost_estimate=_copy_cost(sum(_nbytes(a) for a in halves), 1),
        compiler_params=pltpu.CompilerParams(collective_id=PAIR_ID))(*halves)


def all_reduce_small(parts, rows):
    cdim = parts[0].shape[1]
    n = len(parts)
    vm = pl.BlockSpec(memory_space=pltpu.VMEM)

    def pack(*refs):
        p, o_ref = refs[:n], refs[n]
        at = 0
        for ref in p:
            o_ref[pl.ds(at, ref.shape[0]), :] = ref[...]
            at += ref.shape[0]
        o_ref[pl.ds(at, rows - at), :] = jnp.zeros((rows - at, cdim), F32)

    mine = pl.pallas_call(pack, in_specs=[vm] * n, out_specs=vm, out_shape=_sds((rows, cdim), F32), name="small_pack")(*parts)

    def exchange(mine_ref, buf, send, recv, lsem):
        x, y, c, _ = _place()
        me = 4 * x + 2 * y + c
        peers = [(x ^ (k >> 2), y ^ ((k >> 1) & 1), c ^ (k & 1)) for k in range(1, 8)]
        _handshake(peers)
        own = pltpu.make_async_copy(mine_ref, buf.at[me], lsem)
        own.start()
        cps = []
        for k, to in enumerate(peers):
            cp = pltpu.make_async_remote_copy(src_ref=mine_ref, dst_ref=buf.at[me], send_sem=send.at[k], recv_sem=recv.at[k],
                                              device_id=to, device_id_type=MESH)
            cp.start()
            cps.append(cp)
        for k, (px, py, pc) in enumerate(peers):
            pltpu.make_async_remote_copy(src_ref=mine_ref, dst_ref=buf.at[4 * px + 2 * py + pc], send_sem=send.at[k],
                                         recv_sem=recv.at[k], device_id=(x, y, c), device_id_type=MESH).wait_recv()
        for cp in cps:
            cp.wait_send()
        own.wait()

    landed = pl.kernel(
        exchange, name="small_exchange", out_type=_sds((8, rows, cdim), F32),
        mesh=plsc.ScalarSubcoreMesh(axis_name="sequencer", num_cores=1),
        scratch_types=[pltpu.SemaphoreType.DMA((7,)), pltpu.SemaphoreType.DMA((7,)), pltpu.SemaphoreType.DMA],
        cost_estimate=_copy_cost(rows * cdim * 4, 7),
        compiler_params=pltpu.CompilerParams(collective_id=ALL_ID))(mine)

    def total(buf, o_ref):
        acc = buf[0]
        for d in range(1, 8):
            acc = acc + buf[d]
        o_ref[...] = acc

    return pl.pallas_call(total, in_specs=[vm], out_specs=vm, out_shape=_sds((rows, cdim), F32), name="small_sum")(landed)


def pair_sum(g, theirs, core, tm=256):
    _, r, c = g.shape
    tm = _tile(r // 2, tm)
    nh = r // 2 // tm

    def body(core_ref, a_ref, b_ref, o_ref):
        o_ref[...] = (a_ref[...].astype(F32) + b_ref[...].astype(F32)).astype(BF16)

    blk = (N_CHIPS, tm, c)
    return pl.pallas_call(
        body, grid_spec=pltpu.PrefetchScalarGridSpec(
            num_scalar_prefetch=1, grid=(nh,),
            in_specs=[pl.BlockSpec(blk, lambda i, cr: (0, cr[0] * nh + i, 0)), pl.BlockSpec(blk, lambda i, cr: (0, i, 0))],
            out_specs=pl.BlockSpec(blk, lambda i, cr: (0, i, 0))),
        out_shape=_sds(theirs.shape, BF16), compiler_params=_params(("parallel",)), name="pair_sum")(core, g, theirs)


def chip_sum(own, landed, chip, stack, layer, layers, tm=256):
    _, r, c = own.shape
    tm = _tile(r, tm)

    def body(chip_ref, own_ref, l_ref, *rest):
        acc = own_ref[...].astype(F32)
        for j in range(N_CHIPS - 1):
            acc = acc + l_ref[j].astype(F32)
        rest[-1][...] = acc

    in_specs = [pl.BlockSpec((None, tm, c), lambda i, qr: (qr[0], i, 0)),
                pl.BlockSpec((N_CHIPS - 1, tm, c), lambda i, qr: (0, i, 0))]
    args = [chip, own, landed]
    if stack is not None:
        in_specs.append(ANY)
        args.append(stack)
    return pl.pallas_call(
        body, grid_spec=pltpu.PrefetchScalarGridSpec(
            num_scalar_prefetch=1, grid=(r // tm,), in_specs=in_specs,
            out_specs=pl.BlockSpec((None, tm, c), lambda i, qr: (layer, i, 0))),
        out_shape=_sds((layers, r, c), F32), input_output_aliases={3: 0} if stack is not None else {},
        compiler_params=_params(("parallel",)), name="chip_sum")(*args)


def adamw_joined(w, m, v, g_mine, g_theirs, core, tm=256):
    nl, r, c = w.shape
    tm = _tile(r // 2, tm)
    nh = r // 2 // tm
    bc1 = 1.0 - ADAM_B1 ** ADAM_STEP
    bc2 = 1.0 - ADAM_B2 ** ADAM_STEP

    def body(core_ref, w_ref, m_ref, v_ref, gm_ref, gt_ref, g_ref, d_ref, nm_ref, nv_ref):
        mine = (pl.program_id(1) // nh) == core_ref[0]
        gv = jnp.where(mine, gm_ref[...], gt_ref[...])
        nm = ADAM_B1 * m_ref[...] + (1.0 - ADAM_B1) * gv
        nv = ADAM_B2 * v_ref[...] + (1.0 - ADAM_B2) * (gv * gv)
        g_ref[...] = gv
        d_ref[...] = -ADAM_LR * ((nm / bc1) / (jnp.sqrt(nv / bc2) + ADAM_EPS) + ADAM_WD * w_ref[...])
        nm_ref[...] = nm
        nv_ref[...] = nv

    full = pl.BlockSpec((None, tm, c), lambda l, i, cr: (l, i, 0))
    half = pl.BlockSpec((None, tm, c), lambda l, i, cr: (l, i % nh, 0))
    return pl.pallas_call(
        body, grid_spec=pltpu.PrefetchScalarGridSpec(
            num_scalar_prefetch=1, grid=(nl, r // tm), in_specs=[full, full, full, half, half], out_specs=[full] * 4),
        out_shape=[_sds((nl, r, c), F32)] * 4, compiler_params=_params(("parallel", "parallel")),
        name="adamw_joined")(core, w, m, v, g_mine, g_theirs)


def adamw(w, g, m, v, tm=256):
    shape = w.shape
    c = shape[-1]
    r = w.size // c
    tm = _tile(r, tm)
    bc1 = 1.0 - ADAM_B1 ** ADAM_STEP
    bc2 = 1.0 - ADAM_B2 ** ADAM_STEP

    def body(w_ref, g_ref, m_ref, v_ref, d_ref, nm_ref, nv_ref):
        gv = g_ref[...]
        nm = ADAM_B1 * m_ref[...] + (1.0 - ADAM_B1) * gv
        nv = ADAM_B2 * v_ref[...] + (1.0 - ADAM_B2) * (gv * gv)
        d_ref[...] = -ADAM_LR * ((nm / bc1) / (jnp.sqrt(nv / bc2) + ADAM_EPS) + ADAM_WD * w_ref[...])
        nm_ref[...] = nm
        nv_ref[...] = nv

    outs = pl.pallas_call(
        body, grid=(r // tm,), in_specs=[_rows(tm, c)] * 4, out_specs=[_rows(tm, c)] * 3,
        out_shape=[_sds((r, c), F32)] * 3, compiler_params=_params(("parallel",)), name="adamw")(
            w.reshape(r, c), g.reshape(r, c), m.reshape(r, c), v.reshape(r, c))
    return [o.reshape(shape) for o in outs]


WEIGHTS = ['sc_w_in', 'sc_conv_w', 'sc_w_out', 'mla_w_dq', 'mla_g_q', 'mla_w_uq', 'mla_w_dkv', 'mla_g_kv', 'mla_w_uk',
           'mla_w_uv', 'mla_w_o', 'cf_w_pw1', 'cf_b_pw1', 'cf_dw_w', 'cf_dw_b', 'cf_norm_g', 'cf_norm_b', 'cf_w_pw2',
           'cf_b_pw2', 'ff_w1', 'ff_w2', 'ln_mix_g', 'ln_mix_b', 'ln_ff_g', 'ln_ff_b']
ARGS = ['x'] + WEIGHTS + ['loss_target'] + ['m_' + n for n in WEIGHTS] + ['v_' + n for n in WEIGHTS]


def _sq_relu(h):
    r = jnp.maximum(h.astype(F32), 0.0)
    return (r * r).astype(BF16)


def _mlp_forward(i, x, xb, w1, w2, g, b):
    hb = mm_plain_nn(f"mlp{i}_up", xb, w1, BF16)
    y, yb, xh, rstd = mm_residual_ln(f"mlp{i}_down_ln", hb, w2, x, g, b, a_fn=_sq_relu)
    return (y, yb), dict(xb=xb, hb=hb, xh=xh, rstd=rstd, g=g)


def _mlp_backward(i, dy, sv, w1, w2, dw1, dw2, reduce_after):
    s = dy.shape[0]
    dr, drb, dg, db, _ = ln_backward(f"mlp{i}_ln_bwd", dy, sv["xh"], sv["rstd"], sv["g"])
    tm, tn = _tile(s, 1024), 512

    def epi(acc, e, o):
        o[0][...] = (acc * (2.0 * jnp.maximum(e[0][...].astype(F32), 0.0))).astype(BF16)

    dhb = mm_nt(f"mlp{i}_down_bwd", drb, w2, s, tm, tn, 1024, epi, [_sds((s, w2.k), BF16)], [_ij(tm, tn)],
                [sv["hb"]], [_ij(tm, tn)])[0]
    g_w2 = mm_tn(f"mlp{i}_dw2", sv["hb"], drb, dw2, s, 512, 1024, a_fn=_sq_relu)
    g_w1 = mm_tn(f"mlp{i}_dw1", sv["xb"], dhb, dw1, s, 1024, 512)
    dhb = reduce_after(dhb, {f"w1_{i}": g_w1, f"w2_{i}": g_w2})
    dx = mm_plain_nt(f"mlp{i}_up_bwd", dhb, w1, F32, tn=1024, add=dr, add_scale=ALPHA)
    return dx, dg, db


def kernel(x, sc_w_in, sc_conv_w, sc_w_out, mla_w_dq, mla_g_q, mla_w_uq, mla_w_dkv, mla_g_kv, mla_w_uk, mla_w_uv, mla_w_o, cf_w_pw1, cf_b_pw1, cf_dw_w, cf_dw_b, cf_norm_g, cf_norm_b, cf_w_pw2, cf_b_pw2, ff_w1, ff_w2, ln_mix_g, ln_mix_b, ln_ff_g, ln_ff_b, loss_target, m_sc_w_in, m_sc_conv_w, m_sc_w_out, m_mla_w_dq, m_mla_g_q, m_mla_w_uq, m_mla_w_dkv, m_mla_g_kv, m_mla_w_uk, m_mla_w_uv, m_mla_w_o, m_cf_w_pw1, m_cf_b_pw1, m_cf_dw_w, m_cf_dw_b, m_cf_norm_g, m_cf_norm_b, m_cf_w_pw2, m_cf_b_pw2, m_ff_w1, m_ff_w2, m_ln_mix_g, m_ln_mix_b, m_ln_ff_g, m_ln_ff_b, v_sc_w_in, v_sc_conv_w, v_sc_w_out, v_mla_w_dq, v_mla_g_q, v_mla_w_uq, v_mla_w_dkv, v_mla_g_kv, v_mla_w_uk, v_mla_w_uv, v_mla_w_o, v_cf_w_pw1, v_cf_b_pw1, v_cf_dw_w, v_cf_dw_b, v_cf_norm_g, v_cf_norm_b, v_cf_w_pw2, v_cf_b_pw2, v_ff_w1, v_ff_w2, v_ln_mix_g, v_ln_mix_b, v_ln_ff_g, v_ln_ff_b):
    given = dict(zip(ARGS, (x, sc_w_in, sc_conv_w, sc_w_out, mla_w_dq, mla_g_q, mla_w_uq, mla_w_dkv, mla_g_kv, mla_w_uk, mla_w_uv, mla_w_o, cf_w_pw1, cf_b_pw1, cf_dw_w, cf_dw_b, cf_norm_g, cf_norm_b, cf_w_pw2, cf_b_pw2, ff_w1, ff_w2, ln_mix_g, ln_mix_b, ln_ff_g, ln_ff_b, loss_target, m_sc_w_in, m_sc_conv_w, m_sc_w_out, m_mla_w_dq, m_mla_g_q, m_mla_w_uq, m_mla_w_dkv, m_mla_g_kv, m_mla_w_uk, m_mla_w_uv, m_mla_w_o, m_cf_w_pw1, m_cf_b_pw1, m_cf_dw_w, m_cf_dw_b, m_cf_norm_g, m_cf_norm_b, m_cf_w_pw2, m_cf_b_pw2, m_ff_w1, m_ff_w2, m_ln_mix_g, m_ln_mix_b, m_ln_ff_g, m_ln_ff_b, v_sc_w_in, v_sc_conv_w, v_sc_w_out, v_mla_w_dq, v_mla_g_q, v_mla_w_uq, v_mla_w_dkv, v_mla_g_kv, v_mla_w_uk, v_mla_w_uv, v_mla_w_o, v_cf_w_pw1, v_cf_b_pw1, v_cf_dw_w, v_cf_dw_b, v_cf_norm_g, v_cf_norm_b, v_cf_w_pw2, v_cf_b_pw2, v_ff_w1, v_ff_w2, v_ln_mix_g, v_ln_mix_b, v_ln_ff_g, v_ln_ff_b)))
    s, d = x.shape[1], x.shape[2]
    d_ff = 4 * d
    dq4 = d // N_CHIPS
    xq = lax.axis_index("x") * 2 + lax.axis_index("y")

    w_dkv_pad = jnp.pad(mla_w_dkv[0], ((0, 0), (0, 128 - QK_ROPE)))
    w_uq_pad = jnp.pad(mla_w_uq[0].reshape(Q_LORA, 2, QK_NOPE + QK_ROPE), ((0, 0), (0, 0), (0, HEAD_PAD - QK_NOPE - QK_ROPE)))
    small = jnp.concatenate([
        sc_conv_w.reshape(2 * SC_WIDTH, dq4), cf_b_pw1.reshape(2, dq4), cf_dw_w[0], cf_dw_b, cf_norm_g, cf_norm_b,
        cf_b_pw2, jnp.zeros((5, dq4), F32)], axis=0)
    mlp_w = lambda i: [ff_w1[i].astype(BF16), ff_w2[i].astype(BF16)]
    g_in, g_out, g_w1, g_w2 = [None] * 2, [None] * 2, [None] * DEPTH, [None] * DEPTH
    g_in[0], g_out[0], g_small = gather_shards(
        "gather_mixer0", [sc_w_in[0].astype(BF16), sc_w_out[0].astype(BF16), small], by_columns=(0,))
    (g_w1[0],) = gather_shards("gather_up0", [ff_w1[0].astype(BF16)], by_columns=(0,))
    (g_w2[0],) = gather_shards("gather_down0", [ff_w2[0].astype(BF16)])
    g_dqkv, g_uq, g_uk, g_uv, g_o = gather_shards("gather_mixer1", [
        jnp.concatenate([mla_w_dq[0], w_dkv_pad], axis=1).astype(BF16),
        w_uq_pad.reshape(Q_LORA, 2 * HEAD_PAD).astype(BF16),
        mla_w_uk.reshape(KV_LORA // N_CHIPS, N_HEADS * QK_NOPE).astype(BF16),
        mla_w_uv.reshape(KV_LORA // N_CHIPS, N_HEADS * V_HEAD).astype(BF16), mla_w_o[0].astype(BF16)], by_columns=(1,))
    g_w1[1], g_w2[1] = gather_shards("gather_mlp1", mlp_w(1), by_columns=(0,))
    g_pw1, g_pw2, g_w1[2], g_w2[2] = gather_shards(
        "gather_layer2", [cf_w_pw1[0].astype(BF16), cf_w_pw2[0].astype(BF16)] + mlp_w(2), by_columns=(0, 2))
    g_in[1], g_out[1], g_w1[3], g_w2[3] = gather_shards(
        "gather_layer3", [sc_w_in[1].astype(BF16), sc_w_out[1].astype(BF16)] + mlp_w(3), by_columns=(0, 2))

    wd_t = Q_LORA + KV_LORA + 128
    w_in = [Stk("full", d, 3 * d, g_in[j]) for j in range(2)]
    w_out = [Stk("row", d, d, g_out[j]) for j in range(2)]
    w_dqkv = Stk("row", d, wd_t, g_dqkv)
    w_uq = Stk("full", Q_LORA, N_HEADS * HEAD_PAD, g_uq)
    w_uk = Stk("row", KV_LORA, N_HEADS * QK_NOPE, g_uk)
    w_uv = Stk("row", KV_LORA, N_HEADS * V_HEAD, g_uv)
    w_o = Stk("row", d, d, g_o)
    w_pw1 = Stk("full", d, 2 * d, g_pw1)
    w_pw2 = Stk("row", d, d, g_pw2)
    w_1 = [Stk("full", d, d_ff, g_w1[i]) for i in range(DEPTH)]
    w_2 = [Stk("row", d_ff, d, g_w2[i]) for i in range(DEPTH)]

    def wide(rows):
        return jnp.swapaxes(rows, 0, 1).reshape(rows.shape[1], d)

    conv_w = wide(g_small[:, 0:6]).reshape(2, SC_WIDTH, d)
    b_pw1 = g_small[:, 6:8].reshape(1, 2 * d)
    dw_w = wide(g_small[:, 8:39])
    dw_b, norm_g, norm_b, b_pw2 = (wide(g_small[:, 39 + k:40 + k]) for k in range(4))

    pos = jnp.arange(s, dtype=F32)
    inv_freq = ROPE_THETA ** (-jnp.arange(0, QK_ROPE, 2, dtype=F32) / QK_ROPE)
    ang = pos[:, None] * inv_freq[None, :]
    cos, sin, zero = jnp.cos(ang), jnp.sin(ang), jnp.zeros((s, 128 - QK_ROPE), F32)
    cf = jnp.concatenate([cos, cos, zero], axis=1)
    sf = jnp.concatenate([-sin, sin, zero], axis=1)

    def row(a, i):
        return a[i:i + 1]

    xs = x.reshape(s, d)
    cur = (xs, xs.astype(BF16))
    tape = []
    for i in range(DEPTH):
        mixer, j = i % 3, i // 3
        xf, xb = cur
        lg, lb = row(ln_mix_g, i), row(ln_mix_b, i)
        if mixer == 0:
            u = mm_plain_nn(f"sc{j}_in", xb, w_in[j], F32, tn=3 * dq4)
            gb = short_conv_gate(u, conv_w[j])
            y, yb, xh, rstd = mm_residual_ln(f"sc{j}_out_ln", gb, w_out[j], xf, lg, lb)
            sv = dict(xb=xb, u=u, gb=gb)
        elif mixer == 1:
            t = mm_plain_nn("mla_down", xb, w_dqkv, F32, tn=wd_t // 2)
            cq, ckv, kpe = mla_latents(t, mla_g_q, mla_g_kv, cf, sf)
            qh = mla_queries(cq, w_uq, cf, sf)
            kh = mla_keys(ckv, w_uk, kpe)
            vh = mm_plain_nn("mla_values", ckv, w_uv, BF16, tk=KV_LORA)
            oh = attention(qh, kh, vh)
            y, yb, xh, rstd = mm_residual_ln("mla_out_ln", oh, w_o, xf, lg, lb)
            sv = dict(xb=xb, t=t, cq=cq, ckv=ckv, qh=qh, kh=kh, vh=vh, oh=oh)
        else:
            u = mm_plain_nn("cf_pw1", xb, w_pw1, F32, bias=b_pw1)
            hc = conformer_glu_conv(u, dw_w, dw_b)
            sb = conformer_norm_swish(hc, norm_g, norm_b)
            y, yb, xh, rstd = mm_residual_ln("cf_pw2_ln", sb, w_pw2, xf, lg, lb, bias=b_pw2)
            sv = dict(xb=xb, u=u, hc=hc, sb=sb)
        sv.update(xh=xh, rstd=rstd, g=lg)
        cur, sv_mlp = _mlp_forward(i, y, yb, w_1[i], w_2[i], row(ln_ff_g, i), row(ln_ff_b, i))
        tape.append((sv, sv_mlp))

    dy, loss_part = loss_head(cur[0], loss_target.reshape(s, d))

    grads = {}
    smalls = {}
    g_ln = {n: [None] * DEPTH for n in ("ln_mix_g", "ln_mix_b", "ln_ff_g", "ln_ff_b")}
    conv_grads = [None, None]
    core = lax.axis_index("c").astype(jnp.int32).reshape(1)
    chip = xq.astype(jnp.int32).reshape(1)
    pairs, landed = {}, {}
    ready, theirs = [], {}

    def reduce_after(x, new, early=False):
        out = lax.optimization_barrier((x, *new.values()))
        grads.update(zip(new, out[1:]))
        if early:
            theirs.update(zip(new, pair_exchange(f"pair_exchange_{len(theirs)}", list(out[1:]), True)))
        ready.extend(new)
        return out[0]

    def reduce_layer(i, x):
        late = [n for n in ready if n not in theirs]
        if late:
            theirs.update(zip(late, pair_exchange(f"pair_exchange_layer{i}", [grads[n] for n in late], False)))
        sums = [pair_sum(grads[n], theirs[n], core) for n in ready]
        pairs.update(zip(ready, sums))
        landed.update(zip(ready, chip_exchange(f"chip_exchange_layer{i}", sums)))
        exchanged.append(list(ready))
        ready.clear()
        return lax.optimization_barrier((x, *sums))[0]

    groups = [["in_0", "in_1"], ["out_0", "out_1"], ["dqkv"], ["uq"], ["uk"], ["uv"], ["o"], ["pw1"], ["pw2"],
              [f"w1_{i}" for i in range(DEPTH)], [f"w2_{i}" for i in range(DEPTH)]]
    stacks = [None] * len(groups)
    exchanged = []

    def sum_layer(x, last=False):
        names = exchanged.pop(0)
        if last:
            out = lax.optimization_barrier((x, *[landed[n] for n in names]))
            landed.update(zip(names, out[1:]))
        new = []
        for n in names:
            k = next(k for k, members in enumerate(groups) if n in members)
            stacks[k] = chip_sum(pairs[n], landed[n], chip, stacks[k], groups[k].index(n), len(groups[k]))
            new.append(stacks[k])
        return out[0] if last else lax.optimization_barrier((x, *new))[0]

    for i in reversed(range(DEPTH)):
        mixer, j = i % 3, i // 3
        sv, sv_mlp = tape[i]
        dy, g_ln["ln_ff_g"][i], g_ln["ln_ff_b"][i] = _mlp_backward(
            i, dy, sv_mlp, w_1[i], w_2[i], Stk("col", d, d_ff), Stk("row", d_ff, d),
            lambda x_, new: reduce_after(x_, new, early=i > 0))
        if i == 0:
            dy = reduce_layer("0_mlp", dy)
        dr, drb, g_ln["ln_mix_g"][i], g_ln["ln_mix_b"][i], dr_sum = ln_backward(
            f"mix{i}_ln_bwd", dy, sv["xh"], sv["rstd"], sv["g"])
        if mixer == 0:
            dgate = mm_plain_nt(f"sc{j}_out_bwd", drb, w_out[j], F32)
            dw_out = mm_tn(f"sc{j}_dw_out", sv["gb"], drb, Stk("row", d, d), s, 512, 1024)
            du, conv_grads[j] = short_conv_gate_bwd(sv["u"], conv_w[j], dgate)
            nb = d // 256
            dw_in = mm_tn(
                f"sc{j}_dw_in", sv["xb"], du, Stk("col", d, 3 * d), s, 1024, 256,
                b_spec=pl.BlockSpec((None, s, 256), lambda i_, j_, k_: (j_ // nb, k_, j_ % nb)))
            du = reduce_after(du, {f"in_{j}": dw_in, f"out_{j}": dw_out})
            dy = mm_plain_nt(
                f"sc{j}_in_bwd", du, w_in[j], F32, tn=1024, tk=d, add=dr, add_scale=ALPHA,
                a_spec_fn=(s, lambda tm, tk: pl.BlockSpec((None, tm, tk), lambda i_, j_, k_: (k_, i_, 0))))
        elif mixer == 1:
            do = mm_plain_nt("mla_out_bwd", drb, w_o, BF16)
            g_o = mm_tn("mla_dw_o", sv["oh"], drb, Stk("row", d, d), s, 512, 1024)
            dqh, dkh, dvh = attention_bwd(sv["qh"], sv["kh"], sv["vh"], do)
            dql, dkn, dkpe = mla_unrope_grads(dqh, dkh, cf, sf)
            g_uq = mm_tn("mla_dw_uq", sv["cq"], dql, Stk("col", Q_LORA, N_HEADS * HEAD_PAD), s, Q_LORA, 512)
            dcq = mm_plain_nt("mla_uq_bwd", dql, w_uq, F32, tn=Q_LORA)
            g_uk = mm_tn("mla_dw_uk", sv["ckv"], dkn, Stk("row", KV_LORA, N_HEADS * QK_NOPE), s, KV_LORA, 1024)
            g_uv = mm_tn("mla_dw_uv", sv["ckv"], dvh, Stk("row", KV_LORA, N_HEADS * V_HEAD), s, KV_LORA, 1024)
            dckv = mm_plain_nt("mla_uk_bwd", dkn, w_uk, F32, tn=KV_LORA)
            dckv = mm_plain_nt("mla_uv_bwd", dvh, w_uv, F32, tn=KV_LORA, add=dckv)
            dt, smalls["g_q"], smalls["g_kv"] = mla_latents_bwd(sv["t"], mla_g_q, mla_g_kv, cf, sf, dcq, dckv, dkpe)
            g_dqkv = mm_tn("mla_dw_down", sv["xb"], dt, Stk("row", d, wd_t), s, 512, wd_t)
            dt = reduce_after(dt, {"dqkv": g_dqkv, "uq": g_uq, "uk": g_uk, "uv": g_uv, "o": g_o})
            dy = mm_plain_nt("mla_down_bwd", dt, w_dqkv, F32, tk=wd_t, add=dr, add_scale=ALPHA)
        else:
            dsw = mm_plain_nt("cf_pw2_bwd", drb, w_pw2, F32)
            g_pw2 = mm_tn("cf_dw_pw2", sv["sb"], drb, Stk("row", d, d), s, 512, 1024)
            smalls["b_pw2"] = dr_sum
            dhc, smalls["norm_g"], smalls["norm_b"] = conformer_norm_swish_bwd(sv["hc"], norm_g, norm_b, dsw)
            du, smalls["b_pw1"], smalls["dw_w"], smalls["dw_b"] = conformer_glu_conv_bwd(sv["u"], dw_w, dhc)
            nb = d // 512
            g_pw1 = mm_tn(
                "cf_dw_pw1", sv["xb"], du, Stk("col", d, 2 * d), s, 1024, 512,
                b_spec=pl.BlockSpec((None, s, 512), lambda i_, j_, k_: (j_ // nb, k_, j_ % nb)))
            du = reduce_after(du, {"pw1": g_pw1, "pw2": g_pw2})
            dy = mm_plain_nt(
                "cf_pw1_bwd", du, w_pw1, F32, tn=1024, tk=d, add=dr, add_scale=ALPHA,
                a_spec_fn=(s, lambda tm, tk: pl.BlockSpec((None, tm, tk), lambda i_, j_, k_: (k_, i_, 0))))
        if i < DEPTH - 1:
            dy = sum_layer(dy)
        dy = reduce_layer(i, dy)
    dy = sum_layer(sum_layer(dy, last=True), last=True)
    grad_x = dy.reshape(1, s, d)

    mine = stacks
    other = (pair_share("pair_share_mixers", mine[:9]) + pair_share("pair_share_up", mine[9:10])
             + pair_share("pair_share_down", mine[10:]))

    def padded(get):
        dqkv = jnp.concatenate([get("mla_w_dq")[0], jnp.pad(get("mla_w_dkv")[0], ((0, 0), (0, 128 - QK_ROPE)))], axis=1)
        uq = jnp.pad(get("mla_w_uq")[0].reshape(Q_LORA, 2, QK_NOPE + QK_ROPE),
                     ((0, 0), (0, 0), (0, HEAD_PAD - QK_NOPE - QK_ROPE))).reshape(Q_LORA, 2 * HEAD_PAD)
        return [get("sc_w_in"), get("sc_w_out"), dqkv[None], uq[None],
                get("mla_w_uk").reshape(1, KV_LORA // N_CHIPS, d), get("mla_w_uv").reshape(1, KV_LORA // N_CHIPS, d),
                get("mla_w_o"), get("cf_w_pw1"), get("cf_w_pw2"), get("ff_w1"), get("ff_w2")]

    w_l, m_l, v_l = (padded(lambda n, p=p: given[p + n]) for p in ("", "m_", "v_"))
    res = [adamw_joined(w_l[k], m_l[k], v_l[k], mine[k], other[k], core) for k in range(len(groups))]

    def unpadded(k):
        r_in, r_out, r_dqkv, r_uq, r_uk, r_uv, r_o, r_pw1, r_pw2, r_w1, r_w2 = (r[k] for r in res)
        return {
            "sc_w_in": r_in, "sc_w_out": r_out, "mla_w_dq": r_dqkv[:, :, 0:Q_LORA],
            "mla_w_dkv": r_dqkv[:, :, Q_LORA:Q_LORA + KV_LORA + QK_ROPE],
            "mla_w_uq": r_uq.reshape(1, Q_LORA, 2, HEAD_PAD)[:, :, :, 0:QK_NOPE + QK_ROPE].reshape(mla_w_uq.shape),
            "mla_w_uk": r_uk.reshape(mla_w_uk.shape), "mla_w_uv": r_uv.reshape(mla_w_uv.shape),
            "mla_w_o": r_o, "cf_w_pw1": r_pw1, "cf_w_pw2": r_pw2, "ff_w1": r_w1, "ff_w2": r_w2}

    big_g, big_d, big_m, big_v = (unpadded(k) for k in range(4))

    pad_row = lambda a: jnp.pad(a, ((0, 0), (0, d - a.shape[1])))
    small_parts = ([g for n in ("ln_mix_g", "ln_mix_b", "ln_ff_g", "ln_ff_b") for g in g_ln[n]]
                   + [pad_row(smalls["g_q"]), pad_row(smalls["g_kv"]), conv_grads[0], conv_grads[1],
                      smalls["b_pw1"].reshape(2, d), smalls["dw_w"], smalls["dw_b"], smalls["norm_g"], smalls["norm_b"],
                      smalls["b_pw2"], loss_part])
    red = all_reduce_small(small_parts, 64)
    loss = red[61, 0]

    def shard(rows):
        return lax.dynamic_slice_in_dim(rows, xq * dq4, dq4, axis=1)

    gw = {
        **big_g,
        "ln_mix_g": red[0:4], "ln_mix_b": red[4:8], "ln_ff_g": red[8:12], "ln_ff_b": red[12:16],
        "mla_g_q": red[16:17, 0:Q_LORA], "mla_g_kv": red[17:18, 0:KV_LORA],
        "sc_conv_w": shard(red[18:24]).reshape(2, SC_WIDTH, dq4),
        "cf_b_pw1": lax.dynamic_slice_in_dim(red[24:26].reshape(1, 2 * d), xq * 2 * dq4, 2 * dq4, axis=1),
        "cf_dw_w": shard(red[26:57])[None], "cf_dw_b": shard(red[57:58]), "cf_norm_g": shard(red[58:59]),
        "cf_norm_b": shard(red[59:60]), "cf_b_pw2": shard(red[60:61]),
    }

    upd = {n: [big_d[n], big_m[n], big_v[n]] for n in big_g}

    def pack(names, width, get):
        return jnp.concatenate([get(n).reshape(-1, width) for n in names], axis=0)

    def unpack(names, packed):
        out, at = {}, 0
        for n in names:
            rows = given[n].size // packed.shape[1]
            out[n] = packed[at:at + rows].reshape(given[n].shape)
            at += rows
        return out

    rep = ["ln_mix_g", "ln_mix_b", "ln_ff_g", "ln_ff_b"]
    shd = ["sc_conv_w", "cf_b_pw1", "cf_dw_w", "cf_dw_b", "cf_norm_g", "cf_norm_b", "cf_b_pw2"]
    for names, width in ((rep, d), (shd, dq4), (["mla_g_q"], Q_LORA), (["mla_g_kv"], KV_LORA)):
        res = adamw(pack(names, width, lambda n: given[n]), pack(names, width, lambda n: gw[n]),
                    pack(names, width, lambda n: given["m_" + n]), pack(names, width, lambda n: given["v_" + n]), tm=4096)
        parts = [unpack(names, r) for r in res]
        for n in names:
            upd[n] = [p[n] for p in parts]

    return (loss, grad_x, *[gw[n].reshape(given[n].shape) for n in WEIGHTS], *[upd[n][0] for n in WEIGHTS],
            *[upd[n][1] for n in WEIGHTS], *[upd[n][2] for n in WEIGHTS])
```

```python
import jax
import jax.numpy as jnp
from jax import lax
from jax.experimental import pallas as pl
from jax.experimental.pallas import tpu as pltpu
from jax.experimental.pallas import tpu_sc as plsc

F32 = jnp.float32
BF16 = jnp.bfloat16
MESH = pl.DeviceIdType.MESH

DEPTH = 4
ALPHA = (2.0 * DEPTH) ** 0.25
LN_EPS = 1e-5
RMS_EPS = 1e-6
CHUNK_SHIFT = 6
N_HEADS = 8
QK_NOPE = 128
QK_ROPE = 64
V_HEAD = 128
HEAD_PAD = 256
Q_LORA = 384
KV_LORA = 256
ROPE_THETA = 10000.0
SC_WIDTH = 3
CONF_WIDTH = 31
CONV_PAD = 32
CONV_CHUNK = 64
N_CHIPS = 4
ATTN_SCALE = (QK_NOPE + QK_ROPE) ** -0.5

ADAM_LR = 0.001
ADAM_B1 = 0.9
ADAM_B2 = 0.999
ADAM_EPS = 1e-08
ADAM_WD = 0.01
ADAM_STEP = 10

VMEM_LIMIT = 56 * 2**20

NN = (((1,), (0,)), ((), ()))
NT = (((1,), (1,)), ((), ()))
TN = (((0,), (0,)), ((), ()))


def _params(sem=None):
    return pltpu.CompilerParams(dimension_semantics=sem, vmem_limit_bytes=VMEM_LIMIT)


class Stk:
    def __init__(self, kind, k, n, arr=None, layers=None, layer=None):
        self.kind, self.k, self.n, self.layers, self.layer = kind, k, n, layers, layer
        self.plain = (kind == "row" and layers is None) or kind == "full"
        self.kloc = k // N_CHIPS if kind == "row" else k
        self.nloc = n // N_CHIPS if kind == "col" else n
        if arr is not None and self.plain:
            arr = arr.reshape(k, n)
        self.arr = arr

    @property
    def shape(self):
        if self.plain:
            return (self.k, self.n)
        lead = (N_CHIPS,) if self.layers is None else (N_CHIPS, self.layers)
        return lead + (self.kloc, self.nloc)

    def spec(self, bk, bn, f):
        if self.plain:
            return pl.BlockSpec((bk, bn), f)
        assert self.kloc % bk == 0 and self.nloc % bn == 0, (self.kloc, bk, self.nloc, bn)
        pk, pn = self.kloc // bk, self.nloc // bn
        kind, layer = self.kind, self.layer

        def imap(*g):
            kb, nb = f(*g)
            if kind == "row":
                q, kb, nb = kb // pk, kb % pk, nb
            else:
                q, kb, nb = nb // pn, kb, nb % pn
            return (q, kb, nb) if layer is None else (q, layer, kb, nb)

        block = (None, bk, bn) if layer is None else (None, None, bk, bn)
        return pl.BlockSpec(block, imap)


def _mm(name, mode, a, b, grid, a_spec, b_spec, acc_shape, extras, extra_specs, out_shapes, out_specs, epi, a_fn=None):
    nk = grid[2]
    ne = len(extras)

    def body(*refs):
        a_ref, b_ref = refs[0], refs[1]
        e_refs = refs[2:2 + ne]
        av = a_ref[...] if a_fn is None else a_fn(a_ref[...])
        part = lax.dot_general(av, b_ref[...], mode, preferred_element_type=F32)
        if nk == 1:
            epi(part, e_refs, refs[2 + ne:])
            return
        o_refs = refs[2 + ne:-1]
        acc = refs[-1]
        k = pl.program_id(2)

        @pl.when(k == 0)
        def _():
            acc[...] = part

        @pl.when(k > 0)
        def _():
            acc[...] += part

        @pl.when(k == nk - 1)
        def _():
            epi(acc[...], e_refs, o_refs)

    return pl.pallas_call(
        body, grid=grid, in_specs=[a_spec, b_spec, *extra_specs], out_specs=out_specs, out_shape=out_shapes,
        scratch_shapes=[pltpu.VMEM(acc_shape, F32)] if nk > 1 else [],
        compiler_params=_params(("parallel", "parallel", "arbitrary")), name=name)(a, b, *extras)


def _tile(n, t):
    t = min(n, t)
    while n % t:
        t -= 8
    assert t > 0, (n, t)
    return t


def mm_nn(name, a, w, tm, tn, tk, epi, out_shapes, out_specs, extras=(), extra_specs=(), a_spec=None, a_fn=None):
    m = a.shape[0]
    tm, tn, tk = _tile(m, tm), _tile(w.n, tn), _tile(w.k, tk)
    grid = (m // tm, w.n // tn, w.k // tk)
    a_spec = a_spec or pl.BlockSpec((tm, tk), lambda i, j, k: (i, k))
    b_spec = w.spec(tk, tn, lambda i, j, k: (k, j))
    return _mm(name, NN, a, w.arr, grid, a_spec, b_spec, (tm, tn), extras, extra_specs, out_shapes, out_specs, epi, a_fn)


def mm_nt(name, a, w, m, tm, tn, tk, epi, out_shapes, out_specs, extras=(), extra_specs=(), a_spec=None):
    tm, tn, tk = _tile(m, tm), _tile(w.k, tn), _tile(w.n, tk)
    grid = (m // tm, w.k // tn, w.n // tk)
    a_spec = a_spec or pl.BlockSpec((tm, tk), lambda i, j, k: (i, k))
    b_spec = w.spec(tn, tk, lambda i, j, k: (j, k))
    return _mm(name, NT, a, w.arr, grid, a_spec, b_spec, (tm, tn), extras, extra_specs, out_shapes, out_specs, epi)


def mm_tn(name, a, b, dw, s, tm=512, tn=512, tk=4096, a_spec=None, b_spec=None, a_fn=None):
    tm, tn, tk = _tile(dw.k, tm), _tile(dw.n, tn), _tile(s, tk)
    grid = (dw.k // tm, dw.n // tn, s // tk)
    a_spec = a_spec or pl.BlockSpec((tk, tm), lambda i, j, k: (k, i))
    b_spec = b_spec or pl.BlockSpec((tk, tn), lambda i, j, k: (k, j))

    def epi(acc, e, o):
        o[0][...] = acc.astype(BF16)

    out = _mm(name, TN, a, b, grid, a_spec, b_spec, (tm, tn), (), (), [jax.ShapeDtypeStruct(dw.shape, BF16)],
              [dw.spec(tm, tn, lambda i, j, k: (i, j))], epi, a_fn)[0]
    return out.reshape(N_CHIPS, dw.k // N_CHIPS, dw.n) if dw.plain else out


def _sds(shape, dtype):
    return jax.ShapeDtypeStruct(shape, dtype)


def _ij(tm, tn):
    return pl.BlockSpec((tm, tn), lambda i, j, k: (i, j))


def _i0(tm, c):
    return pl.BlockSpec((tm, c), lambda i, j, k: (i, 0))


def _0j(r, tn):
    return pl.BlockSpec((r, tn), lambda i, j, k: (0, j))


def _layer_norm_rows(r, g, b):
    mu = jnp.mean(r, axis=-1, keepdims=True)
    d = r - mu
    var = jnp.mean(d * d, axis=-1, keepdims=True)
    rstd = lax.rsqrt(var + LN_EPS)
    xh = d * rstd
    return xh * g + b, xh, rstd


def mm_residual_ln(name, a, w, x, g, b, bias=None, tm=512, tk=1024, a_fn=None):
    s, d = x.shape
    tm = _tile(s, tm)
    extras = [x, g, b] + ([bias] if bias is not None else [])
    especs = [_i0(tm, d), _0j(1, d), _0j(1, d)] + ([_0j(1, d)] if bias is not None else [])

    def epi(acc, e, o):
        r = ALPHA * e[0][...] + acc
        if bias is not None:
            r = r + e[3][...]
        y, xh, rstd = _layer_norm_rows(r, e[1][...], e[2][...])
        o[0][...] = y
        o[1][...] = y.astype(BF16)
        o[2][...] = xh
        o[3][...] = rstd

    return mm_nn(name, a, w, tm, d, tk, epi,
                 [_sds((s, d), F32), _sds((s, d), BF16), _sds((s, d), F32), _sds((s, 1), F32)],
                 [_i0(tm, d), _i0(tm, d), _i0(tm, d), _i0(tm, 1)], extras, especs, a_fn=a_fn)


def mm_plain_nn(name, a, w, out_dtype, tm=1024, tn=512, tk=1024, bias=None):
    m = a.shape[0]
    tm, tn = _tile(m, tm), _tile(w.n, tn)
    if w.kind == "col":
        tn = _tile(w.nloc, tn)

    def epi(acc, e, o):
        if bias is not None:
            acc = acc + e[0][...]
        o[0][...] = acc.astype(out_dtype)

    extras, especs = ([bias], [_0j(1, tn)]) if bias is not None else ((), ())
    return mm_nn(name, a, w, tm, tn, tk, epi, [_sds((m, w.n), out_dtype)], [_ij(tm, tn)], extras, especs)[0]


def mm_plain_nt(name, a, w, out_dtype, tm=1024, tn=512, tk=1024, add=None, add_scale=1.0, a_spec_fn=None):
    m = a.shape[0] if a_spec_fn is None else a_spec_fn[0]
    tm, tn = _tile(m, tm), _tile(w.k, tn)
    tk = _tile(w.n, tk)
    if w.kind == "col":
        tk = _tile(w.nloc, tk)
    if w.kind == "row" and not w.plain:
        tn = _tile(w.kloc, tn)

    def epi(acc, e, o):
        if add is not None:
            acc = acc + add_scale * e[0][...].astype(F32)
        o[0][...] = acc.astype(out_dtype)

    extras, especs = ([add], [_ij(tm, tn)]) if add is not None else ((), ())
    a_spec = None if a_spec_fn is None else a_spec_fn[1](tm, tk)
    return mm_nt(name, a, w, m, tm, tn, tk, epi, [_sds((m, w.k), out_dtype)], [_ij(tm, tn)], extras, especs,
                 a_spec=a_spec)[0]


def _rows(tm, c):
    return pl.BlockSpec((tm, c), lambda i: (i, 0))


def _fix(shape):
    nd = len(shape)
    return pl.BlockSpec(shape, lambda i: (0,) * nd)


def _accumulate(ref, val):
    @pl.when(pl.program_id(0) == 0)
    def _():
        ref[...] = jnp.zeros_like(ref)

    ref[...] += val


def ln_backward(name, dy, xhat, rstd, g, tm=256):
    s, d = dy.shape
    tm = _tile(s, tm)

    def body(dy_ref, xh_ref, rstd_ref, g_ref, dr_ref, drb_ref, dg_ref, db_ref, ds_ref):
        dyv, xh = dy_ref[...], xh_ref[...]
        dxh = dyv * g_ref[...]
        m1 = jnp.mean(dxh, axis=-1, keepdims=True)
        m2 = jnp.mean(dxh * xh, axis=-1, keepdims=True)
        dr = rstd_ref[...] * (dxh - m1 - xh * m2)
        dr_ref[...] = dr
        drb_ref[...] = dr.astype(BF16)
        _accumulate(dg_ref, jnp.sum(dyv * xh, axis=0, keepdims=True))
        _accumulate(db_ref, jnp.sum(dyv, axis=0, keepdims=True))
        _accumulate(ds_ref, jnp.sum(dr, axis=0, keepdims=True))

    return pl.pallas_call(
        body, grid=(s // tm,),
        in_specs=[_rows(tm, d), _rows(tm, d), _rows(tm, 1), _fix((1, d))],
        out_specs=[_rows(tm, d), _rows(tm, d), _fix((1, d)), _fix((1, d)), _fix((1, d))],
        out_shape=[_sds((s, d), F32), _sds((s, d), BF16), _sds((1, d), F32), _sds((1, d), F32), _sds((1, d), F32)],
        compiler_params=_params(("arbitrary",)), name=name)(dy, xhat, rstd, g)


def loss_head(y, target, tm=256):
    s, d = y.shape
    tm = _tile(s, tm)

    def body(y_ref, t_ref, dy_ref, loss_ref):
        e = y_ref[...] - t_ref[...]
        dy_ref[...] = e * (1.0 / d)
        part = 0.5 * jnp.sum(jnp.mean(e * e, axis=-1, keepdims=True), axis=0, keepdims=True)
        _accumulate(loss_ref, jnp.broadcast_to(part, (1, d)))

    return pl.pallas_call(
        body, grid=(s // tm,), in_specs=[_rows(tm, d), _rows(tm, d)],
        out_specs=[_rows(tm, d), _fix((1, d))], out_shape=[_sds((s, d), F32), _sds((1, d), F32)],
        compiler_params=_params(("arbitrary",)), name="loss_head")(y, target)


def _cols(s, tc, off=0):
    return pl.BlockSpec((s, tc), lambda i: (0, i + off))


def _shift_down(z, sft, rows):
    return jnp.where(rows >= sft, pltpu.roll(z, sft, 0), 0.0)


def _shift_up(z, sft, rows, s):
    return jnp.where(rows < s - sft, pltpu.roll(z, (s - sft) % s, 0), 0.0)


def short_conv_gate(u, conv_w, tc=256):
    s, d3 = u.shape
    d = d3 // 3
    nb = d // tc

    def body(b_ref, c_ref, h_ref, w_ref, o_ref):
        rows = lax.broadcasted_iota(jnp.int32, (s, tc), 0)
        z = c_ref[...] * h_ref[...]
        cz = jnp.zeros((s, tc), F32)
        for k in range(SC_WIDTH):
            sft = SC_WIDTH - 1 - k
            cz = cz + w_ref[pl.ds(k, 1), :] * (_shift_down(z, sft, rows) if sft else z)
        o_ref[...] = (b_ref[...] * cz).astype(BF16)

    return pl.pallas_call(
        body, grid=(nb,),
        in_specs=[_cols(s, tc), _cols(s, tc, nb), _cols(s, tc, 2 * nb), _cols(SC_WIDTH, tc)],
        out_specs=_cols(s, tc), out_shape=_sds((s, d), BF16),
        compiler_params=_params(("parallel",)), name="short_conv_gate")(u, u, u, conv_w)


def short_conv_gate_bwd(u, conv_w, dg, tc=256):
    s, d3 = u.shape
    d = d3 // 3
    nb = d // tc

    def body(b_ref, c_ref, h_ref, w_ref, dg_ref, du_ref, dw_ref):
        rows = lax.broadcasted_iota(jnp.int32, (s, tc), 0)
        c, h, dgv = c_ref[...], h_ref[...], dg_ref[...]
        z = c * h
        dcz = dgv * b_ref[...]
        cz = jnp.zeros((s, tc), F32)
        dz = jnp.zeros((s, tc), F32)
        for k in range(SC_WIDTH):
            sft = SC_WIDTH - 1 - k
            zs = _shift_down(z, sft, rows) if sft else z
            wk = w_ref[pl.ds(k, 1), :]
            cz = cz + wk * zs
            dz = dz + wk * (_shift_up(dcz, sft, rows, s) if sft else dcz)
            dw_ref[pl.ds(k, 1), :] = jnp.sum(dcz * zs, axis=0, keepdims=True)
        du_ref[0] = (dgv * cz).astype(BF16)
        du_ref[1] = (dz * h).astype(BF16)
        du_ref[2] = (dz * c).astype(BF16)

    return pl.pallas_call(
        body, grid=(nb,),
        in_specs=[_cols(s, tc), _cols(s, tc, nb), _cols(s, tc, 2 * nb), _cols(SC_WIDTH, tc), _cols(s, tc)],
        out_specs=[pl.BlockSpec((3, s, tc), lambda i: (0, 0, i)), _cols(SC_WIDTH, tc)],
        out_shape=[_sds((3, s, d), BF16), _sds((SC_WIDTH, d), F32)],
        compiler_params=_params(("parallel",)), name="short_conv_gate_bwd")(u, u, u, conv_w, dg)


def _store_shifted_down(ref, z, rows):
    s, tc = z.shape
    for b in range(8):
        ref[b, pl.ds(0, CONV_PAD), :] = jnp.zeros((CONV_PAD, tc), F32)
        ref[b, pl.ds(CONV_PAD, s), :] = z if b == 0 else _shift_down(z, b, rows)


def _store_shifted_up(ref, z, rows):
    s, tc = z.shape
    for b in range(8):
        ref[b, pl.ds(0, s), :] = z if b == 0 else _shift_up(z, b, rows, s)
        ref[b, pl.ds(s, CONV_PAD), :] = jnp.zeros((CONV_PAD, tc), F32)


def conformer_glu_conv(u, dw_w, dw_b, tc=128):
    s, d2 = u.shape
    d = d2 // 2
    nb = d // tc

    ch = min(CONV_CHUNK, s)

    def body(a_ref, g_ref, w_ref, b_ref, o_ref, down):
        rows = lax.broadcasted_iota(jnp.int32, (s, tc), 0)
        _store_shifted_down(down, a_ref[...] * jax.nn.sigmoid(g_ref[...]), rows)

        def chunk(ci, carry):
            r0 = pl.multiple_of(ci * ch, ch)
            acc = jnp.broadcast_to(b_ref[...], (ch, tc))
            for k in range(CONF_WIDTH):
                sft = CONF_WIDTH - 1 - k
                acc = acc + w_ref[pl.ds(k, 1), :] * down[sft % 8, pl.ds(CONV_PAD + r0 - (sft // 8) * 8, ch), :]
            o_ref[pl.ds(r0, ch), :] = acc
            return carry

        lax.fori_loop(0, s // ch, chunk, 0)

    return pl.pallas_call(
        body, grid=(nb,),
        in_specs=[_cols(s, tc), _cols(s, tc, nb), _cols(CONF_WIDTH, tc), _cols(1, tc)],
        out_specs=_cols(s, tc), out_shape=_sds((s, d), F32),
        scratch_shapes=[pltpu.VMEM((8, CONV_PAD + s, tc), F32)],
        compiler_params=_params(("parallel",)), name="conformer_glu_conv")(u, u, dw_w, dw_b)


def conformer_glu_conv_bwd(u, dw_w, dhc, tc=128):
    s, d2 = u.shape
    d = d2 // 2
    nb = d // tc
    ch = min(CONV_CHUNK, s)

    def body(a_ref, g_ref, w_ref, dhc_ref, du_ref, dbias_ref, dw_ref, db_ref, down, up, dw_acc, dh_buf):
        rows = lax.broadcasted_iota(jnp.int32, (s, tc), 0)
        a = a_ref[...]
        sg = jax.nn.sigmoid(g_ref[...])
        dhcv = dhc_ref[...]
        _store_shifted_down(down, a * sg, rows)
        _store_shifted_up(up, dhcv, rows)
        dw_acc[...] = jnp.zeros_like(dw_acc)

        def chunk(ci, carry):
            r0 = pl.multiple_of(ci * ch, ch)
            dc = dhc_ref[pl.ds(r0, ch), :]
            dh = jnp.zeros((ch, tc), F32)
            for k in range(CONF_WIDTH):
                sft = CONF_WIDTH - 1 - k
                a8, b = (sft // 8) * 8, sft % 8
                dh = dh + w_ref[pl.ds(k, 1), :] * up[b, pl.ds(r0 + a8, ch), :]
                prod = dc * down[b, pl.ds(CONV_PAD + r0 - a8, ch), :]
                dw_acc[k] += jnp.sum(prod.reshape(ch // 8, 8, tc), axis=0)
            dh_buf[pl.ds(r0, ch), :] = dh
            return carry

        lax.fori_loop(0, s // ch, chunk, 0)
        dh = dh_buf[...]
        da = dh * sg
        dgate = dh * a * sg * (1.0 - sg)
        du_ref[0] = da.astype(BF16)
        du_ref[1] = dgate.astype(BF16)
        dbias_ref[pl.ds(0, 1), :] = jnp.sum(da, axis=0, keepdims=True)
        dbias_ref[pl.ds(1, 1), :] = jnp.sum(dgate, axis=0, keepdims=True)
        db_ref[...] = jnp.sum(dhcv, axis=0, keepdims=True)
        for k in range(CONF_WIDTH):
            dw_ref[pl.ds(k, 1), :] = jnp.sum(dw_acc[k], axis=0, keepdims=True)

    return pl.pallas_call(
        body, grid=(nb,),
        in_specs=[_cols(s, tc), _cols(s, tc, nb), _cols(CONF_WIDTH, tc), _cols(s, tc)],
        out_specs=[pl.BlockSpec((2, s, tc), lambda i: (0, 0, i)), _cols(2, tc), _cols(CONF_WIDTH, tc), _cols(1, tc)],
        out_shape=[_sds((2, s, d), BF16), _sds((2, d), F32), _sds((CONF_WIDTH, d), F32), _sds((1, d), F32)],
        scratch_shapes=[pltpu.VMEM((8, CONV_PAD + s, tc), F32), pltpu.VMEM((8, CONV_PAD + s, tc), F32),
                        pltpu.VMEM((CONF_WIDTH + 1, 8, tc), F32), pltpu.VMEM((s, tc), F32)],
        compiler_params=_params(("parallel",)), name="conformer_glu_conv_bwd")(u, u, dw_w, dhc)


def conformer_norm_swish(hc, g, b, tm=256):
    s, d = hc.shape
    tm = _tile(s, tm)

    def body(h_ref, g_ref, b_ref, o_ref):
        n, _, _ = _layer_norm_rows(h_ref[...], g_ref[...], b_ref[...])
        o_ref[...] = (n * jax.nn.sigmoid(n)).astype(BF16)

    return pl.pallas_call(
        body, grid=(s // tm,), in_specs=[_rows(tm, d), _fix((1, d)), _fix((1, d))], out_specs=_rows(tm, d),
        out_shape=_sds((s, d), BF16), compiler_params=_params(("parallel",)), name="conformer_norm_swish")(hc, g, b)


def conformer_norm_swish_bwd(hc, g, b, ds, tm=256):
    s, d = hc.shape
    tm = _tile(s, tm)

    def body(h_ref, g_ref, b_ref, ds_ref, dh_ref, dg_ref, db_ref):
        n, nh, rstd = _layer_norm_rows(h_ref[...], g_ref[...], b_ref[...])
        sg = jax.nn.sigmoid(n)
        dn = ds_ref[...] * (sg * (1.0 + n * (1.0 - sg)))
        dnh = dn * g_ref[...]
        m1 = jnp.mean(dnh, axis=-1, keepdims=True)
        m2 = jnp.mean(dnh * nh, axis=-1, keepdims=True)
        dh_ref[...] = rstd * (dnh - m1 - nh * m2)
        _accumulate(dg_ref, jnp.sum(dn * nh, axis=0, keepdims=True))
        _accumulate(db_ref, jnp.sum(dn, axis=0, keepdims=True))

    return pl.pallas_call(
        body, grid=(s // tm,), in_specs=[_rows(tm, d), _fix((1, d)), _fix((1, d)), _rows(tm, d)],
        out_specs=[_rows(tm, d), _fix((1, d)), _fix((1, d))],
        out_shape=[_sds((s, d), F32), _sds((1, d), F32), _sds((1, d), F32)],
        compiler_params=_params(("arbitrary",)), name="conformer_norm_swish_bwd")(hc, g, b, ds)


def _swap_halves(x):
    lane = lax.broadcasted_iota(jnp.int32, x.shape, 1)
    return jnp.where(lane < QK_ROPE // 2, pltpu.roll(x, 128 - QK_ROPE // 2, 1), pltpu.roll(x, QK_ROPE // 2, 1))


def _rope(x, cf, sf):
    return x * cf + _swap_halves(x) * sf


def _unrope(dx, cf, sf):
    return dx * cf - _swap_halves(dx) * sf


def _rms_rows(x, g):
    r = lax.rsqrt(jnp.mean(x * x, axis=-1, keepdims=True) + RMS_EPS)
    return x * r, r


def mla_latents(t, g_q, g_kv, cf, sf, tm=256):
    s = t.shape[0]
    tm = _tile(s, tm)

    def body(t_ref, gq_ref, gkv_ref, cf_ref, sf_ref, cq_ref, ckv_ref, kpe_ref):
        xq, _ = _rms_rows(t_ref[:, 0:Q_LORA], gq_ref[...])
        cq_ref[...] = (xq * gq_ref[...]).astype(BF16)
        xkv, _ = _rms_rows(t_ref[:, Q_LORA:Q_LORA + KV_LORA], gkv_ref[...])
        ckv_ref[...] = (xkv * gkv_ref[...]).astype(BF16)
        kpe_ref[...] = _rope(t_ref[:, Q_LORA + KV_LORA:], cf_ref[...], sf_ref[...]).astype(BF16)

    w = Q_LORA + KV_LORA + 128
    return pl.pallas_call(
        body, grid=(s // tm,),
        in_specs=[_rows(tm, w), _fix((1, Q_LORA)), _fix((1, KV_LORA)), _rows(tm, 128), _rows(tm, 128)],
        out_specs=[_rows(tm, Q_LORA), _rows(tm, KV_LORA), _rows(tm, 128)],
        out_shape=[_sds((s, Q_LORA), BF16), _sds((s, KV_LORA), BF16), _sds((s, 128), BF16)],
        compiler_params=_params(("parallel",)), name="mla_latents")(t, g_q, g_kv, cf, sf)


def mla_latents_bwd(t, g_q, g_kv, cf, sf, dcq, dckv, dkpe, tm=256):
    s = t.shape[0]
    tm = _tile(s, tm)
    w = Q_LORA + KV_LORA + 128

    def rms_bwd(x, g, dy):
        xh, r = _rms_rows(x, g)
        dxh = dy * g
        return r * (dxh - xh * jnp.mean(dxh * xh, axis=-1, keepdims=True)), jnp.sum(dy * xh, axis=0, keepdims=True)

    def body(t_ref, gq_ref, gkv_ref, cf_ref, sf_ref, dcq_ref, dckv_ref, dkpe_ref, dt_ref, dgq_ref, dgkv_ref):
        dxq, dgq = rms_bwd(t_ref[:, 0:Q_LORA], gq_ref[...], dcq_ref[...])
        dxkv, dgkv = rms_bwd(t_ref[:, Q_LORA:Q_LORA + KV_LORA], gkv_ref[...], dckv_ref[...])
        dt_ref[:, 0:Q_LORA] = dxq.astype(BF16)
        dt_ref[:, Q_LORA:Q_LORA + KV_LORA] = dxkv.astype(BF16)
        dt_ref[:, Q_LORA + KV_LORA:] = _unrope(dkpe_ref[...], cf_ref[...], sf_ref[...]).astype(BF16)
        _accumulate(dgq_ref, dgq)
        _accumulate(dgkv_ref, dgkv)

    return pl.pallas_call(
        body, grid=(s // tm,),
        in_specs=[_rows(tm, w), _fix((1, Q_LORA)), _fix((1, KV_LORA)), _rows(tm, 128), _rows(tm, 128),
                  _rows(tm, Q_LORA), _rows(tm, KV_LORA), _rows(tm, 128)],
        out_specs=[_rows(tm, w), _fix((1, Q_LORA)), _fix((1, KV_LORA))],
        out_shape=[_sds((s, w), BF16), _sds((1, Q_LORA), F32), _sds((1, KV_LORA), F32)],
        compiler_params=_params(("arbitrary",)), name="mla_latents_bwd")(t, g_q, g_kv, cf, sf, dcq, dckv, dkpe)


def mla_queries(cq, w_uq, cf, sf, tm=512):
    s = cq.shape[0]
    tm = _tile(s, tm)

    def epi(acc, e, o):
        o[0][:, 0:QK_NOPE] = acc[:, 0:QK_NOPE].astype(BF16)
        o[0][:, QK_NOPE:] = _rope(acc[:, QK_NOPE:], e[0][...], e[1][...]).astype(BF16)

    return mm_nn("mla_queries", cq, w_uq, tm, HEAD_PAD, Q_LORA, epi, [_sds((s, N_HEADS * HEAD_PAD), BF16)],
                 [_ij(tm, HEAD_PAD)], [cf, sf], [_i0(tm, 128), _i0(tm, 128)])[0]


def mla_keys(ckv, w_uk, kpe, tm=512):
    s = ckv.shape[0]
    tm = _tile(s, tm)

    def epi(acc, e, o):
        o[0][:, 0:QK_NOPE] = acc.astype(BF16)
        o[0][:, QK_NOPE:] = e[0][...]

    return mm_nn("mla_keys", ckv, w_uk, tm, QK_NOPE, KV_LORA, epi, [_sds((s, N_HEADS * HEAD_PAD), BF16)],
                 [_ij(tm, HEAD_PAD)], [kpe], [_i0(tm, 128)])[0]


def _masked_scores(q, k, qi, tq, kv):
    sc = lax.dot_general(q, k, NT, preferred_element_type=F32) * ATTN_SCALE
    row = lax.broadcasted_iota(jnp.int32, (tq, kv), 0) + qi * tq
    col = lax.broadcasted_iota(jnp.int32, (tq, kv), 1)
    ok = lax.shift_right_logical(col, CHUNK_SHIFT) <= lax.shift_right_logical(row, CHUNK_SHIFT)
    return jnp.where(ok, sc, -1e30)


def attention(q, k, v, tq=512):
    s = q.shape[0]
    tq = _tile(s, tq)
    nq = s // tq

    def body(q_ref, k_ref, v_ref, o_ref):
        for qi in range(nq):
            kv = (qi + 1) * tq
            sc = _masked_scores(q_ref[pl.ds(qi * tq, tq), :], k_ref[pl.ds(0, kv), :], qi, tq, kv)
            p = jnp.exp(sc - jnp.max(sc, axis=-1, keepdims=True))
            o = lax.dot_general(p.astype(BF16), v_ref[pl.ds(0, kv), :], NN, preferred_element_type=F32)
            o_ref[pl.ds(qi * tq, tq), :] = (o / jnp.sum(p, axis=-1, keepdims=True)).astype(BF16)

    hq = pl.BlockSpec((s, HEAD_PAD), lambda h: (0, h))
    hv = pl.BlockSpec((s, V_HEAD), lambda h: (0, h))
    return pl.pallas_call(
        body, grid=(N_HEADS,), in_specs=[hq, hq, hv], out_specs=hv, out_shape=_sds((s, N_HEADS * V_HEAD), BF16),
        compiler_params=_params(("parallel",)), name="attention")(q, k, v)


def attention_bwd(q, k, v, do, tq=512):
    s = q.shape[0]
    tq = _tile(s, tq)
    nq = s // tq

    def body(q_ref, k_ref, v_ref, do_ref, dq_ref, dk_ref, dv_ref, dk_acc, dv_acc):
        dk_acc[...] = jnp.zeros_like(dk_acc)
        dv_acc[...] = jnp.zeros_like(dv_acc)
        for qi in range(nq):
            kv = (qi + 1) * tq
            qt = q_ref[pl.ds(qi * tq, tq), :]
            kt = k_ref[pl.ds(0, kv), :]
            dot = do_ref[pl.ds(qi * tq, tq), :]
            sc = _masked_scores(qt, kt, qi, tq, kv)
            p = jnp.exp(sc - jnp.max(sc, axis=-1, keepdims=True))
            p = p / jnp.sum(p, axis=-1, keepdims=True)
            dp = lax.dot_general(dot, v_ref[pl.ds(0, kv), :], NT, preferred_element_type=F32)
            delta = jnp.sum(p * dp, axis=-1, keepdims=True)
            ds = (p * (dp - delta) * ATTN_SCALE).astype(BF16)
            dq_ref[pl.ds(qi * tq, tq), :] = lax.dot_general(ds, kt, NN, preferred_element_type=F32).astype(BF16)
            dk_acc[pl.ds(0, kv), :] += lax.dot_general(ds, qt, TN, preferred_element_type=F32)
            dv_acc[pl.ds(0, kv), :] += lax.dot_general(p.astype(BF16), dot, TN, preferred_element_type=F32)
        dk_ref[...] = dk_acc[...].astype(BF16)
        dv_ref[...] = dv_acc[...].astype(BF16)

    hq = pl.BlockSpec((s, HEAD_PAD), lambda h: (0, h))
    hv = pl.BlockSpec((s, V_HEAD), lambda h: (0, h))
    return pl.pallas_call(
        body, grid=(N_HEADS,), in_specs=[hq, hq, hv, hv], out_specs=[hq, hq, hv],
        out_shape=[_sds((s, N_HEADS * HEAD_PAD), BF16), _sds((s, N_HEADS * HEAD_PAD), BF16),
                   _sds((s, N_HEADS * V_HEAD), BF16)],
        scratch_shapes=[pltpu.VMEM((s, HEAD_PAD), F32), pltpu.VMEM((s, V_HEAD), F32)],
        compiler_params=_params(("parallel",)), name="attention_bwd")(q, k, v, do)


def mla_unrope_grads(dq, dk, cf, sf, tm=256):
    s = dq.shape[0]
    tm = _tile(s, tm)

    def body(dq_ref, dk_ref, cf_ref, sf_ref, dql_ref, dkn_ref, dkpe_ref):
        cfv, sfv = cf_ref[...], sf_ref[...]
        dkpe = jnp.zeros((tm, 128), F32)
        for h in range(N_HEADS):
            lo = h * HEAD_PAD
            dql_ref[:, lo:lo + QK_NOPE] = dq_ref[:, lo:lo + QK_NOPE]
            dql_ref[:, lo + QK_NOPE:lo + HEAD_PAD] = _unrope(
                dq_ref[:, lo + QK_NOPE:lo + HEAD_PAD].astype(F32), cfv, sfv).astype(BF16)
            dkn_ref[:, h * QK_NOPE:(h + 1) * QK_NOPE] = dk_ref[:, lo:lo + QK_NOPE]
            dkpe = dkpe + dk_ref[:, lo + QK_NOPE:lo + HEAD_PAD].astype(F32)
        dkpe_ref[...] = dkpe

    wq = N_HEADS * HEAD_PAD
    return pl.pallas_call(
        body, grid=(s // tm,), in_specs=[_rows(tm, wq), _rows(tm, wq), _rows(tm, 128), _rows(tm, 128)],
        out_specs=[_rows(tm, wq), _rows(tm, N_HEADS * QK_NOPE), _rows(tm, 128)],
        out_shape=[_sds((s, wq), BF16), _sds((s, N_HEADS * QK_NOPE), BF16), _sds((s, 128), F32)],
        compiler_params=_params(("parallel",)), name="mla_unrope_grads")(dq, dk, cf, sf)


ANY = pl.BlockSpec(memory_space=pl.ANY)
GATHER_ID = 1
CHIP_EXCHANGE_ID = 2
PAIR_ID = 3
ALL_ID = 4


def _nbytes(a):
    return a.size * a.dtype.itemsize


def _copy_cost(operand_bytes, sent_fraction):
    sent = int(operand_bytes * sent_fraction)
    return pl.CostEstimate(flops=0, transcendentals=0, bytes_accessed=2 * sent, remote_bytes_transferred=sent)


def _handshake(peers):
    barrier = pltpu.get_barrier_semaphore()
    for peer in peers:
        pl.semaphore_signal(barrier, inc=1, device_id=peer, device_id_type=MESH)
    pl.semaphore_wait(barrier, len(peers))


def _place():
    x, y, c = lax.axis_index("x"), lax.axis_index("y"), lax.axis_index("c")
    chips = [(1 - x, y), (x, 1 - y), (1 - x, 1 - y)]
    return x, y, c, chips


def _half(ref, hc, axis=0):
    n = ref.shape[axis] // 2
    idx = (slice(None),) * axis + (pl.ds(hc * n, n),)
    return ref.at[idx]


def gather_shards(name, tensors, by_columns=()):
    nt = len(tensors)

    def body(*refs):
        a, g = refs[:nt], refs[nt:2 * nt]
        send, recv = refs[2 * nt:]
        x, y, c, chips = _place()
        q = 2 * x + y
        sib = (x, y, 1 - c)
        _handshake([sib] + [(*chip, c) for chip in chips])

        def whole(t, p):
            if t in by_columns:
                n = a[t].shape[1]
                return g[t].at[:, pl.ds(p * n, n)]
            return g[t].at[p]

        def slot(t, chip, hc):
            return _half(whole(t, 2 * chip[0] + chip[1]), hc)

        def rc(t, k, src, dst, to):
            return pltpu.make_async_remote_copy(src_ref=src, dst_ref=dst, send_sem=send.at[t, k], recv_sem=recv.at[t, k],
                                                device_id=to, device_id_type=MESH)

        sent = []
        for t in range(nt):
            cp = rc(t, 6, a[t], whole(t, q), sib)
            cp.start()
            sent.append(cp)
            for j, chip in enumerate(chips):
                cp = rc(t, j, _half(a[t], c), slot(t, (x, y), c), (*chip, c))
                cp.start()
                sent.append(cp)
        for t in range(nt):
            for j, chip in enumerate(chips):
                landed = slot(t, chip, c)
                rc(t, j, landed, landed, (*chip, c)).wait_recv()
                cp = rc(t, 3 + j, landed, landed, sib)
                cp.start()
                sent.append(cp)
        for t in range(nt):
            for j, chip in enumerate(chips):
                other = slot(t, chip, 1 - c)
                rc(t, 3 + j, other, other, sib).wait_recv()
            own = whole(t, q)
            rc(t, 6, own, own, sib).wait_recv()
        for cp in sent:
            cp.wait_send()

    return pl.kernel(
        body, name=name,
        out_type=[_sds((a.shape[0], N_CHIPS * a.shape[1]) if t in by_columns else (N_CHIPS,) + a.shape, a.dtype)
                  for t, a in enumerate(tensors)],
        mesh=plsc.ScalarSubcoreMesh(axis_name="sequencer", num_cores=1),
        scratch_types=[pltpu.SemaphoreType.DMA((nt, 7)), pltpu.SemaphoreType.DMA((nt, 7))],
        cost_estimate=_copy_cost(sum(_nbytes(a) for a in tensors), 4),
        compiler_params=pltpu.CompilerParams(collective_id=GATHER_ID))(*tensors)


def pair_exchange(name, grads, on_sequencer):
    nt = len(grads)

    def body(*refs):
        g, theirs = refs[:nt], refs[nt:2 * nt]
        send, recv = refs[2 * nt:]
        x, y, c, _ = _place()
        if on_sequencer:
            _handshake([(x, y, 1 - c)])
        cps = []
        for t in range(nt):
            cp = pltpu.make_async_remote_copy(src_ref=_half(g[t], 1 - c, 1), dst_ref=theirs[t], send_sem=send.at[t],
                                              recv_sem=recv.at[t], device_id=(x, y, 1 - c), device_id_type=MESH)
            cp.start()
            cps.append(cp)
        for cp in cps:
            cp.wait()

    if not on_sequencer:
        return pl.pallas_call(
            body, in_specs=[ANY] * nt, out_specs=[ANY] * nt,
            out_shape=[_sds((N_CHIPS, a.shape[1] // 2, a.shape[2]), a.dtype) for a in grads],
            scratch_shapes=[pltpu.SemaphoreType.DMA((nt,)), pltpu.SemaphoreType.DMA((nt,))],
            name=name)(*grads)
    return pl.kernel(
        body, name=name, out_type=[_sds((N_CHIPS, a.shape[1] // 2, a.shape[2]), a.dtype) for a in grads],
        mesh=plsc.ScalarSubcoreMesh(axis_name="sequencer", num_cores=1),
        scratch_types=[pltpu.SemaphoreType.DMA((nt,)), pltpu.SemaphoreType.DMA((nt,))],
        cost_estimate=_copy_cost(sum(_nbytes(a) for a in grads), 0.5),
        compiler_params=pltpu.CompilerParams(collective_id=PAIR_ID))(*grads)


def chip_exchange(name, parts):
    nt = len(parts)

    def body(*refs):
        a, r = refs[:nt], refs[nt:2 * nt]
        send, recv = refs[2 * nt:]
        x, y, c, chips = _place()
        _handshake([(*chip, c) for chip in chips])
        cps = []
        for t in range(nt):
            for j, chip in enumerate(chips):
                cp = pltpu.make_async_remote_copy(
                    src_ref=a[t].at[2 * chip[0] + chip[1]], dst_ref=r[t].at[j], send_sem=send.at[t, j],
                    recv_sem=recv.at[t, j], device_id=(*chip, c), device_id_type=MESH)
                cp.start()
                cps.append(cp)
        for cp in cps:
            cp.wait()

    return pl.kernel(
        body, name=name, out_type=[_sds((N_CHIPS - 1,) + a.shape[1:], a.dtype) for a in parts],
        mesh=plsc.ScalarSubcoreMesh(axis_name="sequencer", num_cores=1),
        scratch_types=[pltpu.SemaphoreType.DMA((nt, 3)), pltpu.SemaphoreType.DMA((nt, 3))],
        cost_estimate=_copy_cost(sum(_nbytes(a) for a in parts), 0.75),
        compiler_params=pltpu.CompilerParams(collective_id=CHIP_EXCHANGE_ID))(*parts)


def pair_share(name, halves):
    nt = len(halves)

    def body(*refs):
        h, other = refs[:nt], refs[nt:2 * nt]
        send, recv = refs[2 * nt:]
        x, y, c, _ = _place()
        _handshake([(x, y, 1 - c)])
        cps = []
        for t in range(nt):
            cp = pltpu.make_async_remote_copy(src_ref=h[t], dst_ref=other[t], send_sem=send.at[t], recv_sem=recv.at[t],
                                              device_id=(x, y, 1 - c), device_id_type=MESH)
            cp.start()
            cps.append(cp)
        for cp in cps:
            cp.wait()

    return pl.kernel(
        body, name=name, out_type=[_sds(a.shape, a.dtype) for a in halves],
        mesh=plsc.ScalarSubcoreMesh(axis_name="sequencer", num_cores=1),
        scratch_types=[pltpu.SemaphoreType.DMA((nt,)), pltpu.SemaphoreType.DMA((nt,))],
        cost_estimate=_copy_cost(sum(_nbytes(a) for a in halves), 1),
        compiler_params=pltpu.CompilerParams(collective_id=PAIR_ID))(*halves)


def all_reduce_small(parts, rows):
    cdim = parts[0].shape[1]
    n = len(parts)
    vm = pl.BlockSpec(memory_space=pltpu.VMEM)

    def pack(*refs):
        p, o_ref = refs[:n], refs[n]
        at = 0
        for ref in p:
            o_ref[pl.ds(at, ref.shape[0]), :] = ref[...]
            at += ref.shape[0]
        o_ref[pl.ds(at, rows - at), :] = jnp.zeros((rows - at, cdim), F32)

    mine = pl.pallas_call(pack, in_specs=[vm] * n, out_specs=vm, out_shape=_sds((rows, cdim), F32), name="small_pack")(*parts)

    def exchange(mine_ref, buf, send, recv, lsem):
        x, y, c, _ = _place()
        me = 4 * x + 2 * y + c
        peers = [(x ^ (k >> 2), y ^ ((k >> 1) & 1), c ^ (k & 1)) for k in range(1, 8)]
        _handshake(peers)
        own = pltpu.make_async_copy(mine_ref, buf.at[me], lsem)
        own.start()
        cps = []
        for k, to in enumerate(peers):
            cp = pltpu.make_async_remote_copy(src_ref=mine_ref, dst_ref=buf.at[me], send_sem=send.at[k], recv_sem=recv.at[k],
                                              device_id=to, device_id_type=MESH)
            cp.start()
            cps.append(cp)
        for k, (px, py, pc) in enumerate(peers):
            pltpu.make_async_remote_copy(src_ref=mine_ref, dst_ref=buf.at[4 * px + 2 * py + pc], send_sem=send.at[k],
                                         recv_sem=recv.at[k], device_id=(x, y, c), device_id_type=MESH).wait_recv()
        for cp in cps:
            cp.wait_send()
        own.wait()

    landed = pl.kernel(
        exchange, name="small_exchange", out_type=_sds((8, rows, cdim), F32),
        mesh=plsc.ScalarSubcoreMesh(axis_name="sequencer", num_cores=1),
        scratch_types=[pltpu.SemaphoreType.DMA((7,)), pltpu.SemaphoreType.DMA((7,)), pltpu.SemaphoreType.DMA],
        cost_estimate=_copy_cost(rows * cdim * 4, 7),
        compiler_params=pltpu.CompilerParams(collective_id=ALL_ID))(mine)

    def total(buf, o_ref):
        acc = buf[0]
        for d in range(1, 8):
            acc = acc + buf[d]
        o_ref[...] = acc

    return pl.pallas_call(total, in_specs=[vm], out_specs=vm, out_shape=_sds((rows, cdim), F32), name="small_sum")(landed)


def pair_sum(g, theirs, core, tm=256):
    _, r, c = g.shape
    tm = _tile(r // 2, tm)
    nh = r // 2 // tm

    def body(core_ref, a_ref, b_ref, o_ref):
        o_ref[...] = (a_ref[...].astype(F32) + b_ref[...].astype(F32)).astype(BF16)

    blk = (N_CHIPS, tm, c)
    return pl.pallas_call(
        body, grid_spec=pltpu.PrefetchScalarGridSpec(
            num_scalar_prefetch=1, grid=(nh,),
            in_specs=[pl.BlockSpec(blk, lambda i, cr: (0, cr[0] * nh + i, 0)), pl.BlockSpec(blk, lambda i, cr: (0, i, 0))],
            out_specs=pl.BlockSpec(blk, lambda i, cr: (0, i, 0))),
        out_shape=_sds(theirs.shape, BF16), compiler_params=_params(("parallel",)), name="pair_sum")(core, g, theirs)


def chip_sum(own, landed, chip, stack, layer, layers, tm=256):
    _, r, c = own.shape
    tm = _tile(r, tm)

    def body(chip_ref, own_ref, l_ref, *rest):
        acc = own_ref[...].astype(F32)
        for j in range(N_CHIPS - 1):
            acc = acc + l_ref[j].astype(F32)
        rest[-1][...] = acc

    in_specs = [pl.BlockSpec((None, tm, c), lambda i, qr: (qr[0], i, 0)),
                pl.BlockSpec((N_CHIPS - 1, tm, c), lambda i, qr: (0, i, 0))]
    args = [chip, own, landed]
    if stack is not None:
        in_specs.append(ANY)
        args.append(stack)
    return pl.pallas_call(
        body, grid_spec=pltpu.PrefetchScalarGridSpec(
            num_scalar_prefetch=1, grid=(r // tm,), in_specs=in_specs,
            out_specs=pl.BlockSpec((None, tm, c), lambda i, qr: (layer, i, 0))),
        out_shape=_sds((layers, r, c), F32), input_output_aliases={3: 0} if stack is not None else {},
        compiler_params=_params(("parallel",)), name="chip_sum")(*args)


def adamw_joined(w, m, v, g_mine, g_theirs, core, tm=512):
    nl, r, c = w.shape
    tm = _tile(r // 2, tm)
    nh = r // 2 // tm
    bc1 = 1.0 - ADAM_B1 ** ADAM_STEP
    bc2 = 1.0 - ADAM_B2 ** ADAM_STEP

    def body(core_ref, w_ref, m_ref, v_ref, gm_ref, gt_ref, g_ref, d_ref, nm_ref, nv_ref):
        mine = (pl.program_id(1) // nh) == core_ref[0]
        gv = jnp.where(mine, gm_ref[...], gt_ref[...])
        nm = ADAM_B1 * m_ref[...] + (1.0 - ADAM_B1) * gv
        nv = ADAM_B2 * v_ref[...] + (1.0 - ADAM_B2) * (gv * gv)
        g_ref[...] = gv
        d_ref[...] = -ADAM_LR * ((nm / bc1) / (jnp.sqrt(nv / bc2) + ADAM_EPS) + ADAM_WD * w_ref[...])
        nm_ref[...] = nm
        nv_ref[...] = nv

    full = pl.BlockSpec((None, tm, c), lambda l, i, cr: (l, i, 0))
    half = pl.BlockSpec((None, tm, c), lambda l, i, cr: (l, i % nh, 0))
    return pl.pallas_call(
        body, grid_spec=pltpu.PrefetchScalarGridSpec(
            num_scalar_prefetch=1, grid=(nl, r // tm), in_specs=[full, full, full, half, half], out_specs=[full] * 4),
        out_shape=[_sds((nl, r, c), F32)] * 4, compiler_params=_params(("parallel", "parallel")),
        name="adamw_joined")(core, w, m, v, g_mine, g_theirs)


def adamw(w, g, m, v, tm=256):
    shape = w.shape
    c = shape[-1]
    r = w.size // c
    tm = _tile(r, tm)
    bc1 = 1.0 - ADAM_B1 ** ADAM_STEP
    bc2 = 1.0 - ADAM_B2 ** ADAM_STEP

    def body(w_ref, g_ref, m_ref, v_ref, d_ref, nm_ref, nv_ref):
        gv = g_ref[...]
        nm = ADAM_B1 * m_ref[...] + (1.0 - ADAM_B1) * gv
        nv = ADAM_B2 * v_ref[...] + (1.0 - ADAM_B2) * (gv * gv)
        d_ref[...] = -ADAM_LR * ((nm / bc1) / (jnp.sqrt(nv / bc2) + ADAM_EPS) + ADAM_WD * w_ref[...])
        nm_ref[...] = nm
        nv_ref[...] = nv

    outs = pl.pallas_call(
        body, grid=(r // tm,), in_specs=[_rows(tm, c)] * 4, out_specs=[_rows(tm, c)] * 3,
        out_shape=[_sds((r, c), F32)] * 3, compiler_params=_params(("parallel",)), name="adamw")(
            w.reshape(r, c), g.reshape(r, c), m.reshape(r, c), v.reshape(r, c))
    return [o.reshape(shape) for o in outs]


WEIGHTS = ['sc_w_in', 'sc_conv_w', 'sc_w_out', 'mla_w_dq', 'mla_g_q', 'mla_w_uq', 'mla_w_dkv', 'mla_g_kv', 'mla_w_uk',
           'mla_w_uv', 'mla_w_o', 'cf_w_pw1', 'cf_b_pw1', 'cf_dw_w', 'cf_dw_b', 'cf_norm_g', 'cf_norm_b', 'cf_w_pw2',
           'cf_b_pw2', 'ff_w1', 'ff_w2', 'ln_mix_g', 'ln_mix_b', 'ln_ff_g', 'ln_ff_b']
ARGS = ['x'] + WEIGHTS + ['loss_target'] + ['m_' + n for n in WEIGHTS] + ['v_' + n for n in WEIGHTS]


def _sq_relu(h):
    r = jnp.maximum(h.astype(F32), 0.0)
    return (r * r).astype(BF16)


def _mlp_forward(i, x, xb, w1, w2, g, b):
    hb = mm_plain_nn(f"mlp{i}_up", xb, w1, BF16, tn=1024)
    y, yb, xh, rstd = mm_residual_ln(f"mlp{i}_down_ln", hb, w2, x, g, b, tk=2048, a_fn=_sq_relu)
    return (y, yb), dict(xb=xb, hb=hb, xh=xh, rstd=rstd, g=g)


def _mlp_backward(i, dy, sv, w1, w2, dw1, dw2, reduce_after):
    s = dy.shape[0]
    dr, drb, dg, db, _ = ln_backward(f"mlp{i}_ln_bwd", dy, sv["xh"], sv["rstd"], sv["g"])
    tm, tn = _tile(s, 1024), 1024

    def epi(acc, e, o):
        o[0][...] = (acc * (2.0 * jnp.maximum(e[0][...].astype(F32), 0.0))).astype(BF16)

    dhb = mm_nt(f"mlp{i}_down_bwd", drb, w2, s, tm, tn, 1024, epi, [_sds((s, w2.k), BF16)], [_ij(tm, tn)],
                [sv["hb"]], [_ij(tm, tn)])[0]
    g_w2 = mm_tn(f"mlp{i}_dw2", sv["hb"], drb, dw2, s, 512, 1024, a_fn=_sq_relu)
    g_w1 = mm_tn(f"mlp{i}_dw1", sv["xb"], dhb, dw1, s, 1024, 512)
    dhb = reduce_after(dhb, {f"w1_{i}": g_w1, f"w2_{i}": g_w2})
    dx = mm_plain_nt(f"mlp{i}_up_bwd", dhb, w1, F32, tn=1024, add=dr, add_scale=ALPHA)
    return dx, dg, db


def kernel(x, sc_w_in, sc_conv_w, sc_w_out, mla_w_dq, mla_g_q, mla_w_uq, mla_w_dkv, mla_g_kv, mla_w_uk, mla_w_uv, mla_w_o, cf_w_pw1, cf_b_pw1, cf_dw_w, cf_dw_b, cf_norm_g, cf_norm_b, cf_w_pw2, cf_b_pw2, ff_w1, ff_w2, ln_mix_g, ln_mix_b, ln_ff_g, ln_ff_b, loss_target, m_sc_w_in, m_sc_conv_w, m_sc_w_out, m_mla_w_dq, m_mla_g_q, m_mla_w_uq, m_mla_w_dkv, m_mla_g_kv, m_mla_w_uk, m_mla_w_uv, m_mla_w_o, m_cf_w_pw1, m_cf_b_pw1, m_cf_dw_w, m_cf_dw_b, m_cf_norm_g, m_cf_norm_b, m_cf_w_pw2, m_cf_b_pw2, m_ff_w1, m_ff_w2, m_ln_mix_g, m_ln_mix_b, m_ln_ff_g, m_ln_ff_b, v_sc_w_in, v_sc_conv_w, v_sc_w_out, v_mla_w_dq, v_mla_g_q, v_mla_w_uq, v_mla_w_dkv, v_mla_g_kv, v_mla_w_uk, v_mla_w_uv, v_mla_w_o, v_cf_w_pw1, v_cf_b_pw1, v_cf_dw_w, v_cf_dw_b, v_cf_norm_g, v_cf_norm_b, v_cf_w_pw2, v_cf_b_pw2, v_ff_w1, v_ff_w2, v_ln_mix_g, v_ln_mix_b, v_ln_ff_g, v_ln_ff_b):
    given = dict(zip(ARGS, (x, sc_w_in, sc_conv_w, sc_w_out, mla_w_dq, mla_g_q, mla_w_uq, mla_w_dkv, mla_g_kv, mla_w_uk, mla_w_uv, mla_w_o, cf_w_pw1, cf_b_pw1, cf_dw_w, cf_dw_b, cf_norm_g, cf_norm_b, cf_w_pw2, cf_b_pw2, ff_w1, ff_w2, ln_mix_g, ln_mix_b, ln_ff_g, ln_ff_b, loss_target, m_sc_w_in, m_sc_conv_w, m_sc_w_out, m_mla_w_dq, m_mla_g_q, m_mla_w_uq, m_mla_w_dkv, m_mla_g_kv, m_mla_w_uk, m_mla_w_uv, m_mla_w_o, m_cf_w_pw1, m_cf_b_pw1, m_cf_dw_w, m_cf_dw_b, m_cf_norm_g, m_cf_norm_b, m_cf_w_pw2, m_cf_b_pw2, m_ff_w1, m_ff_w2, m_ln_mix_g, m_ln_mix_b, m_ln_ff_g, m_ln_ff_b, v_sc_w_in, v_sc_conv_w, v_sc_w_out, v_mla_w_dq, v_mla_g_q, v_mla_w_uq, v_mla_w_dkv, v_mla_g_kv, v_mla_w_uk, v_mla_w_uv, v_mla_w_o, v_cf_w_pw1, v_cf_b_pw1, v_cf_dw_w, v_cf_dw_b, v_cf_norm_g, v_cf_norm_b, v_cf_w_pw2, v_cf_b_pw2, v_ff_w1, v_ff_w2, v_ln_mix_g, v_ln_mix_b, v_ln_ff_g, v_ln_ff_b)))
    s, d = x.shape[1], x.shape[2]
    d_ff = 4 * d
    dq4 = d // N_CHIPS
    xq = lax.axis_index("x") * 2 + lax.axis_index("y")

    w_dkv_pad = jnp.pad(mla_w_dkv[0], ((0, 0), (0, 128 - QK_ROPE)))
    w_uq_pad = jnp.pad(mla_w_uq[0].reshape(Q_LORA, 2, QK_NOPE + QK_ROPE), ((0, 0), (0, 0), (0, HEAD_PAD - QK_NOPE - QK_ROPE)))
    small = jnp.concatenate([
        sc_conv_w.reshape(2 * SC_WIDTH, dq4), cf_b_pw1.reshape(2, dq4), cf_dw_w[0], cf_dw_b, cf_norm_g, cf_norm_b,
        cf_b_pw2, jnp.zeros((5, dq4), F32)], axis=0)
    mlp_w = lambda i: [ff_w1[i].astype(BF16), ff_w2[i].astype(BF16)]
    g_in, g_out, g_w1, g_w2 = [None] * 2, [None] * 2, [None] * DEPTH, [None] * DEPTH
    g_in[0], g_out[0], g_small = gather_shards(
        "gather_mixer0", [sc_w_in[0].astype(BF16), sc_w_out[0].astype(BF16), small], by_columns=(0,))
    (g_w1[0],) = gather_shards("gather_up0", [ff_w1[0].astype(BF16)], by_columns=(0,))
    (g_w2[0],) = gather_shards("gather_down0", [ff_w2[0].astype(BF16)])
    g_dqkv, g_uq, g_uk, g_uv, g_o = gather_shards("gather_mixer1", [
        jnp.concatenate([mla_w_dq[0], w_dkv_pad], axis=1).astype(BF16),
        w_uq_pad.reshape(Q_LORA, 2 * HEAD_PAD).astype(BF16),
        mla_w_uk.reshape(KV_LORA // N_CHIPS, N_HEADS * QK_NOPE).astype(BF16),
        mla_w_uv.reshape(KV_LORA // N_CHIPS, N_HEADS * V_HEAD).astype(BF16), mla_w_o[0].astype(BF16)], by_columns=(1,))
    g_w1[1], g_w2[1] = gather_shards("gather_mlp1", mlp_w(1), by_columns=(0,))
    g_pw1, g_pw2, g_w1[2], g_w2[2] = gather_shards(
        "gather_layer2", [cf_w_pw1[0].astype(BF16), cf_w_pw2[0].astype(BF16)] + mlp_w(2), by_columns=(0, 2))
    g_in[1], g_out[1], g_w1[3], g_w2[3] = gather_shards(
        "gather_layer3", [sc_w_in[1].astype(BF16), sc_w_out[1].astype(BF16)] + mlp_w(3), by_columns=(0, 2))

    wd_t = Q_LORA + KV_LORA + 128
    w_in = [Stk("full", d, 3 * d, g_in[j]) for j in range(2)]
    w_out = [Stk("row", d, d, g_out[j]) for j in range(2)]
    w_dqkv = Stk("row", d, wd_t, g_dqkv)
    w_uq = Stk("full", Q_LORA, N_HEADS * HEAD_PAD, g_uq)
    w_uk = Stk("row", KV_LORA, N_HEADS * QK_NOPE, g_uk)
    w_uv = Stk("row", KV_LORA, N_HEADS * V_HEAD, g_uv)
    w_o = Stk("row", d, d, g_o)
    w_pw1 = Stk("full", d, 2 * d, g_pw1)
    w_pw2 = Stk("row", d, d, g_pw2)
    w_1 = [Stk("full", d, d_ff, g_w1[i]) for i in range(DEPTH)]
    w_2 = [Stk("row", d_ff, d, g_w2[i]) for i in range(DEPTH)]

    def wide(rows):
        return jnp.swapaxes(rows, 0, 1).reshape(rows.shape[1], d)

    conv_w = wide(g_small[:, 0:6]).reshape(2, SC_WIDTH, d)
    b_pw1 = g_small[:, 6:8].reshape(1, 2 * d)
    dw_w = wide(g_small[:, 8:39])
    dw_b, norm_g, norm_b, b_pw2 = (wide(g_small[:, 39 + k:40 + k]) for k in range(4))

    pos = jnp.arange(s, dtype=F32)
    inv_freq = ROPE_THETA ** (-jnp.arange(0, QK_ROPE, 2, dtype=F32) / QK_ROPE)
    ang = pos[:, None] * inv_freq[None, :]
    cos, sin, zero = jnp.cos(ang), jnp.sin(ang), jnp.zeros((s, 128 - QK_ROPE), F32)
    cf = jnp.concatenate([cos, cos, zero], axis=1)
    sf = jnp.concatenate([-sin, sin, zero], axis=1)

    def row(a, i):
        return a[i:i + 1]

    xs = x.reshape(s, d)
    cur = (xs, xs.astype(BF16))
    tape = []
    for i in range(DEPTH):
        mixer, j = i % 3, i // 3
        xf, xb = cur
        lg, lb = row(ln_mix_g, i), row(ln_mix_b, i)
        if mixer == 0:
            u = mm_plain_nn(f"sc{j}_in", xb, w_in[j], F32, tn=3 * dq4)
            gb = short_conv_gate(u, conv_w[j])
            y, yb, xh, rstd = mm_residual_ln(f"sc{j}_out_ln", gb, w_out[j], xf, lg, lb)
            sv = dict(xb=xb, u=u, gb=gb)
        elif mixer == 1:
            t = mm_plain_nn("mla_down", xb, w_dqkv, F32, tn=wd_t // 2)
            cq, ckv, kpe = mla_latents(t, mla_g_q, mla_g_kv, cf, sf)
            qh = mla_queries(cq, w_uq, cf, sf)
            kh = mla_keys(ckv, w_uk, kpe)
            vh = mm_plain_nn("mla_values", ckv, w_uv, BF16, tk=KV_LORA)
            oh = attention(qh, kh, vh)
            y, yb, xh, rstd = mm_residual_ln("mla_out_ln", oh, w_o, xf, lg, lb)
            sv = dict(xb=xb, t=t, cq=cq, ckv=ckv, qh=qh, kh=kh, vh=vh, oh=oh)
        else:
            u = mm_plain_nn("cf_pw1", xb, w_pw1, F32, bias=b_pw1)
            hc = conformer_glu_conv(u, dw_w, dw_b)
            sb = conformer_norm_swish(hc, norm_g, norm_b)
            y, yb, xh, rstd = mm_residual_ln("cf_pw2_ln", sb, w_pw2, xf, lg, lb, bias=b_pw2)
            sv = dict(xb=xb, u=u, hc=hc, sb=sb)
        sv.update(xh=xh, rstd=rstd, g=lg)
        cur, sv_mlp = _mlp_forward(i, y, yb, w_1[i], w_2[i], row(ln_ff_g, i), row(ln_ff_b, i))
        tape.append((sv, sv_mlp))

    dy, loss_part = loss_head(cur[0], loss_target.reshape(s, d))

    grads = {}
    smalls = {}
    g_ln = {n: [None] * DEPTH for n in ("ln_mix_g", "ln_mix_b", "ln_ff_g", "ln_ff_b")}
    conv_grads = [None, None]
    core = lax.axis_index("c").astype(jnp.int32).reshape(1)
    chip = xq.astype(jnp.int32).reshape(1)
    pairs, landed = {}, {}
    ready, theirs = [], {}

    def reduce_after(x, new, early=False):
        out = lax.optimization_barrier((x, *new.values()))
        grads.update(zip(new, out[1:]))
        if early:
            theirs.update(zip(new, pair_exchange(f"pair_exchange_{len(theirs)}", list(out[1:]), True)))
        ready.extend(new)
        return out[0]

    def reduce_layer(i, x):
        late = [n for n in ready if n not in theirs]
        if late:
            theirs.update(zip(late, pair_exchange(f"pair_exchange_layer{i}", [grads[n] for n in late], False)))
        sums = [pair_sum(grads[n], theirs[n], core) for n in ready]
        pairs.update(zip(ready, sums))
        landed.update(zip(ready, chip_exchange(f"chip_exchange_layer{i}", sums)))
        exchanged.append(list(ready))
        ready.clear()
        return lax.optimization_barrier((x, *sums))[0]

    groups = [["in_0", "in_1"], ["out_0", "out_1"], ["dqkv"], ["uq"], ["uk"], ["uv"], ["o"], ["pw1"], ["pw2"],
              [f"w1_{i}" for i in range(DEPTH)], [f"w2_{i}" for i in range(DEPTH)]]
    stacks = [None] * len(groups)
    exchanged = []

    def sum_layer(x, last=False):
        names = exchanged.pop(0)
        if last:
            out = lax.optimization_barrier((x, *[landed[n] for n in names]))
            landed.update(zip(names, out[1:]))
        new = []
        for n in names:
            k = next(k for k, members in enumerate(groups) if n in members)
            stacks[k] = chip_sum(pairs[n], landed[n], chip, stacks[k], groups[k].index(n), len(groups[k]))
            new.append(stacks[k])
        return out[0] if last else lax.optimization_barrier((x, *new))[0]

    for i in reversed(range(DEPTH)):
        mixer, j = i % 3, i // 3
        sv, sv_mlp = tape[i]
        dy, g_ln["ln_ff_g"][i], g_ln["ln_ff_b"][i] = _mlp_backward(
            i, dy, sv_mlp, w_1[i], w_2[i], Stk("col", d, d_ff), Stk("row", d_ff, d),
            lambda x_, new: reduce_after(x_, new, early=i > 0))
        if i == 0:
            dy = reduce_layer("0_mlp", dy)
        dr, drb, g_ln["ln_mix_g"][i], g_ln["ln_mix_b"][i], dr_sum = ln_backward(
            f"mix{i}_ln_bwd", dy, sv["xh"], sv["rstd"], sv["g"])
        if mixer == 0:
            dgate = mm_plain_nt(f"sc{j}_out_bwd", drb, w_out[j], F32)
            dw_out = mm_tn(f"sc{j}_dw_out", sv["gb"], drb, Stk("row", d, d), s, 512, 1024)
            du, conv_grads[j] = short_conv_gate_bwd(sv["u"], conv_w[j], dgate)
            nb = d // 256
            dw_in = mm_tn(
                f"sc{j}_dw_in", sv["xb"], du, Stk("col", d, 3 * d), s, 1024, 256,
                b_spec=pl.BlockSpec((None, s, 256), lambda i_, j_, k_: (j_ // nb, k_, j_ % nb)))
            du = reduce_after(du, {f"in_{j}": dw_in, f"out_{j}": dw_out})
            dy = mm_plain_nt(
                f"sc{j}_in_bwd", du, w_in[j], F32, tn=1024, tk=d, add=dr, add_scale=ALPHA,
                a_spec_fn=(s, lambda tm, tk: pl.BlockSpec((None, tm, tk), lambda i_, j_, k_: (k_, i_, 0))))
        elif mixer == 1:
            do = mm_plain_nt("mla_out_bwd", drb, w_o, BF16)
            g_o = mm_tn("mla_dw_o", sv["oh"], drb, Stk("row", d, d), s, 512, 1024)
            dqh, dkh, dvh = attention_bwd(sv["qh"], sv["kh"], sv["vh"], do)
            dql, dkn, dkpe = mla_unrope_grads(dqh, dkh, cf, sf)
            g_uq = mm_tn("mla_dw_uq", sv["cq"], dql, Stk("col", Q_LORA, N_HEADS * HEAD_PAD), s, Q_LORA, 512)
            dcq = mm_plain_nt("mla_uq_bwd", dql, w_uq, F32, tn=Q_LORA)
            g_uk = mm_tn("mla_dw_uk", sv["ckv"], dkn, Stk("row", KV_LORA, N_HEADS * QK_NOPE), s, KV_LORA, 1024)
            g_uv = mm_tn("mla_dw_uv", sv["ckv"], dvh, Stk("row", KV_LORA, N_HEADS * V_HEAD), s, KV_LORA, 1024)
            dckv = mm_plain_nt("mla_uk_bwd", dkn, w_uk, F32, tn=KV_LORA)
            dckv = mm_plain_nt("mla_uv_bwd", dvh, w_uv, F32, tn=KV_LORA, add=dckv)
            dt, smalls["g_q"], smalls["g_kv"] = mla_latents_bwd(sv["t"], mla_g_q, mla_g_kv, cf, sf, dcq, dckv, dkpe)
            g_dqkv = mm_tn("mla_dw_down", sv["xb"], dt, Stk("row", d, wd_t), s, 512, wd_t)
            dt = reduce_after(dt, {"dqkv": g_dqkv, "uq": g_uq, "uk": g_uk, "uv": g_uv, "o": g_o})
            dy = mm_plain_nt("mla_down_bwd", dt, w_dqkv, F32, tk=wd_t, add=dr, add_scale=ALPHA)
        else:
            dsw = mm_plain_nt("cf_pw2_bwd", drb, w_pw2, F32)
            g_pw2 = mm_tn("cf_dw_pw2", sv["sb"], drb, Stk("row", d, d), s, 512, 1024)
            smalls["b_pw2"] = dr_sum
            dhc, smalls["norm_g"], smalls["norm_b"] = conformer_norm_swish_bwd(sv["hc"], norm_g, norm_b, dsw)
            du, smalls["b_pw1"], smalls["dw_w"], smalls["dw_b"] = conformer_glu_conv_bwd(sv["u"], dw_w, dhc)
            nb = d // 512
            g_pw1 = mm_tn(
                "cf_dw_pw1", sv["xb"], du, Stk("col", d, 2 * d), s, 1024, 512,
                b_spec=pl.BlockSpec((None, s, 512), lambda i_, j_, k_: (j_ // nb, k_, j_ % nb)))
            du = reduce_after(du, {"pw1": g_pw1, "pw2": g_pw2})
            dy = mm_plain_nt(
                "cf_pw1_bwd", du, w_pw1, F32, tn=1024, tk=d, add=dr, add_scale=ALPHA,
                a_spec_fn=(s, lambda tm, tk: pl.BlockSpec((None, tm, tk), lambda i_, j_, k_: (k_, i_, 0))))
        if i < DEPTH - 1:
            dy = sum_layer(dy)
        dy = reduce_layer(i, dy)
    dy = sum_layer(sum_layer(dy, last=True), last=True)
    grad_x = dy.reshape(1, s, d)

    mine = stacks
    other = (pair_share("pair_share_mixers", mine[:9]) + pair_share("pair_share_up", mine[9:10])
             + pair_share("pair_share_down", mine[10:]))

    def padded(get):
        dqkv = jnp.concatenate([get("mla_w_dq")[0], jnp.pad(get("mla_w_dkv")[0], ((0, 0), (0, 128 - QK_ROPE)))], axis=1)
        uq = jnp.pad(get("mla_w_uq")[0].reshape(Q_LORA, 2, QK_NOPE + QK_ROPE),
                     ((0, 0), (0, 0), (0, HEAD_PAD - QK_NOPE - QK_ROPE))).reshape(Q_LORA, 2 * HEAD_PAD)
        return [get("sc_w_in"), get("sc_w_out"), dqkv[None], uq[None],
                get("mla_w_uk").reshape(1, KV_LORA // N_CHIPS, d), get("mla_w_uv").reshape(1, KV_LORA // N_CHIPS, d),
                get("mla_w_o"), get("cf_w_pw1"), get("cf_w_pw2"), get("ff_w1"), get("ff_w2")]

    w_l, m_l, v_l = (padded(lambda n, p=p: given[p + n]) for p in ("", "m_", "v_"))
    res = [adamw_joined(w_l[k], m_l[k], v_l[k], mine[k], other[k], core) for k in range(len(groups))]

    def unpadded(k):
        r_in, r_out, r_dqkv, r_uq, r_uk, r_uv, r_o, r_pw1, r_pw2, r_w1, r_w2 = (r[k] for r in res)
        return {
            "sc_w_in": r_in, "sc_w_out": r_out, "mla_w_dq": r_dqkv[:, :, 0:Q_LORA],
            "mla_w_dkv": r_dqkv[:, :, Q_LORA:Q_LORA + KV_LORA + QK_ROPE],
            "mla_w_uq": r_uq.reshape(1, Q_LORA, 2, HEAD_PAD)[:, :, :, 0:QK_NOPE + QK_ROPE].reshape(mla_w_uq.shape),
            "mla_w_uk": r_uk.reshape(mla_w_uk.shape), "mla_w_uv": r_uv.reshape(mla_w_uv.shape),
            "mla_w_o": r_o, "cf_w_pw1": r_pw1, "cf_w_pw2": r_pw2, "ff_w1": r_w1, "ff_w2": r_w2}

    big_g, big_d, big_m, big_v = (unpadded(k) for k in range(4))

    pad_row = lambda a: jnp.pad(a, ((0, 0), (0, d - a.shape[1])))
    small_parts = ([g for n in ("ln_mix_g", "ln_mix_b", "ln_ff_g", "ln_ff_b") for g in g_ln[n]]
                   + [pad_row(smalls["g_q"]), pad_row(smalls["g_kv"]), conv_grads[0], conv_grads[1],
                      smalls["b_pw1"].reshape(2, d), smalls["dw_w"], smalls["dw_b"], smalls["norm_g"], smalls["norm_b"],
                      smalls["b_pw2"], loss_part])
    red = all_reduce_small(small_parts, 64)
    loss = red[61, 0]

    def shard(rows):
        return lax.dynamic_slice_in_dim(rows, xq * dq4, dq4, axis=1)

    gw = {
        **big_g,
        "ln_mix_g": red[0:4], "ln_mix_b": red[4:8], "ln_ff_g": red[8:12], "ln_ff_b": red[12:16],
        "mla_g_q": red[16:17, 0:Q_LORA], "mla_g_kv": red[17:18, 0:KV_LORA],
        "sc_conv_w": shard(red[18:24]).reshape(2, SC_WIDTH, dq4),
        "cf_b_pw1": lax.dynamic_slice_in_dim(red[24:26].reshape(1, 2 * d), xq * 2 * dq4, 2 * dq4, axis=1),
        "cf_dw_w": shard(red[26:57])[None], "cf_dw_b": shard(red[57:58]), "cf_norm_g": shard(red[58:59]),
        "cf_norm_b": shard(red[59:60]), "cf_b_pw2": shard(red[60:61]),
    }

    upd = {n: [big_d[n], big_m[n], big_v[n]] for n in big_g}

    def pack(names, width, get):
        return jnp.concatenate([get(n).reshape(-1, width) for n in names], axis=0)

    def unpack(names, packed):
        out, at = {}, 0
        for n in names:
            rows = given[n].size // packed.shape[1]
            out[n] = packed[at:at + rows].reshape(given[n].shape)
            at += rows
        return out

    rep = ["ln_mix_g", "ln_mix_b", "ln_ff_g", "ln_ff_b"]
    shd = ["sc_conv_w", "cf_b_pw1", "cf_dw_w", "cf_dw_b", "cf_norm_g", "cf_norm_b", "cf_b_pw2"]
    for names, width in ((rep, d), (shd, dq4), (["mla_g_q"], Q_LORA), (["mla_g_kv"], KV_LORA)):
        res = adamw(pack(names, width, lambda n: given[n]), pack(names, width, lambda n: gw[n]),
                    pack(names, width, lambda n: given["m_" + n]), pack(names, width, lambda n: given["v_" + n]), tm=4096)
        parts = [unpack(names, r) for r in res]
        for n in names:
            upd[n] = [p[n] for p in parts]

    return (loss, grad_x, *[gw[n].reshape(given[n].shape) for n in WEIGHTS], *[upd[n][0] for n in WEIGHTS],
            *[upd[n][1] for n in WEIGHTS], *[upd[n][2] for n in WEIGHTS])
```

```python
import jax
import jax.numpy as jnp
from jax import lax
from jax.experimental import pallas as pl
from jax.experimental.pallas import tpu as pltpu
from jax.experimental.pallas import tpu_sc as plsc

F32 = jnp.float32
BF16 = jnp.bfloat16
MESH = pl.DeviceIdType.MESH

DEPTH = 4
ALPHA = (2.0 * DEPTH) ** 0.25
LN_EPS = 1e-5
RMS_EPS = 1e-6
CHUNK_SHIFT = 6
N_HEADS = 8
QK_NOPE = 128
QK_ROPE = 64
V_HEAD = 128
HEAD_PAD = 256
Q_LORA = 384
KV_LORA = 256
ROPE_THETA = 10000.0
SC_WIDTH = 3
CONF_WIDTH = 31
CONV_PAD = 32
CONV_CHUNK = 64
N_CHIPS = 4
ATTN_SCALE = (QK_NOPE + QK_ROPE) ** -0.5

ADAM_LR = 0.001
ADAM_B1 = 0.9
ADAM_B2 = 0.999
ADAM_EPS = 1e-08
ADAM_WD = 0.01
ADAM_STEP = 10

VMEM_LIMIT = 56 * 2**20

NN = (((1,), (0,)), ((), ()))
NT = (((1,), (1,)), ((), ()))
TN = (((0,), (0,)), ((), ()))


def _params(sem=None):
    return pltpu.CompilerParams(dimension_semantics=sem, vmem_limit_bytes=VMEM_LIMIT)


class Stk:
    def __init__(self, kind, k, n, arr=None, layers=None, layer=None):
        self.kind, self.k, self.n, self.layers, self.layer = kind, k, n, layers, layer
        self.plain = (kind == "row" and layers is None) or kind == "full"
        self.kloc = k // N_CHIPS if kind == "row" else k
        self.nloc = n // N_CHIPS if kind == "col" else n
        if arr is not None and self.plain:
            arr = arr.reshape(k, n)
        self.arr = arr

    @property
    def shape(self):
        if self.plain:
            return (self.k, self.n)
        lead = (N_CHIPS,) if self.layers is None else (N_CHIPS, self.layers)
        return lead + (self.kloc, self.nloc)

    def spec(self, bk, bn, f):
        if self.plain:
            return pl.BlockSpec((bk, bn), f)
        assert self.kloc % bk == 0 and self.nloc % bn == 0, (self.kloc, bk, self.nloc, bn)
        pk, pn = self.kloc // bk, self.nloc // bn
        kind, layer = self.kind, self.layer

        def imap(*g):
            kb, nb = f(*g)
            if kind == "row":
                q, kb, nb = kb // pk, kb % pk, nb
            else:
                q, kb, nb = nb // pn, kb, nb % pn
            return (q, kb, nb) if layer is None else (q, layer, kb, nb)

        block = (None, bk, bn) if layer is None else (None, None, bk, bn)
        return pl.BlockSpec(block, imap)


def _mm(name, mode, a, b, grid, a_spec, b_spec, acc_shape, extras, extra_specs, out_shapes, out_specs, epi, a_fn=None):
    nk = grid[2]
    ne = len(extras)

    def body(*refs):
        a_ref, b_ref = refs[0], refs[1]
        e_refs = refs[2:2 + ne]
        av = a_ref[...] if a_fn is None else a_fn(a_ref[...])
        part = lax.dot_general(av, b_ref[...], mode, preferred_element_type=F32)
        if nk == 1:
            epi(part, e_refs, refs[2 + ne:])
            return
        o_refs = refs[2 + ne:-1]
        acc = refs[-1]
        k = pl.program_id(2)

        @pl.when(k == 0)
        def _():
            acc[...] = part

        @pl.when(k > 0)
        def _():
            acc[...] += part

        @pl.when(k == nk - 1)
        def _():
            epi(acc[...], e_refs, o_refs)

    return pl.pallas_call(
        body, grid=grid, in_specs=[a_spec, b_spec, *extra_specs], out_specs=out_specs, out_shape=out_shapes,
        scratch_shapes=[pltpu.VMEM(acc_shape, F32)] if nk > 1 else [],
        compiler_params=_params(("parallel", "parallel", "arbitrary")), name=name)(a, b, *extras)


def _tile(n, t):
    t = min(n, t)
    while n % t:
        t -= 8
    assert t > 0, (n, t)
    return t


def mm_nn(name, a, w, tm, tn, tk, epi, out_shapes, out_specs, extras=(), extra_specs=(), a_spec=None, a_fn=None):
    m = a.shape[0]
    tm, tn, tk = _tile(m, tm), _tile(w.n, tn), _tile(w.k, tk)
    grid = (m // tm, w.n // tn, w.k // tk)
    a_spec = a_spec or pl.BlockSpec((tm, tk), lambda i, j, k: (i, k))
    b_spec = w.spec(tk, tn, lambda i, j, k: (k, j))
    return _mm(name, NN, a, w.arr, grid, a_spec, b_spec, (tm, tn), extras, extra_specs, out_shapes, out_specs, epi, a_fn)


def mm_nt(name, a, w, m, tm, tn, tk, epi, out_shapes, out_specs, extras=(), extra_specs=(), a_spec=None):
    tm, tn, tk = _tile(m, tm), _tile(w.k, tn), _tile(w.n, tk)
    grid = (m // tm, w.k // tn, w.n // tk)
    a_spec = a_spec or pl.BlockSpec((tm, tk), lambda i, j, k: (i, k))
    b_spec = w.spec(tn, tk, lambda i, j, k: (j, k))
    return _mm(name, NT, a, w.arr, grid, a_spec, b_spec, (tm, tn), extras, extra_specs, out_shapes, out_specs, epi)


def mm_tn(name, a, b, dw, s, tm=512, tn=512, tk=4096, a_spec=None, b_spec=None, a_fn=None):
    tm, tn, tk = _tile(dw.k, tm), _tile(dw.n, tn), _tile(s, tk)
    grid = (dw.k // tm, dw.n // tn, s // tk)
    a_spec = a_spec or pl.BlockSpec((tk, tm), lambda i, j, k: (k, i))
    b_spec = b_spec or pl.BlockSpec((tk, tn), lambda i, j, k: (k, j))

    def epi(acc, e, o):
        o[0][...] = acc.astype(BF16)

    out = _mm(name, TN, a, b, grid, a_spec, b_spec, (tm, tn), (), (), [jax.ShapeDtypeStruct(dw.shape, BF16)],
              [dw.spec(tm, tn, lambda i, j, k: (i, j))], epi, a_fn)[0]
    return out.reshape(N_CHIPS, dw.k // N_CHIPS, dw.n) if dw.plain else out


def _sds(shape, dtype):
    return jax.ShapeDtypeStruct(shape, dtype)


def _ij(tm, tn):
    return pl.BlockSpec((tm, tn), lambda i, j, k: (i, j))


def _i0(tm, c):
    return pl.BlockSpec((tm, c), lambda i, j, k: (i, 0))


def _0j(r, tn):
    return pl.BlockSpec((r, tn), lambda i, j, k: (0, j))


def _layer_norm_rows(r, g, b):
    mu = jnp.mean(r, axis=-1, keepdims=True)
    d = r - mu
    var = jnp.mean(d * d, axis=-1, keepdims=True)
    rstd = lax.rsqrt(var + LN_EPS)
    xh = d * rstd
    return xh * g + b, xh, rstd


def mm_residual_ln(name, a, w, x, g, b, bias=None, tm=512, tk=1024, a_fn=None):
    s, d = x.shape
    tm = _tile(s, tm)
    extras = [x, g, b] + ([bias] if bias is not None else [])
    especs = [_i0(tm, d), _0j(1, d), _0j(1, d)] + ([_0j(1, d)] if bias is not None else [])

    def epi(acc, e, o):
        r = ALPHA * e[0][...] + acc
        if bias is not None:
            r = r + e[3][...]
        y, xh, rstd = _layer_norm_rows(r, e[1][...], e[2][...])
        o[0][...] = y
        o[1][...] = y.astype(BF16)
        o[2][...] = xh
        o[3][...] = rstd

    return mm_nn(name, a, w, tm, d, tk, epi,
                 [_sds((s, d), F32), _sds((s, d), BF16), _sds((s, d), F32), _sds((s, 1), F32)],
                 [_i0(tm, d), _i0(tm, d), _i0(tm, d), _i0(tm, 1)], extras, especs, a_fn=a_fn)


def mm_plain_nn(name, a, w, out_dtype, tm=1024, tn=512, tk=1024, bias=None):
    m = a.shape[0]
    tm, tn = _tile(m, tm), _tile(w.n, tn)
    if w.kind == "col":
        tn = _tile(w.nloc, tn)

    def epi(acc, e, o):
        if bias is not None:
            acc = acc + e[0][...]
        o[0][...] = acc.astype(out_dtype)

    extras, especs = ([bias], [_0j(1, tn)]) if bias is not None else ((), ())
    return mm_nn(name, a, w, tm, tn, tk, epi, [_sds((m, w.n), out_dtype)], [_ij(tm, tn)], extras, especs)[0]


def mm_plain_nt(name, a, w, out_dtype, tm=1024, tn=512, tk=1024, add=None, add_scale=1.0, a_spec_fn=None):
    m = a.shape[0] if a_spec_fn is None else a_spec_fn[0]
    tm, tn = _tile(m, tm), _tile(w.k, tn)
    tk = _tile(w.n, tk)
    if w.kind == "col":
        tk = _tile(w.nloc, tk)
    if w.kind == "row" and not w.plain:
        tn = _tile(w.kloc, tn)

    def epi(acc, e, o):
        if add is not None:
            acc = acc + add_scale * e[0][...].astype(F32)
        o[0][...] = acc.astype(out_dtype)

    extras, especs = ([add], [_ij(tm, tn)]) if add is not None else ((), ())
    a_spec = None if a_spec_fn is None else a_spec_fn[1](tm, tk)
    return mm_nt(name, a, w, m, tm, tn, tk, epi, [_sds((m, w.k), out_dtype)], [_ij(tm, tn)], extras, especs,
                 a_spec=a_spec)[0]


def _rows(tm, c):
    return pl.BlockSpec((tm, c), lambda i: (i, 0))


def _fix(shape):
    nd = len(shape)
    return pl.BlockSpec(shape, lambda i: (0,) * nd)


def _accumulate(ref, val):
    @pl.when(pl.program_id(0) == 0)
    def _():
        ref[...] = jnp.zeros_like(ref)

    ref[...] += val


def ln_backward(name, dy, xhat, rstd, g, tm=256):
    s, d = dy.shape
    tm = _tile(s, tm)

    def body(dy_ref, xh_ref, rstd_ref, g_ref, dr_ref, drb_ref, dg_ref, db_ref, ds_ref):
        dyv, xh = dy_ref[...], xh_ref[...]
        dxh = dyv * g_ref[...]
        m1 = jnp.mean(dxh, axis=-1, keepdims=True)
        m2 = jnp.mean(dxh * xh, axis=-1, keepdims=True)
        dr = rstd_ref[...] * (dxh - m1 - xh * m2)
        dr_ref[...] = dr
        drb_ref[...] = dr.astype(BF16)
        _accumulate(dg_ref, jnp.sum(dyv * xh, axis=0, keepdims=True))
        _accumulate(db_ref, jnp.sum(dyv, axis=0, keepdims=True))
        _accumulate(ds_ref, jnp.sum(dr, axis=0, keepdims=True))

    return pl.pallas_call(
        body, grid=(s // tm,),
        in_specs=[_rows(tm, d), _rows(tm, d), _rows(tm, 1), _fix((1, d))],
        out_specs=[_rows(tm, d), _rows(tm, d), _fix((1, d)), _fix((1, d)), _fix((1, d))],
        out_shape=[_sds((s, d), F32), _sds((s, d), BF16), _sds((1, d), F32), _sds((1, d), F32), _sds((1, d), F32)],
        compiler_params=_params(("arbitrary",)), name=name)(dy, xhat, rstd, g)


def loss_head(y, target, tm=256):
    s, d = y.shape
    tm = _tile(s, tm)

    def body(y_ref, t_ref, dy_ref, loss_ref):
        e = y_ref[...] - t_ref[...]
        dy_ref[...] = e * (1.0 / d)
        part = 0.5 * jnp.sum(jnp.mean(e * e, axis=-1, keepdims=True), axis=0, keepdims=True)
        _accumulate(loss_ref, jnp.broadcast_to(part, (1, d)))

    return pl.pallas_call(
        body, grid=(s // tm,), in_specs=[_rows(tm, d), _rows(tm, d)],
        out_specs=[_rows(tm, d), _fix((1, d))], out_shape=[_sds((s, d), F32), _sds((1, d), F32)],
        compiler_params=_params(("arbitrary",)), name="loss_head")(y, target)


def _cols(s, tc, off=0):
    return pl.BlockSpec((s, tc), lambda i: (0, i + off))


def _shift_down(z, sft, rows):
    return jnp.where(rows >= sft, pltpu.roll(z, sft, 0), 0.0)


def _shift_up(z, sft, rows, s):
    return jnp.where(rows < s - sft, pltpu.roll(z, (s - sft) % s, 0), 0.0)


def short_conv_gate(u, conv_w, tc=256):
    s, d3 = u.shape
    d = d3 // 3
    nb = d // tc

    def body(b_ref, c_ref, h_ref, w_ref, o_ref):
        rows = lax.broadcasted_iota(jnp.int32, (s, tc), 0)
        z = c_ref[...] * h_ref[...]
        cz = jnp.zeros((s, tc), F32)
        for k in range(SC_WIDTH):
            sft = SC_WIDTH - 1 - k
            cz = cz + w_ref[pl.ds(k, 1), :] * (_shift_down(z, sft, rows) if sft else z)
        o_ref[...] = (b_ref[...] * cz).astype(BF16)

    return pl.pallas_call(
        body, grid=(nb,),
        in_specs=[_cols(s, tc), _cols(s, tc, nb), _cols(s, tc, 2 * nb), _cols(SC_WIDTH, tc)],
        out_specs=_cols(s, tc), out_shape=_sds((s, d), BF16),
        compiler_params=_params(("parallel",)), name="short_conv_gate")(u, u, u, conv_w)


def short_conv_gate_bwd(u, conv_w, dg, tc=256):
    s, d3 = u.shape
    d = d3 // 3
    nb = d // tc

    def body(b_ref, c_ref, h_ref, w_ref, dg_ref, du_ref, dw_ref):
        rows = lax.broadcasted_iota(jnp.int32, (s, tc), 0)
        c, h, dgv = c_ref[...], h_ref[...], dg_ref[...]
        z = c * h
        dcz = dgv * b_ref[...]
        cz = jnp.zeros((s, tc), F32)
        dz = jnp.zeros((s, tc), F32)
        for k in range(SC_WIDTH):
            sft = SC_WIDTH - 1 - k
            zs = _shift_down(z, sft, rows) if sft else z
            wk = w_ref[pl.ds(k, 1), :]
            cz = cz + wk * zs
            dz = dz + wk * (_shift_up(dcz, sft, rows, s) if sft else dcz)
            dw_ref[pl.ds(k, 1), :] = jnp.sum(dcz * zs, axis=0, keepdims=True)
        du_ref[0] = (dgv * cz).astype(BF16)
        du_ref[1] = (dz * h).astype(BF16)
        du_ref[2] = (dz * c).astype(BF16)

    return pl.pallas_call(
        body, grid=(nb,),
        in_specs=[_cols(s, tc), _cols(s, tc, nb), _cols(s, tc, 2 * nb), _cols(SC_WIDTH, tc), _cols(s, tc)],
        out_specs=[pl.BlockSpec((3, s, tc), lambda i: (0, 0, i)), _cols(SC_WIDTH, tc)],
        out_shape=[_sds((3, s, d), BF16), _sds((SC_WIDTH, d), F32)],
        compiler_params=_params(("parallel",)), name="short_conv_gate_bwd")(u, u, u, conv_w, dg)


def _store_shifted_down(ref, z, rows):
    s, tc = z.shape
    for b in range(8):
        ref[b, pl.ds(0, CONV_PAD), :] = jnp.zeros((CONV_PAD, tc), F32)
        ref[b, pl.ds(CONV_PAD, s), :] = z if b == 0 else _shift_down(z, b, rows)


def _store_shifted_up(ref, z, rows):
    s, tc = z.shape
    for b in range(8):
        ref[b, pl.ds(0, s), :] = z if b == 0 else _shift_up(z, b, rows, s)
        ref[b, pl.ds(s, CONV_PAD), :] = jnp.zeros((CONV_PAD, tc), F32)


def conformer_glu_conv(u, dw_w, dw_b, tc=128):
    s, d2 = u.shape
    d = d2 // 2
    nb = d // tc

    ch = min(CONV_CHUNK, s)

    def body(a_ref, g_ref, w_ref, b_ref, o_ref, down):
        rows = lax.broadcasted_iota(jnp.int32, (s, tc), 0)
        _store_shifted_down(down, a_ref[...] * jax.nn.sigmoid(g_ref[...]), rows)

        def chunk(ci, carry):
            r0 = pl.multiple_of(ci * ch, ch)
            acc = jnp.broadcast_to(b_ref[...], (ch, tc))
            for k in range(CONF_WIDTH):
                sft = CONF_WIDTH - 1 - k
                acc = acc + w_ref[pl.ds(k, 1), :] * down[sft % 8, pl.ds(CONV_PAD + r0 - (sft // 8) * 8, ch), :]
            o_ref[pl.ds(r0, ch), :] = acc
            return carry

        lax.fori_loop(0, s // ch, chunk, 0)

    return pl.pallas_call(
        body, grid=(nb,),
        in_specs=[_cols(s, tc), _cols(s, tc, nb), _cols(CONF_WIDTH, tc), _cols(1, tc)],
        out_specs=_cols(s, tc), out_shape=_sds((s, d), F32),
        scratch_shapes=[pltpu.VMEM((8, CONV_PAD + s, tc), F32)],
        compiler_params=_params(("parallel",)), name="conformer_glu_conv")(u, u, dw_w, dw_b)


def conformer_glu_conv_bwd(u, dw_w, dhc, tc=128):
    s, d2 = u.shape
    d = d2 // 2
    nb = d // tc
    ch = min(CONV_CHUNK, s)

    def body(a_ref, g_ref, w_ref, dhc_ref, du_ref, dbias_ref, dw_ref, db_ref, down, up, dw_acc, dh_buf):
        rows = lax.broadcasted_iota(jnp.int32, (s, tc), 0)
        a = a_ref[...]
        sg = jax.nn.sigmoid(g_ref[...])
        dhcv = dhc_ref[...]
        _store_shifted_down(down, a * sg, rows)
        _store_shifted_up(up, dhcv, rows)
        dw_acc[...] = jnp.zeros_like(dw_acc)

        def chunk(ci, carry):
            r0 = pl.multiple_of(ci * ch, ch)
            dc = dhc_ref[pl.ds(r0, ch), :]
            dh = jnp.zeros((ch, tc), F32)
            for k in range(CONF_WIDTH):
                sft = CONF_WIDTH - 1 - k
                a8, b = (sft // 8) * 8, sft % 8
                dh = dh + w_ref[pl.ds(k, 1), :] * up[b, pl.ds(r0 + a8, ch), :]
                prod = dc * down[b, pl.ds(CONV_PAD + r0 - a8, ch), :]
                dw_acc[k] += jnp.sum(prod.reshape(ch // 8, 8, tc), axis=0)
            dh_buf[pl.ds(r0, ch), :] = dh
            return carry

        lax.fori_loop(0, s // ch, chunk, 0)
        dh = dh_buf[...]
        da = dh * sg
        dgate = dh * a * sg * (1.0 - sg)
        du_ref[0] = da.astype(BF16)
        du_ref[1] = dgate.astype(BF16)
        dbias_ref[pl.ds(0, 1), :] = jnp.sum(da, axis=0, keepdims=True)
        dbias_ref[pl.ds(1, 1), :] = jnp.sum(dgate, axis=0, keepdims=True)
        db_ref[...] = jnp.sum(dhcv, axis=0, keepdims=True)
        for k in range(CONF_WIDTH):
            dw_ref[pl.ds(k, 1), :] = jnp.sum(dw_acc[k], axis=0, keepdims=True)

    return pl.pallas_call(
        body, grid=(nb,),
        in_specs=[_cols(s, tc), _cols(s, tc, nb), _cols(CONF_WIDTH, tc), _cols(s, tc)],
        out_specs=[pl.BlockSpec((2, s, tc), lambda i: (0, 0, i)), _cols(2, tc), _cols(CONF_WIDTH, tc), _cols(1, tc)],
        out_shape=[_sds((2, s, d), BF16), _sds((2, d), F32), _sds((CONF_WIDTH, d), F32), _sds((1, d), F32)],
        scratch_shapes=[pltpu.VMEM((8, CONV_PAD + s, tc), F32), pltpu.VMEM((8, CONV_PAD + s, tc), F32),
                        pltpu.VMEM((CONF_WIDTH + 1, 8, tc), F32), pltpu.VMEM((s, tc), F32)],
        compiler_params=_params(("parallel",)), name="conformer_glu_conv_bwd")(u, u, dw_w, dhc)


def conformer_norm_swish(hc, g, b, tm=256):
    s, d = hc.shape
    tm = _tile(s, tm)

    def body(h_ref, g_ref, b_ref, o_ref):
        n, _, _ = _layer_norm_rows(h_ref[...], g_ref[...], b_ref[...])
        o_ref[...] = (n * jax.nn.sigmoid(n)).astype(BF16)

    return pl.pallas_call(
        body, grid=(s // tm,), in_specs=[_rows(tm, d), _fix((1, d)), _fix((1, d))], out_specs=_rows(tm, d),
        out_shape=_sds((s, d), BF16), compiler_params=_params(("parallel",)), name="conformer_norm_swish")(hc, g, b)


def conformer_norm_swish_bwd(hc, g, b, ds, tm=256):
    s, d = hc.shape
    tm = _tile(s, tm)

    def body(h_ref, g_ref, b_ref, ds_ref, dh_ref, dg_ref, db_ref):
        n, nh, rstd = _layer_norm_rows(h_ref[...], g_ref[...], b_ref[...])
        sg = jax.nn.sigmoid(n)
        dn = ds_ref[...] * (sg * (1.0 + n * (1.0 - sg)))
        dnh = dn * g_ref[...]
        m1 = jnp.mean(dnh, axis=-1, keepdims=True)
        m2 = jnp.mean(dnh * nh, axis=-1, keepdims=True)
        dh_ref[...] = rstd * (dnh - m1 - nh * m2)
        _accumulate(dg_ref, jnp.sum(dn * nh, axis=0, keepdims=True))
        _accumulate(db_ref, jnp.sum(dn, axis=0, keepdims=True))

    return pl.pallas_call(
        body, grid=(s // tm,), in_specs=[_rows(tm, d), _fix((1, d)), _fix((1, d)), _rows(tm, d)],
        out_specs=[_rows(tm, d), _fix((1, d)), _fix((1, d))],
        out_shape=[_sds((s, d), F32), _sds((1, d), F32), _sds((1, d), F32)],
        compiler_params=_params(("arbitrary",)), name="conformer_norm_swish_bwd")(hc, g, b, ds)


def _swap_halves(x):
    lane = lax.broadcasted_iota(jnp.int32, x.shape, 1)
    return jnp.where(lane < QK_ROPE // 2, pltpu.roll(x, 128 - QK_ROPE // 2, 1), pltpu.roll(x, QK_ROPE // 2, 1))


def _rope(x, cf, sf):
    return x * cf + _swap_halves(x) * sf


def _unrope(dx, cf, sf):
    return dx * cf - _swap_halves(dx) * sf


def _rms_rows(x, g):
    r = lax.rsqrt(jnp.mean(x * x, axis=-1, keepdims=True) + RMS_EPS)
    return x * r, r


def mla_latents(t, g_q, g_kv, cf, sf, tm=256):
    s = t.shape[0]
    tm = _tile(s, tm)

    def body(t_ref, gq_ref, gkv_ref, cf_ref, sf_ref, cq_ref, ckv_ref, kpe_ref):
        xq, _ = _rms_rows(t_ref[:, 0:Q_LORA], gq_ref[...])
        cq_ref[...] = (xq * gq_ref[...]).astype(BF16)
        xkv, _ = _rms_rows(t_ref[:, Q_LORA:Q_LORA + KV_LORA], gkv_ref[...])
        ckv_ref[...] = (xkv * gkv_ref[...]).astype(BF16)
        kpe_ref[...] = _rope(t_ref[:, Q_LORA + KV_LORA:], cf_ref[...], sf_ref[...]).astype(BF16)

    w = Q_LORA + KV_LORA + 128
    return pl.pallas_call(
        body, grid=(s // tm,),
        in_specs=[_rows(tm, w), _fix((1, Q_LORA)), _fix((1, KV_LORA)), _rows(tm, 128), _rows(tm, 128)],
        out_specs=[_rows(tm, Q_LORA), _rows(tm, KV_LORA), _rows(tm, 128)],
        out_shape=[_sds((s, Q_LORA), BF16), _sds((s, KV_LORA), BF16), _sds((s, 128), BF16)],
        compiler_params=_params(("parallel",)), name="mla_latents")(t, g_q, g_kv, cf, sf)


def mla_latents_bwd(t, g_q, g_kv, cf, sf, dcq, dckv, dkpe, tm=256):
    s = t.shape[0]
    tm = _tile(s, tm)
    w = Q_LORA + KV_LORA + 128

    def rms_bwd(x, g, dy):
        xh, r = _rms_rows(x, g)
        dxh = dy * g
        return r * (dxh - xh * jnp.mean(dxh * xh, axis=-1, keepdims=True)), jnp.sum(dy * xh, axis=0, keepdims=True)

    def body(t_ref, gq_ref, gkv_ref, cf_ref, sf_ref, dcq_ref, dckv_ref, dkpe_ref, dt_ref, dgq_ref, dgkv_ref):
        dxq, dgq = rms_bwd(t_ref[:, 0:Q_LORA], gq_ref[...], dcq_ref[...])
        dxkv, dgkv = rms_bwd(t_ref[:, Q_LORA:Q_LORA + KV_LORA], gkv_ref[...], dckv_ref[...])
        dt_ref[:, 0:Q_LORA] = dxq.astype(BF16)
        dt_ref[:, Q_LORA:Q_LORA + KV_LORA] = dxkv.astype(BF16)
        dt_ref[:, Q_LORA + KV_LORA:] = _unrope(dkpe_ref[...], cf_ref[...], sf_ref[...]).astype(BF16)
        _accumulate(dgq_ref, dgq)
        _accumulate(dgkv_ref, dgkv)

    return pl.pallas_call(
        body, grid=(s // tm,),
        in_specs=[_rows(tm, w), _fix((1, Q_LORA)), _fix((1, KV_LORA)), _rows(tm, 128), _rows(tm, 128),
                  _rows(tm, Q_LORA), _rows(tm, KV_LORA), _rows(tm, 128)],
        out_specs=[_rows(tm, w), _fix((1, Q_LORA)), _fix((1, KV_LORA))],
        out_shape=[_sds((s, w), BF16), _sds((1, Q_LORA), F32), _sds((1, KV_LORA), F32)],
        compiler_params=_params(("arbitrary",)), name="mla_latents_bwd")(t, g_q, g_kv, cf, sf, dcq, dckv, dkpe)


def mla_queries(cq, w_uq, cf, sf, tm=512):
    s = cq.shape[0]
    tm = _tile(s, tm)

    def epi(acc, e, o):
        o[0][:, 0:QK_NOPE] = acc[:, 0:QK_NOPE].astype(BF16)
        o[0][:, QK_NOPE:] = _rope(acc[:, QK_NOPE:], e[0][...], e[1][...]).astype(BF16)

    return mm_nn("mla_queries", cq, w_uq, tm, HEAD_PAD, Q_LORA, epi, [_sds((s, N_HEADS * HEAD_PAD), BF16)],
                 [_ij(tm, HEAD_PAD)], [cf, sf], [_i0(tm, 128), _i0(tm, 128)])[0]


def mla_keys(ckv, w_uk, kpe, tm=512):
    s = ckv.shape[0]
    tm = _tile(s, tm)

    def epi(acc, e, o):
        o[0][:, 0:QK_NOPE] = acc.astype(BF16)
        o[0][:, QK_NOPE:] = e[0][...]

    return mm_nn("mla_keys", ckv, w_uk, tm, QK_NOPE, KV_LORA, epi, [_sds((s, N_HEADS * HEAD_PAD), BF16)],
                 [_ij(tm, HEAD_PAD)], [kpe], [_i0(tm, 128)])[0]


def _masked_scores(q, k, qi, tq, kv):
    sc = lax.dot_general(q, k, NT, preferred_element_type=F32) * ATTN_SCALE
    row = lax.broadcasted_iota(jnp.int32, (tq, kv), 0) + qi * tq
    col = lax.broadcasted_iota(jnp.int32, (tq, kv), 1)
    ok = lax.shift_right_logical(col, CHUNK_SHIFT) <= lax.shift_right_logical(row, CHUNK_SHIFT)
    return jnp.where(ok, sc, -1e30)


def attention(q, k, v, tq=512):
    s = q.shape[0]
    tq = _tile(s, tq)
    nq = s // tq

    def body(q_ref, k_ref, v_ref, o_ref):
        for qi in range(nq):
            kv = (qi + 1) * tq
            sc = _masked_scores(q_ref[pl.ds(qi * tq, tq), :], k_ref[pl.ds(0, kv), :], qi, tq, kv)
            p = jnp.exp(sc - jnp.max(sc, axis=-1, keepdims=True))
            o = lax.dot_general(p.astype(BF16), v_ref[pl.ds(0, kv), :], NN, preferred_element_type=F32)
            o_ref[pl.ds(qi * tq, tq), :] = (o / jnp.sum(p, axis=-1, keepdims=True)).astype(BF16)

    hq = pl.BlockSpec((s, HEAD_PAD), lambda h: (0, h))
    hv = pl.BlockSpec((s, V_HEAD), lambda h: (0, h))
    return pl.pallas_call(
        body, grid=(N_HEADS,), in_specs=[hq, hq, hv], out_specs=hv, out_shape=_sds((s, N_HEADS * V_HEAD), BF16),
        compiler_params=_params(("parallel",)), name="attention")(q, k, v)


def attention_bwd(q, k, v, do, tq=512):
    s = q.shape[0]
    tq = _tile(s, tq)
    nq = s // tq

    def body(q_ref, k_ref, v_ref, do_ref, dq_ref, dk_ref, dv_ref, dk_acc, dv_acc):
        dk_acc[...] = jnp.zeros_like(dk_acc)
        dv_acc[...] = jnp.zeros_like(dv_acc)
        for qi in range(nq):
            kv = (qi + 1) * tq
            qt = q_ref[pl.ds(qi * tq, tq), :]
            kt = k_ref[pl.ds(0, kv), :]
            dot = do_ref[pl.ds(qi * tq, tq), :]
            sc = _masked_scores(qt, kt, qi, tq, kv)
            p = jnp.exp(sc - jnp.max(sc, axis=-1, keepdims=True))
            p = p / jnp.sum(p, axis=-1, keepdims=True)
            dp = lax.dot_general(dot, v_ref[pl.ds(0, kv), :], NT, preferred_element_type=F32)
            delta = jnp.sum(p * dp, axis=-1, keepdims=True)
            ds = (p * (dp - delta) * ATTN_SCALE).astype(BF16)
            dq_ref[pl.ds(qi * tq, tq), :] = lax.dot_general(ds, kt, NN, preferred_element_type=F32).astype(BF16)
            dk_acc[pl.ds(0, kv), :] += lax.dot_general(ds, qt, TN, preferred_element_type=F32)
            dv_acc[pl.ds(0, kv), :] += lax.dot_general(p.astype(BF16), dot, TN, preferred_element_type=F32)
        dk_ref[...] = dk_acc[...].astype(BF16)
        dv_ref[...] = dv_acc[...].astype(BF16)

    hq = pl.BlockSpec((s, HEAD_PAD), lambda h: (0, h))
    hv = pl.BlockSpec((s, V_HEAD), lambda h: (0, h))
    return pl.pallas_call(
        body, grid=(N_HEADS,), in_specs=[hq, hq, hv, hv], out_specs=[hq, hq, hv],
        out_shape=[_sds((s, N_HEADS * HEAD_PAD), BF16), _sds((s, N_HEADS * HEAD_PAD), BF16),
                   _sds((s, N_HEADS * V_HEAD), BF16)],
        scratch_shapes=[pltpu.VMEM((s, HEAD_PAD), F32), pltpu.VMEM((s, V_HEAD), F32)],
        compiler_params=_params(("parallel",)), name="attention_bwd")(q, k, v, do)


def mla_unrope_grads(dq, dk, cf, sf, tm=256):
    s = dq.shape[0]
    tm = _tile(s, tm)

    def body(dq_ref, dk_ref, cf_ref, sf_ref, dql_ref, dkn_ref, dkpe_ref):
        cfv, sfv = cf_ref[...], sf_ref[...]
        dkpe = jnp.zeros((tm, 128), F32)
        for h in range(N_HEADS):
            lo = h * HEAD_PAD
            dql_ref[:, lo:lo + QK_NOPE] = dq_ref[:, lo:lo + QK_NOPE]
            dql_ref[:, lo + QK_NOPE:lo + HEAD_PAD] = _unrope(
                dq_ref[:, lo + QK_NOPE:lo + HEAD_PAD].astype(F32), cfv, sfv).astype(BF16)
            dkn_ref[:, h * QK_NOPE:(h + 1) * QK_NOPE] = dk_ref[:, lo:lo + QK_NOPE]
            dkpe = dkpe + dk_ref[:, lo + QK_NOPE:lo + HEAD_PAD].astype(F32)
        dkpe_ref[...] = dkpe

    wq = N_HEADS * HEAD_PAD
    return pl.pallas_call(
        body, grid=(s // tm,), in_specs=[_rows(tm, wq), _rows(tm, wq), _rows(tm, 128), _rows(tm, 128)],
        out_specs=[_rows(tm, wq), _rows(tm, N_HEADS * QK_NOPE), _rows(tm, 128)],
        out_shape=[_sds((s, wq), BF16), _sds((s, N_HEADS * QK_NOPE), BF16), _sds((s, 128), F32)],
        compiler_params=_params(("parallel",)), name="mla_unrope_grads")(dq, dk, cf, sf)


ANY = pl.BlockSpec(memory_space=pl.ANY)
GATHER_ID = 1
CHIP_EXCHANGE_ID = 2
PAIR_ID = 3
ALL_ID = 4


def _nbytes(a):
    return a.size * a.dtype.itemsize


def _copy_cost(operand_bytes, sent_fraction):
    sent = int(operand_bytes * sent_fraction)
    return pl.CostEstimate(flops=0, transcendentals=0, bytes_accessed=2 * sent, remote_bytes_transferred=sent)


def _handshake(peers):
    barrier = pltpu.get_barrier_semaphore()
    for peer in peers:
        pl.semaphore_signal(barrier, inc=1, device_id=peer, device_id_type=MESH)
    pl.semaphore_wait(barrier, len(peers))


def _place():
    x, y, c = lax.axis_index("x"), lax.axis_index("y"), lax.axis_index("c")
    chips = [(1 - x, y), (x, 1 - y), (1 - x, 1 - y)]
    return x, y, c, chips


def _half(ref, hc, axis=0):
    n = ref.shape[axis] // 2
    idx = (slice(None),) * axis + (pl.ds(hc * n, n),)
    return ref.at[idx]


def gather_shards(name, tensors, by_columns=()):
    nt = len(tensors)

    def body(*refs):
        a, g = refs[:nt], refs[nt:2 * nt]
        send, recv = refs[2 * nt:]
        x, y, c, _ = _place()
        q = 2 * x + y
        sib, xn, yn = (x, y, 1 - c), (1 - x, y, c), (x, 1 - y, c)
        q_xn, q_yn, q_diag = 2 * (1 - x) + y, 2 * x + 1 - y, 2 * (1 - x) + 1 - y
        _handshake([sib, xn, yn])

        def whole(t, p):
            if t in by_columns:
                n = a[t].shape[1]
                return g[t].at[:, pl.ds(p * n, n)]
            return g[t].at[p]

        def part(t, p, hc, quarter=None):
            rows = a[t].shape[0]
            if quarter is None:
                return whole(t, p).at[pl.ds(hc * (rows // 2), rows // 2)]
            return whole(t, p).at[pl.ds(hc * (rows // 2) + quarter * (rows // 4), rows // 4)]

        def rc(t, k, src, dst, to):
            return pltpu.make_async_remote_copy(src_ref=src, dst_ref=dst, send_sem=send.at[t, k], recv_sem=recv.at[t, k],
                                                device_id=to, device_id_type=MESH)

        sent = []

        def go(cp):
            cp.start()
            sent.append(cp)

        def landed(t, k, piece, frm):
            rc(t, k, piece, piece, frm).wait_recv()
            return piece

        for t in range(nt):
            go(rc(t, 8, a[t], whole(t, q), sib))
            mine = _half(a[t], c)
            go(rc(t, 0, mine, part(t, q, c), xn))
            go(rc(t, 1, mine, part(t, q, c), yn))
        for t in range(nt):
            from_y = landed(t, 1, part(t, q_yn, c), yn)
            go(rc(t, 2, part(t, q_yn, c, 0), part(t, q_yn, c, 0), xn))
            go(rc(t, 5, from_y, from_y, sib))
            from_x = landed(t, 0, part(t, q_xn, c), xn)
            go(rc(t, 3, part(t, q_xn, c, 1), part(t, q_xn, c, 1), yn))
            go(rc(t, 4, from_x, from_x, sib))
        for t in range(nt):
            for k, frm in ((2, xn), (3, yn)):
                piece = landed(t, k, part(t, q_diag, c, k - 2), frm)
                go(rc(t, 4 + k, piece, piece, sib))
        for t in range(nt):
            landed(t, 4, part(t, q_xn, 1 - c), sib)
            landed(t, 5, part(t, q_yn, 1 - c), sib)
            landed(t, 6, part(t, q_diag, 1 - c, 0), sib)
            landed(t, 7, part(t, q_diag, 1 - c, 1), sib)
            landed(t, 8, whole(t, q), sib)
        for cp in sent:
            cp.wait_send()

    return pl.kernel(
        body, name=name,
        out_type=[_sds((a.shape[0], N_CHIPS * a.shape[1]) if t in by_columns else (N_CHIPS,) + a.shape, a.dtype)
                  for t, a in enumerate(tensors)],
        mesh=plsc.ScalarSubcoreMesh(axis_name="sequencer", num_cores=1),
        scratch_types=[pltpu.SemaphoreType.DMA((nt, 9)), pltpu.SemaphoreType.DMA((nt, 9))],
        cost_estimate=_copy_cost(sum(_nbytes(a) for a in tensors), 4),
        compiler_params=pltpu.CompilerParams(collective_id=GATHER_ID))(*tensors)


def pair_exchange(name, grads, on_sequencer):
    nt = len(grads)

    def body(*refs):
        g, theirs = refs[:nt], refs[nt:2 * nt]
        send, recv = refs[2 * nt:]
        x, y, c, _ = _place()
        if on_sequencer:
            _handshake([(x, y, 1 - c)])
        cps = []
        for t in range(nt):
            cp = pltpu.make_async_remote_copy(src_ref=_half(g[t], 1 - c, 1), dst_ref=theirs[t], send_sem=send.at[t],
                                              recv_sem=recv.at[t], device_id=(x, y, 1 - c), device_id_type=MESH)
            cp.start()
            cps.append(cp)
        for cp in cps:
            cp.wait()

    if not on_sequencer:
        return pl.pallas_call(
            body, in_specs=[ANY] * nt, out_specs=[ANY] * nt,
            out_shape=[_sds((N_CHIPS, a.shape[1] // 2, a.shape[2]), a.dtype) for a in grads],
            scratch_shapes=[pltpu.SemaphoreType.DMA((nt,)), pltpu.SemaphoreType.DMA((nt,))],
            name=name)(*grads)
    return pl.kernel(
        body, name=name, out_type=[_sds((N_CHIPS, a.shape[1] // 2, a.shape[2]), a.dtype) for a in grads],
        mesh=plsc.ScalarSubcoreMesh(axis_name="sequencer", num_cores=1),
        scratch_types=[pltpu.SemaphoreType.DMA((nt,)), pltpu.SemaphoreType.DMA((nt,))],
        cost_estimate=_copy_cost(sum(_nbytes(a) for a in grads), 0.5),
        compiler_params=pltpu.CompilerParams(collective_id=PAIR_ID))(*grads)


def chip_exchange(name, parts):
    nt = len(parts)

    def body(*refs):
        a, r = refs[:nt], refs[nt:2 * nt]
        send, recv = refs[2 * nt:]
        x, y, c, chips = _place()
        _handshake([(*chip, c) for chip in chips])
        cps = []
        for t in range(nt):
            for j, chip in enumerate(chips):
                cp = pltpu.make_async_remote_copy(
                    src_ref=a[t].at[2 * chip[0] + chip[1]], dst_ref=r[t].at[j], send_sem=send.at[t, j],
                    recv_sem=recv.at[t, j], device_id=(*chip, c), device_id_type=MESH)
                cp.start()
                cps.append(cp)
        for cp in cps:
            cp.wait()

    return pl.kernel(
        body, name=name, out_type=[_sds((N_CHIPS - 1,) + a.shape[1:], a.dtype) for a in parts],
        mesh=plsc.ScalarSubcoreMesh(axis_name="sequencer", num_cores=1),
        scratch_types=[pltpu.SemaphoreType.DMA((nt, 3)), pltpu.SemaphoreType.DMA((nt, 3))],
        cost_estimate=_copy_cost(sum(_nbytes(a) for a in parts), 0.75),
        compiler_params=pltpu.CompilerParams(collective_id=CHIP_EXCHANGE_ID))(*parts)


def pair_share(name, halves):
    nt = len(halves)

    def body(*refs):
        h, other = refs[:nt], refs[nt:2 * nt]
        send, recv = refs[2 * nt:]
        x, y, c, _ = _place()
        _handshake([(x, y, 1 - c)])
        cps = []
        for t in range(nt):
            cp = pltpu.make_async_remote_copy(src_ref=h[t], dst_ref=other[t], send_sem=send.at[t], recv_sem=recv.at[t],
                                              device_id=(x, y, 1 - c), device_id_type=MESH)
            cp.start()
            cps.append(cp)
        for cp in cps:
            cp.wait()

    return pl.kernel(
        body, name=name, out_type=[_sds(a.shape, a.dtype) for a in halves],
        mesh=plsc.ScalarSubcoreMesh(axis_name="sequencer", num_cores=1),
        scratch_types=[pltpu.SemaphoreType.DMA((nt,)), pltpu.SemaphoreType.DMA((nt,))],
        cost_estimate=_copy_cost(sum(_nbytes(a) for a in halves), 1),
        compiler_params=pltpu.CompilerParams(collective_id=PAIR_ID))(*halves)


def pack_rows(name, parts, rows):
    cdim = parts[0].shape[1]
    n = len(parts)
    vm = pl.BlockSpec(memory_space=pltpu.VMEM)

    def pack(*refs):
        p, o_ref = refs[:n], refs[n]
        at = 0
        for ref in p:
            o_ref[pl.ds(at, ref.shape[0]), :] = ref[...]
            at += ref.shape[0]
        o_ref[pl.ds(at, rows - at), :] = jnp.zeros((rows - at, cdim), F32)

    return pl.pallas_call(pack, in_specs=[vm] * n, out_specs=vm, out_shape=_sds((rows, cdim), F32), name=name)(*parts)


def all_reduce_small(parts, rows):
    cdim = parts[0].shape[1]
    vm = pl.BlockSpec(memory_space=pltpu.VMEM)
    mine = pack_rows("small_pack", parts, rows)

    def exchange(mine_ref, buf, send, recv, lsem):
        x, y, c, _ = _place()
        me = 4 * x + 2 * y + c
        peers = [(x ^ (k >> 2), y ^ ((k >> 1) & 1), c ^ (k & 1)) for k in range(1, 8)]
        _handshake(peers)
        own = pltpu.make_async_copy(mine_ref, buf.at[me], lsem)
        own.start()
        cps = []
        for k, to in enumerate(peers):
            cp = pltpu.make_async_remote_copy(src_ref=mine_ref, dst_ref=buf.at[me], send_sem=send.at[k], recv_sem=recv.at[k],
                                              device_id=to, device_id_type=MESH)
            cp.start()
            cps.append(cp)
        for k, (px, py, pc) in enumerate(peers):
            pltpu.make_async_remote_copy(src_ref=mine_ref, dst_ref=buf.at[4 * px + 2 * py + pc], send_sem=send.at[k],
                                         recv_sem=recv.at[k], device_id=(x, y, c), device_id_type=MESH).wait_recv()
        for cp in cps:
            cp.wait_send()
        own.wait()

    landed = pl.kernel(
        exchange, name="small_exchange", out_type=_sds((8, rows, cdim), F32),
        mesh=plsc.ScalarSubcoreMesh(axis_name="sequencer", num_cores=1),
        scratch_types=[pltpu.SemaphoreType.DMA((7,)), pltpu.SemaphoreType.DMA((7,)), pltpu.SemaphoreType.DMA],
        cost_estimate=_copy_cost(rows * cdim * 4, 7),
        compiler_params=pltpu.CompilerParams(collective_id=ALL_ID))(mine)

    def total(buf, o_ref):
        acc = buf[0]
        for d in range(1, 8):
            acc = acc + buf[d]
        o_ref[...] = acc

    return pl.pallas_call(total, in_specs=[vm], out_specs=vm, out_shape=_sds((rows, cdim), F32), name="small_sum")(landed)


def pair_sum(g, theirs, core, tm=256):
    _, r, c = g.shape
    tm = _tile(r // 2, tm)
    nh = r // 2 // tm

    def body(core_ref, a_ref, b_ref, o_ref):
        o_ref[...] = (a_ref[...].astype(F32) + b_ref[...].astype(F32)).astype(BF16)

    blk = (N_CHIPS, tm, c)
    return pl.pallas_call(
        body, grid_spec=pltpu.PrefetchScalarGridSpec(
            num_scalar_prefetch=1, grid=(nh,),
            in_specs=[pl.BlockSpec(blk, lambda i, cr: (0, cr[0] * nh + i, 0)), pl.BlockSpec(blk, lambda i, cr: (0, i, 0))],
            out_specs=pl.BlockSpec(blk, lambda i, cr: (0, i, 0))),
        out_shape=_sds(theirs.shape, BF16), compiler_params=_params(("parallel",)), name="pair_sum")(core, g, theirs)


def chip_sum(own, landed, chip, stack, layer, layers, tm=256):
    _, r, c = own.shape
    tm = _tile(r, tm)

    def body(chip_ref, own_ref, l_ref, *rest):
        acc = own_ref[...].astype(F32)
        for j in range(N_CHIPS - 1):
            acc = acc + l_ref[j].astype(F32)
        rest[-1][...] = acc

    in_specs = [pl.BlockSpec((None, tm, c), lambda i, qr: (qr[0], i, 0)),
                pl.BlockSpec((N_CHIPS - 1, tm, c), lambda i, qr: (0, i, 0))]
    args = [chip, own, landed]
    if stack is not None:
        in_specs.append(ANY)
        args.append(stack)
    return pl.pallas_call(
        body, grid_spec=pltpu.PrefetchScalarGridSpec(
            num_scalar_prefetch=1, grid=(r // tm,), in_specs=in_specs,
            out_specs=pl.BlockSpec((None, tm, c), lambda i, qr: (layer, i, 0))),
        out_shape=_sds((layers, r, c), F32), input_output_aliases={3: 0} if stack is not None else {},
        compiler_params=_params(("parallel",)), name="chip_sum")(*args)


def adamw_joined(w, m, v, g_mine, g_theirs, core, tm=512):
    nl, r, c = w.shape
    tm = _tile(r // 2, tm)
    nh = r // 2 // tm
    bc1 = 1.0 - ADAM_B1 ** ADAM_STEP
    bc2 = 1.0 - ADAM_B2 ** ADAM_STEP

    def body(core_ref, w_ref, m_ref, v_ref, gm_ref, gt_ref, g_ref, d_ref, nm_ref, nv_ref):
        mine = (pl.program_id(1) // nh) == core_ref[0]
        gv = jnp.where(mine, gm_ref[...], gt_ref[...])
        nm = ADAM_B1 * m_ref[...] + (1.0 - ADAM_B1) * gv
        nv = ADAM_B2 * v_ref[...] + (1.0 - ADAM_B2) * (gv * gv)
        g_ref[...] = gv
        d_ref[...] = -ADAM_LR * ((nm / bc1) / (jnp.sqrt(nv / bc2) + ADAM_EPS) + ADAM_WD * w_ref[...])
        nm_ref[...] = nm
        nv_ref[...] = nv

    full = pl.BlockSpec((None, tm, c), lambda l, i, cr: (l, i, 0))
    half = pl.BlockSpec((None, tm, c), lambda l, i, cr: (l, i % nh, 0))
    return pl.pallas_call(
        body, grid_spec=pltpu.PrefetchScalarGridSpec(
            num_scalar_prefetch=1, grid=(nl, r // tm), in_specs=[full, full, full, half, half], out_specs=[full] * 4),
        out_shape=[_sds((nl, r, c), F32)] * 4, compiler_params=_params(("parallel", "parallel")),
        name="adamw_joined")(core, w, m, v, g_mine, g_theirs)


def adamw(w, g, m, v, tm=256):
    shape = w.shape
    c = shape[-1]
    r = w.size // c
    tm = _tile(r, tm)
    bc1 = 1.0 - ADAM_B1 ** ADAM_STEP
    bc2 = 1.0 - ADAM_B2 ** ADAM_STEP

    def body(w_ref, g_ref, m_ref, v_ref, d_ref, nm_ref, nv_ref):
        gv = g_ref[...]
        nm = ADAM_B1 * m_ref[...] + (1.0 - ADAM_B1) * gv
        nv = ADAM_B2 * v_ref[...] + (1.0 - ADAM_B2) * (gv * gv)
        d_ref[...] = -ADAM_LR * ((nm / bc1) / (jnp.sqrt(nv / bc2) + ADAM_EPS) + ADAM_WD * w_ref[...])
        nm_ref[...] = nm
        nv_ref[...] = nv

    outs = pl.pallas_call(
        body, grid=(r // tm,), in_specs=[_rows(tm, c)] * 4, out_specs=[_rows(tm, c)] * 3,
        out_shape=[_sds((r, c), F32)] * 3, compiler_params=_params(("parallel",)), name="adamw")(
            w.reshape(r, c), g.reshape(r, c), m.reshape(r, c), v.reshape(r, c))
    return [o.reshape(shape) for o in outs]


WEIGHTS = ['sc_w_in', 'sc_conv_w', 'sc_w_out', 'mla_w_dq', 'mla_g_q', 'mla_w_uq', 'mla_w_dkv', 'mla_g_kv', 'mla_w_uk',
           'mla_w_uv', 'mla_w_o', 'cf_w_pw1', 'cf_b_pw1', 'cf_dw_w', 'cf_dw_b', 'cf_norm_g', 'cf_norm_b', 'cf_w_pw2',
           'cf_b_pw2', 'ff_w1', 'ff_w2', 'ln_mix_g', 'ln_mix_b', 'ln_ff_g', 'ln_ff_b']
ARGS = ['x'] + WEIGHTS + ['loss_target'] + ['m_' + n for n in WEIGHTS] + ['v_' + n for n in WEIGHTS]


def _sq_relu(h):
    r = jnp.maximum(h.astype(F32), 0.0)
    return (r * r).astype(BF16)


def _mlp_forward(i, x, xb, w1, w2, g, b):
    hb = mm_plain_nn(f"mlp{i}_up", xb, w1, BF16, tn=1024)
    y, yb, xh, rstd = mm_residual_ln(f"mlp{i}_down_ln", hb, w2, x, g, b, tk=2048, a_fn=_sq_relu)
    return (y, yb), dict(xb=xb, hb=hb, xh=xh, rstd=rstd, g=g)


def _mlp_backward(i, dy, sv, w1, w2, dw1, dw2, reduce_after):
    s = dy.shape[0]
    dr, drb, dg, db, _ = ln_backward(f"mlp{i}_ln_bwd", dy, sv["xh"], sv["rstd"], sv["g"])
    tm, tn = _tile(s, 1024), 1024

    def epi(acc, e, o):
        o[0][...] = (acc * (2.0 * jnp.maximum(e[0][...].astype(F32), 0.0))).astype(BF16)

    dhb = mm_nt(f"mlp{i}_down_bwd", drb, w2, s, tm, tn, 1024, epi, [_sds((s, w2.k), BF16)], [_ij(tm, tn)],
                [sv["hb"]], [_ij(tm, tn)])[0]
    g_w2 = mm_tn(f"mlp{i}_dw2", sv["hb"], drb, dw2, s, 512, 1024, a_fn=_sq_relu)
    g_w1 = mm_tn(f"mlp{i}_dw1", sv["xb"], dhb, dw1, s, 1024, 512)
    dhb = reduce_after(dhb, {f"w1_{i}": g_w1, f"w2_{i}": g_w2})
    dx = mm_plain_nt(f"mlp{i}_up_bwd", dhb, w1, F32, tn=1024, add=dr, add_scale=ALPHA)
    return dx, dg, db


def kernel(x, sc_w_in, sc_conv_w, sc_w_out, mla_w_dq, mla_g_q, mla_w_uq, mla_w_dkv, mla_g_kv, mla_w_uk, mla_w_uv, mla_w_o, cf_w_pw1, cf_b_pw1, cf_dw_w, cf_dw_b, cf_norm_g, cf_norm_b, cf_w_pw2, cf_b_pw2, ff_w1, ff_w2, ln_mix_g, ln_mix_b, ln_ff_g, ln_ff_b, loss_target, m_sc_w_in, m_sc_conv_w, m_sc_w_out, m_mla_w_dq, m_mla_g_q, m_mla_w_uq, m_mla_w_dkv, m_mla_g_kv, m_mla_w_uk, m_mla_w_uv, m_mla_w_o, m_cf_w_pw1, m_cf_b_pw1, m_cf_dw_w, m_cf_dw_b, m_cf_norm_g, m_cf_norm_b, m_cf_w_pw2, m_cf_b_pw2, m_ff_w1, m_ff_w2, m_ln_mix_g, m_ln_mix_b, m_ln_ff_g, m_ln_ff_b, v_sc_w_in, v_sc_conv_w, v_sc_w_out, v_mla_w_dq, v_mla_g_q, v_mla_w_uq, v_mla_w_dkv, v_mla_g_kv, v_mla_w_uk, v_mla_w_uv, v_mla_w_o, v_cf_w_pw1, v_cf_b_pw1, v_cf_dw_w, v_cf_dw_b, v_cf_norm_g, v_cf_norm_b, v_cf_w_pw2, v_cf_b_pw2, v_ff_w1, v_ff_w2, v_ln_mix_g, v_ln_mix_b, v_ln_ff_g, v_ln_ff_b):
    given = dict(zip(ARGS, (x, sc_w_in, sc_conv_w, sc_w_out, mla_w_dq, mla_g_q, mla_w_uq, mla_w_dkv, mla_g_kv, mla_w_uk, mla_w_uv, mla_w_o, cf_w_pw1, cf_b_pw1, cf_dw_w, cf_dw_b, cf_norm_g, cf_norm_b, cf_w_pw2, cf_b_pw2, ff_w1, ff_w2, ln_mix_g, ln_mix_b, ln_ff_g, ln_ff_b, loss_target, m_sc_w_in, m_sc_conv_w, m_sc_w_out, m_mla_w_dq, m_mla_g_q, m_mla_w_uq, m_mla_w_dkv, m_mla_g_kv, m_mla_w_uk, m_mla_w_uv, m_mla_w_o, m_cf_w_pw1, m_cf_b_pw1, m_cf_dw_w, m_cf_dw_b, m_cf_norm_g, m_cf_norm_b, m_cf_w_pw2, m_cf_b_pw2, m_ff_w1, m_ff_w2, m_ln_mix_g, m_ln_mix_b, m_ln_ff_g, m_ln_ff_b, v_sc_w_in, v_sc_conv_w, v_sc_w_out, v_mla_w_dq, v_mla_g_q, v_mla_w_uq, v_mla_w_dkv, v_mla_g_kv, v_mla_w_uk, v_mla_w_uv, v_mla_w_o, v_cf_w_pw1, v_cf_b_pw1, v_cf_dw_w, v_cf_dw_b, v_cf_norm_g, v_cf_norm_b, v_cf_w_pw2, v_cf_b_pw2, v_ff_w1, v_ff_w2, v_ln_mix_g, v_ln_mix_b, v_ln_ff_g, v_ln_ff_b)))
    s, d = x.shape[1], x.shape[2]
    d_ff = 4 * d
    dq4 = d // N_CHIPS
    xq = lax.axis_index("x") * 2 + lax.axis_index("y")

    w_dkv_pad = jnp.pad(mla_w_dkv[0], ((0, 0), (0, 128 - QK_ROPE)))
    w_uq_pad = jnp.pad(mla_w_uq[0].reshape(Q_LORA, 2, QK_NOPE + QK_ROPE), ((0, 0), (0, 0), (0, HEAD_PAD - QK_NOPE - QK_ROPE)))
    small = pack_rows("vector_weights_pack", [
        sc_conv_w.reshape(2 * SC_WIDTH, dq4), cf_b_pw1.reshape(2, dq4), cf_dw_w[0], cf_dw_b, cf_norm_g, cf_norm_b,
        cf_b_pw2], 64)
    mlp_w = lambda i: [ff_w1[i].astype(BF16), ff_w2[i].astype(BF16)]
    g_in, g_out, g_w1, g_w2 = [None] * 2, [None] * 2, [None] * DEPTH, [None] * DEPTH
    g_in[0], g_out[0], g_small = gather_shards(
        "gather_mixer0", [sc_w_in[0].astype(BF16), sc_w_out[0].astype(BF16), small], by_columns=(0,))
    (g_w1[0],) = gather_shards("gather_up0", [ff_w1[0].astype(BF16)], by_columns=(0,))
    (g_w2[0],) = gather_shards("gather_down0", [ff_w2[0].astype(BF16)])
    g_dqkv, g_uq, g_uk, g_uv, g_o = gather_shards("gather_mixer1", [
        jnp.concatenate([mla_w_dq[0], w_dkv_pad], axis=1).astype(BF16),
        w_uq_pad.reshape(Q_LORA, 2 * HEAD_PAD).astype(BF16),
        mla_w_uk.reshape(KV_LORA // N_CHIPS, N_HEADS * QK_NOPE).astype(BF16),
        mla_w_uv.reshape(KV_LORA // N_CHIPS, N_HEADS * V_HEAD).astype(BF16), mla_w_o[0].astype(BF16)], by_columns=(1,))
    g_w1[1], g_w2[1] = gather_shards("gather_mlp1", mlp_w(1), by_columns=(0,))
    g_pw1, g_pw2, g_w1[2], g_w2[2] = gather_shards(
        "gather_layer2", [cf_w_pw1[0].astype(BF16), cf_w_pw2[0].astype(BF16)] + mlp_w(2), by_columns=(0, 2))
    g_in[1], g_out[1], g_w1[3], g_w2[3] = gather_shards(
        "gather_layer3", [sc_w_in[1].astype(BF16), sc_w_out[1].astype(BF16)] + mlp_w(3), by_columns=(0, 2))

    wd_t = Q_LORA + KV_LORA + 128
    w_in = [Stk("full", d, 3 * d, g_in[j]) for j in range(2)]
    w_out = [Stk("row", d, d, g_out[j]) for j in range(2)]
    w_dqkv = Stk("row", d, wd_t, g_dqkv)
    w_uq = Stk("full", Q_LORA, N_HEADS * HEAD_PAD, g_uq)
    w_uk = Stk("row", KV_LORA, N_HEADS * QK_NOPE, g_uk)
    w_uv = Stk("row", KV_LORA, N_HEADS * V_HEAD, g_uv)
    w_o = Stk("row", d, d, g_o)
    w_pw1 = Stk("full", d, 2 * d, g_pw1)
    w_pw2 = Stk("row", d, d, g_pw2)
    w_1 = [Stk("full", d, d_ff, g_w1[i]) for i in range(DEPTH)]
    w_2 = [Stk("row", d_ff, d, g_w2[i]) for i in range(DEPTH)]

    def wide(rows):
        return jnp.swapaxes(rows, 0, 1).reshape(rows.shape[1], d)

    conv_w = wide(g_small[:, 0:6]).reshape(2, SC_WIDTH, d)
    b_pw1 = g_small[:, 6:8].reshape(1, 2 * d)
    dw_w = wide(g_small[:, 8:39])
    dw_b, norm_g, norm_b, b_pw2 = (wide(g_small[:, 39 + k:40 + k]) for k in range(4))

    pos = jnp.arange(s, dtype=F32)
    inv_freq = ROPE_THETA ** (-jnp.arange(0, QK_ROPE, 2, dtype=F32) / QK_ROPE)
    ang = pos[:, None] * inv_freq[None, :]
    cos, sin, zero = jnp.cos(ang), jnp.sin(ang), jnp.zeros((s, 128 - QK_ROPE), F32)
    cf = jnp.concatenate([cos, cos, zero], axis=1)
    sf = jnp.concatenate([-sin, sin, zero], axis=1)

    def row(a, i):
        return a[i:i + 1]

    xs = x.reshape(s, d)
    cur = (xs, xs.astype(BF16))
    tape = []
    for i in range(DEPTH):
        mixer, j = i % 3, i // 3
        xf, xb = cur
        lg, lb = row(ln_mix_g, i), row(ln_mix_b, i)
        if mixer == 0:
            u = mm_plain_nn(f"sc{j}_in", xb, w_in[j], F32, tn=3 * dq4)
            gb = short_conv_gate(u, conv_w[j])
            y, yb, xh, rstd = mm_residual_ln(f"sc{j}_out_ln", gb, w_out[j], xf, lg, lb)
            sv = dict(xb=xb, u=u, gb=gb)
        elif mixer == 1:
            t = mm_plain_nn("mla_down", xb, w_dqkv, F32, tn=wd_t // 2)
            cq, ckv, kpe = mla_latents(t, mla_g_q, mla_g_kv, cf, sf)
            qh = mla_queries(cq, w_uq, cf, sf)
            kh = mla_keys(ckv, w_uk, kpe)
            vh = mm_plain_nn("mla_values", ckv, w_uv, BF16, tk=KV_LORA)
            oh = attention(qh, kh, vh)
            y, yb, xh, rstd = mm_residual_ln("mla_out_ln", oh, w_o, xf, lg, lb)
            sv = dict(xb=xb, t=t, cq=cq, ckv=ckv, qh=qh, kh=kh, vh=vh, oh=oh)
        else:
            u = mm_plain_nn("cf_pw1", xb, w_pw1, F32, bias=b_pw1)
            hc = conformer_glu_conv(u, dw_w, dw_b)
            sb = conformer_norm_swish(hc, norm_g, norm_b)
            y, yb, xh, rstd = mm_residual_ln("cf_pw2_ln", sb, w_pw2, xf, lg, lb, bias=b_pw2)
            sv = dict(xb=xb, u=u, hc=hc, sb=sb)
        sv.update(xh=xh, rstd=rstd, g=lg)
        cur, sv_mlp = _mlp_forward(i, y, yb, w_1[i], w_2[i], row(ln_ff_g, i), row(ln_ff_b, i))
        tape.append((sv, sv_mlp))

    dy, loss_part = loss_head(cur[0], loss_target.reshape(s, d))

    grads = {}
    smalls = {}
    g_ln = {n: [None] * DEPTH for n in ("ln_mix_g", "ln_mix_b", "ln_ff_g", "ln_ff_b")}
    conv_grads = [None, None]
    core = lax.axis_index("c").astype(jnp.int32).reshape(1)
    chip = xq.astype(jnp.int32).reshape(1)
    pairs, landed = {}, {}
    ready, theirs = [], {}

    def reduce_after(x, new, early=False):
        out = lax.optimization_barrier((x, *new.values()))
        grads.update(zip(new, out[1:]))
        if early:
            theirs.update(zip(new, pair_exchange(f"pair_exchange_{len(theirs)}", list(out[1:]), True)))
        ready.extend(new)
        return out[0]

    def reduce_layer(i, x):
        late = [n for n in ready if n not in theirs]
        if late:
            theirs.update(zip(late, pair_exchange(f"pair_exchange_layer{i}", [grads[n] for n in late], False)))
        sums = [pair_sum(grads[n], theirs[n], core) for n in ready]
        pairs.update(zip(ready, sums))
        landed.update(zip(ready, chip_exchange(f"chip_exchange_layer{i}", sums)))
        exchanged.append(list(ready))
        ready.clear()
        return lax.optimization_barrier((x, *sums))[0]

    groups = [["in_0", "in_1"], ["out_0", "out_1"], ["dqkv"], ["uq"], ["uk"], ["uv"], ["o"], ["pw1"], ["pw2"],
              [f"w1_{i}" for i in range(DEPTH)], [f"w2_{i}" for i in range(DEPTH)]]
    stacks = [None] * len(groups)
    exchanged = []

    def sum_layer(x, last=False):
        names = exchanged.pop(0)
        if last:
            out = lax.optimization_barrier((x, *[landed[n] for n in names]))
            landed.update(zip(names, out[1:]))
        new = []
        for n in names:
            k = next(k for k, members in enumerate(groups) if n in members)
            stacks[k] = chip_sum(pairs[n], landed[n], chip, stacks[k], groups[k].index(n), len(groups[k]))
            new.append(stacks[k])
        return out[0] if last else lax.optimization_barrier((x, *new))[0]

    for i in reversed(range(DEPTH)):
        mixer, j = i % 3, i // 3
        sv, sv_mlp = tape[i]
        dy, g_ln["ln_ff_g"][i], g_ln["ln_ff_b"][i] = _mlp_backward(
            i, dy, sv_mlp, w_1[i], w_2[i], Stk("col", d, d_ff), Stk("row", d_ff, d),
            lambda x_, new: reduce_after(x_, new, early=i > 0))
        if i == 0:
            dy = reduce_layer("0_mlp", dy)
        dr, drb, g_ln["ln_mix_g"][i], g_ln["ln_mix_b"][i], dr_sum = ln_backward(
            f"mix{i}_ln_bwd", dy, sv["xh"], sv["rstd"], sv["g"])
        if mixer == 0:
            dgate = mm_plain_nt(f"sc{j}_out_bwd", drb, w_out[j], F32)
            dw_out = mm_tn(f"sc{j}_dw_out", sv["gb"], drb, Stk("row", d, d), s, 512, 1024)
            du, conv_grads[j] = short_conv_gate_bwd(sv["u"], conv_w[j], dgate)
            nb = d // 256
            dw_in = mm_tn(
                f"sc{j}_dw_in", sv["xb"], du, Stk("col", d, 3 * d), s, 1024, 256,
                b_spec=pl.BlockSpec((None, s, 256), lambda i_, j_, k_: (j_ // nb, k_, j_ % nb)))
            du = reduce_after(du, {f"in_{j}": dw_in, f"out_{j}": dw_out})
            dy = mm_plain_nt(
                f"sc{j}_in_bwd", du, w_in[j], F32, tn=1024, tk=d, add=dr, add_scale=ALPHA,
                a_spec_fn=(s, lambda tm, tk: pl.BlockSpec((None, tm, tk), lambda i_, j_, k_: (k_, i_, 0))))
        elif mixer == 1:
            do = mm_plain_nt("mla_out_bwd", drb, w_o, BF16)
            g_o = mm_tn("mla_dw_o", sv["oh"], drb, Stk("row", d, d), s, 512, 1024)
            dqh, dkh, dvh = attention_bwd(sv["qh"], sv["kh"], sv["vh"], do)
            dql, dkn, dkpe = mla_unrope_grads(dqh, dkh, cf, sf)
            g_uq = mm_tn("mla_dw_uq", sv["cq"], dql, Stk("col", Q_LORA, N_HEADS * HEAD_PAD), s, Q_LORA, 512)
            dcq = mm_plain_nt("mla_uq_bwd", dql, w_uq, F32, tn=Q_LORA)
            g_uk = mm_tn("mla_dw_uk", sv["ckv"], dkn, Stk("row", KV_LORA, N_HEADS * QK_NOPE), s, KV_LORA, 1024)
            g_uv = mm_tn("mla_dw_uv", sv["ckv"], dvh, Stk("row", KV_LORA, N_HEADS * V_HEAD), s, KV_LORA, 1024)
            dckv = mm_plain_nt("mla_uk_bwd", dkn, w_uk, F32, tn=KV_LORA)
            dckv = mm_plain_nt("mla_uv_bwd", dvh, w_uv, F32, tn=KV_LORA, add=dckv)
            dt, smalls["g_q"], smalls["g_kv"] = mla_latents_bwd(sv["t"], mla_g_q, mla_g_kv, cf, sf, dcq, dckv, dkpe)
            g_dqkv = mm_tn("mla_dw_down", sv["xb"], dt, Stk("row", d, wd_t), s, 512, wd_t)
            dt = reduce_after(dt, {"dqkv": g_dqkv, "uq": g_uq, "uk": g_uk, "uv": g_uv, "o": g_o})
            dy = mm_plain_nt("mla_down_bwd", dt, w_dqkv, F32, tk=wd_t, add=dr, add_scale=ALPHA)
        else:
            dsw = mm_plain_nt("cf_pw2_bwd", drb, w_pw2, F32)
            g_pw2 = mm_tn("cf_dw_pw2", sv["sb"], drb, Stk("row", d, d), s, 512, 1024)
            smalls["b_pw2"] = dr_sum
            dhc, smalls["norm_g"], smalls["norm_b"] = conformer_norm_swish_bwd(sv["hc"], norm_g, norm_b, dsw)
            du, smalls["b_pw1"], smalls["dw_w"], smalls["dw_b"] = conformer_glu_conv_bwd(sv["u"], dw_w, dhc)
            nb = d // 512
            g_pw1 = mm_tn(
                "cf_dw_pw1", sv["xb"], du, Stk("col", d, 2 * d), s, 1024, 512,
                b_spec=pl.BlockSpec((None, s, 512), lambda i_, j_, k_: (j_ // nb, k_, j_ % nb)))
            du = reduce_after(du, {"pw1": g_pw1, "pw2": g_pw2})
            dy = mm_plain_nt(
                "cf_pw1_bwd", du, w_pw1, F32, tn=1024, tk=d, add=dr, add_scale=ALPHA,
                a_spec_fn=(s, lambda tm, tk: pl.BlockSpec((None, tm, tk), lambda i_, j_, k_: (k_, i_, 0))))
        if i < DEPTH - 1:
            dy = sum_layer(dy)
        dy = reduce_layer(i, dy)
    dy = sum_layer(sum_layer(dy, last=True), last=True)
    grad_x = dy.reshape(1, s, d)

    mine = stacks
    other = (pair_share("pair_share_mixers", mine[:9]) + pair_share("pair_share_up", mine[9:10])
             + pair_share("pair_share_down", mine[10:]))

    def padded(get):
        dqkv = jnp.concatenate([get("mla_w_dq")[0], jnp.pad(get("mla_w_dkv")[0], ((0, 0), (0, 128 - QK_ROPE)))], axis=1)
        uq = jnp.pad(get("mla_w_uq")[0].reshape(Q_LORA, 2, QK_NOPE + QK_ROPE),
                     ((0, 0), (0, 0), (0, HEAD_PAD - QK_NOPE - QK_ROPE))).reshape(Q_LORA, 2 * HEAD_PAD)
        return [get("sc_w_in"), get("sc_w_out"), dqkv[None], uq[None],
                get("mla_w_uk").reshape(1, KV_LORA // N_CHIPS, d), get("mla_w_uv").reshape(1, KV_LORA // N_CHIPS, d),
                get("mla_w_o"), get("cf_w_pw1"), get("cf_w_pw2"), get("ff_w1"), get("ff_w2")]

    w_l, m_l, v_l = (padded(lambda n, p=p: given[p + n]) for p in ("", "m_", "v_"))
    res = [adamw_joined(w_l[k], m_l[k], v_l[k], mine[k], other[k], core) for k in range(len(groups))]

    def unpadded(k):
        r_in, r_out, r_dqkv, r_uq, r_uk, r_uv, r_o, r_pw1, r_pw2, r_w1, r_w2 = (r[k] for r in res)
        return {
            "sc_w_in": r_in, "sc_w_out": r_out, "mla_w_dq": r_dqkv[:, :, 0:Q_LORA],
            "mla_w_dkv": r_dqkv[:, :, Q_LORA:Q_LORA + KV_LORA + QK_ROPE],
            "mla_w_uq": r_uq.reshape(1, Q_LORA, 2, HEAD_PAD)[:, :, :, 0:QK_NOPE + QK_ROPE].reshape(mla_w_uq.shape),
            "mla_w_uk": r_uk.reshape(mla_w_uk.shape), "mla_w_uv": r_uv.reshape(mla_w_uv.shape),
            "mla_w_o": r_o, "cf_w_pw1": r_pw1, "cf_w_pw2": r_pw2, "ff_w1": r_w1, "ff_w2": r_w2}

    big_g, big_d, big_m, big_v = (unpadded(k) for k in range(4))

    pad_row = lambda a: jnp.pad(a, ((0, 0), (0, d - a.shape[1])))
    small_parts = ([g for n in ("ln_mix_g", "ln_mix_b", "ln_ff_g", "ln_ff_b") for g in g_ln[n]]
                   + [pad_row(smalls["g_q"]), pad_row(smalls["g_kv"]), conv_grads[0], conv_grads[1],
                      smalls["b_pw1"].reshape(2, d), smalls["dw_w"], smalls["dw_b"], smalls["norm_g"], smalls["norm_b"],
                      smalls["b_pw2"], loss_part])
    red = all_reduce_small(small_parts, 64)
    loss = red[61, 0]

    def shard(rows):
        return lax.dynamic_slice_in_dim(rows, xq * dq4, dq4, axis=1)

    gw = {
        **big_g,
        "ln_mix_g": red[0:4], "ln_mix_b": red[4:8], "ln_ff_g": red[8:12], "ln_ff_b": red[12:16],
        "mla_g_q": red[16:17, 0:Q_LORA], "mla_g_kv": red[17:18, 0:KV_LORA],
        "sc_conv_w": shard(red[18:24]).reshape(2, SC_WIDTH, dq4),
        "cf_b_pw1": lax.dynamic_slice_in_dim(red[24:26].reshape(1, 2 * d), xq * 2 * dq4, 2 * dq4, axis=1),
        "cf_dw_w": shard(red[26:57])[None], "cf_dw_b": shard(red[57:58]), "cf_norm_g": shard(red[58:59]),
        "cf_norm_b": shard(red[59:60]), "cf_b_pw2": shard(red[60:61]),
    }

    upd = {n: [big_d[n], big_m[n], big_v[n]] for n in big_g}

    def pack(names, width, get):
        return jnp.concatenate([get(n).reshape(-1, width) for n in names], axis=0)

    def unpack(names, packed):
        out, at = {}, 0
        for n in names:
            rows = given[n].size // packed.shape[1]
            out[n] = packed[at:at + rows].reshape(given[n].shape)
            at += rows
        return out

    rep = ["ln_mix_g", "ln_mix_b", "ln_ff_g", "ln_ff_b"]
    shd = ["sc_conv_w", "cf_b_pw1", "cf_dw_w", "cf_dw_b", "cf_norm_g", "cf_norm_b", "cf_b_pw2"]
    for names, width in ((rep, d), (shd, dq4), (["mla_g_q"], Q_LORA), (["mla_g_kv"], KV_LORA)):
        res = adamw(pack(names, width, lambda n: given[n]), pack(names, width, lambda n: gw[n]),
                    pack(names, width, lambda n: given["m_" + n]), pack(names, width, lambda n: given["v_" + n]), tm=4096)
        parts = [unpack(names, r) for r in res]
        for n in names:
            upd[n] = [p[n] for p in parts]

    return (loss, grad_x, *[gw[n].reshape(given[n].shape) for n in WEIGHTS], *[upd[n][0] for n in WEIGHTS],
            *[upd[n][1] for n in WEIGHTS], *[upd[n][2] for n in WEIGHTS])
```

```python
import jax
import jax.numpy as jnp
from jax import lax
from jax.experimental import pallas as pl
from jax.experimental.pallas import tpu as pltpu
from jax.experimental.pallas import tpu_sc as plsc

F32 = jnp.float32
BF16 = jnp.bfloat16
MESH = pl.DeviceIdType.MESH

DEPTH = 4
ALPHA = (2.0 * DEPTH) ** 0.25
LN_EPS = 1e-5
RMS_EPS = 1e-6
CHUNK_SHIFT = 6
N_HEADS = 8
QK_NOPE = 128
QK_ROPE = 64
V_HEAD = 128
HEAD_PAD = 256
Q_LORA = 384
KV_LORA = 256
ROPE_THETA = 10000.0
SC_WIDTH = 3
CONF_WIDTH = 31
CONV_PAD = 32
CONV_CHUNK = 64
N_CHIPS = 4
ATTN_SCALE = (QK_NOPE + QK_ROPE) ** -0.5

ADAM_LR = 0.001
ADAM_B1 = 0.9
ADAM_B2 = 0.999
ADAM_EPS = 1e-08
ADAM_WD = 0.01
ADAM_STEP = 10

VMEM_LIMIT = 56 * 2**20

NN = (((1,), (0,)), ((), ()))
NT = (((1,), (1,)), ((), ()))
TN = (((0,), (0,)), ((), ()))


def _params(sem=None):
    return pltpu.CompilerParams(dimension_semantics=sem, vmem_limit_bytes=VMEM_LIMIT)


class Stk:
    def __init__(self, kind, k, n, arr=None, layers=None, layer=None):
        self.kind, self.k, self.n, self.layers, self.layer = kind, k, n, layers, layer
        self.plain = (kind == "row" and layers is None) or kind == "full"
        self.kloc = k // N_CHIPS if kind == "row" else k
        self.nloc = n // N_CHIPS if kind == "col" else n
        if arr is not None and self.plain:
            arr = arr.reshape(k, n)
        self.arr = arr

    @property
    def shape(self):
        if self.plain:
            return (self.k, self.n)
        lead = (N_CHIPS,) if self.layers is None else (N_CHIPS, self.layers)
        return lead + (self.kloc, self.nloc)

    def spec(self, bk, bn, f):
        if self.plain:
            return pl.BlockSpec((bk, bn), f)
        assert self.kloc % bk == 0 and self.nloc % bn == 0, (self.kloc, bk, self.nloc, bn)
        pk, pn = self.kloc // bk, self.nloc // bn
        kind, layer = self.kind, self.layer

        def imap(*g):
            kb, nb = f(*g)
            if kind == "row":
                q, kb, nb = kb // pk, kb % pk, nb
            else:
                q, kb, nb = nb // pn, kb, nb % pn
            return (q, kb, nb) if layer is None else (q, layer, kb, nb)

        block = (None, bk, bn) if layer is None else (None, None, bk, bn)
        return pl.BlockSpec(block, imap)


def _mm(name, mode, a, b, grid, a_spec, b_spec, acc_shape, extras, extra_specs, out_shapes, out_specs, epi, a_fn=None):
    nk = grid[2]
    ne = len(extras)

    def body(*refs):
        a_ref, b_ref = refs[0], refs[1]
        e_refs = refs[2:2 + ne]
        av = a_ref[...] if a_fn is None else a_fn(a_ref[...])
        part = lax.dot_general(av, b_ref[...], mode, preferred_element_type=F32)
        if nk == 1:
            epi(part, e_refs, refs[2 + ne:])
            return
        o_refs = refs[2 + ne:-1]
        acc = refs[-1]
        k = pl.program_id(2)

        @pl.when(k == 0)
        def _():
            acc[...] = part

        @pl.when(k > 0)
        def _():
            acc[...] += part

        @pl.when(k == nk - 1)
        def _():
            epi(acc[...], e_refs, o_refs)

    return pl.pallas_call(
        body, grid=grid, in_specs=[a_spec, b_spec, *extra_specs], out_specs=out_specs, out_shape=out_shapes,
        scratch_shapes=[pltpu.VMEM(acc_shape, F32)] if nk > 1 else [],
        compiler_params=_params(("parallel", "parallel", "arbitrary")), name=name)(a, b, *extras)


def _tile(n, t):
    t = min(n, t)
    while n % t:
        t -= 8
    assert t > 0, (n, t)
    return t


def mm_nn(name, a, w, tm, tn, tk, epi, out_shapes, out_specs, extras=(), extra_specs=(), a_spec=None, a_fn=None):
    m = a.shape[0]
    tm, tn, tk = _tile(m, tm), _tile(w.n, tn), _tile(w.k, tk)
    grid = (m // tm, w.n // tn, w.k // tk)
    a_spec = a_spec or pl.BlockSpec((tm, tk), lambda i, j, k: (i, k))
    b_spec = w.spec(tk, tn, lambda i, j, k: (k, j))
    return _mm(name, NN, a, w.arr, grid, a_spec, b_spec, (tm, tn), extras, extra_specs, out_shapes, out_specs, epi, a_fn)


def mm_nt(name, a, w, m, tm, tn, tk, epi, out_shapes, out_specs, extras=(), extra_specs=(), a_spec=None):
    tm, tn, tk = _tile(m, tm), _tile(w.k, tn), _tile(w.n, tk)
    grid = (m // tm, w.k // tn, w.n // tk)
    a_spec = a_spec or pl.BlockSpec((tm, tk), lambda i, j, k: (i, k))
    b_spec = w.spec(tn, tk, lambda i, j, k: (j, k))
    return _mm(name, NT, a, w.arr, grid, a_spec, b_spec, (tm, tn), extras, extra_specs, out_shapes, out_specs, epi)


def mm_tn(name, a, b, dw, s, tm=512, tn=512, tk=4096, a_spec=None, b_spec=None, a_fn=None):
    tm, tn, tk = _tile(dw.k, tm), _tile(dw.n, tn), _tile(s, tk)
    grid = (dw.k // tm, dw.n // tn, s // tk)
    a_spec = a_spec or pl.BlockSpec((tk, tm), lambda i, j, k: (k, i))
    b_spec = b_spec or pl.BlockSpec((tk, tn), lambda i, j, k: (k, j))

    def epi(acc, e, o):
        o[0][...] = acc.astype(BF16)

    out = _mm(name, TN, a, b, grid, a_spec, b_spec, (tm, tn), (), (), [jax.ShapeDtypeStruct(dw.shape, BF16)],
              [dw.spec(tm, tn, lambda i, j, k: (i, j))], epi, a_fn)[0]
    return out.reshape(N_CHIPS, dw.k // N_CHIPS, dw.n) if dw.plain else out


def _sds(shape, dtype):
    return jax.ShapeDtypeStruct(shape, dtype)


def _ij(tm, tn):
    return pl.BlockSpec((tm, tn), lambda i, j, k: (i, j))


def _i0(tm, c):
    return pl.BlockSpec((tm, c), lambda i, j, k: (i, 0))


def _0j(r, tn):
    return pl.BlockSpec((r, tn), lambda i, j, k: (0, j))


def _layer_norm_rows(r, g, b):
    mu = jnp.mean(r, axis=-1, keepdims=True)
    d = r - mu
    var = jnp.mean(d * d, axis=-1, keepdims=True)
    rstd = lax.rsqrt(var + LN_EPS)
    xh = d * rstd
    return xh * g + b, xh, rstd


def mm_residual_ln(name, a, w, x, g, b, bias=None, tm=512, tk=1024, a_fn=None):
    s, d = x.shape
    tm = _tile(s, tm)
    extras = [x, g, b] + ([bias] if bias is not None else [])
    especs = [_i0(tm, d), _0j(1, d), _0j(1, d)] + ([_0j(1, d)] if bias is not None else [])

    def epi(acc, e, o):
        r = ALPHA * e[0][...] + acc
        if bias is not None:
            r = r + e[3][...]
        y, xh, rstd = _layer_norm_rows(r, e[1][...], e[2][...])
        o[0][...] = y
        o[1][...] = y.astype(BF16)
        o[2][...] = xh
        o[3][...] = rstd

    return mm_nn(name, a, w, tm, d, tk, epi,
                 [_sds((s, d), F32), _sds((s, d), BF16), _sds((s, d), F32), _sds((s, 1), F32)],
                 [_i0(tm, d), _i0(tm, d), _i0(tm, d), _i0(tm, 1)], extras, especs, a_fn=a_fn)


def mm_plain_nn(name, a, w, out_dtype, tm=1024, tn=512, tk=1024, bias=None):
    m = a.shape[0]
    tm, tn = _tile(m, tm), _tile(w.n, tn)
    if w.kind == "col":
        tn = _tile(w.nloc, tn)

    def epi(acc, e, o):
        if bias is not None:
            acc = acc + e[0][...]
        o[0][...] = acc.astype(out_dtype)

    extras, especs = ([bias], [_0j(1, tn)]) if bias is not None else ((), ())
    return mm_nn(name, a, w, tm, tn, tk, epi, [_sds((m, w.n), out_dtype)], [_ij(tm, tn)], extras, especs)[0]


def mm_plain_nt(name, a, w, out_dtype, tm=1024, tn=512, tk=1024, add=None, add_scale=1.0, a_spec_fn=None):
    m = a.shape[0] if a_spec_fn is None else a_spec_fn[0]
    tm, tn = _tile(m, tm), _tile(w.k, tn)
    tk = _tile(w.n, tk)
    if w.kind == "col":
        tk = _tile(w.nloc, tk)
    if w.kind == "row" and not w.plain:
        tn = _tile(w.kloc, tn)

    def epi(acc, e, o):
        if add is not None:
            acc = acc + add_scale * e[0][...].astype(F32)
        o[0][...] = acc.astype(out_dtype)

    extras, especs = ([add], [_ij(tm, tn)]) if add is not None else ((), ())
    a_spec = None if a_spec_fn is None else a_spec_fn[1](tm, tk)
    return mm_nt(name, a, w, m, tm, tn, tk, epi, [_sds((m, w.k), out_dtype)], [_ij(tm, tn)], extras, especs,
                 a_spec=a_spec)[0]


def _rows(tm, c):
    return pl.BlockSpec((tm, c), lambda i: (i, 0))


def _fix(shape):
    nd = len(shape)
    return pl.BlockSpec(shape, lambda i: (0,) * nd)


def _accumulate(ref, val):
    @pl.when(pl.program_id(0) == 0)
    def _():
        ref[...] = jnp.zeros_like(ref)

    ref[...] += val


def ln_backward(name, dy, xhat, rstd, g, tm=512):
    s, d = dy.shape
    tm = _tile(s, tm)

    def body(dy_ref, xh_ref, rstd_ref, g_ref, dr_ref, drb_ref, dg_ref, db_ref, ds_ref):
        dyv, xh = dy_ref[...], xh_ref[...]
        dxh = dyv * g_ref[...]
        m1 = jnp.mean(dxh, axis=-1, keepdims=True)
        m2 = jnp.mean(dxh * xh, axis=-1, keepdims=True)
        dr = rstd_ref[...] * (dxh - m1 - xh * m2)
        dr_ref[...] = dr
        drb_ref[...] = dr.astype(BF16)
        _accumulate(dg_ref, jnp.sum(dyv * xh, axis=0, keepdims=True))
        _accumulate(db_ref, jnp.sum(dyv, axis=0, keepdims=True))
        _accumulate(ds_ref, jnp.sum(dr, axis=0, keepdims=True))

    return pl.pallas_call(
        body, grid=(s // tm,),
        in_specs=[_rows(tm, d), _rows(tm, d), _rows(tm, 1), _fix((1, d))],
        out_specs=[_rows(tm, d), _rows(tm, d), _fix((1, d)), _fix((1, d)), _fix((1, d))],
        out_shape=[_sds((s, d), F32), _sds((s, d), BF16), _sds((1, d), F32), _sds((1, d), F32), _sds((1, d), F32)],
        compiler_params=_params(("arbitrary",)), name=name)(dy, xhat, rstd, g)


def loss_head(y, target, tm=256):
    s, d = y.shape
    tm = _tile(s, tm)

    def body(y_ref, t_ref, dy_ref, loss_ref):
        e = y_ref[...] - t_ref[...]
        dy_ref[...] = e * (1.0 / d)
        part = 0.5 * jnp.sum(jnp.mean(e * e, axis=-1, keepdims=True), axis=0, keepdims=True)
        _accumulate(loss_ref, jnp.broadcast_to(part, (1, d)))

    return pl.pallas_call(
        body, grid=(s // tm,), in_specs=[_rows(tm, d), _rows(tm, d)],
        out_specs=[_rows(tm, d), _fix((1, d))], out_shape=[_sds((s, d), F32), _sds((1, d), F32)],
        compiler_params=_params(("arbitrary",)), name="loss_head")(y, target)


def _cols(s, tc, off=0):
    return pl.BlockSpec((s, tc), lambda i: (0, i + off))


def _shift_down(z, sft, rows):
    return jnp.where(rows >= sft, pltpu.roll(z, sft, 0), 0.0)


def _shift_up(z, sft, rows, s):
    return jnp.where(rows < s - sft, pltpu.roll(z, (s - sft) % s, 0), 0.0)


def short_conv_gate(u, conv_w, tc=256):
    s, d3 = u.shape
    d = d3 // 3
    nb = d // tc

    def body(b_ref, c_ref, h_ref, w_ref, o_ref):
        rows = lax.broadcasted_iota(jnp.int32, (s, tc), 0)
        z = c_ref[...] * h_ref[...]
        cz = jnp.zeros((s, tc), F32)
        for k in range(SC_WIDTH):
            sft = SC_WIDTH - 1 - k
            cz = cz + w_ref[pl.ds(k, 1), :] * (_shift_down(z, sft, rows) if sft else z)
        o_ref[...] = (b_ref[...] * cz).astype(BF16)

    return pl.pallas_call(
        body, grid=(nb,),
        in_specs=[_cols(s, tc), _cols(s, tc, nb), _cols(s, tc, 2 * nb), _cols(SC_WIDTH, tc)],
        out_specs=_cols(s, tc), out_shape=_sds((s, d), BF16),
        compiler_params=_params(("parallel",)), name="short_conv_gate")(u, u, u, conv_w)


def short_conv_gate_bwd(u, conv_w, dg, tc=256):
    s, d3 = u.shape
    d = d3 // 3
    nb = d // tc

    def body(b_ref, c_ref, h_ref, w_ref, dg_ref, du_ref, dw_ref):
        rows = lax.broadcasted_iota(jnp.int32, (s, tc), 0)
        c, h, dgv = c_ref[...], h_ref[...], dg_ref[...]
        z = c * h
        dcz = dgv * b_ref[...]
        cz = jnp.zeros((s, tc), F32)
        dz = jnp.zeros((s, tc), F32)
        for k in range(SC_WIDTH):
            sft = SC_WIDTH - 1 - k
            zs = _shift_down(z, sft, rows) if sft else z
            wk = w_ref[pl.ds(k, 1), :]
            cz = cz + wk * zs
            dz = dz + wk * (_shift_up(dcz, sft, rows, s) if sft else dcz)
            dw_ref[pl.ds(k, 1), :] = jnp.sum(dcz * zs, axis=0, keepdims=True)
        du_ref[0] = (dgv * cz).astype(BF16)
        du_ref[1] = (dz * h).astype(BF16)
        du_ref[2] = (dz * c).astype(BF16)

    return pl.pallas_call(
        body, grid=(nb,),
        in_specs=[_cols(s, tc), _cols(s, tc, nb), _cols(s, tc, 2 * nb), _cols(SC_WIDTH, tc), _cols(s, tc)],
        out_specs=[pl.BlockSpec((3, s, tc), lambda i: (0, 0, i)), _cols(SC_WIDTH, tc)],
        out_shape=[_sds((3, s, d), BF16), _sds((SC_WIDTH, d), F32)],
        compiler_params=_params(("parallel",)), name="short_conv_gate_bwd")(u, u, u, conv_w, dg)


def _store_shifted_down(ref, z, rows):
    s, tc = z.shape
    for b in range(8):
        ref[b, pl.ds(0, CONV_PAD), :] = jnp.zeros((CONV_PAD, tc), F32)
        ref[b, pl.ds(CONV_PAD, s), :] = z if b == 0 else _shift_down(z, b, rows)


def _store_shifted_up(ref, z, rows):
    s, tc = z.shape
    for b in range(8):
        ref[b, pl.ds(0, s), :] = z if b == 0 else _shift_up(z, b, rows, s)
        ref[b, pl.ds(s, CONV_PAD), :] = jnp.zeros((CONV_PAD, tc), F32)


def conformer_glu_conv(u, dw_w, dw_b, tc=128):
    s, d2 = u.shape
    d = d2 // 2
    nb = d // tc

    ch = min(CONV_CHUNK, s)

    def body(a_ref, g_ref, w_ref, b_ref, o_ref, down):
        rows = lax.broadcasted_iota(jnp.int32, (s, tc), 0)
        _store_shifted_down(down, a_ref[...] * jax.nn.sigmoid(g_ref[...]), rows)

        def chunk(ci, carry):
            r0 = pl.multiple_of(ci * ch, ch)
            acc = jnp.broadcast_to(b_ref[...], (ch, tc))
            for k in range(CONF_WIDTH):
                sft = CONF_WIDTH - 1 - k
                acc = acc + w_ref[pl.ds(k, 1), :] * down[sft % 8, pl.ds(CONV_PAD + r0 - (sft // 8) * 8, ch), :]
            o_ref[pl.ds(r0, ch), :] = acc
            return carry

        lax.fori_loop(0, s // ch, chunk, 0)

    return pl.pallas_call(
        body, grid=(nb,),
        in_specs=[_cols(s, tc), _cols(s, tc, nb), _cols(CONF_WIDTH, tc), _cols(1, tc)],
        out_specs=_cols(s, tc), out_shape=_sds((s, d), F32),
        scratch_shapes=[pltpu.VMEM((8, CONV_PAD + s, tc), F32)],
        compiler_params=_params(("parallel",)), name="conformer_glu_conv")(u, u, dw_w, dw_b)


def conformer_glu_conv_bwd(u, dw_w, dhc, tc=128):
    s, d2 = u.shape
    d = d2 // 2
    nb = d // tc
    ch = min(CONV_CHUNK, s)

    def body(a_ref, g_ref, w_ref, dhc_ref, du_ref, dbias_ref, dw_ref, db_ref, down, up, dw_acc, dh_buf):
        rows = lax.broadcasted_iota(jnp.int32, (s, tc), 0)
        a = a_ref[...]
        sg = jax.nn.sigmoid(g_ref[...])
        dhcv = dhc_ref[...]
        _store_shifted_down(down, a * sg, rows)
        _store_shifted_up(up, dhcv, rows)
        dw_acc[...] = jnp.zeros_like(dw_acc)

        def chunk(ci, carry):
            r0 = pl.multiple_of(ci * ch, ch)
            dc = dhc_ref[pl.ds(r0, ch), :]
            dh = jnp.zeros((ch, tc), F32)
            for k in range(CONF_WIDTH):
                sft = CONF_WIDTH - 1 - k
                a8, b = (sft // 8) * 8, sft % 8
                dh = dh + w_ref[pl.ds(k, 1), :] * up[b, pl.ds(r0 + a8, ch), :]
                prod = dc * down[b, pl.ds(CONV_PAD + r0 - a8, ch), :]
                dw_acc[k] += jnp.sum(prod.reshape(ch // 8, 8, tc), axis=0)
            dh_buf[pl.ds(r0, ch), :] = dh
            return carry

        lax.fori_loop(0, s // ch, chunk, 0)
        dh = dh_buf[...]
        da = dh * sg
        dgate = dh * a * sg * (1.0 - sg)
        du_ref[0] = da.astype(BF16)
        du_ref[1] = dgate.astype(BF16)
        dbias_ref[pl.ds(0, 1), :] = jnp.sum(da, axis=0, keepdims=True)
        dbias_ref[pl.ds(1, 1), :] = jnp.sum(dgate, axis=0, keepdims=True)
        db_ref[...] = jnp.sum(dhcv, axis=0, keepdims=True)
        for k in range(CONF_WIDTH):
            dw_ref[pl.ds(k, 1), :] = jnp.sum(dw_acc[k], axis=0, keepdims=True)

    return pl.pallas_call(
        body, grid=(nb,),
        in_specs=[_cols(s, tc), _cols(s, tc, nb), _cols(CONF_WIDTH, tc), _cols(s, tc)],
        out_specs=[pl.BlockSpec((2, s, tc), lambda i: (0, 0, i)), _cols(2, tc), _cols(CONF_WIDTH, tc), _cols(1, tc)],
        out_shape=[_sds((2, s, d), BF16), _sds((2, d), F32), _sds((CONF_WIDTH, d), F32), _sds((1, d), F32)],
        scratch_shapes=[pltpu.VMEM((8, CONV_PAD + s, tc), F32), pltpu.VMEM((8, CONV_PAD + s, tc), F32),
                        pltpu.VMEM((CONF_WIDTH + 1, 8, tc), F32), pltpu.VMEM((s, tc), F32)],
        compiler_params=_params(("parallel",)), name="conformer_glu_conv_bwd")(u, u, dw_w, dhc)


def conformer_norm_swish(hc, g, b, tm=256):
    s, d = hc.shape
    tm = _tile(s, tm)

    def body(h_ref, g_ref, b_ref, o_ref):
        n, _, _ = _layer_norm_rows(h_ref[...], g_ref[...], b_ref[...])
        o_ref[...] = (n * jax.nn.sigmoid(n)).astype(BF16)

    return pl.pallas_call(
        body, grid=(s // tm,), in_specs=[_rows(tm, d), _fix((1, d)), _fix((1, d))], out_specs=_rows(tm, d),
        out_shape=_sds((s, d), BF16), compiler_params=_params(("parallel",)), name="conformer_norm_swish")(hc, g, b)


def conformer_norm_swish_bwd(hc, g, b, ds, tm=256):
    s, d = hc.shape
    tm = _tile(s, tm)

    def body(h_ref, g_ref, b_ref, ds_ref, dh_ref, dg_ref, db_ref):
        n, nh, rstd = _layer_norm_rows(h_ref[...], g_ref[...], b_ref[...])
        sg = jax.nn.sigmoid(n)
        dn = ds_ref[...] * (sg * (1.0 + n * (1.0 - sg)))
        dnh = dn * g_ref[...]
        m1 = jnp.mean(dnh, axis=-1, keepdims=True)
        m2 = jnp.mean(dnh * nh, axis=-1, keepdims=True)
        dh_ref[...] = rstd * (dnh - m1 - nh * m2)
        _accumulate(dg_ref, jnp.sum(dn * nh, axis=0, keepdims=True))
        _accumulate(db_ref, jnp.sum(dn, axis=0, keepdims=True))

    return pl.pallas_call(
        body, grid=(s // tm,), in_specs=[_rows(tm, d), _fix((1, d)), _fix((1, d)), _rows(tm, d)],
        out_specs=[_rows(tm, d), _fix((1, d)), _fix((1, d))],
        out_shape=[_sds((s, d), F32), _sds((1, d), F32), _sds((1, d), F32)],
        compiler_params=_params(("arbitrary",)), name="conformer_norm_swish_bwd")(hc, g, b, ds)


def _swap_halves(x):
    lane = lax.broadcasted_iota(jnp.int32, x.shape, 1)
    return jnp.where(lane < QK_ROPE // 2, pltpu.roll(x, 128 - QK_ROPE // 2, 1), pltpu.roll(x, QK_ROPE // 2, 1))


def _rope(x, cf, sf):
    return x * cf + _swap_halves(x) * sf


def _unrope(dx, cf, sf):
    return dx * cf - _swap_halves(dx) * sf


def _rms_rows(x, g):
    r = lax.rsqrt(jnp.mean(x * x, axis=-1, keepdims=True) + RMS_EPS)
    return x * r, r


def mla_latents(t, g_q, g_kv, cf, sf, tm=256):
    s = t.shape[0]
    tm = _tile(s, tm)

    def body(t_ref, gq_ref, gkv_ref, cf_ref, sf_ref, cq_ref, ckv_ref, kpe_ref):
        xq, _ = _rms_rows(t_ref[:, 0:Q_LORA], gq_ref[...])
        cq_ref[...] = (xq * gq_ref[...]).astype(BF16)
        xkv, _ = _rms_rows(t_ref[:, Q_LORA:Q_LORA + KV_LORA], gkv_ref[...])
        ckv_ref[...] = (xkv * gkv_ref[...]).astype(BF16)
        kpe_ref[...] = _rope(t_ref[:, Q_LORA + KV_LORA:], cf_ref[...], sf_ref[...]).astype(BF16)

    w = Q_LORA + KV_LORA + 128
    return pl.pallas_call(
        body, grid=(s // tm,),
        in_specs=[_rows(tm, w), _fix((1, Q_LORA)), _fix((1, KV_LORA)), _rows(tm, 128), _rows(tm, 128)],
        out_specs=[_rows(tm, Q_LORA), _rows(tm, KV_LORA), _rows(tm, 128)],
        out_shape=[_sds((s, Q_LORA), BF16), _sds((s, KV_LORA), BF16), _sds((s, 128), BF16)],
        compiler_params=_params(("parallel",)), name="mla_latents")(t, g_q, g_kv, cf, sf)


def mla_latents_bwd(t, g_q, g_kv, cf, sf, dcq, dckv, dkpe, tm=256):
    s = t.shape[0]
    tm = _tile(s, tm)
    w = Q_LORA + KV_LORA + 128

    def rms_bwd(x, g, dy):
        xh, r = _rms_rows(x, g)
        dxh = dy * g
        return r * (dxh - xh * jnp.mean(dxh * xh, axis=-1, keepdims=True)), jnp.sum(dy * xh, axis=0, keepdims=True)

    def body(t_ref, gq_ref, gkv_ref, cf_ref, sf_ref, dcq_ref, dckv_ref, dkpe_ref, dt_ref, dgq_ref, dgkv_ref):
        dxq, dgq = rms_bwd(t_ref[:, 0:Q_LORA], gq_ref[...], dcq_ref[...])
        dxkv, dgkv = rms_bwd(t_ref[:, Q_LORA:Q_LORA + KV_LORA], gkv_ref[...], dckv_ref[...])
        dt_ref[:, 0:Q_LORA] = dxq.astype(BF16)
        dt_ref[:, Q_LORA:Q_LORA + KV_LORA] = dxkv.astype(BF16)
        dt_ref[:, Q_LORA + KV_LORA:] = _unrope(dkpe_ref[...], cf_ref[...], sf_ref[...]).astype(BF16)
        _accumulate(dgq_ref, dgq)
        _accumulate(dgkv_ref, dgkv)

    return pl.pallas_call(
        body, grid=(s // tm,),
        in_specs=[_rows(tm, w), _fix((1, Q_LORA)), _fix((1, KV_LORA)), _rows(tm, 128), _rows(tm, 128),
                  _rows(tm, Q_LORA), _rows(tm, KV_LORA), _rows(tm, 128)],
        out_specs=[_rows(tm, w), _fix((1, Q_LORA)), _fix((1, KV_LORA))],
        out_shape=[_sds((s, w), BF16), _sds((1, Q_LORA), F32), _sds((1, KV_LORA), F32)],
        compiler_params=_params(("arbitrary",)), name="mla_latents_bwd")(t, g_q, g_kv, cf, sf, dcq, dckv, dkpe)


def mla_queries(cq, w_uq, cf, sf, tm=512):
    s = cq.shape[0]
    tm = _tile(s, tm)

    def epi(acc, e, o):
        o[0][:, 0:QK_NOPE] = acc[:, 0:QK_NOPE].astype(BF16)
        o[0][:, QK_NOPE:] = _rope(acc[:, QK_NOPE:], e[0][...], e[1][...]).astype(BF16)

    return mm_nn("mla_queries", cq, w_uq, tm, HEAD_PAD, Q_LORA, epi, [_sds((s, N_HEADS * HEAD_PAD), BF16)],
                 [_ij(tm, HEAD_PAD)], [cf, sf], [_i0(tm, 128), _i0(tm, 128)])[0]


def mla_keys(ckv, w_uk, kpe, tm=512):
    s = ckv.shape[0]
    tm = _tile(s, tm)

    def epi(acc, e, o):
        o[0][:, 0:QK_NOPE] = acc.astype(BF16)
        o[0][:, QK_NOPE:] = e[0][...]

    return mm_nn("mla_keys", ckv, w_uk, tm, QK_NOPE, KV_LORA, epi, [_sds((s, N_HEADS * HEAD_PAD), BF16)],
                 [_ij(tm, HEAD_PAD)], [kpe], [_i0(tm, 128)])[0]


def _masked_scores(q, k, qi, tq, kv):
    sc = lax.dot_general(q, k, NT, preferred_element_type=F32) * ATTN_SCALE
    row = lax.broadcasted_iota(jnp.int32, (tq, kv), 0) + qi * tq
    col = lax.broadcasted_iota(jnp.int32, (tq, kv), 1)
    ok = lax.shift_right_logical(col, CHUNK_SHIFT) <= lax.shift_right_logical(row, CHUNK_SHIFT)
    return jnp.where(ok, sc, -1e30)


def attention(q, k, v, tq=512):
    s = q.shape[0]
    tq = _tile(s, tq)
    nq = s // tq

    def body(q_ref, k_ref, v_ref, o_ref):
        for qi in range(nq):
            kv = (qi + 1) * tq
            sc = _masked_scores(q_ref[pl.ds(qi * tq, tq), :], k_ref[pl.ds(0, kv), :], qi, tq, kv)
            p = jnp.exp(sc - jnp.max(sc, axis=-1, keepdims=True))
            o = lax.dot_general(p.astype(BF16), v_ref[pl.ds(0, kv), :], NN, preferred_element_type=F32)
            o_ref[pl.ds(qi * tq, tq), :] = (o / jnp.sum(p, axis=-1, keepdims=True)).astype(BF16)

    hq = pl.BlockSpec((s, HEAD_PAD), lambda h: (0, h))
    hv = pl.BlockSpec((s, V_HEAD), lambda h: (0, h))
    return pl.pallas_call(
        body, grid=(N_HEADS,), in_specs=[hq, hq, hv], out_specs=hv, out_shape=_sds((s, N_HEADS * V_HEAD), BF16),
        compiler_params=_params(("parallel",)), name="attention")(q, k, v)


def attention_bwd(q, k, v, do, tq=512):
    s = q.shape[0]
    tq = _tile(s, tq)
    nq = s // tq

    def body(q_ref, k_ref, v_ref, do_ref, dq_ref, dk_ref, dv_ref, dk_acc, dv_acc):
        dk_acc[...] = jnp.zeros_like(dk_acc)
        dv_acc[...] = jnp.zeros_like(dv_acc)
        for qi in range(nq):
            kv = (qi + 1) * tq
            qt = q_ref[pl.ds(qi * tq, tq), :]
            kt = k_ref[pl.ds(0, kv), :]
            dot = do_ref[pl.ds(qi * tq, tq), :]
            sc = _masked_scores(qt, kt, qi, tq, kv)
            p = jnp.exp(sc - jnp.max(sc, axis=-1, keepdims=True))
            p = p / jnp.sum(p, axis=-1, keepdims=True)
            dp = lax.dot_general(dot, v_ref[pl.ds(0, kv), :], NT, preferred_element_type=F32)
            delta = jnp.sum(p * dp, axis=-1, keepdims=True)
            ds = (p * (dp - delta) * ATTN_SCALE).astype(BF16)
            dq_ref[pl.ds(qi * tq, tq), :] = lax.dot_general(ds, kt, NN, preferred_element_type=F32).astype(BF16)
            dk_acc[pl.ds(0, kv), :] += lax.dot_general(ds, qt, TN, preferred_element_type=F32)
            dv_acc[pl.ds(0, kv), :] += lax.dot_general(p.astype(BF16), dot, TN, preferred_element_type=F32)
        dk_ref[...] = dk_acc[...].astype(BF16)
        dv_ref[...] = dv_acc[...].astype(BF16)

    hq = pl.BlockSpec((s, HEAD_PAD), lambda h: (0, h))
    hv = pl.BlockSpec((s, V_HEAD), lambda h: (0, h))
    return pl.pallas_call(
        body, grid=(N_HEADS,), in_specs=[hq, hq, hv, hv], out_specs=[hq, hq, hv],
        out_shape=[_sds((s, N_HEADS * HEAD_PAD), BF16), _sds((s, N_HEADS * HEAD_PAD), BF16),
                   _sds((s, N_HEADS * V_HEAD), BF16)],
        scratch_shapes=[pltpu.VMEM((s, HEAD_PAD), F32), pltpu.VMEM((s, V_HEAD), F32)],
        compiler_params=_params(("parallel",)), name="attention_bwd")(q, k, v, do)


def mla_unrope_grads(dq, dk, cf, sf, tm=256):
    s = dq.shape[0]
    tm = _tile(s, tm)

    def body(dq_ref, dk_ref, cf_ref, sf_ref, dql_ref, dkn_ref, dkpe_ref):
        cfv, sfv = cf_ref[...], sf_ref[...]
        dkpe = jnp.zeros((tm, 128), F32)
        for h in range(N_HEADS):
            lo = h * HEAD_PAD
            dql_ref[:, lo:lo + QK_NOPE] = dq_ref[:, lo:lo + QK_NOPE]
            dql_ref[:, lo + QK_NOPE:lo + HEAD_PAD] = _unrope(
                dq_ref[:, lo + QK_NOPE:lo + HEAD_PAD].astype(F32), cfv, sfv).astype(BF16)
            dkn_ref[:, h * QK_NOPE:(h + 1) * QK_NOPE] = dk_ref[:, lo:lo + QK_NOPE]
            dkpe = dkpe + dk_ref[:, lo + QK_NOPE:lo + HEAD_PAD].astype(F32)
        dkpe_ref[...] = dkpe

    wq = N_HEADS * HEAD_PAD
    return pl.pallas_call(
        body, grid=(s // tm,), in_specs=[_rows(tm, wq), _rows(tm, wq), _rows(tm, 128), _rows(tm, 128)],
        out_specs=[_rows(tm, wq), _rows(tm, N_HEADS * QK_NOPE), _rows(tm, 128)],
        out_shape=[_sds((s, wq), BF16), _sds((s, N_HEADS * QK_NOPE), BF16), _sds((s, 128), F32)],
        compiler_params=_params(("parallel",)), name="mla_unrope_grads")(dq, dk, cf, sf)


ANY = pl.BlockSpec(memory_space=pl.ANY)
GATHER_ID = 1
CHIP_EXCHANGE_ID = 2
PAIR_ID = 3
ALL_ID = 4


def _nbytes(a):
    return a.size * a.dtype.itemsize


def _copy_cost(operand_bytes, sent_fraction):
    sent = int(operand_bytes * sent_fraction)
    return pl.CostEstimate(flops=0, transcendentals=0, bytes_accessed=2 * sent, remote_bytes_transferred=sent)


def _handshake(peers):
    barrier = pltpu.get_barrier_semaphore()
    for peer in peers:
        pl.semaphore_signal(barrier, inc=1, device_id=peer, device_id_type=MESH)
    pl.semaphore_wait(barrier, len(peers))


def _place():
    x, y, c = lax.axis_index("x"), lax.axis_index("y"), lax.axis_index("c")
    chips = [(1 - x, y), (x, 1 - y), (1 - x, 1 - y)]
    return x, y, c, chips


def _half(ref, hc, axis=0):
    n = ref.shape[axis] // 2
    idx = (slice(None),) * axis + (pl.ds(hc * n, n),)
    return ref.at[idx]


def gather_shards(name, tensors, by_columns=()):
    nt = len(tensors)

    def body(*refs):
        a, g = refs[:nt], refs[nt:2 * nt]
        send, recv = refs[2 * nt:]
        x, y, c, _ = _place()
        q = 2 * x + y
        sib, xn, yn = (x, y, 1 - c), (1 - x, y, c), (x, 1 - y, c)
        q_xn, q_yn, q_diag = 2 * (1 - x) + y, 2 * x + 1 - y, 2 * (1 - x) + 1 - y
        _handshake([sib, xn, yn])

        def whole(t, p):
            if t in by_columns:
                n = a[t].shape[1]
                return g[t].at[:, pl.ds(p * n, n)]
            return g[t].at[p]

        def part(t, p, hc, quarter=None):
            rows = a[t].shape[0]
            if quarter is None:
                return whole(t, p).at[pl.ds(hc * (rows // 2), rows // 2)]
            return whole(t, p).at[pl.ds(hc * (rows // 2) + quarter * (rows // 4), rows // 4)]

        def rc(t, k, src, dst, to):
            return pltpu.make_async_remote_copy(src_ref=src, dst_ref=dst, send_sem=send.at[t, k], recv_sem=recv.at[t, k],
                                                device_id=to, device_id_type=MESH)

        sent = []

        def go(cp):
            cp.start()
            sent.append(cp)

        def landed(t, k, piece, frm):
            rc(t, k, piece, piece, frm).wait_recv()
            return piece

        for t in range(nt):
            go(rc(t, 8, a[t], whole(t, q), sib))
            mine = _half(a[t], c)
            go(rc(t, 0, mine, part(t, q, c), xn))
            go(rc(t, 1, mine, part(t, q, c), yn))
        for t in range(nt):
            from_y = landed(t, 1, part(t, q_yn, c), yn)
            go(rc(t, 2, part(t, q_yn, c, 0), part(t, q_yn, c, 0), xn))
            go(rc(t, 5, from_y, from_y, sib))
            from_x = landed(t, 0, part(t, q_xn, c), xn)
            go(rc(t, 3, part(t, q_xn, c, 1), part(t, q_xn, c, 1), yn))
            go(rc(t, 4, from_x, from_x, sib))
        for t in range(nt):
            for k, frm in ((2, xn), (3, yn)):
                piece = landed(t, k, part(t, q_diag, c, k - 2), frm)
                go(rc(t, 4 + k, piece, piece, sib))
        for t in range(nt):
            landed(t, 4, part(t, q_xn, 1 - c), sib)
            landed(t, 5, part(t, q_yn, 1 - c), sib)
            landed(t, 6, part(t, q_diag, 1 - c, 0), sib)
            landed(t, 7, part(t, q_diag, 1 - c, 1), sib)
            landed(t, 8, whole(t, q), sib)
        for cp in sent:
            cp.wait_send()

    return pl.kernel(
        body, name=name,
        out_type=[_sds((a.shape[0], N_CHIPS * a.shape[1]) if t in by_columns else (N_CHIPS,) + a.shape, a.dtype)
                  for t, a in enumerate(tensors)],
        mesh=plsc.ScalarSubcoreMesh(axis_name="sequencer", num_cores=1),
        scratch_types=[pltpu.SemaphoreType.DMA((nt, 9)), pltpu.SemaphoreType.DMA((nt, 9))],
        cost_estimate=_copy_cost(sum(_nbytes(a) for a in tensors), 4),
        compiler_params=pltpu.CompilerParams(collective_id=GATHER_ID))(*tensors)


def pair_exchange(name, grads, on_sequencer):
    nt = len(grads)

    def body(*refs):
        g, theirs = refs[:nt], refs[nt:2 * nt]
        send, recv = refs[2 * nt:]
        x, y, c, _ = _place()
        if on_sequencer:
            _handshake([(x, y, 1 - c)])
        cps = []
        for t in range(nt):
            cp = pltpu.make_async_remote_copy(src_ref=_half(g[t], 1 - c, 1), dst_ref=theirs[t], send_sem=send.at[t],
                                              recv_sem=recv.at[t], device_id=(x, y, 1 - c), device_id_type=MESH)
            cp.start()
            cps.append(cp)
        for cp in cps:
            cp.wait()

    if not on_sequencer:
        return pl.pallas_call(
            body, in_specs=[ANY] * nt, out_specs=[ANY] * nt,
            out_shape=[_sds((N_CHIPS, a.shape[1] // 2, a.shape[2]), a.dtype) for a in grads],
            scratch_shapes=[pltpu.SemaphoreType.DMA((nt,)), pltpu.SemaphoreType.DMA((nt,))],
            name=name)(*grads)
    return pl.kernel(
        body, name=name, out_type=[_sds((N_CHIPS, a.shape[1] // 2, a.shape[2]), a.dtype) for a in grads],
        mesh=plsc.ScalarSubcoreMesh(axis_name="sequencer", num_cores=1),
        scratch_types=[pltpu.SemaphoreType.DMA((nt,)), pltpu.SemaphoreType.DMA((nt,))],
        cost_estimate=_copy_cost(sum(_nbytes(a) for a in grads), 0.5),
        compiler_params=pltpu.CompilerParams(collective_id=PAIR_ID))(*grads)


def chip_exchange(name, parts):
    nt = len(parts)

    def body(*refs):
        a, r = refs[:nt], refs[nt:2 * nt]
        send, recv = refs[2 * nt:]
        x, y, c, chips = _place()
        _handshake([(*chip, c) for chip in chips])
        cps = []
        for t in range(nt):
            for j, chip in enumerate(chips):
                cp = pltpu.make_async_remote_copy(
                    src_ref=a[t].at[2 * chip[0] + chip[1]], dst_ref=r[t].at[j], send_sem=send.at[t, j],
                    recv_sem=recv.at[t, j], device_id=(*chip, c), device_id_type=MESH)
                cp.start()
                cps.append(cp)
        for cp in cps:
            cp.wait()

    return pl.kernel(
        body, name=name, out_type=[_sds((N_CHIPS - 1,) + a.shape[1:], a.dtype) for a in parts],
        mesh=plsc.ScalarSubcoreMesh(axis_name="sequencer", num_cores=1),
        scratch_types=[pltpu.SemaphoreType.DMA((nt, 3)), pltpu.SemaphoreType.DMA((nt, 3))],
        cost_estimate=_copy_cost(sum(_nbytes(a) for a in parts), 0.75),
        compiler_params=pltpu.CompilerParams(collective_id=CHIP_EXCHANGE_ID))(*parts)


def pair_share(name, halves):
    nt = len(halves)

    def body(*refs):
        h, other = refs[:nt], refs[nt:2 * nt]
        send, recv = refs[2 * nt:]
        x, y, c, _ = _place()
        _handshake([(x, y, 1 - c)])
        cps = []
        for t in range(nt):
            cp = pltpu.make_async_remote_copy(src_ref=h[t], dst_ref=other[t], send_sem=send.at[t], recv_sem=recv.at[t],
                                              device_id=(x, y, 1 - c), device_id_type=MESH)
            cp.start()
            cps.append(cp)
        for cp in cps:
            cp.wait()

    return pl.kernel(
        body, name=name, out_type=[_sds(a.shape, a.dtype) for a in halves],
        mesh=plsc.ScalarSubcoreMesh(axis_name="sequencer", num_cores=1),
        scratch_types=[pltpu.SemaphoreType.DMA((nt,)), pltpu.SemaphoreType.DMA((nt,))],
        cost_estimate=_copy_cost(sum(_nbytes(a) for a in halves), 1),
        compiler_params=pltpu.CompilerParams(collective_id=PAIR_ID))(*halves)


def pack_rows(name, parts, rows):
    cdim = parts[0].shape[1]
    n = len(parts)
    vm = pl.BlockSpec(memory_space=pltpu.VMEM)

    def pack(*refs):
        p, o_ref = refs[:n], refs[n]
        at = 0
        for ref in p:
            o_ref[pl.ds(at, ref.shape[0]), :] = ref[...]
            at += ref.shape[0]
        o_ref[pl.ds(at, rows - at), :] = jnp.zeros((rows - at, cdim), F32)

    return pl.pallas_call(pack, in_specs=[vm] * n, out_specs=vm, out_shape=_sds((rows, cdim), F32), name=name)(*parts)


def all_reduce_small(parts, rows):
    cdim = parts[0].shape[1]
    vm = pl.BlockSpec(memory_space=pltpu.VMEM)
    mine = pack_rows("small_pack", parts, rows)

    def exchange(mine_ref, buf, send, recv, lsem):
        x, y, c, _ = _place()
        me = 4 * x + 2 * y + c
        peers = [(x ^ (k >> 2), y ^ ((k >> 1) & 1), c ^ (k & 1)) for k in range(1, 8)]
        _handshake(peers)
        own = pltpu.make_async_copy(mine_ref, buf.at[me], lsem)
        own.start()
        cps = []
        for k, to in enumerate(peers):
            cp = pltpu.make_async_remote_copy(src_ref=mine_ref, dst_ref=buf.at[me], send_sem=send.at[k], recv_sem=recv.at[k],
                                              device_id=to, device_id_type=MESH)
            cp.start()
            cps.append(cp)
        for k, (px, py, pc) in enumerate(peers):
            pltpu.make_async_remote_copy(src_ref=mine_ref, dst_ref=buf.at[4 * px + 2 * py + pc], send_sem=send.at[k],
                                         recv_sem=recv.at[k], device_id=(x, y, c), device_id_type=MESH).wait_recv()
        for cp in cps:
            cp.wait_send()
        own.wait()

    landed = pl.kernel(
        exchange, name="small_exchange", out_type=_sds((8, rows, cdim), F32),
        mesh=plsc.ScalarSubcoreMesh(axis_name="sequencer", num_cores=1),
        scratch_types=[pltpu.SemaphoreType.DMA((7,)), pltpu.SemaphoreType.DMA((7,)), pltpu.SemaphoreType.DMA],
        cost_estimate=_copy_cost(rows * cdim * 4, 7),
        compiler_params=pltpu.CompilerParams(collective_id=ALL_ID))(mine)

    def total(buf, o_ref):
        acc = buf[0]
        for d in range(1, 8):
            acc = acc + buf[d]
        o_ref[...] = acc

    return pl.pallas_call(total, in_specs=[vm], out_specs=vm, out_shape=_sds((rows, cdim), F32), name="small_sum")(landed)


def pair_sum(g, theirs, core, tm=512):
    _, r, c = g.shape
    tm = _tile(r // 2, tm)
    nh = r // 2 // tm

    def body(core_ref, a_ref, b_ref, o_ref):
        o_ref[...] = (a_ref[...].astype(F32) + b_ref[...].astype(F32)).astype(BF16)

    blk = (N_CHIPS, tm, c)
    return pl.pallas_call(
        body, grid_spec=pltpu.PrefetchScalarGridSpec(
            num_scalar_prefetch=1, grid=(nh,),
            in_specs=[pl.BlockSpec(blk, lambda i, cr: (0, cr[0] * nh + i, 0)), pl.BlockSpec(blk, lambda i, cr: (0, i, 0))],
            out_specs=pl.BlockSpec(blk, lambda i, cr: (0, i, 0))),
        out_shape=_sds(theirs.shape, BF16), compiler_params=_params(("parallel",)), name="pair_sum")(core, g, theirs)


def chip_sum(own, landed, chip, stack, layer, layers, tm=512):
    _, r, c = own.shape
    tm = _tile(r, tm)

    def body(chip_ref, own_ref, l_ref, *rest):
        acc = own_ref[...].astype(F32)
        for j in range(N_CHIPS - 1):
            acc = acc + l_ref[j].astype(F32)
        rest[-1][...] = acc

    in_specs = [pl.BlockSpec((None, tm, c), lambda i, qr: (qr[0], i, 0)),
                pl.BlockSpec((N_CHIPS - 1, tm, c), lambda i, qr: (0, i, 0))]
    args = [chip, own, landed]
    if stack is not None:
        in_specs.append(ANY)
        args.append(stack)
    return pl.pallas_call(
        body, grid_spec=pltpu.PrefetchScalarGridSpec(
            num_scalar_prefetch=1, grid=(r // tm,), in_specs=in_specs,
            out_specs=pl.BlockSpec((None, tm, c), lambda i, qr: (layer, i, 0))),
        out_shape=_sds((layers, r, c), F32), input_output_aliases={3: 0} if stack is not None else {},
        compiler_params=_params(("parallel",)), name="chip_sum")(*args)


def _adamw_math(w, g, m, v):
    bc1 = 1.0 - ADAM_B1 ** ADAM_STEP
    bc2 = 1.0 - ADAM_B2 ** ADAM_STEP
    nm = ADAM_B1 * m + (1.0 - ADAM_B1) * g
    nv = ADAM_B2 * v + (1.0 - ADAM_B2) * (g * g)
    return -ADAM_LR * ((nm / bc1) / (jnp.sqrt(nv / bc2) + ADAM_EPS) + ADAM_WD * w), nm, nv


def vector_update(red, chip, ws, ms, vs, where):
    n = len(ws)
    dd = red.shape[1]

    def body(chip_ref, red_ref, *refs):
        w_r, m_r, v_r = refs[0:n], refs[n:2 * n], refs[2 * n:3 * n]
        g_o, d_o, m_o, v_o = (refs[(3 + k) * n:(4 + k) * n] for k in range(4))
        q = chip_ref[0]

        def chip_block(val, width):
            out = val[:, 0:width]
            for p in range(1, val.shape[1] // width):
                out = jnp.where(q == p, val[:, p * width:(p + 1) * width], out)
            return out

        for k in range(n):
            for idx, r0, nr, cols in where[k]:
                width = w_r[k].shape[-1]
                if cols == "chip" and width * N_CHIPS != dd:
                    g = chip_block(jnp.concatenate([red_ref[pl.ds(r0 + j, 1), :] for j in range(nr)], axis=1), width)
                else:
                    g = red_ref[pl.ds(r0, nr), :]
                    g = chip_block(g, width) if cols == "chip" else g if cols == "all" else g[:, 0:cols]
                delta, nm, nv = _adamw_math(w_r[k][idx], g, m_r[k][idx], v_r[k][idx])
                g_o[k][idx] = g
                d_o[k][idx] = delta
                m_o[k][idx] = nm
                v_o[k][idx] = nv

    vm = pl.BlockSpec(memory_space=pltpu.VMEM)
    outs = pl.pallas_call(
        body, in_specs=[pl.BlockSpec(memory_space=pltpu.SMEM), vm] + [vm] * (3 * n), out_specs=[vm] * (4 * n),
        out_shape=[_sds(w.shape, F32) for w in ws] * 4, name="vector_update")(chip, red, *ws, *ms, *vs)
    return [outs[k * n:(k + 1) * n] for k in range(4)]


def adamw_joined(w, m, v, g_mine, g_theirs, core, tm=512):
    nl, r, c = w.shape
    tm = _tile(r // 2, tm)
    nh = r // 2 // tm
    bc1 = 1.0 - ADAM_B1 ** ADAM_STEP
    bc2 = 1.0 - ADAM_B2 ** ADAM_STEP

    def body(core_ref, w_ref, m_ref, v_ref, gm_ref, gt_ref, g_ref, d_ref, nm_ref, nv_ref):
        mine = (pl.program_id(1) // nh) == core_ref[0]
        gv = jnp.where(mine, gm_ref[...], gt_ref[...])
        nm = ADAM_B1 * m_ref[...] + (1.0 - ADAM_B1) * gv
        nv = ADAM_B2 * v_ref[...] + (1.0 - ADAM_B2) * (gv * gv)
        g_ref[...] = gv
        d_ref[...] = -ADAM_LR * ((nm / bc1) / (jnp.sqrt(nv / bc2) + ADAM_EPS) + ADAM_WD * w_ref[...])
        nm_ref[...] = nm
        nv_ref[...] = nv

    full = pl.BlockSpec((None, tm, c), lambda l, i, cr: (l, i, 0))
    half = pl.BlockSpec((None, tm, c), lambda l, i, cr: (l, i % nh, 0))
    return pl.pallas_call(
        body, grid_spec=pltpu.PrefetchScalarGridSpec(
            num_scalar_prefetch=1, grid=(nl, r // tm), in_specs=[full, full, full, half, half], out_specs=[full] * 4),
        out_shape=[_sds((nl, r, c), F32)] * 4, compiler_params=_params(("parallel", "parallel")),
        name="adamw_joined")(core, w, m, v, g_mine, g_theirs)


def adamw(w, g, m, v, tm=256):
    shape = w.shape
    c = shape[-1]
    r = w.size // c
    tm = _tile(r, tm)
    bc1 = 1.0 - ADAM_B1 ** ADAM_STEP
    bc2 = 1.0 - ADAM_B2 ** ADAM_STEP

    def body(w_ref, g_ref, m_ref, v_ref, d_ref, nm_ref, nv_ref):
        gv = g_ref[...]
        nm = ADAM_B1 * m_ref[...] + (1.0 - ADAM_B1) * gv
        nv = ADAM_B2 * v_ref[...] + (1.0 - ADAM_B2) * (gv * gv)
        d_ref[...] = -ADAM_LR * ((nm / bc1) / (jnp.sqrt(nv / bc2) + ADAM_EPS) + ADAM_WD * w_ref[...])
        nm_ref[...] = nm
        nv_ref[...] = nv

    outs = pl.pallas_call(
        body, grid=(r // tm,), in_specs=[_rows(tm, c)] * 4, out_specs=[_rows(tm, c)] * 3,
        out_shape=[_sds((r, c), F32)] * 3, compiler_params=_params(("parallel",)), name="adamw")(
            w.reshape(r, c), g.reshape(r, c), m.reshape(r, c), v.reshape(r, c))
    return [o.reshape(shape) for o in outs]


WEIGHTS = ['sc_w_in', 'sc_conv_w', 'sc_w_out', 'mla_w_dq', 'mla_g_q', 'mla_w_uq', 'mla_w_dkv', 'mla_g_kv', 'mla_w_uk',
           'mla_w_uv', 'mla_w_o', 'cf_w_pw1', 'cf_b_pw1', 'cf_dw_w', 'cf_dw_b', 'cf_norm_g', 'cf_norm_b', 'cf_w_pw2',
           'cf_b_pw2', 'ff_w1', 'ff_w2', 'ln_mix_g', 'ln_mix_b', 'ln_ff_g', 'ln_ff_b']
ARGS = ['x'] + WEIGHTS + ['loss_target'] + ['m_' + n for n in WEIGHTS] + ['v_' + n for n in WEIGHTS]


def _sq_relu(h):
    r = jnp.maximum(h.astype(F32), 0.0)
    return (r * r).astype(BF16)


def _mlp_forward(i, x, xb, w1, w2, g, b):
    hb = mm_plain_nn(f"mlp{i}_up", xb, w1, BF16, tn=1024)
    y, yb, xh, rstd = mm_residual_ln(f"mlp{i}_down_ln", hb, w2, x, g, b, tk=2048, a_fn=_sq_relu)
    return (y, yb), dict(xb=xb, hb=hb, xh=xh, rstd=rstd, g=g)


def _mlp_backward(i, dy, sv, w1, w2, dw1, dw2, reduce_after):
    s = dy.shape[0]
    dr, drb, dg, db, _ = ln_backward(f"mlp{i}_ln_bwd", dy, sv["xh"], sv["rstd"], sv["g"])
    tm, tn = _tile(s, 1024), 1024

    def epi(acc, e, o):
        o[0][...] = (acc * (2.0 * jnp.maximum(e[0][...].astype(F32), 0.0))).astype(BF16)

    dhb = mm_nt(f"mlp{i}_down_bwd", drb, w2, s, tm, tn, 1024, epi, [_sds((s, w2.k), BF16)], [_ij(tm, tn)],
                [sv["hb"]], [_ij(tm, tn)])[0]
    g_w2 = mm_tn(f"mlp{i}_dw2", sv["hb"], drb, dw2, s, 512, 1024, a_fn=_sq_relu)
    g_w1 = mm_tn(f"mlp{i}_dw1", sv["xb"], dhb, dw1, s, 1024, 512)
    dhb = reduce_after(dhb, {f"w1_{i}": g_w1, f"w2_{i}": g_w2})
    dx = mm_plain_nt(f"mlp{i}_up_bwd", dhb, w1, F32, tn=1024, tk=2048, add=dr, add_scale=ALPHA)
    return dx, dg, db


def kernel(x, sc_w_in, sc_conv_w, sc_w_out, mla_w_dq, mla_g_q, mla_w_uq, mla_w_dkv, mla_g_kv, mla_w_uk, mla_w_uv, mla_w_o, cf_w_pw1, cf_b_pw1, cf_dw_w, cf_dw_b, cf_norm_g, cf_norm_b, cf_w_pw2, cf_b_pw2, ff_w1, ff_w2, ln_mix_g, ln_mix_b, ln_ff_g, ln_ff_b, loss_target, m_sc_w_in, m_sc_conv_w, m_sc_w_out, m_mla_w_dq, m_mla_g_q, m_mla_w_uq, m_mla_w_dkv, m_mla_g_kv, m_mla_w_uk, m_mla_w_uv, m_mla_w_o, m_cf_w_pw1, m_cf_b_pw1, m_cf_dw_w, m_cf_dw_b, m_cf_norm_g, m_cf_norm_b, m_cf_w_pw2, m_cf_b_pw2, m_ff_w1, m_ff_w2, m_ln_mix_g, m_ln_mix_b, m_ln_ff_g, m_ln_ff_b, v_sc_w_in, v_sc_conv_w, v_sc_w_out, v_mla_w_dq, v_mla_g_q, v_mla_w_uq, v_mla_w_dkv, v_mla_g_kv, v_mla_w_uk, v_mla_w_uv, v_mla_w_o, v_cf_w_pw1, v_cf_b_pw1, v_cf_dw_w, v_cf_dw_b, v_cf_norm_g, v_cf_norm_b, v_cf_w_pw2, v_cf_b_pw2, v_ff_w1, v_ff_w2, v_ln_mix_g, v_ln_mix_b, v_ln_ff_g, v_ln_ff_b):
    given = dict(zip(ARGS, (x, sc_w_in, sc_conv_w, sc_w_out, mla_w_dq, mla_g_q, mla_w_uq, mla_w_dkv, mla_g_kv, mla_w_uk, mla_w_uv, mla_w_o, cf_w_pw1, cf_b_pw1, cf_dw_w, cf_dw_b, cf_norm_g, cf_norm_b, cf_w_pw2, cf_b_pw2, ff_w1, ff_w2, ln_mix_g, ln_mix_b, ln_ff_g, ln_ff_b, loss_target, m_sc_w_in, m_sc_conv_w, m_sc_w_out, m_mla_w_dq, m_mla_g_q, m_mla_w_uq, m_mla_w_dkv, m_mla_g_kv, m_mla_w_uk, m_mla_w_uv, m_mla_w_o, m_cf_w_pw1, m_cf_b_pw1, m_cf_dw_w, m_cf_dw_b, m_cf_norm_g, m_cf_norm_b, m_cf_w_pw2, m_cf_b_pw2, m_ff_w1, m_ff_w2, m_ln_mix_g, m_ln_mix_b, m_ln_ff_g, m_ln_ff_b, v_sc_w_in, v_sc_conv_w, v_sc_w_out, v_mla_w_dq, v_mla_g_q, v_mla_w_uq, v_mla_w_dkv, v_mla_g_kv, v_mla_w_uk, v_mla_w_uv, v_mla_w_o, v_cf_w_pw1, v_cf_b_pw1, v_cf_dw_w, v_cf_dw_b, v_cf_norm_g, v_cf_norm_b, v_cf_w_pw2, v_cf_b_pw2, v_ff_w1, v_ff_w2, v_ln_mix_g, v_ln_mix_b, v_ln_ff_g, v_ln_ff_b)))
    s, d = x.shape[1], x.shape[2]
    d_ff = 4 * d
    dq4 = d // N_CHIPS
    xq = lax.axis_index("x") * 2 + lax.axis_index("y")

    w_dkv_pad = jnp.pad(mla_w_dkv[0], ((0, 0), (0, 128 - QK_ROPE)))
    w_uq_pad = jnp.pad(mla_w_uq[0].reshape(Q_LORA, 2, QK_NOPE + QK_ROPE), ((0, 0), (0, 0), (0, HEAD_PAD - QK_NOPE - QK_ROPE)))
    small = pack_rows("vector_weights_pack", [
        sc_conv_w.reshape(2 * SC_WIDTH, dq4), cf_b_pw1.reshape(2, dq4), cf_dw_w[0], cf_dw_b, cf_norm_g, cf_norm_b,
        cf_b_pw2], 64)
    mlp_w = lambda i: [ff_w1[i].astype(BF16), ff_w2[i].astype(BF16)]
    g_in, g_out, g_w1, g_w2 = [None] * 2, [None] * 2, [None] * DEPTH, [None] * DEPTH
    g_in[0], g_out[0], g_small = gather_shards(
        "gather_mixer0", [sc_w_in[0].astype(BF16), sc_w_out[0].astype(BF16), small], by_columns=(0,))
    (g_w1[0],) = gather_shards("gather_up0", [ff_w1[0].astype(BF16)], by_columns=(0,))
    (g_w2[0],) = gather_shards("gather_down0", [ff_w2[0].astype(BF16)])
    g_dqkv, g_uq, g_uk, g_uv, g_o = gather_shards("gather_mixer1", [
        jnp.concatenate([mla_w_dq[0], w_dkv_pad], axis=1).astype(BF16),
        w_uq_pad.reshape(Q_LORA, 2 * HEAD_PAD).astype(BF16),
        mla_w_uk.reshape(KV_LORA // N_CHIPS, N_HEADS * QK_NOPE).astype(BF16),
        mla_w_uv.reshape(KV_LORA // N_CHIPS, N_HEADS * V_HEAD).astype(BF16), mla_w_o[0].astype(BF16)], by_columns=(1,))
    g_w1[1], g_w2[1] = gather_shards("gather_mlp1", mlp_w(1), by_columns=(0,))
    g_pw1, g_pw2, g_w1[2], g_w2[2] = gather_shards(
        "gather_layer2", [cf_w_pw1[0].astype(BF16), cf_w_pw2[0].astype(BF16)] + mlp_w(2), by_columns=(0, 2))
    g_in[1], g_out[1], g_w1[3], g_w2[3] = gather_shards(
        "gather_layer3", [sc_w_in[1].astype(BF16), sc_w_out[1].astype(BF16)] + mlp_w(3), by_columns=(0, 2))

    wd_t = Q_LORA + KV_LORA + 128
    w_in = [Stk("full", d, 3 * d, g_in[j]) for j in range(2)]
    w_out = [Stk("row", d, d, g_out[j]) for j in range(2)]
    w_dqkv = Stk("row", d, wd_t, g_dqkv)
    w_uq = Stk("full", Q_LORA, N_HEADS * HEAD_PAD, g_uq)
    w_uk = Stk("row", KV_LORA, N_HEADS * QK_NOPE, g_uk)
    w_uv = Stk("row", KV_LORA, N_HEADS * V_HEAD, g_uv)
    w_o = Stk("row", d, d, g_o)
    w_pw1 = Stk("full", d, 2 * d, g_pw1)
    w_pw2 = Stk("row", d, d, g_pw2)
    w_1 = [Stk("full", d, d_ff, g_w1[i]) for i in range(DEPTH)]
    w_2 = [Stk("row", d_ff, d, g_w2[i]) for i in range(DEPTH)]

    def wide(rows):
        return jnp.swapaxes(rows, 0, 1).reshape(rows.shape[1], d)

    conv_w = wide(g_small[:, 0:6]).reshape(2, SC_WIDTH, d)
    b_pw1 = g_small[:, 6:8].reshape(1, 2 * d)
    dw_w = wide(g_small[:, 8:39])
    dw_b, norm_g, norm_b, b_pw2 = (wide(g_small[:, 39 + k:40 + k]) for k in range(4))

    pos = jnp.arange(s, dtype=F32)
    inv_freq = ROPE_THETA ** (-jnp.arange(0, QK_ROPE, 2, dtype=F32) / QK_ROPE)
    ang = pos[:, None] * inv_freq[None, :]
    cos, sin, zero = jnp.cos(ang), jnp.sin(ang), jnp.zeros((s, 128 - QK_ROPE), F32)
    cf = jnp.concatenate([cos, cos, zero], axis=1)
    sf = jnp.concatenate([-sin, sin, zero], axis=1)

    def row(a, i):
        return a[i:i + 1]

    xs = x.reshape(s, d)
    cur = (xs, xs.astype(BF16))
    tape = []
    for i in range(DEPTH):
        mixer, j = i % 3, i // 3
        xf, xb = cur
        lg, lb = row(ln_mix_g, i), row(ln_mix_b, i)
        if mixer == 0:
            u = mm_plain_nn(f"sc{j}_in", xb, w_in[j], F32, tn=3 * dq4)
            gb = short_conv_gate(u, conv_w[j])
            y, yb, xh, rstd = mm_residual_ln(f"sc{j}_out_ln", gb, w_out[j], xf, lg, lb)
            sv = dict(xb=xb, u=u, gb=gb)
        elif mixer == 1:
            t = mm_plain_nn("mla_down", xb, w_dqkv, F32, tn=wd_t // 2)
            cq, ckv, kpe = mla_latents(t, mla_g_q, mla_g_kv, cf, sf)
            qh = mla_queries(cq, w_uq, cf, sf)
            kh = mla_keys(ckv, w_uk, kpe)
            vh = mm_plain_nn("mla_values", ckv, w_uv, BF16, tk=KV_LORA)
            oh = attention(qh, kh, vh)
            y, yb, xh, rstd = mm_residual_ln("mla_out_ln", oh, w_o, xf, lg, lb)
            sv = dict(xb=xb, t=t, cq=cq, ckv=ckv, qh=qh, kh=kh, vh=vh, oh=oh)
        else:
            u = mm_plain_nn("cf_pw1", xb, w_pw1, F32, bias=b_pw1)
            hc = conformer_glu_conv(u, dw_w, dw_b)
            sb = conformer_norm_swish(hc, norm_g, norm_b)
            y, yb, xh, rstd = mm_residual_ln("cf_pw2_ln", sb, w_pw2, xf, lg, lb, bias=b_pw2)
            sv = dict(xb=xb, u=u, hc=hc, sb=sb)
        sv.update(xh=xh, rstd=rstd, g=lg)
        cur, sv_mlp = _mlp_forward(i, y, yb, w_1[i], w_2[i], row(ln_ff_g, i), row(ln_ff_b, i))
        tape.append((sv, sv_mlp))

    dy, loss_part = loss_head(cur[0], loss_target.reshape(s, d))

    grads = {}
    smalls = {}
    g_ln = {n: [None] * DEPTH for n in ("ln_mix_g", "ln_mix_b", "ln_ff_g", "ln_ff_b")}
    conv_grads = [None, None]
    core = lax.axis_index("c").astype(jnp.int32).reshape(1)
    chip = xq.astype(jnp.int32).reshape(1)
    pairs, landed = {}, {}
    ready, theirs = [], {}

    def reduce_after(x, new, early=False):
        out = lax.optimization_barrier((x, *new.values()))
        grads.update(zip(new, out[1:]))
        if early:
            theirs.update(zip(new, pair_exchange(f"pair_exchange_{len(theirs)}", list(out[1:]), True)))
        ready.extend(new)
        return out[0]

    def reduce_layer(i, x):
        late = [n for n in ready if n not in theirs]
        if late:
            theirs.update(zip(late, pair_exchange(f"pair_exchange_layer{i}", [grads[n] for n in late], False)))
        sums = [pair_sum(grads[n], theirs[n], core) for n in ready]
        pairs.update(zip(ready, sums))
        landed.update(zip(ready, chip_exchange(f"chip_exchange_layer{i}", sums)))
        exchanged.append(list(ready))
        ready.clear()
        return lax.optimization_barrier((x, *sums))[0]

    groups = [["in_0", "in_1"], ["out_0", "out_1"], ["dqkv"], ["uq"], ["uk"], ["uv"], ["o"], ["pw1"], ["pw2"],
              [f"w1_{i}" for i in range(DEPTH)], [f"w2_{i}" for i in range(DEPTH)]]
    stacks = [None] * len(groups)
    exchanged = []

    def sum_layer(x, last=False):
        names = exchanged.pop(0)
        if last:
            out = lax.optimization_barrier((x, *[landed[n] for n in names]))
            landed.update(zip(names, out[1:]))
        new = []
        for n in names:
            k = next(k for k, members in enumerate(groups) if n in members)
            stacks[k] = chip_sum(pairs[n], landed[n], chip, stacks[k], groups[k].index(n), len(groups[k]))
            new.append(stacks[k])
        return out[0] if last else lax.optimization_barrier((x, *new))[0]

    for i in reversed(range(DEPTH)):
        mixer, j = i % 3, i // 3
        sv, sv_mlp = tape[i]
        dy, g_ln["ln_ff_g"][i], g_ln["ln_ff_b"][i] = _mlp_backward(
            i, dy, sv_mlp, w_1[i], w_2[i], Stk("col", d, d_ff), Stk("row", d_ff, d),
            lambda x_, new: reduce_after(x_, new, early=i > 0))
        if i == 0:
            dy = reduce_layer("0_mlp", dy)
        dr, drb, g_ln["ln_mix_g"][i], g_ln["ln_mix_b"][i], dr_sum = ln_backward(
            f"mix{i}_ln_bwd", dy, sv["xh"], sv["rstd"], sv["g"])
        if mixer == 0:
            dgate = mm_plain_nt(f"sc{j}_out_bwd", drb, w_out[j], F32)
            dw_out = mm_tn(f"sc{j}_dw_out", sv["gb"], drb, Stk("row", d, d), s, 512, 1024)
            du, conv_grads[j] = short_conv_gate_bwd(sv["u"], conv_w[j], dgate)
            nb = d // 256
            dw_in = mm_tn(
                f"sc{j}_dw_in", sv["xb"], du, Stk("col", d, 3 * d), s, 1024, 256,
                b_spec=pl.BlockSpec((None, s, 256), lambda i_, j_, k_: (j_ // nb, k_, j_ % nb)))
            du = reduce_after(du, {f"in_{j}": dw_in, f"out_{j}": dw_out})
            dy = mm_plain_nt(
                f"sc{j}_in_bwd", du, w_in[j], F32, tn=1024, tk=d, add=dr, add_scale=ALPHA,
                a_spec_fn=(s, lambda tm, tk: pl.BlockSpec((None, tm, tk), lambda i_, j_, k_: (k_, i_, 0))))
        elif mixer == 1:
            do = mm_plain_nt("mla_out_bwd", drb, w_o, BF16)
            g_o = mm_tn("mla_dw_o", sv["oh"], drb, Stk("row", d, d), s, 512, 1024)
            dqh, dkh, dvh = attention_bwd(sv["qh"], sv["kh"], sv["vh"], do)
            dql, dkn, dkpe = mla_unrope_grads(dqh, dkh, cf, sf)
            g_uq = mm_tn("mla_dw_uq", sv["cq"], dql, Stk("col", Q_LORA, N_HEADS * HEAD_PAD), s, Q_LORA, 512)
            dcq = mm_plain_nt("mla_uq_bwd", dql, w_uq, F32, tn=Q_LORA)
            g_uk = mm_tn("mla_dw_uk", sv["ckv"], dkn, Stk("row", KV_LORA, N_HEADS * QK_NOPE), s, KV_LORA, 1024)
            g_uv = mm_tn("mla_dw_uv", sv["ckv"], dvh, Stk("row", KV_LORA, N_HEADS * V_HEAD), s, KV_LORA, 1024)
            dckv = mm_plain_nt("mla_uk_bwd", dkn, w_uk, F32, tn=KV_LORA)
            dckv = mm_plain_nt("mla_uv_bwd", dvh, w_uv, F32, tn=KV_LORA, add=dckv)
            dt, smalls["g_q"], smalls["g_kv"] = mla_latents_bwd(sv["t"], mla_g_q, mla_g_kv, cf, sf, dcq, dckv, dkpe)
            g_dqkv = mm_tn("mla_dw_down", sv["xb"], dt, Stk("row", d, wd_t), s, 512, wd_t)
            dt = reduce_after(dt, {"dqkv": g_dqkv, "uq": g_uq, "uk": g_uk, "uv": g_uv, "o": g_o})
            dy = mm_plain_nt("mla_down_bwd", dt, w_dqkv, F32, tk=wd_t, add=dr, add_scale=ALPHA)
        else:
            dsw = mm_plain_nt("cf_pw2_bwd", drb, w_pw2, F32)
            g_pw2 = mm_tn("cf_dw_pw2", sv["sb"], drb, Stk("row", d, d), s, 512, 1024)
            smalls["b_pw2"] = dr_sum
            dhc, smalls["norm_g"], smalls["norm_b"] = conformer_norm_swish_bwd(sv["hc"], norm_g, norm_b, dsw)
            du, smalls["b_pw1"], smalls["dw_w"], smalls["dw_b"] = conformer_glu_conv_bwd(sv["u"], dw_w, dhc)
            nb = d // 512
            g_pw1 = mm_tn(
                "cf_dw_pw1", sv["xb"], du, Stk("col", d, 2 * d), s, 1024, 512,
                b_spec=pl.BlockSpec((None, s, 512), lambda i_, j_, k_: (j_ // nb, k_, j_ % nb)))
            du = reduce_after(du, {"pw1": g_pw1, "pw2": g_pw2})
            dy = mm_plain_nt(
                "cf_pw1_bwd", du, w_pw1, F32, tn=1024, tk=d, add=dr, add_scale=ALPHA,
                a_spec_fn=(s, lambda tm, tk: pl.BlockSpec((None, tm, tk), lambda i_, j_, k_: (k_, i_, 0))))
        if i < DEPTH - 1:
            dy = sum_layer(dy)
        dy = reduce_layer(i, dy)
    dy = sum_layer(sum_layer(dy, last=True), last=True)
    grad_x = dy.reshape(1, s, d)

    mine = stacks
    other = (pair_share("pair_share_mixers", mine[:9]) + pair_share("pair_share_up", mine[9:10])
             + pair_share("pair_share_down", mine[10:]))

    def padded(get):
        dqkv = jnp.concatenate([get("mla_w_dq")[0], jnp.pad(get("mla_w_dkv")[0], ((0, 0), (0, 128 - QK_ROPE)))], axis=1)
        uq = jnp.pad(get("mla_w_uq")[0].reshape(Q_LORA, 2, QK_NOPE + QK_ROPE),
                     ((0, 0), (0, 0), (0, HEAD_PAD - QK_NOPE - QK_ROPE))).reshape(Q_LORA, 2 * HEAD_PAD)
        return [get("sc_w_in"), get("sc_w_out"), dqkv[None], uq[None],
                get("mla_w_uk").reshape(1, KV_LORA // N_CHIPS, d), get("mla_w_uv").reshape(1, KV_LORA // N_CHIPS, d),
                get("mla_w_o"), get("cf_w_pw1"), get("cf_w_pw2"), get("ff_w1"), get("ff_w2")]

    w_l, m_l, v_l = (padded(lambda n, p=p: given[p + n]) for p in ("", "m_", "v_"))
    res = [adamw_joined(w_l[k], m_l[k], v_l[k], mine[k], other[k], core) for k in range(len(groups))]

    def unpadded(k):
        r_in, r_out, r_dqkv, r_uq, r_uk, r_uv, r_o, r_pw1, r_pw2, r_w1, r_w2 = (r[k] for r in res)
        return {
            "sc_w_in": r_in, "sc_w_out": r_out, "mla_w_dq": r_dqkv[:, :, 0:Q_LORA],
            "mla_w_dkv": r_dqkv[:, :, Q_LORA:Q_LORA + KV_LORA + QK_ROPE],
            "mla_w_uq": r_uq.reshape(1, Q_LORA, 2, HEAD_PAD)[:, :, :, 0:QK_NOPE + QK_ROPE].reshape(mla_w_uq.shape),
            "mla_w_uk": r_uk.reshape(mla_w_uk.shape), "mla_w_uv": r_uv.reshape(mla_w_uv.shape),
            "mla_w_o": r_o, "cf_w_pw1": r_pw1, "cf_w_pw2": r_pw2, "ff_w1": r_w1, "ff_w2": r_w2}

    big_g, big_d, big_m, big_v = (unpadded(k) for k in range(4))

    pad_row = lambda a: jnp.pad(a, ((0, 0), (0, d - a.shape[1])))
    small_parts = ([g for n in ("ln_mix_g", "ln_mix_b", "ln_ff_g", "ln_ff_b") for g in g_ln[n]]
                   + [pad_row(smalls["g_q"]), pad_row(smalls["g_kv"]), conv_grads[0], conv_grads[1],
                      smalls["b_pw1"].reshape(2, d), smalls["dw_w"], smalls["dw_b"], smalls["norm_g"], smalls["norm_b"],
                      smalls["b_pw2"], loss_part])
    red = all_reduce_small(small_parts, 64)
    loss = red[61, 0]

    where = {
        "ln_mix_g": [((), 0, DEPTH, "all")], "ln_mix_b": [((), 4, DEPTH, "all")],
        "ln_ff_g": [((), 8, DEPTH, "all")], "ln_ff_b": [((), 12, DEPTH, "all")],
        "mla_g_q": [((), 16, 1, Q_LORA)], "mla_g_kv": [((), 17, 1, KV_LORA)],
        "sc_conv_w": [((0,), 18, SC_WIDTH, "chip"), ((1,), 21, SC_WIDTH, "chip")],
        "cf_b_pw1": [((), 24, 2, "chip")], "cf_dw_w": [((0,), 26, CONF_WIDTH, "chip")],
        "cf_dw_b": [((), 57, 1, "chip")], "cf_norm_g": [((), 58, 1, "chip")], "cf_norm_b": [((), 59, 1, "chip")],
        "cf_b_pw2": [((), 60, 1, "chip")]}
    vec = list(where)
    vec_res = vector_update(red, chip, [given[n] for n in vec], [given["m_" + n] for n in vec],
                            [given["v_" + n] for n in vec], [where[n] for n in vec])
    gw = dict(big_g)
    upd = {n: [big_d[n], big_m[n], big_v[n]] for n in big_g}
    for k, n in enumerate(vec):
        gw[n] = vec_res[0][k]
        upd[n] = [vec_res[1][k], vec_res[2][k], vec_res[3][k]]

    return (loss, grad_x, *[gw[n] for n in WEIGHTS], *[upd[n][0] for n in WEIGHTS],
            *[upd[n][1] for n in WEIGHTS], *[upd[n][2] for n in WEIGHTS])
```

```python
import jax
import jax.numpy as jnp
from jax import lax
from jax.experimental import pallas as pl
from jax.experimental.pallas import tpu as pltpu
from jax.experimental.pallas import tpu_sc as plsc

F32 = jnp.float32
BF16 = jnp.bfloat16
MESH = pl.DeviceIdType.MESH

DEPTH = 4
ALPHA = (2.0 * DEPTH) ** 0.25
LN_EPS = 1e-5
RMS_EPS = 1e-6
CHUNK_SHIFT = 6
N_HEADS = 8
QK_NOPE = 128
QK_ROPE = 64
V_HEAD = 128
HEAD_PAD = 256
Q_LORA = 384
KV_LORA = 256
ROPE_THETA = 10000.0
SC_WIDTH = 3
CONF_WIDTH = 31
CONV_PAD = 32
CONV_CHUNK = 64
N_CHIPS = 4
ATTN_SCALE = (QK_NOPE + QK_ROPE) ** -0.5

ADAM_LR = 0.001
ADAM_B1 = 0.9
ADAM_B2 = 0.999
ADAM_EPS = 1e-08
ADAM_WD = 0.01
ADAM_STEP = 10

VMEM_LIMIT = 56 * 2**20

NN = (((1,), (0,)), ((), ()))
NT = (((1,), (1,)), ((), ()))
TN = (((0,), (0,)), ((), ()))


def _params(sem=None):
    return pltpu.CompilerParams(dimension_semantics=sem, vmem_limit_bytes=VMEM_LIMIT)


class Stk:
    def __init__(self, kind, k, n, arr=None, layers=None, layer=None):
        self.kind, self.k, self.n, self.layers, self.layer = kind, k, n, layers, layer
        self.plain = (kind == "row" and layers is None) or kind == "full"
        self.kloc = k // N_CHIPS if kind == "row" else k
        self.nloc = n // N_CHIPS if kind == "col" else n
        if arr is not None and self.plain:
            arr = arr.reshape(k, n)
        self.arr = arr

    @property
    def shape(self):
        if self.plain:
            return (self.k, self.n)
        lead = (N_CHIPS,) if self.layers is None else (N_CHIPS, self.layers)
        return lead + (self.kloc, self.nloc)

    def spec(self, bk, bn, f):
        if self.plain:
            return pl.BlockSpec((bk, bn), f)
        assert self.kloc % bk == 0 and self.nloc % bn == 0, (self.kloc, bk, self.nloc, bn)
        pk, pn = self.kloc // bk, self.nloc // bn
        kind, layer = self.kind, self.layer

        def imap(*g):
            kb, nb = f(*g)
            if kind == "row":
                q, kb, nb = kb // pk, kb % pk, nb
            else:
                q, kb, nb = nb // pn, kb, nb % pn
            return (q, kb, nb) if layer is None else (q, layer, kb, nb)

        block = (None, bk, bn) if layer is None else (None, None, bk, bn)
        return pl.BlockSpec(block, imap)


def _mm(name, mode, a, b, grid, a_spec, b_spec, acc_shape, extras, extra_specs, out_shapes, out_specs, epi, a_fn=None):
    nk = grid[2]
    ne = len(extras)

    def body(*refs):
        a_ref, b_ref = refs[0], refs[1]
        e_refs = refs[2:2 + ne]
        av = a_ref[...] if a_fn is None else a_fn(a_ref[...])
        part = lax.dot_general(av, b_ref[...], mode, preferred_element_type=F32)
        if nk == 1:
            epi(part, e_refs, refs[2 + ne:])
            return
        o_refs = refs[2 + ne:-1]
        acc = refs[-1]
        k = pl.program_id(2)

        @pl.when(k == 0)
        def _():
            acc[...] = part

        @pl.when(k > 0)
        def _():
            acc[...] += part

        @pl.when(k == nk - 1)
        def _():
            epi(acc[...], e_refs, o_refs)

    return pl.pallas_call(
        body, grid=grid, in_specs=[a_spec, b_spec, *extra_specs], out_specs=out_specs, out_shape=out_shapes,
        scratch_shapes=[pltpu.VMEM(acc_shape, F32)] if nk > 1 else [],
        compiler_params=_params(("parallel", "parallel", "arbitrary")), name=name)(a, b, *extras)


def _tile(n, t):
    t = min(n, t)
    while n % t:
        t -= 8
    assert t > 0, (n, t)
    return t


def mm_nn(name, a, w, tm, tn, tk, epi, out_shapes, out_specs, extras=(), extra_specs=(), a_spec=None, a_fn=None):
    m = a.shape[0]
    tm, tn, tk = _tile(m, tm), _tile(w.n, tn), _tile(w.k, tk)
    grid = (m // tm, w.n // tn, w.k // tk)
    a_spec = a_spec or pl.BlockSpec((tm, tk), lambda i, j, k: (i, k))
    b_spec = w.spec(tk, tn, lambda i, j, k: (k, j))
    return _mm(name, NN, a, w.arr, grid, a_spec, b_spec, (tm, tn), extras, extra_specs, out_shapes, out_specs, epi, a_fn)


def mm_nt(name, a, w, m, tm, tn, tk, epi, out_shapes, out_specs, extras=(), extra_specs=(), a_spec=None):
    tm, tn, tk = _tile(m, tm), _tile(w.k, tn), _tile(w.n, tk)
    grid = (m // tm, w.k // tn, w.n // tk)
    a_spec = a_spec or pl.BlockSpec((tm, tk), lambda i, j, k: (i, k))
    b_spec = w.spec(tn, tk, lambda i, j, k: (j, k))
    return _mm(name, NT, a, w.arr, grid, a_spec, b_spec, (tm, tn), extras, extra_specs, out_shapes, out_specs, epi)


def mm_tn(name, a, b, dw, s, tm=512, tn=512, tk=4096, a_spec=None, b_spec=None, a_fn=None):
    tm, tn, tk = _tile(dw.k, tm), _tile(dw.n, tn), _tile(s, tk)
    grid = (dw.k // tm, dw.n // tn, s // tk)
    a_spec = a_spec or pl.BlockSpec((tk, tm), lambda i, j, k: (k, i))
    b_spec = b_spec or pl.BlockSpec((tk, tn), lambda i, j, k: (k, j))

    def epi(acc, e, o):
        o[0][...] = acc.astype(BF16)

    out = _mm(name, TN, a, b, grid, a_spec, b_spec, (tm, tn), (), (), [jax.ShapeDtypeStruct(dw.shape, BF16)],
              [dw.spec(tm, tn, lambda i, j, k: (i, j))], epi, a_fn)[0]
    return out.reshape(N_CHIPS, dw.k // N_CHIPS, dw.n) if dw.plain else out


def _sds(shape, dtype):
    return jax.ShapeDtypeStruct(shape, dtype)


def _ij(tm, tn):
    return pl.BlockSpec((tm, tn), lambda i, j, k: (i, j))


def _i0(tm, c):
    return pl.BlockSpec((tm, c), lambda i, j, k: (i, 0))


def _0j(r, tn):
    return pl.BlockSpec((r, tn), lambda i, j, k: (0, j))


def _layer_norm_rows(r, g, b):
    mu = jnp.mean(r, axis=-1, keepdims=True)
    d = r - mu
    var = jnp.mean(d * d, axis=-1, keepdims=True)
    rstd = lax.rsqrt(var + LN_EPS)
    xh = d * rstd
    return xh * g + b, xh, rstd


def mm_residual_ln(name, a, w, x, g, b, bias=None, tm=512, tk=1024, a_fn=None):
    s, d = x.shape
    tm = _tile(s, tm)
    extras = [x, g, b] + ([bias] if bias is not None else [])
    especs = [_i0(tm, d), _0j(1, d), _0j(1, d)] + ([_0j(1, d)] if bias is not None else [])

    def epi(acc, e, o):
        r = ALPHA * e[0][...] + acc
        if bias is not None:
            r = r + e[3][...]
        y, xh, rstd = _layer_norm_rows(r, e[1][...], e[2][...])
        o[0][...] = y
        o[1][...] = y.astype(BF16)
        o[2][...] = xh
        o[3][...] = rstd

    return mm_nn(name, a, w, tm, d, tk, epi,
                 [_sds((s, d), F32), _sds((s, d), BF16), _sds((s, d), F32), _sds((s, 1), F32)],
                 [_i0(tm, d), _i0(tm, d), _i0(tm, d), _i0(tm, 1)], extras, especs, a_fn=a_fn)


def mm_plain_nn(name, a, w, out_dtype, tm=1024, tn=512, tk=1024, bias=None):
    m = a.shape[0]
    tm, tn = _tile(m, tm), _tile(w.n, tn)
    if w.kind == "col":
        tn = _tile(w.nloc, tn)

    def epi(acc, e, o):
        if bias is not None:
            acc = acc + e[0][...]
        o[0][...] = acc.astype(out_dtype)

    extras, especs = ([bias], [_0j(1, tn)]) if bias is not None else ((), ())
    return mm_nn(name, a, w, tm, tn, tk, epi, [_sds((m, w.n), out_dtype)], [_ij(tm, tn)], extras, especs)[0]


def mm_plain_nt(name, a, w, out_dtype, tm=1024, tn=512, tk=1024, add=None, add_scale=1.0, a_spec_fn=None):
    m = a.shape[0] if a_spec_fn is None else a_spec_fn[0]
    tm, tn = _tile(m, tm), _tile(w.k, tn)
    tk = _tile(w.n, tk)
    if w.kind == "col":
        tk = _tile(w.nloc, tk)
    if w.kind == "row" and not w.plain:
        tn = _tile(w.kloc, tn)

    def epi(acc, e, o):
        if add is not None:
            acc = acc + add_scale * e[0][...].astype(F32)
        o[0][...] = acc.astype(out_dtype)

    extras, especs = ([add], [_ij(tm, tn)]) if add is not None else ((), ())
    a_spec = None if a_spec_fn is None else a_spec_fn[1](tm, tk)
    return mm_nt(name, a, w, m, tm, tn, tk, epi, [_sds((m, w.k), out_dtype)], [_ij(tm, tn)], extras, especs,
                 a_spec=a_spec)[0]


def _rows(tm, c):
    return pl.BlockSpec((tm, c), lambda i: (i, 0))


def _fix(shape):
    nd = len(shape)
    return pl.BlockSpec(shape, lambda i: (0,) * nd)


def _accumulate(ref, val):
    @pl.when(pl.program_id(0) == 0)
    def _():
        ref[...] = jnp.zeros_like(ref)

    ref[...] += val


def ln_backward(name, dy, xhat, rstd, g, tm=512):
    s, d = dy.shape
    tm = _tile(s, tm)

    def body(dy_ref, xh_ref, rstd_ref, g_ref, dr_ref, drb_ref, dg_ref, db_ref, ds_ref):
        dyv, xh = dy_ref[...], xh_ref[...]
        dxh = dyv * g_ref[...]
        m1 = jnp.mean(dxh, axis=-1, keepdims=True)
        m2 = jnp.mean(dxh * xh, axis=-1, keepdims=True)
        dr = rstd_ref[...] * (dxh - m1 - xh * m2)
        dr_ref[...] = dr
        drb_ref[...] = dr.astype(BF16)
        _accumulate(dg_ref, jnp.sum(dyv * xh, axis=0, keepdims=True))
        _accumulate(db_ref, jnp.sum(dyv, axis=0, keepdims=True))
        _accumulate(ds_ref, jnp.sum(dr, axis=0, keepdims=True))

    return pl.pallas_call(
        body, grid=(s // tm,),
        in_specs=[_rows(tm, d), _rows(tm, d), _rows(tm, 1), _fix((1, d))],
        out_specs=[_rows(tm, d), _rows(tm, d), _fix((1, d)), _fix((1, d)), _fix((1, d))],
        out_shape=[_sds((s, d), F32), _sds((s, d), BF16), _sds((1, d), F32), _sds((1, d), F32), _sds((1, d), F32)],
        compiler_params=_params(("arbitrary",)), name=name)(dy, xhat, rstd, g)


def loss_head(y, target, tm=256):
    s, d = y.shape
    tm = _tile(s, tm)

    def body(y_ref, t_ref, dy_ref, loss_ref):
        e = y_ref[...] - t_ref[...]
        dy_ref[...] = e * (1.0 / d)
        part = 0.5 * jnp.sum(jnp.mean(e * e, axis=-1, keepdims=True), axis=0, keepdims=True)
        _accumulate(loss_ref, jnp.broadcast_to(part, (1, d)))

    return pl.pallas_call(
        body, grid=(s // tm,), in_specs=[_rows(tm, d), _rows(tm, d)],
        out_specs=[_rows(tm, d), _fix((1, d))], out_shape=[_sds((s, d), F32), _sds((1, d), F32)],
        compiler_params=_params(("arbitrary",)), name="loss_head")(y, target)


def _cols(s, tc, off=0):
    return pl.BlockSpec((s, tc), lambda i: (0, i + off))


def _shift_down(z, sft, rows):
    return jnp.where(rows >= sft, pltpu.roll(z, sft, 0), 0.0)


def _shift_up(z, sft, rows, s):
    return jnp.where(rows < s - sft, pltpu.roll(z, (s - sft) % s, 0), 0.0)


def short_conv_gate(u, conv_w, tc=256):
    s, d3 = u.shape
    d = d3 // 3
    nb = d // tc

    def body(b_ref, c_ref, h_ref, w_ref, o_ref):
        rows = lax.broadcasted_iota(jnp.int32, (s, tc), 0)
        z = c_ref[...] * h_ref[...]
        cz = jnp.zeros((s, tc), F32)
        for k in range(SC_WIDTH):
            sft = SC_WIDTH - 1 - k
            cz = cz + w_ref[pl.ds(k, 1), :] * (_shift_down(z, sft, rows) if sft else z)
        o_ref[...] = (b_ref[...] * cz).astype(BF16)

    return pl.pallas_call(
        body, grid=(nb,),
        in_specs=[_cols(s, tc), _cols(s, tc, nb), _cols(s, tc, 2 * nb), _cols(SC_WIDTH, tc)],
        out_specs=_cols(s, tc), out_shape=_sds((s, d), BF16),
        compiler_params=_params(("parallel",)), name="short_conv_gate")(u, u, u, conv_w)


def short_conv_gate_bwd(u, conv_w, dg, tc=256):
    s, d3 = u.shape
    d = d3 // 3
    nb = d // tc

    def body(b_ref, c_ref, h_ref, w_ref, dg_ref, du_ref, dw_ref):
        rows = lax.broadcasted_iota(jnp.int32, (s, tc), 0)
        c, h, dgv = c_ref[...], h_ref[...], dg_ref[...]
        z = c * h
        dcz = dgv * b_ref[...]
        cz = jnp.zeros((s, tc), F32)
        dz = jnp.zeros((s, tc), F32)
        for k in range(SC_WIDTH):
            sft = SC_WIDTH - 1 - k
            zs = _shift_down(z, sft, rows) if sft else z
            wk = w_ref[pl.ds(k, 1), :]
            cz = cz + wk * zs
            dz = dz + wk * (_shift_up(dcz, sft, rows, s) if sft else dcz)
            dw_ref[pl.ds(k, 1), :] = jnp.sum(dcz * zs, axis=0, keepdims=True)
        du_ref[0] = (dgv * cz).astype(BF16)
        du_ref[1] = (dz * h).astype(BF16)
        du_ref[2] = (dz * c).astype(BF16)

    return pl.pallas_call(
        body, grid=(nb,),
        in_specs=[_cols(s, tc), _cols(s, tc, nb), _cols(s, tc, 2 * nb), _cols(SC_WIDTH, tc), _cols(s, tc)],
        out_specs=[pl.BlockSpec((3, s, tc), lambda i: (0, 0, i)), _cols(SC_WIDTH, tc)],
        out_shape=[_sds((3, s, d), BF16), _sds((SC_WIDTH, d), F32)],
        compiler_params=_params(("parallel",)), name="short_conv_gate_bwd")(u, u, u, conv_w, dg)


def _store_shifted_down(ref, z, rows):
    s, tc = z.shape
    for b in range(8):
        ref[b, pl.ds(0, CONV_PAD), :] = jnp.zeros((CONV_PAD, tc), F32)
        ref[b, pl.ds(CONV_PAD, s), :] = z if b == 0 else _shift_down(z, b, rows)


def _store_shifted_up(ref, z, rows):
    s, tc = z.shape
    for b in range(8):
        ref[b, pl.ds(0, s), :] = z if b == 0 else _shift_up(z, b, rows, s)
        ref[b, pl.ds(s, CONV_PAD), :] = jnp.zeros((CONV_PAD, tc), F32)


def conformer_glu_conv(u, dw_w, dw_b, tc=128):
    s, d2 = u.shape
    d = d2 // 2
    nb = d // tc

    ch = min(CONV_CHUNK, s)

    def body(a_ref, g_ref, w_ref, b_ref, o_ref, down):
        rows = lax.broadcasted_iota(jnp.int32, (s, tc), 0)
        _store_shifted_down(down, a_ref[...] * jax.nn.sigmoid(g_ref[...]), rows)

        def chunk(ci, carry):
            r0 = pl.multiple_of(ci * ch, ch)
            acc = jnp.broadcast_to(b_ref[...], (ch, tc))
            for k in range(CONF_WIDTH):
                sft = CONF_WIDTH - 1 - k
                acc = acc + w_ref[pl.ds(k, 1), :] * down[sft % 8, pl.ds(CONV_PAD + r0 - (sft // 8) * 8, ch), :]
            o_ref[pl.ds(r0, ch), :] = acc
            return carry

        lax.fori_loop(0, s // ch, chunk, 0)

    return pl.pallas_call(
        body, grid=(nb,),
        in_specs=[_cols(s, tc), _cols(s, tc, nb), _cols(CONF_WIDTH, tc), _cols(1, tc)],
        out_specs=_cols(s, tc), out_shape=_sds((s, d), F32),
        scratch_shapes=[pltpu.VMEM((8, CONV_PAD + s, tc), F32)],
        compiler_params=_params(("parallel",)), name="conformer_glu_conv")(u, u, dw_w, dw_b)


def conformer_glu_conv_bwd(u, dw_w, dhc, tc=128):
    s, d2 = u.shape
    d = d2 // 2
    nb = d // tc
    ch = min(CONV_CHUNK, s)

    def body(a_ref, g_ref, w_ref, dhc_ref, du_ref, dbias_ref, dw_ref, db_ref, down, up, dw_acc, dh_buf):
        rows = lax.broadcasted_iota(jnp.int32, (s, tc), 0)
        a = a_ref[...]
        sg = jax.nn.sigmoid(g_ref[...])
        dhcv = dhc_ref[...]
        _store_shifted_down(down, a * sg, rows)
        _store_shifted_up(up, dhcv, rows)
        dw_acc[...] = jnp.zeros_like(dw_acc)

        def chunk(ci, carry):
            r0 = pl.multiple_of(ci * ch, ch)
            dc = dhc_ref[pl.ds(r0, ch), :]
            dh = jnp.zeros((ch, tc), F32)
            for k in range(CONF_WIDTH):
                sft = CONF_WIDTH - 1 - k
                a8, b = (sft // 8) * 8, sft % 8
                dh = dh + w_ref[pl.ds(k, 1), :] * up[b, pl.ds(r0 + a8, ch), :]
                prod = dc * down[b, pl.ds(CONV_PAD + r0 - a8, ch), :]
                dw_acc[k] += jnp.sum(prod.reshape(ch // 8, 8, tc), axis=0)
            dh_buf[pl.ds(r0, ch), :] = dh
            return carry

        lax.fori_loop(0, s // ch, chunk, 0)
        dh = dh_buf[...]
        da = dh * sg
        dgate = dh * a * sg * (1.0 - sg)
        du_ref[0] = da.astype(BF16)
        du_ref[1] = dgate.astype(BF16)
        dbias_ref[pl.ds(0, 1), :] = jnp.sum(da, axis=0, keepdims=True)
        dbias_ref[pl.ds(1, 1), :] = jnp.sum(dgate, axis=0, keepdims=True)
        db_ref[...] = jnp.sum(dhcv, axis=0, keepdims=True)
        for k in range(CONF_WIDTH):
            dw_ref[pl.ds(k, 1), :] = jnp.sum(dw_acc[k], axis=0, keepdims=True)

    return pl.pallas_call(
        body, grid=(nb,),
        in_specs=[_cols(s, tc), _cols(s, tc, nb), _cols(CONF_WIDTH, tc), _cols(s, tc)],
        out_specs=[pl.BlockSpec((2, s, tc), lambda i: (0, 0, i)), _cols(2, tc), _cols(CONF_WIDTH, tc), _cols(1, tc)],
        out_shape=[_sds((2, s, d), BF16), _sds((2, d), F32), _sds((CONF_WIDTH, d), F32), _sds((1, d), F32)],
        scratch_shapes=[pltpu.VMEM((8, CONV_PAD + s, tc), F32), pltpu.VMEM((8, CONV_PAD + s, tc), F32),
                        pltpu.VMEM((CONF_WIDTH + 1, 8, tc), F32), pltpu.VMEM((s, tc), F32)],
        compiler_params=_params(("parallel",)), name="conformer_glu_conv_bwd")(u, u, dw_w, dhc)


def conformer_norm_swish(hc, g, b, tm=256):
    s, d = hc.shape
    tm = _tile(s, tm)

    def body(h_ref, g_ref, b_ref, o_ref):
        n, _, _ = _layer_norm_rows(h_ref[...], g_ref[...], b_ref[...])
        o_ref[...] = (n * jax.nn.sigmoid(n)).astype(BF16)

    return pl.pallas_call(
        body, grid=(s // tm,), in_specs=[_rows(tm, d), _fix((1, d)), _fix((1, d))], out_specs=_rows(tm, d),
        out_shape=_sds((s, d), BF16), compiler_params=_params(("parallel",)), name="conformer_norm_swish")(hc, g, b)


def conformer_norm_swish_bwd(hc, g, b, ds, tm=256):
    s, d = hc.shape
    tm = _tile(s, tm)

    def body(h_ref, g_ref, b_ref, ds_ref, dh_ref, dg_ref, db_ref):
        n, nh, rstd = _layer_norm_rows(h_ref[...], g_ref[...], b_ref[...])
        sg = jax.nn.sigmoid(n)
        dn = ds_ref[...] * (sg * (1.0 + n * (1.0 - sg)))
        dnh = dn * g_ref[...]
        m1 = jnp.mean(dnh, axis=-1, keepdims=True)
        m2 = jnp.mean(dnh * nh, axis=-1, keepdims=True)
        dh_ref[...] = rstd * (dnh - m1 - nh * m2)
        _accumulate(dg_ref, jnp.sum(dn * nh, axis=0, keepdims=True))
        _accumulate(db_ref, jnp.sum(dn, axis=0, keepdims=True))

    return pl.pallas_call(
        body, grid=(s // tm,), in_specs=[_rows(tm, d), _fix((1, d)), _fix((1, d)), _rows(tm, d)],
        out_specs=[_rows(tm, d), _fix((1, d)), _fix((1, d))],
        out_shape=[_sds((s, d), F32), _sds((1, d), F32), _sds((1, d), F32)],
        compiler_params=_params(("arbitrary",)), name="conformer_norm_swish_bwd")(hc, g, b, ds)


def _swap_halves(x):
    lane = lax.broadcasted_iota(jnp.int32, x.shape, 1)
    return jnp.where(lane < QK_ROPE // 2, pltpu.roll(x, 128 - QK_ROPE // 2, 1), pltpu.roll(x, QK_ROPE // 2, 1))


def _rope(x, cf, sf):
    return x * cf + _swap_halves(x) * sf


def _unrope(dx, cf, sf):
    return dx * cf - _swap_halves(dx) * sf


def _rms_rows(x, g):
    r = lax.rsqrt(jnp.mean(x * x, axis=-1, keepdims=True) + RMS_EPS)
    return x * r, r


def mla_latents(t, g_q, g_kv, cf, sf, tm=256):
    s = t.shape[0]
    tm = _tile(s, tm)

    def body(t_ref, gq_ref, gkv_ref, cf_ref, sf_ref, cq_ref, ckv_ref, kpe_ref):
        xq, _ = _rms_rows(t_ref[:, 0:Q_LORA], gq_ref[...])
        cq_ref[...] = (xq * gq_ref[...]).astype(BF16)
        xkv, _ = _rms_rows(t_ref[:, Q_LORA:Q_LORA + KV_LORA], gkv_ref[...])
        ckv_ref[...] = (xkv * gkv_ref[...]).astype(BF16)
        kpe_ref[...] = _rope(t_ref[:, Q_LORA + KV_LORA:], cf_ref[...], sf_ref[...]).astype(BF16)

    w = Q_LORA + KV_LORA + 128
    return pl.pallas_call(
        body, grid=(s // tm,),
        in_specs=[_rows(tm, w), _fix((1, Q_LORA)), _fix((1, KV_LORA)), _rows(tm, 128), _rows(tm, 128)],
        out_specs=[_rows(tm, Q_LORA), _rows(tm, KV_LORA), _rows(tm, 128)],
        out_shape=[_sds((s, Q_LORA), BF16), _sds((s, KV_LORA), BF16), _sds((s, 128), BF16)],
        compiler_params=_params(("parallel",)), name="mla_latents")(t, g_q, g_kv, cf, sf)


def mla_latents_bwd(t, g_q, g_kv, cf, sf, dcq, dckv, dkpe, tm=256):
    s = t.shape[0]
    tm = _tile(s, tm)
    w = Q_LORA + KV_LORA + 128

    def rms_bwd(x, g, dy):
        xh, r = _rms_rows(x, g)
        dxh = dy * g
        return r * (dxh - xh * jnp.mean(dxh * xh, axis=-1, keepdims=True)), jnp.sum(dy * xh, axis=0, keepdims=True)

    def body(t_ref, gq_ref, gkv_ref, cf_ref, sf_ref, dcq_ref, dckv_ref, dkpe_ref, dt_ref, dgq_ref, dgkv_ref):
        dxq, dgq = rms_bwd(t_ref[:, 0:Q_LORA], gq_ref[...], dcq_ref[...])
        dxkv, dgkv = rms_bwd(t_ref[:, Q_LORA:Q_LORA + KV_LORA], gkv_ref[...], dckv_ref[...])
        dt_ref[:, 0:Q_LORA] = dxq.astype(BF16)
        dt_ref[:, Q_LORA:Q_LORA + KV_LORA] = dxkv.astype(BF16)
        dt_ref[:, Q_LORA + KV_LORA:] = _unrope(dkpe_ref[...], cf_ref[...], sf_ref[...]).astype(BF16)
        _accumulate(dgq_ref, dgq)
        _accumulate(dgkv_ref, dgkv)

    return pl.pallas_call(
        body, grid=(s // tm,),
        in_specs=[_rows(tm, w), _fix((1, Q_LORA)), _fix((1, KV_LORA)), _rows(tm, 128), _rows(tm, 128),
                  _rows(tm, Q_LORA), _rows(tm, KV_LORA), _rows(tm, 128)],
        out_specs=[_rows(tm, w), _fix((1, Q_LORA)), _fix((1, KV_LORA))],
        out_shape=[_sds((s, w), BF16), _sds((1, Q_LORA), F32), _sds((1, KV_LORA), F32)],
        compiler_params=_params(("arbitrary",)), name="mla_latents_bwd")(t, g_q, g_kv, cf, sf, dcq, dckv, dkpe)


def mla_queries(cq, w_uq, cf, sf, tm=512):
    s = cq.shape[0]
    tm = _tile(s, tm)

    def epi(acc, e, o):
        o[0][:, 0:QK_NOPE] = acc[:, 0:QK_NOPE].astype(BF16)
        o[0][:, QK_NOPE:] = _rope(acc[:, QK_NOPE:], e[0][...], e[1][...]).astype(BF16)

    return mm_nn("mla_queries", cq, w_uq, tm, HEAD_PAD, Q_LORA, epi, [_sds((s, N_HEADS * HEAD_PAD), BF16)],
                 [_ij(tm, HEAD_PAD)], [cf, sf], [_i0(tm, 128), _i0(tm, 128)])[0]


def mla_keys(ckv, w_uk, kpe, tm=512):
    s = ckv.shape[0]
    tm = _tile(s, tm)

    def epi(acc, e, o):
        o[0][:, 0:QK_NOPE] = acc.astype(BF16)
        o[0][:, QK_NOPE:] = e[0][...]

    return mm_nn("mla_keys", ckv, w_uk, tm, QK_NOPE, KV_LORA, epi, [_sds((s, N_HEADS * HEAD_PAD), BF16)],
                 [_ij(tm, HEAD_PAD)], [kpe], [_i0(tm, 128)])[0]


def _masked_scores(q, k, qi, tq, kv):
    sc = lax.dot_general(q, k, NT, preferred_element_type=F32) * ATTN_SCALE
    row = lax.broadcasted_iota(jnp.int32, (tq, kv), 0) + qi * tq
    col = lax.broadcasted_iota(jnp.int32, (tq, kv), 1)
    ok = lax.shift_right_logical(col, CHUNK_SHIFT) <= lax.shift_right_logical(row, CHUNK_SHIFT)
    return jnp.where(ok, sc, -1e30)


def attention(q, k, v, tq=512):
    s = q.shape[0]
    tq = _tile(s, tq)
    nq = s // tq

    def body(q_ref, k_ref, v_ref, o_ref):
        for qi in range(nq):
            kv = (qi + 1) * tq
            sc = _masked_scores(q_ref[pl.ds(qi * tq, tq), :], k_ref[pl.ds(0, kv), :], qi, tq, kv)
            p = jnp.exp(sc - jnp.max(sc, axis=-1, keepdims=True))
            o = lax.dot_general(p.astype(BF16), v_ref[pl.ds(0, kv), :], NN, preferred_element_type=F32)
            o_ref[pl.ds(qi * tq, tq), :] = (o / jnp.sum(p, axis=-1, keepdims=True)).astype(BF16)

    hq = pl.BlockSpec((s, HEAD_PAD), lambda h: (0, h))
    hv = pl.BlockSpec((s, V_HEAD), lambda h: (0, h))
    return pl.pallas_call(
        body, grid=(N_HEADS,), in_specs=[hq, hq, hv], out_specs=hv, out_shape=_sds((s, N_HEADS * V_HEAD), BF16),
        compiler_params=_params(("parallel",)), name="attention")(q, k, v)


def attention_bwd(q, k, v, do, tq=512):
    s = q.shape[0]
    tq = _tile(s, tq)
    nq = s // tq

    def body(q_ref, k_ref, v_ref, do_ref, dq_ref, dk_ref, dv_ref, dk_acc, dv_acc):
        dk_acc[...] = jnp.zeros_like(dk_acc)
        dv_acc[...] = jnp.zeros_like(dv_acc)
        for qi in range(nq):
            kv = (qi + 1) * tq
            qt = q_ref[pl.ds(qi * tq, tq), :]
            kt = k_ref[pl.ds(0, kv), :]
            dot = do_ref[pl.ds(qi * tq, tq), :]
            sc = _masked_scores(qt, kt, qi, tq, kv)
            p = jnp.exp(sc - jnp.max(sc, axis=-1, keepdims=True))
            p = p / jnp.sum(p, axis=-1, keepdims=True)
            dp = lax.dot_general(dot, v_ref[pl.ds(0, kv), :], NT, preferred_element_type=F32)
            delta = jnp.sum(p * dp, axis=-1, keepdims=True)
            ds = (p * (dp - delta) * ATTN_SCALE).astype(BF16)
            dq_ref[pl.ds(qi * tq, tq), :] = lax.dot_general(ds, kt, NN, preferred_element_type=F32).astype(BF16)
            dk_acc[pl.ds(0, kv), :] += lax.dot_general(ds, qt, TN, preferred_element_type=F32)
            dv_acc[pl.ds(0, kv), :] += lax.dot_general(p.astype(BF16), dot, TN, preferred_element_type=F32)
        dk_ref[...] = dk_acc[...].astype(BF16)
        dv_ref[...] = dv_acc[...].astype(BF16)

    hq = pl.BlockSpec((s, HEAD_PAD), lambda h: (0, h))
    hv = pl.BlockSpec((s, V_HEAD), lambda h: (0, h))
    return pl.pallas_call(
        body, grid=(N_HEADS,), in_specs=[hq, hq, hv, hv], out_specs=[hq, hq, hv],
        out_shape=[_sds((s, N_HEADS * HEAD_PAD), BF16), _sds((s, N_HEADS * HEAD_PAD), BF16),
                   _sds((s, N_HEADS * V_HEAD), BF16)],
        scratch_shapes=[pltpu.VMEM((s, HEAD_PAD), F32), pltpu.VMEM((s, V_HEAD), F32)],
        compiler_params=_params(("parallel",)), name="attention_bwd")(q, k, v, do)


def mla_unrope_grads(dq, dk, cf, sf, tm=256):
    s = dq.shape[0]
    tm = _tile(s, tm)

    def body(dq_ref, dk_ref, cf_ref, sf_ref, dql_ref, dkn_ref, dkpe_ref):
        cfv, sfv = cf_ref[...], sf_ref[...]
        dkpe = jnp.zeros((tm, 128), F32)
        for h in range(N_HEADS):
            lo = h * HEAD_PAD
            dql_ref[:, lo:lo + QK_NOPE] = dq_ref[:, lo:lo + QK_NOPE]
            dql_ref[:, lo + QK_NOPE:lo + HEAD_PAD] = _unrope(
                dq_ref[:, lo + QK_NOPE:lo + HEAD_PAD].astype(F32), cfv, sfv).astype(BF16)
            dkn_ref[:, h * QK_NOPE:(h + 1) * QK_NOPE] = dk_ref[:, lo:lo + QK_NOPE]
            dkpe = dkpe + dk_ref[:, lo + QK_NOPE:lo + HEAD_PAD].astype(F32)
        dkpe_ref[...] = dkpe

    wq = N_HEADS * HEAD_PAD
    return pl.pallas_call(
        body, grid=(s // tm,), in_specs=[_rows(tm, wq), _rows(tm, wq), _rows(tm, 128), _rows(tm, 128)],
        out_specs=[_rows(tm, wq), _rows(tm, N_HEADS * QK_NOPE), _rows(tm, 128)],
        out_shape=[_sds((s, wq), BF16), _sds((s, N_HEADS * QK_NOPE), BF16), _sds((s, 128), F32)],
        compiler_params=_params(("parallel",)), name="mla_unrope_grads")(dq, dk, cf, sf)


ANY = pl.BlockSpec(memory_space=pl.ANY)
GATHER_ID = 1
CHIP_EXCHANGE_ID = 2
PAIR_ID = 3
ALL_ID = 4


def _nbytes(a):
    return a.size * a.dtype.itemsize


def _copy_cost(operand_bytes, sent_fraction):
    sent = int(operand_bytes * sent_fraction)
    return pl.CostEstimate(flops=0, transcendentals=0, bytes_accessed=2 * sent, remote_bytes_transferred=sent)


def _handshake(peers):
    barrier = pltpu.get_barrier_semaphore()
    for peer in peers:
        pl.semaphore_signal(barrier, inc=1, device_id=peer, device_id_type=MESH)
    pl.semaphore_wait(barrier, len(peers))


def _place():
    x, y, c = lax.axis_index("x"), lax.axis_index("y"), lax.axis_index("c")
    chips = [(1 - x, y), (x, 1 - y), (1 - x, 1 - y)]
    return x, y, c, chips


def _half(ref, hc, axis=0):
    n = ref.shape[axis] // 2
    idx = (slice(None),) * axis + (pl.ds(hc * n, n),)
    return ref.at[idx]


def gather_shards(name, tensors, by_columns=()):
    nt = len(tensors)

    def body(*refs):
        a, g = refs[:nt], refs[nt:2 * nt]
        send, recv = refs[2 * nt:]
        x, y, c, _ = _place()
        q = 2 * x + y
        sib, xn, yn = (x, y, 1 - c), (1 - x, y, c), (x, 1 - y, c)
        q_xn, q_yn, q_diag = 2 * (1 - x) + y, 2 * x + 1 - y, 2 * (1 - x) + 1 - y
        _handshake([sib, xn, yn])

        def whole(t, p):
            if t in by_columns:
                n = a[t].shape[1]
                return g[t].at[:, pl.ds(p * n, n)]
            return g[t].at[p]

        def part(t, p, hc, quarter=None):
            rows = a[t].shape[0]
            if quarter is None:
                return whole(t, p).at[pl.ds(hc * (rows // 2), rows // 2)]
            return whole(t, p).at[pl.ds(hc * (rows // 2) + quarter * (rows // 4), rows // 4)]

        def rc(t, k, src, dst, to):
            return pltpu.make_async_remote_copy(src_ref=src, dst_ref=dst, send_sem=send.at[t, k], recv_sem=recv.at[t, k],
                                                device_id=to, device_id_type=MESH)

        sent = []

        def go(cp):
            cp.start()
            sent.append(cp)

        def landed(t, k, piece, frm):
            rc(t, k, piece, piece, frm).wait_recv()
            return piece

        for t in range(nt):
            go(rc(t, 8, a[t], whole(t, q), sib))
            mine = _half(a[t], c)
            go(rc(t, 0, mine, part(t, q, c), xn))
            go(rc(t, 1, mine, part(t, q, c), yn))
        for t in range(nt):
            from_y = landed(t, 1, part(t, q_yn, c), yn)
            go(rc(t, 2, part(t, q_yn, c, 0), part(t, q_yn, c, 0), xn))
            go(rc(t, 5, from_y, from_y, sib))
            from_x = landed(t, 0, part(t, q_xn, c), xn)
            go(rc(t, 3, part(t, q_xn, c, 1), part(t, q_xn, c, 1), yn))
            go(rc(t, 4, from_x, from_x, sib))
        for t in range(nt):
            for k, frm in ((2, xn), (3, yn)):
                piece = landed(t, k, part(t, q_diag, c, k - 2), frm)
                go(rc(t, 4 + k, piece, piece, sib))
        for t in range(nt):
            landed(t, 4, part(t, q_xn, 1 - c), sib)
            landed(t, 5, part(t, q_yn, 1 - c), sib)
            landed(t, 6, part(t, q_diag, 1 - c, 0), sib)
            landed(t, 7, part(t, q_diag, 1 - c, 1), sib)
            landed(t, 8, whole(t, q), sib)
        for cp in sent:
            cp.wait_send()

    return pl.kernel(
        body, name=name,
        out_type=[_sds((a.shape[0], N_CHIPS * a.shape[1]) if t in by_columns else (N_CHIPS,) + a.shape, a.dtype)
                  for t, a in enumerate(tensors)],
        mesh=plsc.ScalarSubcoreMesh(axis_name="sequencer", num_cores=1),
        scratch_types=[pltpu.SemaphoreType.DMA((nt, 9)), pltpu.SemaphoreType.DMA((nt, 9))],
        cost_estimate=_copy_cost(sum(_nbytes(a) for a in tensors), 4),
        compiler_params=pltpu.CompilerParams(collective_id=GATHER_ID))(*tensors)


def pair_exchange(name, grads, on_sequencer):
    nt = len(grads)

    def body(*refs):
        g, theirs = refs[:nt], refs[nt:2 * nt]
        send, recv = refs[2 * nt:]
        x, y, c, _ = _place()
        if on_sequencer:
            _handshake([(x, y, 1 - c)])
        cps = []
        for t in range(nt):
            cp = pltpu.make_async_remote_copy(src_ref=_half(g[t], 1 - c, 1), dst_ref=theirs[t], send_sem=send.at[t],
                                              recv_sem=recv.at[t], device_id=(x, y, 1 - c), device_id_type=MESH)
            cp.start()
            cps.append(cp)
        for cp in cps:
            cp.wait()

    if not on_sequencer:
        return pl.pallas_call(
            body, in_specs=[ANY] * nt, out_specs=[ANY] * nt,
            out_shape=[_sds((N_CHIPS, a.shape[1] // 2, a.shape[2]), a.dtype) for a in grads],
            scratch_shapes=[pltpu.SemaphoreType.DMA((nt,)), pltpu.SemaphoreType.DMA((nt,))],
            name=name)(*grads)
    return pl.kernel(
        body, name=name, out_type=[_sds((N_CHIPS, a.shape[1] // 2, a.shape[2]), a.dtype) for a in grads],
        mesh=plsc.ScalarSubcoreMesh(axis_name="sequencer", num_cores=1),
        scratch_types=[pltpu.SemaphoreType.DMA((nt,)), pltpu.SemaphoreType.DMA((nt,))],
        cost_estimate=_copy_cost(sum(_nbytes(a) for a in grads), 0.5),
        compiler_params=pltpu.CompilerParams(collective_id=PAIR_ID))(*grads)


def chip_exchange(name, parts):
    nt = len(parts)

    def body(*refs):
        a, r = refs[:nt], refs[nt:2 * nt]
        send, recv = refs[2 * nt:]
        x, y, c, chips = _place()
        _handshake([(*chip, c) for chip in chips])
        cps = []
        for t in range(nt):
            for j, chip in enumerate(chips):
                cp = pltpu.make_async_remote_copy(
                    src_ref=a[t].at[2 * chip[0] + chip[1]], dst_ref=r[t].at[j], send_sem=send.at[t, j],
                    recv_sem=recv.at[t, j], device_id=(*chip, c), device_id_type=MESH)
                cp.start()
                cps.append(cp)
        for cp in cps:
            cp.wait()

    return pl.kernel(
        body, name=name, out_type=[_sds((N_CHIPS - 1,) + a.shape[1:], a.dtype) for a in parts],
        mesh=plsc.ScalarSubcoreMesh(axis_name="sequencer", num_cores=1),
        scratch_types=[pltpu.SemaphoreType.DMA((nt, 3)), pltpu.SemaphoreType.DMA((nt, 3))],
        cost_estimate=_copy_cost(sum(_nbytes(a) for a in parts), 0.75),
        compiler_params=pltpu.CompilerParams(collective_id=CHIP_EXCHANGE_ID))(*parts)


def pair_share(name, halves):
    nt = len(halves)

    def body(*refs):
        h, other = refs[:nt], refs[nt:2 * nt]
        send, recv = refs[2 * nt:]
        x, y, c, _ = _place()
        _handshake([(x, y, 1 - c)])
        cps = []
        for t in range(nt):
            cp = pltpu.make_async_remote_copy(src_ref=h[t], dst_ref=other[t], send_sem=send.at[t], recv_sem=recv.at[t],
                                              device_id=(x, y, 1 - c), device_id_type=MESH)
            cp.start()
            cps.append(cp)
        for cp in cps:
            cp.wait()

    return pl.kernel(
        body, name=name, out_type=[_sds(a.shape, a.dtype) for a in halves],
        mesh=plsc.ScalarSubcoreMesh(axis_name="sequencer", num_cores=1),
        scratch_types=[pltpu.SemaphoreType.DMA((nt,)), pltpu.SemaphoreType.DMA((nt,))],
        cost_estimate=_copy_cost(sum(_nbytes(a) for a in halves), 1),
        compiler_params=pltpu.CompilerParams(collective_id=PAIR_ID))(*halves)


def pack_rows(name, parts, rows):
    cdim = parts[0].shape[1]
    n = len(parts)
    vm = pl.BlockSpec(memory_space=pltpu.VMEM)

    def pack(*refs):
        p, o_ref = refs[:n], refs[n]
        at = 0
        for ref in p:
            o_ref[pl.ds(at, ref.shape[0]), :] = ref[...]
            at += ref.shape[0]
        o_ref[pl.ds(at, rows - at), :] = jnp.zeros((rows - at, cdim), F32)

    return pl.pallas_call(pack, in_specs=[vm] * n, out_specs=vm, out_shape=_sds((rows, cdim), F32), name=name)(*parts)


def all_reduce_small(parts, rows):
    cdim = parts[0].shape[1]
    vm = pl.BlockSpec(memory_space=pltpu.VMEM)
    mine = pack_rows("small_pack", parts, rows)

    def exchange(mine_ref, buf, send, recv, lsem):
        x, y, c, _ = _place()
        me = 4 * x + 2 * y + c
        peers = [(x ^ (k >> 2), y ^ ((k >> 1) & 1), c ^ (k & 1)) for k in range(1, 8)]
        _handshake(peers)
        own = pltpu.make_async_copy(mine_ref, buf.at[me], lsem)
        own.start()
        cps = []
        for k, to in enumerate(peers):
            cp = pltpu.make_async_remote_copy(src_ref=mine_ref, dst_ref=buf.at[me], send_sem=send.at[k], recv_sem=recv.at[k],
                                              device_id=to, device_id_type=MESH)
            cp.start()
            cps.append(cp)
        for k, (px, py, pc) in enumerate(peers):
            pltpu.make_async_remote_copy(src_ref=mine_ref, dst_ref=buf.at[4 * px + 2 * py + pc], send_sem=send.at[k],
                                         recv_sem=recv.at[k], device_id=(x, y, c), device_id_type=MESH).wait_recv()
        for cp in cps:
            cp.wait_send()
        own.wait()

    landed = pl.kernel(
        exchange, name="small_exchange", out_type=_sds((8, rows, cdim), F32),
        mesh=plsc.ScalarSubcoreMesh(axis_name="sequencer", num_cores=1),
        scratch_types=[pltpu.SemaphoreType.DMA((7,)), pltpu.SemaphoreType.DMA((7,)), pltpu.SemaphoreType.DMA],
        cost_estimate=_copy_cost(rows * cdim * 4, 7),
        compiler_params=pltpu.CompilerParams(collective_id=ALL_ID))(mine)

    def total(buf, o_ref):
        acc = buf[0]
        for d in range(1, 8):
            acc = acc + buf[d]
        o_ref[...] = acc

    return pl.pallas_call(total, in_specs=[vm], out_specs=vm, out_shape=_sds((rows, cdim), F32), name="small_sum")(landed)


def pair_sum(g, theirs, core, tm=256):
    _, r, c = g.shape
    tm = _tile(r // 2, tm)
    nh = r // 2 // tm

    def body(core_ref, a_ref, b_ref, o_ref):
        o_ref[...] = (a_ref[...].astype(F32) + b_ref[...].astype(F32)).astype(BF16)

    blk = (N_CHIPS, tm, c)
    return pl.pallas_call(
        body, grid_spec=pltpu.PrefetchScalarGridSpec(
            num_scalar_prefetch=1, grid=(nh,),
            in_specs=[pl.BlockSpec(blk, lambda i, cr: (0, cr[0] * nh + i, 0)), pl.BlockSpec(blk, lambda i, cr: (0, i, 0))],
            out_specs=pl.BlockSpec(blk, lambda i, cr: (0, i, 0))),
        out_shape=_sds(theirs.shape, BF16), compiler_params=_params(("parallel",)), name="pair_sum")(core, g, theirs)


def chip_sum(own, landed, chip, stack, layer, layers, tm=256):
    _, r, c = own.shape
    tm = _tile(r, tm)

    def body(chip_ref, own_ref, l_ref, *rest):
        acc = own_ref[...].astype(F32)
        for j in range(N_CHIPS - 1):
            acc = acc + l_ref[j].astype(F32)
        rest[-1][...] = acc

    in_specs = [pl.BlockSpec((None, tm, c), lambda i, qr: (qr[0], i, 0)),
                pl.BlockSpec((N_CHIPS - 1, tm, c), lambda i, qr: (0, i, 0))]
    args = [chip, own, landed]
    if stack is not None:
        in_specs.append(ANY)
        args.append(stack)
    return pl.pallas_call(
        body, grid_spec=pltpu.PrefetchScalarGridSpec(
            num_scalar_prefetch=1, grid=(r // tm,), in_specs=in_specs,
            out_specs=pl.BlockSpec((None, tm, c), lambda i, qr: (layer, i, 0))),
        out_shape=_sds((layers, r, c), F32), input_output_aliases={3: 0} if stack is not None else {},
        compiler_params=_params(("parallel",)), name="chip_sum")(*args)


def _adamw_math(w, g, m, v):
    bc1 = 1.0 - ADAM_B1 ** ADAM_STEP
    bc2 = 1.0 - ADAM_B2 ** ADAM_STEP
    nm = ADAM_B1 * m + (1.0 - ADAM_B1) * g
    nv = ADAM_B2 * v + (1.0 - ADAM_B2) * (g * g)
    return -ADAM_LR * ((nm / bc1) / (jnp.sqrt(nv / bc2) + ADAM_EPS) + ADAM_WD * w), nm, nv


def vector_update(red, chip, ws, ms, vs, where):
    n = len(ws)
    dd = red.shape[1]

    def body(chip_ref, red_ref, *refs):
        w_r, m_r, v_r = refs[0:n], refs[n:2 * n], refs[2 * n:3 * n]
        g_o, d_o, m_o, v_o = (refs[(3 + k) * n:(4 + k) * n] for k in range(4))
        q = chip_ref[0]

        def chip_block(val, width):
            out = val[:, 0:width]
            for p in range(1, val.shape[1] // width):
                out = jnp.where(q == p, val[:, p * width:(p + 1) * width], out)
            return out

        for k in range(n):
            for idx, r0, nr, cols in where[k]:
                width = w_r[k].shape[-1]
                if cols == "chip" and width * N_CHIPS != dd:
                    g = chip_block(jnp.concatenate([red_ref[pl.ds(r0 + j, 1), :] for j in range(nr)], axis=1), width)
                else:
                    g = red_ref[pl.ds(r0, nr), :]
                    g = chip_block(g, width) if cols == "chip" else g if cols == "all" else g[:, 0:cols]
                delta, nm, nv = _adamw_math(w_r[k][idx], g, m_r[k][idx], v_r[k][idx])
                g_o[k][idx] = g
                d_o[k][idx] = delta
                m_o[k][idx] = nm
                v_o[k][idx] = nv

    vm = pl.BlockSpec(memory_space=pltpu.VMEM)
    outs = pl.pallas_call(
        body, in_specs=[pl.BlockSpec(memory_space=pltpu.SMEM), vm] + [vm] * (3 * n), out_specs=[vm] * (4 * n),
        out_shape=[_sds(w.shape, F32) for w in ws] * 4, name="vector_update")(chip, red, *ws, *ms, *vs)
    return [outs[k * n:(k + 1) * n] for k in range(4)]


def adamw_joined(w, m, v, g_mine, g_theirs, core, tm=512):
    nl, r, c = w.shape
    tm = _tile(r // 2, tm)
    nh = r // 2 // tm
    bc1 = 1.0 - ADAM_B1 ** ADAM_STEP
    bc2 = 1.0 - ADAM_B2 ** ADAM_STEP

    def body(core_ref, w_ref, m_ref, v_ref, gm_ref, gt_ref, g_ref, d_ref, nm_ref, nv_ref):
        mine = (pl.program_id(1) // nh) == core_ref[0]
        gv = jnp.where(mine, gm_ref[...], gt_ref[...])
        nm = ADAM_B1 * m_ref[...] + (1.0 - ADAM_B1) * gv
        nv = ADAM_B2 * v_ref[...] + (1.0 - ADAM_B2) * (gv * gv)
        g_ref[...] = gv
        d_ref[...] = -ADAM_LR * ((nm / bc1) / (jnp.sqrt(nv / bc2) + ADAM_EPS) + ADAM_WD * w_ref[...])
        nm_ref[...] = nm
        nv_ref[...] = nv

    full = pl.BlockSpec((None, tm, c), lambda l, i, cr: (l, i, 0))
    half = pl.BlockSpec((None, tm, c), lambda l, i, cr: (l, i % nh, 0))
    return pl.pallas_call(
        body, grid_spec=pltpu.PrefetchScalarGridSpec(
            num_scalar_prefetch=1, grid=(nl, r // tm), in_specs=[full, full, full, half, half], out_specs=[full] * 4),
        out_shape=[_sds((nl, r, c), F32)] * 4, compiler_params=_params(("parallel", "parallel")),
        name="adamw_joined")(core, w, m, v, g_mine, g_theirs)


def adamw(w, g, m, v, tm=256):
    shape = w.shape
    c = shape[-1]
    r = w.size // c
    tm = _tile(r, tm)
    bc1 = 1.0 - ADAM_B1 ** ADAM_STEP
    bc2 = 1.0 - ADAM_B2 ** ADAM_STEP

    def body(w_ref, g_ref, m_ref, v_ref, d_ref, nm_ref, nv_ref):
        gv = g_ref[...]
        nm = ADAM_B1 * m_ref[...] + (1.0 - ADAM_B1) * gv
        nv = ADAM_B2 * v_ref[...] + (1.0 - ADAM_B2) * (gv * gv)
        d_ref[...] = -ADAM_LR * ((nm / bc1) / (jnp.sqrt(nv / bc2) + ADAM_EPS) + ADAM_WD * w_ref[...])
        nm_ref[...] = nm
        nv_ref[...] = nv

    outs = pl.pallas_call(
        body, grid=(r // tm,), in_specs=[_rows(tm, c)] * 4, out_specs=[_rows(tm, c)] * 3,
        out_shape=[_sds((r, c), F32)] * 3, compiler_params=_params(("parallel",)), name="adamw")(
            w.reshape(r, c), g.reshape(r, c), m.reshape(r, c), v.reshape(r, c))
    return [o.reshape(shape) for o in outs]


WEIGHTS = ['sc_w_in', 'sc_conv_w', 'sc_w_out', 'mla_w_dq', 'mla_g_q', 'mla_w_uq', 'mla_w_dkv', 'mla_g_kv', 'mla_w_uk',
           'mla_w_uv', 'mla_w_o', 'cf_w_pw1', 'cf_b_pw1', 'cf_dw_w', 'cf_dw_b', 'cf_norm_g', 'cf_norm_b', 'cf_w_pw2',
           'cf_b_pw2', 'ff_w1', 'ff_w2', 'ln_mix_g', 'ln_mix_b', 'ln_ff_g', 'ln_ff_b']
ARGS = ['x'] + WEIGHTS + ['loss_target'] + ['m_' + n for n in WEIGHTS] + ['v_' + n for n in WEIGHTS]


def _sq_relu(h):
    r = jnp.maximum(h, jnp.zeros_like(h))
    return r * r


def _mlp_forward(i, x, xb, w1, w2, g, b):
    hb = mm_plain_nn(f"mlp{i}_up", xb, w1, BF16, tn=1024)
    y, yb, xh, rstd = mm_residual_ln(f"mlp{i}_down_ln", hb, w2, x, g, b, tk=2048, a_fn=_sq_relu)
    return (y, yb), dict(xb=xb, hb=hb, xh=xh, rstd=rstd, g=g)


def _mlp_backward(i, dy, sv, w1, w2, dw1, dw2, reduce_after):
    s = dy.shape[0]
    dr, drb, dg, db, _ = ln_backward(f"mlp{i}_ln_bwd", dy, sv["xh"], sv["rstd"], sv["g"])
    tm, tn = _tile(s, 1024), 1024

    def epi(acc, e, o):
        o[0][...] = (acc * (2.0 * jnp.maximum(e[0][...].astype(F32), 0.0))).astype(BF16)

    dhb = mm_nt(f"mlp{i}_down_bwd", drb, w2, s, tm, tn, 1024, epi, [_sds((s, w2.k), BF16)], [_ij(tm, tn)],
                [sv["hb"]], [_ij(tm, tn)])[0]
    g_w2 = mm_tn(f"mlp{i}_dw2", sv["hb"], drb, dw2, s, 512, 1024, a_fn=_sq_relu)
    g_w1 = mm_tn(f"mlp{i}_dw1", sv["xb"], dhb, dw1, s, 1024, 512)
    dhb = reduce_after(dhb, {f"w1_{i}": g_w1, f"w2_{i}": g_w2})
    dx = mm_plain_nt(f"mlp{i}_up_bwd", dhb, w1, F32, tn=1024, tk=2048, add=dr, add_scale=ALPHA)
    return dx, dg, db


def kernel(x, sc_w_in, sc_conv_w, sc_w_out, mla_w_dq, mla_g_q, mla_w_uq, mla_w_dkv, mla_g_kv, mla_w_uk, mla_w_uv, mla_w_o, cf_w_pw1, cf_b_pw1, cf_dw_w, cf_dw_b, cf_norm_g, cf_norm_b, cf_w_pw2, cf_b_pw2, ff_w1, ff_w2, ln_mix_g, ln_mix_b, ln_ff_g, ln_ff_b, loss_target, m_sc_w_in, m_sc_conv_w, m_sc_w_out, m_mla_w_dq, m_mla_g_q, m_mla_w_uq, m_mla_w_dkv, m_mla_g_kv, m_mla_w_uk, m_mla_w_uv, m_mla_w_o, m_cf_w_pw1, m_cf_b_pw1, m_cf_dw_w, m_cf_dw_b, m_cf_norm_g, m_cf_norm_b, m_cf_w_pw2, m_cf_b_pw2, m_ff_w1, m_ff_w2, m_ln_mix_g, m_ln_mix_b, m_ln_ff_g, m_ln_ff_b, v_sc_w_in, v_sc_conv_w, v_sc_w_out, v_mla_w_dq, v_mla_g_q, v_mla_w_uq, v_mla_w_dkv, v_mla_g_kv, v_mla_w_uk, v_mla_w_uv, v_mla_w_o, v_cf_w_pw1, v_cf_b_pw1, v_cf_dw_w, v_cf_dw_b, v_cf_norm_g, v_cf_norm_b, v_cf_w_pw2, v_cf_b_pw2, v_ff_w1, v_ff_w2, v_ln_mix_g, v_ln_mix_b, v_ln_ff_g, v_ln_ff_b):
    given = dict(zip(ARGS, (x, sc_w_in, sc_conv_w, sc_w_out, mla_w_dq, mla_g_q, mla_w_uq, mla_w_dkv, mla_g_kv, mla_w_uk, mla_w_uv, mla_w_o, cf_w_pw1, cf_b_pw1, cf_dw_w, cf_dw_b, cf_norm_g, cf_norm_b, cf_w_pw2, cf_b_pw2, ff_w1, ff_w2, ln_mix_g, ln_mix_b, ln_ff_g, ln_ff_b, loss_target, m_sc_w_in, m_sc_conv_w, m_sc_w_out, m_mla_w_dq, m_mla_g_q, m_mla_w_uq, m_mla_w_dkv, m_mla_g_kv, m_mla_w_uk, m_mla_w_uv, m_mla_w_o, m_cf_w_pw1, m_cf_b_pw1, m_cf_dw_w, m_cf_dw_b, m_cf_norm_g, m_cf_norm_b, m_cf_w_pw2, m_cf_b_pw2, m_ff_w1, m_ff_w2, m_ln_mix_g, m_ln_mix_b, m_ln_ff_g, m_ln_ff_b, v_sc_w_in, v_sc_conv_w, v_sc_w_out, v_mla_w_dq, v_mla_g_q, v_mla_w_uq, v_mla_w_dkv, v_mla_g_kv, v_mla_w_uk, v_mla_w_uv, v_mla_w_o, v_cf_w_pw1, v_cf_b_pw1, v_cf_dw_w, v_cf_dw_b, v_cf_norm_g, v_cf_norm_b, v_cf_w_pw2, v_cf_b_pw2, v_ff_w1, v_ff_w2, v_ln_mix_g, v_ln_mix_b, v_ln_ff_g, v_ln_ff_b)))
    s, d = x.shape[1], x.shape[2]
    d_ff = 4 * d
    dq4 = d // N_CHIPS
    xq = lax.axis_index("x") * 2 + lax.axis_index("y")

    w_dkv_pad = jnp.pad(mla_w_dkv[0], ((0, 0), (0, 128 - QK_ROPE)))
    w_uq_pad = jnp.pad(mla_w_uq[0].reshape(Q_LORA, 2, QK_NOPE + QK_ROPE), ((0, 0), (0, 0), (0, HEAD_PAD - QK_NOPE - QK_ROPE)))
    small = pack_rows("vector_weights_pack", [
        sc_conv_w.reshape(2 * SC_WIDTH, dq4), cf_b_pw1.reshape(2, dq4), cf_dw_w[0], cf_dw_b, cf_norm_g, cf_norm_b,
        cf_b_pw2], 64)
    mlp_w = lambda i: [ff_w1[i].astype(BF16), ff_w2[i].astype(BF16)]
    g_in, g_out, g_w1, g_w2 = [None] * 2, [None] * 2, [None] * DEPTH, [None] * DEPTH
    g_in[0], g_out[0], g_small = gather_shards(
        "gather_mixer0", [sc_w_in[0].astype(BF16), sc_w_out[0].astype(BF16), small], by_columns=(0,))
    (g_w1[0],) = gather_shards("gather_up0", [ff_w1[0].astype(BF16)], by_columns=(0,))
    (g_w2[0],) = gather_shards("gather_down0", [ff_w2[0].astype(BF16)])
    g_dqkv, g_uq, g_uk, g_uv, g_o = gather_shards("gather_mixer1", [
        jnp.concatenate([mla_w_dq[0], w_dkv_pad], axis=1).astype(BF16),
        w_uq_pad.reshape(Q_LORA, 2 * HEAD_PAD).astype(BF16),
        mla_w_uk.reshape(KV_LORA // N_CHIPS, N_HEADS * QK_NOPE).astype(BF16),
        mla_w_uv.reshape(KV_LORA // N_CHIPS, N_HEADS * V_HEAD).astype(BF16), mla_w_o[0].astype(BF16)], by_columns=(1,))
    g_w1[1], g_w2[1] = gather_shards("gather_mlp1", mlp_w(1), by_columns=(0,))
    g_pw1, g_pw2, g_w1[2], g_w2[2] = gather_shards(
        "gather_layer2", [cf_w_pw1[0].astype(BF16), cf_w_pw2[0].astype(BF16)] + mlp_w(2), by_columns=(0, 2))
    g_in[1], g_out[1], g_w1[3], g_w2[3] = gather_shards(
        "gather_layer3", [sc_w_in[1].astype(BF16), sc_w_out[1].astype(BF16)] + mlp_w(3), by_columns=(0, 2))

    wd_t = Q_LORA + KV_LORA + 128
    w_in = [Stk("full", d, 3 * d, g_in[j]) for j in range(2)]
    w_out = [Stk("row", d, d, g_out[j]) for j in range(2)]
    w_dqkv = Stk("row", d, wd_t, g_dqkv)
    w_uq = Stk("full", Q_LORA, N_HEADS * HEAD_PAD, g_uq)
    w_uk = Stk("row", KV_LORA, N_HEADS * QK_NOPE, g_uk)
    w_uv = Stk("row", KV_LORA, N_HEADS * V_HEAD, g_uv)
    w_o = Stk("row", d, d, g_o)
    w_pw1 = Stk("full", d, 2 * d, g_pw1)
    w_pw2 = Stk("row", d, d, g_pw2)
    w_1 = [Stk("full", d, d_ff, g_w1[i]) for i in range(DEPTH)]
    w_2 = [Stk("row", d_ff, d, g_w2[i]) for i in range(DEPTH)]

    def wide(rows):
        return jnp.swapaxes(rows, 0, 1).reshape(rows.shape[1], d)

    conv_w = wide(g_small[:, 0:6]).reshape(2, SC_WIDTH, d)
    b_pw1 = g_small[:, 6:8].reshape(1, 2 * d)
    dw_w = wide(g_small[:, 8:39])
    dw_b, norm_g, norm_b, b_pw2 = (wide(g_small[:, 39 + k:40 + k]) for k in range(4))

    pos = jnp.arange(s, dtype=F32)
    inv_freq = ROPE_THETA ** (-jnp.arange(0, QK_ROPE, 2, dtype=F32) / QK_ROPE)
    ang = pos[:, None] * inv_freq[None, :]
    cos, sin, zero = jnp.cos(ang), jnp.sin(ang), jnp.zeros((s, 128 - QK_ROPE), F32)
    cf = jnp.concatenate([cos, cos, zero], axis=1)
    sf = jnp.concatenate([-sin, sin, zero], axis=1)

    def row(a, i):
        return a[i:i + 1]

    xs = x.reshape(s, d)
    cur = (xs, xs.astype(BF16))
    tape = []
    for i in range(DEPTH):
        mixer, j = i % 3, i // 3
        xf, xb = cur
        lg, lb = row(ln_mix_g, i), row(ln_mix_b, i)
        if mixer == 0:
            u = mm_plain_nn(f"sc{j}_in", xb, w_in[j], F32, tn=3 * dq4)
            gb = short_conv_gate(u, conv_w[j])
            y, yb, xh, rstd = mm_residual_ln(f"sc{j}_out_ln", gb, w_out[j], xf, lg, lb)
            sv = dict(xb=xb, u=u, gb=gb)
        elif mixer == 1:
            t = mm_plain_nn("mla_down", xb, w_dqkv, F32, tn=wd_t // 2)
            cq, ckv, kpe = mla_latents(t, mla_g_q, mla_g_kv, cf, sf)
            qh = mla_queries(cq, w_uq, cf, sf)
            kh = mla_keys(ckv, w_uk, kpe)
            vh = mm_plain_nn("mla_values", ckv, w_uv, BF16, tk=KV_LORA)
            oh = attention(qh, kh, vh)
            y, yb, xh, rstd = mm_residual_ln("mla_out_ln", oh, w_o, xf, lg, lb)
            sv = dict(xb=xb, t=t, cq=cq, ckv=ckv, qh=qh, kh=kh, vh=vh, oh=oh)
        else:
            u = mm_plain_nn("cf_pw1", xb, w_pw1, F32, bias=b_pw1)
            hc = conformer_glu_conv(u, dw_w, dw_b)
            sb = conformer_norm_swish(hc, norm_g, norm_b)
            y, yb, xh, rstd = mm_residual_ln("cf_pw2_ln", sb, w_pw2, xf, lg, lb, bias=b_pw2)
            sv = dict(xb=xb, u=u, hc=hc, sb=sb)
        sv.update(xh=xh, rstd=rstd, g=lg)
        cur, sv_mlp = _mlp_forward(i, y, yb, w_1[i], w_2[i], row(ln_ff_g, i), row(ln_ff_b, i))
        tape.append((sv, sv_mlp))

    dy, loss_part = loss_head(cur[0], loss_target.reshape(s, d))

    grads = {}
    smalls = {}
    g_ln = {n: [None] * DEPTH for n in ("ln_mix_g", "ln_mix_b", "ln_ff_g", "ln_ff_b")}
    conv_grads = [None, None]
    core = lax.axis_index("c").astype(jnp.int32).reshape(1)
    chip = xq.astype(jnp.int32).reshape(1)
    pairs, landed = {}, {}
    ready, theirs = [], {}

    def reduce_after(x, new, early=False):
        out = lax.optimization_barrier((x, *new.values()))
        grads.update(zip(new, out[1:]))
        if early:
            theirs.update(zip(new, pair_exchange(f"pair_exchange_{len(theirs)}", list(out[1:]), True)))
        ready.extend(new)
        return out[0]

    def reduce_layer(i, x):
        late = [n for n in ready if n not in theirs]
        if late:
            theirs.update(zip(late, pair_exchange(f"pair_exchange_layer{i}", [grads[n] for n in late], False)))
        sums = [pair_sum(grads[n], theirs[n], core) for n in ready]
        pairs.update(zip(ready, sums))
        landed.update(zip(ready, chip_exchange(f"chip_exchange_layer{i}", sums)))
        exchanged.append(list(ready))
        ready.clear()
        return lax.optimization_barrier((x, *sums))[0]

    groups = [["in_0", "in_1"], ["out_0", "out_1"], ["dqkv"], ["uq"], ["uk"], ["uv"], ["o"], ["pw1"], ["pw2"],
              [f"w1_{i}" for i in range(DEPTH)], [f"w2_{i}" for i in range(DEPTH)]]
    stacks = [None] * len(groups)
    exchanged = []

    def sum_layer(x, last=False):
        names = exchanged.pop(0)
        if last:
            out = lax.optimization_barrier((x, *[landed[n] for n in names]))
            landed.update(zip(names, out[1:]))
        new = []
        for n in names:
            k = next(k for k, members in enumerate(groups) if n in members)
            stacks[k] = chip_sum(pairs[n], landed[n], chip, stacks[k], groups[k].index(n), len(groups[k]))
            new.append(stacks[k])
        return out[0] if last else lax.optimization_barrier((x, *new))[0]

    for i in reversed(range(DEPTH)):
        mixer, j = i % 3, i // 3
        sv, sv_mlp = tape[i]
        dy, g_ln["ln_ff_g"][i], g_ln["ln_ff_b"][i] = _mlp_backward(
            i, dy, sv_mlp, w_1[i], w_2[i], Stk("col", d, d_ff), Stk("row", d_ff, d),
            lambda x_, new: reduce_after(x_, new, early=i > 0))
        if i == 0:
            dy = reduce_layer("0_mlp", dy)
        dr, drb, g_ln["ln_mix_g"][i], g_ln["ln_mix_b"][i], dr_sum = ln_backward(
            f"mix{i}_ln_bwd", dy, sv["xh"], sv["rstd"], sv["g"])
        if mixer == 0:
            dgate = mm_plain_nt(f"sc{j}_out_bwd", drb, w_out[j], F32)
            dw_out = mm_tn(f"sc{j}_dw_out", sv["gb"], drb, Stk("row", d, d), s, 512, 1024)
            du, conv_grads[j] = short_conv_gate_bwd(sv["u"], conv_w[j], dgate)
            nb = d // 256
            dw_in = mm_tn(
                f"sc{j}_dw_in", sv["xb"], du, Stk("col", d, 3 * d), s, 1024, 256,
                b_spec=pl.BlockSpec((None, s, 256), lambda i_, j_, k_: (j_ // nb, k_, j_ % nb)))
            du = reduce_after(du, {f"in_{j}": dw_in, f"out_{j}": dw_out})
            dy = mm_plain_nt(
                f"sc{j}_in_bwd", du, w_in[j], F32, tn=1024, tk=d, add=dr, add_scale=ALPHA,
                a_spec_fn=(s, lambda tm, tk: pl.BlockSpec((None, tm, tk), lambda i_, j_, k_: (k_, i_, 0))))
        elif mixer == 1:
            do = mm_plain_nt("mla_out_bwd", drb, w_o, BF16)
            g_o = mm_tn("mla_dw_o", sv["oh"], drb, Stk("row", d, d), s, 512, 1024)
            dqh, dkh, dvh = attention_bwd(sv["qh"], sv["kh"], sv["vh"], do)
            dql, dkn, dkpe = mla_unrope_grads(dqh, dkh, cf, sf)
            g_uq = mm_tn("mla_dw_uq", sv["cq"], dql, Stk("col", Q_LORA, N_HEADS * HEAD_PAD), s, Q_LORA, 512)
            dcq = mm_plain_nt("mla_uq_bwd", dql, w_uq, F32, tn=Q_LORA)
            g_uk = mm_tn("mla_dw_uk", sv["ckv"], dkn, Stk("row", KV_LORA, N_HEADS * QK_NOPE), s, KV_LORA, 1024)
            g_uv = mm_tn("mla_dw_uv", sv["ckv"], dvh, Stk("row", KV_LORA, N_HEADS * V_HEAD), s, KV_LORA, 1024)
            dckv = mm_plain_nt("mla_uk_bwd", dkn, w_uk, F32, tn=KV_LORA)
            dckv = mm_plain_nt("mla_uv_bwd", dvh, w_uv, F32, tn=KV_LORA, add=dckv)
            dt, smalls["g_q"], smalls["g_kv"] = mla_latents_bwd(sv["t"], mla_g_q, mla_g_kv, cf, sf, dcq, dckv, dkpe)
            g_dqkv = mm_tn("mla_dw_down", sv["xb"], dt, Stk("row", d, wd_t), s, 512, wd_t)
            dt = reduce_after(dt, {"dqkv": g_dqkv, "uq": g_uq, "uk": g_uk, "uv": g_uv, "o": g_o})
            dy = mm_plain_nt("mla_down_bwd", dt, w_dqkv, F32, tk=wd_t, add=dr, add_scale=ALPHA)
        else:
            dsw = mm_plain_nt("cf_pw2_bwd", drb, w_pw2, F32)
            g_pw2 = mm_tn("cf_dw_pw2", sv["sb"], drb, Stk("row", d, d), s, 512, 1024)
            smalls["b_pw2"] = dr_sum
            dhc, smalls["norm_g"], smalls["norm_b"] = conformer_norm_swish_bwd(sv["hc"], norm_g, norm_b, dsw)
            du, smalls["b_pw1"], smalls["dw_w"], smalls["dw_b"] = conformer_glu_conv_bwd(sv["u"], dw_w, dhc)
            nb = d // 512
            g_pw1 = mm_tn(
                "cf_dw_pw1", sv["xb"], du, Stk("col", d, 2 * d), s, 1024, 512,
                b_spec=pl.BlockSpec((None, s, 512), lambda i_, j_, k_: (j_ // nb, k_, j_ % nb)))
            du = reduce_after(du, {"pw1": g_pw1, "pw2": g_pw2})
            dy = mm_plain_nt(
                "cf_pw1_bwd", du, w_pw1, F32, tn=1024, tk=d, add=dr, add_scale=ALPHA,
                a_spec_fn=(s, lambda tm, tk: pl.BlockSpec((None, tm, tk), lambda i_, j_, k_: (k_, i_, 0))))
        if i < DEPTH - 1:
            dy = sum_layer(dy)
        dy = reduce_layer(i, dy)
    dy = sum_layer(sum_layer(dy, last=True), last=True)
    grad_x = dy.reshape(1, s, d)

    mine = stacks
    other = (pair_share("pair_share_mixers", mine[:9]) + pair_share("pair_share_up", mine[9:10])
             + pair_share("pair_share_down", mine[10:]))

    def padded(get):
        dqkv = jnp.concatenate([get("mla_w_dq")[0], jnp.pad(get("mla_w_dkv")[0], ((0, 0), (0, 128 - QK_ROPE)))], axis=1)
        uq = jnp.pad(get("mla_w_uq")[0].reshape(Q_LORA, 2, QK_NOPE + QK_ROPE),
                     ((0, 0), (0, 0), (0, HEAD_PAD - QK_NOPE - QK_ROPE))).reshape(Q_LORA, 2 * HEAD_PAD)
        return [get("sc_w_in"), get("sc_w_out"), dqkv[None], uq[None],
                get("mla_w_uk").reshape(1, KV_LORA // N_CHIPS, d), get("mla_w_uv").reshape(1, KV_LORA // N_CHIPS, d),
                get("mla_w_o"), get("cf_w_pw1"), get("cf_w_pw2"), get("ff_w1"), get("ff_w2")]

    w_l, m_l, v_l = (padded(lambda n, p=p: given[p + n]) for p in ("", "m_", "v_"))
    res = [adamw_joined(w_l[k], m_l[k], v_l[k], mine[k], other[k], core) for k in range(len(groups))]

    def unpadded(k):
        r_in, r_out, r_dqkv, r_uq, r_uk, r_uv, r_o, r_pw1, r_pw2, r_w1, r_w2 = (r[k] for r in res)
        return {
            "sc_w_in": r_in, "sc_w_out": r_out, "mla_w_dq": r_dqkv[:, :, 0:Q_LORA],
            "mla_w_dkv": r_dqkv[:, :, Q_LORA:Q_LORA + KV_LORA + QK_ROPE],
            "mla_w_uq": r_uq.reshape(1, Q_LORA, 2, HEAD_PAD)[:, :, :, 0:QK_NOPE + QK_ROPE].reshape(mla_w_uq.shape),
            "mla_w_uk": r_uk.reshape(mla_w_uk.shape), "mla_w_uv": r_uv.reshape(mla_w_uv.shape),
            "mla_w_o": r_o, "cf_w_pw1": r_pw1, "cf_w_pw2": r_pw2, "ff_w1": r_w1, "ff_w2": r_w2}

    big_g, big_d, big_m, big_v = (unpadded(k) for k in range(4))

    pad_row = lambda a: jnp.pad(a, ((0, 0), (0, d - a.shape[1])))
    small_parts = ([g for n in ("ln_mix_g", "ln_mix_b", "ln_ff_g", "ln_ff_b") for g in g_ln[n]]
                   + [pad_row(smalls["g_q"]), pad_row(smalls["g_kv"]), conv_grads[0], conv_grads[1],
                      smalls["b_pw1"].reshape(2, d), smalls["dw_w"], smalls["dw_b"], smalls["norm_g"], smalls["norm_b"],
                      smalls["b_pw2"], loss_part])
    red = all_reduce_small(small_parts, 64)
    loss = red[61, 0]

    where = {
        "ln_mix_g": [((), 0, DEPTH, "all")], "ln_mix_b": [((), 4, DEPTH, "all")],
        "ln_ff_g": [((), 8, DEPTH, "all")], "ln_ff_b": [((), 12, DEPTH, "all")],
        "mla_g_q": [((), 16, 1, Q_LORA)], "mla_g_kv": [((), 17, 1, KV_LORA)],
        "sc_conv_w": [((0,), 18, SC_WIDTH, "chip"), ((1,), 21, SC_WIDTH, "chip")],
        "cf_b_pw1": [((), 24, 2, "chip")], "cf_dw_w": [((0,), 26, CONF_WIDTH, "chip")],
        "cf_dw_b": [((), 57, 1, "chip")], "cf_norm_g": [((), 58, 1, "chip")], "cf_norm_b": [((), 59, 1, "chip")],
        "cf_b_pw2": [((), 60, 1, "chip")]}
    vec = list(where)
    vec_res = vector_update(red, chip, [given[n] for n in vec], [given["m_" + n] for n in vec],
                            [given["v_" + n] for n in vec], [where[n] for n in vec])
    gw = dict(big_g)
    upd = {n: [big_d[n], big_m[n], big_v[n]] for n in big_g}
    for k, n in enumerate(vec):
        gw[n] = vec_res[0][k]
        upd[n] = [vec_res[1][k], vec_res[2][k], vec_res[3][k]]

    return (loss, grad_x, *[gw[n] for n in WEIGHTS], *[upd[n][0] for n in WEIGHTS],
            *[upd[n][1] for n in WEIGHTS], *[upd[n][2] for n in WEIGHTS])
```

```python
import jax
import jax.numpy as jnp
from jax import lax
from jax.experimental import pallas as pl
from jax.experimental.pallas import tpu as pltpu
from jax.experimental.pallas import tpu_sc as plsc

F32 = jnp.float32
BF16 = jnp.bfloat16
MESH = pl.DeviceIdType.MESH

DEPTH = 4
ALPHA = (2.0 * DEPTH) ** 0.25
LN_EPS = 1e-5
RMS_EPS = 1e-6
CHUNK_SHIFT = 6
N_HEADS = 8
QK_NOPE = 128
QK_ROPE = 64
V_HEAD = 128
HEAD_PAD = 256
Q_LORA = 384
KV_LORA = 256
ROPE_THETA = 10000.0
SC_WIDTH = 3
CONF_WIDTH = 31
CONV_PAD = 32
CONV_CHUNK = 64
N_CHIPS = 4
ATTN_SCALE = (QK_NOPE + QK_ROPE) ** -0.5

ADAM_LR = 0.001
ADAM_B1 = 0.9
ADAM_B2 = 0.999
ADAM_EPS = 1e-08
ADAM_WD = 0.01
ADAM_STEP = 10

VMEM_LIMIT = 56 * 2**20

NN = (((1,), (0,)), ((), ()))
NT = (((1,), (1,)), ((), ()))
TN = (((0,), (0,)), ((), ()))


def _params(sem=None):
    return pltpu.CompilerParams(dimension_semantics=sem, vmem_limit_bytes=VMEM_LIMIT)


class Stk:
    def __init__(self, kind, k, n, arr=None, layers=None, layer=None):
        self.kind, self.k, self.n, self.layers, self.layer = kind, k, n, layers, layer
        self.plain = (kind == "row" and layers is None) or kind == "full"
        self.kloc = k // N_CHIPS if kind == "row" else k
        self.nloc = n // N_CHIPS if kind == "col" else n
        if arr is not None and self.plain:
            arr = arr.reshape(k, n)
        self.arr = arr

    @property
    def shape(self):
        if self.plain:
            return (self.k, self.n)
        lead = (N_CHIPS,) if self.layers is None else (N_CHIPS, self.layers)
        return lead + (self.kloc, self.nloc)

    def spec(self, bk, bn, f):
        if self.plain:
            return pl.BlockSpec((bk, bn), f)
        assert self.kloc % bk == 0 and self.nloc % bn == 0, (self.kloc, bk, self.nloc, bn)
        pk, pn = self.kloc // bk, self.nloc // bn
        kind, layer = self.kind, self.layer

        def imap(*g):
            kb, nb = f(*g)
            if kind == "row":
                q, kb, nb = kb // pk, kb % pk, nb
            else:
                q, kb, nb = nb // pn, kb, nb % pn
            return (q, kb, nb) if layer is None else (q, layer, kb, nb)

        block = (None, bk, bn) if layer is None else (None, None, bk, bn)
        return pl.BlockSpec(block, imap)


def _mm(name, mode, a, b, grid, a_spec, b_spec, acc_shape, extras, extra_specs, out_shapes, out_specs, epi, a_fn=None):
    nk = grid[2]
    ne = len(extras)

    def body(*refs):
        a_ref, b_ref = refs[0], refs[1]
        e_refs = refs[2:2 + ne]
        av = a_ref[...] if a_fn is None else a_fn(a_ref[...])
        part = lax.dot_general(av, b_ref[...], mode, preferred_element_type=F32)
        if nk == 1:
            epi(part, e_refs, refs[2 + ne:])
            return
        o_refs = refs[2 + ne:-1]
        acc = refs[-1]
        k = pl.program_id(2)

        @pl.when(k == 0)
        def _():
            acc[...] = part

        @pl.when(k > 0)
        def _():
            acc[...] += part

        @pl.when(k == nk - 1)
        def _():
            epi(acc[...], e_refs, o_refs)

    return pl.pallas_call(
        body, grid=grid, in_specs=[a_spec, b_spec, *extra_specs], out_specs=out_specs, out_shape=out_shapes,
        scratch_shapes=[pltpu.VMEM(acc_shape, F32)] if nk > 1 else [],
        compiler_params=_params(("parallel", "parallel", "arbitrary")), name=name)(a, b, *extras)


def _tile(n, t):
    t = min(n, t)
    while n % t:
        t -= 8
    assert t > 0, (n, t)
    return t


def mm_nn(name, a, w, tm, tn, tk, epi, out_shapes, out_specs, extras=(), extra_specs=(), a_spec=None, a_fn=None):
    m = a.shape[0]
    tm, tn, tk = _tile(m, tm), _tile(w.n, tn), _tile(w.k, tk)
    grid = (m // tm, w.n // tn, w.k // tk)
    a_spec = a_spec or pl.BlockSpec((tm, tk), lambda i, j, k: (i, k))
    b_spec = w.spec(tk, tn, lambda i, j, k: (k, j))
    return _mm(name, NN, a, w.arr, grid, a_spec, b_spec, (tm, tn), extras, extra_specs, out_shapes, out_specs, epi, a_fn)


def mm_nt(name, a, w, m, tm, tn, tk, epi, out_shapes, out_specs, extras=(), extra_specs=(), a_spec=None):
    tm, tn, tk = _tile(m, tm), _tile(w.k, tn), _tile(w.n, tk)
    grid = (m // tm, w.k // tn, w.n // tk)
    a_spec = a_spec or pl.BlockSpec((tm, tk), lambda i, j, k: (i, k))
    b_spec = w.spec(tn, tk, lambda i, j, k: (j, k))
    return _mm(name, NT, a, w.arr, grid, a_spec, b_spec, (tm, tn), extras, extra_specs, out_shapes, out_specs, epi)


def mm_tn(name, a, b, dw, s, tm=512, tn=512, tk=4096, a_spec=None, b_spec=None, a_fn=None):
    tm, tn, tk = _tile(dw.k, tm), _tile(dw.n, tn), _tile(s, tk)
    grid = (dw.k // tm, dw.n // tn, s // tk)
    a_spec = a_spec or pl.BlockSpec((tk, tm), lambda i, j, k: (k, i))
    b_spec = b_spec or pl.BlockSpec((tk, tn), lambda i, j, k: (k, j))

    def epi(acc, e, o):
        o[0][...] = acc.astype(BF16)

    out = _mm(name, TN, a, b, grid, a_spec, b_spec, (tm, tn), (), (), [jax.ShapeDtypeStruct(dw.shape, BF16)],
              [dw.spec(tm, tn, lambda i, j, k: (i, j))], epi, a_fn)[0]
    return out.reshape(N_CHIPS, dw.k // N_CHIPS, dw.n) if dw.plain else out


def _sds(shape, dtype):
    return jax.ShapeDtypeStruct(shape, dtype)


def _ij(tm, tn):
    return pl.BlockSpec((tm, tn), lambda i, j, k: (i, j))


def _i0(tm, c):
    return pl.BlockSpec((tm, c), lambda i, j, k: (i, 0))


def _0j(r, tn):
    return pl.BlockSpec((r, tn), lambda i, j, k: (0, j))


def _layer_norm_rows(r, g, b):
    mu = jnp.mean(r, axis=-1, keepdims=True)
    d = r - mu
    var = jnp.mean(d * d, axis=-1, keepdims=True)
    rstd = lax.rsqrt(var + LN_EPS)
    xh = d * rstd
    return xh * g + b, xh, rstd


def mm_residual_ln(name, a, w, x, g, b, bias=None, tm=512, tk=1024, a_fn=None):
    s, d = x.shape
    tm = _tile(s, tm)
    extras = [x, g, b] + ([bias] if bias is not None else [])
    especs = [_i0(tm, d), _0j(1, d), _0j(1, d)] + ([_0j(1, d)] if bias is not None else [])

    def epi(acc, e, o):
        r = ALPHA * e[0][...] + acc
        if bias is not None:
            r = r + e[3][...]
        y, xh, rstd = _layer_norm_rows(r, e[1][...], e[2][...])
        o[0][...] = y
        o[1][...] = y.astype(BF16)
        o[2][...] = xh
        o[3][...] = rstd

    return mm_nn(name, a, w, tm, d, tk, epi,
                 [_sds((s, d), F32), _sds((s, d), BF16), _sds((s, d), F32), _sds((s, 1), F32)],
                 [_i0(tm, d), _i0(tm, d), _i0(tm, d), _i0(tm, 1)], extras, especs, a_fn=a_fn)


def mm_plain_nn(name, a, w, out_dtype, tm=1024, tn=512, tk=1024, bias=None):
    m = a.shape[0]
    tm, tn = _tile(m, tm), _tile(w.n, tn)
    if w.kind == "col":
        tn = _tile(w.nloc, tn)

    def epi(acc, e, o):
        if bias is not None:
            acc = acc + e[0][...]
        o[0][...] = acc.astype(out_dtype)

    extras, especs = ([bias], [_0j(1, tn)]) if bias is not None else ((), ())
    return mm_nn(name, a, w, tm, tn, tk, epi, [_sds((m, w.n), out_dtype)], [_ij(tm, tn)], extras, especs)[0]


def mm_plain_nt(name, a, w, out_dtype, tm=1024, tn=512, tk=1024, add=None, add_scale=1.0, a_spec_fn=None):
    m = a.shape[0] if a_spec_fn is None else a_spec_fn[0]
    tm, tn = _tile(m, tm), _tile(w.k, tn)
    tk = _tile(w.n, tk)
    if w.kind == "col":
        tk = _tile(w.nloc, tk)
    if w.kind == "row" and not w.plain:
        tn = _tile(w.kloc, tn)

    def epi(acc, e, o):
        if add is not None:
            acc = acc + add_scale * e[0][...].astype(F32)
        o[0][...] = acc.astype(out_dtype)

    extras, especs = ([add], [_ij(tm, tn)]) if add is not None else ((), ())
    a_spec = None if a_spec_fn is None else a_spec_fn[1](tm, tk)
    return mm_nt(name, a, w, m, tm, tn, tk, epi, [_sds((m, w.k), out_dtype)], [_ij(tm, tn)], extras, especs,
                 a_spec=a_spec)[0]


def _rows(tm, c):
    return pl.BlockSpec((tm, c), lambda i: (i, 0))


def _fix(shape):
    nd = len(shape)
    return pl.BlockSpec(shape, lambda i: (0,) * nd)


def _accumulate(ref, val):
    @pl.when(pl.program_id(0) == 0)
    def _():
        ref[...] = jnp.zeros_like(ref)

    ref[...] += val


def ln_backward(name, dy, xhat, rstd, g, tm=512):
    s, d = dy.shape
    tm = _tile(s, tm)

    def body(dy_ref, xh_ref, rstd_ref, g_ref, dr_ref, drb_ref, dg_ref, db_ref, ds_ref):
        dyv, xh = dy_ref[...], xh_ref[...]
        dxh = dyv * g_ref[...]
        m1 = jnp.mean(dxh, axis=-1, keepdims=True)
        m2 = jnp.mean(dxh * xh, axis=-1, keepdims=True)
        dr = rstd_ref[...] * (dxh - m1 - xh * m2)
        dr_ref[...] = dr
        drb_ref[...] = dr.astype(BF16)
        _accumulate(dg_ref, jnp.sum(dyv * xh, axis=0, keepdims=True))
        _accumulate(db_ref, jnp.sum(dyv, axis=0, keepdims=True))
        _accumulate(ds_ref, jnp.sum(dr, axis=0, keepdims=True))

    return pl.pallas_call(
        body, grid=(s // tm,),
        in_specs=[_rows(tm, d), _rows(tm, d), _rows(tm, 1), _fix((1, d))],
        out_specs=[_rows(tm, d), _rows(tm, d), _fix((1, d)), _fix((1, d)), _fix((1, d))],
        out_shape=[_sds((s, d), F32), _sds((s, d), BF16), _sds((1, d), F32), _sds((1, d), F32), _sds((1, d), F32)],
        compiler_params=_params(("arbitrary",)), name=name)(dy, xhat, rstd, g)


def loss_head(y, target, tm=512):
    s, d = y.shape
    tm = _tile(s, tm)

    def body(y_ref, t_ref, dy_ref, loss_ref):
        e = y_ref[...] - t_ref[...]
        dy_ref[...] = e * (1.0 / d)
        part = 0.5 * jnp.sum(jnp.mean(e * e, axis=-1, keepdims=True), axis=0, keepdims=True)
        _accumulate(loss_ref, jnp.broadcast_to(part, (1, d)))

    return pl.pallas_call(
        body, grid=(s // tm,), in_specs=[_rows(tm, d), _rows(tm, d)],
        out_specs=[_rows(tm, d), _fix((1, d))], out_shape=[_sds((s, d), F32), _sds((1, d), F32)],
        compiler_params=_params(("arbitrary",)), name="loss_head")(y, target)


def _cols(s, tc, off=0):
    return pl.BlockSpec((s, tc), lambda i: (0, i + off))


def _shift_down(z, sft, rows):
    return jnp.where(rows >= sft, pltpu.roll(z, sft, 0), 0.0)


def _shift_up(z, sft, rows, s):
    return jnp.where(rows < s - sft, pltpu.roll(z, (s - sft) % s, 0), 0.0)


def short_conv_gate(u, conv_w, tc=256):
    s, d3 = u.shape
    d = d3 // 3
    nb = d // tc

    def body(b_ref, c_ref, h_ref, w_ref, o_ref):
        rows = lax.broadcasted_iota(jnp.int32, (s, tc), 0)
        z = c_ref[...] * h_ref[...]
        cz = jnp.zeros((s, tc), F32)
        for k in range(SC_WIDTH):
            sft = SC_WIDTH - 1 - k
            cz = cz + w_ref[pl.ds(k, 1), :] * (_shift_down(z, sft, rows) if sft else z)
        o_ref[...] = (b_ref[...] * cz).astype(BF16)

    return pl.pallas_call(
        body, grid=(nb,),
        in_specs=[_cols(s, tc), _cols(s, tc, nb), _cols(s, tc, 2 * nb), _cols(SC_WIDTH, tc)],
        out_specs=_cols(s, tc), out_shape=_sds((s, d), BF16),
        compiler_params=_params(("parallel",)), name="short_conv_gate")(u, u, u, conv_w)


def short_conv_gate_bwd(u, conv_w, dg, tc=256):
    s, d3 = u.shape
    d = d3 // 3
    nb = d // tc

    def body(b_ref, c_ref, h_ref, w_ref, dg_ref, du_ref, dw_ref):
        rows = lax.broadcasted_iota(jnp.int32, (s, tc), 0)
        c, h, dgv = c_ref[...], h_ref[...], dg_ref[...]
        z = c * h
        dcz = dgv * b_ref[...]
        cz = jnp.zeros((s, tc), F32)
        dz = jnp.zeros((s, tc), F32)
        for k in range(SC_WIDTH):
            sft = SC_WIDTH - 1 - k
            zs = _shift_down(z, sft, rows) if sft else z
            wk = w_ref[pl.ds(k, 1), :]
            cz = cz + wk * zs
            dz = dz + wk * (_shift_up(dcz, sft, rows, s) if sft else dcz)
            dw_ref[pl.ds(k, 1), :] = jnp.sum(dcz * zs, axis=0, keepdims=True)
        du_ref[0] = (dgv * cz).astype(BF16)
        du_ref[1] = (dz * h).astype(BF16)
        du_ref[2] = (dz * c).astype(BF16)

    return pl.pallas_call(
        body, grid=(nb,),
        in_specs=[_cols(s, tc), _cols(s, tc, nb), _cols(s, tc, 2 * nb), _cols(SC_WIDTH, tc), _cols(s, tc)],
        out_specs=[pl.BlockSpec((3, s, tc), lambda i: (0, 0, i)), _cols(SC_WIDTH, tc)],
        out_shape=[_sds((3, s, d), BF16), _sds((SC_WIDTH, d), F32)],
        compiler_params=_params(("parallel",)), name="short_conv_gate_bwd")(u, u, u, conv_w, dg)


def _store_shifted_down(ref, z, rows):
    s, tc = z.shape
    for b in range(8):
        ref[b, pl.ds(0, CONV_PAD), :] = jnp.zeros((CONV_PAD, tc), F32)
        ref[b, pl.ds(CONV_PAD, s), :] = z if b == 0 else _shift_down(z, b, rows)


def _store_shifted_up(ref, z, rows):
    s, tc = z.shape
    for b in range(8):
        ref[b, pl.ds(0, s), :] = z if b == 0 else _shift_up(z, b, rows, s)
        ref[b, pl.ds(s, CONV_PAD), :] = jnp.zeros((CONV_PAD, tc), F32)


def conformer_glu_conv(u, dw_w, dw_b, tc=128):
    s, d2 = u.shape
    d = d2 // 2
    nb = d // tc

    ch = min(CONV_CHUNK, s)

    def body(a_ref, g_ref, w_ref, b_ref, o_ref, down):
        rows = lax.broadcasted_iota(jnp.int32, (s, tc), 0)
        _store_shifted_down(down, a_ref[...] * jax.nn.sigmoid(g_ref[...]), rows)

        def chunk(ci, carry):
            r0 = pl.multiple_of(ci * ch, ch)
            acc = jnp.broadcast_to(b_ref[...], (ch, tc))
            for k in range(CONF_WIDTH):
                sft = CONF_WIDTH - 1 - k
                acc = acc + w_ref[pl.ds(k, 1), :] * down[sft % 8, pl.ds(CONV_PAD + r0 - (sft // 8) * 8, ch), :]
            o_ref[pl.ds(r0, ch), :] = acc
            return carry

        lax.fori_loop(0, s // ch, chunk, 0)

    return pl.pallas_call(
        body, grid=(nb,),
        in_specs=[_cols(s, tc), _cols(s, tc, nb), _cols(CONF_WIDTH, tc), _cols(1, tc)],
        out_specs=_cols(s, tc), out_shape=_sds((s, d), F32),
        scratch_shapes=[pltpu.VMEM((8, CONV_PAD + s, tc), F32)],
        compiler_params=_params(("parallel",)), name="conformer_glu_conv")(u, u, dw_w, dw_b)


def conformer_glu_conv_bwd(u, dw_w, dhc, tc=128):
    s, d2 = u.shape
    d = d2 // 2
    nb = d // tc
    ch = min(CONV_CHUNK, s)

    def body(a_ref, g_ref, w_ref, dhc_ref, du_ref, dbias_ref, dw_ref, db_ref, down, up, dw_acc, dh_buf):
        rows = lax.broadcasted_iota(jnp.int32, (s, tc), 0)
        a = a_ref[...]
        sg = jax.nn.sigmoid(g_ref[...])
        dhcv = dhc_ref[...]
        _store_shifted_down(down, a * sg, rows)
        _store_shifted_up(up, dhcv, rows)
        dw_acc[...] = jnp.zeros_like(dw_acc)

        def chunk(ci, carry):
            r0 = pl.multiple_of(ci * ch, ch)
            dc = dhc_ref[pl.ds(r0, ch), :]
            dh = jnp.zeros((ch, tc), F32)
            for k in range(CONF_WIDTH):
                sft = CONF_WIDTH - 1 - k
                a8, b = (sft // 8) * 8, sft % 8
                dh = dh + w_ref[pl.ds(k, 1), :] * up[b, pl.ds(r0 + a8, ch), :]
                prod = dc * down[b, pl.ds(CONV_PAD + r0 - a8, ch), :]
                dw_acc[k] += jnp.sum(prod.reshape(ch // 8, 8, tc), axis=0)
            dh_buf[pl.ds(r0, ch), :] = dh
            return carry

        lax.fori_loop(0, s // ch, chunk, 0)
        dh = dh_buf[...]
        da = dh * sg
        dgate = dh * a * sg * (1.0 - sg)
        du_ref[0] = da.astype(BF16)
        du_ref[1] = dgate.astype(BF16)
        dbias_ref[pl.ds(0, 1), :] = jnp.sum(da, axis=0, keepdims=True)
        dbias_ref[pl.ds(1, 1), :] = jnp.sum(dgate, axis=0, keepdims=True)
        db_ref[...] = jnp.sum(dhcv, axis=0, keepdims=True)
        for k in range(CONF_WIDTH):
            dw_ref[pl.ds(k, 1), :] = jnp.sum(dw_acc[k], axis=0, keepdims=True)

    return pl.pallas_call(
        body, grid=(nb,),
        in_specs=[_cols(s, tc), _cols(s, tc, nb), _cols(CONF_WIDTH, tc), _cols(s, tc)],
        out_specs=[pl.BlockSpec((2, s, tc), lambda i: (0, 0, i)), _cols(2, tc), _cols(CONF_WIDTH, tc), _cols(1, tc)],
        out_shape=[_sds((2, s, d), BF16), _sds((2, d), F32), _sds((CONF_WIDTH, d), F32), _sds((1, d), F32)],
        scratch_shapes=[pltpu.VMEM((8, CONV_PAD + s, tc), F32), pltpu.VMEM((8, CONV_PAD + s, tc), F32),
                        pltpu.VMEM((CONF_WIDTH + 1, 8, tc), F32), pltpu.VMEM((s, tc), F32)],
        compiler_params=_params(("parallel",)), name="conformer_glu_conv_bwd")(u, u, dw_w, dhc)


def conformer_norm_swish(hc, g, b, tm=512):
    s, d = hc.shape
    tm = _tile(s, tm)

    def body(h_ref, g_ref, b_ref, o_ref):
        n, _, _ = _layer_norm_rows(h_ref[...], g_ref[...], b_ref[...])
        o_ref[...] = (n * jax.nn.sigmoid(n)).astype(BF16)

    return pl.pallas_call(
        body, grid=(s // tm,), in_specs=[_rows(tm, d), _fix((1, d)), _fix((1, d))], out_specs=_rows(tm, d),
        out_shape=_sds((s, d), BF16), compiler_params=_params(("parallel",)), name="conformer_norm_swish")(hc, g, b)


def conformer_norm_swish_bwd(hc, g, b, ds, tm=512):
    s, d = hc.shape
    tm = _tile(s, tm)

    def body(h_ref, g_ref, b_ref, ds_ref, dh_ref, dg_ref, db_ref):
        n, nh, rstd = _layer_norm_rows(h_ref[...], g_ref[...], b_ref[...])
        sg = jax.nn.sigmoid(n)
        dn = ds_ref[...] * (sg * (1.0 + n * (1.0 - sg)))
        dnh = dn * g_ref[...]
        m1 = jnp.mean(dnh, axis=-1, keepdims=True)
        m2 = jnp.mean(dnh * nh, axis=-1, keepdims=True)
        dh_ref[...] = rstd * (dnh - m1 - nh * m2)
        _accumulate(dg_ref, jnp.sum(dn * nh, axis=0, keepdims=True))
        _accumulate(db_ref, jnp.sum(dn, axis=0, keepdims=True))

    return pl.pallas_call(
        body, grid=(s // tm,), in_specs=[_rows(tm, d), _fix((1, d)), _fix((1, d)), _rows(tm, d)],
        out_specs=[_rows(tm, d), _fix((1, d)), _fix((1, d))],
        out_shape=[_sds((s, d), F32), _sds((1, d), F32), _sds((1, d), F32)],
        compiler_params=_params(("arbitrary",)), name="conformer_norm_swish_bwd")(hc, g, b, ds)


def _swap_halves(x):
    lane = lax.broadcasted_iota(jnp.int32, x.shape, 1)
    return jnp.where(lane < QK_ROPE // 2, pltpu.roll(x, 128 - QK_ROPE // 2, 1), pltpu.roll(x, QK_ROPE // 2, 1))


def _rope(x, cf, sf):
    return x * cf + _swap_halves(x) * sf


def _unrope(dx, cf, sf):
    return dx * cf - _swap_halves(dx) * sf


def _rms_rows(x, g):
    r = lax.rsqrt(jnp.mean(x * x, axis=-1, keepdims=True) + RMS_EPS)
    return x * r, r


def mla_latents(t, g_q, g_kv, cf, sf, tm=512):
    s = t.shape[0]
    tm = _tile(s, tm)

    def body(t_ref, gq_ref, gkv_ref, cf_ref, sf_ref, cq_ref, ckv_ref, kpe_ref):
        xq, _ = _rms_rows(t_ref[:, 0:Q_LORA], gq_ref[...])
        cq_ref[...] = (xq * gq_ref[...]).astype(BF16)
        xkv, _ = _rms_rows(t_ref[:, Q_LORA:Q_LORA + KV_LORA], gkv_ref[...])
        ckv_ref[...] = (xkv * gkv_ref[...]).astype(BF16)
        kpe_ref[...] = _rope(t_ref[:, Q_LORA + KV_LORA:], cf_ref[...], sf_ref[...]).astype(BF16)

    w = Q_LORA + KV_LORA + 128
    return pl.pallas_call(
        body, grid=(s // tm,),
        in_specs=[_rows(tm, w), _fix((1, Q_LORA)), _fix((1, KV_LORA)), _rows(tm, 128), _rows(tm, 128)],
        out_specs=[_rows(tm, Q_LORA), _rows(tm, KV_LORA), _rows(tm, 128)],
        out_shape=[_sds((s, Q_LORA), BF16), _sds((s, KV_LORA), BF16), _sds((s, 128), BF16)],
        compiler_params=_params(("parallel",)), name="mla_latents")(t, g_q, g_kv, cf, sf)


def mla_latents_bwd(t, g_q, g_kv, cf, sf, dcq, dckv, dkpe, tm=512):
    s = t.shape[0]
    tm = _tile(s, tm)
    w = Q_LORA + KV_LORA + 128

    def rms_bwd(x, g, dy):
        xh, r = _rms_rows(x, g)
        dxh = dy * g
        return r * (dxh - xh * jnp.mean(dxh * xh, axis=-1, keepdims=True)), jnp.sum(dy * xh, axis=0, keepdims=True)

    def body(t_ref, gq_ref, gkv_ref, cf_ref, sf_ref, dcq_ref, dckv_ref, dkpe_ref, dt_ref, dgq_ref, dgkv_ref):
        dxq, dgq = rms_bwd(t_ref[:, 0:Q_LORA], gq_ref[...], dcq_ref[...])
        dxkv, dgkv = rms_bwd(t_ref[:, Q_LORA:Q_LORA + KV_LORA], gkv_ref[...], dckv_ref[...])
        dt_ref[:, 0:Q_LORA] = dxq.astype(BF16)
        dt_ref[:, Q_LORA:Q_LORA + KV_LORA] = dxkv.astype(BF16)
        dt_ref[:, Q_LORA + KV_LORA:] = _unrope(dkpe_ref[...], cf_ref[...], sf_ref[...]).astype(BF16)
        _accumulate(dgq_ref, dgq)
        _accumulate(dgkv_ref, dgkv)

    return pl.pallas_call(
        body, grid=(s // tm,),
        in_specs=[_rows(tm, w), _fix((1, Q_LORA)), _fix((1, KV_LORA)), _rows(tm, 128), _rows(tm, 128),
                  _rows(tm, Q_LORA), _rows(tm, KV_LORA), _rows(tm, 128)],
        out_specs=[_rows(tm, w), _fix((1, Q_LORA)), _fix((1, KV_LORA))],
        out_shape=[_sds((s, w), BF16), _sds((1, Q_LORA), F32), _sds((1, KV_LORA), F32)],
        compiler_params=_params(("arbitrary",)), name="mla_latents_bwd")(t, g_q, g_kv, cf, sf, dcq, dckv, dkpe)


def mla_queries(cq, w_uq, cf, sf, tm=2048):
    s = cq.shape[0]
    tm = _tile(s, tm)

    def epi(acc, e, o):
        o[0][:, 0:QK_NOPE] = acc[:, 0:QK_NOPE].astype(BF16)
        o[0][:, QK_NOPE:] = _rope(acc[:, QK_NOPE:], e[0][...], e[1][...]).astype(BF16)

    return mm_nn("mla_queries", cq, w_uq, tm, HEAD_PAD, Q_LORA, epi, [_sds((s, N_HEADS * HEAD_PAD), BF16)],
                 [_ij(tm, HEAD_PAD)], [cf, sf], [_i0(tm, 128), _i0(tm, 128)])[0]


def mla_keys(ckv, w_uk, kpe, tm=2048):
    s = ckv.shape[0]
    tm = _tile(s, tm)

    def epi(acc, e, o):
        o[0][:, 0:QK_NOPE] = acc.astype(BF16)
        o[0][:, QK_NOPE:] = e[0][...]

    return mm_nn("mla_keys", ckv, w_uk, tm, QK_NOPE, KV_LORA, epi, [_sds((s, N_HEADS * HEAD_PAD), BF16)],
                 [_ij(tm, HEAD_PAD)], [kpe], [_i0(tm, 128)])[0]


def _masked_scores(q, k, qi, tq, kv):
    sc = lax.dot_general(q, k, NT, preferred_element_type=F32) * ATTN_SCALE
    row = lax.broadcasted_iota(jnp.int32, (tq, kv), 0) + qi * tq
    col = lax.broadcasted_iota(jnp.int32, (tq, kv), 1)
    ok = lax.shift_right_logical(col, CHUNK_SHIFT) <= lax.shift_right_logical(row, CHUNK_SHIFT)
    return jnp.where(ok, sc, -1e30)


def attention(q, k, v, tq=512):
    s = q.shape[0]
    tq = _tile(s, tq)
    nq = s // tq

    def body(q_ref, k_ref, v_ref, o_ref):
        for qi in range(nq):
            kv = (qi + 1) * tq
            sc = _masked_scores(q_ref[pl.ds(qi * tq, tq), :], k_ref[pl.ds(0, kv), :], qi, tq, kv)
            p = jnp.exp(sc - jnp.max(sc, axis=-1, keepdims=True))
            o = lax.dot_general(p.astype(BF16), v_ref[pl.ds(0, kv), :], NN, preferred_element_type=F32)
            o_ref[pl.ds(qi * tq, tq), :] = (o / jnp.sum(p, axis=-1, keepdims=True)).astype(BF16)

    hq = pl.BlockSpec((s, HEAD_PAD), lambda h: (0, h))
    hv = pl.BlockSpec((s, V_HEAD), lambda h: (0, h))
    return pl.pallas_call(
        body, grid=(N_HEADS,), in_specs=[hq, hq, hv], out_specs=hv, out_shape=_sds((s, N_HEADS * V_HEAD), BF16),
        compiler_params=_params(("parallel",)), name="attention")(q, k, v)


def attention_bwd(q, k, v, do, tq=512):
    s = q.shape[0]
    tq = _tile(s, tq)
    nq = s // tq

    def body(q_ref, k_ref, v_ref, do_ref, dq_ref, dk_ref, dv_ref, dk_acc, dv_acc):
        dk_acc[...] = jnp.zeros_like(dk_acc)
        dv_acc[...] = jnp.zeros_like(dv_acc)
        for qi in range(nq):
            kv = (qi + 1) * tq
            qt = q_ref[pl.ds(qi * tq, tq), :]
            kt = k_ref[pl.ds(0, kv), :]
            dot = do_ref[pl.ds(qi * tq, tq), :]
            sc = _masked_scores(qt, kt, qi, tq, kv)
            p = jnp.exp(sc - jnp.max(sc, axis=-1, keepdims=True))
            p = p / jnp.sum(p, axis=-1, keepdims=True)
            dp = lax.dot_general(dot, v_ref[pl.ds(0, kv), :], NT, preferred_element_type=F32)
            delta = jnp.sum(p * dp, axis=-1, keepdims=True)
            ds = (p * (dp - delta) * ATTN_SCALE).astype(BF16)
            dq_ref[pl.ds(qi * tq, tq), :] = lax.dot_general(ds, kt, NN, preferred_element_type=F32).astype(BF16)
            dk_acc[pl.ds(0, kv), :] += lax.dot_general(ds, qt, TN, preferred_element_type=F32)
            dv_acc[pl.ds(0, kv), :] += lax.dot_general(p.astype(BF16), dot, TN, preferred_element_type=F32)
        dk_ref[...] = dk_acc[...].astype(BF16)
        dv_ref[...] = dv_acc[...].astype(BF16)

    hq = pl.BlockSpec((s, HEAD_PAD), lambda h: (0, h))
    hv = pl.BlockSpec((s, V_HEAD), lambda h: (0, h))
    return pl.pallas_call(
        body, grid=(N_HEADS,), in_specs=[hq, hq, hv, hv], out_specs=[hq, hq, hv],
        out_shape=[_sds((s, N_HEADS * HEAD_PAD), BF16), _sds((s, N_HEADS * HEAD_PAD), BF16),
                   _sds((s, N_HEADS * V_HEAD), BF16)],
        scratch_shapes=[pltpu.VMEM((s, HEAD_PAD), F32), pltpu.VMEM((s, V_HEAD), F32)],
        compiler_params=_params(("parallel",)), name="attention_bwd")(q, k, v, do)


def mla_unrope_grads(dq, dk, cf, sf, tm=512):
    s = dq.shape[0]
    tm = _tile(s, tm)

    def body(dq_ref, dk_ref, cf_ref, sf_ref, dql_ref, dkn_ref, dkpe_ref):
        cfv, sfv = cf_ref[...], sf_ref[...]
        dkpe = jnp.zeros((tm, 128), F32)
        for h in range(N_HEADS):
            lo = h * HEAD_PAD
            dql_ref[:, lo:lo + QK_NOPE] = dq_ref[:, lo:lo + QK_NOPE]
            dql_ref[:, lo + QK_NOPE:lo + HEAD_PAD] = _unrope(
                dq_ref[:, lo + QK_NOPE:lo + HEAD_PAD].astype(F32), cfv, sfv).astype(BF16)
            dkn_ref[:, h * QK_NOPE:(h + 1) * QK_NOPE] = dk_ref[:, lo:lo + QK_NOPE]
            dkpe = dkpe + dk_ref[:, lo + QK_NOPE:lo + HEAD_PAD].astype(F32)
        dkpe_ref[...] = dkpe

    wq = N_HEADS * HEAD_PAD
    return pl.pallas_call(
        body, grid=(s // tm,), in_specs=[_rows(tm, wq), _rows(tm, wq), _rows(tm, 128), _rows(tm, 128)],
        out_specs=[_rows(tm, wq), _rows(tm, N_HEADS * QK_NOPE), _rows(tm, 128)],
        out_shape=[_sds((s, wq), BF16), _sds((s, N_HEADS * QK_NOPE), BF16), _sds((s, 128), F32)],
        compiler_params=_params(("parallel",)), name="mla_unrope_grads")(dq, dk, cf, sf)


ANY = pl.BlockSpec(memory_space=pl.ANY)
GATHER_ID = 1
CHIP_EXCHANGE_ID = 2
PAIR_ID = 3
ALL_ID = 4


def _nbytes(a):
    return a.size * a.dtype.itemsize


def _copy_cost(operand_bytes, sent_fraction):
    sent = int(operand_bytes * sent_fraction)
    return pl.CostEstimate(flops=0, transcendentals=0, bytes_accessed=2 * sent, remote_bytes_transferred=sent)


def _handshake(peers):
    barrier = pltpu.get_barrier_semaphore()
    for peer in peers:
        pl.semaphore_signal(barrier, inc=1, device_id=peer, device_id_type=MESH)
    pl.semaphore_wait(barrier, len(peers))


def _place():
    x, y, c = lax.axis_index("x"), lax.axis_index("y"), lax.axis_index("c")
    chips = [(1 - x, y), (x, 1 - y), (1 - x, 1 - y)]
    return x, y, c, chips


def _half(ref, hc, axis=0):
    n = ref.shape[axis] // 2
    idx = (slice(None),) * axis + (pl.ds(hc * n, n),)
    return ref.at[idx]


def gather_shards(name, tensors, by_columns=()):
    nt = len(tensors)

    def body(*refs):
        a, g = refs[:nt], refs[nt:2 * nt]
        send, recv = refs[2 * nt:]
        x, y, c, _ = _place()
        q = 2 * x + y
        sib, xn, yn = (x, y, 1 - c), (1 - x, y, c), (x, 1 - y, c)
        q_xn, q_yn, q_diag = 2 * (1 - x) + y, 2 * x + 1 - y, 2 * (1 - x) + 1 - y
        _handshake([sib, xn, yn])

        def whole(t, p):
            if t in by_columns:
                n = a[t].shape[1]
                return g[t].at[:, pl.ds(p * n, n)]
            return g[t].at[p]

        def part(t, p, hc, quarter=None):
            rows = a[t].shape[0]
            if quarter is None:
                return whole(t, p).at[pl.ds(hc * (rows // 2), rows // 2)]
            return whole(t, p).at[pl.ds(hc * (rows // 2) + quarter * (rows // 4), rows // 4)]

        def rc(t, k, src, dst, to):
            return pltpu.make_async_remote_copy(src_ref=src, dst_ref=dst, send_sem=send.at[t, k], recv_sem=recv.at[t, k],
                                                device_id=to, device_id_type=MESH)

        sent = []

        def go(cp):
            cp.start()
            sent.append(cp)

        def landed(t, k, piece, frm):
            rc(t, k, piece, piece, frm).wait_recv()
            return piece

        for t in range(nt):
            go(rc(t, 8, a[t], whole(t, q), sib))
            mine = _half(a[t], c)
            go(rc(t, 0, mine, part(t, q, c), xn))
            go(rc(t, 1, mine, part(t, q, c), yn))
        for t in range(nt):
            from_y = landed(t, 1, part(t, q_yn, c), yn)
            go(rc(t, 2, part(t, q_yn, c, 0), part(t, q_yn, c, 0), xn))
            go(rc(t, 5, from_y, from_y, sib))
            from_x = landed(t, 0, part(t, q_xn, c), xn)
            go(rc(t, 3, part(t, q_xn, c, 1), part(t, q_xn, c, 1), yn))
            go(rc(t, 4, from_x, from_x, sib))
        for t in range(nt):
            for k, frm in ((2, xn), (3, yn)):
                piece = landed(t, k, part(t, q_diag, c, k - 2), frm)
                go(rc(t, 4 + k, piece, piece, sib))
        for t in range(nt):
            landed(t, 4, part(t, q_xn, 1 - c), sib)
            landed(t, 5, part(t, q_yn, 1 - c), sib)
            landed(t, 6, part(t, q_diag, 1 - c, 0), sib)
            landed(t, 7, part(t, q_diag, 1 - c, 1), sib)
            landed(t, 8, whole(t, q), sib)
        for cp in sent:
            cp.wait_send()

    return pl.kernel(
        body, name=name,
        out_type=[_sds((a.shape[0], N_CHIPS * a.shape[1]) if t in by_columns else (N_CHIPS,) + a.shape, a.dtype)
                  for t, a in enumerate(tensors)],
        mesh=plsc.ScalarSubcoreMesh(axis_name="sequencer", num_cores=1),
        scratch_types=[pltpu.SemaphoreType.DMA((nt, 9)), pltpu.SemaphoreType.DMA((nt, 9))],
        cost_estimate=_copy_cost(sum(_nbytes(a) for a in tensors), 4),
        compiler_params=pltpu.CompilerParams(collective_id=GATHER_ID))(*tensors)


def pair_exchange(name, grads, on_sequencer):
    nt = len(grads)

    def body(*refs):
        g, theirs = refs[:nt], refs[nt:2 * nt]
        send, recv = refs[2 * nt:]
        x, y, c, _ = _place()
        if on_sequencer:
            _handshake([(x, y, 1 - c)])
        cps = []
        for t in range(nt):
            cp = pltpu.make_async_remote_copy(src_ref=_half(g[t], 1 - c, 1), dst_ref=theirs[t], send_sem=send.at[t],
                                              recv_sem=recv.at[t], device_id=(x, y, 1 - c), device_id_type=MESH)
            cp.start()
            cps.append(cp)
        for cp in cps:
            cp.wait()

    if not on_sequencer:
        return pl.pallas_call(
            body, in_specs=[ANY] * nt, out_specs=[ANY] * nt,
            out_shape=[_sds((N_CHIPS, a.shape[1] // 2, a.shape[2]), a.dtype) for a in grads],
            scratch_shapes=[pltpu.SemaphoreType.DMA((nt,)), pltpu.SemaphoreType.DMA((nt,))],
            name=name)(*grads)
    return pl.kernel(
        body, name=name, out_type=[_sds((N_CHIPS, a.shape[1] // 2, a.shape[2]), a.dtype) for a in grads],
        mesh=plsc.ScalarSubcoreMesh(axis_name="sequencer", num_cores=1),
        scratch_types=[pltpu.SemaphoreType.DMA((nt,)), pltpu.SemaphoreType.DMA((nt,))],
        cost_estimate=_copy_cost(sum(_nbytes(a) for a in grads), 0.5),
        compiler_params=pltpu.CompilerParams(collective_id=PAIR_ID))(*grads)


def chip_exchange(name, parts):
    nt = len(parts)

    def body(*refs):
        a, r = refs[:nt], refs[nt:2 * nt]
        send, recv = refs[2 * nt:]
        x, y, c, chips = _place()
        _handshake([(*chip, c) for chip in chips])
        cps = []
        for t in range(nt):
            for j, chip in enumerate(chips):
                cp = pltpu.make_async_remote_copy(
                    src_ref=a[t].at[2 * chip[0] + chip[1]], dst_ref=r[t].at[j], send_sem=send.at[t, j],
                    recv_sem=recv.at[t, j], device_id=(*chip, c), device_id_type=MESH)
                cp.start()
                cps.append(cp)
        for cp in cps:
            cp.wait()

    return pl.kernel(
        body, name=name, out_type=[_sds((N_CHIPS - 1,) + a.shape[1:], a.dtype) for a in parts],
        mesh=plsc.ScalarSubcoreMesh(axis_name="sequencer", num_cores=1),
        scratch_types=[pltpu.SemaphoreType.DMA((nt, 3)), pltpu.SemaphoreType.DMA((nt, 3))],
        cost_estimate=_copy_cost(sum(_nbytes(a) for a in parts), 0.75),
        compiler_params=pltpu.CompilerParams(collective_id=CHIP_EXCHANGE_ID))(*parts)


def pair_share(name, halves):
    nt = len(halves)

    def body(*refs):
        h, other = refs[:nt], refs[nt:2 * nt]
        send, recv = refs[2 * nt:]
        x, y, c, _ = _place()
        _handshake([(x, y, 1 - c)])
        cps = []
        for t in range(nt):
            cp = pltpu.make_async_remote_copy(src_ref=h[t], dst_ref=other[t], send_sem=send.at[t], recv_sem=recv.at[t],
                                              device_id=(x, y, 1 - c), device_id_type=MESH)
            cp.start()
            cps.append(cp)
        for cp in cps:
            cp.wait()

    return pl.kernel(
        body, name=name, out_type=[_sds(a.shape, a.dtype) for a in halves],
        mesh=plsc.ScalarSubcoreMesh(axis_name="sequencer", num_cores=1),
        scratch_types=[pltpu.SemaphoreType.DMA((nt,)), pltpu.SemaphoreType.DMA((nt,))],
        cost_estimate=_copy_cost(sum(_nbytes(a) for a in halves), 1),
        compiler_params=pltpu.CompilerParams(collective_id=PAIR_ID))(*halves)


def pack_rows(name, parts, rows):
    cdim = parts[0].shape[1]
    n = len(parts)
    vm = pl.BlockSpec(memory_space=pltpu.VMEM)

    def pack(*refs):
        p, o_ref = refs[:n], refs[n]
        at = 0
        for ref in p:
            o_ref[pl.ds(at, ref.shape[0]), :] = ref[...]
            at += ref.shape[0]
        o_ref[pl.ds(at, rows - at), :] = jnp.zeros((rows - at, cdim), F32)

    return pl.pallas_call(pack, in_specs=[vm] * n, out_specs=vm, out_shape=_sds((rows, cdim), F32), name=name)(*parts)


def all_reduce_small(parts, rows):
    cdim = parts[0].shape[1]
    vm = pl.BlockSpec(memory_space=pltpu.VMEM)
    mine = pack_rows("small_pack", parts, rows)

    def exchange(mine_ref, buf, send, recv, lsem):
        x, y, c, _ = _place()
        me = 4 * x + 2 * y + c
        peers = [(x ^ (k >> 2), y ^ ((k >> 1) & 1), c ^ (k & 1)) for k in range(1, 8)]
        _handshake(peers)
        own = pltpu.make_async_copy(mine_ref, buf.at[me], lsem)
        own.start()
        cps = []
        for k, to in enumerate(peers):
            cp = pltpu.make_async_remote_copy(src_ref=mine_ref, dst_ref=buf.at[me], send_sem=send.at[k], recv_sem=recv.at[k],
                                              device_id=to, device_id_type=MESH)
            cp.start()
            cps.append(cp)
        for k, (px, py, pc) in enumerate(peers):
            pltpu.make_async_remote_copy(src_ref=mine_ref, dst_ref=buf.at[4 * px + 2 * py + pc], send_sem=send.at[k],
                                         recv_sem=recv.at[k], device_id=(x, y, c), device_id_type=MESH).wait_recv()
        for cp in cps:
            cp.wait_send()
        own.wait()

    landed = pl.kernel(
        exchange, name="small_exchange", out_type=_sds((8, rows, cdim), F32),
        mesh=plsc.ScalarSubcoreMesh(axis_name="sequencer", num_cores=1),
        scratch_types=[pltpu.SemaphoreType.DMA((7,)), pltpu.SemaphoreType.DMA((7,)), pltpu.SemaphoreType.DMA],
        cost_estimate=_copy_cost(rows * cdim * 4, 7),
        compiler_params=pltpu.CompilerParams(collective_id=ALL_ID))(mine)

    def total(buf, o_ref):
        acc = buf[0]
        for d in range(1, 8):
            acc = acc + buf[d]
        o_ref[...] = acc

    return pl.pallas_call(total, in_specs=[vm], out_specs=vm, out_shape=_sds((rows, cdim), F32), name="small_sum")(landed)


def pair_sum(g, theirs, core, tm=256):
    _, r, c = g.shape
    tm = _tile(r // 2, tm)
    nh = r // 2 // tm

    def body(core_ref, a_ref, b_ref, o_ref):
        o_ref[...] = (a_ref[...].astype(F32) + b_ref[...].astype(F32)).astype(BF16)

    blk = (N_CHIPS, tm, c)
    return pl.pallas_call(
        body, grid_spec=pltpu.PrefetchScalarGridSpec(
            num_scalar_prefetch=1, grid=(nh,),
            in_specs=[pl.BlockSpec(blk, lambda i, cr: (0, cr[0] * nh + i, 0)), pl.BlockSpec(blk, lambda i, cr: (0, i, 0))],
            out_specs=pl.BlockSpec(blk, lambda i, cr: (0, i, 0))),
        out_shape=_sds(theirs.shape, BF16), compiler_params=_params(("parallel",)), name="pair_sum")(core, g, theirs)


def chip_sum(own, landed, chip, stack, layer, layers, tm=256):
    _, r, c = own.shape
    tm = _tile(r, tm)

    def body(chip_ref, own_ref, l_ref, *rest):
        acc = own_ref[...].astype(F32)
        for j in range(N_CHIPS - 1):
            acc = acc + l_ref[j].astype(F32)
        rest[-1][...] = acc

    in_specs = [pl.BlockSpec((None, tm, c), lambda i, qr: (qr[0], i, 0)),
                pl.BlockSpec((N_CHIPS - 1, tm, c), lambda i, qr: (0, i, 0))]
    args = [chip, own, landed]
    if stack is not None:
        in_specs.append(ANY)
        args.append(stack)
    return pl.pallas_call(
        body, grid_spec=pltpu.PrefetchScalarGridSpec(
            num_scalar_prefetch=1, grid=(r // tm,), in_specs=in_specs,
            out_specs=pl.BlockSpec((None, tm, c), lambda i, qr: (layer, i, 0))),
        out_shape=_sds((layers, r, c), F32), input_output_aliases={3: 0} if stack is not None else {},
        compiler_params=_params(("parallel",)), name="chip_sum")(*args)


def _adamw_math(w, g, m, v):
    bc1 = 1.0 - ADAM_B1 ** ADAM_STEP
    bc2 = 1.0 - ADAM_B2 ** ADAM_STEP
    nm = ADAM_B1 * m + (1.0 - ADAM_B1) * g
    nv = ADAM_B2 * v + (1.0 - ADAM_B2) * (g * g)
    return -ADAM_LR * ((nm / bc1) / (jnp.sqrt(nv / bc2) + ADAM_EPS) + ADAM_WD * w), nm, nv


def vector_update(red, chip, ws, ms, vs, where):
    n = len(ws)
    dd = red.shape[1]

    def body(chip_ref, red_ref, *refs):
        w_r, m_r, v_r = refs[0:n], refs[n:2 * n], refs[2 * n:3 * n]
        g_o, d_o, m_o, v_o = (refs[(3 + k) * n:(4 + k) * n] for k in range(4))
        q = chip_ref[0]

        def chip_block(val, width):
            out = val[:, 0:width]
            for p in range(1, val.shape[1] // width):
                out = jnp.where(q == p, val[:, p * width:(p + 1) * width], out)
            return out

        for k in range(n):
            for idx, r0, nr, cols in where[k]:
                width = w_r[k].shape[-1]
                if cols == "chip" and width * N_CHIPS != dd:
                    g = chip_block(jnp.concatenate([red_ref[pl.ds(r0 + j, 1), :] for j in range(nr)], axis=1), width)
                else:
                    g = red_ref[pl.ds(r0, nr), :]
                    g = chip_block(g, width) if cols == "chip" else g if cols == "all" else g[:, 0:cols]
                delta, nm, nv = _adamw_math(w_r[k][idx], g, m_r[k][idx], v_r[k][idx])
                g_o[k][idx] = g
                d_o[k][idx] = delta
                m_o[k][idx] = nm
                v_o[k][idx] = nv

    vm = pl.BlockSpec(memory_space=pltpu.VMEM)
    outs = pl.pallas_call(
        body, in_specs=[pl.BlockSpec(memory_space=pltpu.SMEM), vm] + [vm] * (3 * n), out_specs=[vm] * (4 * n),
        out_shape=[_sds(w.shape, F32) for w in ws] * 4, name="vector_update")(chip, red, *ws, *ms, *vs)
    return [outs[k * n:(k + 1) * n] for k in range(4)]


def adamw_joined(w, m, v, g_mine, g_theirs, core, tm=512):
    nl, r, c = w.shape
    tm = _tile(r // 2, tm)
    nh = r // 2 // tm

    def body(core_ref, w_ref, m_ref, v_ref, gm_ref, gt_ref, g_ref, d_ref, nm_ref, nv_ref):
        mine = (pl.program_id(1) // nh) == core_ref[0]
        gv = jnp.where(mine, gm_ref[...], gt_ref[...])
        g_ref[...] = gv
        d_ref[...], nm_ref[...], nv_ref[...] = _adamw_math(w_ref[...], gv, m_ref[...], v_ref[...])

    full = pl.BlockSpec((None, tm, c), lambda l, i, cr: (l, i, 0))
    half = pl.BlockSpec((None, tm, c), lambda l, i, cr: (l, i % nh, 0))
    return pl.pallas_call(
        body, grid_spec=pltpu.PrefetchScalarGridSpec(
            num_scalar_prefetch=1, grid=(nl, r // tm), in_specs=[full, full, full, half, half], out_specs=[full] * 4),
        out_shape=[_sds((nl, r, c), F32)] * 4, compiler_params=_params(("parallel", "parallel")),
        name="adamw_joined")(core, w, m, v, g_mine, g_theirs)


WEIGHTS = ['sc_w_in', 'sc_conv_w', 'sc_w_out', 'mla_w_dq', 'mla_g_q', 'mla_w_uq', 'mla_w_dkv', 'mla_g_kv', 'mla_w_uk',
           'mla_w_uv', 'mla_w_o', 'cf_w_pw1', 'cf_b_pw1', 'cf_dw_w', 'cf_dw_b', 'cf_norm_g', 'cf_norm_b', 'cf_w_pw2',
           'cf_b_pw2', 'ff_w1', 'ff_w2', 'ln_mix_g', 'ln_mix_b', 'ln_ff_g', 'ln_ff_b']
ARGS = ['x'] + WEIGHTS + ['loss_target'] + ['m_' + n for n in WEIGHTS] + ['v_' + n for n in WEIGHTS]


def _sq_relu(h):
    r = jnp.maximum(h, jnp.zeros_like(h))
    return r * r


def _mlp_forward(i, x, xb, w1, w2, g, b):
    hb = mm_plain_nn(f"mlp{i}_up", xb, w1, BF16, tn=1024)
    y, yb, xh, rstd = mm_residual_ln(f"mlp{i}_down_ln", hb, w2, x, g, b, tk=2048, a_fn=_sq_relu)
    return (y, yb), dict(xb=xb, hb=hb, xh=xh, rstd=rstd, g=g)


def _mlp_backward(i, dy, sv, w1, w2, dw1, dw2, reduce_after):
    s = dy.shape[0]
    dr, drb, dg, db, _ = ln_backward(f"mlp{i}_ln_bwd", dy, sv["xh"], sv["rstd"], sv["g"])
    tm, tn = _tile(s, 1024), 1024

    def epi(acc, e, o):
        o[0][...] = (acc * (2.0 * jnp.maximum(e[0][...].astype(F32), 0.0))).astype(BF16)

    dhb = mm_nt(f"mlp{i}_down_bwd", drb, w2, s, tm, tn, 1024, epi, [_sds((s, w2.k), BF16)], [_ij(tm, tn)],
                [sv["hb"]], [_ij(tm, tn)])[0]
    g_w2 = mm_tn(f"mlp{i}_dw2", sv["hb"], drb, dw2, s, 512, 1024, a_fn=_sq_relu)
    g_w1 = mm_tn(f"mlp{i}_dw1", sv["xb"], dhb, dw1, s, 1024, 512)
    dhb = reduce_after(dhb, {f"w1_{i}": g_w1, f"w2_{i}": g_w2})
    dx = mm_plain_nt(f"mlp{i}_up_bwd", dhb, w1, F32, tn=1024, tk=2048, add=dr, add_scale=ALPHA)
    return dx, dg, db


def kernel(x, sc_w_in, sc_conv_w, sc_w_out, mla_w_dq, mla_g_q, mla_w_uq, mla_w_dkv, mla_g_kv, mla_w_uk, mla_w_uv, mla_w_o, cf_w_pw1, cf_b_pw1, cf_dw_w, cf_dw_b, cf_norm_g, cf_norm_b, cf_w_pw2, cf_b_pw2, ff_w1, ff_w2, ln_mix_g, ln_mix_b, ln_ff_g, ln_ff_b, loss_target, m_sc_w_in, m_sc_conv_w, m_sc_w_out, m_mla_w_dq, m_mla_g_q, m_mla_w_uq, m_mla_w_dkv, m_mla_g_kv, m_mla_w_uk, m_mla_w_uv, m_mla_w_o, m_cf_w_pw1, m_cf_b_pw1, m_cf_dw_w, m_cf_dw_b, m_cf_norm_g, m_cf_norm_b, m_cf_w_pw2, m_cf_b_pw2, m_ff_w1, m_ff_w2, m_ln_mix_g, m_ln_mix_b, m_ln_ff_g, m_ln_ff_b, v_sc_w_in, v_sc_conv_w, v_sc_w_out, v_mla_w_dq, v_mla_g_q, v_mla_w_uq, v_mla_w_dkv, v_mla_g_kv, v_mla_w_uk, v_mla_w_uv, v_mla_w_o, v_cf_w_pw1, v_cf_b_pw1, v_cf_dw_w, v_cf_dw_b, v_cf_norm_g, v_cf_norm_b, v_cf_w_pw2, v_cf_b_pw2, v_ff_w1, v_ff_w2, v_ln_mix_g, v_ln_mix_b, v_ln_ff_g, v_ln_ff_b):
    given = dict(zip(ARGS, (x, sc_w_in, sc_conv_w, sc_w_out, mla_w_dq, mla_g_q, mla_w_uq, mla_w_dkv, mla_g_kv, mla_w_uk, mla_w_uv, mla_w_o, cf_w_pw1, cf_b_pw1, cf_dw_w, cf_dw_b, cf_norm_g, cf_norm_b, cf_w_pw2, cf_b_pw2, ff_w1, ff_w2, ln_mix_g, ln_mix_b, ln_ff_g, ln_ff_b, loss_target, m_sc_w_in, m_sc_conv_w, m_sc_w_out, m_mla_w_dq, m_mla_g_q, m_mla_w_uq, m_mla_w_dkv, m_mla_g_kv, m_mla_w_uk, m_mla_w_uv, m_mla_w_o, m_cf_w_pw1, m_cf_b_pw1, m_cf_dw_w, m_cf_dw_b, m_cf_norm_g, m_cf_norm_b, m_cf_w_pw2, m_cf_b_pw2, m_ff_w1, m_ff_w2, m_ln_mix_g, m_ln_mix_b, m_ln_ff_g, m_ln_ff_b, v_sc_w_in, v_sc_conv_w, v_sc_w_out, v_mla_w_dq, v_mla_g_q, v_mla_w_uq, v_mla_w_dkv, v_mla_g_kv, v_mla_w_uk, v_mla_w_uv, v_mla_w_o, v_cf_w_pw1, v_cf_b_pw1, v_cf_dw_w, v_cf_dw_b, v_cf_norm_g, v_cf_norm_b, v_cf_w_pw2, v_cf_b_pw2, v_ff_w1, v_ff_w2, v_ln_mix_g, v_ln_mix_b, v_ln_ff_g, v_ln_ff_b)))
    s, d = x.shape[1], x.shape[2]
    d_ff = 4 * d
    dq4 = d // N_CHIPS
    xq = lax.axis_index("x") * 2 + lax.axis_index("y")

    w_dkv_pad = jnp.pad(mla_w_dkv[0], ((0, 0), (0, 128 - QK_ROPE)))
    w_uq_pad = jnp.pad(mla_w_uq[0].reshape(Q_LORA, 2, QK_NOPE + QK_ROPE), ((0, 0), (0, 0), (0, HEAD_PAD - QK_NOPE - QK_ROPE)))
    small = pack_rows("vector_weights_pack", [
        sc_conv_w.reshape(2 * SC_WIDTH, dq4), cf_b_pw1.reshape(2, dq4), cf_dw_w[0], cf_dw_b, cf_norm_g, cf_norm_b,
        cf_b_pw2], 64)
    mlp_w = lambda i: [ff_w1[i].astype(BF16), ff_w2[i].astype(BF16)]
    g_in, g_out, g_w1, g_w2 = [None] * 2, [None] * 2, [None] * DEPTH, [None] * DEPTH
    g_in[0], g_out[0], g_small = gather_shards(
        "gather_mixer0", [sc_w_in[0].astype(BF16), sc_w_out[0].astype(BF16), small], by_columns=(0,))
    (g_w1[0],) = gather_shards("gather_up0", [ff_w1[0].astype(BF16)], by_columns=(0,))
    (g_w2[0],) = gather_shards("gather_down0", [ff_w2[0].astype(BF16)])
    g_dqkv, g_uq, g_uk, g_uv, g_o = gather_shards("gather_mixer1", [
        jnp.concatenate([mla_w_dq[0], w_dkv_pad], axis=1).astype(BF16),
        w_uq_pad.reshape(Q_LORA, 2 * HEAD_PAD).astype(BF16),
        mla_w_uk.reshape(KV_LORA // N_CHIPS, N_HEADS * QK_NOPE).astype(BF16),
        mla_w_uv.reshape(KV_LORA // N_CHIPS, N_HEADS * V_HEAD).astype(BF16), mla_w_o[0].astype(BF16)], by_columns=(1,))
    g_w1[1], g_w2[1] = gather_shards("gather_mlp1", mlp_w(1), by_columns=(0,))
    g_pw1, g_pw2, g_w1[2], g_w2[2] = gather_shards(
        "gather_layer2", [cf_w_pw1[0].astype(BF16), cf_w_pw2[0].astype(BF16)] + mlp_w(2), by_columns=(0, 2))
    g_in[1], g_out[1], g_w1[3], g_w2[3] = gather_shards(
        "gather_layer3", [sc_w_in[1].astype(BF16), sc_w_out[1].astype(BF16)] + mlp_w(3), by_columns=(0, 2))

    wd_t = Q_LORA + KV_LORA + 128
    w_in = [Stk("full", d, 3 * d, g_in[j]) for j in range(2)]
    w_out = [Stk("row", d, d, g_out[j]) for j in range(2)]
    w_dqkv = Stk("row", d, wd_t, g_dqkv)
    w_uq = Stk("full", Q_LORA, N_HEADS * HEAD_PAD, g_uq)
    w_uk = Stk("row", KV_LORA, N_HEADS * QK_NOPE, g_uk)
    w_uv = Stk("row", KV_LORA, N_HEADS * V_HEAD, g_uv)
    w_o = Stk("row", d, d, g_o)
    w_pw1 = Stk("full", d, 2 * d, g_pw1)
    w_pw2 = Stk("row", d, d, g_pw2)
    w_1 = [Stk("full", d, d_ff, g_w1[i]) for i in range(DEPTH)]
    w_2 = [Stk("row", d_ff, d, g_w2[i]) for i in range(DEPTH)]

    def wide(rows):
        return jnp.swapaxes(rows, 0, 1).reshape(rows.shape[1], d)

    conv_w = wide(g_small[:, 0:6]).reshape(2, SC_WIDTH, d)
    b_pw1 = g_small[:, 6:8].reshape(1, 2 * d)
    dw_w = wide(g_small[:, 8:39])
    dw_b, norm_g, norm_b, b_pw2 = (wide(g_small[:, 39 + k:40 + k]) for k in range(4))

    pos = jnp.arange(s, dtype=F32)
    inv_freq = ROPE_THETA ** (-jnp.arange(0, QK_ROPE, 2, dtype=F32) / QK_ROPE)
    ang = pos[:, None] * inv_freq[None, :]
    cos, sin, zero = jnp.cos(ang), jnp.sin(ang), jnp.zeros((s, 128 - QK_ROPE), F32)
    cf = jnp.concatenate([cos, cos, zero], axis=1)
    sf = jnp.concatenate([-sin, sin, zero], axis=1)

    def row(a, i):
        return a[i:i + 1]

    xs = x.reshape(s, d)
    cur = (xs, xs.astype(BF16))
    tape = []
    for i in range(DEPTH):
        mixer, j = i % 3, i // 3
        xf, xb = cur
        lg, lb = row(ln_mix_g, i), row(ln_mix_b, i)
        if mixer == 0:
            u = mm_plain_nn(f"sc{j}_in", xb, w_in[j], F32, tn=3 * dq4)
            gb = short_conv_gate(u, conv_w[j])
            y, yb, xh, rstd = mm_residual_ln(f"sc{j}_out_ln", gb, w_out[j], xf, lg, lb)
            sv = dict(xb=xb, u=u, gb=gb)
        elif mixer == 1:
            t = mm_plain_nn("mla_down", xb, w_dqkv, F32, tn=wd_t // 2)
            cq, ckv, kpe = mla_latents(t, mla_g_q, mla_g_kv, cf, sf)
            qh = mla_queries(cq, w_uq, cf, sf)
            kh = mla_keys(ckv, w_uk, kpe)
            vh = mm_plain_nn("mla_values", ckv, w_uv, BF16, tk=KV_LORA)
            oh = attention(qh, kh, vh)
            y, yb, xh, rstd = mm_residual_ln("mla_out_ln", oh, w_o, xf, lg, lb)
            sv = dict(xb=xb, t=t, cq=cq, ckv=ckv, qh=qh, kh=kh, vh=vh, oh=oh)
        else:
            u = mm_plain_nn("cf_pw1", xb, w_pw1, F32, bias=b_pw1)
            hc = conformer_glu_conv(u, dw_w, dw_b)
            sb = conformer_norm_swish(hc, norm_g, norm_b)
            y, yb, xh, rstd = mm_residual_ln("cf_pw2_ln", sb, w_pw2, xf, lg, lb, bias=b_pw2)
            sv = dict(xb=xb, u=u, hc=hc, sb=sb)
        sv.update(xh=xh, rstd=rstd, g=lg)
        cur, sv_mlp = _mlp_forward(i, y, yb, w_1[i], w_2[i], row(ln_ff_g, i), row(ln_ff_b, i))
        tape.append((sv, sv_mlp))

    dy, loss_part = loss_head(cur[0], loss_target.reshape(s, d))

    grads = {}
    smalls = {}
    g_ln = {n: [None] * DEPTH for n in ("ln_mix_g", "ln_mix_b", "ln_ff_g", "ln_ff_b")}
    conv_grads = [None, None]
    core = lax.axis_index("c").astype(jnp.int32).reshape(1)
    chip = xq.astype(jnp.int32).reshape(1)
    pairs, landed = {}, {}
    ready, theirs = [], {}

    def reduce_after(x, new, early=False):
        out = lax.optimization_barrier((x, *new.values()))
        grads.update(zip(new, out[1:]))
        if early:
            theirs.update(zip(new, pair_exchange(f"pair_exchange_{len(theirs)}", list(out[1:]), True)))
        ready.extend(new)
        return out[0]

    def reduce_layer(i, x):
        late = [n for n in ready if n not in theirs]
        if late:
            theirs.update(zip(late, pair_exchange(f"pair_exchange_layer{i}", [grads[n] for n in late], False)))
        sums = [pair_sum(grads[n], theirs[n], core) for n in ready]
        pairs.update(zip(ready, sums))
        landed.update(zip(ready, chip_exchange(f"chip_exchange_layer{i}", sums)))
        exchanged.append(list(ready))
        ready.clear()
        return lax.optimization_barrier((x, *sums))[0]

    groups = [["in_0", "in_1"], ["out_0", "out_1"], ["dqkv"], ["uq"], ["uk"], ["uv"], ["o"], ["pw1"], ["pw2"],
              [f"w1_{i}" for i in range(DEPTH)], [f"w2_{i}" for i in range(DEPTH)]]
    stacks = [None] * len(groups)
    exchanged = []

    def sum_layer(x, last=False):
        names = exchanged.pop(0)
        if last:
            out = lax.optimization_barrier((x, *[landed[n] for n in names]))
            landed.update(zip(names, out[1:]))
        new = []
        for n in names:
            k = next(k for k, members in enumerate(groups) if n in members)
            stacks[k] = chip_sum(pairs[n], landed[n], chip, stacks[k], groups[k].index(n), len(groups[k]))
            new.append(stacks[k])
        return out[0] if last else lax.optimization_barrier((x, *new))[0]

    for i in reversed(range(DEPTH)):
        mixer, j = i % 3, i // 3
        sv, sv_mlp = tape[i]
        dy, g_ln["ln_ff_g"][i], g_ln["ln_ff_b"][i] = _mlp_backward(
            i, dy, sv_mlp, w_1[i], w_2[i], Stk("col", d, d_ff), Stk("row", d_ff, d),
            lambda x_, new: reduce_after(x_, new, early=i > 0))
        if i == 0:
            dy = reduce_layer("0_mlp", dy)
        dr, drb, g_ln["ln_mix_g"][i], g_ln["ln_mix_b"][i], dr_sum = ln_backward(
            f"mix{i}_ln_bwd", dy, sv["xh"], sv["rstd"], sv["g"])
        if mixer == 0:
            dgate = mm_plain_nt(f"sc{j}_out_bwd", drb, w_out[j], F32)
            dw_out = mm_tn(f"sc{j}_dw_out", sv["gb"], drb, Stk("row", d, d), s, 512, 1024)
            du, conv_grads[j] = short_conv_gate_bwd(sv["u"], conv_w[j], dgate)
            nb = d // 256
            dw_in = mm_tn(
                f"sc{j}_dw_in", sv["xb"], du, Stk("col", d, 3 * d), s, 1024, 256,
                b_spec=pl.BlockSpec((None, s, 256), lambda i_, j_, k_: (j_ // nb, k_, j_ % nb)))
            du = reduce_after(du, {f"in_{j}": dw_in, f"out_{j}": dw_out})
            dy = mm_plain_nt(
                f"sc{j}_in_bwd", du, w_in[j], F32, tn=1024, tk=d, add=dr, add_scale=ALPHA,
                a_spec_fn=(s, lambda tm, tk: pl.BlockSpec((None, tm, tk), lambda i_, j_, k_: (k_, i_, 0))))
        elif mixer == 1:
            do = mm_plain_nt("mla_out_bwd", drb, w_o, BF16)
            g_o = mm_tn("mla_dw_o", sv["oh"], drb, Stk("row", d, d), s, 512, 1024)
            dqh, dkh, dvh = attention_bwd(sv["qh"], sv["kh"], sv["vh"], do)
            dql, dkn, dkpe = mla_unrope_grads(dqh, dkh, cf, sf)
            g_uq = mm_tn("mla_dw_uq", sv["cq"], dql, Stk("col", Q_LORA, N_HEADS * HEAD_PAD), s, Q_LORA, 512)
            dcq = mm_plain_nt("mla_uq_bwd", dql, w_uq, F32, tn=Q_LORA)
            g_uk = mm_tn("mla_dw_uk", sv["ckv"], dkn, Stk("row", KV_LORA, N_HEADS * QK_NOPE), s, KV_LORA, 1024)
            g_uv = mm_tn("mla_dw_uv", sv["ckv"], dvh, Stk("row", KV_LORA, N_HEADS * V_HEAD), s, KV_LORA, 1024)
            dckv = mm_plain_nt("mla_uk_bwd", dkn, w_uk, F32, tn=KV_LORA)
            dckv = mm_plain_nt("mla_uv_bwd", dvh, w_uv, F32, tn=KV_LORA, add=dckv)
            dt, smalls["g_q"], smalls["g_kv"] = mla_latents_bwd(sv["t"], mla_g_q, mla_g_kv, cf, sf, dcq, dckv, dkpe)
            g_dqkv = mm_tn("mla_dw_down", sv["xb"], dt, Stk("row", d, wd_t), s, 512, wd_t)
            dt = reduce_after(dt, {"dqkv": g_dqkv, "uq": g_uq, "uk": g_uk, "uv": g_uv, "o": g_o})
            dy = mm_plain_nt("mla_down_bwd", dt, w_dqkv, F32, tk=wd_t, add=dr, add_scale=ALPHA)
        else:
            dsw = mm_plain_nt("cf_pw2_bwd", drb, w_pw2, F32)
            g_pw2 = mm_tn("cf_dw_pw2", sv["sb"], drb, Stk("row", d, d), s, 512, 1024)
            smalls["b_pw2"] = dr_sum
            dhc, smalls["norm_g"], smalls["norm_b"] = conformer_norm_swish_bwd(sv["hc"], norm_g, norm_b, dsw)
            du, smalls["b_pw1"], smalls["dw_w"], smalls["dw_b"] = conformer_glu_conv_bwd(sv["u"], dw_w, dhc)
            nb = d // 512
            g_pw1 = mm_tn(
                "cf_dw_pw1", sv["xb"], du, Stk("col", d, 2 * d), s, 1024, 512,
                b_spec=pl.BlockSpec((None, s, 512), lambda i_, j_, k_: (j_ // nb, k_, j_ % nb)))
            du = reduce_after(du, {"pw1": g_pw1, "pw2": g_pw2})
            dy = mm_plain_nt(
                "cf_pw1_bwd", du, w_pw1, F32, tn=1024, tk=d, add=dr, add_scale=ALPHA,
                a_spec_fn=(s, lambda tm, tk: pl.BlockSpec((None, tm, tk), lambda i_, j_, k_: (k_, i_, 0))))
        if i < DEPTH - 1:
            dy = sum_layer(dy)
        dy = reduce_layer(i, dy)
    dy = sum_layer(sum_layer(dy, last=True), last=True)
    grad_x = dy.reshape(1, s, d)

    mine = stacks
    other = (pair_share("pair_share_mixers", mine[:9]) + pair_share("pair_share_up", mine[9:10])
             + pair_share("pair_share_down", mine[10:]))

    def padded(get):
        dqkv = jnp.concatenate([get("mla_w_dq")[0], jnp.pad(get("mla_w_dkv")[0], ((0, 0), (0, 128 - QK_ROPE)))], axis=1)
        uq = jnp.pad(get("mla_w_uq")[0].reshape(Q_LORA, 2, QK_NOPE + QK_ROPE),
                     ((0, 0), (0, 0), (0, HEAD_PAD - QK_NOPE - QK_ROPE))).reshape(Q_LORA, 2 * HEAD_PAD)
        return [get("sc_w_in"), get("sc_w_out"), dqkv[None], uq[None],
                get("mla_w_uk").reshape(1, KV_LORA // N_CHIPS, d), get("mla_w_uv").reshape(1, KV_LORA // N_CHIPS, d),
                get("mla_w_o"), get("cf_w_pw1"), get("cf_w_pw2"), get("ff_w1"), get("ff_w2")]

    w_l, m_l, v_l = (padded(lambda n, p=p: given[p + n]) for p in ("", "m_", "v_"))
    res = [adamw_joined(w_l[k], m_l[k], v_l[k], mine[k], other[k], core) for k in range(len(groups))]

    def unpadded(k):
        r_in, r_out, r_dqkv, r_uq, r_uk, r_uv, r_o, r_pw1, r_pw2, r_w1, r_w2 = (r[k] for r in res)
        return {
            "sc_w_in": r_in, "sc_w_out": r_out, "mla_w_dq": r_dqkv[:, :, 0:Q_LORA],
            "mla_w_dkv": r_dqkv[:, :, Q_LORA:Q_LORA + KV_LORA + QK_ROPE],
            "mla_w_uq": r_uq.reshape(1, Q_LORA, 2, HEAD_PAD)[:, :, :, 0:QK_NOPE + QK_ROPE].reshape(mla_w_uq.shape),
            "mla_w_uk": r_uk.reshape(mla_w_uk.shape), "mla_w_uv": r_uv.reshape(mla_w_uv.shape),
            "mla_w_o": r_o, "cf_w_pw1": r_pw1, "cf_w_pw2": r_pw2, "ff_w1": r_w1, "ff_w2": r_w2}

    big_g, big_d, big_m, big_v = (unpadded(k) for k in range(4))

    pad_row = lambda a: jnp.pad(a, ((0, 0), (0, d - a.shape[1])))
    small_parts = ([g for n in ("ln_mix_g", "ln_mix_b", "ln_ff_g", "ln_ff_b") for g in g_ln[n]]
                   + [pad_row(smalls["g_q"]), pad_row(smalls["g_kv"]), conv_grads[0], conv_grads[1],
                      smalls["b_pw1"].reshape(2, d), smalls["dw_w"], smalls["dw_b"], smalls["norm_g"], smalls["norm_b"],
                      smalls["b_pw2"], loss_part])
    red = all_reduce_small(small_parts, 64)
    loss = red[61, 0]

    where = {
        "ln_mix_g": [((), 0, DEPTH, "all")], "ln_mix_b": [((), 4, DEPTH, "all")],
        "ln_ff_g": [((), 8, DEPTH, "all")], "ln_ff_b": [((), 12, DEPTH, "all")],
        "mla_g_q": [((), 16, 1, Q_LORA)], "mla_g_kv": [((), 17, 1, KV_LORA)],
        "sc_conv_w": [((0,), 18, SC_WIDTH, "chip"), ((1,), 21, SC_WIDTH, "chip")],
        "cf_b_pw1": [((), 24, 2, "chip")], "cf_dw_w": [((0,), 26, CONF_WIDTH, "chip")],
        "cf_dw_b": [((), 57, 1, "chip")], "cf_norm_g": [((), 58, 1, "chip")], "cf_norm_b": [((), 59, 1, "chip")],
        "cf_b_pw2": [((), 60, 1, "chip")]}
    vec = list(where)
    vec_res = vector_update(red, chip, [given[n] for n in vec], [given["m_" + n] for n in vec],
                            [given["v_" + n] for n in vec], [where[n] for n in vec])
    gw = dict(big_g)
    upd = {n: [big_d[n], big_m[n], big_v[n]] for n in big_g}
    for k, n in enumerate(vec):
        gw[n] = vec_res[0][k]
        upd[n] = [vec_res[1][k], vec_res[2][k], vec_res[3][k]]

    return (loss, grad_x, *[gw[n] for n in WEIGHTS], *[upd[n][0] for n in WEIGHTS],
            *[upd[n][1] for n in WEIGHTS], *[upd[n][2] for n in WEIGHTS])
```

```python
import jax
import jax.numpy as jnp
from jax import lax
from jax.experimental import pallas as pl
from jax.experimental.pallas import tpu as pltpu
from jax.experimental.pallas import tpu_sc as plsc

F32 = jnp.float32
BF16 = jnp.bfloat16
MESH = pl.DeviceIdType.MESH

DEPTH = 4
ALPHA = (2.0 * DEPTH) ** 0.25
LN_EPS = 1e-5
RMS_EPS = 1e-6
CHUNK_SHIFT = 6
N_HEADS = 8
QK_NOPE = 128
QK_ROPE = 64
V_HEAD = 128
HEAD_PAD = 256
Q_LORA = 384
KV_LORA = 256
ROPE_THETA = 10000.0
SC_WIDTH = 3
CONF_WIDTH = 31
CONV_PAD = 32
CONV_CHUNK = 64
N_CHIPS = 4
ATTN_SCALE = (QK_NOPE + QK_ROPE) ** -0.5

ADAM_LR = 0.001
ADAM_B1 = 0.9
ADAM_B2 = 0.999
ADAM_EPS = 1e-08
ADAM_WD = 0.01
ADAM_STEP = 10

VMEM_LIMIT = 56 * 2**20

NN = (((1,), (0,)), ((), ()))
NT = (((1,), (1,)), ((), ()))
TN = (((0,), (0,)), ((), ()))


def _params(sem=None):
    return pltpu.CompilerParams(dimension_semantics=sem, vmem_limit_bytes=VMEM_LIMIT)


class Stk:
    def __init__(self, kind, k, n, arr=None, layers=None, layer=None):
        self.kind, self.k, self.n, self.layers, self.layer = kind, k, n, layers, layer
        self.plain = (kind == "row" and layers is None) or kind == "full"
        self.kloc = k // N_CHIPS if kind == "row" else k
        self.nloc = n // N_CHIPS if kind == "col" else n
        if arr is not None and self.plain:
            arr = arr.reshape(k, n)
        self.arr = arr

    @property
    def shape(self):
        if self.plain:
            return (self.k, self.n)
        lead = (N_CHIPS,) if self.layers is None else (N_CHIPS, self.layers)
        return lead + (self.kloc, self.nloc)

    def spec(self, bk, bn, f):
        if self.plain:
            return pl.BlockSpec((bk, bn), f)
        assert self.kloc % bk == 0 and self.nloc % bn == 0, (self.kloc, bk, self.nloc, bn)
        pk, pn = self.kloc // bk, self.nloc // bn
        kind, layer = self.kind, self.layer

        def imap(*g):
            kb, nb = f(*g)
            if kind == "row":
                q, kb, nb = kb // pk, kb % pk, nb
            else:
                q, kb, nb = nb // pn, kb, nb % pn
            return (q, kb, nb) if layer is None else (q, layer, kb, nb)

        block = (None, bk, bn) if layer is None else (None, None, bk, bn)
        return pl.BlockSpec(block, imap)


def _mm(name, mode, a, b, grid, a_spec, b_spec, acc_shape, extras, extra_specs, out_shapes, out_specs, epi, a_fn=None):
    nk = grid[2]
    ne = len(extras)

    def body(*refs):
        a_ref, b_ref = refs[0], refs[1]
        e_refs = refs[2:2 + ne]
        av = a_ref[...] if a_fn is None else a_fn(a_ref[...])
        part = lax.dot_general(av, b_ref[...], mode, preferred_element_type=F32)
        if nk == 1:
            epi(part, e_refs, refs[2 + ne:])
            return
        o_refs = refs[2 + ne:-1]
        acc = refs[-1]
        k = pl.program_id(2)

        @pl.when(k == 0)
        def _():
            acc[...] = part

        @pl.when(k > 0)
        def _():
            acc[...] += part

        @pl.when(k == nk - 1)
        def _():
            epi(acc[...], e_refs, o_refs)

    return pl.pallas_call(
        body, grid=grid, in_specs=[a_spec, b_spec, *extra_specs], out_specs=out_specs, out_shape=out_shapes,
        scratch_shapes=[pltpu.VMEM(acc_shape, F32)] if nk > 1 else [],
        compiler_params=_params(("parallel", "parallel", "arbitrary")), name=name)(a, b, *extras)


def _tile(n, t):
    t = min(n, t)
    while n % t:
        t -= 8
    assert t > 0, (n, t)
    return t


def mm_nn(name, a, w, tm, tn, tk, epi, out_shapes, out_specs, extras=(), extra_specs=(), a_spec=None, a_fn=None):
    m = a.shape[0]
    tm, tn, tk = _tile(m, tm), _tile(w.n, tn), _tile(w.k, tk)
    grid = (m // tm, w.n // tn, w.k // tk)
    a_spec = a_spec or pl.BlockSpec((tm, tk), lambda i, j, k: (i, k))
    b_spec = w.spec(tk, tn, lambda i, j, k: (k, j))
    return _mm(name, NN, a, w.arr, grid, a_spec, b_spec, (tm, tn), extras, extra_specs, out_shapes, out_specs, epi, a_fn)


def mm_nt(name, a, w, m, tm, tn, tk, epi, out_shapes, out_specs, extras=(), extra_specs=(), a_spec=None):
    tm, tn, tk = _tile(m, tm), _tile(w.k, tn), _tile(w.n, tk)
    grid = (m // tm, w.k // tn, w.n // tk)
    a_spec = a_spec or pl.BlockSpec((tm, tk), lambda i, j, k: (i, k))
    b_spec = w.spec(tn, tk, lambda i, j, k: (j, k))
    return _mm(name, NT, a, w.arr, grid, a_spec, b_spec, (tm, tn), extras, extra_specs, out_shapes, out_specs, epi)


def mm_tn(name, a, b, dw, s, tm=512, tn=512, tk=4096, a_spec=None, b_spec=None, a_fn=None):
    tm, tn, tk = _tile(dw.k, tm), _tile(dw.n, tn), _tile(s, tk)
    grid = (dw.k // tm, dw.n // tn, s // tk)
    a_spec = a_spec or pl.BlockSpec((tk, tm), lambda i, j, k: (k, i))
    b_spec = b_spec or pl.BlockSpec((tk, tn), lambda i, j, k: (k, j))

    def epi(acc, e, o):
        o[0][...] = acc.astype(BF16)

    out = _mm(name, TN, a, b, grid, a_spec, b_spec, (tm, tn), (), (), [jax.ShapeDtypeStruct(dw.shape, BF16)],
              [dw.spec(tm, tn, lambda i, j, k: (i, j))], epi, a_fn)[0]
    return out.reshape(N_CHIPS, dw.k // N_CHIPS, dw.n) if dw.plain else out


def _sds(shape, dtype):
    return jax.ShapeDtypeStruct(shape, dtype)


def _ij(tm, tn):
    return pl.BlockSpec((tm, tn), lambda i, j, k: (i, j))


def _i0(tm, c):
    return pl.BlockSpec((tm, c), lambda i, j, k: (i, 0))


def _0j(r, tn):
    return pl.BlockSpec((r, tn), lambda i, j, k: (0, j))


def _layer_norm_rows(r, g, b):
    mu = jnp.mean(r, axis=-1, keepdims=True)
    d = r - mu
    var = jnp.mean(d * d, axis=-1, keepdims=True)
    rstd = lax.rsqrt(var + LN_EPS)
    xh = d * rstd
    return xh * g + b, xh, rstd


def mm_residual_ln(name, a, w, x, g, b, bias=None, tm=512, tk=1024, a_fn=None):
    s, d = x.shape
    tm = _tile(s, tm)
    extras = [x, g, b] + ([bias] if bias is not None else [])
    especs = [_i0(tm, d), _0j(1, d), _0j(1, d)] + ([_0j(1, d)] if bias is not None else [])

    def epi(acc, e, o):
        r = ALPHA * e[0][...] + acc
        if bias is not None:
            r = r + e[3][...]
        y, xh, rstd = _layer_norm_rows(r, e[1][...], e[2][...])
        o[0][...] = y
        o[1][...] = y.astype(BF16)
        o[2][...] = xh
        o[3][...] = rstd

    return mm_nn(name, a, w, tm, d, tk, epi,
                 [_sds((s, d), F32), _sds((s, d), BF16), _sds((s, d), F32), _sds((s, 1), F32)],
                 [_i0(tm, d), _i0(tm, d), _i0(tm, d), _i0(tm, 1)], extras, especs, a_fn=a_fn)


def mm_plain_nn(name, a, w, out_dtype, tm=1024, tn=512, tk=1024, bias=None):
    m = a.shape[0]
    tm, tn = _tile(m, tm), _tile(w.n, tn)
    if w.kind == "col":
        tn = _tile(w.nloc, tn)

    def epi(acc, e, o):
        if bias is not None:
            acc = acc + e[0][...]
        o[0][...] = acc.astype(out_dtype)

    extras, especs = ([bias], [_0j(1, tn)]) if bias is not None else ((), ())
    return mm_nn(name, a, w, tm, tn, tk, epi, [_sds((m, w.n), out_dtype)], [_ij(tm, tn)], extras, especs)[0]


def mm_plain_nt(name, a, w, out_dtype, tm=1024, tn=512, tk=1024, add=None, add_scale=1.0, a_spec_fn=None):
    m = a.shape[0] if a_spec_fn is None else a_spec_fn[0]
    tm, tn = _tile(m, tm), _tile(w.k, tn)
    tk = _tile(w.n, tk)
    if w.kind == "col":
        tk = _tile(w.nloc, tk)
    if w.kind == "row" and not w.plain:
        tn = _tile(w.kloc, tn)

    def epi(acc, e, o):
        if add is not None:
            acc = acc + add_scale * e[0][...].astype(F32)
        o[0][...] = acc.astype(out_dtype)

    extras, especs = ([add], [_ij(tm, tn)]) if add is not None else ((), ())
    a_spec = None if a_spec_fn is None else a_spec_fn[1](tm, tk)
    return mm_nt(name, a, w, m, tm, tn, tk, epi, [_sds((m, w.k), out_dtype)], [_ij(tm, tn)], extras, especs,
                 a_spec=a_spec)[0]


def _rows(tm, c):
    return pl.BlockSpec((tm, c), lambda i: (i, 0))


def _fix(shape):
    nd = len(shape)
    return pl.BlockSpec(shape, lambda i: (0,) * nd)


def _accumulate(ref, val):
    @pl.when(pl.program_id(0) == 0)
    def _():
        ref[...] = jnp.zeros_like(ref)

    ref[...] += val


def ln_backward(name, dy, xhat, rstd, g, tm=512):
    s, d = dy.shape
    tm = _tile(s, tm)

    def body(dy_ref, xh_ref, rstd_ref, g_ref, dr_ref, drb_ref, dg_ref, db_ref, ds_ref):
        dyv, xh = dy_ref[...], xh_ref[...]
        dxh = dyv * g_ref[...]
        m1 = jnp.mean(dxh, axis=-1, keepdims=True)
        m2 = jnp.mean(dxh * xh, axis=-1, keepdims=True)
        dr = rstd_ref[...] * (dxh - m1 - xh * m2)
        dr_ref[...] = dr
        drb_ref[...] = dr.astype(BF16)
        _accumulate(dg_ref, jnp.sum(dyv * xh, axis=0, keepdims=True))
        _accumulate(db_ref, jnp.sum(dyv, axis=0, keepdims=True))
        _accumulate(ds_ref, jnp.sum(dr, axis=0, keepdims=True))

    return pl.pallas_call(
        body, grid=(s // tm,),
        in_specs=[_rows(tm, d), _rows(tm, d), _rows(tm, 1), _fix((1, d))],
        out_specs=[_rows(tm, d), _rows(tm, d), _fix((1, d)), _fix((1, d)), _fix((1, d))],
        out_shape=[_sds((s, d), F32), _sds((s, d), BF16), _sds((1, d), F32), _sds((1, d), F32), _sds((1, d), F32)],
        compiler_params=_params(("arbitrary",)), name=name)(dy, xhat, rstd, g)


def loss_head(y, target, tm=512):
    s, d = y.shape
    tm = _tile(s, tm)

    def body(y_ref, t_ref, dy_ref, loss_ref):
        e = y_ref[...] - t_ref[...]
        dy_ref[...] = e * (1.0 / d)
        part = 0.5 * jnp.sum(jnp.mean(e * e, axis=-1, keepdims=True), axis=0, keepdims=True)
        _accumulate(loss_ref, jnp.broadcast_to(part, (1, d)))

    return pl.pallas_call(
        body, grid=(s // tm,), in_specs=[_rows(tm, d), _rows(tm, d)],
        out_specs=[_rows(tm, d), _fix((1, d))], out_shape=[_sds((s, d), F32), _sds((1, d), F32)],
        compiler_params=_params(("arbitrary",)), name="loss_head")(y, target)


def _cols(s, tc, off=0):
    return pl.BlockSpec((s, tc), lambda i: (0, i + off))


def _shift_down(z, sft, rows):
    return jnp.where(rows >= sft, pltpu.roll(z, sft, 0), 0.0)


def _shift_up(z, sft, rows, s):
    return jnp.where(rows < s - sft, pltpu.roll(z, (s - sft) % s, 0), 0.0)


def short_conv_gate(u, conv_w, tc=256):
    s, d3 = u.shape
    d = d3 // 3
    nb = d // tc

    def body(b_ref, c_ref, h_ref, w_ref, o_ref):
        rows = lax.broadcasted_iota(jnp.int32, (s, tc), 0)
        z = c_ref[...] * h_ref[...]
        cz = jnp.zeros((s, tc), F32)
        for k in range(SC_WIDTH):
            sft = SC_WIDTH - 1 - k
            cz = cz + w_ref[pl.ds(k, 1), :] * (_shift_down(z, sft, rows) if sft else z)
        o_ref[...] = (b_ref[...] * cz).astype(BF16)

    return pl.pallas_call(
        body, grid=(nb,),
        in_specs=[_cols(s, tc), _cols(s, tc, nb), _cols(s, tc, 2 * nb), _cols(SC_WIDTH, tc)],
        out_specs=_cols(s, tc), out_shape=_sds((s, d), BF16),
        compiler_params=_params(("parallel",)), name="short_conv_gate")(u, u, u, conv_w)


def short_conv_gate_bwd(u, conv_w, dg, tc=256):
    s, d3 = u.shape
    d = d3 // 3
    nb = d // tc

    def body(b_ref, c_ref, h_ref, w_ref, dg_ref, du_ref, dw_ref):
        rows = lax.broadcasted_iota(jnp.int32, (s, tc), 0)
        c, h, dgv = c_ref[...], h_ref[...], dg_ref[...]
        z = c * h
        dcz = dgv * b_ref[...]
        cz = jnp.zeros((s, tc), F32)
        dz = jnp.zeros((s, tc), F32)
        for k in range(SC_WIDTH):
            sft = SC_WIDTH - 1 - k
            zs = _shift_down(z, sft, rows) if sft else z
            wk = w_ref[pl.ds(k, 1), :]
            cz = cz + wk * zs
            dz = dz + wk * (_shift_up(dcz, sft, rows, s) if sft else dcz)
            dw_ref[pl.ds(k, 1), :] = jnp.sum(dcz * zs, axis=0, keepdims=True)
        du_ref[0] = (dgv * cz).astype(BF16)
        du_ref[1] = (dz * h).astype(BF16)
        du_ref[2] = (dz * c).astype(BF16)

    return pl.pallas_call(
        body, grid=(nb,),
        in_specs=[_cols(s, tc), _cols(s, tc, nb), _cols(s, tc, 2 * nb), _cols(SC_WIDTH, tc), _cols(s, tc)],
        out_specs=[pl.BlockSpec((3, s, tc), lambda i: (0, 0, i)), _cols(SC_WIDTH, tc)],
        out_shape=[_sds((3, s, d), BF16), _sds((SC_WIDTH, d), F32)],
        compiler_params=_params(("parallel",)), name="short_conv_gate_bwd")(u, u, u, conv_w, dg)


def _store_shifted_down(ref, z, rows):
    s, tc = z.shape
    for b in range(8):
        ref[b, pl.ds(0, CONV_PAD), :] = jnp.zeros((CONV_PAD, tc), F32)
        ref[b, pl.ds(CONV_PAD, s), :] = z if b == 0 else _shift_down(z, b, rows)


def _store_shifted_up(ref, z, rows):
    s, tc = z.shape
    for b in range(8):
        ref[b, pl.ds(0, s), :] = z if b == 0 else _shift_up(z, b, rows, s)
        ref[b, pl.ds(s, CONV_PAD), :] = jnp.zeros((CONV_PAD, tc), F32)


def conformer_glu_conv(u, dw_w, dw_b, tc=128):
    s, d2 = u.shape
    d = d2 // 2
    nb = d // tc

    ch = min(CONV_CHUNK, s)

    def body(a_ref, g_ref, w_ref, b_ref, o_ref, down):
        rows = lax.broadcasted_iota(jnp.int32, (s, tc), 0)
        _store_shifted_down(down, a_ref[...] * jax.nn.sigmoid(g_ref[...]), rows)

        def chunk(ci, carry):
            r0 = pl.multiple_of(ci * ch, ch)
            acc = jnp.broadcast_to(b_ref[...], (ch, tc))
            for k in range(CONF_WIDTH):
                sft = CONF_WIDTH - 1 - k
                acc = acc + w_ref[pl.ds(k, 1), :] * down[sft % 8, pl.ds(CONV_PAD + r0 - (sft // 8) * 8, ch), :]
            o_ref[pl.ds(r0, ch), :] = acc
            return carry

        lax.fori_loop(0, s // ch, chunk, 0)

    return pl.pallas_call(
        body, grid=(nb,),
        in_specs=[_cols(s, tc), _cols(s, tc, nb), _cols(CONF_WIDTH, tc), _cols(1, tc)],
        out_specs=_cols(s, tc), out_shape=_sds((s, d), F32),
        scratch_shapes=[pltpu.VMEM((8, CONV_PAD + s, tc), F32)],
        compiler_params=_params(("parallel",)), name="conformer_glu_conv")(u, u, dw_w, dw_b)


def conformer_glu_conv_bwd(u, dw_w, dhc, tc=128):
    s, d2 = u.shape
    d = d2 // 2
    nb = d // tc
    ch = min(CONV_CHUNK, s)

    def body(a_ref, g_ref, w_ref, dhc_ref, du_ref, dbias_ref, dw_ref, db_ref, down, up, dw_acc, dh_buf):
        rows = lax.broadcasted_iota(jnp.int32, (s, tc), 0)
        a = a_ref[...]
        sg = jax.nn.sigmoid(g_ref[...])
        dhcv = dhc_ref[...]
        _store_shifted_down(down, a * sg, rows)
        _store_shifted_up(up, dhcv, rows)
        dw_acc[...] = jnp.zeros_like(dw_acc)

        def chunk(ci, carry):
            r0 = pl.multiple_of(ci * ch, ch)
            dc = dhc_ref[pl.ds(r0, ch), :]
            dh = jnp.zeros((ch, tc), F32)
            for k in range(CONF_WIDTH):
                sft = CONF_WIDTH - 1 - k
                a8, b = (sft // 8) * 8, sft % 8
                dh = dh + w_ref[pl.ds(k, 1), :] * up[b, pl.ds(r0 + a8, ch), :]
                prod = dc * down[b, pl.ds(CONV_PAD + r0 - a8, ch), :]
                dw_acc[k] += jnp.sum(prod.reshape(ch // 8, 8, tc), axis=0)
            dh_buf[pl.ds(r0, ch), :] = dh
            return carry

        lax.fori_loop(0, s // ch, chunk, 0)
        dh = dh_buf[...]
        da = dh * sg
        dgate = dh * a * sg * (1.0 - sg)
        du_ref[0] = da.astype(BF16)
        du_ref[1] = dgate.astype(BF16)
        dbias_ref[pl.ds(0, 1), :] = jnp.sum(da, axis=0, keepdims=True)
        dbias_ref[pl.ds(1, 1), :] = jnp.sum(dgate, axis=0, keepdims=True)
        db_ref[...] = jnp.sum(dhcv, axis=0, keepdims=True)
        for k in range(CONF_WIDTH):
            dw_ref[pl.ds(k, 1), :] = jnp.sum(dw_acc[k], axis=0, keepdims=True)

    return pl.pallas_call(
        body, grid=(nb,),
        in_specs=[_cols(s, tc), _cols(s, tc, nb), _cols(CONF_WIDTH, tc), _cols(s, tc)],
        out_specs=[pl.BlockSpec((2, s, tc), lambda i: (0, 0, i)), _cols(2, tc), _cols(CONF_WIDTH, tc), _cols(1, tc)],
        out_shape=[_sds((2, s, d), BF16), _sds((2, d), F32), _sds((CONF_WIDTH, d), F32), _sds((1, d), F32)],
        scratch_shapes=[pltpu.VMEM((8, CONV_PAD + s, tc), F32), pltpu.VMEM((8, CONV_PAD + s, tc), F32),
                        pltpu.VMEM((CONF_WIDTH + 1, 8, tc), F32), pltpu.VMEM((s, tc), F32)],
        compiler_params=_params(("parallel",)), name="conformer_glu_conv_bwd")(u, u, dw_w, dhc)


def conformer_norm_swish(hc, g, b, tm=512):
    s, d = hc.shape
    tm = _tile(s, tm)

    def body(h_ref, g_ref, b_ref, o_ref):
        n, _, _ = _layer_norm_rows(h_ref[...], g_ref[...], b_ref[...])
        o_ref[...] = (n * jax.nn.sigmoid(n)).astype(BF16)

    return pl.pallas_call(
        body, grid=(s // tm,), in_specs=[_rows(tm, d), _fix((1, d)), _fix((1, d))], out_specs=_rows(tm, d),
        out_shape=_sds((s, d), BF16), compiler_params=_params(("parallel",)), name="conformer_norm_swish")(hc, g, b)


def conformer_norm_swish_bwd(hc, g, b, ds, tm=512):
    s, d = hc.shape
    tm = _tile(s, tm)

    def body(h_ref, g_ref, b_ref, ds_ref, dh_ref, dg_ref, db_ref):
        n, nh, rstd = _layer_norm_rows(h_ref[...], g_ref[...], b_ref[...])
        sg = jax.nn.sigmoid(n)
        dn = ds_ref[...] * (sg * (1.0 + n * (1.0 - sg)))
        dnh = dn * g_ref[...]
        m1 = jnp.mean(dnh, axis=-1, keepdims=True)
        m2 = jnp.mean(dnh * nh, axis=-1, keepdims=True)
        dh_ref[...] = rstd * (dnh - m1 - nh * m2)
        _accumulate(dg_ref, jnp.sum(dn * nh, axis=0, keepdims=True))
        _accumulate(db_ref, jnp.sum(dn, axis=0, keepdims=True))

    return pl.pallas_call(
        body, grid=(s // tm,), in_specs=[_rows(tm, d), _fix((1, d)), _fix((1, d)), _rows(tm, d)],
        out_specs=[_rows(tm, d), _fix((1, d)), _fix((1, d))],
        out_shape=[_sds((s, d), F32), _sds((1, d), F32), _sds((1, d), F32)],
        compiler_params=_params(("arbitrary",)), name="conformer_norm_swish_bwd")(hc, g, b, ds)


def _swap_halves(x):
    lane = lax.broadcasted_iota(jnp.int32, x.shape, 1)
    return jnp.where(lane < QK_ROPE // 2, pltpu.roll(x, 128 - QK_ROPE // 2, 1), pltpu.roll(x, QK_ROPE // 2, 1))


def _rope(x, cf, sf):
    return x * cf + _swap_halves(x) * sf


def _unrope(dx, cf, sf):
    return dx * cf - _swap_halves(dx) * sf


def _rms_rows(x, g):
    r = lax.rsqrt(jnp.mean(x * x, axis=-1, keepdims=True) + RMS_EPS)
    return x * r, r


def mla_latents(t, g_q, g_kv, cf, sf, tm=512):
    s = t.shape[0]
    tm = _tile(s, tm)

    def body(t_ref, gq_ref, gkv_ref, cf_ref, sf_ref, cq_ref, ckv_ref, kpe_ref):
        xq, _ = _rms_rows(t_ref[:, 0:Q_LORA], gq_ref[...])
        cq_ref[...] = (xq * gq_ref[...]).astype(BF16)
        xkv, _ = _rms_rows(t_ref[:, Q_LORA:Q_LORA + KV_LORA], gkv_ref[...])
        ckv_ref[...] = (xkv * gkv_ref[...]).astype(BF16)
        kpe_ref[...] = _rope(t_ref[:, Q_LORA + KV_LORA:], cf_ref[...], sf_ref[...]).astype(BF16)

    w = Q_LORA + KV_LORA + 128
    return pl.pallas_call(
        body, grid=(s // tm,),
        in_specs=[_rows(tm, w), _fix((1, Q_LORA)), _fix((1, KV_LORA)), _rows(tm, 128), _rows(tm, 128)],
        out_specs=[_rows(tm, Q_LORA), _rows(tm, KV_LORA), _rows(tm, 128)],
        out_shape=[_sds((s, Q_LORA), BF16), _sds((s, KV_LORA), BF16), _sds((s, 128), BF16)],
        compiler_params=_params(("parallel",)), name="mla_latents")(t, g_q, g_kv, cf, sf)


def mla_latents_bwd(t, g_q, g_kv, cf, sf, dcq, dckv, dkpe, tm=512):
    s = t.shape[0]
    tm = _tile(s, tm)
    w = Q_LORA + KV_LORA + 128

    def rms_bwd(x, g, dy):
        xh, r = _rms_rows(x, g)
        dxh = dy * g
        return r * (dxh - xh * jnp.mean(dxh * xh, axis=-1, keepdims=True)), jnp.sum(dy * xh, axis=0, keepdims=True)

    def body(t_ref, gq_ref, gkv_ref, cf_ref, sf_ref, dcq_ref, dckv_ref, dkpe_ref, dt_ref, dgq_ref, dgkv_ref):
        dxq, dgq = rms_bwd(t_ref[:, 0:Q_LORA], gq_ref[...], dcq_ref[...])
        dxkv, dgkv = rms_bwd(t_ref[:, Q_LORA:Q_LORA + KV_LORA], gkv_ref[...], dckv_ref[...])
        dt_ref[:, 0:Q_LORA] = dxq.astype(BF16)
        dt_ref[:, Q_LORA:Q_LORA + KV_LORA] = dxkv.astype(BF16)
        dt_ref[:, Q_LORA + KV_LORA:] = _unrope(dkpe_ref[...], cf_ref[...], sf_ref[...]).astype(BF16)
        _accumulate(dgq_ref, dgq)
        _accumulate(dgkv_ref, dgkv)

    return pl.pallas_call(
        body, grid=(s // tm,),
        in_specs=[_rows(tm, w), _fix((1, Q_LORA)), _fix((1, KV_LORA)), _rows(tm, 128), _rows(tm, 128),
                  _rows(tm, Q_LORA), _rows(tm, KV_LORA), _rows(tm, 128)],
        out_specs=[_rows(tm, w), _fix((1, Q_LORA)), _fix((1, KV_LORA))],
        out_shape=[_sds((s, w), BF16), _sds((1, Q_LORA), F32), _sds((1, KV_LORA), F32)],
        compiler_params=_params(("arbitrary",)), name="mla_latents_bwd")(t, g_q, g_kv, cf, sf, dcq, dckv, dkpe)


def mla_queries(cq, w_uq, cf, sf, tm=2048):
    s = cq.shape[0]
    tm = _tile(s, tm)

    def epi(acc, e, o):
        o[0][:, 0:QK_NOPE] = acc[:, 0:QK_NOPE].astype(BF16)
        o[0][:, QK_NOPE:] = _rope(acc[:, QK_NOPE:], e[0][...], e[1][...]).astype(BF16)

    return mm_nn("mla_queries", cq, w_uq, tm, HEAD_PAD, Q_LORA, epi, [_sds((s, N_HEADS * HEAD_PAD), BF16)],
                 [_ij(tm, HEAD_PAD)], [cf, sf], [_i0(tm, 128), _i0(tm, 128)])[0]


def mla_keys(ckv, w_uk, kpe, tm=2048):
    s = ckv.shape[0]
    tm = _tile(s, tm)

    def epi(acc, e, o):
        o[0][:, 0:QK_NOPE] = acc.astype(BF16)
        o[0][:, QK_NOPE:] = e[0][...]

    return mm_nn("mla_keys", ckv, w_uk, tm, QK_NOPE, KV_LORA, epi, [_sds((s, N_HEADS * HEAD_PAD), BF16)],
                 [_ij(tm, HEAD_PAD)], [kpe], [_i0(tm, 128)])[0]


def _masked_scores(q, k, qi, tq, kv):
    sc = lax.dot_general(q, k, NT, preferred_element_type=F32) * ATTN_SCALE
    row = lax.broadcasted_iota(jnp.int32, (tq, kv), 0) + qi * tq
    col = lax.broadcasted_iota(jnp.int32, (tq, kv), 1)
    ok = lax.shift_right_logical(col, CHUNK_SHIFT) <= lax.shift_right_logical(row, CHUNK_SHIFT)
    return jnp.where(ok, sc, -1e30)


def attention(q, k, v, tq=512):
    s = q.shape[0]
    tq = _tile(s, tq)
    nq = s // tq

    def body(q_ref, k_ref, v_ref, o_ref):
        for qi in range(nq):
            kv = (qi + 1) * tq
            sc = _masked_scores(q_ref[pl.ds(qi * tq, tq), :], k_ref[pl.ds(0, kv), :], qi, tq, kv)
            p = jnp.exp(sc - jnp.max(sc, axis=-1, keepdims=True))
            o = lax.dot_general(p.astype(BF16), v_ref[pl.ds(0, kv), :], NN, preferred_element_type=F32)
            o_ref[pl.ds(qi * tq, tq), :] = (o / jnp.sum(p, axis=-1, keepdims=True)).astype(BF16)

    hq = pl.BlockSpec((s, HEAD_PAD), lambda h: (0, h))
    hv = pl.BlockSpec((s, V_HEAD), lambda h: (0, h))
    return pl.pallas_call(
        body, grid=(N_HEADS,), in_specs=[hq, hq, hv], out_specs=hv, out_shape=_sds((s, N_HEADS * V_HEAD), BF16),
        compiler_params=_params(("parallel",)), name="attention")(q, k, v)


def attention_bwd(q, k, v, do, tq=512):
    s = q.shape[0]
    tq = _tile(s, tq)
    nq = s // tq

    def body(q_ref, k_ref, v_ref, do_ref, dq_ref, dk_ref, dv_ref, dk_acc, dv_acc):
        dk_acc[...] = jnp.zeros_like(dk_acc)
        dv_acc[...] = jnp.zeros_like(dv_acc)
        for qi in range(nq):
            kv = (qi + 1) * tq
            qt = q_ref[pl.ds(qi * tq, tq), :]
            kt = k_ref[pl.ds(0, kv), :]
            dot = do_ref[pl.ds(qi * tq, tq), :]
            sc = _masked_scores(qt, kt, qi, tq, kv)
            p = jnp.exp(sc - jnp.max(sc, axis=-1, keepdims=True))
            p = p / jnp.sum(p, axis=-1, keepdims=True)
            dp = lax.dot_general(dot, v_ref[pl.ds(0, kv), :], NT, preferred_element_type=F32)
            delta = jnp.sum(p * dp, axis=-1, keepdims=True)
            ds = (p * (dp - delta) * ATTN_SCALE).astype(BF16)
            dq_ref[pl.ds(qi * tq, tq), :] = lax.dot_general(ds, kt, NN, preferred_element_type=F32).astype(BF16)
            dk_acc[pl.ds(0, kv), :] += lax.dot_general(ds, qt, TN, preferred_element_type=F32)
            dv_acc[pl.ds(0, kv), :] += lax.dot_general(p.astype(BF16), dot, TN, preferred_element_type=F32)
        dk_ref[...] = dk_acc[...].astype(BF16)
        dv_ref[...] = dv_acc[...].astype(BF16)

    hq = pl.BlockSpec((s, HEAD_PAD), lambda h: (0, h))
    hv = pl.BlockSpec((s, V_HEAD), lambda h: (0, h))
    return pl.pallas_call(
        body, grid=(N_HEADS,), in_specs=[hq, hq, hv, hv], out_specs=[hq, hq, hv],
        out_shape=[_sds((s, N_HEADS * HEAD_PAD), BF16), _sds((s, N_HEADS * HEAD_PAD), BF16),
                   _sds((s, N_HEADS * V_HEAD), BF16)],
        scratch_shapes=[pltpu.VMEM((s, HEAD_PAD), F32), pltpu.VMEM((s, V_HEAD), F32)],
        compiler_params=_params(("parallel",)), name="attention_bwd")(q, k, v, do)


def mla_unrope_grads(dq, dk, cf, sf, tm=512):
    s = dq.shape[0]
    tm = _tile(s, tm)

    def body(dq_ref, dk_ref, cf_ref, sf_ref, dql_ref, dkn_ref, dkpe_ref):
        cfv, sfv = cf_ref[...], sf_ref[...]
        dkpe = jnp.zeros((tm, 128), F32)
        for h in range(N_HEADS):
            lo = h * HEAD_PAD
            dql_ref[:, lo:lo + QK_NOPE] = dq_ref[:, lo:lo + QK_NOPE]
            dql_ref[:, lo + QK_NOPE:lo + HEAD_PAD] = _unrope(
                dq_ref[:, lo + QK_NOPE:lo + HEAD_PAD].astype(F32), cfv, sfv).astype(BF16)
            dkn_ref[:, h * QK_NOPE:(h + 1) * QK_NOPE] = dk_ref[:, lo:lo + QK_NOPE]
            dkpe = dkpe + dk_ref[:, lo + QK_NOPE:lo + HEAD_PAD].astype(F32)
        dkpe_ref[...] = dkpe

    wq = N_HEADS * HEAD_PAD
    return pl.pallas_call(
        body, grid=(s // tm,), in_specs=[_rows(tm, wq), _rows(tm, wq), _rows(tm, 128), _rows(tm, 128)],
        out_specs=[_rows(tm, wq), _rows(tm, N_HEADS * QK_NOPE), _rows(tm, 128)],
        out_shape=[_sds((s, wq), BF16), _sds((s, N_HEADS * QK_NOPE), BF16), _sds((s, 128), F32)],
        compiler_params=_params(("parallel",)), name="mla_unrope_grads")(dq, dk, cf, sf)


ANY = pl.BlockSpec(memory_space=pl.ANY)
GATHER_ID = 1
CHIP_EXCHANGE_ID = 2
PAIR_ID = 3
ALL_ID = 4


def _nbytes(a):
    return a.size * a.dtype.itemsize


def _copy_cost(operand_bytes, sent_fraction):
    sent = int(operand_bytes * sent_fraction)
    return pl.CostEstimate(flops=0, transcendentals=0, bytes_accessed=2 * sent, remote_bytes_transferred=sent)


def _handshake(peers):
    barrier = pltpu.get_barrier_semaphore()
    for peer in peers:
        pl.semaphore_signal(barrier, inc=1, device_id=peer, device_id_type=MESH)
    pl.semaphore_wait(barrier, len(peers))


def _place():
    x, y, c = lax.axis_index("x"), lax.axis_index("y"), lax.axis_index("c")
    chips = [(1 - x, y), (x, 1 - y), (1 - x, 1 - y)]
    return x, y, c, chips


def _half(ref, hc, axis=0):
    n = ref.shape[axis] // 2
    idx = (slice(None),) * axis + (pl.ds(hc * n, n),)
    return ref.at[idx]


def gather_shards(name, tensors, by_columns=()):
    nt = len(tensors)

    def body(*refs):
        a, g = refs[:nt], refs[nt:2 * nt]
        send, recv = refs[2 * nt:]
        x, y, c, _ = _place()
        q = 2 * x + y
        sib, xn, yn = (x, y, 1 - c), (1 - x, y, c), (x, 1 - y, c)
        q_xn, q_yn, q_diag = 2 * (1 - x) + y, 2 * x + 1 - y, 2 * (1 - x) + 1 - y
        _handshake([sib, xn, yn])

        def whole(t, p):
            if t in by_columns:
                n = a[t].shape[1]
                return g[t].at[:, pl.ds(p * n, n)]
            return g[t].at[p]

        def part(t, p, hc, quarter=None):
            rows = a[t].shape[0]
            if quarter is None:
                return whole(t, p).at[pl.ds(hc * (rows // 2), rows // 2)]
            return whole(t, p).at[pl.ds(hc * (rows // 2) + quarter * (rows // 4), rows // 4)]

        def rc(t, k, src, dst, to):
            return pltpu.make_async_remote_copy(src_ref=src, dst_ref=dst, send_sem=send.at[t, k], recv_sem=recv.at[t, k],
                                                device_id=to, device_id_type=MESH)

        sent = []

        def go(cp):
            cp.start()
            sent.append(cp)

        def landed(t, k, piece, frm):
            rc(t, k, piece, piece, frm).wait_recv()
            return piece

        for t in range(nt):
            go(rc(t, 8, a[t], whole(t, q), sib))
            mine = _half(a[t], c)
            go(rc(t, 0, mine, part(t, q, c), xn))
            go(rc(t, 1, mine, part(t, q, c), yn))
        for t in range(nt):
            from_y = landed(t, 1, part(t, q_yn, c), yn)
            go(rc(t, 2, part(t, q_yn, c, 0), part(t, q_yn, c, 0), xn))
            go(rc(t, 5, from_y, from_y, sib))
            from_x = landed(t, 0, part(t, q_xn, c), xn)
            go(rc(t, 3, part(t, q_xn, c, 1), part(t, q_xn, c, 1), yn))
            go(rc(t, 4, from_x, from_x, sib))
        for t in range(nt):
            for k, frm in ((2, xn), (3, yn)):
                piece = landed(t, k, part(t, q_diag, c, k - 2), frm)
                go(rc(t, 4 + k, piece, piece, sib))
        for t in range(nt):
            landed(t, 4, part(t, q_xn, 1 - c), sib)
            landed(t, 5, part(t, q_yn, 1 - c), sib)
            landed(t, 6, part(t, q_diag, 1 - c, 0), sib)
            landed(t, 7, part(t, q_diag, 1 - c, 1), sib)
            landed(t, 8, whole(t, q), sib)
        for cp in sent:
            cp.wait_send()

    return pl.kernel(
        body, name=name,
        out_type=[_sds((a.shape[0], N_CHIPS * a.shape[1]) if t in by_columns else (N_CHIPS,) + a.shape, a.dtype)
                  for t, a in enumerate(tensors)],
        mesh=plsc.ScalarSubcoreMesh(axis_name="sequencer", num_cores=1),
        scratch_types=[pltpu.SemaphoreType.DMA((nt, 9)), pltpu.SemaphoreType.DMA((nt, 9))],
        cost_estimate=_copy_cost(sum(_nbytes(a) for a in tensors), 4),
        compiler_params=pltpu.CompilerParams(collective_id=GATHER_ID))(*tensors)


def pair_exchange(name, grads, on_sequencer):
    nt = len(grads)

    def body(*refs):
        g, theirs = refs[:nt], refs[nt:2 * nt]
        send, recv = refs[2 * nt:]
        x, y, c, _ = _place()
        if on_sequencer:
            _handshake([(x, y, 1 - c)])
        cps = []
        for t in range(nt):
            cp = pltpu.make_async_remote_copy(src_ref=_half(g[t], 1 - c, 1), dst_ref=theirs[t], send_sem=send.at[t],
                                              recv_sem=recv.at[t], device_id=(x, y, 1 - c), device_id_type=MESH)
            cp.start()
            cps.append(cp)
        for cp in cps:
            cp.wait()

    if not on_sequencer:
        return pl.pallas_call(
            body, in_specs=[ANY] * nt, out_specs=[ANY] * nt,
            out_shape=[_sds((N_CHIPS, a.shape[1] // 2, a.shape[2]), a.dtype) for a in grads],
            scratch_shapes=[pltpu.SemaphoreType.DMA((nt,)), pltpu.SemaphoreType.DMA((nt,))],
            name=name)(*grads)
    return pl.kernel(
        body, name=name, out_type=[_sds((N_CHIPS, a.shape[1] // 2, a.shape[2]), a.dtype) for a in grads],
        mesh=plsc.ScalarSubcoreMesh(axis_name="sequencer", num_cores=1),
        scratch_types=[pltpu.SemaphoreType.DMA((nt,)), pltpu.SemaphoreType.DMA((nt,))],
        cost_estimate=_copy_cost(sum(_nbytes(a) for a in grads), 0.5),
        compiler_params=pltpu.CompilerParams(collective_id=PAIR_ID))(*grads)


def chip_exchange(name, parts):
    nt = len(parts)

    def body(*refs):
        a, r = refs[:nt], refs[nt:2 * nt]
        send, recv = refs[2 * nt:]
        x, y, c, chips = _place()
        _handshake([(*chip, c) for chip in chips])
        cps = []
        for t in range(nt):
            for j, chip in enumerate(chips):
                cp = pltpu.make_async_remote_copy(
                    src_ref=a[t].at[2 * chip[0] + chip[1]], dst_ref=r[t].at[j], send_sem=send.at[t, j],
                    recv_sem=recv.at[t, j], device_id=(*chip, c), device_id_type=MESH)
                cp.start()
                cps.append(cp)
        for cp in cps:
            cp.wait()

    return pl.kernel(
        body, name=name, out_type=[_sds((N_CHIPS - 1,) + a.shape[1:], a.dtype) for a in parts],
        mesh=plsc.ScalarSubcoreMesh(axis_name="sequencer", num_cores=1),
        scratch_types=[pltpu.SemaphoreType.DMA((nt, 3)), pltpu.SemaphoreType.DMA((nt, 3))],
        cost_estimate=_copy_cost(sum(_nbytes(a) for a in parts), 0.75),
        compiler_params=pltpu.CompilerParams(collective_id=CHIP_EXCHANGE_ID))(*parts)


def pair_share(name, halves):
    nt = len(halves)

    def body(*refs):
        h, other = refs[:nt], refs[nt:2 * nt]
        send, recv = refs[2 * nt:]
        x, y, c, _ = _place()
        _handshake([(x, y, 1 - c)])
        cps = []
        for t in range(nt):
            cp = pltpu.make_async_remote_copy(src_ref=h[t], dst_ref=other[t], send_sem=send.at[t], recv_sem=recv.at[t],
                                              device_id=(x, y, 1 - c), device_id_type=MESH)
            cp.start()
            cps.append(cp)
        for cp in cps:
            cp.wait()

    return pl.kernel(
        body, name=name, out_type=[_sds(a.shape, a.dtype) for a in halves],
        mesh=plsc.ScalarSubcoreMesh(axis_name="sequencer", num_cores=1),
        scratch_types=[pltpu.SemaphoreType.DMA((nt,)), pltpu.SemaphoreType.DMA((nt,))],
        cost_estimate=_copy_cost(sum(_nbytes(a) for a in halves), 1),
        compiler_params=pltpu.CompilerParams(collective_id=PAIR_ID))(*halves)


def pack_rows(name, parts, rows):
    cdim = parts[0].shape[1]
    n = len(parts)
    vm = pl.BlockSpec(memory_space=pltpu.VMEM)

    def pack(*refs):
        p, o_ref = refs[:n], refs[n]
        at = 0
        for ref in p:
            o_ref[pl.ds(at, ref.shape[0]), :] = ref[...]
            at += ref.shape[0]
        o_ref[pl.ds(at, rows - at), :] = jnp.zeros((rows - at, cdim), F32)

    return pl.pallas_call(pack, in_specs=[vm] * n, out_specs=vm, out_shape=_sds((rows, cdim), F32), name=name)(*parts)


def all_reduce_small(parts, rows):
    cdim = parts[0].shape[1]
    vm = pl.BlockSpec(memory_space=pltpu.VMEM)
    mine = pack_rows("small_pack", parts, rows)

    def exchange(mine_ref, buf, send, recv, lsem):
        x, y, c, _ = _place()
        me = 4 * x + 2 * y + c
        peers = [(x ^ (k >> 2), y ^ ((k >> 1) & 1), c ^ (k & 1)) for k in range(1, 8)]
        _handshake(peers)
        own = pltpu.make_async_copy(mine_ref, buf.at[me], lsem)
        own.start()
        cps = []
        for k, to in enumerate(peers):
            cp = pltpu.make_async_remote_copy(src_ref=mine_ref, dst_ref=buf.at[me], send_sem=send.at[k], recv_sem=recv.at[k],
                                              device_id=to, device_id_type=MESH)
            cp.start()
            cps.append(cp)
        for k, (px, py, pc) in enumerate(peers):
            pltpu.make_async_remote_copy(src_ref=mine_ref, dst_ref=buf.at[4 * px + 2 * py + pc], send_sem=send.at[k],
                                         recv_sem=recv.at[k], device_id=(x, y, c), device_id_type=MESH).wait_recv()
        for cp in cps:
            cp.wait_send()
        own.wait()

    landed = pl.kernel(
        exchange, name="small_exchange", out_type=_sds((8, rows, cdim), F32),
        mesh=plsc.ScalarSubcoreMesh(axis_name="sequencer", num_cores=1),
        scratch_types=[pltpu.SemaphoreType.DMA((7,)), pltpu.SemaphoreType.DMA((7,)), pltpu.SemaphoreType.DMA],
        cost_estimate=_copy_cost(rows * cdim * 4, 7),
        compiler_params=pltpu.CompilerParams(collective_id=ALL_ID))(mine)

    def total(buf, o_ref):
        acc = buf[0]
        for d in range(1, 8):
            acc = acc + buf[d]
        o_ref[...] = acc

    return pl.pallas_call(total, in_specs=[vm], out_specs=vm, out_shape=_sds((rows, cdim), F32), name="small_sum")(landed)


def pair_sum(g, theirs, core, tm=256):
    _, r, c = g.shape
    tm = _tile(r // 2, tm)
    nh = r // 2 // tm

    def body(core_ref, a_ref, b_ref, o_ref):
        o_ref[...] = (a_ref[...].astype(F32) + b_ref[...].astype(F32)).astype(BF16)

    blk = (N_CHIPS, tm, c)
    return pl.pallas_call(
        body, grid_spec=pltpu.PrefetchScalarGridSpec(
            num_scalar_prefetch=1, grid=(nh,),
            in_specs=[pl.BlockSpec(blk, lambda i, cr: (0, cr[0] * nh + i, 0)), pl.BlockSpec(blk, lambda i, cr: (0, i, 0))],
            out_specs=pl.BlockSpec(blk, lambda i, cr: (0, i, 0))),
        out_shape=_sds(theirs.shape, BF16), compiler_params=_params(("parallel",)), name="pair_sum")(core, g, theirs)


def chip_sum(own, landed, chip, stack, layer, layers, tm=256):
    _, r, c = own.shape
    tm = _tile(r, tm)

    def body(chip_ref, own_ref, l_ref, *rest):
        acc = own_ref[...].astype(F32)
        for j in range(N_CHIPS - 1):
            acc = acc + l_ref[j].astype(F32)
        rest[-1][...] = acc

    in_specs = [pl.BlockSpec((None, tm, c), lambda i, qr: (qr[0], i, 0)),
                pl.BlockSpec((N_CHIPS - 1, tm, c), lambda i, qr: (0, i, 0))]
    args = [chip, own, landed]
    if stack is not None:
        in_specs.append(ANY)
        args.append(stack)
    return pl.pallas_call(
        body, grid_spec=pltpu.PrefetchScalarGridSpec(
            num_scalar_prefetch=1, grid=(r // tm,), in_specs=in_specs,
            out_specs=pl.BlockSpec((None, tm, c), lambda i, qr: (layer, i, 0))),
        out_shape=_sds((layers, r, c), F32), input_output_aliases={3: 0} if stack is not None else {},
        compiler_params=_params(("parallel",)), name="chip_sum")(*args)


def _adamw_math(w, g, m, v):
    bc1 = 1.0 - ADAM_B1 ** ADAM_STEP
    bc2 = 1.0 - ADAM_B2 ** ADAM_STEP
    nm = ADAM_B1 * m + (1.0 - ADAM_B1) * g
    nv = ADAM_B2 * v + (1.0 - ADAM_B2) * (g * g)
    return -ADAM_LR * ((nm / bc1) / (jnp.sqrt(nv / bc2) + ADAM_EPS) + ADAM_WD * w), nm, nv


def vector_update(red, chip, ws, ms, vs, where):
    n = len(ws)
    dd = red.shape[1]

    def body(chip_ref, red_ref, *refs):
        w_r, m_r, v_r = refs[0:n], refs[n:2 * n], refs[2 * n:3 * n]
        g_o, d_o, m_o, v_o = (refs[(3 + k) * n:(4 + k) * n] for k in range(4))
        q = chip_ref[0]

        def chip_block(val, width):
            out = val[:, 0:width]
            for p in range(1, val.shape[1] // width):
                out = jnp.where(q == p, val[:, p * width:(p + 1) * width], out)
            return out

        for k in range(n):
            for idx, r0, nr, cols in where[k]:
                width = w_r[k].shape[-1]
                if cols == "chip" and width * N_CHIPS != dd:
                    g = chip_block(jnp.concatenate([red_ref[pl.ds(r0 + j, 1), :] for j in range(nr)], axis=1), width)
                else:
                    g = red_ref[pl.ds(r0, nr), :]
                    g = chip_block(g, width) if cols == "chip" else g if cols == "all" else g[:, 0:cols]
                delta, nm, nv = _adamw_math(w_r[k][idx], g, m_r[k][idx], v_r[k][idx])
                g_o[k][idx] = g
                d_o[k][idx] = delta
                m_o[k][idx] = nm
                v_o[k][idx] = nv

    vm = pl.BlockSpec(memory_space=pltpu.VMEM)
    outs = pl.pallas_call(
        body, in_specs=[pl.BlockSpec(memory_space=pltpu.SMEM), vm] + [vm] * (3 * n), out_specs=[vm] * (4 * n),
        out_shape=[_sds(w.shape, F32) for w in ws] * 4, name="vector_update")(chip, red, *ws, *ms, *vs)
    return [outs[k * n:(k + 1) * n] for k in range(4)]


def adamw_joined(w, m, v, g_mine, g_theirs, core, tm=512):
    nl, r, c = w.shape
    tm = _tile(r // 2, tm)
    nh = r // 2 // tm

    def body(core_ref, w_ref, m_ref, v_ref, gm_ref, gt_ref, g_ref, d_ref, nm_ref, nv_ref):
        mine = (pl.program_id(1) // nh) == core_ref[0]
        gv = jnp.where(mine, gm_ref[...], gt_ref[...])
        g_ref[...] = gv
        d_ref[...], nm_ref[...], nv_ref[...] = _adamw_math(w_ref[...], gv, m_ref[...], v_ref[...])

    full = pl.BlockSpec((None, tm, c), lambda l, i, cr: (l, i, 0))
    half = pl.BlockSpec((None, tm, c), lambda l, i, cr: (l, i % nh, 0))
    return pl.pallas_call(
        body, grid_spec=pltpu.PrefetchScalarGridSpec(
            num_scalar_prefetch=1, grid=(nl, r // tm), in_specs=[full, full, full, half, half], out_specs=[full] * 4),
        out_shape=[_sds((nl, r, c), F32)] * 4, compiler_params=_params(("parallel", "parallel")),
        name="adamw_joined")(core, w, m, v, g_mine, g_theirs)


WEIGHTS = ['sc_w_in', 'sc_conv_w', 'sc_w_out', 'mla_w_dq', 'mla_g_q', 'mla_w_uq', 'mla_w_dkv', 'mla_g_kv', 'mla_w_uk',
           'mla_w_uv', 'mla_w_o', 'cf_w_pw1', 'cf_b_pw1', 'cf_dw_w', 'cf_dw_b', 'cf_norm_g', 'cf_norm_b', 'cf_w_pw2',
           'cf_b_pw2', 'ff_w1', 'ff_w2', 'ln_mix_g', 'ln_mix_b', 'ln_ff_g', 'ln_ff_b']
ARGS = ['x'] + WEIGHTS + ['loss_target'] + ['m_' + n for n in WEIGHTS] + ['v_' + n for n in WEIGHTS]


def _sq_relu(h):
    r = jnp.maximum(h, jnp.zeros_like(h))
    return r * r


def _mlp_forward(i, x, xb, w1, w2, g, b):
    hb = mm_plain_nn(f"mlp{i}_up", xb, w1, BF16, tn=1024)
    y, yb, xh, rstd = mm_residual_ln(f"mlp{i}_down_ln", hb, w2, x, g, b, tk=4096, a_fn=_sq_relu)
    return (y, yb), dict(xb=xb, hb=hb, xh=xh, rstd=rstd, g=g)


def _mlp_backward(i, dy, sv, w1, w2, dw1, dw2, reduce_after):
    s = dy.shape[0]
    dr, drb, dg, db, _ = ln_backward(f"mlp{i}_ln_bwd", dy, sv["xh"], sv["rstd"], sv["g"])
    tm, tn = _tile(s, 1024), 1024

    def epi(acc, e, o):
        o[0][...] = (acc * (2.0 * jnp.maximum(e[0][...].astype(F32), 0.0))).astype(BF16)

    dhb = mm_nt(f"mlp{i}_down_bwd", drb, w2, s, tm, tn, 1024, epi, [_sds((s, w2.k), BF16)], [_ij(tm, tn)],
                [sv["hb"]], [_ij(tm, tn)])[0]
    g_w2 = mm_tn(f"mlp{i}_dw2", sv["hb"], drb, dw2, s, 512, 1024, a_fn=_sq_relu)
    g_w1 = mm_tn(f"mlp{i}_dw1", sv["xb"], dhb, dw1, s, 1024, 512)
    dhb = reduce_after(dhb, {f"w1_{i}": g_w1, f"w2_{i}": g_w2})
    dx = mm_plain_nt(f"mlp{i}_up_bwd", dhb, w1, F32, tm=512, tn=1024, tk=4096, add=dr, add_scale=ALPHA)
    return dx, dg, db


def kernel(x, sc_w_in, sc_conv_w, sc_w_out, mla_w_dq, mla_g_q, mla_w_uq, mla_w_dkv, mla_g_kv, mla_w_uk, mla_w_uv, mla_w_o, cf_w_pw1, cf_b_pw1, cf_dw_w, cf_dw_b, cf_norm_g, cf_norm_b, cf_w_pw2, cf_b_pw2, ff_w1, ff_w2, ln_mix_g, ln_mix_b, ln_ff_g, ln_ff_b, loss_target, m_sc_w_in, m_sc_conv_w, m_sc_w_out, m_mla_w_dq, m_mla_g_q, m_mla_w_uq, m_mla_w_dkv, m_mla_g_kv, m_mla_w_uk, m_mla_w_uv, m_mla_w_o, m_cf_w_pw1, m_cf_b_pw1, m_cf_dw_w, m_cf_dw_b, m_cf_norm_g, m_cf_norm_b, m_cf_w_pw2, m_cf_b_pw2, m_ff_w1, m_ff_w2, m_ln_mix_g, m_ln_mix_b, m_ln_ff_g, m_ln_ff_b, v_sc_w_in, v_sc_conv_w, v_sc_w_out, v_mla_w_dq, v_mla_g_q, v_mla_w_uq, v_mla_w_dkv, v_mla_g_kv, v_mla_w_uk, v_mla_w_uv, v_mla_w_o, v_cf_w_pw1, v_cf_b_pw1, v_cf_dw_w, v_cf_dw_b, v_cf_norm_g, v_cf_norm_b, v_cf_w_pw2, v_cf_b_pw2, v_ff_w1, v_ff_w2, v_ln_mix_g, v_ln_mix_b, v_ln_ff_g, v_ln_ff_b):
    given = dict(zip(ARGS, (x, sc_w_in, sc_conv_w, sc_w_out, mla_w_dq, mla_g_q, mla_w_uq, mla_w_dkv, mla_g_kv, mla_w_uk, mla_w_uv, mla_w_o, cf_w_pw1, cf_b_pw1, cf_dw_w, cf_dw_b, cf_norm_g, cf_norm_b, cf_w_pw2, cf_b_pw2, ff_w1, ff_w2, ln_mix_g, ln_mix_b, ln_ff_g, ln_ff_b, loss_target, m_sc_w_in, m_sc_conv_w, m_sc_w_out, m_mla_w_dq, m_mla_g_q, m_mla_w_uq, m_mla_w_dkv, m_mla_g_kv, m_mla_w_uk, m_mla_w_uv, m_mla_w_o, m_cf_w_pw1, m_cf_b_pw1, m_cf_dw_w, m_cf_dw_b, m_cf_norm_g, m_cf_norm_b, m_cf_w_pw2, m_cf_b_pw2, m_ff_w1, m_ff_w2, m_ln_mix_g, m_ln_mix_b, m_ln_ff_g, m_ln_ff_b, v_sc_w_in, v_sc_conv_w, v_sc_w_out, v_mla_w_dq, v_mla_g_q, v_mla_w_uq, v_mla_w_dkv, v_mla_g_kv, v_mla_w_uk, v_mla_w_uv, v_mla_w_o, v_cf_w_pw1, v_cf_b_pw1, v_cf_dw_w, v_cf_dw_b, v_cf_norm_g, v_cf_norm_b, v_cf_w_pw2, v_cf_b_pw2, v_ff_w1, v_ff_w2, v_ln_mix_g, v_ln_mix_b, v_ln_ff_g, v_ln_ff_b)))
    s, d = x.shape[1], x.shape[2]
    d_ff = 4 * d
    dq4 = d // N_CHIPS
    xq = lax.axis_index("x") * 2 + lax.axis_index("y")

    w_dkv_pad = jnp.pad(mla_w_dkv[0], ((0, 0), (0, 128 - QK_ROPE)))
    w_uq_pad = jnp.pad(mla_w_uq[0].reshape(Q_LORA, 2, QK_NOPE + QK_ROPE), ((0, 0), (0, 0), (0, HEAD_PAD - QK_NOPE - QK_ROPE)))
    small = pack_rows("vector_weights_pack", [
        sc_conv_w.reshape(2 * SC_WIDTH, dq4), cf_b_pw1.reshape(2, dq4), cf_dw_w[0], cf_dw_b, cf_norm_g, cf_norm_b,
        cf_b_pw2], 64)
    mlp_w = lambda i: [ff_w1[i].astype(BF16), ff_w2[i].astype(BF16)]
    g_in, g_out, g_w1, g_w2 = [None] * 2, [None] * 2, [None] * DEPTH, [None] * DEPTH
    g_in[0], g_out[0], g_small = gather_shards(
        "gather_mixer0", [sc_w_in[0].astype(BF16), sc_w_out[0].astype(BF16), small], by_columns=(0,))
    (g_w1[0],) = gather_shards("gather_up0", [ff_w1[0].astype(BF16)], by_columns=(0,))
    (g_w2[0],) = gather_shards("gather_down0", [ff_w2[0].astype(BF16)])
    g_dqkv, g_uq, g_uk, g_uv, g_o = gather_shards("gather_mixer1", [
        jnp.concatenate([mla_w_dq[0], w_dkv_pad], axis=1).astype(BF16),
        w_uq_pad.reshape(Q_LORA, 2 * HEAD_PAD).astype(BF16),
        mla_w_uk.reshape(KV_LORA // N_CHIPS, N_HEADS * QK_NOPE).astype(BF16),
        mla_w_uv.reshape(KV_LORA // N_CHIPS, N_HEADS * V_HEAD).astype(BF16), mla_w_o[0].astype(BF16)], by_columns=(1,))
    g_w1[1], g_w2[1] = gather_shards("gather_mlp1", mlp_w(1), by_columns=(0,))
    g_pw1, g_pw2, g_w1[2], g_w2[2] = gather_shards(
        "gather_layer2", [cf_w_pw1[0].astype(BF16), cf_w_pw2[0].astype(BF16)] + mlp_w(2), by_columns=(0, 2))
    g_in[1], g_out[1], g_w1[3], g_w2[3] = gather_shards(
        "gather_layer3", [sc_w_in[1].astype(BF16), sc_w_out[1].astype(BF16)] + mlp_w(3), by_columns=(0, 2))

    wd_t = Q_LORA + KV_LORA + 128
    w_in = [Stk("full", d, 3 * d, g_in[j]) for j in range(2)]
    w_out = [Stk("row", d, d, g_out[j]) for j in range(2)]
    w_dqkv = Stk("row", d, wd_t, g_dqkv)
    w_uq = Stk("full", Q_LORA, N_HEADS * HEAD_PAD, g_uq)
    w_uk = Stk("row", KV_LORA, N_HEADS * QK_NOPE, g_uk)
    w_uv = Stk("row", KV_LORA, N_HEADS * V_HEAD, g_uv)
    w_o = Stk("row", d, d, g_o)
    w_pw1 = Stk("full", d, 2 * d, g_pw1)
    w_pw2 = Stk("row", d, d, g_pw2)
    w_1 = [Stk("full", d, d_ff, g_w1[i]) for i in range(DEPTH)]
    w_2 = [Stk("row", d_ff, d, g_w2[i]) for i in range(DEPTH)]

    def wide(rows):
        return jnp.swapaxes(rows, 0, 1).reshape(rows.shape[1], d)

    conv_w = wide(g_small[:, 0:6]).reshape(2, SC_WIDTH, d)
    b_pw1 = g_small[:, 6:8].reshape(1, 2 * d)
    dw_w = wide(g_small[:, 8:39])
    dw_b, norm_g, norm_b, b_pw2 = (wide(g_small[:, 39 + k:40 + k]) for k in range(4))

    pos = jnp.arange(s, dtype=F32)
    inv_freq = ROPE_THETA ** (-jnp.arange(0, QK_ROPE, 2, dtype=F32) / QK_ROPE)
    ang = pos[:, None] * inv_freq[None, :]
    cos, sin, zero = jnp.cos(ang), jnp.sin(ang), jnp.zeros((s, 128 - QK_ROPE), F32)
    cf = jnp.concatenate([cos, cos, zero], axis=1)
    sf = jnp.concatenate([-sin, sin, zero], axis=1)

    def row(a, i):
        return a[i:i + 1]

    xs = x.reshape(s, d)
    cur = (xs, xs.astype(BF16))
    tape = []
    for i in range(DEPTH):
        mixer, j = i % 3, i // 3
        xf, xb = cur
        lg, lb = row(ln_mix_g, i), row(ln_mix_b, i)
        if mixer == 0:
            u = mm_plain_nn(f"sc{j}_in", xb, w_in[j], F32, tn=3 * dq4)
            gb = short_conv_gate(u, conv_w[j])
            y, yb, xh, rstd = mm_residual_ln(f"sc{j}_out_ln", gb, w_out[j], xf, lg, lb)
            sv = dict(xb=xb, u=u, gb=gb)
        elif mixer == 1:
            t = mm_plain_nn("mla_down", xb, w_dqkv, F32, tn=wd_t // 2)
            cq, ckv, kpe = mla_latents(t, mla_g_q, mla_g_kv, cf, sf)
            qh = mla_queries(cq, w_uq, cf, sf)
            kh = mla_keys(ckv, w_uk, kpe)
            vh = mm_plain_nn("mla_values", ckv, w_uv, BF16, tk=KV_LORA)
            oh = attention(qh, kh, vh)
            y, yb, xh, rstd = mm_residual_ln("mla_out_ln", oh, w_o, xf, lg, lb)
            sv = dict(xb=xb, t=t, cq=cq, ckv=ckv, qh=qh, kh=kh, vh=vh, oh=oh)
        else:
            u = mm_plain_nn("cf_pw1", xb, w_pw1, F32, bias=b_pw1)
            hc = conformer_glu_conv(u, dw_w, dw_b)
            sb = conformer_norm_swish(hc, norm_g, norm_b)
            y, yb, xh, rstd = mm_residual_ln("cf_pw2_ln", sb, w_pw2, xf, lg, lb, bias=b_pw2)
            sv = dict(xb=xb, u=u, hc=hc, sb=sb)
        sv.update(xh=xh, rstd=rstd, g=lg)
        cur, sv_mlp = _mlp_forward(i, y, yb, w_1[i], w_2[i], row(ln_ff_g, i), row(ln_ff_b, i))
        tape.append((sv, sv_mlp))

    dy, loss_part = loss_head(cur[0], loss_target.reshape(s, d))

    grads = {}
    smalls = {}
    g_ln = {n: [None] * DEPTH for n in ("ln_mix_g", "ln_mix_b", "ln_ff_g", "ln_ff_b")}
    conv_grads = [None, None]
    core = lax.axis_index("c").astype(jnp.int32).reshape(1)
    chip = xq.astype(jnp.int32).reshape(1)
    pairs, landed = {}, {}
    ready, theirs = [], {}

    def reduce_after(x, new, early=False):
        out = lax.optimization_barrier((x, *new.values()))
        grads.update(zip(new, out[1:]))
        if early:
            theirs.update(zip(new, pair_exchange(f"pair_exchange_{len(theirs)}", list(out[1:]), True)))
        ready.extend(new)
        return out[0]

    def reduce_layer(i, x):
        late = [n for n in ready if n not in theirs]
        if late:
            theirs.update(zip(late, pair_exchange(f"pair_exchange_layer{i}", [grads[n] for n in late], False)))
        sums = [pair_sum(grads[n], theirs[n], core) for n in ready]
        pairs.update(zip(ready, sums))
        landed.update(zip(ready, chip_exchange(f"chip_exchange_layer{i}", sums)))
        exchanged.append(list(ready))
        ready.clear()
        return lax.optimization_barrier((x, *sums))[0]

    groups = [["in_0", "in_1"], ["out_0", "out_1"], ["dqkv"], ["uq"], ["uk"], ["uv"], ["o"], ["pw1"], ["pw2"],
              [f"w1_{i}" for i in range(DEPTH)], [f"w2_{i}" for i in range(DEPTH)]]
    stacks = [None] * len(groups)
    exchanged = []

    def sum_layer(x, last=False):
        names = exchanged.pop(0)
        if last:
            out = lax.optimization_barrier((x, *[landed[n] for n in names]))
            landed.update(zip(names, out[1:]))
        new = []
        for n in names:
            k = next(k for k, members in enumerate(groups) if n in members)
            stacks[k] = chip_sum(pairs[n], landed[n], chip, stacks[k], groups[k].index(n), len(groups[k]))
            new.append(stacks[k])
        return out[0] if last else lax.optimization_barrier((x, *new))[0]

    for i in reversed(range(DEPTH)):
        mixer, j = i % 3, i // 3
        sv, sv_mlp = tape[i]
        dy, g_ln["ln_ff_g"][i], g_ln["ln_ff_b"][i] = _mlp_backward(
            i, dy, sv_mlp, w_1[i], w_2[i], Stk("col", d, d_ff), Stk("row", d_ff, d),
            lambda x_, new: reduce_after(x_, new, early=i > 0))
        if i == 0:
            dy = reduce_layer("0_mlp", dy)
        dr, drb, g_ln["ln_mix_g"][i], g_ln["ln_mix_b"][i], dr_sum = ln_backward(
            f"mix{i}_ln_bwd", dy, sv["xh"], sv["rstd"], sv["g"])
        if mixer == 0:
            dgate = mm_plain_nt(f"sc{j}_out_bwd", drb, w_out[j], F32)
            dw_out = mm_tn(f"sc{j}_dw_out", sv["gb"], drb, Stk("row", d, d), s, 512, 1024)
            du, conv_grads[j] = short_conv_gate_bwd(sv["u"], conv_w[j], dgate)
            nb = d // 256
            dw_in = mm_tn(
                f"sc{j}_dw_in", sv["xb"], du, Stk("col", d, 3 * d), s, 1024, 256,
                b_spec=pl.BlockSpec((None, s, 256), lambda i_, j_, k_: (j_ // nb, k_, j_ % nb)))
            du = reduce_after(du, {f"in_{j}": dw_in, f"out_{j}": dw_out})
            dy = mm_plain_nt(
                f"sc{j}_in_bwd", du, w_in[j], F32, tn=1024, tk=d, add=dr, add_scale=ALPHA,
                a_spec_fn=(s, lambda tm, tk: pl.BlockSpec((None, tm, tk), lambda i_, j_, k_: (k_, i_, 0))))
        elif mixer == 1:
            do = mm_plain_nt("mla_out_bwd", drb, w_o, BF16)
            g_o = mm_tn("mla_dw_o", sv["oh"], drb, Stk("row", d, d), s, 512, 1024)
            dqh, dkh, dvh = attention_bwd(sv["qh"], sv["kh"], sv["vh"], do)
            dql, dkn, dkpe = mla_unrope_grads(dqh, dkh, cf, sf)
            g_uq = mm_tn("mla_dw_uq", sv["cq"], dql, Stk("col", Q_LORA, N_HEADS * HEAD_PAD), s, Q_LORA, 512)
            dcq = mm_plain_nt("mla_uq_bwd", dql, w_uq, F32, tn=Q_LORA)
            g_uk = mm_tn("mla_dw_uk", sv["ckv"], dkn, Stk("row", KV_LORA, N_HEADS * QK_NOPE), s, KV_LORA, 1024)
            g_uv = mm_tn("mla_dw_uv", sv["ckv"], dvh, Stk("row", KV_LORA, N_HEADS * V_HEAD), s, KV_LORA, 1024)
            dckv = mm_plain_nt("mla_uk_bwd", dkn, w_uk, F32, tn=KV_LORA)
            dckv = mm_plain_nt("mla_uv_bwd", dvh, w_uv, F32, tn=KV_LORA, add=dckv)
            dt, smalls["g_q"], smalls["g_kv"] = mla_latents_bwd(sv["t"], mla_g_q, mla_g_kv, cf, sf, dcq, dckv, dkpe)
            g_dqkv = mm_tn("mla_dw_down", sv["xb"], dt, Stk("row", d, wd_t), s, 512, wd_t)
            dt = reduce_after(dt, {"dqkv": g_dqkv, "uq": g_uq, "uk": g_uk, "uv": g_uv, "o": g_o})
            dy = mm_plain_nt("mla_down_bwd", dt, w_dqkv, F32, tk=wd_t, add=dr, add_scale=ALPHA)
        else:
            dsw = mm_plain_nt("cf_pw2_bwd", drb, w_pw2, F32)
            g_pw2 = mm_tn("cf_dw_pw2", sv["sb"], drb, Stk("row", d, d), s, 512, 1024)
            smalls["b_pw2"] = dr_sum
            dhc, smalls["norm_g"], smalls["norm_b"] = conformer_norm_swish_bwd(sv["hc"], norm_g, norm_b, dsw)
            du, smalls["b_pw1"], smalls["dw_w"], smalls["dw_b"] = conformer_glu_conv_bwd(sv["u"], dw_w, dhc)
            nb = d // 512
            g_pw1 = mm_tn(
                "cf_dw_pw1", sv["xb"], du, Stk("col", d, 2 * d), s, 1024, 512,
                b_spec=pl.BlockSpec((None, s, 512), lambda i_, j_, k_: (j_ // nb, k_, j_ % nb)))
            du = reduce_after(du, {"pw1": g_pw1, "pw2": g_pw2})
            dy = mm_plain_nt(
                "cf_pw1_bwd", du, w_pw1, F32, tn=1024, tk=d, add=dr, add_scale=ALPHA,
                a_spec_fn=(s, lambda tm, tk: pl.BlockSpec((None, tm, tk), lambda i_, j_, k_: (k_, i_, 0))))
        if i < DEPTH - 1:
            dy = sum_layer(dy)
        dy = reduce_layer(i, dy)
    dy = sum_layer(sum_layer(dy, last=True), last=True)
    grad_x = dy.reshape(1, s, d)

    mine = stacks
    other = (pair_share("pair_share_mixers", mine[:9]) + pair_share("pair_share_up", mine[9:10])
             + pair_share("pair_share_down", mine[10:]))

    def padded(get):
        dqkv = jnp.concatenate([get("mla_w_dq")[0], jnp.pad(get("mla_w_dkv")[0], ((0, 0), (0, 128 - QK_ROPE)))], axis=1)
        uq = jnp.pad(get("mla_w_uq")[0].reshape(Q_LORA, 2, QK_NOPE + QK_ROPE),
                     ((0, 0), (0, 0), (0, HEAD_PAD - QK_NOPE - QK_ROPE))).reshape(Q_LORA, 2 * HEAD_PAD)
        return [get("sc_w_in"), get("sc_w_out"), dqkv[None], uq[None],
                get("mla_w_uk").reshape(1, KV_LORA // N_CHIPS, d), get("mla_w_uv").reshape(1, KV_LORA // N_CHIPS, d),
                get("mla_w_o"), get("cf_w_pw1"), get("cf_w_pw2"), get("ff_w1"), get("ff_w2")]

    w_l, m_l, v_l = (padded(lambda n, p=p: given[p + n]) for p in ("", "m_", "v_"))
    res = [adamw_joined(w_l[k], m_l[k], v_l[k], mine[k], other[k], core) for k in range(len(groups))]

    def unpadded(k):
        r_in, r_out, r_dqkv, r_uq, r_uk, r_uv, r_o, r_pw1, r_pw2, r_w1, r_w2 = (r[k] for r in res)
        return {
            "sc_w_in": r_in, "sc_w_out": r_out, "mla_w_dq": r_dqkv[:, :, 0:Q_LORA],
            "mla_w_dkv": r_dqkv[:, :, Q_LORA:Q_LORA + KV_LORA + QK_ROPE],
            "mla_w_uq": r_uq.reshape(1, Q_LORA, 2, HEAD_PAD)[:, :, :, 0:QK_NOPE + QK_ROPE].reshape(mla_w_uq.shape),
            "mla_w_uk": r_uk.reshape(mla_w_uk.shape), "mla_w_uv": r_uv.reshape(mla_w_uv.shape),
            "mla_w_o": r_o, "cf_w_pw1": r_pw1, "cf_w_pw2": r_pw2, "ff_w1": r_w1, "ff_w2": r_w2}

    big_g, big_d, big_m, big_v = (unpadded(k) for k in range(4))

    pad_row = lambda a: jnp.pad(a, ((0, 0), (0, d - a.shape[1])))
    small_parts = ([g for n in ("ln_mix_g", "ln_mix_b", "ln_ff_g", "ln_ff_b") for g in g_ln[n]]
                   + [pad_row(smalls["g_q"]), pad_row(smalls["g_kv"]), conv_grads[0], conv_grads[1],
                      smalls["b_pw1"].reshape(2, d), smalls["dw_w"], smalls["dw_b"], smalls["norm_g"], smalls["norm_b"],
                      smalls["b_pw2"], loss_part])
    red = all_reduce_small(small_parts, 64)
    loss = red[61, 0]

    where = {
        "ln_mix_g": [((), 0, DEPTH, "all")], "ln_mix_b": [((), 4, DEPTH, "all")],
        "ln_ff_g": [((), 8, DEPTH, "all")], "ln_ff_b": [((), 12, DEPTH, "all")],
        "mla_g_q": [((), 16, 1, Q_LORA)], "mla_g_kv": [((), 17, 1, KV_LORA)],
        "sc_conv_w": [((0,), 18, SC_WIDTH, "chip"), ((1,), 21, SC_WIDTH, "chip")],
        "cf_b_pw1": [((), 24, 2, "chip")], "cf_dw_w": [((0,), 26, CONF_WIDTH, "chip")],
        "cf_dw_b": [((), 57, 1, "chip")], "cf_norm_g": [((), 58, 1, "chip")], "cf_norm_b": [((), 59, 1, "chip")],
        "cf_b_pw2": [((), 60, 1, "chip")]}
    vec = list(where)
    vec_res = vector_update(red, chip, [given[n] for n in vec], [given["m_" + n] for n in vec],
                            [given["v_" + n] for n in vec], [where[n] for n in vec])
    gw = dict(big_g)
    upd = {n: [big_d[n], big_m[n], big_v[n]] for n in big_g}
    for k, n in enumerate(vec):
        gw[n] = vec_res[0][k]
        upd[n] = [vec_res[1][k], vec_res[2][k], vec_res[3][k]]

    return (loss, grad_x, *[gw[n] for n in WEIGHTS], *[upd[n][0] for n in WEIGHTS],
            *[upd[n][1] for n in WEIGHTS], *[upd[n][2] for n in WEIGHTS])
```

```python
import jax
import jax.numpy as jnp
from jax import lax
from jax.experimental import pallas as pl
from jax.experimental.pallas import tpu as pltpu
from jax.experimental.pallas import tpu_sc as plsc

F32 = jnp.float32
BF16 = jnp.bfloat16
MESH = pl.DeviceIdType.MESH

DEPTH = 4
ALPHA = (2.0 * DEPTH) ** 0.25
LN_EPS = 1e-5
RMS_EPS = 1e-6
CHUNK_SHIFT = 6
N_HEADS = 8
QK_NOPE = 128
QK_ROPE = 64
V_HEAD = 128
HEAD_PAD = 256
Q_LORA = 384
KV_LORA = 256
ROPE_THETA = 10000.0
SC_WIDTH = 3
CONF_WIDTH = 31
CONV_PAD = 32
CONV_CHUNK = 64
N_CHIPS = 4
ATTN_SCALE = (QK_NOPE + QK_ROPE) ** -0.5

ADAM_LR = 0.001
ADAM_B1 = 0.9
ADAM_B2 = 0.999
ADAM_EPS = 1e-08
ADAM_WD = 0.01
ADAM_STEP = 10

VMEM_LIMIT = 56 * 2**20

NN = (((1,), (0,)), ((), ()))
NT = (((1,), (1,)), ((), ()))
TN = (((0,), (0,)), ((), ()))


def _params(sem=None):
    return pltpu.CompilerParams(dimension_semantics=sem, vmem_limit_bytes=VMEM_LIMIT)


class Stk:
    def __init__(self, kind, k, n, arr=None, layers=None, layer=None):
        self.kind, self.k, self.n, self.layers, self.layer = kind, k, n, layers, layer
        self.plain = (kind == "row" and layers is None) or kind == "full"
        self.kloc = k // N_CHIPS if kind == "row" else k
        self.nloc = n // N_CHIPS if kind == "col" else n
        if arr is not None and self.plain:
            arr = arr.reshape(k, n)
        self.arr = arr

    @property
    def shape(self):
        if self.plain:
            return (self.k, self.n)
        lead = (N_CHIPS,) if self.layers is None else (N_CHIPS, self.layers)
        return lead + (self.kloc, self.nloc)

    def spec(self, bk, bn, f):
        if self.plain:
            return pl.BlockSpec((bk, bn), f)
        assert self.kloc % bk == 0 and self.nloc % bn == 0, (self.kloc, bk, self.nloc, bn)
        pk, pn = self.kloc // bk, self.nloc // bn
        kind, layer = self.kind, self.layer

        def imap(*g):
            kb, nb = f(*g)
            if kind == "row":
                q, kb, nb = kb // pk, kb % pk, nb
            else:
                q, kb, nb = nb // pn, kb, nb % pn
            return (q, kb, nb) if layer is None else (q, layer, kb, nb)

        block = (None, bk, bn) if layer is None else (None, None, bk, bn)
        return pl.BlockSpec(block, imap)


def _mm(name, mode, a, b, grid, a_spec, b_spec, acc_shape, extras, extra_specs, out_shapes, out_specs, epi, a_fn=None):
    nk = grid[2]
    ne = len(extras)

    def body(*refs):
        a_ref, b_ref = refs[0], refs[1]
        e_refs = refs[2:2 + ne]
        av = a_ref[...] if a_fn is None else a_fn(a_ref[...])
        part = lax.dot_general(av, b_ref[...], mode, preferred_element_type=F32)
        if nk == 1:
            epi(part, e_refs, refs[2 + ne:])
            return
        o_refs = refs[2 + ne:-1]
        acc = refs[-1]
        k = pl.program_id(2)

        @pl.when(k == 0)
        def _():
            acc[...] = part

        @pl.when(k > 0)
        def _():
            acc[...] += part

        @pl.when(k == nk - 1)
        def _():
            epi(acc[...], e_refs, o_refs)

    return pl.pallas_call(
        body, grid=grid, in_specs=[a_spec, b_spec, *extra_specs], out_specs=out_specs, out_shape=out_shapes,
        scratch_shapes=[pltpu.VMEM(acc_shape, F32)] if nk > 1 else [],
        compiler_params=_params(("parallel", "parallel", "arbitrary")), name=name)(a, b, *extras)


def _tile(n, t):
    t = min(n, t)
    while n % t:
        t -= 8
    assert t > 0, (n, t)
    return t


def mm_nn(name, a, w, tm, tn, tk, epi, out_shapes, out_specs, extras=(), extra_specs=(), a_spec=None, a_fn=None):
    m = a.shape[0]
    tm, tn, tk = _tile(m, tm), _tile(w.n, tn), _tile(w.k, tk)
    grid = (m // tm, w.n // tn, w.k // tk)
    a_spec = a_spec or pl.BlockSpec((tm, tk), lambda i, j, k: (i, k))
    b_spec = w.spec(tk, tn, lambda i, j, k: (k, j))
    return _mm(name, NN, a, w.arr, grid, a_spec, b_spec, (tm, tn), extras, extra_specs, out_shapes, out_specs, epi, a_fn)


def mm_nt(name, a, w, m, tm, tn, tk, epi, out_shapes, out_specs, extras=(), extra_specs=(), a_spec=None):
    tm, tn, tk = _tile(m, tm), _tile(w.k, tn), _tile(w.n, tk)
    grid = (m // tm, w.k // tn, w.n // tk)
    a_spec = a_spec or pl.BlockSpec((tm, tk), lambda i, j, k: (i, k))
    b_spec = w.spec(tn, tk, lambda i, j, k: (j, k))
    return _mm(name, NT, a, w.arr, grid, a_spec, b_spec, (tm, tn), extras, extra_specs, out_shapes, out_specs, epi)


def mm_tn(name, a, b, dw, s, tm=512, tn=512, tk=4096, a_spec=None, b_spec=None, a_fn=None):
    tm, tn, tk = _tile(dw.k, tm), _tile(dw.n, tn), _tile(s, tk)
    grid = (dw.k // tm, dw.n // tn, s // tk)
    a_spec = a_spec or pl.BlockSpec((tk, tm), lambda i, j, k: (k, i))
    b_spec = b_spec or pl.BlockSpec((tk, tn), lambda i, j, k: (k, j))

    def epi(acc, e, o):
        o[0][...] = acc.astype(BF16)

    out = _mm(name, TN, a, b, grid, a_spec, b_spec, (tm, tn), (), (), [jax.ShapeDtypeStruct(dw.shape, BF16)],
              [dw.spec(tm, tn, lambda i, j, k: (i, j))], epi, a_fn)[0]
    return out.reshape(N_CHIPS, dw.k // N_CHIPS, dw.n) if dw.plain else out


def _sds(shape, dtype):
    return jax.ShapeDtypeStruct(shape, dtype)


def _ij(tm, tn):
    return pl.BlockSpec((tm, tn), lambda i, j, k: (i, j))


def _i0(tm, c):
    return pl.BlockSpec((tm, c), lambda i, j, k: (i, 0))


def _0j(r, tn):
    return pl.BlockSpec((r, tn), lambda i, j, k: (0, j))


def _layer_norm_rows(r, g, b):
    mu = jnp.mean(r, axis=-1, keepdims=True)
    d = r - mu
    var = jnp.mean(d * d, axis=-1, keepdims=True)
    rstd = lax.rsqrt(var + LN_EPS)
    xh = d * rstd
    return xh * g + b, xh, rstd


def mm_residual_ln(name, a, w, x, g, b, bias=None, tm=512, tk=1024, a_fn=None):
    s, d = x.shape
    tm = _tile(s, tm)
    extras = [x, g, b] + ([bias] if bias is not None else [])
    especs = [_i0(tm, d), _0j(1, d), _0j(1, d)] + ([_0j(1, d)] if bias is not None else [])

    def epi(acc, e, o):
        r = ALPHA * e[0][...] + acc
        if bias is not None:
            r = r + e[3][...]
        y, xh, rstd = _layer_norm_rows(r, e[1][...], e[2][...])
        o[0][...] = y
        o[1][...] = y.astype(BF16)
        o[2][...] = xh
        o[3][...] = rstd

    return mm_nn(name, a, w, tm, d, tk, epi,
                 [_sds((s, d), F32), _sds((s, d), BF16), _sds((s, d), F32), _sds((s, 1), F32)],
                 [_i0(tm, d), _i0(tm, d), _i0(tm, d), _i0(tm, 1)], extras, especs, a_fn=a_fn)


def mm_plain_nn(name, a, w, out_dtype, tm=1024, tn=512, tk=1024, bias=None):
    m = a.shape[0]
    tm, tn = _tile(m, tm), _tile(w.n, tn)
    if w.kind == "col":
        tn = _tile(w.nloc, tn)

    def epi(acc, e, o):
        if bias is not None:
            acc = acc + e[0][...]
        o[0][...] = acc.astype(out_dtype)

    extras, especs = ([bias], [_0j(1, tn)]) if bias is not None else ((), ())
    return mm_nn(name, a, w, tm, tn, tk, epi, [_sds((m, w.n), out_dtype)], [_ij(tm, tn)], extras, especs)[0]


def mm_plain_nt(name, a, w, out_dtype, tm=1024, tn=512, tk=1024, add=None, add_scale=1.0, a_spec_fn=None):
    m = a.shape[0] if a_spec_fn is None else a_spec_fn[0]
    tm, tn = _tile(m, tm), _tile(w.k, tn)
    tk = _tile(w.n, tk)
    if w.kind == "col":
        tk = _tile(w.nloc, tk)
    if w.kind == "row" and not w.plain:
        tn = _tile(w.kloc, tn)

    def epi(acc, e, o):
        if add is not None:
            acc = acc + add_scale * e[0][...].astype(F32)
        o[0][...] = acc.astype(out_dtype)

    extras, especs = ([add], [_ij(tm, tn)]) if add is not None else ((), ())
    a_spec = None if a_spec_fn is None else a_spec_fn[1](tm, tk)
    return mm_nt(name, a, w, m, tm, tn, tk, epi, [_sds((m, w.k), out_dtype)], [_ij(tm, tn)], extras, especs,
                 a_spec=a_spec)[0]


def _rows(tm, c):
    return pl.BlockSpec((tm, c), lambda i: (i, 0))


def _fix(shape):
    nd = len(shape)
    return pl.BlockSpec(shape, lambda i: (0,) * nd)


def _accumulate(ref, val):
    @pl.when(pl.program_id(0) == 0)
    def _():
        ref[...] = jnp.zeros_like(ref)

    ref[...] += val


def ln_backward(name, dy, xhat, rstd, g, tm=512):
    s, d = dy.shape
    tm = _tile(s, tm)

    def body(dy_ref, xh_ref, rstd_ref, g_ref, dr_ref, drb_ref, dg_ref, db_ref, ds_ref):
        dyv, xh = dy_ref[...], xh_ref[...]
        dxh = dyv * g_ref[...]
        m1 = jnp.mean(dxh, axis=-1, keepdims=True)
        m2 = jnp.mean(dxh * xh, axis=-1, keepdims=True)
        dr = rstd_ref[...] * (dxh - m1 - xh * m2)
        dr_ref[...] = dr
        drb_ref[...] = dr.astype(BF16)
        _accumulate(dg_ref, jnp.sum(dyv * xh, axis=0, keepdims=True))
        _accumulate(db_ref, jnp.sum(dyv, axis=0, keepdims=True))
        _accumulate(ds_ref, jnp.sum(dr, axis=0, keepdims=True))

    return pl.pallas_call(
        body, grid=(s // tm,),
        in_specs=[_rows(tm, d), _rows(tm, d), _rows(tm, 1), _fix((1, d))],
        out_specs=[_rows(tm, d), _rows(tm, d), _fix((1, d)), _fix((1, d)), _fix((1, d))],
        out_shape=[_sds((s, d), F32), _sds((s, d), BF16), _sds((1, d), F32), _sds((1, d), F32), _sds((1, d), F32)],
        compiler_params=_params(("arbitrary",)), name=name)(dy, xhat, rstd, g)


def loss_head(y, target, tm=512):
    s, d = y.shape
    tm = _tile(s, tm)

    def body(y_ref, t_ref, dy_ref, loss_ref):
        e = y_ref[...] - t_ref[...]
        dy_ref[...] = e * (1.0 / d)
        part = 0.5 * jnp.sum(jnp.mean(e * e, axis=-1, keepdims=True), axis=0, keepdims=True)
        _accumulate(loss_ref, jnp.broadcast_to(part, (1, d)))

    return pl.pallas_call(
        body, grid=(s // tm,), in_specs=[_rows(tm, d), _rows(tm, d)],
        out_specs=[_rows(tm, d), _fix((1, d))], out_shape=[_sds((s, d), F32), _sds((1, d), F32)],
        compiler_params=_params(("arbitrary",)), name="loss_head")(y, target)


def _cols(s, tc, off=0):
    return pl.BlockSpec((s, tc), lambda i: (0, i + off))


def _shift_down(z, sft, rows):
    return jnp.where(rows >= sft, pltpu.roll(z, sft, 0), 0.0)


def _shift_up(z, sft, rows, s):
    return jnp.where(rows < s - sft, pltpu.roll(z, (s - sft) % s, 0), 0.0)


def short_conv_gate(u, conv_w, tc=256):
    s, d3 = u.shape
    d = d3 // 3
    nb = d // tc

    def body(b_ref, c_ref, h_ref, w_ref, o_ref):
        rows = lax.broadcasted_iota(jnp.int32, (s, tc), 0)
        z = c_ref[...].astype(F32) * h_ref[...].astype(F32)
        cz = jnp.zeros((s, tc), F32)
        for k in range(SC_WIDTH):
            sft = SC_WIDTH - 1 - k
            cz = cz + w_ref[pl.ds(k, 1), :] * (_shift_down(z, sft, rows) if sft else z)
        o_ref[...] = (b_ref[...].astype(F32) * cz).astype(BF16)

    return pl.pallas_call(
        body, grid=(nb,),
        in_specs=[_cols(s, tc), _cols(s, tc, nb), _cols(s, tc, 2 * nb), _cols(SC_WIDTH, tc)],
        out_specs=_cols(s, tc), out_shape=_sds((s, d), BF16),
        compiler_params=_params(("parallel",)), name="short_conv_gate")(u, u, u, conv_w)


def short_conv_gate_bwd(u, conv_w, dg, tc=256):
    s, d3 = u.shape
    d = d3 // 3
    nb = d // tc

    def body(b_ref, c_ref, h_ref, w_ref, dg_ref, du_ref, dw_ref):
        rows = lax.broadcasted_iota(jnp.int32, (s, tc), 0)
        c, h, dgv = c_ref[...].astype(F32), h_ref[...].astype(F32), dg_ref[...]
        z = c * h
        dcz = dgv * b_ref[...].astype(F32)
        cz = jnp.zeros((s, tc), F32)
        dz = jnp.zeros((s, tc), F32)
        for k in range(SC_WIDTH):
            sft = SC_WIDTH - 1 - k
            zs = _shift_down(z, sft, rows) if sft else z
            wk = w_ref[pl.ds(k, 1), :]
            cz = cz + wk * zs
            dz = dz + wk * (_shift_up(dcz, sft, rows, s) if sft else dcz)
            dw_ref[pl.ds(k, 1), :] = jnp.sum(dcz * zs, axis=0, keepdims=True)
        du_ref[0] = (dgv * cz).astype(BF16)
        du_ref[1] = (dz * h).astype(BF16)
        du_ref[2] = (dz * c).astype(BF16)

    return pl.pallas_call(
        body, grid=(nb,),
        in_specs=[_cols(s, tc), _cols(s, tc, nb), _cols(s, tc, 2 * nb), _cols(SC_WIDTH, tc), _cols(s, tc)],
        out_specs=[pl.BlockSpec((3, s, tc), lambda i: (0, 0, i)), _cols(SC_WIDTH, tc)],
        out_shape=[_sds((3, s, d), BF16), _sds((SC_WIDTH, d), F32)],
        compiler_params=_params(("parallel",)), name="short_conv_gate_bwd")(u, u, u, conv_w, dg)


def _store_shifted_down(ref, z, rows):
    s, tc = z.shape
    for b in range(8):
        ref[b, pl.ds(0, CONV_PAD), :] = jnp.zeros((CONV_PAD, tc), F32)
        ref[b, pl.ds(CONV_PAD, s), :] = z if b == 0 else _shift_down(z, b, rows)


def _store_shifted_up(ref, z, rows):
    s, tc = z.shape
    for b in range(8):
        ref[b, pl.ds(0, s), :] = z if b == 0 else _shift_up(z, b, rows, s)
        ref[b, pl.ds(s, CONV_PAD), :] = jnp.zeros((CONV_PAD, tc), F32)


def conformer_glu_conv(u, dw_w, dw_b, tc=128):
    s, d2 = u.shape
    d = d2 // 2
    nb = d // tc

    ch = min(CONV_CHUNK, s)

    def body(a_ref, g_ref, w_ref, b_ref, o_ref, down):
        rows = lax.broadcasted_iota(jnp.int32, (s, tc), 0)
        _store_shifted_down(down, a_ref[...].astype(F32) * jax.nn.sigmoid(g_ref[...].astype(F32)), rows)

        def chunk(ci, carry):
            r0 = pl.multiple_of(ci * ch, ch)
            acc = jnp.broadcast_to(b_ref[...], (ch, tc))
            for k in range(CONF_WIDTH):
                sft = CONF_WIDTH - 1 - k
                acc = acc + w_ref[pl.ds(k, 1), :] * down[sft % 8, pl.ds(CONV_PAD + r0 - (sft // 8) * 8, ch), :]
            o_ref[pl.ds(r0, ch), :] = acc
            return carry

        lax.fori_loop(0, s // ch, chunk, 0)

    return pl.pallas_call(
        body, grid=(nb,),
        in_specs=[_cols(s, tc), _cols(s, tc, nb), _cols(CONF_WIDTH, tc), _cols(1, tc)],
        out_specs=_cols(s, tc), out_shape=_sds((s, d), F32),
        scratch_shapes=[pltpu.VMEM((8, CONV_PAD + s, tc), F32)],
        compiler_params=_params(("parallel",)), name="conformer_glu_conv")(u, u, dw_w, dw_b)


def conformer_glu_conv_bwd(u, dw_w, dhc, tc=128):
    s, d2 = u.shape
    d = d2 // 2
    nb = d // tc
    ch = min(CONV_CHUNK, s)

    def body(a_ref, g_ref, w_ref, dhc_ref, du_ref, dbias_ref, dw_ref, db_ref, down, up, dw_acc, dh_buf):
        rows = lax.broadcasted_iota(jnp.int32, (s, tc), 0)
        a = a_ref[...].astype(F32)
        sg = jax.nn.sigmoid(g_ref[...].astype(F32))
        dhcv = dhc_ref[...]
        _store_shifted_down(down, a * sg, rows)
        _store_shifted_up(up, dhcv, rows)
        dw_acc[...] = jnp.zeros_like(dw_acc)

        def chunk(ci, carry):
            r0 = pl.multiple_of(ci * ch, ch)
            dc = dhc_ref[pl.ds(r0, ch), :]
            dh = jnp.zeros((ch, tc), F32)
            for k in range(CONF_WIDTH):
                sft = CONF_WIDTH - 1 - k
                a8, b = (sft // 8) * 8, sft % 8
                dh = dh + w_ref[pl.ds(k, 1), :] * up[b, pl.ds(r0 + a8, ch), :]
                prod = dc * down[b, pl.ds(CONV_PAD + r0 - a8, ch), :]
                dw_acc[k] += jnp.sum(prod.reshape(ch // 8, 8, tc), axis=0)
            dh_buf[pl.ds(r0, ch), :] = dh
            return carry

        lax.fori_loop(0, s // ch, chunk, 0)
        dh = dh_buf[...]
        da = dh * sg
        dgate = dh * a * sg * (1.0 - sg)
        du_ref[0] = da.astype(BF16)
        du_ref[1] = dgate.astype(BF16)
        dbias_ref[pl.ds(0, 1), :] = jnp.sum(da, axis=0, keepdims=True)
        dbias_ref[pl.ds(1, 1), :] = jnp.sum(dgate, axis=0, keepdims=True)
        db_ref[...] = jnp.sum(dhcv, axis=0, keepdims=True)
        for k in range(CONF_WIDTH):
            dw_ref[pl.ds(k, 1), :] = jnp.sum(dw_acc[k], axis=0, keepdims=True)

    return pl.pallas_call(
        body, grid=(nb,),
        in_specs=[_cols(s, tc), _cols(s, tc, nb), _cols(CONF_WIDTH, tc), _cols(s, tc)],
        out_specs=[pl.BlockSpec((2, s, tc), lambda i: (0, 0, i)), _cols(2, tc), _cols(CONF_WIDTH, tc), _cols(1, tc)],
        out_shape=[_sds((2, s, d), BF16), _sds((2, d), F32), _sds((CONF_WIDTH, d), F32), _sds((1, d), F32)],
        scratch_shapes=[pltpu.VMEM((8, CONV_PAD + s, tc), F32), pltpu.VMEM((8, CONV_PAD + s, tc), F32),
                        pltpu.VMEM((CONF_WIDTH + 1, 8, tc), F32), pltpu.VMEM((s, tc), F32)],
        compiler_params=_params(("parallel",)), name="conformer_glu_conv_bwd")(u, u, dw_w, dhc)


def conformer_norm_swish(hc, g, b, tm=512):
    s, d = hc.shape
    tm = _tile(s, tm)

    def body(h_ref, g_ref, b_ref, o_ref):
        n, _, _ = _layer_norm_rows(h_ref[...], g_ref[...], b_ref[...])
        o_ref[...] = (n * jax.nn.sigmoid(n)).astype(BF16)

    return pl.pallas_call(
        body, grid=(s // tm,), in_specs=[_rows(tm, d), _fix((1, d)), _fix((1, d))], out_specs=_rows(tm, d),
        out_shape=_sds((s, d), BF16), compiler_params=_params(("parallel",)), name="conformer_norm_swish")(hc, g, b)


def conformer_norm_swish_bwd(hc, g, b, ds, tm=512):
    s, d = hc.shape
    tm = _tile(s, tm)

    def body(h_ref, g_ref, b_ref, ds_ref, dh_ref, dg_ref, db_ref):
        n, nh, rstd = _layer_norm_rows(h_ref[...], g_ref[...], b_ref[...])
        sg = jax.nn.sigmoid(n)
        dn = ds_ref[...] * (sg * (1.0 + n * (1.0 - sg)))
        dnh = dn * g_ref[...]
        m1 = jnp.mean(dnh, axis=-1, keepdims=True)
        m2 = jnp.mean(dnh * nh, axis=-1, keepdims=True)
        dh_ref[...] = rstd * (dnh - m1 - nh * m2)
        _accumulate(dg_ref, jnp.sum(dn * nh, axis=0, keepdims=True))
        _accumulate(db_ref, jnp.sum(dn, axis=0, keepdims=True))

    return pl.pallas_call(
        body, grid=(s // tm,), in_specs=[_rows(tm, d), _fix((1, d)), _fix((1, d)), _rows(tm, d)],
        out_specs=[_rows(tm, d), _fix((1, d)), _fix((1, d))],
        out_shape=[_sds((s, d), F32), _sds((1, d), F32), _sds((1, d), F32)],
        compiler_params=_params(("arbitrary",)), name="conformer_norm_swish_bwd")(hc, g, b, ds)


def _swap_halves(x):
    lane = lax.broadcasted_iota(jnp.int32, x.shape, 1)
    return jnp.where(lane < QK_ROPE // 2, pltpu.roll(x, 128 - QK_ROPE // 2, 1), pltpu.roll(x, QK_ROPE // 2, 1))


def _rope(x, cf, sf):
    return x * cf + _swap_halves(x) * sf


def _unrope(dx, cf, sf):
    return dx * cf - _swap_halves(dx) * sf


def _rms_rows(x, g):
    r = lax.rsqrt(jnp.mean(x * x, axis=-1, keepdims=True) + RMS_EPS)
    return x * r, r


def mla_latents(t, g_q, g_kv, cf, sf, tm=512):
    s = t.shape[0]
    tm = _tile(s, tm)

    def body(t_ref, gq_ref, gkv_ref, cf_ref, sf_ref, cq_ref, ckv_ref, kpe_ref):
        xq, _ = _rms_rows(t_ref[:, 0:Q_LORA], gq_ref[...])
        cq_ref[...] = (xq * gq_ref[...]).astype(BF16)
        xkv, _ = _rms_rows(t_ref[:, Q_LORA:Q_LORA + KV_LORA], gkv_ref[...])
        ckv_ref[...] = (xkv * gkv_ref[...]).astype(BF16)
        kpe_ref[...] = _rope(t_ref[:, Q_LORA + KV_LORA:], cf_ref[...], sf_ref[...]).astype(BF16)

    w = Q_LORA + KV_LORA + 128
    return pl.pallas_call(
        body, grid=(s // tm,),
        in_specs=[_rows(tm, w), _fix((1, Q_LORA)), _fix((1, KV_LORA)), _rows(tm, 128), _rows(tm, 128)],
        out_specs=[_rows(tm, Q_LORA), _rows(tm, KV_LORA), _rows(tm, 128)],
        out_shape=[_sds((s, Q_LORA), BF16), _sds((s, KV_LORA), BF16), _sds((s, 128), BF16)],
        compiler_params=_params(("parallel",)), name="mla_latents")(t, g_q, g_kv, cf, sf)


def mla_latents_bwd(t, g_q, g_kv, cf, sf, dcq, dckv, dkpe, tm=512):
    s = t.shape[0]
    tm = _tile(s, tm)
    w = Q_LORA + KV_LORA + 128

    def rms_bwd(x, g, dy):
        xh, r = _rms_rows(x, g)
        dxh = dy * g
        return r * (dxh - xh * jnp.mean(dxh * xh, axis=-1, keepdims=True)), jnp.sum(dy * xh, axis=0, keepdims=True)

    def body(t_ref, gq_ref, gkv_ref, cf_ref, sf_ref, dcq_ref, dckv_ref, dkpe_ref, dt_ref, dgq_ref, dgkv_ref):
        dxq, dgq = rms_bwd(t_ref[:, 0:Q_LORA], gq_ref[...], dcq_ref[...])
        dxkv, dgkv = rms_bwd(t_ref[:, Q_LORA:Q_LORA + KV_LORA], gkv_ref[...], dckv_ref[...])
        dt_ref[:, 0:Q_LORA] = dxq.astype(BF16)
        dt_ref[:, Q_LORA:Q_LORA + KV_LORA] = dxkv.astype(BF16)
        dt_ref[:, Q_LORA + KV_LORA:] = _unrope(dkpe_ref[...], cf_ref[...], sf_ref[...]).astype(BF16)
        _accumulate(dgq_ref, dgq)
        _accumulate(dgkv_ref, dgkv)

    return pl.pallas_call(
        body, grid=(s // tm,),
        in_specs=[_rows(tm, w), _fix((1, Q_LORA)), _fix((1, KV_LORA)), _rows(tm, 128), _rows(tm, 128),
                  _rows(tm, Q_LORA), _rows(tm, KV_LORA), _rows(tm, 128)],
        out_specs=[_rows(tm, w), _fix((1, Q_LORA)), _fix((1, KV_LORA))],
        out_shape=[_sds((s, w), BF16), _sds((1, Q_LORA), F32), _sds((1, KV_LORA), F32)],
        compiler_params=_params(("arbitrary",)), name="mla_latents_bwd")(t, g_q, g_kv, cf, sf, dcq, dckv, dkpe)


def mla_queries(cq, w_uq, cf, sf, tm=2048):
    s = cq.shape[0]
    tm = _tile(s, tm)

    def epi(acc, e, o):
        o[0][:, 0:QK_NOPE] = acc[:, 0:QK_NOPE].astype(BF16)
        o[0][:, QK_NOPE:] = _rope(acc[:, QK_NOPE:], e[0][...], e[1][...]).astype(BF16)

    return mm_nn("mla_queries", cq, w_uq, tm, HEAD_PAD, Q_LORA, epi, [_sds((s, N_HEADS * HEAD_PAD), BF16)],
                 [_ij(tm, HEAD_PAD)], [cf, sf], [_i0(tm, 128), _i0(tm, 128)])[0]


def mla_keys(ckv, w_uk, kpe, tm=2048):
    s = ckv.shape[0]
    tm = _tile(s, tm)

    def epi(acc, e, o):
        o[0][:, 0:QK_NOPE] = acc.astype(BF16)
        o[0][:, QK_NOPE:] = e[0][...]

    return mm_nn("mla_keys", ckv, w_uk, tm, QK_NOPE, KV_LORA, epi, [_sds((s, N_HEADS * HEAD_PAD), BF16)],
                 [_ij(tm, HEAD_PAD)], [kpe], [_i0(tm, 128)])[0]


def _masked_scores(q, k, qi, tq, kv):
    sc = lax.dot_general(q, k, NT, preferred_element_type=F32) * ATTN_SCALE
    row = lax.broadcasted_iota(jnp.int32, (tq, kv), 0) + qi * tq
    col = lax.broadcasted_iota(jnp.int32, (tq, kv), 1)
    ok = lax.shift_right_logical(col, CHUNK_SHIFT) <= lax.shift_right_logical(row, CHUNK_SHIFT)
    return jnp.where(ok, sc, -1e30)


def attention(q, k, v, tq=512):
    s = q.shape[0]
    tq = _tile(s, tq)
    nq = s // tq

    def body(q_ref, k_ref, v_ref, o_ref):
        for qi in range(nq):
            kv = (qi + 1) * tq
            sc = _masked_scores(q_ref[pl.ds(qi * tq, tq), :], k_ref[pl.ds(0, kv), :], qi, tq, kv)
            p = jnp.exp(sc - jnp.max(sc, axis=-1, keepdims=True))
            o = lax.dot_general(p.astype(BF16), v_ref[pl.ds(0, kv), :], NN, preferred_element_type=F32)
            o_ref[pl.ds(qi * tq, tq), :] = (o / jnp.sum(p, axis=-1, keepdims=True)).astype(BF16)

    hq = pl.BlockSpec((s, HEAD_PAD), lambda h: (0, h))
    hv = pl.BlockSpec((s, V_HEAD), lambda h: (0, h))
    return pl.pallas_call(
        body, grid=(N_HEADS,), in_specs=[hq, hq, hv], out_specs=hv, out_shape=_sds((s, N_HEADS * V_HEAD), BF16),
        compiler_params=_params(("parallel",)), name="attention")(q, k, v)


def attention_bwd(q, k, v, do, tq=512):
    s = q.shape[0]
    tq = _tile(s, tq)
    nq = s // tq

    def body(q_ref, k_ref, v_ref, do_ref, dq_ref, dk_ref, dv_ref, dk_acc, dv_acc):
        dk_acc[...] = jnp.zeros_like(dk_acc)
        dv_acc[...] = jnp.zeros_like(dv_acc)
        for qi in range(nq):
            kv = (qi + 1) * tq
            qt = q_ref[pl.ds(qi * tq, tq), :]
            kt = k_ref[pl.ds(0, kv), :]
            dot = do_ref[pl.ds(qi * tq, tq), :]
            sc = _masked_scores(qt, kt, qi, tq, kv)
            p = jnp.exp(sc - jnp.max(sc, axis=-1, keepdims=True))
            p = p / jnp.sum(p, axis=-1, keepdims=True)
            dp = lax.dot_general(dot, v_ref[pl.ds(0, kv), :], NT, preferred_element_type=F32)
            delta = jnp.sum(p * dp, axis=-1, keepdims=True)
            ds = (p * (dp - delta) * ATTN_SCALE).astype(BF16)
            dq_ref[pl.ds(qi * tq, tq), :] = lax.dot_general(ds, kt, NN, preferred_element_type=F32).astype(BF16)
            dk_acc[pl.ds(0, kv), :] += lax.dot_general(ds, qt, TN, preferred_element_type=F32)
            dv_acc[pl.ds(0, kv), :] += lax.dot_general(p.astype(BF16), dot, TN, preferred_element_type=F32)
        dk_ref[...] = dk_acc[...].astype(BF16)
        dv_ref[...] = dv_acc[...].astype(BF16)

    hq = pl.BlockSpec((s, HEAD_PAD), lambda h: (0, h))
    hv = pl.BlockSpec((s, V_HEAD), lambda h: (0, h))
    return pl.pallas_call(
        body, grid=(N_HEADS,), in_specs=[hq, hq, hv, hv], out_specs=[hq, hq, hv],
        out_shape=[_sds((s, N_HEADS * HEAD_PAD), BF16), _sds((s, N_HEADS * HEAD_PAD), BF16),
                   _sds((s, N_HEADS * V_HEAD), BF16)],
        scratch_shapes=[pltpu.VMEM((s, HEAD_PAD), F32), pltpu.VMEM((s, V_HEAD), F32)],
        compiler_params=_params(("parallel",)), name="attention_bwd")(q, k, v, do)


def mla_unrope_grads(dq, dk, cf, sf, tm=512):
    s = dq.shape[0]
    tm = _tile(s, tm)

    def body(dq_ref, dk_ref, cf_ref, sf_ref, dql_ref, dkn_ref, dkpe_ref):
        cfv, sfv = cf_ref[...], sf_ref[...]
        dkpe = jnp.zeros((tm, 128), F32)
        for h in range(N_HEADS):
            lo = h * HEAD_PAD
            dql_ref[:, lo:lo + QK_NOPE] = dq_ref[:, lo:lo + QK_NOPE]
            dql_ref[:, lo + QK_NOPE:lo + HEAD_PAD] = _unrope(
                dq_ref[:, lo + QK_NOPE:lo + HEAD_PAD].astype(F32), cfv, sfv).astype(BF16)
            dkn_ref[:, h * QK_NOPE:(h + 1) * QK_NOPE] = dk_ref[:, lo:lo + QK_NOPE]
            dkpe = dkpe + dk_ref[:, lo + QK_NOPE:lo + HEAD_PAD].astype(F32)
        dkpe_ref[...] = dkpe

    wq = N_HEADS * HEAD_PAD
    return pl.pallas_call(
        body, grid=(s // tm,), in_specs=[_rows(tm, wq), _rows(tm, wq), _rows(tm, 128), _rows(tm, 128)],
        out_specs=[_rows(tm, wq), _rows(tm, N_HEADS * QK_NOPE), _rows(tm, 128)],
        out_shape=[_sds((s, wq), BF16), _sds((s, N_HEADS * QK_NOPE), BF16), _sds((s, 128), F32)],
        compiler_params=_params(("parallel",)), name="mla_unrope_grads")(dq, dk, cf, sf)


ANY = pl.BlockSpec(memory_space=pl.ANY)
GATHER_ID = 1
CHIP_EXCHANGE_ID = 2
PAIR_ID = 3
ALL_ID = 4


def _nbytes(a):
    return a.size * a.dtype.itemsize


def _copy_cost(operand_bytes, sent_fraction):
    sent = int(operand_bytes * sent_fraction)
    return pl.CostEstimate(flops=0, transcendentals=0, bytes_accessed=2 * sent, remote_bytes_transferred=sent)


def _handshake(peers):
    barrier = pltpu.get_barrier_semaphore()
    for peer in peers:
        pl.semaphore_signal(barrier, inc=1, device_id=peer, device_id_type=MESH)
    pl.semaphore_wait(barrier, len(peers))


def _place():
    x, y, c = lax.axis_index("x"), lax.axis_index("y"), lax.axis_index("c")
    chips = [(1 - x, y), (x, 1 - y), (1 - x, 1 - y)]
    return x, y, c, chips


def _half(ref, hc, axis=0):
    n = ref.shape[axis] // 2
    idx = (slice(None),) * axis + (pl.ds(hc * n, n),)
    return ref.at[idx]


def gather_shards(name, tensors, by_columns=()):
    nt = len(tensors)

    def body(*refs):
        a, g = refs[:nt], refs[nt:2 * nt]
        send, recv = refs[2 * nt:]
        x, y, c, _ = _place()
        q = 2 * x + y
        sib, xn, yn = (x, y, 1 - c), (1 - x, y, c), (x, 1 - y, c)
        q_xn, q_yn, q_diag = 2 * (1 - x) + y, 2 * x + 1 - y, 2 * (1 - x) + 1 - y
        _handshake([sib, xn, yn])

        def whole(t, p):
            if t in by_columns:
                n = a[t].shape[1]
                return g[t].at[:, pl.ds(p * n, n)]
            return g[t].at[p]

        def part(t, p, hc, quarter=None):
            rows = a[t].shape[0]
            if quarter is None:
                return whole(t, p).at[pl.ds(hc * (rows // 2), rows // 2)]
            return whole(t, p).at[pl.ds(hc * (rows // 2) + quarter * (rows // 4), rows // 4)]

        def rc(t, k, src, dst, to):
            return pltpu.make_async_remote_copy(src_ref=src, dst_ref=dst, send_sem=send.at[t, k], recv_sem=recv.at[t, k],
                                                device_id=to, device_id_type=MESH)

        sent = []

        def go(cp):
            cp.start()
            sent.append(cp)

        def landed(t, k, piece, frm):
            rc(t, k, piece, piece, frm).wait_recv()
            return piece

        for t in range(nt):
            go(rc(t, 8, a[t], whole(t, q), sib))
            mine = _half(a[t], c)
            go(rc(t, 0, mine, part(t, q, c), xn))
            go(rc(t, 1, mine, part(t, q, c), yn))
        for t in range(nt):
            from_y = landed(t, 1, part(t, q_yn, c), yn)
            go(rc(t, 2, part(t, q_yn, c, 0), part(t, q_yn, c, 0), xn))
            go(rc(t, 5, from_y, from_y, sib))
            from_x = landed(t, 0, part(t, q_xn, c), xn)
            go(rc(t, 3, part(t, q_xn, c, 1), part(t, q_xn, c, 1), yn))
            go(rc(t, 4, from_x, from_x, sib))
        for t in range(nt):
            for k, frm in ((2, xn), (3, yn)):
                piece = landed(t, k, part(t, q_diag, c, k - 2), frm)
                go(rc(t, 4 + k, piece, piece, sib))
        for t in range(nt):
            landed(t, 4, part(t, q_xn, 1 - c), sib)
            landed(t, 5, part(t, q_yn, 1 - c), sib)
            landed(t, 6, part(t, q_diag, 1 - c, 0), sib)
            landed(t, 7, part(t, q_diag, 1 - c, 1), sib)
            landed(t, 8, whole(t, q), sib)
        for cp in sent:
            cp.wait_send()

    return pl.kernel(
        body, name=name,
        out_type=[_sds((a.shape[0], N_CHIPS * a.shape[1]) if t in by_columns else (N_CHIPS,) + a.shape, a.dtype)
                  for t, a in enumerate(tensors)],
        mesh=plsc.ScalarSubcoreMesh(axis_name="sequencer", num_cores=1),
        scratch_types=[pltpu.SemaphoreType.DMA((nt, 9)), pltpu.SemaphoreType.DMA((nt, 9))],
        cost_estimate=_copy_cost(sum(_nbytes(a) for a in tensors), 4),
        compiler_params=pltpu.CompilerParams(collective_id=GATHER_ID))(*tensors)


def pair_exchange(name, grads, on_sequencer):
    nt = len(grads)

    def body(*refs):
        g, theirs = refs[:nt], refs[nt:2 * nt]
        send, recv = refs[2 * nt:]
        x, y, c, _ = _place()
        if on_sequencer:
            _handshake([(x, y, 1 - c)])
        cps = []
        for t in range(nt):
            cp = pltpu.make_async_remote_copy(src_ref=_half(g[t], 1 - c, 1), dst_ref=theirs[t], send_sem=send.at[t],
                                              recv_sem=recv.at[t], device_id=(x, y, 1 - c), device_id_type=MESH)
            cp.start()
            cps.append(cp)
        for cp in cps:
            cp.wait()

    if not on_sequencer:
        return pl.pallas_call(
            body, in_specs=[ANY] * nt, out_specs=[ANY] * nt,
            out_shape=[_sds((N_CHIPS, a.shape[1] // 2, a.shape[2]), a.dtype) for a in grads],
            scratch_shapes=[pltpu.SemaphoreType.DMA((nt,)), pltpu.SemaphoreType.DMA((nt,))],
            name=name)(*grads)
    return pl.kernel(
        body, name=name, out_type=[_sds((N_CHIPS, a.shape[1] // 2, a.shape[2]), a.dtype) for a in grads],
        mesh=plsc.ScalarSubcoreMesh(axis_name="sequencer", num_cores=1),
        scratch_types=[pltpu.SemaphoreType.DMA((nt,)), pltpu.SemaphoreType.DMA((nt,))],
        cost_estimate=_copy_cost(sum(_nbytes(a) for a in grads), 0.5),
        compiler_params=pltpu.CompilerParams(collective_id=PAIR_ID))(*grads)


def chip_exchange(name, parts):
    nt = len(parts)

    def body(*refs):
        a, r = refs[:nt], refs[nt:2 * nt]
        send, recv = refs[2 * nt:]
        x, y, c, chips = _place()
        _handshake([(*chip, c) for chip in chips])
        cps = []
        for t in range(nt):
            for j, chip in enumerate(chips):
                cp = pltpu.make_async_remote_copy(
                    src_ref=a[t].at[2 * chip[0] + chip[1]], dst_ref=r[t].at[j], send_sem=send.at[t, j],
                    recv_sem=recv.at[t, j], device_id=(*chip, c), device_id_type=MESH)
                cp.start()
                cps.append(cp)
        for cp in cps:
            cp.wait()

    return pl.kernel(
        body, name=name, out_type=[_sds((N_CHIPS - 1,) + a.shape[1:], a.dtype) for a in parts],
        mesh=plsc.ScalarSubcoreMesh(axis_name="sequencer", num_cores=1),
        scratch_types=[pltpu.SemaphoreType.DMA((nt, 3)), pltpu.SemaphoreType.DMA((nt, 3))],
        cost_estimate=_copy_cost(sum(_nbytes(a) for a in parts), 0.75),
        compiler_params=pltpu.CompilerParams(collective_id=CHIP_EXCHANGE_ID))(*parts)


def pair_share(name, halves):
    nt = len(halves)

    def body(*refs):
        h, other = refs[:nt], refs[nt:2 * nt]
        send, recv = refs[2 * nt:]
        x, y, c, _ = _place()
        _handshake([(x, y, 1 - c)])
        cps = []
        for t in range(nt):
            cp = pltpu.make_async_remote_copy(src_ref=h[t], dst_ref=other[t], send_sem=send.at[t], recv_sem=recv.at[t],
                                              device_id=(x, y, 1 - c), device_id_type=MESH)
            cp.start()
            cps.append(cp)
        for cp in cps:
            cp.wait()

    return pl.kernel(
        body, name=name, out_type=[_sds(a.shape, a.dtype) for a in halves],
        mesh=plsc.ScalarSubcoreMesh(axis_name="sequencer", num_cores=1),
        scratch_types=[pltpu.SemaphoreType.DMA((nt,)), pltpu.SemaphoreType.DMA((nt,))],
        cost_estimate=_copy_cost(sum(_nbytes(a) for a in halves), 1),
        compiler_params=pltpu.CompilerParams(collective_id=PAIR_ID))(*halves)


def pack_rows(name, parts, rows):
    cdim = parts[0].shape[1]
    n = len(parts)
    vm = pl.BlockSpec(memory_space=pltpu.VMEM)

    def pack(*refs):
        p, o_ref = refs[:n], refs[n]
        at = 0
        for ref in p:
            o_ref[pl.ds(at, ref.shape[0]), :] = ref[...]
            at += ref.shape[0]
        o_ref[pl.ds(at, rows - at), :] = jnp.zeros((rows - at, cdim), F32)

    return pl.pallas_call(pack, in_specs=[vm] * n, out_specs=vm, out_shape=_sds((rows, cdim), F32), name=name)(*parts)


def all_reduce_small(parts, rows):
    cdim = parts[0].shape[1]
    vm = pl.BlockSpec(memory_space=pltpu.VMEM)
    mine = pack_rows("small_pack", parts, rows)

    def exchange(mine_ref, buf, send, recv, lsem):
        x, y, c, _ = _place()
        me = 4 * x + 2 * y + c
        peers = [(x ^ (k >> 2), y ^ ((k >> 1) & 1), c ^ (k & 1)) for k in range(1, 8)]
        _handshake(peers)
        own = pltpu.make_async_copy(mine_ref, buf.at[me], lsem)
        own.start()
        cps = []
        for k, to in enumerate(peers):
            cp = pltpu.make_async_remote_copy(src_ref=mine_ref, dst_ref=buf.at[me], send_sem=send.at[k], recv_sem=recv.at[k],
                                              device_id=to, device_id_type=MESH)
            cp.start()
            cps.append(cp)
        for k, (px, py, pc) in enumerate(peers):
            pltpu.make_async_remote_copy(src_ref=mine_ref, dst_ref=buf.at[4 * px + 2 * py + pc], send_sem=send.at[k],
                                         recv_sem=recv.at[k], device_id=(x, y, c), device_id_type=MESH).wait_recv()
        for cp in cps:
            cp.wait_send()
        own.wait()

    landed = pl.kernel(
        exchange, name="small_exchange", out_type=_sds((8, rows, cdim), F32),
        mesh=plsc.ScalarSubcoreMesh(axis_name="sequencer", num_cores=1),
        scratch_types=[pltpu.SemaphoreType.DMA((7,)), pltpu.SemaphoreType.DMA((7,)), pltpu.SemaphoreType.DMA],
        cost_estimate=_copy_cost(rows * cdim * 4, 7),
        compiler_params=pltpu.CompilerParams(collective_id=ALL_ID))(mine)

    def total(buf, o_ref):
        acc = buf[0]
        for d in range(1, 8):
            acc = acc + buf[d]
        o_ref[...] = acc

    return pl.pallas_call(total, in_specs=[vm], out_specs=vm, out_shape=_sds((rows, cdim), F32), name="small_sum")(landed)


def pair_sum(g, theirs, core, tm=256):
    _, r, c = g.shape
    tm = _tile(r // 2, tm)
    nh = r // 2 // tm

    def body(core_ref, a_ref, b_ref, o_ref):
        o_ref[...] = (a_ref[...].astype(F32) + b_ref[...].astype(F32)).astype(BF16)

    blk = (N_CHIPS, tm, c)
    return pl.pallas_call(
        body, grid_spec=pltpu.PrefetchScalarGridSpec(
            num_scalar_prefetch=1, grid=(nh,),
            in_specs=[pl.BlockSpec(blk, lambda i, cr: (0, cr[0] * nh + i, 0)), pl.BlockSpec(blk, lambda i, cr: (0, i, 0))],
            out_specs=pl.BlockSpec(blk, lambda i, cr: (0, i, 0))),
        out_shape=_sds(theirs.shape, BF16), compiler_params=_params(("parallel",)), name="pair_sum")(core, g, theirs)


def chip_sum(own, landed, chip, stack, layer, layers, tm=256):
    _, r, c = own.shape
    tm = _tile(r, tm)

    def body(chip_ref, own_ref, l_ref, *rest):
        acc = own_ref[...].astype(F32)
        for j in range(N_CHIPS - 1):
            acc = acc + l_ref[j].astype(F32)
        rest[-1][...] = acc

    in_specs = [pl.BlockSpec((None, tm, c), lambda i, qr: (qr[0], i, 0)),
                pl.BlockSpec((N_CHIPS - 1, tm, c), lambda i, qr: (0, i, 0))]
    args = [chip, own, landed]
    if stack is not None:
        in_specs.append(ANY)
        args.append(stack)
    return pl.pallas_call(
        body, grid_spec=pltpu.PrefetchScalarGridSpec(
            num_scalar_prefetch=1, grid=(r // tm,), in_specs=in_specs,
            out_specs=pl.BlockSpec((None, tm, c), lambda i, qr: (layer, i, 0))),
        out_shape=_sds((layers, r, c), F32), input_output_aliases={3: 0} if stack is not None else {},
        compiler_params=_params(("parallel",)), name="chip_sum")(*args)


def _adamw_math(w, g, m, v):
    bc1 = 1.0 - ADAM_B1 ** ADAM_STEP
    bc2 = 1.0 - ADAM_B2 ** ADAM_STEP
    nm = ADAM_B1 * m + (1.0 - ADAM_B1) * g
    nv = ADAM_B2 * v + (1.0 - ADAM_B2) * (g * g)
    return -ADAM_LR * ((nm / bc1) / (jnp.sqrt(nv / bc2) + ADAM_EPS) + ADAM_WD * w), nm, nv


def vector_update(red, chip, ws, ms, vs, where):
    n = len(ws)
    dd = red.shape[1]

    def body(chip_ref, red_ref, *refs):
        w_r, m_r, v_r = refs[0:n], refs[n:2 * n], refs[2 * n:3 * n]
        g_o, d_o, m_o, v_o = (refs[(3 + k) * n:(4 + k) * n] for k in range(4))
        q = chip_ref[0]

        def chip_block(val, width):
            out = val[:, 0:width]
            for p in range(1, val.shape[1] // width):
                out = jnp.where(q == p, val[:, p * width:(p + 1) * width], out)
            return out

        for k in range(n):
            for idx, r0, nr, cols in where[k]:
                width = w_r[k].shape[-1]
                if cols == "chip" and width * N_CHIPS != dd:
                    g = chip_block(jnp.concatenate([red_ref[pl.ds(r0 + j, 1), :] for j in range(nr)], axis=1), width)
                else:
                    g = red_ref[pl.ds(r0, nr), :]
                    g = chip_block(g, width) if cols == "chip" else g if cols == "all" else g[:, 0:cols]
                delta, nm, nv = _adamw_math(w_r[k][idx], g, m_r[k][idx], v_r[k][idx])
                g_o[k][idx] = g
                d_o[k][idx] = delta
                m_o[k][idx] = nm
                v_o[k][idx] = nv

    vm = pl.BlockSpec(memory_space=pltpu.VMEM)
    outs = pl.pallas_call(
        body, in_specs=[pl.BlockSpec(memory_space=pltpu.SMEM), vm] + [vm] * (3 * n), out_specs=[vm] * (4 * n),
        out_shape=[_sds(w.shape, F32) for w in ws] * 4, name="vector_update")(chip, red, *ws, *ms, *vs)
    return [outs[k * n:(k + 1) * n] for k in range(4)]


def adamw_joined(w, m, v, g_mine, g_theirs, core, tm=512):
    nl, r, c = w.shape
    tm = _tile(r // 2, tm)
    nh = r // 2 // tm

    def body(core_ref, w_ref, m_ref, v_ref, gm_ref, gt_ref, g_ref, d_ref, nm_ref, nv_ref):
        mine = (pl.program_id(1) // nh) == core_ref[0]
        gv = jnp.where(mine, gm_ref[...], gt_ref[...])
        g_ref[...] = gv
        d_ref[...], nm_ref[...], nv_ref[...] = _adamw_math(w_ref[...], gv, m_ref[...], v_ref[...])

    full = pl.BlockSpec((None, tm, c), lambda l, i, cr: (l, i, 0))
    half = pl.BlockSpec((None, tm, c), lambda l, i, cr: (l, i % nh, 0))
    return pl.pallas_call(
        body, grid_spec=pltpu.PrefetchScalarGridSpec(
            num_scalar_prefetch=1, grid=(nl, r // tm), in_specs=[full, full, full, half, half], out_specs=[full] * 4),
        out_shape=[_sds((nl, r, c), F32)] * 4, compiler_params=_params(("parallel", "parallel")),
        name="adamw_joined")(core, w, m, v, g_mine, g_theirs)


WEIGHTS = ['sc_w_in', 'sc_conv_w', 'sc_w_out', 'mla_w_dq', 'mla_g_q', 'mla_w_uq', 'mla_w_dkv', 'mla_g_kv', 'mla_w_uk',
           'mla_w_uv', 'mla_w_o', 'cf_w_pw1', 'cf_b_pw1', 'cf_dw_w', 'cf_dw_b', 'cf_norm_g', 'cf_norm_b', 'cf_w_pw2',
           'cf_b_pw2', 'ff_w1', 'ff_w2', 'ln_mix_g', 'ln_mix_b', 'ln_ff_g', 'ln_ff_b']
ARGS = ['x'] + WEIGHTS + ['loss_target'] + ['m_' + n for n in WEIGHTS] + ['v_' + n for n in WEIGHTS]


def _sq_relu(h):
    r = jnp.maximum(h, jnp.zeros_like(h))
    return r * r


def _mlp_forward(i, x, xb, w1, w2, g, b):
    hb = mm_plain_nn(f"mlp{i}_up", xb, w1, BF16, tn=1024)
    y, yb, xh, rstd = mm_residual_ln(f"mlp{i}_down_ln", hb, w2, x, g, b, tk=4096, a_fn=_sq_relu)
    return (y, yb), dict(xb=xb, hb=hb, xh=xh, rstd=rstd, g=g)


def _mlp_backward(i, dy, sv, w1, w2, dw1, dw2, reduce_after):
    s = dy.shape[0]
    dr, drb, dg, db, _ = ln_backward(f"mlp{i}_ln_bwd", dy, sv["xh"], sv["rstd"], sv["g"])
    tm, tn = _tile(s, 1024), 1024

    def epi(acc, e, o):
        o[0][...] = (acc * (2.0 * jnp.maximum(e[0][...].astype(F32), 0.0))).astype(BF16)

    dhb = mm_nt(f"mlp{i}_down_bwd", drb, w2, s, tm, tn, 1024, epi, [_sds((s, w2.k), BF16)], [_ij(tm, tn)],
                [sv["hb"]], [_ij(tm, tn)])[0]
    g_w2 = mm_tn(f"mlp{i}_dw2", sv["hb"], drb, dw2, s, 512, 1024, a_fn=_sq_relu)
    g_w1 = mm_tn(f"mlp{i}_dw1", sv["xb"], dhb, dw1, s, 1024, 512)
    dhb = reduce_after(dhb, {f"w1_{i}": g_w1, f"w2_{i}": g_w2})
    dx = mm_plain_nt(f"mlp{i}_up_bwd", dhb, w1, F32, tm=512, tn=1024, tk=4096, add=dr, add_scale=ALPHA)
    return dx, dg, db


def kernel(x, sc_w_in, sc_conv_w, sc_w_out, mla_w_dq, mla_g_q, mla_w_uq, mla_w_dkv, mla_g_kv, mla_w_uk, mla_w_uv, mla_w_o, cf_w_pw1, cf_b_pw1, cf_dw_w, cf_dw_b, cf_norm_g, cf_norm_b, cf_w_pw2, cf_b_pw2, ff_w1, ff_w2, ln_mix_g, ln_mix_b, ln_ff_g, ln_ff_b, loss_target, m_sc_w_in, m_sc_conv_w, m_sc_w_out, m_mla_w_dq, m_mla_g_q, m_mla_w_uq, m_mla_w_dkv, m_mla_g_kv, m_mla_w_uk, m_mla_w_uv, m_mla_w_o, m_cf_w_pw1, m_cf_b_pw1, m_cf_dw_w, m_cf_dw_b, m_cf_norm_g, m_cf_norm_b, m_cf_w_pw2, m_cf_b_pw2, m_ff_w1, m_ff_w2, m_ln_mix_g, m_ln_mix_b, m_ln_ff_g, m_ln_ff_b, v_sc_w_in, v_sc_conv_w, v_sc_w_out, v_mla_w_dq, v_mla_g_q, v_mla_w_uq, v_mla_w_dkv, v_mla_g_kv, v_mla_w_uk, v_mla_w_uv, v_mla_w_o, v_cf_w_pw1, v_cf_b_pw1, v_cf_dw_w, v_cf_dw_b, v_cf_norm_g, v_cf_norm_b, v_cf_w_pw2, v_cf_b_pw2, v_ff_w1, v_ff_w2, v_ln_mix_g, v_ln_mix_b, v_ln_ff_g, v_ln_ff_b):
    given = dict(zip(ARGS, (x, sc_w_in, sc_conv_w, sc_w_out, mla_w_dq, mla_g_q, mla_w_uq, mla_w_dkv, mla_g_kv, mla_w_uk, mla_w_uv, mla_w_o, cf_w_pw1, cf_b_pw1, cf_dw_w, cf_dw_b, cf_norm_g, cf_norm_b, cf_w_pw2, cf_b_pw2, ff_w1, ff_w2, ln_mix_g, ln_mix_b, ln_ff_g, ln_ff_b, loss_target, m_sc_w_in, m_sc_conv_w, m_sc_w_out, m_mla_w_dq, m_mla_g_q, m_mla_w_uq, m_mla_w_dkv, m_mla_g_kv, m_mla_w_uk, m_mla_w_uv, m_mla_w_o, m_cf_w_pw1, m_cf_b_pw1, m_cf_dw_w, m_cf_dw_b, m_cf_norm_g, m_cf_norm_b, m_cf_w_pw2, m_cf_b_pw2, m_ff_w1, m_ff_w2, m_ln_mix_g, m_ln_mix_b, m_ln_ff_g, m_ln_ff_b, v_sc_w_in, v_sc_conv_w, v_sc_w_out, v_mla_w_dq, v_mla_g_q, v_mla_w_uq, v_mla_w_dkv, v_mla_g_kv, v_mla_w_uk, v_mla_w_uv, v_mla_w_o, v_cf_w_pw1, v_cf_b_pw1, v_cf_dw_w, v_cf_dw_b, v_cf_norm_g, v_cf_norm_b, v_cf_w_pw2, v_cf_b_pw2, v_ff_w1, v_ff_w2, v_ln_mix_g, v_ln_mix_b, v_ln_ff_g, v_ln_ff_b)))
    s, d = x.shape[1], x.shape[2]
    d_ff = 4 * d
    dq4 = d // N_CHIPS
    xq = lax.axis_index("x") * 2 + lax.axis_index("y")

    w_dkv_pad = jnp.pad(mla_w_dkv[0], ((0, 0), (0, 128 - QK_ROPE)))
    w_uq_pad = jnp.pad(mla_w_uq[0].reshape(Q_LORA, 2, QK_NOPE + QK_ROPE), ((0, 0), (0, 0), (0, HEAD_PAD - QK_NOPE - QK_ROPE)))
    small = pack_rows("vector_weights_pack", [
        sc_conv_w.reshape(2 * SC_WIDTH, dq4), cf_b_pw1.reshape(2, dq4), cf_dw_w[0], cf_dw_b, cf_norm_g, cf_norm_b,
        cf_b_pw2], 64)
    mlp_w = lambda i: [ff_w1[i].astype(BF16), ff_w2[i].astype(BF16)]
    g_in, g_out, g_w1, g_w2 = [None] * 2, [None] * 2, [None] * DEPTH, [None] * DEPTH
    (g_in[0],) = gather_shards("gather_in0", [sc_w_in[0].astype(BF16)], by_columns=(0,))
    g_out[0], g_small = gather_shards("gather_mixer0", [sc_w_out[0].astype(BF16), small])
    (g_w1[0],) = gather_shards("gather_up0", [ff_w1[0].astype(BF16)], by_columns=(0,))
    (g_w2[0],) = gather_shards("gather_down0", [ff_w2[0].astype(BF16)])
    g_dqkv, g_uq, g_uk, g_uv, g_o = gather_shards("gather_mixer1", [
        jnp.concatenate([mla_w_dq[0], w_dkv_pad], axis=1).astype(BF16),
        w_uq_pad.reshape(Q_LORA, 2 * HEAD_PAD).astype(BF16),
        mla_w_uk.reshape(KV_LORA // N_CHIPS, N_HEADS * QK_NOPE).astype(BF16),
        mla_w_uv.reshape(KV_LORA // N_CHIPS, N_HEADS * V_HEAD).astype(BF16), mla_w_o[0].astype(BF16)], by_columns=(1,))
    g_w1[1], g_w2[1] = gather_shards("gather_mlp1", mlp_w(1), by_columns=(0,))
    g_pw1, g_pw2, g_w1[2], g_w2[2] = gather_shards(
        "gather_layer2", [cf_w_pw1[0].astype(BF16), cf_w_pw2[0].astype(BF16)] + mlp_w(2), by_columns=(0, 2))
    g_in[1], g_out[1], g_w1[3], g_w2[3] = gather_shards(
        "gather_layer3", [sc_w_in[1].astype(BF16), sc_w_out[1].astype(BF16)] + mlp_w(3), by_columns=(0, 2))

    wd_t = Q_LORA + KV_LORA + 128
    w_in = [Stk("full", d, 3 * d, g_in[j]) for j in range(2)]
    w_out = [Stk("row", d, d, g_out[j]) for j in range(2)]
    w_dqkv = Stk("row", d, wd_t, g_dqkv)
    w_uq = Stk("full", Q_LORA, N_HEADS * HEAD_PAD, g_uq)
    w_uk = Stk("row", KV_LORA, N_HEADS * QK_NOPE, g_uk)
    w_uv = Stk("row", KV_LORA, N_HEADS * V_HEAD, g_uv)
    w_o = Stk("row", d, d, g_o)
    w_pw1 = Stk("full", d, 2 * d, g_pw1)
    w_pw2 = Stk("row", d, d, g_pw2)
    w_1 = [Stk("full", d, d_ff, g_w1[i]) for i in range(DEPTH)]
    w_2 = [Stk("row", d_ff, d, g_w2[i]) for i in range(DEPTH)]

    def wide(rows):
        return jnp.swapaxes(rows, 0, 1).reshape(rows.shape[1], d)

    conv_w = wide(g_small[:, 0:6]).reshape(2, SC_WIDTH, d)
    b_pw1 = g_small[:, 6:8].reshape(1, 2 * d)
    dw_w = wide(g_small[:, 8:39])
    dw_b, norm_g, norm_b, b_pw2 = (wide(g_small[:, 39 + k:40 + k]) for k in range(4))

    pos = jnp.arange(s, dtype=F32)
    inv_freq = ROPE_THETA ** (-jnp.arange(0, QK_ROPE, 2, dtype=F32) / QK_ROPE)
    ang = pos[:, None] * inv_freq[None, :]
    cos, sin, zero = jnp.cos(ang), jnp.sin(ang), jnp.zeros((s, 128 - QK_ROPE), F32)
    cf = jnp.concatenate([cos, cos, zero], axis=1)
    sf = jnp.concatenate([-sin, sin, zero], axis=1)

    def row(a, i):
        return a[i:i + 1]

    xs = x.reshape(s, d)
    cur = (xs, xs.astype(BF16))
    tape = []
    for i in range(DEPTH):
        mixer, j = i % 3, i // 3
        xf, xb = cur
        lg, lb = row(ln_mix_g, i), row(ln_mix_b, i)
        if mixer == 0:
            u = mm_plain_nn(f"sc{j}_in", xb, w_in[j], BF16, tn=3 * dq4)
            gb = short_conv_gate(u, conv_w[j])
            y, yb, xh, rstd = mm_residual_ln(f"sc{j}_out_ln", gb, w_out[j], xf, lg, lb)
            sv = dict(xb=xb, u=u, gb=gb)
        elif mixer == 1:
            t = mm_plain_nn("mla_down", xb, w_dqkv, F32, tn=wd_t // 2)
            cq, ckv, kpe = mla_latents(t, mla_g_q, mla_g_kv, cf, sf)
            qh = mla_queries(cq, w_uq, cf, sf)
            kh = mla_keys(ckv, w_uk, kpe)
            vh = mm_plain_nn("mla_values", ckv, w_uv, BF16, tk=KV_LORA)
            oh = attention(qh, kh, vh)
            y, yb, xh, rstd = mm_residual_ln("mla_out_ln", oh, w_o, xf, lg, lb)
            sv = dict(xb=xb, t=t, cq=cq, ckv=ckv, qh=qh, kh=kh, vh=vh, oh=oh)
        else:
            u = mm_plain_nn("cf_pw1", xb, w_pw1, BF16, bias=b_pw1)
            hc = conformer_glu_conv(u, dw_w, dw_b)
            sb = conformer_norm_swish(hc, norm_g, norm_b)
            y, yb, xh, rstd = mm_residual_ln("cf_pw2_ln", sb, w_pw2, xf, lg, lb, bias=b_pw2)
            sv = dict(xb=xb, u=u, hc=hc, sb=sb)
        sv.update(xh=xh, rstd=rstd, g=lg)
        cur, sv_mlp = _mlp_forward(i, y, yb, w_1[i], w_2[i], row(ln_ff_g, i), row(ln_ff_b, i))
        tape.append((sv, sv_mlp))

    dy, loss_part = loss_head(cur[0], loss_target.reshape(s, d))

    grads = {}
    smalls = {}
    g_ln = {n: [None] * DEPTH for n in ("ln_mix_g", "ln_mix_b", "ln_ff_g", "ln_ff_b")}
    conv_grads = [None, None]
    core = lax.axis_index("c").astype(jnp.int32).reshape(1)
    chip = xq.astype(jnp.int32).reshape(1)
    pairs, landed = {}, {}
    ready, theirs = [], {}

    def reduce_after(x, new, early=False):
        out = lax.optimization_barrier((x, *new.values()))
        grads.update(zip(new, out[1:]))
        if early:
            theirs.update(zip(new, pair_exchange(f"pair_exchange_{len(theirs)}", list(out[1:]), True)))
        ready.extend(new)
        return out[0]

    def reduce_layer(i, x):
        late = [n for n in ready if n not in theirs]
        if late:
            theirs.update(zip(late, pair_exchange(f"pair_exchange_layer{i}", [grads[n] for n in late], False)))
        sums = [pair_sum(grads[n], theirs[n], core) for n in ready]
        pairs.update(zip(ready, sums))
        landed.update(zip(ready, chip_exchange(f"chip_exchange_layer{i}", sums)))
        exchanged.append(list(ready))
        ready.clear()
        return lax.optimization_barrier((x, *sums))[0]

    groups = [["in_0", "in_1"], ["out_0", "out_1"], ["dqkv"], ["uq"], ["uk"], ["uv"], ["o"], ["pw1"], ["pw2"],
              [f"w1_{i}" for i in range(DEPTH)], [f"w2_{i}" for i in range(DEPTH)]]
    stacks = [None] * len(groups)
    exchanged = []

    def sum_layer(x, last=False):
        names = exchanged.pop(0)
        if last:
            out = lax.optimization_barrier((x, *[landed[n] for n in names]))
            landed.update(zip(names, out[1:]))
        new = []
        for n in names:
            k = next(k for k, members in enumerate(groups) if n in members)
            stacks[k] = chip_sum(pairs[n], landed[n], chip, stacks[k], groups[k].index(n), len(groups[k]))
            new.append(stacks[k])
        return out[0] if last else lax.optimization_barrier((x, *new))[0]

    for i in reversed(range(DEPTH)):
        mixer, j = i % 3, i // 3
        sv, sv_mlp = tape[i]
        dy, g_ln["ln_ff_g"][i], g_ln["ln_ff_b"][i] = _mlp_backward(
            i, dy, sv_mlp, w_1[i], w_2[i], Stk("col", d, d_ff), Stk("row", d_ff, d),
            lambda x_, new: reduce_after(x_, new, early=i > 0))
        if i == 0:
            dy = reduce_layer("0_mlp", dy)
        dr, drb, g_ln["ln_mix_g"][i], g_ln["ln_mix_b"][i], dr_sum = ln_backward(
            f"mix{i}_ln_bwd", dy, sv["xh"], sv["rstd"], sv["g"])
        if mixer == 0:
            dgate = mm_plain_nt(f"sc{j}_out_bwd", drb, w_out[j], F32)
            dw_out = mm_tn(f"sc{j}_dw_out", sv["gb"], drb, Stk("row", d, d), s, 512, 1024)
            du, conv_grads[j] = short_conv_gate_bwd(sv["u"], conv_w[j], dgate)
            nb = d // 256
            dw_in = mm_tn(
                f"sc{j}_dw_in", sv["xb"], du, Stk("col", d, 3 * d), s, 1024, 256,
                b_spec=pl.BlockSpec((None, s, 256), lambda i_, j_, k_: (j_ // nb, k_, j_ % nb)))
            du = reduce_after(du, {f"in_{j}": dw_in, f"out_{j}": dw_out})
            dy = mm_plain_nt(
                f"sc{j}_in_bwd", du, w_in[j], F32, tn=1024, tk=d, add=dr, add_scale=ALPHA,
                a_spec_fn=(s, lambda tm, tk: pl.BlockSpec((None, tm, tk), lambda i_, j_, k_: (k_, i_, 0))))
        elif mixer == 1:
            do = mm_plain_nt("mla_out_bwd", drb, w_o, BF16)
            g_o = mm_tn("mla_dw_o", sv["oh"], drb, Stk("row", d, d), s, 512, 1024)
            dqh, dkh, dvh = attention_bwd(sv["qh"], sv["kh"], sv["vh"], do)
            dql, dkn, dkpe = mla_unrope_grads(dqh, dkh, cf, sf)
            g_uq = mm_tn("mla_dw_uq", sv["cq"], dql, Stk("col", Q_LORA, N_HEADS * HEAD_PAD), s, Q_LORA, 512)
            dcq = mm_plain_nt("mla_uq_bwd", dql, w_uq, F32, tn=Q_LORA)
            g_uk = mm_tn("mla_dw_uk", sv["ckv"], dkn, Stk("row", KV_LORA, N_HEADS * QK_NOPE), s, KV_LORA, 1024)
            g_uv = mm_tn("mla_dw_uv", sv["ckv"], dvh, Stk("row", KV_LORA, N_HEADS * V_HEAD), s, KV_LORA, 1024)
            dckv = mm_plain_nt("mla_uk_bwd", dkn, w_uk, F32, tn=KV_LORA)
            dckv = mm_plain_nt("mla_uv_bwd", dvh, w_uv, F32, tn=KV_LORA, add=dckv)
            dt, smalls["g_q"], smalls["g_kv"] = mla_latents_bwd(sv["t"], mla_g_q, mla_g_kv, cf, sf, dcq, dckv, dkpe)
            g_dqkv = mm_tn("mla_dw_down", sv["xb"], dt, Stk("row", d, wd_t), s, 512, wd_t)
            dt = reduce_after(dt, {"dqkv": g_dqkv, "uq": g_uq, "uk": g_uk, "uv": g_uv, "o": g_o})
            dy = mm_plain_nt("mla_down_bwd", dt, w_dqkv, F32, tk=wd_t, add=dr, add_scale=ALPHA)
        else:
            dsw = mm_plain_nt("cf_pw2_bwd", drb, w_pw2, F32)
            g_pw2 = mm_tn("cf_dw_pw2", sv["sb"], drb, Stk("row", d, d), s, 512, 1024)
            smalls["b_pw2"] = dr_sum
            dhc, smalls["norm_g"], smalls["norm_b"] = conformer_norm_swish_bwd(sv["hc"], norm_g, norm_b, dsw)
            du, smalls["b_pw1"], smalls["dw_w"], smalls["dw_b"] = conformer_glu_conv_bwd(sv["u"], dw_w, dhc)
            nb = d // 512
            g_pw1 = mm_tn(
                "cf_dw_pw1", sv["xb"], du, Stk("col", d, 2 * d), s, 1024, 512,
                b_spec=pl.BlockSpec((None, s, 512), lambda i_, j_, k_: (j_ // nb, k_, j_ % nb)))
            du = reduce_after(du, {"pw1": g_pw1, "pw2": g_pw2})
            dy = mm_plain_nt(
                "cf_pw1_bwd", du, w_pw1, F32, tn=1024, tk=d, add=dr, add_scale=ALPHA,
                a_spec_fn=(s, lambda tm, tk: pl.BlockSpec((None, tm, tk), lambda i_, j_, k_: (k_, i_, 0))))
        if i < DEPTH - 1:
            dy = sum_layer(dy)
        dy = reduce_layer(i, dy)
    dy = sum_layer(sum_layer(dy, last=True), last=True)
    grad_x = dy.reshape(1, s, d)

    mine = stacks
    other = (pair_share("pair_share_mixers", mine[:9]) + pair_share("pair_share_up", mine[9:10])
             + pair_share("pair_share_down", mine[10:]))

    def padded(get):
        dqkv = jnp.concatenate([get("mla_w_dq")[0], jnp.pad(get("mla_w_dkv")[0], ((0, 0), (0, 128 - QK_ROPE)))], axis=1)
        uq = jnp.pad(get("mla_w_uq")[0].reshape(Q_LORA, 2, QK_NOPE + QK_ROPE),
                     ((0, 0), (0, 0), (0, HEAD_PAD - QK_NOPE - QK_ROPE))).reshape(Q_LORA, 2 * HEAD_PAD)
        return [get("sc_w_in"), get("sc_w_out"), dqkv[None], uq[None],
                get("mla_w_uk").reshape(1, KV_LORA // N_CHIPS, d), get("mla_w_uv").reshape(1, KV_LORA // N_CHIPS, d),
                get("mla_w_o"), get("cf_w_pw1"), get("cf_w_pw2"), get("ff_w1"), get("ff_w2")]

    w_l, m_l, v_l = (padded(lambda n, p=p: given[p + n]) for p in ("", "m_", "v_"))
    res = [adamw_joined(w_l[k], m_l[k], v_l[k], mine[k], other[k], core) for k in range(len(groups))]

    def unpadded(k):
        r_in, r_out, r_dqkv, r_uq, r_uk, r_uv, r_o, r_pw1, r_pw2, r_w1, r_w2 = (r[k] for r in res)
        return {
            "sc_w_in": r_in, "sc_w_out": r_out, "mla_w_dq": r_dqkv[:, :, 0:Q_LORA],
            "mla_w_dkv": r_dqkv[:, :, Q_LORA:Q_LORA + KV_LORA + QK_ROPE],
            "mla_w_uq": r_uq.reshape(1, Q_LORA, 2, HEAD_PAD)[:, :, :, 0:QK_NOPE + QK_ROPE].reshape(mla_w_uq.shape),
            "mla_w_uk": r_uk.reshape(mla_w_uk.shape), "mla_w_uv": r_uv.reshape(mla_w_uv.shape),
            "mla_w_o": r_o, "cf_w_pw1": r_pw1, "cf_w_pw2": r_pw2, "ff_w1": r_w1, "ff_w2": r_w2}

    big_g, big_d, big_m, big_v = (unpadded(k) for k in range(4))

    pad_row = lambda a: jnp.pad(a, ((0, 0), (0, d - a.shape[1])))
    small_parts = ([g for n in ("ln_mix_g", "ln_mix_b", "ln_ff_g", "ln_ff_b") for g in g_ln[n]]
                   + [pad_row(smalls["g_q"]), pad_row(smalls["g_kv"]), conv_grads[0], conv_grads[1],
                      smalls["b_pw1"].reshape(2, d), smalls["dw_w"], smalls["dw_b"], smalls["norm_g"], smalls["norm_b"],
                      smalls["b_pw2"], loss_part])
    red = all_reduce_small(small_parts, 64)
    loss = red[61, 0]

    where = {
        "ln_mix_g": [((), 0, DEPTH, "all")], "ln_mix_b": [((), 4, DEPTH, "all")],
        "ln_ff_g": [((), 8, DEPTH, "all")], "ln_ff_b": [((), 12, DEPTH, "all")],
        "mla_g_q": [((), 16, 1, Q_LORA)], "mla_g_kv": [((), 17, 1, KV_LORA)],
        "sc_conv_w": [((0,), 18, SC_WIDTH, "chip"), ((1,), 21, SC_WIDTH, "chip")],
        "cf_b_pw1": [((), 24, 2, "chip")], "cf_dw_w": [((0,), 26, CONF_WIDTH, "chip")],
        "cf_dw_b": [((), 57, 1, "chip")], "cf_norm_g": [((), 58, 1, "chip")], "cf_norm_b": [((), 59, 1, "chip")],
        "cf_b_pw2": [((), 60, 1, "chip")]}
    vec = list(where)
    vec_res = vector_update(red, chip, [given[n] for n in vec], [given["m_" + n] for n in vec],
                            [given["v_" + n] for n in vec], [where[n] for n in vec])
    gw = dict(big_g)
    upd = {n: [big_d[n], big_m[n], big_v[n]] for n in big_g}
    for k, n in enumerate(vec):
        gw[n] = vec_res[0][k]
        upd[n] = [vec_res[1][k], vec_res[2][k], vec_res[3][k]]

    return (loss, grad_x, *[gw[n] for n in WEIGHTS], *[upd[n][0] for n in WEIGHTS],
            *[upd[n][1] for n in WEIGHTS], *[upd[n][2] for n in WEIGHTS])
```

```python
import jax
import jax.numpy as jnp
from jax import lax
from jax.experimental import pallas as pl
from jax.experimental.pallas import tpu as pltpu
from jax.experimental.pallas import tpu_sc as plsc

F32 = jnp.float32
BF16 = jnp.bfloat16
MESH = pl.DeviceIdType.MESH

DEPTH = 4
ALPHA = (2.0 * DEPTH) ** 0.25
LN_EPS = 1e-5
RMS_EPS = 1e-6
CHUNK_SHIFT = 6
N_HEADS = 8
QK_NOPE = 128
QK_ROPE = 64
V_HEAD = 128
HEAD_PAD = 256
Q_LORA = 384
KV_LORA = 256
ROPE_THETA = 10000.0
SC_WIDTH = 3
CONF_WIDTH = 31
CONV_PAD = 32
CONV_CHUNK = 64
N_CHIPS = 4
ATTN_SCALE = (QK_NOPE + QK_ROPE) ** -0.5

ADAM_LR = 0.001
ADAM_B1 = 0.9
ADAM_B2 = 0.999
ADAM_EPS = 1e-08
ADAM_WD = 0.01
ADAM_STEP = 10

VMEM_LIMIT = 56 * 2**20

NN = (((1,), (0,)), ((), ()))
NT = (((1,), (1,)), ((), ()))
TN = (((0,), (0,)), ((), ()))


def _params(sem=None):
    return pltpu.CompilerParams(dimension_semantics=sem, vmem_limit_bytes=VMEM_LIMIT)


class Stk:
    def __init__(self, kind, k, n, arr=None, layers=None, layer=None):
        self.kind, self.k, self.n, self.layers, self.layer = kind, k, n, layers, layer
        self.plain = (kind == "row" and layers is None) or kind == "full"
        self.kloc = k // N_CHIPS if kind == "row" else k
        self.nloc = n // N_CHIPS if kind == "col" else n
        if arr is not None and self.plain:
            arr = arr.reshape(k, n)
        self.arr = arr

    @property
    def shape(self):
        if self.plain:
            return (self.k, self.n)
        lead = (N_CHIPS,) if self.layers is None else (N_CHIPS, self.layers)
        return lead + (self.kloc, self.nloc)

    def spec(self, bk, bn, f):
        if self.plain:
            return pl.BlockSpec((bk, bn), f)
        assert self.kloc % bk == 0 and self.nloc % bn == 0, (self.kloc, bk, self.nloc, bn)
        pk, pn = self.kloc // bk, self.nloc // bn
        kind, layer = self.kind, self.layer

        def imap(*g):
            kb, nb = f(*g)
            if kind == "row":
                q, kb, nb = kb // pk, kb % pk, nb
            else:
                q, kb, nb = nb // pn, kb, nb % pn
            return (q, kb, nb) if layer is None else (q, layer, kb, nb)

        block = (None, bk, bn) if layer is None else (None, None, bk, bn)
        return pl.BlockSpec(block, imap)


def _mm(name, mode, a, b, grid, a_spec, b_spec, acc_shape, extras, extra_specs, out_shapes, out_specs, epi, a_fn=None,
        rows_in_order=False):
    nk = grid[2]
    ne = len(extras)

    def body(*refs):
        a_ref, b_ref = refs[0], refs[1]
        e_refs = refs[2:2 + ne]
        av = a_ref[...] if a_fn is None else a_fn(a_ref[...])
        part = lax.dot_general(av, b_ref[...], mode, preferred_element_type=F32)
        if nk == 1:
            epi(part, e_refs, refs[2 + ne:])
            return
        o_refs = refs[2 + ne:-1]
        acc = refs[-1]
        k = pl.program_id(2)

        @pl.when(k == 0)
        def _():
            acc[...] = part

        @pl.when(k > 0)
        def _():
            acc[...] += part

        @pl.when(k == nk - 1)
        def _():
            epi(acc[...], e_refs, o_refs)

    return pl.pallas_call(
        body, grid=grid, in_specs=[a_spec, b_spec, *extra_specs], out_specs=out_specs, out_shape=out_shapes,
        scratch_shapes=[pltpu.VMEM(acc_shape, F32)] if nk > 1 else [],
        compiler_params=_params(("arbitrary",) * 3 if rows_in_order else ("parallel", "parallel", "arbitrary")),
        name=name)(a, b, *extras)


def _tile(n, t):
    t = min(n, t)
    while n % t:
        t -= 8
    assert t > 0, (n, t)
    return t


def mm_nn(name, a, w, tm, tn, tk, epi, out_shapes, out_specs, extras=(), extra_specs=(), a_spec=None, a_fn=None):
    m = a.shape[0]
    tm, tn, tk = _tile(m, tm), _tile(w.n, tn), _tile(w.k, tk)
    grid = (m // tm, w.n // tn, w.k // tk)
    a_spec = a_spec or pl.BlockSpec((tm, tk), lambda i, j, k: (i, k))
    b_spec = w.spec(tk, tn, lambda i, j, k: (k, j))
    return _mm(name, NN, a, w.arr, grid, a_spec, b_spec, (tm, tn), extras, extra_specs, out_shapes, out_specs, epi, a_fn)


def mm_nt(name, a, w, m, tm, tn, tk, epi, out_shapes, out_specs, extras=(), extra_specs=(), a_spec=None,
          rows_in_order=False):
    tm, tn, tk = _tile(m, tm), _tile(w.k, tn), _tile(w.n, tk)
    grid = (m // tm, w.k // tn, w.n // tk)
    a_spec = a_spec or pl.BlockSpec((tm, tk), lambda i, j, k: (i, k))
    b_spec = w.spec(tn, tk, lambda i, j, k: (j, k))
    return _mm(name, NT, a, w.arr, grid, a_spec, b_spec, (tm, tn), extras, extra_specs, out_shapes, out_specs, epi,
               rows_in_order=rows_in_order)


def mm_tn(name, a, b, dw, s, tm=512, tn=512, tk=4096, a_spec=None, b_spec=None, a_fn=None):
    tm, tn, tk = _tile(dw.k, tm), _tile(dw.n, tn), _tile(s, tk)
    grid = (dw.k // tm, dw.n // tn, s // tk)
    a_spec = a_spec or pl.BlockSpec((tk, tm), lambda i, j, k: (k, i))
    b_spec = b_spec or pl.BlockSpec((tk, tn), lambda i, j, k: (k, j))

    def epi(acc, e, o):
        o[0][...] = acc.astype(BF16)

    out = _mm(name, TN, a, b, grid, a_spec, b_spec, (tm, tn), (), (), [jax.ShapeDtypeStruct(dw.shape, BF16)],
              [dw.spec(tm, tn, lambda i, j, k: (i, j))], epi, a_fn)[0]
    return out.reshape(N_CHIPS, dw.k // N_CHIPS, dw.n) if dw.plain else out


def _sds(shape, dtype):
    return jax.ShapeDtypeStruct(shape, dtype)


def _ij(tm, tn):
    return pl.BlockSpec((tm, tn), lambda i, j, k: (i, j))


def _i0(tm, c):
    return pl.BlockSpec((tm, c), lambda i, j, k: (i, 0))


def _0j(r, tn):
    return pl.BlockSpec((r, tn), lambda i, j, k: (0, j))


def _layer_norm_rows(r, g, b):
    mu = jnp.mean(r, axis=-1, keepdims=True)
    d = r - mu
    var = jnp.mean(d * d, axis=-1, keepdims=True)
    rstd = lax.rsqrt(var + LN_EPS)
    xh = d * rstd
    return xh * g + b, xh, rstd


def mm_residual_ln(name, a, w, x, g, b, bias=None, tm=512, tk=1024, a_fn=None):
    s, d = x.shape
    tm = _tile(s, tm)
    extras = [x, g, b] + ([bias] if bias is not None else [])
    especs = [_i0(tm, d), _0j(1, d), _0j(1, d)] + ([_0j(1, d)] if bias is not None else [])

    def epi(acc, e, o):
        r = ALPHA * e[0][...] + acc
        if bias is not None:
            r = r + e[3][...]
        y, xh, rstd = _layer_norm_rows(r, e[1][...], e[2][...])
        o[0][...] = y
        o[1][...] = y.astype(BF16)
        o[2][...] = xh
        o[3][...] = rstd

    return mm_nn(name, a, w, tm, d, tk, epi,
                 [_sds((s, d), F32), _sds((s, d), BF16), _sds((s, d), F32), _sds((s, 1), F32)],
                 [_i0(tm, d), _i0(tm, d), _i0(tm, d), _i0(tm, 1)], extras, especs, a_fn=a_fn)


def mm_plain_nn(name, a, w, out_dtype, tm=1024, tn=512, tk=1024, bias=None):
    m = a.shape[0]
    tm, tn = _tile(m, tm), _tile(w.n, tn)
    if w.kind == "col":
        tn = _tile(w.nloc, tn)

    def epi(acc, e, o):
        if bias is not None:
            acc = acc + e[0][...]
        o[0][...] = acc.astype(out_dtype)

    extras, especs = ([bias], [_0j(1, tn)]) if bias is not None else ((), ())
    return mm_nn(name, a, w, tm, tn, tk, epi, [_sds((m, w.n), out_dtype)], [_ij(tm, tn)], extras, especs)[0]


def mm_plain_nt(name, a, w, out_dtype, tm=1024, tn=512, tk=1024, add=None, add_scale=1.0, a_spec_fn=None):
    m = a.shape[0] if a_spec_fn is None else a_spec_fn[0]
    tm, tn = _tile(m, tm), _tile(w.k, tn)
    tk = _tile(w.n, tk)
    if w.kind == "col":
        tk = _tile(w.nloc, tk)
    if w.kind == "row" and not w.plain:
        tn = _tile(w.kloc, tn)

    def epi(acc, e, o):
        if add is not None:
            acc = acc + add_scale * e[0][...].astype(F32)
        o[0][...] = acc.astype(out_dtype)

    extras, especs = ([add], [_ij(tm, tn)]) if add is not None else ((), ())
    a_spec = None if a_spec_fn is None else a_spec_fn[1](tm, tk)
    return mm_nt(name, a, w, m, tm, tn, tk, epi, [_sds((m, w.k), out_dtype)], [_ij(tm, tn)], extras, especs,
                 a_spec=a_spec)[0]


def _rows(tm, c):
    return pl.BlockSpec((tm, c), lambda i: (i, 0))


def _fix(shape):
    nd = len(shape)
    return pl.BlockSpec(shape, lambda i: (0,) * nd)


def _accumulate(ref, val):
    @pl.when(pl.program_id(0) == 0)
    def _():
        ref[...] = jnp.zeros_like(ref)

    ref[...] += val


def _ln_backward_rows(dyv, xh, rstd, g, dr_ref, drb_ref, dg_ref, db_ref, ds_ref):
    dxh = dyv * g
    m1 = jnp.mean(dxh, axis=-1, keepdims=True)
    m2 = jnp.mean(dxh * xh, axis=-1, keepdims=True)
    dr = rstd * (dxh - m1 - xh * m2)
    dr_ref[...] = dr
    drb_ref[...] = dr.astype(BF16)
    _accumulate(dg_ref, jnp.sum(dyv * xh, axis=0, keepdims=True))
    _accumulate(db_ref, jnp.sum(dyv, axis=0, keepdims=True))
    _accumulate(ds_ref, jnp.sum(dr, axis=0, keepdims=True))


def mm_nt_ln_backward(name, a, w, add, xhat, rstd, g, tm=512, tk=1024, a_spec_fn=None):
    m, d = add.shape
    tm, tk = _tile(m, tm), _tile(w.n, tk)

    def epi(acc, e, o):
        _ln_backward_rows(acc + ALPHA * e[0][...], e[1][...], e[2][...], e[3][...], *o)

    vec = pl.BlockSpec((1, d), lambda i, j, k: (0, 0))
    a_spec = None if a_spec_fn is None else a_spec_fn(tm, tk)
    return mm_nt(name, a, w, m, tm, d, tk, epi,
                 [_sds((m, d), F32), _sds((m, d), BF16), _sds((1, d), F32), _sds((1, d), F32), _sds((1, d), F32)],
                 [_i0(tm, d), _i0(tm, d), vec, vec, vec], [add, xhat, rstd, g],
                 [_i0(tm, d), _i0(tm, d), _i0(tm, 1), vec], a_spec=a_spec, rows_in_order=True)


def loss_ln_backward(y, target, xhat, rstd, g, tm=512):
    s, d = y.shape
    tm = _tile(s, tm)

    def body(y_ref, t_ref, xh_ref, rstd_ref, g_ref, dr_ref, drb_ref, dg_ref, db_ref, ds_ref, loss_ref):
        e = y_ref[...] - t_ref[...]
        part = 0.5 * jnp.sum(jnp.mean(e * e, axis=-1, keepdims=True), axis=0, keepdims=True)
        _accumulate(loss_ref, jnp.broadcast_to(part, (1, d)))
        _ln_backward_rows(e * (1.0 / d), xh_ref[...], rstd_ref[...], g_ref[...], dr_ref, drb_ref, dg_ref, db_ref, ds_ref)

    return pl.pallas_call(
        body, grid=(s // tm,),
        in_specs=[_rows(tm, d), _rows(tm, d), _rows(tm, d), _rows(tm, 1), _fix((1, d))],
        out_specs=[_rows(tm, d), _rows(tm, d)] + [_fix((1, d))] * 4,
        out_shape=[_sds((s, d), F32), _sds((s, d), BF16)] + [_sds((1, d), F32)] * 4,
        compiler_params=_params(("arbitrary",)), name="loss_ln_backward")(y, target, xhat, rstd, g)


def _cols(s, tc, off=0):
    return pl.BlockSpec((s, tc), lambda i: (0, i + off))


def _shift_down(z, sft, rows):
    return jnp.where(rows >= sft, pltpu.roll(z, sft, 0), 0.0)


def _shift_up(z, sft, rows, s):
    return jnp.where(rows < s - sft, pltpu.roll(z, (s - sft) % s, 0), 0.0)


def short_conv_gate(u, conv_w, tc=256):
    s, d3 = u.shape
    d = d3 // 3
    nb = d // tc

    def body(b_ref, c_ref, h_ref, w_ref, o_ref):
        rows = lax.broadcasted_iota(jnp.int32, (s, tc), 0)
        z = c_ref[...] * h_ref[...]
        cz = jnp.zeros((s, tc), F32)
        for k in range(SC_WIDTH):
            sft = SC_WIDTH - 1 - k
            cz = cz + w_ref[pl.ds(k, 1), :] * (_shift_down(z, sft, rows) if sft else z)
        o_ref[...] = (b_ref[...] * cz).astype(BF16)

    return pl.pallas_call(
        body, grid=(nb,),
        in_specs=[_cols(s, tc), _cols(s, tc, nb), _cols(s, tc, 2 * nb), _cols(SC_WIDTH, tc)],
        out_specs=_cols(s, tc), out_shape=_sds((s, d), BF16),
        compiler_params=_params(("parallel",)), name="short_conv_gate")(u, u, u, conv_w)


def short_conv_gate_bwd(u, conv_w, dg, tc=256):
    s, d3 = u.shape
    d = d3 // 3
    nb = d // tc

    def body(b_ref, c_ref, h_ref, w_ref, dg_ref, du_ref, dw_ref):
        rows = lax.broadcasted_iota(jnp.int32, (s, tc), 0)
        c, h, dgv = c_ref[...], h_ref[...], dg_ref[...]
        z = c * h
        dcz = dgv * b_ref[...]
        cz = jnp.zeros((s, tc), F32)
        dz = jnp.zeros((s, tc), F32)
        for k in range(SC_WIDTH):
            sft = SC_WIDTH - 1 - k
            zs = _shift_down(z, sft, rows) if sft else z
            wk = w_ref[pl.ds(k, 1), :]
            cz = cz + wk * zs
            dz = dz + wk * (_shift_up(dcz, sft, rows, s) if sft else dcz)
            dw_ref[pl.ds(k, 1), :] = jnp.sum(dcz * zs, axis=0, keepdims=True)
        du_ref[0] = (dgv * cz).astype(BF16)
        du_ref[1] = (dz * h).astype(BF16)
        du_ref[2] = (dz * c).astype(BF16)

    return pl.pallas_call(
        body, grid=(nb,),
        in_specs=[_cols(s, tc), _cols(s, tc, nb), _cols(s, tc, 2 * nb), _cols(SC_WIDTH, tc), _cols(s, tc)],
        out_specs=[pl.BlockSpec((3, s, tc), lambda i: (0, 0, i)), _cols(SC_WIDTH, tc)],
        out_shape=[_sds((3, s, d), BF16), _sds((SC_WIDTH, d), F32)],
        compiler_params=_params(("parallel",)), name="short_conv_gate_bwd")(u, u, u, conv_w, dg)


def _store_shifted_down(ref, z, rows):
    s, tc = z.shape
    for b in range(8):
        ref[b, pl.ds(0, CONV_PAD), :] = jnp.zeros((CONV_PAD, tc), F32)
        ref[b, pl.ds(CONV_PAD, s), :] = z if b == 0 else _shift_down(z, b, rows)


def _store_shifted_up(ref, z, rows):
    s, tc = z.shape
    for b in range(8):
        ref[b, pl.ds(0, s), :] = z if b == 0 else _shift_up(z, b, rows, s)
        ref[b, pl.ds(s, CONV_PAD), :] = jnp.zeros((CONV_PAD, tc), F32)


def conformer_glu_conv(u, dw_w, dw_b, tc=128):
    s, d2 = u.shape
    d = d2 // 2
    nb = d // tc

    ch = min(CONV_CHUNK, s)

    def body(a_ref, g_ref, w_ref, b_ref, o_ref, down):
        rows = lax.broadcasted_iota(jnp.int32, (s, tc), 0)
        _store_shifted_down(down, a_ref[...] * jax.nn.sigmoid(g_ref[...]), rows)

        def chunk(ci, carry):
            r0 = pl.multiple_of(ci * ch, ch)
            acc = jnp.broadcast_to(b_ref[...], (ch, tc))
            for k in range(CONF_WIDTH):
                sft = CONF_WIDTH - 1 - k
                acc = acc + w_ref[pl.ds(k, 1), :] * down[sft % 8, pl.ds(CONV_PAD + r0 - (sft // 8) * 8, ch), :]
            o_ref[pl.ds(r0, ch), :] = acc
            return carry

        lax.fori_loop(0, s // ch, chunk, 0)

    return pl.pallas_call(
        body, grid=(nb,),
        in_specs=[_cols(s, tc), _cols(s, tc, nb), _cols(CONF_WIDTH, tc), _cols(1, tc)],
        out_specs=_cols(s, tc), out_shape=_sds((s, d), F32),
        scratch_shapes=[pltpu.VMEM((8, CONV_PAD + s, tc), F32)],
        compiler_params=_params(("parallel",)), name="conformer_glu_conv")(u, u, dw_w, dw_b)


def conformer_glu_conv_bwd(u, dw_w, dhc, tc=128):
    s, d2 = u.shape
    d = d2 // 2
    nb = d // tc
    ch = min(CONV_CHUNK, s)

    def body(a_ref, g_ref, w_ref, dhc_ref, du_ref, dbias_ref, dw_ref, db_ref, down, up, dw_acc, dh_buf):
        rows = lax.broadcasted_iota(jnp.int32, (s, tc), 0)
        a = a_ref[...]
        sg = jax.nn.sigmoid(g_ref[...])
        dhcv = dhc_ref[...]
        _store_shifted_down(down, a * sg, rows)
        _store_shifted_up(up, dhcv, rows)
        dw_acc[...] = jnp.zeros_like(dw_acc)

        def chunk(ci, carry):
            r0 = pl.multiple_of(ci * ch, ch)
            dc = dhc_ref[pl.ds(r0, ch), :]
            dh = jnp.zeros((ch, tc), F32)
            for k in range(CONF_WIDTH):
                sft = CONF_WIDTH - 1 - k
                a8, b = (sft // 8) * 8, sft % 8
                dh = dh + w_ref[pl.ds(k, 1), :] * up[b, pl.ds(r0 + a8, ch), :]
                prod = dc * down[b, pl.ds(CONV_PAD + r0 - a8, ch), :]
                dw_acc[k] += jnp.sum(prod.reshape(ch // 8, 8, tc), axis=0)
            dh_buf[pl.ds(r0, ch), :] = dh
            return carry

        lax.fori_loop(0, s // ch, chunk, 0)
        dh = dh_buf[...]
        da = dh * sg
        dgate = dh * a * sg * (1.0 - sg)
        du_ref[0] = da.astype(BF16)
        du_ref[1] = dgate.astype(BF16)
        dbias_ref[pl.ds(0, 1), :] = jnp.sum(da, axis=0, keepdims=True)
        dbias_ref[pl.ds(1, 1), :] = jnp.sum(dgate, axis=0, keepdims=True)
        db_ref[...] = jnp.sum(dhcv, axis=0, keepdims=True)
        for k in range(CONF_WIDTH):
            dw_ref[pl.ds(k, 1), :] = jnp.sum(dw_acc[k], axis=0, keepdims=True)

    return pl.pallas_call(
        body, grid=(nb,),
        in_specs=[_cols(s, tc), _cols(s, tc, nb), _cols(CONF_WIDTH, tc), _cols(s, tc)],
        out_specs=[pl.BlockSpec((2, s, tc), lambda i: (0, 0, i)), _cols(2, tc), _cols(CONF_WIDTH, tc), _cols(1, tc)],
        out_shape=[_sds((2, s, d), BF16), _sds((2, d), F32), _sds((CONF_WIDTH, d), F32), _sds((1, d), F32)],
        scratch_shapes=[pltpu.VMEM((8, CONV_PAD + s, tc), F32), pltpu.VMEM((8, CONV_PAD + s, tc), F32),
                        pltpu.VMEM((CONF_WIDTH + 1, 8, tc), F32), pltpu.VMEM((s, tc), F32)],
        compiler_params=_params(("parallel",)), name="conformer_glu_conv_bwd")(u, u, dw_w, dhc)


def conformer_norm_swish(hc, g, b, tm=512):
    s, d = hc.shape
    tm = _tile(s, tm)

    def body(h_ref, g_ref, b_ref, o_ref):
        n, _, _ = _layer_norm_rows(h_ref[...], g_ref[...], b_ref[...])
        o_ref[...] = (n * jax.nn.sigmoid(n)).astype(BF16)

    return pl.pallas_call(
        body, grid=(s // tm,), in_specs=[_rows(tm, d), _fix((1, d)), _fix((1, d))], out_specs=_rows(tm, d),
        out_shape=_sds((s, d), BF16), compiler_params=_params(("parallel",)), name="conformer_norm_swish")(hc, g, b)


def conformer_norm_swish_bwd(hc, g, b, ds, tm=512):
    s, d = hc.shape
    tm = _tile(s, tm)

    def body(h_ref, g_ref, b_ref, ds_ref, dh_ref, dg_ref, db_ref):
        n, nh, rstd = _layer_norm_rows(h_ref[...], g_ref[...], b_ref[...])
        sg = jax.nn.sigmoid(n)
        dn = ds_ref[...] * (sg * (1.0 + n * (1.0 - sg)))
        dnh = dn * g_ref[...]
        m1 = jnp.mean(dnh, axis=-1, keepdims=True)
        m2 = jnp.mean(dnh * nh, axis=-1, keepdims=True)
        dh_ref[...] = rstd * (dnh - m1 - nh * m2)
        _accumulate(dg_ref, jnp.sum(dn * nh, axis=0, keepdims=True))
        _accumulate(db_ref, jnp.sum(dn, axis=0, keepdims=True))

    return pl.pallas_call(
        body, grid=(s // tm,), in_specs=[_rows(tm, d), _fix((1, d)), _fix((1, d)), _rows(tm, d)],
        out_specs=[_rows(tm, d), _fix((1, d)), _fix((1, d))],
        out_shape=[_sds((s, d), F32), _sds((1, d), F32), _sds((1, d), F32)],
        compiler_params=_params(("arbitrary",)), name="conformer_norm_swish_bwd")(hc, g, b, ds)


def _swap_halves(x):
    lane = lax.broadcasted_iota(jnp.int32, x.shape, 1)
    return jnp.where(lane < QK_ROPE // 2, pltpu.roll(x, 128 - QK_ROPE // 2, 1), pltpu.roll(x, QK_ROPE // 2, 1))


def _rope(x, cf, sf):
    return x * cf + _swap_halves(x) * sf


def _unrope(dx, cf, sf):
    return dx * cf - _swap_halves(dx) * sf


def _rms_rows(x, g):
    r = lax.rsqrt(jnp.mean(x * x, axis=-1, keepdims=True) + RMS_EPS)
    return x * r, r


def mla_latents(t, g_q, g_kv, cf, sf, tm=512):
    s = t.shape[0]
    tm = _tile(s, tm)

    def body(t_ref, gq_ref, gkv_ref, cf_ref, sf_ref, cq_ref, ckv_ref, kpe_ref):
        xq, _ = _rms_rows(t_ref[:, 0:Q_LORA], gq_ref[...])
        cq_ref[...] = (xq * gq_ref[...]).astype(BF16)
        xkv, _ = _rms_rows(t_ref[:, Q_LORA:Q_LORA + KV_LORA], gkv_ref[...])
        ckv_ref[...] = (xkv * gkv_ref[...]).astype(BF16)
        kpe_ref[...] = _rope(t_ref[:, Q_LORA + KV_LORA:], cf_ref[...], sf_ref[...]).astype(BF16)

    w = Q_LORA + KV_LORA + 128
    return pl.pallas_call(
        body, grid=(s // tm,),
        in_specs=[_rows(tm, w), _fix((1, Q_LORA)), _fix((1, KV_LORA)), _rows(tm, 128), _rows(tm, 128)],
        out_specs=[_rows(tm, Q_LORA), _rows(tm, KV_LORA), _rows(tm, 128)],
        out_shape=[_sds((s, Q_LORA), BF16), _sds((s, KV_LORA), BF16), _sds((s, 128), BF16)],
        compiler_params=_params(("parallel",)), name="mla_latents")(t, g_q, g_kv, cf, sf)


def mla_latents_bwd(t, g_q, g_kv, cf, sf, dcq, dckv, dkpe, tm=512):
    s = t.shape[0]
    tm = _tile(s, tm)
    w = Q_LORA + KV_LORA + 128

    def rms_bwd(x, g, dy):
        xh, r = _rms_rows(x, g)
        dxh = dy * g
        return r * (dxh - xh * jnp.mean(dxh * xh, axis=-1, keepdims=True)), jnp.sum(dy * xh, axis=0, keepdims=True)

    def body(t_ref, gq_ref, gkv_ref, cf_ref, sf_ref, dcq_ref, dckv_ref, dkpe_ref, dt_ref, dgq_ref, dgkv_ref):
        dxq, dgq = rms_bwd(t_ref[:, 0:Q_LORA], gq_ref[...], dcq_ref[...])
        dxkv, dgkv = rms_bwd(t_ref[:, Q_LORA:Q_LORA + KV_LORA], gkv_ref[...], dckv_ref[...])
        dt_ref[:, 0:Q_LORA] = dxq.astype(BF16)
        dt_ref[:, Q_LORA:Q_LORA + KV_LORA] = dxkv.astype(BF16)
        dt_ref[:, Q_LORA + KV_LORA:] = _unrope(dkpe_ref[...], cf_ref[...], sf_ref[...]).astype(BF16)
        _accumulate(dgq_ref, dgq)
        _accumulate(dgkv_ref, dgkv)

    return pl.pallas_call(
        body, grid=(s // tm,),
        in_specs=[_rows(tm, w), _fix((1, Q_LORA)), _fix((1, KV_LORA)), _rows(tm, 128), _rows(tm, 128),
                  _rows(tm, Q_LORA), _rows(tm, KV_LORA), _rows(tm, 128)],
        out_specs=[_rows(tm, w), _fix((1, Q_LORA)), _fix((1, KV_LORA))],
        out_shape=[_sds((s, w), BF16), _sds((1, Q_LORA), F32), _sds((1, KV_LORA), F32)],
        compiler_params=_params(("arbitrary",)), name="mla_latents_bwd")(t, g_q, g_kv, cf, sf, dcq, dckv, dkpe)


def mla_queries(cq, w_uq, cf, sf, tm=2048):
    s = cq.shape[0]
    tm = _tile(s, tm)

    def epi(acc, e, o):
        o[0][:, 0:QK_NOPE] = acc[:, 0:QK_NOPE].astype(BF16)
        o[0][:, QK_NOPE:] = _rope(acc[:, QK_NOPE:], e[0][...], e[1][...]).astype(BF16)

    return mm_nn("mla_queries", cq, w_uq, tm, HEAD_PAD, Q_LORA, epi, [_sds((s, N_HEADS * HEAD_PAD), BF16)],
                 [_ij(tm, HEAD_PAD)], [cf, sf], [_i0(tm, 128), _i0(tm, 128)])[0]


def mla_keys(ckv, w_uk, kpe, tm=2048):
    s = ckv.shape[0]
    tm = _tile(s, tm)

    def epi(acc, e, o):
        o[0][:, 0:QK_NOPE] = acc.astype(BF16)
        o[0][:, QK_NOPE:] = e[0][...]

    return mm_nn("mla_keys", ckv, w_uk, tm, QK_NOPE, KV_LORA, epi, [_sds((s, N_HEADS * HEAD_PAD), BF16)],
                 [_ij(tm, HEAD_PAD)], [kpe], [_i0(tm, 128)])[0]


def _masked_scores(q, k, qi, tq, kv):
    sc = lax.dot_general(q, k, NT, preferred_element_type=F32) * ATTN_SCALE
    row = lax.broadcasted_iota(jnp.int32, (tq, kv), 0) + qi * tq
    col = lax.broadcasted_iota(jnp.int32, (tq, kv), 1)
    ok = lax.shift_right_logical(col, CHUNK_SHIFT) <= lax.shift_right_logical(row, CHUNK_SHIFT)
    return jnp.where(ok, sc, -1e30)


def attention(q, k, v, tq=512):
    s = q.shape[0]
    tq = _tile(s, tq)
    nq = s // tq

    def body(q_ref, k_ref, v_ref, o_ref):
        for qi in range(nq):
            kv = (qi + 1) * tq
            sc = _masked_scores(q_ref[pl.ds(qi * tq, tq), :], k_ref[pl.ds(0, kv), :], qi, tq, kv)
            p = jnp.exp(sc - jnp.max(sc, axis=-1, keepdims=True))
            o = lax.dot_general(p.astype(BF16), v_ref[pl.ds(0, kv), :], NN, preferred_element_type=F32)
            o_ref[pl.ds(qi * tq, tq), :] = (o / jnp.sum(p, axis=-1, keepdims=True)).astype(BF16)

    hq = pl.BlockSpec((s, HEAD_PAD), lambda h: (0, h))
    hv = pl.BlockSpec((s, V_HEAD), lambda h: (0, h))
    return pl.pallas_call(
        body, grid=(N_HEADS,), in_specs=[hq, hq, hv], out_specs=hv, out_shape=_sds((s, N_HEADS * V_HEAD), BF16),
        compiler_params=_params(("parallel",)), name="attention")(q, k, v)


def attention_bwd(q, k, v, do, tq=512):
    s = q.shape[0]
    tq = _tile(s, tq)
    nq = s // tq

    def body(q_ref, k_ref, v_ref, do_ref, dq_ref, dk_ref, dv_ref, dk_acc, dv_acc):
        dk_acc[...] = jnp.zeros_like(dk_acc)
        dv_acc[...] = jnp.zeros_like(dv_acc)
        for qi in range(nq):
            kv = (qi + 1) * tq
            qt = q_ref[pl.ds(qi * tq, tq), :]
            kt = k_ref[pl.ds(0, kv), :]
            dot = do_ref[pl.ds(qi * tq, tq), :]
            sc = _masked_scores(qt, kt, qi, tq, kv)
            p = jnp.exp(sc - jnp.max(sc, axis=-1, keepdims=True))
            p = p / jnp.sum(p, axis=-1, keepdims=True)
            dp = lax.dot_general(dot, v_ref[pl.ds(0, kv), :], NT, preferred_element_type=F32)
            delta = jnp.sum(p * dp, axis=-1, keepdims=True)
            ds = (p * (dp - delta) * ATTN_SCALE).astype(BF16)
            dq_ref[pl.ds(qi * tq, tq), :] = lax.dot_general(ds, kt, NN, preferred_element_type=F32).astype(BF16)
            dk_acc[pl.ds(0, kv), :] += lax.dot_general(ds, qt, TN, preferred_element_type=F32)
            dv_acc[pl.ds(0, kv), :] += lax.dot_general(p.astype(BF16), dot, TN, preferred_element_type=F32)
        dk_ref[...] = dk_acc[...].astype(BF16)
        dv_ref[...] = dv_acc[...].astype(BF16)

    hq = pl.BlockSpec((s, HEAD_PAD), lambda h: (0, h))
    hv = pl.BlockSpec((s, V_HEAD), lambda h: (0, h))
    return pl.pallas_call(
        body, grid=(N_HEADS,), in_specs=[hq, hq, hv, hv], out_specs=[hq, hq, hv],
        out_shape=[_sds((s, N_HEADS * HEAD_PAD), BF16), _sds((s, N_HEADS * HEAD_PAD), BF16),
                   _sds((s, N_HEADS * V_HEAD), BF16)],
        scratch_shapes=[pltpu.VMEM((s, HEAD_PAD), F32), pltpu.VMEM((s, V_HEAD), F32)],
        compiler_params=_params(("parallel",)), name="attention_bwd")(q, k, v, do)


def mla_unrope_grads(dq, dk, cf, sf, tm=512):
    s = dq.shape[0]
    tm = _tile(s, tm)

    def body(dq_ref, dk_ref, cf_ref, sf_ref, dql_ref, dkn_ref, dkpe_ref):
        cfv, sfv = cf_ref[...], sf_ref[...]
        dkpe = jnp.zeros((tm, 128), F32)
        for h in range(N_HEADS):
            lo = h * HEAD_PAD
            dql_ref[:, lo:lo + QK_NOPE] = dq_ref[:, lo:lo + QK_NOPE]
            dql_ref[:, lo + QK_NOPE:lo + HEAD_PAD] = _unrope(
                dq_ref[:, lo + QK_NOPE:lo + HEAD_PAD].astype(F32), cfv, sfv).astype(BF16)
            dkn_ref[:, h * QK_NOPE:(h + 1) * QK_NOPE] = dk_ref[:, lo:lo + QK_NOPE]
            dkpe = dkpe + dk_ref[:, lo + QK_NOPE:lo + HEAD_PAD].astype(F32)
        dkpe_ref[...] = dkpe

    wq = N_HEADS * HEAD_PAD
    return pl.pallas_call(
        body, grid=(s // tm,), in_specs=[_rows(tm, wq), _rows(tm, wq), _rows(tm, 128), _rows(tm, 128)],
        out_specs=[_rows(tm, wq), _rows(tm, N_HEADS * QK_NOPE), _rows(tm, 128)],
        out_shape=[_sds((s, wq), BF16), _sds((s, N_HEADS * QK_NOPE), BF16), _sds((s, 128), F32)],
        compiler_params=_params(("parallel",)), name="mla_unrope_grads")(dq, dk, cf, sf)


ANY = pl.BlockSpec(memory_space=pl.ANY)
GATHER_ID = 1
CHIP_EXCHANGE_ID = 2
PAIR_ID = 3
ALL_ID = 4


def _nbytes(a):
    return a.size * a.dtype.itemsize


def _copy_cost(operand_bytes, sent_fraction):
    sent = int(operand_bytes * sent_fraction)
    return pl.CostEstimate(flops=0, transcendentals=0, bytes_accessed=2 * sent, remote_bytes_transferred=sent)


def _handshake(peers):
    barrier = pltpu.get_barrier_semaphore()
    for peer in peers:
        pl.semaphore_signal(barrier, inc=1, device_id=peer, device_id_type=MESH)
    pl.semaphore_wait(barrier, len(peers))


def _place():
    x, y, c = lax.axis_index("x"), lax.axis_index("y"), lax.axis_index("c")
    chips = [(1 - x, y), (x, 1 - y), (1 - x, 1 - y)]
    return x, y, c, chips


def _half(ref, hc, axis=0):
    n = ref.shape[axis] // 2
    idx = (slice(None),) * axis + (pl.ds(hc * n, n),)
    return ref.at[idx]


def gather_shards(name, tensors, by_columns=()):
    nt = len(tensors)

    def body(*refs):
        a, g = refs[:nt], refs[nt:2 * nt]
        send, recv = refs[2 * nt:]
        x, y, c, _ = _place()
        q = 2 * x + y
        sib, xn, yn = (x, y, 1 - c), (1 - x, y, c), (x, 1 - y, c)
        q_xn, q_yn, q_diag = 2 * (1 - x) + y, 2 * x + 1 - y, 2 * (1 - x) + 1 - y
        _handshake([sib, xn, yn])

        def whole(t, p):
            if t in by_columns:
                n = a[t].shape[1]
                return g[t].at[:, pl.ds(p * n, n)]
            return g[t].at[p]

        def part(t, p, hc, quarter=None):
            rows = a[t].shape[0]
            if quarter is None:
                return whole(t, p).at[pl.ds(hc * (rows // 2), rows // 2)]
            return whole(t, p).at[pl.ds(hc * (rows // 2) + quarter * (rows // 4), rows // 4)]

        def rc(t, k, src, dst, to):
            return pltpu.make_async_remote_copy(src_ref=src, dst_ref=dst, send_sem=send.at[t, k], recv_sem=recv.at[t, k],
                                                device_id=to, device_id_type=MESH)

        sent = []

        def go(cp):
            cp.start()
            sent.append(cp)

        def landed(t, k, piece, frm):
            rc(t, k, piece, piece, frm).wait_recv()
            return piece

        for t in range(nt):
            go(rc(t, 8, a[t], whole(t, q), sib))
            mine = _half(a[t], c)
            go(rc(t, 0, mine, part(t, q, c), xn))
            go(rc(t, 1, mine, part(t, q, c), yn))
        for t in range(nt):
            from_y = landed(t, 1, part(t, q_yn, c), yn)
            go(rc(t, 2, part(t, q_yn, c, 0), part(t, q_yn, c, 0), xn))
            go(rc(t, 5, from_y, from_y, sib))
            from_x = landed(t, 0, part(t, q_xn, c), xn)
            go(rc(t, 3, part(t, q_xn, c, 1), part(t, q_xn, c, 1), yn))
            go(rc(t, 4, from_x, from_x, sib))
        for t in range(nt):
            for k, frm in ((2, xn), (3, yn)):
                piece = landed(t, k, part(t, q_diag, c, k - 2), frm)
                go(rc(t, 4 + k, piece, piece, sib))
        for t in range(nt):
            landed(t, 4, part(t, q_xn, 1 - c), sib)
            landed(t, 5, part(t, q_yn, 1 - c), sib)
            landed(t, 6, part(t, q_diag, 1 - c, 0), sib)
            landed(t, 7, part(t, q_diag, 1 - c, 1), sib)
            landed(t, 8, whole(t, q), sib)
        for cp in sent:
            cp.wait_send()

    return pl.kernel(
        body, name=name,
        out_type=[_sds((a.shape[0], N_CHIPS * a.shape[1]) if t in by_columns else (N_CHIPS,) + a.shape, a.dtype)
                  for t, a in enumerate(tensors)],
        mesh=plsc.ScalarSubcoreMesh(axis_name="sequencer", num_cores=1),
        scratch_types=[pltpu.SemaphoreType.DMA((nt, 9)), pltpu.SemaphoreType.DMA((nt, 9))],
        cost_estimate=_copy_cost(sum(_nbytes(a) for a in tensors), 4),
        compiler_params=pltpu.CompilerParams(collective_id=GATHER_ID))(*tensors)


def pair_exchange(name, grads, on_sequencer):
    nt = len(grads)

    def body(*refs):
        g, theirs = refs[:nt], refs[nt:2 * nt]
        send, recv = refs[2 * nt:]
        x, y, c, _ = _place()
        if on_sequencer:
            _handshake([(x, y, 1 - c)])
        cps = []
        for t in range(nt):
            cp = pltpu.make_async_remote_copy(src_ref=_half(g[t], 1 - c, 1), dst_ref=theirs[t], send_sem=send.at[t],
                                              recv_sem=recv.at[t], device_id=(x, y, 1 - c), device_id_type=MESH)
            cp.start()
            cps.append(cp)
        for cp in cps:
            cp.wait()

    if not on_sequencer:
        return pl.pallas_call(
            body, in_specs=[ANY] * nt, out_specs=[ANY] * nt,
            out_shape=[_sds((N_CHIPS, a.shape[1] // 2, a.shape[2]), a.dtype) for a in grads],
            scratch_shapes=[pltpu.SemaphoreType.DMA((nt,)), pltpu.SemaphoreType.DMA((nt,))],
            name=name)(*grads)
    return pl.kernel(
        body, name=name, out_type=[_sds((N_CHIPS, a.shape[1] // 2, a.shape[2]), a.dtype) for a in grads],
        mesh=plsc.ScalarSubcoreMesh(axis_name="sequencer", num_cores=1),
        scratch_types=[pltpu.SemaphoreType.DMA((nt,)), pltpu.SemaphoreType.DMA((nt,))],
        cost_estimate=_copy_cost(sum(_nbytes(a) for a in grads), 0.5),
        compiler_params=pltpu.CompilerParams(collective_id=PAIR_ID))(*grads)


def chip_exchange(name, parts):
    nt = len(parts)

    def body(*refs):
        a, r = refs[:nt], refs[nt:2 * nt]
        send, recv = refs[2 * nt:]
        x, y, c, chips = _place()
        _handshake([(*chip, c) for chip in chips])
        cps = []
        for t in range(nt):
            for j, chip in enumerate(chips):
                cp = pltpu.make_async_remote_copy(
                    src_ref=a[t].at[2 * chip[0] + chip[1]], dst_ref=r[t].at[j], send_sem=send.at[t, j],
                    recv_sem=recv.at[t, j], device_id=(*chip, c), device_id_type=MESH)
                cp.start()
                cps.append(cp)
        for cp in cps:
            cp.wait()

    return pl.kernel(
        body, name=name, out_type=[_sds((N_CHIPS - 1,) + a.shape[1:], a.dtype) for a in parts],
        mesh=plsc.ScalarSubcoreMesh(axis_name="sequencer", num_cores=1),
        scratch_types=[pltpu.SemaphoreType.DMA((nt, 3)), pltpu.SemaphoreType.DMA((nt, 3))],
        cost_estimate=_copy_cost(sum(_nbytes(a) for a in parts), 0.75),
        compiler_params=pltpu.CompilerParams(collective_id=CHIP_EXCHANGE_ID))(*parts)


def pair_share(name, halves):
    nt = len(halves)

    def body(*refs):
        h, other = refs[:nt], refs[nt:2 * nt]
        send, recv = refs[2 * nt:]
        x, y, c, _ = _place()
        _handshake([(x, y, 1 - c)])
        cps = []
        for t in range(nt):
            cp = pltpu.make_async_remote_copy(src_ref=h[t], dst_ref=other[t], send_sem=send.at[t], recv_sem=recv.at[t],
                                              device_id=(x, y, 1 - c), device_id_type=MESH)
            cp.start()
            cps.append(cp)
        for cp in cps:
            cp.wait()

    return pl.kernel(
        body, name=name, out_type=[_sds(a.shape, a.dtype) for a in halves],
        mesh=plsc.ScalarSubcoreMesh(axis_name="sequencer", num_cores=1),
        scratch_types=[pltpu.SemaphoreType.DMA((nt,)), pltpu.SemaphoreType.DMA((nt,))],
        cost_estimate=_copy_cost(sum(_nbytes(a) for a in halves), 1),
        compiler_params=pltpu.CompilerParams(collective_id=PAIR_ID))(*halves)


def pack_rows(name, parts, rows):
    cdim = parts[0].shape[1]
    n = len(parts)
    vm = pl.BlockSpec(memory_space=pltpu.VMEM)

    def pack(*refs):
        p, o_ref = refs[:n], refs[n]
        at = 0
        for ref in p:
            o_ref[pl.ds(at, ref.shape[0]), :] = ref[...]
            at += ref.shape[0]
        o_ref[pl.ds(at, rows - at), :] = jnp.zeros((rows - at, cdim), F32)

    return pl.pallas_call(pack, in_specs=[vm] * n, out_specs=vm, out_shape=_sds((rows, cdim), F32), name=name)(*parts)


def all_reduce_small(parts, rows):
    cdim = parts[0].shape[1]
    vm = pl.BlockSpec(memory_space=pltpu.VMEM)
    mine = pack_rows("small_pack", parts, rows)

    def exchange(mine_ref, buf, send, recv, lsem):
        x, y, c, _ = _place()
        me = 4 * x + 2 * y + c
        peers = [(x ^ (k >> 2), y ^ ((k >> 1) & 1), c ^ (k & 1)) for k in range(1, 8)]
        _handshake(peers)
        own = pltpu.make_async_copy(mine_ref, buf.at[me], lsem)
        own.start()
        cps = []
        for k, to in enumerate(peers):
            cp = pltpu.make_async_remote_copy(src_ref=mine_ref, dst_ref=buf.at[me], send_sem=send.at[k], recv_sem=recv.at[k],
                                              device_id=to, device_id_type=MESH)
            cp.start()
            cps.append(cp)
        for k, (px, py, pc) in enumerate(peers):
            pltpu.make_async_remote_copy(src_ref=mine_ref, dst_ref=buf.at[4 * px + 2 * py + pc], send_sem=send.at[k],
                                         recv_sem=recv.at[k], device_id=(x, y, c), device_id_type=MESH).wait_recv()
        for cp in cps:
            cp.wait_send()
        own.wait()

    landed = pl.kernel(
        exchange, name="small_exchange", out_type=_sds((8, rows, cdim), F32),
        mesh=plsc.ScalarSubcoreMesh(axis_name="sequencer", num_cores=1),
        scratch_types=[pltpu.SemaphoreType.DMA((7,)), pltpu.SemaphoreType.DMA((7,)), pltpu.SemaphoreType.DMA],
        cost_estimate=_copy_cost(rows * cdim * 4, 7),
        compiler_params=pltpu.CompilerParams(collective_id=ALL_ID))(mine)

    def total(buf, o_ref):
        acc = buf[0]
        for d in range(1, 8):
            acc = acc + buf[d]
        o_ref[...] = acc

    return pl.pallas_call(total, in_specs=[vm], out_specs=vm, out_shape=_sds((rows, cdim), F32), name="small_sum")(landed)


def pair_sum(g, theirs, core, tm=256):
    _, r, c = g.shape
    tm = _tile(r // 2, tm)
    nh = r // 2 // tm

    def body(core_ref, a_ref, b_ref, o_ref):
        o_ref[...] = (a_ref[...].astype(F32) + b_ref[...].astype(F32)).astype(BF16)

    blk = (N_CHIPS, tm, c)
    return pl.pallas_call(
        body, grid_spec=pltpu.PrefetchScalarGridSpec(
            num_scalar_prefetch=1, grid=(nh,),
            in_specs=[pl.BlockSpec(blk, lambda i, cr: (0, cr[0] * nh + i, 0)), pl.BlockSpec(blk, lambda i, cr: (0, i, 0))],
            out_specs=pl.BlockSpec(blk, lambda i, cr: (0, i, 0))),
        out_shape=_sds(theirs.shape, BF16), compiler_params=_params(("parallel",)), name="pair_sum")(core, g, theirs)


def chip_sum(own, landed, chip, stack, layer, layers, tm=256):
    _, r, c = own.shape
    tm = _tile(r, tm)

    def body(chip_ref, own_ref, l_ref, *rest):
        acc = own_ref[...].astype(F32)
        for j in range(N_CHIPS - 1):
            acc = acc + l_ref[j].astype(F32)
        rest[-1][...] = acc

    in_specs = [pl.BlockSpec((None, tm, c), lambda i, qr: (qr[0], i, 0)),
                pl.BlockSpec((N_CHIPS - 1, tm, c), lambda i, qr: (0, i, 0))]
    args = [chip, own, landed]
    if stack is not None:
        in_specs.append(ANY)
        args.append(stack)
    return pl.pallas_call(
        body, grid_spec=pltpu.PrefetchScalarGridSpec(
            num_scalar_prefetch=1, grid=(r // tm,), in_specs=in_specs,
            out_specs=pl.BlockSpec((None, tm, c), lambda i, qr: (layer, i, 0))),
        out_shape=_sds((layers, r, c), F32), input_output_aliases={3: 0} if stack is not None else {},
        compiler_params=_params(("parallel",)), name="chip_sum")(*args)


def _adamw_math(w, g, m, v):
    bc1 = 1.0 - ADAM_B1 ** ADAM_STEP
    bc2 = 1.0 - ADAM_B2 ** ADAM_STEP
    nm = ADAM_B1 * m + (1.0 - ADAM_B1) * g
    nv = ADAM_B2 * v + (1.0 - ADAM_B2) * (g * g)
    return -ADAM_LR * ((nm / bc1) / (jnp.sqrt(nv / bc2) + ADAM_EPS) + ADAM_WD * w), nm, nv


def vector_update(red, chip, ws, ms, vs, where):
    n = len(ws)
    dd = red.shape[1]

    def body(chip_ref, red_ref, *refs):
        w_r, m_r, v_r = refs[0:n], refs[n:2 * n], refs[2 * n:3 * n]
        g_o, d_o, m_o, v_o = (refs[(3 + k) * n:(4 + k) * n] for k in range(4))
        q = chip_ref[0]

        def chip_block(val, width):
            out = val[:, 0:width]
            for p in range(1, val.shape[1] // width):
                out = jnp.where(q == p, val[:, p * width:(p + 1) * width], out)
            return out

        for k in range(n):
            for idx, r0, nr, cols in where[k]:
                width = w_r[k].shape[-1]
                if cols == "chip" and width * N_CHIPS != dd:
                    g = chip_block(jnp.concatenate([red_ref[pl.ds(r0 + j, 1), :] for j in range(nr)], axis=1), width)
                else:
                    g = red_ref[pl.ds(r0, nr), :]
                    g = chip_block(g, width) if cols == "chip" else g if cols == "all" else g[:, 0:cols]
                delta, nm, nv = _adamw_math(w_r[k][idx], g, m_r[k][idx], v_r[k][idx])
                g_o[k][idx] = g
                d_o[k][idx] = delta
                m_o[k][idx] = nm
                v_o[k][idx] = nv

    vm = pl.BlockSpec(memory_space=pltpu.VMEM)
    outs = pl.pallas_call(
        body, in_specs=[pl.BlockSpec(memory_space=pltpu.SMEM), vm] + [vm] * (3 * n), out_specs=[vm] * (4 * n),
        out_shape=[_sds(w.shape, F32) for w in ws] * 4, name="vector_update")(chip, red, *ws, *ms, *vs)
    return [outs[k * n:(k + 1) * n] for k in range(4)]


def adamw_joined(w, m, v, g_mine, g_theirs, core, tm=512):
    nl, r, c = w.shape
    tm = _tile(r // 2, tm)
    nh = r // 2 // tm

    def body(core_ref, w_ref, m_ref, v_ref, gm_ref, gt_ref, g_ref, d_ref, nm_ref, nv_ref):
        mine = (pl.program_id(1) // nh) == core_ref[0]
        gv = jnp.where(mine, gm_ref[...], gt_ref[...])
        g_ref[...] = gv
        d_ref[...], nm_ref[...], nv_ref[...] = _adamw_math(w_ref[...], gv, m_ref[...], v_ref[...])

    full = pl.BlockSpec((None, tm, c), lambda l, i, cr: (l, i, 0))
    half = pl.BlockSpec((None, tm, c), lambda l, i, cr: (l, i % nh, 0))
    return pl.pallas_call(
        body, grid_spec=pltpu.PrefetchScalarGridSpec(
            num_scalar_prefetch=1, grid=(nl, r // tm), in_specs=[full, full, full, half, half], out_specs=[full] * 4),
        out_shape=[_sds((nl, r, c), F32)] * 4, compiler_params=_params(("parallel", "parallel")),
        name="adamw_joined")(core, w, m, v, g_mine, g_theirs)


WEIGHTS = ['sc_w_in', 'sc_conv_w', 'sc_w_out', 'mla_w_dq', 'mla_g_q', 'mla_w_uq', 'mla_w_dkv', 'mla_g_kv', 'mla_w_uk',
           'mla_w_uv', 'mla_w_o', 'cf_w_pw1', 'cf_b_pw1', 'cf_dw_w', 'cf_dw_b', 'cf_norm_g', 'cf_norm_b', 'cf_w_pw2',
           'cf_b_pw2', 'ff_w1', 'ff_w2', 'ln_mix_g', 'ln_mix_b', 'ln_ff_g', 'ln_ff_b']
ARGS = ['x'] + WEIGHTS + ['loss_target'] + ['m_' + n for n in WEIGHTS] + ['v_' + n for n in WEIGHTS]


def _sq_relu(h):
    r = jnp.maximum(h, jnp.zeros_like(h))
    return r * r


def _mlp_forward(i, x, xb, w1, w2, g, b):
    hb = mm_plain_nn(f"mlp{i}_up", xb, w1, BF16, tn=1024)
    y, yb, xh, rstd = mm_residual_ln(f"mlp{i}_down_ln", hb, w2, x, g, b, tk=4096, a_fn=_sq_relu)
    return (y, yb), dict(xb=xb, hb=hb, xh=xh, rstd=rstd, g=g)


def _mlp_backward(i, dr, drb, sv, w1, w2, dw1, dw2, reduce_after, mixer_ln):
    s = dr.shape[0]
    tm, tn = _tile(s, 1024), 1024

    def epi(acc, e, o):
        o[0][...] = (acc * (2.0 * jnp.maximum(e[0][...].astype(F32), 0.0))).astype(BF16)

    dhb = mm_nt(f"mlp{i}_down_bwd", drb, w2, s, tm, tn, 1024, epi, [_sds((s, w2.k), BF16)], [_ij(tm, tn)],
                [sv["hb"]], [_ij(tm, tn)])[0]
    g_w2 = mm_tn(f"mlp{i}_dw2", sv["hb"], drb, dw2, s, 512, 1024, a_fn=_sq_relu)
    g_w1 = mm_tn(f"mlp{i}_dw1", sv["xb"], dhb, dw1, s, 1024, 512)
    dhb = reduce_after(dhb, {f"w1_{i}": g_w1, f"w2_{i}": g_w2})
    return mm_nt_ln_backward(f"mlp{i}_up_bwd", dhb, w1, dr, *mixer_ln, tk=2048)


def kernel(x, sc_w_in, sc_conv_w, sc_w_out, mla_w_dq, mla_g_q, mla_w_uq, mla_w_dkv, mla_g_kv, mla_w_uk, mla_w_uv, mla_w_o, cf_w_pw1, cf_b_pw1, cf_dw_w, cf_dw_b, cf_norm_g, cf_norm_b, cf_w_pw2, cf_b_pw2, ff_w1, ff_w2, ln_mix_g, ln_mix_b, ln_ff_g, ln_ff_b, loss_target, m_sc_w_in, m_sc_conv_w, m_sc_w_out, m_mla_w_dq, m_mla_g_q, m_mla_w_uq, m_mla_w_dkv, m_mla_g_kv, m_mla_w_uk, m_mla_w_uv, m_mla_w_o, m_cf_w_pw1, m_cf_b_pw1, m_cf_dw_w, m_cf_dw_b, m_cf_norm_g, m_cf_norm_b, m_cf_w_pw2, m_cf_b_pw2, m_ff_w1, m_ff_w2, m_ln_mix_g, m_ln_mix_b, m_ln_ff_g, m_ln_ff_b, v_sc_w_in, v_sc_conv_w, v_sc_w_out, v_mla_w_dq, v_mla_g_q, v_mla_w_uq, v_mla_w_dkv, v_mla_g_kv, v_mla_w_uk, v_mla_w_uv, v_mla_w_o, v_cf_w_pw1, v_cf_b_pw1, v_cf_dw_w, v_cf_dw_b, v_cf_norm_g, v_cf_norm_b, v_cf_w_pw2, v_cf_b_pw2, v_ff_w1, v_ff_w2, v_ln_mix_g, v_ln_mix_b, v_ln_ff_g, v_ln_ff_b):
    given = dict(zip(ARGS, (x, sc_w_in, sc_conv_w, sc_w_out, mla_w_dq, mla_g_q, mla_w_uq, mla_w_dkv, mla_g_kv, mla_w_uk, mla_w_uv, mla_w_o, cf_w_pw1, cf_b_pw1, cf_dw_w, cf_dw_b, cf_norm_g, cf_norm_b, cf_w_pw2, cf_b_pw2, ff_w1, ff_w2, ln_mix_g, ln_mix_b, ln_ff_g, ln_ff_b, loss_target, m_sc_w_in, m_sc_conv_w, m_sc_w_out, m_mla_w_dq, m_mla_g_q, m_mla_w_uq, m_mla_w_dkv, m_mla_g_kv, m_mla_w_uk, m_mla_w_uv, m_mla_w_o, m_cf_w_pw1, m_cf_b_pw1, m_cf_dw_w, m_cf_dw_b, m_cf_norm_g, m_cf_norm_b, m_cf_w_pw2, m_cf_b_pw2, m_ff_w1, m_ff_w2, m_ln_mix_g, m_ln_mix_b, m_ln_ff_g, m_ln_ff_b, v_sc_w_in, v_sc_conv_w, v_sc_w_out, v_mla_w_dq, v_mla_g_q, v_mla_w_uq, v_mla_w_dkv, v_mla_g_kv, v_mla_w_uk, v_mla_w_uv, v_mla_w_o, v_cf_w_pw1, v_cf_b_pw1, v_cf_dw_w, v_cf_dw_b, v_cf_norm_g, v_cf_norm_b, v_cf_w_pw2, v_cf_b_pw2, v_ff_w1, v_ff_w2, v_ln_mix_g, v_ln_mix_b, v_ln_ff_g, v_ln_ff_b)))
    s, d = x.shape[1], x.shape[2]
    d_ff = 4 * d
    dq4 = d // N_CHIPS
    xq = lax.axis_index("x") * 2 + lax.axis_index("y")

    w_dkv_pad = jnp.pad(mla_w_dkv[0], ((0, 0), (0, 128 - QK_ROPE)))
    w_uq_pad = jnp.pad(mla_w_uq[0].reshape(Q_LORA, 2, QK_NOPE + QK_ROPE), ((0, 0), (0, 0), (0, HEAD_PAD - QK_NOPE - QK_ROPE)))
    small = pack_rows("vector_weights_pack", [
        sc_conv_w.reshape(2 * SC_WIDTH, dq4), cf_b_pw1.reshape(2, dq4), cf_dw_w[0], cf_dw_b, cf_norm_g, cf_norm_b,
        cf_b_pw2], 64)
    mlp_w = lambda i: [ff_w1[i].astype(BF16), ff_w2[i].astype(BF16)]
    g_in, g_out, g_w1, g_w2 = [None] * 2, [None] * 2, [None] * DEPTH, [None] * DEPTH
    g_in[0], g_out[0], g_small = gather_shards(
        "gather_mixer0", [sc_w_in[0].astype(BF16), sc_w_out[0].astype(BF16), small], by_columns=(0,))
    (g_w1[0],) = gather_shards("gather_up0", [ff_w1[0].astype(BF16)], by_columns=(0,))
    (g_w2[0],) = gather_shards("gather_down0", [ff_w2[0].astype(BF16)])
    g_dqkv, g_uq, g_uk, g_uv, g_o = gather_shards("gather_mixer1", [
        jnp.concatenate([mla_w_dq[0], w_dkv_pad], axis=1).astype(BF16),
        w_uq_pad.reshape(Q_LORA, 2 * HEAD_PAD).astype(BF16),
        mla_w_uk.reshape(KV_LORA // N_CHIPS, N_HEADS * QK_NOPE).astype(BF16),
        mla_w_uv.reshape(KV_LORA // N_CHIPS, N_HEADS * V_HEAD).astype(BF16), mla_w_o[0].astype(BF16)], by_columns=(1,))
    g_w1[1], g_w2[1] = gather_shards("gather_mlp1", mlp_w(1), by_columns=(0,))
    g_pw1, g_pw2, g_w1[2], g_w2[2] = gather_shards(
        "gather_layer2", [cf_w_pw1[0].astype(BF16), cf_w_pw2[0].astype(BF16)] + mlp_w(2), by_columns=(0, 2))
    g_in[1], g_out[1], g_w1[3], g_w2[3] = gather_shards(
        "gather_layer3", [sc_w_in[1].astype(BF16), sc_w_out[1].astype(BF16)] + mlp_w(3), by_columns=(0, 2))

    wd_t = Q_LORA + KV_LORA + 128
    w_in = [Stk("full", d, 3 * d, g_in[j]) for j in range(2)]
    w_out = [Stk("row", d, d, g_out[j]) for j in range(2)]
    w_dqkv = Stk("row", d, wd_t, g_dqkv)
    w_uq = Stk("full", Q_LORA, N_HEADS * HEAD_PAD, g_uq)
    w_uk = Stk("row", KV_LORA, N_HEADS * QK_NOPE, g_uk)
    w_uv = Stk("row", KV_LORA, N_HEADS * V_HEAD, g_uv)
    w_o = Stk("row", d, d, g_o)
    w_pw1 = Stk("full", d, 2 * d, g_pw1)
    w_pw2 = Stk("row", d, d, g_pw2)
    w_1 = [Stk("full", d, d_ff, g_w1[i]) for i in range(DEPTH)]
    w_2 = [Stk("row", d_ff, d, g_w2[i]) for i in range(DEPTH)]

    def wide(rows):
        return jnp.swapaxes(rows, 0, 1).reshape(rows.shape[1], d)

    conv_w = wide(g_small[:, 0:6]).reshape(2, SC_WIDTH, d)
    b_pw1 = g_small[:, 6:8].reshape(1, 2 * d)
    dw_w = wide(g_small[:, 8:39])
    dw_b, norm_g, norm_b, b_pw2 = (wide(g_small[:, 39 + k:40 + k]) for k in range(4))

    pos = jnp.arange(s, dtype=F32)
    inv_freq = ROPE_THETA ** (-jnp.arange(0, QK_ROPE, 2, dtype=F32) / QK_ROPE)
    ang = pos[:, None] * inv_freq[None, :]
    cos, sin, zero = jnp.cos(ang), jnp.sin(ang), jnp.zeros((s, 128 - QK_ROPE), F32)
    cf = jnp.concatenate([cos, cos, zero], axis=1)
    sf = jnp.concatenate([-sin, sin, zero], axis=1)

    def row(a, i):
        return a[i:i + 1]

    xs = x.reshape(s, d)
    cur = (xs, xs.astype(BF16))
    tape = []
    for i in range(DEPTH):
        mixer, j = i % 3, i // 3
        xf, xb = cur
        lg, lb = row(ln_mix_g, i), row(ln_mix_b, i)
        if mixer == 0:
            u = mm_plain_nn(f"sc{j}_in", xb, w_in[j], F32, tn=3 * dq4)
            gb = short_conv_gate(u, conv_w[j])
            y, yb, xh, rstd = mm_residual_ln(f"sc{j}_out_ln", gb, w_out[j], xf, lg, lb)
            sv = dict(xb=xb, u=u, gb=gb)
        elif mixer == 1:
            t = mm_plain_nn("mla_down", xb, w_dqkv, F32, tn=wd_t // 2)
            cq, ckv, kpe = mla_latents(t, mla_g_q, mla_g_kv, cf, sf)
            qh = mla_queries(cq, w_uq, cf, sf)
            kh = mla_keys(ckv, w_uk, kpe)
            vh = mm_plain_nn("mla_values", ckv, w_uv, BF16, tk=KV_LORA)
            oh = attention(qh, kh, vh)
            y, yb, xh, rstd = mm_residual_ln("mla_out_ln", oh, w_o, xf, lg, lb)
            sv = dict(xb=xb, t=t, cq=cq, ckv=ckv, qh=qh, kh=kh, vh=vh, oh=oh)
        else:
            u = mm_plain_nn("cf_pw1", xb, w_pw1, F32, bias=b_pw1)
            hc = conformer_glu_conv(u, dw_w, dw_b)
            sb = conformer_norm_swish(hc, norm_g, norm_b)
            y, yb, xh, rstd = mm_residual_ln("cf_pw2_ln", sb, w_pw2, xf, lg, lb, bias=b_pw2)
            sv = dict(xb=xb, u=u, hc=hc, sb=sb)
        sv.update(xh=xh, rstd=rstd, g=lg)
        cur, sv_mlp = _mlp_forward(i, y, yb, w_1[i], w_2[i], row(ln_ff_g, i), row(ln_ff_b, i))
        tape.append((sv, sv_mlp))

    g_ln = {n: [None] * DEPTH for n in ("ln_mix_g", "ln_mix_b", "ln_ff_g", "ln_ff_b")}
    last = tape[DEPTH - 1][1]
    dr, drb, g_ln["ln_ff_g"][DEPTH - 1], g_ln["ln_ff_b"][DEPTH - 1], _, loss_part = loss_ln_backward(
        cur[0], loss_target.reshape(s, d), last["xh"], last["rstd"], last["g"])

    grads = {}
    smalls = {}
    conv_grads = [None, None]
    core = lax.axis_index("c").astype(jnp.int32).reshape(1)
    chip = xq.astype(jnp.int32).reshape(1)
    pairs, landed = {}, {}
    ready, theirs = [], {}

    def reduce_after(x, new, early=False):
        out = lax.optimization_barrier((x, *new.values()))
        grads.update(zip(new, out[1:]))
        if early:
            theirs.update(zip(new, pair_exchange(f"pair_exchange_{len(theirs)}", list(out[1:]), True)))
        ready.extend(new)
        return out[0]

    def reduce_layer(i, x):
        late = [n for n in ready if n not in theirs]
        if late:
            theirs.update(zip(late, pair_exchange(f"pair_exchange_layer{i}", [grads[n] for n in late], False)))
        sums = [pair_sum(grads[n], theirs[n], core) for n in ready]
        pairs.update(zip(ready, sums))
        landed.update(zip(ready, chip_exchange(f"chip_exchange_layer{i}", sums)))
        exchanged.append(list(ready))
        ready.clear()
        return lax.optimization_barrier((x, *sums))[0]

    groups = [["in_0", "in_1"], ["out_0", "out_1"], ["dqkv"], ["uq"], ["uk"], ["uv"], ["o"], ["pw1"], ["pw2"],
              [f"w1_{i}" for i in range(DEPTH)], [f"w2_{i}" for i in range(DEPTH)]]
    stacks = [None] * len(groups)
    exchanged = []

    def sum_layer(x, last=False):
        names = exchanged.pop(0)
        if last:
            out = lax.optimization_barrier((x, *[landed[n] for n in names]))
            landed.update(zip(names, out[1:]))
        new = []
        for n in names:
            k = next(k for k, members in enumerate(groups) if n in members)
            stacks[k] = chip_sum(pairs[n], landed[n], chip, stacks[k], groups[k].index(n), len(groups[k]))
            new.append(stacks[k])
        return out[0] if last else lax.optimization_barrier((x, *new))[0]

    for i in reversed(range(DEPTH)):
        mixer, j = i % 3, i // 3
        sv, sv_mlp = tape[i]
        dr, drb, g_ln["ln_mix_g"][i], g_ln["ln_mix_b"][i], dr_sum = _mlp_backward(
            i, dr, drb, sv_mlp, w_1[i], w_2[i], Stk("col", d, d_ff), Stk("row", d_ff, d),
            lambda x_, new: reduce_after(x_, new, early=i > 0), (sv["xh"], sv["rstd"], sv["g"]))
        if i == 0:
            dr = reduce_layer("0_mlp", dr)

        def to_input(name, a, w, tk, a_spec_fn=None):
            if i == 0:
                spec = None if a_spec_fn is None else (s, a_spec_fn)
                return mm_plain_nt(name, a, w, F32, tn=1024, tk=tk, add=dr, add_scale=ALPHA, a_spec_fn=spec), None
            prev = tape[i - 1][1]
            out = mm_nt_ln_backward(name, a, w, dr, prev["xh"], prev["rstd"], prev["g"], tk=tk, a_spec_fn=a_spec_fn)
            g_ln["ln_ff_g"][i - 1], g_ln["ln_ff_b"][i - 1] = out[2], out[3]
            return out[0], out[1]

        parts_of = lambda tm, tk: pl.BlockSpec((None, tm, tk), lambda i_, j_, k_: (k_, i_, 0))
        if mixer == 0:
            dgate = mm_plain_nt(f"sc{j}_out_bwd", drb, w_out[j], F32)
            dw_out = mm_tn(f"sc{j}_dw_out", sv["gb"], drb, Stk("row", d, d), s, 512, 1024)
            du, conv_grads[j] = short_conv_gate_bwd(sv["u"], conv_w[j], dgate)
            nb = d // 256
            dw_in = mm_tn(
                f"sc{j}_dw_in", sv["xb"], du, Stk("col", d, 3 * d), s, 1024, 256,
                b_spec=pl.BlockSpec((None, s, 256), lambda i_, j_, k_: (j_ // nb, k_, j_ % nb)))
            du = reduce_after(du, {f"in_{j}": dw_in, f"out_{j}": dw_out})
            dr, drb = to_input(f"sc{j}_in_bwd", du, w_in[j], d, parts_of)
        elif mixer == 1:
            do = mm_plain_nt("mla_out_bwd", drb, w_o, BF16)
            g_o = mm_tn("mla_dw_o", sv["oh"], drb, Stk("row", d, d), s, 512, 1024)
            dqh, dkh, dvh = attention_bwd(sv["qh"], sv["kh"], sv["vh"], do)
            dql, dkn, dkpe = mla_unrope_grads(dqh, dkh, cf, sf)
            g_uq = mm_tn("mla_dw_uq", sv["cq"], dql, Stk("col", Q_LORA, N_HEADS * HEAD_PAD), s, Q_LORA, 512)
            dcq = mm_plain_nt("mla_uq_bwd", dql, w_uq, F32, tn=Q_LORA)
            g_uk = mm_tn("mla_dw_uk", sv["ckv"], dkn, Stk("row", KV_LORA, N_HEADS * QK_NOPE), s, KV_LORA, 1024)
            g_uv = mm_tn("mla_dw_uv", sv["ckv"], dvh, Stk("row", KV_LORA, N_HEADS * V_HEAD), s, KV_LORA, 1024)
            dckv = mm_plain_nt("mla_uk_bwd", dkn, w_uk, F32, tn=KV_LORA)
            dckv = mm_plain_nt("mla_uv_bwd", dvh, w_uv, F32, tn=KV_LORA, add=dckv)
            dt, smalls["g_q"], smalls["g_kv"] = mla_latents_bwd(sv["t"], mla_g_q, mla_g_kv, cf, sf, dcq, dckv, dkpe)
            g_dqkv = mm_tn("mla_dw_down", sv["xb"], dt, Stk("row", d, wd_t), s, 512, wd_t)
            dt = reduce_after(dt, {"dqkv": g_dqkv, "uq": g_uq, "uk": g_uk, "uv": g_uv, "o": g_o})
            dr, drb = to_input("mla_down_bwd", dt, w_dqkv, wd_t)
        else:
            dsw = mm_plain_nt("cf_pw2_bwd", drb, w_pw2, F32)
            g_pw2 = mm_tn("cf_dw_pw2", sv["sb"], drb, Stk("row", d, d), s, 512, 1024)
            smalls["b_pw2"] = dr_sum
            dhc, smalls["norm_g"], smalls["norm_b"] = conformer_norm_swish_bwd(sv["hc"], norm_g, norm_b, dsw)
            du, smalls["b_pw1"], smalls["dw_w"], smalls["dw_b"] = conformer_glu_conv_bwd(sv["u"], dw_w, dhc)
            nb = d // 512
            g_pw1 = mm_tn(
                "cf_dw_pw1", sv["xb"], du, Stk("col", d, 2 * d), s, 1024, 512,
                b_spec=pl.BlockSpec((None, s, 512), lambda i_, j_, k_: (j_ // nb, k_, j_ % nb)))
            du = reduce_after(du, {"pw1": g_pw1, "pw2": g_pw2})
            dr, drb = to_input("cf_pw1_bwd", du, w_pw1, d, parts_of)
        if i < DEPTH - 1:
            dr = sum_layer(dr)
        dr = reduce_layer(i, dr)
    grad_x = sum_layer(sum_layer(dr, last=True), last=True).reshape(1, s, d)

    mine = stacks
    other = (pair_share("pair_share_mixers", mine[:9]) + pair_share("pair_share_up", mine[9:10])
             + pair_share("pair_share_down", mine[10:]))

    def padded(get):
        dqkv = jnp.concatenate([get("mla_w_dq")[0], jnp.pad(get("mla_w_dkv")[0], ((0, 0), (0, 128 - QK_ROPE)))], axis=1)
        uq = jnp.pad(get("mla_w_uq")[0].reshape(Q_LORA, 2, QK_NOPE + QK_ROPE),
                     ((0, 0), (0, 0), (0, HEAD_PAD - QK_NOPE - QK_ROPE))).reshape(Q_LORA, 2 * HEAD_PAD)
        return [get("sc_w_in"), get("sc_w_out"), dqkv[None], uq[None],
                get("mla_w_uk").reshape(1, KV_LORA // N_CHIPS, d), get("mla_w_uv").reshape(1, KV_LORA // N_CHIPS, d),
                get("mla_w_o"), get("cf_w_pw1"), get("cf_w_pw2"), get("ff_w1"), get("ff_w2")]

    w_l, m_l, v_l = (padded(lambda n, p=p: given[p + n]) for p in ("", "m_", "v_"))
    res = [adamw_joined(w_l[k], m_l[k], v_l[k], mine[k], other[k], core) for k in range(len(groups))]

    def unpadded(k):
        r_in, r_out, r_dqkv, r_uq, r_uk, r_uv, r_o, r_pw1, r_pw2, r_w1, r_w2 = (r[k] for r in res)
        return {
            "sc_w_in": r_in, "sc_w_out": r_out, "mla_w_dq": r_dqkv[:, :, 0:Q_LORA],
            "mla_w_dkv": r_dqkv[:, :, Q_LORA:Q_LORA + KV_LORA + QK_ROPE],
            "mla_w_uq": r_uq.reshape(1, Q_LORA, 2, HEAD_PAD)[:, :, :, 0:QK_NOPE + QK_ROPE].reshape(mla_w_uq.shape),
            "mla_w_uk": r_uk.reshape(mla_w_uk.shape), "mla_w_uv": r_uv.reshape(mla_w_uv.shape),
            "mla_w_o": r_o, "cf_w_pw1": r_pw1, "cf_w_pw2": r_pw2, "ff_w1": r_w1, "ff_w2": r_w2}

    big_g, big_d, big_m, big_v = (unpadded(k) for k in range(4))

    pad_row = lambda a: jnp.pad(a, ((0, 0), (0, d - a.shape[1])))
    small_parts = ([g for n in ("ln_mix_g", "ln_mix_b", "ln_ff_g", "ln_ff_b") for g in g_ln[n]]
                   + [pad_row(smalls["g_q"]), pad_row(smalls["g_kv"]), conv_grads[0], conv_grads[1],
                      smalls["b_pw1"].reshape(2, d), smalls["dw_w"], smalls["dw_b"], smalls["norm_g"], smalls["norm_b"],
                      smalls["b_pw2"], loss_part])
    red = all_reduce_small(small_parts, 64)
    loss = red[61, 0]

    where = {
        "ln_mix_g": [((), 0, DEPTH, "all")], "ln_mix_b": [((), 4, DEPTH, "all")],
        "ln_ff_g": [((), 8, DEPTH, "all")], "ln_ff_b": [((), 12, DEPTH, "all")],
        "mla_g_q": [((), 16, 1, Q_LORA)], "mla_g_kv": [((), 17, 1, KV_LORA)],
        "sc_conv_w": [((0,), 18, SC_WIDTH, "chip"), ((1,), 21, SC_WIDTH, "chip")],
        "cf_b_pw1": [((), 24, 2, "chip")], "cf_dw_w": [((0,), 26, CONF_WIDTH, "chip")],
        "cf_dw_b": [((), 57, 1, "chip")], "cf_norm_g": [((), 58, 1, "chip")], "cf_norm_b": [((), 59, 1, "chip")],
        "cf_b_pw2": [((), 60, 1, "chip")]}
    vec = list(where)
    vec_res = vector_update(red, chip, [given[n] for n in vec], [given["m_" + n] for n in vec],
                            [given["v_" + n] for n in vec], [where[n] for n in vec])
    gw = dict(big_g)
    upd = {n: [big_d[n], big_m[n], big_v[n]] for n in big_g}
    for k, n in enumerate(vec):
        gw[n] = vec_res[0][k]
        upd[n] = [vec_res[1][k], vec_res[2][k], vec_res[3][k]]

    return (loss, grad_x, *[gw[n] for n in WEIGHTS], *[upd[n][0] for n in WEIGHTS],
            *[upd[n][1] for n in WEIGHTS], *[upd[n][2] for n in WEIGHTS])
```

```python
import jax
import jax.numpy as jnp
from jax import lax
from jax.experimental import pallas as pl
from jax.experimental.pallas import tpu as pltpu
from jax.experimental.pallas import tpu_sc as plsc

F32 = jnp.float32
BF16 = jnp.bfloat16
MESH = pl.DeviceIdType.MESH

DEPTH = 4
ALPHA = (2.0 * DEPTH) ** 0.25
LN_EPS = 1e-5
RMS_EPS = 1e-6
CHUNK_SHIFT = 6
N_HEADS = 8
QK_NOPE = 128
QK_ROPE = 64
V_HEAD = 128
HEAD_PAD = 256
Q_LORA = 384
KV_LORA = 256
ROPE_THETA = 10000.0
SC_WIDTH = 3
CONF_WIDTH = 31
CONV_PAD = 32
CONV_CHUNK = 64
N_CHIPS = 4
ATTN_SCALE = (QK_NOPE + QK_ROPE) ** -0.5

ADAM_LR = 0.001
ADAM_B1 = 0.9
ADAM_B2 = 0.999
ADAM_EPS = 1e-08
ADAM_WD = 0.01
ADAM_STEP = 10

VMEM_LIMIT = 56 * 2**20

NN = (((1,), (0,)), ((), ()))
NT = (((1,), (1,)), ((), ()))
TN = (((0,), (0,)), ((), ()))


def _params(sem=None):
    return pltpu.CompilerParams(dimension_semantics=sem, vmem_limit_bytes=VMEM_LIMIT)


class Stk:
    def __init__(self, kind, k, n, arr=None, layers=None, layer=None):
        self.kind, self.k, self.n, self.layers, self.layer = kind, k, n, layers, layer
        self.plain = (kind == "row" and layers is None) or kind == "full"
        self.kloc = k // N_CHIPS if kind == "row" else k
        self.nloc = n // N_CHIPS if kind == "col" else n
        if arr is not None and self.plain:
            arr = arr.reshape(k, n)
        self.arr = arr

    @property
    def shape(self):
        if self.plain:
            return (self.k, self.n)
        lead = (N_CHIPS,) if self.layers is None else (N_CHIPS, self.layers)
        return lead + (self.kloc, self.nloc)

    def spec(self, bk, bn, f):
        if self.plain:
            return pl.BlockSpec((bk, bn), f)
        assert self.kloc % bk == 0 and self.nloc % bn == 0, (self.kloc, bk, self.nloc, bn)
        pk, pn = self.kloc // bk, self.nloc // bn
        kind, layer = self.kind, self.layer

        def imap(*g):
            kb, nb = f(*g)
            if kind == "row":
                q, kb, nb = kb // pk, kb % pk, nb
            else:
                q, kb, nb = nb // pn, kb, nb % pn
            return (q, kb, nb) if layer is None else (q, layer, kb, nb)

        block = (None, bk, bn) if layer is None else (None, None, bk, bn)
        return pl.BlockSpec(block, imap)


def _mm(name, mode, a, b, grid, a_spec, b_spec, acc_shape, extras, extra_specs, out_shapes, out_specs, epi, a_fn=None,
        rows_in_order=False):
    nk = grid[2]
    ne = len(extras)

    def body(*refs):
        a_ref, b_ref = refs[0], refs[1]
        e_refs = refs[2:2 + ne]
        av = a_ref[...] if a_fn is None else a_fn(a_ref[...])
        part = lax.dot_general(av, b_ref[...], mode, preferred_element_type=F32)
        if nk == 1:
            epi(part, e_refs, refs[2 + ne:])
            return
        o_refs = refs[2 + ne:-1]
        acc = refs[-1]
        k = pl.program_id(2)

        @pl.when(k == 0)
        def _():
            acc[...] = part

        @pl.when(k > 0)
        def _():
            acc[...] += part

        @pl.when(k == nk - 1)
        def _():
            epi(acc[...], e_refs, o_refs)

    return pl.pallas_call(
        body, grid=grid, in_specs=[a_spec, b_spec, *extra_specs], out_specs=out_specs, out_shape=out_shapes,
        scratch_shapes=[pltpu.VMEM(acc_shape, F32)] if nk > 1 else [],
        compiler_params=_params(("arbitrary",) * 3 if rows_in_order else ("parallel", "parallel", "arbitrary")),
        name=name)(a, b, *extras)


def _tile(n, t):
    t = min(n, t)
    while n % t:
        t -= 8
    assert t > 0, (n, t)
    return t


def mm_nn(name, a, w, tm, tn, tk, epi, out_shapes, out_specs, extras=(), extra_specs=(), a_spec=None, a_fn=None):
    m = a.shape[0]
    tm, tn, tk = _tile(m, tm), _tile(w.n, tn), _tile(w.k, tk)
    grid = (m // tm, w.n // tn, w.k // tk)
    a_spec = a_spec or pl.BlockSpec((tm, tk), lambda i, j, k: (i, k))
    b_spec = w.spec(tk, tn, lambda i, j, k: (k, j))
    return _mm(name, NN, a, w.arr, grid, a_spec, b_spec, (tm, tn), extras, extra_specs, out_shapes, out_specs, epi, a_fn)


def mm_nt(name, a, w, m, tm, tn, tk, epi, out_shapes, out_specs, extras=(), extra_specs=(), a_spec=None,
          rows_in_order=False):
    tm, tn, tk = _tile(m, tm), _tile(w.k, tn), _tile(w.n, tk)
    grid = (m // tm, w.k // tn, w.n // tk)
    a_spec = a_spec or pl.BlockSpec((tm, tk), lambda i, j, k: (i, k))
    b_spec = w.spec(tn, tk, lambda i, j, k: (j, k))
    return _mm(name, NT, a, w.arr, grid, a_spec, b_spec, (tm, tn), extras, extra_specs, out_shapes, out_specs, epi,
               rows_in_order=rows_in_order)


def mm_tn(name, a, b, dw, s, tm=512, tn=512, tk=4096, a_spec=None, b_spec=None, a_fn=None):
    tm, tn, tk = _tile(dw.k, tm), _tile(dw.n, tn), _tile(s, tk)
    grid = (dw.k // tm, dw.n // tn, s // tk)
    a_spec = a_spec or pl.BlockSpec((tk, tm), lambda i, j, k: (k, i))
    b_spec = b_spec or pl.BlockSpec((tk, tn), lambda i, j, k: (k, j))

    def epi(acc, e, o):
        o[0][...] = acc.astype(BF16)

    out = _mm(name, TN, a, b, grid, a_spec, b_spec, (tm, tn), (), (), [jax.ShapeDtypeStruct(dw.shape, BF16)],
              [dw.spec(tm, tn, lambda i, j, k: (i, j))], epi, a_fn)[0]
    return out.reshape(N_CHIPS, dw.k // N_CHIPS, dw.n) if dw.plain else out


def _sds(shape, dtype):
    return jax.ShapeDtypeStruct(shape, dtype)


def _ij(tm, tn):
    return pl.BlockSpec((tm, tn), lambda i, j, k: (i, j))


def _i0(tm, c):
    return pl.BlockSpec((tm, c), lambda i, j, k: (i, 0))


def _0j(r, tn):
    return pl.BlockSpec((r, tn), lambda i, j, k: (0, j))


def _layer_norm_rows(r, g, b):
    mu = jnp.mean(r, axis=-1, keepdims=True)
    d = r - mu
    var = jnp.mean(d * d, axis=-1, keepdims=True)
    rstd = lax.rsqrt(var + LN_EPS)
    xh = d * rstd
    return xh * g + b, xh, rstd


def mm_residual_ln(name, a, w, x, g, b, bias=None, tm=512, tk=1024, a_fn=None):
    s, d = x.shape
    tm = _tile(s, tm)
    extras = [x, g, b] + ([bias] if bias is not None else [])
    especs = [_i0(tm, d), _0j(1, d), _0j(1, d)] + ([_0j(1, d)] if bias is not None else [])

    def epi(acc, e, o):
        r = ALPHA * e[0][...] + acc
        if bias is not None:
            r = r + e[3][...]
        y, xh, rstd = _layer_norm_rows(r, e[1][...], e[2][...])
        o[0][...] = y
        o[1][...] = y.astype(BF16)
        o[2][...] = xh
        o[3][...] = rstd

    return mm_nn(name, a, w, tm, d, tk, epi,
                 [_sds((s, d), F32), _sds((s, d), BF16), _sds((s, d), F32), _sds((s, 1), F32)],
                 [_i0(tm, d), _i0(tm, d), _i0(tm, d), _i0(tm, 1)], extras, especs, a_fn=a_fn)


def mm_plain_nn(name, a, w, out_dtype, tm=1024, tn=512, tk=1024, bias=None):
    m = a.shape[0]
    tm, tn = _tile(m, tm), _tile(w.n, tn)
    if w.kind == "col":
        tn = _tile(w.nloc, tn)

    def epi(acc, e, o):
        if bias is not None:
            acc = acc + e[0][...]
        o[0][...] = acc.astype(out_dtype)

    extras, especs = ([bias], [_0j(1, tn)]) if bias is not None else ((), ())
    return mm_nn(name, a, w, tm, tn, tk, epi, [_sds((m, w.n), out_dtype)], [_ij(tm, tn)], extras, especs)[0]


def mm_plain_nt(name, a, w, out_dtype, tm=1024, tn=512, tk=1024, add=None, add_scale=1.0, a_spec_fn=None):
    m = a.shape[0] if a_spec_fn is None else a_spec_fn[0]
    tm, tn = _tile(m, tm), _tile(w.k, tn)
    tk = _tile(w.n, tk)
    if w.kind == "col":
        tk = _tile(w.nloc, tk)
    if w.kind == "row" and not w.plain:
        tn = _tile(w.kloc, tn)

    def epi(acc, e, o):
        if add is not None:
            acc = acc + add_scale * e[0][...].astype(F32)
        o[0][...] = acc.astype(out_dtype)

    extras, especs = ([add], [_ij(tm, tn)]) if add is not None else ((), ())
    a_spec = None if a_spec_fn is None else a_spec_fn[1](tm, tk)
    return mm_nt(name, a, w, m, tm, tn, tk, epi, [_sds((m, w.k), out_dtype)], [_ij(tm, tn)], extras, especs,
                 a_spec=a_spec)[0]


def _rows(tm, c):
    return pl.BlockSpec((tm, c), lambda i: (i, 0))


def _fix(shape):
    nd = len(shape)
    return pl.BlockSpec(shape, lambda i: (0,) * nd)


def _accumulate(ref, val):
    @pl.when(pl.program_id(0) == 0)
    def _():
        ref[...] = jnp.zeros_like(ref)

    ref[...] += val


def _ln_backward_rows(dyv, xh, rstd, g, dr_ref, drb_ref, dg_ref, db_ref, ds_ref):
    dxh = dyv * g
    m1 = jnp.mean(dxh, axis=-1, keepdims=True)
    m2 = jnp.mean(dxh * xh, axis=-1, keepdims=True)
    dr = rstd * (dxh - m1 - xh * m2)
    dr_ref[...] = dr
    drb_ref[...] = dr.astype(BF16)
    _accumulate(dg_ref, jnp.sum(dyv * xh, axis=0, keepdims=True))
    _accumulate(db_ref, jnp.sum(dyv, axis=0, keepdims=True))
    _accumulate(ds_ref, jnp.sum(dr, axis=0, keepdims=True))


def mm_nt_ln_backward(name, a, w, add, xhat, rstd, g, tm=512, tk=1024, a_spec_fn=None):
    m, d = add.shape
    tm, tk = _tile(m, tm), _tile(w.n, tk)

    def epi(acc, e, o):
        _ln_backward_rows(acc + ALPHA * e[0][...], e[1][...], e[2][...], e[3][...], *o)

    vec = pl.BlockSpec((1, d), lambda i, j, k: (0, 0))
    a_spec = None if a_spec_fn is None else a_spec_fn(tm, tk)
    return mm_nt(name, a, w, m, tm, d, tk, epi,
                 [_sds((m, d), F32), _sds((m, d), BF16), _sds((1, d), F32), _sds((1, d), F32), _sds((1, d), F32)],
                 [_i0(tm, d), _i0(tm, d), vec, vec, vec], [add, xhat, rstd, g],
                 [_i0(tm, d), _i0(tm, d), _i0(tm, 1), vec], a_spec=a_spec, rows_in_order=True)


def loss_ln_backward(y, target, xhat, rstd, g, tm=512):
    s, d = y.shape
    tm = _tile(s, tm)

    def body(y_ref, t_ref, xh_ref, rstd_ref, g_ref, dr_ref, drb_ref, dg_ref, db_ref, ds_ref, loss_ref):
        e = y_ref[...] - t_ref[...]
        part = 0.5 * jnp.sum(jnp.mean(e * e, axis=-1, keepdims=True), axis=0, keepdims=True)
        _accumulate(loss_ref, jnp.broadcast_to(part, (1, d)))
        _ln_backward_rows(e * (1.0 / d), xh_ref[...], rstd_ref[...], g_ref[...], dr_ref, drb_ref, dg_ref, db_ref, ds_ref)

    return pl.pallas_call(
        body, grid=(s // tm,),
        in_specs=[_rows(tm, d), _rows(tm, d), _rows(tm, d), _rows(tm, 1), _fix((1, d))],
        out_specs=[_rows(tm, d), _rows(tm, d)] + [_fix((1, d))] * 4,
        out_shape=[_sds((s, d), F32), _sds((s, d), BF16)] + [_sds((1, d), F32)] * 4,
        compiler_params=_params(("arbitrary",)), name="loss_ln_backward")(y, target, xhat, rstd, g)


def _cols(s, tc, off=0):
    return pl.BlockSpec((s, tc), lambda i: (0, i + off))


def _shift_down(z, sft, rows):
    return jnp.where(rows >= sft, pltpu.roll(z, sft, 0), 0.0)


def _shift_up(z, sft, rows, s):
    return jnp.where(rows < s - sft, pltpu.roll(z, (s - sft) % s, 0), 0.0)


def short_conv_gate(u, conv_w, tc=256):
    s, d3 = u.shape
    d = d3 // 3
    nb = d // tc

    def body(b_ref, c_ref, h_ref, w_ref, o_ref):
        rows = lax.broadcasted_iota(jnp.int32, (s, tc), 0)
        z = c_ref[...] * h_ref[...]
        cz = jnp.zeros((s, tc), F32)
        for k in range(SC_WIDTH):
            sft = SC_WIDTH - 1 - k
            cz = cz + w_ref[pl.ds(k, 1), :] * (_shift_down(z, sft, rows) if sft else z)
        o_ref[...] = (b_ref[...] * cz).astype(BF16)

    return pl.pallas_call(
        body, grid=(nb,),
        in_specs=[_cols(s, tc), _cols(s, tc, nb), _cols(s, tc, 2 * nb), _cols(SC_WIDTH, tc)],
        out_specs=_cols(s, tc), out_shape=_sds((s, d), BF16),
        compiler_params=_params(("parallel",)), name="short_conv_gate")(u, u, u, conv_w)


def short_conv_gate_bwd(u, conv_w, dg, tc=256):
    s, d3 = u.shape
    d = d3 // 3
    nb = d // tc

    def body(b_ref, c_ref, h_ref, w_ref, dg_ref, du_ref, dw_ref):
        rows = lax.broadcasted_iota(jnp.int32, (s, tc), 0)
        c, h, dgv = c_ref[...], h_ref[...], dg_ref[...]
        z = c * h
        dcz = dgv * b_ref[...]
        cz = jnp.zeros((s, tc), F32)
        dz = jnp.zeros((s, tc), F32)
        for k in range(SC_WIDTH):
            sft = SC_WIDTH - 1 - k
            zs = _shift_down(z, sft, rows) if sft else z
            wk = w_ref[pl.ds(k, 1), :]
            cz = cz + wk * zs
            dz = dz + wk * (_shift_up(dcz, sft, rows, s) if sft else dcz)
            dw_ref[pl.ds(k, 1), :] = jnp.sum(dcz * zs, axis=0, keepdims=True)
        du_ref[0] = (dgv * cz).astype(BF16)
        du_ref[1] = (dz * h).astype(BF16)
        du_ref[2] = (dz * c).astype(BF16)

    return pl.pallas_call(
        body, grid=(nb,),
        in_specs=[_cols(s, tc), _cols(s, tc, nb), _cols(s, tc, 2 * nb), _cols(SC_WIDTH, tc), _cols(s, tc)],
        out_specs=[pl.BlockSpec((3, s, tc), lambda i: (0, 0, i)), _cols(SC_WIDTH, tc)],
        out_shape=[_sds((3, s, d), BF16), _sds((SC_WIDTH, d), F32)],
        compiler_params=_params(("parallel",)), name="short_conv_gate_bwd")(u, u, u, conv_w, dg)


def _store_shifted_down(ref, z, rows):
    s, tc = z.shape
    for b in range(8):
        ref[b, pl.ds(0, CONV_PAD), :] = jnp.zeros((CONV_PAD, tc), F32)
        ref[b, pl.ds(CONV_PAD, s), :] = z if b == 0 else _shift_down(z, b, rows)


def _store_shifted_up(ref, z, rows):
    s, tc = z.shape
    for b in range(8):
        ref[b, pl.ds(0, s), :] = z if b == 0 else _shift_up(z, b, rows, s)
        ref[b, pl.ds(s, CONV_PAD), :] = jnp.zeros((CONV_PAD, tc), F32)


def conformer_glu_conv(u, dw_w, dw_b, tc=128):
    s, d2 = u.shape
    d = d2 // 2
    nb = d // tc

    ch = min(CONV_CHUNK, s)

    def body(a_ref, g_ref, w_ref, b_ref, o_ref, down):
        rows = lax.broadcasted_iota(jnp.int32, (s, tc), 0)
        _store_shifted_down(down, a_ref[...] * jax.nn.sigmoid(g_ref[...]), rows)

        def chunk(ci, carry):
            r0 = pl.multiple_of(ci * ch, ch)
            acc = jnp.broadcast_to(b_ref[...], (ch, tc))
            for k in range(CONF_WIDTH):
                sft = CONF_WIDTH - 1 - k
                acc = acc + w_ref[pl.ds(k, 1), :] * down[sft % 8, pl.ds(CONV_PAD + r0 - (sft // 8) * 8, ch), :]
            o_ref[pl.ds(r0, ch), :] = acc
            return carry

        lax.fori_loop(0, s // ch, chunk, 0)

    return pl.pallas_call(
        body, grid=(nb,),
        in_specs=[_cols(s, tc), _cols(s, tc, nb), _cols(CONF_WIDTH, tc), _cols(1, tc)],
        out_specs=_cols(s, tc), out_shape=_sds((s, d), F32),
        scratch_shapes=[pltpu.VMEM((8, CONV_PAD + s, tc), F32)],
        compiler_params=_params(("parallel",)), name="conformer_glu_conv")(u, u, dw_w, dw_b)


def conformer_glu_conv_bwd(u, dw_w, dhc, tc=128):
    s, d2 = u.shape
    d = d2 // 2
    nb = d // tc
    ch = min(CONV_CHUNK, s)

    def body(a_ref, g_ref, w_ref, dhc_ref, du_ref, dbias_ref, dw_ref, db_ref, down, up, dw_acc, dh_buf):
        rows = lax.broadcasted_iota(jnp.int32, (s, tc), 0)
        a = a_ref[...]
        sg = jax.nn.sigmoid(g_ref[...])
        dhcv = dhc_ref[...]
        _store_shifted_down(down, a * sg, rows)
        _store_shifted_up(up, dhcv, rows)
        dw_acc[...] = jnp.zeros_like(dw_acc)

        def chunk(ci, carry):
            r0 = pl.multiple_of(ci * ch, ch)
            dc = dhc_ref[pl.ds(r0, ch), :]
            dh = jnp.zeros((ch, tc), F32)
            for k in range(CONF_WIDTH):
                sft = CONF_WIDTH - 1 - k
                a8, b = (sft // 8) * 8, sft % 8
                dh = dh + w_ref[pl.ds(k, 1), :] * up[b, pl.ds(r0 + a8, ch), :]
                prod = dc * down[b, pl.ds(CONV_PAD + r0 - a8, ch), :]
                dw_acc[k] += jnp.sum(prod.reshape(ch // 8, 8, tc), axis=0)
            dh_buf[pl.ds(r0, ch), :] = dh
            return carry

        lax.fori_loop(0, s // ch, chunk, 0)
        dh = dh_buf[...]
        da = dh * sg
        dgate = dh * a * sg * (1.0 - sg)
        du_ref[0] = da.astype(BF16)
        du_ref[1] = dgate.astype(BF16)
        dbias_ref[pl.ds(0, 1), :] = jnp.sum(da, axis=0, keepdims=True)
        dbias_ref[pl.ds(1, 1), :] = jnp.sum(dgate, axis=0, keepdims=True)
        db_ref[...] = jnp.sum(dhcv, axis=0, keepdims=True)
        for k in range(CONF_WIDTH):
            dw_ref[pl.ds(k, 1), :] = jnp.sum(dw_acc[k], axis=0, keepdims=True)

    return pl.pallas_call(
        body, grid=(nb,),
        in_specs=[_cols(s, tc), _cols(s, tc, nb), _cols(CONF_WIDTH, tc), _cols(s, tc)],
        out_specs=[pl.BlockSpec((2, s, tc), lambda i: (0, 0, i)), _cols(2, tc), _cols(CONF_WIDTH, tc), _cols(1, tc)],
        out_shape=[_sds((2, s, d), BF16), _sds((2, d), F32), _sds((CONF_WIDTH, d), F32), _sds((1, d), F32)],
        scratch_shapes=[pltpu.VMEM((8, CONV_PAD + s, tc), F32), pltpu.VMEM((8, CONV_PAD + s, tc), F32),
                        pltpu.VMEM((CONF_WIDTH + 1, 8, tc), F32), pltpu.VMEM((s, tc), F32)],
        compiler_params=_params(("parallel",)), name="conformer_glu_conv_bwd")(u, u, dw_w, dhc)


def conformer_norm_swish(hc, g, b, tm=512):
    s, d = hc.shape
    tm = _tile(s, tm)

    def body(h_ref, g_ref, b_ref, o_ref):
        n, _, _ = _layer_norm_rows(h_ref[...], g_ref[...], b_ref[...])
        o_ref[...] = (n * jax.nn.sigmoid(n)).astype(BF16)

    return pl.pallas_call(
        body, grid=(s // tm,), in_specs=[_rows(tm, d), _fix((1, d)), _fix((1, d))], out_specs=_rows(tm, d),
        out_shape=_sds((s, d), BF16), compiler_params=_params(("parallel",)), name="conformer_norm_swish")(hc, g, b)


def conformer_norm_swish_bwd(hc, g, b, ds, tm=512):
    s, d = hc.shape
    tm = _tile(s, tm)

    def body(h_ref, g_ref, b_ref, ds_ref, dh_ref, dg_ref, db_ref):
        n, nh, rstd = _layer_norm_rows(h_ref[...], g_ref[...], b_ref[...])
        sg = jax.nn.sigmoid(n)
        dn = ds_ref[...] * (sg * (1.0 + n * (1.0 - sg)))
        dnh = dn * g_ref[...]
        m1 = jnp.mean(dnh, axis=-1, keepdims=True)
        m2 = jnp.mean(dnh * nh, axis=-1, keepdims=True)
        dh_ref[...] = rstd * (dnh - m1 - nh * m2)
        _accumulate(dg_ref, jnp.sum(dn * nh, axis=0, keepdims=True))
        _accumulate(db_ref, jnp.sum(dn, axis=0, keepdims=True))

    return pl.pallas_call(
        body, grid=(s // tm,), in_specs=[_rows(tm, d), _fix((1, d)), _fix((1, d)), _rows(tm, d)],
        out_specs=[_rows(tm, d), _fix((1, d)), _fix((1, d))],
        out_shape=[_sds((s, d), F32), _sds((1, d), F32), _sds((1, d), F32)],
        compiler_params=_params(("arbitrary",)), name="conformer_norm_swish_bwd")(hc, g, b, ds)


def _swap_halves(x):
    lane = lax.broadcasted_iota(jnp.int32, x.shape, 1)
    return jnp.where(lane < QK_ROPE // 2, pltpu.roll(x, 128 - QK_ROPE // 2, 1), pltpu.roll(x, QK_ROPE // 2, 1))


def _rope(x, cf, sf):
    return x * cf + _swap_halves(x) * sf


def _unrope(dx, cf, sf):
    return dx * cf - _swap_halves(dx) * sf


def _rms_rows(x, g):
    r = lax.rsqrt(jnp.mean(x * x, axis=-1, keepdims=True) + RMS_EPS)
    return x * r, r


def mla_latents(t, g_q, g_kv, cf, sf, tm=512):
    s = t.shape[0]
    tm = _tile(s, tm)

    def body(t_ref, gq_ref, gkv_ref, cf_ref, sf_ref, cq_ref, ckv_ref, kpe_ref):
        xq, _ = _rms_rows(t_ref[:, 0:Q_LORA], gq_ref[...])
        cq_ref[...] = (xq * gq_ref[...]).astype(BF16)
        xkv, _ = _rms_rows(t_ref[:, Q_LORA:Q_LORA + KV_LORA], gkv_ref[...])
        ckv_ref[...] = (xkv * gkv_ref[...]).astype(BF16)
        kpe_ref[...] = _rope(t_ref[:, Q_LORA + KV_LORA:], cf_ref[...], sf_ref[...]).astype(BF16)

    w = Q_LORA + KV_LORA + 128
    return pl.pallas_call(
        body, grid=(s // tm,),
        in_specs=[_rows(tm, w), _fix((1, Q_LORA)), _fix((1, KV_LORA)), _rows(tm, 128), _rows(tm, 128)],
        out_specs=[_rows(tm, Q_LORA), _rows(tm, KV_LORA), _rows(tm, 128)],
        out_shape=[_sds((s, Q_LORA), BF16), _sds((s, KV_LORA), BF16), _sds((s, 128), BF16)],
        compiler_params=_params(("parallel",)), name="mla_latents")(t, g_q, g_kv, cf, sf)


def mla_latents_bwd(t, g_q, g_kv, cf, sf, dcq, dckv, dkpe, tm=512):
    s = t.shape[0]
    tm = _tile(s, tm)
    w = Q_LORA + KV_LORA + 128

    def rms_bwd(x, g, dy):
        xh, r = _rms_rows(x, g)
        dxh = dy * g
        return r * (dxh - xh * jnp.mean(dxh * xh, axis=-1, keepdims=True)), jnp.sum(dy * xh, axis=0, keepdims=True)

    def body(t_ref, gq_ref, gkv_ref, cf_ref, sf_ref, dcq_ref, dckv_ref, dkpe_ref, dt_ref, dgq_ref, dgkv_ref):
        dxq, dgq = rms_bwd(t_ref[:, 0:Q_LORA], gq_ref[...], dcq_ref[...])
        dxkv, dgkv = rms_bwd(t_ref[:, Q_LORA:Q_LORA + KV_LORA], gkv_ref[...], dckv_ref[...])
        dt_ref[:, 0:Q_LORA] = dxq.astype(BF16)
        dt_ref[:, Q_LORA:Q_LORA + KV_LORA] = dxkv.astype(BF16)
        dt_ref[:, Q_LORA + KV_LORA:] = _unrope(dkpe_ref[...], cf_ref[...], sf_ref[...]).astype(BF16)
        _accumulate(dgq_ref, dgq)
        _accumulate(dgkv_ref, dgkv)

    return pl.pallas_call(
        body, grid=(s // tm,),
        in_specs=[_rows(tm, w), _fix((1, Q_LORA)), _fix((1, KV_LORA)), _rows(tm, 128), _rows(tm, 128),
                  _rows(tm, Q_LORA), _rows(tm, KV_LORA), _rows(tm, 128)],
        out_specs=[_rows(tm, w), _fix((1, Q_LORA)), _fix((1, KV_LORA))],
        out_shape=[_sds((s, w), BF16), _sds((1, Q_LORA), F32), _sds((1, KV_LORA), F32)],
        compiler_params=_params(("arbitrary",)), name="mla_latents_bwd")(t, g_q, g_kv, cf, sf, dcq, dckv, dkpe)


def mla_queries(cq, w_uq, cf, sf, tm=2048):
    s = cq.shape[0]
    tm = _tile(s, tm)

    def epi(acc, e, o):
        o[0][:, 0:QK_NOPE] = acc[:, 0:QK_NOPE].astype(BF16)
        o[0][:, QK_NOPE:] = _rope(acc[:, QK_NOPE:], e[0][...], e[1][...]).astype(BF16)

    return mm_nn("mla_queries", cq, w_uq, tm, HEAD_PAD, Q_LORA, epi, [_sds((s, N_HEADS * HEAD_PAD), BF16)],
                 [_ij(tm, HEAD_PAD)], [cf, sf], [_i0(tm, 128), _i0(tm, 128)])[0]


def mla_keys(ckv, w_uk, kpe, tm=2048):
    s = ckv.shape[0]
    tm = _tile(s, tm)

    def epi(acc, e, o):
        o[0][:, 0:QK_NOPE] = acc.astype(BF16)
        o[0][:, QK_NOPE:] = e[0][...]

    return mm_nn("mla_keys", ckv, w_uk, tm, QK_NOPE, KV_LORA, epi, [_sds((s, N_HEADS * HEAD_PAD), BF16)],
                 [_ij(tm, HEAD_PAD)], [kpe], [_i0(tm, 128)])[0]


def _masked_scores(q, k, qi, tq, kv):
    sc = lax.dot_general(q, k, NT, preferred_element_type=F32) * ATTN_SCALE
    row = lax.broadcasted_iota(jnp.int32, (tq, kv), 0) + qi * tq
    col = lax.broadcasted_iota(jnp.int32, (tq, kv), 1)
    ok = lax.shift_right_logical(col, CHUNK_SHIFT) <= lax.shift_right_logical(row, CHUNK_SHIFT)
    return jnp.where(ok, sc, -1e30)


def attention(q, k, v, tq=512):
    s = q.shape[0]
    tq = _tile(s, tq)
    nq = s // tq

    def body(q_ref, k_ref, v_ref, o_ref):
        for qi in range(nq):
            kv = (qi + 1) * tq
            sc = _masked_scores(q_ref[pl.ds(qi * tq, tq), :], k_ref[pl.ds(0, kv), :], qi, tq, kv)
            p = jnp.exp(sc - jnp.max(sc, axis=-1, keepdims=True))
            o = lax.dot_general(p.astype(BF16), v_ref[pl.ds(0, kv), :], NN, preferred_element_type=F32)
            o_ref[pl.ds(qi * tq, tq), :] = (o / jnp.sum(p, axis=-1, keepdims=True)).astype(BF16)

    hq = pl.BlockSpec((s, HEAD_PAD), lambda h: (0, h))
    hv = pl.BlockSpec((s, V_HEAD), lambda h: (0, h))
    return pl.pallas_call(
        body, grid=(N_HEADS,), in_specs=[hq, hq, hv], out_specs=hv, out_shape=_sds((s, N_HEADS * V_HEAD), BF16),
        compiler_params=_params(("parallel",)), name="attention")(q, k, v)


def attention_bwd(q, k, v, do, tq=512):
    s = q.shape[0]
    tq = _tile(s, tq)
    nq = s // tq

    def body(q_ref, k_ref, v_ref, do_ref, dq_ref, dk_ref, dv_ref, dk_acc, dv_acc):
        dk_acc[...] = jnp.zeros_like(dk_acc)
        dv_acc[...] = jnp.zeros_like(dv_acc)
        for qi in range(nq):
            kv = (qi + 1) * tq
            qt = q_ref[pl.ds(qi * tq, tq), :]
            kt = k_ref[pl.ds(0, kv), :]
            dot = do_ref[pl.ds(qi * tq, tq), :]
            sc = _masked_scores(qt, kt, qi, tq, kv)
            p = jnp.exp(sc - jnp.max(sc, axis=-1, keepdims=True))
            p = p / jnp.sum(p, axis=-1, keepdims=True)
            dp = lax.dot_general(dot, v_ref[pl.ds(0, kv), :], NT, preferred_element_type=F32)
            delta = jnp.sum(p * dp, axis=-1, keepdims=True)
            ds = (p * (dp - delta) * ATTN_SCALE).astype(BF16)
            dq_ref[pl.ds(qi * tq, tq), :] = lax.dot_general(ds, kt, NN, preferred_element_type=F32).astype(BF16)
            dk_acc[pl.ds(0, kv), :] += lax.dot_general(ds, qt, TN, preferred_element_type=F32)
            dv_acc[pl.ds(0, kv), :] += lax.dot_general(p.astype(BF16), dot, TN, preferred_element_type=F32)
        dk_ref[...] = dk_acc[...].astype(BF16)
        dv_ref[...] = dv_acc[...].astype(BF16)

    hq = pl.BlockSpec((s, HEAD_PAD), lambda h: (0, h))
    hv = pl.BlockSpec((s, V_HEAD), lambda h: (0, h))
    return pl.pallas_call(
        body, grid=(N_HEADS,), in_specs=[hq, hq, hv, hv], out_specs=[hq, hq, hv],
        out_shape=[_sds((s, N_HEADS * HEAD_PAD), BF16), _sds((s, N_HEADS * HEAD_PAD), BF16),
                   _sds((s, N_HEADS * V_HEAD), BF16)],
        scratch_shapes=[pltpu.VMEM((s, HEAD_PAD), F32), pltpu.VMEM((s, V_HEAD), F32)],
        compiler_params=_params(("parallel",)), name="attention_bwd")(q, k, v, do)


def mla_unrope_grads(dq, dk, cf, sf, tm=512):
    s = dq.shape[0]
    tm = _tile(s, tm)

    def body(dq_ref, dk_ref, cf_ref, sf_ref, dql_ref, dkn_ref, dkpe_ref):
        cfv, sfv = cf_ref[...], sf_ref[...]
        dkpe = jnp.zeros((tm, 128), F32)
        for h in range(N_HEADS):
            lo = h * HEAD_PAD
            dql_ref[:, lo:lo + QK_NOPE] = dq_ref[:, lo:lo + QK_NOPE]
            dql_ref[:, lo + QK_NOPE:lo + HEAD_PAD] = _unrope(
                dq_ref[:, lo + QK_NOPE:lo + HEAD_PAD].astype(F32), cfv, sfv).astype(BF16)
            dkn_ref[:, h * QK_NOPE:(h + 1) * QK_NOPE] = dk_ref[:, lo:lo + QK_NOPE]
            dkpe = dkpe + dk_ref[:, lo + QK_NOPE:lo + HEAD_PAD].astype(F32)
        dkpe_ref[...] = dkpe

    wq = N_HEADS * HEAD_PAD
    return pl.pallas_call(
        body, grid=(s // tm,), in_specs=[_rows(tm, wq), _rows(tm, wq), _rows(tm, 128), _rows(tm, 128)],
        out_specs=[_rows(tm, wq), _rows(tm, N_HEADS * QK_NOPE), _rows(tm, 128)],
        out_shape=[_sds((s, wq), BF16), _sds((s, N_HEADS * QK_NOPE), BF16), _sds((s, 128), F32)],
        compiler_params=_params(("parallel",)), name="mla_unrope_grads")(dq, dk, cf, sf)


ANY = pl.BlockSpec(memory_space=pl.ANY)
GATHER_ID = 1
CHIP_EXCHANGE_ID = 2
PAIR_ID = 3
ALL_ID = 4


def _nbytes(a):
    return a.size * a.dtype.itemsize


def _copy_cost(operand_bytes, sent_fraction):
    sent = int(operand_bytes * sent_fraction)
    return pl.CostEstimate(flops=0, transcendentals=0, bytes_accessed=2 * sent, remote_bytes_transferred=sent)


def _handshake(peers):
    barrier = pltpu.get_barrier_semaphore()
    for peer in peers:
        pl.semaphore_signal(barrier, inc=1, device_id=peer, device_id_type=MESH)
    pl.semaphore_wait(barrier, len(peers))


def _place():
    x, y, c = lax.axis_index("x"), lax.axis_index("y"), lax.axis_index("c")
    chips = [(1 - x, y), (x, 1 - y), (1 - x, 1 - y)]
    return x, y, c, chips


def _half(ref, hc, axis=0):
    n = ref.shape[axis] // 2
    idx = (slice(None),) * axis + (pl.ds(hc * n, n),)
    return ref.at[idx]


def gather_shards(name, tensors, by_columns=()):
    nt = len(tensors)

    def body(*refs):
        a, g = refs[:nt], refs[nt:2 * nt]
        send, recv = refs[2 * nt:]
        x, y, c, _ = _place()
        q = 2 * x + y
        sib, xn, yn = (x, y, 1 - c), (1 - x, y, c), (x, 1 - y, c)
        q_xn, q_yn, q_diag = 2 * (1 - x) + y, 2 * x + 1 - y, 2 * (1 - x) + 1 - y
        _handshake([sib, xn, yn])

        def whole(t, p):
            if t in by_columns:
                n = a[t].shape[1]
                return g[t].at[:, pl.ds(p * n, n)]
            return g[t].at[p]

        def part(t, p, hc, quarter=None):
            rows = a[t].shape[0]
            if quarter is None:
                return whole(t, p).at[pl.ds(hc * (rows // 2), rows // 2)]
            return whole(t, p).at[pl.ds(hc * (rows // 2) + quarter * (rows // 4), rows // 4)]

        def rc(t, k, src, dst, to):
            return pltpu.make_async_remote_copy(src_ref=src, dst_ref=dst, send_sem=send.at[t, k], recv_sem=recv.at[t, k],
                                                device_id=to, device_id_type=MESH)

        sent = []

        def go(cp):
            cp.start()
            sent.append(cp)

        def landed(t, k, piece, frm):
            rc(t, k, piece, piece, frm).wait_recv()
            return piece

        for t in range(nt):
            go(rc(t, 8, a[t], whole(t, q), sib))
            mine = _half(a[t], c)
            go(rc(t, 0, mine, part(t, q, c), xn))
            go(rc(t, 1, mine, part(t, q, c), yn))
        for t in range(nt):
            from_y = landed(t, 1, part(t, q_yn, c), yn)
            go(rc(t, 2, part(t, q_yn, c, 0), part(t, q_yn, c, 0), xn))
            go(rc(t, 5, from_y, from_y, sib))
            from_x = landed(t, 0, part(t, q_xn, c), xn)
            go(rc(t, 3, part(t, q_xn, c, 1), part(t, q_xn, c, 1), yn))
            go(rc(t, 4, from_x, from_x, sib))
        for t in range(nt):
            for k, frm in ((2, xn), (3, yn)):
                piece = landed(t, k, part(t, q_diag, c, k - 2), frm)
                go(rc(t, 4 + k, piece, piece, sib))
        for t in range(nt):
            landed(t, 4, part(t, q_xn, 1 - c), sib)
            landed(t, 5, part(t, q_yn, 1 - c), sib)
            landed(t, 6, part(t, q_diag, 1 - c, 0), sib)
            landed(t, 7, part(t, q_diag, 1 - c, 1), sib)
            landed(t, 8, whole(t, q), sib)
        for cp in sent:
            cp.wait_send()

    return pl.kernel(
        body, name=name,
        out_type=[_sds((a.shape[0], N_CHIPS * a.shape[1]) if t in by_columns else (N_CHIPS,) + a.shape, a.dtype)
                  for t, a in enumerate(tensors)],
        mesh=plsc.ScalarSubcoreMesh(axis_name="sequencer", num_cores=1),
        scratch_types=[pltpu.SemaphoreType.DMA((nt, 9)), pltpu.SemaphoreType.DMA((nt, 9))],
        cost_estimate=_copy_cost(sum(_nbytes(a) for a in tensors), 4),
        compiler_params=pltpu.CompilerParams(collective_id=GATHER_ID))(*tensors)


def pair_exchange(name, grads, on_sequencer):
    nt = len(grads)

    def body(*refs):
        g, theirs = refs[:nt], refs[nt:2 * nt]
        send, recv = refs[2 * nt:]
        x, y, c, _ = _place()
        if on_sequencer:
            _handshake([(x, y, 1 - c)])
        cps = []
        for t in range(nt):
            cp = pltpu.make_async_remote_copy(src_ref=_half(g[t], 1 - c, 1), dst_ref=theirs[t], send_sem=send.at[t],
                                              recv_sem=recv.at[t], device_id=(x, y, 1 - c), device_id_type=MESH)
            cp.start()
            cps.append(cp)
        for cp in cps:
            cp.wait()

    if not on_sequencer:
        return pl.pallas_call(
            body, in_specs=[ANY] * nt, out_specs=[ANY] * nt,
            out_shape=[_sds((N_CHIPS, a.shape[1] // 2, a.shape[2]), a.dtype) for a in grads],
            scratch_shapes=[pltpu.SemaphoreType.DMA((nt,)), pltpu.SemaphoreType.DMA((nt,))],
            name=name)(*grads)
    return pl.kernel(
        body, name=name, out_type=[_sds((N_CHIPS, a.shape[1] // 2, a.shape[2]), a.dtype) for a in grads],
        mesh=plsc.ScalarSubcoreMesh(axis_name="sequencer", num_cores=1),
        scratch_types=[pltpu.SemaphoreType.DMA((nt,)), pltpu.SemaphoreType.DMA((nt,))],
        cost_estimate=_copy_cost(sum(_nbytes(a) for a in grads), 0.5),
        compiler_params=pltpu.CompilerParams(collective_id=PAIR_ID))(*grads)


def chip_exchange(name, parts):
    nt = len(parts)

    def body(*refs):
        a, r = refs[:nt], refs[nt:2 * nt]
        send, recv = refs[2 * nt:]
        x, y, c, chips = _place()
        _handshake([(*chip, c) for chip in chips])
        cps = []
        for t in range(nt):
            for j, chip in enumerate(chips):
                cp = pltpu.make_async_remote_copy(
                    src_ref=a[t].at[2 * chip[0] + chip[1]], dst_ref=r[t].at[j], send_sem=send.at[t, j],
                    recv_sem=recv.at[t, j], device_id=(*chip, c), device_id_type=MESH)
                cp.start()
                cps.append(cp)
        for cp in cps:
            cp.wait()

    return pl.kernel(
        body, name=name, out_type=[_sds((N_CHIPS - 1,) + a.shape[1:], a.dtype) for a in parts],
        mesh=plsc.ScalarSubcoreMesh(axis_name="sequencer", num_cores=1),
        scratch_types=[pltpu.SemaphoreType.DMA((nt, 3)), pltpu.SemaphoreType.DMA((nt, 3))],
        cost_estimate=_copy_cost(sum(_nbytes(a) for a in parts), 0.75),
        compiler_params=pltpu.CompilerParams(collective_id=CHIP_EXCHANGE_ID))(*parts)


def pair_share(name, halves):
    nt = len(halves)

    def body(*refs):
        h, other = refs[:nt], refs[nt:2 * nt]
        send, recv = refs[2 * nt:]
        x, y, c, _ = _place()
        _handshake([(x, y, 1 - c)])
        cps = []
        for t in range(nt):
            cp = pltpu.make_async_remote_copy(src_ref=h[t], dst_ref=other[t], send_sem=send.at[t], recv_sem=recv.at[t],
                                              device_id=(x, y, 1 - c), device_id_type=MESH)
            cp.start()
            cps.append(cp)
        for cp in cps:
            cp.wait()

    return pl.kernel(
        body, name=name, out_type=[_sds(a.shape, a.dtype) for a in halves],
        mesh=plsc.ScalarSubcoreMesh(axis_name="sequencer", num_cores=1),
        scratch_types=[pltpu.SemaphoreType.DMA((nt,)), pltpu.SemaphoreType.DMA((nt,))],
        cost_estimate=_copy_cost(sum(_nbytes(a) for a in halves), 1),
        compiler_params=pltpu.CompilerParams(collective_id=PAIR_ID))(*halves)


def pack_rows(name, parts, rows):
    cdim = parts[0].shape[1]
    n = len(parts)
    vm = pl.BlockSpec(memory_space=pltpu.VMEM)

    def pack(*refs):
        p, o_ref = refs[:n], refs[n]
        at = 0
        for ref in p:
            o_ref[pl.ds(at, ref.shape[0]), :] = ref[...]
            at += ref.shape[0]
        o_ref[pl.ds(at, rows - at), :] = jnp.zeros((rows - at, cdim), F32)

    return pl.pallas_call(pack, in_specs=[vm] * n, out_specs=vm, out_shape=_sds((rows, cdim), F32), name=name)(*parts)


def all_reduce_small(parts, rows):
    cdim = parts[0].shape[1]
    vm = pl.BlockSpec(memory_space=pltpu.VMEM)
    mine = pack_rows("small_pack", parts, rows)

    def exchange(mine_ref, buf, send, recv, lsem):
        x, y, c, _ = _place()
        me = 4 * x + 2 * y + c
        peers = [(x ^ (k >> 2), y ^ ((k >> 1) & 1), c ^ (k & 1)) for k in range(1, 8)]
        _handshake(peers)
        own = pltpu.make_async_copy(mine_ref, buf.at[me], lsem)
        own.start()
        cps = []
        for k, to in enumerate(peers):
            cp = pltpu.make_async_remote_copy(src_ref=mine_ref, dst_ref=buf.at[me], send_sem=send.at[k], recv_sem=recv.at[k],
                                              device_id=to, device_id_type=MESH)
            cp.start()
            cps.append(cp)
        for k, (px, py, pc) in enumerate(peers):
            pltpu.make_async_remote_copy(src_ref=mine_ref, dst_ref=buf.at[4 * px + 2 * py + pc], send_sem=send.at[k],
                                         recv_sem=recv.at[k], device_id=(x, y, c), device_id_type=MESH).wait_recv()
        for cp in cps:
            cp.wait_send()
        own.wait()

    landed = pl.kernel(
        exchange, name="small_exchange", out_type=_sds((8, rows, cdim), F32),
        mesh=plsc.ScalarSubcoreMesh(axis_name="sequencer", num_cores=1),
        scratch_types=[pltpu.SemaphoreType.DMA((7,)), pltpu.SemaphoreType.DMA((7,)), pltpu.SemaphoreType.DMA],
        cost_estimate=_copy_cost(rows * cdim * 4, 7),
        compiler_params=pltpu.CompilerParams(collective_id=ALL_ID))(mine)

    def total(buf, o_ref):
        acc = buf[0]
        for d in range(1, 8):
            acc = acc + buf[d]
        o_ref[...] = acc

    return pl.pallas_call(total, in_specs=[vm], out_specs=vm, out_shape=_sds((rows, cdim), F32), name="small_sum")(landed)


def pair_sum(g, theirs, core, tm=256):
    _, r, c = g.shape
    tm = _tile(r // 2, tm)
    nh = r // 2 // tm

    def body(core_ref, a_ref, b_ref, o_ref):
        o_ref[...] = (a_ref[...].astype(F32) + b_ref[...].astype(F32)).astype(BF16)

    blk = (N_CHIPS, tm, c)
    return pl.pallas_call(
        body, grid_spec=pltpu.PrefetchScalarGridSpec(
            num_scalar_prefetch=1, grid=(nh,),
            in_specs=[pl.BlockSpec(blk, lambda i, cr: (0, cr[0] * nh + i, 0)), pl.BlockSpec(blk, lambda i, cr: (0, i, 0))],
            out_specs=pl.BlockSpec(blk, lambda i, cr: (0, i, 0))),
        out_shape=_sds(theirs.shape, BF16), compiler_params=_params(("parallel",)), name="pair_sum")(core, g, theirs)


def chip_sum(own, landed, chip, stack, layer, layers, tm=256):
    _, r, c = own.shape
    tm = _tile(r, tm)

    def body(chip_ref, own_ref, l_ref, *rest):
        acc = own_ref[...].astype(F32)
        for j in range(N_CHIPS - 1):
            acc = acc + l_ref[j].astype(F32)
        rest[-1][...] = acc

    in_specs = [pl.BlockSpec((None, tm, c), lambda i, qr: (qr[0], i, 0)),
                pl.BlockSpec((N_CHIPS - 1, tm, c), lambda i, qr: (0, i, 0))]
    args = [chip, own, landed]
    if stack is not None:
        in_specs.append(ANY)
        args.append(stack)
    return pl.pallas_call(
        body, grid_spec=pltpu.PrefetchScalarGridSpec(
            num_scalar_prefetch=1, grid=(r // tm,), in_specs=in_specs,
            out_specs=pl.BlockSpec((None, tm, c), lambda i, qr: (layer, i, 0))),
        out_shape=_sds((layers, r, c), F32), input_output_aliases={3: 0} if stack is not None else {},
        compiler_params=_params(("parallel",)), name="chip_sum")(*args)


def _adamw_math(w, g, m, v):
    bc1 = 1.0 - ADAM_B1 ** ADAM_STEP
    bc2 = 1.0 - ADAM_B2 ** ADAM_STEP
    nm = ADAM_B1 * m + (1.0 - ADAM_B1) * g
    nv = ADAM_B2 * v + (1.0 - ADAM_B2) * (g * g)
    return -ADAM_LR * ((nm / bc1) / (jnp.sqrt(nv / bc2) + ADAM_EPS) + ADAM_WD * w), nm, nv


def vector_update(red, chip, ws, ms, vs, where):
    n = len(ws)
    dd = red.shape[1]

    def body(chip_ref, red_ref, *refs):
        w_r, m_r, v_r = refs[0:n], refs[n:2 * n], refs[2 * n:3 * n]
        g_o, d_o, m_o, v_o = (refs[(3 + k) * n:(4 + k) * n] for k in range(4))
        q = chip_ref[0]

        def chip_block(val, width):
            out = val[:, 0:width]
            for p in range(1, val.shape[1] // width):
                out = jnp.where(q == p, val[:, p * width:(p + 1) * width], out)
            return out

        for k in range(n):
            for idx, r0, nr, cols in where[k]:
                width = w_r[k].shape[-1]
                if cols == "chip" and width * N_CHIPS != dd:
                    g = chip_block(jnp.concatenate([red_ref[pl.ds(r0 + j, 1), :] for j in range(nr)], axis=1), width)
                else:
                    g = red_ref[pl.ds(r0, nr), :]
                    g = chip_block(g, width) if cols == "chip" else g if cols == "all" else g[:, 0:cols]
                delta, nm, nv = _adamw_math(w_r[k][idx], g, m_r[k][idx], v_r[k][idx])
                g_o[k][idx] = g
                d_o[k][idx] = delta
                m_o[k][idx] = nm
                v_o[k][idx] = nv

    vm = pl.BlockSpec(memory_space=pltpu.VMEM)
    outs = pl.pallas_call(
        body, in_specs=[pl.BlockSpec(memory_space=pltpu.SMEM), vm] + [vm] * (3 * n), out_specs=[vm] * (4 * n),
        out_shape=[_sds(w.shape, F32) for w in ws] * 4, name="vector_update")(chip, red, *ws, *ms, *vs)
    return [outs[k * n:(k + 1) * n] for k in range(4)]


def adamw_joined(w, m, v, g_mine, g_theirs, core, tm=512):
    nl, r, c = w.shape
    tm = _tile(r // 2, tm)
    nh = r // 2 // tm

    def body(core_ref, w_ref, m_ref, v_ref, gm_ref, gt_ref, g_ref, d_ref, nm_ref, nv_ref):
        mine = (pl.program_id(1) // nh) == core_ref[0]
        gv = jnp.where(mine, gm_ref[...], gt_ref[...])
        g_ref[...] = gv
        d_ref[...], nm_ref[...], nv_ref[...] = _adamw_math(w_ref[...], gv, m_ref[...], v_ref[...])

    full = pl.BlockSpec((None, tm, c), lambda l, i, cr: (l, i, 0))
    half = pl.BlockSpec((None, tm, c), lambda l, i, cr: (l, i % nh, 0))
    return pl.pallas_call(
        body, grid_spec=pltpu.PrefetchScalarGridSpec(
            num_scalar_prefetch=1, grid=(nl, r // tm), in_specs=[full, full, full, half, half], out_specs=[full] * 4),
        out_shape=[_sds((nl, r, c), F32)] * 4, compiler_params=_params(("parallel", "parallel")),
        name="adamw_joined")(core, w, m, v, g_mine, g_theirs)


WEIGHTS = ['sc_w_in', 'sc_conv_w', 'sc_w_out', 'mla_w_dq', 'mla_g_q', 'mla_w_uq', 'mla_w_dkv', 'mla_g_kv', 'mla_w_uk',
           'mla_w_uv', 'mla_w_o', 'cf_w_pw1', 'cf_b_pw1', 'cf_dw_w', 'cf_dw_b', 'cf_norm_g', 'cf_norm_b', 'cf_w_pw2',
           'cf_b_pw2', 'ff_w1', 'ff_w2', 'ln_mix_g', 'ln_mix_b', 'ln_ff_g', 'ln_ff_b']
ARGS = ['x'] + WEIGHTS + ['loss_target'] + ['m_' + n for n in WEIGHTS] + ['v_' + n for n in WEIGHTS]


def _sq_relu(h):
    r = jnp.maximum(h, jnp.zeros_like(h))
    return r * r


def _mlp_forward(i, x, xb, w1, w2, g, b):
    hb = mm_plain_nn(f"mlp{i}_up", xb, w1, BF16, tn=1024)
    y, yb, xh, rstd = mm_residual_ln(f"mlp{i}_down_ln", hb, w2, x, g, b, tk=4096, a_fn=_sq_relu)
    return (y, yb), dict(xb=xb, hb=hb, xh=xh, rstd=rstd, g=g)


def _mlp_backward(i, dr, drb, sv, w1, w2, dw1, dw2, reduce_after, mixer_ln):
    s = dr.shape[0]
    tm, tn = _tile(s, 1024), 1024

    def epi(acc, e, o):
        o[0][...] = (acc * (2.0 * jnp.maximum(e[0][...].astype(F32), 0.0))).astype(BF16)

    dhb = mm_nt(f"mlp{i}_down_bwd", drb, w2, s, tm, tn, 1024, epi, [_sds((s, w2.k), BF16)], [_ij(tm, tn)],
                [sv["hb"]], [_ij(tm, tn)])[0]
    g_w2 = mm_tn(f"mlp{i}_dw2", sv["hb"], drb, dw2, s, 512, 1024, a_fn=_sq_relu)
    g_w1 = mm_tn(f"mlp{i}_dw1", sv["xb"], dhb, dw1, s, 1024, 512)
    dhb = reduce_after(dhb, {f"w1_{i}": g_w1, f"w2_{i}": g_w2})
    return mm_nt_ln_backward(f"mlp{i}_up_bwd", dhb, w1, dr, *mixer_ln, tk=2048)


def kernel(x, sc_w_in, sc_conv_w, sc_w_out, mla_w_dq, mla_g_q, mla_w_uq, mla_w_dkv, mla_g_kv, mla_w_uk, mla_w_uv, mla_w_o, cf_w_pw1, cf_b_pw1, cf_dw_w, cf_dw_b, cf_norm_g, cf_norm_b, cf_w_pw2, cf_b_pw2, ff_w1, ff_w2, ln_mix_g, ln_mix_b, ln_ff_g, ln_ff_b, loss_target, m_sc_w_in, m_sc_conv_w, m_sc_w_out, m_mla_w_dq, m_mla_g_q, m_mla_w_uq, m_mla_w_dkv, m_mla_g_kv, m_mla_w_uk, m_mla_w_uv, m_mla_w_o, m_cf_w_pw1, m_cf_b_pw1, m_cf_dw_w, m_cf_dw_b, m_cf_norm_g, m_cf_norm_b, m_cf_w_pw2, m_cf_b_pw2, m_ff_w1, m_ff_w2, m_ln_mix_g, m_ln_mix_b, m_ln_ff_g, m_ln_ff_b, v_sc_w_in, v_sc_conv_w, v_sc_w_out, v_mla_w_dq, v_mla_g_q, v_mla_w_uq, v_mla_w_dkv, v_mla_g_kv, v_mla_w_uk, v_mla_w_uv, v_mla_w_o, v_cf_w_pw1, v_cf_b_pw1, v_cf_dw_w, v_cf_dw_b, v_cf_norm_g, v_cf_norm_b, v_cf_w_pw2, v_cf_b_pw2, v_ff_w1, v_ff_w2, v_ln_mix_g, v_ln_mix_b, v_ln_ff_g, v_ln_ff_b):
    given = dict(zip(ARGS, (x, sc_w_in, sc_conv_w, sc_w_out, mla_w_dq, mla_g_q, mla_w_uq, mla_w_dkv, mla_g_kv, mla_w_uk, mla_w_uv, mla_w_o, cf_w_pw1, cf_b_pw1, cf_dw_w, cf_dw_b, cf_norm_g, cf_norm_b, cf_w_pw2, cf_b_pw2, ff_w1, ff_w2, ln_mix_g, ln_mix_b, ln_ff_g, ln_ff_b, loss_target, m_sc_w_in, m_sc_conv_w, m_sc_w_out, m_mla_w_dq, m_mla_g_q, m_mla_w_uq, m_mla_w_dkv, m_mla_g_kv, m_mla_w_uk, m_mla_w_uv, m_mla_w_o, m_cf_w_pw1, m_cf_b_pw1, m_cf_dw_w, m_cf_dw_b, m_cf_norm_g, m_cf_norm_b, m_cf_w_pw2, m_cf_b_pw2, m_ff_w1, m_ff_w2, m_ln_mix_g, m_ln_mix_b, m_ln_ff_g, m_ln_ff_b, v_sc_w_in, v_sc_conv_w, v_sc_w_out, v_mla_w_dq, v_mla_g_q, v_mla_w_uq, v_mla_w_dkv, v_mla_g_kv, v_mla_w_uk, v_mla_w_uv, v_mla_w_o, v_cf_w_pw1, v_cf_b_pw1, v_cf_dw_w, v_cf_dw_b, v_cf_norm_g, v_cf_norm_b, v_cf_w_pw2, v_cf_b_pw2, v_ff_w1, v_ff_w2, v_ln_mix_g, v_ln_mix_b, v_ln_ff_g, v_ln_ff_b)))
    s, d = x.shape[1], x.shape[2]
    d_ff = 4 * d
    dq4 = d // N_CHIPS
    xq = lax.axis_index("x") * 2 + lax.axis_index("y")

    w_dkv_pad = jnp.pad(mla_w_dkv[0], ((0, 0), (0, 128 - QK_ROPE)))
    w_uq_pad = jnp.pad(mla_w_uq[0].reshape(Q_LORA, 2, QK_NOPE + QK_ROPE), ((0, 0), (0, 0), (0, HEAD_PAD - QK_NOPE - QK_ROPE)))
    small = pack_rows("vector_weights_pack", [
        sc_conv_w.reshape(2 * SC_WIDTH, dq4), cf_b_pw1.reshape(2, dq4), cf_dw_w[0], cf_dw_b, cf_norm_g, cf_norm_b,
        cf_b_pw2], 64)
    mlp_w = lambda i: [ff_w1[i].astype(BF16), ff_w2[i].astype(BF16)]
    g_in, g_out, g_w1, g_w2 = [None] * 2, [None] * 2, [None] * DEPTH, [None] * DEPTH
    g_in[0], g_out[0], g_small = gather_shards(
        "gather_mixer0", [sc_w_in[0].astype(BF16), sc_w_out[0].astype(BF16), small], by_columns=(0,))
    (g_w1[0],) = gather_shards("gather_up0", [ff_w1[0].astype(BF16)], by_columns=(0,))
    (g_w2[0],) = gather_shards("gather_down0", [ff_w2[0].astype(BF16)])
    g_dqkv, g_uq, g_uk, g_uv, g_o = gather_shards("gather_mixer1", [
        jnp.concatenate([mla_w_dq[0], w_dkv_pad], axis=1).astype(BF16),
        w_uq_pad.reshape(Q_LORA, 2 * HEAD_PAD).astype(BF16),
        mla_w_uk.reshape(KV_LORA // N_CHIPS, N_HEADS * QK_NOPE).astype(BF16),
        mla_w_uv.reshape(KV_LORA // N_CHIPS, N_HEADS * V_HEAD).astype(BF16), mla_w_o[0].astype(BF16)], by_columns=(1,))
    g_w1[1], g_w2[1] = gather_shards("gather_mlp1", mlp_w(1), by_columns=(0,))
    g_pw1, g_pw2, g_w1[2], g_w2[2] = gather_shards(
        "gather_layer2", [cf_w_pw1[0].astype(BF16), cf_w_pw2[0].astype(BF16)] + mlp_w(2), by_columns=(0, 2))
    g_in[1], g_out[1], g_w1[3], g_w2[3] = gather_shards(
        "gather_layer3", [sc_w_in[1].astype(BF16), sc_w_out[1].astype(BF16)] + mlp_w(3), by_columns=(0, 2))

    wd_t = Q_LORA + KV_LORA + 128
    w_in = [Stk("full", d, 3 * d, g_in[j]) for j in range(2)]
    w_out = [Stk("row", d, d, g_out[j]) for j in range(2)]
    w_dqkv = Stk("row", d, wd_t, g_dqkv)
    w_uq = Stk("full", Q_LORA, N_HEADS * HEAD_PAD, g_uq)
    w_uk = Stk("row", KV_LORA, N_HEADS * QK_NOPE, g_uk)
    w_uv = Stk("row", KV_LORA, N_HEADS * V_HEAD, g_uv)
    w_o = Stk("row", d, d, g_o)
    w_pw1 = Stk("full", d, 2 * d, g_pw1)
    w_pw2 = Stk("row", d, d, g_pw2)
    w_1 = [Stk("full", d, d_ff, g_w1[i]) for i in range(DEPTH)]
    w_2 = [Stk("row", d_ff, d, g_w2[i]) for i in range(DEPTH)]

    def wide(rows):
        return jnp.swapaxes(rows, 0, 1).reshape(rows.shape[1], d)

    conv_w = wide(g_small[:, 0:6]).reshape(2, SC_WIDTH, d)
    b_pw1 = g_small[:, 6:8].reshape(1, 2 * d)
    dw_w = wide(g_small[:, 8:39])
    dw_b, norm_g, norm_b, b_pw2 = (wide(g_small[:, 39 + k:40 + k]) for k in range(4))

    pos = jnp.arange(s, dtype=F32)
    inv_freq = ROPE_THETA ** (-jnp.arange(0, QK_ROPE, 2, dtype=F32) / QK_ROPE)
    ang = pos[:, None] * inv_freq[None, :]
    cos, sin, zero = jnp.cos(ang), jnp.sin(ang), jnp.zeros((s, 128 - QK_ROPE), F32)
    cf = jnp.concatenate([cos, cos, zero], axis=1)
    sf = jnp.concatenate([-sin, sin, zero], axis=1)

    def row(a, i):
        return a[i:i + 1]

    xs = x.reshape(s, d)
    cur = (xs, xs.astype(BF16))
    tape = []
    for i in range(DEPTH):
        mixer, j = i % 3, i // 3
        xf, xb = cur
        lg, lb = row(ln_mix_g, i), row(ln_mix_b, i)
        if mixer == 0:
            u = mm_plain_nn(f"sc{j}_in", xb, w_in[j], F32, tn=3 * dq4)
            gb = short_conv_gate(u, conv_w[j])
            y, yb, xh, rstd = mm_residual_ln(f"sc{j}_out_ln", gb, w_out[j], xf, lg, lb)
            sv = dict(xb=xb, u=u, gb=gb)
        elif mixer == 1:
            t = mm_plain_nn("mla_down", xb, w_dqkv, F32, tn=wd_t // 2)
            cq, ckv, kpe = mla_latents(t, mla_g_q, mla_g_kv, cf, sf)
            qh = mla_queries(cq, w_uq, cf, sf)
            kh = mla_keys(ckv, w_uk, kpe)
            vh = mm_plain_nn("mla_values", ckv, w_uv, BF16, tk=KV_LORA)
            oh = attention(qh, kh, vh)
            y, yb, xh, rstd = mm_residual_ln("mla_out_ln", oh, w_o, xf, lg, lb)
            sv = dict(xb=xb, t=t, cq=cq, ckv=ckv, qh=qh, kh=kh, vh=vh, oh=oh)
        else:
            u = mm_plain_nn("cf_pw1", xb, w_pw1, F32, bias=b_pw1)
            hc = conformer_glu_conv(u, dw_w, dw_b)
            sb = conformer_norm_swish(hc, norm_g, norm_b)
            y, yb, xh, rstd = mm_residual_ln("cf_pw2_ln", sb, w_pw2, xf, lg, lb, bias=b_pw2)
            sv = dict(xb=xb, u=u, hc=hc, sb=sb)
        sv.update(xh=xh, rstd=rstd, g=lg)
        cur, sv_mlp = _mlp_forward(i, y, yb, w_1[i], w_2[i], row(ln_ff_g, i), row(ln_ff_b, i))
        tape.append((sv, sv_mlp))

    g_ln = {n: [None] * DEPTH for n in ("ln_mix_g", "ln_mix_b", "ln_ff_g", "ln_ff_b")}
    last = tape[DEPTH - 1][1]
    dr, drb, g_ln["ln_ff_g"][DEPTH - 1], g_ln["ln_ff_b"][DEPTH - 1], _, loss_part = loss_ln_backward(
        cur[0], loss_target.reshape(s, d), last["xh"], last["rstd"], last["g"])

    grads = {}
    smalls = {}
    conv_grads = [None, None]
    core = lax.axis_index("c").astype(jnp.int32).reshape(1)
    chip = xq.astype(jnp.int32).reshape(1)
    pairs, landed = {}, {}
    ready, theirs = [], {}

    def hold(xs, others):
        live = [x for x in xs if x is not None]
        out = lax.optimization_barrier((*live, *others))
        rest = iter(out[:len(live)])
        return tuple(None if x is None else next(rest) for x in xs), list(out[len(live):])

    def reduce_after(x, new, early=False):
        out = lax.optimization_barrier((x, *new.values()))
        grads.update(zip(new, out[1:]))
        if early:
            theirs.update(zip(new, pair_exchange(f"pair_exchange_{len(theirs)}", list(out[1:]), True)))
        ready.extend(new)
        return out[0]

    def reduce_layer(i, x):
        late = [n for n in ready if n not in theirs]
        if late:
            theirs.update(zip(late, pair_exchange(f"pair_exchange_layer{i}", [grads[n] for n in late], False)))
        sums = [pair_sum(grads[n], theirs[n], core) for n in ready]
        pairs.update(zip(ready, sums))
        landed.update(zip(ready, chip_exchange(f"chip_exchange_layer{i}", sums)))
        exchanged.append(list(ready))
        ready.clear()
        return hold(x, sums)[0]

    groups = [["in_0", "in_1"], ["out_0", "out_1"], ["dqkv"], ["uq"], ["uk"], ["uv"], ["o"], ["pw1"], ["pw2"],
              [f"w1_{i}" for i in range(DEPTH)], [f"w2_{i}" for i in range(DEPTH)]]
    stacks = [None] * len(groups)
    exchanged = []

    def sum_layer(x, last=False):
        names = exchanged.pop(0)
        if last:
            x, held = hold(x, [landed[n] for n in names])
            landed.update(zip(names, held))
        new = []
        for n in names:
            k = next(k for k, members in enumerate(groups) if n in members)
            stacks[k] = chip_sum(pairs[n], landed[n], chip, stacks[k], groups[k].index(n), len(groups[k]))
            new.append(stacks[k])
        return x if last else hold(x, new)[0]

    for i in reversed(range(DEPTH)):
        mixer, j = i % 3, i // 3
        sv, sv_mlp = tape[i]
        dr, drb, g_ln["ln_mix_g"][i], g_ln["ln_mix_b"][i], dr_sum = _mlp_backward(
            i, dr, drb, sv_mlp, w_1[i], w_2[i], Stk("col", d, d_ff), Stk("row", d_ff, d),
            lambda x_, new: reduce_after(x_, new, early=i > 0), (sv["xh"], sv["rstd"], sv["g"]))
        if i == 0:
            dr, drb = reduce_layer("0_mlp", (dr, drb))

        def to_input(name, a, w, tk, a_spec_fn=None):
            if i == 0:
                spec = None if a_spec_fn is None else (s, a_spec_fn)
                return mm_plain_nt(name, a, w, F32, tn=1024, tk=tk, add=dr, add_scale=ALPHA, a_spec_fn=spec), None
            prev = tape[i - 1][1]
            out = mm_nt_ln_backward(name, a, w, dr, prev["xh"], prev["rstd"], prev["g"], tk=tk, a_spec_fn=a_spec_fn)
            g_ln["ln_ff_g"][i - 1], g_ln["ln_ff_b"][i - 1] = out[2], out[3]
            return out[0], out[1]

        parts_of = lambda tm, tk: pl.BlockSpec((None, tm, tk), lambda i_, j_, k_: (k_, i_, 0))
        if mixer == 0:
            dgate = mm_plain_nt(f"sc{j}_out_bwd", drb, w_out[j], F32)
            dw_out = mm_tn(f"sc{j}_dw_out", sv["gb"], drb, Stk("row", d, d), s, 512, 1024)
            du, conv_grads[j] = short_conv_gate_bwd(sv["u"], conv_w[j], dgate)
            nb = d // 256
            dw_in = mm_tn(
                f"sc{j}_dw_in", sv["xb"], du, Stk("col", d, 3 * d), s, 1024, 256,
                b_spec=pl.BlockSpec((None, s, 256), lambda i_, j_, k_: (j_ // nb, k_, j_ % nb)))
            du = reduce_after(du, {f"in_{j}": dw_in, f"out_{j}": dw_out})
            dr, drb = to_input(f"sc{j}_in_bwd", du, w_in[j], d, parts_of)
        elif mixer == 1:
            do = mm_plain_nt("mla_out_bwd", drb, w_o, BF16)
            g_o = mm_tn("mla_dw_o", sv["oh"], drb, Stk("row", d, d), s, 512, 1024)
            dqh, dkh, dvh = attention_bwd(sv["qh"], sv["kh"], sv["vh"], do)
            dql, dkn, dkpe = mla_unrope_grads(dqh, dkh, cf, sf)
            g_uq = mm_tn("mla_dw_uq", sv["cq"], dql, Stk("col", Q_LORA, N_HEADS * HEAD_PAD), s, Q_LORA, 512)
            dcq = mm_plain_nt("mla_uq_bwd", dql, w_uq, F32, tn=Q_LORA)
            g_uk = mm_tn("mla_dw_uk", sv["ckv"], dkn, Stk("row", KV_LORA, N_HEADS * QK_NOPE), s, KV_LORA, 1024)
            g_uv = mm_tn("mla_dw_uv", sv["ckv"], dvh, Stk("row", KV_LORA, N_HEADS * V_HEAD), s, KV_LORA, 1024)
            dckv = mm_plain_nt("mla_uk_bwd", dkn, w_uk, F32, tn=KV_LORA)
            dckv = mm_plain_nt("mla_uv_bwd", dvh, w_uv, F32, tn=KV_LORA, add=dckv)
            dt, smalls["g_q"], smalls["g_kv"] = mla_latents_bwd(sv["t"], mla_g_q, mla_g_kv, cf, sf, dcq, dckv, dkpe)
            g_dqkv = mm_tn("mla_dw_down", sv["xb"], dt, Stk("row", d, wd_t), s, 512, wd_t)
            dt = reduce_after(dt, {"dqkv": g_dqkv, "uq": g_uq, "uk": g_uk, "uv": g_uv, "o": g_o})
            dr, drb = to_input("mla_down_bwd", dt, w_dqkv, wd_t)
        else:
            dsw = mm_plain_nt("cf_pw2_bwd", drb, w_pw2, F32)
            g_pw2 = mm_tn("cf_dw_pw2", sv["sb"], drb, Stk("row", d, d), s, 512, 1024)
            smalls["b_pw2"] = dr_sum
            dhc, smalls["norm_g"], smalls["norm_b"] = conformer_norm_swish_bwd(sv["hc"], norm_g, norm_b, dsw)
            du, smalls["b_pw1"], smalls["dw_w"], smalls["dw_b"] = conformer_glu_conv_bwd(sv["u"], dw_w, dhc)
            nb = d // 512
            g_pw1 = mm_tn(
                "cf_dw_pw1", sv["xb"], du, Stk("col", d, 2 * d), s, 1024, 512,
                b_spec=pl.BlockSpec((None, s, 512), lambda i_, j_, k_: (j_ // nb, k_, j_ % nb)))
            du = reduce_after(du, {"pw1": g_pw1, "pw2": g_pw2})
            dr, drb = to_input("cf_pw1_bwd", du, w_pw1, d, parts_of)
        if i < DEPTH - 1:
            dr, drb = sum_layer((dr, drb))
        dr, drb = reduce_layer(i, (dr, drb))
    grad_x = sum_layer(sum_layer((dr, None), last=True), last=True)[0].reshape(1, s, d)

    mine = stacks
    other = (pair_share("pair_share_mixers", mine[:9]) + pair_share("pair_share_up", mine[9:10])
             + pair_share("pair_share_down", mine[10:]))

    def padded(get):
        dqkv = jnp.concatenate([get("mla_w_dq")[0], jnp.pad(get("mla_w_dkv")[0], ((0, 0), (0, 128 - QK_ROPE)))], axis=1)
        uq = jnp.pad(get("mla_w_uq")[0].reshape(Q_LORA, 2, QK_NOPE + QK_ROPE),
                     ((0, 0), (0, 0), (0, HEAD_PAD - QK_NOPE - QK_ROPE))).reshape(Q_LORA, 2 * HEAD_PAD)
        return [get("sc_w_in"), get("sc_w_out"), dqkv[None], uq[None],
                get("mla_w_uk").reshape(1, KV_LORA // N_CHIPS, d), get("mla_w_uv").reshape(1, KV_LORA // N_CHIPS, d),
                get("mla_w_o"), get("cf_w_pw1"), get("cf_w_pw2"), get("ff_w1"), get("ff_w2")]

    w_l, m_l, v_l = (padded(lambda n, p=p: given[p + n]) for p in ("", "m_", "v_"))
    res = [adamw_joined(w_l[k], m_l[k], v_l[k], mine[k], other[k], core) for k in range(len(groups))]

    def unpadded(k):
        r_in, r_out, r_dqkv, r_uq, r_uk, r_uv, r_o, r_pw1, r_pw2, r_w1, r_w2 = (r[k] for r in res)
        return {
            "sc_w_in": r_in, "sc_w_out": r_out, "mla_w_dq": r_dqkv[:, :, 0:Q_LORA],
            "mla_w_dkv": r_dqkv[:, :, Q_LORA:Q_LORA + KV_LORA + QK_ROPE],
            "mla_w_uq": r_uq.reshape(1, Q_LORA, 2, HEAD_PAD)[:, :, :, 0:QK_NOPE + QK_ROPE].reshape(mla_w_uq.shape),
            "mla_w_uk": r_uk.reshape(mla_w_uk.shape), "mla_w_uv": r_uv.reshape(mla_w_uv.shape),
            "mla_w_o": r_o, "cf_w_pw1": r_pw1, "cf_w_pw2": r_pw2, "ff_w1": r_w1, "ff_w2": r_w2}

    big_g, big_d, big_m, big_v = (unpadded(k) for k in range(4))

    pad_row = lambda a: jnp.pad(a, ((0, 0), (0, d - a.shape[1])))
    small_parts = ([g for n in ("ln_mix_g", "ln_mix_b", "ln_ff_g", "ln_ff_b") for g in g_ln[n]]
                   + [pad_row(smalls["g_q"]), pad_row(smalls["g_kv"]), conv_grads[0], conv_grads[1],
                      smalls["b_pw1"].reshape(2, d), smalls["dw_w"], smalls["dw_b"], smalls["norm_g"], smalls["norm_b"],
                      smalls["b_pw2"], loss_part])
    red = all_reduce_small(small_parts, 64)
    loss = red[61, 0]

    where = {
        "ln_mix_g": [((), 0, DEPTH, "all")], "ln_mix_b": [((), 4, DEPTH, "all")],
        "ln_ff_g": [((), 8, DEPTH, "all")], "ln_ff_b": [((), 12, DEPTH, "all")],
        "mla_g_q": [((), 16, 1, Q_LORA)], "mla_g_kv": [((), 17, 1, KV_LORA)],
        "sc_conv_w": [((0,), 18, SC_WIDTH, "chip"), ((1,), 21, SC_WIDTH, "chip")],
        "cf_b_pw1": [((), 24, 2, "chip")], "cf_dw_w": [((0,), 26, CONF_WIDTH, "chip")],
        "cf_dw_b": [((), 57, 1, "chip")], "cf_norm_g": [((), 58, 1, "chip")], "cf_norm_b": [((), 59, 1, "chip")],
        "cf_b_pw2": [((), 60, 1, "chip")]}
    vec = list(where)
    vec_res = vector_update(red, chip, [given[n] for n in vec], [given["m_" + n] for n in vec],
                            [given["v_" + n] for n in vec], [where[n] for n in vec])
    gw = dict(big_g)
    upd = {n: [big_d[n], big_m[n], big_v[n]] for n in big_g}
    for k, n in enumerate(vec):
        gw[n] = vec_res[0][k]
        upd[n] = [vec_res[1][k], vec_res[2][k], vec_res[3][k]]

    return (loss, grad_x, *[gw[n] for n in WEIGHTS], *[upd[n][0] for n in WEIGHTS],
            *[upd[n][1] for n in WEIGHTS], *[upd[n][2] for n in WEIGHTS])
```

```python
import jax
import jax.numpy as jnp
from jax import lax
from jax.experimental import pallas as pl
from jax.experimental.pallas import tpu as pltpu
from jax.experimental.pallas import tpu_sc as plsc

F32 = jnp.float32
BF16 = jnp.bfloat16
MESH = pl.DeviceIdType.MESH

DEPTH = 4
ALPHA = (2.0 * DEPTH) ** 0.25
LN_EPS = 1e-5
RMS_EPS = 1e-6
CHUNK_SHIFT = 6
N_HEADS = 8
QK_NOPE = 128
QK_ROPE = 64
V_HEAD = 128
HEAD_PAD = 256
Q_LORA = 384
KV_LORA = 256
ROPE_THETA = 10000.0
SC_WIDTH = 3
CONF_WIDTH = 31
CONV_PAD = 32
CONV_CHUNK = 64
N_CHIPS = 4
ATTN_SCALE = (QK_NOPE + QK_ROPE) ** -0.5

ADAM_LR = 0.001
ADAM_B1 = 0.9
ADAM_B2 = 0.999
ADAM_EPS = 1e-08
ADAM_WD = 0.01
ADAM_STEP = 10

VMEM_LIMIT = 56 * 2**20

NN = (((1,), (0,)), ((), ()))
NT = (((1,), (1,)), ((), ()))
TN = (((0,), (0,)), ((), ()))


def _params(sem=None):
    return pltpu.CompilerParams(dimension_semantics=sem, vmem_limit_bytes=VMEM_LIMIT)


class Stk:
    def __init__(self, kind, k, n, arr=None, layers=None, layer=None):
        self.kind, self.k, self.n, self.layers, self.layer = kind, k, n, layers, layer
        self.plain = (kind == "row" and layers is None) or kind == "full"
        self.kloc = k // N_CHIPS if kind == "row" else k
        self.nloc = n // N_CHIPS if kind == "col" else n
        if arr is not None and self.plain:
            arr = arr.reshape(k, n)
        self.arr = arr

    @property
    def shape(self):
        if self.plain:
            return (self.k, self.n)
        lead = (N_CHIPS,) if self.layers is None else (N_CHIPS, self.layers)
        return lead + (self.kloc, self.nloc)

    def spec(self, bk, bn, f):
        if self.plain:
            return pl.BlockSpec((bk, bn), f)
        assert self.kloc % bk == 0 and self.nloc % bn == 0, (self.kloc, bk, self.nloc, bn)
        pk, pn = self.kloc // bk, self.nloc // bn
        kind, layer = self.kind, self.layer

        def imap(*g):
            kb, nb = f(*g)
            if kind == "row":
                q, kb, nb = kb // pk, kb % pk, nb
            else:
                q, kb, nb = nb // pn, kb, nb % pn
            return (q, kb, nb) if layer is None else (q, layer, kb, nb)

        block = (None, bk, bn) if layer is None else (None, None, bk, bn)
        return pl.BlockSpec(block, imap)


def _mm(name, mode, a, b, grid, a_spec, b_spec, acc_shape, extras, extra_specs, out_shapes, out_specs, epi, a_fn=None,
        rows_in_order=False):
    nk = grid[2]
    ne = len(extras)

    def body(*refs):
        a_ref, b_ref = refs[0], refs[1]
        e_refs = refs[2:2 + ne]
        av = a_ref[...] if a_fn is None else a_fn(a_ref[...])
        part = lax.dot_general(av, b_ref[...], mode, preferred_element_type=F32)
        if nk == 1:
            epi(part, e_refs, refs[2 + ne:])
            return
        o_refs = refs[2 + ne:-1]
        acc = refs[-1]
        k = pl.program_id(2)

        @pl.when(k == 0)
        def _():
            acc[...] = part

        @pl.when(k > 0)
        def _():
            acc[...] += part

        @pl.when(k == nk - 1)
        def _():
            epi(acc[...], e_refs, o_refs)

    return pl.pallas_call(
        body, grid=grid, in_specs=[a_spec, b_spec, *extra_specs], out_specs=out_specs, out_shape=out_shapes,
        scratch_shapes=[pltpu.VMEM(acc_shape, F32)] if nk > 1 else [],
        compiler_params=_params(("arbitrary",) * 3 if rows_in_order else ("parallel", "parallel", "arbitrary")),
        name=name)(a, b, *extras)


def _tile(n, t):
    t = min(n, t)
    while n % t:
        t -= 8
    assert t > 0, (n, t)
    return t


def mm_nn(name, a, w, tm, tn, tk, epi, out_shapes, out_specs, extras=(), extra_specs=(), a_spec=None, a_fn=None):
    m = a.shape[0]
    tm, tn, tk = _tile(m, tm), _tile(w.n, tn), _tile(w.k, tk)
    grid = (m // tm, w.n // tn, w.k // tk)
    a_spec = a_spec or pl.BlockSpec((tm, tk), lambda i, j, k: (i, k))
    b_spec = w.spec(tk, tn, lambda i, j, k: (k, j))
    return _mm(name, NN, a, w.arr, grid, a_spec, b_spec, (tm, tn), extras, extra_specs, out_shapes, out_specs, epi, a_fn)


def mm_nt(name, a, w, m, tm, tn, tk, epi, out_shapes, out_specs, extras=(), extra_specs=(), a_spec=None,
          rows_in_order=False):
    tm, tn, tk = _tile(m, tm), _tile(w.k, tn), _tile(w.n, tk)
    grid = (m // tm, w.k // tn, w.n // tk)
    a_spec = a_spec or pl.BlockSpec((tm, tk), lambda i, j, k: (i, k))
    b_spec = w.spec(tn, tk, lambda i, j, k: (j, k))
    return _mm(name, NT, a, w.arr, grid, a_spec, b_spec, (tm, tn), extras, extra_specs, out_shapes, out_specs, epi,
               rows_in_order=rows_in_order)


def mm_tn(name, a, b, dw, s, tm=512, tn=512, tk=4096, a_spec=None, b_spec=None, a_fn=None):
    tm, tn, tk = _tile(dw.k, tm), _tile(dw.n, tn), _tile(s, tk)
    grid = (dw.k // tm, dw.n // tn, s // tk)
    a_spec = a_spec or pl.BlockSpec((tk, tm), lambda i, j, k: (k, i))
    b_spec = b_spec or pl.BlockSpec((tk, tn), lambda i, j, k: (k, j))

    def epi(acc, e, o):
        o[0][...] = acc.astype(BF16)

    out = _mm(name, TN, a, b, grid, a_spec, b_spec, (tm, tn), (), (), [jax.ShapeDtypeStruct(dw.shape, BF16)],
              [dw.spec(tm, tn, lambda i, j, k: (i, j))], epi, a_fn)[0]
    return out.reshape(N_CHIPS, dw.k // N_CHIPS, dw.n) if dw.plain else out


def _sds(shape, dtype):
    return jax.ShapeDtypeStruct(shape, dtype)


def _ij(tm, tn):
    return pl.BlockSpec((tm, tn), lambda i, j, k: (i, j))


def _i0(tm, c):
    return pl.BlockSpec((tm, c), lambda i, j, k: (i, 0))


def _0j(r, tn):
    return pl.BlockSpec((r, tn), lambda i, j, k: (0, j))


def _layer_norm_rows(r, g, b):
    mu = jnp.mean(r, axis=-1, keepdims=True)
    d = r - mu
    var = jnp.mean(d * d, axis=-1, keepdims=True)
    rstd = lax.rsqrt(var + LN_EPS)
    xh = d * rstd
    return xh * g + b, xh, rstd


def mm_residual_ln(name, a, w, x, g, b, bias=None, tm=512, tk=1024, a_fn=None):
    s, d = x.shape
    tm = _tile(s, tm)
    extras = [x, g, b] + ([bias] if bias is not None else [])
    especs = [_i0(tm, d), _0j(1, d), _0j(1, d)] + ([_0j(1, d)] if bias is not None else [])

    def epi(acc, e, o):
        r = ALPHA * e[0][...] + acc
        if bias is not None:
            r = r + e[3][...]
        y, xh, rstd = _layer_norm_rows(r, e[1][...], e[2][...])
        o[0][...] = y
        o[1][...] = y.astype(BF16)
        o[2][...] = xh
        o[3][...] = rstd

    return mm_nn(name, a, w, tm, d, tk, epi,
                 [_sds((s, d), F32), _sds((s, d), BF16), _sds((s, d), F32), _sds((s, 1), F32)],
                 [_i0(tm, d), _i0(tm, d), _i0(tm, d), _i0(tm, 1)], extras, especs, a_fn=a_fn)


def mm_plain_nn(name, a, w, out_dtype, tm=1024, tn=512, tk=1024, bias=None):
    m = a.shape[0]
    tm, tn = _tile(m, tm), _tile(w.n, tn)
    if w.kind == "col":
        tn = _tile(w.nloc, tn)

    def epi(acc, e, o):
        if bias is not None:
            acc = acc + e[0][...]
        o[0][...] = acc.astype(out_dtype)

    extras, especs = ([bias], [_0j(1, tn)]) if bias is not None else ((), ())
    return mm_nn(name, a, w, tm, tn, tk, epi, [_sds((m, w.n), out_dtype)], [_ij(tm, tn)], extras, especs)[0]


def mm_plain_nt(name, a, w, out_dtype, tm=1024, tn=512, tk=1024, add=None, add_scale=1.0, a_spec_fn=None):
    m = a.shape[0] if a_spec_fn is None else a_spec_fn[0]
    tm, tn = _tile(m, tm), _tile(w.k, tn)
    tk = _tile(w.n, tk)
    if w.kind == "col":
        tk = _tile(w.nloc, tk)
    if w.kind == "row" and not w.plain:
        tn = _tile(w.kloc, tn)

    def epi(acc, e, o):
        if add is not None:
            acc = acc + add_scale * e[0][...].astype(F32)
        o[0][...] = acc.astype(out_dtype)

    extras, especs = ([add], [_ij(tm, tn)]) if add is not None else ((), ())
    a_spec = None if a_spec_fn is None else a_spec_fn[1](tm, tk)
    return mm_nt(name, a, w, m, tm, tn, tk, epi, [_sds((m, w.k), out_dtype)], [_ij(tm, tn)], extras, especs,
                 a_spec=a_spec)[0]


def _rows(tm, c):
    return pl.BlockSpec((tm, c), lambda i: (i, 0))


def _fix(shape):
    nd = len(shape)
    return pl.BlockSpec(shape, lambda i: (0,) * nd)


def _accumulate(ref, val):
    @pl.when(pl.program_id(0) == 0)
    def _():
        ref[...] = jnp.zeros_like(ref)

    ref[...] += val


def _ln_backward_rows(dyv, xh, rstd, g, dr_ref, drb_ref, dg_ref, db_ref, ds_ref):
    dxh = dyv * g
    m1 = jnp.mean(dxh, axis=-1, keepdims=True)
    m2 = jnp.mean(dxh * xh, axis=-1, keepdims=True)
    dr = rstd * (dxh - m1 - xh * m2)
    dr_ref[...] = dr
    drb_ref[...] = dr.astype(BF16)
    _accumulate(dg_ref, jnp.sum(dyv * xh, axis=0, keepdims=True))
    _accumulate(db_ref, jnp.sum(dyv, axis=0, keepdims=True))
    _accumulate(ds_ref, jnp.sum(dr, axis=0, keepdims=True))


def mm_nt_ln_backward(name, a, w, add, xhat, rstd, g, tm=512, tk=1024, a_spec_fn=None):
    m, d = add.shape
    tm, tk = _tile(m, tm), _tile(w.n, tk)

    def epi(acc, e, o):
        _ln_backward_rows(acc + ALPHA * e[0][...], e[1][...], e[2][...], e[3][...], *o)

    vec = pl.BlockSpec((1, d), lambda i, j, k: (0, 0))
    a_spec = None if a_spec_fn is None else a_spec_fn(tm, tk)
    return mm_nt(name, a, w, m, tm, d, tk, epi,
                 [_sds((m, d), F32), _sds((m, d), BF16), _sds((1, d), F32), _sds((1, d), F32), _sds((1, d), F32)],
                 [_i0(tm, d), _i0(tm, d), vec, vec, vec], [add, xhat, rstd, g],
                 [_i0(tm, d), _i0(tm, d), _i0(tm, 1), vec], a_spec=a_spec, rows_in_order=True)


def loss_ln_backward(y, target, xhat, rstd, g, tm=512):
    s, d = y.shape
    tm = _tile(s, tm)

    def body(y_ref, t_ref, xh_ref, rstd_ref, g_ref, dr_ref, drb_ref, dg_ref, db_ref, ds_ref, loss_ref):
        e = y_ref[...] - t_ref[...]
        part = 0.5 * jnp.sum(jnp.mean(e * e, axis=-1, keepdims=True), axis=0, keepdims=True)
        _accumulate(loss_ref, jnp.broadcast_to(part, (1, d)))
        _ln_backward_rows(e * (1.0 / d), xh_ref[...], rstd_ref[...], g_ref[...], dr_ref, drb_ref, dg_ref, db_ref, ds_ref)

    return pl.pallas_call(
        body, grid=(s // tm,),
        in_specs=[_rows(tm, d), _rows(tm, d), _rows(tm, d), _rows(tm, 1), _fix((1, d))],
        out_specs=[_rows(tm, d), _rows(tm, d)] + [_fix((1, d))] * 4,
        out_shape=[_sds((s, d), F32), _sds((s, d), BF16)] + [_sds((1, d), F32)] * 4,
        compiler_params=_params(("arbitrary",)), name="loss_ln_backward")(y, target, xhat, rstd, g)


def _cols(s, tc, off=0):
    return pl.BlockSpec((s, tc), lambda i: (0, i + off))


def _shift_down(z, sft, rows):
    return jnp.where(rows >= sft, pltpu.roll(z, sft, 0), 0.0)


def _shift_up(z, sft, rows, s):
    return jnp.where(rows < s - sft, pltpu.roll(z, (s - sft) % s, 0), 0.0)


def short_conv_gate(u, conv_w, tc=256):
    s, d3 = u.shape
    d = d3 // 3
    nb = d // tc

    def body(b_ref, c_ref, h_ref, w_ref, o_ref):
        rows = lax.broadcasted_iota(jnp.int32, (s, tc), 0)
        z = c_ref[...] * h_ref[...]
        cz = jnp.zeros((s, tc), F32)
        for k in range(SC_WIDTH):
            sft = SC_WIDTH - 1 - k
            cz = cz + w_ref[pl.ds(k, 1), :] * (_shift_down(z, sft, rows) if sft else z)
        o_ref[...] = (b_ref[...] * cz).astype(BF16)

    return pl.pallas_call(
        body, grid=(nb,),
        in_specs=[_cols(s, tc), _cols(s, tc, nb), _cols(s, tc, 2 * nb), _cols(SC_WIDTH, tc)],
        out_specs=_cols(s, tc), out_shape=_sds((s, d), BF16),
        compiler_params=_params(("parallel",)), name="short_conv_gate")(u, u, u, conv_w)


def short_conv_gate_bwd(u, conv_w, dg, tc=256):
    s, d3 = u.shape
    d = d3 // 3
    nb = d // tc

    def body(b_ref, c_ref, h_ref, w_ref, dg_ref, du_ref, dw_ref):
        rows = lax.broadcasted_iota(jnp.int32, (s, tc), 0)
        c, h, dgv = c_ref[...], h_ref[...], dg_ref[...]
        z = c * h
        dcz = dgv * b_ref[...]
        cz = jnp.zeros((s, tc), F32)
        dz = jnp.zeros((s, tc), F32)
        for k in range(SC_WIDTH):
            sft = SC_WIDTH - 1 - k
            zs = _shift_down(z, sft, rows) if sft else z
            wk = w_ref[pl.ds(k, 1), :]
            cz = cz + wk * zs
            dz = dz + wk * (_shift_up(dcz, sft, rows, s) if sft else dcz)
            dw_ref[pl.ds(k, 1), :] = jnp.sum(dcz * zs, axis=0, keepdims=True)
        du_ref[0] = (dgv * cz).astype(BF16)
        du_ref[1] = (dz * h).astype(BF16)
        du_ref[2] = (dz * c).astype(BF16)

    return pl.pallas_call(
        body, grid=(nb,),
        in_specs=[_cols(s, tc), _cols(s, tc, nb), _cols(s, tc, 2 * nb), _cols(SC_WIDTH, tc), _cols(s, tc)],
        out_specs=[pl.BlockSpec((3, s, tc), lambda i: (0, 0, i)), _cols(SC_WIDTH, tc)],
        out_shape=[_sds((3, s, d), BF16), _sds((SC_WIDTH, d), F32)],
        compiler_params=_params(("parallel",)), name="short_conv_gate_bwd")(u, u, u, conv_w, dg)


def _store_shifted_down(ref, z, rows):
    s, tc = z.shape
    for b in range(8):
        ref[b, pl.ds(0, CONV_PAD), :] = jnp.zeros((CONV_PAD, tc), F32)
        ref[b, pl.ds(CONV_PAD, s), :] = z if b == 0 else _shift_down(z, b, rows)


def _store_shifted_up(ref, z, rows):
    s, tc = z.shape
    for b in range(8):
        ref[b, pl.ds(0, s), :] = z if b == 0 else _shift_up(z, b, rows, s)
        ref[b, pl.ds(s, CONV_PAD), :] = jnp.zeros((CONV_PAD, tc), F32)


def conformer_glu_conv(u, dw_w, dw_b, tc=128):
    s, d2 = u.shape
    d = d2 // 2
    nb = d // tc

    ch = min(CONV_CHUNK, s)

    def body(a_ref, g_ref, w_ref, b_ref, o_ref, down):
        rows = lax.broadcasted_iota(jnp.int32, (s, tc), 0)
        _store_shifted_down(down, a_ref[...] * jax.nn.sigmoid(g_ref[...]), rows)

        def chunk(ci, carry):
            r0 = pl.multiple_of(ci * ch, ch)
            acc = jnp.broadcast_to(b_ref[...], (ch, tc))
            for k in range(CONF_WIDTH):
                sft = CONF_WIDTH - 1 - k
                acc = acc + w_ref[pl.ds(k, 1), :] * down[sft % 8, pl.ds(CONV_PAD + r0 - (sft // 8) * 8, ch), :]
            o_ref[pl.ds(r0, ch), :] = acc
            return carry

        lax.fori_loop(0, s // ch, chunk, 0)

    return pl.pallas_call(
        body, grid=(nb,),
        in_specs=[_cols(s, tc), _cols(s, tc, nb), _cols(CONF_WIDTH, tc), _cols(1, tc)],
        out_specs=_cols(s, tc), out_shape=_sds((s, d), F32),
        scratch_shapes=[pltpu.VMEM((8, CONV_PAD + s, tc), F32)],
        compiler_params=_params(("parallel",)), name="conformer_glu_conv")(u, u, dw_w, dw_b)


def conformer_glu_conv_bwd(u, dw_w, dhc, tc=128):
    s, d2 = u.shape
    d = d2 // 2
    nb = d // tc
    ch = min(CONV_CHUNK, s)

    def body(a_ref, g_ref, w_ref, dhc_ref, du_ref, dbias_ref, dw_ref, db_ref, down, up, dw_acc, dh_buf):
        rows = lax.broadcasted_iota(jnp.int32, (s, tc), 0)
        a = a_ref[...]
        sg = jax.nn.sigmoid(g_ref[...])
        dhcv = dhc_ref[...]
        _store_shifted_down(down, a * sg, rows)
        _store_shifted_up(up, dhcv, rows)
        dw_acc[...] = jnp.zeros_like(dw_acc)

        def chunk(ci, carry):
            r0 = pl.multiple_of(ci * ch, ch)
            dc = dhc_ref[pl.ds(r0, ch), :]
            dh = jnp.zeros((ch, tc), F32)
            for k in range(CONF_WIDTH):
                sft = CONF_WIDTH - 1 - k
                a8, b = (sft // 8) * 8, sft % 8
                dh = dh + w_ref[pl.ds(k, 1), :] * up[b, pl.ds(r0 + a8, ch), :]
                prod = dc * down[b, pl.ds(CONV_PAD + r0 - a8, ch), :]
                dw_acc[k] += jnp.sum(prod.reshape(ch // 8, 8, tc), axis=0)
            dh_buf[pl.ds(r0, ch), :] = dh
            return carry

        lax.fori_loop(0, s // ch, chunk, 0)
        dh = dh_buf[...]
        da = dh * sg
        dgate = dh * a * sg * (1.0 - sg)
        du_ref[0] = da.astype(BF16)
        du_ref[1] = dgate.astype(BF16)
        dbias_ref[pl.ds(0, 1), :] = jnp.sum(da, axis=0, keepdims=True)
        dbias_ref[pl.ds(1, 1), :] = jnp.sum(dgate, axis=0, keepdims=True)
        db_ref[...] = jnp.sum(dhcv, axis=0, keepdims=True)
        for k in range(CONF_WIDTH):
            dw_ref[pl.ds(k, 1), :] = jnp.sum(dw_acc[k], axis=0, keepdims=True)

    return pl.pallas_call(
        body, grid=(nb,),
        in_specs=[_cols(s, tc), _cols(s, tc, nb), _cols(CONF_WIDTH, tc), _cols(s, tc)],
        out_specs=[pl.BlockSpec((2, s, tc), lambda i: (0, 0, i)), _cols(2, tc), _cols(CONF_WIDTH, tc), _cols(1, tc)],
        out_shape=[_sds((2, s, d), BF16), _sds((2, d), F32), _sds((CONF_WIDTH, d), F32), _sds((1, d), F32)],
        scratch_shapes=[pltpu.VMEM((8, CONV_PAD + s, tc), F32), pltpu.VMEM((8, CONV_PAD + s, tc), F32),
                        pltpu.VMEM((CONF_WIDTH + 1, 8, tc), F32), pltpu.VMEM((s, tc), F32)],
        compiler_params=_params(("parallel",)), name="conformer_glu_conv_bwd")(u, u, dw_w, dhc)


def conformer_norm_swish(hc, g, b, tm=512):
    s, d = hc.shape
    tm = _tile(s, tm)

    def body(h_ref, g_ref, b_ref, o_ref):
        n, _, _ = _layer_norm_rows(h_ref[...], g_ref[...], b_ref[...])
        o_ref[...] = (n * jax.nn.sigmoid(n)).astype(BF16)

    return pl.pallas_call(
        body, grid=(s // tm,), in_specs=[_rows(tm, d), _fix((1, d)), _fix((1, d))], out_specs=_rows(tm, d),
        out_shape=_sds((s, d), BF16), compiler_params=_params(("parallel",)), name="conformer_norm_swish")(hc, g, b)


def conformer_norm_swish_bwd(hc, g, b, ds, tm=512):
    s, d = hc.shape
    tm = _tile(s, tm)

    def body(h_ref, g_ref, b_ref, ds_ref, dh_ref, dg_ref, db_ref):
        n, nh, rstd = _layer_norm_rows(h_ref[...], g_ref[...], b_ref[...])
        sg = jax.nn.sigmoid(n)
        dn = ds_ref[...] * (sg * (1.0 + n * (1.0 - sg)))
        dnh = dn * g_ref[...]
        m1 = jnp.mean(dnh, axis=-1, keepdims=True)
        m2 = jnp.mean(dnh * nh, axis=-1, keepdims=True)
        dh_ref[...] = rstd * (dnh - m1 - nh * m2)
        _accumulate(dg_ref, jnp.sum(dn * nh, axis=0, keepdims=True))
        _accumulate(db_ref, jnp.sum(dn, axis=0, keepdims=True))

    return pl.pallas_call(
        body, grid=(s // tm,), in_specs=[_rows(tm, d), _fix((1, d)), _fix((1, d)), _rows(tm, d)],
        out_specs=[_rows(tm, d), _fix((1, d)), _fix((1, d))],
        out_shape=[_sds((s, d), F32), _sds((1, d), F32), _sds((1, d), F32)],
        compiler_params=_params(("arbitrary",)), name="conformer_norm_swish_bwd")(hc, g, b, ds)


def _swap_halves(x):
    lane = lax.broadcasted_iota(jnp.int32, x.shape, 1)
    return jnp.where(lane < QK_ROPE // 2, pltpu.roll(x, 128 - QK_ROPE // 2, 1), pltpu.roll(x, QK_ROPE // 2, 1))


def _rope(x, cf, sf):
    return x * cf + _swap_halves(x) * sf


def _unrope(dx, cf, sf):
    return dx * cf - _swap_halves(dx) * sf


def _rms_rows(x, g):
    r = lax.rsqrt(jnp.mean(x * x, axis=-1, keepdims=True) + RMS_EPS)
    return x * r, r


def mla_latents(t, g_q, g_kv, cf, sf, tm=512):
    s = t.shape[0]
    tm = _tile(s, tm)

    def body(t_ref, gq_ref, gkv_ref, cf_ref, sf_ref, cq_ref, ckv_ref, kpe_ref):
        xq, _ = _rms_rows(t_ref[:, 0:Q_LORA], gq_ref[...])
        cq_ref[...] = (xq * gq_ref[...]).astype(BF16)
        xkv, _ = _rms_rows(t_ref[:, Q_LORA:Q_LORA + KV_LORA], gkv_ref[...])
        ckv_ref[...] = (xkv * gkv_ref[...]).astype(BF16)
        kpe_ref[...] = _rope(t_ref[:, Q_LORA + KV_LORA:], cf_ref[...], sf_ref[...]).astype(BF16)

    w = Q_LORA + KV_LORA + 128
    return pl.pallas_call(
        body, grid=(s // tm,),
        in_specs=[_rows(tm, w), _fix((1, Q_LORA)), _fix((1, KV_LORA)), _rows(tm, 128), _rows(tm, 128)],
        out_specs=[_rows(tm, Q_LORA), _rows(tm, KV_LORA), _rows(tm, 128)],
        out_shape=[_sds((s, Q_LORA), BF16), _sds((s, KV_LORA), BF16), _sds((s, 128), BF16)],
        compiler_params=_params(("parallel",)), name="mla_latents")(t, g_q, g_kv, cf, sf)


def mla_latents_bwd(t, g_q, g_kv, cf, sf, dcq, dckv, dkpe, tm=512):
    s = t.shape[0]
    tm = _tile(s, tm)
    w = Q_LORA + KV_LORA + 128

    def rms_bwd(x, g, dy):
        xh, r = _rms_rows(x, g)
        dxh = dy * g
        return r * (dxh - xh * jnp.mean(dxh * xh, axis=-1, keepdims=True)), jnp.sum(dy * xh, axis=0, keepdims=True)

    def body(t_ref, gq_ref, gkv_ref, cf_ref, sf_ref, dcq_ref, dckv_ref, dkpe_ref, dt_ref, dgq_ref, dgkv_ref):
        dxq, dgq = rms_bwd(t_ref[:, 0:Q_LORA], gq_ref[...], dcq_ref[...])
        dxkv, dgkv = rms_bwd(t_ref[:, Q_LORA:Q_LORA + KV_LORA], gkv_ref[...], dckv_ref[...])
        dt_ref[:, 0:Q_LORA] = dxq.astype(BF16)
        dt_ref[:, Q_LORA:Q_LORA + KV_LORA] = dxkv.astype(BF16)
        dt_ref[:, Q_LORA + KV_LORA:] = _unrope(dkpe_ref[...], cf_ref[...], sf_ref[...]).astype(BF16)
        _accumulate(dgq_ref, dgq)
        _accumulate(dgkv_ref, dgkv)

    return pl.pallas_call(
        body, grid=(s // tm,),
        in_specs=[_rows(tm, w), _fix((1, Q_LORA)), _fix((1, KV_LORA)), _rows(tm, 128), _rows(tm, 128),
                  _rows(tm, Q_LORA), _rows(tm, KV_LORA), _rows(tm, 128)],
        out_specs=[_rows(tm, w), _fix((1, Q_LORA)), _fix((1, KV_LORA))],
        out_shape=[_sds((s, w), BF16), _sds((1, Q_LORA), F32), _sds((1, KV_LORA), F32)],
        compiler_params=_params(("arbitrary",)), name="mla_latents_bwd")(t, g_q, g_kv, cf, sf, dcq, dckv, dkpe)


def mla_queries(cq, w_uq, cf, sf, tm=2048):
    s = cq.shape[0]
    tm = _tile(s, tm)

    def epi(acc, e, o):
        o[0][:, 0:QK_NOPE] = acc[:, 0:QK_NOPE].astype(BF16)
        o[0][:, QK_NOPE:] = _rope(acc[:, QK_NOPE:], e[0][...], e[1][...]).astype(BF16)

    return mm_nn("mla_queries", cq, w_uq, tm, HEAD_PAD, Q_LORA, epi, [_sds((s, N_HEADS * HEAD_PAD), BF16)],
                 [_ij(tm, HEAD_PAD)], [cf, sf], [_i0(tm, 128), _i0(tm, 128)])[0]


def mla_keys(ckv, w_uk, kpe, tm=2048):
    s = ckv.shape[0]
    tm = _tile(s, tm)

    def epi(acc, e, o):
        o[0][:, 0:QK_NOPE] = acc.astype(BF16)
        o[0][:, QK_NOPE:] = e[0][...]

    return mm_nn("mla_keys", ckv, w_uk, tm, QK_NOPE, KV_LORA, epi, [_sds((s, N_HEADS * HEAD_PAD), BF16)],
                 [_ij(tm, HEAD_PAD)], [kpe], [_i0(tm, 128)])[0]


def _masked_scores(q, k, qi, tq, kv):
    sc = lax.dot_general(q, k, NT, preferred_element_type=F32) * ATTN_SCALE
    row = lax.broadcasted_iota(jnp.int32, (tq, kv), 0) + qi * tq
    col = lax.broadcasted_iota(jnp.int32, (tq, kv), 1)
    ok = lax.shift_right_logical(col, CHUNK_SHIFT) <= lax.shift_right_logical(row, CHUNK_SHIFT)
    return jnp.where(ok, sc, -1e30)


def attention(q, k, v, tq=512):
    s = q.shape[0]
    tq = _tile(s, tq)
    nq = s // tq

    def body(q_ref, k_ref, v_ref, o_ref):
        for qi in range(nq):
            kv = (qi + 1) * tq
            sc = _masked_scores(q_ref[pl.ds(qi * tq, tq), :], k_ref[pl.ds(0, kv), :], qi, tq, kv)
            p = jnp.exp(sc - jnp.max(sc, axis=-1, keepdims=True))
            o = lax.dot_general(p.astype(BF16), v_ref[pl.ds(0, kv), :], NN, preferred_element_type=F32)
            o_ref[pl.ds(qi * tq, tq), :] = (o / jnp.sum(p, axis=-1, keepdims=True)).astype(BF16)

    hq = pl.BlockSpec((s, HEAD_PAD), lambda h: (0, h))
    hv = pl.BlockSpec((s, V_HEAD), lambda h: (0, h))
    return pl.pallas_call(
        body, grid=(N_HEADS,), in_specs=[hq, hq, hv], out_specs=hv, out_shape=_sds((s, N_HEADS * V_HEAD), BF16),
        compiler_params=_params(("parallel",)), name="attention")(q, k, v)


def attention_bwd(q, k, v, do, tq=512):
    s = q.shape[0]
    tq = _tile(s, tq)
    nq = s // tq

    def body(q_ref, k_ref, v_ref, do_ref, dq_ref, dk_ref, dv_ref, dk_acc, dv_acc):
        dk_acc[...] = jnp.zeros_like(dk_acc)
        dv_acc[...] = jnp.zeros_like(dv_acc)
        for qi in range(nq):
            kv = (qi + 1) * tq
            qt = q_ref[pl.ds(qi * tq, tq), :]
            kt = k_ref[pl.ds(0, kv), :]
            dot = do_ref[pl.ds(qi * tq, tq), :]
            sc = _masked_scores(qt, kt, qi, tq, kv)
            p = jnp.exp(sc - jnp.max(sc, axis=-1, keepdims=True))
            p = p / jnp.sum(p, axis=-1, keepdims=True)
            dp = lax.dot_general(dot, v_ref[pl.ds(0, kv), :], NT, preferred_element_type=F32)
            delta = jnp.sum(p * dp, axis=-1, keepdims=True)
            ds = (p * (dp - delta) * ATTN_SCALE).astype(BF16)
            dq_ref[pl.ds(qi * tq, tq), :] = lax.dot_general(ds, kt, NN, preferred_element_type=F32).astype(BF16)
            dk_acc[pl.ds(0, kv), :] += lax.dot_general(ds, qt, TN, preferred_element_type=F32)
            dv_acc[pl.ds(0, kv), :] += lax.dot_general(p.astype(BF16), dot, TN, preferred_element_type=F32)
        dk_ref[...] = dk_acc[...].astype(BF16)
        dv_ref[...] = dv_acc[...].astype(BF16)

    hq = pl.BlockSpec((s, HEAD_PAD), lambda h: (0, h))
    hv = pl.BlockSpec((s, V_HEAD), lambda h: (0, h))
    return pl.pallas_call(
        body, grid=(N_HEADS,), in_specs=[hq, hq, hv, hv], out_specs=[hq, hq, hv],
        out_shape=[_sds((s, N_HEADS * HEAD_PAD), BF16), _sds((s, N_HEADS * HEAD_PAD), BF16),
                   _sds((s, N_HEADS * V_HEAD), BF16)],
        scratch_shapes=[pltpu.VMEM((s, HEAD_PAD), F32), pltpu.VMEM((s, V_HEAD), F32)],
        compiler_params=_params(("parallel",)), name="attention_bwd")(q, k, v, do)


def mla_unrope_grads(dq, dk, cf, sf, tm=512):
    s = dq.shape[0]
    tm = _tile(s, tm)

    def body(dq_ref, dk_ref, cf_ref, sf_ref, dql_ref, dkn_ref, dkpe_ref):
        cfv, sfv = cf_ref[...], sf_ref[...]
        dkpe = jnp.zeros((tm, 128), F32)
        for h in range(N_HEADS):
            lo = h * HEAD_PAD
            dql_ref[:, lo:lo + QK_NOPE] = dq_ref[:, lo:lo + QK_NOPE]
            dql_ref[:, lo + QK_NOPE:lo + HEAD_PAD] = _unrope(
                dq_ref[:, lo + QK_NOPE:lo + HEAD_PAD].astype(F32), cfv, sfv).astype(BF16)
            dkn_ref[:, h * QK_NOPE:(h + 1) * QK_NOPE] = dk_ref[:, lo:lo + QK_NOPE]
            dkpe = dkpe + dk_ref[:, lo + QK_NOPE:lo + HEAD_PAD].astype(F32)
        dkpe_ref[...] = dkpe

    wq = N_HEADS * HEAD_PAD
    return pl.pallas_call(
        body, grid=(s // tm,), in_specs=[_rows(tm, wq), _rows(tm, wq), _rows(tm, 128), _rows(tm, 128)],
        out_specs=[_rows(tm, wq), _rows(tm, N_HEADS * QK_NOPE), _rows(tm, 128)],
        out_shape=[_sds((s, wq), BF16), _sds((s, N_HEADS * QK_NOPE), BF16), _sds((s, 128), F32)],
        compiler_params=_params(("parallel",)), name="mla_unrope_grads")(dq, dk, cf, sf)


ANY = pl.BlockSpec(memory_space=pl.ANY)
GATHER_ID = 1
CHIP_EXCHANGE_ID = 2
PAIR_ID = 3
ALL_ID = 4


def _nbytes(a):
    return a.size * a.dtype.itemsize


def _copy_cost(operand_bytes, sent_fraction):
    sent = int(operand_bytes * sent_fraction)
    return pl.CostEstimate(flops=0, transcendentals=0, bytes_accessed=2 * sent, remote_bytes_transferred=sent)


def _handshake(peers):
    barrier = pltpu.get_barrier_semaphore()
    for peer in peers:
        pl.semaphore_signal(barrier, inc=1, device_id=peer, device_id_type=MESH)
    pl.semaphore_wait(barrier, len(peers))


def _place():
    x, y, c = lax.axis_index("x"), lax.axis_index("y"), lax.axis_index("c")
    chips = [(1 - x, y), (x, 1 - y), (1 - x, 1 - y)]
    return x, y, c, chips


def _half(ref, hc, axis=0):
    n = ref.shape[axis] // 2
    idx = (slice(None),) * axis + (pl.ds(hc * n, n),)
    return ref.at[idx]


def gather_shards(name, tensors, by_columns=()):
    nt = len(tensors)

    def body(*refs):
        a, g = refs[:nt], refs[nt:2 * nt]
        send, recv = refs[2 * nt:]
        x, y, c, _ = _place()
        q = 2 * x + y
        sib, xn, yn = (x, y, 1 - c), (1 - x, y, c), (x, 1 - y, c)
        q_xn, q_yn, q_diag = 2 * (1 - x) + y, 2 * x + 1 - y, 2 * (1 - x) + 1 - y
        _handshake([sib, xn, yn])

        def whole(t, p):
            if t in by_columns:
                n = a[t].shape[1]
                return g[t].at[:, pl.ds(p * n, n)]
            return g[t].at[p]

        def part(t, p, hc, quarter=None):
            rows = a[t].shape[0]
            if quarter is None:
                return whole(t, p).at[pl.ds(hc * (rows // 2), rows // 2)]
            return whole(t, p).at[pl.ds(hc * (rows // 2) + quarter * (rows // 4), rows // 4)]

        def rc(t, k, src, dst, to):
            return pltpu.make_async_remote_copy(src_ref=src, dst_ref=dst, send_sem=send.at[t, k], recv_sem=recv.at[t, k],
                                                device_id=to, device_id_type=MESH)

        sent = []

        def go(cp):
            cp.start()
            sent.append(cp)

        def landed(t, k, piece, frm):
            rc(t, k, piece, piece, frm).wait_recv()
            return piece

        for t in range(nt):
            go(rc(t, 8, a[t], whole(t, q), sib))
            mine = _half(a[t], c)
            go(rc(t, 0, mine, part(t, q, c), xn))
            go(rc(t, 1, mine, part(t, q, c), yn))
        for t in range(nt):
            from_y = landed(t, 1, part(t, q_yn, c), yn)
            go(rc(t, 2, part(t, q_yn, c, 0), part(t, q_yn, c, 0), xn))
            go(rc(t, 5, from_y, from_y, sib))
            from_x = landed(t, 0, part(t, q_xn, c), xn)
            go(rc(t, 3, part(t, q_xn, c, 1), part(t, q_xn, c, 1), yn))
            go(rc(t, 4, from_x, from_x, sib))
        for t in range(nt):
            for k, frm in ((2, xn), (3, yn)):
                piece = landed(t, k, part(t, q_diag, c, k - 2), frm)
                go(rc(t, 4 + k, piece, piece, sib))
        for t in range(nt):
            landed(t, 4, part(t, q_xn, 1 - c), sib)
            landed(t, 5, part(t, q_yn, 1 - c), sib)
            landed(t, 6, part(t, q_diag, 1 - c, 0), sib)
            landed(t, 7, part(t, q_diag, 1 - c, 1), sib)
            landed(t, 8, whole(t, q), sib)
        for cp in sent:
            cp.wait_send()

    return pl.kernel(
        body, name=name,
        out_type=[_sds((a.shape[0], N_CHIPS * a.shape[1]) if t in by_columns else (N_CHIPS,) + a.shape, a.dtype)
                  for t, a in enumerate(tensors)],
        mesh=plsc.ScalarSubcoreMesh(axis_name="sequencer", num_cores=1),
        scratch_types=[pltpu.SemaphoreType.DMA((nt, 9)), pltpu.SemaphoreType.DMA((nt, 9))],
        cost_estimate=_copy_cost(sum(_nbytes(a) for a in tensors), 4),
        compiler_params=pltpu.CompilerParams(collective_id=GATHER_ID))(*tensors)


def pair_exchange(name, grads, on_sequencer):
    nt = len(grads)

    def body(*refs):
        g, theirs = refs[:nt], refs[nt:2 * nt]
        send, recv = refs[2 * nt:]
        x, y, c, _ = _place()
        if on_sequencer:
            _handshake([(x, y, 1 - c)])
        cps = []
        for t in range(nt):
            cp = pltpu.make_async_remote_copy(src_ref=_half(g[t], 1 - c, 1), dst_ref=theirs[t], send_sem=send.at[t],
                                              recv_sem=recv.at[t], device_id=(x, y, 1 - c), device_id_type=MESH)
            cp.start()
            cps.append(cp)
        for cp in cps:
            cp.wait()

    if not on_sequencer:
        return pl.pallas_call(
            body, in_specs=[ANY] * nt, out_specs=[ANY] * nt,
            out_shape=[_sds((N_CHIPS, a.shape[1] // 2, a.shape[2]), a.dtype) for a in grads],
            scratch_shapes=[pltpu.SemaphoreType.DMA((nt,)), pltpu.SemaphoreType.DMA((nt,))],
            name=name)(*grads)
    return pl.kernel(
        body, name=name, out_type=[_sds((N_CHIPS, a.shape[1] // 2, a.shape[2]), a.dtype) for a in grads],
        mesh=plsc.ScalarSubcoreMesh(axis_name="sequencer", num_cores=1),
        scratch_types=[pltpu.SemaphoreType.DMA((nt,)), pltpu.SemaphoreType.DMA((nt,))],
        cost_estimate=_copy_cost(sum(_nbytes(a) for a in grads), 0.5),
        compiler_params=pltpu.CompilerParams(collective_id=PAIR_ID))(*grads)


def chip_exchange(name, parts):
    nt = len(parts)

    def body(*refs):
        a, r = refs[:nt], refs[nt:2 * nt]
        send, recv = refs[2 * nt:]
        x, y, c, chips = _place()
        _handshake([(*chip, c) for chip in chips])
        cps = []
        for t in range(nt):
            for j, chip in enumerate(chips):
                cp = pltpu.make_async_remote_copy(
                    src_ref=a[t].at[2 * chip[0] + chip[1]], dst_ref=r[t].at[j], send_sem=send.at[t, j],
                    recv_sem=recv.at[t, j], device_id=(*chip, c), device_id_type=MESH)
                cp.start()
                cps.append(cp)
        for cp in cps:
            cp.wait()

    return pl.kernel(
        body, name=name, out_type=[_sds((N_CHIPS - 1,) + a.shape[1:], a.dtype) for a in parts],
        mesh=plsc.ScalarSubcoreMesh(axis_name="sequencer", num_cores=1),
        scratch_types=[pltpu.SemaphoreType.DMA((nt, 3)), pltpu.SemaphoreType.DMA((nt, 3))],
        cost_estimate=_copy_cost(sum(_nbytes(a) for a in parts), 0.75),
        compiler_params=pltpu.CompilerParams(collective_id=CHIP_EXCHANGE_ID))(*parts)


def pair_share(name, halves):
    nt = len(halves)

    def body(*refs):
        h, other = refs[:nt], refs[nt:2 * nt]
        send, recv = refs[2 * nt:]
        x, y, c, _ = _place()
        _handshake([(x, y, 1 - c)])
        cps = []
        for t in range(nt):
            cp = pltpu.make_async_remote_copy(src_ref=h[t], dst_ref=other[t], send_sem=send.at[t], recv_sem=recv.at[t],
                                              device_id=(x, y, 1 - c), device_id_type=MESH)
            cp.start()
            cps.append(cp)
        for cp in cps:
            cp.wait()

    return pl.kernel(
        body, name=name, out_type=[_sds(a.shape, a.dtype) for a in halves],
        mesh=plsc.ScalarSubcoreMesh(axis_name="sequencer", num_cores=1),
        scratch_types=[pltpu.SemaphoreType.DMA((nt,)), pltpu.SemaphoreType.DMA((nt,))],
        cost_estimate=_copy_cost(sum(_nbytes(a) for a in halves), 1),
        compiler_params=pltpu.CompilerParams(collective_id=PAIR_ID))(*halves)


def pack_rows(name, parts, rows):
    cdim = parts[0].shape[1]
    n = len(parts)
    vm = pl.BlockSpec(memory_space=pltpu.VMEM)

    def pack(*refs):
        p, o_ref = refs[:n], refs[n]
        at = 0
        for ref in p:
            o_ref[pl.ds(at, ref.shape[0]), :] = ref[...]
            at += ref.shape[0]
        o_ref[pl.ds(at, rows - at), :] = jnp.zeros((rows - at, cdim), F32)

    return pl.pallas_call(pack, in_specs=[vm] * n, out_specs=vm, out_shape=_sds((rows, cdim), F32), name=name)(*parts)


def all_reduce_small(parts, rows):
    cdim = parts[0].shape[1]
    vm = pl.BlockSpec(memory_space=pltpu.VMEM)
    mine = pack_rows("small_pack", parts, rows)

    def exchange(mine_ref, buf, send, recv, lsem):
        x, y, c, _ = _place()
        me = 4 * x + 2 * y + c
        peers = [(x ^ (k >> 2), y ^ ((k >> 1) & 1), c ^ (k & 1)) for k in range(1, 8)]
        _handshake(peers)
        own = pltpu.make_async_copy(mine_ref, buf.at[me], lsem)
        own.start()
        cps = []
        for k, to in enumerate(peers):
            cp = pltpu.make_async_remote_copy(src_ref=mine_ref, dst_ref=buf.at[me], send_sem=send.at[k], recv_sem=recv.at[k],
                                              device_id=to, device_id_type=MESH)
            cp.start()
            cps.append(cp)
        for k, (px, py, pc) in enumerate(peers):
            pltpu.make_async_remote_copy(src_ref=mine_ref, dst_ref=buf.at[4 * px + 2 * py + pc], send_sem=send.at[k],
                                         recv_sem=recv.at[k], device_id=(x, y, c), device_id_type=MESH).wait_recv()
        for cp in cps:
            cp.wait_send()
        own.wait()

    landed = pl.kernel(
        exchange, name="small_exchange", out_type=_sds((8, rows, cdim), F32),
        mesh=plsc.ScalarSubcoreMesh(axis_name="sequencer", num_cores=1),
        scratch_types=[pltpu.SemaphoreType.DMA((7,)), pltpu.SemaphoreType.DMA((7,)), pltpu.SemaphoreType.DMA],
        cost_estimate=_copy_cost(rows * cdim * 4, 7),
        compiler_params=pltpu.CompilerParams(collective_id=ALL_ID))(mine)

    def total(buf, o_ref):
        acc = buf[0]
        for d in range(1, 8):
            acc = acc + buf[d]
        o_ref[...] = acc

    return pl.pallas_call(total, in_specs=[vm], out_specs=vm, out_shape=_sds((rows, cdim), F32), name="small_sum")(landed)


def pair_sum(g, theirs, core, tm=256):
    _, r, c = g.shape
    tm = _tile(r // 2, tm)
    nh = r // 2 // tm

    def body(core_ref, a_ref, b_ref, o_ref):
        o_ref[...] = (a_ref[...].astype(F32) + b_ref[...].astype(F32)).astype(BF16)

    blk = (N_CHIPS, tm, c)
    return pl.pallas_call(
        body, grid_spec=pltpu.PrefetchScalarGridSpec(
            num_scalar_prefetch=1, grid=(nh,),
            in_specs=[pl.BlockSpec(blk, lambda i, cr: (0, cr[0] * nh + i, 0)), pl.BlockSpec(blk, lambda i, cr: (0, i, 0))],
            out_specs=pl.BlockSpec(blk, lambda i, cr: (0, i, 0))),
        out_shape=_sds(theirs.shape, BF16), compiler_params=_params(("parallel",)), name="pair_sum")(core, g, theirs)


def chip_sum(own, landed, chip, stack, layer, layers, tm=256):
    _, r, c = own.shape
    tm = _tile(r, tm)

    def body(chip_ref, own_ref, l_ref, *rest):
        acc = own_ref[...].astype(F32)
        for j in range(N_CHIPS - 1):
            acc = acc + l_ref[j].astype(F32)
        rest[-1][...] = acc

    in_specs = [pl.BlockSpec((None, tm, c), lambda i, qr: (qr[0], i, 0)),
                pl.BlockSpec((N_CHIPS - 1, tm, c), lambda i, qr: (0, i, 0))]
    args = [chip, own, landed]
    if stack is not None:
        in_specs.append(ANY)
        args.append(stack)
    return pl.pallas_call(
        body, grid_spec=pltpu.PrefetchScalarGridSpec(
            num_scalar_prefetch=1, grid=(r // tm,), in_specs=in_specs,
            out_specs=pl.BlockSpec((None, tm, c), lambda i, qr: (layer, i, 0))),
        out_shape=_sds((layers, r, c), F32), input_output_aliases={3: 0} if stack is not None else {},
        compiler_params=_params(("parallel",)), name="chip_sum")(*args)


def _adamw_math(w, g, m, v):
    bc1 = 1.0 - ADAM_B1 ** ADAM_STEP
    bc2 = 1.0 - ADAM_B2 ** ADAM_STEP
    nm = ADAM_B1 * m + (1.0 - ADAM_B1) * g
    nv = ADAM_B2 * v + (1.0 - ADAM_B2) * (g * g)
    return -ADAM_LR * ((nm / bc1) / (jnp.sqrt(nv / bc2) + ADAM_EPS) + ADAM_WD * w), nm, nv


def vector_update(red, chip, ws, ms, vs, where):
    n = len(ws)
    dd = red.shape[1]

    def body(chip_ref, red_ref, *refs):
        w_r, m_r, v_r = refs[0:n], refs[n:2 * n], refs[2 * n:3 * n]
        g_o, d_o, m_o, v_o = (refs[(3 + k) * n:(4 + k) * n] for k in range(4))
        q = chip_ref[0]

        def chip_block(val, width):
            out = val[:, 0:width]
            for p in range(1, val.shape[1] // width):
                out = jnp.where(q == p, val[:, p * width:(p + 1) * width], out)
            return out

        for k in range(n):
            for idx, r0, nr, cols in where[k]:
                width = w_r[k].shape[-1]
                if cols == "chip" and width * N_CHIPS != dd:
                    g = chip_block(jnp.concatenate([red_ref[pl.ds(r0 + j, 1), :] for j in range(nr)], axis=1), width)
                else:
                    g = red_ref[pl.ds(r0, nr), :]
                    g = chip_block(g, width) if cols == "chip" else g if cols == "all" else g[:, 0:cols]
                delta, nm, nv = _adamw_math(w_r[k][idx], g, m_r[k][idx], v_r[k][idx])
                g_o[k][idx] = g
                d_o[k][idx] = delta
                m_o[k][idx] = nm
                v_o[k][idx] = nv

    vm = pl.BlockSpec(memory_space=pltpu.VMEM)
    outs = pl.pallas_call(
        body, in_specs=[pl.BlockSpec(memory_space=pltpu.SMEM), vm] + [vm] * (3 * n), out_specs=[vm] * (4 * n),
        out_shape=[_sds(w.shape, F32) for w in ws] * 4, name="vector_update")(chip, red, *ws, *ms, *vs)
    return [outs[k * n:(k + 1) * n] for k in range(4)]


def adamw_joined(w, m, v, g_mine, g_theirs, core, tm=512):
    nl, r, c = w.shape
    tm = _tile(r // 2, tm)
    nh = r // 2 // tm

    def body(core_ref, w_ref, m_ref, v_ref, gm_ref, gt_ref, g_ref, d_ref, nm_ref, nv_ref):
        mine = (pl.program_id(1) // nh) == core_ref[0]
        gv = jnp.where(mine, gm_ref[...], gt_ref[...])
        g_ref[...] = gv
        d_ref[...], nm_ref[...], nv_ref[...] = _adamw_math(w_ref[...], gv, m_ref[...], v_ref[...])

    full = pl.BlockSpec((None, tm, c), lambda l, i, cr: (l, i, 0))
    half = pl.BlockSpec((None, tm, c), lambda l, i, cr: (l, i % nh, 0))
    return pl.pallas_call(
        body, grid_spec=pltpu.PrefetchScalarGridSpec(
            num_scalar_prefetch=1, grid=(nl, r // tm), in_specs=[full, full, full, half, half], out_specs=[full] * 4),
        out_shape=[_sds((nl, r, c), F32)] * 4, compiler_params=_params(("parallel", "parallel")),
        name="adamw_joined")(core, w, m, v, g_mine, g_theirs)


WEIGHTS = ['sc_w_in', 'sc_conv_w', 'sc_w_out', 'mla_w_dq', 'mla_g_q', 'mla_w_uq', 'mla_w_dkv', 'mla_g_kv', 'mla_w_uk',
           'mla_w_uv', 'mla_w_o', 'cf_w_pw1', 'cf_b_pw1', 'cf_dw_w', 'cf_dw_b', 'cf_norm_g', 'cf_norm_b', 'cf_w_pw2',
           'cf_b_pw2', 'ff_w1', 'ff_w2', 'ln_mix_g', 'ln_mix_b', 'ln_ff_g', 'ln_ff_b']
ARGS = ['x'] + WEIGHTS + ['loss_target'] + ['m_' + n for n in WEIGHTS] + ['v_' + n for n in WEIGHTS]


def _sq_relu(h):
    r = jnp.maximum(h, jnp.zeros_like(h))
    return r * r


def _mlp_forward(i, x, xb, w1, w2, g, b):
    hb = mm_plain_nn(f"mlp{i}_up", xb, w1, BF16, tm=2048, tn=1024)
    y, yb, xh, rstd = mm_residual_ln(f"mlp{i}_down_ln", hb, w2, x, g, b, tk=4096, a_fn=_sq_relu)
    return (y, yb), dict(xb=xb, hb=hb, xh=xh, rstd=rstd, g=g)


def _mlp_backward(i, dr, drb, sv, w1, w2, dw1, dw2, reduce_after, mixer_ln):
    s = dr.shape[0]
    tm, tn = _tile(s, 1024), 1024

    def epi(acc, e, o):
        o[0][...] = (acc * (2.0 * jnp.maximum(e[0][...].astype(F32), 0.0))).astype(BF16)

    dhb = mm_nt(f"mlp{i}_down_bwd", drb, w2, s, tm, tn, 1024, epi, [_sds((s, w2.k), BF16)], [_ij(tm, tn)],
                [sv["hb"]], [_ij(tm, tn)])[0]
    g_w2 = mm_tn(f"mlp{i}_dw2", sv["hb"], drb, dw2, s, 1024, 1024, a_fn=_sq_relu)
    g_w1 = mm_tn(f"mlp{i}_dw1", sv["xb"], dhb, dw1, s, 1024, 1024)
    dhb = reduce_after(dhb, {f"w1_{i}": g_w1, f"w2_{i}": g_w2})
    return mm_nt_ln_backward(f"mlp{i}_up_bwd", dhb, w1, dr, *mixer_ln, tk=2048)


def kernel(x, sc_w_in, sc_conv_w, sc_w_out, mla_w_dq, mla_g_q, mla_w_uq, mla_w_dkv, mla_g_kv, mla_w_uk, mla_w_uv, mla_w_o, cf_w_pw1, cf_b_pw1, cf_dw_w, cf_dw_b, cf_norm_g, cf_norm_b, cf_w_pw2, cf_b_pw2, ff_w1, ff_w2, ln_mix_g, ln_mix_b, ln_ff_g, ln_ff_b, loss_target, m_sc_w_in, m_sc_conv_w, m_sc_w_out, m_mla_w_dq, m_mla_g_q, m_mla_w_uq, m_mla_w_dkv, m_mla_g_kv, m_mla_w_uk, m_mla_w_uv, m_mla_w_o, m_cf_w_pw1, m_cf_b_pw1, m_cf_dw_w, m_cf_dw_b, m_cf_norm_g, m_cf_norm_b, m_cf_w_pw2, m_cf_b_pw2, m_ff_w1, m_ff_w2, m_ln_mix_g, m_ln_mix_b, m_ln_ff_g, m_ln_ff_b, v_sc_w_in, v_sc_conv_w, v_sc_w_out, v_mla_w_dq, v_mla_g_q, v_mla_w_uq, v_mla_w_dkv, v_mla_g_kv, v_mla_w_uk, v_mla_w_uv, v_mla_w_o, v_cf_w_pw1, v_cf_b_pw1, v_cf_dw_w, v_cf_dw_b, v_cf_norm_g, v_cf_norm_b, v_cf_w_pw2, v_cf_b_pw2, v_ff_w1, v_ff_w2, v_ln_mix_g, v_ln_mix_b, v_ln_ff_g, v_ln_ff_b):
    given = dict(zip(ARGS, (x, sc_w_in, sc_conv_w, sc_w_out, mla_w_dq, mla_g_q, mla_w_uq, mla_w_dkv, mla_g_kv, mla_w_uk, mla_w_uv, mla_w_o, cf_w_pw1, cf_b_pw1, cf_dw_w, cf_dw_b, cf_norm_g, cf_norm_b, cf_w_pw2, cf_b_pw2, ff_w1, ff_w2, ln_mix_g, ln_mix_b, ln_ff_g, ln_ff_b, loss_target, m_sc_w_in, m_sc_conv_w, m_sc_w_out, m_mla_w_dq, m_mla_g_q, m_mla_w_uq, m_mla_w_dkv, m_mla_g_kv, m_mla_w_uk, m_mla_w_uv, m_mla_w_o, m_cf_w_pw1, m_cf_b_pw1, m_cf_dw_w, m_cf_dw_b, m_cf_norm_g, m_cf_norm_b, m_cf_w_pw2, m_cf_b_pw2, m_ff_w1, m_ff_w2, m_ln_mix_g, m_ln_mix_b, m_ln_ff_g, m_ln_ff_b, v_sc_w_in, v_sc_conv_w, v_sc_w_out, v_mla_w_dq, v_mla_g_q, v_mla_w_uq, v_mla_w_dkv, v_mla_g_kv, v_mla_w_uk, v_mla_w_uv, v_mla_w_o, v_cf_w_pw1, v_cf_b_pw1, v_cf_dw_w, v_cf_dw_b, v_cf_norm_g, v_cf_norm_b, v_cf_w_pw2, v_cf_b_pw2, v_ff_w1, v_ff_w2, v_ln_mix_g, v_ln_mix_b, v_ln_ff_g, v_ln_ff_b)))
    s, d = x.shape[1], x.shape[2]
    d_ff = 4 * d
    dq4 = d // N_CHIPS
    xq = lax.axis_index("x") * 2 + lax.axis_index("y")

    w_dkv_pad = jnp.pad(mla_w_dkv[0], ((0, 0), (0, 128 - QK_ROPE)))
    w_uq_pad = jnp.pad(mla_w_uq[0].reshape(Q_LORA, 2, QK_NOPE + QK_ROPE), ((0, 0), (0, 0), (0, HEAD_PAD - QK_NOPE - QK_ROPE)))
    small = pack_rows("vector_weights_pack", [
        sc_conv_w.reshape(2 * SC_WIDTH, dq4), cf_b_pw1.reshape(2, dq4), cf_dw_w[0], cf_dw_b, cf_norm_g, cf_norm_b,
        cf_b_pw2], 64)
    mlp_w = lambda i: [ff_w1[i].astype(BF16), ff_w2[i].astype(BF16)]
    g_in, g_out, g_w1, g_w2 = [None] * 2, [None] * 2, [None] * DEPTH, [None] * DEPTH
    g_in[0], g_out[0], g_small = gather_shards(
        "gather_mixer0", [sc_w_in[0].astype(BF16), sc_w_out[0].astype(BF16), small], by_columns=(0,))
    (g_w1[0],) = gather_shards("gather_up0", [ff_w1[0].astype(BF16)], by_columns=(0,))
    (g_w2[0],) = gather_shards("gather_down0", [ff_w2[0].astype(BF16)])
    g_dqkv, g_uq, g_uk, g_uv, g_o = gather_shards("gather_mixer1", [
        jnp.concatenate([mla_w_dq[0], w_dkv_pad], axis=1).astype(BF16),
        w_uq_pad.reshape(Q_LORA, 2 * HEAD_PAD).astype(BF16),
        mla_w_uk.reshape(KV_LORA // N_CHIPS, N_HEADS * QK_NOPE).astype(BF16),
        mla_w_uv.reshape(KV_LORA // N_CHIPS, N_HEADS * V_HEAD).astype(BF16), mla_w_o[0].astype(BF16)], by_columns=(1,))
    g_w1[1], g_w2[1] = gather_shards("gather_mlp1", mlp_w(1), by_columns=(0,))
    g_pw1, g_pw2, g_w1[2], g_w2[2] = gather_shards(
        "gather_layer2", [cf_w_pw1[0].astype(BF16), cf_w_pw2[0].astype(BF16)] + mlp_w(2), by_columns=(0, 2))
    g_in[1], g_out[1], g_w1[3], g_w2[3] = gather_shards(
        "gather_layer3", [sc_w_in[1].astype(BF16), sc_w_out[1].astype(BF16)] + mlp_w(3), by_columns=(0, 2))

    wd_t = Q_LORA + KV_LORA + 128
    w_in = [Stk("full", d, 3 * d, g_in[j]) for j in range(2)]
    w_out = [Stk("row", d, d, g_out[j]) for j in range(2)]
    w_dqkv = Stk("row", d, wd_t, g_dqkv)
    w_uq = Stk("full", Q_LORA, N_HEADS * HEAD_PAD, g_uq)
    w_uk = Stk("row", KV_LORA, N_HEADS * QK_NOPE, g_uk)
    w_uv = Stk("row", KV_LORA, N_HEADS * V_HEAD, g_uv)
    w_o = Stk("row", d, d, g_o)
    w_pw1 = Stk("full", d, 2 * d, g_pw1)
    w_pw2 = Stk("row", d, d, g_pw2)
    w_1 = [Stk("full", d, d_ff, g_w1[i]) for i in range(DEPTH)]
    w_2 = [Stk("row", d_ff, d, g_w2[i]) for i in range(DEPTH)]

    def wide(rows):
        return jnp.swapaxes(rows, 0, 1).reshape(rows.shape[1], d)

    conv_w = wide(g_small[:, 0:6]).reshape(2, SC_WIDTH, d)
    b_pw1 = g_small[:, 6:8].reshape(1, 2 * d)
    dw_w = wide(g_small[:, 8:39])
    dw_b, norm_g, norm_b, b_pw2 = (wide(g_small[:, 39 + k:40 + k]) for k in range(4))

    pos = jnp.arange(s, dtype=F32)
    inv_freq = ROPE_THETA ** (-jnp.arange(0, QK_ROPE, 2, dtype=F32) / QK_ROPE)
    ang = pos[:, None] * inv_freq[None, :]
    cos, sin, zero = jnp.cos(ang), jnp.sin(ang), jnp.zeros((s, 128 - QK_ROPE), F32)
    cf = jnp.concatenate([cos, cos, zero], axis=1)
    sf = jnp.concatenate([-sin, sin, zero], axis=1)

    def row(a, i):
        return a[i:i + 1]

    xs = x.reshape(s, d)
    cur = (xs, xs.astype(BF16))
    tape = []
    for i in range(DEPTH):
        mixer, j = i % 3, i // 3
        xf, xb = cur
        lg, lb = row(ln_mix_g, i), row(ln_mix_b, i)
        if mixer == 0:
            u = mm_plain_nn(f"sc{j}_in", xb, w_in[j], F32, tn=3 * dq4)
            gb = short_conv_gate(u, conv_w[j])
            y, yb, xh, rstd = mm_residual_ln(f"sc{j}_out_ln", gb, w_out[j], xf, lg, lb)
            sv = dict(xb=xb, u=u, gb=gb)
        elif mixer == 1:
            t = mm_plain_nn("mla_down", xb, w_dqkv, F32, tn=wd_t // 2)
            cq, ckv, kpe = mla_latents(t, mla_g_q, mla_g_kv, cf, sf)
            qh = mla_queries(cq, w_uq, cf, sf)
            kh = mla_keys(ckv, w_uk, kpe)
            vh = mm_plain_nn("mla_values", ckv, w_uv, BF16, tk=KV_LORA)
            oh = attention(qh, kh, vh)
            y, yb, xh, rstd = mm_residual_ln("mla_out_ln", oh, w_o, xf, lg, lb)
            sv = dict(xb=xb, t=t, cq=cq, ckv=ckv, qh=qh, kh=kh, vh=vh, oh=oh)
        else:
            u = mm_plain_nn("cf_pw1", xb, w_pw1, F32, bias=b_pw1)
            hc = conformer_glu_conv(u, dw_w, dw_b)
            sb = conformer_norm_swish(hc, norm_g, norm_b)
            y, yb, xh, rstd = mm_residual_ln("cf_pw2_ln", sb, w_pw2, xf, lg, lb, bias=b_pw2)
            sv = dict(xb=xb, u=u, hc=hc, sb=sb)
        sv.update(xh=xh, rstd=rstd, g=lg)
        cur, sv_mlp = _mlp_forward(i, y, yb, w_1[i], w_2[i], row(ln_ff_g, i), row(ln_ff_b, i))
        tape.append((sv, sv_mlp))

    g_ln = {n: [None] * DEPTH for n in ("ln_mix_g", "ln_mix_b", "ln_ff_g", "ln_ff_b")}
    last = tape[DEPTH - 1][1]
    dr, drb, g_ln["ln_ff_g"][DEPTH - 1], g_ln["ln_ff_b"][DEPTH - 1], _, loss_part = loss_ln_backward(
        cur[0], loss_target.reshape(s, d), last["xh"], last["rstd"], last["g"])

    grads = {}
    smalls = {}
    conv_grads = [None, None]
    core = lax.axis_index("c").astype(jnp.int32).reshape(1)
    chip = xq.astype(jnp.int32).reshape(1)
    pairs, landed = {}, {}
    ready, theirs = [], {}

    def hold(xs, others):
        live = [x for x in xs if x is not None]
        out = lax.optimization_barrier((*live, *others))
        rest = iter(out[:len(live)])
        return tuple(None if x is None else next(rest) for x in xs), list(out[len(live):])

    def reduce_after(x, new, early=False):
        out = lax.optimization_barrier((x, *new.values()))
        grads.update(zip(new, out[1:]))
        if early:
            theirs.update(zip(new, pair_exchange(f"pair_exchange_{len(theirs)}", list(out[1:]), True)))
        ready.extend(new)
        return out[0]

    def reduce_layer(i, x):
        late = [n for n in ready if n not in theirs]
        if late:
            theirs.update(zip(late, pair_exchange(f"pair_exchange_layer{i}", [grads[n] for n in late], False)))
        sums = [pair_sum(grads[n], theirs[n], core) for n in ready]
        pairs.update(zip(ready, sums))
        landed.update(zip(ready, chip_exchange(f"chip_exchange_layer{i}", sums)))
        exchanged.append(list(ready))
        ready.clear()
        return hold(x, sums)[0]

    groups = [["in_0", "in_1"], ["out_0", "out_1"], ["dqkv"], ["uq"], ["uk"], ["uv"], ["o"], ["pw1"], ["pw2"],
              [f"w1_{i}" for i in range(DEPTH)], [f"w2_{i}" for i in range(DEPTH)]]
    stacks = [None] * len(groups)
    exchanged = []

    def sum_layer(x, last=False):
        names = exchanged.pop(0)
        if last:
            x, held = hold(x, [landed[n] for n in names])
            landed.update(zip(names, held))
        new = []
        for n in names:
            k = next(k for k, members in enumerate(groups) if n in members)
            stacks[k] = chip_sum(pairs[n], landed[n], chip, stacks[k], groups[k].index(n), len(groups[k]))
            new.append(stacks[k])
        return x if last else hold(x, new)[0]

    for i in reversed(range(DEPTH)):
        mixer, j = i % 3, i // 3
        sv, sv_mlp = tape[i]
        dr, drb, g_ln["ln_mix_g"][i], g_ln["ln_mix_b"][i], dr_sum = _mlp_backward(
            i, dr, drb, sv_mlp, w_1[i], w_2[i], Stk("col", d, d_ff), Stk("row", d_ff, d),
            lambda x_, new: reduce_after(x_, new, early=i > 0), (sv["xh"], sv["rstd"], sv["g"]))
        if i == 0:
            dr, drb = reduce_layer("0_mlp", (dr, drb))

        def to_input(name, a, w, tk, a_spec_fn=None):
            if i == 0:
                spec = None if a_spec_fn is None else (s, a_spec_fn)
                return mm_plain_nt(name, a, w, F32, tn=1024, tk=tk, add=dr, add_scale=ALPHA, a_spec_fn=spec), None
            prev = tape[i - 1][1]
            out = mm_nt_ln_backward(name, a, w, dr, prev["xh"], prev["rstd"], prev["g"], tk=tk, a_spec_fn=a_spec_fn)
            g_ln["ln_ff_g"][i - 1], g_ln["ln_ff_b"][i - 1] = out[2], out[3]
            return out[0], out[1]

        parts_of = lambda tm, tk: pl.BlockSpec((None, tm, tk), lambda i_, j_, k_: (k_, i_, 0))
        if mixer == 0:
            dgate = mm_plain_nt(f"sc{j}_out_bwd", drb, w_out[j], F32)
            dw_out = mm_tn(f"sc{j}_dw_out", sv["gb"], drb, Stk("row", d, d), s, 512, 1024)
            du, conv_grads[j] = short_conv_gate_bwd(sv["u"], conv_w[j], dgate)
            nb = d // 256
            dw_in = mm_tn(
                f"sc{j}_dw_in", sv["xb"], du, Stk("col", d, 3 * d), s, 1024, 256,
                b_spec=pl.BlockSpec((None, s, 256), lambda i_, j_, k_: (j_ // nb, k_, j_ % nb)))
            du = reduce_after(du, {f"in_{j}": dw_in, f"out_{j}": dw_out})
            dr, drb = to_input(f"sc{j}_in_bwd", du, w_in[j], d, parts_of)
        elif mixer == 1:
            do = mm_plain_nt("mla_out_bwd", drb, w_o, BF16)
            g_o = mm_tn("mla_dw_o", sv["oh"], drb, Stk("row", d, d), s, 512, 1024)
            dqh, dkh, dvh = attention_bwd(sv["qh"], sv["kh"], sv["vh"], do)
            dql, dkn, dkpe = mla_unrope_grads(dqh, dkh, cf, sf)
            g_uq = mm_tn("mla_dw_uq", sv["cq"], dql, Stk("col", Q_LORA, N_HEADS * HEAD_PAD), s, Q_LORA, 512)
            dcq = mm_plain_nt("mla_uq_bwd", dql, w_uq, F32, tn=Q_LORA)
            g_uk = mm_tn("mla_dw_uk", sv["ckv"], dkn, Stk("row", KV_LORA, N_HEADS * QK_NOPE), s, KV_LORA, 1024)
            g_uv = mm_tn("mla_dw_uv", sv["ckv"], dvh, Stk("row", KV_LORA, N_HEADS * V_HEAD), s, KV_LORA, 1024)
            dckv = mm_plain_nt("mla_uk_bwd", dkn, w_uk, F32, tn=KV_LORA)
            dckv = mm_plain_nt("mla_uv_bwd", dvh, w_uv, F32, tn=KV_LORA, add=dckv)
            dt, smalls["g_q"], smalls["g_kv"] = mla_latents_bwd(sv["t"], mla_g_q, mla_g_kv, cf, sf, dcq, dckv, dkpe)
            g_dqkv = mm_tn("mla_dw_down", sv["xb"], dt, Stk("row", d, wd_t), s, 512, wd_t)
            dt = reduce_after(dt, {"dqkv": g_dqkv, "uq": g_uq, "uk": g_uk, "uv": g_uv, "o": g_o})
            dr, drb = to_input("mla_down_bwd", dt, w_dqkv, wd_t)
        else:
            dsw = mm_plain_nt("cf_pw2_bwd", drb, w_pw2, F32)
            g_pw2 = mm_tn("cf_dw_pw2", sv["sb"], drb, Stk("row", d, d), s, 512, 1024)
            smalls["b_pw2"] = dr_sum
            dhc, smalls["norm_g"], smalls["norm_b"] = conformer_norm_swish_bwd(sv["hc"], norm_g, norm_b, dsw)
            du, smalls["b_pw1"], smalls["dw_w"], smalls["dw_b"] = conformer_glu_conv_bwd(sv["u"], dw_w, dhc)
            nb = d // 512
            g_pw1 = mm_tn(
                "cf_dw_pw1", sv["xb"], du, Stk("col", d, 2 * d), s, 1024, 512,
                b_spec=pl.BlockSpec((None, s, 512), lambda i_, j_, k_: (j_ // nb, k_, j_ % nb)))
            du = reduce_after(du, {"pw1": g_pw1, "pw2": g_pw2})
            dr, drb = to_input("cf_pw1_bwd", du, w_pw1, d, parts_of)
        if i < DEPTH - 1:
            dr, drb = sum_layer((dr, drb))
        dr, drb = reduce_layer(i, (dr, drb))
    grad_x = sum_layer(sum_layer((dr, None), last=True), last=True)[0].reshape(1, s, d)

    mine = stacks
    other = (pair_share("pair_share_mixers", mine[:9]) + pair_share("pair_share_up", mine[9:10])
             + pair_share("pair_share_down", mine[10:]))

    def padded(get):
        dqkv = jnp.concatenate([get("mla_w_dq")[0], jnp.pad(get("mla_w_dkv")[0], ((0, 0), (0, 128 - QK_ROPE)))], axis=1)
        uq = jnp.pad(get("mla_w_uq")[0].reshape(Q_LORA, 2, QK_NOPE + QK_ROPE),
                     ((0, 0), (0, 0), (0, HEAD_PAD - QK_NOPE - QK_ROPE))).reshape(Q_LORA, 2 * HEAD_PAD)
        return [get("sc_w_in"), get("sc_w_out"), dqkv[None], uq[None],
                get("mla_w_uk").reshape(1, KV_LORA // N_CHIPS, d), get("mla_w_uv").reshape(1, KV_LORA // N_CHIPS, d),
                get("mla_w_o"), get("cf_w_pw1"), get("cf_w_pw2"), get("ff_w1"), get("ff_w2")]

    w_l, m_l, v_l = (padded(lambda n, p=p: given[p + n]) for p in ("", "m_", "v_"))
    res = [adamw_joined(w_l[k], m_l[k], v_l[k], mine[k], other[k], core) for k in range(len(groups))]

    def unpadded(k):
        r_in, r_out, r_dqkv, r_uq, r_uk, r_uv, r_o, r_pw1, r_pw2, r_w1, r_w2 = (r[k] for r in res)
        return {
            "sc_w_in": r_in, "sc_w_out": r_out, "mla_w_dq": r_dqkv[:, :, 0:Q_LORA],
            "mla_w_dkv": r_dqkv[:, :, Q_LORA:Q_LORA + KV_LORA + QK_ROPE],
            "mla_w_uq": r_uq.reshape(1, Q_LORA, 2, HEAD_PAD)[:, :, :, 0:QK_NOPE + QK_ROPE].reshape(mla_w_uq.shape),
            "mla_w_uk": r_uk.reshape(mla_w_uk.shape), "mla_w_uv": r_uv.reshape(mla_w_uv.shape),
            "mla_w_o": r_o, "cf_w_pw1": r_pw1, "cf_w_pw2": r_pw2, "ff_w1": r_w1, "ff_w2": r_w2}

    big_g, big_d, big_m, big_v = (unpadded(k) for k in range(4))

    pad_row = lambda a: jnp.pad(a, ((0, 0), (0, d - a.shape[1])))
    small_parts = ([g for n in ("ln_mix_g", "ln_mix_b", "ln_ff_g", "ln_ff_b") for g in g_ln[n]]
                   + [pad_row(smalls["g_q"]), pad_row(smalls["g_kv"]), conv_grads[0], conv_grads[1],
                      smalls["b_pw1"].reshape(2, d), smalls["dw_w"], smalls["dw_b"], smalls["norm_g"], smalls["norm_b"],
                      smalls["b_pw2"], loss_part])
    red = all_reduce_small(small_parts, 64)
    loss = red[61, 0]

    where = {
        "ln_mix_g": [((), 0, DEPTH, "all")], "ln_mix_b": [((), 4, DEPTH, "all")],
        "ln_ff_g": [((), 8, DEPTH, "all")], "ln_ff_b": [((), 12, DEPTH, "all")],
        "mla_g_q": [((), 16, 1, Q_LORA)], "mla_g_kv": [((), 17, 1, KV_LORA)],
        "sc_conv_w": [((0,), 18, SC_WIDTH, "chip"), ((1,), 21, SC_WIDTH, "chip")],
        "cf_b_pw1": [((), 24, 2, "chip")], "cf_dw_w": [((0,), 26, CONF_WIDTH, "chip")],
        "cf_dw_b": [((), 57, 1, "chip")], "cf_norm_g": [((), 58, 1, "chip")], "cf_norm_b": [((), 59, 1, "chip")],
        "cf_b_pw2": [((), 60, 1, "chip")]}
    vec = list(where)
    vec_res = vector_update(red, chip, [given[n] for n in vec], [given["m_" + n] for n in vec],
                            [given["v_" + n] for n in vec], [where[n] for n in vec])
    gw = dict(big_g)
    upd = {n: [big_d[n], big_m[n], big_v[n]] for n in big_g}
    for k, n in enumerate(vec):
        gw[n] = vec_res[0][k]
        upd[n] = [vec_res[1][k], vec_res[2][k], vec_res[3][k]]

    return (loss, grad_x, *[gw[n] for n in WEIGHTS], *[upd[n][0] for n in WEIGHTS],
            *[upd[n][1] for n in WEIGHTS], *[upd[n][2] for n in WEIGHTS])
```

```python
import jax
import jax.numpy as jnp
from jax import lax
from jax.experimental import pallas as pl
from jax.experimental.pallas import tpu as pltpu
from jax.experimental.pallas import tpu_sc as plsc

F32 = jnp.float32
BF16 = jnp.bfloat16
MESH = pl.DeviceIdType.MESH

DEPTH = 4
ALPHA = (2.0 * DEPTH) ** 0.25
LN_EPS = 1e-5
RMS_EPS = 1e-6
CHUNK_SHIFT = 6
N_HEADS = 8
QK_NOPE = 128
QK_ROPE = 64
V_HEAD = 128
HEAD_PAD = 256
Q_LORA = 384
KV_LORA = 256
ROPE_THETA = 10000.0
SC_WIDTH = 3
CONF_WIDTH = 31
CONV_PAD = 32
CONV_CHUNK = 64
N_CHIPS = 4
ATTN_SCALE = (QK_NOPE + QK_ROPE) ** -0.5

ADAM_LR = 0.001
ADAM_B1 = 0.9
ADAM_B2 = 0.999
ADAM_EPS = 1e-08
ADAM_WD = 0.01
ADAM_STEP = 10

VMEM_LIMIT = 56 * 2**20

NN = (((1,), (0,)), ((), ()))
NT = (((1,), (1,)), ((), ()))
TN = (((0,), (0,)), ((), ()))


def _params(sem=None):
    return pltpu.CompilerParams(dimension_semantics=sem, vmem_limit_bytes=VMEM_LIMIT)


class Stk:
    def __init__(self, kind, k, n, arr=None, layers=None, layer=None):
        self.kind, self.k, self.n, self.layers, self.layer = kind, k, n, layers, layer
        self.plain = (kind == "row" and layers is None) or kind == "full"
        self.kloc = k // N_CHIPS if kind == "row" else k
        self.nloc = n // N_CHIPS if kind == "col" else n
        if arr is not None and self.plain:
            arr = arr.reshape(k, n)
        self.arr = arr

    @property
    def shape(self):
        if self.plain:
            return (self.k, self.n)
        lead = (N_CHIPS,) if self.layers is None else (N_CHIPS, self.layers)
        return lead + (self.kloc, self.nloc)

    def spec(self, bk, bn, f):
        if self.plain:
            return pl.BlockSpec((bk, bn), f)
        assert self.kloc % bk == 0 and self.nloc % bn == 0, (self.kloc, bk, self.nloc, bn)
        pk, pn = self.kloc // bk, self.nloc // bn
        kind, layer = self.kind, self.layer

        def imap(*g):
            kb, nb = f(*g)
            if kind == "row":
                q, kb, nb = kb // pk, kb % pk, nb
            else:
                q, kb, nb = nb // pn, kb, nb % pn
            return (q, kb, nb) if layer is None else (q, layer, kb, nb)

        block = (None, bk, bn) if layer is None else (None, None, bk, bn)
        return pl.BlockSpec(block, imap)


def _mm(name, mode, a, b, grid, a_spec, b_spec, acc_shape, extras, extra_specs, out_shapes, out_specs, epi, a_fn=None,
        rows_in_order=False):
    nk = grid[2]
    ne = len(extras)

    def body(*refs):
        a_ref, b_ref = refs[0], refs[1]
        e_refs = refs[2:2 + ne]
        av = a_ref[...] if a_fn is None else a_fn(a_ref[...])
        part = lax.dot_general(av, b_ref[...], mode, preferred_element_type=F32)
        if nk == 1:
            epi(part, e_refs, refs[2 + ne:])
            return
        o_refs = refs[2 + ne:-1]
        acc = refs[-1]
        k = pl.program_id(2)

        @pl.when(k == 0)
        def _():
            acc[...] = part

        @pl.when(k > 0)
        def _():
            acc[...] += part

        @pl.when(k == nk - 1)
        def _():
            epi(acc[...], e_refs, o_refs)

    return pl.pallas_call(
        body, grid=grid, in_specs=[a_spec, b_spec, *extra_specs], out_specs=out_specs, out_shape=out_shapes,
        scratch_shapes=[pltpu.VMEM(acc_shape, F32)] if nk > 1 else [],
        compiler_params=_params(("arbitrary",) * 3 if rows_in_order else ("parallel", "parallel", "arbitrary")),
        name=name)(a, b, *extras)


def _tile(n, t):
    t = min(n, t)
    while n % t:
        t -= 8
    assert t > 0, (n, t)
    return t


def mm_nn(name, a, w, tm, tn, tk, epi, out_shapes, out_specs, extras=(), extra_specs=(), a_spec=None, a_fn=None):
    m = a.shape[0]
    tm, tn, tk = _tile(m, tm), _tile(w.n, tn), _tile(w.k, tk)
    grid = (m // tm, w.n // tn, w.k // tk)
    a_spec = a_spec or pl.BlockSpec((tm, tk), lambda i, j, k: (i, k))
    b_spec = w.spec(tk, tn, lambda i, j, k: (k, j))
    return _mm(name, NN, a, w.arr, grid, a_spec, b_spec, (tm, tn), extras, extra_specs, out_shapes, out_specs, epi, a_fn)


def mm_nt(name, a, w, m, tm, tn, tk, epi, out_shapes, out_specs, extras=(), extra_specs=(), a_spec=None,
          rows_in_order=False):
    tm, tn, tk = _tile(m, tm), _tile(w.k, tn), _tile(w.n, tk)
    grid = (m // tm, w.k // tn, w.n // tk)
    a_spec = a_spec or pl.BlockSpec((tm, tk), lambda i, j, k: (i, k))
    b_spec = w.spec(tn, tk, lambda i, j, k: (j, k))
    return _mm(name, NT, a, w.arr, grid, a_spec, b_spec, (tm, tn), extras, extra_specs, out_shapes, out_specs, epi,
               rows_in_order=rows_in_order)


def mm_tn(name, a, b, dw, s, tm=512, tn=512, tk=4096, a_spec=None, b_spec=None, a_fn=None):
    tm, tn, tk = _tile(dw.k, tm), _tile(dw.n, tn), _tile(s, tk)
    grid = (dw.k // tm, dw.n // tn, s // tk)
    a_spec = a_spec or pl.BlockSpec((tk, tm), lambda i, j, k: (k, i))
    b_spec = b_spec or pl.BlockSpec((tk, tn), lambda i, j, k: (k, j))

    def epi(acc, e, o):
        o[0][...] = acc.astype(BF16)

    out = _mm(name, TN, a, b, grid, a_spec, b_spec, (tm, tn), (), (), [jax.ShapeDtypeStruct(dw.shape, BF16)],
              [dw.spec(tm, tn, lambda i, j, k: (i, j))], epi, a_fn)[0]
    return out.reshape(N_CHIPS, dw.k // N_CHIPS, dw.n) if dw.plain else out


def _sds(shape, dtype):
    return jax.ShapeDtypeStruct(shape, dtype)


def _ij(tm, tn):
    return pl.BlockSpec((tm, tn), lambda i, j, k: (i, j))


def _i0(tm, c):
    return pl.BlockSpec((tm, c), lambda i, j, k: (i, 0))


def _0j(r, tn):
    return pl.BlockSpec((r, tn), lambda i, j, k: (0, j))


def _layer_norm_rows(r, g, b):
    mu = jnp.mean(r, axis=-1, keepdims=True)
    d = r - mu
    var = jnp.mean(d * d, axis=-1, keepdims=True)
    rstd = lax.rsqrt(var + LN_EPS)
    xh = d * rstd
    return xh * g + b, xh, rstd


def mm_residual_ln(name, a, w, x, g, b, bias=None, tm=512, tk=1024, a_fn=None):
    s, d = x.shape
    tm = _tile(s, tm)
    extras = [x, g, b] + ([bias] if bias is not None else [])
    especs = [_i0(tm, d), _0j(1, d), _0j(1, d)] + ([_0j(1, d)] if bias is not None else [])

    def epi(acc, e, o):
        r = ALPHA * e[0][...] + acc
        if bias is not None:
            r = r + e[3][...]
        y, xh, rstd = _layer_norm_rows(r, e[1][...], e[2][...])
        o[0][...] = y
        o[1][...] = y.astype(BF16)
        o[2][...] = xh
        o[3][...] = rstd

    return mm_nn(name, a, w, tm, d, tk, epi,
                 [_sds((s, d), F32), _sds((s, d), BF16), _sds((s, d), F32), _sds((s, 1), F32)],
                 [_i0(tm, d), _i0(tm, d), _i0(tm, d), _i0(tm, 1)], extras, especs, a_fn=a_fn)


def mm_plain_nn(name, a, w, out_dtype, tm=1024, tn=512, tk=1024, bias=None):
    m = a.shape[0]
    tm, tn = _tile(m, tm), _tile(w.n, tn)
    if w.kind == "col":
        tn = _tile(w.nloc, tn)

    def epi(acc, e, o):
        if bias is not None:
            acc = acc + e[0][...]
        o[0][...] = acc.astype(out_dtype)

    extras, especs = ([bias], [_0j(1, tn)]) if bias is not None else ((), ())
    return mm_nn(name, a, w, tm, tn, tk, epi, [_sds((m, w.n), out_dtype)], [_ij(tm, tn)], extras, especs)[0]


def mm_plain_nt(name, a, w, out_dtype, tm=1024, tn=512, tk=1024, add=None, add_scale=1.0, a_spec_fn=None):
    m = a.shape[0] if a_spec_fn is None else a_spec_fn[0]
    tm, tn = _tile(m, tm), _tile(w.k, tn)
    tk = _tile(w.n, tk)
    if w.kind == "col":
        tk = _tile(w.nloc, tk)
    if w.kind == "row" and not w.plain:
        tn = _tile(w.kloc, tn)

    def epi(acc, e, o):
        if add is not None:
            acc = acc + add_scale * e[0][...].astype(F32)
        o[0][...] = acc.astype(out_dtype)

    extras, especs = ([add], [_ij(tm, tn)]) if add is not None else ((), ())
    a_spec = None if a_spec_fn is None else a_spec_fn[1](tm, tk)
    return mm_nt(name, a, w, m, tm, tn, tk, epi, [_sds((m, w.k), out_dtype)], [_ij(tm, tn)], extras, especs,
                 a_spec=a_spec)[0]


def _rows(tm, c):
    return pl.BlockSpec((tm, c), lambda i: (i, 0))


def _fix(shape):
    nd = len(shape)
    return pl.BlockSpec(shape, lambda i: (0,) * nd)


def _accumulate(ref, val):
    @pl.when(pl.program_id(0) == 0)
    def _():
        ref[...] = jnp.zeros_like(ref)

    ref[...] += val


def _ln_backward_rows(dyv, xh, rstd, g, dr_ref, drb_ref, dg_ref, db_ref, ds_ref):
    dxh = dyv * g
    m1 = jnp.mean(dxh, axis=-1, keepdims=True)
    m2 = jnp.mean(dxh * xh, axis=-1, keepdims=True)
    dr = rstd * (dxh - m1 - xh * m2)
    dr_ref[...] = dr
    drb_ref[...] = dr.astype(BF16)
    _accumulate(dg_ref, jnp.sum(dyv * xh, axis=0, keepdims=True))
    _accumulate(db_ref, jnp.sum(dyv, axis=0, keepdims=True))
    _accumulate(ds_ref, jnp.sum(dr, axis=0, keepdims=True))


def mm_nt_ln_backward(name, a, w, add, xhat, rstd, g, tm=512, tk=1024, a_spec_fn=None):
    m, d = add.shape
    tm, tk = _tile(m, tm), _tile(w.n, tk)

    def epi(acc, e, o):
        _ln_backward_rows(acc + ALPHA * e[0][...], e[1][...], e[2][...], e[3][...], *o)

    vec = pl.BlockSpec((1, d), lambda i, j, k: (0, 0))
    a_spec = None if a_spec_fn is None else a_spec_fn(tm, tk)
    return mm_nt(name, a, w, m, tm, d, tk, epi,
                 [_sds((m, d), F32), _sds((m, d), BF16), _sds((1, d), F32), _sds((1, d), F32), _sds((1, d), F32)],
                 [_i0(tm, d), _i0(tm, d), vec, vec, vec], [add, xhat, rstd, g],
                 [_i0(tm, d), _i0(tm, d), _i0(tm, 1), vec], a_spec=a_spec, rows_in_order=True)


def loss_ln_backward(y, target, xhat, rstd, g, tm=512):
    s, d = y.shape
    tm = _tile(s, tm)

    def body(y_ref, t_ref, xh_ref, rstd_ref, g_ref, dr_ref, drb_ref, dg_ref, db_ref, ds_ref, loss_ref):
        e = y_ref[...] - t_ref[...]
        part = 0.5 * jnp.sum(jnp.mean(e * e, axis=-1, keepdims=True), axis=0, keepdims=True)
        _accumulate(loss_ref, jnp.broadcast_to(part, (1, d)))
        _ln_backward_rows(e * (1.0 / d), xh_ref[...], rstd_ref[...], g_ref[...], dr_ref, drb_ref, dg_ref, db_ref, ds_ref)

    return pl.pallas_call(
        body, grid=(s // tm,),
        in_specs=[_rows(tm, d), _rows(tm, d), _rows(tm, d), _rows(tm, 1), _fix((1, d))],
        out_specs=[_rows(tm, d), _rows(tm, d)] + [_fix((1, d))] * 4,
        out_shape=[_sds((s, d), F32), _sds((s, d), BF16)] + [_sds((1, d), F32)] * 4,
        compiler_params=_params(("arbitrary",)), name="loss_ln_backward")(y, target, xhat, rstd, g)


def _cols(s, tc, off=0):
    return pl.BlockSpec((s, tc), lambda i: (0, i + off))


def _shift_down(z, sft, rows):
    return jnp.where(rows >= sft, pltpu.roll(z, sft, 0), 0.0)


def _shift_up(z, sft, rows, s):
    return jnp.where(rows < s - sft, pltpu.roll(z, (s - sft) % s, 0), 0.0)


def short_conv_gate(u, conv_w, tc=256):
    s, d3 = u.shape
    d = d3 // 3
    nb = d // tc

    def body(b_ref, c_ref, h_ref, w_ref, o_ref):
        rows = lax.broadcasted_iota(jnp.int32, (s, tc), 0)
        z = c_ref[...] * h_ref[...]
        cz = jnp.zeros((s, tc), F32)
        for k in range(SC_WIDTH):
            sft = SC_WIDTH - 1 - k
            cz = cz + w_ref[pl.ds(k, 1), :] * (_shift_down(z, sft, rows) if sft else z)
        o_ref[...] = (b_ref[...] * cz).astype(BF16)

    return pl.pallas_call(
        body, grid=(nb,),
        in_specs=[_cols(s, tc), _cols(s, tc, nb), _cols(s, tc, 2 * nb), _cols(SC_WIDTH, tc)],
        out_specs=_cols(s, tc), out_shape=_sds((s, d), BF16),
        compiler_params=_params(("parallel",)), name="short_conv_gate")(u, u, u, conv_w)


def short_conv_gate_bwd(u, conv_w, dg, tc=256):
    s, d3 = u.shape
    d = d3 // 3
    nb = d // tc

    def body(b_ref, c_ref, h_ref, w_ref, dg_ref, du_ref, dw_ref):
        rows = lax.broadcasted_iota(jnp.int32, (s, tc), 0)
        c, h, dgv = c_ref[...], h_ref[...], dg_ref[...]
        z = c * h
        dcz = dgv * b_ref[...]
        cz = jnp.zeros((s, tc), F32)
        dz = jnp.zeros((s, tc), F32)
        for k in range(SC_WIDTH):
            sft = SC_WIDTH - 1 - k
            zs = _shift_down(z, sft, rows) if sft else z
            wk = w_ref[pl.ds(k, 1), :]
            cz = cz + wk * zs
            dz = dz + wk * (_shift_up(dcz, sft, rows, s) if sft else dcz)
            dw_ref[pl.ds(k, 1), :] = jnp.sum(dcz * zs, axis=0, keepdims=True)
        du_ref[0] = (dgv * cz).astype(BF16)
        du_ref[1] = (dz * h).astype(BF16)
        du_ref[2] = (dz * c).astype(BF16)

    return pl.pallas_call(
        body, grid=(nb,),
        in_specs=[_cols(s, tc), _cols(s, tc, nb), _cols(s, tc, 2 * nb), _cols(SC_WIDTH, tc), _cols(s, tc)],
        out_specs=[pl.BlockSpec((3, s, tc), lambda i: (0, 0, i)), _cols(SC_WIDTH, tc)],
        out_shape=[_sds((3, s, d), BF16), _sds((SC_WIDTH, d), F32)],
        compiler_params=_params(("parallel",)), name="short_conv_gate_bwd")(u, u, u, conv_w, dg)


def _store_shifted_down(ref, z, rows):
    s, tc = z.shape
    for b in range(8):
        ref[b, pl.ds(0, CONV_PAD), :] = jnp.zeros((CONV_PAD, tc), F32)
        ref[b, pl.ds(CONV_PAD, s), :] = z if b == 0 else _shift_down(z, b, rows)


def _store_shifted_up(ref, z, rows):
    s, tc = z.shape
    for b in range(8):
        ref[b, pl.ds(0, s), :] = z if b == 0 else _shift_up(z, b, rows, s)
        ref[b, pl.ds(s, CONV_PAD), :] = jnp.zeros((CONV_PAD, tc), F32)


def conformer_glu_conv(u, dw_w, dw_b, tc=128):
    s, d2 = u.shape
    d = d2 // 2
    nb = d // tc

    ch = min(CONV_CHUNK, s)

    def body(a_ref, g_ref, w_ref, b_ref, o_ref, down):
        rows = lax.broadcasted_iota(jnp.int32, (s, tc), 0)
        _store_shifted_down(down, a_ref[...] * jax.nn.sigmoid(g_ref[...]), rows)

        def chunk(ci, carry):
            r0 = pl.multiple_of(ci * ch, ch)
            acc = jnp.broadcast_to(b_ref[...], (ch, tc))
            for k in range(CONF_WIDTH):
                sft = CONF_WIDTH - 1 - k
                acc = acc + w_ref[pl.ds(k, 1), :] * down[sft % 8, pl.ds(CONV_PAD + r0 - (sft // 8) * 8, ch), :]
            o_ref[pl.ds(r0, ch), :] = acc
            return carry

        lax.fori_loop(0, s // ch, chunk, 0)

    return pl.pallas_call(
        body, grid=(nb,),
        in_specs=[_cols(s, tc), _cols(s, tc, nb), _cols(CONF_WIDTH, tc), _cols(1, tc)],
        out_specs=_cols(s, tc), out_shape=_sds((s, d), F32),
        scratch_shapes=[pltpu.VMEM((8, CONV_PAD + s, tc), F32)],
        compiler_params=_params(("parallel",)), name="conformer_glu_conv")(u, u, dw_w, dw_b)


def conformer_glu_conv_bwd(u, dw_w, dhc, tc=128):
    s, d2 = u.shape
    d = d2 // 2
    nb = d // tc
    ch = min(CONV_CHUNK, s)

    def body(a_ref, g_ref, w_ref, dhc_ref, du_ref, dbias_ref, dw_ref, db_ref, down, up, dw_acc, dh_buf):
        rows = lax.broadcasted_iota(jnp.int32, (s, tc), 0)
        a = a_ref[...]
        sg = jax.nn.sigmoid(g_ref[...])
        dhcv = dhc_ref[...]
        _store_shifted_down(down, a * sg, rows)
        _store_shifted_up(up, dhcv, rows)
        dw_acc[...] = jnp.zeros_like(dw_acc)

        def chunk(ci, carry):
            r0 = pl.multiple_of(ci * ch, ch)
            dc = dhc_ref[pl.ds(r0, ch), :]
            dh = jnp.zeros((ch, tc), F32)
            for k in range(CONF_WIDTH):
                sft = CONF_WIDTH - 1 - k
                a8, b = (sft // 8) * 8, sft % 8
                dh = dh + w_ref[pl.ds(k, 1), :] * up[b, pl.ds(r0 + a8, ch), :]
                prod = dc * down[b, pl.ds(CONV_PAD + r0 - a8, ch), :]
                dw_acc[k] += jnp.sum(prod.reshape(ch // 8, 8, tc), axis=0)
            dh_buf[pl.ds(r0, ch), :] = dh
            return carry

        lax.fori_loop(0, s // ch, chunk, 0)
        dh = dh_buf[...]
        da = dh * sg
        dgate = dh * a * sg * (1.0 - sg)
        du_ref[0] = da.astype(BF16)
        du_ref[1] = dgate.astype(BF16)
        dbias_ref[pl.ds(0, 1), :] = jnp.sum(da, axis=0, keepdims=True)
        dbias_ref[pl.ds(1, 1), :] = jnp.sum(dgate, axis=0, keepdims=True)
        db_ref[...] = jnp.sum(dhcv, axis=0, keepdims=True)
        for k in range(CONF_WIDTH):
            dw_ref[pl.ds(k, 1), :] = jnp.sum(dw_acc[k], axis=0, keepdims=True)

    return pl.pallas_call(
        body, grid=(nb,),
        in_specs=[_cols(s, tc), _cols(s, tc, nb), _cols(CONF_WIDTH, tc), _cols(s, tc)],
        out_specs=[pl.BlockSpec((2, s, tc), lambda i: (0, 0, i)), _cols(2, tc), _cols(CONF_WIDTH, tc), _cols(1, tc)],
        out_shape=[_sds((2, s, d), BF16), _sds((2, d), F32), _sds((CONF_WIDTH, d), F32), _sds((1, d), F32)],
        scratch_shapes=[pltpu.VMEM((8, CONV_PAD + s, tc), F32), pltpu.VMEM((8, CONV_PAD + s, tc), F32),
                        pltpu.VMEM((CONF_WIDTH + 1, 8, tc), F32), pltpu.VMEM((s, tc), F32)],
        compiler_params=_params(("parallel",)), name="conformer_glu_conv_bwd")(u, u, dw_w, dhc)


def conformer_norm_swish(hc, g, b, tm=512):
    s, d = hc.shape
    tm = _tile(s, tm)

    def body(h_ref, g_ref, b_ref, o_ref):
        n, _, _ = _layer_norm_rows(h_ref[...], g_ref[...], b_ref[...])
        o_ref[...] = (n * jax.nn.sigmoid(n)).astype(BF16)

    return pl.pallas_call(
        body, grid=(s // tm,), in_specs=[_rows(tm, d), _fix((1, d)), _fix((1, d))], out_specs=_rows(tm, d),
        out_shape=_sds((s, d), BF16), compiler_params=_params(("parallel",)), name="conformer_norm_swish")(hc, g, b)


def conformer_norm_swish_bwd(hc, g, b, ds, tm=512):
    s, d = hc.shape
    tm = _tile(s, tm)

    def body(h_ref, g_ref, b_ref, ds_ref, dh_ref, dg_ref, db_ref):
        n, nh, rstd = _layer_norm_rows(h_ref[...], g_ref[...], b_ref[...])
        sg = jax.nn.sigmoid(n)
        dn = ds_ref[...] * (sg * (1.0 + n * (1.0 - sg)))
        dnh = dn * g_ref[...]
        m1 = jnp.mean(dnh, axis=-1, keepdims=True)
        m2 = jnp.mean(dnh * nh, axis=-1, keepdims=True)
        dh_ref[...] = rstd * (dnh - m1 - nh * m2)
        _accumulate(dg_ref, jnp.sum(dn * nh, axis=0, keepdims=True))
        _accumulate(db_ref, jnp.sum(dn, axis=0, keepdims=True))

    return pl.pallas_call(
        body, grid=(s // tm,), in_specs=[_rows(tm, d), _fix((1, d)), _fix((1, d)), _rows(tm, d)],
        out_specs=[_rows(tm, d), _fix((1, d)), _fix((1, d))],
        out_shape=[_sds((s, d), F32), _sds((1, d), F32), _sds((1, d), F32)],
        compiler_params=_params(("arbitrary",)), name="conformer_norm_swish_bwd")(hc, g, b, ds)


def _swap_halves(x):
    lane = lax.broadcasted_iota(jnp.int32, x.shape, 1)
    return jnp.where(lane < QK_ROPE // 2, pltpu.roll(x, 128 - QK_ROPE // 2, 1), pltpu.roll(x, QK_ROPE // 2, 1))


def _rope(x, cf, sf):
    return x * cf + _swap_halves(x) * sf


def _unrope(dx, cf, sf):
    return dx * cf - _swap_halves(dx) * sf


def _rms_rows(x, g):
    r = lax.rsqrt(jnp.mean(x * x, axis=-1, keepdims=True) + RMS_EPS)
    return x * r, r


def mla_latents(t, g_q, g_kv, cf, sf, tm=512):
    s = t.shape[0]
    tm = _tile(s, tm)

    def body(t_ref, gq_ref, gkv_ref, cf_ref, sf_ref, cq_ref, ckv_ref, kpe_ref):
        xq, _ = _rms_rows(t_ref[:, 0:Q_LORA], gq_ref[...])
        cq_ref[...] = (xq * gq_ref[...]).astype(BF16)
        xkv, _ = _rms_rows(t_ref[:, Q_LORA:Q_LORA + KV_LORA], gkv_ref[...])
        ckv_ref[...] = (xkv * gkv_ref[...]).astype(BF16)
        kpe_ref[...] = _rope(t_ref[:, Q_LORA + KV_LORA:], cf_ref[...], sf_ref[...]).astype(BF16)

    w = Q_LORA + KV_LORA + 128
    return pl.pallas_call(
        body, grid=(s // tm,),
        in_specs=[_rows(tm, w), _fix((1, Q_LORA)), _fix((1, KV_LORA)), _rows(tm, 128), _rows(tm, 128)],
        out_specs=[_rows(tm, Q_LORA), _rows(tm, KV_LORA), _rows(tm, 128)],
        out_shape=[_sds((s, Q_LORA), BF16), _sds((s, KV_LORA), BF16), _sds((s, 128), BF16)],
        compiler_params=_params(("parallel",)), name="mla_latents")(t, g_q, g_kv, cf, sf)


def mla_latents_bwd(t, g_q, g_kv, cf, sf, dcq, dckv, dkpe, tm=512):
    s = t.shape[0]
    tm = _tile(s, tm)
    w = Q_LORA + KV_LORA + 128

    def rms_bwd(x, g, dy):
        xh, r = _rms_rows(x, g)
        dxh = dy * g
        return r * (dxh - xh * jnp.mean(dxh * xh, axis=-1, keepdims=True)), jnp.sum(dy * xh, axis=0, keepdims=True)

    def body(t_ref, gq_ref, gkv_ref, cf_ref, sf_ref, dcq_ref, dckv_ref, dkpe_ref, dt_ref, dgq_ref, dgkv_ref):
        dxq, dgq = rms_bwd(t_ref[:, 0:Q_LORA], gq_ref[...], dcq_ref[...])
        dxkv, dgkv = rms_bwd(t_ref[:, Q_LORA:Q_LORA + KV_LORA], gkv_ref[...], dckv_ref[...])
        dt_ref[:, 0:Q_LORA] = dxq.astype(BF16)
        dt_ref[:, Q_LORA:Q_LORA + KV_LORA] = dxkv.astype(BF16)
        dt_ref[:, Q_LORA + KV_LORA:] = _unrope(dkpe_ref[...], cf_ref[...], sf_ref[...]).astype(BF16)
        _accumulate(dgq_ref, dgq)
        _accumulate(dgkv_ref, dgkv)

    return pl.pallas_call(
        body, grid=(s // tm,),
        in_specs=[_rows(tm, w), _fix((1, Q_LORA)), _fix((1, KV_LORA)), _rows(tm, 128), _rows(tm, 128),
                  _rows(tm, Q_LORA), _rows(tm, KV_LORA), _rows(tm, 128)],
        out_specs=[_rows(tm, w), _fix((1, Q_LORA)), _fix((1, KV_LORA))],
        out_shape=[_sds((s, w), BF16), _sds((1, Q_LORA), F32), _sds((1, KV_LORA), F32)],
        compiler_params=_params(("arbitrary",)), name="mla_latents_bwd")(t, g_q, g_kv, cf, sf, dcq, dckv, dkpe)


def mla_queries(cq, w_uq, cf, sf, tm=2048):
    s = cq.shape[0]
    tm = _tile(s, tm)

    def epi(acc, e, o):
        o[0][:, 0:QK_NOPE] = acc[:, 0:QK_NOPE].astype(BF16)
        o[0][:, QK_NOPE:] = _rope(acc[:, QK_NOPE:], e[0][...], e[1][...]).astype(BF16)

    return mm_nn("mla_queries", cq, w_uq, tm, HEAD_PAD, Q_LORA, epi, [_sds((s, N_HEADS * HEAD_PAD), BF16)],
                 [_ij(tm, HEAD_PAD)], [cf, sf], [_i0(tm, 128), _i0(tm, 128)])[0]


def mla_keys(ckv, w_uk, kpe, tm=2048):
    s = ckv.shape[0]
    tm = _tile(s, tm)

    def epi(acc, e, o):
        o[0][:, 0:QK_NOPE] = acc.astype(BF16)
        o[0][:, QK_NOPE:] = e[0][...]

    return mm_nn("mla_keys", ckv, w_uk, tm, QK_NOPE, KV_LORA, epi, [_sds((s, N_HEADS * HEAD_PAD), BF16)],
                 [_ij(tm, HEAD_PAD)], [kpe], [_i0(tm, 128)])[0]


def _masked_scores(q, k, tq, kv):
    sc = lax.dot_general(q, k, NT, preferred_element_type=F32) * ATTN_SCALE
    row = lax.broadcasted_iota(jnp.int32, (tq, tq), 0)
    col = lax.broadcasted_iota(jnp.int32, (tq, tq), 1)
    ok = lax.shift_right_logical(col, CHUNK_SHIFT) <= lax.shift_right_logical(row, CHUNK_SHIFT)
    own = jnp.where(ok, sc[:, kv - tq:], -1e30)
    return own if kv == tq else jnp.concatenate([sc[:, :kv - tq], own], axis=1)


def attention(q, k, v, tq=512):
    s = q.shape[0]
    tq = _tile(s, tq)
    nq = s // tq

    def body(q_ref, k_ref, v_ref, o_ref):
        for qi in range(nq):
            kv = (qi + 1) * tq
            sc = _masked_scores(q_ref[pl.ds(qi * tq, tq), :], k_ref[pl.ds(0, kv), :], tq, kv)
            p = jnp.exp(sc - jnp.max(sc, axis=-1, keepdims=True))
            o = lax.dot_general(p.astype(BF16), v_ref[pl.ds(0, kv), :], NN, preferred_element_type=F32)
            o_ref[pl.ds(qi * tq, tq), :] = (o / jnp.sum(p, axis=-1, keepdims=True)).astype(BF16)

    hq = pl.BlockSpec((s, HEAD_PAD), lambda h: (0, h))
    hv = pl.BlockSpec((s, V_HEAD), lambda h: (0, h))
    return pl.pallas_call(
        body, grid=(N_HEADS,), in_specs=[hq, hq, hv], out_specs=hv, out_shape=_sds((s, N_HEADS * V_HEAD), BF16),
        compiler_params=_params(("parallel",)), name="attention")(q, k, v)


def attention_bwd(q, k, v, do, tq=512):
    s = q.shape[0]
    tq = _tile(s, tq)
    nq = s // tq

    def body(q_ref, k_ref, v_ref, do_ref, dq_ref, dk_ref, dv_ref, dk_acc, dv_acc):
        dk_acc[...] = jnp.zeros_like(dk_acc)
        dv_acc[...] = jnp.zeros_like(dv_acc)
        for qi in range(nq):
            kv = (qi + 1) * tq
            qt = q_ref[pl.ds(qi * tq, tq), :]
            kt = k_ref[pl.ds(0, kv), :]
            dot = do_ref[pl.ds(qi * tq, tq), :]
            sc = _masked_scores(qt, kt, tq, kv)
            p = jnp.exp(sc - jnp.max(sc, axis=-1, keepdims=True))
            p = p / jnp.sum(p, axis=-1, keepdims=True)
            dp = lax.dot_general(dot, v_ref[pl.ds(0, kv), :], NT, preferred_element_type=F32)
            delta = jnp.sum(p * dp, axis=-1, keepdims=True)
            ds = (p * (dp - delta) * ATTN_SCALE).astype(BF16)
            dq_ref[pl.ds(qi * tq, tq), :] = lax.dot_general(ds, kt, NN, preferred_element_type=F32).astype(BF16)
            dk_acc[pl.ds(0, kv), :] += lax.dot_general(ds, qt, TN, preferred_element_type=F32)
            dv_acc[pl.ds(0, kv), :] += lax.dot_general(p.astype(BF16), dot, TN, preferred_element_type=F32)
        dk_ref[...] = dk_acc[...].astype(BF16)
        dv_ref[...] = dv_acc[...].astype(BF16)

    hq = pl.BlockSpec((s, HEAD_PAD), lambda h: (0, h))
    hv = pl.BlockSpec((s, V_HEAD), lambda h: (0, h))
    return pl.pallas_call(
        body, grid=(N_HEADS,), in_specs=[hq, hq, hv, hv], out_specs=[hq, hq, hv],
        out_shape=[_sds((s, N_HEADS * HEAD_PAD), BF16), _sds((s, N_HEADS * HEAD_PAD), BF16),
                   _sds((s, N_HEADS * V_HEAD), BF16)],
        scratch_shapes=[pltpu.VMEM((s, HEAD_PAD), F32), pltpu.VMEM((s, V_HEAD), F32)],
        compiler_params=_params(("parallel",)), name="attention_bwd")(q, k, v, do)


def mla_unrope_grads(dq, dk, cf, sf, tm=512):
    s = dq.shape[0]
    tm = _tile(s, tm)

    def body(dq_ref, dk_ref, cf_ref, sf_ref, dql_ref, dkn_ref, dkpe_ref):
        cfv, sfv = cf_ref[...], sf_ref[...]
        dkpe = jnp.zeros((tm, 128), F32)
        for h in range(N_HEADS):
            lo = h * HEAD_PAD
            dql_ref[:, lo:lo + QK_NOPE] = dq_ref[:, lo:lo + QK_NOPE]
            dql_ref[:, lo + QK_NOPE:lo + HEAD_PAD] = _unrope(
                dq_ref[:, lo + QK_NOPE:lo + HEAD_PAD].astype(F32), cfv, sfv).astype(BF16)
            dkn_ref[:, h * QK_NOPE:(h + 1) * QK_NOPE] = dk_ref[:, lo:lo + QK_NOPE]
            dkpe = dkpe + dk_ref[:, lo + QK_NOPE:lo + HEAD_PAD].astype(F32)
        dkpe_ref[...] = dkpe

    wq = N_HEADS * HEAD_PAD
    return pl.pallas_call(
        body, grid=(s // tm,), in_specs=[_rows(tm, wq), _rows(tm, wq), _rows(tm, 128), _rows(tm, 128)],
        out_specs=[_rows(tm, wq), _rows(tm, N_HEADS * QK_NOPE), _rows(tm, 128)],
        out_shape=[_sds((s, wq), BF16), _sds((s, N_HEADS * QK_NOPE), BF16), _sds((s, 128), F32)],
        compiler_params=_params(("parallel",)), name="mla_unrope_grads")(dq, dk, cf, sf)


ANY = pl.BlockSpec(memory_space=pl.ANY)
GATHER_ID = 1
CHIP_EXCHANGE_ID = 2
PAIR_ID = 3
ALL_ID = 4


def _nbytes(a):
    return a.size * a.dtype.itemsize


def _copy_cost(operand_bytes, sent_fraction):
    sent = int(operand_bytes * sent_fraction)
    return pl.CostEstimate(flops=0, transcendentals=0, bytes_accessed=2 * sent, remote_bytes_transferred=sent)


def _handshake(peers):
    barrier = pltpu.get_barrier_semaphore()
    for peer in peers:
        pl.semaphore_signal(barrier, inc=1, device_id=peer, device_id_type=MESH)
    pl.semaphore_wait(barrier, len(peers))


def _place():
    x, y, c = lax.axis_index("x"), lax.axis_index("y"), lax.axis_index("c")
    chips = [(1 - x, y), (x, 1 - y), (1 - x, 1 - y)]
    return x, y, c, chips


def _half(ref, hc, axis=0):
    n = ref.shape[axis] // 2
    idx = (slice(None),) * axis + (pl.ds(hc * n, n),)
    return ref.at[idx]


def gather_shards(name, tensors, by_columns=()):
    nt = len(tensors)

    def body(*refs):
        a, g = refs[:nt], refs[nt:2 * nt]
        send, recv = refs[2 * nt:]
        x, y, c, _ = _place()
        q = 2 * x + y
        sib, xn, yn = (x, y, 1 - c), (1 - x, y, c), (x, 1 - y, c)
        q_xn, q_yn, q_diag = 2 * (1 - x) + y, 2 * x + 1 - y, 2 * (1 - x) + 1 - y
        _handshake([sib, xn, yn])

        def whole(t, p):
            if t in by_columns:
                n = a[t].shape[1]
                return g[t].at[:, pl.ds(p * n, n)]
            return g[t].at[p]

        def part(t, p, hc, quarter=None):
            rows = a[t].shape[0]
            if quarter is None:
                return whole(t, p).at[pl.ds(hc * (rows // 2), rows // 2)]
            return whole(t, p).at[pl.ds(hc * (rows // 2) + quarter * (rows // 4), rows // 4)]

        def rc(t, k, src, dst, to):
            return pltpu.make_async_remote_copy(src_ref=src, dst_ref=dst, send_sem=send.at[t, k], recv_sem=recv.at[t, k],
                                                device_id=to, device_id_type=MESH)

        sent = []

        def go(cp):
            cp.start()
            sent.append(cp)

        def landed(t, k, piece, frm):
            rc(t, k, piece, piece, frm).wait_recv()
            return piece

        for t in range(nt):
            go(rc(t, 8, a[t], whole(t, q), sib))
            mine = _half(a[t], c)
            go(rc(t, 0, mine, part(t, q, c), xn))
            go(rc(t, 1, mine, part(t, q, c), yn))
        for t in range(nt):
            from_y = landed(t, 1, part(t, q_yn, c), yn)
            go(rc(t, 2, part(t, q_yn, c, 0), part(t, q_yn, c, 0), xn))
            go(rc(t, 5, from_y, from_y, sib))
            from_x = landed(t, 0, part(t, q_xn, c), xn)
            go(rc(t, 3, part(t, q_xn, c, 1), part(t, q_xn, c, 1), yn))
            go(rc(t, 4, from_x, from_x, sib))
        for t in range(nt):
            for k, frm in ((2, xn), (3, yn)):
                piece = landed(t, k, part(t, q_diag, c, k - 2), frm)
                go(rc(t, 4 + k, piece, piece, sib))
        for t in range(nt):
            landed(t, 4, part(t, q_xn, 1 - c), sib)
            landed(t, 5, part(t, q_yn, 1 - c), sib)
            landed(t, 6, part(t, q_diag, 1 - c, 0), sib)
            landed(t, 7, part(t, q_diag, 1 - c, 1), sib)
            landed(t, 8, whole(t, q), sib)
        for cp in sent:
            cp.wait_send()

    return pl.kernel(
        body, name=name,
        out_type=[_sds((a.shape[0], N_CHIPS * a.shape[1]) if t in by_columns else (N_CHIPS,) + a.shape, a.dtype)
                  for t, a in enumerate(tensors)],
        mesh=plsc.ScalarSubcoreMesh(axis_name="sequencer", num_cores=1),
        scratch_types=[pltpu.SemaphoreType.DMA((nt, 9)), pltpu.SemaphoreType.DMA((nt, 9))],
        cost_estimate=_copy_cost(sum(_nbytes(a) for a in tensors), 4),
        compiler_params=pltpu.CompilerParams(collective_id=GATHER_ID))(*tensors)


def pair_exchange(name, grads, on_sequencer):
    nt = len(grads)

    def body(*refs):
        g, theirs = refs[:nt], refs[nt:2 * nt]
        send, recv = refs[2 * nt:]
        x, y, c, _ = _place()
        if on_sequencer:
            _handshake([(x, y, 1 - c)])
        cps = []
        for t in range(nt):
            cp = pltpu.make_async_remote_copy(src_ref=_half(g[t], 1 - c, 1), dst_ref=theirs[t], send_sem=send.at[t],
                                              recv_sem=recv.at[t], device_id=(x, y, 1 - c), device_id_type=MESH)
            cp.start()
            cps.append(cp)
        for cp in cps:
            cp.wait()

    if not on_sequencer:
        return pl.pallas_call(
            body, in_specs=[ANY] * nt, out_specs=[ANY] * nt,
            out_shape=[_sds((N_CHIPS, a.shape[1] // 2, a.shape[2]), a.dtype) for a in grads],
            scratch_shapes=[pltpu.SemaphoreType.DMA((nt,)), pltpu.SemaphoreType.DMA((nt,))],
            name=name)(*grads)
    return pl.kernel(
        body, name=name, out_type=[_sds((N_CHIPS, a.shape[1] // 2, a.shape[2]), a.dtype) for a in grads],
        mesh=plsc.ScalarSubcoreMesh(axis_name="sequencer", num_cores=1),
        scratch_types=[pltpu.SemaphoreType.DMA((nt,)), pltpu.SemaphoreType.DMA((nt,))],
        cost_estimate=_copy_cost(sum(_nbytes(a) for a in grads), 0.5),
        compiler_params=pltpu.CompilerParams(collective_id=PAIR_ID))(*grads)


def chip_exchange(name, parts):
    nt = len(parts)

    def body(*refs):
        a, r = refs[:nt], refs[nt:2 * nt]
        send, recv = refs[2 * nt:]
        x, y, c, chips = _place()
        _handshake([(*chip, c) for chip in chips])
        cps = []
        for t in range(nt):
            for j, chip in enumerate(chips):
                cp = pltpu.make_async_remote_copy(
                    src_ref=a[t].at[2 * chip[0] + chip[1]], dst_ref=r[t].at[j], send_sem=send.at[t, j],
                    recv_sem=recv.at[t, j], device_id=(*chip, c), device_id_type=MESH)
                cp.start()
                cps.append(cp)
        for cp in cps:
            cp.wait()

    return pl.kernel(
        body, name=name, out_type=[_sds((N_CHIPS - 1,) + a.shape[1:], a.dtype) for a in parts],
        mesh=plsc.ScalarSubcoreMesh(axis_name="sequencer", num_cores=1),
        scratch_types=[pltpu.SemaphoreType.DMA((nt, 3)), pltpu.SemaphoreType.DMA((nt, 3))],
        cost_estimate=_copy_cost(sum(_nbytes(a) for a in parts), 0.75),
        compiler_params=pltpu.CompilerParams(collective_id=CHIP_EXCHANGE_ID))(*parts)


def pair_share(name, halves):
    nt = len(halves)

    def body(*refs):
        h, other = refs[:nt], refs[nt:2 * nt]
        send, recv = refs[2 * nt:]
        x, y, c, _ = _place()
        _handshake([(x, y, 1 - c)])
        cps = []
        for t in range(nt):
            cp = pltpu.make_async_remote_copy(src_ref=h[t], dst_ref=other[t], send_sem=send.at[t], recv_sem=recv.at[t],
                                              device_id=(x, y, 1 - c), device_id_type=MESH)
            cp.start()
            cps.append(cp)
        for cp in cps:
            cp.wait()

    return pl.kernel(
        body, name=name, out_type=[_sds(a.shape, a.dtype) for a in halves],
        mesh=plsc.ScalarSubcoreMesh(axis_name="sequencer", num_cores=1),
        scratch_types=[pltpu.SemaphoreType.DMA((nt,)), pltpu.SemaphoreType.DMA((nt,))],
        cost_estimate=_copy_cost(sum(_nbytes(a) for a in halves), 1),
        compiler_params=pltpu.CompilerParams(collective_id=PAIR_ID))(*halves)


def pack_rows(name, parts, rows):
    cdim = parts[0].shape[1]
    n = len(parts)
    vm = pl.BlockSpec(memory_space=pltpu.VMEM)

    def pack(*refs):
        p, o_ref = refs[:n], refs[n]
        at = 0
        for ref in p:
            o_ref[pl.ds(at, ref.shape[0]), :] = ref[...]
            at += ref.shape[0]
        o_ref[pl.ds(at, rows - at), :] = jnp.zeros((rows - at, cdim), F32)

    return pl.pallas_call(pack, in_specs=[vm] * n, out_specs=vm, out_shape=_sds((rows, cdim), F32), name=name)(*parts)


def all_reduce_small(parts, rows):
    cdim = parts[0].shape[1]
    vm = pl.BlockSpec(memory_space=pltpu.VMEM)
    mine = pack_rows("small_pack", parts, rows)

    def exchange(mine_ref, buf, send, recv, lsem):
        x, y, c, _ = _place()
        me = 4 * x + 2 * y + c
        peers = [(x ^ (k >> 2), y ^ ((k >> 1) & 1), c ^ (k & 1)) for k in range(1, 8)]
        _handshake(peers)
        own = pltpu.make_async_copy(mine_ref, buf.at[me], lsem)
        own.start()
        cps = []
        for k, to in enumerate(peers):
            cp = pltpu.make_async_remote_copy(src_ref=mine_ref, dst_ref=buf.at[me], send_sem=send.at[k], recv_sem=recv.at[k],
                                              device_id=to, device_id_type=MESH)
            cp.start()
            cps.append(cp)
        for k, (px, py, pc) in enumerate(peers):
            pltpu.make_async_remote_copy(src_ref=mine_ref, dst_ref=buf.at[4 * px + 2 * py + pc], send_sem=send.at[k],
                                         recv_sem=recv.at[k], device_id=(x, y, c), device_id_type=MESH).wait_recv()
        for cp in cps:
            cp.wait_send()
        own.wait()

    landed = pl.kernel(
        exchange, name="small_exchange", out_type=_sds((8, rows, cdim), F32),
        mesh=plsc.ScalarSubcoreMesh(axis_name="sequencer", num_cores=1),
        scratch_types=[pltpu.SemaphoreType.DMA((7,)), pltpu.SemaphoreType.DMA((7,)), pltpu.SemaphoreType.DMA],
        cost_estimate=_copy_cost(rows * cdim * 4, 7),
        compiler_params=pltpu.CompilerParams(collective_id=ALL_ID))(mine)

    def total(buf, o_ref):
        acc = buf[0]
        for d in range(1, 8):
            acc = acc + buf[d]
        o_ref[...] = acc

    return pl.pallas_call(total, in_specs=[vm], out_specs=vm, out_shape=_sds((rows, cdim), F32), name="small_sum")(landed)


def pair_sum(g, theirs, core, tm=256):
    _, r, c = g.shape
    tm = _tile(r // 2, tm)
    nh = r // 2 // tm

    def body(core_ref, a_ref, b_ref, o_ref):
        o_ref[...] = (a_ref[...].astype(F32) + b_ref[...].astype(F32)).astype(BF16)

    blk = (N_CHIPS, tm, c)
    return pl.pallas_call(
        body, grid_spec=pltpu.PrefetchScalarGridSpec(
            num_scalar_prefetch=1, grid=(nh,),
            in_specs=[pl.BlockSpec(blk, lambda i, cr: (0, cr[0] * nh + i, 0)), pl.BlockSpec(blk, lambda i, cr: (0, i, 0))],
            out_specs=pl.BlockSpec(blk, lambda i, cr: (0, i, 0))),
        out_shape=_sds(theirs.shape, BF16), compiler_params=_params(("parallel",)), name="pair_sum")(core, g, theirs)


def chip_sum(own, landed, chip, stack, layer, layers, tm=256):
    _, r, c = own.shape
    tm = _tile(r, tm)

    def body(chip_ref, own_ref, l_ref, *rest):
        acc = own_ref[...].astype(F32)
        for j in range(N_CHIPS - 1):
            acc = acc + l_ref[j].astype(F32)
        rest[-1][...] = acc

    in_specs = [pl.BlockSpec((None, tm, c), lambda i, qr: (qr[0], i, 0)),
                pl.BlockSpec((N_CHIPS - 1, tm, c), lambda i, qr: (0, i, 0))]
    args = [chip, own, landed]
    if stack is not None:
        in_specs.append(ANY)
        args.append(stack)
    return pl.pallas_call(
        body, grid_spec=pltpu.PrefetchScalarGridSpec(
            num_scalar_prefetch=1, grid=(r // tm,), in_specs=in_specs,
            out_specs=pl.BlockSpec((None, tm, c), lambda i, qr: (layer, i, 0))),
        out_shape=_sds((layers, r, c), F32), input_output_aliases={3: 0} if stack is not None else {},
        compiler_params=_params(("parallel",)), name="chip_sum")(*args)


def _adamw_math(w, g, m, v):
    bc1 = 1.0 - ADAM_B1 ** ADAM_STEP
    bc2 = 1.0 - ADAM_B2 ** ADAM_STEP
    nm = ADAM_B1 * m + (1.0 - ADAM_B1) * g
    nv = ADAM_B2 * v + (1.0 - ADAM_B2) * (g * g)
    return -ADAM_LR * ((nm / bc1) / (jnp.sqrt(nv / bc2) + ADAM_EPS) + ADAM_WD * w), nm, nv


def vector_update(red, chip, ws, ms, vs, where):
    n = len(ws)
    dd = red.shape[1]

    def body(chip_ref, red_ref, *refs):
        w_r, m_r, v_r = refs[0:n], refs[n:2 * n], refs[2 * n:3 * n]
        g_o, d_o, m_o, v_o = (refs[(3 + k) * n:(4 + k) * n] for k in range(4))
        q = chip_ref[0]

        def chip_block(val, width):
            out = val[:, 0:width]
            for p in range(1, val.shape[1] // width):
                out = jnp.where(q == p, val[:, p * width:(p + 1) * width], out)
            return out

        for k in range(n):
            for idx, r0, nr, cols in where[k]:
                width = w_r[k].shape[-1]
                if cols == "chip" and width * N_CHIPS != dd:
                    g = chip_block(jnp.concatenate([red_ref[pl.ds(r0 + j, 1), :] for j in range(nr)], axis=1), width)
                else:
                    g = red_ref[pl.ds(r0, nr), :]
                    g = chip_block(g, width) if cols == "chip" else g if cols == "all" else g[:, 0:cols]
                delta, nm, nv = _adamw_math(w_r[k][idx], g, m_r[k][idx], v_r[k][idx])
                g_o[k][idx] = g
                d_o[k][idx] = delta
                m_o[k][idx] = nm
                v_o[k][idx] = nv

    vm = pl.BlockSpec(memory_space=pltpu.VMEM)
    outs = pl.pallas_call(
        body, in_specs=[pl.BlockSpec(memory_space=pltpu.SMEM), vm] + [vm] * (3 * n), out_specs=[vm] * (4 * n),
        out_shape=[_sds(w.shape, F32) for w in ws] * 4, name="vector_update")(chip, red, *ws, *ms, *vs)
    return [outs[k * n:(k + 1) * n] for k in range(4)]


def adamw_joined(w, m, v, g_mine, g_theirs, core, tm=512):
    nl, r, c = w.shape
    tm = _tile(r // 2, tm)
    nh = r // 2 // tm

    def body(core_ref, w_ref, m_ref, v_ref, gm_ref, gt_ref, g_ref, d_ref, nm_ref, nv_ref):
        mine = (pl.program_id(1) // nh) == core_ref[0]
        gv = jnp.where(mine, gm_ref[...], gt_ref[...])
        g_ref[...] = gv
        d_ref[...], nm_ref[...], nv_ref[...] = _adamw_math(w_ref[...], gv, m_ref[...], v_ref[...])

    full = pl.BlockSpec((None, tm, c), lambda l, i, cr: (l, i, 0))
    half = pl.BlockSpec((None, tm, c), lambda l, i, cr: (l, i % nh, 0))
    return pl.pallas_call(
        body, grid_spec=pltpu.PrefetchScalarGridSpec(
            num_scalar_prefetch=1, grid=(nl, r // tm), in_specs=[full, full, full, half, half], out_specs=[full] * 4),
        out_shape=[_sds((nl, r, c), F32)] * 4, compiler_params=_params(("parallel", "parallel")),
        name="adamw_joined")(core, w, m, v, g_mine, g_theirs)


WEIGHTS = ['sc_w_in', 'sc_conv_w', 'sc_w_out', 'mla_w_dq', 'mla_g_q', 'mla_w_uq', 'mla_w_dkv', 'mla_g_kv', 'mla_w_uk',
           'mla_w_uv', 'mla_w_o', 'cf_w_pw1', 'cf_b_pw1', 'cf_dw_w', 'cf_dw_b', 'cf_norm_g', 'cf_norm_b', 'cf_w_pw2',
           'cf_b_pw2', 'ff_w1', 'ff_w2', 'ln_mix_g', 'ln_mix_b', 'ln_ff_g', 'ln_ff_b']
ARGS = ['x'] + WEIGHTS + ['loss_target'] + ['m_' + n for n in WEIGHTS] + ['v_' + n for n in WEIGHTS]


def _sq_relu(h):
    r = jnp.maximum(h, jnp.zeros_like(h))
    return r * r


def _mlp_forward(i, x, xb, w1, w2, g, b):
    hb = mm_plain_nn(f"mlp{i}_up", xb, w1, BF16, tm=2048, tn=1024)
    y, yb, xh, rstd = mm_residual_ln(f"mlp{i}_down_ln", hb, w2, x, g, b, tk=4096, a_fn=_sq_relu)
    return (y, yb), dict(xb=xb, hb=hb, xh=xh, rstd=rstd, g=g)


def _mlp_backward(i, dr, drb, sv, w1, w2, dw1, dw2, reduce_after, mixer_ln):
    s = dr.shape[0]
    tm, tn = _tile(s, 1024), 1024

    def epi(acc, e, o):
        o[0][...] = (acc * (2.0 * jnp.maximum(e[0][...].astype(F32), 0.0))).astype(BF16)

    dhb = mm_nt(f"mlp{i}_down_bwd", drb, w2, s, tm, tn, 1024, epi, [_sds((s, w2.k), BF16)], [_ij(tm, tn)],
                [sv["hb"]], [_ij(tm, tn)])[0]
    g_w2 = mm_tn(f"mlp{i}_dw2", sv["hb"], drb, dw2, s, 1024, 1024, a_fn=_sq_relu)
    g_w1 = mm_tn(f"mlp{i}_dw1", sv["xb"], dhb, dw1, s, 1024, 1024)
    dhb = reduce_after(dhb, {f"w1_{i}": g_w1, f"w2_{i}": g_w2})
    return mm_nt_ln_backward(f"mlp{i}_up_bwd", dhb, w1, dr, *mixer_ln, tk=2048)


def kernel(x, sc_w_in, sc_conv_w, sc_w_out, mla_w_dq, mla_g_q, mla_w_uq, mla_w_dkv, mla_g_kv, mla_w_uk, mla_w_uv, mla_w_o, cf_w_pw1, cf_b_pw1, cf_dw_w, cf_dw_b, cf_norm_g, cf_norm_b, cf_w_pw2, cf_b_pw2, ff_w1, ff_w2, ln_mix_g, ln_mix_b, ln_ff_g, ln_ff_b, loss_target, m_sc_w_in, m_sc_conv_w, m_sc_w_out, m_mla_w_dq, m_mla_g_q, m_mla_w_uq, m_mla_w_dkv, m_mla_g_kv, m_mla_w_uk, m_mla_w_uv, m_mla_w_o, m_cf_w_pw1, m_cf_b_pw1, m_cf_dw_w, m_cf_dw_b, m_cf_norm_g, m_cf_norm_b, m_cf_w_pw2, m_cf_b_pw2, m_ff_w1, m_ff_w2, m_ln_mix_g, m_ln_mix_b, m_ln_ff_g, m_ln_ff_b, v_sc_w_in, v_sc_conv_w, v_sc_w_out, v_mla_w_dq, v_mla_g_q, v_mla_w_uq, v_mla_w_dkv, v_mla_g_kv, v_mla_w_uk, v_mla_w_uv, v_mla_w_o, v_cf_w_pw1, v_cf_b_pw1, v_cf_dw_w, v_cf_dw_b, v_cf_norm_g, v_cf_norm_b, v_cf_w_pw2, v_cf_b_pw2, v_ff_w1, v_ff_w2, v_ln_mix_g, v_ln_mix_b, v_ln_ff_g, v_ln_ff_b):
    given = dict(zip(ARGS, (x, sc_w_in, sc_conv_w, sc_w_out, mla_w_dq, mla_g_q, mla_w_uq, mla_w_dkv, mla_g_kv, mla_w_uk, mla_w_uv, mla_w_o, cf_w_pw1, cf_b_pw1, cf_dw_w, cf_dw_b, cf_norm_g, cf_norm_b, cf_w_pw2, cf_b_pw2, ff_w1, ff_w2, ln_mix_g, ln_mix_b, ln_ff_g, ln_ff_b, loss_target, m_sc_w_in, m_sc_conv_w, m_sc_w_out, m_mla_w_dq, m_mla_g_q, m_mla_w_uq, m_mla_w_dkv, m_mla_g_kv, m_mla_w_uk, m_mla_w_uv, m_mla_w_o, m_cf_w_pw1, m_cf_b_pw1, m_cf_dw_w, m_cf_dw_b, m_cf_norm_g, m_cf_norm_b, m_cf_w_pw2, m_cf_b_pw2, m_ff_w1, m_ff_w2, m_ln_mix_g, m_ln_mix_b, m_ln_ff_g, m_ln_ff_b, v_sc_w_in, v_sc_conv_w, v_sc_w_out, v_mla_w_dq, v_mla_g_q, v_mla_w_uq, v_mla_w_dkv, v_mla_g_kv, v_mla_w_uk, v_mla_w_uv, v_mla_w_o, v_cf_w_pw1, v_cf_b_pw1, v_cf_dw_w, v_cf_dw_b, v_cf_norm_g, v_cf_norm_b, v_cf_w_pw2, v_cf_b_pw2, v_ff_w1, v_ff_w2, v_ln_mix_g, v_ln_mix_b, v_ln_ff_g, v_ln_ff_b)))
    s, d = x.shape[1], x.shape[2]
    d_ff = 4 * d
    dq4 = d // N_CHIPS
    xq = lax.axis_index("x") * 2 + lax.axis_index("y")

    w_dkv_pad = jnp.pad(mla_w_dkv[0], ((0, 0), (0, 128 - QK_ROPE)))
    w_uq_pad = jnp.pad(mla_w_uq[0].reshape(Q_LORA, 2, QK_NOPE + QK_ROPE), ((0, 0), (0, 0), (0, HEAD_PAD - QK_NOPE - QK_ROPE)))
    small = pack_rows("vector_weights_pack", [
        sc_conv_w.reshape(2 * SC_WIDTH, dq4), cf_b_pw1.reshape(2, dq4), cf_dw_w[0], cf_dw_b, cf_norm_g, cf_norm_b,
        cf_b_pw2], 64)
    mlp_w = lambda i: [ff_w1[i].astype(BF16), ff_w2[i].astype(BF16)]
    g_in, g_out, g_w1, g_w2 = [None] * 2, [None] * 2, [None] * DEPTH, [None] * DEPTH
    g_in[0], g_out[0], g_small = gather_shards(
        "gather_mixer0", [sc_w_in[0].astype(BF16), sc_w_out[0].astype(BF16), small], by_columns=(0,))
    (g_w1[0],) = gather_shards("gather_up0", [ff_w1[0].astype(BF16)], by_columns=(0,))
    (g_w2[0],) = gather_shards("gather_down0", [ff_w2[0].astype(BF16)])
    g_dqkv, g_uq, g_uk, g_uv, g_o = gather_shards("gather_mixer1", [
        jnp.concatenate([mla_w_dq[0], w_dkv_pad], axis=1).astype(BF16),
        w_uq_pad.reshape(Q_LORA, 2 * HEAD_PAD).astype(BF16),
        mla_w_uk.reshape(KV_LORA // N_CHIPS, N_HEADS * QK_NOPE).astype(BF16),
        mla_w_uv.reshape(KV_LORA // N_CHIPS, N_HEADS * V_HEAD).astype(BF16), mla_w_o[0].astype(BF16)], by_columns=(1,))
    g_w1[1], g_w2[1] = gather_shards("gather_mlp1", mlp_w(1), by_columns=(0,))
    g_pw1, g_pw2, g_w1[2], g_w2[2] = gather_shards(
        "gather_layer2", [cf_w_pw1[0].astype(BF16), cf_w_pw2[0].astype(BF16)] + mlp_w(2), by_columns=(0, 2))
    g_in[1], g_out[1], g_w1[3], g_w2[3] = gather_shards(
        "gather_layer3", [sc_w_in[1].astype(BF16), sc_w_out[1].astype(BF16)] + mlp_w(3), by_columns=(0, 2))

    wd_t = Q_LORA + KV_LORA + 128
    w_in = [Stk("full", d, 3 * d, g_in[j]) for j in range(2)]
    w_out = [Stk("row", d, d, g_out[j]) for j in range(2)]
    w_dqkv = Stk("row", d, wd_t, g_dqkv)
    w_uq = Stk("full", Q_LORA, N_HEADS * HEAD_PAD, g_uq)
    w_uk = Stk("row", KV_LORA, N_HEADS * QK_NOPE, g_uk)
    w_uv = Stk("row", KV_LORA, N_HEADS * V_HEAD, g_uv)
    w_o = Stk("row", d, d, g_o)
    w_pw1 = Stk("full", d, 2 * d, g_pw1)
    w_pw2 = Stk("row", d, d, g_pw2)
    w_1 = [Stk("full", d, d_ff, g_w1[i]) for i in range(DEPTH)]
    w_2 = [Stk("row", d_ff, d, g_w2[i]) for i in range(DEPTH)]

    def wide(rows):
        return jnp.swapaxes(rows, 0, 1).reshape(rows.shape[1], d)

    conv_w = wide(g_small[:, 0:6]).reshape(2, SC_WIDTH, d)
    b_pw1 = g_small[:, 6:8].reshape(1, 2 * d)
    dw_w = wide(g_small[:, 8:39])
    dw_b, norm_g, norm_b, b_pw2 = (wide(g_small[:, 39 + k:40 + k]) for k in range(4))

    pos = jnp.arange(s, dtype=F32)
    inv_freq = ROPE_THETA ** (-jnp.arange(0, QK_ROPE, 2, dtype=F32) / QK_ROPE)
    ang = pos[:, None] * inv_freq[None, :]
    cos, sin, zero = jnp.cos(ang), jnp.sin(ang), jnp.zeros((s, 128 - QK_ROPE), F32)
    cf = jnp.concatenate([cos, cos, zero], axis=1)
    sf = jnp.concatenate([-sin, sin, zero], axis=1)

    def row(a, i):
        return a[i:i + 1]

    xs = x.reshape(s, d)
    cur = (xs, xs.astype(BF16))
    tape = []
    for i in range(DEPTH):
        mixer, j = i % 3, i // 3
        xf, xb = cur
        lg, lb = row(ln_mix_g, i), row(ln_mix_b, i)
        if mixer == 0:
            u = mm_plain_nn(f"sc{j}_in", xb, w_in[j], F32, tn=3 * dq4)
            gb = short_conv_gate(u, conv_w[j])
            y, yb, xh, rstd = mm_residual_ln(f"sc{j}_out_ln", gb, w_out[j], xf, lg, lb)
            sv = dict(xb=xb, u=u, gb=gb)
        elif mixer == 1:
            t = mm_plain_nn("mla_down", xb, w_dqkv, F32, tn=wd_t // 2)
            cq, ckv, kpe = mla_latents(t, mla_g_q, mla_g_kv, cf, sf)
            qh = mla_queries(cq, w_uq, cf, sf)
            kh = mla_keys(ckv, w_uk, kpe)
            vh = mm_plain_nn("mla_values", ckv, w_uv, BF16, tk=KV_LORA)
            oh = attention(qh, kh, vh)
            y, yb, xh, rstd = mm_residual_ln("mla_out_ln", oh, w_o, xf, lg, lb)
            sv = dict(xb=xb, t=t, cq=cq, ckv=ckv, qh=qh, kh=kh, vh=vh, oh=oh)
        else:
            u = mm_plain_nn("cf_pw1", xb, w_pw1, F32, bias=b_pw1)
            hc = conformer_glu_conv(u, dw_w, dw_b)
            sb = conformer_norm_swish(hc, norm_g, norm_b)
            y, yb, xh, rstd = mm_residual_ln("cf_pw2_ln", sb, w_pw2, xf, lg, lb, bias=b_pw2)
            sv = dict(xb=xb, u=u, hc=hc, sb=sb)
        sv.update(xh=xh, rstd=rstd, g=lg)
        cur, sv_mlp = _mlp_forward(i, y, yb, w_1[i], w_2[i], row(ln_ff_g, i), row(ln_ff_b, i))
        tape.append((sv, sv_mlp))

    g_ln = {n: [None] * DEPTH for n in ("ln_mix_g", "ln_mix_b", "ln_ff_g", "ln_ff_b")}
    last = tape[DEPTH - 1][1]
    dr, drb, g_ln["ln_ff_g"][DEPTH - 1], g_ln["ln_ff_b"][DEPTH - 1], _, loss_part = loss_ln_backward(
        cur[0], loss_target.reshape(s, d), last["xh"], last["rstd"], last["g"])

    grads = {}
    smalls = {}
    conv_grads = [None, None]
    core = lax.axis_index("c").astype(jnp.int32).reshape(1)
    chip = xq.astype(jnp.int32).reshape(1)
    pairs, landed = {}, {}
    ready, theirs = [], {}

    def hold(xs, others):
        live = [x for x in xs if x is not None]
        out = lax.optimization_barrier((*live, *others))
        rest = iter(out[:len(live)])
        return tuple(None if x is None else next(rest) for x in xs), list(out[len(live):])

    def reduce_after(x, new, early=False):
        out = lax.optimization_barrier((x, *new.values()))
        grads.update(zip(new, out[1:]))
        if early:
            theirs.update(zip(new, pair_exchange(f"pair_exchange_{len(theirs)}", list(out[1:]), True)))
        ready.extend(new)
        return out[0]

    def reduce_layer(i, x):
        late = [n for n in ready if n not in theirs]
        if late:
            theirs.update(zip(late, pair_exchange(f"pair_exchange_layer{i}", [grads[n] for n in late], False)))
        sums = [pair_sum(grads[n], theirs[n], core) for n in ready]
        pairs.update(zip(ready, sums))
        landed.update(zip(ready, chip_exchange(f"chip_exchange_layer{i}", sums)))
        exchanged.append(list(ready))
        ready.clear()
        return hold(x, sums)[0]

    groups = [["in_0", "in_1"], ["out_0", "out_1"], ["dqkv"], ["uq"], ["uk"], ["uv"], ["o"], ["pw1"], ["pw2"],
              [f"w1_{i}" for i in range(DEPTH)], [f"w2_{i}" for i in range(DEPTH)]]
    stacks = [None] * len(groups)
    exchanged = []

    def sum_layer(x, last=False):
        names = exchanged.pop(0)
        if last:
            x, held = hold(x, [landed[n] for n in names])
            landed.update(zip(names, held))
        new = []
        for n in names:
            k = next(k for k, members in enumerate(groups) if n in members)
            stacks[k] = chip_sum(pairs[n], landed[n], chip, stacks[k], groups[k].index(n), len(groups[k]))
            new.append(stacks[k])
        return x if last else hold(x, new)[0]

    for i in reversed(range(DEPTH)):
        mixer, j = i % 3, i // 3
        sv, sv_mlp = tape[i]
        dr, drb, g_ln["ln_mix_g"][i], g_ln["ln_mix_b"][i], dr_sum = _mlp_backward(
            i, dr, drb, sv_mlp, w_1[i], w_2[i], Stk("col", d, d_ff), Stk("row", d_ff, d),
            lambda x_, new: reduce_after(x_, new, early=i > 0), (sv["xh"], sv["rstd"], sv["g"]))
        if i == 0:
            dr, drb = reduce_layer("0_mlp", (dr, drb))

        def to_input(name, a, w, tk, a_spec_fn=None):
            if i == 0:
                spec = None if a_spec_fn is None else (s, a_spec_fn)
                return mm_plain_nt(name, a, w, F32, tn=1024, tk=tk, add=dr, add_scale=ALPHA, a_spec_fn=spec), None
            prev = tape[i - 1][1]
            out = mm_nt_ln_backward(name, a, w, dr, prev["xh"], prev["rstd"], prev["g"], tk=tk, a_spec_fn=a_spec_fn)
            g_ln["ln_ff_g"][i - 1], g_ln["ln_ff_b"][i - 1] = out[2], out[3]
            return out[0], out[1]

        parts_of = lambda tm, tk: pl.BlockSpec((None, tm, tk), lambda i_, j_, k_: (k_, i_, 0))
        if mixer == 0:
            dgate = mm_plain_nt(f"sc{j}_out_bwd", drb, w_out[j], F32)
            dw_out = mm_tn(f"sc{j}_dw_out", sv["gb"], drb, Stk("row", d, d), s, 512, 1024)
            du, conv_grads[j] = short_conv_gate_bwd(sv["u"], conv_w[j], dgate)
            nb = d // 256
            dw_in = mm_tn(
                f"sc{j}_dw_in", sv["xb"], du, Stk("col", d, 3 * d), s, 1024, 256,
                b_spec=pl.BlockSpec((None, s, 256), lambda i_, j_, k_: (j_ // nb, k_, j_ % nb)))
            du = reduce_after(du, {f"in_{j}": dw_in, f"out_{j}": dw_out})
            dr, drb = to_input(f"sc{j}_in_bwd", du, w_in[j], d, parts_of)
        elif mixer == 1:
            do = mm_plain_nt("mla_out_bwd", drb, w_o, BF16)
            g_o = mm_tn("mla_dw_o", sv["oh"], drb, Stk("row", d, d), s, 512, 1024)
            dqh, dkh, dvh = attention_bwd(sv["qh"], sv["kh"], sv["vh"], do)
            dql, dkn, dkpe = mla_unrope_grads(dqh, dkh, cf, sf)
            g_uq = mm_tn("mla_dw_uq", sv["cq"], dql, Stk("col", Q_LORA, N_HEADS * HEAD_PAD), s, Q_LORA, 512)
            dcq = mm_plain_nt("mla_uq_bwd", dql, w_uq, F32, tn=Q_LORA)
            g_uk = mm_tn("mla_dw_uk", sv["ckv"], dkn, Stk("row", KV_LORA, N_HEADS * QK_NOPE), s, KV_LORA, 1024)
            g_uv = mm_tn("mla_dw_uv", sv["ckv"], dvh, Stk("row", KV_LORA, N_HEADS * V_HEAD), s, KV_LORA, 1024)
            dckv = mm_plain_nt("mla_uk_bwd", dkn, w_uk, F32, tn=KV_LORA)
            dckv = mm_plain_nt("mla_uv_bwd", dvh, w_uv, F32, tn=KV_LORA, add=dckv)
            dt, smalls["g_q"], smalls["g_kv"] = mla_latents_bwd(sv["t"], mla_g_q, mla_g_kv, cf, sf, dcq, dckv, dkpe)
            g_dqkv = mm_tn("mla_dw_down", sv["xb"], dt, Stk("row", d, wd_t), s, 512, wd_t)
            dt = reduce_after(dt, {"dqkv": g_dqkv, "uq": g_uq, "uk": g_uk, "uv": g_uv, "o": g_o})
            dr, drb = to_input("mla_down_bwd", dt, w_dqkv, wd_t)
        else:
            dsw = mm_plain_nt("cf_pw2_bwd", drb, w_pw2, F32)
            g_pw2 = mm_tn("cf_dw_pw2", sv["sb"], drb, Stk("row", d, d), s, 512, 1024)
            smalls["b_pw2"] = dr_sum
            dhc, smalls["norm_g"], smalls["norm_b"] = conformer_norm_swish_bwd(sv["hc"], norm_g, norm_b, dsw)
            du, smalls["b_pw1"], smalls["dw_w"], smalls["dw_b"] = conformer_glu_conv_bwd(sv["u"], dw_w, dhc)
            nb = d // 512
            g_pw1 = mm_tn(
                "cf_dw_pw1", sv["xb"], du, Stk("col", d, 2 * d), s, 1024, 512,
                b_spec=pl.BlockSpec((None, s, 512), lambda i_, j_, k_: (j_ // nb, k_, j_ % nb)))
            du = reduce_after(du, {"pw1": g_pw1, "pw2": g_pw2})
            dr, drb = to_input("cf_pw1_bwd", du, w_pw1, d, parts_of)
        if i < DEPTH - 1:
            dr, drb = sum_layer((dr, drb))
        dr, drb = reduce_layer(i, (dr, drb))
    grad_x = sum_layer(sum_layer((dr, None), last=True), last=True)[0].reshape(1, s, d)

    mine = stacks
    other = (pair_share("pair_share_mixers", mine[:9]) + pair_share("pair_share_up", mine[9:10])
             + pair_share("pair_share_down", mine[10:]))

    def padded(get):
        dqkv = jnp.concatenate([get("mla_w_dq")[0], jnp.pad(get("mla_w_dkv")[0], ((0, 0), (0, 128 - QK_ROPE)))], axis=1)
        uq = jnp.pad(get("mla_w_uq")[0].reshape(Q_LORA, 2, QK_NOPE + QK_ROPE),
                     ((0, 0), (0, 0), (0, HEAD_PAD - QK_NOPE - QK_ROPE))).reshape(Q_LORA, 2 * HEAD_PAD)
        return [get("sc_w_in"), get("sc_w_out"), dqkv[None], uq[None],
                get("mla_w_uk").reshape(1, KV_LORA // N_CHIPS, d), get("mla_w_uv").reshape(1, KV_LORA // N_CHIPS, d),
                get("mla_w_o"), get("cf_w_pw1"), get("cf_w_pw2"), get("ff_w1"), get("ff_w2")]

    w_l, m_l, v_l = (padded(lambda n, p=p: given[p + n]) for p in ("", "m_", "v_"))
    res = [adamw_joined(w_l[k], m_l[k], v_l[k], mine[k], other[k], core) for k in range(len(groups))]

    def unpadded(k):
        r_in, r_out, r_dqkv, r_uq, r_uk, r_uv, r_o, r_pw1, r_pw2, r_w1, r_w2 = (r[k] for r in res)
        return {
            "sc_w_in": r_in, "sc_w_out": r_out, "mla_w_dq": r_dqkv[:, :, 0:Q_LORA],
            "mla_w_dkv": r_dqkv[:, :, Q_LORA:Q_LORA + KV_LORA + QK_ROPE],
            "mla_w_uq": r_uq.reshape(1, Q_LORA, 2, HEAD_PAD)[:, :, :, 0:QK_NOPE + QK_ROPE].reshape(mla_w_uq.shape),
            "mla_w_uk": r_uk.reshape(mla_w_uk.shape), "mla_w_uv": r_uv.reshape(mla_w_uv.shape),
            "mla_w_o": r_o, "cf_w_pw1": r_pw1, "cf_w_pw2": r_pw2, "ff_w1": r_w1, "ff_w2": r_w2}

    big_g, big_d, big_m, big_v = (unpadded(k) for k in range(4))

    pad_row = lambda a: jnp.pad(a, ((0, 0), (0, d - a.shape[1])))
    small_parts = ([g for n in ("ln_mix_g", "ln_mix_b", "ln_ff_g", "ln_ff_b") for g in g_ln[n]]
                   + [pad_row(smalls["g_q"]), pad_row(smalls["g_kv"]), conv_grads[0], conv_grads[1],
                      smalls["b_pw1"].reshape(2, d), smalls["dw_w"], smalls["dw_b"], smalls["norm_g"], smalls["norm_b"],
                      smalls["b_pw2"], loss_part])
    red = all_reduce_small(small_parts, 64)
    loss = red[61, 0]

    where = {
        "ln_mix_g": [((), 0, DEPTH, "all")], "ln_mix_b": [((), 4, DEPTH, "all")],
        "ln_ff_g": [((), 8, DEPTH, "all")], "ln_ff_b": [((), 12, DEPTH, "all")],
        "mla_g_q": [((), 16, 1, Q_LORA)], "mla_g_kv": [((), 17, 1, KV_LORA)],
        "sc_conv_w": [((0,), 18, SC_WIDTH, "chip"), ((1,), 21, SC_WIDTH, "chip")],
        "cf_b_pw1": [((), 24, 2, "chip")], "cf_dw_w": [((0,), 26, CONF_WIDTH, "chip")],
        "cf_dw_b": [((), 57, 1, "chip")], "cf_norm_g": [((), 58, 1, "chip")], "cf_norm_b": [((), 59, 1, "chip")],
        "cf_b_pw2": [((), 60, 1, "chip")]}
    vec = list(where)
    vec_res = vector_update(red, chip, [given[n] for n in vec], [given["m_" + n] for n in vec],
                            [given["v_" + n] for n in vec], [where[n] for n in vec])
    gw = dict(big_g)
    upd = {n: [big_d[n], big_m[n], big_v[n]] for n in big_g}
    for k, n in enumerate(vec):
        gw[n] = vec_res[0][k]
        upd[n] = [vec_res[1][k], vec_res[2][k], vec_res[3][k]]

    return (loss, grad_x, *[gw[n] for n in WEIGHTS], *[upd[n][0] for n in WEIGHTS],
            *[upd[n][1] for n in WEIGHTS], *[upd[n][2] for n in WEIGHTS])
```

```python
import jax
import jax.numpy as jnp
from jax import lax
from jax.experimental import pallas as pl
from jax.experimental.pallas import tpu as pltpu
from jax.experimental.pallas import tpu_sc as plsc

F32 = jnp.float32
BF16 = jnp.bfloat16
MESH = pl.DeviceIdType.MESH

DEPTH = 4
ALPHA = (2.0 * DEPTH) ** 0.25
LN_EPS = 1e-5
RMS_EPS = 1e-6
CHUNK_SHIFT = 6
N_HEADS = 8
QK_NOPE = 128
QK_ROPE = 64
V_HEAD = 128
HEAD_PAD = 256
Q_LORA = 384
KV_LORA = 256
ROPE_THETA = 10000.0
SC_WIDTH = 3
CONF_WIDTH = 31
CONV_PAD = 32
CONV_CHUNK = 64
N_CHIPS = 4
ATTN_SCALE = (QK_NOPE + QK_ROPE) ** -0.5

ADAM_LR = 0.001
ADAM_B1 = 0.9
ADAM_B2 = 0.999
ADAM_EPS = 1e-08
ADAM_WD = 0.01
ADAM_STEP = 10

VMEM_LIMIT = 56 * 2**20

NN = (((1,), (0,)), ((), ()))
NT = (((1,), (1,)), ((), ()))
TN = (((0,), (0,)), ((), ()))


def _params(sem=None):
    return pltpu.CompilerParams(dimension_semantics=sem, vmem_limit_bytes=VMEM_LIMIT)


class Stk:
    def __init__(self, kind, k, n, arr=None, layers=None, layer=None):
        self.kind, self.k, self.n, self.layers, self.layer = kind, k, n, layers, layer
        self.plain = (kind == "row" and layers is None) or kind == "full"
        self.kloc = k // N_CHIPS if kind == "row" else k
        self.nloc = n // N_CHIPS if kind == "col" else n
        if arr is not None and self.plain:
            arr = arr.reshape(k, n)
        self.arr = arr

    @property
    def shape(self):
        if self.plain:
            return (self.k, self.n)
        lead = (N_CHIPS,) if self.layers is None else (N_CHIPS, self.layers)
        return lead + (self.kloc, self.nloc)

    def spec(self, bk, bn, f):
        if self.plain:
            return pl.BlockSpec((bk, bn), f)
        assert self.kloc % bk == 0 and self.nloc % bn == 0, (self.kloc, bk, self.nloc, bn)
        pk, pn = self.kloc // bk, self.nloc // bn
        kind, layer = self.kind, self.layer

        def imap(*g):
            kb, nb = f(*g)
            if kind == "row":
                q, kb, nb = kb // pk, kb % pk, nb
            else:
                q, kb, nb = nb // pn, kb, nb % pn
            return (q, kb, nb) if layer is None else (q, layer, kb, nb)

        block = (None, bk, bn) if layer is None else (None, None, bk, bn)
        return pl.BlockSpec(block, imap)


def _mm(name, mode, a, b, grid, a_spec, b_spec, acc_shape, extras, extra_specs, out_shapes, out_specs, epi, a_fn=None,
        rows_in_order=False):
    nk = grid[2]
    ne = len(extras)

    def body(*refs):
        a_ref, b_ref = refs[0], refs[1]
        e_refs = refs[2:2 + ne]
        av = a_ref[...] if a_fn is None else a_fn(a_ref[...])
        part = lax.dot_general(av, b_ref[...], mode, preferred_element_type=F32)
        if nk == 1:
            epi(part, e_refs, refs[2 + ne:])
            return
        o_refs = refs[2 + ne:-1]
        acc = refs[-1]
        k = pl.program_id(2)

        @pl.when(k == 0)
        def _():
            acc[...] = part

        @pl.when(k > 0)
        def _():
            acc[...] += part

        @pl.when(k == nk - 1)
        def _():
            epi(acc[...], e_refs, o_refs)

    return pl.pallas_call(
        body, grid=grid, in_specs=[a_spec, b_spec, *extra_specs], out_specs=out_specs, out_shape=out_shapes,
        scratch_shapes=[pltpu.VMEM(acc_shape, F32)] if nk > 1 else [],
        compiler_params=_params(("arbitrary",) * 3 if rows_in_order else ("parallel", "parallel", "arbitrary")),
        name=name)(a, b, *extras)


def _tile(n, t):
    t = min(n, t)
    while n % t:
        t -= 8
    assert t > 0, (n, t)
    return t


def mm_nn(name, a, w, tm, tn, tk, epi, out_shapes, out_specs, extras=(), extra_specs=(), a_spec=None, a_fn=None):
    m = a.shape[0]
    tm, tn, tk = _tile(m, tm), _tile(w.n, tn), _tile(w.k, tk)
    grid = (m // tm, w.n // tn, w.k // tk)
    a_spec = a_spec or pl.BlockSpec((tm, tk), lambda i, j, k: (i, k))
    b_spec = w.spec(tk, tn, lambda i, j, k: (k, j))
    return _mm(name, NN, a, w.arr, grid, a_spec, b_spec, (tm, tn), extras, extra_specs, out_shapes, out_specs, epi, a_fn)


def mm_nt(name, a, w, m, tm, tn, tk, epi, out_shapes, out_specs, extras=(), extra_specs=(), a_spec=None,
          rows_in_order=False):
    tm, tn, tk = _tile(m, tm), _tile(w.k, tn), _tile(w.n, tk)
    grid = (m // tm, w.k // tn, w.n // tk)
    a_spec = a_spec or pl.BlockSpec((tm, tk), lambda i, j, k: (i, k))
    b_spec = w.spec(tn, tk, lambda i, j, k: (j, k))
    return _mm(name, NT, a, w.arr, grid, a_spec, b_spec, (tm, tn), extras, extra_specs, out_shapes, out_specs, epi,
               rows_in_order=rows_in_order)


def mm_tn(name, a, b, dw, s, tm=512, tn=512, tk=4096, a_spec=None, b_spec=None, a_fn=None):
    tm, tn, tk = _tile(dw.k, tm), _tile(dw.n, tn), _tile(s, tk)
    grid = (dw.k // tm, dw.n // tn, s // tk)
    a_spec = a_spec or pl.BlockSpec((tk, tm), lambda i, j, k: (k, i))
    b_spec = b_spec or pl.BlockSpec((tk, tn), lambda i, j, k: (k, j))

    def epi(acc, e, o):
        o[0][...] = acc.astype(BF16)

    out = _mm(name, TN, a, b, grid, a_spec, b_spec, (tm, tn), (), (), [jax.ShapeDtypeStruct(dw.shape, BF16)],
              [dw.spec(tm, tn, lambda i, j, k: (i, j))], epi, a_fn)[0]
    return out.reshape(N_CHIPS, dw.k // N_CHIPS, dw.n) if dw.plain else out


def _sds(shape, dtype):
    return jax.ShapeDtypeStruct(shape, dtype)


def _ij(tm, tn):
    return pl.BlockSpec((tm, tn), lambda i, j, k: (i, j))


def _i0(tm, c):
    return pl.BlockSpec((tm, c), lambda i, j, k: (i, 0))


def _0j(r, tn):
    return pl.BlockSpec((r, tn), lambda i, j, k: (0, j))


def _layer_norm_rows(r, g, b):
    mu = jnp.mean(r, axis=-1, keepdims=True)
    d = r - mu
    var = jnp.mean(d * d, axis=-1, keepdims=True)
    rstd = lax.rsqrt(var + LN_EPS)
    xh = d * rstd
    return xh * g + b, xh, rstd


def mm_residual_ln(name, a, w, x, g, b, bias=None, tm=512, tk=1024, a_fn=None):
    s, d = x.shape
    tm = _tile(s, tm)
    extras = [x, g, b] + ([bias] if bias is not None else [])
    especs = [_i0(tm, d), _0j(1, d), _0j(1, d)] + ([_0j(1, d)] if bias is not None else [])

    def epi(acc, e, o):
        r = ALPHA * e[0][...] + acc
        if bias is not None:
            r = r + e[3][...]
        y, xh, rstd = _layer_norm_rows(r, e[1][...], e[2][...])
        o[0][...] = y
        o[1][...] = y.astype(BF16)
        o[2][...] = xh
        o[3][...] = rstd

    return mm_nn(name, a, w, tm, d, tk, epi,
                 [_sds((s, d), F32), _sds((s, d), BF16), _sds((s, d), F32), _sds((s, 1), F32)],
                 [_i0(tm, d), _i0(tm, d), _i0(tm, d), _i0(tm, 1)], extras, especs, a_fn=a_fn)


def mm_plain_nn(name, a, w, out_dtype, tm=1024, tn=512, tk=1024, bias=None):
    m = a.shape[0]
    tm, tn = _tile(m, tm), _tile(w.n, tn)
    if w.kind == "col":
        tn = _tile(w.nloc, tn)

    def epi(acc, e, o):
        if bias is not None:
            acc = acc + e[0][...]
        o[0][...] = acc.astype(out_dtype)

    extras, especs = ([bias], [_0j(1, tn)]) if bias is not None else ((), ())
    return mm_nn(name, a, w, tm, tn, tk, epi, [_sds((m, w.n), out_dtype)], [_ij(tm, tn)], extras, especs)[0]


def mm_plain_nt(name, a, w, out_dtype, tm=1024, tn=512, tk=1024, add=None, add_scale=1.0, a_spec_fn=None):
    m = a.shape[0] if a_spec_fn is None else a_spec_fn[0]
    tm, tn = _tile(m, tm), _tile(w.k, tn)
    tk = _tile(w.n, tk)
    if w.kind == "col":
        tk = _tile(w.nloc, tk)
    if w.kind == "row" and not w.plain:
        tn = _tile(w.kloc, tn)

    def epi(acc, e, o):
        if add is not None:
            acc = acc + add_scale * e[0][...].astype(F32)
        o[0][...] = acc.astype(out_dtype)

    extras, especs = ([add], [_ij(tm, tn)]) if add is not None else ((), ())
    a_spec = None if a_spec_fn is None else a_spec_fn[1](tm, tk)
    return mm_nt(name, a, w, m, tm, tn, tk, epi, [_sds((m, w.k), out_dtype)], [_ij(tm, tn)], extras, especs,
                 a_spec=a_spec)[0]


def _rows(tm, c):
    return pl.BlockSpec((tm, c), lambda i: (i, 0))


def _fix(shape):
    nd = len(shape)
    return pl.BlockSpec(shape, lambda i: (0,) * nd)


def _accumulate(ref, val):
    @pl.when(pl.program_id(0) == 0)
    def _():
        ref[...] = jnp.zeros_like(ref)

    ref[...] += val


def _ln_backward_rows(dyv, xh, rstd, g, dr_ref, drb_ref, dg_ref, db_ref, ds_ref):
    dxh = dyv * g
    m1 = jnp.mean(dxh, axis=-1, keepdims=True)
    m2 = jnp.mean(dxh * xh, axis=-1, keepdims=True)
    dr = rstd * (dxh - m1 - xh * m2)
    dr_ref[...] = dr
    drb_ref[...] = dr.astype(BF16)
    _accumulate(dg_ref, jnp.sum(dyv * xh, axis=0, keepdims=True))
    _accumulate(db_ref, jnp.sum(dyv, axis=0, keepdims=True))
    _accumulate(ds_ref, jnp.sum(dr, axis=0, keepdims=True))


def mm_nt_ln_backward(name, a, w, add, xhat, rstd, g, tm=512, tk=1024, a_spec_fn=None):
    m, d = add.shape
    tm, tk = _tile(m, tm), _tile(w.n, tk)

    def epi(acc, e, o):
        _ln_backward_rows(acc + ALPHA * e[0][...], e[1][...], e[2][...], e[3][...], *o)

    vec = pl.BlockSpec((1, d), lambda i, j, k: (0, 0))
    a_spec = None if a_spec_fn is None else a_spec_fn(tm, tk)
    return mm_nt(name, a, w, m, tm, d, tk, epi,
                 [_sds((m, d), F32), _sds((m, d), BF16), _sds((1, d), F32), _sds((1, d), F32), _sds((1, d), F32)],
                 [_i0(tm, d), _i0(tm, d), vec, vec, vec], [add, xhat, rstd, g],
                 [_i0(tm, d), _i0(tm, d), _i0(tm, 1), vec], a_spec=a_spec, rows_in_order=True)


def loss_ln_backward(y, target, xhat, rstd, g, tm=512):
    s, d = y.shape
    tm = _tile(s, tm)

    def body(y_ref, t_ref, xh_ref, rstd_ref, g_ref, dr_ref, drb_ref, dg_ref, db_ref, ds_ref, loss_ref):
        e = y_ref[...] - t_ref[...]
        part = 0.5 * jnp.sum(jnp.mean(e * e, axis=-1, keepdims=True), axis=0, keepdims=True)
        _accumulate(loss_ref, jnp.broadcast_to(part, (1, d)))
        _ln_backward_rows(e * (1.0 / d), xh_ref[...], rstd_ref[...], g_ref[...], dr_ref, drb_ref, dg_ref, db_ref, ds_ref)

    return pl.pallas_call(
        body, grid=(s // tm,),
        in_specs=[_rows(tm, d), _rows(tm, d), _rows(tm, d), _rows(tm, 1), _fix((1, d))],
        out_specs=[_rows(tm, d), _rows(tm, d)] + [_fix((1, d))] * 4,
        out_shape=[_sds((s, d), F32), _sds((s, d), BF16)] + [_sds((1, d), F32)] * 4,
        compiler_params=_params(("arbitrary",)), name="loss_ln_backward")(y, target, xhat, rstd, g)


def _cols(s, tc, off=0):
    return pl.BlockSpec((s, tc), lambda i: (0, i + off))


def _shift_down(z, sft, rows):
    return jnp.where(rows >= sft, pltpu.roll(z, sft, 0), 0.0)


def _shift_up(z, sft, rows, s):
    return jnp.where(rows < s - sft, pltpu.roll(z, (s - sft) % s, 0), 0.0)


def short_conv_gate(u, conv_w, tc=256):
    s, d3 = u.shape
    d = d3 // 3
    nb = d // tc

    def body(b_ref, c_ref, h_ref, w_ref, o_ref):
        rows = lax.broadcasted_iota(jnp.int32, (s, tc), 0)
        z = c_ref[...] * h_ref[...]
        cz = jnp.zeros((s, tc), F32)
        for k in range(SC_WIDTH):
            sft = SC_WIDTH - 1 - k
            cz = cz + w_ref[pl.ds(k, 1), :] * (_shift_down(z, sft, rows) if sft else z)
        o_ref[...] = (b_ref[...] * cz).astype(BF16)

    return pl.pallas_call(
        body, grid=(nb,),
        in_specs=[_cols(s, tc), _cols(s, tc, nb), _cols(s, tc, 2 * nb), _cols(SC_WIDTH, tc)],
        out_specs=_cols(s, tc), out_shape=_sds((s, d), BF16),
        compiler_params=_params(("parallel",)), name="short_conv_gate")(u, u, u, conv_w)


def short_conv_gate_bwd(u, conv_w, dg, tc=256):
    s, d3 = u.shape
    d = d3 // 3
    nb = d // tc

    def body(b_ref, c_ref, h_ref, w_ref, dg_ref, du_ref, dw_ref):
        rows = lax.broadcasted_iota(jnp.int32, (s, tc), 0)
        c, h, dgv = c_ref[...], h_ref[...], dg_ref[...]
        z = c * h
        dcz = dgv * b_ref[...]
        cz = jnp.zeros((s, tc), F32)
        dz = jnp.zeros((s, tc), F32)
        for k in range(SC_WIDTH):
            sft = SC_WIDTH - 1 - k
            zs = _shift_down(z, sft, rows) if sft else z
            wk = w_ref[pl.ds(k, 1), :]
            cz = cz + wk * zs
            dz = dz + wk * (_shift_up(dcz, sft, rows, s) if sft else dcz)
            dw_ref[pl.ds(k, 1), :] = jnp.sum(dcz * zs, axis=0, keepdims=True)
        du_ref[0] = (dgv * cz).astype(BF16)
        du_ref[1] = (dz * h).astype(BF16)
        du_ref[2] = (dz * c).astype(BF16)

    return pl.pallas_call(
        body, grid=(nb,),
        in_specs=[_cols(s, tc), _cols(s, tc, nb), _cols(s, tc, 2 * nb), _cols(SC_WIDTH, tc), _cols(s, tc)],
        out_specs=[pl.BlockSpec((3, s, tc), lambda i: (0, 0, i)), _cols(SC_WIDTH, tc)],
        out_shape=[_sds((3, s, d), BF16), _sds((SC_WIDTH, d), F32)],
        compiler_params=_params(("parallel",)), name="short_conv_gate_bwd")(u, u, u, conv_w, dg)


def _store_shifted_down(ref, z, rows):
    s, tc = z.shape
    for b in range(8):
        ref[b, pl.ds(0, CONV_PAD), :] = jnp.zeros((CONV_PAD, tc), F32)
        ref[b, pl.ds(CONV_PAD, s), :] = z if b == 0 else _shift_down(z, b, rows)


def _store_shifted_up(ref, z, rows):
    s, tc = z.shape
    for b in range(8):
        ref[b, pl.ds(0, s), :] = z if b == 0 else _shift_up(z, b, rows, s)
        ref[b, pl.ds(s, CONV_PAD), :] = jnp.zeros((CONV_PAD, tc), F32)


def conformer_glu_conv(u, dw_w, dw_b, tc=128):
    s, d2 = u.shape
    d = d2 // 2
    nb = d // tc

    ch = min(CONV_CHUNK, s)

    def body(a_ref, g_ref, w_ref, b_ref, o_ref, down):
        rows = lax.broadcasted_iota(jnp.int32, (s, tc), 0)
        _store_shifted_down(down, a_ref[...] * jax.nn.sigmoid(g_ref[...]), rows)

        def chunk(ci, carry):
            r0 = pl.multiple_of(ci * ch, ch)
            acc = jnp.broadcast_to(b_ref[...], (ch, tc))
            for k in range(CONF_WIDTH):
                sft = CONF_WIDTH - 1 - k
                acc = acc + w_ref[pl.ds(k, 1), :] * down[sft % 8, pl.ds(CONV_PAD + r0 - (sft // 8) * 8, ch), :]
            o_ref[pl.ds(r0, ch), :] = acc
            return carry

        lax.fori_loop(0, s // ch, chunk, 0)

    return pl.pallas_call(
        body, grid=(nb,),
        in_specs=[_cols(s, tc), _cols(s, tc, nb), _cols(CONF_WIDTH, tc), _cols(1, tc)],
        out_specs=_cols(s, tc), out_shape=_sds((s, d), F32),
        scratch_shapes=[pltpu.VMEM((8, CONV_PAD + s, tc), F32)],
        compiler_params=_params(("parallel",)), name="conformer_glu_conv")(u, u, dw_w, dw_b)


def conformer_glu_conv_bwd(u, dw_w, dhc, tc=128):
    s, d2 = u.shape
    d = d2 // 2
    nb = d // tc
    ch = min(CONV_CHUNK, s)

    def body(a_ref, g_ref, w_ref, dhc_ref, du_ref, dbias_ref, dw_ref, db_ref, down, up, dw_acc, dh_buf):
        rows = lax.broadcasted_iota(jnp.int32, (s, tc), 0)
        a = a_ref[...]
        sg = jax.nn.sigmoid(g_ref[...])
        dhcv = dhc_ref[...]
        _store_shifted_down(down, a * sg, rows)
        _store_shifted_up(up, dhcv, rows)
        dw_acc[...] = jnp.zeros_like(dw_acc)

        def chunk(ci, carry):
            r0 = pl.multiple_of(ci * ch, ch)
            dc = dhc_ref[pl.ds(r0, ch), :]
            dh = jnp.zeros((ch, tc), F32)
            for k in range(CONF_WIDTH):
                sft = CONF_WIDTH - 1 - k
                a8, b = (sft // 8) * 8, sft % 8
                dh = dh + w_ref[pl.ds(k, 1), :] * up[b, pl.ds(r0 + a8, ch), :]
                prod = dc * down[b, pl.ds(CONV_PAD + r0 - a8, ch), :]
                dw_acc[k] += jnp.sum(prod.reshape(ch // 8, 8, tc), axis=0)
            dh_buf[pl.ds(r0, ch), :] = dh
            return carry

        lax.fori_loop(0, s // ch, chunk, 0)
        dh = dh_buf[...]
        da = dh * sg
        dgate = dh * a * sg * (1.0 - sg)
        du_ref[0] = da.astype(BF16)
        du_ref[1] = dgate.astype(BF16)
        dbias_ref[pl.ds(0, 1), :] = jnp.sum(da, axis=0, keepdims=True)
        dbias_ref[pl.ds(1, 1), :] = jnp.sum(dgate, axis=0, keepdims=True)
        db_ref[...] = jnp.sum(dhcv, axis=0, keepdims=True)
        for k in range(CONF_WIDTH):
            dw_ref[pl.ds(k, 1), :] = jnp.sum(dw_acc[k], axis=0, keepdims=True)

    return pl.pallas_call(
        body, grid=(nb,),
        in_specs=[_cols(s, tc), _cols(s, tc, nb), _cols(CONF_WIDTH, tc), _cols(s, tc)],
        out_specs=[pl.BlockSpec((2, s, tc), lambda i: (0, 0, i)), _cols(2, tc), _cols(CONF_WIDTH, tc), _cols(1, tc)],
        out_shape=[_sds((2, s, d), BF16), _sds((2, d), F32), _sds((CONF_WIDTH, d), F32), _sds((1, d), F32)],
        scratch_shapes=[pltpu.VMEM((8, CONV_PAD + s, tc), F32), pltpu.VMEM((8, CONV_PAD + s, tc), F32),
                        pltpu.VMEM((CONF_WIDTH + 1, 8, tc), F32), pltpu.VMEM((s, tc), F32)],
        compiler_params=_params(("parallel",)), name="conformer_glu_conv_bwd")(u, u, dw_w, dhc)


def conformer_norm_swish(hc, g, b, tm=512):
    s, d = hc.shape
    tm = _tile(s, tm)

    def body(h_ref, g_ref, b_ref, o_ref):
        n, _, _ = _layer_norm_rows(h_ref[...], g_ref[...], b_ref[...])
        o_ref[...] = (n * jax.nn.sigmoid(n)).astype(BF16)

    return pl.pallas_call(
        body, grid=(s // tm,), in_specs=[_rows(tm, d), _fix((1, d)), _fix((1, d))], out_specs=_rows(tm, d),
        out_shape=_sds((s, d), BF16), compiler_params=_params(("parallel",)), name="conformer_norm_swish")(hc, g, b)


def conformer_norm_swish_bwd(hc, g, b, ds, tm=512):
    s, d = hc.shape
    tm = _tile(s, tm)

    def body(h_ref, g_ref, b_ref, ds_ref, dh_ref, dg_ref, db_ref):
        n, nh, rstd = _layer_norm_rows(h_ref[...], g_ref[...], b_ref[...])
        sg = jax.nn.sigmoid(n)
        dn = ds_ref[...] * (sg * (1.0 + n * (1.0 - sg)))
        dnh = dn * g_ref[...]
        m1 = jnp.mean(dnh, axis=-1, keepdims=True)
        m2 = jnp.mean(dnh * nh, axis=-1, keepdims=True)
        dh_ref[...] = rstd * (dnh - m1 - nh * m2)
        _accumulate(dg_ref, jnp.sum(dn * nh, axis=0, keepdims=True))
        _accumulate(db_ref, jnp.sum(dn, axis=0, keepdims=True))

    return pl.pallas_call(
        body, grid=(s // tm,), in_specs=[_rows(tm, d), _fix((1, d)), _fix((1, d)), _rows(tm, d)],
        out_specs=[_rows(tm, d), _fix((1, d)), _fix((1, d))],
        out_shape=[_sds((s, d), F32), _sds((1, d), F32), _sds((1, d), F32)],
        compiler_params=_params(("arbitrary",)), name="conformer_norm_swish_bwd")(hc, g, b, ds)


def _swap_halves(x):
    lane = lax.broadcasted_iota(jnp.int32, x.shape, 1)
    return jnp.where(lane < QK_ROPE // 2, pltpu.roll(x, 128 - QK_ROPE // 2, 1), pltpu.roll(x, QK_ROPE // 2, 1))


def _rope(x, cf, sf):
    return x * cf + _swap_halves(x) * sf


def _unrope(dx, cf, sf):
    return dx * cf - _swap_halves(dx) * sf


def _rms_rows(x, g):
    r = lax.rsqrt(jnp.mean(x * x, axis=-1, keepdims=True) + RMS_EPS)
    return x * r, r


def mla_latents(t, g_q, g_kv, cf, sf, tm=512):
    s = t.shape[0]
    tm = _tile(s, tm)

    def body(t_ref, gq_ref, gkv_ref, cf_ref, sf_ref, cq_ref, ckv_ref, kpe_ref):
        xq, _ = _rms_rows(t_ref[:, 0:Q_LORA], gq_ref[...])
        cq_ref[...] = (xq * gq_ref[...]).astype(BF16)
        xkv, _ = _rms_rows(t_ref[:, Q_LORA:Q_LORA + KV_LORA], gkv_ref[...])
        ckv_ref[...] = (xkv * gkv_ref[...]).astype(BF16)
        kpe_ref[...] = _rope(t_ref[:, Q_LORA + KV_LORA:], cf_ref[...], sf_ref[...]).astype(BF16)

    w = Q_LORA + KV_LORA + 128
    return pl.pallas_call(
        body, grid=(s // tm,),
        in_specs=[_rows(tm, w), _fix((1, Q_LORA)), _fix((1, KV_LORA)), _rows(tm, 128), _rows(tm, 128)],
        out_specs=[_rows(tm, Q_LORA), _rows(tm, KV_LORA), _rows(tm, 128)],
        out_shape=[_sds((s, Q_LORA), BF16), _sds((s, KV_LORA), BF16), _sds((s, 128), BF16)],
        compiler_params=_params(("parallel",)), name="mla_latents")(t, g_q, g_kv, cf, sf)


def mla_latents_bwd(t, g_q, g_kv, cf, sf, dcq, dckv, dkpe, tm=512):
    s = t.shape[0]
    tm = _tile(s, tm)
    w = Q_LORA + KV_LORA + 128

    def rms_bwd(x, g, dy):
        xh, r = _rms_rows(x, g)
        dxh = dy * g
        return r * (dxh - xh * jnp.mean(dxh * xh, axis=-1, keepdims=True)), jnp.sum(dy * xh, axis=0, keepdims=True)

    def body(t_ref, gq_ref, gkv_ref, cf_ref, sf_ref, dcq_ref, dckv_ref, dkpe_ref, dt_ref, dgq_ref, dgkv_ref):
        dxq, dgq = rms_bwd(t_ref[:, 0:Q_LORA], gq_ref[...], dcq_ref[...])
        dxkv, dgkv = rms_bwd(t_ref[:, Q_LORA:Q_LORA + KV_LORA], gkv_ref[...], dckv_ref[...])
        dt_ref[:, 0:Q_LORA] = dxq.astype(BF16)
        dt_ref[:, Q_LORA:Q_LORA + KV_LORA] = dxkv.astype(BF16)
        dt_ref[:, Q_LORA + KV_LORA:] = _unrope(dkpe_ref[...], cf_ref[...], sf_ref[...]).astype(BF16)
        _accumulate(dgq_ref, dgq)
        _accumulate(dgkv_ref, dgkv)

    return pl.pallas_call(
        body, grid=(s // tm,),
        in_specs=[_rows(tm, w), _fix((1, Q_LORA)), _fix((1, KV_LORA)), _rows(tm, 128), _rows(tm, 128),
                  _rows(tm, Q_LORA), _rows(tm, KV_LORA), _rows(tm, 128)],
        out_specs=[_rows(tm, w), _fix((1, Q_LORA)), _fix((1, KV_LORA))],
        out_shape=[_sds((s, w), BF16), _sds((1, Q_LORA), F32), _sds((1, KV_LORA), F32)],
        compiler_params=_params(("arbitrary",)), name="mla_latents_bwd")(t, g_q, g_kv, cf, sf, dcq, dckv, dkpe)


def mla_queries(cq, w_uq, cf, sf, tm=2048):
    s = cq.shape[0]
    tm = _tile(s, tm)

    def epi(acc, e, o):
        o[0][:, 0:QK_NOPE] = acc[:, 0:QK_NOPE].astype(BF16)
        o[0][:, QK_NOPE:] = _rope(acc[:, QK_NOPE:], e[0][...], e[1][...]).astype(BF16)

    return mm_nn("mla_queries", cq, w_uq, tm, HEAD_PAD, Q_LORA, epi, [_sds((s, N_HEADS * HEAD_PAD), BF16)],
                 [_ij(tm, HEAD_PAD)], [cf, sf], [_i0(tm, 128), _i0(tm, 128)])[0]


def mla_keys(ckv, w_uk, kpe, tm=2048):
    s = ckv.shape[0]
    tm = _tile(s, tm)

    def epi(acc, e, o):
        o[0][:, 0:QK_NOPE] = acc.astype(BF16)
        o[0][:, QK_NOPE:] = e[0][...]

    return mm_nn("mla_keys", ckv, w_uk, tm, QK_NOPE, KV_LORA, epi, [_sds((s, N_HEADS * HEAD_PAD), BF16)],
                 [_ij(tm, HEAD_PAD)], [kpe], [_i0(tm, 128)])[0]


def _masked_scores(q, k, tq, kv):
    sc = lax.dot_general(q, k, NT, preferred_element_type=F32) * ATTN_SCALE
    row = lax.broadcasted_iota(jnp.int32, (tq, tq), 0)
    col = lax.broadcasted_iota(jnp.int32, (tq, tq), 1)
    ok = lax.shift_right_logical(col, CHUNK_SHIFT) <= lax.shift_right_logical(row, CHUNK_SHIFT)
    own = jnp.where(ok, sc[:, kv - tq:], -1e30)
    return own if kv == tq else jnp.concatenate([sc[:, :kv - tq], own], axis=1)


def attention(q, k, v, tq=512):
    s = q.shape[0]
    tq = _tile(s, tq)
    nq = s // tq

    def body(q_ref, k_ref, v_ref, o_ref):
        for qi in range(nq):
            kv = (qi + 1) * tq
            sc = _masked_scores(q_ref[pl.ds(qi * tq, tq), :], k_ref[pl.ds(0, kv), :], tq, kv)
            p = jnp.exp(sc - jnp.max(sc, axis=-1, keepdims=True))
            o = lax.dot_general(p.astype(BF16), v_ref[pl.ds(0, kv), :], NN, preferred_element_type=F32)
            o_ref[pl.ds(qi * tq, tq), :] = (o / jnp.sum(p, axis=-1, keepdims=True)).astype(BF16)

    hq = pl.BlockSpec((s, HEAD_PAD), lambda h: (0, h))
    hv = pl.BlockSpec((s, V_HEAD), lambda h: (0, h))
    return pl.pallas_call(
        body, grid=(N_HEADS,), in_specs=[hq, hq, hv], out_specs=hv, out_shape=_sds((s, N_HEADS * V_HEAD), BF16),
        compiler_params=_params(("parallel",)), name="attention")(q, k, v)


def attention_bwd(q, k, v, do, tq=512):
    s = q.shape[0]
    tq = _tile(s, tq)
    nq = s // tq

    def body(q_ref, k_ref, v_ref, do_ref, dq_ref, dk_ref, dv_ref, dk_acc, dv_acc):
        dk_acc[...] = jnp.zeros_like(dk_acc)
        dv_acc[...] = jnp.zeros_like(dv_acc)
        for qi in range(nq):
            kv = (qi + 1) * tq
            qt = q_ref[pl.ds(qi * tq, tq), :]
            kt = k_ref[pl.ds(0, kv), :]
            dot = do_ref[pl.ds(qi * tq, tq), :]
            sc = _masked_scores(qt, kt, tq, kv)
            p = jnp.exp(sc - jnp.max(sc, axis=-1, keepdims=True))
            p = p / jnp.sum(p, axis=-1, keepdims=True)
            dp = lax.dot_general(dot, v_ref[pl.ds(0, kv), :], NT, preferred_element_type=F32)
            delta = jnp.sum(p * dp, axis=-1, keepdims=True)
            ds = (p * (dp - delta) * ATTN_SCALE).astype(BF16)
            dq_ref[pl.ds(qi * tq, tq), :] = lax.dot_general(ds, kt, NN, preferred_element_type=F32).astype(BF16)
            dk_acc[pl.ds(0, kv), :] += lax.dot_general(ds, qt, TN, preferred_element_type=F32)
            dv_acc[pl.ds(0, kv), :] += lax.dot_general(p.astype(BF16), dot, TN, preferred_element_type=F32)
        dk_ref[...] = dk_acc[...].astype(BF16)
        dv_ref[...] = dv_acc[...].astype(BF16)

    hq = pl.BlockSpec((s, HEAD_PAD), lambda h: (0, h))
    hv = pl.BlockSpec((s, V_HEAD), lambda h: (0, h))
    return pl.pallas_call(
        body, grid=(N_HEADS,), in_specs=[hq, hq, hv, hv], out_specs=[hq, hq, hv],
        out_shape=[_sds((s, N_HEADS * HEAD_PAD), BF16), _sds((s, N_HEADS * HEAD_PAD), BF16),
                   _sds((s, N_HEADS * V_HEAD), BF16)],
        scratch_shapes=[pltpu.VMEM((s, HEAD_PAD), F32), pltpu.VMEM((s, V_HEAD), F32)],
        compiler_params=_params(("parallel",)), name="attention_bwd")(q, k, v, do)


def mla_unrope_grads(dq, dk, cf, sf, tm=512):
    s = dq.shape[0]
    tm = _tile(s, tm)

    def body(dq_ref, dk_ref, cf_ref, sf_ref, dql_ref, dkn_ref, dkpe_ref):
        cfv, sfv = cf_ref[...], sf_ref[...]
        dkpe = jnp.zeros((tm, 128), F32)
        for h in range(N_HEADS):
            lo = h * HEAD_PAD
            dql_ref[:, lo:lo + QK_NOPE] = dq_ref[:, lo:lo + QK_NOPE]
            dql_ref[:, lo + QK_NOPE:lo + HEAD_PAD] = _unrope(
                dq_ref[:, lo + QK_NOPE:lo + HEAD_PAD].astype(F32), cfv, sfv).astype(BF16)
            dkn_ref[:, h * QK_NOPE:(h + 1) * QK_NOPE] = dk_ref[:, lo:lo + QK_NOPE]
            dkpe = dkpe + dk_ref[:, lo + QK_NOPE:lo + HEAD_PAD].astype(F32)
        dkpe_ref[...] = dkpe

    wq = N_HEADS * HEAD_PAD
    return pl.pallas_call(
        body, grid=(s // tm,), in_specs=[_rows(tm, wq), _rows(tm, wq), _rows(tm, 128), _rows(tm, 128)],
        out_specs=[_rows(tm, wq), _rows(tm, N_HEADS * QK_NOPE), _rows(tm, 128)],
        out_shape=[_sds((s, wq), BF16), _sds((s, N_HEADS * QK_NOPE), BF16), _sds((s, 128), F32)],
        compiler_params=_params(("parallel",)), name="mla_unrope_grads")(dq, dk, cf, sf)


ANY = pl.BlockSpec(memory_space=pl.ANY)
GATHER_ID = 1
CHIP_EXCHANGE_ID = 2
PAIR_ID = 3
ALL_ID = 4


def _nbytes(a):
    return a.size * a.dtype.itemsize


def _copy_cost(operand_bytes, sent_fraction):
    sent = int(operand_bytes * sent_fraction)
    return pl.CostEstimate(flops=0, transcendentals=0, bytes_accessed=2 * sent, remote_bytes_transferred=sent)


def _handshake(peers):
    barrier = pltpu.get_barrier_semaphore()
    for peer in peers:
        pl.semaphore_signal(barrier, inc=1, device_id=peer, device_id_type=MESH)
    pl.semaphore_wait(barrier, len(peers))


def _place():
    x, y, c = lax.axis_index("x"), lax.axis_index("y"), lax.axis_index("c")
    chips = [(1 - x, y), (x, 1 - y), (1 - x, 1 - y)]
    return x, y, c, chips


def _half(ref, hc, axis=0):
    n = ref.shape[axis] // 2
    idx = (slice(None),) * axis + (pl.ds(hc * n, n),)
    return ref.at[idx]


def gather_shards(name, tensors, by_columns=()):
    nt = len(tensors)

    def body(*refs):
        a, g = refs[:nt], refs[nt:2 * nt]
        send, recv = refs[2 * nt:]
        x, y, c, _ = _place()
        q = 2 * x + y
        sib, xn, yn = (x, y, 1 - c), (1 - x, y, c), (x, 1 - y, c)
        q_xn, q_yn, q_diag = 2 * (1 - x) + y, 2 * x + 1 - y, 2 * (1 - x) + 1 - y
        _handshake([sib, xn, yn])

        def whole(t, p):
            if t in by_columns:
                n = a[t].shape[1]
                return g[t].at[:, pl.ds(p * n, n)]
            return g[t].at[p]

        def part(t, p, hc, quarter=None):
            rows = a[t].shape[0]
            if quarter is None:
                return whole(t, p).at[pl.ds(hc * (rows // 2), rows // 2)]
            return whole(t, p).at[pl.ds(hc * (rows // 2) + quarter * (rows // 4), rows // 4)]

        def rc(t, k, src, dst, to):
            return pltpu.make_async_remote_copy(src_ref=src, dst_ref=dst, send_sem=send.at[t, k], recv_sem=recv.at[t, k],
                                                device_id=to, device_id_type=MESH)

        sent = []

        def go(cp):
            cp.start()
            sent.append(cp)

        def landed(t, k, piece, frm):
            rc(t, k, piece, piece, frm).wait_recv()
            return piece

        for t in range(nt):
            go(rc(t, 8, a[t], whole(t, q), sib))
            mine = _half(a[t], c)
            go(rc(t, 0, mine, part(t, q, c), xn))
            go(rc(t, 1, mine, part(t, q, c), yn))
        for t in range(nt):
            from_y = landed(t, 1, part(t, q_yn, c), yn)
            go(rc(t, 2, part(t, q_yn, c, 0), part(t, q_yn, c, 0), xn))
            go(rc(t, 5, from_y, from_y, sib))
            from_x = landed(t, 0, part(t, q_xn, c), xn)
            go(rc(t, 3, part(t, q_xn, c, 1), part(t, q_xn, c, 1), yn))
            go(rc(t, 4, from_x, from_x, sib))
        for t in range(nt):
            for k, frm in ((2, xn), (3, yn)):
                piece = landed(t, k, part(t, q_diag, c, k - 2), frm)
                go(rc(t, 4 + k, piece, piece, sib))
        for t in range(nt):
            landed(t, 4, part(t, q_xn, 1 - c), sib)
            landed(t, 5, part(t, q_yn, 1 - c), sib)
            landed(t, 6, part(t, q_diag, 1 - c, 0), sib)
            landed(t, 7, part(t, q_diag, 1 - c, 1), sib)
            landed(t, 8, whole(t, q), sib)
        for cp in sent:
            cp.wait_send()

    return pl.kernel(
        body, name=name,
        out_type=[_sds((a.shape[0], N_CHIPS * a.shape[1]) if t in by_columns else (N_CHIPS,) + a.shape, a.dtype)
                  for t, a in enumerate(tensors)],
        mesh=plsc.ScalarSubcoreMesh(axis_name="sequencer", num_cores=1),
        scratch_types=[pltpu.SemaphoreType.DMA((nt, 9)), pltpu.SemaphoreType.DMA((nt, 9))],
        cost_estimate=_copy_cost(sum(_nbytes(a) for a in tensors), 4),
        compiler_params=pltpu.CompilerParams(collective_id=GATHER_ID))(*tensors)


def pair_exchange(name, grads, on_sequencer):
    nt = len(grads)

    def body(*refs):
        g, theirs = refs[:nt], refs[nt:2 * nt]
        send, recv = refs[2 * nt:]
        x, y, c, _ = _place()
        if on_sequencer:
            _handshake([(x, y, 1 - c)])
        cps = []
        for t in range(nt):
            cp = pltpu.make_async_remote_copy(src_ref=_half(g[t], 1 - c, 1), dst_ref=theirs[t], send_sem=send.at[t],
                                              recv_sem=recv.at[t], device_id=(x, y, 1 - c), device_id_type=MESH)
            cp.start()
            cps.append(cp)
        for cp in cps:
            cp.wait()

    if not on_sequencer:
        return pl.pallas_call(
            body, in_specs=[ANY] * nt, out_specs=[ANY] * nt,
            out_shape=[_sds((N_CHIPS, a.shape[1] // 2, a.shape[2]), a.dtype) for a in grads],
            scratch_shapes=[pltpu.SemaphoreType.DMA((nt,)), pltpu.SemaphoreType.DMA((nt,))],
            name=name)(*grads)
    return pl.kernel(
        body, name=name, out_type=[_sds((N_CHIPS, a.shape[1] // 2, a.shape[2]), a.dtype) for a in grads],
        mesh=plsc.ScalarSubcoreMesh(axis_name="sequencer", num_cores=1),
        scratch_types=[pltpu.SemaphoreType.DMA((nt,)), pltpu.SemaphoreType.DMA((nt,))],
        cost_estimate=_copy_cost(sum(_nbytes(a) for a in grads), 0.5),
        compiler_params=pltpu.CompilerParams(collective_id=PAIR_ID))(*grads)


def chip_exchange(name, parts):
    nt = len(parts)

    def body(*refs):
        a, r = refs[:nt], refs[nt:2 * nt]
        send, recv = refs[2 * nt:]
        x, y, c, chips = _place()
        _handshake([(*chip, c) for chip in chips])
        cps = []
        for t in range(nt):
            for j, chip in enumerate(chips):
                cp = pltpu.make_async_remote_copy(
                    src_ref=a[t].at[2 * chip[0] + chip[1]], dst_ref=r[t].at[j], send_sem=send.at[t, j],
                    recv_sem=recv.at[t, j], device_id=(*chip, c), device_id_type=MESH)
                cp.start()
                cps.append(cp)
        for cp in cps:
            cp.wait()

    return pl.kernel(
        body, name=name, out_type=[_sds((N_CHIPS - 1,) + a.shape[1:], a.dtype) for a in parts],
        mesh=plsc.ScalarSubcoreMesh(axis_name="sequencer", num_cores=1),
        scratch_types=[pltpu.SemaphoreType.DMA((nt, 3)), pltpu.SemaphoreType.DMA((nt, 3))],
        cost_estimate=_copy_cost(sum(_nbytes(a) for a in parts), 0.75),
        compiler_params=pltpu.CompilerParams(collective_id=CHIP_EXCHANGE_ID))(*parts)


def pair_share(name, halves):
    nt = len(halves)

    def body(*refs):
        h, other = refs[:nt], refs[nt:2 * nt]
        send, recv = refs[2 * nt:]
        x, y, c, _ = _place()
        _handshake([(x, y, 1 - c)])
        cps = []
        for t in range(nt):
            cp = pltpu.make_async_remote_copy(src_ref=h[t], dst_ref=other[t], send_sem=send.at[t], recv_sem=recv.at[t],
                                              device_id=(x, y, 1 - c), device_id_type=MESH)
            cp.start()
            cps.append(cp)
        for cp in cps:
            cp.wait()

    return pl.kernel(
        body, name=name, out_type=[_sds(a.shape, a.dtype) for a in halves],
        mesh=plsc.ScalarSubcoreMesh(axis_name="sequencer", num_cores=1),
        scratch_types=[pltpu.SemaphoreType.DMA((nt,)), pltpu.SemaphoreType.DMA((nt,))],
        cost_estimate=_copy_cost(sum(_nbytes(a) for a in halves), 1),
        compiler_params=pltpu.CompilerParams(collective_id=PAIR_ID))(*halves)


def pack_rows(name, parts, rows):
    cdim = parts[0].shape[1]
    n = len(parts)
    vm = pl.BlockSpec(memory_space=pltpu.VMEM)

    def pack(*refs):
        p, o_ref = refs[:n], refs[n]
        at = 0
        for ref in p:
            o_ref[pl.ds(at, ref.shape[0]), :] = ref[...]
            at += ref.shape[0]
        o_ref[pl.ds(at, rows - at), :] = jnp.zeros((rows - at, cdim), F32)

    return pl.pallas_call(pack, in_specs=[vm] * n, out_specs=vm, out_shape=_sds((rows, cdim), F32), name=name)(*parts)


def all_reduce_small(parts, rows):
    cdim = parts[0].shape[1]
    vm = pl.BlockSpec(memory_space=pltpu.VMEM)
    mine = pack_rows("small_pack", parts, rows)

    def exchange(mine_ref, buf, send, recv, lsem):
        x, y, c, _ = _place()
        me = 4 * x + 2 * y + c
        peers = [(x ^ (k >> 2), y ^ ((k >> 1) & 1), c ^ (k & 1)) for k in range(1, 8)]
        _handshake(peers)
        own = pltpu.make_async_copy(mine_ref, buf.at[me], lsem)
        own.start()
        cps = []
        for k, to in enumerate(peers):
            cp = pltpu.make_async_remote_copy(src_ref=mine_ref, dst_ref=buf.at[me], send_sem=send.at[k], recv_sem=recv.at[k],
                                              device_id=to, device_id_type=MESH)
            cp.start()
            cps.append(cp)
        for k, (px, py, pc) in enumerate(peers):
            pltpu.make_async_remote_copy(src_ref=mine_ref, dst_ref=buf.at[4 * px + 2 * py + pc], send_sem=send.at[k],
                                         recv_sem=recv.at[k], device_id=(x, y, c), device_id_type=MESH).wait_recv()
        for cp in cps:
            cp.wait_send()
        own.wait()

    landed = pl.kernel(
        exchange, name="small_exchange", out_type=_sds((8, rows, cdim), F32),
        mesh=plsc.ScalarSubcoreMesh(axis_name="sequencer", num_cores=1),
        scratch_types=[pltpu.SemaphoreType.DMA((7,)), pltpu.SemaphoreType.DMA((7,)), pltpu.SemaphoreType.DMA],
        cost_estimate=_copy_cost(rows * cdim * 4, 7),
        compiler_params=pltpu.CompilerParams(collective_id=ALL_ID))(mine)

    def total(buf, o_ref):
        acc = buf[0]
        for d in range(1, 8):
            acc = acc + buf[d]
        o_ref[...] = acc

    return pl.pallas_call(total, in_specs=[vm], out_specs=vm, out_shape=_sds((rows, cdim), F32), name="small_sum")(landed)


def pair_sum(g, theirs, core, tm=256):
    _, r, c = g.shape
    tm = _tile(r // 2, tm)
    nh = r // 2 // tm

    def body(core_ref, a_ref, b_ref, o_ref):
        o_ref[...] = (a_ref[...].astype(F32) + b_ref[...].astype(F32)).astype(BF16)

    blk = (N_CHIPS, tm, c)
    return pl.pallas_call(
        body, grid_spec=pltpu.PrefetchScalarGridSpec(
            num_scalar_prefetch=1, grid=(nh,),
            in_specs=[pl.BlockSpec(blk, lambda i, cr: (0, cr[0] * nh + i, 0)), pl.BlockSpec(blk, lambda i, cr: (0, i, 0))],
            out_specs=pl.BlockSpec(blk, lambda i, cr: (0, i, 0))),
        out_shape=_sds(theirs.shape, BF16), compiler_params=_params(("parallel",)), name="pair_sum")(core, g, theirs)


def chip_sum(own, landed, chip, stack, layer, layers, tm=256):
    _, r, c = own.shape
    tm = _tile(r, tm)

    def body(chip_ref, own_ref, l_ref, *rest):
        acc = own_ref[...].astype(F32)
        for j in range(N_CHIPS - 1):
            acc = acc + l_ref[j].astype(F32)
        rest[-1][...] = acc

    in_specs = [pl.BlockSpec((None, tm, c), lambda i, qr: (qr[0], i, 0)),
                pl.BlockSpec((N_CHIPS - 1, tm, c), lambda i, qr: (0, i, 0))]
    args = [chip, own, landed]
    if stack is not None:
        in_specs.append(ANY)
        args.append(stack)
    return pl.pallas_call(
        body, grid_spec=pltpu.PrefetchScalarGridSpec(
            num_scalar_prefetch=1, grid=(r // tm,), in_specs=in_specs,
            out_specs=pl.BlockSpec((None, tm, c), lambda i, qr: (layer, i, 0))),
        out_shape=_sds((layers, r, c), F32), input_output_aliases={3: 0} if stack is not None else {},
        compiler_params=_params(("parallel",)), name="chip_sum")(*args)


def _adamw_math(w, g, m, v):
    bc1 = 1.0 - ADAM_B1 ** ADAM_STEP
    bc2 = 1.0 - ADAM_B2 ** ADAM_STEP
    nm = ADAM_B1 * m + (1.0 - ADAM_B1) * g
    nv = ADAM_B2 * v + (1.0 - ADAM_B2) * (g * g)
    return -ADAM_LR * ((nm / bc1) / (jnp.sqrt(nv / bc2) + ADAM_EPS) + ADAM_WD * w), nm, nv


def vector_update(red, chip, ws, ms, vs, where):
    n = len(ws)
    dd = red.shape[1]

    def body(chip_ref, red_ref, *refs):
        w_r, m_r, v_r = refs[0:n], refs[n:2 * n], refs[2 * n:3 * n]
        g_o, d_o, m_o, v_o = (refs[(3 + k) * n:(4 + k) * n] for k in range(4))
        q = chip_ref[0]

        def chip_block(val, width):
            out = val[:, 0:width]
            for p in range(1, val.shape[1] // width):
                out = jnp.where(q == p, val[:, p * width:(p + 1) * width], out)
            return out

        for k in range(n):
            for idx, r0, nr, cols in where[k]:
                width = w_r[k].shape[-1]
                if cols == "chip" and width * N_CHIPS != dd:
                    g = chip_block(jnp.concatenate([red_ref[pl.ds(r0 + j, 1), :] for j in range(nr)], axis=1), width)
                else:
                    g = red_ref[pl.ds(r0, nr), :]
                    g = chip_block(g, width) if cols == "chip" else g if cols == "all" else g[:, 0:cols]
                delta, nm, nv = _adamw_math(w_r[k][idx], g, m_r[k][idx], v_r[k][idx])
                g_o[k][idx] = g
                d_o[k][idx] = delta
                m_o[k][idx] = nm
                v_o[k][idx] = nv

    vm = pl.BlockSpec(memory_space=pltpu.VMEM)
    outs = pl.pallas_call(
        body, in_specs=[pl.BlockSpec(memory_space=pltpu.SMEM), vm] + [vm] * (3 * n), out_specs=[vm] * (4 * n),
        out_shape=[_sds(w.shape, F32) for w in ws] * 4, name="vector_update")(chip, red, *ws, *ms, *vs)
    return [outs[k * n:(k + 1) * n] for k in range(4)]


def adamw_joined(w, m, v, g_mine, g_theirs, core, tm=512):
    nl, r, c = w.shape
    tm = _tile(r // 2, tm)
    nh = r // 2 // tm

    def body(core_ref, w_ref, m_ref, v_ref, gm_ref, gt_ref, g_ref, d_ref, nm_ref, nv_ref):
        mine = (pl.program_id(1) // nh) == core_ref[0]
        gv = jnp.where(mine, gm_ref[...], gt_ref[...])
        g_ref[...] = gv
        d_ref[...], nm_ref[...], nv_ref[...] = _adamw_math(w_ref[...], gv, m_ref[...], v_ref[...])

    full = pl.BlockSpec((None, tm, c), lambda l, i, cr: (l, i, 0))
    mine = pl.BlockSpec((None, tm, c), lambda l, i, cr: (l, jnp.where(i // nh == cr[0], i % nh, 0), 0))
    theirs = pl.BlockSpec((None, tm, c), lambda l, i, cr: (l, jnp.where(i // nh == cr[0], 0, i % nh), 0))
    return pl.pallas_call(
        body, grid_spec=pltpu.PrefetchScalarGridSpec(
            num_scalar_prefetch=1, grid=(nl, r // tm), in_specs=[full, full, full, mine, theirs], out_specs=[full] * 4),
        out_shape=[_sds((nl, r, c), F32)] * 4, compiler_params=_params(("parallel", "parallel")),
        name="adamw_joined")(core, w, m, v, g_mine, g_theirs)


WEIGHTS = ['sc_w_in', 'sc_conv_w', 'sc_w_out', 'mla_w_dq', 'mla_g_q', 'mla_w_uq', 'mla_w_dkv', 'mla_g_kv', 'mla_w_uk',
           'mla_w_uv', 'mla_w_o', 'cf_w_pw1', 'cf_b_pw1', 'cf_dw_w', 'cf_dw_b', 'cf_norm_g', 'cf_norm_b', 'cf_w_pw2',
           'cf_b_pw2', 'ff_w1', 'ff_w2', 'ln_mix_g', 'ln_mix_b', 'ln_ff_g', 'ln_ff_b']
ARGS = ['x'] + WEIGHTS + ['loss_target'] + ['m_' + n for n in WEIGHTS] + ['v_' + n for n in WEIGHTS]


def _sq_relu(h):
    r = jnp.maximum(h, jnp.zeros_like(h))
    return r * r


def _mlp_forward(i, x, xb, w1, w2, g, b):
    hb = mm_plain_nn(f"mlp{i}_up", xb, w1, BF16, tm=2048, tn=1024)
    y, yb, xh, rstd = mm_residual_ln(f"mlp{i}_down_ln", hb, w2, x, g, b, tk=4096, a_fn=_sq_relu)
    return (y, yb), dict(xb=xb, hb=hb, xh=xh, rstd=rstd, g=g)


def _mlp_backward(i, dr, drb, sv, w1, w2, dw1, dw2, reduce_after, mixer_ln):
    s = dr.shape[0]
    tm, tn = _tile(s, 1024), 1024

    def epi(acc, e, o):
        o[0][...] = (acc * (2.0 * jnp.maximum(e[0][...].astype(F32), 0.0))).astype(BF16)

    dhb = mm_nt(f"mlp{i}_down_bwd", drb, w2, s, tm, tn, 1024, epi, [_sds((s, w2.k), BF16)], [_ij(tm, tn)],
                [sv["hb"]], [_ij(tm, tn)])[0]
    g_w2 = mm_tn(f"mlp{i}_dw2", sv["hb"], drb, dw2, s, 1024, 1024, a_fn=_sq_relu)
    g_w1 = mm_tn(f"mlp{i}_dw1", sv["xb"], dhb, dw1, s, 1024, 1024)
    dhb = reduce_after(dhb, {f"w1_{i}": g_w1, f"w2_{i}": g_w2})
    return mm_nt_ln_backward(f"mlp{i}_up_bwd", dhb, w1, dr, *mixer_ln, tk=2048)


def kernel(x, sc_w_in, sc_conv_w, sc_w_out, mla_w_dq, mla_g_q, mla_w_uq, mla_w_dkv, mla_g_kv, mla_w_uk, mla_w_uv, mla_w_o, cf_w_pw1, cf_b_pw1, cf_dw_w, cf_dw_b, cf_norm_g, cf_norm_b, cf_w_pw2, cf_b_pw2, ff_w1, ff_w2, ln_mix_g, ln_mix_b, ln_ff_g, ln_ff_b, loss_target, m_sc_w_in, m_sc_conv_w, m_sc_w_out, m_mla_w_dq, m_mla_g_q, m_mla_w_uq, m_mla_w_dkv, m_mla_g_kv, m_mla_w_uk, m_mla_w_uv, m_mla_w_o, m_cf_w_pw1, m_cf_b_pw1, m_cf_dw_w, m_cf_dw_b, m_cf_norm_g, m_cf_norm_b, m_cf_w_pw2, m_cf_b_pw2, m_ff_w1, m_ff_w2, m_ln_mix_g, m_ln_mix_b, m_ln_ff_g, m_ln_ff_b, v_sc_w_in, v_sc_conv_w, v_sc_w_out, v_mla_w_dq, v_mla_g_q, v_mla_w_uq, v_mla_w_dkv, v_mla_g_kv, v_mla_w_uk, v_mla_w_uv, v_mla_w_o, v_cf_w_pw1, v_cf_b_pw1, v_cf_dw_w, v_cf_dw_b, v_cf_norm_g, v_cf_norm_b, v_cf_w_pw2, v_cf_b_pw2, v_ff_w1, v_ff_w2, v_ln_mix_g, v_ln_mix_b, v_ln_ff_g, v_ln_ff_b):
    given = dict(zip(ARGS, (x, sc_w_in, sc_conv_w, sc_w_out, mla_w_dq, mla_g_q, mla_w_uq, mla_w_dkv, mla_g_kv, mla_w_uk, mla_w_uv, mla_w_o, cf_w_pw1, cf_b_pw1, cf_dw_w, cf_dw_b, cf_norm_g, cf_norm_b, cf_w_pw2, cf_b_pw2, ff_w1, ff_w2, ln_mix_g, ln_mix_b, ln_ff_g, ln_ff_b, loss_target, m_sc_w_in, m_sc_conv_w, m_sc_w_out, m_mla_w_dq, m_mla_g_q, m_mla_w_uq, m_mla_w_dkv, m_mla_g_kv, m_mla_w_uk, m_mla_w_uv, m_mla_w_o, m_cf_w_pw1, m_cf_b_pw1, m_cf_dw_w, m_cf_dw_b, m_cf_norm_g, m_cf_norm_b, m_cf_w_pw2, m_cf_b_pw2, m_ff_w1, m_ff_w2, m_ln_mix_g, m_ln_mix_b, m_ln_ff_g, m_ln_ff_b, v_sc_w_in, v_sc_conv_w, v_sc_w_out, v_mla_w_dq, v_mla_g_q, v_mla_w_uq, v_mla_w_dkv, v_mla_g_kv, v_mla_w_uk, v_mla_w_uv, v_mla_w_o, v_cf_w_pw1, v_cf_b_pw1, v_cf_dw_w, v_cf_dw_b, v_cf_norm_g, v_cf_norm_b, v_cf_w_pw2, v_cf_b_pw2, v_ff_w1, v_ff_w2, v_ln_mix_g, v_ln_mix_b, v_ln_ff_g, v_ln_ff_b)))
    s, d = x.shape[1], x.shape[2]
    d_ff = 4 * d
    dq4 = d // N_CHIPS
    xq = lax.axis_index("x") * 2 + lax.axis_index("y")

    w_dkv_pad = jnp.pad(mla_w_dkv[0], ((0, 0), (0, 128 - QK_ROPE)))
    w_uq_pad = jnp.pad(mla_w_uq[0].reshape(Q_LORA, 2, QK_NOPE + QK_ROPE), ((0, 0), (0, 0), (0, HEAD_PAD - QK_NOPE - QK_ROPE)))
    small = pack_rows("vector_weights_pack", [
        sc_conv_w.reshape(2 * SC_WIDTH, dq4), cf_b_pw1.reshape(2, dq4), cf_dw_w[0], cf_dw_b, cf_norm_g, cf_norm_b,
        cf_b_pw2], 64)
    mlp_w = lambda i: [ff_w1[i].astype(BF16), ff_w2[i].astype(BF16)]
    g_in, g_out, g_w1, g_w2 = [None] * 2, [None] * 2, [None] * DEPTH, [None] * DEPTH
    g_in[0], g_out[0], g_small = gather_shards(
        "gather_mixer0", [sc_w_in[0].astype(BF16), sc_w_out[0].astype(BF16), small], by_columns=(0,))
    (g_w1[0],) = gather_shards("gather_up0", [ff_w1[0].astype(BF16)], by_columns=(0,))
    (g_w2[0],) = gather_shards("gather_down0", [ff_w2[0].astype(BF16)])
    g_dqkv, g_uq, g_uk, g_uv, g_o = gather_shards("gather_mixer1", [
        jnp.concatenate([mla_w_dq[0], w_dkv_pad], axis=1).astype(BF16),
        w_uq_pad.reshape(Q_LORA, 2 * HEAD_PAD).astype(BF16),
        mla_w_uk.reshape(KV_LORA // N_CHIPS, N_HEADS * QK_NOPE).astype(BF16),
        mla_w_uv.reshape(KV_LORA // N_CHIPS, N_HEADS * V_HEAD).astype(BF16), mla_w_o[0].astype(BF16)], by_columns=(1,))
    g_w1[1], g_w2[1] = gather_shards("gather_mlp1", mlp_w(1), by_columns=(0,))
    g_pw1, g_pw2, g_w1[2], g_w2[2] = gather_shards(
        "gather_layer2", [cf_w_pw1[0].astype(BF16), cf_w_pw2[0].astype(BF16)] + mlp_w(2), by_columns=(0, 2))
    g_in[1], g_out[1], g_w1[3], g_w2[3] = gather_shards(
        "gather_layer3", [sc_w_in[1].astype(BF16), sc_w_out[1].astype(BF16)] + mlp_w(3), by_columns=(0, 2))

    wd_t = Q_LORA + KV_LORA + 128
    w_in = [Stk("full", d, 3 * d, g_in[j]) for j in range(2)]
    w_out = [Stk("row", d, d, g_out[j]) for j in range(2)]
    w_dqkv = Stk("row", d, wd_t, g_dqkv)
    w_uq = Stk("full", Q_LORA, N_HEADS * HEAD_PAD, g_uq)
    w_uk = Stk("row", KV_LORA, N_HEADS * QK_NOPE, g_uk)
    w_uv = Stk("row", KV_LORA, N_HEADS * V_HEAD, g_uv)
    w_o = Stk("row", d, d, g_o)
    w_pw1 = Stk("full", d, 2 * d, g_pw1)
    w_pw2 = Stk("row", d, d, g_pw2)
    w_1 = [Stk("full", d, d_ff, g_w1[i]) for i in range(DEPTH)]
    w_2 = [Stk("row", d_ff, d, g_w2[i]) for i in range(DEPTH)]

    def wide(rows):
        return jnp.swapaxes(rows, 0, 1).reshape(rows.shape[1], d)

    conv_w = wide(g_small[:, 0:6]).reshape(2, SC_WIDTH, d)
    b_pw1 = g_small[:, 6:8].reshape(1, 2 * d)
    dw_w = wide(g_small[:, 8:39])
    dw_b, norm_g, norm_b, b_pw2 = (wide(g_small[:, 39 + k:40 + k]) for k in range(4))

    pos = jnp.arange(s, dtype=F32)
    inv_freq = ROPE_THETA ** (-jnp.arange(0, QK_ROPE, 2, dtype=F32) / QK_ROPE)
    ang = pos[:, None] * inv_freq[None, :]
    cos, sin, zero = jnp.cos(ang), jnp.sin(ang), jnp.zeros((s, 128 - QK_ROPE), F32)
    cf = jnp.concatenate([cos, cos, zero], axis=1)
    sf = jnp.concatenate([-sin, sin, zero], axis=1)

    def row(a, i):
        return a[i:i + 1]

    xs = x.reshape(s, d)
    cur = (xs, xs.astype(BF16))
    tape = []
    for i in range(DEPTH):
        mixer, j = i % 3, i // 3
        xf, xb = cur
        lg, lb = row(ln_mix_g, i), row(ln_mix_b, i)
        if mixer == 0:
            u = mm_plain_nn(f"sc{j}_in", xb, w_in[j], F32, tn=3 * dq4)
            gb = short_conv_gate(u, conv_w[j])
            y, yb, xh, rstd = mm_residual_ln(f"sc{j}_out_ln", gb, w_out[j], xf, lg, lb)
            sv = dict(xb=xb, u=u, gb=gb)
        elif mixer == 1:
            t = mm_plain_nn("mla_down", xb, w_dqkv, F32, tn=wd_t // 2)
            cq, ckv, kpe = mla_latents(t, mla_g_q, mla_g_kv, cf, sf)
            qh = mla_queries(cq, w_uq, cf, sf)
            kh = mla_keys(ckv, w_uk, kpe)
            vh = mm_plain_nn("mla_values", ckv, w_uv, BF16, tk=KV_LORA)
            oh = attention(qh, kh, vh)
            y, yb, xh, rstd = mm_residual_ln("mla_out_ln", oh, w_o, xf, lg, lb)
            sv = dict(xb=xb, t=t, cq=cq, ckv=ckv, qh=qh, kh=kh, vh=vh, oh=oh)
        else:
            u = mm_plain_nn("cf_pw1", xb, w_pw1, F32, bias=b_pw1)
            hc = conformer_glu_conv(u, dw_w, dw_b)
            sb = conformer_norm_swish(hc, norm_g, norm_b)
            y, yb, xh, rstd = mm_residual_ln("cf_pw2_ln", sb, w_pw2, xf, lg, lb, bias=b_pw2)
            sv = dict(xb=xb, u=u, hc=hc, sb=sb)
        sv.update(xh=xh, rstd=rstd, g=lg)
        cur, sv_mlp = _mlp_forward(i, y, yb, w_1[i], w_2[i], row(ln_ff_g, i), row(ln_ff_b, i))
        tape.append((sv, sv_mlp))

    g_ln = {n: [None] * DEPTH for n in ("ln_mix_g", "ln_mix_b", "ln_ff_g", "ln_ff_b")}
    last = tape[DEPTH - 1][1]
    dr, drb, g_ln["ln_ff_g"][DEPTH - 1], g_ln["ln_ff_b"][DEPTH - 1], _, loss_part = loss_ln_backward(
        cur[0], loss_target.reshape(s, d), last["xh"], last["rstd"], last["g"])

    grads = {}
    smalls = {}
    conv_grads = [None, None]
    core = lax.axis_index("c").astype(jnp.int32).reshape(1)
    chip = xq.astype(jnp.int32).reshape(1)
    pairs, landed = {}, {}
    ready, theirs = [], {}

    def hold(xs, others):
        live = [x for x in xs if x is not None]
        out = lax.optimization_barrier((*live, *others))
        rest = iter(out[:len(live)])
        return tuple(None if x is None else next(rest) for x in xs), list(out[len(live):])

    def reduce_after(x, new, early=False):
        out = lax.optimization_barrier((x, *new.values()))
        grads.update(zip(new, out[1:]))
        if early:
            theirs.update(zip(new, pair_exchange(f"pair_exchange_{len(theirs)}", list(out[1:]), True)))
        ready.extend(new)
        return out[0]

    def reduce_layer(i, x):
        late = [n for n in ready if n not in theirs]
        if late:
            theirs.update(zip(late, pair_exchange(f"pair_exchange_layer{i}", [grads[n] for n in late], False)))
        sums = [pair_sum(grads[n], theirs[n], core) for n in ready]
        pairs.update(zip(ready, sums))
        landed.update(zip(ready, chip_exchange(f"chip_exchange_layer{i}", sums)))
        exchanged.append(list(ready))
        ready.clear()
        return hold(x, sums)[0]

    groups = [["in_0", "in_1"], ["out_0", "out_1"], ["dqkv"], ["uq"], ["uk"], ["uv"], ["o"], ["pw1"], ["pw2"],
              [f"w1_{i}" for i in range(DEPTH)], [f"w2_{i}" for i in range(DEPTH)]]
    stacks = [None] * len(groups)
    exchanged = []

    def sum_layer(x, last=False):
        names = exchanged.pop(0)
        if last:
            x, held = hold(x, [landed[n] for n in names])
            landed.update(zip(names, held))
        new = []
        for n in names:
            k = next(k for k, members in enumerate(groups) if n in members)
            stacks[k] = chip_sum(pairs[n], landed[n], chip, stacks[k], groups[k].index(n), len(groups[k]))
            new.append(stacks[k])
        return x if last else hold(x, new)[0]

    for i in reversed(range(DEPTH)):
        mixer, j = i % 3, i // 3
        sv, sv_mlp = tape[i]
        dr, drb, g_ln["ln_mix_g"][i], g_ln["ln_mix_b"][i], dr_sum = _mlp_backward(
            i, dr, drb, sv_mlp, w_1[i], w_2[i], Stk("col", d, d_ff), Stk("row", d_ff, d),
            lambda x_, new: reduce_after(x_, new, early=i > 0), (sv["xh"], sv["rstd"], sv["g"]))
        if i == 0:
            dr, drb = reduce_layer("0_mlp", (dr, drb))

        def to_input(name, a, w, tk, a_spec_fn=None):
            if i == 0:
                spec = None if a_spec_fn is None else (s, a_spec_fn)
                return mm_plain_nt(name, a, w, F32, tn=1024, tk=tk, add=dr, add_scale=ALPHA, a_spec_fn=spec), None
            prev = tape[i - 1][1]
            out = mm_nt_ln_backward(name, a, w, dr, prev["xh"], prev["rstd"], prev["g"], tk=tk, a_spec_fn=a_spec_fn)
            g_ln["ln_ff_g"][i - 1], g_ln["ln_ff_b"][i - 1] = out[2], out[3]
            return out[0], out[1]

        parts_of = lambda tm, tk: pl.BlockSpec((None, tm, tk), lambda i_, j_, k_: (k_, i_, 0))
        if mixer == 0:
            dgate = mm_plain_nt(f"sc{j}_out_bwd", drb, w_out[j], F32)
            dw_out = mm_tn(f"sc{j}_dw_out", sv["gb"], drb, Stk("row", d, d), s, 512, 1024)
            du, conv_grads[j] = short_conv_gate_bwd(sv["u"], conv_w[j], dgate)
            nb = d // 256
            dw_in = mm_tn(
                f"sc{j}_dw_in", sv["xb"], du, Stk("col", d, 3 * d), s, 1024, 256,
                b_spec=pl.BlockSpec((None, s, 256), lambda i_, j_, k_: (j_ // nb, k_, j_ % nb)))
            du = reduce_after(du, {f"in_{j}": dw_in, f"out_{j}": dw_out})
            dr, drb = to_input(f"sc{j}_in_bwd", du, w_in[j], d, parts_of)
        elif mixer == 1:
            do = mm_plain_nt("mla_out_bwd", drb, w_o, BF16)
            g_o = mm_tn("mla_dw_o", sv["oh"], drb, Stk("row", d, d), s, 512, 1024)
            dqh, dkh, dvh = attention_bwd(sv["qh"], sv["kh"], sv["vh"], do)
            dql, dkn, dkpe = mla_unrope_grads(dqh, dkh, cf, sf)
            g_uq = mm_tn("mla_dw_uq", sv["cq"], dql, Stk("col", Q_LORA, N_HEADS * HEAD_PAD), s, Q_LORA, 512)
            dcq = mm_plain_nt("mla_uq_bwd", dql, w_uq, F32, tn=Q_LORA)
            g_uk = mm_tn("mla_dw_uk", sv["ckv"], dkn, Stk("row", KV_LORA, N_HEADS * QK_NOPE), s, KV_LORA, 1024)
            g_uv = mm_tn("mla_dw_uv", sv["ckv"], dvh, Stk("row", KV_LORA, N_HEADS * V_HEAD), s, KV_LORA, 1024)
            dckv = mm_plain_nt("mla_uk_bwd", dkn, w_uk, F32, tn=KV_LORA)
            dckv = mm_plain_nt("mla_uv_bwd", dvh, w_uv, F32, tn=KV_LORA, add=dckv)
            dt, smalls["g_q"], smalls["g_kv"] = mla_latents_bwd(sv["t"], mla_g_q, mla_g_kv, cf, sf, dcq, dckv, dkpe)
            g_dqkv = mm_tn("mla_dw_down", sv["xb"], dt, Stk("row", d, wd_t), s, 512, wd_t)
            dt = reduce_after(dt, {"dqkv": g_dqkv, "uq": g_uq, "uk": g_uk, "uv": g_uv, "o": g_o})
            dr, drb = to_input("mla_down_bwd", dt, w_dqkv, wd_t)
        else:
            dsw = mm_plain_nt("cf_pw2_bwd", drb, w_pw2, F32)
            g_pw2 = mm_tn("cf_dw_pw2", sv["sb"], drb, Stk("row", d, d), s, 512, 1024)
            smalls["b_pw2"] = dr_sum
            dhc, smalls["norm_g"], smalls["norm_b"] = conformer_norm_swish_bwd(sv["hc"], norm_g, norm_b, dsw)
            du, smalls["b_pw1"], smalls["dw_w"], smalls["dw_b"] = conformer_glu_conv_bwd(sv["u"], dw_w, dhc)
            nb = d // 512
            g_pw1 = mm_tn(
                "cf_dw_pw1", sv["xb"], du, Stk("col", d, 2 * d), s, 1024, 512,
                b_spec=pl.BlockSpec((None, s, 512), lambda i_, j_, k_: (j_ // nb, k_, j_ % nb)))
            du = reduce_after(du, {"pw1": g_pw1, "pw2": g_pw2})
            dr, drb = to_input("cf_pw1_bwd", du, w_pw1, d, parts_of)
        if i < DEPTH - 1:
            dr, drb = sum_layer((dr, drb))
        dr, drb = reduce_layer(i, (dr, drb))
    grad_x = sum_layer(sum_layer((dr, None), last=True), last=True)[0].reshape(1, s, d)

    mine = stacks
    other = (pair_share("pair_share_mixers", mine[:9]) + pair_share("pair_share_up", mine[9:10])
             + pair_share("pair_share_down", mine[10:]))

    def padded(get):
        dqkv = jnp.concatenate([get("mla_w_dq")[0], jnp.pad(get("mla_w_dkv")[0], ((0, 0), (0, 128 - QK_ROPE)))], axis=1)
        uq = jnp.pad(get("mla_w_uq")[0].reshape(Q_LORA, 2, QK_NOPE + QK_ROPE),
                     ((0, 0), (0, 0), (0, HEAD_PAD - QK_NOPE - QK_ROPE))).reshape(Q_LORA, 2 * HEAD_PAD)
        return [get("sc_w_in"), get("sc_w_out"), dqkv[None], uq[None],
                get("mla_w_uk").reshape(1, KV_LORA // N_CHIPS, d), get("mla_w_uv").reshape(1, KV_LORA // N_CHIPS, d),
                get("mla_w_o"), get("cf_w_pw1"), get("cf_w_pw2"), get("ff_w1"), get("ff_w2")]

    w_l, m_l, v_l = (padded(lambda n, p=p: given[p + n]) for p in ("", "m_", "v_"))
    res = [adamw_joined(w_l[k], m_l[k], v_l[k], mine[k], other[k], core) for k in range(len(groups))]

    def unpadded(k):
        r_in, r_out, r_dqkv, r_uq, r_uk, r_uv, r_o, r_pw1, r_pw2, r_w1, r_w2 = (r[k] for r in res)
        return {
            "sc_w_in": r_in, "sc_w_out": r_out, "mla_w_dq": r_dqkv[:, :, 0:Q_LORA],
            "mla_w_dkv": r_dqkv[:, :, Q_LORA:Q_LORA + KV_LORA + QK_ROPE],
            "mla_w_uq": r_uq.reshape(1, Q_LORA, 2, HEAD_PAD)[:, :, :, 0:QK_NOPE + QK_ROPE].reshape(mla_w_uq.shape),
            "mla_w_uk": r_uk.reshape(mla_w_uk.shape), "mla_w_uv": r_uv.reshape(mla_w_uv.shape),
            "mla_w_o": r_o, "cf_w_pw1": r_pw1, "cf_w_pw2": r_pw2, "ff_w1": r_w1, "ff_w2": r_w2}

    big_g, big_d, big_m, big_v = (unpadded(k) for k in range(4))

    pad_row = lambda a: jnp.pad(a, ((0, 0), (0, d - a.shape[1])))
    small_parts = ([g for n in ("ln_mix_g", "ln_mix_b", "ln_ff_g", "ln_ff_b") for g in g_ln[n]]
                   + [pad_row(smalls["g_q"]), pad_row(smalls["g_kv"]), conv_grads[0], conv_grads[1],
                      smalls["b_pw1"].reshape(2, d), smalls["dw_w"], smalls["dw_b"], smalls["norm_g"], smalls["norm_b"],
                      smalls["b_pw2"], loss_part])
    red = all_reduce_small(small_parts, 64)
    loss = red[61, 0]

    where = {
        "ln_mix_g": [((), 0, DEPTH, "all")], "ln_mix_b": [((), 4, DEPTH, "all")],
        "ln_ff_g": [((), 8, DEPTH, "all")], "ln_ff_b": [((), 12, DEPTH, "all")],
        "mla_g_q": [((), 16, 1, Q_LORA)], "mla_g_kv": [((), 17, 1, KV_LORA)],
        "sc_conv_w": [((0,), 18, SC_WIDTH, "chip"), ((1,), 21, SC_WIDTH, "chip")],
        "cf_b_pw1": [((), 24, 2, "chip")], "cf_dw_w": [((0,), 26, CONF_WIDTH, "chip")],
        "cf_dw_b": [((), 57, 1, "chip")], "cf_norm_g": [((), 58, 1, "chip")], "cf_norm_b": [((), 59, 1, "chip")],
        "cf_b_pw2": [((), 60, 1, "chip")]}
    vec = list(where)
    vec_res = vector_update(red, chip, [given[n] for n in vec], [given["m_" + n] for n in vec],
                            [given["v_" + n] for n in vec], [where[n] for n in vec])
    gw = dict(big_g)
    upd = {n: [big_d[n], big_m[n], big_v[n]] for n in big_g}
    for k, n in enumerate(vec):
        gw[n] = vec_res[0][k]
        upd[n] = [vec_res[1][k], vec_res[2][k], vec_res[3][k]]

    return (loss, grad_x, *[gw[n] for n in WEIGHTS], *[upd[n][0] for n in WEIGHTS],
            *[upd[n][1] for n in WEIGHTS], *[upd[n][2] for n in WEIGHTS])
```

```python
import jax
import jax.numpy as jnp
from jax import lax
from jax.experimental import pallas as pl
from jax.experimental.pallas import tpu as pltpu
from jax.experimental.pallas import tpu_sc as plsc

F32 = jnp.float32
BF16 = jnp.bfloat16
MESH = pl.DeviceIdType.MESH

DEPTH = 4
ALPHA = (2.0 * DEPTH) ** 0.25
LN_EPS = 1e-5
RMS_EPS = 1e-6
CHUNK_SHIFT = 6
N_HEADS = 8
QK_NOPE = 128
QK_ROPE = 64
V_HEAD = 128
HEAD_PAD = 256
Q_LORA = 384
KV_LORA = 256
ROPE_THETA = 10000.0
SC_WIDTH = 3
CONF_WIDTH = 31
CONV_PAD = 32
CONV_CHUNK = 64
N_CHIPS = 4
ATTN_SCALE = (QK_NOPE + QK_ROPE) ** -0.5

ADAM_LR = 0.001
ADAM_B1 = 0.9
ADAM_B2 = 0.999
ADAM_EPS = 1e-08
ADAM_WD = 0.01
ADAM_STEP = 10

VMEM_LIMIT = 56 * 2**20

NN = (((1,), (0,)), ((), ()))
NT = (((1,), (1,)), ((), ()))
TN = (((0,), (0,)), ((), ()))


def _params(sem=None):
    return pltpu.CompilerParams(dimension_semantics=sem, vmem_limit_bytes=VMEM_LIMIT)


class Stk:
    def __init__(self, kind, k, n, arr=None):
        self.kind, self.k, self.n = kind, k, n
        self.plain = kind != "col"
        self.nloc = n // N_CHIPS if kind == "col" else n
        self.arr = arr.reshape(k, n) if arr is not None and self.plain else arr

    @property
    def shape(self):
        return (self.k, self.n) if self.plain else (N_CHIPS, self.k, self.nloc)

    def spec(self, bk, bn, f):
        if self.plain:
            return pl.BlockSpec((bk, bn), f)
        assert self.k % bk == 0 and self.nloc % bn == 0, (self.k, bk, self.nloc, bn)
        pn = self.nloc // bn

        def imap(*g):
            kb, nb = f(*g)
            return nb // pn, kb, nb % pn

        return pl.BlockSpec((None, bk, bn), imap)


def _mm(name, mode, a, b, grid, a_spec, b_spec, acc_shape, extras, extra_specs, out_shapes, out_specs, epi, a_fn=None,
        rows_in_order=False):
    nk = grid[2]
    ne = len(extras)

    def body(*refs):
        a_ref, b_ref = refs[0], refs[1]
        e_refs = refs[2:2 + ne]
        av = a_ref[...] if a_fn is None else a_fn(a_ref[...])
        part = lax.dot_general(av, b_ref[...], mode, preferred_element_type=F32)
        if nk == 1:
            epi(part, e_refs, refs[2 + ne:])
            return
        o_refs = refs[2 + ne:-1]
        acc = refs[-1]
        k = pl.program_id(2)

        @pl.when(k == 0)
        def _():
            acc[...] = part

        @pl.when(k > 0)
        def _():
            acc[...] += part

        @pl.when(k == nk - 1)
        def _():
            epi(acc[...], e_refs, o_refs)

    return pl.pallas_call(
        body, grid=grid, in_specs=[a_spec, b_spec, *extra_specs], out_specs=out_specs, out_shape=out_shapes,
        scratch_shapes=[pltpu.VMEM(acc_shape, F32)] if nk > 1 else [],
        compiler_params=_params(("arbitrary",) * 3 if rows_in_order else ("parallel", "parallel", "arbitrary")),
        name=name)(a, b, *extras)


def _tile(n, t):
    t = min(n, t)
    while n % t:
        t -= 8
    assert t > 0, (n, t)
    return t


def mm_nn(name, a, w, tm, tn, tk, epi, out_shapes, out_specs, extras=(), extra_specs=(), a_spec=None, a_fn=None):
    m = a.shape[0]
    tm, tn, tk = _tile(m, tm), _tile(w.n, tn), _tile(w.k, tk)
    grid = (m // tm, w.n // tn, w.k // tk)
    a_spec = a_spec or pl.BlockSpec((tm, tk), lambda i, j, k: (i, k))
    b_spec = w.spec(tk, tn, lambda i, j, k: (k, j))
    return _mm(name, NN, a, w.arr, grid, a_spec, b_spec, (tm, tn), extras, extra_specs, out_shapes, out_specs, epi, a_fn)


def mm_nt(name, a, w, m, tm, tn, tk, epi, out_shapes, out_specs, extras=(), extra_specs=(), a_spec=None,
          rows_in_order=False):
    tm, tn, tk = _tile(m, tm), _tile(w.k, tn), _tile(w.n, tk)
    grid = (m // tm, w.k // tn, w.n // tk)
    a_spec = a_spec or pl.BlockSpec((tm, tk), lambda i, j, k: (i, k))
    b_spec = w.spec(tn, tk, lambda i, j, k: (j, k))
    return _mm(name, NT, a, w.arr, grid, a_spec, b_spec, (tm, tn), extras, extra_specs, out_shapes, out_specs, epi,
               rows_in_order=rows_in_order)


def mm_tn(name, a, b, dw, s, tm=512, tn=512, tk=4096, a_spec=None, b_spec=None, a_fn=None):
    tm, tn, tk = _tile(dw.k, tm), _tile(dw.n, tn), _tile(s, tk)
    grid = (dw.k // tm, dw.n // tn, s // tk)
    a_spec = a_spec or pl.BlockSpec((tk, tm), lambda i, j, k: (k, i))
    b_spec = b_spec or pl.BlockSpec((tk, tn), lambda i, j, k: (k, j))

    def epi(acc, e, o):
        o[0][...] = acc.astype(BF16)

    out = _mm(name, TN, a, b, grid, a_spec, b_spec, (tm, tn), (), (), [jax.ShapeDtypeStruct(dw.shape, BF16)],
              [dw.spec(tm, tn, lambda i, j, k: (i, j))], epi, a_fn)[0]
    return out.reshape(N_CHIPS, dw.k // N_CHIPS, dw.n) if dw.plain else out


def _sds(shape, dtype):
    return jax.ShapeDtypeStruct(shape, dtype)


def _ij(tm, tn):
    return pl.BlockSpec((tm, tn), lambda i, j, k: (i, j))


def _i0(tm, c):
    return pl.BlockSpec((tm, c), lambda i, j, k: (i, 0))


def _0j(r, tn):
    return pl.BlockSpec((r, tn), lambda i, j, k: (0, j))


def _layer_norm_rows(r, g, b):
    mu = jnp.mean(r, axis=-1, keepdims=True)
    d = r - mu
    var = jnp.mean(d * d, axis=-1, keepdims=True)
    rstd = lax.rsqrt(var + LN_EPS)
    xh = d * rstd
    return xh * g + b, xh, rstd


def mm_residual_ln(name, a, w, x, g, b, bias=None, tm=512, tk=1024, a_fn=None):
    s, d = x.shape
    tm = _tile(s, tm)
    extras = [x, g, b] + ([bias] if bias is not None else [])
    especs = [_i0(tm, d), _0j(1, d), _0j(1, d)] + ([_0j(1, d)] if bias is not None else [])

    def epi(acc, e, o):
        r = ALPHA * e[0][...] + acc
        if bias is not None:
            r = r + e[3][...]
        y, xh, rstd = _layer_norm_rows(r, e[1][...], e[2][...])
        o[0][...] = y
        o[1][...] = y.astype(BF16)
        o[2][...] = xh
        o[3][...] = rstd

    return mm_nn(name, a, w, tm, d, tk, epi,
                 [_sds((s, d), F32), _sds((s, d), BF16), _sds((s, d), F32), _sds((s, 1), F32)],
                 [_i0(tm, d), _i0(tm, d), _i0(tm, d), _i0(tm, 1)], extras, especs, a_fn=a_fn)


def mm_plain_nn(name, a, w, out_dtype, tm=1024, tn=512, tk=1024, bias=None):
    m = a.shape[0]
    tm, tn = _tile(m, tm), _tile(w.n, tn)

    def epi(acc, e, o):
        if bias is not None:
            acc = acc + e[0][...]
        o[0][...] = acc.astype(out_dtype)

    extras, especs = ([bias], [_0j(1, tn)]) if bias is not None else ((), ())
    return mm_nn(name, a, w, tm, tn, tk, epi, [_sds((m, w.n), out_dtype)], [_ij(tm, tn)], extras, especs)[0]


def mm_plain_nt(name, a, w, out_dtype, tm=1024, tn=512, tk=1024, add=None, add_scale=1.0, a_spec_fn=None):
    m = a.shape[0] if a_spec_fn is None else a_spec_fn[0]
    tm, tn = _tile(m, tm), _tile(w.k, tn)
    tk = _tile(w.n, tk)

    def epi(acc, e, o):
        if add is not None:
            acc = acc + add_scale * e[0][...].astype(F32)
        o[0][...] = acc.astype(out_dtype)

    extras, especs = ([add], [_ij(tm, tn)]) if add is not None else ((), ())
    a_spec = None if a_spec_fn is None else a_spec_fn[1](tm, tk)
    return mm_nt(name, a, w, m, tm, tn, tk, epi, [_sds((m, w.k), out_dtype)], [_ij(tm, tn)], extras, especs,
                 a_spec=a_spec)[0]


def _rows(tm, c):
    return pl.BlockSpec((tm, c), lambda i: (i, 0))


def _fix(shape):
    nd = len(shape)
    return pl.BlockSpec(shape, lambda i: (0,) * nd)


def _accumulate(ref, val):
    @pl.when(pl.program_id(0) == 0)
    def _():
        ref[...] = jnp.zeros_like(ref)

    ref[...] += val


def _ln_backward_rows(dyv, xh, rstd, g, dr_ref, drb_ref, dg_ref, db_ref, ds_ref):
    dxh = dyv * g
    m1 = jnp.mean(dxh, axis=-1, keepdims=True)
    m2 = jnp.mean(dxh * xh, axis=-1, keepdims=True)
    dr = rstd * (dxh - m1 - xh * m2)
    dr_ref[...] = dr
    drb_ref[...] = dr.astype(BF16)
    _accumulate(dg_ref, jnp.sum(dyv * xh, axis=0, keepdims=True))
    _accumulate(db_ref, jnp.sum(dyv, axis=0, keepdims=True))
    _accumulate(ds_ref, jnp.sum(dr, axis=0, keepdims=True))


def mm_nt_ln_backward(name, a, w, add, xhat, rstd, g, tm=512, tk=1024, a_spec_fn=None):
    m, d = add.shape
    tm, tk = _tile(m, tm), _tile(w.n, tk)

    def epi(acc, e, o):
        _ln_backward_rows(acc + ALPHA * e[0][...], e[1][...], e[2][...], e[3][...], *o)

    vec = pl.BlockSpec((1, d), lambda i, j, k: (0, 0))
    a_spec = None if a_spec_fn is None else a_spec_fn(tm, tk)
    return mm_nt(name, a, w, m, tm, d, tk, epi,
                 [_sds((m, d), F32), _sds((m, d), BF16), _sds((1, d), F32), _sds((1, d), F32), _sds((1, d), F32)],
                 [_i0(tm, d), _i0(tm, d), vec, vec, vec], [add, xhat, rstd, g],
                 [_i0(tm, d), _i0(tm, d), _i0(tm, 1), vec], a_spec=a_spec, rows_in_order=True)


def loss_ln_backward(y, target, xhat, rstd, g, tm=512):
    s, d = y.shape
    tm = _tile(s, tm)

    def body(y_ref, t_ref, xh_ref, rstd_ref, g_ref, dr_ref, drb_ref, dg_ref, db_ref, ds_ref, loss_ref):
        e = y_ref[...] - t_ref[...]
        part = 0.5 * jnp.sum(jnp.mean(e * e, axis=-1, keepdims=True), axis=0, keepdims=True)
        _accumulate(loss_ref, jnp.broadcast_to(part, (1, d)))
        _ln_backward_rows(e * (1.0 / d), xh_ref[...], rstd_ref[...], g_ref[...], dr_ref, drb_ref, dg_ref, db_ref, ds_ref)

    return pl.pallas_call(
        body, grid=(s // tm,),
        in_specs=[_rows(tm, d), _rows(tm, d), _rows(tm, d), _rows(tm, 1), _fix((1, d))],
        out_specs=[_rows(tm, d), _rows(tm, d)] + [_fix((1, d))] * 4,
        out_shape=[_sds((s, d), F32), _sds((s, d), BF16)] + [_sds((1, d), F32)] * 4,
        compiler_params=_params(("arbitrary",)), name="loss_ln_backward")(y, target, xhat, rstd, g)


def _cols(s, tc, off=0):
    return pl.BlockSpec((s, tc), lambda i: (0, i + off))


def _shift_down(z, sft, rows):
    return jnp.where(rows >= sft, pltpu.roll(z, sft, 0), 0.0)


def _shift_up(z, sft, rows, s):
    return jnp.where(rows < s - sft, pltpu.roll(z, (s - sft) % s, 0), 0.0)


def short_conv_gate(u, conv_w, tc=256):
    s, d3 = u.shape
    d = d3 // 3
    nb = d // tc

    def body(b_ref, c_ref, h_ref, w_ref, o_ref):
        rows = lax.broadcasted_iota(jnp.int32, (s, tc), 0)
        z = c_ref[...] * h_ref[...]
        cz = jnp.zeros((s, tc), F32)
        for k in range(SC_WIDTH):
            sft = SC_WIDTH - 1 - k
            cz = cz + w_ref[pl.ds(k, 1), :] * (_shift_down(z, sft, rows) if sft else z)
        o_ref[...] = (b_ref[...] * cz).astype(BF16)

    return pl.pallas_call(
        body, grid=(nb,),
        in_specs=[_cols(s, tc), _cols(s, tc, nb), _cols(s, tc, 2 * nb), _cols(SC_WIDTH, tc)],
        out_specs=_cols(s, tc), out_shape=_sds((s, d), BF16),
        compiler_params=_params(("parallel",)), name="short_conv_gate")(u, u, u, conv_w)


def short_conv_gate_bwd(u, conv_w, dg, tc=256):
    s, d3 = u.shape
    d = d3 // 3
    nb = d // tc

    def body(b_ref, c_ref, h_ref, w_ref, dg_ref, du_ref, dw_ref):
        rows = lax.broadcasted_iota(jnp.int32, (s, tc), 0)
        c, h, dgv = c_ref[...], h_ref[...], dg_ref[...]
        z = c * h
        dcz = dgv * b_ref[...]
        cz = jnp.zeros((s, tc), F32)
        dz = jnp.zeros((s, tc), F32)
        for k in range(SC_WIDTH):
            sft = SC_WIDTH - 1 - k
            zs = _shift_down(z, sft, rows) if sft else z
            wk = w_ref[pl.ds(k, 1), :]
            cz = cz + wk * zs
            dz = dz + wk * (_shift_up(dcz, sft, rows, s) if sft else dcz)
            dw_ref[pl.ds(k, 1), :] = jnp.sum(dcz * zs, axis=0, keepdims=True)
        du_ref[0] = (dgv * cz).astype(BF16)
        du_ref[1] = (dz * h).astype(BF16)
        du_ref[2] = (dz * c).astype(BF16)

    return pl.pallas_call(
        body, grid=(nb,),
        in_specs=[_cols(s, tc), _cols(s, tc, nb), _cols(s, tc, 2 * nb), _cols(SC_WIDTH, tc), _cols(s, tc)],
        out_specs=[pl.BlockSpec((3, s, tc), lambda i: (0, 0, i)), _cols(SC_WIDTH, tc)],
        out_shape=[_sds((3, s, d), BF16), _sds((SC_WIDTH, d), F32)],
        compiler_params=_params(("parallel",)), name="short_conv_gate_bwd")(u, u, u, conv_w, dg)


def _store_shifted_down(ref, z, rows):
    s, tc = z.shape
    for b in range(8):
        ref[b, pl.ds(0, CONV_PAD), :] = jnp.zeros((CONV_PAD, tc), F32)
        ref[b, pl.ds(CONV_PAD, s), :] = z if b == 0 else _shift_down(z, b, rows)


def _store_shifted_up(ref, z, rows):
    s, tc = z.shape
    for b in range(8):
        ref[b, pl.ds(0, s), :] = z if b == 0 else _shift_up(z, b, rows, s)
        ref[b, pl.ds(s, CONV_PAD), :] = jnp.zeros((CONV_PAD, tc), F32)


def conformer_glu_conv(u, dw_w, dw_b, tc=128):
    s, d2 = u.shape
    d = d2 // 2
    nb = d // tc

    ch = min(CONV_CHUNK, s)

    def body(a_ref, g_ref, w_ref, b_ref, o_ref, down):
        rows = lax.broadcasted_iota(jnp.int32, (s, tc), 0)
        _store_shifted_down(down, a_ref[...] * jax.nn.sigmoid(g_ref[...]), rows)

        def chunk(ci, carry):
            r0 = pl.multiple_of(ci * ch, ch)
            acc = jnp.broadcast_to(b_ref[...], (ch, tc))
            for k in range(CONF_WIDTH):
                sft = CONF_WIDTH - 1 - k
                acc = acc + w_ref[pl.ds(k, 1), :] * down[sft % 8, pl.ds(CONV_PAD + r0 - (sft // 8) * 8, ch), :]
            o_ref[pl.ds(r0, ch), :] = acc
            return carry

        lax.fori_loop(0, s // ch, chunk, 0)

    return pl.pallas_call(
        body, grid=(nb,),
        in_specs=[_cols(s, tc), _cols(s, tc, nb), _cols(CONF_WIDTH, tc), _cols(1, tc)],
        out_specs=_cols(s, tc), out_shape=_sds((s, d), F32),
        scratch_shapes=[pltpu.VMEM((8, CONV_PAD + s, tc), F32)],
        compiler_params=_params(("parallel",)), name="conformer_glu_conv")(u, u, dw_w, dw_b)


def conformer_glu_conv_bwd(u, dw_w, dhc, tc=128):
    s, d2 = u.shape
    d = d2 // 2
    nb = d // tc
    ch = min(CONV_CHUNK, s)

    def body(a_ref, g_ref, w_ref, dhc_ref, du_ref, dbias_ref, dw_ref, db_ref, down, up, dw_acc, dh_buf):
        rows = lax.broadcasted_iota(jnp.int32, (s, tc), 0)
        a = a_ref[...]
        sg = jax.nn.sigmoid(g_ref[...])
        dhcv = dhc_ref[...]
        _store_shifted_down(down, a * sg, rows)
        _store_shifted_up(up, dhcv, rows)
        dw_acc[...] = jnp.zeros_like(dw_acc)

        def chunk(ci, carry):
            r0 = pl.multiple_of(ci * ch, ch)
            dc = dhc_ref[pl.ds(r0, ch), :]
            dh = jnp.zeros((ch, tc), F32)
            for k in range(CONF_WIDTH):
                sft = CONF_WIDTH - 1 - k
                a8, b = (sft // 8) * 8, sft % 8
                dh = dh + w_ref[pl.ds(k, 1), :] * up[b, pl.ds(r0 + a8, ch), :]
                prod = dc * down[b, pl.ds(CONV_PAD + r0 - a8, ch), :]
                dw_acc[k] += jnp.sum(prod.reshape(ch // 8, 8, tc), axis=0)
            dh_buf[pl.ds(r0, ch), :] = dh
            return carry

        lax.fori_loop(0, s // ch, chunk, 0)
        dh = dh_buf[...]
        da = dh * sg
        dgate = dh * a * sg * (1.0 - sg)
        du_ref[0] = da.astype(BF16)
        du_ref[1] = dgate.astype(BF16)
        dbias_ref[pl.ds(0, 1), :] = jnp.sum(da, axis=0, keepdims=True)
        dbias_ref[pl.ds(1, 1), :] = jnp.sum(dgate, axis=0, keepdims=True)
        db_ref[...] = jnp.sum(dhcv, axis=0, keepdims=True)
        for k in range(CONF_WIDTH):
            dw_ref[pl.ds(k, 1), :] = jnp.sum(dw_acc[k], axis=0, keepdims=True)

    return pl.pallas_call(
        body, grid=(nb,),
        in_specs=[_cols(s, tc), _cols(s, tc, nb), _cols(CONF_WIDTH, tc), _cols(s, tc)],
        out_specs=[pl.BlockSpec((2, s, tc), lambda i: (0, 0, i)), _cols(2, tc), _cols(CONF_WIDTH, tc), _cols(1, tc)],
        out_shape=[_sds((2, s, d), BF16), _sds((2, d), F32), _sds((CONF_WIDTH, d), F32), _sds((1, d), F32)],
        scratch_shapes=[pltpu.VMEM((8, CONV_PAD + s, tc), F32), pltpu.VMEM((8, CONV_PAD + s, tc), F32),
                        pltpu.VMEM((CONF_WIDTH + 1, 8, tc), F32), pltpu.VMEM((s, tc), F32)],
        compiler_params=_params(("parallel",)), name="conformer_glu_conv_bwd")(u, u, dw_w, dhc)


def conformer_norm_swish(hc, g, b, tm=512):
    s, d = hc.shape
    tm = _tile(s, tm)

    def body(h_ref, g_ref, b_ref, o_ref):
        n, _, _ = _layer_norm_rows(h_ref[...], g_ref[...], b_ref[...])
        o_ref[...] = (n * jax.nn.sigmoid(n)).astype(BF16)

    return pl.pallas_call(
        body, grid=(s // tm,), in_specs=[_rows(tm, d), _fix((1, d)), _fix((1, d))], out_specs=_rows(tm, d),
        out_shape=_sds((s, d), BF16), compiler_params=_params(("parallel",)), name="conformer_norm_swish")(hc, g, b)


def conformer_norm_swish_bwd(hc, g, b, ds, tm=512):
    s, d = hc.shape
    tm = _tile(s, tm)

    def body(h_ref, g_ref, b_ref, ds_ref, dh_ref, dg_ref, db_ref):
        n, nh, rstd = _layer_norm_rows(h_ref[...], g_ref[...], b_ref[...])
        sg = jax.nn.sigmoid(n)
        dn = ds_ref[...] * (sg * (1.0 + n * (1.0 - sg)))
        dnh = dn * g_ref[...]
        m1 = jnp.mean(dnh, axis=-1, keepdims=True)
        m2 = jnp.mean(dnh * nh, axis=-1, keepdims=True)
        dh_ref[...] = rstd * (dnh - m1 - nh * m2)
        _accumulate(dg_ref, jnp.sum(dn * nh, axis=0, keepdims=True))
        _accumulate(db_ref, jnp.sum(dn, axis=0, keepdims=True))

    return pl.pallas_call(
        body, grid=(s // tm,), in_specs=[_rows(tm, d), _fix((1, d)), _fix((1, d)), _rows(tm, d)],
        out_specs=[_rows(tm, d), _fix((1, d)), _fix((1, d))],
        out_shape=[_sds((s, d), F32), _sds((1, d), F32), _sds((1, d), F32)],
        compiler_params=_params(("arbitrary",)), name="conformer_norm_swish_bwd")(hc, g, b, ds)


def _swap_halves(x):
    lane = lax.broadcasted_iota(jnp.int32, x.shape, 1)
    return jnp.where(lane < QK_ROPE // 2, pltpu.roll(x, 128 - QK_ROPE // 2, 1), pltpu.roll(x, QK_ROPE // 2, 1))


def _rope(x, cf, sf):
    return x * cf + _swap_halves(x) * sf


def _unrope(dx, cf, sf):
    return dx * cf - _swap_halves(dx) * sf


def _rms_rows(x, g):
    r = lax.rsqrt(jnp.mean(x * x, axis=-1, keepdims=True) + RMS_EPS)
    return x * r, r


def mla_latents(t, g_q, g_kv, cf, sf, tm=512):
    s = t.shape[0]
    tm = _tile(s, tm)

    def body(t_ref, gq_ref, gkv_ref, cf_ref, sf_ref, cq_ref, ckv_ref, kpe_ref):
        xq, _ = _rms_rows(t_ref[:, 0:Q_LORA], gq_ref[...])
        cq_ref[...] = (xq * gq_ref[...]).astype(BF16)
        xkv, _ = _rms_rows(t_ref[:, Q_LORA:Q_LORA + KV_LORA], gkv_ref[...])
        ckv_ref[...] = (xkv * gkv_ref[...]).astype(BF16)
        kpe_ref[...] = _rope(t_ref[:, Q_LORA + KV_LORA:], cf_ref[...], sf_ref[...]).astype(BF16)

    w = Q_LORA + KV_LORA + 128
    return pl.pallas_call(
        body, grid=(s // tm,),
        in_specs=[_rows(tm, w), _fix((1, Q_LORA)), _fix((1, KV_LORA)), _rows(tm, 128), _rows(tm, 128)],
        out_specs=[_rows(tm, Q_LORA), _rows(tm, KV_LORA), _rows(tm, 128)],
        out_shape=[_sds((s, Q_LORA), BF16), _sds((s, KV_LORA), BF16), _sds((s, 128), BF16)],
        compiler_params=_params(("parallel",)), name="mla_latents")(t, g_q, g_kv, cf, sf)


def mla_latents_bwd(t, g_q, g_kv, cf, sf, dcq, dckv, dkpe, tm=512):
    s = t.shape[0]
    tm = _tile(s, tm)
    w = Q_LORA + KV_LORA + 128

    def rms_bwd(x, g, dy):
        xh, r = _rms_rows(x, g)
        dxh = dy * g
        return r * (dxh - xh * jnp.mean(dxh * xh, axis=-1, keepdims=True)), jnp.sum(dy * xh, axis=0, keepdims=True)

    def body(t_ref, gq_ref, gkv_ref, cf_ref, sf_ref, dcq_ref, dckv_ref, dkpe_ref, dt_ref, dgq_ref, dgkv_ref):
        dxq, dgq = rms_bwd(t_ref[:, 0:Q_LORA], gq_ref[...], dcq_ref[...])
        dxkv, dgkv = rms_bwd(t_ref[:, Q_LORA:Q_LORA + KV_LORA], gkv_ref[...], dckv_ref[...])
        dt_ref[:, 0:Q_LORA] = dxq.astype(BF16)
        dt_ref[:, Q_LORA:Q_LORA + KV_LORA] = dxkv.astype(BF16)
        dt_ref[:, Q_LORA + KV_LORA:] = _unrope(dkpe_ref[...], cf_ref[...], sf_ref[...]).astype(BF16)
        _accumulate(dgq_ref, dgq)
        _accumulate(dgkv_ref, dgkv)

    return pl.pallas_call(
        body, grid=(s // tm,),
        in_specs=[_rows(tm, w), _fix((1, Q_LORA)), _fix((1, KV_LORA)), _rows(tm, 128), _rows(tm, 128),
                  _rows(tm, Q_LORA), _rows(tm, KV_LORA), _rows(tm, 128)],
        out_specs=[_rows(tm, w), _fix((1, Q_LORA)), _fix((1, KV_LORA))],
        out_shape=[_sds((s, w), BF16), _sds((1, Q_LORA), F32), _sds((1, KV_LORA), F32)],
        compiler_params=_params(("arbitrary",)), name="mla_latents_bwd")(t, g_q, g_kv, cf, sf, dcq, dckv, dkpe)


def mla_queries(cq, w_uq, cf, sf, tm=2048):
    s = cq.shape[0]
    tm = _tile(s, tm)

    def epi(acc, e, o):
        o[0][:, 0:QK_NOPE] = acc[:, 0:QK_NOPE].astype(BF16)
        o[0][:, QK_NOPE:] = _rope(acc[:, QK_NOPE:], e[0][...], e[1][...]).astype(BF16)

    return mm_nn("mla_queries", cq, w_uq, tm, HEAD_PAD, Q_LORA, epi, [_sds((s, N_HEADS * HEAD_PAD), BF16)],
                 [_ij(tm, HEAD_PAD)], [cf, sf], [_i0(tm, 128), _i0(tm, 128)])[0]


def mla_keys(ckv, w_uk, kpe, tm=2048):
    s = ckv.shape[0]
    tm = _tile(s, tm)

    def epi(acc, e, o):
        o[0][:, 0:QK_NOPE] = acc.astype(BF16)
        o[0][:, QK_NOPE:] = e[0][...]

    return mm_nn("mla_keys", ckv, w_uk, tm, QK_NOPE, KV_LORA, epi, [_sds((s, N_HEADS * HEAD_PAD), BF16)],
                 [_ij(tm, HEAD_PAD)], [kpe], [_i0(tm, 128)])[0]


def _masked_scores(q, k, tq, kv):
    sc = lax.dot_general(q, k, NT, preferred_element_type=F32) * ATTN_SCALE
    row = lax.broadcasted_iota(jnp.int32, (tq, tq), 0)
    col = lax.broadcasted_iota(jnp.int32, (tq, tq), 1)
    ok = lax.shift_right_logical(col, CHUNK_SHIFT) <= lax.shift_right_logical(row, CHUNK_SHIFT)
    own = jnp.where(ok, sc[:, kv - tq:], -1e30)
    return own if kv == tq else jnp.concatenate([sc[:, :kv - tq], own], axis=1)


def attention(q, k, v, tq=512):
    s = q.shape[0]
    tq = _tile(s, tq)
    nq = s // tq

    def body(q_ref, k_ref, v_ref, o_ref):
        for qi in range(nq):
            kv = (qi + 1) * tq
            sc = _masked_scores(q_ref[pl.ds(qi * tq, tq), :], k_ref[pl.ds(0, kv), :], tq, kv)
            p = jnp.exp(sc - jnp.max(sc, axis=-1, keepdims=True))
            o = lax.dot_general(p.astype(BF16), v_ref[pl.ds(0, kv), :], NN, preferred_element_type=F32)
            o_ref[pl.ds(qi * tq, tq), :] = (o / jnp.sum(p, axis=-1, keepdims=True)).astype(BF16)

    hq = pl.BlockSpec((s, HEAD_PAD), lambda h: (0, h))
    hv = pl.BlockSpec((s, V_HEAD), lambda h: (0, h))
    return pl.pallas_call(
        body, grid=(N_HEADS,), in_specs=[hq, hq, hv], out_specs=hv, out_shape=_sds((s, N_HEADS * V_HEAD), BF16),
        compiler_params=_params(("parallel",)), name="attention")(q, k, v)


def attention_bwd(q, k, v, do, tq=512):
    s = q.shape[0]
    tq = _tile(s, tq)
    nq = s // tq

    def body(q_ref, k_ref, v_ref, do_ref, dq_ref, dk_ref, dv_ref, dk_acc, dv_acc):
        dk_acc[...] = jnp.zeros_like(dk_acc)
        dv_acc[...] = jnp.zeros_like(dv_acc)
        for qi in range(nq):
            kv = (qi + 1) * tq
            qt = q_ref[pl.ds(qi * tq, tq), :]
            kt = k_ref[pl.ds(0, kv), :]
            dot = do_ref[pl.ds(qi * tq, tq), :]
            sc = _masked_scores(qt, kt, tq, kv)
            p = jnp.exp(sc - jnp.max(sc, axis=-1, keepdims=True))
            p = p / jnp.sum(p, axis=-1, keepdims=True)
            dp = lax.dot_general(dot, v_ref[pl.ds(0, kv), :], NT, preferred_element_type=F32)
            delta = jnp.sum(p * dp, axis=-1, keepdims=True)
            ds = (p * (dp - delta) * ATTN_SCALE).astype(BF16)
            dq_ref[pl.ds(qi * tq, tq), :] = lax.dot_general(ds, kt, NN, preferred_element_type=F32).astype(BF16)
            dk_acc[pl.ds(0, kv), :] += lax.dot_general(ds, qt, TN, preferred_element_type=F32)
            dv_acc[pl.ds(0, kv), :] += lax.dot_general(p.astype(BF16), dot, TN, preferred_element_type=F32)
        dk_ref[...] = dk_acc[...].astype(BF16)
        dv_ref[...] = dv_acc[...].astype(BF16)

    hq = pl.BlockSpec((s, HEAD_PAD), lambda h: (0, h))
    hv = pl.BlockSpec((s, V_HEAD), lambda h: (0, h))
    return pl.pallas_call(
        body, grid=(N_HEADS,), in_specs=[hq, hq, hv, hv], out_specs=[hq, hq, hv],
        out_shape=[_sds((s, N_HEADS * HEAD_PAD), BF16), _sds((s, N_HEADS * HEAD_PAD), BF16),
                   _sds((s, N_HEADS * V_HEAD), BF16)],
        scratch_shapes=[pltpu.VMEM((s, HEAD_PAD), F32), pltpu.VMEM((s, V_HEAD), F32)],
        compiler_params=_params(("parallel",)), name="attention_bwd")(q, k, v, do)


def mla_unrope_grads(dq, dk, cf, sf, tm=512):
    s = dq.shape[0]
    tm = _tile(s, tm)

    def body(dq_ref, dk_ref, cf_ref, sf_ref, dql_ref, dkn_ref, dkpe_ref):
        cfv, sfv = cf_ref[...], sf_ref[...]
        dkpe = jnp.zeros((tm, 128), F32)
        for h in range(N_HEADS):
            lo = h * HEAD_PAD
            dql_ref[:, lo:lo + QK_NOPE] = dq_ref[:, lo:lo + QK_NOPE]
            dql_ref[:, lo + QK_NOPE:lo + HEAD_PAD] = _unrope(
                dq_ref[:, lo + QK_NOPE:lo + HEAD_PAD].astype(F32), cfv, sfv).astype(BF16)
            dkn_ref[:, h * QK_NOPE:(h + 1) * QK_NOPE] = dk_ref[:, lo:lo + QK_NOPE]
            dkpe = dkpe + dk_ref[:, lo + QK_NOPE:lo + HEAD_PAD].astype(F32)
        dkpe_ref[...] = dkpe

    wq = N_HEADS * HEAD_PAD
    return pl.pallas_call(
        body, grid=(s // tm,), in_specs=[_rows(tm, wq), _rows(tm, wq), _rows(tm, 128), _rows(tm, 128)],
        out_specs=[_rows(tm, wq), _rows(tm, N_HEADS * QK_NOPE), _rows(tm, 128)],
        out_shape=[_sds((s, wq), BF16), _sds((s, N_HEADS * QK_NOPE), BF16), _sds((s, 128), F32)],
        compiler_params=_params(("parallel",)), name="mla_unrope_grads")(dq, dk, cf, sf)


ANY = pl.BlockSpec(memory_space=pl.ANY)
GATHER_ID = 1
CHIP_EXCHANGE_ID = 2
PAIR_ID = 3
ALL_ID = 4


def _nbytes(a):
    return a.size * a.dtype.itemsize


def _copy_cost(operand_bytes, sent_fraction):
    sent = int(operand_bytes * sent_fraction)
    return pl.CostEstimate(flops=0, transcendentals=0, bytes_accessed=2 * sent, remote_bytes_transferred=sent)


def _handshake(peers):
    barrier = pltpu.get_barrier_semaphore()
    for peer in peers:
        pl.semaphore_signal(barrier, inc=1, device_id=peer, device_id_type=MESH)
    pl.semaphore_wait(barrier, len(peers))


def _place():
    x, y, c = lax.axis_index("x"), lax.axis_index("y"), lax.axis_index("c")
    chips = [(1 - x, y), (x, 1 - y), (1 - x, 1 - y)]
    return x, y, c, chips


def _half(ref, hc, axis=0):
    n = ref.shape[axis] // 2
    idx = (slice(None),) * axis + (pl.ds(hc * n, n),)
    return ref.at[idx]


def gather_shards(name, tensors, by_columns=()):
    nt = len(tensors)

    def body(*refs):
        a, g = refs[:nt], refs[nt:2 * nt]
        send, recv = refs[2 * nt:]
        x, y, c, _ = _place()
        q = 2 * x + y
        sib, xn, yn = (x, y, 1 - c), (1 - x, y, c), (x, 1 - y, c)
        q_xn, q_yn, q_diag = 2 * (1 - x) + y, 2 * x + 1 - y, 2 * (1 - x) + 1 - y
        _handshake([sib, xn, yn])

        def whole(t, p):
            if t in by_columns:
                n = a[t].shape[1]
                return g[t].at[:, pl.ds(p * n, n)]
            return g[t].at[p]

        def part(t, p, hc, quarter=None):
            rows = a[t].shape[0]
            if quarter is None:
                return whole(t, p).at[pl.ds(hc * (rows // 2), rows // 2)]
            return whole(t, p).at[pl.ds(hc * (rows // 2) + quarter * (rows // 4), rows // 4)]

        def rc(t, k, src, dst, to):
            return pltpu.make_async_remote_copy(src_ref=src, dst_ref=dst, send_sem=send.at[t, k], recv_sem=recv.at[t, k],
                                                device_id=to, device_id_type=MESH)

        sent = []

        def go(cp):
            cp.start()
            sent.append(cp)

        def landed(t, k, piece, frm):
            rc(t, k, piece, piece, frm).wait_recv()
            return piece

        for t in range(nt):
            go(rc(t, 8, a[t], whole(t, q), sib))
            mine = _half(a[t], c)
            go(rc(t, 0, mine, part(t, q, c), xn))
            go(rc(t, 1, mine, part(t, q, c), yn))
        for t in range(nt):
            from_y = landed(t, 1, part(t, q_yn, c), yn)
            go(rc(t, 2, part(t, q_yn, c, 0), part(t, q_yn, c, 0), xn))
            go(rc(t, 5, from_y, from_y, sib))
            from_x = landed(t, 0, part(t, q_xn, c), xn)
            go(rc(t, 3, part(t, q_xn, c, 1), part(t, q_xn, c, 1), yn))
            go(rc(t, 4, from_x, from_x, sib))
        for t in range(nt):
            for k, frm in ((2, xn), (3, yn)):
                piece = landed(t, k, part(t, q_diag, c, k - 2), frm)
                go(rc(t, 4 + k, piece, piece, sib))
        for t in range(nt):
            landed(t, 4, part(t, q_xn, 1 - c), sib)
            landed(t, 5, part(t, q_yn, 1 - c), sib)
            landed(t, 6, part(t, q_diag, 1 - c, 0), sib)
            landed(t, 7, part(t, q_diag, 1 - c, 1), sib)
            landed(t, 8, whole(t, q), sib)
        for cp in sent:
            cp.wait_send()

    return pl.kernel(
        body, name=name,
        out_type=[_sds((a.shape[0], N_CHIPS * a.shape[1]) if t in by_columns else (N_CHIPS,) + a.shape, a.dtype)
                  for t, a in enumerate(tensors)],
        mesh=plsc.ScalarSubcoreMesh(axis_name="sequencer", num_cores=1),
        scratch_types=[pltpu.SemaphoreType.DMA((nt, 9)), pltpu.SemaphoreType.DMA((nt, 9))],
        cost_estimate=_copy_cost(sum(_nbytes(a) for a in tensors), 4),
        compiler_params=pltpu.CompilerParams(collective_id=GATHER_ID))(*tensors)


def pair_exchange(name, grads, on_sequencer):
    nt = len(grads)

    def body(*refs):
        g, theirs = refs[:nt], refs[nt:2 * nt]
        send, recv = refs[2 * nt:]
        x, y, c, _ = _place()
        if on_sequencer:
            _handshake([(x, y, 1 - c)])
        cps = []
        for t in range(nt):
            cp = pltpu.make_async_remote_copy(src_ref=_half(g[t], 1 - c, 1), dst_ref=theirs[t], send_sem=send.at[t],
                                              recv_sem=recv.at[t], device_id=(x, y, 1 - c), device_id_type=MESH)
            cp.start()
            cps.append(cp)
        for cp in cps:
            cp.wait()

    if not on_sequencer:
        return pl.pallas_call(
            body, in_specs=[ANY] * nt, out_specs=[ANY] * nt,
            out_shape=[_sds((N_CHIPS, a.shape[1] // 2, a.shape[2]), a.dtype) for a in grads],
            scratch_shapes=[pltpu.SemaphoreType.DMA((nt,)), pltpu.SemaphoreType.DMA((nt,))],
            name=name)(*grads)
    return pl.kernel(
        body, name=name, out_type=[_sds((N_CHIPS, a.shape[1] // 2, a.shape[2]), a.dtype) for a in grads],
        mesh=plsc.ScalarSubcoreMesh(axis_name="sequencer", num_cores=1),
        scratch_types=[pltpu.SemaphoreType.DMA((nt,)), pltpu.SemaphoreType.DMA((nt,))],
        cost_estimate=_copy_cost(sum(_nbytes(a) for a in grads), 0.5),
        compiler_params=pltpu.CompilerParams(collective_id=PAIR_ID))(*grads)


def chip_exchange(name, parts):
    nt = len(parts)

    def body(*refs):
        a, r = refs[:nt], refs[nt:2 * nt]
        send, recv = refs[2 * nt:]
        x, y, c, chips = _place()
        _handshake([(*chip, c) for chip in chips])
        cps = []
        for t in range(nt):
            for j, chip in enumerate(chips):
                cp = pltpu.make_async_remote_copy(
                    src_ref=a[t].at[2 * chip[0] + chip[1]], dst_ref=r[t].at[j], send_sem=send.at[t, j],
                    recv_sem=recv.at[t, j], device_id=(*chip, c), device_id_type=MESH)
                cp.start()
                cps.append(cp)
        for cp in cps:
            cp.wait()

    return pl.kernel(
        body, name=name, out_type=[_sds((N_CHIPS - 1,) + a.shape[1:], a.dtype) for a in parts],
        mesh=plsc.ScalarSubcoreMesh(axis_name="sequencer", num_cores=1),
        scratch_types=[pltpu.SemaphoreType.DMA((nt, 3)), pltpu.SemaphoreType.DMA((nt, 3))],
        cost_estimate=_copy_cost(sum(_nbytes(a) for a in parts), 0.75),
        compiler_params=pltpu.CompilerParams(collective_id=CHIP_EXCHANGE_ID))(*parts)


def pair_share(name, halves):
    nt = len(halves)

    def body(*refs):
        h, other = refs[:nt], refs[nt:2 * nt]
        send, recv = refs[2 * nt:]
        x, y, c, _ = _place()
        _handshake([(x, y, 1 - c)])
        cps = []
        for t in range(nt):
            cp = pltpu.make_async_remote_copy(src_ref=h[t], dst_ref=other[t], send_sem=send.at[t], recv_sem=recv.at[t],
                                              device_id=(x, y, 1 - c), device_id_type=MESH)
            cp.start()
            cps.append(cp)
        for cp in cps:
            cp.wait()

    return pl.kernel(
        body, name=name, out_type=[_sds(a.shape, a.dtype) for a in halves],
        mesh=plsc.ScalarSubcoreMesh(axis_name="sequencer", num_cores=1),
        scratch_types=[pltpu.SemaphoreType.DMA((nt,)), pltpu.SemaphoreType.DMA((nt,))],
        cost_estimate=_copy_cost(sum(_nbytes(a) for a in halves), 1),
        compiler_params=pltpu.CompilerParams(collective_id=PAIR_ID))(*halves)


def pack_rows(name, parts, rows):
    cdim = parts[0].shape[1]
    n = len(parts)
    vm = pl.BlockSpec(memory_space=pltpu.VMEM)

    def pack(*refs):
        p, o_ref = refs[:n], refs[n]
        at = 0
        for ref in p:
            o_ref[pl.ds(at, ref.shape[0]), :] = ref[...]
            at += ref.shape[0]
        o_ref[pl.ds(at, rows - at), :] = jnp.zeros((rows - at, cdim), F32)

    return pl.pallas_call(pack, in_specs=[vm] * n, out_specs=vm, out_shape=_sds((rows, cdim), F32), name=name)(*parts)


def all_reduce_small(parts, rows):
    cdim = parts[0].shape[1]
    vm = pl.BlockSpec(memory_space=pltpu.VMEM)
    mine = pack_rows("small_pack", parts, rows)

    def exchange(mine_ref, buf, send, recv, lsem):
        x, y, c, _ = _place()
        me = 4 * x + 2 * y + c
        peers = [(x ^ (k >> 2), y ^ ((k >> 1) & 1), c ^ (k & 1)) for k in range(1, 8)]
        _handshake(peers)
        own = pltpu.make_async_copy(mine_ref, buf.at[me], lsem)
        own.start()
        cps = []
        for k, to in enumerate(peers):
            cp = pltpu.make_async_remote_copy(src_ref=mine_ref, dst_ref=buf.at[me], send_sem=send.at[k], recv_sem=recv.at[k],
                                              device_id=to, device_id_type=MESH)
            cp.start()
            cps.append(cp)
        for k, (px, py, pc) in enumerate(peers):
            pltpu.make_async_remote_copy(src_ref=mine_ref, dst_ref=buf.at[4 * px + 2 * py + pc], send_sem=send.at[k],
                                         recv_sem=recv.at[k], device_id=(x, y, c), device_id_type=MESH).wait_recv()
        for cp in cps:
            cp.wait_send()
        own.wait()

    landed = pl.kernel(
        exchange, name="small_exchange", out_type=_sds((8, rows, cdim), F32),
        mesh=plsc.ScalarSubcoreMesh(axis_name="sequencer", num_cores=1),
        scratch_types=[pltpu.SemaphoreType.DMA((7,)), pltpu.SemaphoreType.DMA((7,)), pltpu.SemaphoreType.DMA],
        cost_estimate=_copy_cost(rows * cdim * 4, 7),
        compiler_params=pltpu.CompilerParams(collective_id=ALL_ID))(mine)

    def total(buf, o_ref):
        acc = buf[0]
        for d in range(1, 8):
            acc = acc + buf[d]
        o_ref[...] = acc

    return pl.pallas_call(total, in_specs=[vm], out_specs=vm, out_shape=_sds((rows, cdim), F32), name="small_sum")(landed)


def pair_sum(gs, theirs, core, tm=256):
    n = len(gs)
    _, r, c = gs[0].shape
    tm = _tile(r // 2, tm)
    nh = r // 2 // tm

    def body(core_ref, *refs):
        for a_ref, b_ref, o_ref in zip(refs[:n], refs[n:2 * n], refs[2 * n:]):
            o_ref[...] = (a_ref[...].astype(F32) + b_ref[...].astype(F32)).astype(BF16)

    blk = (N_CHIPS, tm, c)
    own = pl.BlockSpec(blk, lambda i, cr: (0, cr[0] * nh + i, 0))
    half = pl.BlockSpec(blk, lambda i, cr: (0, i, 0))
    return pl.pallas_call(
        body, grid_spec=pltpu.PrefetchScalarGridSpec(
            num_scalar_prefetch=1, grid=(nh,), in_specs=[own] * n + [half] * n, out_specs=[half] * n),
        out_shape=[_sds(t.shape, BF16) for t in theirs], compiler_params=_params(("parallel",)),
        name="pair_sum")(core, *gs, *theirs)


def chip_sum(own, landed, chip, stack, layer, layers, tm=256):
    _, r, c = own.shape
    tm = _tile(r, tm)

    def body(chip_ref, own_ref, l_ref, *rest):
        acc = own_ref[...].astype(F32)
        for j in range(N_CHIPS - 1):
            acc = acc + l_ref[j].astype(F32)
        rest[-1][...] = acc

    in_specs = [pl.BlockSpec((None, tm, c), lambda i, qr: (qr[0], i, 0)),
                pl.BlockSpec((N_CHIPS - 1, tm, c), lambda i, qr: (0, i, 0))]
    args = [chip, own, landed]
    if stack is not None:
        in_specs.append(ANY)
        args.append(stack)
    return pl.pallas_call(
        body, grid_spec=pltpu.PrefetchScalarGridSpec(
            num_scalar_prefetch=1, grid=(r // tm,), in_specs=in_specs,
            out_specs=pl.BlockSpec((None, tm, c), lambda i, qr: (layer, i, 0))),
        out_shape=_sds((layers, r, c), F32), input_output_aliases={3: 0} if stack is not None else {},
        compiler_params=_params(("parallel",)), name="chip_sum")(*args)


def _adamw_math(w, g, m, v):
    bc1 = 1.0 - ADAM_B1 ** ADAM_STEP
    bc2 = 1.0 - ADAM_B2 ** ADAM_STEP
    nm = ADAM_B1 * m + (1.0 - ADAM_B1) * g
    nv = ADAM_B2 * v + (1.0 - ADAM_B2) * (g * g)
    return -ADAM_LR * ((nm / bc1) / (jnp.sqrt(nv / bc2) + ADAM_EPS) + ADAM_WD * w), nm, nv


def vector_update(red, chip, ws, ms, vs, where):
    n = len(ws)
    dd = red.shape[1]

    def body(chip_ref, red_ref, *refs):
        w_r, m_r, v_r = refs[0:n], refs[n:2 * n], refs[2 * n:3 * n]
        g_o, d_o, m_o, v_o = (refs[(3 + k) * n:(4 + k) * n] for k in range(4))
        q = chip_ref[0]

        def chip_block(val, width):
            out = val[:, 0:width]
            for p in range(1, val.shape[1] // width):
                out = jnp.where(q == p, val[:, p * width:(p + 1) * width], out)
            return out

        for k in range(n):
            for idx, r0, nr, cols in where[k]:
                width = w_r[k].shape[-1]
                if cols == "chip" and width * N_CHIPS != dd:
                    g = chip_block(jnp.concatenate([red_ref[pl.ds(r0 + j, 1), :] for j in range(nr)], axis=1), width)
                else:
                    g = red_ref[pl.ds(r0, nr), :]
                    g = chip_block(g, width) if cols == "chip" else g if cols == "all" else g[:, 0:cols]
                delta, nm, nv = _adamw_math(w_r[k][idx], g, m_r[k][idx], v_r[k][idx])
                g_o[k][idx] = g
                d_o[k][idx] = delta
                m_o[k][idx] = nm
                v_o[k][idx] = nv

    vm = pl.BlockSpec(memory_space=pltpu.VMEM)
    outs = pl.pallas_call(
        body, in_specs=[pl.BlockSpec(memory_space=pltpu.SMEM), vm] + [vm] * (3 * n), out_specs=[vm] * (4 * n),
        out_shape=[_sds(w.shape, F32) for w in ws] * 4, name="vector_update")(chip, red, *ws, *ms, *vs)
    return [outs[k * n:(k + 1) * n] for k in range(4)]


def adamw_joined(w, m, v, g_mine, g_theirs, core, tm=512):
    nl, r, c = w.shape
    tm = _tile(r // 2, tm)
    nh = r // 2 // tm

    def body(core_ref, w_ref, m_ref, v_ref, gm_ref, gt_ref, g_ref, d_ref, nm_ref, nv_ref):
        mine = (pl.program_id(1) // nh) == core_ref[0]
        gv = jnp.where(mine, gm_ref[...], gt_ref[...])
        g_ref[...] = gv
        d_ref[...], nm_ref[...], nv_ref[...] = _adamw_math(w_ref[...], gv, m_ref[...], v_ref[...])

    full = pl.BlockSpec((None, tm, c), lambda l, i, cr: (l, i, 0))
    mine = pl.BlockSpec((None, tm, c), lambda l, i, cr: (l, jnp.where(i // nh == cr[0], i % nh, 0), 0))
    theirs = pl.BlockSpec((None, tm, c), lambda l, i, cr: (l, jnp.where(i // nh == cr[0], 0, i % nh), 0))
    return pl.pallas_call(
        body, grid_spec=pltpu.PrefetchScalarGridSpec(
            num_scalar_prefetch=1, grid=(nl, r // tm), in_specs=[full, full, full, mine, theirs], out_specs=[full] * 4),
        out_shape=[_sds((nl, r, c), F32)] * 4, compiler_params=_params(("parallel", "parallel")),
        name="adamw_joined")(core, w, m, v, g_mine, g_theirs)


WEIGHTS = ['sc_w_in', 'sc_conv_w', 'sc_w_out', 'mla_w_dq', 'mla_g_q', 'mla_w_uq', 'mla_w_dkv', 'mla_g_kv', 'mla_w_uk',
           'mla_w_uv', 'mla_w_o', 'cf_w_pw1', 'cf_b_pw1', 'cf_dw_w', 'cf_dw_b', 'cf_norm_g', 'cf_norm_b', 'cf_w_pw2',
           'cf_b_pw2', 'ff_w1', 'ff_w2', 'ln_mix_g', 'ln_mix_b', 'ln_ff_g', 'ln_ff_b']
ARGS = ['x'] + WEIGHTS + ['loss_target'] + ['m_' + n for n in WEIGHTS] + ['v_' + n for n in WEIGHTS]


def _sq_relu(h):
    r = jnp.maximum(h, jnp.zeros_like(h))
    return r * r


def _mlp_forward(i, x, xb, w1, w2, g, b):
    hb = mm_plain_nn(f"mlp{i}_up", xb, w1, BF16, tm=2048, tn=1024)
    y, yb, xh, rstd = mm_residual_ln(f"mlp{i}_down_ln", hb, w2, x, g, b, tk=4096, a_fn=_sq_relu)
    return (y, yb), dict(xb=xb, hb=hb, xh=xh, rstd=rstd, g=g)


def _mlp_backward(i, dr, drb, sv, w1, w2, dw1, dw2, reduce_after, mixer_ln):
    s = dr.shape[0]
    tm, tn = _tile(s, 1024), 1024

    def epi(acc, e, o):
        o[0][...] = (acc * (2.0 * jnp.maximum(e[0][...].astype(F32), 0.0))).astype(BF16)

    dhb = mm_nt(f"mlp{i}_down_bwd", drb, w2, s, tm, tn, 1024, epi, [_sds((s, w2.k), BF16)], [_ij(tm, tn)],
                [sv["hb"]], [_ij(tm, tn)])[0]
    g_w2 = mm_tn(f"mlp{i}_dw2", sv["hb"], drb, dw2, s, 1024, 1024, a_fn=_sq_relu)
    g_w1 = mm_tn(f"mlp{i}_dw1", sv["xb"], dhb, dw1, s, 1024, 1024)
    dhb = reduce_after(dhb, {f"w1_{i}": g_w1, f"w2_{i}": g_w2})
    return mm_nt_ln_backward(f"mlp{i}_up_bwd", dhb, w1, dr, *mixer_ln, tk=2048)


def kernel(x, sc_w_in, sc_conv_w, sc_w_out, mla_w_dq, mla_g_q, mla_w_uq, mla_w_dkv, mla_g_kv, mla_w_uk, mla_w_uv, mla_w_o, cf_w_pw1, cf_b_pw1, cf_dw_w, cf_dw_b, cf_norm_g, cf_norm_b, cf_w_pw2, cf_b_pw2, ff_w1, ff_w2, ln_mix_g, ln_mix_b, ln_ff_g, ln_ff_b, loss_target, m_sc_w_in, m_sc_conv_w, m_sc_w_out, m_mla_w_dq, m_mla_g_q, m_mla_w_uq, m_mla_w_dkv, m_mla_g_kv, m_mla_w_uk, m_mla_w_uv, m_mla_w_o, m_cf_w_pw1, m_cf_b_pw1, m_cf_dw_w, m_cf_dw_b, m_cf_norm_g, m_cf_norm_b, m_cf_w_pw2, m_cf_b_pw2, m_ff_w1, m_ff_w2, m_ln_mix_g, m_ln_mix_b, m_ln_ff_g, m_ln_ff_b, v_sc_w_in, v_sc_conv_w, v_sc_w_out, v_mla_w_dq, v_mla_g_q, v_mla_w_uq, v_mla_w_dkv, v_mla_g_kv, v_mla_w_uk, v_mla_w_uv, v_mla_w_o, v_cf_w_pw1, v_cf_b_pw1, v_cf_dw_w, v_cf_dw_b, v_cf_norm_g, v_cf_norm_b, v_cf_w_pw2, v_cf_b_pw2, v_ff_w1, v_ff_w2, v_ln_mix_g, v_ln_mix_b, v_ln_ff_g, v_ln_ff_b):
    given = dict(zip(ARGS, (x, sc_w_in, sc_conv_w, sc_w_out, mla_w_dq, mla_g_q, mla_w_uq, mla_w_dkv, mla_g_kv, mla_w_uk, mla_w_uv, mla_w_o, cf_w_pw1, cf_b_pw1, cf_dw_w, cf_dw_b, cf_norm_g, cf_norm_b, cf_w_pw2, cf_b_pw2, ff_w1, ff_w2, ln_mix_g, ln_mix_b, ln_ff_g, ln_ff_b, loss_target, m_sc_w_in, m_sc_conv_w, m_sc_w_out, m_mla_w_dq, m_mla_g_q, m_mla_w_uq, m_mla_w_dkv, m_mla_g_kv, m_mla_w_uk, m_mla_w_uv, m_mla_w_o, m_cf_w_pw1, m_cf_b_pw1, m_cf_dw_w, m_cf_dw_b, m_cf_norm_g, m_cf_norm_b, m_cf_w_pw2, m_cf_b_pw2, m_ff_w1, m_ff_w2, m_ln_mix_g, m_ln_mix_b, m_ln_ff_g, m_ln_ff_b, v_sc_w_in, v_sc_conv_w, v_sc_w_out, v_mla_w_dq, v_mla_g_q, v_mla_w_uq, v_mla_w_dkv, v_mla_g_kv, v_mla_w_uk, v_mla_w_uv, v_mla_w_o, v_cf_w_pw1, v_cf_b_pw1, v_cf_dw_w, v_cf_dw_b, v_cf_norm_g, v_cf_norm_b, v_cf_w_pw2, v_cf_b_pw2, v_ff_w1, v_ff_w2, v_ln_mix_g, v_ln_mix_b, v_ln_ff_g, v_ln_ff_b)))
    s, d = x.shape[1], x.shape[2]
    d_ff = 4 * d
    dq4 = d // N_CHIPS
    xq = lax.axis_index("x") * 2 + lax.axis_index("y")

    w_dkv_pad = jnp.pad(mla_w_dkv[0], ((0, 0), (0, 128 - QK_ROPE)))
    w_uq_pad = jnp.pad(mla_w_uq[0].reshape(Q_LORA, 2, QK_NOPE + QK_ROPE), ((0, 0), (0, 0), (0, HEAD_PAD - QK_NOPE - QK_ROPE)))
    small = pack_rows("vector_weights_pack", [
        sc_conv_w.reshape(2 * SC_WIDTH, dq4), cf_b_pw1.reshape(2, dq4), cf_dw_w[0], cf_dw_b, cf_norm_g, cf_norm_b,
        cf_b_pw2], 64)
    mlp_w = lambda i: [ff_w1[i].astype(BF16), ff_w2[i].astype(BF16)]
    g_in, g_out, g_w1, g_w2 = [None] * 2, [None] * 2, [None] * DEPTH, [None] * DEPTH
    g_in[0], g_out[0], g_small = gather_shards(
        "gather_mixer0", [sc_w_in[0].astype(BF16), sc_w_out[0].astype(BF16), small], by_columns=(0,))
    (g_w1[0],) = gather_shards("gather_up0", [ff_w1[0].astype(BF16)], by_columns=(0,))
    (g_w2[0],) = gather_shards("gather_down0", [ff_w2[0].astype(BF16)])
    g_dqkv, g_uq, g_uk, g_uv, g_o = gather_shards("gather_mixer1", [
        jnp.concatenate([mla_w_dq[0], w_dkv_pad], axis=1).astype(BF16),
        w_uq_pad.reshape(Q_LORA, 2 * HEAD_PAD).astype(BF16),
        mla_w_uk.reshape(KV_LORA // N_CHIPS, N_HEADS * QK_NOPE).astype(BF16),
        mla_w_uv.reshape(KV_LORA // N_CHIPS, N_HEADS * V_HEAD).astype(BF16), mla_w_o[0].astype(BF16)], by_columns=(1,))
    g_w1[1], g_w2[1] = gather_shards("gather_mlp1", mlp_w(1), by_columns=(0,))
    g_pw1, g_pw2, g_w1[2], g_w2[2] = gather_shards(
        "gather_layer2", [cf_w_pw1[0].astype(BF16), cf_w_pw2[0].astype(BF16)] + mlp_w(2), by_columns=(0, 2))
    g_in[1], g_out[1], g_w1[3], g_w2[3] = gather_shards(
        "gather_layer3", [sc_w_in[1].astype(BF16), sc_w_out[1].astype(BF16)] + mlp_w(3), by_columns=(0, 2))

    wd_t = Q_LORA + KV_LORA + 128
    w_in = [Stk("full", d, 3 * d, g_in[j]) for j in range(2)]
    w_out = [Stk("row", d, d, g_out[j]) for j in range(2)]
    w_dqkv = Stk("row", d, wd_t, g_dqkv)
    w_uq = Stk("full", Q_LORA, N_HEADS * HEAD_PAD, g_uq)
    w_uk = Stk("row", KV_LORA, N_HEADS * QK_NOPE, g_uk)
    w_uv = Stk("row", KV_LORA, N_HEADS * V_HEAD, g_uv)
    w_o = Stk("row", d, d, g_o)
    w_pw1 = Stk("full", d, 2 * d, g_pw1)
    w_pw2 = Stk("row", d, d, g_pw2)
    w_1 = [Stk("full", d, d_ff, g_w1[i]) for i in range(DEPTH)]
    w_2 = [Stk("row", d_ff, d, g_w2[i]) for i in range(DEPTH)]

    def wide(rows):
        return jnp.swapaxes(rows, 0, 1).reshape(rows.shape[1], d)

    conv_w = wide(g_small[:, 0:6]).reshape(2, SC_WIDTH, d)
    b_pw1 = g_small[:, 6:8].reshape(1, 2 * d)
    dw_w = wide(g_small[:, 8:39])
    dw_b, norm_g, norm_b, b_pw2 = (wide(g_small[:, 39 + k:40 + k]) for k in range(4))

    pos = jnp.arange(s, dtype=F32)
    inv_freq = ROPE_THETA ** (-jnp.arange(0, QK_ROPE, 2, dtype=F32) / QK_ROPE)
    ang = pos[:, None] * inv_freq[None, :]
    cos, sin, zero = jnp.cos(ang), jnp.sin(ang), jnp.zeros((s, 128 - QK_ROPE), F32)
    cf = jnp.concatenate([cos, cos, zero], axis=1)
    sf = jnp.concatenate([-sin, sin, zero], axis=1)

    def row(a, i):
        return a[i:i + 1]

    xs = x.reshape(s, d)
    cur = (xs, xs.astype(BF16))
    tape = []
    for i in range(DEPTH):
        mixer, j = i % 3, i // 3
        xf, xb = cur
        lg, lb = row(ln_mix_g, i), row(ln_mix_b, i)
        if mixer == 0:
            u = mm_plain_nn(f"sc{j}_in", xb, w_in[j], F32, tn=3 * dq4)
            gb = short_conv_gate(u, conv_w[j])
            y, yb, xh, rstd = mm_residual_ln(f"sc{j}_out_ln", gb, w_out[j], xf, lg, lb)
            sv = dict(xb=xb, u=u, gb=gb)
        elif mixer == 1:
            t = mm_plain_nn("mla_down", xb, w_dqkv, F32, tn=wd_t // 2)
            cq, ckv, kpe = mla_latents(t, mla_g_q, mla_g_kv, cf, sf)
            qh = mla_queries(cq, w_uq, cf, sf)
            kh = mla_keys(ckv, w_uk, kpe)
            vh = mm_plain_nn("mla_values", ckv, w_uv, BF16, tk=KV_LORA)
            oh = attention(qh, kh, vh)
            y, yb, xh, rstd = mm_residual_ln("mla_out_ln", oh, w_o, xf, lg, lb)
            sv = dict(xb=xb, t=t, cq=cq, ckv=ckv, qh=qh, kh=kh, vh=vh, oh=oh)
        else:
            u = mm_plain_nn("cf_pw1", xb, w_pw1, F32, bias=b_pw1)
            hc = conformer_glu_conv(u, dw_w, dw_b)
            sb = conformer_norm_swish(hc, norm_g, norm_b)
            y, yb, xh, rstd = mm_residual_ln("cf_pw2_ln", sb, w_pw2, xf, lg, lb, bias=b_pw2)
            sv = dict(xb=xb, u=u, hc=hc, sb=sb)
        sv.update(xh=xh, rstd=rstd, g=lg)
        cur, sv_mlp = _mlp_forward(i, y, yb, w_1[i], w_2[i], row(ln_ff_g, i), row(ln_ff_b, i))
        tape.append((sv, sv_mlp))

    g_ln = {n: [None] * DEPTH for n in ("ln_mix_g", "ln_mix_b", "ln_ff_g", "ln_ff_b")}
    last = tape[DEPTH - 1][1]
    dr, drb, g_ln["ln_ff_g"][DEPTH - 1], g_ln["ln_ff_b"][DEPTH - 1], _, loss_part = loss_ln_backward(
        cur[0], loss_target.reshape(s, d), last["xh"], last["rstd"], last["g"])

    grads = {}
    smalls = {}
    conv_grads = [None, None]
    core = lax.axis_index("c").astype(jnp.int32).reshape(1)
    chip = xq.astype(jnp.int32).reshape(1)
    pairs, landed = {}, {}
    ready, theirs = [], {}

    def hold(xs, others):
        live = [x for x in xs if x is not None]
        out = lax.optimization_barrier((*live, *others))
        rest = iter(out[:len(live)])
        return tuple(None if x is None else next(rest) for x in xs), list(out[len(live):])

    def reduce_after(x, new, early=False):
        out = lax.optimization_barrier((x, *new.values()))
        grads.update(zip(new, out[1:]))
        if early:
            theirs.update(zip(new, pair_exchange(f"pair_exchange_{len(theirs)}", list(out[1:]), True)))
        ready.extend(new)
        return out[0]

    def reduce_layer(i, x):
        late = [n for n in ready if n not in theirs]
        if late:
            theirs.update(zip(late, pair_exchange(f"pair_exchange_layer{i}", [grads[n] for n in late], False)))
        by_shape = {}
        for n in ready:
            by_shape.setdefault(grads[n].shape, []).append(n)
        for names in by_shape.values():
            pairs.update(zip(names, pair_sum([grads[n] for n in names], [theirs[n] for n in names], core)))
        sums = [pairs[n] for n in ready]
        landed.update(zip(ready, chip_exchange(f"chip_exchange_layer{i}", sums)))
        exchanged.append(list(ready))
        ready.clear()
        return hold(x, sums)[0]

    groups = [["in_0", "in_1"], ["out_0", "out_1"], ["dqkv"], ["uq"], ["uk"], ["uv"], ["o"], ["pw1"], ["pw2"],
              [f"w1_{i}" for i in range(DEPTH)], [f"w2_{i}" for i in range(DEPTH)]]
    stacks = [None] * len(groups)
    exchanged = []

    def sum_layer(x, last=False):
        names = exchanged.pop(0)
        if last:
            x, held = hold(x, [landed[n] for n in names])
            landed.update(zip(names, held))
        new = []
        for n in names:
            k = next(k for k, members in enumerate(groups) if n in members)
            stacks[k] = chip_sum(pairs[n], landed[n], chip, stacks[k], groups[k].index(n), len(groups[k]))
            new.append(stacks[k])
        return x if last else hold(x, new)[0]

    for i in reversed(range(DEPTH)):
        mixer, j = i % 3, i // 3
        sv, sv_mlp = tape[i]
        dr, drb, g_ln["ln_mix_g"][i], g_ln["ln_mix_b"][i], dr_sum = _mlp_backward(
            i, dr, drb, sv_mlp, w_1[i], w_2[i], Stk("col", d, d_ff), Stk("row", d_ff, d),
            lambda x_, new: reduce_after(x_, new, early=i > 0), (sv["xh"], sv["rstd"], sv["g"]))
        if i == 0:
            dr, drb = reduce_layer("0_mlp", (dr, drb))

        def to_input(name, a, w, tk, a_spec_fn=None):
            if i == 0:
                spec = None if a_spec_fn is None else (s, a_spec_fn)
                return mm_plain_nt(name, a, w, F32, tn=1024, tk=tk, add=dr, add_scale=ALPHA, a_spec_fn=spec), None
            prev = tape[i - 1][1]
            out = mm_nt_ln_backward(name, a, w, dr, prev["xh"], prev["rstd"], prev["g"], tk=tk, a_spec_fn=a_spec_fn)
            g_ln["ln_ff_g"][i - 1], g_ln["ln_ff_b"][i - 1] = out[2], out[3]
            return out[0], out[1]

        parts_of = lambda tm, tk: pl.BlockSpec((None, tm, tk), lambda i_, j_, k_: (k_, i_, 0))
        if mixer == 0:
            dgate = mm_plain_nt(f"sc{j}_out_bwd", drb, w_out[j], F32)
            dw_out = mm_tn(f"sc{j}_dw_out", sv["gb"], drb, Stk("row", d, d), s, 512, 1024)
            du, conv_grads[j] = short_conv_gate_bwd(sv["u"], conv_w[j], dgate)
            nb = d // 256
            dw_in = mm_tn(
                f"sc{j}_dw_in", sv["xb"], du, Stk("col", d, 3 * d), s, 1024, 256,
                b_spec=pl.BlockSpec((None, s, 256), lambda i_, j_, k_: (j_ // nb, k_, j_ % nb)))
            du = reduce_after(du, {f"in_{j}": dw_in, f"out_{j}": dw_out})
            dr, drb = to_input(f"sc{j}_in_bwd", du, w_in[j], d, parts_of)
        elif mixer == 1:
            do = mm_plain_nt("mla_out_bwd", drb, w_o, BF16)
            g_o = mm_tn("mla_dw_o", sv["oh"], drb, Stk("row", d, d), s, 512, 1024)
            dqh, dkh, dvh = attention_bwd(sv["qh"], sv["kh"], sv["vh"], do)
            dql, dkn, dkpe = mla_unrope_grads(dqh, dkh, cf, sf)
            g_uq = mm_tn("mla_dw_uq", sv["cq"], dql, Stk("col", Q_LORA, N_HEADS * HEAD_PAD), s, Q_LORA, 512)
            dcq = mm_plain_nt("mla_uq_bwd", dql, w_uq, F32, tn=Q_LORA)
            g_uk = mm_tn("mla_dw_uk", sv["ckv"], dkn, Stk("row", KV_LORA, N_HEADS * QK_NOPE), s, KV_LORA, 1024)
            g_uv = mm_tn("mla_dw_uv", sv["ckv"], dvh, Stk("row", KV_LORA, N_HEADS * V_HEAD), s, KV_LORA, 1024)
            dckv = mm_plain_nt("mla_uk_bwd", dkn, w_uk, F32, tn=KV_LORA)
            dckv = mm_plain_nt("mla_uv_bwd", dvh, w_uv, F32, tn=KV_LORA, add=dckv)
            dt, smalls["g_q"], smalls["g_kv"] = mla_latents_bwd(sv["t"], mla_g_q, mla_g_kv, cf, sf, dcq, dckv, dkpe)
            g_dqkv = mm_tn("mla_dw_down", sv["xb"], dt, Stk("row", d, wd_t), s, 512, wd_t)
            dt = reduce_after(dt, {"dqkv": g_dqkv, "uq": g_uq, "uk": g_uk, "uv": g_uv, "o": g_o})
            dr, drb = to_input("mla_down_bwd", dt, w_dqkv, wd_t)
        else:
            dsw = mm_plain_nt("cf_pw2_bwd", drb, w_pw2, F32)
            g_pw2 = mm_tn("cf_dw_pw2", sv["sb"], drb, Stk("row", d, d), s, 512, 1024)
            smalls["b_pw2"] = dr_sum
            dhc, smalls["norm_g"], smalls["norm_b"] = conformer_norm_swish_bwd(sv["hc"], norm_g, norm_b, dsw)
            du, smalls["b_pw1"], smalls["dw_w"], smalls["dw_b"] = conformer_glu_conv_bwd(sv["u"], dw_w, dhc)
            nb = d // 512
            g_pw1 = mm_tn(
                "cf_dw_pw1", sv["xb"], du, Stk("col", d, 2 * d), s, 1024, 512,
                b_spec=pl.BlockSpec((None, s, 512), lambda i_, j_, k_: (j_ // nb, k_, j_ % nb)))
            du = reduce_after(du, {"pw1": g_pw1, "pw2": g_pw2})
            dr, drb = to_input("cf_pw1_bwd", du, w_pw1, d, parts_of)
        if i < DEPTH - 1:
            dr, drb = sum_layer((dr, drb))
        dr, drb = reduce_layer(i, (dr, drb))
    grad_x = sum_layer(sum_layer((dr, None), last=True), last=True)[0].reshape(1, s, d)

    mine = stacks
    other = (pair_share("pair_share_mixers", mine[:9]) + pair_share("pair_share_up", mine[9:10])
             + pair_share("pair_share_down", mine[10:]))

    def padded(get):
        dqkv = jnp.concatenate([get("mla_w_dq")[0], jnp.pad(get("mla_w_dkv")[0], ((0, 0), (0, 128 - QK_ROPE)))], axis=1)
        uq = jnp.pad(get("mla_w_uq")[0].reshape(Q_LORA, 2, QK_NOPE + QK_ROPE),
                     ((0, 0), (0, 0), (0, HEAD_PAD - QK_NOPE - QK_ROPE))).reshape(Q_LORA, 2 * HEAD_PAD)
        return [get("sc_w_in"), get("sc_w_out"), dqkv[None], uq[None],
                get("mla_w_uk").reshape(1, KV_LORA // N_CHIPS, d), get("mla_w_uv").reshape(1, KV_LORA // N_CHIPS, d),
                get("mla_w_o"), get("cf_w_pw1"), get("cf_w_pw2"), get("ff_w1"), get("ff_w2")]

    w_l, m_l, v_l = (padded(lambda n, p=p: given[p + n]) for p in ("", "m_", "v_"))
    res = [adamw_joined(w_l[k], m_l[k], v_l[k], mine[k], other[k], core) for k in range(len(groups))]

    def unpadded(k):
        r_in, r_out, r_dqkv, r_uq, r_uk, r_uv, r_o, r_pw1, r_pw2, r_w1, r_w2 = (r[k] for r in res)
        return {
            "sc_w_in": r_in, "sc_w_out": r_out, "mla_w_dq": r_dqkv[:, :, 0:Q_LORA],
            "mla_w_dkv": r_dqkv[:, :, Q_LORA:Q_LORA + KV_LORA + QK_ROPE],
            "mla_w_uq": r_uq.reshape(1, Q_LORA, 2, HEAD_PAD)[:, :, :, 0:QK_NOPE + QK_ROPE].reshape(mla_w_uq.shape),
            "mla_w_uk": r_uk.reshape(mla_w_uk.shape), "mla_w_uv": r_uv.reshape(mla_w_uv.shape),
            "mla_w_o": r_o, "cf_w_pw1": r_pw1, "cf_w_pw2": r_pw2, "ff_w1": r_w1, "ff_w2": r_w2}

    big_g, big_d, big_m, big_v = (unpadded(k) for k in range(4))

    pad_row = lambda a: jnp.pad(a, ((0, 0), (0, d - a.shape[1])))
    small_parts = ([g for n in ("ln_mix_g", "ln_mix_b", "ln_ff_g", "ln_ff_b") for g in g_ln[n]]
                   + [pad_row(smalls["g_q"]), pad_row(smalls["g_kv"]), conv_grads[0], conv_grads[1],
                      smalls["b_pw1"].reshape(2, d), smalls["dw_w"], smalls["dw_b"], smalls["norm_g"], smalls["norm_b"],
                      smalls["b_pw2"], loss_part])
    red = all_reduce_small(small_parts, 64)
    loss = red[61, 0]

    where = {
        "ln_mix_g": [((), 0, DEPTH, "all")], "ln_mix_b": [((), 4, DEPTH, "all")],
        "ln_ff_g": [((), 8, DEPTH, "all")], "ln_ff_b": [((), 12, DEPTH, "all")],
        "mla_g_q": [((), 16, 1, Q_LORA)], "mla_g_kv": [((), 17, 1, KV_LORA)],
        "sc_conv_w": [((0,), 18, SC_WIDTH, "chip"), ((1,), 21, SC_WIDTH, "chip")],
        "cf_b_pw1": [((), 24, 2, "chip")], "cf_dw_w": [((0,), 26, CONF_WIDTH, "chip")],
        "cf_dw_b": [((), 57, 1, "chip")], "cf_norm_g": [((), 58, 1, "chip")], "cf_norm_b": [((), 59, 1, "chip")],
        "cf_b_pw2": [((), 60, 1, "chip")]}
    vec = list(where)
    vec_res = vector_update(red, chip, [given[n] for n in vec], [given["m_" + n] for n in vec],
                            [given["v_" + n] for n in vec], [where[n] for n in vec])
    gw = dict(big_g)
    upd = {n: [big_d[n], big_m[n], big_v[n]] for n in big_g}
    for k, n in enumerate(vec):
        gw[n] = vec_res[0][k]
        upd[n] = [vec_res[1][k], vec_res[2][k], vec_res[3][k]]

    return (loss, grad_x, *[gw[n] for n in WEIGHTS], *[upd[n][0] for n in WEIGHTS],
            *[upd[n][1] for n in WEIGHTS], *[upd[n][2] for n in WEIGHTS])
```

```python
import jax
import jax.numpy as jnp
from jax import lax
from jax.experimental import pallas as pl
from jax.experimental.pallas import tpu as pltpu
from jax.experimental.pallas import tpu_sc as plsc

F32 = jnp.float32
BF16 = jnp.bfloat16
MESH = pl.DeviceIdType.MESH

DEPTH = 4
ALPHA = (2.0 * DEPTH) ** 0.25
LN_EPS = 1e-5
RMS_EPS = 1e-6
CHUNK_SHIFT = 6
N_HEADS = 8
QK_NOPE = 128
QK_ROPE = 64
V_HEAD = 128
HEAD_PAD = 256
Q_LORA = 384
KV_LORA = 256
ROPE_THETA = 10000.0
SC_WIDTH = 3
CONF_WIDTH = 31
CONV_PAD = 32
CONV_CHUNK = 64
N_CHIPS = 4
ATTN_SCALE = (QK_NOPE + QK_ROPE) ** -0.5

ADAM_LR = 0.001
ADAM_B1 = 0.9
ADAM_B2 = 0.999
ADAM_EPS = 1e-08
ADAM_WD = 0.01
ADAM_STEP = 10

VMEM_LIMIT = 56 * 2**20

NN = (((1,), (0,)), ((), ()))
NT = (((1,), (1,)), ((), ()))
TN = (((0,), (0,)), ((), ()))


def _params(sem=None):
    return pltpu.CompilerParams(dimension_semantics=sem, vmem_limit_bytes=VMEM_LIMIT)


class Stk:
    def __init__(self, kind, k, n, arr=None):
        self.kind, self.k, self.n = kind, k, n
        self.plain = kind != "col"
        self.nloc = n // N_CHIPS if kind == "col" else n
        self.arr = arr.reshape(k, n) if arr is not None and self.plain else arr

    @property
    def shape(self):
        return (self.k, self.n) if self.plain else (N_CHIPS, self.k, self.nloc)

    def spec(self, bk, bn, f, resident=False):
        if self.plain:
            return pl.BlockSpec((bk, bn), f, pipeline_mode=pl.Buffered(1)) if resident else pl.BlockSpec((bk, bn), f)
        assert self.k % bk == 0 and self.nloc % bn == 0, (self.k, bk, self.nloc, bn)
        pn = self.nloc // bn

        def imap(*g):
            kb, nb = f(*g)
            return nb // pn, kb, nb % pn

        return pl.BlockSpec((None, bk, bn), imap)


def _mm(name, mode, a, b, grid, a_spec, b_spec, acc_shape, extras, extra_specs, out_shapes, out_specs, epi, a_fn=None,
        rows_in_order=False):
    nk = grid[2]
    ne = len(extras)

    def body(*refs):
        a_ref, b_ref = refs[0], refs[1]
        e_refs = refs[2:2 + ne]
        av = a_ref[...] if a_fn is None else a_fn(a_ref[...])
        part = lax.dot_general(av, b_ref[...], mode, preferred_element_type=F32)
        if nk == 1:
            epi(part, e_refs, refs[2 + ne:])
            return
        o_refs = refs[2 + ne:-1]
        acc = refs[-1]
        k = pl.program_id(2)

        @pl.when(k == 0)
        def _():
            acc[...] = part

        @pl.when(k > 0)
        def _():
            acc[...] += part

        @pl.when(k == nk - 1)
        def _():
            epi(acc[...], e_refs, o_refs)

    return pl.pallas_call(
        body, grid=grid, in_specs=[a_spec, b_spec, *extra_specs], out_specs=out_specs, out_shape=out_shapes,
        scratch_shapes=[pltpu.VMEM(acc_shape, F32)] if nk > 1 else [],
        compiler_params=_params(("arbitrary",) * 3 if rows_in_order else ("parallel", "parallel", "arbitrary")),
        name=name)(a, b, *extras)


def _tile(n, t):
    t = min(n, t)
    while n % t:
        t -= 8
    assert t > 0, (n, t)
    return t


def mm_nn(name, a, w, tm, tn, tk, epi, out_shapes, out_specs, extras=(), extra_specs=(), a_spec=None, a_fn=None):
    m = a.shape[0]
    tm, tn, tk = _tile(m, tm), _tile(w.n, tn), _tile(w.k, tk)
    grid = (m // tm, w.n // tn, w.k // tk)
    a_spec = a_spec or pl.BlockSpec((tm, tk), lambda i, j, k: (i, k))
    b_spec = w.spec(tk, tn, lambda i, j, k: (k, j))
    return _mm(name, NN, a, w.arr, grid, a_spec, b_spec, (tm, tn), extras, extra_specs, out_shapes, out_specs, epi, a_fn)


def mm_nt(name, a, w, m, tm, tn, tk, epi, out_shapes, out_specs, extras=(), extra_specs=(), a_spec=None,
          rows_in_order=False):
    tm, tn, tk = _tile(m, tm), _tile(w.k, tn), _tile(w.n, tk)
    grid = (m // tm, w.k // tn, w.n // tk)
    a_spec = a_spec or pl.BlockSpec((tm, tk), lambda i, j, k: (i, k))
    b_spec = w.spec(tn, tk, lambda i, j, k: (j, k), resident=grid[1] == 1 and grid[2] == 1)
    return _mm(name, NT, a, w.arr, grid, a_spec, b_spec, (tm, tn), extras, extra_specs, out_shapes, out_specs, epi,
               rows_in_order=rows_in_order)


def mm_tn(name, a, b, dw, s, tm=512, tn=512, tk=4096, a_spec=None, b_spec=None, a_fn=None):
    tm, tn, tk = _tile(dw.k, tm), _tile(dw.n, tn), _tile(s, tk)
    grid = (dw.k // tm, dw.n // tn, s // tk)
    a_spec = a_spec or pl.BlockSpec((tk, tm), lambda i, j, k: (k, i))
    b_spec = b_spec or pl.BlockSpec((tk, tn), lambda i, j, k: (k, j))

    def epi(acc, e, o):
        o[0][...] = acc.astype(BF16)

    out = _mm(name, TN, a, b, grid, a_spec, b_spec, (tm, tn), (), (), [jax.ShapeDtypeStruct(dw.shape, BF16)],
              [dw.spec(tm, tn, lambda i, j, k: (i, j))], epi, a_fn)[0]
    return out.reshape(N_CHIPS, dw.k // N_CHIPS, dw.n) if dw.plain else out


def _sds(shape, dtype):
    return jax.ShapeDtypeStruct(shape, dtype)


def _ij(tm, tn):
    return pl.BlockSpec((tm, tn), lambda i, j, k: (i, j))


def _i0(tm, c):
    return pl.BlockSpec((tm, c), lambda i, j, k: (i, 0))


def _0j(r, tn):
    return pl.BlockSpec((r, tn), lambda i, j, k: (0, j))


def _layer_norm_rows(r, g, b):
    mu = jnp.mean(r, axis=-1, keepdims=True)
    d = r - mu
    var = jnp.mean(d * d, axis=-1, keepdims=True)
    rstd = lax.rsqrt(var + LN_EPS)
    xh = d * rstd
    return xh * g + b, xh, rstd


def mm_residual_ln(name, a, w, x, g, b, bias=None, tm=512, tk=1024, a_fn=None):
    s, d = x.shape
    tm = _tile(s, tm)
    extras = [x, g, b] + ([bias] if bias is not None else [])
    especs = [_i0(tm, d), _0j(1, d), _0j(1, d)] + ([_0j(1, d)] if bias is not None else [])

    def epi(acc, e, o):
        r = ALPHA * e[0][...] + acc
        if bias is not None:
            r = r + e[3][...]
        y, xh, rstd = _layer_norm_rows(r, e[1][...], e[2][...])
        o[0][...] = y
        o[1][...] = y.astype(BF16)
        o[2][...] = xh
        o[3][...] = rstd

    return mm_nn(name, a, w, tm, d, tk, epi,
                 [_sds((s, d), F32), _sds((s, d), BF16), _sds((s, d), F32), _sds((s, 1), F32)],
                 [_i0(tm, d), _i0(tm, d), _i0(tm, d), _i0(tm, 1)], extras, especs, a_fn=a_fn)


def mm_plain_nn(name, a, w, out_dtype, tm=1024, tn=512, tk=1024, bias=None):
    m = a.shape[0]
    tm, tn = _tile(m, tm), _tile(w.n, tn)

    def epi(acc, e, o):
        if bias is not None:
            acc = acc + e[0][...]
        o[0][...] = acc.astype(out_dtype)

    extras, especs = ([bias], [_0j(1, tn)]) if bias is not None else ((), ())
    return mm_nn(name, a, w, tm, tn, tk, epi, [_sds((m, w.n), out_dtype)], [_ij(tm, tn)], extras, especs)[0]


def mm_plain_nt(name, a, w, out_dtype, tm=1024, tn=512, tk=1024, add=None, add_scale=1.0, a_spec_fn=None):
    m = a.shape[0] if a_spec_fn is None else a_spec_fn[0]
    tm, tn = _tile(m, tm), _tile(w.k, tn)
    tk = _tile(w.n, tk)

    def epi(acc, e, o):
        if add is not None:
            acc = acc + add_scale * e[0][...].astype(F32)
        o[0][...] = acc.astype(out_dtype)

    extras, especs = ([add], [_ij(tm, tn)]) if add is not None else ((), ())
    a_spec = None if a_spec_fn is None else a_spec_fn[1](tm, tk)
    return mm_nt(name, a, w, m, tm, tn, tk, epi, [_sds((m, w.k), out_dtype)], [_ij(tm, tn)], extras, especs,
                 a_spec=a_spec)[0]


def _rows(tm, c):
    return pl.BlockSpec((tm, c), lambda i: (i, 0))


def _fix(shape):
    nd = len(shape)
    return pl.BlockSpec(shape, lambda i: (0,) * nd)


def _accumulate(ref, val):
    @pl.when(pl.program_id(0) == 0)
    def _():
        ref[...] = jnp.zeros_like(ref)

    ref[...] += val


def _ln_backward_rows(dyv, xh, rstd, g, dr_ref, drb_ref, dg_ref, db_ref, ds_ref):
    dxh = dyv * g
    m1 = jnp.mean(dxh, axis=-1, keepdims=True)
    m2 = jnp.mean(dxh * xh, axis=-1, keepdims=True)
    dr = rstd * (dxh - m1 - xh * m2)
    dr_ref[...] = dr
    drb_ref[...] = dr.astype(BF16)
    _accumulate(dg_ref, jnp.sum(dyv * xh, axis=0, keepdims=True))
    _accumulate(db_ref, jnp.sum(dyv, axis=0, keepdims=True))
    _accumulate(ds_ref, jnp.sum(dr, axis=0, keepdims=True))


def mm_nt_ln_backward(name, a, w, add, xhat, rstd, g, tm=512, tk=1024, a_spec_fn=None):
    m, d = add.shape
    tm, tk = _tile(m, tm), _tile(w.n, tk)

    def epi(acc, e, o):
        _ln_backward_rows(acc + ALPHA * e[0][...], e[1][...], e[2][...], e[3][...], *o)

    vec = pl.BlockSpec((1, d), lambda i, j, k: (0, 0))
    a_spec = None if a_spec_fn is None else a_spec_fn(tm, tk)
    return mm_nt(name, a, w, m, tm, d, tk, epi,
                 [_sds((m, d), F32), _sds((m, d), BF16), _sds((1, d), F32), _sds((1, d), F32), _sds((1, d), F32)],
                 [_i0(tm, d), _i0(tm, d), vec, vec, vec], [add, xhat, rstd, g],
                 [_i0(tm, d), _i0(tm, d), _i0(tm, 1), vec], a_spec=a_spec, rows_in_order=True)


def loss_ln_backward(y, target, xhat, rstd, g, tm=512):
    s, d = y.shape
    tm = _tile(s, tm)

    def body(y_ref, t_ref, xh_ref, rstd_ref, g_ref, dr_ref, drb_ref, dg_ref, db_ref, ds_ref, loss_ref):
        e = y_ref[...] - t_ref[...]
        part = 0.5 * jnp.sum(jnp.mean(e * e, axis=-1, keepdims=True), axis=0, keepdims=True)
        _accumulate(loss_ref, jnp.broadcast_to(part, (1, d)))
        _ln_backward_rows(e * (1.0 / d), xh_ref[...], rstd_ref[...], g_ref[...], dr_ref, drb_ref, dg_ref, db_ref, ds_ref)

    return pl.pallas_call(
        body, grid=(s // tm,),
        in_specs=[_rows(tm, d), _rows(tm, d), _rows(tm, d), _rows(tm, 1), _fix((1, d))],
        out_specs=[_rows(tm, d), _rows(tm, d)] + [_fix((1, d))] * 4,
        out_shape=[_sds((s, d), F32), _sds((s, d), BF16)] + [_sds((1, d), F32)] * 4,
        compiler_params=_params(("arbitrary",)), name="loss_ln_backward")(y, target, xhat, rstd, g)


def _cols(s, tc, off=0):
    return pl.BlockSpec((s, tc), lambda i: (0, i + off))


def _shift_down(z, sft, rows):
    return jnp.where(rows >= sft, pltpu.roll(z, sft, 0), 0.0)


def _shift_up(z, sft, rows, s):
    return jnp.where(rows < s - sft, pltpu.roll(z, (s - sft) % s, 0), 0.0)


def short_conv_gate(u, conv_w, tc=256):
    s, d3 = u.shape
    d = d3 // 3
    nb = d // tc

    def body(b_ref, c_ref, h_ref, w_ref, o_ref):
        rows = lax.broadcasted_iota(jnp.int32, (s, tc), 0)
        z = c_ref[...] * h_ref[...]
        cz = jnp.zeros((s, tc), F32)
        for k in range(SC_WIDTH):
            sft = SC_WIDTH - 1 - k
            cz = cz + w_ref[pl.ds(k, 1), :] * (_shift_down(z, sft, rows) if sft else z)
        o_ref[...] = (b_ref[...] * cz).astype(BF16)

    return pl.pallas_call(
        body, grid=(nb,),
        in_specs=[_cols(s, tc), _cols(s, tc, nb), _cols(s, tc, 2 * nb), _cols(SC_WIDTH, tc)],
        out_specs=_cols(s, tc), out_shape=_sds((s, d), BF16),
        compiler_params=_params(("parallel",)), name="short_conv_gate")(u, u, u, conv_w)


def short_conv_gate_bwd(u, conv_w, dg, tc=256):
    s, d3 = u.shape
    d = d3 // 3
    nb = d // tc

    def body(b_ref, c_ref, h_ref, w_ref, dg_ref, du_ref, dw_ref):
        rows = lax.broadcasted_iota(jnp.int32, (s, tc), 0)
        c, h, dgv = c_ref[...], h_ref[...], dg_ref[...]
        z = c * h
        dcz = dgv * b_ref[...]
        cz = jnp.zeros((s, tc), F32)
        dz = jnp.zeros((s, tc), F32)
        for k in range(SC_WIDTH):
            sft = SC_WIDTH - 1 - k
            zs = _shift_down(z, sft, rows) if sft else z
            wk = w_ref[pl.ds(k, 1), :]
            cz = cz + wk * zs
            dz = dz + wk * (_shift_up(dcz, sft, rows, s) if sft else dcz)
            dw_ref[pl.ds(k, 1), :] = jnp.sum(dcz * zs, axis=0, keepdims=True)
        du_ref[0] = (dgv * cz).astype(BF16)
        du_ref[1] = (dz * h).astype(BF16)
        du_ref[2] = (dz * c).astype(BF16)

    return pl.pallas_call(
        body, grid=(nb,),
        in_specs=[_cols(s, tc), _cols(s, tc, nb), _cols(s, tc, 2 * nb), _cols(SC_WIDTH, tc), _cols(s, tc)],
        out_specs=[pl.BlockSpec((3, s, tc), lambda i: (0, 0, i)), _cols(SC_WIDTH, tc)],
        out_shape=[_sds((3, s, d), BF16), _sds((SC_WIDTH, d), F32)],
        compiler_params=_params(("parallel",)), name="short_conv_gate_bwd")(u, u, u, conv_w, dg)


def _store_shifted_down(ref, z, rows):
    s, tc = z.shape
    for b in range(8):
        ref[b, pl.ds(0, CONV_PAD), :] = jnp.zeros((CONV_PAD, tc), F32)
        ref[b, pl.ds(CONV_PAD, s), :] = z if b == 0 else _shift_down(z, b, rows)


def _store_shifted_up(ref, z, rows):
    s, tc = z.shape
    for b in range(8):
        ref[b, pl.ds(0, s), :] = z if b == 0 else _shift_up(z, b, rows, s)
        ref[b, pl.ds(s, CONV_PAD), :] = jnp.zeros((CONV_PAD, tc), F32)


def conformer_glu_conv(u, dw_w, dw_b, tc=128):
    s, d2 = u.shape
    d = d2 // 2
    nb = d // tc

    ch = min(CONV_CHUNK, s)

    def body(a_ref, g_ref, w_ref, b_ref, o_ref, down):
        rows = lax.broadcasted_iota(jnp.int32, (s, tc), 0)
        _store_shifted_down(down, a_ref[...] * jax.nn.sigmoid(g_ref[...]), rows)

        def chunk(ci, carry):
            r0 = pl.multiple_of(ci * ch, ch)
            acc = jnp.broadcast_to(b_ref[...], (ch, tc))
            for k in range(CONF_WIDTH):
                sft = CONF_WIDTH - 1 - k
                acc = acc + w_ref[pl.ds(k, 1), :] * down[sft % 8, pl.ds(CONV_PAD + r0 - (sft // 8) * 8, ch), :]
            o_ref[pl.ds(r0, ch), :] = acc
            return carry

        lax.fori_loop(0, s // ch, chunk, 0)

    return pl.pallas_call(
        body, grid=(nb,),
        in_specs=[_cols(s, tc), _cols(s, tc, nb), _cols(CONF_WIDTH, tc), _cols(1, tc)],
        out_specs=_cols(s, tc), out_shape=_sds((s, d), F32),
        scratch_shapes=[pltpu.VMEM((8, CONV_PAD + s, tc), F32)],
        compiler_params=_params(("parallel",)), name="conformer_glu_conv")(u, u, dw_w, dw_b)


def conformer_glu_conv_bwd(u, dw_w, dhc, tc=128):
    s, d2 = u.shape
    d = d2 // 2
    nb = d // tc
    ch = min(CONV_CHUNK, s)

    def body(a_ref, g_ref, w_ref, dhc_ref, du_ref, dbias_ref, dw_ref, db_ref, down, up, dw_acc, dh_buf):
        rows = lax.broadcasted_iota(jnp.int32, (s, tc), 0)
        a = a_ref[...]
        sg = jax.nn.sigmoid(g_ref[...])
        dhcv = dhc_ref[...]
        _store_shifted_down(down, a * sg, rows)
        _store_shifted_up(up, dhcv, rows)
        dw_acc[...] = jnp.zeros_like(dw_acc)

        def chunk(ci, carry):
            r0 = pl.multiple_of(ci * ch, ch)
            dc = dhc_ref[pl.ds(r0, ch), :]
            dh = jnp.zeros((ch, tc), F32)
            for k in range(CONF_WIDTH):
                sft = CONF_WIDTH - 1 - k
                a8, b = (sft // 8) * 8, sft % 8
                dh = dh + w_ref[pl.ds(k, 1), :] * up[b, pl.ds(r0 + a8, ch), :]
                prod = dc * down[b, pl.ds(CONV_PAD + r0 - a8, ch), :]
                dw_acc[k] += jnp.sum(prod.reshape(ch // 8, 8, tc), axis=0)
            dh_buf[pl.ds(r0, ch), :] = dh
            return carry

        lax.fori_loop(0, s // ch, chunk, 0)
        dh = dh_buf[...]
        da = dh * sg
        dgate = dh * a * sg * (1.0 - sg)
        du_ref[0] = da.astype(BF16)
        du_ref[1] = dgate.astype(BF16)
        dbias_ref[pl.ds(0, 1), :] = jnp.sum(da, axis=0, keepdims=True)
        dbias_ref[pl.ds(1, 1), :] = jnp.sum(dgate, axis=0, keepdims=True)
        db_ref[...] = jnp.sum(dhcv, axis=0, keepdims=True)
        for k in range(CONF_WIDTH):
            dw_ref[pl.ds(k, 1), :] = jnp.sum(dw_acc[k], axis=0, keepdims=True)

    return pl.pallas_call(
        body, grid=(nb,),
        in_specs=[_cols(s, tc), _cols(s, tc, nb), _cols(CONF_WIDTH, tc), _cols(s, tc)],
        out_specs=[pl.BlockSpec((2, s, tc), lambda i: (0, 0, i)), _cols(2, tc), _cols(CONF_WIDTH, tc), _cols(1, tc)],
        out_shape=[_sds((2, s, d), BF16), _sds((2, d), F32), _sds((CONF_WIDTH, d), F32), _sds((1, d), F32)],
        scratch_shapes=[pltpu.VMEM((8, CONV_PAD + s, tc), F32), pltpu.VMEM((8, CONV_PAD + s, tc), F32),
                        pltpu.VMEM((CONF_WIDTH + 1, 8, tc), F32), pltpu.VMEM((s, tc), F32)],
        compiler_params=_params(("parallel",)), name="conformer_glu_conv_bwd")(u, u, dw_w, dhc)


def conformer_norm_swish(hc, g, b, tm=512):
    s, d = hc.shape
    tm = _tile(s, tm)

    def body(h_ref, g_ref, b_ref, o_ref):
        n, _, _ = _layer_norm_rows(h_ref[...], g_ref[...], b_ref[...])
        o_ref[...] = (n * jax.nn.sigmoid(n)).astype(BF16)

    return pl.pallas_call(
        body, grid=(s // tm,), in_specs=[_rows(tm, d), _fix((1, d)), _fix((1, d))], out_specs=_rows(tm, d),
        out_shape=_sds((s, d), BF16), compiler_params=_params(("parallel",)), name="conformer_norm_swish")(hc, g, b)


def conformer_norm_swish_bwd(hc, g, b, ds, tm=512):
    s, d = hc.shape
    tm = _tile(s, tm)

    def body(h_ref, g_ref, b_ref, ds_ref, dh_ref, dg_ref, db_ref):
        n, nh, rstd = _layer_norm_rows(h_ref[...], g_ref[...], b_ref[...])
        sg = jax.nn.sigmoid(n)
        dn = ds_ref[...] * (sg * (1.0 + n * (1.0 - sg)))
        dnh = dn * g_ref[...]
        m1 = jnp.mean(dnh, axis=-1, keepdims=True)
        m2 = jnp.mean(dnh * nh, axis=-1, keepdims=True)
        dh_ref[...] = rstd * (dnh - m1 - nh * m2)
        _accumulate(dg_ref, jnp.sum(dn * nh, axis=0, keepdims=True))
        _accumulate(db_ref, jnp.sum(dn, axis=0, keepdims=True))

    return pl.pallas_call(
        body, grid=(s // tm,), in_specs=[_rows(tm, d), _fix((1, d)), _fix((1, d)), _rows(tm, d)],
        out_specs=[_rows(tm, d), _fix((1, d)), _fix((1, d))],
        out_shape=[_sds((s, d), F32), _sds((1, d), F32), _sds((1, d), F32)],
        compiler_params=_params(("arbitrary",)), name="conformer_norm_swish_bwd")(hc, g, b, ds)


def _swap_halves(x):
    lane = lax.broadcasted_iota(jnp.int32, x.shape, 1)
    return jnp.where(lane < QK_ROPE // 2, pltpu.roll(x, 128 - QK_ROPE // 2, 1), pltpu.roll(x, QK_ROPE // 2, 1))


def _rope(x, cf, sf):
    return x * cf + _swap_halves(x) * sf


def _unrope(dx, cf, sf):
    return dx * cf - _swap_halves(dx) * sf


def _rms_rows(x, g):
    r = lax.rsqrt(jnp.mean(x * x, axis=-1, keepdims=True) + RMS_EPS)
    return x * r, r


def mla_latents(t, g_q, g_kv, cf, sf, tm=512):
    s = t.shape[0]
    tm = _tile(s, tm)

    def body(t_ref, gq_ref, gkv_ref, cf_ref, sf_ref, cq_ref, ckv_ref, kpe_ref):
        xq, _ = _rms_rows(t_ref[:, 0:Q_LORA], gq_ref[...])
        cq_ref[...] = (xq * gq_ref[...]).astype(BF16)
        xkv, _ = _rms_rows(t_ref[:, Q_LORA:Q_LORA + KV_LORA], gkv_ref[...])
        ckv_ref[...] = (xkv * gkv_ref[...]).astype(BF16)
        kpe_ref[...] = _rope(t_ref[:, Q_LORA + KV_LORA:], cf_ref[...], sf_ref[...]).astype(BF16)

    w = Q_LORA + KV_LORA + 128
    return pl.pallas_call(
        body, grid=(s // tm,),
        in_specs=[_rows(tm, w), _fix((1, Q_LORA)), _fix((1, KV_LORA)), _rows(tm, 128), _rows(tm, 128)],
        out_specs=[_rows(tm, Q_LORA), _rows(tm, KV_LORA), _rows(tm, 128)],
        out_shape=[_sds((s, Q_LORA), BF16), _sds((s, KV_LORA), BF16), _sds((s, 128), BF16)],
        compiler_params=_params(("parallel",)), name="mla_latents")(t, g_q, g_kv, cf, sf)


def mla_latents_bwd(t, g_q, g_kv, cf, sf, dcq, dckv, dkpe, tm=512):
    s = t.shape[0]
    tm = _tile(s, tm)
    w = Q_LORA + KV_LORA + 128

    def rms_bwd(x, g, dy):
        xh, r = _rms_rows(x, g)
        dxh = dy * g
        return r * (dxh - xh * jnp.mean(dxh * xh, axis=-1, keepdims=True)), jnp.sum(dy * xh, axis=0, keepdims=True)

    def body(t_ref, gq_ref, gkv_ref, cf_ref, sf_ref, dcq_ref, dckv_ref, dkpe_ref, dt_ref, dgq_ref, dgkv_ref):
        dxq, dgq = rms_bwd(t_ref[:, 0:Q_LORA], gq_ref[...], dcq_ref[...])
        dxkv, dgkv = rms_bwd(t_ref[:, Q_LORA:Q_LORA + KV_LORA], gkv_ref[...], dckv_ref[...])
        dt_ref[:, 0:Q_LORA] = dxq.astype(BF16)
        dt_ref[:, Q_LORA:Q_LORA + KV_LORA] = dxkv.astype(BF16)
        dt_ref[:, Q_LORA + KV_LORA:] = _unrope(dkpe_ref[...], cf_ref[...], sf_ref[...]).astype(BF16)
        _accumulate(dgq_ref, dgq)
        _accumulate(dgkv_ref, dgkv)

    return pl.pallas_call(
        body, grid=(s // tm,),
        in_specs=[_rows(tm, w), _fix((1, Q_LORA)), _fix((1, KV_LORA)), _rows(tm, 128), _rows(tm, 128),
                  _rows(tm, Q_LORA), _rows(tm, KV_LORA), _rows(tm, 128)],
        out_specs=[_rows(tm, w), _fix((1, Q_LORA)), _fix((1, KV_LORA))],
        out_shape=[_sds((s, w), BF16), _sds((1, Q_LORA), F32), _sds((1, KV_LORA), F32)],
        compiler_params=_params(("arbitrary",)), name="mla_latents_bwd")(t, g_q, g_kv, cf, sf, dcq, dckv, dkpe)


def mla_queries(cq, w_uq, cf, sf, tm=2048):
    s = cq.shape[0]
    tm = _tile(s, tm)

    def epi(acc, e, o):
        o[0][:, 0:QK_NOPE] = acc[:, 0:QK_NOPE].astype(BF16)
        o[0][:, QK_NOPE:] = _rope(acc[:, QK_NOPE:], e[0][...], e[1][...]).astype(BF16)

    return mm_nn("mla_queries", cq, w_uq, tm, HEAD_PAD, Q_LORA, epi, [_sds((s, N_HEADS * HEAD_PAD), BF16)],
                 [_ij(tm, HEAD_PAD)], [cf, sf], [_i0(tm, 128), _i0(tm, 128)])[0]


def mla_keys(ckv, w_uk, kpe, tm=2048):
    s = ckv.shape[0]
    tm = _tile(s, tm)

    def epi(acc, e, o):
        o[0][:, 0:QK_NOPE] = acc.astype(BF16)
        o[0][:, QK_NOPE:] = e[0][...]

    return mm_nn("mla_keys", ckv, w_uk, tm, QK_NOPE, KV_LORA, epi, [_sds((s, N_HEADS * HEAD_PAD), BF16)],
                 [_ij(tm, HEAD_PAD)], [kpe], [_i0(tm, 128)])[0]


def _masked_scores(q, k, tq, kv):
    sc = lax.dot_general(q, k, NT, preferred_element_type=F32) * ATTN_SCALE
    row = lax.broadcasted_iota(jnp.int32, (tq, tq), 0)
    col = lax.broadcasted_iota(jnp.int32, (tq, tq), 1)
    ok = lax.shift_right_logical(col, CHUNK_SHIFT) <= lax.shift_right_logical(row, CHUNK_SHIFT)
    own = jnp.where(ok, sc[:, kv - tq:], -1e30)
    return own if kv == tq else jnp.concatenate([sc[:, :kv - tq], own], axis=1)


def attention(q, k, v, tq=512):
    s = q.shape[0]
    tq = _tile(s, tq)
    nq = s // tq

    def body(q_ref, k_ref, v_ref, o_ref):
        for qi in range(nq):
            kv = (qi + 1) * tq
            sc = _masked_scores(q_ref[pl.ds(qi * tq, tq), :], k_ref[pl.ds(0, kv), :], tq, kv)
            p = jnp.exp(sc - jnp.max(sc, axis=-1, keepdims=True))
            o = lax.dot_general(p.astype(BF16), v_ref[pl.ds(0, kv), :], NN, preferred_element_type=F32)
            o_ref[pl.ds(qi * tq, tq), :] = (o / jnp.sum(p, axis=-1, keepdims=True)).astype(BF16)

    hq = pl.BlockSpec((s, HEAD_PAD), lambda h: (0, h))
    hv = pl.BlockSpec((s, V_HEAD), lambda h: (0, h))
    return pl.pallas_call(
        body, grid=(N_HEADS,), in_specs=[hq, hq, hv], out_specs=hv, out_shape=_sds((s, N_HEADS * V_HEAD), BF16),
        compiler_params=_params(("parallel",)), name="attention")(q, k, v)


def attention_bwd(q, k, v, do, tq=512):
    s = q.shape[0]
    tq = _tile(s, tq)
    nq = s // tq

    def body(q_ref, k_ref, v_ref, do_ref, dq_ref, dk_ref, dv_ref, dk_acc, dv_acc):
        dk_acc[...] = jnp.zeros_like(dk_acc)
        dv_acc[...] = jnp.zeros_like(dv_acc)
        for qi in range(nq):
            kv = (qi + 1) * tq
            qt = q_ref[pl.ds(qi * tq, tq), :]
            kt = k_ref[pl.ds(0, kv), :]
            dot = do_ref[pl.ds(qi * tq, tq), :]
            sc = _masked_scores(qt, kt, tq, kv)
            p = jnp.exp(sc - jnp.max(sc, axis=-1, keepdims=True))
            p = p / jnp.sum(p, axis=-1, keepdims=True)
            dp = lax.dot_general(dot, v_ref[pl.ds(0, kv), :], NT, preferred_element_type=F32)
            delta = jnp.sum(p * dp, axis=-1, keepdims=True)
            ds = (p * (dp - delta) * ATTN_SCALE).astype(BF16)
            dq_ref[pl.ds(qi * tq, tq), :] = lax.dot_general(ds, kt, NN, preferred_element_type=F32).astype(BF16)
            dk_acc[pl.ds(0, kv), :] += lax.dot_general(ds, qt, TN, preferred_element_type=F32)
            dv_acc[pl.ds(0, kv), :] += lax.dot_general(p.astype(BF16), dot, TN, preferred_element_type=F32)
        dk_ref[...] = dk_acc[...].astype(BF16)
        dv_ref[...] = dv_acc[...].astype(BF16)

    hq = pl.BlockSpec((s, HEAD_PAD), lambda h: (0, h))
    hv = pl.BlockSpec((s, V_HEAD), lambda h: (0, h))
    return pl.pallas_call(
        body, grid=(N_HEADS,), in_specs=[hq, hq, hv, hv], out_specs=[hq, hq, hv],
        out_shape=[_sds((s, N_HEADS * HEAD_PAD), BF16), _sds((s, N_HEADS * HEAD_PAD), BF16),
                   _sds((s, N_HEADS * V_HEAD), BF16)],
        scratch_shapes=[pltpu.VMEM((s, HEAD_PAD), F32), pltpu.VMEM((s, V_HEAD), F32)],
        compiler_params=_params(("parallel",)), name="attention_bwd")(q, k, v, do)


def mla_unrope_grads(dq, dk, cf, sf, tm=512):
    s = dq.shape[0]
    tm = _tile(s, tm)

    def body(dq_ref, dk_ref, cf_ref, sf_ref, dql_ref, dkn_ref, dkpe_ref):
        cfv, sfv = cf_ref[...], sf_ref[...]
        dkpe = jnp.zeros((tm, 128), F32)
        for h in range(N_HEADS):
            lo = h * HEAD_PAD
            dql_ref[:, lo:lo + QK_NOPE] = dq_ref[:, lo:lo + QK_NOPE]
            dql_ref[:, lo + QK_NOPE:lo + HEAD_PAD] = _unrope(
                dq_ref[:, lo + QK_NOPE:lo + HEAD_PAD].astype(F32), cfv, sfv).astype(BF16)
            dkn_ref[:, h * QK_NOPE:(h + 1) * QK_NOPE] = dk_ref[:, lo:lo + QK_NOPE]
            dkpe = dkpe + dk_ref[:, lo + QK_NOPE:lo + HEAD_PAD].astype(F32)
        dkpe_ref[...] = dkpe

    wq = N_HEADS * HEAD_PAD
    return pl.pallas_call(
        body, grid=(s // tm,), in_specs=[_rows(tm, wq), _rows(tm, wq), _rows(tm, 128), _rows(tm, 128)],
        out_specs=[_rows(tm, wq), _rows(tm, N_HEADS * QK_NOPE), _rows(tm, 128)],
        out_shape=[_sds((s, wq), BF16), _sds((s, N_HEADS * QK_NOPE), BF16), _sds((s, 128), F32)],
        compiler_params=_params(("parallel",)), name="mla_unrope_grads")(dq, dk, cf, sf)


ANY = pl.BlockSpec(memory_space=pl.ANY)
GATHER_ID = 1
CHIP_EXCHANGE_ID = 2
PAIR_ID = 3
ALL_ID = 4


def _nbytes(a):
    return a.size * a.dtype.itemsize


def _copy_cost(operand_bytes, sent_fraction):
    sent = int(operand_bytes * sent_fraction)
    return pl.CostEstimate(flops=0, transcendentals=0, bytes_accessed=2 * sent, remote_bytes_transferred=sent)


def _handshake(peers):
    barrier = pltpu.get_barrier_semaphore()
    for peer in peers:
        pl.semaphore_signal(barrier, inc=1, device_id=peer, device_id_type=MESH)
    pl.semaphore_wait(barrier, len(peers))


def _place():
    x, y, c = lax.axis_index("x"), lax.axis_index("y"), lax.axis_index("c")
    chips = [(1 - x, y), (x, 1 - y), (1 - x, 1 - y)]
    return x, y, c, chips


def _half(ref, hc, axis=0):
    n = ref.shape[axis] // 2
    idx = (slice(None),) * axis + (pl.ds(hc * n, n),)
    return ref.at[idx]


def gather_shards(name, tensors, by_columns=()):
    nt = len(tensors)

    def body(*refs):
        a, g = refs[:nt], refs[nt:2 * nt]
        send, recv = refs[2 * nt:]
        x, y, c, _ = _place()
        q = 2 * x + y
        sib, xn, yn = (x, y, 1 - c), (1 - x, y, c), (x, 1 - y, c)
        q_xn, q_yn, q_diag = 2 * (1 - x) + y, 2 * x + 1 - y, 2 * (1 - x) + 1 - y
        _handshake([sib, xn, yn])

        def whole(t, p):
            if t in by_columns:
                n = a[t].shape[1]
                return g[t].at[:, pl.ds(p * n, n)]
            return g[t].at[p]

        def part(t, p, hc, quarter=None):
            rows = a[t].shape[0]
            if quarter is None:
                return whole(t, p).at[pl.ds(hc * (rows // 2), rows // 2)]
            return whole(t, p).at[pl.ds(hc * (rows // 2) + quarter * (rows // 4), rows // 4)]

        def rc(t, k, src, dst, to):
            return pltpu.make_async_remote_copy(src_ref=src, dst_ref=dst, send_sem=send.at[t, k], recv_sem=recv.at[t, k],
                                                device_id=to, device_id_type=MESH)

        sent = []

        def go(cp):
            cp.start()
            sent.append(cp)

        def landed(t, k, piece, frm):
            rc(t, k, piece, piece, frm).wait_recv()
            return piece

        for t in range(nt):
            go(rc(t, 8, a[t], whole(t, q), sib))
            mine = _half(a[t], c)
            go(rc(t, 0, mine, part(t, q, c), xn))
            go(rc(t, 1, mine, part(t, q, c), yn))
        for t in range(nt):
            from_y = landed(t, 1, part(t, q_yn, c), yn)
            go(rc(t, 2, part(t, q_yn, c, 0), part(t, q_yn, c, 0), xn))
            go(rc(t, 5, from_y, from_y, sib))
            from_x = landed(t, 0, part(t, q_xn, c), xn)
            go(rc(t, 3, part(t, q_xn, c, 1), part(t, q_xn, c, 1), yn))
            go(rc(t, 4, from_x, from_x, sib))
        for t in range(nt):
            for k, frm in ((2, xn), (3, yn)):
                piece = landed(t, k, part(t, q_diag, c, k - 2), frm)
                go(rc(t, 4 + k, piece, piece, sib))
        for t in range(nt):
            landed(t, 4, part(t, q_xn, 1 - c), sib)
            landed(t, 5, part(t, q_yn, 1 - c), sib)
            landed(t, 6, part(t, q_diag, 1 - c, 0), sib)
            landed(t, 7, part(t, q_diag, 1 - c, 1), sib)
            landed(t, 8, whole(t, q), sib)
        for cp in sent:
            cp.wait_send()

    return pl.kernel(
        body, name=name,
        out_type=[_sds((a.shape[0], N_CHIPS * a.shape[1]) if t in by_columns else (N_CHIPS,) + a.shape, a.dtype)
                  for t, a in enumerate(tensors)],
        mesh=plsc.ScalarSubcoreMesh(axis_name="sequencer", num_cores=1),
        scratch_types=[pltpu.SemaphoreType.DMA((nt, 9)), pltpu.SemaphoreType.DMA((nt, 9))],
        cost_estimate=_copy_cost(sum(_nbytes(a) for a in tensors), 4),
        compiler_params=pltpu.CompilerParams(collective_id=GATHER_ID))(*tensors)


def pair_exchange(name, grads, on_sequencer):
    nt = len(grads)

    def body(*refs):
        g, theirs = refs[:nt], refs[nt:2 * nt]
        send, recv = refs[2 * nt:]
        x, y, c, _ = _place()
        if on_sequencer:
            _handshake([(x, y, 1 - c)])
        cps = []
        for t in range(nt):
            cp = pltpu.make_async_remote_copy(src_ref=_half(g[t], 1 - c, 1), dst_ref=theirs[t], send_sem=send.at[t],
                                              recv_sem=recv.at[t], device_id=(x, y, 1 - c), device_id_type=MESH)
            cp.start()
            cps.append(cp)
        for cp in cps:
            cp.wait()

    if not on_sequencer:
        return pl.pallas_call(
            body, in_specs=[ANY] * nt, out_specs=[ANY] * nt,
            out_shape=[_sds((N_CHIPS, a.shape[1] // 2, a.shape[2]), a.dtype) for a in grads],
            scratch_shapes=[pltpu.SemaphoreType.DMA((nt,)), pltpu.SemaphoreType.DMA((nt,))],
            name=name)(*grads)
    return pl.kernel(
        body, name=name, out_type=[_sds((N_CHIPS, a.shape[1] // 2, a.shape[2]), a.dtype) for a in grads],
        mesh=plsc.ScalarSubcoreMesh(axis_name="sequencer", num_cores=1),
        scratch_types=[pltpu.SemaphoreType.DMA((nt,)), pltpu.SemaphoreType.DMA((nt,))],
        cost_estimate=_copy_cost(sum(_nbytes(a) for a in grads), 0.5),
        compiler_params=pltpu.CompilerParams(collective_id=PAIR_ID))(*grads)


def chip_exchange(name, parts):
    nt = len(parts)

    def body(*refs):
        a, r = refs[:nt], refs[nt:2 * nt]
        send, recv = refs[2 * nt:]
        x, y, c, chips = _place()
        _handshake([(*chip, c) for chip in chips])
        cps = []
        for t in range(nt):
            for j, chip in enumerate(chips):
                cp = pltpu.make_async_remote_copy(
                    src_ref=a[t].at[2 * chip[0] + chip[1]], dst_ref=r[t].at[j], send_sem=send.at[t, j],
                    recv_sem=recv.at[t, j], device_id=(*chip, c), device_id_type=MESH)
                cp.start()
                cps.append(cp)
        for cp in cps:
            cp.wait()

    return pl.kernel(
        body, name=name, out_type=[_sds((N_CHIPS - 1,) + a.shape[1:], a.dtype) for a in parts],
        mesh=plsc.ScalarSubcoreMesh(axis_name="sequencer", num_cores=1),
        scratch_types=[pltpu.SemaphoreType.DMA((nt, 3)), pltpu.SemaphoreType.DMA((nt, 3))],
        cost_estimate=_copy_cost(sum(_nbytes(a) for a in parts), 0.75),
        compiler_params=pltpu.CompilerParams(collective_id=CHIP_EXCHANGE_ID))(*parts)


def pair_share(name, halves):
    nt = len(halves)

    def body(*refs):
        h, other = refs[:nt], refs[nt:2 * nt]
        send, recv = refs[2 * nt:]
        x, y, c, _ = _place()
        _handshake([(x, y, 1 - c)])
        cps = []
        for t in range(nt):
            cp = pltpu.make_async_remote_copy(src_ref=h[t], dst_ref=other[t], send_sem=send.at[t], recv_sem=recv.at[t],
                                              device_id=(x, y, 1 - c), device_id_type=MESH)
            cp.start()
            cps.append(cp)
        for cp in cps:
            cp.wait()

    return pl.kernel(
        body, name=name, out_type=[_sds(a.shape, a.dtype) for a in halves],
        mesh=plsc.ScalarSubcoreMesh(axis_name="sequencer", num_cores=1),
        scratch_types=[pltpu.SemaphoreType.DMA((nt,)), pltpu.SemaphoreType.DMA((nt,))],
        cost_estimate=_copy_cost(sum(_nbytes(a) for a in halves), 1),
        compiler_params=pltpu.CompilerParams(collective_id=PAIR_ID))(*halves)


def pack_rows(name, parts, rows):
    cdim = parts[0].shape[1]
    n = len(parts)
    vm = pl.BlockSpec(memory_space=pltpu.VMEM)

    def pack(*refs):
        p, o_ref = refs[:n], refs[n]
        at = 0
        for ref in p:
            o_ref[pl.ds(at, ref.shape[0]), :] = ref[...]
            at += ref.shape[0]
        o_ref[pl.ds(at, rows - at), :] = jnp.zeros((rows - at, cdim), F32)

    return pl.pallas_call(pack, in_specs=[vm] * n, out_specs=vm, out_shape=_sds((rows, cdim), F32), name=name)(*parts)


def all_reduce_small(parts, rows):
    cdim = parts[0].shape[1]
    vm = pl.BlockSpec(memory_space=pltpu.VMEM)
    mine = pack_rows("small_pack", parts, rows)

    def exchange(mine_ref, buf, send, recv, lsem):
        x, y, c, _ = _place()
        me = 4 * x + 2 * y + c
        peers = [(x ^ (k >> 2), y ^ ((k >> 1) & 1), c ^ (k & 1)) for k in range(1, 8)]
        _handshake(peers)
        own = pltpu.make_async_copy(mine_ref, buf.at[me], lsem)
        own.start()
        cps = []
        for k, to in enumerate(peers):
            cp = pltpu.make_async_remote_copy(src_ref=mine_ref, dst_ref=buf.at[me], send_sem=send.at[k], recv_sem=recv.at[k],
                                              device_id=to, device_id_type=MESH)
            cp.start()
            cps.append(cp)
        for k, (px, py, pc) in enumerate(peers):
            pltpu.make_async_remote_copy(src_ref=mine_ref, dst_ref=buf.at[4 * px + 2 * py + pc], send_sem=send.at[k],
                                         recv_sem=recv.at[k], device_id=(x, y, c), device_id_type=MESH).wait_recv()
        for cp in cps:
            cp.wait_send()
        own.wait()

    landed = pl.kernel(
        exchange, name="small_exchange", out_type=_sds((8, rows, cdim), F32),
        mesh=plsc.ScalarSubcoreMesh(axis_name="sequencer", num_cores=1),
        scratch_types=[pltpu.SemaphoreType.DMA((7,)), pltpu.SemaphoreType.DMA((7,)), pltpu.SemaphoreType.DMA],
        cost_estimate=_copy_cost(rows * cdim * 4, 7),
        compiler_params=pltpu.CompilerParams(collective_id=ALL_ID))(mine)

    def total(buf, o_ref):
        acc = buf[0]
        for d in range(1, 8):
            acc = acc + buf[d]
        o_ref[...] = acc

    return pl.pallas_call(total, in_specs=[vm], out_specs=vm, out_shape=_sds((rows, cdim), F32), name="small_sum")(landed)


def pair_sum(gs, theirs, core, tm=256):
    n = len(gs)
    _, r, c = gs[0].shape
    tm = _tile(r // 2, tm)
    nh = r // 2 // tm

    def body(core_ref, *refs):
        for a_ref, b_ref, o_ref in zip(refs[:n], refs[n:2 * n], refs[2 * n:]):
            o_ref[...] = (a_ref[...].astype(F32) + b_ref[...].astype(F32)).astype(BF16)

    blk = (N_CHIPS, tm, c)
    own = pl.BlockSpec(blk, lambda i, cr: (0, cr[0] * nh + i, 0))
    half = pl.BlockSpec(blk, lambda i, cr: (0, i, 0))
    return pl.pallas_call(
        body, grid_spec=pltpu.PrefetchScalarGridSpec(
            num_scalar_prefetch=1, grid=(nh,), in_specs=[own] * n + [half] * n, out_specs=[half] * n),
        out_shape=[_sds(t.shape, BF16) for t in theirs], compiler_params=_params(("parallel",)),
        name="pair_sum")(core, *gs, *theirs)


def chip_sum(own, landed, chip, stack, layer, layers, tm=256):
    _, r, c = own.shape
    tm = _tile(r, tm)

    def body(chip_ref, own_ref, l_ref, *rest):
        acc = own_ref[...].astype(F32)
        for j in range(N_CHIPS - 1):
            acc = acc + l_ref[j].astype(F32)
        rest[-1][...] = acc

    in_specs = [pl.BlockSpec((None, tm, c), lambda i, qr: (qr[0], i, 0)),
                pl.BlockSpec((N_CHIPS - 1, tm, c), lambda i, qr: (0, i, 0))]
    args = [chip, own, landed]
    if stack is not None:
        in_specs.append(ANY)
        args.append(stack)
    return pl.pallas_call(
        body, grid_spec=pltpu.PrefetchScalarGridSpec(
            num_scalar_prefetch=1, grid=(r // tm,), in_specs=in_specs,
            out_specs=pl.BlockSpec((None, tm, c), lambda i, qr: (layer, i, 0))),
        out_shape=_sds((layers, r, c), F32), input_output_aliases={3: 0} if stack is not None else {},
        compiler_params=_params(("parallel",)), name="chip_sum")(*args)


def _adamw_math(w, g, m, v):
    bc1 = 1.0 - ADAM_B1 ** ADAM_STEP
    bc2 = 1.0 - ADAM_B2 ** ADAM_STEP
    nm = ADAM_B1 * m + (1.0 - ADAM_B1) * g
    nv = ADAM_B2 * v + (1.0 - ADAM_B2) * (g * g)
    return -ADAM_LR * ((nm / bc1) / (jnp.sqrt(nv / bc2) + ADAM_EPS) + ADAM_WD * w), nm, nv


def vector_update(red, chip, ws, ms, vs, where):
    n = len(ws)
    dd = red.shape[1]

    def body(chip_ref, red_ref, *refs):
        w_r, m_r, v_r = refs[0:n], refs[n:2 * n], refs[2 * n:3 * n]
        g_o, d_o, m_o, v_o = (refs[(3 + k) * n:(4 + k) * n] for k in range(4))
        q = chip_ref[0]

        def chip_block(val, width):
            out = val[:, 0:width]
            for p in range(1, val.shape[1] // width):
                out = jnp.where(q == p, val[:, p * width:(p + 1) * width], out)
            return out

        for k in range(n):
            for idx, r0, nr, cols in where[k]:
                width = w_r[k].shape[-1]
                if cols == "chip" and width * N_CHIPS != dd:
                    g = chip_block(jnp.concatenate([red_ref[pl.ds(r0 + j, 1), :] for j in range(nr)], axis=1), width)
                else:
                    g = red_ref[pl.ds(r0, nr), :]
                    g = chip_block(g, width) if cols == "chip" else g if cols == "all" else g[:, 0:cols]
                delta, nm, nv = _adamw_math(w_r[k][idx], g, m_r[k][idx], v_r[k][idx])
                g_o[k][idx] = g
                d_o[k][idx] = delta
                m_o[k][idx] = nm
                v_o[k][idx] = nv

    vm = pl.BlockSpec(memory_space=pltpu.VMEM)
    outs = pl.pallas_call(
        body, in_specs=[pl.BlockSpec(memory_space=pltpu.SMEM), vm] + [vm] * (3 * n), out_specs=[vm] * (4 * n),
        out_shape=[_sds(w.shape, F32) for w in ws] * 4, name="vector_update")(chip, red, *ws, *ms, *vs)
    return [outs[k * n:(k + 1) * n] for k in range(4)]


def adamw_joined(w, m, v, g_mine, g_theirs, core, tm=512):
    nl, r, c = w.shape
    tm = _tile(r // 2, tm)
    nh = r // 2 // tm

    def body(core_ref, w_ref, m_ref, v_ref, gm_ref, gt_ref, g_ref, d_ref, nm_ref, nv_ref):
        mine = (pl.program_id(1) // nh) == core_ref[0]
        gv = jnp.where(mine, gm_ref[...], gt_ref[...])
        g_ref[...] = gv
        d_ref[...], nm_ref[...], nv_ref[...] = _adamw_math(w_ref[...], gv, m_ref[...], v_ref[...])

    full = pl.BlockSpec((None, tm, c), lambda l, i, cr: (l, i, 0))
    mine = pl.BlockSpec((None, tm, c), lambda l, i, cr: (l, jnp.where(i // nh == cr[0], i % nh, 0), 0))
    theirs = pl.BlockSpec((None, tm, c), lambda l, i, cr: (l, jnp.where(i // nh == cr[0], 0, i % nh), 0))
    return pl.pallas_call(
        body, grid_spec=pltpu.PrefetchScalarGridSpec(
            num_scalar_prefetch=1, grid=(nl, r // tm), in_specs=[full, full, full, mine, theirs], out_specs=[full] * 4),
        out_shape=[_sds((nl, r, c), F32)] * 4, compiler_params=_params(("parallel", "parallel")),
        name="adamw_joined")(core, w, m, v, g_mine, g_theirs)


WEIGHTS = ['sc_w_in', 'sc_conv_w', 'sc_w_out', 'mla_w_dq', 'mla_g_q', 'mla_w_uq', 'mla_w_dkv', 'mla_g_kv', 'mla_w_uk',
           'mla_w_uv', 'mla_w_o', 'cf_w_pw1', 'cf_b_pw1', 'cf_dw_w', 'cf_dw_b', 'cf_norm_g', 'cf_norm_b', 'cf_w_pw2',
           'cf_b_pw2', 'ff_w1', 'ff_w2', 'ln_mix_g', 'ln_mix_b', 'ln_ff_g', 'ln_ff_b']
ARGS = ['x'] + WEIGHTS + ['loss_target'] + ['m_' + n for n in WEIGHTS] + ['v_' + n for n in WEIGHTS]


def _sq_relu(h):
    r = jnp.maximum(h, jnp.zeros_like(h))
    return r * r


def _mlp_forward(i, x, xb, w1, w2, g, b):
    hb = mm_plain_nn(f"mlp{i}_up", xb, w1, BF16, tm=2048, tn=1024)
    y, yb, xh, rstd = mm_residual_ln(f"mlp{i}_down_ln", hb, w2, x, g, b, tk=4096, a_fn=_sq_relu)
    return (y, yb), dict(xb=xb, hb=hb, xh=xh, rstd=rstd, g=g)


def _mlp_backward(i, dr, drb, sv, w1, w2, dw1, dw2, reduce_after, mixer_ln):
    s = dr.shape[0]
    tm, tn = _tile(s, 1024), 1024

    def epi(acc, e, o):
        o[0][...] = (acc * (2.0 * jnp.maximum(e[0][...].astype(F32), 0.0))).astype(BF16)

    dhb = mm_nt(f"mlp{i}_down_bwd", drb, w2, s, tm, tn, 1024, epi, [_sds((s, w2.k), BF16)], [_ij(tm, tn)],
                [sv["hb"]], [_ij(tm, tn)])[0]
    g_w2 = mm_tn(f"mlp{i}_dw2", sv["hb"], drb, dw2, s, 1024, 1024, a_fn=_sq_relu)
    g_w1 = mm_tn(f"mlp{i}_dw1", sv["xb"], dhb, dw1, s, 1024, 1024)
    dhb = reduce_after(dhb, {f"w1_{i}": g_w1, f"w2_{i}": g_w2})
    return mm_nt_ln_backward(f"mlp{i}_up_bwd", dhb, w1, dr, *mixer_ln, tk=4096)


def kernel(x, sc_w_in, sc_conv_w, sc_w_out, mla_w_dq, mla_g_q, mla_w_uq, mla_w_dkv, mla_g_kv, mla_w_uk, mla_w_uv, mla_w_o, cf_w_pw1, cf_b_pw1, cf_dw_w, cf_dw_b, cf_norm_g, cf_norm_b, cf_w_pw2, cf_b_pw2, ff_w1, ff_w2, ln_mix_g, ln_mix_b, ln_ff_g, ln_ff_b, loss_target, m_sc_w_in, m_sc_conv_w, m_sc_w_out, m_mla_w_dq, m_mla_g_q, m_mla_w_uq, m_mla_w_dkv, m_mla_g_kv, m_mla_w_uk, m_mla_w_uv, m_mla_w_o, m_cf_w_pw1, m_cf_b_pw1, m_cf_dw_w, m_cf_dw_b, m_cf_norm_g, m_cf_norm_b, m_cf_w_pw2, m_cf_b_pw2, m_ff_w1, m_ff_w2, m_ln_mix_g, m_ln_mix_b, m_ln_ff_g, m_ln_ff_b, v_sc_w_in, v_sc_conv_w, v_sc_w_out, v_mla_w_dq, v_mla_g_q, v_mla_w_uq, v_mla_w_dkv, v_mla_g_kv, v_mla_w_uk, v_mla_w_uv, v_mla_w_o, v_cf_w_pw1, v_cf_b_pw1, v_cf_dw_w, v_cf_dw_b, v_cf_norm_g, v_cf_norm_b, v_cf_w_pw2, v_cf_b_pw2, v_ff_w1, v_ff_w2, v_ln_mix_g, v_ln_mix_b, v_ln_ff_g, v_ln_ff_b):
    given = dict(zip(ARGS, (x, sc_w_in, sc_conv_w, sc_w_out, mla_w_dq, mla_g_q, mla_w_uq, mla_w_dkv, mla_g_kv, mla_w_uk, mla_w_uv, mla_w_o, cf_w_pw1, cf_b_pw1, cf_dw_w, cf_dw_b, cf_norm_g, cf_norm_b, cf_w_pw2, cf_b_pw2, ff_w1, ff_w2, ln_mix_g, ln_mix_b, ln_ff_g, ln_ff_b, loss_target, m_sc_w_in, m_sc_conv_w, m_sc_w_out, m_mla_w_dq, m_mla_g_q, m_mla_w_uq, m_mla_w_dkv, m_mla_g_kv, m_mla_w_uk, m_mla_w_uv, m_mla_w_o, m_cf_w_pw1, m_cf_b_pw1, m_cf_dw_w, m_cf_dw_b, m_cf_norm_g, m_cf_norm_b, m_cf_w_pw2, m_cf_b_pw2, m_ff_w1, m_ff_w2, m_ln_mix_g, m_ln_mix_b, m_ln_ff_g, m_ln_ff_b, v_sc_w_in, v_sc_conv_w, v_sc_w_out, v_mla_w_dq, v_mla_g_q, v_mla_w_uq, v_mla_w_dkv, v_mla_g_kv, v_mla_w_uk, v_mla_w_uv, v_mla_w_o, v_cf_w_pw1, v_cf_b_pw1, v_cf_dw_w, v_cf_dw_b, v_cf_norm_g, v_cf_norm_b, v_cf_w_pw2, v_cf_b_pw2, v_ff_w1, v_ff_w2, v_ln_mix_g, v_ln_mix_b, v_ln_ff_g, v_ln_ff_b)))
    s, d = x.shape[1], x.shape[2]
    d_ff = 4 * d
    dq4 = d // N_CHIPS
    xq = lax.axis_index("x") * 2 + lax.axis_index("y")

    w_dkv_pad = jnp.pad(mla_w_dkv[0], ((0, 0), (0, 128 - QK_ROPE)))
    w_uq_pad = jnp.pad(mla_w_uq[0].reshape(Q_LORA, 2, QK_NOPE + QK_ROPE), ((0, 0), (0, 0), (0, HEAD_PAD - QK_NOPE - QK_ROPE)))
    small = pack_rows("vector_weights_pack", [
        sc_conv_w.reshape(2 * SC_WIDTH, dq4), cf_b_pw1.reshape(2, dq4), cf_dw_w[0], cf_dw_b, cf_norm_g, cf_norm_b,
        cf_b_pw2], 64)
    mlp_w = lambda i: [ff_w1[i].astype(BF16), ff_w2[i].astype(BF16)]
    g_in, g_out, g_w1, g_w2 = [None] * 2, [None] * 2, [None] * DEPTH, [None] * DEPTH
    g_in[0], g_out[0], g_small = gather_shards(
        "gather_mixer0", [sc_w_in[0].astype(BF16), sc_w_out[0].astype(BF16), small], by_columns=(0,))
    (g_w1[0],) = gather_shards("gather_up0", [ff_w1[0].astype(BF16)], by_columns=(0,))
    (g_w2[0],) = gather_shards("gather_down0", [ff_w2[0].astype(BF16)])
    g_dqkv, g_uq, g_uk, g_uv, g_o = gather_shards("gather_mixer1", [
        jnp.concatenate([mla_w_dq[0], w_dkv_pad], axis=1).astype(BF16),
        w_uq_pad.reshape(Q_LORA, 2 * HEAD_PAD).astype(BF16),
        mla_w_uk.reshape(KV_LORA // N_CHIPS, N_HEADS * QK_NOPE).astype(BF16),
        mla_w_uv.reshape(KV_LORA // N_CHIPS, N_HEADS * V_HEAD).astype(BF16), mla_w_o[0].astype(BF16)], by_columns=(1,))
    g_w1[1], g_w2[1] = gather_shards("gather_mlp1", mlp_w(1), by_columns=(0,))
    g_pw1, g_pw2, g_w1[2], g_w2[2] = gather_shards(
        "gather_layer2", [cf_w_pw1[0].astype(BF16), cf_w_pw2[0].astype(BF16)] + mlp_w(2), by_columns=(0, 2))
    g_in[1], g_out[1], g_w1[3], g_w2[3] = gather_shards(
        "gather_layer3", [sc_w_in[1].astype(BF16), sc_w_out[1].astype(BF16)] + mlp_w(3), by_columns=(0, 2))

    wd_t = Q_LORA + KV_LORA + 128
    w_in = [Stk("full", d, 3 * d, g_in[j]) for j in range(2)]
    w_out = [Stk("row", d, d, g_out[j]) for j in range(2)]
    w_dqkv = Stk("row", d, wd_t, g_dqkv)
    w_uq = Stk("full", Q_LORA, N_HEADS * HEAD_PAD, g_uq)
    w_uk = Stk("row", KV_LORA, N_HEADS * QK_NOPE, g_uk)
    w_uv = Stk("row", KV_LORA, N_HEADS * V_HEAD, g_uv)
    w_o = Stk("row", d, d, g_o)
    w_pw1 = Stk("full", d, 2 * d, g_pw1)
    w_pw2 = Stk("row", d, d, g_pw2)
    w_1 = [Stk("full", d, d_ff, g_w1[i]) for i in range(DEPTH)]
    w_2 = [Stk("row", d_ff, d, g_w2[i]) for i in range(DEPTH)]

    def wide(rows):
        return jnp.swapaxes(rows, 0, 1).reshape(rows.shape[1], d)

    conv_w = wide(g_small[:, 0:6]).reshape(2, SC_WIDTH, d)
    b_pw1 = g_small[:, 6:8].reshape(1, 2 * d)
    dw_w = wide(g_small[:, 8:39])
    dw_b, norm_g, norm_b, b_pw2 = (wide(g_small[:, 39 + k:40 + k]) for k in range(4))

    pos = jnp.arange(s, dtype=F32)
    inv_freq = ROPE_THETA ** (-jnp.arange(0, QK_ROPE, 2, dtype=F32) / QK_ROPE)
    ang = pos[:, None] * inv_freq[None, :]
    cos, sin, zero = jnp.cos(ang), jnp.sin(ang), jnp.zeros((s, 128 - QK_ROPE), F32)
    cf = jnp.concatenate([cos, cos, zero], axis=1)
    sf = jnp.concatenate([-sin, sin, zero], axis=1)

    def row(a, i):
        return a[i:i + 1]

    xs = x.reshape(s, d)
    cur = (xs, xs.astype(BF16))
    tape = []
    for i in range(DEPTH):
        mixer, j = i % 3, i // 3
        xf, xb = cur
        lg, lb = row(ln_mix_g, i), row(ln_mix_b, i)
        if mixer == 0:
            u = mm_plain_nn(f"sc{j}_in", xb, w_in[j], F32, tn=3 * dq4)
            gb = short_conv_gate(u, conv_w[j])
            y, yb, xh, rstd = mm_residual_ln(f"sc{j}_out_ln", gb, w_out[j], xf, lg, lb)
            sv = dict(xb=xb, u=u, gb=gb)
        elif mixer == 1:
            t = mm_plain_nn("mla_down", xb, w_dqkv, F32, tn=wd_t // 2)
            cq, ckv, kpe = mla_latents(t, mla_g_q, mla_g_kv, cf, sf)
            qh = mla_queries(cq, w_uq, cf, sf)
            kh = mla_keys(ckv, w_uk, kpe)
            vh = mm_plain_nn("mla_values", ckv, w_uv, BF16, tk=KV_LORA)
            oh = attention(qh, kh, vh)
            y, yb, xh, rstd = mm_residual_ln("mla_out_ln", oh, w_o, xf, lg, lb)
            sv = dict(xb=xb, t=t, cq=cq, ckv=ckv, qh=qh, kh=kh, vh=vh, oh=oh)
        else:
            u = mm_plain_nn("cf_pw1", xb, w_pw1, F32, bias=b_pw1)
            hc = conformer_glu_conv(u, dw_w, dw_b)
            sb = conformer_norm_swish(hc, norm_g, norm_b)
            y, yb, xh, rstd = mm_residual_ln("cf_pw2_ln", sb, w_pw2, xf, lg, lb, bias=b_pw2)
            sv = dict(xb=xb, u=u, hc=hc, sb=sb)
        sv.update(xh=xh, rstd=rstd, g=lg)
        cur, sv_mlp = _mlp_forward(i, y, yb, w_1[i], w_2[i], row(ln_ff_g, i), row(ln_ff_b, i))
        tape.append((sv, sv_mlp))

    g_ln = {n: [None] * DEPTH for n in ("ln_mix_g", "ln_mix_b", "ln_ff_g", "ln_ff_b")}
    last = tape[DEPTH - 1][1]
    dr, drb, g_ln["ln_ff_g"][DEPTH - 1], g_ln["ln_ff_b"][DEPTH - 1], _, loss_part = loss_ln_backward(
        cur[0], loss_target.reshape(s, d), last["xh"], last["rstd"], last["g"])

    grads = {}
    smalls = {}
    conv_grads = [None, None]
    core = lax.axis_index("c").astype(jnp.int32).reshape(1)
    chip = xq.astype(jnp.int32).reshape(1)
    pairs, landed = {}, {}
    ready, theirs = [], {}

    def hold(xs, others):
        live = [x for x in xs if x is not None]
        out = lax.optimization_barrier((*live, *others))
        rest = iter(out[:len(live)])
        return tuple(None if x is None else next(rest) for x in xs), list(out[len(live):])

    def reduce_after(x, new, early=False):
        out = lax.optimization_barrier((x, *new.values()))
        grads.update(zip(new, out[1:]))
        if early:
            theirs.update(zip(new, pair_exchange(f"pair_exchange_{len(theirs)}", list(out[1:]), True)))
        ready.extend(new)
        return out[0]

    def reduce_layer(i, x):
        late = [n for n in ready if n not in theirs]
        if late:
            theirs.update(zip(late, pair_exchange(f"pair_exchange_layer{i}", [grads[n] for n in late], False)))
        by_shape = {}
        for n in ready:
            by_shape.setdefault(grads[n].shape, []).append(n)
        for names in by_shape.values():
            pairs.update(zip(names, pair_sum([grads[n] for n in names], [theirs[n] for n in names], core)))
        sums = [pairs[n] for n in ready]
        landed.update(zip(ready, chip_exchange(f"chip_exchange_layer{i}", sums)))
        exchanged.append(list(ready))
        ready.clear()
        return hold(x, sums)[0]

    groups = [["in_0", "in_1"], ["out_0", "out_1"], ["dqkv"], ["uq"], ["uk"], ["uv"], ["o"], ["pw1"], ["pw2"],
              [f"w1_{i}" for i in range(DEPTH)], [f"w2_{i}" for i in range(DEPTH)]]
    stacks = [None] * len(groups)
    exchanged = []

    def sum_layer(x, last=False):
        names = exchanged.pop(0)
        if last:
            x, held = hold(x, [landed[n] for n in names])
            landed.update(zip(names, held))
        new = []
        for n in names:
            k = next(k for k, members in enumerate(groups) if n in members)
            stacks[k] = chip_sum(pairs[n], landed[n], chip, stacks[k], groups[k].index(n), len(groups[k]))
            new.append(stacks[k])
        return x if last else hold(x, new)[0]

    for i in reversed(range(DEPTH)):
        mixer, j = i % 3, i // 3
        sv, sv_mlp = tape[i]
        dr, drb, g_ln["ln_mix_g"][i], g_ln["ln_mix_b"][i], dr_sum = _mlp_backward(
            i, dr, drb, sv_mlp, w_1[i], w_2[i], Stk("col", d, d_ff), Stk("row", d_ff, d),
            lambda x_, new: reduce_after(x_, new, early=i > 0), (sv["xh"], sv["rstd"], sv["g"]))
        if i == 0:
            dr, drb = reduce_layer("0_mlp", (dr, drb))

        def to_input(name, a, w, tk, a_spec_fn=None):
            if i == 0:
                spec = None if a_spec_fn is None else (s, a_spec_fn)
                return mm_plain_nt(name, a, w, F32, tn=1024, tk=tk, add=dr, add_scale=ALPHA, a_spec_fn=spec), None
            prev = tape[i - 1][1]
            out = mm_nt_ln_backward(name, a, w, dr, prev["xh"], prev["rstd"], prev["g"], tk=tk, a_spec_fn=a_spec_fn)
            g_ln["ln_ff_g"][i - 1], g_ln["ln_ff_b"][i - 1] = out[2], out[3]
            return out[0], out[1]

        parts_of = lambda tm, tk: pl.BlockSpec((None, tm, tk), lambda i_, j_, k_: (k_, i_, 0))
        if mixer == 0:
            dgate = mm_plain_nt(f"sc{j}_out_bwd", drb, w_out[j], F32)
            dw_out = mm_tn(f"sc{j}_dw_out", sv["gb"], drb, Stk("row", d, d), s, 512, 1024)
            du, conv_grads[j] = short_conv_gate_bwd(sv["u"], conv_w[j], dgate)
            nb = d // 256
            dw_in = mm_tn(
                f"sc{j}_dw_in", sv["xb"], du, Stk("col", d, 3 * d), s, 1024, 256,
                b_spec=pl.BlockSpec((None, s, 256), lambda i_, j_, k_: (j_ // nb, k_, j_ % nb)))
            du = reduce_after(du, {f"in_{j}": dw_in, f"out_{j}": dw_out})
            dr, drb = to_input(f"sc{j}_in_bwd", du, w_in[j], d, parts_of)
        elif mixer == 1:
            do = mm_plain_nt("mla_out_bwd", drb, w_o, BF16)
            g_o = mm_tn("mla_dw_o", sv["oh"], drb, Stk("row", d, d), s, 512, 1024)
            dqh, dkh, dvh = attention_bwd(sv["qh"], sv["kh"], sv["vh"], do)
            dql, dkn, dkpe = mla_unrope_grads(dqh, dkh, cf, sf)
            g_uq = mm_tn("mla_dw_uq", sv["cq"], dql, Stk("col", Q_LORA, N_HEADS * HEAD_PAD), s, Q_LORA, 512)
            dcq = mm_plain_nt("mla_uq_bwd", dql, w_uq, F32, tn=Q_LORA)
            g_uk = mm_tn("mla_dw_uk", sv["ckv"], dkn, Stk("row", KV_LORA, N_HEADS * QK_NOPE), s, KV_LORA, 1024)
            g_uv = mm_tn("mla_dw_uv", sv["ckv"], dvh, Stk("row", KV_LORA, N_HEADS * V_HEAD), s, KV_LORA, 1024)
            dckv = mm_plain_nt("mla_uk_bwd", dkn, w_uk, F32, tn=KV_LORA)
            dckv = mm_plain_nt("mla_uv_bwd", dvh, w_uv, F32, tn=KV_LORA, add=dckv)
            dt, smalls["g_q"], smalls["g_kv"] = mla_latents_bwd(sv["t"], mla_g_q, mla_g_kv, cf, sf, dcq, dckv, dkpe)
            g_dqkv = mm_tn("mla_dw_down", sv["xb"], dt, Stk("row", d, wd_t), s, 512, wd_t)
            dt = reduce_after(dt, {"dqkv": g_dqkv, "uq": g_uq, "uk": g_uk, "uv": g_uv, "o": g_o})
            dr, drb = to_input("mla_down_bwd", dt, w_dqkv, wd_t)
        else:
            dsw = mm_plain_nt("cf_pw2_bwd", drb, w_pw2, F32)
            g_pw2 = mm_tn("cf_dw_pw2", sv["sb"], drb, Stk("row", d, d), s, 512, 1024)
            smalls["b_pw2"] = dr_sum
            dhc, smalls["norm_g"], smalls["norm_b"] = conformer_norm_swish_bwd(sv["hc"], norm_g, norm_b, dsw)
            du, smalls["b_pw1"], smalls["dw_w"], smalls["dw_b"] = conformer_glu_conv_bwd(sv["u"], dw_w, dhc)
            nb = d // 512
            g_pw1 = mm_tn(
                "cf_dw_pw1", sv["xb"], du, Stk("col", d, 2 * d), s, 1024, 512,
                b_spec=pl.BlockSpec((None, s, 512), lambda i_, j_, k_: (j_ // nb, k_, j_ % nb)))
            du = reduce_after(du, {"pw1": g_pw1, "pw2": g_pw2})
            dr, drb = to_input("cf_pw1_bwd", du, w_pw1, d, parts_of)
        if i < DEPTH - 1:
            dr, drb = sum_layer((dr, drb))
        dr, drb = reduce_layer(i, (dr, drb))
    grad_x = sum_layer(sum_layer((dr, None), last=True), last=True)[0].reshape(1, s, d)

    mine = stacks
    other = (pair_share("pair_share_mixers", mine[:9]) + pair_share("pair_share_up", mine[9:10])
             + pair_share("pair_share_down", mine[10:]))

    def padded(get):
        dqkv = jnp.concatenate([get("mla_w_dq")[0], jnp.pad(get("mla_w_dkv")[0], ((0, 0), (0, 128 - QK_ROPE)))], axis=1)
        uq = jnp.pad(get("mla_w_uq")[0].reshape(Q_LORA, 2, QK_NOPE + QK_ROPE),
                     ((0, 0), (0, 0), (0, HEAD_PAD - QK_NOPE - QK_ROPE))).reshape(Q_LORA, 2 * HEAD_PAD)
        return [get("sc_w_in"), get("sc_w_out"), dqkv[None], uq[None],
                get("mla_w_uk").reshape(1, KV_LORA // N_CHIPS, d), get("mla_w_uv").reshape(1, KV_LORA // N_CHIPS, d),
                get("mla_w_o"), get("cf_w_pw1"), get("cf_w_pw2"), get("ff_w1"), get("ff_w2")]

    w_l, m_l, v_l = (padded(lambda n, p=p: given[p + n]) for p in ("", "m_", "v_"))
    res = [adamw_joined(w_l[k], m_l[k], v_l[k], mine[k], other[k], core) for k in range(len(groups))]

    def unpadded(k):
        r_in, r_out, r_dqkv, r_uq, r_uk, r_uv, r_o, r_pw1, r_pw2, r_w1, r_w2 = (r[k] for r in res)
        return {
            "sc_w_in": r_in, "sc_w_out": r_out, "mla_w_dq": r_dqkv[:, :, 0:Q_LORA],
            "mla_w_dkv": r_dqkv[:, :, Q_LORA:Q_LORA + KV_LORA + QK_ROPE],
            "mla_w_uq": r_uq.reshape(1, Q_LORA, 2, HEAD_PAD)[:, :, :, 0:QK_NOPE + QK_ROPE].reshape(mla_w_uq.shape),
            "mla_w_uk": r_uk.reshape(mla_w_uk.shape), "mla_w_uv": r_uv.reshape(mla_w_uv.shape),
            "mla_w_o": r_o, "cf_w_pw1": r_pw1, "cf_w_pw2": r_pw2, "ff_w1": r_w1, "ff_w2": r_w2}

    big_g, big_d, big_m, big_v = (unpadded(k) for k in range(4))

    pad_row = lambda a: jnp.pad(a, ((0, 0), (0, d - a.shape[1])))
    small_parts = ([g for n in ("ln_mix_g", "ln_mix_b", "ln_ff_g", "ln_ff_b") for g in g_ln[n]]
                   + [pad_row(smalls["g_q"]), pad_row(smalls["g_kv"]), conv_grads[0], conv_grads[1],
                      smalls["b_pw1"].reshape(2, d), smalls["dw_w"], smalls["dw_b"], smalls["norm_g"], smalls["norm_b"],
                      smalls["b_pw2"], loss_part])
    red = all_reduce_small(small_parts, 64)
    loss = red[61, 0]

    where = {
        "ln_mix_g": [((), 0, DEPTH, "all")], "ln_mix_b": [((), 4, DEPTH, "all")],
        "ln_ff_g": [((), 8, DEPTH, "all")], "ln_ff_b": [((), 12, DEPTH, "all")],
        "mla_g_q": [((), 16, 1, Q_LORA)], "mla_g_kv": [((), 17, 1, KV_LORA)],
        "sc_conv_w": [((0,), 18, SC_WIDTH, "chip"), ((1,), 21, SC_WIDTH, "chip")],
        "cf_b_pw1": [((), 24, 2, "chip")], "cf_dw_w": [((0,), 26, CONF_WIDTH, "chip")],
        "cf_dw_b": [((), 57, 1, "chip")], "cf_norm_g": [((), 58, 1, "chip")], "cf_norm_b": [((), 59, 1, "chip")],
        "cf_b_pw2": [((), 60, 1, "chip")]}
    vec = list(where)
    vec_res = vector_update(red, chip, [given[n] for n in vec], [given["m_" + n] for n in vec],
                            [given["v_" + n] for n in vec], [where[n] for n in vec])
    gw = dict(big_g)
    upd = {n: [big_d[n], big_m[n], big_v[n]] for n in big_g}
    for k, n in enumerate(vec):
        gw[n] = vec_res[0][k]
        upd[n] = [vec_res[1][k], vec_res[2][k], vec_res[3][k]]

    return (loss, grad_x, *[gw[n] for n in WEIGHTS], *[upd[n][0] for n in WEIGHTS],
            *[upd[n][1] for n in WEIGHTS], *[upd[n][2] for n in WEIGHTS])
```

```python
import jax
import jax.numpy as jnp
from jax import lax
from jax.experimental import pallas as pl
from jax.experimental.pallas import tpu as pltpu
from jax.experimental.pallas import tpu_sc as plsc

F32 = jnp.float32
BF16 = jnp.bfloat16
MESH = pl.DeviceIdType.MESH

DEPTH = 4
ALPHA = (2.0 * DEPTH) ** 0.25
LN_EPS = 1e-5
RMS_EPS = 1e-6
CHUNK_SHIFT = 6
N_HEADS = 8
QK_NOPE = 128
QK_ROPE = 64
V_HEAD = 128
HEAD_PAD = 256
Q_LORA = 384
KV_LORA = 256
ROPE_THETA = 10000.0
SC_WIDTH = 3
CONF_WIDTH = 31
CONV_PAD = 32
CONV_CHUNK = 64
N_CHIPS = 4
ATTN_SCALE = (QK_NOPE + QK_ROPE) ** -0.5

ADAM_LR = 0.001
ADAM_B1 = 0.9
ADAM_B2 = 0.999
ADAM_EPS = 1e-08
ADAM_WD = 0.01
ADAM_STEP = 10

VMEM_LIMIT = 56 * 2**20

NN = (((1,), (0,)), ((), ()))
NT = (((1,), (1,)), ((), ()))
TN = (((0,), (0,)), ((), ()))


def _params(sem=None):
    return pltpu.CompilerParams(dimension_semantics=sem, vmem_limit_bytes=VMEM_LIMIT)


class Stk:
    def __init__(self, kind, k, n, arr=None):
        self.kind, self.k, self.n = kind, k, n
        self.plain = kind != "col"
        self.nloc = n // N_CHIPS if kind == "col" else n
        self.arr = arr.reshape(k, n) if arr is not None and self.plain else arr

    @property
    def shape(self):
        return (self.k, self.n) if self.plain else (N_CHIPS, self.k, self.nloc)

    def spec(self, bk, bn, f, resident=False):
        if self.plain:
            return pl.BlockSpec((bk, bn), f, pipeline_mode=pl.Buffered(1)) if resident else pl.BlockSpec((bk, bn), f)
        assert self.k % bk == 0 and self.nloc % bn == 0, (self.k, bk, self.nloc, bn)
        pn = self.nloc // bn

        def imap(*g):
            kb, nb = f(*g)
            return nb // pn, kb, nb % pn

        return pl.BlockSpec((None, bk, bn), imap)


def _mm(name, mode, a, b, grid, a_spec, b_spec, acc_shape, extras, extra_specs, out_shapes, out_specs, epi, a_fn=None,
        rows_in_order=False):
    nk = grid[2]
    ne = len(extras)

    def body(*refs):
        a_ref, b_ref = refs[0], refs[1]
        e_refs = refs[2:2 + ne]
        av = a_ref[...] if a_fn is None else a_fn(a_ref[...])
        part = lax.dot_general(av, b_ref[...], mode, preferred_element_type=F32)
        if nk == 1:
            epi(part, e_refs, refs[2 + ne:])
            return
        o_refs = refs[2 + ne:-1]
        acc = refs[-1]
        k = pl.program_id(2)

        @pl.when(k == 0)
        def _():
            acc[...] = part

        @pl.when(k > 0)
        def _():
            acc[...] += part

        @pl.when(k == nk - 1)
        def _():
            epi(acc[...], e_refs, o_refs)

    return pl.pallas_call(
        body, grid=grid, in_specs=[a_spec, b_spec, *extra_specs], out_specs=out_specs, out_shape=out_shapes,
        scratch_shapes=[pltpu.VMEM(acc_shape, F32)] if nk > 1 else [],
        compiler_params=_params(("arbitrary",) * 3 if rows_in_order else ("parallel", "parallel", "arbitrary")),
        name=name)(a, b, *extras)


def _tile(n, t):
    t = min(n, t)
    while n % t:
        t -= 8
    assert t > 0, (n, t)
    return t


def mm_nn(name, a, w, tm, tn, tk, epi, out_shapes, out_specs, extras=(), extra_specs=(), a_spec=None, a_fn=None):
    m = a.shape[0]
    tm, tn, tk = _tile(m, tm), _tile(w.n, tn), _tile(w.k, tk)
    grid = (m // tm, w.n // tn, w.k // tk)
    a_spec = a_spec or pl.BlockSpec((tm, tk), lambda i, j, k: (i, k))
    b_spec = w.spec(tk, tn, lambda i, j, k: (k, j))
    return _mm(name, NN, a, w.arr, grid, a_spec, b_spec, (tm, tn), extras, extra_specs, out_shapes, out_specs, epi, a_fn)


def mm_nt(name, a, w, m, tm, tn, tk, epi, out_shapes, out_specs, extras=(), extra_specs=(), a_spec=None,
          rows_in_order=False, a_fn=None):
    tm, tn, tk = _tile(m, tm), _tile(w.k, tn), _tile(w.n, tk)
    grid = (m // tm, w.k // tn, w.n // tk)
    a_spec = a_spec or pl.BlockSpec((tm, tk), lambda i, j, k: (i, k))
    b_spec = w.spec(tn, tk, lambda i, j, k: (j, k), resident=grid[1] == 1 and grid[2] == 1)
    return _mm(name, NT, a, w.arr, grid, a_spec, b_spec, (tm, tn), extras, extra_specs, out_shapes, out_specs, epi,
               a_fn=a_fn, rows_in_order=rows_in_order)


def mm_tn(name, a, b, dw, s, tm=512, tn=512, tk=4096, a_spec=None, b_spec=None, a_fn=None):
    tm, tn, tk = _tile(dw.k, tm), _tile(dw.n, tn), _tile(s, tk)
    grid = (dw.k // tm, dw.n // tn, s // tk)
    a_spec = a_spec or pl.BlockSpec((tk, tm), lambda i, j, k: (k, i))
    b_spec = b_spec or pl.BlockSpec((tk, tn), lambda i, j, k: (k, j))

    def epi(acc, e, o):
        o[0][...] = acc.astype(BF16)

    out = _mm(name, TN, a, b, grid, a_spec, b_spec, (tm, tn), (), (), [jax.ShapeDtypeStruct(dw.shape, BF16)],
              [dw.spec(tm, tn, lambda i, j, k: (i, j))], epi, a_fn)[0]
    return out.reshape(N_CHIPS, dw.k // N_CHIPS, dw.n) if dw.plain else out


def _sds(shape, dtype):
    return jax.ShapeDtypeStruct(shape, dtype)


def _ij(tm, tn):
    return pl.BlockSpec((tm, tn), lambda i, j, k: (i, j))


def _i0(tm, c):
    return pl.BlockSpec((tm, c), lambda i, j, k: (i, 0))


def _0j(r, tn):
    return pl.BlockSpec((r, tn), lambda i, j, k: (0, j))


def _layer_norm_rows(r, g, b):
    mu = jnp.mean(r, axis=-1, keepdims=True)
    d = r - mu
    var = jnp.mean(d * d, axis=-1, keepdims=True)
    rstd = lax.rsqrt(var + LN_EPS)
    xh = d * rstd
    return xh * g + b, xh, rstd


def mm_residual_ln(name, a, w, x, g, b, bias=None, tm=512, tk=1024, a_fn=None):
    s, d = x.shape
    tm = _tile(s, tm)
    extras = [x, g, b] + ([bias] if bias is not None else [])
    especs = [_i0(tm, d), _0j(1, d), _0j(1, d)] + ([_0j(1, d)] if bias is not None else [])

    def epi(acc, e, o):
        r = ALPHA * e[0][...] + acc
        if bias is not None:
            r = r + e[3][...]
        y, xh, rstd = _layer_norm_rows(r, e[1][...], e[2][...])
        o[0][...] = y
        o[1][...] = y.astype(BF16)
        o[2][...] = xh
        o[3][...] = rstd

    return mm_nn(name, a, w, tm, d, tk, epi,
                 [_sds((s, d), F32), _sds((s, d), BF16), _sds((s, d), F32), _sds((s, 1), F32)],
                 [_i0(tm, d), _i0(tm, d), _i0(tm, d), _i0(tm, 1)], extras, especs, a_fn=a_fn)


def mm_plain_nn(name, a, w, out_dtype, tm=1024, tn=512, tk=1024, bias=None):
    m = a.shape[0]
    tm, tn = _tile(m, tm), _tile(w.n, tn)

    def epi(acc, e, o):
        if bias is not None:
            acc = acc + e[0][...]
        o[0][...] = acc.astype(out_dtype)

    extras, especs = ([bias], [_0j(1, tn)]) if bias is not None else ((), ())
    return mm_nn(name, a, w, tm, tn, tk, epi, [_sds((m, w.n), out_dtype)], [_ij(tm, tn)], extras, especs)[0]


def mm_plain_nt(name, a, w, out_dtype, tm=1024, tn=512, tk=1024, add=None, add_scale=1.0, a_spec_fn=None):
    m = a.shape[0] if a_spec_fn is None else a_spec_fn[0]
    tm, tn = _tile(m, tm), _tile(w.k, tn)
    tk = _tile(w.n, tk)

    def epi(acc, e, o):
        if add is not None:
            acc = acc + add_scale * e[0][...].astype(F32)
        o[0][...] = acc.astype(out_dtype)

    extras, especs = ([add], [_ij(tm, tn)]) if add is not None else ((), ())
    a_spec = None if a_spec_fn is None else a_spec_fn[1](tm, tk)
    return mm_nt(name, a, w, m, tm, tn, tk, epi, [_sds((m, w.k), out_dtype)], [_ij(tm, tn)], extras, especs,
                 a_spec=a_spec)[0]


def _rows(tm, c):
    return pl.BlockSpec((tm, c), lambda i: (i, 0))


def _fix(shape):
    nd = len(shape)
    return pl.BlockSpec(shape, lambda i: (0,) * nd)


def _accumulate(ref, val):
    @pl.when(pl.program_id(0) == 0)
    def _():
        ref[...] = jnp.zeros_like(ref)

    ref[...] += val


def _ln_backward_rows(dyv, xh, rstd, g, dr_ref, drb_ref, dg_ref, db_ref, ds_ref):
    dxh = dyv * g
    m1 = jnp.mean(dxh, axis=-1, keepdims=True)
    m2 = jnp.mean(dxh * xh, axis=-1, keepdims=True)
    dr = rstd * (dxh - m1 - xh * m2)
    dr_ref[...] = dr
    drb_ref[...] = dr.astype(BF16)
    _accumulate(dg_ref, jnp.sum(dyv * xh, axis=0, keepdims=True))
    _accumulate(db_ref, jnp.sum(dyv, axis=0, keepdims=True))
    _accumulate(ds_ref, jnp.sum(dr, axis=0, keepdims=True))


def mm_nt_ln_backward(name, a, w, add, xhat, rstd, g, tm=512, tk=1024, a_spec_fn=None, a_fn=None):
    m, d = add.shape
    tm, tk = _tile(m, tm), _tile(w.n, tk)

    def epi(acc, e, o):
        _ln_backward_rows(acc + ALPHA * e[0][...], e[1][...], e[2][...], e[3][...], *o)

    vec = pl.BlockSpec((1, d), lambda i, j, k: (0, 0))
    a_spec = None if a_spec_fn is None else a_spec_fn(tm, tk)
    return mm_nt(name, a, w, m, tm, d, tk, epi,
                 [_sds((m, d), F32), _sds((m, d), BF16), _sds((1, d), F32), _sds((1, d), F32), _sds((1, d), F32)],
                 [_i0(tm, d), _i0(tm, d), vec, vec, vec], [add, xhat, rstd, g],
                 [_i0(tm, d), _i0(tm, d), _i0(tm, 1), vec], a_spec=a_spec, rows_in_order=True, a_fn=a_fn)


def loss_ln_backward(y, target, xhat, rstd, g, tm=512):
    s, d = y.shape
    tm = _tile(s, tm)

    def body(y_ref, t_ref, xh_ref, rstd_ref, g_ref, dr_ref, drb_ref, dg_ref, db_ref, ds_ref, loss_ref):
        e = y_ref[...] - t_ref[...]
        part = 0.5 * jnp.sum(jnp.mean(e * e, axis=-1, keepdims=True), axis=0, keepdims=True)
        _accumulate(loss_ref, jnp.broadcast_to(part, (1, d)))
        _ln_backward_rows(e * (1.0 / d), xh_ref[...], rstd_ref[...], g_ref[...], dr_ref, drb_ref, dg_ref, db_ref, ds_ref)

    return pl.pallas_call(
        body, grid=(s // tm,),
        in_specs=[_rows(tm, d), _rows(tm, d), _rows(tm, d), _rows(tm, 1), _fix((1, d))],
        out_specs=[_rows(tm, d), _rows(tm, d)] + [_fix((1, d))] * 4,
        out_shape=[_sds((s, d), F32), _sds((s, d), BF16)] + [_sds((1, d), F32)] * 4,
        compiler_params=_params(("arbitrary",)), name="loss_ln_backward")(y, target, xhat, rstd, g)


def _cols(s, tc, off=0):
    return pl.BlockSpec((s, tc), lambda i: (0, i + off))


def _shift_down(z, sft, rows):
    return jnp.where(rows >= sft, pltpu.roll(z, sft, 0), 0.0)


def _shift_up(z, sft, rows, s):
    return jnp.where(rows < s - sft, pltpu.roll(z, (s - sft) % s, 0), 0.0)


def short_conv_gate(u, conv_w, tc=256):
    s, d3 = u.shape
    d = d3 // 3
    nb = d // tc

    def body(b_ref, c_ref, h_ref, w_ref, o_ref):
        rows = lax.broadcasted_iota(jnp.int32, (s, tc), 0)
        z = c_ref[...] * h_ref[...]
        cz = jnp.zeros((s, tc), F32)
        for k in range(SC_WIDTH):
            sft = SC_WIDTH - 1 - k
            cz = cz + w_ref[pl.ds(k, 1), :] * (_shift_down(z, sft, rows) if sft else z)
        o_ref[...] = (b_ref[...] * cz).astype(BF16)

    return pl.pallas_call(
        body, grid=(nb,),
        in_specs=[_cols(s, tc), _cols(s, tc, nb), _cols(s, tc, 2 * nb), _cols(SC_WIDTH, tc)],
        out_specs=_cols(s, tc), out_shape=_sds((s, d), BF16),
        compiler_params=_params(("parallel",)), name="short_conv_gate")(u, u, u, conv_w)


def short_conv_gate_bwd(u, conv_w, dg, tc=256):
    s, d3 = u.shape
    d = d3 // 3
    nb = d // tc

    def body(b_ref, c_ref, h_ref, w_ref, dg_ref, du_ref, dw_ref):
        rows = lax.broadcasted_iota(jnp.int32, (s, tc), 0)
        c, h, dgv = c_ref[...], h_ref[...], dg_ref[...]
        z = c * h
        dcz = dgv * b_ref[...]
        cz = jnp.zeros((s, tc), F32)
        dz = jnp.zeros((s, tc), F32)
        for k in range(SC_WIDTH):
            sft = SC_WIDTH - 1 - k
            zs = _shift_down(z, sft, rows) if sft else z
            wk = w_ref[pl.ds(k, 1), :]
            cz = cz + wk * zs
            dz = dz + wk * (_shift_up(dcz, sft, rows, s) if sft else dcz)
            dw_ref[pl.ds(k, 1), :] = jnp.sum(dcz * zs, axis=0, keepdims=True)
        du_ref[0] = (dgv * cz).astype(BF16)
        du_ref[1] = (dz * h).astype(BF16)
        du_ref[2] = (dz * c).astype(BF16)

    return pl.pallas_call(
        body, grid=(nb,),
        in_specs=[_cols(s, tc), _cols(s, tc, nb), _cols(s, tc, 2 * nb), _cols(SC_WIDTH, tc), _cols(s, tc)],
        out_specs=[pl.BlockSpec((3, s, tc), lambda i: (0, 0, i)), _cols(SC_WIDTH, tc)],
        out_shape=[_sds((3, s, d), BF16), _sds((SC_WIDTH, d), F32)],
        compiler_params=_params(("parallel",)), name="short_conv_gate_bwd")(u, u, u, conv_w, dg)


def _store_shifted_down(ref, z, rows):
    s, tc = z.shape
    for b in range(8):
        ref[b, pl.ds(0, CONV_PAD), :] = jnp.zeros((CONV_PAD, tc), F32)
        ref[b, pl.ds(CONV_PAD, s), :] = z if b == 0 else _shift_down(z, b, rows)


def _store_shifted_up(ref, z, rows):
    s, tc = z.shape
    for b in range(8):
        ref[b, pl.ds(0, s), :] = z if b == 0 else _shift_up(z, b, rows, s)
        ref[b, pl.ds(s, CONV_PAD), :] = jnp.zeros((CONV_PAD, tc), F32)


def conformer_glu_conv(u, dw_w, dw_b, tc=128):
    s, d2 = u.shape
    d = d2 // 2
    nb = d // tc

    ch = min(CONV_CHUNK, s)

    def body(a_ref, g_ref, w_ref, b_ref, o_ref, down):
        rows = lax.broadcasted_iota(jnp.int32, (s, tc), 0)
        _store_shifted_down(down, a_ref[...] * jax.nn.sigmoid(g_ref[...]), rows)

        def chunk(ci, carry):
            r0 = pl.multiple_of(ci * ch, ch)
            acc = jnp.broadcast_to(b_ref[...], (ch, tc))
            for k in range(CONF_WIDTH):
                sft = CONF_WIDTH - 1 - k
                acc = acc + w_ref[pl.ds(k, 1), :] * down[sft % 8, pl.ds(CONV_PAD + r0 - (sft // 8) * 8, ch), :]
            o_ref[pl.ds(r0, ch), :] = acc
            return carry

        lax.fori_loop(0, s // ch, chunk, 0)

    return pl.pallas_call(
        body, grid=(nb,),
        in_specs=[_cols(s, tc), _cols(s, tc, nb), _cols(CONF_WIDTH, tc), _cols(1, tc)],
        out_specs=_cols(s, tc), out_shape=_sds((s, d), F32),
        scratch_shapes=[pltpu.VMEM((8, CONV_PAD + s, tc), F32)],
        compiler_params=_params(("parallel",)), name="conformer_glu_conv")(u, u, dw_w, dw_b)


def conformer_glu_conv_bwd(u, dw_w, dhc, tc=128):
    s, d2 = u.shape
    d = d2 // 2
    nb = d // tc
    ch = min(CONV_CHUNK, s)

    def body(a_ref, g_ref, w_ref, dhc_ref, du_ref, dbias_ref, dw_ref, db_ref, down, up, dw_acc, dh_buf):
        rows = lax.broadcasted_iota(jnp.int32, (s, tc), 0)
        a = a_ref[...]
        sg = jax.nn.sigmoid(g_ref[...])
        dhcv = dhc_ref[...]
        _store_shifted_down(down, a * sg, rows)
        _store_shifted_up(up, dhcv, rows)
        dw_acc[...] = jnp.zeros_like(dw_acc)

        def chunk(ci, carry):
            r0 = pl.multiple_of(ci * ch, ch)
            dc = dhc_ref[pl.ds(r0, ch), :]
            dh = jnp.zeros((ch, tc), F32)
            for k in range(CONF_WIDTH):
                sft = CONF_WIDTH - 1 - k
                a8, b = (sft // 8) * 8, sft % 8
                dh = dh + w_ref[pl.ds(k, 1), :] * up[b, pl.ds(r0 + a8, ch), :]
                prod = dc * down[b, pl.ds(CONV_PAD + r0 - a8, ch), :]
                dw_acc[k] += jnp.sum(prod.reshape(ch // 8, 8, tc), axis=0)
            dh_buf[pl.ds(r0, ch), :] = dh
            return carry

        lax.fori_loop(0, s // ch, chunk, 0)
        dh = dh_buf[...]
        da = dh * sg
        dgate = dh * a * sg * (1.0 - sg)
        du_ref[0] = da.astype(BF16)
        du_ref[1] = dgate.astype(BF16)
        dbias_ref[pl.ds(0, 1), :] = jnp.sum(da, axis=0, keepdims=True)
        dbias_ref[pl.ds(1, 1), :] = jnp.sum(dgate, axis=0, keepdims=True)
        db_ref[...] = jnp.sum(dhcv, axis=0, keepdims=True)
        for k in range(CONF_WIDTH):
            dw_ref[pl.ds(k, 1), :] = jnp.sum(dw_acc[k], axis=0, keepdims=True)

    return pl.pallas_call(
        body, grid=(nb,),
        in_specs=[_cols(s, tc), _cols(s, tc, nb), _cols(CONF_WIDTH, tc), _cols(s, tc)],
        out_specs=[pl.BlockSpec((2, s, tc), lambda i: (0, 0, i)), _cols(2, tc), _cols(CONF_WIDTH, tc), _cols(1, tc)],
        out_shape=[_sds((2, s, d), BF16), _sds((2, d), F32), _sds((CONF_WIDTH, d), F32), _sds((1, d), F32)],
        scratch_shapes=[pltpu.VMEM((8, CONV_PAD + s, tc), F32), pltpu.VMEM((8, CONV_PAD + s, tc), F32),
                        pltpu.VMEM((CONF_WIDTH + 1, 8, tc), F32), pltpu.VMEM((s, tc), F32)],
        compiler_params=_params(("parallel",)), name="conformer_glu_conv_bwd")(u, u, dw_w, dhc)


def conformer_norm_swish(hc, g, b, tm=512):
    s, d = hc.shape
    tm = _tile(s, tm)

    def body(h_ref, g_ref, b_ref, o_ref):
        n, _, _ = _layer_norm_rows(h_ref[...], g_ref[...], b_ref[...])
        o_ref[...] = (n * jax.nn.sigmoid(n)).astype(BF16)

    return pl.pallas_call(
        body, grid=(s // tm,), in_specs=[_rows(tm, d), _fix((1, d)), _fix((1, d))], out_specs=_rows(tm, d),
        out_shape=_sds((s, d), BF16), compiler_params=_params(("parallel",)), name="conformer_norm_swish")(hc, g, b)


def conformer_norm_swish_bwd(hc, g, b, ds, tm=512):
    s, d = hc.shape
    tm = _tile(s, tm)

    def body(h_ref, g_ref, b_ref, ds_ref, dh_ref, dg_ref, db_ref):
        n, nh, rstd = _layer_norm_rows(h_ref[...], g_ref[...], b_ref[...])
        sg = jax.nn.sigmoid(n)
        dn = ds_ref[...] * (sg * (1.0 + n * (1.0 - sg)))
        dnh = dn * g_ref[...]
        m1 = jnp.mean(dnh, axis=-1, keepdims=True)
        m2 = jnp.mean(dnh * nh, axis=-1, keepdims=True)
        dh_ref[...] = rstd * (dnh - m1 - nh * m2)
        _accumulate(dg_ref, jnp.sum(dn * nh, axis=0, keepdims=True))
        _accumulate(db_ref, jnp.sum(dn, axis=0, keepdims=True))

    return pl.pallas_call(
        body, grid=(s // tm,), in_specs=[_rows(tm, d), _fix((1, d)), _fix((1, d)), _rows(tm, d)],
        out_specs=[_rows(tm, d), _fix((1, d)), _fix((1, d))],
        out_shape=[_sds((s, d), F32), _sds((1, d), F32), _sds((1, d), F32)],
        compiler_params=_params(("arbitrary",)), name="conformer_norm_swish_bwd")(hc, g, b, ds)


def _swap_halves(x):
    lane = lax.broadcasted_iota(jnp.int32, x.shape, 1)
    return jnp.where(lane < QK_ROPE // 2, pltpu.roll(x, 128 - QK_ROPE // 2, 1), pltpu.roll(x, QK_ROPE // 2, 1))


def _rope(x, cf, sf):
    return x * cf + _swap_halves(x) * sf


def _unrope(dx, cf, sf):
    return dx * cf - _swap_halves(dx) * sf


def _rms_rows(x, g):
    r = lax.rsqrt(jnp.mean(x * x, axis=-1, keepdims=True) + RMS_EPS)
    return x * r, r


def mla_latents(t, g_q, g_kv, cf, sf, tm=512):
    s = t.shape[0]
    tm = _tile(s, tm)

    def body(t_ref, gq_ref, gkv_ref, cf_ref, sf_ref, cq_ref, ckv_ref, kpe_ref):
        xq, _ = _rms_rows(t_ref[:, 0:Q_LORA], gq_ref[...])
        cq_ref[...] = (xq * gq_ref[...]).astype(BF16)
        xkv, _ = _rms_rows(t_ref[:, Q_LORA:Q_LORA + KV_LORA], gkv_ref[...])
        ckv_ref[...] = (xkv * gkv_ref[...]).astype(BF16)
        kpe_ref[...] = _rope(t_ref[:, Q_LORA + KV_LORA:], cf_ref[...], sf_ref[...]).astype(BF16)

    w = Q_LORA + KV_LORA + 128
    return pl.pallas_call(
        body, grid=(s // tm,),
        in_specs=[_rows(tm, w), _fix((1, Q_LORA)), _fix((1, KV_LORA)), _rows(tm, 128), _rows(tm, 128)],
        out_specs=[_rows(tm, Q_LORA), _rows(tm, KV_LORA), _rows(tm, 128)],
        out_shape=[_sds((s, Q_LORA), BF16), _sds((s, KV_LORA), BF16), _sds((s, 128), BF16)],
        compiler_params=_params(("parallel",)), name="mla_latents")(t, g_q, g_kv, cf, sf)


def mla_latents_bwd(t, g_q, g_kv, cf, sf, dcq, dckv, dkpe, tm=512):
    s = t.shape[0]
    tm = _tile(s, tm)
    w = Q_LORA + KV_LORA + 128

    def rms_bwd(x, g, dy):
        xh, r = _rms_rows(x, g)
        dxh = dy * g
        return r * (dxh - xh * jnp.mean(dxh * xh, axis=-1, keepdims=True)), jnp.sum(dy * xh, axis=0, keepdims=True)

    def body(t_ref, gq_ref, gkv_ref, cf_ref, sf_ref, dcq_ref, dckv_ref, dkpe_ref, dt_ref, dgq_ref, dgkv_ref):
        dxq, dgq = rms_bwd(t_ref[:, 0:Q_LORA], gq_ref[...], dcq_ref[...])
        dxkv, dgkv = rms_bwd(t_ref[:, Q_LORA:Q_LORA + KV_LORA], gkv_ref[...], dckv_ref[...])
        dt_ref[:, 0:Q_LORA] = dxq.astype(BF16)
        dt_ref[:, Q_LORA:Q_LORA + KV_LORA] = dxkv.astype(BF16)
        dt_ref[:, Q_LORA + KV_LORA:] = _unrope(dkpe_ref[...], cf_ref[...], sf_ref[...]).astype(BF16)
        _accumulate(dgq_ref, dgq)
        _accumulate(dgkv_ref, dgkv)

    return pl.pallas_call(
        body, grid=(s // tm,),
        in_specs=[_rows(tm, w), _fix((1, Q_LORA)), _fix((1, KV_LORA)), _rows(tm, 128), _rows(tm, 128),
                  _rows(tm, Q_LORA), _rows(tm, KV_LORA), _rows(tm, 128)],
        out_specs=[_rows(tm, w), _fix((1, Q_LORA)), _fix((1, KV_LORA))],
        out_shape=[_sds((s, w), BF16), _sds((1, Q_LORA), F32), _sds((1, KV_LORA), F32)],
        compiler_params=_params(("arbitrary",)), name="mla_latents_bwd")(t, g_q, g_kv, cf, sf, dcq, dckv, dkpe)


def mla_queries(cq, w_uq, cf, sf, tm=2048):
    s = cq.shape[0]
    tm = _tile(s, tm)

    def epi(acc, e, o):
        o[0][:, 0:QK_NOPE] = acc[:, 0:QK_NOPE].astype(BF16)
        o[0][:, QK_NOPE:] = _rope(acc[:, QK_NOPE:], e[0][...], e[1][...]).astype(BF16)

    return mm_nn("mla_queries", cq, w_uq, tm, HEAD_PAD, Q_LORA, epi, [_sds((s, N_HEADS * HEAD_PAD), BF16)],
                 [_ij(tm, HEAD_PAD)], [cf, sf], [_i0(tm, 128), _i0(tm, 128)])[0]


def mla_keys(ckv, w_uk, kpe, tm=2048):
    s = ckv.shape[0]
    tm = _tile(s, tm)

    def epi(acc, e, o):
        o[0][:, 0:QK_NOPE] = acc.astype(BF16)
        o[0][:, QK_NOPE:] = e[0][...]

    return mm_nn("mla_keys", ckv, w_uk, tm, QK_NOPE, KV_LORA, epi, [_sds((s, N_HEADS * HEAD_PAD), BF16)],
                 [_ij(tm, HEAD_PAD)], [kpe], [_i0(tm, 128)])[0]


def _masked_scores(q, k, tq, kv):
    sc = lax.dot_general(q, k, NT, preferred_element_type=F32) * ATTN_SCALE
    row = lax.broadcasted_iota(jnp.int32, (tq, tq), 0)
    col = lax.broadcasted_iota(jnp.int32, (tq, tq), 1)
    ok = lax.shift_right_logical(col, CHUNK_SHIFT) <= lax.shift_right_logical(row, CHUNK_SHIFT)
    own = jnp.where(ok, sc[:, kv - tq:], -1e30)
    return own if kv == tq else jnp.concatenate([sc[:, :kv - tq], own], axis=1)


def attention(q, k, v, tq=512):
    s = q.shape[0]
    tq = _tile(s, tq)
    nq = s // tq

    def body(q_ref, k_ref, v_ref, o_ref):
        for qi in range(nq):
            kv = (qi + 1) * tq
            sc = _masked_scores(q_ref[pl.ds(qi * tq, tq), :], k_ref[pl.ds(0, kv), :], tq, kv)
            p = jnp.exp(sc - jnp.max(sc, axis=-1, keepdims=True))
            o = lax.dot_general(p.astype(BF16), v_ref[pl.ds(0, kv), :], NN, preferred_element_type=F32)
            o_ref[pl.ds(qi * tq, tq), :] = (o / jnp.sum(p, axis=-1, keepdims=True)).astype(BF16)

    hq = pl.BlockSpec((s, HEAD_PAD), lambda h: (0, h))
    hv = pl.BlockSpec((s, V_HEAD), lambda h: (0, h))
    return pl.pallas_call(
        body, grid=(N_HEADS,), in_specs=[hq, hq, hv], out_specs=hv, out_shape=_sds((s, N_HEADS * V_HEAD), BF16),
        compiler_params=_params(("parallel",)), name="attention")(q, k, v)


def attention_bwd(q, k, v, do, tq=512):
    s = q.shape[0]
    tq = _tile(s, tq)
    nq = s // tq

    def body(q_ref, k_ref, v_ref, do_ref, dq_ref, dk_ref, dv_ref, dk_acc, dv_acc):
        dk_acc[...] = jnp.zeros_like(dk_acc)
        dv_acc[...] = jnp.zeros_like(dv_acc)
        for qi in range(nq):
            kv = (qi + 1) * tq
            qt = q_ref[pl.ds(qi * tq, tq), :]
            kt = k_ref[pl.ds(0, kv), :]
            dot = do_ref[pl.ds(qi * tq, tq), :]
            sc = _masked_scores(qt, kt, tq, kv)
            p = jnp.exp(sc - jnp.max(sc, axis=-1, keepdims=True))
            p = p / jnp.sum(p, axis=-1, keepdims=True)
            dp = lax.dot_general(dot, v_ref[pl.ds(0, kv), :], NT, preferred_element_type=F32)
            delta = jnp.sum(p * dp, axis=-1, keepdims=True)
            ds = (p * (dp - delta) * ATTN_SCALE).astype(BF16)
            dq_ref[pl.ds(qi * tq, tq), :] = lax.dot_general(ds, kt, NN, preferred_element_type=F32).astype(BF16)
            dk_acc[pl.ds(0, kv), :] += lax.dot_general(ds, qt, TN, preferred_element_type=F32)
            dv_acc[pl.ds(0, kv), :] += lax.dot_general(p.astype(BF16), dot, TN, preferred_element_type=F32)
        dk_ref[...] = dk_acc[...].astype(BF16)
        dv_ref[...] = dv_acc[...].astype(BF16)

    hq = pl.BlockSpec((s, HEAD_PAD), lambda h: (0, h))
    hv = pl.BlockSpec((s, V_HEAD), lambda h: (0, h))
    return pl.pallas_call(
        body, grid=(N_HEADS,), in_specs=[hq, hq, hv, hv], out_specs=[hq, hq, hv],
        out_shape=[_sds((s, N_HEADS * HEAD_PAD), BF16), _sds((s, N_HEADS * HEAD_PAD), BF16),
                   _sds((s, N_HEADS * V_HEAD), BF16)],
        scratch_shapes=[pltpu.VMEM((s, HEAD_PAD), F32), pltpu.VMEM((s, V_HEAD), F32)],
        compiler_params=_params(("parallel",)), name="attention_bwd")(q, k, v, do)


def mla_unrope_grads(dq, dk, cf, sf, tm=512):
    s = dq.shape[0]
    tm = _tile(s, tm)

    def body(dq_ref, dk_ref, cf_ref, sf_ref, dql_ref, dkn_ref, dkpe_ref):
        cfv, sfv = cf_ref[...], sf_ref[...]
        dkpe = jnp.zeros((tm, 128), F32)
        for h in range(N_HEADS):
            lo = h * HEAD_PAD
            dql_ref[:, lo:lo + QK_NOPE] = dq_ref[:, lo:lo + QK_NOPE]
            dql_ref[:, lo + QK_NOPE:lo + HEAD_PAD] = _unrope(
                dq_ref[:, lo + QK_NOPE:lo + HEAD_PAD].astype(F32), cfv, sfv).astype(BF16)
            dkn_ref[:, h * QK_NOPE:(h + 1) * QK_NOPE] = dk_ref[:, lo:lo + QK_NOPE]
            dkpe = dkpe + dk_ref[:, lo + QK_NOPE:lo + HEAD_PAD].astype(F32)
        dkpe_ref[...] = dkpe

    wq = N_HEADS * HEAD_PAD
    return pl.pallas_call(
        body, grid=(s // tm,), in_specs=[_rows(tm, wq), _rows(tm, wq), _rows(tm, 128), _rows(tm, 128)],
        out_specs=[_rows(tm, wq), _rows(tm, N_HEADS * QK_NOPE), _rows(tm, 128)],
        out_shape=[_sds((s, wq), BF16), _sds((s, N_HEADS * QK_NOPE), BF16), _sds((s, 128), F32)],
        compiler_params=_params(("parallel",)), name="mla_unrope_grads")(dq, dk, cf, sf)


ANY = pl.BlockSpec(memory_space=pl.ANY)
GATHER_ID = 1
CHIP_EXCHANGE_ID = 2
PAIR_ID = 3
ALL_ID = 4


def _nbytes(a):
    return a.size * a.dtype.itemsize


def _copy_cost(operand_bytes, sent_fraction):
    sent = int(operand_bytes * sent_fraction)
    return pl.CostEstimate(flops=0, transcendentals=0, bytes_accessed=2 * sent, remote_bytes_transferred=sent)


def _handshake(peers):
    barrier = pltpu.get_barrier_semaphore()
    for peer in peers:
        pl.semaphore_signal(barrier, inc=1, device_id=peer, device_id_type=MESH)
    pl.semaphore_wait(barrier, len(peers))


def _place():
    x, y, c = lax.axis_index("x"), lax.axis_index("y"), lax.axis_index("c")
    chips = [(1 - x, y), (x, 1 - y), (1 - x, 1 - y)]
    return x, y, c, chips


def _half(ref, hc, axis=0):
    n = ref.shape[axis] // 2
    idx = (slice(None),) * axis + (pl.ds(hc * n, n),)
    return ref.at[idx]


def gather_shards(name, tensors, by_columns=()):
    nt = len(tensors)

    def body(*refs):
        a, g = refs[:nt], refs[nt:2 * nt]
        send, recv = refs[2 * nt:]
        x, y, c, _ = _place()
        q = 2 * x + y
        sib, xn, yn = (x, y, 1 - c), (1 - x, y, c), (x, 1 - y, c)
        q_xn, q_yn, q_diag = 2 * (1 - x) + y, 2 * x + 1 - y, 2 * (1 - x) + 1 - y
        _handshake([sib, xn, yn])

        def whole(t, p):
            if t in by_columns:
                n = a[t].shape[1]
                return g[t].at[:, pl.ds(p * n, n)]
            return g[t].at[p]

        def part(t, p, hc, quarter=None):
            rows = a[t].shape[0]
            if quarter is None:
                return whole(t, p).at[pl.ds(hc * (rows // 2), rows // 2)]
            return whole(t, p).at[pl.ds(hc * (rows // 2) + quarter * (rows // 4), rows // 4)]

        def rc(t, k, src, dst, to):
            return pltpu.make_async_remote_copy(src_ref=src, dst_ref=dst, send_sem=send.at[t, k], recv_sem=recv.at[t, k],
                                                device_id=to, device_id_type=MESH)

        sent = []

        def go(cp):
            cp.start()
            sent.append(cp)

        def landed(t, k, piece, frm):
            rc(t, k, piece, piece, frm).wait_recv()
            return piece

        for t in range(nt):
            go(rc(t, 8, a[t], whole(t, q), sib))
            mine = _half(a[t], c)
            go(rc(t, 0, mine, part(t, q, c), xn))
            go(rc(t, 1, mine, part(t, q, c), yn))
        for t in range(nt):
            from_y = landed(t, 1, part(t, q_yn, c), yn)
            go(rc(t, 2, part(t, q_yn, c, 0), part(t, q_yn, c, 0), xn))
            go(rc(t, 5, from_y, from_y, sib))
            from_x = landed(t, 0, part(t, q_xn, c), xn)
            go(rc(t, 3, part(t, q_xn, c, 1), part(t, q_xn, c, 1), yn))
            go(rc(t, 4, from_x, from_x, sib))
        for t in range(nt):
            for k, frm in ((2, xn), (3, yn)):
                piece = landed(t, k, part(t, q_diag, c, k - 2), frm)
                go(rc(t, 4 + k, piece, piece, sib))
        for t in range(nt):
            landed(t, 4, part(t, q_xn, 1 - c), sib)
            landed(t, 5, part(t, q_yn, 1 - c), sib)
            landed(t, 6, part(t, q_diag, 1 - c, 0), sib)
            landed(t, 7, part(t, q_diag, 1 - c, 1), sib)
            landed(t, 8, whole(t, q), sib)
        for cp in sent:
            cp.wait_send()

    return pl.kernel(
        body, name=name,
        out_type=[_sds((a.shape[0], N_CHIPS * a.shape[1]) if t in by_columns else (N_CHIPS,) + a.shape, a.dtype)
                  for t, a in enumerate(tensors)],
        mesh=plsc.ScalarSubcoreMesh(axis_name="sequencer", num_cores=1),
        scratch_types=[pltpu.SemaphoreType.DMA((nt, 9)), pltpu.SemaphoreType.DMA((nt, 9))],
        cost_estimate=_copy_cost(sum(_nbytes(a) for a in tensors), 4),
        compiler_params=pltpu.CompilerParams(collective_id=GATHER_ID))(*tensors)


def pair_exchange(name, grads, on_sequencer):
    nt = len(grads)

    def body(*refs):
        g, theirs = refs[:nt], refs[nt:2 * nt]
        send, recv = refs[2 * nt:]
        x, y, c, _ = _place()
        if on_sequencer:
            _handshake([(x, y, 1 - c)])
        cps = []
        for t in range(nt):
            cp = pltpu.make_async_remote_copy(src_ref=_half(g[t], 1 - c, 1), dst_ref=theirs[t], send_sem=send.at[t],
                                              recv_sem=recv.at[t], device_id=(x, y, 1 - c), device_id_type=MESH)
            cp.start()
            cps.append(cp)
        for cp in cps:
            cp.wait()

    if not on_sequencer:
        return pl.pallas_call(
            body, in_specs=[ANY] * nt, out_specs=[ANY] * nt,
            out_shape=[_sds((N_CHIPS, a.shape[1] // 2, a.shape[2]), a.dtype) for a in grads],
            scratch_shapes=[pltpu.SemaphoreType.DMA((nt,)), pltpu.SemaphoreType.DMA((nt,))],
            name=name)(*grads)
    return pl.kernel(
        body, name=name, out_type=[_sds((N_CHIPS, a.shape[1] // 2, a.shape[2]), a.dtype) for a in grads],
        mesh=plsc.ScalarSubcoreMesh(axis_name="sequencer", num_cores=1),
        scratch_types=[pltpu.SemaphoreType.DMA((nt,)), pltpu.SemaphoreType.DMA((nt,))],
        cost_estimate=_copy_cost(sum(_nbytes(a) for a in grads), 0.5),
        compiler_params=pltpu.CompilerParams(collective_id=PAIR_ID))(*grads)


def chip_exchange(name, parts):
    nt = len(parts)

    def body(*refs):
        a, r = refs[:nt], refs[nt:2 * nt]
        send, recv = refs[2 * nt:]
        x, y, c, chips = _place()
        _handshake([(*chip, c) for chip in chips])
        cps = []
        for t in range(nt):
            for j, chip in enumerate(chips):
                cp = pltpu.make_async_remote_copy(
                    src_ref=a[t].at[2 * chip[0] + chip[1]], dst_ref=r[t].at[j], send_sem=send.at[t, j],
                    recv_sem=recv.at[t, j], device_id=(*chip, c), device_id_type=MESH)
                cp.start()
                cps.append(cp)
        for cp in cps:
            cp.wait()

    return pl.kernel(
        body, name=name, out_type=[_sds((N_CHIPS - 1,) + a.shape[1:], a.dtype) for a in parts],
        mesh=plsc.ScalarSubcoreMesh(axis_name="sequencer", num_cores=1),
        scratch_types=[pltpu.SemaphoreType.DMA((nt, 3)), pltpu.SemaphoreType.DMA((nt, 3))],
        cost_estimate=_copy_cost(sum(_nbytes(a) for a in parts), 0.75),
        compiler_params=pltpu.CompilerParams(collective_id=CHIP_EXCHANGE_ID))(*parts)


def pair_share(name, halves):
    nt = len(halves)

    def body(*refs):
        h, other = refs[:nt], refs[nt:2 * nt]
        send, recv = refs[2 * nt:]
        x, y, c, _ = _place()
        _handshake([(x, y, 1 - c)])
        cps = []
        for t in range(nt):
            cp = pltpu.make_async_remote_copy(src_ref=h[t], dst_ref=other[t], send_sem=send.at[t], recv_sem=recv.at[t],
                                              device_id=(x, y, 1 - c), device_id_type=MESH)
            cp.start()
            cps.append(cp)
        for cp in cps:
            cp.wait()

    return pl.kernel(
        body, name=name, out_type=[_sds(a.shape, a.dtype) for a in halves],
        mesh=plsc.ScalarSubcoreMesh(axis_name="sequencer", num_cores=1),
        scratch_types=[pltpu.SemaphoreType.DMA((nt,)), pltpu.SemaphoreType.DMA((nt,))],
        cost_estimate=_copy_cost(sum(_nbytes(a) for a in halves), 1),
        compiler_params=pltpu.CompilerParams(collective_id=PAIR_ID))(*halves)


def pack_rows(name, parts, rows):
    cdim = parts[0].shape[1]
    n = len(parts)
    vm = pl.BlockSpec(memory_space=pltpu.VMEM)

    def pack(*refs):
        p, o_ref = refs[:n], refs[n]
        at = 0
        for ref in p:
            o_ref[pl.ds(at, ref.shape[0]), :] = ref[...]
            at += ref.shape[0]
        o_ref[pl.ds(at, rows - at), :] = jnp.zeros((rows - at, cdim), F32)

    return pl.pallas_call(pack, in_specs=[vm] * n, out_specs=vm, out_shape=_sds((rows, cdim), F32), name=name)(*parts)


def all_reduce_small(parts, rows):
    cdim = parts[0].shape[1]
    vm = pl.BlockSpec(memory_space=pltpu.VMEM)
    mine = pack_rows("small_pack", parts, rows)

    def exchange(mine_ref, buf, send, recv, lsem):
        x, y, c, _ = _place()
        me = 4 * x + 2 * y + c
        peers = [(x ^ (k >> 2), y ^ ((k >> 1) & 1), c ^ (k & 1)) for k in range(1, 8)]
        _handshake(peers)
        own = pltpu.make_async_copy(mine_ref, buf.at[me], lsem)
        own.start()
        cps = []
        for k, to in enumerate(peers):
            cp = pltpu.make_async_remote_copy(src_ref=mine_ref, dst_ref=buf.at[me], send_sem=send.at[k], recv_sem=recv.at[k],
                                              device_id=to, device_id_type=MESH)
            cp.start()
            cps.append(cp)
        for k, (px, py, pc) in enumerate(peers):
            pltpu.make_async_remote_copy(src_ref=mine_ref, dst_ref=buf.at[4 * px + 2 * py + pc], send_sem=send.at[k],
                                         recv_sem=recv.at[k], device_id=(x, y, c), device_id_type=MESH).wait_recv()
        for cp in cps:
            cp.wait_send()
        own.wait()

    landed = pl.kernel(
        exchange, name="small_exchange", out_type=_sds((8, rows, cdim), F32),
        mesh=plsc.ScalarSubcoreMesh(axis_name="sequencer", num_cores=1),
        scratch_types=[pltpu.SemaphoreType.DMA((7,)), pltpu.SemaphoreType.DMA((7,)), pltpu.SemaphoreType.DMA],
        cost_estimate=_copy_cost(rows * cdim * 4, 7),
        compiler_params=pltpu.CompilerParams(collective_id=ALL_ID))(mine)

    def total(buf, o_ref):
        acc = buf[0]
        for d in range(1, 8):
            acc = acc + buf[d]
        o_ref[...] = acc

    return pl.pallas_call(total, in_specs=[vm], out_specs=vm, out_shape=_sds((rows, cdim), F32), name="small_sum")(landed)


def pair_sum(gs, theirs, core, tm=256):
    n = len(gs)
    _, r, c = gs[0].shape
    tm = _tile(r // 2, tm)
    nh = r // 2 // tm

    def body(core_ref, *refs):
        for a_ref, b_ref, o_ref in zip(refs[:n], refs[n:2 * n], refs[2 * n:]):
            o_ref[...] = (a_ref[...].astype(F32) + b_ref[...].astype(F32)).astype(BF16)

    blk = (N_CHIPS, tm, c)
    own = pl.BlockSpec(blk, lambda i, cr: (0, cr[0] * nh + i, 0))
    half = pl.BlockSpec(blk, lambda i, cr: (0, i, 0))
    return pl.pallas_call(
        body, grid_spec=pltpu.PrefetchScalarGridSpec(
            num_scalar_prefetch=1, grid=(nh,), in_specs=[own] * n + [half] * n, out_specs=[half] * n),
        out_shape=[_sds(t.shape, BF16) for t in theirs], compiler_params=_params(("parallel",)),
        name="pair_sum")(core, *gs, *theirs)


def chip_sum(own, landed, chip, stack, layer, layers, tm=256):
    _, r, c = own.shape
    tm = _tile(r, tm)

    def body(chip_ref, own_ref, l_ref, *rest):
        acc = own_ref[...].astype(F32)
        for j in range(N_CHIPS - 1):
            acc = acc + l_ref[j].astype(F32)
        rest[-1][...] = acc

    in_specs = [pl.BlockSpec((None, tm, c), lambda i, qr: (qr[0], i, 0)),
                pl.BlockSpec((N_CHIPS - 1, tm, c), lambda i, qr: (0, i, 0))]
    args = [chip, own, landed]
    if stack is not None:
        in_specs.append(ANY)
        args.append(stack)
    return pl.pallas_call(
        body, grid_spec=pltpu.PrefetchScalarGridSpec(
            num_scalar_prefetch=1, grid=(r // tm,), in_specs=in_specs,
            out_specs=pl.BlockSpec((None, tm, c), lambda i, qr: (layer, i, 0))),
        out_shape=_sds((layers, r, c), F32), input_output_aliases={3: 0} if stack is not None else {},
        compiler_params=_params(("parallel",)), name="chip_sum")(*args)


def _adamw_math(w, g, m, v):
    bc1 = 1.0 - ADAM_B1 ** ADAM_STEP
    bc2 = 1.0 - ADAM_B2 ** ADAM_STEP
    nm = ADAM_B1 * m + (1.0 - ADAM_B1) * g
    nv = ADAM_B2 * v + (1.0 - ADAM_B2) * (g * g)
    return -ADAM_LR * ((nm / bc1) / (jnp.sqrt(nv / bc2) + ADAM_EPS) + ADAM_WD * w), nm, nv


def vector_update(red, chip, ws, ms, vs, where):
    n = len(ws)
    dd = red.shape[1]

    def body(chip_ref, red_ref, *refs):
        w_r, m_r, v_r = refs[0:n], refs[n:2 * n], refs[2 * n:3 * n]
        g_o, d_o, m_o, v_o = (refs[(3 + k) * n:(4 + k) * n] for k in range(4))
        q = chip_ref[0]

        def chip_block(val, width):
            out = val[:, 0:width]
            for p in range(1, val.shape[1] // width):
                out = jnp.where(q == p, val[:, p * width:(p + 1) * width], out)
            return out

        for k in range(n):
            for idx, r0, nr, cols in where[k]:
                width = w_r[k].shape[-1]
                if cols == "chip" and width * N_CHIPS != dd:
                    g = chip_block(jnp.concatenate([red_ref[pl.ds(r0 + j, 1), :] for j in range(nr)], axis=1), width)
                else:
                    g = red_ref[pl.ds(r0, nr), :]
                    g = chip_block(g, width) if cols == "chip" else g if cols == "all" else g[:, 0:cols]
                delta, nm, nv = _adamw_math(w_r[k][idx], g, m_r[k][idx], v_r[k][idx])
                g_o[k][idx] = g
                d_o[k][idx] = delta
                m_o[k][idx] = nm
                v_o[k][idx] = nv

    vm = pl.BlockSpec(memory_space=pltpu.VMEM)
    outs = pl.pallas_call(
        body, in_specs=[pl.BlockSpec(memory_space=pltpu.SMEM), vm] + [vm] * (3 * n), out_specs=[vm] * (4 * n),
        out_shape=[_sds(w.shape, F32) for w in ws] * 4, name="vector_update")(chip, red, *ws, *ms, *vs)
    return [outs[k * n:(k + 1) * n] for k in range(4)]


def adamw_joined(w, m, v, g_mine, g_theirs, core, tm=512):
    nl, r, c = w.shape
    tm = _tile(r // 2, tm)
    nh = r // 2 // tm

    def body(core_ref, w_ref, m_ref, v_ref, gm_ref, gt_ref, g_ref, d_ref, nm_ref, nv_ref):
        mine = (pl.program_id(1) // nh) == core_ref[0]
        gv = jnp.where(mine, gm_ref[...], gt_ref[...])
        g_ref[...] = gv
        d_ref[...], nm_ref[...], nv_ref[...] = _adamw_math(w_ref[...], gv, m_ref[...], v_ref[...])

    full = pl.BlockSpec((None, tm, c), lambda l, i, cr: (l, i, 0))
    mine = pl.BlockSpec((None, tm, c), lambda l, i, cr: (l, jnp.where(i // nh == cr[0], i % nh, 0), 0))
    theirs = pl.BlockSpec((None, tm, c), lambda l, i, cr: (l, jnp.where(i // nh == cr[0], 0, i % nh), 0))
    return pl.pallas_call(
        body, grid_spec=pltpu.PrefetchScalarGridSpec(
            num_scalar_prefetch=1, grid=(nl, r // tm), in_specs=[full, full, full, mine, theirs], out_specs=[full] * 4),
        out_shape=[_sds((nl, r, c), F32)] * 4, compiler_params=_params(("parallel", "parallel")),
        name="adamw_joined")(core, w, m, v, g_mine, g_theirs)


WEIGHTS = ['sc_w_in', 'sc_conv_w', 'sc_w_out', 'mla_w_dq', 'mla_g_q', 'mla_w_uq', 'mla_w_dkv', 'mla_g_kv', 'mla_w_uk',
           'mla_w_uv', 'mla_w_o', 'cf_w_pw1', 'cf_b_pw1', 'cf_dw_w', 'cf_dw_b', 'cf_norm_g', 'cf_norm_b', 'cf_w_pw2',
           'cf_b_pw2', 'ff_w1', 'ff_w2', 'ln_mix_g', 'ln_mix_b', 'ln_ff_g', 'ln_ff_b']
ARGS = ['x'] + WEIGHTS + ['loss_target'] + ['m_' + n for n in WEIGHTS] + ['v_' + n for n in WEIGHTS]


def _sq_relu(h):
    r = jnp.maximum(h, jnp.zeros_like(h))
    return r * r


def _mlp_forward(i, x, xb, w1, w2, g, b):
    hb = mm_plain_nn(f"mlp{i}_up", xb, w1, BF16, tm=2048, tn=1024)
    y, yb, xh, rstd = mm_residual_ln(f"mlp{i}_down_ln", hb, w2, x, g, b, tk=4096, a_fn=_sq_relu)
    return (y, yb), dict(xb=xb, hb=hb, xh=xh, rstd=rstd, g=g)


def _mlp_backward(i, dr, drb, sv, w1, w2, dw1, dw2, reduce_after, mixer_ln):
    s = dr.shape[0]
    tm, tn = _tile(s, 1024), 1024

    def epi(acc, e, o):
        o[0][...] = (acc * (2.0 * jnp.maximum(e[0][...].astype(F32), 0.0))).astype(BF16)

    dhb = mm_nt(f"mlp{i}_down_bwd", drb, w2, s, tm, tn, 1024, epi, [_sds((s, w2.k), BF16)], [_ij(tm, tn)],
                [sv["hb"]], [_ij(tm, tn)])[0]
    g_w2 = mm_tn(f"mlp{i}_dw2", sv["hb"], drb, dw2, s, 1024, 1024, a_fn=_sq_relu)
    g_w1 = mm_tn(f"mlp{i}_dw1", sv["xb"], dhb, dw1, s, 1024, 1024)
    dhb = reduce_after(dhb, {f"w1_{i}": g_w1, f"w2_{i}": g_w2})
    return mm_nt_ln_backward(f"mlp{i}_up_bwd", dhb, w1, dr, *mixer_ln, tk=4096)


def kernel(x, sc_w_in, sc_conv_w, sc_w_out, mla_w_dq, mla_g_q, mla_w_uq, mla_w_dkv, mla_g_kv, mla_w_uk, mla_w_uv, mla_w_o, cf_w_pw1, cf_b_pw1, cf_dw_w, cf_dw_b, cf_norm_g, cf_norm_b, cf_w_pw2, cf_b_pw2, ff_w1, ff_w2, ln_mix_g, ln_mix_b, ln_ff_g, ln_ff_b, loss_target, m_sc_w_in, m_sc_conv_w, m_sc_w_out, m_mla_w_dq, m_mla_g_q, m_mla_w_uq, m_mla_w_dkv, m_mla_g_kv, m_mla_w_uk, m_mla_w_uv, m_mla_w_o, m_cf_w_pw1, m_cf_b_pw1, m_cf_dw_w, m_cf_dw_b, m_cf_norm_g, m_cf_norm_b, m_cf_w_pw2, m_cf_b_pw2, m_ff_w1, m_ff_w2, m_ln_mix_g, m_ln_mix_b, m_ln_ff_g, m_ln_ff_b, v_sc_w_in, v_sc_conv_w, v_sc_w_out, v_mla_w_dq, v_mla_g_q, v_mla_w_uq, v_mla_w_dkv, v_mla_g_kv, v_mla_w_uk, v_mla_w_uv, v_mla_w_o, v_cf_w_pw1, v_cf_b_pw1, v_cf_dw_w, v_cf_dw_b, v_cf_norm_g, v_cf_norm_b, v_cf_w_pw2, v_cf_b_pw2, v_ff_w1, v_ff_w2, v_ln_mix_g, v_ln_mix_b, v_ln_ff_g, v_ln_ff_b):
    given = dict(zip(ARGS, (x, sc_w_in, sc_conv_w, sc_w_out, mla_w_dq, mla_g_q, mla_w_uq, mla_w_dkv, mla_g_kv, mla_w_uk, mla_w_uv, mla_w_o, cf_w_pw1, cf_b_pw1, cf_dw_w, cf_dw_b, cf_norm_g, cf_norm_b, cf_w_pw2, cf_b_pw2, ff_w1, ff_w2, ln_mix_g, ln_mix_b, ln_ff_g, ln_ff_b, loss_target, m_sc_w_in, m_sc_conv_w, m_sc_w_out, m_mla_w_dq, m_mla_g_q, m_mla_w_uq, m_mla_w_dkv, m_mla_g_kv, m_mla_w_uk, m_mla_w_uv, m_mla_w_o, m_cf_w_pw1, m_cf_b_pw1, m_cf_dw_w, m_cf_dw_b, m_cf_norm_g, m_cf_norm_b, m_cf_w_pw2, m_cf_b_pw2, m_ff_w1, m_ff_w2, m_ln_mix_g, m_ln_mix_b, m_ln_ff_g, m_ln_ff_b, v_sc_w_in, v_sc_conv_w, v_sc_w_out, v_mla_w_dq, v_mla_g_q, v_mla_w_uq, v_mla_w_dkv, v_mla_g_kv, v_mla_w_uk, v_mla_w_uv, v_mla_w_o, v_cf_w_pw1, v_cf_b_pw1, v_cf_dw_w, v_cf_dw_b, v_cf_norm_g, v_cf_norm_b, v_cf_w_pw2, v_cf_b_pw2, v_ff_w1, v_ff_w2, v_ln_mix_g, v_ln_mix_b, v_ln_ff_g, v_ln_ff_b)))
    s, d = x.shape[1], x.shape[2]
    d_ff = 4 * d
    dq4 = d // N_CHIPS
    xq = lax.axis_index("x") * 2 + lax.axis_index("y")

    w_dkv_pad = jnp.pad(mla_w_dkv[0], ((0, 0), (0, 128 - QK_ROPE)))
    w_uq_pad = jnp.pad(mla_w_uq[0].reshape(Q_LORA, 2, QK_NOPE + QK_ROPE), ((0, 0), (0, 0), (0, HEAD_PAD - QK_NOPE - QK_ROPE)))
    small = pack_rows("vector_weights_pack", [
        sc_conv_w.reshape(2 * SC_WIDTH, dq4), cf_b_pw1.reshape(2, dq4), cf_dw_w[0], cf_dw_b, cf_norm_g, cf_norm_b,
        cf_b_pw2], 64)
    mlp_w = lambda i: [ff_w1[i].astype(BF16), ff_w2[i].astype(BF16)]
    g_in, g_out, g_w1, g_w2 = [None] * 2, [None] * 2, [None] * DEPTH, [None] * DEPTH
    g_in[0], g_out[0], g_small = gather_shards(
        "gather_mixer0", [sc_w_in[0].astype(BF16), sc_w_out[0].astype(BF16), small], by_columns=(0,))
    (g_w1[0],) = gather_shards("gather_up0", [ff_w1[0].astype(BF16)], by_columns=(0,))
    (g_w2[0],) = gather_shards("gather_down0", [ff_w2[0].astype(BF16)])
    g_dqkv, g_uq, g_uk, g_uv, g_o = gather_shards("gather_mixer1", [
        jnp.concatenate([mla_w_dq[0], w_dkv_pad], axis=1).astype(BF16),
        w_uq_pad.reshape(Q_LORA, 2 * HEAD_PAD).astype(BF16),
        mla_w_uk.reshape(KV_LORA // N_CHIPS, N_HEADS * QK_NOPE).astype(BF16),
        mla_w_uv.reshape(KV_LORA // N_CHIPS, N_HEADS * V_HEAD).astype(BF16), mla_w_o[0].astype(BF16)], by_columns=(1,))
    g_w1[1], g_w2[1] = gather_shards("gather_mlp1", mlp_w(1), by_columns=(0,))
    g_pw1, g_pw2, g_w1[2], g_w2[2] = gather_shards(
        "gather_layer2", [cf_w_pw1[0].astype(BF16), cf_w_pw2[0].astype(BF16)] + mlp_w(2), by_columns=(0, 2))
    g_in[1], g_out[1], g_w1[3], g_w2[3] = gather_shards(
        "gather_layer3", [sc_w_in[1].astype(BF16), sc_w_out[1].astype(BF16)] + mlp_w(3), by_columns=(0, 2))

    wd_t = Q_LORA + KV_LORA + 128
    w_in = [Stk("full", d, 3 * d, g_in[j]) for j in range(2)]
    w_out = [Stk("row", d, d, g_out[j]) for j in range(2)]
    w_dqkv = Stk("row", d, wd_t, g_dqkv)
    w_uq = Stk("full", Q_LORA, N_HEADS * HEAD_PAD, g_uq)
    w_uk = Stk("row", KV_LORA, N_HEADS * QK_NOPE, g_uk)
    w_uv = Stk("row", KV_LORA, N_HEADS * V_HEAD, g_uv)
    w_o = Stk("row", d, d, g_o)
    w_pw1 = Stk("full", d, 2 * d, g_pw1)
    w_pw2 = Stk("row", d, d, g_pw2)
    w_1 = [Stk("full", d, d_ff, g_w1[i]) for i in range(DEPTH)]
    w_2 = [Stk("row", d_ff, d, g_w2[i]) for i in range(DEPTH)]

    def wide(rows):
        return jnp.swapaxes(rows, 0, 1).reshape(rows.shape[1], d)

    conv_w = wide(g_small[:, 0:6]).reshape(2, SC_WIDTH, d)
    b_pw1 = g_small[:, 6:8].reshape(1, 2 * d)
    dw_w = wide(g_small[:, 8:39])
    dw_b, norm_g, norm_b, b_pw2 = (wide(g_small[:, 39 + k:40 + k]) for k in range(4))

    pos = jnp.arange(s, dtype=F32)
    inv_freq = ROPE_THETA ** (-jnp.arange(0, QK_ROPE, 2, dtype=F32) / QK_ROPE)
    ang = pos[:, None] * inv_freq[None, :]
    cos, sin, zero = jnp.cos(ang), jnp.sin(ang), jnp.zeros((s, 128 - QK_ROPE), F32)
    cf = jnp.concatenate([cos, cos, zero], axis=1)
    sf = jnp.concatenate([-sin, sin, zero], axis=1)

    def row(a, i):
        return a[i:i + 1]

    xs = x.reshape(s, d)
    cur = (xs, xs.astype(BF16))
    tape = []
    for i in range(DEPTH):
        mixer, j = i % 3, i // 3
        xf, xb = cur
        lg, lb = row(ln_mix_g, i), row(ln_mix_b, i)
        if mixer == 0:
            u = mm_plain_nn(f"sc{j}_in", xb, w_in[j], F32, tn=3 * dq4)
            gb = short_conv_gate(u, conv_w[j])
            y, yb, xh, rstd = mm_residual_ln(f"sc{j}_out_ln", gb, w_out[j], xf, lg, lb)
            sv = dict(xb=xb, u=u, gb=gb)
        elif mixer == 1:
            t = mm_plain_nn("mla_down", xb, w_dqkv, F32, tn=wd_t // 2)
            cq, ckv, kpe = mla_latents(t, mla_g_q, mla_g_kv, cf, sf)
            qh = mla_queries(cq, w_uq, cf, sf)
            kh = mla_keys(ckv, w_uk, kpe)
            vh = mm_plain_nn("mla_values", ckv, w_uv, BF16, tk=KV_LORA)
            oh = attention(qh, kh, vh)
            y, yb, xh, rstd = mm_residual_ln("mla_out_ln", oh, w_o, xf, lg, lb)
            sv = dict(xb=xb, t=t, cq=cq, ckv=ckv, qh=qh, kh=kh, vh=vh, oh=oh)
        else:
            u = mm_plain_nn("cf_pw1", xb, w_pw1, F32, bias=b_pw1)
            hc = conformer_glu_conv(u, dw_w, dw_b)
            sb = conformer_norm_swish(hc, norm_g, norm_b)
            y, yb, xh, rstd = mm_residual_ln("cf_pw2_ln", sb, w_pw2, xf, lg, lb, bias=b_pw2)
            sv = dict(xb=xb, u=u, hc=hc, sb=sb)
        sv.update(xh=xh, rstd=rstd, g=lg)
        cur, sv_mlp = _mlp_forward(i, y, yb, w_1[i], w_2[i], row(ln_ff_g, i), row(ln_ff_b, i))
        tape.append((sv, sv_mlp))

    g_ln = {n: [None] * DEPTH for n in ("ln_mix_g", "ln_mix_b", "ln_ff_g", "ln_ff_b")}
    last = tape[DEPTH - 1][1]
    dr, drb, g_ln["ln_ff_g"][DEPTH - 1], g_ln["ln_ff_b"][DEPTH - 1], _, loss_part = loss_ln_backward(
        cur[0], loss_target.reshape(s, d), last["xh"], last["rstd"], last["g"])

    grads = {}
    smalls = {}
    conv_grads = [None, None]
    core = lax.axis_index("c").astype(jnp.int32).reshape(1)
    chip = xq.astype(jnp.int32).reshape(1)
    pairs, landed = {}, {}
    ready, theirs = [], {}

    def hold(xs, others):
        live = [x for x in xs if x is not None]
        out = lax.optimization_barrier((*live, *others))
        rest = iter(out[:len(live)])
        return tuple(None if x is None else next(rest) for x in xs), list(out[len(live):])

    def reduce_after(x, new, early=False):
        out = lax.optimization_barrier((x, *new.values()))
        grads.update(zip(new, out[1:]))
        if early:
            theirs.update(zip(new, pair_exchange(f"pair_exchange_{len(theirs)}", list(out[1:]), True)))
        ready.extend(new)
        return out[0]

    def reduce_layer(i, x):
        late = [n for n in ready if n not in theirs]
        if late:
            theirs.update(zip(late, pair_exchange(f"pair_exchange_layer{i}", [grads[n] for n in late], False)))
        by_shape = {}
        for n in ready:
            by_shape.setdefault(grads[n].shape, []).append(n)
        for names in by_shape.values():
            pairs.update(zip(names, pair_sum([grads[n] for n in names], [theirs[n] for n in names], core)))
        sums = [pairs[n] for n in ready]
        landed.update(zip(ready, chip_exchange(f"chip_exchange_layer{i}", sums)))
        exchanged.append(list(ready))
        ready.clear()
        return hold(x, sums)[0]

    groups = [["in_0", "in_1"], ["out_0", "out_1"], ["dqkv"], ["uq"], ["uk"], ["uv"], ["o"], ["pw1"], ["pw2"],
              [f"w1_{i}" for i in range(DEPTH)], [f"w2_{i}" for i in range(DEPTH)]]
    stacks = [None] * len(groups)
    exchanged = []

    def sum_layer(x, last=False):
        names = exchanged.pop(0)
        if last:
            x, held = hold(x, [landed[n] for n in names])
            landed.update(zip(names, held))
        new = []
        for n in names:
            k = next(k for k, members in enumerate(groups) if n in members)
            stacks[k] = chip_sum(pairs[n], landed[n], chip, stacks[k], groups[k].index(n), len(groups[k]))
            new.append(stacks[k])
        return x if last else hold(x, new)[0]

    for i in reversed(range(DEPTH)):
        mixer, j = i % 3, i // 3
        sv, sv_mlp = tape[i]
        dr, drb, g_ln["ln_mix_g"][i], g_ln["ln_mix_b"][i], dr_sum = _mlp_backward(
            i, dr, drb, sv_mlp, w_1[i], w_2[i], Stk("col", d, d_ff), Stk("row", d_ff, d),
            lambda x_, new: reduce_after(x_, new, early=i > 0), (sv["xh"], sv["rstd"], sv["g"]))
        if i == 0:
            dr, drb = reduce_layer("0_mlp", (dr, drb))

        def to_input(name, a, w, tk, a_spec_fn=None):
            if i == 0:
                spec = None if a_spec_fn is None else (s, a_spec_fn)
                return mm_plain_nt(name, a, w, F32, tn=1024, tk=tk, add=dr, add_scale=ALPHA, a_spec_fn=spec), None
            prev = tape[i - 1][1]
            if a_spec_fn is None:
                out = mm_nt_ln_backward(name, a, w, dr, prev["xh"], prev["rstd"], prev["g"], tk=tk)
            else:
                nparts = a.shape[0]
                out = mm_nt_ln_backward(
                    name, a, w, dr, prev["xh"], prev["rstd"], prev["g"], tk=w.n,
                    a_spec_fn=lambda tm, tk_: pl.BlockSpec((nparts, tm, d), lambda i_, j_, k_: (0, i_, 0)),
                    a_fn=lambda blk: jnp.concatenate([blk[p] for p in range(nparts)], axis=1))
            g_ln["ln_ff_g"][i - 1], g_ln["ln_ff_b"][i - 1] = out[2], out[3]
            return out[0], out[1]

        parts_of = lambda tm, tk: pl.BlockSpec((None, tm, tk), lambda i_, j_, k_: (k_, i_, 0))
        if mixer == 0:
            dgate = mm_plain_nt(f"sc{j}_out_bwd", drb, w_out[j], F32)
            dw_out = mm_tn(f"sc{j}_dw_out", sv["gb"], drb, Stk("row", d, d), s, 512, 1024)
            du, conv_grads[j] = short_conv_gate_bwd(sv["u"], conv_w[j], dgate)
            nb = d // 256
            dw_in = mm_tn(
                f"sc{j}_dw_in", sv["xb"], du, Stk("col", d, 3 * d), s, 1024, 256,
                b_spec=pl.BlockSpec((None, s, 256), lambda i_, j_, k_: (j_ // nb, k_, j_ % nb)))
            du = reduce_after(du, {f"in_{j}": dw_in, f"out_{j}": dw_out})
            dr, drb = to_input(f"sc{j}_in_bwd", du, w_in[j], d, parts_of)
        elif mixer == 1:
            do = mm_plain_nt("mla_out_bwd", drb, w_o, BF16)
            g_o = mm_tn("mla_dw_o", sv["oh"], drb, Stk("row", d, d), s, 512, 1024)
            dqh, dkh, dvh = attention_bwd(sv["qh"], sv["kh"], sv["vh"], do)
            dql, dkn, dkpe = mla_unrope_grads(dqh, dkh, cf, sf)
            g_uq = mm_tn("mla_dw_uq", sv["cq"], dql, Stk("col", Q_LORA, N_HEADS * HEAD_PAD), s, Q_LORA, 512)
            dcq = mm_plain_nt("mla_uq_bwd", dql, w_uq, F32, tn=Q_LORA)
            g_uk = mm_tn("mla_dw_uk", sv["ckv"], dkn, Stk("row", KV_LORA, N_HEADS * QK_NOPE), s, KV_LORA, 1024)
            g_uv = mm_tn("mla_dw_uv", sv["ckv"], dvh, Stk("row", KV_LORA, N_HEADS * V_HEAD), s, KV_LORA, 1024)
            dckv = mm_plain_nt("mla_uk_bwd", dkn, w_uk, F32, tn=KV_LORA)
            dckv = mm_plain_nt("mla_uv_bwd", dvh, w_uv, F32, tn=KV_LORA, add=dckv)
            dt, smalls["g_q"], smalls["g_kv"] = mla_latents_bwd(sv["t"], mla_g_q, mla_g_kv, cf, sf, dcq, dckv, dkpe)
            g_dqkv = mm_tn("mla_dw_down", sv["xb"], dt, Stk("row", d, wd_t), s, 512, wd_t)
            dt = reduce_after(dt, {"dqkv": g_dqkv, "uq": g_uq, "uk": g_uk, "uv": g_uv, "o": g_o})
            dr, drb = to_input("mla_down_bwd", dt, w_dqkv, wd_t)
        else:
            dsw = mm_plain_nt("cf_pw2_bwd", drb, w_pw2, F32)
            g_pw2 = mm_tn("cf_dw_pw2", sv["sb"], drb, Stk("row", d, d), s, 512, 1024)
            smalls["b_pw2"] = dr_sum
            dhc, smalls["norm_g"], smalls["norm_b"] = conformer_norm_swish_bwd(sv["hc"], norm_g, norm_b, dsw)
            du, smalls["b_pw1"], smalls["dw_w"], smalls["dw_b"] = conformer_glu_conv_bwd(sv["u"], dw_w, dhc)
            nb = d // 512
            g_pw1 = mm_tn(
                "cf_dw_pw1", sv["xb"], du, Stk("col", d, 2 * d), s, 1024, 512,
                b_spec=pl.BlockSpec((None, s, 512), lambda i_, j_, k_: (j_ // nb, k_, j_ % nb)))
            du = reduce_after(du, {"pw1": g_pw1, "pw2": g_pw2})
            dr, drb = to_input("cf_pw1_bwd", du, w_pw1, d, parts_of)
        if i < DEPTH - 1:
            dr, drb = sum_layer((dr, drb))
        dr, drb = reduce_layer(i, (dr, drb))
    grad_x = sum_layer(sum_layer((dr, None), last=True), last=True)[0].reshape(1, s, d)

    mine = stacks
    other = (pair_share("pair_share_mixers", mine[:9]) + pair_share("pair_share_up", mine[9:10])
             + pair_share("pair_share_down", mine[10:]))

    def padded(get):
        dqkv = jnp.concatenate([get("mla_w_dq")[0], jnp.pad(get("mla_w_dkv")[0], ((0, 0), (0, 128 - QK_ROPE)))], axis=1)
        uq = jnp.pad(get("mla_w_uq")[0].reshape(Q_LORA, 2, QK_NOPE + QK_ROPE),
                     ((0, 0), (0, 0), (0, HEAD_PAD - QK_NOPE - QK_ROPE))).reshape(Q_LORA, 2 * HEAD_PAD)
        return [get("sc_w_in"), get("sc_w_out"), dqkv[None], uq[None],
                get("mla_w_uk").reshape(1, KV_LORA // N_CHIPS, d), get("mla_w_uv").reshape(1, KV_LORA // N_CHIPS, d),
                get("mla_w_o"), get("cf_w_pw1"), get("cf_w_pw2"), get("ff_w1"), get("ff_w2")]

    w_l, m_l, v_l = (padded(lambda n, p=p: given[p + n]) for p in ("", "m_", "v_"))
    res = [adamw_joined(w_l[k], m_l[k], v_l[k], mine[k], other[k], core) for k in range(len(groups))]

    def unpadded(k):
        r_in, r_out, r_dqkv, r_uq, r_uk, r_uv, r_o, r_pw1, r_pw2, r_w1, r_w2 = (r[k] for r in res)
        return {
            "sc_w_in": r_in, "sc_w_out": r_out, "mla_w_dq": r_dqkv[:, :, 0:Q_LORA],
            "mla_w_dkv": r_dqkv[:, :, Q_LORA:Q_LORA + KV_LORA + QK_ROPE],
            "mla_w_uq": r_uq.reshape(1, Q_LORA, 2, HEAD_PAD)[:, :, :, 0:QK_NOPE + QK_ROPE].reshape(mla_w_uq.shape),
            "mla_w_uk": r_uk.reshape(mla_w_uk.shape), "mla_w_uv": r_uv.reshape(mla_w_uv.shape),
            "mla_w_o": r_o, "cf_w_pw1": r_pw1, "cf_w_pw2": r_pw2, "ff_w1": r_w1, "ff_w2": r_w2}

    big_g, big_d, big_m, big_v = (unpadded(k) for k in range(4))

    pad_row = lambda a: jnp.pad(a, ((0, 0), (0, d - a.shape[1])))
    small_parts = ([g for n in ("ln_mix_g", "ln_mix_b", "ln_ff_g", "ln_ff_b") for g in g_ln[n]]
                   + [pad_row(smalls["g_q"]), pad_row(smalls["g_kv"]), conv_grads[0], conv_grads[1],
                      smalls["b_pw1"].reshape(2, d), smalls["dw_w"], smalls["dw_b"], smalls["norm_g"], smalls["norm_b"],
                      smalls["b_pw2"], loss_part])
    red = all_reduce_small(small_parts, 64)
    loss = red[61, 0]

    where = {
        "ln_mix_g": [((), 0, DEPTH, "all")], "ln_mix_b": [((), 4, DEPTH, "all")],
        "ln_ff_g": [((), 8, DEPTH, "all")], "ln_ff_b": [((), 12, DEPTH, "all")],
        "mla_g_q": [((), 16, 1, Q_LORA)], "mla_g_kv": [((), 17, 1, KV_LORA)],
        "sc_conv_w": [((0,), 18, SC_WIDTH, "chip"), ((1,), 21, SC_WIDTH, "chip")],
        "cf_b_pw1": [((), 24, 2, "chip")], "cf_dw_w": [((0,), 26, CONF_WIDTH, "chip")],
        "cf_dw_b": [((), 57, 1, "chip")], "cf_norm_g": [((), 58, 1, "chip")], "cf_norm_b": [((), 59, 1, "chip")],
        "cf_b_pw2": [((), 60, 1, "chip")]}
    vec = list(where)
    vec_res = vector_update(red, chip, [given[n] for n in vec], [given["m_" + n] for n in vec],
                            [given["v_" + n] for n in vec], [where[n] for n in vec])
    gw = dict(big_g)
    upd = {n: [big_d[n], big_m[n], big_v[n]] for n in big_g}
    for k, n in enumerate(vec):
        gw[n] = vec_res[0][k]
        upd[n] = [vec_res[1][k], vec_res[2][k], vec_res[3][k]]

    return (loss, grad_x, *[gw[n] for n in WEIGHTS], *[upd[n][0] for n in WEIGHTS],
            *[upd[n][1] for n in WEIGHTS], *[upd[n][2] for n in WEIGHTS])
```

```python
import jax
import jax.numpy as jnp
from jax import lax
from jax.experimental import pallas as pl
from jax.experimental.pallas import tpu as pltpu
from jax.experimental.pallas import tpu_sc as plsc

F32 = jnp.float32
BF16 = jnp.bfloat16
MESH = pl.DeviceIdType.MESH

DEPTH = 4
ALPHA = (2.0 * DEPTH) ** 0.25
LN_EPS = 1e-5
RMS_EPS = 1e-6
CHUNK_SHIFT = 6
N_HEADS = 8
QK_NOPE = 128
QK_ROPE = 64
V_HEAD = 128
HEAD_PAD = 256
Q_LORA = 384
KV_LORA = 256
ROPE_THETA = 10000.0
SC_WIDTH = 3
CONF_WIDTH = 31
CONV_PAD = 32
CONV_CHUNK = 64
N_CHIPS = 4
ATTN_SCALE = (QK_NOPE + QK_ROPE) ** -0.5

ADAM_LR = 0.001
ADAM_B1 = 0.9
ADAM_B2 = 0.999
ADAM_EPS = 1e-08
ADAM_WD = 0.01
ADAM_STEP = 10

VMEM_LIMIT = 56 * 2**20

NN = (((1,), (0,)), ((), ()))
NT = (((1,), (1,)), ((), ()))
TN = (((0,), (0,)), ((), ()))


def _params(sem=None):
    return pltpu.CompilerParams(dimension_semantics=sem, vmem_limit_bytes=VMEM_LIMIT)


class Stk:
    def __init__(self, kind, k, n, arr=None):
        self.kind, self.k, self.n = kind, k, n
        self.plain = kind != "col"
        self.nloc = n // N_CHIPS if kind == "col" else n
        self.arr = arr.reshape(k, n) if arr is not None and self.plain else arr

    @property
    def shape(self):
        return (self.k, self.n) if self.plain else (N_CHIPS, self.k, self.nloc)

    def spec(self, bk, bn, f, resident=False):
        if self.plain:
            return pl.BlockSpec((bk, bn), f, pipeline_mode=pl.Buffered(1)) if resident else pl.BlockSpec((bk, bn), f)
        assert self.k % bk == 0 and self.nloc % bn == 0, (self.k, bk, self.nloc, bn)
        pn = self.nloc // bn

        def imap(*g):
            kb, nb = f(*g)
            return nb // pn, kb, nb % pn

        return pl.BlockSpec((None, bk, bn), imap)


def _mm(name, mode, a, b, grid, a_spec, b_spec, acc_shape, extras, extra_specs, out_shapes, out_specs, epi, a_fn=None,
        rows_in_order=False):
    nk = grid[2]
    ne = len(extras)

    def body(*refs):
        a_ref, b_ref = refs[0], refs[1]
        e_refs = refs[2:2 + ne]
        av = a_ref[...] if a_fn is None else a_fn(a_ref[...])
        part = lax.dot_general(av, b_ref[...], mode, preferred_element_type=F32)
        if nk == 1:
            epi(part, e_refs, refs[2 + ne:])
            return
        o_refs = refs[2 + ne:-1]
        acc = refs[-1]
        k = pl.program_id(2)

        @pl.when(k == 0)
        def _():
            acc[...] = part

        @pl.when(k > 0)
        def _():
            acc[...] += part

        @pl.when(k == nk - 1)
        def _():
            epi(acc[...], e_refs, o_refs)

    return pl.pallas_call(
        body, grid=grid, in_specs=[a_spec, b_spec, *extra_specs], out_specs=out_specs, out_shape=out_shapes,
        scratch_shapes=[pltpu.VMEM(acc_shape, F32)] if nk > 1 else [],
        compiler_params=_params(("arbitrary",) * 3 if rows_in_order else ("parallel", "parallel", "arbitrary")),
        name=name)(a, b, *extras)


def _tile(n, t):
    t = min(n, t)
    while n % t:
        t -= 8
    assert t > 0, (n, t)
    return t


def mm_nn(name, a, w, tm, tn, tk, epi, out_shapes, out_specs, extras=(), extra_specs=(), a_spec=None, a_fn=None):
    m = a.shape[0]
    tm, tn, tk = _tile(m, tm), _tile(w.n, tn), _tile(w.k, tk)
    grid = (m // tm, w.n // tn, w.k // tk)
    a_spec = a_spec or pl.BlockSpec((tm, tk), lambda i, j, k: (i, k))
    b_spec = w.spec(tk, tn, lambda i, j, k: (k, j))
    return _mm(name, NN, a, w.arr, grid, a_spec, b_spec, (tm, tn), extras, extra_specs, out_shapes, out_specs, epi, a_fn)


def mm_nt(name, a, w, m, tm, tn, tk, epi, out_shapes, out_specs, extras=(), extra_specs=(), a_spec=None,
          rows_in_order=False, a_fn=None):
    tm, tn, tk = _tile(m, tm), _tile(w.k, tn), _tile(w.n, tk)
    grid = (m // tm, w.k // tn, w.n // tk)
    a_spec = a_spec or pl.BlockSpec((tm, tk), lambda i, j, k: (i, k))
    b_spec = w.spec(tn, tk, lambda i, j, k: (j, k), resident=grid[1] == 1 and grid[2] == 1)
    return _mm(name, NT, a, w.arr, grid, a_spec, b_spec, (tm, tn), extras, extra_specs, out_shapes, out_specs, epi,
               a_fn=a_fn, rows_in_order=rows_in_order)


def mm_tn(name, a, b, dw, s, tm=512, tn=512, tk=4096, a_spec=None, b_spec=None, a_fn=None):
    tm, tn, tk = _tile(dw.k, tm), _tile(dw.n, tn), _tile(s, tk)
    grid = (dw.k // tm, dw.n // tn, s // tk)
    a_spec = a_spec or pl.BlockSpec((tk, tm), lambda i, j, k: (k, i))
    b_spec = b_spec or pl.BlockSpec((tk, tn), lambda i, j, k: (k, j))

    def epi(acc, e, o):
        o[0][...] = acc.astype(BF16)

    out = _mm(name, TN, a, b, grid, a_spec, b_spec, (tm, tn), (), (), [jax.ShapeDtypeStruct(dw.shape, BF16)],
              [dw.spec(tm, tn, lambda i, j, k: (i, j))], epi, a_fn)[0]
    return out.reshape(N_CHIPS, dw.k // N_CHIPS, dw.n) if dw.plain else out


def _sds(shape, dtype):
    return jax.ShapeDtypeStruct(shape, dtype)


def _ij(tm, tn):
    return pl.BlockSpec((tm, tn), lambda i, j, k: (i, j))


def _i0(tm, c):
    return pl.BlockSpec((tm, c), lambda i, j, k: (i, 0))


def _0j(r, tn):
    return pl.BlockSpec((r, tn), lambda i, j, k: (0, j))


def _layer_norm_rows(r, g, b):
    mu = jnp.mean(r, axis=-1, keepdims=True)
    d = r - mu
    var = jnp.mean(d * d, axis=-1, keepdims=True)
    rstd = lax.rsqrt(var + LN_EPS)
    xh = d * rstd
    return xh * g + b, xh, rstd


def mm_residual_ln(name, a, w, x, g, b, bias=None, tm=512, tk=1024, a_fn=None):
    s, d = x.shape
    tm = _tile(s, tm)
    extras = [x, g, b] + ([bias] if bias is not None else [])
    especs = [_i0(tm, d), _0j(1, d), _0j(1, d)] + ([_0j(1, d)] if bias is not None else [])

    def epi(acc, e, o):
        r = ALPHA * e[0][...] + acc
        if bias is not None:
            r = r + e[3][...]
        y, xh, rstd = _layer_norm_rows(r, e[1][...], e[2][...])
        o[0][...] = y
        o[1][...] = y.astype(BF16)
        o[2][...] = xh
        o[3][...] = rstd

    return mm_nn(name, a, w, tm, d, tk, epi,
                 [_sds((s, d), F32), _sds((s, d), BF16), _sds((s, d), F32), _sds((s, 1), F32)],
                 [_i0(tm, d), _i0(tm, d), _i0(tm, d), _i0(tm, 1)], extras, especs, a_fn=a_fn)


def mm_plain_nn(name, a, w, out_dtype, tm=1024, tn=512, tk=1024, bias=None):
    m = a.shape[0]
    tm, tn = _tile(m, tm), _tile(w.n, tn)

    def epi(acc, e, o):
        if bias is not None:
            acc = acc + e[0][...]
        o[0][...] = acc.astype(out_dtype)

    extras, especs = ([bias], [_0j(1, tn)]) if bias is not None else ((), ())
    return mm_nn(name, a, w, tm, tn, tk, epi, [_sds((m, w.n), out_dtype)], [_ij(tm, tn)], extras, especs)[0]


def mm_plain_nt(name, a, w, out_dtype, tm=1024, tn=512, tk=1024, add=None, add_scale=1.0, a_spec_fn=None, a_fn=None):
    m = a.shape[0] if a_spec_fn is None else a_spec_fn[0]
    tm, tn = _tile(m, tm), _tile(w.k, tn)
    tk = _tile(w.n, tk)

    def epi(acc, e, o):
        if add is not None:
            acc = acc + add_scale * e[0][...].astype(F32)
        o[0][...] = acc.astype(out_dtype)

    extras, especs = ([add], [_ij(tm, tn)]) if add is not None else ((), ())
    a_spec = None if a_spec_fn is None else a_spec_fn[1](tm, tk)
    return mm_nt(name, a, w, m, tm, tn, tk, epi, [_sds((m, w.k), out_dtype)], [_ij(tm, tn)], extras, especs,
                 a_spec=a_spec, a_fn=a_fn)[0]


def _rows(tm, c):
    return pl.BlockSpec((tm, c), lambda i: (i, 0))


def _fix(shape):
    nd = len(shape)
    return pl.BlockSpec(shape, lambda i: (0,) * nd)


def _accumulate(ref, val):
    @pl.when(pl.program_id(0) == 0)
    def _():
        ref[...] = jnp.zeros_like(ref)

    ref[...] += val


def _ln_backward_rows(dyv, xh, rstd, g, dr_ref, drb_ref, dg_ref, db_ref, ds_ref):
    dxh = dyv * g
    m1 = jnp.mean(dxh, axis=-1, keepdims=True)
    m2 = jnp.mean(dxh * xh, axis=-1, keepdims=True)
    dr = rstd * (dxh - m1 - xh * m2)
    dr_ref[...] = dr
    drb_ref[...] = dr.astype(BF16)
    _accumulate(dg_ref, jnp.sum(dyv * xh, axis=0, keepdims=True))
    _accumulate(db_ref, jnp.sum(dyv, axis=0, keepdims=True))
    _accumulate(ds_ref, jnp.sum(dr, axis=0, keepdims=True))


def mm_nt_ln_backward(name, a, w, add, xhat, rstd, g, tm=512, tk=1024, a_spec_fn=None, a_fn=None):
    m, d = add.shape
    tm, tk = _tile(m, tm), _tile(w.n, tk)

    def epi(acc, e, o):
        _ln_backward_rows(acc + ALPHA * e[0][...], e[1][...], e[2][...], e[3][...], *o)

    vec = pl.BlockSpec((1, d), lambda i, j, k: (0, 0))
    a_spec = None if a_spec_fn is None else a_spec_fn(tm, tk)
    return mm_nt(name, a, w, m, tm, d, tk, epi,
                 [_sds((m, d), F32), _sds((m, d), BF16), _sds((1, d), F32), _sds((1, d), F32), _sds((1, d), F32)],
                 [_i0(tm, d), _i0(tm, d), vec, vec, vec], [add, xhat, rstd, g],
                 [_i0(tm, d), _i0(tm, d), _i0(tm, 1), vec], a_spec=a_spec, rows_in_order=True, a_fn=a_fn)


def loss_ln_backward(y, target, xhat, rstd, g, tm=512):
    s, d = y.shape
    tm = _tile(s, tm)

    def body(y_ref, t_ref, xh_ref, rstd_ref, g_ref, dr_ref, drb_ref, dg_ref, db_ref, ds_ref, loss_ref):
        e = y_ref[...] - t_ref[...]
        part = 0.5 * jnp.sum(jnp.mean(e * e, axis=-1, keepdims=True), axis=0, keepdims=True)
        _accumulate(loss_ref, jnp.broadcast_to(part, (1, d)))
        _ln_backward_rows(e * (1.0 / d), xh_ref[...], rstd_ref[...], g_ref[...], dr_ref, drb_ref, dg_ref, db_ref, ds_ref)

    return pl.pallas_call(
        body, grid=(s // tm,),
        in_specs=[_rows(tm, d), _rows(tm, d), _rows(tm, d), _rows(tm, 1), _fix((1, d))],
        out_specs=[_rows(tm, d), _rows(tm, d)] + [_fix((1, d))] * 4,
        out_shape=[_sds((s, d), F32), _sds((s, d), BF16)] + [_sds((1, d), F32)] * 4,
        compiler_params=_params(("arbitrary",)), name="loss_ln_backward")(y, target, xhat, rstd, g)


def _cols(s, tc, off=0):
    return pl.BlockSpec((s, tc), lambda i: (0, i + off))


def _shift_down(z, sft, rows):
    return jnp.where(rows >= sft, pltpu.roll(z, sft, 0), 0.0)


def _shift_up(z, sft, rows, s):
    return jnp.where(rows < s - sft, pltpu.roll(z, (s - sft) % s, 0), 0.0)


def short_conv_gate(u, conv_w, tc=256):
    s, d3 = u.shape
    d = d3 // 3
    nb = d // tc

    def body(b_ref, c_ref, h_ref, w_ref, o_ref):
        rows = lax.broadcasted_iota(jnp.int32, (s, tc), 0)
        z = c_ref[...] * h_ref[...]
        cz = jnp.zeros((s, tc), F32)
        for k in range(SC_WIDTH):
            sft = SC_WIDTH - 1 - k
            cz = cz + w_ref[pl.ds(k, 1), :] * (_shift_down(z, sft, rows) if sft else z)
        o_ref[...] = (b_ref[...] * cz).astype(BF16)

    return pl.pallas_call(
        body, grid=(nb,),
        in_specs=[_cols(s, tc), _cols(s, tc, nb), _cols(s, tc, 2 * nb), _cols(SC_WIDTH, tc)],
        out_specs=_cols(s, tc), out_shape=_sds((s, d), BF16),
        compiler_params=_params(("parallel",)), name="short_conv_gate")(u, u, u, conv_w)


def short_conv_gate_bwd(u, conv_w, dg, tc=256):
    s, d3 = u.shape
    d = d3 // 3
    nb = d // tc

    def body(b_ref, c_ref, h_ref, w_ref, dg_ref, du_ref, dw_ref):
        rows = lax.broadcasted_iota(jnp.int32, (s, tc), 0)
        c, h, dgv = c_ref[...], h_ref[...], dg_ref[...]
        z = c * h
        dcz = dgv * b_ref[...]
        cz = jnp.zeros((s, tc), F32)
        dz = jnp.zeros((s, tc), F32)
        for k in range(SC_WIDTH):
            sft = SC_WIDTH - 1 - k
            zs = _shift_down(z, sft, rows) if sft else z
            wk = w_ref[pl.ds(k, 1), :]
            cz = cz + wk * zs
            dz = dz + wk * (_shift_up(dcz, sft, rows, s) if sft else dcz)
            dw_ref[pl.ds(k, 1), :] = jnp.sum(dcz * zs, axis=0, keepdims=True)
        du_ref[0] = (dgv * cz).astype(BF16)
        du_ref[1] = (dz * h).astype(BF16)
        du_ref[2] = (dz * c).astype(BF16)

    return pl.pallas_call(
        body, grid=(nb,),
        in_specs=[_cols(s, tc), _cols(s, tc, nb), _cols(s, tc, 2 * nb), _cols(SC_WIDTH, tc), _cols(s, tc)],
        out_specs=[pl.BlockSpec((3, s, tc), lambda i: (0, 0, i)), _cols(SC_WIDTH, tc)],
        out_shape=[_sds((3, s, d), BF16), _sds((SC_WIDTH, d), F32)],
        compiler_params=_params(("parallel",)), name="short_conv_gate_bwd")(u, u, u, conv_w, dg)


def _store_shifted_down(ref, z, rows):
    s, tc = z.shape
    for b in range(8):
        ref[b, pl.ds(0, CONV_PAD), :] = jnp.zeros((CONV_PAD, tc), F32)
        ref[b, pl.ds(CONV_PAD, s), :] = z if b == 0 else _shift_down(z, b, rows)


def _store_shifted_up(ref, z, rows):
    s, tc = z.shape
    for b in range(8):
        ref[b, pl.ds(0, s), :] = z if b == 0 else _shift_up(z, b, rows, s)
        ref[b, pl.ds(s, CONV_PAD), :] = jnp.zeros((CONV_PAD, tc), F32)


def conformer_glu_conv(u, dw_w, dw_b, tc=128):
    s, d2 = u.shape
    d = d2 // 2
    nb = d // tc

    ch = min(CONV_CHUNK, s)

    def body(a_ref, g_ref, w_ref, b_ref, o_ref, down):
        rows = lax.broadcasted_iota(jnp.int32, (s, tc), 0)
        _store_shifted_down(down, a_ref[...] * jax.nn.sigmoid(g_ref[...]), rows)

        def chunk(ci, carry):
            r0 = pl.multiple_of(ci * ch, ch)
            acc = jnp.broadcast_to(b_ref[...], (ch, tc))
            for k in range(CONF_WIDTH):
                sft = CONF_WIDTH - 1 - k
                acc = acc + w_ref[pl.ds(k, 1), :] * down[sft % 8, pl.ds(CONV_PAD + r0 - (sft // 8) * 8, ch), :]
            o_ref[pl.ds(r0, ch), :] = acc
            return carry

        lax.fori_loop(0, s // ch, chunk, 0)

    return pl.pallas_call(
        body, grid=(nb,),
        in_specs=[_cols(s, tc), _cols(s, tc, nb), _cols(CONF_WIDTH, tc), _cols(1, tc)],
        out_specs=_cols(s, tc), out_shape=_sds((s, d), F32),
        scratch_shapes=[pltpu.VMEM((8, CONV_PAD + s, tc), F32)],
        compiler_params=_params(("parallel",)), name="conformer_glu_conv")(u, u, dw_w, dw_b)


def conformer_glu_conv_bwd(u, dw_w, dhc, tc=128):
    s, d2 = u.shape
    d = d2 // 2
    nb = d // tc
    ch = min(CONV_CHUNK, s)

    def body(a_ref, g_ref, w_ref, dhc_ref, du_ref, dbias_ref, dw_ref, db_ref, down, up, dw_acc, dh_buf):
        rows = lax.broadcasted_iota(jnp.int32, (s, tc), 0)
        a = a_ref[...]
        sg = jax.nn.sigmoid(g_ref[...])
        dhcv = dhc_ref[...]
        _store_shifted_down(down, a * sg, rows)
        _store_shifted_up(up, dhcv, rows)
        dw_acc[...] = jnp.zeros_like(dw_acc)

        def chunk(ci, carry):
            r0 = pl.multiple_of(ci * ch, ch)
            dc = dhc_ref[pl.ds(r0, ch), :]
            dh = jnp.zeros((ch, tc), F32)
            for k in range(CONF_WIDTH):
                sft = CONF_WIDTH - 1 - k
                a8, b = (sft // 8) * 8, sft % 8
                dh = dh + w_ref[pl.ds(k, 1), :] * up[b, pl.ds(r0 + a8, ch), :]
                prod = dc * down[b, pl.ds(CONV_PAD + r0 - a8, ch), :]
                dw_acc[k] += jnp.sum(prod.reshape(ch // 8, 8, tc), axis=0)
            dh_buf[pl.ds(r0, ch), :] = dh
            return carry

        lax.fori_loop(0, s // ch, chunk, 0)
        dh = dh_buf[...]
        da = dh * sg
        dgate = dh * a * sg * (1.0 - sg)
        du_ref[0] = da.astype(BF16)
        du_ref[1] = dgate.astype(BF16)
        dbias_ref[pl.ds(0, 1), :] = jnp.sum(da, axis=0, keepdims=True)
        dbias_ref[pl.ds(1, 1), :] = jnp.sum(dgate, axis=0, keepdims=True)
        db_ref[...] = jnp.sum(dhcv, axis=0, keepdims=True)
        for k in range(CONF_WIDTH):
            dw_ref[pl.ds(k, 1), :] = jnp.sum(dw_acc[k], axis=0, keepdims=True)

    return pl.pallas_call(
        body, grid=(nb,),
        in_specs=[_cols(s, tc), _cols(s, tc, nb), _cols(CONF_WIDTH, tc), _cols(s, tc)],
        out_specs=[pl.BlockSpec((2, s, tc), lambda i: (0, 0, i)), _cols(2, tc), _cols(CONF_WIDTH, tc), _cols(1, tc)],
        out_shape=[_sds((2, s, d), BF16), _sds((2, d), F32), _sds((CONF_WIDTH, d), F32), _sds((1, d), F32)],
        scratch_shapes=[pltpu.VMEM((8, CONV_PAD + s, tc), F32), pltpu.VMEM((8, CONV_PAD + s, tc), F32),
                        pltpu.VMEM((CONF_WIDTH + 1, 8, tc), F32), pltpu.VMEM((s, tc), F32)],
        compiler_params=_params(("parallel",)), name="conformer_glu_conv_bwd")(u, u, dw_w, dhc)


def conformer_norm_swish(hc, g, b, tm=512):
    s, d = hc.shape
    tm = _tile(s, tm)

    def body(h_ref, g_ref, b_ref, o_ref):
        n, _, _ = _layer_norm_rows(h_ref[...], g_ref[...], b_ref[...])
        o_ref[...] = (n * jax.nn.sigmoid(n)).astype(BF16)

    return pl.pallas_call(
        body, grid=(s // tm,), in_specs=[_rows(tm, d), _fix((1, d)), _fix((1, d))], out_specs=_rows(tm, d),
        out_shape=_sds((s, d), BF16), compiler_params=_params(("parallel",)), name="conformer_norm_swish")(hc, g, b)


def conformer_norm_swish_bwd(hc, g, b, ds, tm=512):
    s, d = hc.shape
    tm = _tile(s, tm)

    def body(h_ref, g_ref, b_ref, ds_ref, dh_ref, dg_ref, db_ref):
        n, nh, rstd = _layer_norm_rows(h_ref[...], g_ref[...], b_ref[...])
        sg = jax.nn.sigmoid(n)
        dn = ds_ref[...] * (sg * (1.0 + n * (1.0 - sg)))
        dnh = dn * g_ref[...]
        m1 = jnp.mean(dnh, axis=-1, keepdims=True)
        m2 = jnp.mean(dnh * nh, axis=-1, keepdims=True)
        dh_ref[...] = rstd * (dnh - m1 - nh * m2)
        _accumulate(dg_ref, jnp.sum(dn * nh, axis=0, keepdims=True))
        _accumulate(db_ref, jnp.sum(dn, axis=0, keepdims=True))

    return pl.pallas_call(
        body, grid=(s // tm,), in_specs=[_rows(tm, d), _fix((1, d)), _fix((1, d)), _rows(tm, d)],
        out_specs=[_rows(tm, d), _fix((1, d)), _fix((1, d))],
        out_shape=[_sds((s, d), F32), _sds((1, d), F32), _sds((1, d), F32)],
        compiler_params=_params(("arbitrary",)), name="conformer_norm_swish_bwd")(hc, g, b, ds)


def _swap_halves(x):
    lane = lax.broadcasted_iota(jnp.int32, x.shape, 1)
    return jnp.where(lane < QK_ROPE // 2, pltpu.roll(x, 128 - QK_ROPE // 2, 1), pltpu.roll(x, QK_ROPE // 2, 1))


def _rope(x, cf, sf):
    return x * cf + _swap_halves(x) * sf


def _unrope(dx, cf, sf):
    return dx * cf - _swap_halves(dx) * sf


def _rms_rows(x, g):
    r = lax.rsqrt(jnp.mean(x * x, axis=-1, keepdims=True) + RMS_EPS)
    return x * r, r


def mla_latents(t, g_q, g_kv, cf, sf, tm=512):
    s = t.shape[0]
    tm = _tile(s, tm)

    def body(t_ref, gq_ref, gkv_ref, cf_ref, sf_ref, cq_ref, ckv_ref, kpe_ref):
        xq, _ = _rms_rows(t_ref[:, 0:Q_LORA], gq_ref[...])
        cq_ref[...] = (xq * gq_ref[...]).astype(BF16)
        xkv, _ = _rms_rows(t_ref[:, Q_LORA:Q_LORA + KV_LORA], gkv_ref[...])
        ckv_ref[...] = (xkv * gkv_ref[...]).astype(BF16)
        kpe_ref[...] = _rope(t_ref[:, Q_LORA + KV_LORA:], cf_ref[...], sf_ref[...]).astype(BF16)

    w = Q_LORA + KV_LORA + 128
    return pl.pallas_call(
        body, grid=(s // tm,),
        in_specs=[_rows(tm, w), _fix((1, Q_LORA)), _fix((1, KV_LORA)), _rows(tm, 128), _rows(tm, 128)],
        out_specs=[_rows(tm, Q_LORA), _rows(tm, KV_LORA), _rows(tm, 128)],
        out_shape=[_sds((s, Q_LORA), BF16), _sds((s, KV_LORA), BF16), _sds((s, 128), BF16)],
        compiler_params=_params(("parallel",)), name="mla_latents")(t, g_q, g_kv, cf, sf)


def mla_latents_bwd(t, g_q, g_kv, cf, sf, dcq, dckv, dkpe, tm=512):
    s = t.shape[0]
    tm = _tile(s, tm)
    w = Q_LORA + KV_LORA + 128

    def rms_bwd(x, g, dy):
        xh, r = _rms_rows(x, g)
        dxh = dy * g
        return r * (dxh - xh * jnp.mean(dxh * xh, axis=-1, keepdims=True)), jnp.sum(dy * xh, axis=0, keepdims=True)

    def body(t_ref, gq_ref, gkv_ref, cf_ref, sf_ref, dcq_ref, dckv_ref, dkpe_ref, dt_ref, dgq_ref, dgkv_ref):
        dxq, dgq = rms_bwd(t_ref[:, 0:Q_LORA], gq_ref[...], dcq_ref[...])
        dxkv, dgkv = rms_bwd(t_ref[:, Q_LORA:Q_LORA + KV_LORA], gkv_ref[...], dckv_ref[...])
        dt_ref[:, 0:Q_LORA] = dxq.astype(BF16)
        dt_ref[:, Q_LORA:Q_LORA + KV_LORA] = dxkv.astype(BF16)
        dt_ref[:, Q_LORA + KV_LORA:] = _unrope(dkpe_ref[...], cf_ref[...], sf_ref[...]).astype(BF16)
        _accumulate(dgq_ref, dgq)
        _accumulate(dgkv_ref, dgkv)

    return pl.pallas_call(
        body, grid=(s // tm,),
        in_specs=[_rows(tm, w), _fix((1, Q_LORA)), _fix((1, KV_LORA)), _rows(tm, 128), _rows(tm, 128),
                  _rows(tm, Q_LORA), _rows(tm, KV_LORA), _rows(tm, 128)],
        out_specs=[_rows(tm, w), _fix((1, Q_LORA)), _fix((1, KV_LORA))],
        out_shape=[_sds((s, w), BF16), _sds((1, Q_LORA), F32), _sds((1, KV_LORA), F32)],
        compiler_params=_params(("arbitrary",)), name="mla_latents_bwd")(t, g_q, g_kv, cf, sf, dcq, dckv, dkpe)


def mla_queries(cq, w_uq, cf, sf, tm=2048):
    s = cq.shape[0]
    tm = _tile(s, tm)

    def epi(acc, e, o):
        o[0][:, 0:QK_NOPE] = acc[:, 0:QK_NOPE].astype(BF16)
        o[0][:, QK_NOPE:] = _rope(acc[:, QK_NOPE:], e[0][...], e[1][...]).astype(BF16)

    return mm_nn("mla_queries", cq, w_uq, tm, HEAD_PAD, Q_LORA, epi, [_sds((s, N_HEADS * HEAD_PAD), BF16)],
                 [_ij(tm, HEAD_PAD)], [cf, sf], [_i0(tm, 128), _i0(tm, 128)])[0]


def mla_keys(ckv, w_uk, kpe, tm=2048):
    s = ckv.shape[0]
    tm = _tile(s, tm)

    def epi(acc, e, o):
        o[0][:, 0:QK_NOPE] = acc.astype(BF16)
        o[0][:, QK_NOPE:] = e[0][...]

    return mm_nn("mla_keys", ckv, w_uk, tm, QK_NOPE, KV_LORA, epi, [_sds((s, N_HEADS * HEAD_PAD), BF16)],
                 [_ij(tm, HEAD_PAD)], [kpe], [_i0(tm, 128)])[0]


def _masked_scores(q, k, tq, kv):
    sc = lax.dot_general(q, k, NT, preferred_element_type=F32) * ATTN_SCALE
    row = lax.broadcasted_iota(jnp.int32, (tq, tq), 0)
    col = lax.broadcasted_iota(jnp.int32, (tq, tq), 1)
    ok = lax.shift_right_logical(col, CHUNK_SHIFT) <= lax.shift_right_logical(row, CHUNK_SHIFT)
    own = jnp.where(ok, sc[:, kv - tq:], -1e30)
    return own if kv == tq else jnp.concatenate([sc[:, :kv - tq], own], axis=1)


def attention(q, k, v, tq=512):
    s = q.shape[0]
    tq = _tile(s, tq)
    nq = s // tq

    def body(q_ref, k_ref, v_ref, o_ref):
        for qi in range(nq):
            kv = (qi + 1) * tq
            sc = _masked_scores(q_ref[pl.ds(qi * tq, tq), :], k_ref[pl.ds(0, kv), :], tq, kv)
            p = jnp.exp(sc - jnp.max(sc, axis=-1, keepdims=True))
            o = lax.dot_general(p.astype(BF16), v_ref[pl.ds(0, kv), :], NN, preferred_element_type=F32)
            o_ref[pl.ds(qi * tq, tq), :] = (o / jnp.sum(p, axis=-1, keepdims=True)).astype(BF16)

    hq = pl.BlockSpec((s, HEAD_PAD), lambda h: (0, h))
    hv = pl.BlockSpec((s, V_HEAD), lambda h: (0, h))
    return pl.pallas_call(
        body, grid=(N_HEADS,), in_specs=[hq, hq, hv], out_specs=hv, out_shape=_sds((s, N_HEADS * V_HEAD), BF16),
        compiler_params=_params(("parallel",)), name="attention")(q, k, v)


def attention_bwd(q, k, v, do, tq=512):
    s = q.shape[0]
    tq = _tile(s, tq)
    nq = s // tq

    def body(q_ref, k_ref, v_ref, do_ref, dq_ref, dk_ref, dv_ref, dk_acc, dv_acc):
        dk_acc[...] = jnp.zeros_like(dk_acc)
        dv_acc[...] = jnp.zeros_like(dv_acc)
        for qi in range(nq):
            kv = (qi + 1) * tq
            qt = q_ref[pl.ds(qi * tq, tq), :]
            kt = k_ref[pl.ds(0, kv), :]
            dot = do_ref[pl.ds(qi * tq, tq), :]
            sc = _masked_scores(qt, kt, tq, kv)
            p = jnp.exp(sc - jnp.max(sc, axis=-1, keepdims=True))
            p = p / jnp.sum(p, axis=-1, keepdims=True)
            dp = lax.dot_general(dot, v_ref[pl.ds(0, kv), :], NT, preferred_element_type=F32)
            delta = jnp.sum(p * dp, axis=-1, keepdims=True)
            ds = (p * (dp - delta) * ATTN_SCALE).astype(BF16)
            dq_ref[pl.ds(qi * tq, tq), :] = lax.dot_general(ds, kt, NN, preferred_element_type=F32).astype(BF16)
            dk_acc[pl.ds(0, kv), :] += lax.dot_general(ds, qt, TN, preferred_element_type=F32)
            dv_acc[pl.ds(0, kv), :] += lax.dot_general(p.astype(BF16), dot, TN, preferred_element_type=F32)
        dk_ref[...] = dk_acc[...].astype(BF16)
        dv_ref[...] = dv_acc[...].astype(BF16)

    hq = pl.BlockSpec((s, HEAD_PAD), lambda h: (0, h))
    hv = pl.BlockSpec((s, V_HEAD), lambda h: (0, h))
    return pl.pallas_call(
        body, grid=(N_HEADS,), in_specs=[hq, hq, hv, hv], out_specs=[hq, hq, hv],
        out_shape=[_sds((s, N_HEADS * HEAD_PAD), BF16), _sds((s, N_HEADS * HEAD_PAD), BF16),
                   _sds((s, N_HEADS * V_HEAD), BF16)],
        scratch_shapes=[pltpu.VMEM((s, HEAD_PAD), F32), pltpu.VMEM((s, V_HEAD), F32)],
        compiler_params=_params(("parallel",)), name="attention_bwd")(q, k, v, do)


def mla_unrope_grads(dq, dk, cf, sf, tm=512):
    s = dq.shape[0]
    tm = _tile(s, tm)

    def body(dq_ref, dk_ref, cf_ref, sf_ref, dql_ref, dkn_ref, dkpe_ref):
        cfv, sfv = cf_ref[...], sf_ref[...]
        dkpe = jnp.zeros((tm, 128), F32)
        for h in range(N_HEADS):
            lo = h * HEAD_PAD
            dql_ref[:, lo:lo + QK_NOPE] = dq_ref[:, lo:lo + QK_NOPE]
            dql_ref[:, lo + QK_NOPE:lo + HEAD_PAD] = _unrope(
                dq_ref[:, lo + QK_NOPE:lo + HEAD_PAD].astype(F32), cfv, sfv).astype(BF16)
            dkn_ref[:, h * QK_NOPE:(h + 1) * QK_NOPE] = dk_ref[:, lo:lo + QK_NOPE]
            dkpe = dkpe + dk_ref[:, lo + QK_NOPE:lo + HEAD_PAD].astype(F32)
        dkpe_ref[...] = dkpe

    wq = N_HEADS * HEAD_PAD
    return pl.pallas_call(
        body, grid=(s // tm,), in_specs=[_rows(tm, wq), _rows(tm, wq), _rows(tm, 128), _rows(tm, 128)],
        out_specs=[_rows(tm, wq), _rows(tm, N_HEADS * QK_NOPE), _rows(tm, 128)],
        out_shape=[_sds((s, wq), BF16), _sds((s, N_HEADS * QK_NOPE), BF16), _sds((s, 128), F32)],
        compiler_params=_params(("parallel",)), name="mla_unrope_grads")(dq, dk, cf, sf)


ANY = pl.BlockSpec(memory_space=pl.ANY)
GATHER_ID = 1
CHIP_EXCHANGE_ID = 2
PAIR_ID = 3
ALL_ID = 4


def _nbytes(a):
    return a.size * a.dtype.itemsize


def _copy_cost(operand_bytes, sent_fraction):
    sent = int(operand_bytes * sent_fraction)
    return pl.CostEstimate(flops=0, transcendentals=0, bytes_accessed=2 * sent, remote_bytes_transferred=sent)


def _handshake(peers):
    barrier = pltpu.get_barrier_semaphore()
    for peer in peers:
        pl.semaphore_signal(barrier, inc=1, device_id=peer, device_id_type=MESH)
    pl.semaphore_wait(barrier, len(peers))


def _place():
    x, y, c = lax.axis_index("x"), lax.axis_index("y"), lax.axis_index("c")
    chips = [(1 - x, y), (x, 1 - y), (1 - x, 1 - y)]
    return x, y, c, chips


def _half(ref, hc, axis=0):
    n = ref.shape[axis] // 2
    idx = (slice(None),) * axis + (pl.ds(hc * n, n),)
    return ref.at[idx]


def gather_shards(name, tensors, by_columns=()):
    nt = len(tensors)

    def body(*refs):
        a, g = refs[:nt], refs[nt:2 * nt]
        send, recv = refs[2 * nt:]
        x, y, c, _ = _place()
        q = 2 * x + y
        sib, xn, yn = (x, y, 1 - c), (1 - x, y, c), (x, 1 - y, c)
        q_xn, q_yn, q_diag = 2 * (1 - x) + y, 2 * x + 1 - y, 2 * (1 - x) + 1 - y
        _handshake([sib, xn, yn])

        def whole(t, p):
            if t in by_columns:
                n = a[t].shape[1]
                return g[t].at[:, pl.ds(p * n, n)]
            return g[t].at[p]

        def part(t, p, hc, quarter=None):
            rows = a[t].shape[0]
            if quarter is None:
                return whole(t, p).at[pl.ds(hc * (rows // 2), rows // 2)]
            return whole(t, p).at[pl.ds(hc * (rows // 2) + quarter * (rows // 4), rows // 4)]

        def rc(t, k, src, dst, to):
            return pltpu.make_async_remote_copy(src_ref=src, dst_ref=dst, send_sem=send.at[t, k], recv_sem=recv.at[t, k],
                                                device_id=to, device_id_type=MESH)

        sent = []

        def go(cp):
            cp.start()
            sent.append(cp)

        def landed(t, k, piece, frm):
            rc(t, k, piece, piece, frm).wait_recv()
            return piece

        for t in range(nt):
            go(rc(t, 8, a[t], whole(t, q), sib))
            mine = _half(a[t], c)
            go(rc(t, 0, mine, part(t, q, c), xn))
            go(rc(t, 1, mine, part(t, q, c), yn))
        for t in range(nt):
            from_y = landed(t, 1, part(t, q_yn, c), yn)
            go(rc(t, 2, part(t, q_yn, c, 0), part(t, q_yn, c, 0), xn))
            go(rc(t, 5, from_y, from_y, sib))
            from_x = landed(t, 0, part(t, q_xn, c), xn)
            go(rc(t, 3, part(t, q_xn, c, 1), part(t, q_xn, c, 1), yn))
            go(rc(t, 4, from_x, from_x, sib))
        for t in range(nt):
            for k, frm in ((2, xn), (3, yn)):
                piece = landed(t, k, part(t, q_diag, c, k - 2), frm)
                go(rc(t, 4 + k, piece, piece, sib))
        for t in range(nt):
            landed(t, 4, part(t, q_xn, 1 - c), sib)
            landed(t, 5, part(t, q_yn, 1 - c), sib)
            landed(t, 6, part(t, q_diag, 1 - c, 0), sib)
            landed(t, 7, part(t, q_diag, 1 - c, 1), sib)
            landed(t, 8, whole(t, q), sib)
        for cp in sent:
            cp.wait_send()

    return pl.kernel(
        body, name=name,
        out_type=[_sds((a.shape[0], N_CHIPS * a.shape[1]) if t in by_columns else (N_CHIPS,) + a.shape, a.dtype)
                  for t, a in enumerate(tensors)],
        mesh=plsc.ScalarSubcoreMesh(axis_name="sequencer", num_cores=1),
        scratch_types=[pltpu.SemaphoreType.DMA((nt, 9)), pltpu.SemaphoreType.DMA((nt, 9))],
        cost_estimate=_copy_cost(sum(_nbytes(a) for a in tensors), 4),
        compiler_params=pltpu.CompilerParams(collective_id=GATHER_ID))(*tensors)


def pair_exchange(name, grads, on_sequencer):
    nt = len(grads)

    def body(*refs):
        g, theirs = refs[:nt], refs[nt:2 * nt]
        send, recv = refs[2 * nt:]
        x, y, c, _ = _place()
        if on_sequencer:
            _handshake([(x, y, 1 - c)])
        cps = []
        for t in range(nt):
            cp = pltpu.make_async_remote_copy(src_ref=_half(g[t], 1 - c, 1), dst_ref=theirs[t], send_sem=send.at[t],
                                              recv_sem=recv.at[t], device_id=(x, y, 1 - c), device_id_type=MESH)
            cp.start()
            cps.append(cp)
        for cp in cps:
            cp.wait()

    if not on_sequencer:
        return pl.pallas_call(
            body, in_specs=[ANY] * nt, out_specs=[ANY] * nt,
            out_shape=[_sds((N_CHIPS, a.shape[1] // 2, a.shape[2]), a.dtype) for a in grads],
            scratch_shapes=[pltpu.SemaphoreType.DMA((nt,)), pltpu.SemaphoreType.DMA((nt,))],
            name=name)(*grads)
    return pl.kernel(
        body, name=name, out_type=[_sds((N_CHIPS, a.shape[1] // 2, a.shape[2]), a.dtype) for a in grads],
        mesh=plsc.ScalarSubcoreMesh(axis_name="sequencer", num_cores=1),
        scratch_types=[pltpu.SemaphoreType.DMA((nt,)), pltpu.SemaphoreType.DMA((nt,))],
        cost_estimate=_copy_cost(sum(_nbytes(a) for a in grads), 0.5),
        compiler_params=pltpu.CompilerParams(collective_id=PAIR_ID))(*grads)


def chip_exchange(name, parts):
    nt = len(parts)

    def body(*refs):
        a, r = refs[:nt], refs[nt:2 * nt]
        send, recv = refs[2 * nt:]
        x, y, c, chips = _place()
        _handshake([(*chip, c) for chip in chips])
        cps = []
        for t in range(nt):
            for j, chip in enumerate(chips):
                cp = pltpu.make_async_remote_copy(
                    src_ref=a[t].at[2 * chip[0] + chip[1]], dst_ref=r[t].at[j], send_sem=send.at[t, j],
                    recv_sem=recv.at[t, j], device_id=(*chip, c), device_id_type=MESH)
                cp.start()
                cps.append(cp)
        for cp in cps:
            cp.wait()

    return pl.kernel(
        body, name=name, out_type=[_sds((N_CHIPS - 1,) + a.shape[1:], a.dtype) for a in parts],
        mesh=plsc.ScalarSubcoreMesh(axis_name="sequencer", num_cores=1),
        scratch_types=[pltpu.SemaphoreType.DMA((nt, 3)), pltpu.SemaphoreType.DMA((nt, 3))],
        cost_estimate=_copy_cost(sum(_nbytes(a) for a in parts), 0.75),
        compiler_params=pltpu.CompilerParams(collective_id=CHIP_EXCHANGE_ID))(*parts)


def pair_share(name, halves):
    nt = len(halves)

    def body(*refs):
        h, other = refs[:nt], refs[nt:2 * nt]
        send, recv = refs[2 * nt:]
        x, y, c, _ = _place()
        _handshake([(x, y, 1 - c)])
        cps = []
        for t in range(nt):
            cp = pltpu.make_async_remote_copy(src_ref=h[t], dst_ref=other[t], send_sem=send.at[t], recv_sem=recv.at[t],
                                              device_id=(x, y, 1 - c), device_id_type=MESH)
            cp.start()
            cps.append(cp)
        for cp in cps:
            cp.wait()

    return pl.kernel(
        body, name=name, out_type=[_sds(a.shape, a.dtype) for a in halves],
        mesh=plsc.ScalarSubcoreMesh(axis_name="sequencer", num_cores=1),
        scratch_types=[pltpu.SemaphoreType.DMA((nt,)), pltpu.SemaphoreType.DMA((nt,))],
        cost_estimate=_copy_cost(sum(_nbytes(a) for a in halves), 1),
        compiler_params=pltpu.CompilerParams(collective_id=PAIR_ID))(*halves)


def pack_rows(name, parts, rows):
    cdim = parts[0].shape[1]
    n = len(parts)
    vm = pl.BlockSpec(memory_space=pltpu.VMEM)

    def pack(*refs):
        p, o_ref = refs[:n], refs[n]
        at = 0
        for ref in p:
            o_ref[pl.ds(at, ref.shape[0]), :] = ref[...]
            at += ref.shape[0]
        o_ref[pl.ds(at, rows - at), :] = jnp.zeros((rows - at, cdim), F32)

    return pl.pallas_call(pack, in_specs=[vm] * n, out_specs=vm, out_shape=_sds((rows, cdim), F32), name=name)(*parts)


def all_reduce_small(parts, rows):
    cdim = parts[0].shape[1]
    vm = pl.BlockSpec(memory_space=pltpu.VMEM)
    mine = pack_rows("small_pack", parts, rows)

    def exchange(mine_ref, buf, send, recv, lsem):
        x, y, c, _ = _place()
        me = 4 * x + 2 * y + c
        peers = [(x ^ (k >> 2), y ^ ((k >> 1) & 1), c ^ (k & 1)) for k in range(1, 8)]
        _handshake(peers)
        own = pltpu.make_async_copy(mine_ref, buf.at[me], lsem)
        own.start()
        cps = []
        for k, to in enumerate(peers):
            cp = pltpu.make_async_remote_copy(src_ref=mine_ref, dst_ref=buf.at[me], send_sem=send.at[k], recv_sem=recv.at[k],
                                              device_id=to, device_id_type=MESH)
            cp.start()
            cps.append(cp)
        for k, (px, py, pc) in enumerate(peers):
            pltpu.make_async_remote_copy(src_ref=mine_ref, dst_ref=buf.at[4 * px + 2 * py + pc], send_sem=send.at[k],
                                         recv_sem=recv.at[k], device_id=(x, y, c), device_id_type=MESH).wait_recv()
        for cp in cps:
            cp.wait_send()
        own.wait()

    landed = pl.kernel(
        exchange, name="small_exchange", out_type=_sds((8, rows, cdim), F32),
        mesh=plsc.ScalarSubcoreMesh(axis_name="sequencer", num_cores=1),
        scratch_types=[pltpu.SemaphoreType.DMA((7,)), pltpu.SemaphoreType.DMA((7,)), pltpu.SemaphoreType.DMA],
        cost_estimate=_copy_cost(rows * cdim * 4, 7),
        compiler_params=pltpu.CompilerParams(collective_id=ALL_ID))(mine)

    def total(buf, o_ref):
        acc = buf[0]
        for d in range(1, 8):
            acc = acc + buf[d]
        o_ref[...] = acc

    return pl.pallas_call(total, in_specs=[vm], out_specs=vm, out_shape=_sds((rows, cdim), F32), name="small_sum")(landed)


def pair_sum(gs, theirs, core, tm=256):
    n = len(gs)
    _, r, c = gs[0].shape
    tm = _tile(r // 2, tm)
    nh = r // 2 // tm

    def body(core_ref, *refs):
        for a_ref, b_ref, o_ref in zip(refs[:n], refs[n:2 * n], refs[2 * n:]):
            o_ref[...] = (a_ref[...].astype(F32) + b_ref[...].astype(F32)).astype(BF16)

    blk = (N_CHIPS, tm, c)
    own = pl.BlockSpec(blk, lambda i, cr: (0, cr[0] * nh + i, 0))
    half = pl.BlockSpec(blk, lambda i, cr: (0, i, 0))
    return pl.pallas_call(
        body, grid_spec=pltpu.PrefetchScalarGridSpec(
            num_scalar_prefetch=1, grid=(nh,), in_specs=[own] * n + [half] * n, out_specs=[half] * n),
        out_shape=[_sds(t.shape, BF16) for t in theirs], compiler_params=_params(("parallel",)),
        name="pair_sum")(core, *gs, *theirs)


def chip_sum(own, landed, chip, stack, layer, layers, tm=256):
    _, r, c = own.shape
    tm = _tile(r, tm)

    def body(chip_ref, own_ref, l_ref, *rest):
        acc = own_ref[...].astype(F32)
        for j in range(N_CHIPS - 1):
            acc = acc + l_ref[j].astype(F32)
        rest[-1][...] = acc

    in_specs = [pl.BlockSpec((None, tm, c), lambda i, qr: (qr[0], i, 0)),
                pl.BlockSpec((N_CHIPS - 1, tm, c), lambda i, qr: (0, i, 0))]
    args = [chip, own, landed]
    if stack is not None:
        in_specs.append(ANY)
        args.append(stack)
    return pl.pallas_call(
        body, grid_spec=pltpu.PrefetchScalarGridSpec(
            num_scalar_prefetch=1, grid=(r // tm,), in_specs=in_specs,
            out_specs=pl.BlockSpec((None, tm, c), lambda i, qr: (layer, i, 0))),
        out_shape=_sds((layers, r, c), F32), input_output_aliases={3: 0} if stack is not None else {},
        compiler_params=_params(("parallel",)), name="chip_sum")(*args)


def _adamw_math(w, g, m, v):
    bc1 = 1.0 - ADAM_B1 ** ADAM_STEP
    bc2 = 1.0 - ADAM_B2 ** ADAM_STEP
    nm = ADAM_B1 * m + (1.0 - ADAM_B1) * g
    nv = ADAM_B2 * v + (1.0 - ADAM_B2) * (g * g)
    return -ADAM_LR * ((nm / bc1) / (jnp.sqrt(nv / bc2) + ADAM_EPS) + ADAM_WD * w), nm, nv


def vector_update(red, chip, ws, ms, vs, where):
    n = len(ws)
    dd = red.shape[1]

    def body(chip_ref, red_ref, *refs):
        w_r, m_r, v_r = refs[0:n], refs[n:2 * n], refs[2 * n:3 * n]
        g_o, d_o, m_o, v_o = (refs[(3 + k) * n:(4 + k) * n] for k in range(4))
        q = chip_ref[0]

        def chip_block(val, width):
            out = val[:, 0:width]
            for p in range(1, val.shape[1] // width):
                out = jnp.where(q == p, val[:, p * width:(p + 1) * width], out)
            return out

        for k in range(n):
            for idx, r0, nr, cols in where[k]:
                width = w_r[k].shape[-1]
                if cols == "chip" and width * N_CHIPS != dd:
                    g = chip_block(jnp.concatenate([red_ref[pl.ds(r0 + j, 1), :] for j in range(nr)], axis=1), width)
                else:
                    g = red_ref[pl.ds(r0, nr), :]
                    g = chip_block(g, width) if cols == "chip" else g if cols == "all" else g[:, 0:cols]
                delta, nm, nv = _adamw_math(w_r[k][idx], g, m_r[k][idx], v_r[k][idx])
                g_o[k][idx] = g
                d_o[k][idx] = delta
                m_o[k][idx] = nm
                v_o[k][idx] = nv

    vm = pl.BlockSpec(memory_space=pltpu.VMEM)
    outs = pl.pallas_call(
        body, in_specs=[pl.BlockSpec(memory_space=pltpu.SMEM), vm] + [vm] * (3 * n), out_specs=[vm] * (4 * n),
        out_shape=[_sds(w.shape, F32) for w in ws] * 4, name="vector_update")(chip, red, *ws, *ms, *vs)
    return [outs[k * n:(k + 1) * n] for k in range(4)]


def adamw_joined(w, m, v, g_mine, g_theirs, core, tm=512):
    nl, r, c = w.shape
    tm = _tile(r // 2, tm)
    nh = r // 2 // tm

    def body(core_ref, w_ref, m_ref, v_ref, gm_ref, gt_ref, g_ref, d_ref, nm_ref, nv_ref):
        mine = (pl.program_id(1) // nh) == core_ref[0]
        gv = jnp.where(mine, gm_ref[...], gt_ref[...])
        g_ref[...] = gv
        d_ref[...], nm_ref[...], nv_ref[...] = _adamw_math(w_ref[...], gv, m_ref[...], v_ref[...])

    full = pl.BlockSpec((None, tm, c), lambda l, i, cr: (l, i, 0))
    mine = pl.BlockSpec((None, tm, c), lambda l, i, cr: (l, jnp.where(i // nh == cr[0], i % nh, 0), 0))
    theirs = pl.BlockSpec((None, tm, c), lambda l, i, cr: (l, jnp.where(i // nh == cr[0], 0, i % nh), 0))
    return pl.pallas_call(
        body, grid_spec=pltpu.PrefetchScalarGridSpec(
            num_scalar_prefetch=1, grid=(nl, r // tm), in_specs=[full, full, full, mine, theirs], out_specs=[full] * 4),
        out_shape=[_sds((nl, r, c), F32)] * 4, compiler_params=_params(("parallel", "parallel")),
        name="adamw_joined")(core, w, m, v, g_mine, g_theirs)


WEIGHTS = ['sc_w_in', 'sc_conv_w', 'sc_w_out', 'mla_w_dq', 'mla_g_q', 'mla_w_uq', 'mla_w_dkv', 'mla_g_kv', 'mla_w_uk',
           'mla_w_uv', 'mla_w_o', 'cf_w_pw1', 'cf_b_pw1', 'cf_dw_w', 'cf_dw_b', 'cf_norm_g', 'cf_norm_b', 'cf_w_pw2',
           'cf_b_pw2', 'ff_w1', 'ff_w2', 'ln_mix_g', 'ln_mix_b', 'ln_ff_g', 'ln_ff_b']
ARGS = ['x'] + WEIGHTS + ['loss_target'] + ['m_' + n for n in WEIGHTS] + ['v_' + n for n in WEIGHTS]


def _sq_relu(h):
    r = jnp.maximum(h, jnp.zeros_like(h))
    return r * r


def _mlp_forward(i, x, xb, w1, w2, g, b):
    hb = mm_plain_nn(f"mlp{i}_up", xb, w1, BF16, tm=2048, tn=1024)
    y, yb, xh, rstd = mm_residual_ln(f"mlp{i}_down_ln", hb, w2, x, g, b, tk=4096, a_fn=_sq_relu)
    return (y, yb), dict(xb=xb, hb=hb, xh=xh, rstd=rstd, g=g)


def _mlp_backward(i, dr, drb, sv, w1, w2, dw1, dw2, reduce_after, mixer_ln):
    s = dr.shape[0]
    tm, tn = _tile(s, 1024), 1024

    def epi(acc, e, o):
        o[0][...] = (acc * (2.0 * jnp.maximum(e[0][...].astype(F32), 0.0))).astype(BF16)

    dhb = mm_nt(f"mlp{i}_down_bwd", drb, w2, s, tm, tn, 1024, epi, [_sds((s, w2.k), BF16)], [_ij(tm, tn)],
                [sv["hb"]], [_ij(tm, tn)])[0]
    g_w2 = mm_tn(f"mlp{i}_dw2", sv["hb"], drb, dw2, s, 1024, 1024, a_fn=_sq_relu)
    g_w1 = mm_tn(f"mlp{i}_dw1", sv["xb"], dhb, dw1, s, 1024, 1024)
    dhb = reduce_after(dhb, {f"w1_{i}": g_w1, f"w2_{i}": g_w2})
    return mm_nt_ln_backward(f"mlp{i}_up_bwd", dhb, w1, dr, *mixer_ln, tk=4096)


def kernel(x, sc_w_in, sc_conv_w, sc_w_out, mla_w_dq, mla_g_q, mla_w_uq, mla_w_dkv, mla_g_kv, mla_w_uk, mla_w_uv, mla_w_o, cf_w_pw1, cf_b_pw1, cf_dw_w, cf_dw_b, cf_norm_g, cf_norm_b, cf_w_pw2, cf_b_pw2, ff_w1, ff_w2, ln_mix_g, ln_mix_b, ln_ff_g, ln_ff_b, loss_target, m_sc_w_in, m_sc_conv_w, m_sc_w_out, m_mla_w_dq, m_mla_g_q, m_mla_w_uq, m_mla_w_dkv, m_mla_g_kv, m_mla_w_uk, m_mla_w_uv, m_mla_w_o, m_cf_w_pw1, m_cf_b_pw1, m_cf_dw_w, m_cf_dw_b, m_cf_norm_g, m_cf_norm_b, m_cf_w_pw2, m_cf_b_pw2, m_ff_w1, m_ff_w2, m_ln_mix_g, m_ln_mix_b, m_ln_ff_g, m_ln_ff_b, v_sc_w_in, v_sc_conv_w, v_sc_w_out, v_mla_w_dq, v_mla_g_q, v_mla_w_uq, v_mla_w_dkv, v_mla_g_kv, v_mla_w_uk, v_mla_w_uv, v_mla_w_o, v_cf_w_pw1, v_cf_b_pw1, v_cf_dw_w, v_cf_dw_b, v_cf_norm_g, v_cf_norm_b, v_cf_w_pw2, v_cf_b_pw2, v_ff_w1, v_ff_w2, v_ln_mix_g, v_ln_mix_b, v_ln_ff_g, v_ln_ff_b):
    given = dict(zip(ARGS, (x, sc_w_in, sc_conv_w, sc_w_out, mla_w_dq, mla_g_q, mla_w_uq, mla_w_dkv, mla_g_kv, mla_w_uk, mla_w_uv, mla_w_o, cf_w_pw1, cf_b_pw1, cf_dw_w, cf_dw_b, cf_norm_g, cf_norm_b, cf_w_pw2, cf_b_pw2, ff_w1, ff_w2, ln_mix_g, ln_mix_b, ln_ff_g, ln_ff_b, loss_target, m_sc_w_in, m_sc_conv_w, m_sc_w_out, m_mla_w_dq, m_mla_g_q, m_mla_w_uq, m_mla_w_dkv, m_mla_g_kv, m_mla_w_uk, m_mla_w_uv, m_mla_w_o, m_cf_w_pw1, m_cf_b_pw1, m_cf_dw_w, m_cf_dw_b, m_cf_norm_g, m_cf_norm_b, m_cf_w_pw2, m_cf_b_pw2, m_ff_w1, m_ff_w2, m_ln_mix_g, m_ln_mix_b, m_ln_ff_g, m_ln_ff_b, v_sc_w_in, v_sc_conv_w, v_sc_w_out, v_mla_w_dq, v_mla_g_q, v_mla_w_uq, v_mla_w_dkv, v_mla_g_kv, v_mla_w_uk, v_mla_w_uv, v_mla_w_o, v_cf_w_pw1, v_cf_b_pw1, v_cf_dw_w, v_cf_dw_b, v_cf_norm_g, v_cf_norm_b, v_cf_w_pw2, v_cf_b_pw2, v_ff_w1, v_ff_w2, v_ln_mix_g, v_ln_mix_b, v_ln_ff_g, v_ln_ff_b)))
    s, d = x.shape[1], x.shape[2]
    d_ff = 4 * d
    dq4 = d // N_CHIPS
    xq = lax.axis_index("x") * 2 + lax.axis_index("y")

    w_dkv_pad = jnp.pad(mla_w_dkv[0], ((0, 0), (0, 128 - QK_ROPE)))
    w_uq_pad = jnp.pad(mla_w_uq[0].reshape(Q_LORA, 2, QK_NOPE + QK_ROPE), ((0, 0), (0, 0), (0, HEAD_PAD - QK_NOPE - QK_ROPE)))
    small = pack_rows("vector_weights_pack", [
        sc_conv_w.reshape(2 * SC_WIDTH, dq4), cf_b_pw1.reshape(2, dq4), cf_dw_w[0], cf_dw_b, cf_norm_g, cf_norm_b,
        cf_b_pw2], 64)
    mlp_w = lambda i: [ff_w1[i].astype(BF16), ff_w2[i].astype(BF16)]
    g_in, g_out, g_w1, g_w2 = [None] * 2, [None] * 2, [None] * DEPTH, [None] * DEPTH
    g_in[0], g_out[0], g_small = gather_shards(
        "gather_mixer0", [sc_w_in[0].astype(BF16), sc_w_out[0].astype(BF16), small], by_columns=(0,))
    (g_w1[0],) = gather_shards("gather_up0", [ff_w1[0].astype(BF16)], by_columns=(0,))
    (g_w2[0],) = gather_shards("gather_down0", [ff_w2[0].astype(BF16)])
    g_dqkv, g_uq, g_uk, g_uv, g_o = gather_shards("gather_mixer1", [
        jnp.concatenate([mla_w_dq[0], w_dkv_pad], axis=1).astype(BF16),
        w_uq_pad.reshape(Q_LORA, 2 * HEAD_PAD).astype(BF16),
        mla_w_uk.reshape(KV_LORA // N_CHIPS, N_HEADS * QK_NOPE).astype(BF16),
        mla_w_uv.reshape(KV_LORA // N_CHIPS, N_HEADS * V_HEAD).astype(BF16), mla_w_o[0].astype(BF16)], by_columns=(1,))
    g_w1[1], g_w2[1] = gather_shards("gather_mlp1", mlp_w(1), by_columns=(0,))
    g_pw1, g_pw2, g_w1[2], g_w2[2] = gather_shards(
        "gather_layer2", [cf_w_pw1[0].astype(BF16), cf_w_pw2[0].astype(BF16)] + mlp_w(2), by_columns=(0, 2))
    g_in[1], g_out[1], g_w1[3], g_w2[3] = gather_shards(
        "gather_layer3", [sc_w_in[1].astype(BF16), sc_w_out[1].astype(BF16)] + mlp_w(3), by_columns=(0, 2))

    wd_t = Q_LORA + KV_LORA + 128
    w_in = [Stk("full", d, 3 * d, g_in[j]) for j in range(2)]
    w_out = [Stk("row", d, d, g_out[j]) for j in range(2)]
    w_dqkv = Stk("row", d, wd_t, g_dqkv)
    w_uq = Stk("full", Q_LORA, N_HEADS * HEAD_PAD, g_uq)
    w_uk = Stk("row", KV_LORA, N_HEADS * QK_NOPE, g_uk)
    w_uv = Stk("row", KV_LORA, N_HEADS * V_HEAD, g_uv)
    w_o = Stk("row", d, d, g_o)
    w_pw1 = Stk("full", d, 2 * d, g_pw1)
    w_pw2 = Stk("row", d, d, g_pw2)
    w_1 = [Stk("full", d, d_ff, g_w1[i]) for i in range(DEPTH)]
    w_2 = [Stk("row", d_ff, d, g_w2[i]) for i in range(DEPTH)]

    def wide(rows):
        return jnp.swapaxes(rows, 0, 1).reshape(rows.shape[1], d)

    conv_w = wide(g_small[:, 0:6]).reshape(2, SC_WIDTH, d)
    b_pw1 = g_small[:, 6:8].reshape(1, 2 * d)
    dw_w = wide(g_small[:, 8:39])
    dw_b, norm_g, norm_b, b_pw2 = (wide(g_small[:, 39 + k:40 + k]) for k in range(4))

    pos = jnp.arange(s, dtype=F32)
    inv_freq = ROPE_THETA ** (-jnp.arange(0, QK_ROPE, 2, dtype=F32) / QK_ROPE)
    ang = pos[:, None] * inv_freq[None, :]
    cos, sin, zero = jnp.cos(ang), jnp.sin(ang), jnp.zeros((s, 128 - QK_ROPE), F32)
    cf = jnp.concatenate([cos, cos, zero], axis=1)
    sf = jnp.concatenate([-sin, sin, zero], axis=1)

    def row(a, i):
        return a[i:i + 1]

    xs = x.reshape(s, d)
    cur = (xs, xs.astype(BF16))
    tape = []
    for i in range(DEPTH):
        mixer, j = i % 3, i // 3
        xf, xb = cur
        lg, lb = row(ln_mix_g, i), row(ln_mix_b, i)
        if mixer == 0:
            u = mm_plain_nn(f"sc{j}_in", xb, w_in[j], F32, tn=3 * dq4)
            gb = short_conv_gate(u, conv_w[j])
            y, yb, xh, rstd = mm_residual_ln(f"sc{j}_out_ln", gb, w_out[j], xf, lg, lb)
            sv = dict(xb=xb, u=u, gb=gb)
        elif mixer == 1:
            t = mm_plain_nn("mla_down", xb, w_dqkv, F32, tn=wd_t // 2)
            cq, ckv, kpe = mla_latents(t, mla_g_q, mla_g_kv, cf, sf)
            qh = mla_queries(cq, w_uq, cf, sf)
            kh = mla_keys(ckv, w_uk, kpe)
            vh = mm_plain_nn("mla_values", ckv, w_uv, BF16, tk=KV_LORA)
            oh = attention(qh, kh, vh)
            y, yb, xh, rstd = mm_residual_ln("mla_out_ln", oh, w_o, xf, lg, lb)
            sv = dict(xb=xb, t=t, cq=cq, ckv=ckv, qh=qh, kh=kh, vh=vh, oh=oh)
        else:
            u = mm_plain_nn("cf_pw1", xb, w_pw1, F32, bias=b_pw1)
            hc = conformer_glu_conv(u, dw_w, dw_b)
            sb = conformer_norm_swish(hc, norm_g, norm_b)
            y, yb, xh, rstd = mm_residual_ln("cf_pw2_ln", sb, w_pw2, xf, lg, lb, bias=b_pw2)
            sv = dict(xb=xb, u=u, hc=hc, sb=sb)
        sv.update(xh=xh, rstd=rstd, g=lg)
        cur, sv_mlp = _mlp_forward(i, y, yb, w_1[i], w_2[i], row(ln_ff_g, i), row(ln_ff_b, i))
        tape.append((sv, sv_mlp))

    g_ln = {n: [None] * DEPTH for n in ("ln_mix_g", "ln_mix_b", "ln_ff_g", "ln_ff_b")}
    last = tape[DEPTH - 1][1]
    dr, drb, g_ln["ln_ff_g"][DEPTH - 1], g_ln["ln_ff_b"][DEPTH - 1], _, loss_part = loss_ln_backward(
        cur[0], loss_target.reshape(s, d), last["xh"], last["rstd"], last["g"])

    grads = {}
    smalls = {}
    conv_grads = [None, None]
    core = lax.axis_index("c").astype(jnp.int32).reshape(1)
    chip = xq.astype(jnp.int32).reshape(1)
    pairs, landed = {}, {}
    ready, theirs = [], {}

    def hold(xs, others):
        live = [x for x in xs if x is not None]
        out = lax.optimization_barrier((*live, *others))
        rest = iter(out[:len(live)])
        return tuple(None if x is None else next(rest) for x in xs), list(out[len(live):])

    def reduce_after(x, new, early=False):
        out = lax.optimization_barrier((x, *new.values()))
        grads.update(zip(new, out[1:]))
        if early:
            theirs.update(zip(new, pair_exchange(f"pair_exchange_{len(theirs)}", list(out[1:]), True)))
        ready.extend(new)
        return out[0]

    def reduce_layer(i, x):
        late = [n for n in ready if n not in theirs]
        if late:
            theirs.update(zip(late, pair_exchange(f"pair_exchange_layer{i}", [grads[n] for n in late], False)))
        by_shape = {}
        for n in ready:
            by_shape.setdefault(grads[n].shape, []).append(n)
        for names in by_shape.values():
            pairs.update(zip(names, pair_sum([grads[n] for n in names], [theirs[n] for n in names], core)))
        sums = [pairs[n] for n in ready]
        landed.update(zip(ready, chip_exchange(f"chip_exchange_layer{i}", sums)))
        exchanged.append(list(ready))
        ready.clear()
        return hold(x, sums)[0]

    groups = [["in_0", "in_1"], ["out_0", "out_1"], ["dqkv"], ["uq"], ["uk"], ["uv"], ["o"], ["pw1"], ["pw2"],
              [f"w1_{i}" for i in range(DEPTH)], [f"w2_{i}" for i in range(DEPTH)]]
    stacks = [None] * len(groups)
    exchanged = []

    def sum_layer(x, last=False):
        names = exchanged.pop(0)
        if last:
            x, held = hold(x, [landed[n] for n in names])
            landed.update(zip(names, held))
        new = []
        for n in names:
            k = next(k for k, members in enumerate(groups) if n in members)
            stacks[k] = chip_sum(pairs[n], landed[n], chip, stacks[k], groups[k].index(n), len(groups[k]))
            new.append(stacks[k])
        return x if last else hold(x, new)[0]

    for i in reversed(range(DEPTH)):
        mixer, j = i % 3, i // 3
        sv, sv_mlp = tape[i]
        dr, drb, g_ln["ln_mix_g"][i], g_ln["ln_mix_b"][i], dr_sum = _mlp_backward(
            i, dr, drb, sv_mlp, w_1[i], w_2[i], Stk("col", d, d_ff), Stk("row", d_ff, d),
            lambda x_, new: reduce_after(x_, new, early=i > 0), (sv["xh"], sv["rstd"], sv["g"]))
        if i == 0:
            dr, drb = reduce_layer("0_mlp", (dr, drb))

        def to_input(name, a, w, tk, in_parts=False):
            side_by_side = {}
            if in_parts:
                nparts = a.shape[0]
                side_by_side = dict(
                    a_spec_fn=lambda tm, tk_: pl.BlockSpec((nparts, tm, d), lambda i_, j_, k_: (0, i_, 0)),
                    a_fn=lambda blk: jnp.concatenate([blk[p] for p in range(nparts)], axis=1))
                tk = w.n
            if i == 0:
                if side_by_side:
                    side_by_side["a_spec_fn"] = (s, side_by_side["a_spec_fn"])
                return mm_plain_nt(name, a, w, F32, tm=512, tn=1024, tk=tk, add=dr, add_scale=ALPHA, **side_by_side), None
            prev = tape[i - 1][1]
            out = mm_nt_ln_backward(name, a, w, dr, prev["xh"], prev["rstd"], prev["g"], tk=tk, **side_by_side)
            g_ln["ln_ff_g"][i - 1], g_ln["ln_ff_b"][i - 1] = out[2], out[3]
            return out[0], out[1]

        if mixer == 0:
            dgate = mm_plain_nt(f"sc{j}_out_bwd", drb, w_out[j], F32)
            dw_out = mm_tn(f"sc{j}_dw_out", sv["gb"], drb, Stk("row", d, d), s, 512, 1024)
            du, conv_grads[j] = short_conv_gate_bwd(sv["u"], conv_w[j], dgate)
            nb = d // 256
            dw_in = mm_tn(
                f"sc{j}_dw_in", sv["xb"], du, Stk("col", d, 3 * d), s, 1024, 256,
                b_spec=pl.BlockSpec((None, s, 256), lambda i_, j_, k_: (j_ // nb, k_, j_ % nb)))
            du = reduce_after(du, {f"in_{j}": dw_in, f"out_{j}": dw_out})
            dr, drb = to_input(f"sc{j}_in_bwd", du, w_in[j], d, in_parts=True)
        elif mixer == 1:
            do = mm_plain_nt("mla_out_bwd", drb, w_o, BF16)
            g_o = mm_tn("mla_dw_o", sv["oh"], drb, Stk("row", d, d), s, 512, 1024)
            dqh, dkh, dvh = attention_bwd(sv["qh"], sv["kh"], sv["vh"], do)
            dql, dkn, dkpe = mla_unrope_grads(dqh, dkh, cf, sf)
            g_uq = mm_tn("mla_dw_uq", sv["cq"], dql, Stk("col", Q_LORA, N_HEADS * HEAD_PAD), s, Q_LORA, 512)
            dcq = mm_plain_nt("mla_uq_bwd", dql, w_uq, F32, tn=Q_LORA)
            g_uk = mm_tn("mla_dw_uk", sv["ckv"], dkn, Stk("row", KV_LORA, N_HEADS * QK_NOPE), s, KV_LORA, 1024)
            g_uv = mm_tn("mla_dw_uv", sv["ckv"], dvh, Stk("row", KV_LORA, N_HEADS * V_HEAD), s, KV_LORA, 1024)
            dckv = mm_plain_nt("mla_uk_bwd", dkn, w_uk, F32, tn=KV_LORA)
            dckv = mm_plain_nt("mla_uv_bwd", dvh, w_uv, F32, tn=KV_LORA, add=dckv)
            dt, smalls["g_q"], smalls["g_kv"] = mla_latents_bwd(sv["t"], mla_g_q, mla_g_kv, cf, sf, dcq, dckv, dkpe)
            g_dqkv = mm_tn("mla_dw_down", sv["xb"], dt, Stk("row", d, wd_t), s, 512, wd_t)
            dt = reduce_after(dt, {"dqkv": g_dqkv, "uq": g_uq, "uk": g_uk, "uv": g_uv, "o": g_o})
            dr, drb = to_input("mla_down_bwd", dt, w_dqkv, wd_t)
        else:
            dsw = mm_plain_nt("cf_pw2_bwd", drb, w_pw2, F32)
            g_pw2 = mm_tn("cf_dw_pw2", sv["sb"], drb, Stk("row", d, d), s, 512, 1024)
            smalls["b_pw2"] = dr_sum
            dhc, smalls["norm_g"], smalls["norm_b"] = conformer_norm_swish_bwd(sv["hc"], norm_g, norm_b, dsw)
            du, smalls["b_pw1"], smalls["dw_w"], smalls["dw_b"] = conformer_glu_conv_bwd(sv["u"], dw_w, dhc)
            nb = d // 512
            g_pw1 = mm_tn(
                "cf_dw_pw1", sv["xb"], du, Stk("col", d, 2 * d), s, 1024, 512,
                b_spec=pl.BlockSpec((None, s, 512), lambda i_, j_, k_: (j_ // nb, k_, j_ % nb)))
            du = reduce_after(du, {"pw1": g_pw1, "pw2": g_pw2})
            dr, drb = to_input("cf_pw1_bwd", du, w_pw1, d, in_parts=True)
        if i < DEPTH - 1:
            dr, drb = sum_layer((dr, drb))
        dr, drb = reduce_layer(i, (dr, drb))
    grad_x = sum_layer(sum_layer((dr, None), last=True), last=True)[0].reshape(1, s, d)

    mine = stacks
    other = (pair_share("pair_share_mixers", mine[:9]) + pair_share("pair_share_up", mine[9:10])
             + pair_share("pair_share_down", mine[10:]))

    def padded(get):
        dqkv = jnp.concatenate([get("mla_w_dq")[0], jnp.pad(get("mla_w_dkv")[0], ((0, 0), (0, 128 - QK_ROPE)))], axis=1)
        uq = jnp.pad(get("mla_w_uq")[0].reshape(Q_LORA, 2, QK_NOPE + QK_ROPE),
                     ((0, 0), (0, 0), (0, HEAD_PAD - QK_NOPE - QK_ROPE))).reshape(Q_LORA, 2 * HEAD_PAD)
        return [get("sc_w_in"), get("sc_w_out"), dqkv[None], uq[None],
                get("mla_w_uk").reshape(1, KV_LORA // N_CHIPS, d), get("mla_w_uv").reshape(1, KV_LORA // N_CHIPS, d),
                get("mla_w_o"), get("cf_w_pw1"), get("cf_w_pw2"), get("ff_w1"), get("ff_w2")]

    w_l, m_l, v_l = (padded(lambda n, p=p: given[p + n]) for p in ("", "m_", "v_"))
    res = [adamw_joined(w_l[k], m_l[k], v_l[k], mine[k], other[k], core) for k in range(len(groups))]

    def unpadded(k):
        r_in, r_out, r_dqkv, r_uq, r_uk, r_uv, r_o, r_pw1, r_pw2, r_w1, r_w2 = (r[k] for r in res)
        return {
            "sc_w_in": r_in, "sc_w_out": r_out, "mla_w_dq": r_dqkv[:, :, 0:Q_LORA],
            "mla_w_dkv": r_dqkv[:, :, Q_LORA:Q_LORA + KV_LORA + QK_ROPE],
            "mla_w_uq": r_uq.reshape(1, Q_LORA, 2, HEAD_PAD)[:, :, :, 0:QK_NOPE + QK_ROPE].reshape(mla_w_uq.shape),
            "mla_w_uk": r_uk.reshape(mla_w_uk.shape), "mla_w_uv": r_uv.reshape(mla_w_uv.shape),
            "mla_w_o": r_o, "cf_w_pw1": r_pw1, "cf_w_pw2": r_pw2, "ff_w1": r_w1, "ff_w2": r_w2}

    big_g, big_d, big_m, big_v = (unpadded(k) for k in range(4))

    pad_row = lambda a: jnp.pad(a, ((0, 0), (0, d - a.shape[1])))
    small_parts = ([g for n in ("ln_mix_g", "ln_mix_b", "ln_ff_g", "ln_ff_b") for g in g_ln[n]]
                   + [pad_row(smalls["g_q"]), pad_row(smalls["g_kv"]), conv_grads[0], conv_grads[1],
                      smalls["b_pw1"].reshape(2, d), smalls["dw_w"], smalls["dw_b"], smalls["norm_g"], smalls["norm_b"],
                      smalls["b_pw2"], loss_part])
    red = all_reduce_small(small_parts, 64)
    loss = red[61, 0]

    where = {
        "ln_mix_g": [((), 0, DEPTH, "all")], "ln_mix_b": [((), 4, DEPTH, "all")],
        "ln_ff_g": [((), 8, DEPTH, "all")], "ln_ff_b": [((), 12, DEPTH, "all")],
        "mla_g_q": [((), 16, 1, Q_LORA)], "mla_g_kv": [((), 17, 1, KV_LORA)],
        "sc_conv_w": [((0,), 18, SC_WIDTH, "chip"), ((1,), 21, SC_WIDTH, "chip")],
        "cf_b_pw1": [((), 24, 2, "chip")], "cf_dw_w": [((0,), 26, CONF_WIDTH, "chip")],
        "cf_dw_b": [((), 57, 1, "chip")], "cf_norm_g": [((), 58, 1, "chip")], "cf_norm_b": [((), 59, 1, "chip")],
        "cf_b_pw2": [((), 60, 1, "chip")]}
    vec = list(where)
    vec_res = vector_update(red, chip, [given[n] for n in vec], [given["m_" + n] for n in vec],
                            [given["v_" + n] for n in vec], [where[n] for n in vec])
    gw = dict(big_g)
    upd = {n: [big_d[n], big_m[n], big_v[n]] for n in big_g}
    for k, n in enumerate(vec):
        gw[n] = vec_res[0][k]
        upd[n] = [vec_res[1][k], vec_res[2][k], vec_res[3][k]]

    return (loss, grad_x, *[gw[n] for n in WEIGHTS], *[upd[n][0] for n in WEIGHTS],
            *[upd[n][1] for n in WEIGHTS], *[upd[n][2] for n in WEIGHTS])
```

```python
import jax
import jax.numpy as jnp
from jax import lax
from jax.experimental import pallas as pl
from jax.experimental.pallas import tpu as pltpu
from jax.experimental.pallas import tpu_sc as plsc

F32 = jnp.float32
BF16 = jnp.bfloat16
MESH = pl.DeviceIdType.MESH

DEPTH = 4
ALPHA = (2.0 * DEPTH) ** 0.25
LN_EPS = 1e-5
RMS_EPS = 1e-6
CHUNK_SHIFT = 6
N_HEADS = 8
QK_NOPE = 128
QK_ROPE = 64
V_HEAD = 128
HEAD_PAD = 256
Q_LORA = 384
KV_LORA = 256
ROPE_THETA = 10000.0
SC_WIDTH = 3
CONF_WIDTH = 31
CONV_PAD = 32
CONV_CHUNK = 64
N_CHIPS = 4
ATTN_SCALE = (QK_NOPE + QK_ROPE) ** -0.5

ADAM_LR = 0.001
ADAM_B1 = 0.9
ADAM_B2 = 0.999
ADAM_EPS = 1e-08
ADAM_WD = 0.01
ADAM_STEP = 10

VMEM_LIMIT = 56 * 2**20

NN = (((1,), (0,)), ((), ()))
NT = (((1,), (1,)), ((), ()))
TN = (((0,), (0,)), ((), ()))


def _params(sem=None):
    return pltpu.CompilerParams(dimension_semantics=sem, vmem_limit_bytes=VMEM_LIMIT)


class Stk:
    def __init__(self, kind, k, n, arr=None):
        self.kind, self.k, self.n = kind, k, n
        self.plain = kind != "col"
        self.nloc = n // N_CHIPS if kind == "col" else n
        self.arr = arr.reshape(k, n) if arr is not None and self.plain else arr

    @property
    def shape(self):
        return (self.k, self.n) if self.plain else (N_CHIPS, self.k, self.nloc)

    def spec(self, bk, bn, f, resident=False):
        if self.plain:
            return pl.BlockSpec((bk, bn), f, pipeline_mode=pl.Buffered(1)) if resident else pl.BlockSpec((bk, bn), f)
        assert self.k % bk == 0 and self.nloc % bn == 0, (self.k, bk, self.nloc, bn)
        pn = self.nloc // bn

        def imap(*g):
            kb, nb = f(*g)
            return nb // pn, kb, nb % pn

        return pl.BlockSpec((None, bk, bn), imap)


def _mm(name, mode, a, b, grid, a_spec, b_spec, acc_shape, extras, extra_specs, out_shapes, out_specs, epi, a_fn=None,
        rows_in_order=False):
    nk = grid[2]
    ne = len(extras)

    def body(*refs):
        a_ref, b_ref = refs[0], refs[1]
        e_refs = refs[2:2 + ne]
        av = a_ref[...] if a_fn is None else a_fn(a_ref[...])
        part = lax.dot_general(av, b_ref[...], mode, preferred_element_type=F32)
        if nk == 1:
            epi(part, e_refs, refs[2 + ne:])
            return
        o_refs = refs[2 + ne:-1]
        acc = refs[-1]
        k = pl.program_id(2)

        @pl.when(k == 0)
        def _():
            acc[...] = part

        @pl.when(k > 0)
        def _():
            acc[...] += part

        @pl.when(k == nk - 1)
        def _():
            epi(acc[...], e_refs, o_refs)

    return pl.pallas_call(
        body, grid=grid, in_specs=[a_spec, b_spec, *extra_specs], out_specs=out_specs, out_shape=out_shapes,
        scratch_shapes=[pltpu.VMEM(acc_shape, F32)] if nk > 1 else [],
        compiler_params=_params(("arbitrary",) * 3 if rows_in_order else ("parallel", "parallel", "arbitrary")),
        name=name)(a, b, *extras)


def _tile(n, t):
    t = min(n, t)
    while n % t:
        t -= 8
    assert t > 0, (n, t)
    return t


def mm_nn(name, a, w, tm, tn, tk, epi, out_shapes, out_specs, extras=(), extra_specs=(), a_spec=None, a_fn=None):
    m = a.shape[0]
    tm, tn, tk = _tile(m, tm), _tile(w.n, tn), _tile(w.k, tk)
    grid = (m // tm, w.n // tn, w.k // tk)
    a_spec = a_spec or pl.BlockSpec((tm, tk), lambda i, j, k: (i, k))
    b_spec = w.spec(tk, tn, lambda i, j, k: (k, j))
    return _mm(name, NN, a, w.arr, grid, a_spec, b_spec, (tm, tn), extras, extra_specs, out_shapes, out_specs, epi, a_fn)


def mm_nt(name, a, w, m, tm, tn, tk, epi, out_shapes, out_specs, extras=(), extra_specs=(), a_spec=None,
          rows_in_order=False, a_fn=None):
    tm, tn, tk = _tile(m, tm), _tile(w.k, tn), _tile(w.n, tk)
    grid = (m // tm, w.k // tn, w.n // tk)
    a_spec = a_spec or pl.BlockSpec((tm, tk), lambda i, j, k: (i, k))
    b_spec = w.spec(tn, tk, lambda i, j, k: (j, k), resident=grid[1] == 1 and grid[2] == 1)
    return _mm(name, NT, a, w.arr, grid, a_spec, b_spec, (tm, tn), extras, extra_specs, out_shapes, out_specs, epi,
               a_fn=a_fn, rows_in_order=rows_in_order)


def mm_tn(name, a, b, dw, s, tm=512, tn=512, tk=4096, a_spec=None, b_spec=None, a_fn=None):
    tm, tn, tk = _tile(dw.k, tm), _tile(dw.n, tn), _tile(s, tk)
    grid = (dw.k // tm, dw.n // tn, s // tk)
    a_spec = a_spec or pl.BlockSpec((tk, tm), lambda i, j, k: (k, i))
    b_spec = b_spec or pl.BlockSpec((tk, tn), lambda i, j, k: (k, j))

    def epi(acc, e, o):
        o[0][...] = acc.astype(BF16)

    out = _mm(name, TN, a, b, grid, a_spec, b_spec, (tm, tn), (), (), [jax.ShapeDtypeStruct(dw.shape, BF16)],
              [dw.spec(tm, tn, lambda i, j, k: (i, j))], epi, a_fn)[0]
    return out.reshape(N_CHIPS, dw.k // N_CHIPS, dw.n) if dw.plain else out


def _sds(shape, dtype):
    return jax.ShapeDtypeStruct(shape, dtype)


def _ij(tm, tn):
    return pl.BlockSpec((tm, tn), lambda i, j, k: (i, j))


def _i0(tm, c):
    return pl.BlockSpec((tm, c), lambda i, j, k: (i, 0))


def _0j(r, tn):
    return pl.BlockSpec((r, tn), lambda i, j, k: (0, j))


def _layer_norm_rows(r, g, b):
    mu = jnp.mean(r, axis=-1, keepdims=True)
    d = r - mu
    var = jnp.mean(d * d, axis=-1, keepdims=True)
    rstd = lax.rsqrt(var + LN_EPS)
    xh = d * rstd
    return xh * g + b, xh, rstd


def mm_residual_ln(name, a, w, x, g, b, bias=None, tm=512, tk=1024, a_fn=None):
    s, d = x.shape
    tm = _tile(s, tm)
    extras = [x, g, b] + ([bias] if bias is not None else [])
    especs = [_i0(tm, d), _0j(1, d), _0j(1, d)] + ([_0j(1, d)] if bias is not None else [])

    def epi(acc, e, o):
        r = ALPHA * e[0][...] + acc
        if bias is not None:
            r = r + e[3][...]
        y, xh, rstd = _layer_norm_rows(r, e[1][...], e[2][...])
        o[0][...] = y
        o[1][...] = y.astype(BF16)
        o[2][...] = xh
        o[3][...] = rstd

    return mm_nn(name, a, w, tm, d, tk, epi,
                 [_sds((s, d), F32), _sds((s, d), BF16), _sds((s, d), F32), _sds((s, 1), F32)],
                 [_i0(tm, d), _i0(tm, d), _i0(tm, d), _i0(tm, 1)], extras, especs, a_fn=a_fn)


def mm_plain_nn(name, a, w, out_dtype, tm=1024, tn=512, tk=1024, bias=None):
    m = a.shape[0]
    tm, tn = _tile(m, tm), _tile(w.n, tn)

    def epi(acc, e, o):
        if bias is not None:
            acc = acc + e[0][...]
        o[0][...] = acc.astype(out_dtype)

    extras, especs = ([bias], [_0j(1, tn)]) if bias is not None else ((), ())
    return mm_nn(name, a, w, tm, tn, tk, epi, [_sds((m, w.n), out_dtype)], [_ij(tm, tn)], extras, especs)[0]


def mm_plain_nt(name, a, w, out_dtype, tm=1024, tn=512, tk=1024, add=None, add_scale=1.0, a_spec_fn=None, a_fn=None):
    m = a.shape[0] if a_spec_fn is None else a_spec_fn[0]
    tm, tn = _tile(m, tm), _tile(w.k, tn)
    tk = _tile(w.n, tk)

    def epi(acc, e, o):
        if add is not None:
            acc = acc + add_scale * e[0][...].astype(F32)
        o[0][...] = acc.astype(out_dtype)

    extras, especs = ([add], [_ij(tm, tn)]) if add is not None else ((), ())
    a_spec = None if a_spec_fn is None else a_spec_fn[1](tm, tk)
    return mm_nt(name, a, w, m, tm, tn, tk, epi, [_sds((m, w.k), out_dtype)], [_ij(tm, tn)], extras, especs,
                 a_spec=a_spec, a_fn=a_fn)[0]


def _rows(tm, c):
    return pl.BlockSpec((tm, c), lambda i: (i, 0))


def _fix(shape):
    nd = len(shape)
    return pl.BlockSpec(shape, lambda i: (0,) * nd)


def _accumulate(ref, val):
    @pl.when(pl.program_id(0) == 0)
    def _():
        ref[...] = jnp.zeros_like(ref)

    ref[...] += val


def _ln_backward_rows(dyv, xh, rstd, g, dr_ref, drb_ref, dg_ref, db_ref, ds_ref):
    dxh = dyv * g
    m1 = jnp.mean(dxh, axis=-1, keepdims=True)
    m2 = jnp.mean(dxh * xh, axis=-1, keepdims=True)
    dr = rstd * (dxh - m1 - xh * m2)
    dr_ref[...] = dr
    drb_ref[...] = dr.astype(BF16)
    _accumulate(dg_ref, jnp.sum(dyv * xh, axis=0, keepdims=True))
    _accumulate(db_ref, jnp.sum(dyv, axis=0, keepdims=True))
    _accumulate(ds_ref, jnp.sum(dr, axis=0, keepdims=True))


def mm_nt_ln_backward(name, a, w, add, xhat, rstd, g, tm=512, tk=1024, a_spec_fn=None, a_fn=None):
    m, d = add.shape
    tm, tk = _tile(m, tm), _tile(w.n, tk)

    def epi(acc, e, o):
        _ln_backward_rows(acc + ALPHA * e[0][...], e[1][...], e[2][...], e[3][...], *o)

    vec = pl.BlockSpec((1, d), lambda i, j, k: (0, 0))
    a_spec = None if a_spec_fn is None else a_spec_fn(tm, tk)
    return mm_nt(name, a, w, m, tm, d, tk, epi,
                 [_sds((m, d), F32), _sds((m, d), BF16), _sds((1, d), F32), _sds((1, d), F32), _sds((1, d), F32)],
                 [_i0(tm, d), _i0(tm, d), vec, vec, vec], [add, xhat, rstd, g],
                 [_i0(tm, d), _i0(tm, d), _i0(tm, 1), vec], a_spec=a_spec, rows_in_order=True, a_fn=a_fn)


def loss_ln_backward(y, target, xhat, rstd, g, tm=512):
    s, d = y.shape
    tm = _tile(s, tm)

    def body(y_ref, t_ref, xh_ref, rstd_ref, g_ref, dr_ref, drb_ref, dg_ref, db_ref, ds_ref, loss_ref):
        e = y_ref[...] - t_ref[...]
        part = 0.5 * jnp.sum(jnp.mean(e * e, axis=-1, keepdims=True), axis=0, keepdims=True)
        _accumulate(loss_ref, jnp.broadcast_to(part, (1, d)))
        _ln_backward_rows(e * (1.0 / d), xh_ref[...], rstd_ref[...], g_ref[...], dr_ref, drb_ref, dg_ref, db_ref, ds_ref)

    return pl.pallas_call(
        body, grid=(s // tm,),
        in_specs=[_rows(tm, d), _rows(tm, d), _rows(tm, d), _rows(tm, 1), _fix((1, d))],
        out_specs=[_rows(tm, d), _rows(tm, d)] + [_fix((1, d))] * 4,
        out_shape=[_sds((s, d), F32), _sds((s, d), BF16)] + [_sds((1, d), F32)] * 4,
        compiler_params=_params(("arbitrary",)), name="loss_ln_backward")(y, target, xhat, rstd, g)


def _cols(s, tc, off=0):
    return pl.BlockSpec((s, tc), lambda i: (0, i + off))


def _shift_down(z, sft, rows):
    return jnp.where(rows >= sft, pltpu.roll(z, sft, 0), 0.0)


def _shift_up(z, sft, rows, s):
    return jnp.where(rows < s - sft, pltpu.roll(z, (s - sft) % s, 0), 0.0)


def short_conv_gate(u, conv_w, tc=256):
    s, d3 = u.shape
    d = d3 // 3
    nb = d // tc

    def body(b_ref, c_ref, h_ref, w_ref, o_ref):
        rows = lax.broadcasted_iota(jnp.int32, (s, tc), 0)
        z = c_ref[...] * h_ref[...]
        cz = jnp.zeros((s, tc), F32)
        for k in range(SC_WIDTH):
            sft = SC_WIDTH - 1 - k
            cz = cz + w_ref[pl.ds(k, 1), :] * (_shift_down(z, sft, rows) if sft else z)
        o_ref[...] = (b_ref[...] * cz).astype(BF16)

    return pl.pallas_call(
        body, grid=(nb,),
        in_specs=[_cols(s, tc), _cols(s, tc, nb), _cols(s, tc, 2 * nb), _cols(SC_WIDTH, tc)],
        out_specs=_cols(s, tc), out_shape=_sds((s, d), BF16),
        compiler_params=_params(("parallel",)), name="short_conv_gate")(u, u, u, conv_w)


def short_conv_gate_bwd(u, conv_w, dg, tc=256):
    s, d3 = u.shape
    d = d3 // 3
    nb = d // tc

    def body(b_ref, c_ref, h_ref, w_ref, dg_ref, du_ref, dw_ref):
        rows = lax.broadcasted_iota(jnp.int32, (s, tc), 0)
        c, h, dgv = c_ref[...], h_ref[...], dg_ref[...]
        z = c * h
        dcz = dgv * b_ref[...]
        cz = jnp.zeros((s, tc), F32)
        dz = jnp.zeros((s, tc), F32)
        for k in range(SC_WIDTH):
            sft = SC_WIDTH - 1 - k
            zs = _shift_down(z, sft, rows) if sft else z
            wk = w_ref[pl.ds(k, 1), :]
            cz = cz + wk * zs
            dz = dz + wk * (_shift_up(dcz, sft, rows, s) if sft else dcz)
            dw_ref[pl.ds(k, 1), :] = jnp.sum(dcz * zs, axis=0, keepdims=True)
        du_ref[0] = (dgv * cz).astype(BF16)
        du_ref[1] = (dz * h).astype(BF16)
        du_ref[2] = (dz * c).astype(BF16)

    return pl.pallas_call(
        body, grid=(nb,),
        in_specs=[_cols(s, tc), _cols(s, tc, nb), _cols(s, tc, 2 * nb), _cols(SC_WIDTH, tc), _cols(s, tc)],
        out_specs=[pl.BlockSpec((3, s, tc), lambda i: (0, 0, i)), _cols(SC_WIDTH, tc)],
        out_shape=[_sds((3, s, d), BF16), _sds((SC_WIDTH, d), F32)],
        compiler_params=_params(("parallel",)), name="short_conv_gate_bwd")(u, u, u, conv_w, dg)


def _store_shifted_down(ref, z, rows):
    s, tc = z.shape
    for b in range(8):
        ref[b, pl.ds(0, CONV_PAD), :] = jnp.zeros((CONV_PAD, tc), F32)
        ref[b, pl.ds(CONV_PAD, s), :] = z if b == 0 else _shift_down(z, b, rows)


def _store_shifted_up(ref, z, rows):
    s, tc = z.shape
    for b in range(8):
        ref[b, pl.ds(0, s), :] = z if b == 0 else _shift_up(z, b, rows, s)
        ref[b, pl.ds(s, CONV_PAD), :] = jnp.zeros((CONV_PAD, tc), F32)


def conformer_glu_conv(u, dw_w, dw_b, tc=128):
    s, d2 = u.shape
    d = d2 // 2
    nb = d // tc

    ch = min(CONV_CHUNK, s)

    def body(a_ref, g_ref, w_ref, b_ref, o_ref, down):
        rows = lax.broadcasted_iota(jnp.int32, (s, tc), 0)
        _store_shifted_down(down, a_ref[...] * jax.nn.sigmoid(g_ref[...]), rows)

        def chunk(ci, carry):
            r0 = pl.multiple_of(ci * ch, ch)
            acc = jnp.broadcast_to(b_ref[...], (ch, tc))
            for k in range(CONF_WIDTH):
                sft = CONF_WIDTH - 1 - k
                acc = acc + w_ref[pl.ds(k, 1), :] * down[sft % 8, pl.ds(CONV_PAD + r0 - (sft // 8) * 8, ch), :]
            o_ref[pl.ds(r0, ch), :] = acc
            return carry

        lax.fori_loop(0, s // ch, chunk, 0)

    return pl.pallas_call(
        body, grid=(nb,),
        in_specs=[_cols(s, tc), _cols(s, tc, nb), _cols(CONF_WIDTH, tc), _cols(1, tc)],
        out_specs=_cols(s, tc), out_shape=_sds((s, d), F32),
        scratch_shapes=[pltpu.VMEM((8, CONV_PAD + s, tc), F32)],
        compiler_params=_params(("parallel",)), name="conformer_glu_conv")(u, u, dw_w, dw_b)


def conformer_glu_conv_bwd(u, dw_w, dhc, tc=128):
    s, d2 = u.shape
    d = d2 // 2
    nb = d // tc
    ch = min(CONV_CHUNK, s)

    def body(a_ref, g_ref, w_ref, dhc_ref, du_ref, dbias_ref, dw_ref, db_ref, down, up, dw_acc, dh_buf):
        rows = lax.broadcasted_iota(jnp.int32, (s, tc), 0)
        a = a_ref[...]
        sg = jax.nn.sigmoid(g_ref[...])
        dhcv = dhc_ref[...]
        _store_shifted_down(down, a * sg, rows)
        _store_shifted_up(up, dhcv, rows)
        dw_acc[...] = jnp.zeros_like(dw_acc)

        def chunk(ci, carry):
            r0 = pl.multiple_of(ci * ch, ch)
            dc = dhc_ref[pl.ds(r0, ch), :]
            dh = jnp.zeros((ch, tc), F32)
            for k in range(CONF_WIDTH):
                sft = CONF_WIDTH - 1 - k
                a8, b = (sft // 8) * 8, sft % 8
                dh = dh + w_ref[pl.ds(k, 1), :] * up[b, pl.ds(r0 + a8, ch), :]
                prod = dc * down[b, pl.ds(CONV_PAD + r0 - a8, ch), :]
                dw_acc[k] += jnp.sum(prod.reshape(ch // 8, 8, tc), axis=0)
            dh_buf[pl.ds(r0, ch), :] = dh
            return carry

        lax.fori_loop(0, s // ch, chunk, 0)
        dh = dh_buf[...]
        da = dh * sg
        dgate = dh * a * sg * (1.0 - sg)
        du_ref[0] = da.astype(BF16)
        du_ref[1] = dgate.astype(BF16)
        dbias_ref[pl.ds(0, 1), :] = jnp.sum(da, axis=0, keepdims=True)
        dbias_ref[pl.ds(1, 1), :] = jnp.sum(dgate, axis=0, keepdims=True)
        db_ref[...] = jnp.sum(dhcv, axis=0, keepdims=True)
        for k in range(CONF_WIDTH):
            dw_ref[pl.ds(k, 1), :] = jnp.sum(dw_acc[k], axis=0, keepdims=True)

    return pl.pallas_call(
        body, grid=(nb,),
        in_specs=[_cols(s, tc), _cols(s, tc, nb), _cols(CONF_WIDTH, tc), _cols(s, tc)],
        out_specs=[pl.BlockSpec((2, s, tc), lambda i: (0, 0, i)), _cols(2, tc), _cols(CONF_WIDTH, tc), _cols(1, tc)],
        out_shape=[_sds((2, s, d), BF16), _sds((2, d), F32), _sds((CONF_WIDTH, d), F32), _sds((1, d), F32)],
        scratch_shapes=[pltpu.VMEM((8, CONV_PAD + s, tc), F32), pltpu.VMEM((8, CONV_PAD + s, tc), F32),
                        pltpu.VMEM((CONF_WIDTH + 1, 8, tc), F32), pltpu.VMEM((s, tc), F32)],
        compiler_params=_params(("parallel",)), name="conformer_glu_conv_bwd")(u, u, dw_w, dhc)


def conformer_norm_swish(hc, g, b, tm=512):
    s, d = hc.shape
    tm = _tile(s, tm)

    def body(h_ref, g_ref, b_ref, o_ref):
        n, _, _ = _layer_norm_rows(h_ref[...], g_ref[...], b_ref[...])
        o_ref[...] = (n * jax.nn.sigmoid(n)).astype(BF16)

    return pl.pallas_call(
        body, grid=(s // tm,), in_specs=[_rows(tm, d), _fix((1, d)), _fix((1, d))], out_specs=_rows(tm, d),
        out_shape=_sds((s, d), BF16), compiler_params=_params(("parallel",)), name="conformer_norm_swish")(hc, g, b)


def conformer_norm_swish_bwd(hc, g, b, ds, tm=512):
    s, d = hc.shape
    tm = _tile(s, tm)

    def body(h_ref, g_ref, b_ref, ds_ref, dh_ref, dg_ref, db_ref):
        n, nh, rstd = _layer_norm_rows(h_ref[...], g_ref[...], b_ref[...])
        sg = jax.nn.sigmoid(n)
        dn = ds_ref[...] * (sg * (1.0 + n * (1.0 - sg)))
        dnh = dn * g_ref[...]
        m1 = jnp.mean(dnh, axis=-1, keepdims=True)
        m2 = jnp.mean(dnh * nh, axis=-1, keepdims=True)
        dh_ref[...] = rstd * (dnh - m1 - nh * m2)
        _accumulate(dg_ref, jnp.sum(dn * nh, axis=0, keepdims=True))
        _accumulate(db_ref, jnp.sum(dn, axis=0, keepdims=True))

    return pl.pallas_call(
        body, grid=(s // tm,), in_specs=[_rows(tm, d), _fix((1, d)), _fix((1, d)), _rows(tm, d)],
        out_specs=[_rows(tm, d), _fix((1, d)), _fix((1, d))],
        out_shape=[_sds((s, d), F32), _sds((1, d), F32), _sds((1, d), F32)],
        compiler_params=_params(("arbitrary",)), name="conformer_norm_swish_bwd")(hc, g, b, ds)


def _swap_halves(x):
    lane = lax.broadcasted_iota(jnp.int32, x.shape, 1)
    return jnp.where(lane < QK_ROPE // 2, pltpu.roll(x, 128 - QK_ROPE // 2, 1), pltpu.roll(x, QK_ROPE // 2, 1))


def _rope(x, cf, sf):
    return x * cf + _swap_halves(x) * sf


def _unrope(dx, cf, sf):
    return dx * cf - _swap_halves(dx) * sf


def _rms_rows(x, g):
    r = lax.rsqrt(jnp.mean(x * x, axis=-1, keepdims=True) + RMS_EPS)
    return x * r, r


def mla_latents(t, g_q, g_kv, cf, sf, tm=512):
    s = t.shape[0]
    tm = _tile(s, tm)

    def body(t_ref, gq_ref, gkv_ref, cf_ref, sf_ref, cq_ref, ckv_ref, kpe_ref):
        xq, _ = _rms_rows(t_ref[:, 0:Q_LORA], gq_ref[...])
        cq_ref[...] = (xq * gq_ref[...]).astype(BF16)
        xkv, _ = _rms_rows(t_ref[:, Q_LORA:Q_LORA + KV_LORA], gkv_ref[...])
        ckv_ref[...] = (xkv * gkv_ref[...]).astype(BF16)
        kpe_ref[...] = _rope(t_ref[:, Q_LORA + KV_LORA:], cf_ref[...], sf_ref[...]).astype(BF16)

    w = Q_LORA + KV_LORA + 128
    return pl.pallas_call(
        body, grid=(s // tm,),
        in_specs=[_rows(tm, w), _fix((1, Q_LORA)), _fix((1, KV_LORA)), _rows(tm, 128), _rows(tm, 128)],
        out_specs=[_rows(tm, Q_LORA), _rows(tm, KV_LORA), _rows(tm, 128)],
        out_shape=[_sds((s, Q_LORA), BF16), _sds((s, KV_LORA), BF16), _sds((s, 128), BF16)],
        compiler_params=_params(("parallel",)), name="mla_latents")(t, g_q, g_kv, cf, sf)


def mla_latents_bwd(t, g_q, g_kv, cf, sf, dcq, dckv, dkpe, tm=512):
    s = t.shape[0]
    tm = _tile(s, tm)
    w = Q_LORA + KV_LORA + 128

    def rms_bwd(x, g, dy):
        xh, r = _rms_rows(x, g)
        dxh = dy * g
        return r * (dxh - xh * jnp.mean(dxh * xh, axis=-1, keepdims=True)), jnp.sum(dy * xh, axis=0, keepdims=True)

    def body(t_ref, gq_ref, gkv_ref, cf_ref, sf_ref, dcq_ref, dckv_ref, dkpe_ref, dt_ref, dgq_ref, dgkv_ref):
        dxq, dgq = rms_bwd(t_ref[:, 0:Q_LORA], gq_ref[...], dcq_ref[...])
        dxkv, dgkv = rms_bwd(t_ref[:, Q_LORA:Q_LORA + KV_LORA], gkv_ref[...], dckv_ref[...])
        dt_ref[:, 0:Q_LORA] = dxq.astype(BF16)
        dt_ref[:, Q_LORA:Q_LORA + KV_LORA] = dxkv.astype(BF16)
        dt_ref[:, Q_LORA + KV_LORA:] = _unrope(dkpe_ref[...], cf_ref[...], sf_ref[...]).astype(BF16)
        _accumulate(dgq_ref, dgq)
        _accumulate(dgkv_ref, dgkv)

    return pl.pallas_call(
        body, grid=(s // tm,),
        in_specs=[_rows(tm, w), _fix((1, Q_LORA)), _fix((1, KV_LORA)), _rows(tm, 128), _rows(tm, 128),
                  _rows(tm, Q_LORA), _rows(tm, KV_LORA), _rows(tm, 128)],
        out_specs=[_rows(tm, w), _fix((1, Q_LORA)), _fix((1, KV_LORA))],
        out_shape=[_sds((s, w), BF16), _sds((1, Q_LORA), F32), _sds((1, KV_LORA), F32)],
        compiler_params=_params(("arbitrary",)), name="mla_latents_bwd")(t, g_q, g_kv, cf, sf, dcq, dckv, dkpe)


def mla_queries(cq, w_uq, cf, sf, tm=2048):
    s = cq.shape[0]
    tm = _tile(s, tm)

    def epi(acc, e, o):
        o[0][:, 0:QK_NOPE] = acc[:, 0:QK_NOPE].astype(BF16)
        o[0][:, QK_NOPE:] = _rope(acc[:, QK_NOPE:], e[0][...], e[1][...]).astype(BF16)

    return mm_nn("mla_queries", cq, w_uq, tm, HEAD_PAD, Q_LORA, epi, [_sds((s, N_HEADS * HEAD_PAD), BF16)],
                 [_ij(tm, HEAD_PAD)], [cf, sf], [_i0(tm, 128), _i0(tm, 128)])[0]


def mla_keys(ckv, w_uk, kpe, tm=2048):
    s = ckv.shape[0]
    tm = _tile(s, tm)

    def epi(acc, e, o):
        o[0][:, 0:QK_NOPE] = acc.astype(BF16)
        o[0][:, QK_NOPE:] = e[0][...]

    return mm_nn("mla_keys", ckv, w_uk, tm, QK_NOPE, KV_LORA, epi, [_sds((s, N_HEADS * HEAD_PAD), BF16)],
                 [_ij(tm, HEAD_PAD)], [kpe], [_i0(tm, 128)])[0]


def _masked_scores(q, k, tq, kv):
    sc = lax.dot_general(q, k, NT, preferred_element_type=F32) * ATTN_SCALE
    row = lax.broadcasted_iota(jnp.int32, (tq, tq), 0)
    col = lax.broadcasted_iota(jnp.int32, (tq, tq), 1)
    ok = lax.shift_right_logical(col, CHUNK_SHIFT) <= lax.shift_right_logical(row, CHUNK_SHIFT)
    own = jnp.where(ok, sc[:, kv - tq:], -1e30)
    return own if kv == tq else jnp.concatenate([sc[:, :kv - tq], own], axis=1)


def attention(q, k, v, tq=512):
    s = q.shape[0]
    tq = _tile(s, tq)
    nq = s // tq

    def body(q_ref, k_ref, v_ref, o_ref):
        for qi in range(nq):
            kv = (qi + 1) * tq
            sc = _masked_scores(q_ref[pl.ds(qi * tq, tq), :], k_ref[pl.ds(0, kv), :], tq, kv)
            p = jnp.exp(sc - jnp.max(sc, axis=-1, keepdims=True))
            o = lax.dot_general(p.astype(BF16), v_ref[pl.ds(0, kv), :], NN, preferred_element_type=F32)
            o_ref[pl.ds(qi * tq, tq), :] = (o / jnp.sum(p, axis=-1, keepdims=True)).astype(BF16)

    hq = pl.BlockSpec((s, HEAD_PAD), lambda h: (0, h))
    hv = pl.BlockSpec((s, V_HEAD), lambda h: (0, h))
    return pl.pallas_call(
        body, grid=(N_HEADS,), in_specs=[hq, hq, hv], out_specs=hv, out_shape=_sds((s, N_HEADS * V_HEAD), BF16),
        compiler_params=_params(("parallel",)), name="attention")(q, k, v)


def attention_bwd(q, k, v, do, tq=512):
    s = q.shape[0]
    tq = _tile(s, tq)
    nq = s // tq

    def body(q_ref, k_ref, v_ref, do_ref, dq_ref, dk_ref, dv_ref, dk_acc, dv_acc):
        dk_acc[...] = jnp.zeros_like(dk_acc)
        dv_acc[...] = jnp.zeros_like(dv_acc)
        for qi in range(nq):
            kv = (qi + 1) * tq
            qt = q_ref[pl.ds(qi * tq, tq), :]
            kt = k_ref[pl.ds(0, kv), :]
            dot = do_ref[pl.ds(qi * tq, tq), :]
            sc = _masked_scores(qt, kt, tq, kv)
            p = jnp.exp(sc - jnp.max(sc, axis=-1, keepdims=True))
            p = p / jnp.sum(p, axis=-1, keepdims=True)
            dp = lax.dot_general(dot, v_ref[pl.ds(0, kv), :], NT, preferred_element_type=F32)
            delta = jnp.sum(p * dp, axis=-1, keepdims=True)
            ds = (p * (dp - delta) * ATTN_SCALE).astype(BF16)
            dq_ref[pl.ds(qi * tq, tq), :] = lax.dot_general(ds, kt, NN, preferred_element_type=F32).astype(BF16)
            dk_acc[pl.ds(0, kv), :] += lax.dot_general(ds, qt, TN, preferred_element_type=F32)
            dv_acc[pl.ds(0, kv), :] += lax.dot_general(p.astype(BF16), dot, TN, preferred_element_type=F32)
        dk_ref[...] = dk_acc[...].astype(BF16)
        dv_ref[...] = dv_acc[...].astype(BF16)

    hq = pl.BlockSpec((s, HEAD_PAD), lambda h: (0, h))
    hv = pl.BlockSpec((s, V_HEAD), lambda h: (0, h))
    return pl.pallas_call(
        body, grid=(N_HEADS,), in_specs=[hq, hq, hv, hv], out_specs=[hq, hq, hv],
        out_shape=[_sds((s, N_HEADS * HEAD_PAD), BF16), _sds((s, N_HEADS * HEAD_PAD), BF16),
                   _sds((s, N_HEADS * V_HEAD), BF16)],
        scratch_shapes=[pltpu.VMEM((s, HEAD_PAD), F32), pltpu.VMEM((s, V_HEAD), F32)],
        compiler_params=_params(("parallel",)), name="attention_bwd")(q, k, v, do)


def mla_unrope_grads(dq, dk, cf, sf, tm=512):
    s = dq.shape[0]
    tm = _tile(s, tm)

    def body(dq_ref, dk_ref, cf_ref, sf_ref, dql_ref, dkn_ref, dkpe_ref):
        cfv, sfv = cf_ref[...], sf_ref[...]
        dkpe = jnp.zeros((tm, 128), F32)
        for h in range(N_HEADS):
            lo = h * HEAD_PAD
            dql_ref[:, lo:lo + QK_NOPE] = dq_ref[:, lo:lo + QK_NOPE]
            dql_ref[:, lo + QK_NOPE:lo + HEAD_PAD] = _unrope(
                dq_ref[:, lo + QK_NOPE:lo + HEAD_PAD].astype(F32), cfv, sfv).astype(BF16)
            dkn_ref[:, h * QK_NOPE:(h + 1) * QK_NOPE] = dk_ref[:, lo:lo + QK_NOPE]
            dkpe = dkpe + dk_ref[:, lo + QK_NOPE:lo + HEAD_PAD].astype(F32)
        dkpe_ref[...] = dkpe

    wq = N_HEADS * HEAD_PAD
    return pl.pallas_call(
        body, grid=(s // tm,), in_specs=[_rows(tm, wq), _rows(tm, wq), _rows(tm, 128), _rows(tm, 128)],
        out_specs=[_rows(tm, wq), _rows(tm, N_HEADS * QK_NOPE), _rows(tm, 128)],
        out_shape=[_sds((s, wq), BF16), _sds((s, N_HEADS * QK_NOPE), BF16), _sds((s, 128), F32)],
        compiler_params=_params(("parallel",)), name="mla_unrope_grads")(dq, dk, cf, sf)


ANY = pl.BlockSpec(memory_space=pl.ANY)
GATHER_ID = 1
CHIP_EXCHANGE_ID = 2
PAIR_ID = 3
ALL_ID = 4


def _nbytes(a):
    return a.size * a.dtype.itemsize


def _copy_cost(operand_bytes, sent_fraction):
    sent = int(operand_bytes * sent_fraction)
    return pl.CostEstimate(flops=0, transcendentals=0, bytes_accessed=2 * sent, remote_bytes_transferred=sent)


def _handshake(peers):
    barrier = pltpu.get_barrier_semaphore()
    for peer in peers:
        pl.semaphore_signal(barrier, inc=1, device_id=peer, device_id_type=MESH)
    pl.semaphore_wait(barrier, len(peers))


def _place():
    x, y, c = lax.axis_index("x"), lax.axis_index("y"), lax.axis_index("c")
    chips = [(1 - x, y), (x, 1 - y), (1 - x, 1 - y)]
    return x, y, c, chips


def _half(ref, hc, axis=0):
    n = ref.shape[axis] // 2
    idx = (slice(None),) * axis + (pl.ds(hc * n, n),)
    return ref.at[idx]


def gather_shards(name, tensors, by_columns=()):
    nt = len(tensors)

    def body(*refs):
        a, g = refs[:nt], refs[nt:2 * nt]
        send, recv = refs[2 * nt:]
        x, y, c, _ = _place()
        q = 2 * x + y
        sib, xn, yn = (x, y, 1 - c), (1 - x, y, c), (x, 1 - y, c)
        q_xn, q_yn, q_diag = 2 * (1 - x) + y, 2 * x + 1 - y, 2 * (1 - x) + 1 - y
        _handshake([sib, xn, yn])

        def whole(t, p):
            if t in by_columns:
                n = a[t].shape[1]
                return g[t].at[:, pl.ds(p * n, n)]
            return g[t].at[p]

        def part(t, p, hc, quarter=None):
            rows = a[t].shape[0]
            if quarter is None:
                return whole(t, p).at[pl.ds(hc * (rows // 2), rows // 2)]
            return whole(t, p).at[pl.ds(hc * (rows // 2) + quarter * (rows // 4), rows // 4)]

        def rc(t, k, src, dst, to):
            return pltpu.make_async_remote_copy(src_ref=src, dst_ref=dst, send_sem=send.at[t, k], recv_sem=recv.at[t, k],
                                                device_id=to, device_id_type=MESH)

        sent = []

        def go(cp):
            cp.start()
            sent.append(cp)

        def landed(t, k, piece, frm):
            rc(t, k, piece, piece, frm).wait_recv()
            return piece

        for t in range(nt):
            go(rc(t, 8, a[t], whole(t, q), sib))
            mine = _half(a[t], c)
            go(rc(t, 0, mine, part(t, q, c), xn))
            go(rc(t, 1, mine, part(t, q, c), yn))
        for t in range(nt):
            from_y = landed(t, 1, part(t, q_yn, c), yn)
            go(rc(t, 2, part(t, q_yn, c, 0), part(t, q_yn, c, 0), xn))
            go(rc(t, 5, from_y, from_y, sib))
            from_x = landed(t, 0, part(t, q_xn, c), xn)
            go(rc(t, 3, part(t, q_xn, c, 1), part(t, q_xn, c, 1), yn))
            go(rc(t, 4, from_x, from_x, sib))
        for t in range(nt):
            for k, frm in ((2, xn), (3, yn)):
                piece = landed(t, k, part(t, q_diag, c, k - 2), frm)
                go(rc(t, 4 + k, piece, piece, sib))
        for t in range(nt):
            landed(t, 4, part(t, q_xn, 1 - c), sib)
            landed(t, 5, part(t, q_yn, 1 - c), sib)
            landed(t, 6, part(t, q_diag, 1 - c, 0), sib)
            landed(t, 7, part(t, q_diag, 1 - c, 1), sib)
            landed(t, 8, whole(t, q), sib)
        for cp in sent:
            cp.wait_send()

    return pl.kernel(
        body, name=name,
        out_type=[_sds((a.shape[0], N_CHIPS * a.shape[1]) if t in by_columns else (N_CHIPS,) + a.shape, a.dtype)
                  for t, a in enumerate(tensors)],
        mesh=plsc.ScalarSubcoreMesh(axis_name="sequencer", num_cores=1),
        scratch_types=[pltpu.SemaphoreType.DMA((nt, 9)), pltpu.SemaphoreType.DMA((nt, 9))],
        cost_estimate=_copy_cost(sum(_nbytes(a) for a in tensors), 4),
        compiler_params=pltpu.CompilerParams(collective_id=GATHER_ID))(*tensors)


def pair_exchange(name, grads, on_sequencer):
    nt = len(grads)

    def body(*refs):
        g, theirs = refs[:nt], refs[nt:2 * nt]
        send, recv = refs[2 * nt:]
        x, y, c, _ = _place()
        if on_sequencer:
            _handshake([(x, y, 1 - c)])
        cps = []
        for t in range(nt):
            cp = pltpu.make_async_remote_copy(src_ref=_half(g[t], 1 - c, 1), dst_ref=theirs[t], send_sem=send.at[t],
                                              recv_sem=recv.at[t], device_id=(x, y, 1 - c), device_id_type=MESH)
            cp.start()
            cps.append(cp)
        for cp in cps:
            cp.wait()

    if not on_sequencer:
        return pl.pallas_call(
            body, in_specs=[ANY] * nt, out_specs=[ANY] * nt,
            out_shape=[_sds((N_CHIPS, a.shape[1] // 2, a.shape[2]), a.dtype) for a in grads],
            scratch_shapes=[pltpu.SemaphoreType.DMA((nt,)), pltpu.SemaphoreType.DMA((nt,))],
            name=name)(*grads)
    return pl.kernel(
        body, name=name, out_type=[_sds((N_CHIPS, a.shape[1] // 2, a.shape[2]), a.dtype) for a in grads],
        mesh=plsc.ScalarSubcoreMesh(axis_name="sequencer", num_cores=1),
        scratch_types=[pltpu.SemaphoreType.DMA((nt,)), pltpu.SemaphoreType.DMA((nt,))],
        cost_estimate=_copy_cost(sum(_nbytes(a) for a in grads), 0.5),
        compiler_params=pltpu.CompilerParams(collective_id=PAIR_ID))(*grads)


def chip_exchange(name, parts):
    nt = len(parts)

    def body(*refs):
        a, r = refs[:nt], refs[nt:2 * nt]
        send, recv = refs[2 * nt:]
        x, y, c, chips = _place()
        _handshake([(*chip, c) for chip in chips])
        cps = []
        for t in range(nt):
            for j, chip in enumerate(chips):
                cp = pltpu.make_async_remote_copy(
                    src_ref=a[t].at[2 * chip[0] + chip[1]], dst_ref=r[t].at[j], send_sem=send.at[t, j],
                    recv_sem=recv.at[t, j], device_id=(*chip, c), device_id_type=MESH)
                cp.start()
                cps.append(cp)
        for cp in cps:
            cp.wait()

    return pl.kernel(
        body, name=name, out_type=[_sds((N_CHIPS - 1,) + a.shape[1:], a.dtype) for a in parts],
        mesh=plsc.ScalarSubcoreMesh(axis_name="sequencer", num_cores=1),
        scratch_types=[pltpu.SemaphoreType.DMA((nt, 3)), pltpu.SemaphoreType.DMA((nt, 3))],
        cost_estimate=_copy_cost(sum(_nbytes(a) for a in parts), 0.75),
        compiler_params=pltpu.CompilerParams(collective_id=CHIP_EXCHANGE_ID))(*parts)


def pair_share(name, halves):
    nt = len(halves)

    def body(*refs):
        h, other = refs[:nt], refs[nt:2 * nt]
        send, recv = refs[2 * nt:]
        x, y, c, _ = _place()
        _handshake([(x, y, 1 - c)])
        cps = []
        for t in range(nt):
            cp = pltpu.make_async_remote_copy(src_ref=h[t], dst_ref=other[t], send_sem=send.at[t], recv_sem=recv.at[t],
                                              device_id=(x, y, 1 - c), device_id_type=MESH)
            cp.start()
            cps.append(cp)
        for cp in cps:
            cp.wait()

    return pl.kernel(
        body, name=name, out_type=[_sds(a.shape, a.dtype) for a in halves],
        mesh=plsc.ScalarSubcoreMesh(axis_name="sequencer", num_cores=1),
        scratch_types=[pltpu.SemaphoreType.DMA((nt,)), pltpu.SemaphoreType.DMA((nt,))],
        cost_estimate=_copy_cost(sum(_nbytes(a) for a in halves), 1),
        compiler_params=pltpu.CompilerParams(collective_id=PAIR_ID))(*halves)


def pack_rows(name, parts, rows):
    cdim = parts[0].shape[1]
    n = len(parts)
    vm = pl.BlockSpec(memory_space=pltpu.VMEM)

    def pack(*refs):
        p, o_ref = refs[:n], refs[n]
        at = 0
        for ref in p:
            o_ref[pl.ds(at, ref.shape[0]), :] = ref[...]
            at += ref.shape[0]
        o_ref[pl.ds(at, rows - at), :] = jnp.zeros((rows - at, cdim), F32)

    return pl.pallas_call(pack, in_specs=[vm] * n, out_specs=vm, out_shape=_sds((rows, cdim), F32), name=name)(*parts)


def all_reduce_small(parts, rows):
    cdim = parts[0].shape[1]
    vm = pl.BlockSpec(memory_space=pltpu.VMEM)
    mine = pack_rows("small_pack", parts, rows)

    def exchange(mine_ref, buf, send, recv, lsem):
        x, y, c, _ = _place()
        me = 4 * x + 2 * y + c
        peers = [(x ^ (k >> 2), y ^ ((k >> 1) & 1), c ^ (k & 1)) for k in range(1, 8)]
        _handshake(peers)
        own = pltpu.make_async_copy(mine_ref, buf.at[me], lsem)
        own.start()
        cps = []
        for k, to in enumerate(peers):
            cp = pltpu.make_async_remote_copy(src_ref=mine_ref, dst_ref=buf.at[me], send_sem=send.at[k], recv_sem=recv.at[k],
                                              device_id=to, device_id_type=MESH)
            cp.start()
            cps.append(cp)
        for k, (px, py, pc) in enumerate(peers):
            pltpu.make_async_remote_copy(src_ref=mine_ref, dst_ref=buf.at[4 * px + 2 * py + pc], send_sem=send.at[k],
                                         recv_sem=recv.at[k], device_id=(x, y, c), device_id_type=MESH).wait_recv()
        for cp in cps:
            cp.wait_send()
        own.wait()

    landed = pl.kernel(
        exchange, name="small_exchange", out_type=_sds((8, rows, cdim), F32),
        mesh=plsc.ScalarSubcoreMesh(axis_name="sequencer", num_cores=1),
        scratch_types=[pltpu.SemaphoreType.DMA((7,)), pltpu.SemaphoreType.DMA((7,)), pltpu.SemaphoreType.DMA],
        cost_estimate=_copy_cost(rows * cdim * 4, 7),
        compiler_params=pltpu.CompilerParams(collective_id=ALL_ID))(mine)

    def total(buf, o_ref):
        acc = buf[0]
        for d in range(1, 8):
            acc = acc + buf[d]
        o_ref[...] = acc

    return pl.pallas_call(total, in_specs=[vm], out_specs=vm, out_shape=_sds((rows, cdim), F32), name="small_sum")(landed)


def pair_sum(gs, theirs, core, tm=256):
    n = len(gs)
    _, r, c = gs[0].shape
    tm = _tile(r // 2, tm)
    nh = r // 2 // tm

    def body(core_ref, *refs):
        for a_ref, b_ref, o_ref in zip(refs[:n], refs[n:2 * n], refs[2 * n:]):
            o_ref[...] = (a_ref[...].astype(F32) + b_ref[...].astype(F32)).astype(BF16)

    blk = (N_CHIPS, tm, c)
    own = pl.BlockSpec(blk, lambda i, cr: (0, cr[0] * nh + i, 0))
    half = pl.BlockSpec(blk, lambda i, cr: (0, i, 0))
    return pl.pallas_call(
        body, grid_spec=pltpu.PrefetchScalarGridSpec(
            num_scalar_prefetch=1, grid=(nh,), in_specs=[own] * n + [half] * n, out_specs=[half] * n),
        out_shape=[_sds(t.shape, BF16) for t in theirs], compiler_params=_params(("parallel",)),
        name="pair_sum")(core, *gs, *theirs)


def chip_sum(own, landed, chip, stack, layer, layers, tm=256):
    _, r, c = own.shape
    tm = _tile(r, tm)

    def body(chip_ref, own_ref, l_ref, *rest):
        acc = own_ref[...].astype(F32)
        for j in range(N_CHIPS - 1):
            acc = acc + l_ref[j].astype(F32)
        rest[-1][...] = acc

    in_specs = [pl.BlockSpec((None, tm, c), lambda i, qr: (qr[0], i, 0)),
                pl.BlockSpec((N_CHIPS - 1, tm, c), lambda i, qr: (0, i, 0))]
    args = [chip, own, landed]
    if stack is not None:
        in_specs.append(ANY)
        args.append(stack)
    return pl.pallas_call(
        body, grid_spec=pltpu.PrefetchScalarGridSpec(
            num_scalar_prefetch=1, grid=(r // tm,), in_specs=in_specs,
            out_specs=pl.BlockSpec((None, tm, c), lambda i, qr: (layer, i, 0))),
        out_shape=_sds((layers, r, c), F32), input_output_aliases={3: 0} if stack is not None else {},
        compiler_params=_params(("parallel",)), name="chip_sum")(*args)


def _adamw_math(w, g, m, v):
    bc1 = 1.0 - ADAM_B1 ** ADAM_STEP
    bc2 = 1.0 - ADAM_B2 ** ADAM_STEP
    nm = ADAM_B1 * m + (1.0 - ADAM_B1) * g
    nv = ADAM_B2 * v + (1.0 - ADAM_B2) * (g * g)
    return -ADAM_LR * ((nm / bc1) / (jnp.sqrt(nv / bc2) + ADAM_EPS) + ADAM_WD * w), nm, nv


def vector_update(red, chip, ws, ms, vs, where):
    n = len(ws)
    dd = red.shape[1]

    def body(chip_ref, red_ref, *refs):
        w_r, m_r, v_r = refs[0:n], refs[n:2 * n], refs[2 * n:3 * n]
        g_o, d_o, m_o, v_o = (refs[(3 + k) * n:(4 + k) * n] for k in range(4))
        q = chip_ref[0]

        def chip_block(val, width):
            out = val[:, 0:width]
            for p in range(1, val.shape[1] // width):
                out = jnp.where(q == p, val[:, p * width:(p + 1) * width], out)
            return out

        for k in range(n):
            for idx, r0, nr, cols in where[k]:
                width = w_r[k].shape[-1]
                if cols == "chip" and width * N_CHIPS != dd:
                    g = chip_block(jnp.concatenate([red_ref[pl.ds(r0 + j, 1), :] for j in range(nr)], axis=1), width)
                else:
                    g = red_ref[pl.ds(r0, nr), :]
                    g = chip_block(g, width) if cols == "chip" else g if cols == "all" else g[:, 0:cols]
                delta, nm, nv = _adamw_math(w_r[k][idx], g, m_r[k][idx], v_r[k][idx])
                g_o[k][idx] = g
                d_o[k][idx] = delta
                m_o[k][idx] = nm
                v_o[k][idx] = nv

    vm = pl.BlockSpec(memory_space=pltpu.VMEM)
    outs = pl.pallas_call(
        body, in_specs=[pl.BlockSpec(memory_space=pltpu.SMEM), vm] + [vm] * (3 * n), out_specs=[vm] * (4 * n),
        out_shape=[_sds(w.shape, F32) for w in ws] * 4, name="vector_update")(chip, red, *ws, *ms, *vs)
    return [outs[k * n:(k + 1) * n] for k in range(4)]


def adamw_joined(w, m, v, g_mine, g_theirs, core, tm=512):
    nl, r, c = w.shape
    tm = _tile(r // 2, tm)
    nh = r // 2 // tm

    def body(core_ref, w_ref, m_ref, v_ref, gm_ref, gt_ref, g_ref, d_ref, nm_ref, nv_ref):
        mine = (pl.program_id(1) // nh) == core_ref[0]
        gv = jnp.where(mine, gm_ref[...], gt_ref[...])
        g_ref[...] = gv
        d_ref[...], nm_ref[...], nv_ref[...] = _adamw_math(w_ref[...], gv, m_ref[...], v_ref[...])

    full = pl.BlockSpec((None, tm, c), lambda l, i, cr: (l, i, 0))
    mine = pl.BlockSpec((None, tm, c), lambda l, i, cr: (l, jnp.where(i // nh == cr[0], i % nh, 0), 0))
    theirs = pl.BlockSpec((None, tm, c), lambda l, i, cr: (l, jnp.where(i // nh == cr[0], 0, i % nh), 0))
    return pl.pallas_call(
        body, grid_spec=pltpu.PrefetchScalarGridSpec(
            num_scalar_prefetch=1, grid=(nl, r // tm), in_specs=[full, full, full, mine, theirs], out_specs=[full] * 4),
        out_shape=[_sds((nl, r, c), F32)] * 4, compiler_params=_params(("parallel", "parallel")),
        name="adamw_joined")(core, w, m, v, g_mine, g_theirs)


WEIGHTS = ['sc_w_in', 'sc_conv_w', 'sc_w_out', 'mla_w_dq', 'mla_g_q', 'mla_w_uq', 'mla_w_dkv', 'mla_g_kv', 'mla_w_uk',
           'mla_w_uv', 'mla_w_o', 'cf_w_pw1', 'cf_b_pw1', 'cf_dw_w', 'cf_dw_b', 'cf_norm_g', 'cf_norm_b', 'cf_w_pw2',
           'cf_b_pw2', 'ff_w1', 'ff_w2', 'ln_mix_g', 'ln_mix_b', 'ln_ff_g', 'ln_ff_b']
ARGS = ['x'] + WEIGHTS + ['loss_target'] + ['m_' + n for n in WEIGHTS] + ['v_' + n for n in WEIGHTS]


def _sq_relu(h):
    r = jnp.maximum(h, jnp.zeros_like(h))
    return r * r


def _mlp_forward(i, x, xb, w1, w2, g, b):
    hb = mm_plain_nn(f"mlp{i}_up", xb, w1, BF16, tm=2048, tn=1024)
    y, yb, xh, rstd = mm_residual_ln(f"mlp{i}_down_ln", hb, w2, x, g, b, tk=4096, a_fn=_sq_relu)
    return (y, yb), dict(xb=xb, hb=hb, xh=xh, rstd=rstd, g=g)


def _mlp_backward(i, dr, drb, sv, w1, w2, dw1, dw2, reduce_after, mixer_ln):
    s = dr.shape[0]
    tm, tn = _tile(s, 2048), 1024

    def epi(acc, e, o):
        o[0][...] = (acc * (2.0 * jnp.maximum(e[0][...].astype(F32), 0.0))).astype(BF16)

    dhb = mm_nt(f"mlp{i}_down_bwd", drb, w2, s, tm, tn, 1024, epi, [_sds((s, w2.k), BF16)], [_ij(tm, tn)],
                [sv["hb"]], [_ij(tm, tn)])[0]
    g_w2 = mm_tn(f"mlp{i}_dw2", sv["hb"], drb, dw2, s, 1024, 1024, a_fn=_sq_relu)
    g_w1 = mm_tn(f"mlp{i}_dw1", sv["xb"], dhb, dw1, s, 1024, 1024)
    dhb = reduce_after(dhb, {f"w1_{i}": g_w1, f"w2_{i}": g_w2})
    return mm_nt_ln_backward(f"mlp{i}_up_bwd", dhb, w1, dr, *mixer_ln, tk=4096)


def kernel(x, sc_w_in, sc_conv_w, sc_w_out, mla_w_dq, mla_g_q, mla_w_uq, mla_w_dkv, mla_g_kv, mla_w_uk, mla_w_uv, mla_w_o, cf_w_pw1, cf_b_pw1, cf_dw_w, cf_dw_b, cf_norm_g, cf_norm_b, cf_w_pw2, cf_b_pw2, ff_w1, ff_w2, ln_mix_g, ln_mix_b, ln_ff_g, ln_ff_b, loss_target, m_sc_w_in, m_sc_conv_w, m_sc_w_out, m_mla_w_dq, m_mla_g_q, m_mla_w_uq, m_mla_w_dkv, m_mla_g_kv, m_mla_w_uk, m_mla_w_uv, m_mla_w_o, m_cf_w_pw1, m_cf_b_pw1, m_cf_dw_w, m_cf_dw_b, m_cf_norm_g, m_cf_norm_b, m_cf_w_pw2, m_cf_b_pw2, m_ff_w1, m_ff_w2, m_ln_mix_g, m_ln_mix_b, m_ln_ff_g, m_ln_ff_b, v_sc_w_in, v_sc_conv_w, v_sc_w_out, v_mla_w_dq, v_mla_g_q, v_mla_w_uq, v_mla_w_dkv, v_mla_g_kv, v_mla_w_uk, v_mla_w_uv, v_mla_w_o, v_cf_w_pw1, v_cf_b_pw1, v_cf_dw_w, v_cf_dw_b, v_cf_norm_g, v_cf_norm_b, v_cf_w_pw2, v_cf_b_pw2, v_ff_w1, v_ff_w2, v_ln_mix_g, v_ln_mix_b, v_ln_ff_g, v_ln_ff_b):
    given = dict(zip(ARGS, (x, sc_w_in, sc_conv_w, sc_w_out, mla_w_dq, mla_g_q, mla_w_uq, mla_w_dkv, mla_g_kv, mla_w_uk, mla_w_uv, mla_w_o, cf_w_pw1, cf_b_pw1, cf_dw_w, cf_dw_b, cf_norm_g, cf_norm_b, cf_w_pw2, cf_b_pw2, ff_w1, ff_w2, ln_mix_g, ln_mix_b, ln_ff_g, ln_ff_b, loss_target, m_sc_w_in, m_sc_conv_w, m_sc_w_out, m_mla_w_dq, m_mla_g_q, m_mla_w_uq, m_mla_w_dkv, m_mla_g_kv, m_mla_w_uk, m_mla_w_uv, m_mla_w_o, m_cf_w_pw1, m_cf_b_pw1, m_cf_dw_w, m_cf_dw_b, m_cf_norm_g, m_cf_norm_b, m_cf_w_pw2, m_cf_b_pw2, m_ff_w1, m_ff_w2, m_ln_mix_g, m_ln_mix_b, m_ln_ff_g, m_ln_ff_b, v_sc_w_in, v_sc_conv_w, v_sc_w_out, v_mla_w_dq, v_mla_g_q, v_mla_w_uq, v_mla_w_dkv, v_mla_g_kv, v_mla_w_uk, v_mla_w_uv, v_mla_w_o, v_cf_w_pw1, v_cf_b_pw1, v_cf_dw_w, v_cf_dw_b, v_cf_norm_g, v_cf_norm_b, v_cf_w_pw2, v_cf_b_pw2, v_ff_w1, v_ff_w2, v_ln_mix_g, v_ln_mix_b, v_ln_ff_g, v_ln_ff_b)))
    s, d = x.shape[1], x.shape[2]
    d_ff = 4 * d
    dq4 = d // N_CHIPS
    xq = lax.axis_index("x") * 2 + lax.axis_index("y")

    w_dkv_pad = jnp.pad(mla_w_dkv[0], ((0, 0), (0, 128 - QK_ROPE)))
    w_uq_pad = jnp.pad(mla_w_uq[0].reshape(Q_LORA, 2, QK_NOPE + QK_ROPE), ((0, 0), (0, 0), (0, HEAD_PAD - QK_NOPE - QK_ROPE)))
    small = pack_rows("vector_weights_pack", [
        sc_conv_w.reshape(2 * SC_WIDTH, dq4), cf_b_pw1.reshape(2, dq4), cf_dw_w[0], cf_dw_b, cf_norm_g, cf_norm_b,
        cf_b_pw2], 64)
    mlp_w = lambda i: [ff_w1[i].astype(BF16), ff_w2[i].astype(BF16)]
    g_in, g_out, g_w1, g_w2 = [None] * 2, [None] * 2, [None] * DEPTH, [None] * DEPTH
    g_in[0], g_out[0], g_small = gather_shards(
        "gather_mixer0", [sc_w_in[0].astype(BF16), sc_w_out[0].astype(BF16), small], by_columns=(0,))
    (g_w1[0],) = gather_shards("gather_up0", [ff_w1[0].astype(BF16)], by_columns=(0,))
    (g_w2[0],) = gather_shards("gather_down0", [ff_w2[0].astype(BF16)])
    g_dqkv, g_uq, g_uk, g_uv, g_o = gather_shards("gather_mixer1", [
        jnp.concatenate([mla_w_dq[0], w_dkv_pad], axis=1).astype(BF16),
        w_uq_pad.reshape(Q_LORA, 2 * HEAD_PAD).astype(BF16),
        mla_w_uk.reshape(KV_LORA // N_CHIPS, N_HEADS * QK_NOPE).astype(BF16),
        mla_w_uv.reshape(KV_LORA // N_CHIPS, N_HEADS * V_HEAD).astype(BF16), mla_w_o[0].astype(BF16)], by_columns=(1,))
    g_w1[1], g_w2[1] = gather_shards("gather_mlp1", mlp_w(1), by_columns=(0,))
    g_pw1, g_pw2, g_w1[2], g_w2[2] = gather_shards(
        "gather_layer2", [cf_w_pw1[0].astype(BF16), cf_w_pw2[0].astype(BF16)] + mlp_w(2), by_columns=(0, 2))
    g_in[1], g_out[1], g_w1[3], g_w2[3] = gather_shards(
        "gather_layer3", [sc_w_in[1].astype(BF16), sc_w_out[1].astype(BF16)] + mlp_w(3), by_columns=(0, 2))

    wd_t = Q_LORA + KV_LORA + 128
    w_in = [Stk("full", d, 3 * d, g_in[j]) for j in range(2)]
    w_out = [Stk("row", d, d, g_out[j]) for j in range(2)]
    w_dqkv = Stk("row", d, wd_t, g_dqkv)
    w_uq = Stk("full", Q_LORA, N_HEADS * HEAD_PAD, g_uq)
    w_uk = Stk("row", KV_LORA, N_HEADS * QK_NOPE, g_uk)
    w_uv = Stk("row", KV_LORA, N_HEADS * V_HEAD, g_uv)
    w_o = Stk("row", d, d, g_o)
    w_pw1 = Stk("full", d, 2 * d, g_pw1)
    w_pw2 = Stk("row", d, d, g_pw2)
    w_1 = [Stk("full", d, d_ff, g_w1[i]) for i in range(DEPTH)]
    w_2 = [Stk("row", d_ff, d, g_w2[i]) for i in range(DEPTH)]

    def wide(rows):
        return jnp.swapaxes(rows, 0, 1).reshape(rows.shape[1], d)

    conv_w = wide(g_small[:, 0:6]).reshape(2, SC_WIDTH, d)
    b_pw1 = g_small[:, 6:8].reshape(1, 2 * d)
    dw_w = wide(g_small[:, 8:39])
    dw_b, norm_g, norm_b, b_pw2 = (wide(g_small[:, 39 + k:40 + k]) for k in range(4))

    pos = jnp.arange(s, dtype=F32)
    inv_freq = ROPE_THETA ** (-jnp.arange(0, QK_ROPE, 2, dtype=F32) / QK_ROPE)
    ang = pos[:, None] * inv_freq[None, :]
    cos, sin, zero = jnp.cos(ang), jnp.sin(ang), jnp.zeros((s, 128 - QK_ROPE), F32)
    cf = jnp.concatenate([cos, cos, zero], axis=1)
    sf = jnp.concatenate([-sin, sin, zero], axis=1)

    def row(a, i):
        return a[i:i + 1]

    xs = x.reshape(s, d)
    cur = (xs, xs.astype(BF16))
    tape = []
    for i in range(DEPTH):
        mixer, j = i % 3, i // 3
        xf, xb = cur
        lg, lb = row(ln_mix_g, i), row(ln_mix_b, i)
        if mixer == 0:
            u = mm_plain_nn(f"sc{j}_in", xb, w_in[j], F32, tn=3 * dq4)
            gb = short_conv_gate(u, conv_w[j])
            y, yb, xh, rstd = mm_residual_ln(f"sc{j}_out_ln", gb, w_out[j], xf, lg, lb)
            sv = dict(xb=xb, u=u, gb=gb)
        elif mixer == 1:
            t = mm_plain_nn("mla_down", xb, w_dqkv, F32, tn=wd_t // 2)
            cq, ckv, kpe = mla_latents(t, mla_g_q, mla_g_kv, cf, sf)
            qh = mla_queries(cq, w_uq, cf, sf)
            kh = mla_keys(ckv, w_uk, kpe)
            vh = mm_plain_nn("mla_values", ckv, w_uv, BF16, tk=KV_LORA)
            oh = attention(qh, kh, vh)
            y, yb, xh, rstd = mm_residual_ln("mla_out_ln", oh, w_o, xf, lg, lb)
            sv = dict(xb=xb, t=t, cq=cq, ckv=ckv, qh=qh, kh=kh, vh=vh, oh=oh)
        else:
            u = mm_plain_nn("cf_pw1", xb, w_pw1, F32, bias=b_pw1)
            hc = conformer_glu_conv(u, dw_w, dw_b)
            sb = conformer_norm_swish(hc, norm_g, norm_b)
            y, yb, xh, rstd = mm_residual_ln("cf_pw2_ln", sb, w_pw2, xf, lg, lb, bias=b_pw2)
            sv = dict(xb=xb, u=u, hc=hc, sb=sb)
        sv.update(xh=xh, rstd=rstd, g=lg)
        cur, sv_mlp = _mlp_forward(i, y, yb, w_1[i], w_2[i], row(ln_ff_g, i), row(ln_ff_b, i))
        tape.append((sv, sv_mlp))

    g_ln = {n: [None] * DEPTH for n in ("ln_mix_g", "ln_mix_b", "ln_ff_g", "ln_ff_b")}
    last = tape[DEPTH - 1][1]
    dr, drb, g_ln["ln_ff_g"][DEPTH - 1], g_ln["ln_ff_b"][DEPTH - 1], _, loss_part = loss_ln_backward(
        cur[0], loss_target.reshape(s, d), last["xh"], last["rstd"], last["g"])

    grads = {}
    smalls = {}
    conv_grads = [None, None]
    core = lax.axis_index("c").astype(jnp.int32).reshape(1)
    chip = xq.astype(jnp.int32).reshape(1)
    pairs, landed = {}, {}
    ready, theirs = [], {}

    def hold(xs, others):
        live = [x for x in xs if x is not None]
        out = lax.optimization_barrier((*live, *others))
        rest = iter(out[:len(live)])
        return tuple(None if x is None else next(rest) for x in xs), list(out[len(live):])

    def reduce_after(x, new, early=False):
        out = lax.optimization_barrier((x, *new.values()))
        grads.update(zip(new, out[1:]))
        if early:
            theirs.update(zip(new, pair_exchange(f"pair_exchange_{len(theirs)}", list(out[1:]), True)))
        ready.extend(new)
        return out[0]

    def reduce_layer(i, x):
        late = [n for n in ready if n not in theirs]
        if late:
            theirs.update(zip(late, pair_exchange(f"pair_exchange_layer{i}", [grads[n] for n in late], False)))
        by_shape = {}
        for n in ready:
            by_shape.setdefault(grads[n].shape, []).append(n)
        for names in by_shape.values():
            pairs.update(zip(names, pair_sum([grads[n] for n in names], [theirs[n] for n in names], core)))
        sums = [pairs[n] for n in ready]
        landed.update(zip(ready, chip_exchange(f"chip_exchange_layer{i}", sums)))
        exchanged.append(list(ready))
        ready.clear()
        return hold(x, sums)[0]

    groups = [["in_0", "in_1"], ["out_0", "out_1"], ["dqkv"], ["uq"], ["uk"], ["uv"], ["o"], ["pw1"], ["pw2"],
              [f"w1_{i}" for i in range(DEPTH)], [f"w2_{i}" for i in range(DEPTH)]]
    stacks = [None] * len(groups)
    exchanged = []

    def sum_layer(x, last=False):
        names = exchanged.pop(0)
        if last:
            x, held = hold(x, [landed[n] for n in names])
            landed.update(zip(names, held))
        new = []
        for n in names:
            k = next(k for k, members in enumerate(groups) if n in members)
            stacks[k] = chip_sum(pairs[n], landed[n], chip, stacks[k], groups[k].index(n), len(groups[k]))
            new.append(stacks[k])
        return x if last else hold(x, new)[0]

    for i in reversed(range(DEPTH)):
        mixer, j = i % 3, i // 3
        sv, sv_mlp = tape[i]
        dr, drb, g_ln["ln_mix_g"][i], g_ln["ln_mix_b"][i], dr_sum = _mlp_backward(
            i, dr, drb, sv_mlp, w_1[i], w_2[i], Stk("col", d, d_ff), Stk("row", d_ff, d),
            lambda x_, new: reduce_after(x_, new, early=i > 0), (sv["xh"], sv["rstd"], sv["g"]))
        if i == 0:
            dr, drb = reduce_layer("0_mlp", (dr, drb))

        def to_input(name, a, w, tk, in_parts=False):
            side_by_side = {}
            if in_parts:
                nparts = a.shape[0]
                side_by_side = dict(
                    a_spec_fn=lambda tm, tk_: pl.BlockSpec((nparts, tm, d), lambda i_, j_, k_: (0, i_, 0)),
                    a_fn=lambda blk: jnp.concatenate([blk[p] for p in range(nparts)], axis=1))
                tk = w.n
            if i == 0:
                if side_by_side:
                    side_by_side["a_spec_fn"] = (s, side_by_side["a_spec_fn"])
                return mm_plain_nt(name, a, w, F32, tm=512, tn=1024, tk=tk, add=dr, add_scale=ALPHA, **side_by_side), None
            prev = tape[i - 1][1]
            out = mm_nt_ln_backward(name, a, w, dr, prev["xh"], prev["rstd"], prev["g"], tk=tk, **side_by_side)
            g_ln["ln_ff_g"][i - 1], g_ln["ln_ff_b"][i - 1] = out[2], out[3]
            return out[0], out[1]

        if mixer == 0:
            dgate = mm_plain_nt(f"sc{j}_out_bwd", drb, w_out[j], F32)
            dw_out = mm_tn(f"sc{j}_dw_out", sv["gb"], drb, Stk("row", d, d), s, 512, 1024)
            du, conv_grads[j] = short_conv_gate_bwd(sv["u"], conv_w[j], dgate)
            nb = d // 256
            dw_in = mm_tn(
                f"sc{j}_dw_in", sv["xb"], du, Stk("col", d, 3 * d), s, 1024, 256,
                b_spec=pl.BlockSpec((None, s, 256), lambda i_, j_, k_: (j_ // nb, k_, j_ % nb)))
            du = reduce_after(du, {f"in_{j}": dw_in, f"out_{j}": dw_out})
            dr, drb = to_input(f"sc{j}_in_bwd", du, w_in[j], d, in_parts=True)
        elif mixer == 1:
            do = mm_plain_nt("mla_out_bwd", drb, w_o, BF16)
            g_o = mm_tn("mla_dw_o", sv["oh"], drb, Stk("row", d, d), s, 512, 1024)
            dqh, dkh, dvh = attention_bwd(sv["qh"], sv["kh"], sv["vh"], do)
            dql, dkn, dkpe = mla_unrope_grads(dqh, dkh, cf, sf)
            g_uq = mm_tn("mla_dw_uq", sv["cq"], dql, Stk("col", Q_LORA, N_HEADS * HEAD_PAD), s, Q_LORA, 512)
            dcq = mm_plain_nt("mla_uq_bwd", dql, w_uq, F32, tn=Q_LORA)
            g_uk = mm_tn("mla_dw_uk", sv["ckv"], dkn, Stk("row", KV_LORA, N_HEADS * QK_NOPE), s, KV_LORA, 1024)
            g_uv = mm_tn("mla_dw_uv", sv["ckv"], dvh, Stk("row", KV_LORA, N_HEADS * V_HEAD), s, KV_LORA, 1024)
            dckv = mm_plain_nt("mla_uk_bwd", dkn, w_uk, F32, tn=KV_LORA)
            dckv = mm_plain_nt("mla_uv_bwd", dvh, w_uv, F32, tn=KV_LORA, add=dckv)
            dt, smalls["g_q"], smalls["g_kv"] = mla_latents_bwd(sv["t"], mla_g_q, mla_g_kv, cf, sf, dcq, dckv, dkpe)
            g_dqkv = mm_tn("mla_dw_down", sv["xb"], dt, Stk("row", d, wd_t), s, 512, wd_t)
            dt = reduce_after(dt, {"dqkv": g_dqkv, "uq": g_uq, "uk": g_uk, "uv": g_uv, "o": g_o})
            dr, drb = to_input("mla_down_bwd", dt, w_dqkv, wd_t)
        else:
            dsw = mm_plain_nt("cf_pw2_bwd", drb, w_pw2, F32)
            g_pw2 = mm_tn("cf_dw_pw2", sv["sb"], drb, Stk("row", d, d), s, 512, 1024)
            smalls["b_pw2"] = dr_sum
            dhc, smalls["norm_g"], smalls["norm_b"] = conformer_norm_swish_bwd(sv["hc"], norm_g, norm_b, dsw)
            du, smalls["b_pw1"], smalls["dw_w"], smalls["dw_b"] = conformer_glu_conv_bwd(sv["u"], dw_w, dhc)
            nb = d // 512
            g_pw1 = mm_tn(
                "cf_dw_pw1", sv["xb"], du, Stk("col", d, 2 * d), s, 1024, 512,
                b_spec=pl.BlockSpec((None, s, 512), lambda i_, j_, k_: (j_ // nb, k_, j_ % nb)))
            du = reduce_after(du, {"pw1": g_pw1, "pw2": g_pw2})
            dr, drb = to_input("cf_pw1_bwd", du, w_pw1, d, in_parts=True)
        if i < DEPTH - 1:
            dr, drb = sum_layer((dr, drb))
        dr, drb = reduce_layer(i, (dr, drb))
    grad_x = sum_layer(sum_layer((dr, None), last=True), last=True)[0].reshape(1, s, d)

    mine = stacks
    other = (pair_share("pair_share_mixers", mine[:9]) + pair_share("pair_share_up", mine[9:10])
             + pair_share("pair_share_down", mine[10:]))

    def padded(get):
        dqkv = jnp.concatenate([get("mla_w_dq")[0], jnp.pad(get("mla_w_dkv")[0], ((0, 0), (0, 128 - QK_ROPE)))], axis=1)
        uq = jnp.pad(get("mla_w_uq")[0].reshape(Q_LORA, 2, QK_NOPE + QK_ROPE),
                     ((0, 0), (0, 0), (0, HEAD_PAD - QK_NOPE - QK_ROPE))).reshape(Q_LORA, 2 * HEAD_PAD)
        return [get("sc_w_in"), get("sc_w_out"), dqkv[None], uq[None],
                get("mla_w_uk").reshape(1, KV_LORA // N_CHIPS, d), get("mla_w_uv").reshape(1, KV_LORA // N_CHIPS, d),
                get("mla_w_o"), get("cf_w_pw1"), get("cf_w_pw2"), get("ff_w1"), get("ff_w2")]

    w_l, m_l, v_l = (padded(lambda n, p=p: given[p + n]) for p in ("", "m_", "v_"))
    res = [adamw_joined(w_l[k], m_l[k], v_l[k], mine[k], other[k], core) for k in range(len(groups))]

    def unpadded(k):
        r_in, r_out, r_dqkv, r_uq, r_uk, r_uv, r_o, r_pw1, r_pw2, r_w1, r_w2 = (r[k] for r in res)
        return {
            "sc_w_in": r_in, "sc_w_out": r_out, "mla_w_dq": r_dqkv[:, :, 0:Q_LORA],
            "mla_w_dkv": r_dqkv[:, :, Q_LORA:Q_LORA + KV_LORA + QK_ROPE],
            "mla_w_uq": r_uq.reshape(1, Q_LORA, 2, HEAD_PAD)[:, :, :, 0:QK_NOPE + QK_ROPE].reshape(mla_w_uq.shape),
            "mla_w_uk": r_uk.reshape(mla_w_uk.shape), "mla_w_uv": r_uv.reshape(mla_w_uv.shape),
            "mla_w_o": r_o, "cf_w_pw1": r_pw1, "cf_w_pw2": r_pw2, "ff_w1": r_w1, "ff_w2": r_w2}

    big_g, big_d, big_m, big_v = (unpadded(k) for k in range(4))

    pad_row = lambda a: jnp.pad(a, ((0, 0), (0, d - a.shape[1])))
    small_parts = ([g for n in ("ln_mix_g", "ln_mix_b", "ln_ff_g", "ln_ff_b") for g in g_ln[n]]
                   + [pad_row(smalls["g_q"]), pad_row(smalls["g_kv"]), conv_grads[0], conv_grads[1],
                      smalls["b_pw1"].reshape(2, d), smalls["dw_w"], smalls["dw_b"], smalls["norm_g"], smalls["norm_b"],
                      smalls["b_pw2"], loss_part])
    red = all_reduce_small(small_parts, 64)
    loss = red[61, 0]

    where = {
        "ln_mix_g": [((), 0, DEPTH, "all")], "ln_mix_b": [((), 4, DEPTH, "all")],
        "ln_ff_g": [((), 8, DEPTH, "all")], "ln_ff_b": [((), 12, DEPTH, "all")],
        "mla_g_q": [((), 16, 1, Q_LORA)], "mla_g_kv": [((), 17, 1, KV_LORA)],
        "sc_conv_w": [((0,), 18, SC_WIDTH, "chip"), ((1,), 21, SC_WIDTH, "chip")],
        "cf_b_pw1": [((), 24, 2, "chip")], "cf_dw_w": [((0,), 26, CONF_WIDTH, "chip")],
        "cf_dw_b": [((), 57, 1, "chip")], "cf_norm_g": [((), 58, 1, "chip")], "cf_norm_b": [((), 59, 1, "chip")],
        "cf_b_pw2": [((), 60, 1, "chip")]}
    vec = list(where)
    vec_res = vector_update(red, chip, [given[n] for n in vec], [given["m_" + n] for n in vec],
                            [given["v_" + n] for n in vec], [where[n] for n in vec])
    gw = dict(big_g)
    upd = {n: [big_d[n], big_m[n], big_v[n]] for n in big_g}
    for k, n in enumerate(vec):
        gw[n] = vec_res[0][k]
        upd[n] = [vec_res[1][k], vec_res[2][k], vec_res[3][k]]

    return (loss, grad_x, *[gw[n] for n in WEIGHTS], *[upd[n][0] for n in WEIGHTS],
            *[upd[n][1] for n in WEIGHTS], *[upd[n][2] for n in WEIGHTS])
```

```python
import jax
import jax.numpy as jnp
from jax import lax
from jax.experimental import pallas as pl
from jax.experimental.pallas import tpu as pltpu
from jax.experimental.pallas import tpu_sc as plsc

F32 = jnp.float32
BF16 = jnp.bfloat16
MESH = pl.DeviceIdType.MESH

DEPTH = 4
ALPHA = (2.0 * DEPTH) ** 0.25
LN_EPS = 1e-5
RMS_EPS = 1e-6
CHUNK_SHIFT = 6
N_HEADS = 8
QK_NOPE = 128
QK_ROPE = 64
V_HEAD = 128
HEAD_PAD = 256
Q_LORA = 384
KV_LORA = 256
ROPE_THETA = 10000.0
SC_WIDTH = 3
CONF_WIDTH = 31
CONV_PAD = 32
CONV_CHUNK = 64
N_CHIPS = 4
ATTN_SCALE = (QK_NOPE + QK_ROPE) ** -0.5

ADAM_LR = 0.001
ADAM_B1 = 0.9
ADAM_B2 = 0.999
ADAM_EPS = 1e-08
ADAM_WD = 0.01
ADAM_STEP = 10

VMEM_LIMIT = 56 * 2**20

NN = (((1,), (0,)), ((), ()))
NT = (((1,), (1,)), ((), ()))
TN = (((0,), (0,)), ((), ()))


def _params(sem=None):
    return pltpu.CompilerParams(dimension_semantics=sem, vmem_limit_bytes=VMEM_LIMIT)


class Stk:
    def __init__(self, kind, k, n, arr=None):
        self.kind, self.k, self.n = kind, k, n
        self.plain = kind != "col"
        self.nloc = n // N_CHIPS if kind == "col" else n
        self.arr = arr.reshape(k, n) if arr is not None and self.plain else arr

    @property
    def shape(self):
        return (self.k, self.n) if self.plain else (N_CHIPS, self.k, self.nloc)

    def spec(self, bk, bn, f, resident=False):
        if self.plain:
            return pl.BlockSpec((bk, bn), f, pipeline_mode=pl.Buffered(1)) if resident else pl.BlockSpec((bk, bn), f)
        assert self.k % bk == 0 and self.nloc % bn == 0, (self.k, bk, self.nloc, bn)
        pn = self.nloc // bn

        def imap(*g):
            kb, nb = f(*g)
            return nb // pn, kb, nb % pn

        return pl.BlockSpec((None, bk, bn), imap)


def _mm(name, mode, a, b, grid, a_spec, b_spec, acc_shape, extras, extra_specs, out_shapes, out_specs, epi, a_fn=None,
        rows_in_order=False):
    nk = grid[2]
    ne = len(extras)

    def body(*refs):
        a_ref, b_ref = refs[0], refs[1]
        e_refs = refs[2:2 + ne]
        av = a_ref[...] if a_fn is None else a_fn(a_ref[...])
        part = lax.dot_general(av, b_ref[...], mode, preferred_element_type=F32)
        if nk == 1:
            epi(part, e_refs, refs[2 + ne:])
            return
        o_refs = refs[2 + ne:-1]
        acc = refs[-1]
        k = pl.program_id(2)

        @pl.when(k == 0)
        def _():
            acc[...] = part

        @pl.when(k > 0)
        def _():
            acc[...] += part

        @pl.when(k == nk - 1)
        def _():
            epi(acc[...], e_refs, o_refs)

    return pl.pallas_call(
        body, grid=grid, in_specs=[a_spec, b_spec, *extra_specs], out_specs=out_specs, out_shape=out_shapes,
        scratch_shapes=[pltpu.VMEM(acc_shape, F32)] if nk > 1 else [],
        compiler_params=_params(("arbitrary",) * 3 if rows_in_order else ("parallel", "parallel", "arbitrary")),
        name=name)(a, b, *extras)


def _tile(n, t):
    t = min(n, t)
    while n % t:
        t -= 8
    assert t > 0, (n, t)
    return t


def mm_nn(name, a, w, tm, tn, tk, epi, out_shapes, out_specs, extras=(), extra_specs=(), a_spec=None, a_fn=None):
    m = a.shape[0]
    tm, tn, tk = _tile(m, tm), _tile(w.n, tn), _tile(w.k, tk)
    grid = (m // tm, w.n // tn, w.k // tk)
    a_spec = a_spec or pl.BlockSpec((tm, tk), lambda i, j, k: (i, k))
    b_spec = w.spec(tk, tn, lambda i, j, k: (k, j))
    return _mm(name, NN, a, w.arr, grid, a_spec, b_spec, (tm, tn), extras, extra_specs, out_shapes, out_specs, epi, a_fn)


def mm_nt(name, a, w, m, tm, tn, tk, epi, out_shapes, out_specs, extras=(), extra_specs=(), a_spec=None,
          rows_in_order=False, a_fn=None):
    tm, tn, tk = _tile(m, tm), _tile(w.k, tn), _tile(w.n, tk)
    grid = (m // tm, w.k // tn, w.n // tk)
    a_spec = a_spec or pl.BlockSpec((tm, tk), lambda i, j, k: (i, k))
    b_spec = w.spec(tn, tk, lambda i, j, k: (j, k), resident=grid[1] == 1 and grid[2] == 1)
    return _mm(name, NT, a, w.arr, grid, a_spec, b_spec, (tm, tn), extras, extra_specs, out_shapes, out_specs, epi,
               a_fn=a_fn, rows_in_order=rows_in_order)


def mm_tn(name, a, b, dw, s, tm=512, tn=512, tk=4096, a_spec=None, b_spec=None, a_fn=None):
    tm, tn, tk = _tile(dw.k, tm), _tile(dw.n, tn), _tile(s, tk)
    grid = (dw.k // tm, dw.n // tn, s // tk)
    a_spec = a_spec or pl.BlockSpec((tk, tm), lambda i, j, k: (k, i))
    b_spec = b_spec or pl.BlockSpec((tk, tn), lambda i, j, k: (k, j))

    def epi(acc, e, o):
        o[0][...] = acc.astype(BF16)

    out = _mm(name, TN, a, b, grid, a_spec, b_spec, (tm, tn), (), (), [jax.ShapeDtypeStruct(dw.shape, BF16)],
              [dw.spec(tm, tn, lambda i, j, k: (i, j))], epi, a_fn)[0]
    return out.reshape(N_CHIPS, dw.k // N_CHIPS, dw.n) if dw.plain else out


def _sds(shape, dtype):
    return jax.ShapeDtypeStruct(shape, dtype)


def _ij(tm, tn):
    return pl.BlockSpec((tm, tn), lambda i, j, k: (i, j))


def _i0(tm, c):
    return pl.BlockSpec((tm, c), lambda i, j, k: (i, 0))


def _0j(r, tn):
    return pl.BlockSpec((r, tn), lambda i, j, k: (0, j))


def _layer_norm_rows(r, g, b):
    mu = jnp.mean(r, axis=-1, keepdims=True)
    d = r - mu
    var = jnp.mean(d * d, axis=-1, keepdims=True)
    rstd = lax.rsqrt(var + LN_EPS)
    xh = d * rstd
    return xh * g + b, xh, rstd


def mm_residual_ln(name, a, w, x, g, b, bias=None, tm=512, tk=1024, a_fn=None):
    s, d = x.shape
    tm = _tile(s, tm)
    extras = [x, g, b] + ([bias] if bias is not None else [])
    especs = [_i0(tm, d), _0j(1, d), _0j(1, d)] + ([_0j(1, d)] if bias is not None else [])

    def epi(acc, e, o):
        r = ALPHA * e[0][...] + acc
        if bias is not None:
            r = r + e[3][...]
        y, xh, rstd = _layer_norm_rows(r, e[1][...], e[2][...])
        o[0][...] = y
        o[1][...] = y.astype(BF16)
        o[2][...] = xh
        o[3][...] = rstd

    return mm_nn(name, a, w, tm, d, tk, epi,
                 [_sds((s, d), F32), _sds((s, d), BF16), _sds((s, d), F32), _sds((s, 1), F32)],
                 [_i0(tm, d), _i0(tm, d), _i0(tm, d), _i0(tm, 1)], extras, especs, a_fn=a_fn)


def mm_plain_nn(name, a, w, out_dtype, tm=1024, tn=512, tk=1024, bias=None):
    m = a.shape[0]
    tm, tn = _tile(m, tm), _tile(w.n, tn)

    def epi(acc, e, o):
        if bias is not None:
            acc = acc + e[0][...]
        o[0][...] = acc.astype(out_dtype)

    extras, especs = ([bias], [_0j(1, tn)]) if bias is not None else ((), ())
    return mm_nn(name, a, w, tm, tn, tk, epi, [_sds((m, w.n), out_dtype)], [_ij(tm, tn)], extras, especs)[0]


def mm_plain_nt(name, a, w, out_dtype, tm=1024, tn=512, tk=1024, add=None, add_scale=1.0, a_spec_fn=None, a_fn=None):
    m = a.shape[0] if a_spec_fn is None else a_spec_fn[0]
    tm, tn = _tile(m, tm), _tile(w.k, tn)
    tk = _tile(w.n, tk)

    def epi(acc, e, o):
        if add is not None:
            acc = acc + add_scale * e[0][...].astype(F32)
        o[0][...] = acc.astype(out_dtype)

    extras, especs = ([add], [_ij(tm, tn)]) if add is not None else ((), ())
    a_spec = None if a_spec_fn is None else a_spec_fn[1](tm, tk)
    return mm_nt(name, a, w, m, tm, tn, tk, epi, [_sds((m, w.k), out_dtype)], [_ij(tm, tn)], extras, especs,
                 a_spec=a_spec, a_fn=a_fn)[0]


def _rows(tm, c):
    return pl.BlockSpec((tm, c), lambda i: (i, 0))


def _fix(shape):
    nd = len(shape)
    return pl.BlockSpec(shape, lambda i: (0,) * nd)


def _accumulate(ref, val):
    @pl.when(pl.program_id(0) == 0)
    def _():
        ref[...] = jnp.zeros_like(ref)

    ref[...] += val


def _ln_backward_rows(dyv, xh, rstd, g, dr_ref, drb_ref, dg_ref, db_ref, ds_ref):
    dxh = dyv * g
    m1 = jnp.mean(dxh, axis=-1, keepdims=True)
    m2 = jnp.mean(dxh * xh, axis=-1, keepdims=True)
    dr = rstd * (dxh - m1 - xh * m2)
    dr_ref[...] = dr
    drb_ref[...] = dr.astype(BF16)
    _accumulate(dg_ref, jnp.sum(dyv * xh, axis=0, keepdims=True))
    _accumulate(db_ref, jnp.sum(dyv, axis=0, keepdims=True))
    _accumulate(ds_ref, jnp.sum(dr, axis=0, keepdims=True))


def mm_nt_ln_backward(name, a, w, add, xhat, rstd, g, tm=512, tk=1024, a_spec_fn=None, a_fn=None):
    m, d = add.shape
    tm, tk = _tile(m, tm), _tile(w.n, tk)

    def epi(acc, e, o):
        _ln_backward_rows(acc + ALPHA * e[0][...], e[1][...], e[2][...], e[3][...], *o)

    vec = pl.BlockSpec((1, d), lambda i, j, k: (0, 0))
    a_spec = None if a_spec_fn is None else a_spec_fn(tm, tk)
    return mm_nt(name, a, w, m, tm, d, tk, epi,
                 [_sds((m, d), F32), _sds((m, d), BF16), _sds((1, d), F32), _sds((1, d), F32), _sds((1, d), F32)],
                 [_i0(tm, d), _i0(tm, d), vec, vec, vec], [add, xhat, rstd, g],
                 [_i0(tm, d), _i0(tm, d), _i0(tm, 1), vec], a_spec=a_spec, rows_in_order=True, a_fn=a_fn)


def loss_ln_backward(y, target, xhat, rstd, g, tm=512):
    s, d = y.shape
    tm = _tile(s, tm)

    def body(y_ref, t_ref, xh_ref, rstd_ref, g_ref, dr_ref, drb_ref, dg_ref, db_ref, ds_ref, loss_ref):
        e = y_ref[...] - t_ref[...]
        part = 0.5 * jnp.sum(jnp.mean(e * e, axis=-1, keepdims=True), axis=0, keepdims=True)
        _accumulate(loss_ref, jnp.broadcast_to(part, (1, d)))
        _ln_backward_rows(e * (1.0 / d), xh_ref[...], rstd_ref[...], g_ref[...], dr_ref, drb_ref, dg_ref, db_ref, ds_ref)

    return pl.pallas_call(
        body, grid=(s // tm,),
        in_specs=[_rows(tm, d), _rows(tm, d), _rows(tm, d), _rows(tm, 1), _fix((1, d))],
        out_specs=[_rows(tm, d), _rows(tm, d)] + [_fix((1, d))] * 4,
        out_shape=[_sds((s, d), F32), _sds((s, d), BF16)] + [_sds((1, d), F32)] * 4,
        compiler_params=_params(("arbitrary",)), name="loss_ln_backward")(y, target, xhat, rstd, g)


def _cols(s, tc, off=0):
    return pl.BlockSpec((s, tc), lambda i: (0, i + off))


def _shift_down(z, sft, rows):
    return jnp.where(rows >= sft, pltpu.roll(z, sft, 0), 0.0)


def _shift_up(z, sft, rows, s):
    return jnp.where(rows < s - sft, pltpu.roll(z, (s - sft) % s, 0), 0.0)


def short_conv_gate(u, conv_w, tc=256):
    s, d3 = u.shape
    d = d3 // 3
    nb = d // tc

    def body(b_ref, c_ref, h_ref, w_ref, o_ref):
        rows = lax.broadcasted_iota(jnp.int32, (s, tc), 0)
        z = c_ref[...] * h_ref[...]
        cz = jnp.zeros((s, tc), F32)
        for k in range(SC_WIDTH):
            sft = SC_WIDTH - 1 - k
            cz = cz + w_ref[pl.ds(k, 1), :] * (_shift_down(z, sft, rows) if sft else z)
        o_ref[...] = (b_ref[...] * cz).astype(BF16)

    return pl.pallas_call(
        body, grid=(nb,),
        in_specs=[_cols(s, tc), _cols(s, tc, nb), _cols(s, tc, 2 * nb), _cols(SC_WIDTH, tc)],
        out_specs=_cols(s, tc), out_shape=_sds((s, d), BF16),
        compiler_params=_params(("parallel",)), name="short_conv_gate")(u, u, u, conv_w)


def short_conv_gate_bwd(u, conv_w, dg, tc=256):
    s, d3 = u.shape
    d = d3 // 3
    nb = d // tc

    def body(b_ref, c_ref, h_ref, w_ref, dg_ref, du_ref, dw_ref):
        rows = lax.broadcasted_iota(jnp.int32, (s, tc), 0)
        c, h, dgv = c_ref[...], h_ref[...], dg_ref[...]
        z = c * h
        dcz = dgv * b_ref[...]
        cz = jnp.zeros((s, tc), F32)
        dz = jnp.zeros((s, tc), F32)
        for k in range(SC_WIDTH):
            sft = SC_WIDTH - 1 - k
            zs = _shift_down(z, sft, rows) if sft else z
            wk = w_ref[pl.ds(k, 1), :]
            cz = cz + wk * zs
            dz = dz + wk * (_shift_up(dcz, sft, rows, s) if sft else dcz)
            dw_ref[pl.ds(k, 1), :] = jnp.sum(dcz * zs, axis=0, keepdims=True)
        du_ref[0] = (dgv * cz).astype(BF16)
        du_ref[1] = (dz * h).astype(BF16)
        du_ref[2] = (dz * c).astype(BF16)

    return pl.pallas_call(
        body, grid=(nb,),
        in_specs=[_cols(s, tc), _cols(s, tc, nb), _cols(s, tc, 2 * nb), _cols(SC_WIDTH, tc), _cols(s, tc)],
        out_specs=[pl.BlockSpec((3, s, tc), lambda i: (0, 0, i)), _cols(SC_WIDTH, tc)],
        out_shape=[_sds((3, s, d), BF16), _sds((SC_WIDTH, d), F32)],
        compiler_params=_params(("parallel",)), name="short_conv_gate_bwd")(u, u, u, conv_w, dg)


def _store_shifted_down(ref, z, rows):
    s, tc = z.shape
    for b in range(8):
        ref[b, pl.ds(0, CONV_PAD), :] = jnp.zeros((CONV_PAD, tc), F32)
        ref[b, pl.ds(CONV_PAD, s), :] = z if b == 0 else _shift_down(z, b, rows)


def _store_shifted_up(ref, z, rows):
    s, tc = z.shape
    for b in range(8):
        ref[b, pl.ds(0, s), :] = z if b == 0 else _shift_up(z, b, rows, s)
        ref[b, pl.ds(s, CONV_PAD), :] = jnp.zeros((CONV_PAD, tc), F32)


def conformer_glu_conv(u, dw_w, dw_b, tc=128):
    s, d2 = u.shape
    d = d2 // 2
    nb = d // tc

    ch = min(CONV_CHUNK, s)

    def body(a_ref, g_ref, w_ref, b_ref, o_ref, down):
        rows = lax.broadcasted_iota(jnp.int32, (s, tc), 0)
        _store_shifted_down(down, a_ref[...] * jax.nn.sigmoid(g_ref[...]), rows)

        def chunk(ci, carry):
            r0 = pl.multiple_of(ci * ch, ch)
            acc = jnp.broadcast_to(b_ref[...], (ch, tc))
            for k in range(CONF_WIDTH):
                sft = CONF_WIDTH - 1 - k
                acc = acc + w_ref[pl.ds(k, 1), :] * down[sft % 8, pl.ds(CONV_PAD + r0 - (sft // 8) * 8, ch), :]
            o_ref[pl.ds(r0, ch), :] = acc
            return carry

        lax.fori_loop(0, s // ch, chunk, 0)

    return pl.pallas_call(
        body, grid=(nb,),
        in_specs=[_cols(s, tc), _cols(s, tc, nb), _cols(CONF_WIDTH, tc), _cols(1, tc)],
        out_specs=_cols(s, tc), out_shape=_sds((s, d), F32),
        scratch_shapes=[pltpu.VMEM((8, CONV_PAD + s, tc), F32)],
        compiler_params=_params(("parallel",)), name="conformer_glu_conv")(u, u, dw_w, dw_b)


def conformer_glu_conv_bwd(u, dw_w, dhc, tc=128):
    s, d2 = u.shape
    d = d2 // 2
    nb = d // tc
    ch = min(CONV_CHUNK, s)

    def body(a_ref, g_ref, w_ref, dhc_ref, du_ref, dbias_ref, dw_ref, db_ref, down, up, dw_acc, dh_buf):
        rows = lax.broadcasted_iota(jnp.int32, (s, tc), 0)
        a = a_ref[...]
        sg = jax.nn.sigmoid(g_ref[...])
        dhcv = dhc_ref[...]
        _store_shifted_down(down, a * sg, rows)
        _store_shifted_up(up, dhcv, rows)
        dw_acc[...] = jnp.zeros_like(dw_acc)

        def chunk(ci, carry):
            r0 = pl.multiple_of(ci * ch, ch)
            dc = dhc_ref[pl.ds(r0, ch), :]
            dh = jnp.zeros((ch, tc), F32)
            for k in range(CONF_WIDTH):
                sft = CONF_WIDTH - 1 - k
                a8, b = (sft // 8) * 8, sft % 8
                dh = dh + w_ref[pl.ds(k, 1), :] * up[b, pl.ds(r0 + a8, ch), :]
                prod = dc * down[b, pl.ds(CONV_PAD + r0 - a8, ch), :]
                dw_acc[k] += jnp.sum(prod.reshape(ch // 8, 8, tc), axis=0)
            dh_buf[pl.ds(r0, ch), :] = dh
            return carry

        lax.fori_loop(0, s // ch, chunk, 0)
        dh = dh_buf[...]
        da = dh * sg
        dgate = dh * a * sg * (1.0 - sg)
        du_ref[0] = da.astype(BF16)
        du_ref[1] = dgate.astype(BF16)
        dbias_ref[pl.ds(0, 1), :] = jnp.sum(da, axis=0, keepdims=True)
        dbias_ref[pl.ds(1, 1), :] = jnp.sum(dgate, axis=0, keepdims=True)
        db_ref[...] = jnp.sum(dhcv, axis=0, keepdims=True)
        for k in range(CONF_WIDTH):
            dw_ref[pl.ds(k, 1), :] = jnp.sum(dw_acc[k], axis=0, keepdims=True)

    return pl.pallas_call(
        body, grid=(nb,),
        in_specs=[_cols(s, tc), _cols(s, tc, nb), _cols(CONF_WIDTH, tc), _cols(s, tc)],
        out_specs=[pl.BlockSpec((2, s, tc), lambda i: (0, 0, i)), _cols(2, tc), _cols(CONF_WIDTH, tc), _cols(1, tc)],
        out_shape=[_sds((2, s, d), BF16), _sds((2, d), F32), _sds((CONF_WIDTH, d), F32), _sds((1, d), F32)],
        scratch_shapes=[pltpu.VMEM((8, CONV_PAD + s, tc), F32), pltpu.VMEM((8, CONV_PAD + s, tc), F32),
                        pltpu.VMEM((CONF_WIDTH + 1, 8, tc), F32), pltpu.VMEM((s, tc), F32)],
        compiler_params=_params(("parallel",)), name="conformer_glu_conv_bwd")(u, u, dw_w, dhc)


def conformer_norm_swish(hc, g, b, tm=512):
    s, d = hc.shape
    tm = _tile(s, tm)

    def body(h_ref, g_ref, b_ref, o_ref):
        n, _, _ = _layer_norm_rows(h_ref[...], g_ref[...], b_ref[...])
        o_ref[...] = (n * jax.nn.sigmoid(n)).astype(BF16)

    return pl.pallas_call(
        body, grid=(s // tm,), in_specs=[_rows(tm, d), _fix((1, d)), _fix((1, d))], out_specs=_rows(tm, d),
        out_shape=_sds((s, d), BF16), compiler_params=_params(("parallel",)), name="conformer_norm_swish")(hc, g, b)


def conformer_norm_swish_bwd(hc, g, b, ds, tm=512):
    s, d = hc.shape
    tm = _tile(s, tm)

    def body(h_ref, g_ref, b_ref, ds_ref, dh_ref, dg_ref, db_ref):
        n, nh, rstd = _layer_norm_rows(h_ref[...], g_ref[...], b_ref[...])
        sg = jax.nn.sigmoid(n)
        dn = ds_ref[...] * (sg * (1.0 + n * (1.0 - sg)))
        dnh = dn * g_ref[...]
        m1 = jnp.mean(dnh, axis=-1, keepdims=True)
        m2 = jnp.mean(dnh * nh, axis=-1, keepdims=True)
        dh_ref[...] = rstd * (dnh - m1 - nh * m2)
        _accumulate(dg_ref, jnp.sum(dn * nh, axis=0, keepdims=True))
        _accumulate(db_ref, jnp.sum(dn, axis=0, keepdims=True))

    return pl.pallas_call(
        body, grid=(s // tm,), in_specs=[_rows(tm, d), _fix((1, d)), _fix((1, d)), _rows(tm, d)],
        out_specs=[_rows(tm, d), _fix((1, d)), _fix((1, d))],
        out_shape=[_sds((s, d), F32), _sds((1, d), F32), _sds((1, d), F32)],
        compiler_params=_params(("arbitrary",)), name="conformer_norm_swish_bwd")(hc, g, b, ds)


def _swap_halves(x):
    lane = lax.broadcasted_iota(jnp.int32, x.shape, 1)
    return jnp.where(lane < QK_ROPE // 2, pltpu.roll(x, 128 - QK_ROPE // 2, 1), pltpu.roll(x, QK_ROPE // 2, 1))


def _rope(x, cf, sf):
    return x * cf + _swap_halves(x) * sf


def _unrope(dx, cf, sf):
    return dx * cf - _swap_halves(dx) * sf


def _rms_rows(x, g):
    r = lax.rsqrt(jnp.mean(x * x, axis=-1, keepdims=True) + RMS_EPS)
    return x * r, r


def mla_latents(t, g_q, g_kv, cf, sf, tm=512):
    s = t.shape[0]
    tm = _tile(s, tm)

    def body(t_ref, gq_ref, gkv_ref, cf_ref, sf_ref, cq_ref, ckv_ref, kpe_ref):
        xq, _ = _rms_rows(t_ref[:, 0:Q_LORA], gq_ref[...])
        cq_ref[...] = (xq * gq_ref[...]).astype(BF16)
        xkv, _ = _rms_rows(t_ref[:, Q_LORA:Q_LORA + KV_LORA], gkv_ref[...])
        ckv_ref[...] = (xkv * gkv_ref[...]).astype(BF16)
        kpe_ref[...] = _rope(t_ref[:, Q_LORA + KV_LORA:], cf_ref[...], sf_ref[...]).astype(BF16)

    w = Q_LORA + KV_LORA + 128
    return pl.pallas_call(
        body, grid=(s // tm,),
        in_specs=[_rows(tm, w), _fix((1, Q_LORA)), _fix((1, KV_LORA)), _rows(tm, 128), _rows(tm, 128)],
        out_specs=[_rows(tm, Q_LORA), _rows(tm, KV_LORA), _rows(tm, 128)],
        out_shape=[_sds((s, Q_LORA), BF16), _sds((s, KV_LORA), BF16), _sds((s, 128), BF16)],
        compiler_params=_params(("parallel",)), name="mla_latents")(t, g_q, g_kv, cf, sf)


def mla_latents_bwd(t, g_q, g_kv, cf, sf, dcq, dckv, dkpe, tm=512):
    s = t.shape[0]
    tm = _tile(s, tm)
    w = Q_LORA + KV_LORA + 128

    def rms_bwd(x, g, dy):
        xh, r = _rms_rows(x, g)
        dxh = dy * g
        return r * (dxh - xh * jnp.mean(dxh * xh, axis=-1, keepdims=True)), jnp.sum(dy * xh, axis=0, keepdims=True)

    def body(t_ref, gq_ref, gkv_ref, cf_ref, sf_ref, dcq_ref, dckv_ref, dkpe_ref, dt_ref, dgq_ref, dgkv_ref):
        dxq, dgq = rms_bwd(t_ref[:, 0:Q_LORA], gq_ref[...], dcq_ref[...])
        dxkv, dgkv = rms_bwd(t_ref[:, Q_LORA:Q_LORA + KV_LORA], gkv_ref[...], dckv_ref[...])
        dt_ref[:, 0:Q_LORA] = dxq.astype(BF16)
        dt_ref[:, Q_LORA:Q_LORA + KV_LORA] = dxkv.astype(BF16)
        dt_ref[:, Q_LORA + KV_LORA:] = _unrope(dkpe_ref[...], cf_ref[...], sf_ref[...]).astype(BF16)
        _accumulate(dgq_ref, dgq)
        _accumulate(dgkv_ref, dgkv)

    return pl.pallas_call(
        body, grid=(s // tm,),
        in_specs=[_rows(tm, w), _fix((1, Q_LORA)), _fix((1, KV_LORA)), _rows(tm, 128), _rows(tm, 128),
                  _rows(tm, Q_LORA), _rows(tm, KV_LORA), _rows(tm, 128)],
        out_specs=[_rows(tm, w), _fix((1, Q_LORA)), _fix((1, KV_LORA))],
        out_shape=[_sds((s, w), BF16), _sds((1, Q_LORA), F32), _sds((1, KV_LORA), F32)],
        compiler_params=_params(("arbitrary",)), name="mla_latents_bwd")(t, g_q, g_kv, cf, sf, dcq, dckv, dkpe)


def mla_queries(cq, w_uq, cf, sf, tm=2048):
    s = cq.shape[0]
    tm = _tile(s, tm)

    def epi(acc, e, o):
        o[0][:, 0:QK_NOPE] = acc[:, 0:QK_NOPE].astype(BF16)
        o[0][:, QK_NOPE:] = _rope(acc[:, QK_NOPE:], e[0][...], e[1][...]).astype(BF16)

    return mm_nn("mla_queries", cq, w_uq, tm, HEAD_PAD, Q_LORA, epi, [_sds((s, N_HEADS * HEAD_PAD), BF16)],
                 [_ij(tm, HEAD_PAD)], [cf, sf], [_i0(tm, 128), _i0(tm, 128)])[0]


def mla_keys(ckv, w_uk, kpe, tm=2048):
    s = ckv.shape[0]
    tm = _tile(s, tm)

    def epi(acc, e, o):
        o[0][:, 0:QK_NOPE] = acc.astype(BF16)
        o[0][:, QK_NOPE:] = e[0][...]

    return mm_nn("mla_keys", ckv, w_uk, tm, QK_NOPE, KV_LORA, epi, [_sds((s, N_HEADS * HEAD_PAD), BF16)],
                 [_ij(tm, HEAD_PAD)], [kpe], [_i0(tm, 128)])[0]


def _masked_scores(q, k, tq, kv):
    sc = lax.dot_general(q, k, NT, preferred_element_type=F32) * ATTN_SCALE
    row = lax.broadcasted_iota(jnp.int32, (tq, tq), 0)
    col = lax.broadcasted_iota(jnp.int32, (tq, tq), 1)
    ok = lax.shift_right_logical(col, CHUNK_SHIFT) <= lax.shift_right_logical(row, CHUNK_SHIFT)
    own = jnp.where(ok, sc[:, kv - tq:], -1e30)
    return own if kv == tq else jnp.concatenate([sc[:, :kv - tq], own], axis=1)


def attention(q, k, v, tq=512):
    s = q.shape[0]
    tq = _tile(s, tq)
    nq = s // tq

    def body(q_ref, k_ref, v_ref, o_ref):
        for qi in range(nq):
            kv = (qi + 1) * tq
            sc = _masked_scores(q_ref[pl.ds(qi * tq, tq), :], k_ref[pl.ds(0, kv), :], tq, kv)
            p = jnp.exp(sc - jnp.max(sc, axis=-1, keepdims=True))
            o = lax.dot_general(p.astype(BF16), v_ref[pl.ds(0, kv), :], NN, preferred_element_type=F32)
            o_ref[pl.ds(qi * tq, tq), :] = (o / jnp.sum(p, axis=-1, keepdims=True)).astype(BF16)

    hq = pl.BlockSpec((s, HEAD_PAD), lambda h: (0, h))
    hv = pl.BlockSpec((s, V_HEAD), lambda h: (0, h))
    return pl.pallas_call(
        body, grid=(N_HEADS,), in_specs=[hq, hq, hv], out_specs=hv, out_shape=_sds((s, N_HEADS * V_HEAD), BF16),
        compiler_params=_params(("parallel",)), name="attention")(q, k, v)


def attention_bwd(q, k, v, do, tq=512):
    s = q.shape[0]
    tq = _tile(s, tq)
    nq = s // tq

    def body(q_ref, k_ref, v_ref, do_ref, dq_ref, dk_ref, dv_ref, dk_acc, dv_acc):
        dk_acc[...] = jnp.zeros_like(dk_acc)
        dv_acc[...] = jnp.zeros_like(dv_acc)
        for qi in range(nq):
            kv = (qi + 1) * tq
            qt = q_ref[pl.ds(qi * tq, tq), :]
            kt = k_ref[pl.ds(0, kv), :]
            dot = do_ref[pl.ds(qi * tq, tq), :]
            sc = _masked_scores(qt, kt, tq, kv)
            p = jnp.exp(sc - jnp.max(sc, axis=-1, keepdims=True))
            p = p / jnp.sum(p, axis=-1, keepdims=True)
            dp = lax.dot_general(dot, v_ref[pl.ds(0, kv), :], NT, preferred_element_type=F32)
            delta = jnp.sum(p * dp, axis=-1, keepdims=True)
            ds = (p * (dp - delta) * ATTN_SCALE).astype(BF16)
            dq_ref[pl.ds(qi * tq, tq), :] = lax.dot_general(ds, kt, NN, preferred_element_type=F32).astype(BF16)
            dk_acc[pl.ds(0, kv), :] += lax.dot_general(ds, qt, TN, preferred_element_type=F32)
            dv_acc[pl.ds(0, kv), :] += lax.dot_general(p.astype(BF16), dot, TN, preferred_element_type=F32)
        dk_ref[...] = dk_acc[...].astype(BF16)
        dv_ref[...] = dv_acc[...].astype(BF16)

    hq = pl.BlockSpec((s, HEAD_PAD), lambda h: (0, h))
    hv = pl.BlockSpec((s, V_HEAD), lambda h: (0, h))
    return pl.pallas_call(
        body, grid=(N_HEADS,), in_specs=[hq, hq, hv, hv], out_specs=[hq, hq, hv],
        out_shape=[_sds((s, N_HEADS * HEAD_PAD), BF16), _sds((s, N_HEADS * HEAD_PAD), BF16),
                   _sds((s, N_HEADS * V_HEAD), BF16)],
        scratch_shapes=[pltpu.VMEM((s, HEAD_PAD), F32), pltpu.VMEM((s, V_HEAD), F32)],
        compiler_params=_params(("parallel",)), name="attention_bwd")(q, k, v, do)


def mla_unrope_grads(dq, dk, cf, sf, tm=512):
    s = dq.shape[0]
    tm = _tile(s, tm)

    def body(dq_ref, dk_ref, cf_ref, sf_ref, dql_ref, dkn_ref, dkpe_ref):
        cfv, sfv = cf_ref[...], sf_ref[...]
        dkpe = jnp.zeros((tm, 128), F32)
        for h in range(N_HEADS):
            lo = h * HEAD_PAD
            dql_ref[:, lo:lo + QK_NOPE] = dq_ref[:, lo:lo + QK_NOPE]
            dql_ref[:, lo + QK_NOPE:lo + HEAD_PAD] = _unrope(
                dq_ref[:, lo + QK_NOPE:lo + HEAD_PAD].astype(F32), cfv, sfv).astype(BF16)
            dkn_ref[:, h * QK_NOPE:(h + 1) * QK_NOPE] = dk_ref[:, lo:lo + QK_NOPE]
            dkpe = dkpe + dk_ref[:, lo + QK_NOPE:lo + HEAD_PAD].astype(F32)
        dkpe_ref[...] = dkpe

    wq = N_HEADS * HEAD_PAD
    return pl.pallas_call(
        body, grid=(s // tm,), in_specs=[_rows(tm, wq), _rows(tm, wq), _rows(tm, 128), _rows(tm, 128)],
        out_specs=[_rows(tm, wq), _rows(tm, N_HEADS * QK_NOPE), _rows(tm, 128)],
        out_shape=[_sds((s, wq), BF16), _sds((s, N_HEADS * QK_NOPE), BF16), _sds((s, 128), F32)],
        compiler_params=_params(("parallel",)), name="mla_unrope_grads")(dq, dk, cf, sf)


ANY = pl.BlockSpec(memory_space=pl.ANY)
GATHER_ID = 1
CHIP_EXCHANGE_ID = 2
PAIR_ID = 3
ALL_ID = 4


def _nbytes(a):
    return a.size * a.dtype.itemsize


def _copy_cost(operand_bytes, sent_fraction):
    sent = int(operand_bytes * sent_fraction)
    return pl.CostEstimate(flops=0, transcendentals=0, bytes_accessed=2 * sent, remote_bytes_transferred=sent)


def _handshake(peers):
    barrier = pltpu.get_barrier_semaphore()
    for peer in peers:
        pl.semaphore_signal(barrier, inc=1, device_id=peer, device_id_type=MESH)
    pl.semaphore_wait(barrier, len(peers))


def _place():
    x, y, c = lax.axis_index("x"), lax.axis_index("y"), lax.axis_index("c")
    chips = [(1 - x, y), (x, 1 - y), (1 - x, 1 - y)]
    return x, y, c, chips


def _half(ref, hc, axis=0):
    n = ref.shape[axis] // 2
    idx = (slice(None),) * axis + (pl.ds(hc * n, n),)
    return ref.at[idx]


def gather_shards(name, tensors, by_columns=()):
    nt = len(tensors)

    def body(*refs):
        a, g = refs[:nt], refs[nt:2 * nt]
        send, recv = refs[2 * nt:]
        x, y, c, _ = _place()
        q = 2 * x + y
        sib, xn, yn = (x, y, 1 - c), (1 - x, y, c), (x, 1 - y, c)
        q_xn, q_yn, q_diag = 2 * (1 - x) + y, 2 * x + 1 - y, 2 * (1 - x) + 1 - y
        _handshake([sib, xn, yn])

        def whole(t, p):
            if t in by_columns:
                n = a[t].shape[1]
                return g[t].at[:, pl.ds(p * n, n)]
            return g[t].at[p]

        def part(t, p, hc, quarter=None):
            rows = a[t].shape[0]
            if quarter is None:
                return whole(t, p).at[pl.ds(hc * (rows // 2), rows // 2)]
            return whole(t, p).at[pl.ds(hc * (rows // 2) + quarter * (rows // 4), rows // 4)]

        def rc(t, k, src, dst, to):
            return pltpu.make_async_remote_copy(src_ref=src, dst_ref=dst, send_sem=send.at[t, k], recv_sem=recv.at[t, k],
                                                device_id=to, device_id_type=MESH)

        sent = []

        def go(cp):
            cp.start()
            sent.append(cp)

        def landed(t, k, piece, frm):
            rc(t, k, piece, piece, frm).wait_recv()
            return piece

        for t in range(nt):
            go(rc(t, 8, a[t], whole(t, q), sib))
            mine = _half(a[t], c)
            go(rc(t, 0, mine, part(t, q, c), xn))
            go(rc(t, 1, mine, part(t, q, c), yn))
        for t in range(nt):
            from_y = landed(t, 1, part(t, q_yn, c), yn)
            go(rc(t, 2, part(t, q_yn, c, 0), part(t, q_yn, c, 0), xn))
            go(rc(t, 5, from_y, from_y, sib))
            from_x = landed(t, 0, part(t, q_xn, c), xn)
            go(rc(t, 3, part(t, q_xn, c, 1), part(t, q_xn, c, 1), yn))
            go(rc(t, 4, from_x, from_x, sib))
        for t in range(nt):
            for k, frm in ((2, xn), (3, yn)):
                piece = landed(t, k, part(t, q_diag, c, k - 2), frm)
                go(rc(t, 4 + k, piece, piece, sib))
        for t in range(nt):
            landed(t, 4, part(t, q_xn, 1 - c), sib)
            landed(t, 5, part(t, q_yn, 1 - c), sib)
            landed(t, 6, part(t, q_diag, 1 - c, 0), sib)
            landed(t, 7, part(t, q_diag, 1 - c, 1), sib)
            landed(t, 8, whole(t, q), sib)
        for cp in sent:
            cp.wait_send()

    return pl.kernel(
        body, name=name,
        out_type=[_sds((a.shape[0], N_CHIPS * a.shape[1]) if t in by_columns else (N_CHIPS,) + a.shape, a.dtype)
                  for t, a in enumerate(tensors)],
        mesh=plsc.ScalarSubcoreMesh(axis_name="sequencer", num_cores=1),
        scratch_types=[pltpu.SemaphoreType.DMA((nt, 9)), pltpu.SemaphoreType.DMA((nt, 9))],
        cost_estimate=_copy_cost(sum(_nbytes(a) for a in tensors), 4),
        compiler_params=pltpu.CompilerParams(collective_id=GATHER_ID))(*tensors)


def pair_exchange(name, grads, on_sequencer):
    nt = len(grads)

    def body(*refs):
        g, theirs = refs[:nt], refs[nt:2 * nt]
        send, recv = refs[2 * nt:]
        x, y, c, _ = _place()
        if on_sequencer:
            _handshake([(x, y, 1 - c)])
        cps = []
        for t in range(nt):
            cp = pltpu.make_async_remote_copy(src_ref=_half(g[t], 1 - c, 1), dst_ref=theirs[t], send_sem=send.at[t],
                                              recv_sem=recv.at[t], device_id=(x, y, 1 - c), device_id_type=MESH)
            cp.start()
            cps.append(cp)
        for cp in cps:
            cp.wait()

    if not on_sequencer:
        return pl.pallas_call(
            body, in_specs=[ANY] * nt, out_specs=[ANY] * nt,
            out_shape=[_sds((N_CHIPS, a.shape[1] // 2, a.shape[2]), a.dtype) for a in grads],
            scratch_shapes=[pltpu.SemaphoreType.DMA((nt,)), pltpu.SemaphoreType.DMA((nt,))],
            name=name)(*grads)
    return pl.kernel(
        body, name=name, out_type=[_sds((N_CHIPS, a.shape[1] // 2, a.shape[2]), a.dtype) for a in grads],
        mesh=plsc.ScalarSubcoreMesh(axis_name="sequencer", num_cores=1),
        scratch_types=[pltpu.SemaphoreType.DMA((nt,)), pltpu.SemaphoreType.DMA((nt,))],
        cost_estimate=_copy_cost(sum(_nbytes(a) for a in grads), 0.5),
        compiler_params=pltpu.CompilerParams(collective_id=PAIR_ID))(*grads)


def chip_exchange(name, parts):
    nt = len(parts)

    def body(*refs):
        a, r = refs[:nt], refs[nt:2 * nt]
        send, recv = refs[2 * nt:]
        x, y, c, chips = _place()
        _handshake([(*chip, c) for chip in chips])
        cps = []
        for t in range(nt):
            for j, chip in enumerate(chips):
                cp = pltpu.make_async_remote_copy(
                    src_ref=a[t].at[2 * chip[0] + chip[1]], dst_ref=r[t].at[j], send_sem=send.at[t, j],
                    recv_sem=recv.at[t, j], device_id=(*chip, c), device_id_type=MESH)
                cp.start()
                cps.append(cp)
        for cp in cps:
            cp.wait()

    return pl.kernel(
        body, name=name, out_type=[_sds((N_CHIPS - 1,) + a.shape[1:], a.dtype) for a in parts],
        mesh=plsc.ScalarSubcoreMesh(axis_name="sequencer", num_cores=1),
        scratch_types=[pltpu.SemaphoreType.DMA((nt, 3)), pltpu.SemaphoreType.DMA((nt, 3))],
        cost_estimate=_copy_cost(sum(_nbytes(a) for a in parts), 0.75),
        compiler_params=pltpu.CompilerParams(collective_id=CHIP_EXCHANGE_ID))(*parts)


def pair_share(name, halves):
    nt = len(halves)

    def body(*refs):
        h, other = refs[:nt], refs[nt:2 * nt]
        send, recv = refs[2 * nt:]
        x, y, c, _ = _place()
        _handshake([(x, y, 1 - c)])
        cps = []
        for t in range(nt):
            cp = pltpu.make_async_remote_copy(src_ref=h[t], dst_ref=other[t], send_sem=send.at[t], recv_sem=recv.at[t],
                                              device_id=(x, y, 1 - c), device_id_type=MESH)
            cp.start()
            cps.append(cp)
        for cp in cps:
            cp.wait()

    return pl.kernel(
        body, name=name, out_type=[_sds(a.shape, a.dtype) for a in halves],
        mesh=plsc.ScalarSubcoreMesh(axis_name="sequencer", num_cores=1),
        scratch_types=[pltpu.SemaphoreType.DMA((nt,)), pltpu.SemaphoreType.DMA((nt,))],
        cost_estimate=_copy_cost(sum(_nbytes(a) for a in halves), 1),
        compiler_params=pltpu.CompilerParams(collective_id=PAIR_ID))(*halves)


def pack_rows(name, parts, rows):
    cdim = parts[0].shape[1]
    n = len(parts)
    vm = pl.BlockSpec(memory_space=pltpu.VMEM)

    def pack(*refs):
        p, o_ref = refs[:n], refs[n]
        at = 0
        for ref in p:
            o_ref[pl.ds(at, ref.shape[0]), :] = ref[...]
            at += ref.shape[0]
        o_ref[pl.ds(at, rows - at), :] = jnp.zeros((rows - at, cdim), F32)

    return pl.pallas_call(pack, in_specs=[vm] * n, out_specs=vm, out_shape=_sds((rows, cdim), F32), name=name)(*parts)


def all_reduce_small(parts, rows):
    cdim = parts[0].shape[1]
    vm = pl.BlockSpec(memory_space=pltpu.VMEM)
    mine = pack_rows("small_pack", parts, rows)

    def exchange(mine_ref, buf, send, recv, lsem):
        x, y, c, _ = _place()
        me = 4 * x + 2 * y + c
        peers = [(x ^ (k >> 2), y ^ ((k >> 1) & 1), c ^ (k & 1)) for k in range(1, 8)]
        _handshake(peers)
        own = pltpu.make_async_copy(mine_ref, buf.at[me], lsem)
        own.start()
        cps = []
        for k, to in enumerate(peers):
            cp = pltpu.make_async_remote_copy(src_ref=mine_ref, dst_ref=buf.at[me], send_sem=send.at[k], recv_sem=recv.at[k],
                                              device_id=to, device_id_type=MESH)
            cp.start()
            cps.append(cp)
        for k, (px, py, pc) in enumerate(peers):
            pltpu.make_async_remote_copy(src_ref=mine_ref, dst_ref=buf.at[4 * px + 2 * py + pc], send_sem=send.at[k],
                                         recv_sem=recv.at[k], device_id=(x, y, c), device_id_type=MESH).wait_recv()
        for cp in cps:
            cp.wait_send()
        own.wait()

    landed = pl.kernel(
        exchange, name="small_exchange", out_type=_sds((8, rows, cdim), F32),
        mesh=plsc.ScalarSubcoreMesh(axis_name="sequencer", num_cores=1),
        scratch_types=[pltpu.SemaphoreType.DMA((7,)), pltpu.SemaphoreType.DMA((7,)), pltpu.SemaphoreType.DMA],
        cost_estimate=_copy_cost(rows * cdim * 4, 7),
        compiler_params=pltpu.CompilerParams(collective_id=ALL_ID))(mine)

    def total(buf, o_ref):
        acc = buf[0]
        for d in range(1, 8):
            acc = acc + buf[d]
        o_ref[...] = acc

    return pl.pallas_call(total, in_specs=[vm], out_specs=vm, out_shape=_sds((rows, cdim), F32), name="small_sum")(landed)


def pair_sum(gs, theirs, core, tm=256):
    n = len(gs)
    _, r, c = gs[0].shape
    tm = _tile(r // 2, tm)
    nh = r // 2 // tm

    def body(core_ref, *refs):
        for a_ref, b_ref, o_ref in zip(refs[:n], refs[n:2 * n], refs[2 * n:]):
            o_ref[...] = (a_ref[...].astype(F32) + b_ref[...].astype(F32)).astype(BF16)

    blk = (N_CHIPS, tm, c)
    own = pl.BlockSpec(blk, lambda i, cr: (0, cr[0] * nh + i, 0))
    half = pl.BlockSpec(blk, lambda i, cr: (0, i, 0))
    return pl.pallas_call(
        body, grid_spec=pltpu.PrefetchScalarGridSpec(
            num_scalar_prefetch=1, grid=(nh,), in_specs=[own] * n + [half] * n, out_specs=[half] * n),
        out_shape=[_sds(t.shape, BF16) for t in theirs], compiler_params=_params(("parallel",)),
        name="pair_sum")(core, *gs, *theirs)


def chip_sum(own, landed, chip, stack, layer, layers, tm=256):
    _, r, c = own.shape
    tm = _tile(r, tm)

    def body(chip_ref, own_ref, l_ref, *rest):
        acc = own_ref[...].astype(F32)
        for j in range(N_CHIPS - 1):
            acc = acc + l_ref[j].astype(F32)
        rest[-1][...] = acc

    in_specs = [pl.BlockSpec((None, tm, c), lambda i, qr: (qr[0], i, 0)),
                pl.BlockSpec((N_CHIPS - 1, tm, c), lambda i, qr: (0, i, 0))]
    args = [chip, own, landed]
    if stack is not None:
        in_specs.append(ANY)
        args.append(stack)
    return pl.pallas_call(
        body, grid_spec=pltpu.PrefetchScalarGridSpec(
            num_scalar_prefetch=1, grid=(r // tm,), in_specs=in_specs,
            out_specs=pl.BlockSpec((None, tm, c), lambda i, qr: (layer, i, 0))),
        out_shape=_sds((layers, r, c), F32), input_output_aliases={3: 0} if stack is not None else {},
        compiler_params=_params(("parallel",)), name="chip_sum")(*args)


def _adamw_math(w, g, m, v):
    bc1 = 1.0 - ADAM_B1 ** ADAM_STEP
    bc2 = 1.0 - ADAM_B2 ** ADAM_STEP
    nm = ADAM_B1 * m + (1.0 - ADAM_B1) * g
    nv = ADAM_B2 * v + (1.0 - ADAM_B2) * (g * g)
    return -ADAM_LR * ((nm / bc1) / (jnp.sqrt(nv / bc2) + ADAM_EPS) + ADAM_WD * w), nm, nv


def vector_update(red, chip, ws, ms, vs, where):
    n = len(ws)
    dd = red.shape[1]

    def body(chip_ref, red_ref, *refs):
        w_r, m_r, v_r = refs[0:n], refs[n:2 * n], refs[2 * n:3 * n]
        g_o, d_o, m_o, v_o = (refs[(3 + k) * n:(4 + k) * n] for k in range(4))
        q = chip_ref[0]

        def chip_block(val, width):
            out = val[:, 0:width]
            for p in range(1, val.shape[1] // width):
                out = jnp.where(q == p, val[:, p * width:(p + 1) * width], out)
            return out

        for k in range(n):
            for idx, r0, nr, cols in where[k]:
                width = w_r[k].shape[-1]
                if cols == "chip" and width * N_CHIPS != dd:
                    g = chip_block(jnp.concatenate([red_ref[pl.ds(r0 + j, 1), :] for j in range(nr)], axis=1), width)
                else:
                    g = red_ref[pl.ds(r0, nr), :]
                    g = chip_block(g, width) if cols == "chip" else g if cols == "all" else g[:, 0:cols]
                delta, nm, nv = _adamw_math(w_r[k][idx], g, m_r[k][idx], v_r[k][idx])
                g_o[k][idx] = g
                d_o[k][idx] = delta
                m_o[k][idx] = nm
                v_o[k][idx] = nv

    vm = pl.BlockSpec(memory_space=pltpu.VMEM)
    outs = pl.pallas_call(
        body, in_specs=[pl.BlockSpec(memory_space=pltpu.SMEM), vm] + [vm] * (3 * n), out_specs=[vm] * (4 * n),
        out_shape=[_sds(w.shape, F32) for w in ws] * 4, name="vector_update")(chip, red, *ws, *ms, *vs)
    return [outs[k * n:(k + 1) * n] for k in range(4)]


def adamw_joined(w, m, v, g_mine, g_theirs, core, tm=512):
    nl, r, c = w.shape
    tm = _tile(r // 2, tm)
    nh = r // 2 // tm

    def body(core_ref, w_ref, m_ref, v_ref, gm_ref, gt_ref, g_ref, d_ref, nm_ref, nv_ref):
        mine = (pl.program_id(1) // nh) == core_ref[0]
        gv = jnp.where(mine, gm_ref[...], gt_ref[...])
        g_ref[...] = gv
        d_ref[...], nm_ref[...], nv_ref[...] = _adamw_math(w_ref[...], gv, m_ref[...], v_ref[...])

    full = pl.BlockSpec((None, tm, c), lambda l, i, cr: (l, i, 0))
    mine = pl.BlockSpec((None, tm, c), lambda l, i, cr: (l, jnp.where(i // nh == cr[0], i % nh, 0), 0))
    theirs = pl.BlockSpec((None, tm, c), lambda l, i, cr: (l, jnp.where(i // nh == cr[0], 0, i % nh), 0))
    return pl.pallas_call(
        body, grid_spec=pltpu.PrefetchScalarGridSpec(
            num_scalar_prefetch=1, grid=(nl, r // tm), in_specs=[full, full, full, mine, theirs], out_specs=[full] * 4),
        out_shape=[_sds((nl, r, c), F32)] * 4, compiler_params=_params(("parallel", "parallel")),
        name="adamw_joined")(core, w, m, v, g_mine, g_theirs)


WEIGHTS = ['sc_w_in', 'sc_conv_w', 'sc_w_out', 'mla_w_dq', 'mla_g_q', 'mla_w_uq', 'mla_w_dkv', 'mla_g_kv', 'mla_w_uk',
           'mla_w_uv', 'mla_w_o', 'cf_w_pw1', 'cf_b_pw1', 'cf_dw_w', 'cf_dw_b', 'cf_norm_g', 'cf_norm_b', 'cf_w_pw2',
           'cf_b_pw2', 'ff_w1', 'ff_w2', 'ln_mix_g', 'ln_mix_b', 'ln_ff_g', 'ln_ff_b']
ARGS = ['x'] + WEIGHTS + ['loss_target'] + ['m_' + n for n in WEIGHTS] + ['v_' + n for n in WEIGHTS]


def _sq_relu(h):
    r = jnp.maximum(h, jnp.zeros_like(h))
    return r * r


def _mlp_forward(i, x, xb, w1, w2, g, b):
    hb = mm_plain_nn(f"mlp{i}_up", xb, w1, BF16, tm=2048, tn=1024)
    y, yb, xh, rstd = mm_residual_ln(f"mlp{i}_down_ln", hb, w2, x, g, b, tk=4096, a_fn=_sq_relu)
    return (y, yb), dict(xb=xb, hb=hb, xh=xh, rstd=rstd, g=g)


def _mlp_backward(i, dr, drb, sv, w1, w2, dw1, dw2, reduce_after, mixer_ln):
    s = dr.shape[0]
    tm, tn = _tile(s, 1024), 1024

    def epi(acc, e, o):
        o[0][...] = (acc * (2.0 * jnp.maximum(e[0][...].astype(F32), 0.0))).astype(BF16)

    dhb = mm_nt(f"mlp{i}_down_bwd", drb, w2, s, tm, tn, 1024, epi, [_sds((s, w2.k), BF16)], [_ij(tm, tn)],
                [sv["hb"]], [_ij(tm, tn)])[0]
    g_w2 = mm_tn(f"mlp{i}_dw2", sv["hb"], drb, dw2, s, 1024, 1024, a_fn=_sq_relu)
    g_w1 = mm_tn(f"mlp{i}_dw1", sv["xb"], dhb, dw1, s, 1024, 1024)
    dhb = reduce_after(dhb, {f"w1_{i}": g_w1, f"w2_{i}": g_w2})
    return mm_nt_ln_backward(f"mlp{i}_up_bwd", dhb, w1, dr, *mixer_ln, tk=4096)


def kernel(x, sc_w_in, sc_conv_w, sc_w_out, mla_w_dq, mla_g_q, mla_w_uq, mla_w_dkv, mla_g_kv, mla_w_uk, mla_w_uv, mla_w_o, cf_w_pw1, cf_b_pw1, cf_dw_w, cf_dw_b, cf_norm_g, cf_norm_b, cf_w_pw2, cf_b_pw2, ff_w1, ff_w2, ln_mix_g, ln_mix_b, ln_ff_g, ln_ff_b, loss_target, m_sc_w_in, m_sc_conv_w, m_sc_w_out, m_mla_w_dq, m_mla_g_q, m_mla_w_uq, m_mla_w_dkv, m_mla_g_kv, m_mla_w_uk, m_mla_w_uv, m_mla_w_o, m_cf_w_pw1, m_cf_b_pw1, m_cf_dw_w, m_cf_dw_b, m_cf_norm_g, m_cf_norm_b, m_cf_w_pw2, m_cf_b_pw2, m_ff_w1, m_ff_w2, m_ln_mix_g, m_ln_mix_b, m_ln_ff_g, m_ln_ff_b, v_sc_w_in, v_sc_conv_w, v_sc_w_out, v_mla_w_dq, v_mla_g_q, v_mla_w_uq, v_mla_w_dkv, v_mla_g_kv, v_mla_w_uk, v_mla_w_uv, v_mla_w_o, v_cf_w_pw1, v_cf_b_pw1, v_cf_dw_w, v_cf_dw_b, v_cf_norm_g, v_cf_norm_b, v_cf_w_pw2, v_cf_b_pw2, v_ff_w1, v_ff_w2, v_ln_mix_g, v_ln_mix_b, v_ln_ff_g, v_ln_ff_b):
    given = dict(zip(ARGS, (x, sc_w_in, sc_conv_w, sc_w_out, mla_w_dq, mla_g_q, mla_w_uq, mla_w_dkv, mla_g_kv, mla_w_uk, mla_w_uv, mla_w_o, cf_w_pw1, cf_b_pw1, cf_dw_w, cf_dw_b, cf_norm_g, cf_norm_b, cf_w_pw2, cf_b_pw2, ff_w1, ff_w2, ln_mix_g, ln_mix_b, ln_ff_g, ln_ff_b, loss_target, m_sc_w_in, m_sc_conv_w, m_sc_w_out, m_mla_w_dq, m_mla_g_q, m_mla_w_uq, m_mla_w_dkv, m_mla_g_kv, m_mla_w_uk, m_mla_w_uv, m_mla_w_o, m_cf_w_pw1, m_cf_b_pw1, m_cf_dw_w, m_cf_dw_b, m_cf_norm_g, m_cf_norm_b, m_cf_w_pw2, m_cf_b_pw2, m_ff_w1, m_ff_w2, m_ln_mix_g, m_ln_mix_b, m_ln_ff_g, m_ln_ff_b, v_sc_w_in, v_sc_conv_w, v_sc_w_out, v_mla_w_dq, v_mla_g_q, v_mla_w_uq, v_mla_w_dkv, v_mla_g_kv, v_mla_w_uk, v_mla_w_uv, v_mla_w_o, v_cf_w_pw1, v_cf_b_pw1, v_cf_dw_w, v_cf_dw_b, v_cf_norm_g, v_cf_norm_b, v_cf_w_pw2, v_cf_b_pw2, v_ff_w1, v_ff_w2, v_ln_mix_g, v_ln_mix_b, v_ln_ff_g, v_ln_ff_b)))
    s, d = x.shape[1], x.shape[2]
    d_ff = 4 * d
    dq4 = d // N_CHIPS
    xq = lax.axis_index("x") * 2 + lax.axis_index("y")

    w_dkv_pad = jnp.pad(mla_w_dkv[0], ((0, 0), (0, 128 - QK_ROPE)))
    w_uq_pad = jnp.pad(mla_w_uq[0].reshape(Q_LORA, 2, QK_NOPE + QK_ROPE), ((0, 0), (0, 0), (0, HEAD_PAD - QK_NOPE - QK_ROPE)))
    small = pack_rows("vector_weights_pack", [
        sc_conv_w.reshape(2 * SC_WIDTH, dq4), cf_b_pw1.reshape(2, dq4), cf_dw_w[0], cf_dw_b, cf_norm_g, cf_norm_b,
        cf_b_pw2], 64)
    mlp_w = lambda i: [ff_w1[i].astype(BF16), ff_w2[i].astype(BF16)]
    g_in, g_out, g_w1, g_w2 = [None] * 2, [None] * 2, [None] * DEPTH, [None] * DEPTH
    g_in[0], g_out[0], g_small = gather_shards(
        "gather_mixer0", [sc_w_in[0].astype(BF16), sc_w_out[0].astype(BF16), small], by_columns=(0,))
    (g_w1[0],) = gather_shards("gather_up0", [ff_w1[0].astype(BF16)], by_columns=(0,))
    (g_w2[0],) = gather_shards("gather_down0", [ff_w2[0].astype(BF16)])
    g_dqkv, g_uq, g_uk, g_uv, g_o = gather_shards("gather_mixer1", [
        jnp.concatenate([mla_w_dq[0], w_dkv_pad], axis=1).astype(BF16),
        w_uq_pad.reshape(Q_LORA, 2 * HEAD_PAD).astype(BF16),
        mla_w_uk.reshape(KV_LORA // N_CHIPS, N_HEADS * QK_NOPE).astype(BF16),
        mla_w_uv.reshape(KV_LORA // N_CHIPS, N_HEADS * V_HEAD).astype(BF16), mla_w_o[0].astype(BF16)], by_columns=(1,))
    g_w1[1], g_w2[1] = gather_shards("gather_mlp1", mlp_w(1), by_columns=(0,))
    g_pw1, g_pw2, g_w1[2], g_w2[2] = gather_shards(
        "gather_layer2", [cf_w_pw1[0].astype(BF16), cf_w_pw2[0].astype(BF16)] + mlp_w(2), by_columns=(0, 2))
    g_in[1], g_out[1], g_w1[3], g_w2[3] = gather_shards(
        "gather_layer3", [sc_w_in[1].astype(BF16), sc_w_out[1].astype(BF16)] + mlp_w(3), by_columns=(0, 2))

    wd_t = Q_LORA + KV_LORA + 128
    w_in = [Stk("full", d, 3 * d, g_in[j]) for j in range(2)]
    w_out = [Stk("row", d, d, g_out[j]) for j in range(2)]
    w_dqkv = Stk("row", d, wd_t, g_dqkv)
    w_uq = Stk("full", Q_LORA, N_HEADS * HEAD_PAD, g_uq)
    w_uk = Stk("row", KV_LORA, N_HEADS * QK_NOPE, g_uk)
    w_uv = Stk("row", KV_LORA, N_HEADS * V_HEAD, g_uv)
    w_o = Stk("row", d, d, g_o)
    w_pw1 = Stk("full", d, 2 * d, g_pw1)
    w_pw2 = Stk("row", d, d, g_pw2)
    w_1 = [Stk("full", d, d_ff, g_w1[i]) for i in range(DEPTH)]
    w_2 = [Stk("row", d_ff, d, g_w2[i]) for i in range(DEPTH)]

    def wide(rows):
        return jnp.swapaxes(rows, 0, 1).reshape(rows.shape[1], d)

    conv_w = wide(g_small[:, 0:6]).reshape(2, SC_WIDTH, d)
    b_pw1 = g_small[:, 6:8].reshape(1, 2 * d)
    dw_w = wide(g_small[:, 8:39])
    dw_b, norm_g, norm_b, b_pw2 = (wide(g_small[:, 39 + k:40 + k]) for k in range(4))

    pos = jnp.arange(s, dtype=F32)
    inv_freq = ROPE_THETA ** (-jnp.arange(0, QK_ROPE, 2, dtype=F32) / QK_ROPE)
    ang = pos[:, None] * inv_freq[None, :]
    cos, sin, zero = jnp.cos(ang), jnp.sin(ang), jnp.zeros((s, 128 - QK_ROPE), F32)
    cf = jnp.concatenate([cos, cos, zero], axis=1)
    sf = jnp.concatenate([-sin, sin, zero], axis=1)

    def row(a, i):
        return a[i:i + 1]

    xs = x.reshape(s, d)
    cur = (xs, xs.astype(BF16))
    tape = []
    for i in range(DEPTH):
        mixer, j = i % 3, i // 3
        xf, xb = cur
        lg, lb = row(ln_mix_g, i), row(ln_mix_b, i)
        if mixer == 0:
            u = mm_plain_nn(f"sc{j}_in", xb, w_in[j], F32, tn=3 * dq4)
            gb = short_conv_gate(u, conv_w[j])
            y, yb, xh, rstd = mm_residual_ln(f"sc{j}_out_ln", gb, w_out[j], xf, lg, lb, tm=1024)
            sv = dict(xb=xb, u=u, gb=gb)
        elif mixer == 1:
            t = mm_plain_nn("mla_down", xb, w_dqkv, F32, tn=wd_t // 2)
            cq, ckv, kpe = mla_latents(t, mla_g_q, mla_g_kv, cf, sf)
            qh = mla_queries(cq, w_uq, cf, sf)
            kh = mla_keys(ckv, w_uk, kpe)
            vh = mm_plain_nn("mla_values", ckv, w_uv, BF16, tk=KV_LORA)
            oh = attention(qh, kh, vh)
            y, yb, xh, rstd = mm_residual_ln("mla_out_ln", oh, w_o, xf, lg, lb, tm=1024)
            sv = dict(xb=xb, t=t, cq=cq, ckv=ckv, qh=qh, kh=kh, vh=vh, oh=oh)
        else:
            u = mm_plain_nn("cf_pw1", xb, w_pw1, F32, bias=b_pw1)
            hc = conformer_glu_conv(u, dw_w, dw_b)
            sb = conformer_norm_swish(hc, norm_g, norm_b)
            y, yb, xh, rstd = mm_residual_ln("cf_pw2_ln", sb, w_pw2, xf, lg, lb, bias=b_pw2, tm=1024)
            sv = dict(xb=xb, u=u, hc=hc, sb=sb)
        sv.update(xh=xh, rstd=rstd, g=lg)
        cur, sv_mlp = _mlp_forward(i, y, yb, w_1[i], w_2[i], row(ln_ff_g, i), row(ln_ff_b, i))
        tape.append((sv, sv_mlp))

    g_ln = {n: [None] * DEPTH for n in ("ln_mix_g", "ln_mix_b", "ln_ff_g", "ln_ff_b")}
    last = tape[DEPTH - 1][1]
    dr, drb, g_ln["ln_ff_g"][DEPTH - 1], g_ln["ln_ff_b"][DEPTH - 1], _, loss_part = loss_ln_backward(
        cur[0], loss_target.reshape(s, d), last["xh"], last["rstd"], last["g"])

    grads = {}
    smalls = {}
    conv_grads = [None, None]
    core = lax.axis_index("c").astype(jnp.int32).reshape(1)
    chip = xq.astype(jnp.int32).reshape(1)
    pairs, landed = {}, {}
    ready, theirs = [], {}

    def hold(xs, others):
        live = [x for x in xs if x is not None]
        out = lax.optimization_barrier((*live, *others))
        rest = iter(out[:len(live)])
        return tuple(None if x is None else next(rest) for x in xs), list(out[len(live):])

    def reduce_after(x, new, early=False):
        out = lax.optimization_barrier((x, *new.values()))
        grads.update(zip(new, out[1:]))
        if early:
            theirs.update(zip(new, pair_exchange(f"pair_exchange_{len(theirs)}", list(out[1:]), True)))
        ready.extend(new)
        return out[0]

    def reduce_layer(i, x):
        late = [n for n in ready if n not in theirs]
        if late:
            theirs.update(zip(late, pair_exchange(f"pair_exchange_layer{i}", [grads[n] for n in late], False)))
        by_shape = {}
        for n in ready:
            by_shape.setdefault(grads[n].shape, []).append(n)
        for names in by_shape.values():
            pairs.update(zip(names, pair_sum([grads[n] for n in names], [theirs[n] for n in names], core)))
        sums = [pairs[n] for n in ready]
        landed.update(zip(ready, chip_exchange(f"chip_exchange_layer{i}", sums)))
        exchanged.append(list(ready))
        ready.clear()
        return hold(x, sums)[0]

    groups = [["in_0", "in_1"], ["out_0", "out_1"], ["dqkv"], ["uq"], ["uk"], ["uv"], ["o"], ["pw1"], ["pw2"],
              [f"w1_{i}" for i in range(DEPTH)], [f"w2_{i}" for i in range(DEPTH)]]
    stacks = [None] * len(groups)
    exchanged = []

    def sum_layer(x, last=False):
        names = exchanged.pop(0)
        if last:
            x, held = hold(x, [landed[n] for n in names])
            landed.update(zip(names, held))
        new = []
        for n in names:
            k = next(k for k, members in enumerate(groups) if n in members)
            stacks[k] = chip_sum(pairs[n], landed[n], chip, stacks[k], groups[k].index(n), len(groups[k]))
            new.append(stacks[k])
        return x if last else hold(x, new)[0]

    for i in reversed(range(DEPTH)):
        mixer, j = i % 3, i // 3
        sv, sv_mlp = tape[i]
        dr, drb, g_ln["ln_mix_g"][i], g_ln["ln_mix_b"][i], dr_sum = _mlp_backward(
            i, dr, drb, sv_mlp, w_1[i], w_2[i], Stk("col", d, d_ff), Stk("row", d_ff, d),
            lambda x_, new: reduce_after(x_, new, early=i > 0), (sv["xh"], sv["rstd"], sv["g"]))
        if i == 0:
            dr, drb = reduce_layer("0_mlp", (dr, drb))

        def to_input(name, a, w, tk, in_parts=False):
            side_by_side = {}
            if in_parts:
                nparts = a.shape[0]
                side_by_side = dict(
                    a_spec_fn=lambda tm, tk_: pl.BlockSpec((nparts, tm, d), lambda i_, j_, k_: (0, i_, 0)),
                    a_fn=lambda blk: jnp.concatenate([blk[p] for p in range(nparts)], axis=1))
                tk = w.n
            if i == 0:
                if side_by_side:
                    side_by_side["a_spec_fn"] = (s, side_by_side["a_spec_fn"])
                return mm_plain_nt(name, a, w, F32, tm=512, tn=1024, tk=tk, add=dr, add_scale=ALPHA, **side_by_side), None
            prev = tape[i - 1][1]
            out = mm_nt_ln_backward(name, a, w, dr, prev["xh"], prev["rstd"], prev["g"], tk=tk, **side_by_side)
            g_ln["ln_ff_g"][i - 1], g_ln["ln_ff_b"][i - 1] = out[2], out[3]
            return out[0], out[1]

        if mixer == 0:
            dgate = mm_plain_nt(f"sc{j}_out_bwd", drb, w_out[j], F32)
            dw_out = mm_tn(f"sc{j}_dw_out", sv["gb"], drb, Stk("row", d, d), s, 512, 1024)
            du, conv_grads[j] = short_conv_gate_bwd(sv["u"], conv_w[j], dgate)
            nb = d // 256
            dw_in = mm_tn(
                f"sc{j}_dw_in", sv["xb"], du, Stk("col", d, 3 * d), s, 1024, 256,
                b_spec=pl.BlockSpec((None, s, 256), lambda i_, j_, k_: (j_ // nb, k_, j_ % nb)))
            du = reduce_after(du, {f"in_{j}": dw_in, f"out_{j}": dw_out})
            dr, drb = to_input(f"sc{j}_in_bwd", du, w_in[j], d, in_parts=True)
        elif mixer == 1:
            do = mm_plain_nt("mla_out_bwd", drb, w_o, BF16)
            g_o = mm_tn("mla_dw_o", sv["oh"], drb, Stk("row", d, d), s, 512, 1024)
            dqh, dkh, dvh = attention_bwd(sv["qh"], sv["kh"], sv["vh"], do)
            dql, dkn, dkpe = mla_unrope_grads(dqh, dkh, cf, sf)
            g_uq = mm_tn("mla_dw_uq", sv["cq"], dql, Stk("col", Q_LORA, N_HEADS * HEAD_PAD), s, Q_LORA, 512)
            dcq = mm_plain_nt("mla_uq_bwd", dql, w_uq, F32, tn=Q_LORA)
            g_uk = mm_tn("mla_dw_uk", sv["ckv"], dkn, Stk("row", KV_LORA, N_HEADS * QK_NOPE), s, KV_LORA, 1024)
            g_uv = mm_tn("mla_dw_uv", sv["ckv"], dvh, Stk("row", KV_LORA, N_HEADS * V_HEAD), s, KV_LORA, 1024)
            dckv = mm_plain_nt("mla_uk_bwd", dkn, w_uk, F32, tn=KV_LORA)
            dckv = mm_plain_nt("mla_uv_bwd", dvh, w_uv, F32, tn=KV_LORA, add=dckv)
            dt, smalls["g_q"], smalls["g_kv"] = mla_latents_bwd(sv["t"], mla_g_q, mla_g_kv, cf, sf, dcq, dckv, dkpe)
            g_dqkv = mm_tn("mla_dw_down", sv["xb"], dt, Stk("row", d, wd_t), s, 512, wd_t)
            dt = reduce_after(dt, {"dqkv": g_dqkv, "uq": g_uq, "uk": g_uk, "uv": g_uv, "o": g_o})
            dr, drb = to_input("mla_down_bwd", dt, w_dqkv, wd_t)
        else:
            dsw = mm_plain_nt("cf_pw2_bwd", drb, w_pw2, F32)
            g_pw2 = mm_tn("cf_dw_pw2", sv["sb"], drb, Stk("row", d, d), s, 512, 1024)
            smalls["b_pw2"] = dr_sum
            dhc, smalls["norm_g"], smalls["norm_b"] = conformer_norm_swish_bwd(sv["hc"], norm_g, norm_b, dsw)
            du, smalls["b_pw1"], smalls["dw_w"], smalls["dw_b"] = conformer_glu_conv_bwd(sv["u"], dw_w, dhc)
            nb = d // 512
            g_pw1 = mm_tn(
                "cf_dw_pw1", sv["xb"], du, Stk("col", d, 2 * d), s, 1024, 512,
                b_spec=pl.BlockSpec((None, s, 512), lambda i_, j_, k_: (j_ // nb, k_, j_ % nb)))
            du = reduce_after(du, {"pw1": g_pw1, "pw2": g_pw2})
            dr, drb = to_input("cf_pw1_bwd", du, w_pw1, d, in_parts=True)
        if i < DEPTH - 1:
            dr, drb = sum_layer((dr, drb))
        dr, drb = reduce_layer(i, (dr, drb))
    grad_x = sum_layer(sum_layer((dr, None), last=True), last=True)[0].reshape(1, s, d)

    mine = stacks
    other = (pair_share("pair_share_mixers", mine[:9]) + pair_share("pair_share_up", mine[9:10])
             + pair_share("pair_share_down", mine[10:]))

    def padded(get):
        dqkv = jnp.concatenate([get("mla_w_dq")[0], jnp.pad(get("mla_w_dkv")[0], ((0, 0), (0, 128 - QK_ROPE)))], axis=1)
        uq = jnp.pad(get("mla_w_uq")[0].reshape(Q_LORA, 2, QK_NOPE + QK_ROPE),
                     ((0, 0), (0, 0), (0, HEAD_PAD - QK_NOPE - QK_ROPE))).reshape(Q_LORA, 2 * HEAD_PAD)
        return [get("sc_w_in"), get("sc_w_out"), dqkv[None], uq[None],
                get("mla_w_uk").reshape(1, KV_LORA // N_CHIPS, d), get("mla_w_uv").reshape(1, KV_LORA // N_CHIPS, d),
                get("mla_w_o"), get("cf_w_pw1"), get("cf_w_pw2"), get("ff_w1"), get("ff_w2")]

    w_l, m_l, v_l = (padded(lambda n, p=p: given[p + n]) for p in ("", "m_", "v_"))
    res = [adamw_joined(w_l[k], m_l[k], v_l[k], mine[k], other[k], core) for k in range(len(groups))]

    def unpadded(k):
        r_in, r_out, r_dqkv, r_uq, r_uk, r_uv, r_o, r_pw1, r_pw2, r_w1, r_w2 = (r[k] for r in res)
        return {
            "sc_w_in": r_in, "sc_w_out": r_out, "mla_w_dq": r_dqkv[:, :, 0:Q_LORA],
            "mla_w_dkv": r_dqkv[:, :, Q_LORA:Q_LORA + KV_LORA + QK_ROPE],
            "mla_w_uq": r_uq.reshape(1, Q_LORA, 2, HEAD_PAD)[:, :, :, 0:QK_NOPE + QK_ROPE].reshape(mla_w_uq.shape),
            "mla_w_uk": r_uk.reshape(mla_w_uk.shape), "mla_w_uv": r_uv.reshape(mla_w_uv.shape),
            "mla_w_o": r_o, "cf_w_pw1": r_pw1, "cf_w_pw2": r_pw2, "ff_w1": r_w1, "ff_w2": r_w2}

    big_g, big_d, big_m, big_v = (unpadded(k) for k in range(4))

    pad_row = lambda a: jnp.pad(a, ((0, 0), (0, d - a.shape[1])))
    small_parts = ([g for n in ("ln_mix_g", "ln_mix_b", "ln_ff_g", "ln_ff_b") for g in g_ln[n]]
                   + [pad_row(smalls["g_q"]), pad_row(smalls["g_kv"]), conv_grads[0], conv_grads[1],
                      smalls["b_pw1"].reshape(2, d), smalls["dw_w"], smalls["dw_b"], smalls["norm_g"], smalls["norm_b"],
                      smalls["b_pw2"], loss_part])
    red = all_reduce_small(small_parts, 64)
    loss = red[61, 0]

    where = {
        "ln_mix_g": [((), 0, DEPTH, "all")], "ln_mix_b": [((), 4, DEPTH, "all")],
        "ln_ff_g": [((), 8, DEPTH, "all")], "ln_ff_b": [((), 12, DEPTH, "all")],
        "mla_g_q": [((), 16, 1, Q_LORA)], "mla_g_kv": [((), 17, 1, KV_LORA)],
        "sc_conv_w": [((0,), 18, SC_WIDTH, "chip"), ((1,), 21, SC_WIDTH, "chip")],
        "cf_b_pw1": [((), 24, 2, "chip")], "cf_dw_w": [((0,), 26, CONF_WIDTH, "chip")],
        "cf_dw_b": [((), 57, 1, "chip")], "cf_norm_g": [((), 58, 1, "chip")], "cf_norm_b": [((), 59, 1, "chip")],
        "cf_b_pw2": [((), 60, 1, "chip")]}
    vec = list(where)
    vec_res = vector_update(red, chip, [given[n] for n in vec], [given["m_" + n] for n in vec],
                            [given["v_" + n] for n in vec], [where[n] for n in vec])
    gw = dict(big_g)
    upd = {n: [big_d[n], big_m[n], big_v[n]] for n in big_g}
    for k, n in enumerate(vec):
        gw[n] = vec_res[0][k]
        upd[n] = [vec_res[1][k], vec_res[2][k], vec_res[3][k]]

    return (loss, grad_x, *[gw[n] for n in WEIGHTS], *[upd[n][0] for n in WEIGHTS],
            *[upd[n][1] for n in WEIGHTS], *[upd[n][2] for n in WEIGHTS])
```

```python
import jax
import jax.numpy as jnp
from jax import lax
from jax.experimental import pallas as pl
from jax.experimental.pallas import tpu as pltpu
from jax.experimental.pallas import tpu_sc as plsc

F32 = jnp.float32
BF16 = jnp.bfloat16
MESH = pl.DeviceIdType.MESH

DEPTH = 4
ALPHA = (2.0 * DEPTH) ** 0.25
LN_EPS = 1e-5
RMS_EPS = 1e-6
CHUNK_SHIFT = 6
N_HEADS = 8
QK_NOPE = 128
QK_ROPE = 64
V_HEAD = 128
HEAD_PAD = 256
Q_LORA = 384
KV_LORA = 256
ROPE_THETA = 10000.0
SC_WIDTH = 3
CONF_WIDTH = 31
CONV_PAD = 32
CONV_CHUNK = 64
N_CHIPS = 4
ATTN_SCALE = (QK_NOPE + QK_ROPE) ** -0.5

ADAM_LR = 0.001
ADAM_B1 = 0.9
ADAM_B2 = 0.999
ADAM_EPS = 1e-08
ADAM_WD = 0.01
ADAM_STEP = 10

VMEM_LIMIT = 56 * 2**20

NN = (((1,), (0,)), ((), ()))
NT = (((1,), (1,)), ((), ()))
TN = (((0,), (0,)), ((), ()))


def _params(sem=None):
    return pltpu.CompilerParams(dimension_semantics=sem, vmem_limit_bytes=VMEM_LIMIT)


class Stk:
    def __init__(self, kind, k, n, arr=None):
        self.kind, self.k, self.n = kind, k, n
        self.plain = kind != "col"
        self.nloc = n // N_CHIPS if kind == "col" else n
        self.arr = arr.reshape(k, n) if arr is not None and self.plain else arr

    @property
    def shape(self):
        return (self.k, self.n) if self.plain else (N_CHIPS, self.k, self.nloc)

    def spec(self, bk, bn, f, resident=False):
        if self.plain:
            return pl.BlockSpec((bk, bn), f, pipeline_mode=pl.Buffered(1)) if resident else pl.BlockSpec((bk, bn), f)
        assert self.k % bk == 0 and self.nloc % bn == 0, (self.k, bk, self.nloc, bn)
        pn = self.nloc // bn

        def imap(*g):
            kb, nb = f(*g)
            return nb // pn, kb, nb % pn

        return pl.BlockSpec((None, bk, bn), imap)


def _mm(name, mode, a, b, grid, a_spec, b_spec, acc_shape, extras, extra_specs, out_shapes, out_specs, epi, a_fn=None,
        rows_in_order=False):
    nk = grid[2]
    ne = len(extras)

    def body(*refs):
        a_ref, b_ref = refs[0], refs[1]
        e_refs = refs[2:2 + ne]
        av = a_ref[...] if a_fn is None else a_fn(a_ref[...])
        part = lax.dot_general(av, b_ref[...], mode, preferred_element_type=F32)
        if nk == 1:
            epi(part, e_refs, refs[2 + ne:])
            return
        o_refs = refs[2 + ne:-1]
        acc = refs[-1]
        k = pl.program_id(2)

        @pl.when(k == 0)
        def _():
            acc[...] = part

        @pl.when(k > 0)
        def _():
            acc[...] += part

        @pl.when(k == nk - 1)
        def _():
            epi(acc[...], e_refs, o_refs)

    return pl.pallas_call(
        body, grid=grid, in_specs=[a_spec, b_spec, *extra_specs], out_specs=out_specs, out_shape=out_shapes,
        scratch_shapes=[pltpu.VMEM(acc_shape, F32)] if nk > 1 else [],
        compiler_params=_params(("arbitrary",) * 3 if rows_in_order else ("parallel", "parallel", "arbitrary")),
        name=name)(a, b, *extras)


def _tile(n, t):
    t = min(n, t)
    while n % t:
        t -= 8
    assert t > 0, (n, t)
    return t


def mm_nn(name, a, w, tm, tn, tk, epi, out_shapes, out_specs, extras=(), extra_specs=(), a_spec=None, a_fn=None):
    m = a.shape[0]
    tm, tn, tk = _tile(m, tm), _tile(w.n, tn), _tile(w.k, tk)
    grid = (m // tm, w.n // tn, w.k // tk)
    a_spec = a_spec or pl.BlockSpec((tm, tk), lambda i, j, k: (i, k))
    b_spec = w.spec(tk, tn, lambda i, j, k: (k, j))
    return _mm(name, NN, a, w.arr, grid, a_spec, b_spec, (tm, tn), extras, extra_specs, out_shapes, out_specs, epi, a_fn)


def mm_nt(name, a, w, m, tm, tn, tk, epi, out_shapes, out_specs, extras=(), extra_specs=(), a_spec=None,
          rows_in_order=False, a_fn=None):
    tm, tn, tk = _tile(m, tm), _tile(w.k, tn), _tile(w.n, tk)
    grid = (m // tm, w.k // tn, w.n // tk)
    a_spec = a_spec or pl.BlockSpec((tm, tk), lambda i, j, k: (i, k))
    b_spec = w.spec(tn, tk, lambda i, j, k: (j, k), resident=grid[1] == 1 and grid[2] == 1)
    return _mm(name, NT, a, w.arr, grid, a_spec, b_spec, (tm, tn), extras, extra_specs, out_shapes, out_specs, epi,
               a_fn=a_fn, rows_in_order=rows_in_order)


def mm_tn(name, a, b, dw, s, tm=512, tn=512, tk=4096, a_spec=None, b_spec=None, a_fn=None):
    tm, tn, tk = _tile(dw.k, tm), _tile(dw.n, tn), _tile(s, tk)
    grid = (dw.k // tm, dw.n // tn, s // tk)
    a_spec = a_spec or pl.BlockSpec((tk, tm), lambda i, j, k: (k, i))
    b_spec = b_spec or pl.BlockSpec((tk, tn), lambda i, j, k: (k, j))

    def epi(acc, e, o):
        o[0][...] = acc.astype(BF16)

    out = _mm(name, TN, a, b, grid, a_spec, b_spec, (tm, tn), (), (), [jax.ShapeDtypeStruct(dw.shape, BF16)],
              [dw.spec(tm, tn, lambda i, j, k: (i, j))], epi, a_fn)[0]
    return out.reshape(N_CHIPS, dw.k // N_CHIPS, dw.n) if dw.plain else out


def _sds(shape, dtype):
    return jax.ShapeDtypeStruct(shape, dtype)


def _ij(tm, tn):
    return pl.BlockSpec((tm, tn), lambda i, j, k: (i, j))


def _i0(tm, c):
    return pl.BlockSpec((tm, c), lambda i, j, k: (i, 0))


def _0j(r, tn):
    return pl.BlockSpec((r, tn), lambda i, j, k: (0, j))


def _layer_norm_rows(r, g, b):
    mu = jnp.mean(r, axis=-1, keepdims=True)
    d = r - mu
    var = jnp.mean(d * d, axis=-1, keepdims=True)
    rstd = lax.rsqrt(var + LN_EPS)
    xh = d * rstd
    return xh * g + b, xh, rstd


def mm_residual_ln(name, a, w, x, g, b, bias=None, tm=512, tk=1024, a_fn=None):
    s, d = x.shape
    tm = _tile(s, tm)
    extras = [x, g, b] + ([bias] if bias is not None else [])
    especs = [_i0(tm, d), _0j(1, d), _0j(1, d)] + ([_0j(1, d)] if bias is not None else [])

    def epi(acc, e, o):
        r = ALPHA * e[0][...] + acc
        if bias is not None:
            r = r + e[3][...]
        y, xh, rstd = _layer_norm_rows(r, e[1][...], e[2][...])
        o[0][...] = y
        o[1][...] = y.astype(BF16)
        o[2][...] = xh
        o[3][...] = rstd

    return mm_nn(name, a, w, tm, d, tk, epi,
                 [_sds((s, d), F32), _sds((s, d), BF16), _sds((s, d), F32), _sds((s, 1), F32)],
                 [_i0(tm, d), _i0(tm, d), _i0(tm, d), _i0(tm, 1)], extras, especs, a_fn=a_fn)


def mm_plain_nn(name, a, w, out_dtype, tm=1024, tn=512, tk=1024, bias=None):
    m = a.shape[0]
    tm, tn = _tile(m, tm), _tile(w.n, tn)

    def epi(acc, e, o):
        if bias is not None:
            acc = acc + e[0][...]
        o[0][...] = acc.astype(out_dtype)

    extras, especs = ([bias], [_0j(1, tn)]) if bias is not None else ((), ())
    return mm_nn(name, a, w, tm, tn, tk, epi, [_sds((m, w.n), out_dtype)], [_ij(tm, tn)], extras, especs)[0]


def mm_plain_nt(name, a, w, out_dtype, tm=1024, tn=512, tk=1024, add=None, add_scale=1.0, a_spec_fn=None, a_fn=None):
    m = a.shape[0] if a_spec_fn is None else a_spec_fn[0]
    tm, tn = _tile(m, tm), _tile(w.k, tn)
    tk = _tile(w.n, tk)

    def epi(acc, e, o):
        if add is not None:
            acc = acc + add_scale * e[0][...].astype(F32)
        o[0][...] = acc.astype(out_dtype)

    extras, especs = ([add], [_ij(tm, tn)]) if add is not None else ((), ())
    a_spec = None if a_spec_fn is None else a_spec_fn[1](tm, tk)
    return mm_nt(name, a, w, m, tm, tn, tk, epi, [_sds((m, w.k), out_dtype)], [_ij(tm, tn)], extras, especs,
                 a_spec=a_spec, a_fn=a_fn)[0]


def _rows(tm, c):
    return pl.BlockSpec((tm, c), lambda i: (i, 0))


def _fix(shape):
    nd = len(shape)
    return pl.BlockSpec(shape, lambda i: (0,) * nd)


def _accumulate(ref, val):
    @pl.when(pl.program_id(0) == 0)
    def _():
        ref[...] = jnp.zeros_like(ref)

    ref[...] += val


def _ln_backward_rows(dyv, xh, rstd, g, dr_ref, drb_ref, dg_ref, db_ref, ds_ref):
    dxh = dyv * g
    m1 = jnp.mean(dxh, axis=-1, keepdims=True)
    m2 = jnp.mean(dxh * xh, axis=-1, keepdims=True)
    dr = rstd * (dxh - m1 - xh * m2)
    dr_ref[...] = dr
    drb_ref[...] = dr.astype(BF16)
    _accumulate(dg_ref, jnp.sum(dyv * xh, axis=0, keepdims=True))
    _accumulate(db_ref, jnp.sum(dyv, axis=0, keepdims=True))
    _accumulate(ds_ref, jnp.sum(dr, axis=0, keepdims=True))


def mm_nt_ln_backward(name, a, w, add, xhat, rstd, g, tm=512, tk=1024, a_spec_fn=None, a_fn=None):
    m, d = add.shape
    tm, tk = _tile(m, tm), _tile(w.n, tk)

    def epi(acc, e, o):
        _ln_backward_rows(acc + ALPHA * e[0][...], e[1][...], e[2][...], e[3][...], *o)

    vec = pl.BlockSpec((1, d), lambda i, j, k: (0, 0))
    a_spec = None if a_spec_fn is None else a_spec_fn(tm, tk)
    return mm_nt(name, a, w, m, tm, d, tk, epi,
                 [_sds((m, d), F32), _sds((m, d), BF16), _sds((1, d), F32), _sds((1, d), F32), _sds((1, d), F32)],
                 [_i0(tm, d), _i0(tm, d), vec, vec, vec], [add, xhat, rstd, g],
                 [_i0(tm, d), _i0(tm, d), _i0(tm, 1), vec], a_spec=a_spec, rows_in_order=True, a_fn=a_fn)


def loss_ln_backward(y, target, xhat, rstd, g, tm=512):
    s, d = y.shape
    tm = _tile(s, tm)

    def body(y_ref, t_ref, xh_ref, rstd_ref, g_ref, dr_ref, drb_ref, dg_ref, db_ref, ds_ref, loss_ref):
        e = y_ref[...] - t_ref[...]
        part = 0.5 * jnp.sum(jnp.mean(e * e, axis=-1, keepdims=True), axis=0, keepdims=True)
        _accumulate(loss_ref, jnp.broadcast_to(part, (1, d)))
        _ln_backward_rows(e * (1.0 / d), xh_ref[...], rstd_ref[...], g_ref[...], dr_ref, drb_ref, dg_ref, db_ref, ds_ref)

    return pl.pallas_call(
        body, grid=(s // tm,),
        in_specs=[_rows(tm, d), _rows(tm, d), _rows(tm, d), _rows(tm, 1), _fix((1, d))],
        out_specs=[_rows(tm, d), _rows(tm, d)] + [_fix((1, d))] * 4,
        out_shape=[_sds((s, d), F32), _sds((s, d), BF16)] + [_sds((1, d), F32)] * 4,
        compiler_params=_params(("arbitrary",)), name="loss_ln_backward")(y, target, xhat, rstd, g)


def _cols(s, tc, off=0):
    return pl.BlockSpec((s, tc), lambda i: (0, i + off))


def _shift_down(z, sft, rows):
    return jnp.where(rows >= sft, pltpu.roll(z, sft, 0), 0.0)


def _shift_up(z, sft, rows, s):
    return jnp.where(rows < s - sft, pltpu.roll(z, (s - sft) % s, 0), 0.0)


def short_conv_gate(u, conv_w, tc=256):
    s, d3 = u.shape
    d = d3 // 3
    nb = d // tc

    def body(b_ref, c_ref, h_ref, w_ref, o_ref):
        rows = lax.broadcasted_iota(jnp.int32, (s, tc), 0)
        z = c_ref[...] * h_ref[...]
        cz = jnp.zeros((s, tc), F32)
        for k in range(SC_WIDTH):
            sft = SC_WIDTH - 1 - k
            cz = cz + w_ref[pl.ds(k, 1), :] * (_shift_down(z, sft, rows) if sft else z)
        o_ref[...] = (b_ref[...] * cz).astype(BF16)

    return pl.pallas_call(
        body, grid=(nb,),
        in_specs=[_cols(s, tc), _cols(s, tc, nb), _cols(s, tc, 2 * nb), _cols(SC_WIDTH, tc)],
        out_specs=_cols(s, tc), out_shape=_sds((s, d), BF16),
        compiler_params=_params(("parallel",)), name="short_conv_gate")(u, u, u, conv_w)


def short_conv_gate_bwd(u, conv_w, dg, tc=256):
    s, d3 = u.shape
    d = d3 // 3
    nb = d // tc

    def body(b_ref, c_ref, h_ref, w_ref, dg_ref, du_ref, dw_ref):
        rows = lax.broadcasted_iota(jnp.int32, (s, tc), 0)
        c, h, dgv = c_ref[...], h_ref[...], dg_ref[...]
        z = c * h
        dcz = dgv * b_ref[...]
        cz = jnp.zeros((s, tc), F32)
        dz = jnp.zeros((s, tc), F32)
        for k in range(SC_WIDTH):
            sft = SC_WIDTH - 1 - k
            zs = _shift_down(z, sft, rows) if sft else z
            wk = w_ref[pl.ds(k, 1), :]
            cz = cz + wk * zs
            dz = dz + wk * (_shift_up(dcz, sft, rows, s) if sft else dcz)
            dw_ref[pl.ds(k, 1), :] = jnp.sum(dcz * zs, axis=0, keepdims=True)
        du_ref[0] = (dgv * cz).astype(BF16)
        du_ref[1] = (dz * h).astype(BF16)
        du_ref[2] = (dz * c).astype(BF16)

    return pl.pallas_call(
        body, grid=(nb,),
        in_specs=[_cols(s, tc), _cols(s, tc, nb), _cols(s, tc, 2 * nb), _cols(SC_WIDTH, tc), _cols(s, tc)],
        out_specs=[pl.BlockSpec((3, s, tc), lambda i: (0, 0, i)), _cols(SC_WIDTH, tc)],
        out_shape=[_sds((3, s, d), BF16), _sds((SC_WIDTH, d), F32)],
        compiler_params=_params(("parallel",)), name="short_conv_gate_bwd")(u, u, u, conv_w, dg)


def _store_shifted_down(ref, z, rows):
    s, tc = z.shape
    for b in range(8):
        ref[b, pl.ds(0, CONV_PAD), :] = jnp.zeros((CONV_PAD, tc), F32)
        ref[b, pl.ds(CONV_PAD, s), :] = z if b == 0 else _shift_down(z, b, rows)


def _store_shifted_up(ref, z, rows):
    s, tc = z.shape
    for b in range(8):
        ref[b, pl.ds(0, s), :] = z if b == 0 else _shift_up(z, b, rows, s)
        ref[b, pl.ds(s, CONV_PAD), :] = jnp.zeros((CONV_PAD, tc), F32)


def conformer_glu_conv(u, dw_w, dw_b, tc=128):
    s, d2 = u.shape
    d = d2 // 2
    nb = d // tc

    ch = min(CONV_CHUNK, s)

    def body(a_ref, g_ref, w_ref, b_ref, o_ref, down):
        rows = lax.broadcasted_iota(jnp.int32, (s, tc), 0)
        _store_shifted_down(down, a_ref[...] * jax.nn.sigmoid(g_ref[...]), rows)

        def chunk(ci, carry):
            r0 = pl.multiple_of(ci * ch, ch)
            acc = jnp.broadcast_to(b_ref[...], (ch, tc))
            for k in range(CONF_WIDTH):
                sft = CONF_WIDTH - 1 - k
                acc = acc + w_ref[pl.ds(k, 1), :] * down[sft % 8, pl.ds(CONV_PAD + r0 - (sft // 8) * 8, ch), :]
            o_ref[pl.ds(r0, ch), :] = acc
            return carry

        lax.fori_loop(0, s // ch, chunk, 0)

    return pl.pallas_call(
        body, grid=(nb,),
        in_specs=[_cols(s, tc), _cols(s, tc, nb), _cols(CONF_WIDTH, tc), _cols(1, tc)],
        out_specs=_cols(s, tc), out_shape=_sds((s, d), F32),
        scratch_shapes=[pltpu.VMEM((8, CONV_PAD + s, tc), F32)],
        compiler_params=_params(("parallel",)), name="conformer_glu_conv")(u, u, dw_w, dw_b)


def conformer_glu_conv_bwd(u, dw_w, dhc, tc=128):
    s, d2 = u.shape
    d = d2 // 2
    nb = d // tc
    ch = min(CONV_CHUNK, s)

    def body(a_ref, g_ref, w_ref, dhc_ref, du_ref, dbias_ref, dw_ref, db_ref, down, up, dw_acc, dh_buf):
        rows = lax.broadcasted_iota(jnp.int32, (s, tc), 0)
        a = a_ref[...]
        sg = jax.nn.sigmoid(g_ref[...])
        dhcv = dhc_ref[...]
        _store_shifted_down(down, a * sg, rows)
        _store_shifted_up(up, dhcv, rows)
        dw_acc[...] = jnp.zeros_like(dw_acc)

        def chunk(ci, carry):
            r0 = pl.multiple_of(ci * ch, ch)
            dc = dhc_ref[pl.ds(r0, ch), :]
            dh = jnp.zeros((ch, tc), F32)
            for k in range(CONF_WIDTH):
                sft = CONF_WIDTH - 1 - k
                a8, b = (sft // 8) * 8, sft % 8
                dh = dh + w_ref[pl.ds(k, 1), :] * up[b, pl.ds(r0 + a8, ch), :]
                prod = dc * down[b, pl.ds(CONV_PAD + r0 - a8, ch), :]
                dw_acc[k] += jnp.sum(prod.reshape(ch // 8, 8, tc), axis=0)
            dh_buf[pl.ds(r0, ch), :] = dh
            return carry

        lax.fori_loop(0, s // ch, chunk, 0)
        dh = dh_buf[...]
        da = dh * sg
        dgate = dh * a * sg * (1.0 - sg)
        du_ref[0] = da.astype(BF16)
        du_ref[1] = dgate.astype(BF16)
        dbias_ref[pl.ds(0, 1), :] = jnp.sum(da, axis=0, keepdims=True)
        dbias_ref[pl.ds(1, 1), :] = jnp.sum(dgate, axis=0, keepdims=True)
        db_ref[...] = jnp.sum(dhcv, axis=0, keepdims=True)
        for k in range(CONF_WIDTH):
            dw_ref[pl.ds(k, 1), :] = jnp.sum(dw_acc[k], axis=0, keepdims=True)

    return pl.pallas_call(
        body, grid=(nb,),
        in_specs=[_cols(s, tc), _cols(s, tc, nb), _cols(CONF_WIDTH, tc), _cols(s, tc)],
        out_specs=[pl.BlockSpec((2, s, tc), lambda i: (0, 0, i)), _cols(2, tc), _cols(CONF_WIDTH, tc), _cols(1, tc)],
        out_shape=[_sds((2, s, d), BF16), _sds((2, d), F32), _sds((CONF_WIDTH, d), F32), _sds((1, d), F32)],
        scratch_shapes=[pltpu.VMEM((8, CONV_PAD + s, tc), F32), pltpu.VMEM((8, CONV_PAD + s, tc), F32),
                        pltpu.VMEM((CONF_WIDTH + 1, 8, tc), F32), pltpu.VMEM((s, tc), F32)],
        compiler_params=_params(("parallel",)), name="conformer_glu_conv_bwd")(u, u, dw_w, dhc)


def conformer_norm_swish(hc, g, b, tm=512):
    s, d = hc.shape
    tm = _tile(s, tm)

    def body(h_ref, g_ref, b_ref, o_ref):
        n, _, _ = _layer_norm_rows(h_ref[...], g_ref[...], b_ref[...])
        o_ref[...] = (n * jax.nn.sigmoid(n)).astype(BF16)

    return pl.pallas_call(
        body, grid=(s // tm,), in_specs=[_rows(tm, d), _fix((1, d)), _fix((1, d))], out_specs=_rows(tm, d),
        out_shape=_sds((s, d), BF16), compiler_params=_params(("parallel",)), name="conformer_norm_swish")(hc, g, b)


def conformer_norm_swish_bwd(hc, g, b, ds, tm=512):
    s, d = hc.shape
    tm = _tile(s, tm)

    def body(h_ref, g_ref, b_ref, ds_ref, dh_ref, dg_ref, db_ref):
        n, nh, rstd = _layer_norm_rows(h_ref[...], g_ref[...], b_ref[...])
        sg = jax.nn.sigmoid(n)
        dn = ds_ref[...] * (sg * (1.0 + n * (1.0 - sg)))
        dnh = dn * g_ref[...]
        m1 = jnp.mean(dnh, axis=-1, keepdims=True)
        m2 = jnp.mean(dnh * nh, axis=-1, keepdims=True)
        dh_ref[...] = rstd * (dnh - m1 - nh * m2)
        _accumulate(dg_ref, jnp.sum(dn * nh, axis=0, keepdims=True))
        _accumulate(db_ref, jnp.sum(dn, axis=0, keepdims=True))

    return pl.pallas_call(
        body, grid=(s // tm,), in_specs=[_rows(tm, d), _fix((1, d)), _fix((1, d)), _rows(tm, d)],
        out_specs=[_rows(tm, d), _fix((1, d)), _fix((1, d))],
        out_shape=[_sds((s, d), F32), _sds((1, d), F32), _sds((1, d), F32)],
        compiler_params=_params(("arbitrary",)), name="conformer_norm_swish_bwd")(hc, g, b, ds)


def _swap_halves(x):
    lane = lax.broadcasted_iota(jnp.int32, x.shape, 1)
    return jnp.where(lane < QK_ROPE // 2, pltpu.roll(x, 128 - QK_ROPE // 2, 1), pltpu.roll(x, QK_ROPE // 2, 1))


def _rope(x, cf, sf):
    return x * cf + _swap_halves(x) * sf


def _unrope(dx, cf, sf):
    return dx * cf - _swap_halves(dx) * sf


def _rms_rows(x, g):
    r = lax.rsqrt(jnp.mean(x * x, axis=-1, keepdims=True) + RMS_EPS)
    return x * r, r


def mla_latents(t, g_q, g_kv, cf, sf, tm=512):
    s = t.shape[0]
    tm = _tile(s, tm)

    def body(t_ref, gq_ref, gkv_ref, cf_ref, sf_ref, cq_ref, ckv_ref, kpe_ref):
        xq, _ = _rms_rows(t_ref[:, 0:Q_LORA], gq_ref[...])
        cq_ref[...] = (xq * gq_ref[...]).astype(BF16)
        xkv, _ = _rms_rows(t_ref[:, Q_LORA:Q_LORA + KV_LORA], gkv_ref[...])
        ckv_ref[...] = (xkv * gkv_ref[...]).astype(BF16)
        kpe_ref[...] = _rope(t_ref[:, Q_LORA + KV_LORA:], cf_ref[...], sf_ref[...]).astype(BF16)

    w = Q_LORA + KV_LORA + 128
    return pl.pallas_call(
        body, grid=(s // tm,),
        in_specs=[_rows(tm, w), _fix((1, Q_LORA)), _fix((1, KV_LORA)), _rows(tm, 128), _rows(tm, 128)],
        out_specs=[_rows(tm, Q_LORA), _rows(tm, KV_LORA), _rows(tm, 128)],
        out_shape=[_sds((s, Q_LORA), BF16), _sds((s, KV_LORA), BF16), _sds((s, 128), BF16)],
        compiler_params=_params(("parallel",)), name="mla_latents")(t, g_q, g_kv, cf, sf)


def mla_latents_bwd(t, g_q, g_kv, cf, sf, dcq, dckv, dkpe, tm=512):
    s = t.shape[0]
    tm = _tile(s, tm)
    w = Q_LORA + KV_LORA + 128

    def rms_bwd(x, g, dy):
        xh, r = _rms_rows(x, g)
        dxh = dy * g
        return r * (dxh - xh * jnp.mean(dxh * xh, axis=-1, keepdims=True)), jnp.sum(dy * xh, axis=0, keepdims=True)

    def body(t_ref, gq_ref, gkv_ref, cf_ref, sf_ref, dcq_ref, dckv_ref, dkpe_ref, dt_ref, dgq_ref, dgkv_ref):
        dxq, dgq = rms_bwd(t_ref[:, 0:Q_LORA], gq_ref[...], dcq_ref[...])
        dxkv, dgkv = rms_bwd(t_ref[:, Q_LORA:Q_LORA + KV_LORA], gkv_ref[...], dckv_ref[...])
        dt_ref[:, 0:Q_LORA] = dxq.astype(BF16)
        dt_ref[:, Q_LORA:Q_LORA + KV_LORA] = dxkv.astype(BF16)
        dt_ref[:, Q_LORA + KV_LORA:] = _unrope(dkpe_ref[...], cf_ref[...], sf_ref[...]).astype(BF16)
        _accumulate(dgq_ref, dgq)
        _accumulate(dgkv_ref, dgkv)

    return pl.pallas_call(
        body, grid=(s // tm,),
        in_specs=[_rows(tm, w), _fix((1, Q_LORA)), _fix((1, KV_LORA)), _rows(tm, 128), _rows(tm, 128),
                  _rows(tm, Q_LORA), _rows(tm, KV_LORA), _rows(tm, 128)],
        out_specs=[_rows(tm, w), _fix((1, Q_LORA)), _fix((1, KV_LORA))],
        out_shape=[_sds((s, w), BF16), _sds((1, Q_LORA), F32), _sds((1, KV_LORA), F32)],
        compiler_params=_params(("arbitrary",)), name="mla_latents_bwd")(t, g_q, g_kv, cf, sf, dcq, dckv, dkpe)


def mla_queries(cq, w_uq, cf, sf, tm=2048):
    s = cq.shape[0]
    tm = _tile(s, tm)

    def epi(acc, e, o):
        o[0][:, 0:QK_NOPE] = acc[:, 0:QK_NOPE].astype(BF16)
        o[0][:, QK_NOPE:] = _rope(acc[:, QK_NOPE:], e[0][...], e[1][...]).astype(BF16)

    return mm_nn("mla_queries", cq, w_uq, tm, HEAD_PAD, Q_LORA, epi, [_sds((s, N_HEADS * HEAD_PAD), BF16)],
                 [_ij(tm, HEAD_PAD)], [cf, sf], [_i0(tm, 128), _i0(tm, 128)])[0]


def mla_keys(ckv, w_uk, kpe, tm=2048):
    s = ckv.shape[0]
    tm = _tile(s, tm)

    def epi(acc, e, o):
        o[0][:, 0:QK_NOPE] = acc.astype(BF16)
        o[0][:, QK_NOPE:] = e[0][...]

    return mm_nn("mla_keys", ckv, w_uk, tm, QK_NOPE, KV_LORA, epi, [_sds((s, N_HEADS * HEAD_PAD), BF16)],
                 [_ij(tm, HEAD_PAD)], [kpe], [_i0(tm, 128)])[0]


def _masked_scores(q, k, tq, kv):
    sc = lax.dot_general(q, k, NT, preferred_element_type=F32) * ATTN_SCALE
    row = lax.broadcasted_iota(jnp.int32, (tq, tq), 0)
    col = lax.broadcasted_iota(jnp.int32, (tq, tq), 1)
    ok = lax.shift_right_logical(col, CHUNK_SHIFT) <= lax.shift_right_logical(row, CHUNK_SHIFT)
    own = jnp.where(ok, sc[:, kv - tq:], -1e30)
    return own if kv == tq else jnp.concatenate([sc[:, :kv - tq], own], axis=1)


def attention(q, k, v, tq=512):
    s = q.shape[0]
    tq = _tile(s, tq)
    nq = s // tq

    def body(q_ref, k_ref, v_ref, o_ref):
        for qi in range(nq):
            kv = (qi + 1) * tq
            sc = _masked_scores(q_ref[pl.ds(qi * tq, tq), :], k_ref[pl.ds(0, kv), :], tq, kv)
            p = jnp.exp(sc - jnp.max(sc, axis=-1, keepdims=True))
            o = lax.dot_general(p.astype(BF16), v_ref[pl.ds(0, kv), :], NN, preferred_element_type=F32)
            o_ref[pl.ds(qi * tq, tq), :] = (o / jnp.sum(p, axis=-1, keepdims=True)).astype(BF16)

    hq = pl.BlockSpec((s, HEAD_PAD), lambda h: (0, h))
    hv = pl.BlockSpec((s, V_HEAD), lambda h: (0, h))
    return pl.pallas_call(
        body, grid=(N_HEADS,), in_specs=[hq, hq, hv], out_specs=hv, out_shape=_sds((s, N_HEADS * V_HEAD), BF16),
        compiler_params=_params(("parallel",)), name="attention")(q, k, v)


def attention_bwd(q, k, v, do, tq=512):
    s = q.shape[0]
    tq = _tile(s, tq)
    nq = s // tq

    def body(q_ref, k_ref, v_ref, do_ref, dq_ref, dk_ref, dv_ref, dk_acc, dv_acc):
        dk_acc[...] = jnp.zeros_like(dk_acc)
        dv_acc[...] = jnp.zeros_like(dv_acc)
        for qi in range(nq):
            kv = (qi + 1) * tq
            qt = q_ref[pl.ds(qi * tq, tq), :]
            kt = k_ref[pl.ds(0, kv), :]
            dot = do_ref[pl.ds(qi * tq, tq), :]
            sc = _masked_scores(qt, kt, tq, kv)
            p = jnp.exp(sc - jnp.max(sc, axis=-1, keepdims=True))
            p = p / jnp.sum(p, axis=-1, keepdims=True)
            dp = lax.dot_general(dot, v_ref[pl.ds(0, kv), :], NT, preferred_element_type=F32)
            delta = jnp.sum(p * dp, axis=-1, keepdims=True)
            ds = (p * (dp - delta) * ATTN_SCALE).astype(BF16)
            dq_ref[pl.ds(qi * tq, tq), :] = lax.dot_general(ds, kt, NN, preferred_element_type=F32).astype(BF16)
            dk_acc[pl.ds(0, kv), :] += lax.dot_general(ds, qt, TN, preferred_element_type=F32)
            dv_acc[pl.ds(0, kv), :] += lax.dot_general(p.astype(BF16), dot, TN, preferred_element_type=F32)
        dk_ref[...] = dk_acc[...].astype(BF16)
        dv_ref[...] = dv_acc[...].astype(BF16)

    hq = pl.BlockSpec((s, HEAD_PAD), lambda h: (0, h))
    hv = pl.BlockSpec((s, V_HEAD), lambda h: (0, h))
    return pl.pallas_call(
        body, grid=(N_HEADS,), in_specs=[hq, hq, hv, hv], out_specs=[hq, hq, hv],
        out_shape=[_sds((s, N_HEADS * HEAD_PAD), BF16), _sds((s, N_HEADS * HEAD_PAD), BF16),
                   _sds((s, N_HEADS * V_HEAD), BF16)],
        scratch_shapes=[pltpu.VMEM((s, HEAD_PAD), F32), pltpu.VMEM((s, V_HEAD), F32)],
        compiler_params=_params(("parallel",)), name="attention_bwd")(q, k, v, do)


def mla_unrope_grads(dq, dk, cf, sf, tm=512):
    s = dq.shape[0]
    tm = _tile(s, tm)

    def body(dq_ref, dk_ref, cf_ref, sf_ref, dql_ref, dkn_ref, dkpe_ref):
        cfv, sfv = cf_ref[...], sf_ref[...]
        dkpe = jnp.zeros((tm, 128), F32)
        for h in range(N_HEADS):
            lo = h * HEAD_PAD
            dql_ref[:, lo:lo + QK_NOPE] = dq_ref[:, lo:lo + QK_NOPE]
            dql_ref[:, lo + QK_NOPE:lo + HEAD_PAD] = _unrope(
                dq_ref[:, lo + QK_NOPE:lo + HEAD_PAD].astype(F32), cfv, sfv).astype(BF16)
            dkn_ref[:, h * QK_NOPE:(h + 1) * QK_NOPE] = dk_ref[:, lo:lo + QK_NOPE]
            dkpe = dkpe + dk_ref[:, lo + QK_NOPE:lo + HEAD_PAD].astype(F32)
        dkpe_ref[...] = dkpe

    wq = N_HEADS * HEAD_PAD
    return pl.pallas_call(
        body, grid=(s // tm,), in_specs=[_rows(tm, wq), _rows(tm, wq), _rows(tm, 128), _rows(tm, 128)],
        out_specs=[_rows(tm, wq), _rows(tm, N_HEADS * QK_NOPE), _rows(tm, 128)],
        out_shape=[_sds((s, wq), BF16), _sds((s, N_HEADS * QK_NOPE), BF16), _sds((s, 128), F32)],
        compiler_params=_params(("parallel",)), name="mla_unrope_grads")(dq, dk, cf, sf)


ANY = pl.BlockSpec(memory_space=pl.ANY)
GATHER_ID = 1
CHIP_EXCHANGE_ID = 2
PAIR_ID = 3
ALL_ID = 4


def _nbytes(a):
    return a.size * a.dtype.itemsize


def _copy_cost(operand_bytes, sent_fraction):
    sent = int(operand_bytes * sent_fraction)
    return pl.CostEstimate(flops=0, transcendentals=0, bytes_accessed=2 * sent, remote_bytes_transferred=sent)


def _handshake(peers):
    barrier = pltpu.get_barrier_semaphore()
    for peer in peers:
        pl.semaphore_signal(barrier, inc=1, device_id=peer, device_id_type=MESH)
    pl.semaphore_wait(barrier, len(peers))


def _place():
    x, y, c = lax.axis_index("x"), lax.axis_index("y"), lax.axis_index("c")
    chips = [(1 - x, y), (x, 1 - y), (1 - x, 1 - y)]
    return x, y, c, chips


def _half(ref, hc, axis=0):
    n = ref.shape[axis] // 2
    idx = (slice(None),) * axis + (pl.ds(hc * n, n),)
    return ref.at[idx]


def gather_shards(name, tensors, by_columns=()):
    nt = len(tensors)

    def body(*refs):
        a, g = refs[:nt], refs[nt:2 * nt]
        send, recv = refs[2 * nt:]
        x, y, c, _ = _place()
        q = 2 * x + y
        sib, xn, yn = (x, y, 1 - c), (1 - x, y, c), (x, 1 - y, c)
        q_xn, q_yn, q_diag = 2 * (1 - x) + y, 2 * x + 1 - y, 2 * (1 - x) + 1 - y
        _handshake([sib, xn, yn])

        def whole(t, p):
            if t in by_columns:
                n = a[t].shape[1]
                return g[t].at[:, pl.ds(p * n, n)]
            return g[t].at[p]

        def part(t, p, hc, quarter=None):
            rows = a[t].shape[0]
            if quarter is None:
                return whole(t, p).at[pl.ds(hc * (rows // 2), rows // 2)]
            return whole(t, p).at[pl.ds(hc * (rows // 2) + quarter * (rows // 4), rows // 4)]

        def rc(t, k, src, dst, to):
            return pltpu.make_async_remote_copy(src_ref=src, dst_ref=dst, send_sem=send.at[t, k], recv_sem=recv.at[t, k],
                                                device_id=to, device_id_type=MESH)

        sent = []

        def go(cp):
            cp.start()
            sent.append(cp)

        def landed(t, k, piece, frm):
            rc(t, k, piece, piece, frm).wait_recv()
            return piece

        for t in range(nt):
            go(rc(t, 8, a[t], whole(t, q), sib))
            mine = _half(a[t], c)
            go(rc(t, 0, mine, part(t, q, c), xn))
            go(rc(t, 1, mine, part(t, q, c), yn))
        for t in range(nt):
            from_y = landed(t, 1, part(t, q_yn, c), yn)
            go(rc(t, 2, part(t, q_yn, c, 0), part(t, q_yn, c, 0), xn))
            go(rc(t, 5, from_y, from_y, sib))
            from_x = landed(t, 0, part(t, q_xn, c), xn)
            go(rc(t, 3, part(t, q_xn, c, 1), part(t, q_xn, c, 1), yn))
            go(rc(t, 4, from_x, from_x, sib))
        for t in range(nt):
            for k, frm in ((2, xn), (3, yn)):
                piece = landed(t, k, part(t, q_diag, c, k - 2), frm)
                go(rc(t, 4 + k, piece, piece, sib))
        for t in range(nt):
            landed(t, 4, part(t, q_xn, 1 - c), sib)
            landed(t, 5, part(t, q_yn, 1 - c), sib)
            landed(t, 6, part(t, q_diag, 1 - c, 0), sib)
            landed(t, 7, part(t, q_diag, 1 - c, 1), sib)
            landed(t, 8, whole(t, q), sib)
        for cp in sent:
            cp.wait_send()

    return pl.kernel(
        body, name=name,
        out_type=[_sds((a.shape[0], N_CHIPS * a.shape[1]) if t in by_columns else (N_CHIPS,) + a.shape, a.dtype)
                  for t, a in enumerate(tensors)],
        mesh=plsc.ScalarSubcoreMesh(axis_name="sequencer", num_cores=1),
        scratch_types=[pltpu.SemaphoreType.DMA((nt, 9)), pltpu.SemaphoreType.DMA((nt, 9))],
        cost_estimate=_copy_cost(sum(_nbytes(a) for a in tensors), 4),
        compiler_params=pltpu.CompilerParams(collective_id=GATHER_ID))(*tensors)


def pair_exchange(name, grads, on_sequencer):
    nt = len(grads)

    def body(*refs):
        g, theirs = refs[:nt], refs[nt:2 * nt]
        send, recv = refs[2 * nt:]
        x, y, c, _ = _place()
        if on_sequencer:
            _handshake([(x, y, 1 - c)])
        cps = []
        for t in range(nt):
            cp = pltpu.make_async_remote_copy(src_ref=_half(g[t], 1 - c, 1), dst_ref=theirs[t], send_sem=send.at[t],
                                              recv_sem=recv.at[t], device_id=(x, y, 1 - c), device_id_type=MESH)
            cp.start()
            cps.append(cp)
        for cp in cps:
            cp.wait()

    if not on_sequencer:
        return pl.pallas_call(
            body, in_specs=[ANY] * nt, out_specs=[ANY] * nt,
            out_shape=[_sds((N_CHIPS, a.shape[1] // 2, a.shape[2]), a.dtype) for a in grads],
            scratch_shapes=[pltpu.SemaphoreType.DMA((nt,)), pltpu.SemaphoreType.DMA((nt,))],
            name=name)(*grads)
    return pl.kernel(
        body, name=name, out_type=[_sds((N_CHIPS, a.shape[1] // 2, a.shape[2]), a.dtype) for a in grads],
        mesh=plsc.ScalarSubcoreMesh(axis_name="sequencer", num_cores=1),
        scratch_types=[pltpu.SemaphoreType.DMA((nt,)), pltpu.SemaphoreType.DMA((nt,))],
        cost_estimate=_copy_cost(sum(_nbytes(a) for a in grads), 0.5),
        compiler_params=pltpu.CompilerParams(collective_id=PAIR_ID))(*grads)


def chip_exchange(name, parts):
    nt = len(parts)

    def body(*refs):
        a, r = refs[:nt], refs[nt:2 * nt]
        send, recv = refs[2 * nt:]
        x, y, c, chips = _place()
        _handshake([(*chip, c) for chip in chips])
        cps = []
        for t in range(nt):
            for j, chip in enumerate(chips):
                cp = pltpu.make_async_remote_copy(
                    src_ref=a[t].at[2 * chip[0] + chip[1]], dst_ref=r[t].at[j], send_sem=send.at[t, j],
                    recv_sem=recv.at[t, j], device_id=(*chip, c), device_id_type=MESH)
                cp.start()
                cps.append(cp)
        for cp in cps:
            cp.wait()

    return pl.kernel(
        body, name=name, out_type=[_sds((N_CHIPS - 1,) + a.shape[1:], a.dtype) for a in parts],
        mesh=plsc.ScalarSubcoreMesh(axis_name="sequencer", num_cores=1),
        scratch_types=[pltpu.SemaphoreType.DMA((nt, 3)), pltpu.SemaphoreType.DMA((nt, 3))],
        cost_estimate=_copy_cost(sum(_nbytes(a) for a in parts), 0.75),
        compiler_params=pltpu.CompilerParams(collective_id=CHIP_EXCHANGE_ID))(*parts)


def pair_share(name, halves):
    nt = len(halves)

    def body(*refs):
        h, other = refs[:nt], refs[nt:2 * nt]
        send, recv = refs[2 * nt:]
        x, y, c, _ = _place()
        _handshake([(x, y, 1 - c)])
        cps = []
        for t in range(nt):
            cp = pltpu.make_async_remote_copy(src_ref=h[t], dst_ref=other[t], send_sem=send.at[t], recv_sem=recv.at[t],
                                              device_id=(x, y, 1 - c), device_id_type=MESH)
            cp.start()
            cps.append(cp)
        for cp in cps:
            cp.wait()

    return pl.kernel(
        body, name=name, out_type=[_sds(a.shape, a.dtype) for a in halves],
        mesh=plsc.ScalarSubcoreMesh(axis_name="sequencer", num_cores=1),
        scratch_types=[pltpu.SemaphoreType.DMA((nt,)), pltpu.SemaphoreType.DMA((nt,))],
        cost_estimate=_copy_cost(sum(_nbytes(a) for a in halves), 1),
        compiler_params=pltpu.CompilerParams(collective_id=PAIR_ID))(*halves)


def pack_rows(name, parts, rows):
    cdim = parts[0].shape[1]
    n = len(parts)
    vm = pl.BlockSpec(memory_space=pltpu.VMEM)

    def pack(*refs):
        p, o_ref = refs[:n], refs[n]
        at = 0
        for ref in p:
            o_ref[pl.ds(at, ref.shape[0]), :] = ref[...]
            at += ref.shape[0]
        o_ref[pl.ds(at, rows - at), :] = jnp.zeros((rows - at, cdim), F32)

    return pl.pallas_call(pack, in_specs=[vm] * n, out_specs=vm, out_shape=_sds((rows, cdim), F32), name=name)(*parts)


def all_reduce_small(parts, rows):
    cdim = parts[0].shape[1]
    vm = pl.BlockSpec(memory_space=pltpu.VMEM)
    mine = pack_rows("small_pack", parts, rows)

    def exchange(mine_ref, buf, send, recv, lsem):
        x, y, c, _ = _place()
        me = 4 * x + 2 * y + c
        peers = [(x ^ (k >> 2), y ^ ((k >> 1) & 1), c ^ (k & 1)) for k in range(1, 8)]
        _handshake(peers)
        own = pltpu.make_async_copy(mine_ref, buf.at[me], lsem)
        own.start()
        cps = []
        for k, to in enumerate(peers):
            cp = pltpu.make_async_remote_copy(src_ref=mine_ref, dst_ref=buf.at[me], send_sem=send.at[k], recv_sem=recv.at[k],
                                              device_id=to, device_id_type=MESH)
            cp.start()
            cps.append(cp)
        for k, (px, py, pc) in enumerate(peers):
            pltpu.make_async_remote_copy(src_ref=mine_ref, dst_ref=buf.at[4 * px + 2 * py + pc], send_sem=send.at[k],
                                         recv_sem=recv.at[k], device_id=(x, y, c), device_id_type=MESH).wait_recv()
        for cp in cps:
            cp.wait_send()
        own.wait()

    landed = pl.kernel(
        exchange, name="small_exchange", out_type=_sds((8, rows, cdim), F32),
        mesh=plsc.ScalarSubcoreMesh(axis_name="sequencer", num_cores=1),
        scratch_types=[pltpu.SemaphoreType.DMA((7,)), pltpu.SemaphoreType.DMA((7,)), pltpu.SemaphoreType.DMA],
        cost_estimate=_copy_cost(rows * cdim * 4, 7),
        compiler_params=pltpu.CompilerParams(collective_id=ALL_ID))(mine)

    def total(buf, o_ref):
        acc = buf[0]
        for d in range(1, 8):
            acc = acc + buf[d]
        o_ref[...] = acc

    return pl.pallas_call(total, in_specs=[vm], out_specs=vm, out_shape=_sds((rows, cdim), F32), name="small_sum")(landed)


def pair_sum(gs, theirs, core, tm=256):
    n = len(gs)
    _, r, c = gs[0].shape
    tm = _tile(r // 2, tm)
    nh = r // 2 // tm

    def body(core_ref, *refs):
        for a_ref, b_ref, o_ref in zip(refs[:n], refs[n:2 * n], refs[2 * n:]):
            o_ref[...] = (a_ref[...].astype(F32) + b_ref[...].astype(F32)).astype(BF16)

    blk = (N_CHIPS, tm, c)
    own = pl.BlockSpec(blk, lambda i, cr: (0, cr[0] * nh + i, 0))
    half = pl.BlockSpec(blk, lambda i, cr: (0, i, 0))
    return pl.pallas_call(
        body, grid_spec=pltpu.PrefetchScalarGridSpec(
            num_scalar_prefetch=1, grid=(nh,), in_specs=[own] * n + [half] * n, out_specs=[half] * n),
        out_shape=[_sds(t.shape, BF16) for t in theirs], compiler_params=_params(("parallel",)),
        name="pair_sum")(core, *gs, *theirs)


def chip_sum(own, landed, chip, stack, layer, layers, tm=256):
    _, r, c = own.shape
    tm = _tile(r, tm)

    def body(chip_ref, own_ref, l_ref, *rest):
        acc = own_ref[...].astype(F32)
        for j in range(N_CHIPS - 1):
            acc = acc + l_ref[j].astype(F32)
        rest[-1][...] = acc

    in_specs = [pl.BlockSpec((None, tm, c), lambda i, qr: (qr[0], i, 0)),
                pl.BlockSpec((N_CHIPS - 1, tm, c), lambda i, qr: (0, i, 0))]
    args = [chip, own, landed]
    if stack is not None:
        in_specs.append(ANY)
        args.append(stack)
    return pl.pallas_call(
        body, grid_spec=pltpu.PrefetchScalarGridSpec(
            num_scalar_prefetch=1, grid=(r // tm,), in_specs=in_specs,
            out_specs=pl.BlockSpec((None, tm, c), lambda i, qr: (layer, i, 0))),
        out_shape=_sds((layers, r, c), F32), input_output_aliases={3: 0} if stack is not None else {},
        compiler_params=_params(("parallel",)), name="chip_sum")(*args)


def _adamw_math(w, g, m, v):
    bc1 = 1.0 - ADAM_B1 ** ADAM_STEP
    bc2 = 1.0 - ADAM_B2 ** ADAM_STEP
    nm = ADAM_B1 * m + (1.0 - ADAM_B1) * g
    nv = ADAM_B2 * v + (1.0 - ADAM_B2) * (g * g)
    return -ADAM_LR * ((nm / bc1) / (jnp.sqrt(nv / bc2) + ADAM_EPS) + ADAM_WD * w), nm, nv


def vector_update(red, chip, ws, ms, vs, where):
    n = len(ws)
    dd = red.shape[1]

    def body(chip_ref, red_ref, *refs):
        w_r, m_r, v_r = refs[0:n], refs[n:2 * n], refs[2 * n:3 * n]
        g_o, d_o, m_o, v_o = (refs[(3 + k) * n:(4 + k) * n] for k in range(4))
        q = chip_ref[0]

        def chip_block(val, width):
            out = val[:, 0:width]
            for p in range(1, val.shape[1] // width):
                out = jnp.where(q == p, val[:, p * width:(p + 1) * width], out)
            return out

        for k in range(n):
            for idx, r0, nr, cols in where[k]:
                width = w_r[k].shape[-1]
                if cols == "chip" and width * N_CHIPS != dd:
                    g = chip_block(jnp.concatenate([red_ref[pl.ds(r0 + j, 1), :] for j in range(nr)], axis=1), width)
                else:
                    g = red_ref[pl.ds(r0, nr), :]
                    g = chip_block(g, width) if cols == "chip" else g if cols == "all" else g[:, 0:cols]
                delta, nm, nv = _adamw_math(w_r[k][idx], g, m_r[k][idx], v_r[k][idx])
                g_o[k][idx] = g
                d_o[k][idx] = delta
                m_o[k][idx] = nm
                v_o[k][idx] = nv

    vm = pl.BlockSpec(memory_space=pltpu.VMEM)
    outs = pl.pallas_call(
        body, in_specs=[pl.BlockSpec(memory_space=pltpu.SMEM), vm] + [vm] * (3 * n), out_specs=[vm] * (4 * n),
        out_shape=[_sds(w.shape, F32) for w in ws] * 4, name="vector_update")(chip, red, *ws, *ms, *vs)
    return [outs[k * n:(k + 1) * n] for k in range(4)]


def adamw_joined(w, m, v, g_mine, g_theirs, core, tm=512):
    nl, r, c = w.shape
    tm = _tile(r // 2, tm)
    nh = r // 2 // tm

    def body(core_ref, w_ref, m_ref, v_ref, gm_ref, gt_ref, g_ref, d_ref, nm_ref, nv_ref):
        mine = (pl.program_id(1) // nh) == core_ref[0]
        gv = jnp.where(mine, gm_ref[...], gt_ref[...])
        g_ref[...] = gv
        d_ref[...], nm_ref[...], nv_ref[...] = _adamw_math(w_ref[...], gv, m_ref[...], v_ref[...])

    full = pl.BlockSpec((None, tm, c), lambda l, i, cr: (l, i, 0))
    mine = pl.BlockSpec((None, tm, c), lambda l, i, cr: (l, jnp.where(i // nh == cr[0], i % nh, 0), 0))
    theirs = pl.BlockSpec((None, tm, c), lambda l, i, cr: (l, jnp.where(i // nh == cr[0], 0, i % nh), 0))
    return pl.pallas_call(
        body, grid_spec=pltpu.PrefetchScalarGridSpec(
            num_scalar_prefetch=1, grid=(nl, r // tm), in_specs=[full, full, full, mine, theirs], out_specs=[full] * 4),
        out_shape=[_sds((nl, r, c), F32)] * 4, compiler_params=_params(("parallel", "parallel")),
        name="adamw_joined")(core, w, m, v, g_mine, g_theirs)


WEIGHTS = ['sc_w_in', 'sc_conv_w', 'sc_w_out', 'mla_w_dq', 'mla_g_q', 'mla_w_uq', 'mla_w_dkv', 'mla_g_kv', 'mla_w_uk',
           'mla_w_uv', 'mla_w_o', 'cf_w_pw1', 'cf_b_pw1', 'cf_dw_w', 'cf_dw_b', 'cf_norm_g', 'cf_norm_b', 'cf_w_pw2',
           'cf_b_pw2', 'ff_w1', 'ff_w2', 'ln_mix_g', 'ln_mix_b', 'ln_ff_g', 'ln_ff_b']
ARGS = ['x'] + WEIGHTS + ['loss_target'] + ['m_' + n for n in WEIGHTS] + ['v_' + n for n in WEIGHTS]


def _sq_relu(h):
    r = jnp.maximum(h, jnp.zeros_like(h))
    return r * r


def mlp_weight_grads(name, hb, drb, xb, dhb, t=1024):
    s, d_ff = hb.shape
    d = xb.shape[1]
    assert d_ff // N_CHIPS == t, (d_ff, t)
    n2 = d_ff // t

    def body(hb_ref, drb_ref, xb_ref, dhb_ref, o2_ref, o1_ref):
        i = pl.program_id(0)

        @pl.when(i < n2)
        def _():
            o2_ref[...] = lax.dot_general(_sq_relu(hb_ref[...]), drb_ref[...], TN, preferred_element_type=F32).astype(BF16)

        @pl.when(i >= n2)
        def _():
            o1_ref[...] = lax.dot_general(xb_ref[...], dhb_ref[...], TN, preferred_element_type=F32).astype(BF16)

    whole = lambda: pl.BlockSpec((s, d), lambda i: (0, 0), pipeline_mode=pl.Buffered(1))
    g_w2, g_w1 = pl.pallas_call(
        body, grid=(n2 + N_CHIPS,),
        in_specs=[pl.BlockSpec((s, t), lambda i: (0, jnp.minimum(i, n2 - 1))), whole(), whole(),
                  pl.BlockSpec((s, t), lambda i: (0, jnp.maximum(i - n2, 0)))],
        out_specs=[pl.BlockSpec((t, d), lambda i: (jnp.minimum(i, n2 - 1), 0)),
                   pl.BlockSpec((None, d, t), lambda i: (jnp.maximum(i - n2, 0), 0, 0))],
        out_shape=[_sds((d_ff, d), BF16), _sds((N_CHIPS, d, t), BF16)],
        compiler_params=_params(("arbitrary",)), name=name)(hb, drb, xb, dhb)
    return g_w1, g_w2.reshape(N_CHIPS, d_ff // N_CHIPS, d)


def _mlp_forward(i, x, xb, w1, w2, g, b):
    hb = mm_plain_nn(f"mlp{i}_up", xb, w1, BF16, tm=2048, tn=1024)
    y, yb, xh, rstd = mm_residual_ln(f"mlp{i}_down_ln", hb, w2, x, g, b, tk=4096, a_fn=_sq_relu)
    return (y, yb), dict(xb=xb, hb=hb, xh=xh, rstd=rstd, g=g)


def _mlp_backward(i, dr, drb, sv, w1, w2, dw1, dw2, reduce_after, mixer_ln):
    s = dr.shape[0]
    tm, tn = _tile(s, 1024), 1024

    def epi(acc, e, o):
        o[0][...] = (acc * (2.0 * jnp.maximum(e[0][...].astype(F32), 0.0))).astype(BF16)

    dhb = mm_nt(f"mlp{i}_down_bwd", drb, w2, s, tm, tn, 1024, epi, [_sds((s, w2.k), BF16)], [_ij(tm, tn)],
                [sv["hb"]], [_ij(tm, tn)])[0]
    g_w1, g_w2 = mlp_weight_grads(f"mlp{i}_dw", sv["hb"], drb, sv["xb"], dhb)
    dhb = reduce_after(dhb, {f"w1_{i}": g_w1, f"w2_{i}": g_w2})
    return mm_nt_ln_backward(f"mlp{i}_up_bwd", dhb, w1, dr, *mixer_ln, tk=4096)


def kernel(x, sc_w_in, sc_conv_w, sc_w_out, mla_w_dq, mla_g_q, mla_w_uq, mla_w_dkv, mla_g_kv, mla_w_uk, mla_w_uv, mla_w_o, cf_w_pw1, cf_b_pw1, cf_dw_w, cf_dw_b, cf_norm_g, cf_norm_b, cf_w_pw2, cf_b_pw2, ff_w1, ff_w2, ln_mix_g, ln_mix_b, ln_ff_g, ln_ff_b, loss_target, m_sc_w_in, m_sc_conv_w, m_sc_w_out, m_mla_w_dq, m_mla_g_q, m_mla_w_uq, m_mla_w_dkv, m_mla_g_kv, m_mla_w_uk, m_mla_w_uv, m_mla_w_o, m_cf_w_pw1, m_cf_b_pw1, m_cf_dw_w, m_cf_dw_b, m_cf_norm_g, m_cf_norm_b, m_cf_w_pw2, m_cf_b_pw2, m_ff_w1, m_ff_w2, m_ln_mix_g, m_ln_mix_b, m_ln_ff_g, m_ln_ff_b, v_sc_w_in, v_sc_conv_w, v_sc_w_out, v_mla_w_dq, v_mla_g_q, v_mla_w_uq, v_mla_w_dkv, v_mla_g_kv, v_mla_w_uk, v_mla_w_uv, v_mla_w_o, v_cf_w_pw1, v_cf_b_pw1, v_cf_dw_w, v_cf_dw_b, v_cf_norm_g, v_cf_norm_b, v_cf_w_pw2, v_cf_b_pw2, v_ff_w1, v_ff_w2, v_ln_mix_g, v_ln_mix_b, v_ln_ff_g, v_ln_ff_b):
    given = dict(zip(ARGS, (x, sc_w_in, sc_conv_w, sc_w_out, mla_w_dq, mla_g_q, mla_w_uq, mla_w_dkv, mla_g_kv, mla_w_uk, mla_w_uv, mla_w_o, cf_w_pw1, cf_b_pw1, cf_dw_w, cf_dw_b, cf_norm_g, cf_norm_b, cf_w_pw2, cf_b_pw2, ff_w1, ff_w2, ln_mix_g, ln_mix_b, ln_ff_g, ln_ff_b, loss_target, m_sc_w_in, m_sc_conv_w, m_sc_w_out, m_mla_w_dq, m_mla_g_q, m_mla_w_uq, m_mla_w_dkv, m_mla_g_kv, m_mla_w_uk, m_mla_w_uv, m_mla_w_o, m_cf_w_pw1, m_cf_b_pw1, m_cf_dw_w, m_cf_dw_b, m_cf_norm_g, m_cf_norm_b, m_cf_w_pw2, m_cf_b_pw2, m_ff_w1, m_ff_w2, m_ln_mix_g, m_ln_mix_b, m_ln_ff_g, m_ln_ff_b, v_sc_w_in, v_sc_conv_w, v_sc_w_out, v_mla_w_dq, v_mla_g_q, v_mla_w_uq, v_mla_w_dkv, v_mla_g_kv, v_mla_w_uk, v_mla_w_uv, v_mla_w_o, v_cf_w_pw1, v_cf_b_pw1, v_cf_dw_w, v_cf_dw_b, v_cf_norm_g, v_cf_norm_b, v_cf_w_pw2, v_cf_b_pw2, v_ff_w1, v_ff_w2, v_ln_mix_g, v_ln_mix_b, v_ln_ff_g, v_ln_ff_b)))
    s, d = x.shape[1], x.shape[2]
    d_ff = 4 * d
    dq4 = d // N_CHIPS
    xq = lax.axis_index("x") * 2 + lax.axis_index("y")

    w_dkv_pad = jnp.pad(mla_w_dkv[0], ((0, 0), (0, 128 - QK_ROPE)))
    w_uq_pad = jnp.pad(mla_w_uq[0].reshape(Q_LORA, 2, QK_NOPE + QK_ROPE), ((0, 0), (0, 0), (0, HEAD_PAD - QK_NOPE - QK_ROPE)))
    small = pack_rows("vector_weights_pack", [
        sc_conv_w.reshape(2 * SC_WIDTH, dq4), cf_b_pw1.reshape(2, dq4), cf_dw_w[0], cf_dw_b, cf_norm_g, cf_norm_b,
        cf_b_pw2], 64)
    mlp_w = lambda i: [ff_w1[i].astype(BF16), ff_w2[i].astype(BF16)]
    g_in, g_out, g_w1, g_w2 = [None] * 2, [None] * 2, [None] * DEPTH, [None] * DEPTH
    g_in[0], g_out[0], g_small = gather_shards(
        "gather_mixer0", [sc_w_in[0].astype(BF16), sc_w_out[0].astype(BF16), small], by_columns=(0,))
    (g_w1[0],) = gather_shards("gather_up0", [ff_w1[0].astype(BF16)], by_columns=(0,))
    (g_w2[0],) = gather_shards("gather_down0", [ff_w2[0].astype(BF16)])
    g_dqkv, g_uq, g_uk, g_uv, g_o = gather_shards("gather_mixer1", [
        jnp.concatenate([mla_w_dq[0], w_dkv_pad], axis=1).astype(BF16),
        w_uq_pad.reshape(Q_LORA, 2 * HEAD_PAD).astype(BF16),
        mla_w_uk.reshape(KV_LORA // N_CHIPS, N_HEADS * QK_NOPE).astype(BF16),
        mla_w_uv.reshape(KV_LORA // N_CHIPS, N_HEADS * V_HEAD).astype(BF16), mla_w_o[0].astype(BF16)], by_columns=(1,))
    g_w1[1], g_w2[1] = gather_shards("gather_mlp1", mlp_w(1), by_columns=(0,))
    g_pw1, g_pw2, g_w1[2], g_w2[2] = gather_shards(
        "gather_layer2", [cf_w_pw1[0].astype(BF16), cf_w_pw2[0].astype(BF16)] + mlp_w(2), by_columns=(0, 2))
    g_in[1], g_out[1], g_w1[3], g_w2[3] = gather_shards(
        "gather_layer3", [sc_w_in[1].astype(BF16), sc_w_out[1].astype(BF16)] + mlp_w(3), by_columns=(0, 2))

    wd_t = Q_LORA + KV_LORA + 128
    w_in = [Stk("full", d, 3 * d, g_in[j]) for j in range(2)]
    w_out = [Stk("row", d, d, g_out[j]) for j in range(2)]
    w_dqkv = Stk("row", d, wd_t, g_dqkv)
    w_uq = Stk("full", Q_LORA, N_HEADS * HEAD_PAD, g_uq)
    w_uk = Stk("row", KV_LORA, N_HEADS * QK_NOPE, g_uk)
    w_uv = Stk("row", KV_LORA, N_HEADS * V_HEAD, g_uv)
    w_o = Stk("row", d, d, g_o)
    w_pw1 = Stk("full", d, 2 * d, g_pw1)
    w_pw2 = Stk("row", d, d, g_pw2)
    w_1 = [Stk("full", d, d_ff, g_w1[i]) for i in range(DEPTH)]
    w_2 = [Stk("row", d_ff, d, g_w2[i]) for i in range(DEPTH)]

    def wide(rows):
        return jnp.swapaxes(rows, 0, 1).reshape(rows.shape[1], d)

    conv_w = wide(g_small[:, 0:6]).reshape(2, SC_WIDTH, d)
    b_pw1 = g_small[:, 6:8].reshape(1, 2 * d)
    dw_w = wide(g_small[:, 8:39])
    dw_b, norm_g, norm_b, b_pw2 = (wide(g_small[:, 39 + k:40 + k]) for k in range(4))

    pos = jnp.arange(s, dtype=F32)
    inv_freq = ROPE_THETA ** (-jnp.arange(0, QK_ROPE, 2, dtype=F32) / QK_ROPE)
    ang = pos[:, None] * inv_freq[None, :]
    cos, sin, zero = jnp.cos(ang), jnp.sin(ang), jnp.zeros((s, 128 - QK_ROPE), F32)
    cf = jnp.concatenate([cos, cos, zero], axis=1)
    sf = jnp.concatenate([-sin, sin, zero], axis=1)

    def row(a, i):
        return a[i:i + 1]

    xs = x.reshape(s, d)
    cur = (xs, xs.astype(BF16))
    tape = []
    for i in range(DEPTH):
        mixer, j = i % 3, i // 3
        xf, xb = cur
        lg, lb = row(ln_mix_g, i), row(ln_mix_b, i)
        if mixer == 0:
            u = mm_plain_nn(f"sc{j}_in", xb, w_in[j], F32, tn=3 * dq4)
            gb = short_conv_gate(u, conv_w[j])
            y, yb, xh, rstd = mm_residual_ln(f"sc{j}_out_ln", gb, w_out[j], xf, lg, lb)
            sv = dict(xb=xb, u=u, gb=gb)
        elif mixer == 1:
            t = mm_plain_nn("mla_down", xb, w_dqkv, F32, tn=wd_t // 2)
            cq, ckv, kpe = mla_latents(t, mla_g_q, mla_g_kv, cf, sf)
            qh = mla_queries(cq, w_uq, cf, sf)
            kh = mla_keys(ckv, w_uk, kpe)
            vh = mm_plain_nn("mla_values", ckv, w_uv, BF16, tk=KV_LORA)
            oh = attention(qh, kh, vh)
            y, yb, xh, rstd = mm_residual_ln("mla_out_ln", oh, w_o, xf, lg, lb)
            sv = dict(xb=xb, t=t, cq=cq, ckv=ckv, qh=qh, kh=kh, vh=vh, oh=oh)
        else:
            u = mm_plain_nn("cf_pw1", xb, w_pw1, F32, bias=b_pw1)
            hc = conformer_glu_conv(u, dw_w, dw_b)
            sb = conformer_norm_swish(hc, norm_g, norm_b)
            y, yb, xh, rstd = mm_residual_ln("cf_pw2_ln", sb, w_pw2, xf, lg, lb, bias=b_pw2)
            sv = dict(xb=xb, u=u, hc=hc, sb=sb)
        sv.update(xh=xh, rstd=rstd, g=lg)
        cur, sv_mlp = _mlp_forward(i, y, yb, w_1[i], w_2[i], row(ln_ff_g, i), row(ln_ff_b, i))
        tape.append((sv, sv_mlp))

    g_ln = {n: [None] * DEPTH for n in ("ln_mix_g", "ln_mix_b", "ln_ff_g", "ln_ff_b")}
    last = tape[DEPTH - 1][1]
    dr, drb, g_ln["ln_ff_g"][DEPTH - 1], g_ln["ln_ff_b"][DEPTH - 1], _, loss_part = loss_ln_backward(
        cur[0], loss_target.reshape(s, d), last["xh"], last["rstd"], last["g"])

    grads = {}
    smalls = {}
    conv_grads = [None, None]
    core = lax.axis_index("c").astype(jnp.int32).reshape(1)
    chip = xq.astype(jnp.int32).reshape(1)
    pairs, landed = {}, {}
    ready, theirs = [], {}

    def hold(xs, others):
        live = [x for x in xs if x is not None]
        out = lax.optimization_barrier((*live, *others))
        rest = iter(out[:len(live)])
        return tuple(None if x is None else next(rest) for x in xs), list(out[len(live):])

    def reduce_after(x, new, early=False):
        out = lax.optimization_barrier((x, *new.values()))
        grads.update(zip(new, out[1:]))
        if early:
            theirs.update(zip(new, pair_exchange(f"pair_exchange_{len(theirs)}", list(out[1:]), True)))
        ready.extend(new)
        return out[0]

    def reduce_layer(i, x):
        late = [n for n in ready if n not in theirs]
        if late:
            theirs.update(zip(late, pair_exchange(f"pair_exchange_layer{i}", [grads[n] for n in late], False)))
        by_shape = {}
        for n in ready:
            by_shape.setdefault(grads[n].shape, []).append(n)
        for names in by_shape.values():
            pairs.update(zip(names, pair_sum([grads[n] for n in names], [theirs[n] for n in names], core)))
        sums = [pairs[n] for n in ready]
        landed.update(zip(ready, chip_exchange(f"chip_exchange_layer{i}", sums)))
        exchanged.append(list(ready))
        ready.clear()
        return hold(x, sums)[0]

    groups = [["in_0", "in_1"], ["out_0", "out_1"], ["dqkv"], ["uq"], ["uk"], ["uv"], ["o"], ["pw1"], ["pw2"],
              [f"w1_{i}" for i in range(DEPTH)], [f"w2_{i}" for i in range(DEPTH)]]
    stacks = [None] * len(groups)
    exchanged = []

    def sum_layer(x, last=False):
        names = exchanged.pop(0)
        if last:
            x, held = hold(x, [landed[n] for n in names])
            landed.update(zip(names, held))
        new = []
        for n in names:
            k = next(k for k, members in enumerate(groups) if n in members)
            stacks[k] = chip_sum(pairs[n], landed[n], chip, stacks[k], groups[k].index(n), len(groups[k]))
            new.append(stacks[k])
        return x if last else hold(x, new)[0]

    for i in reversed(range(DEPTH)):
        mixer, j = i % 3, i // 3
        sv, sv_mlp = tape[i]
        dr, drb, g_ln["ln_mix_g"][i], g_ln["ln_mix_b"][i], dr_sum = _mlp_backward(
            i, dr, drb, sv_mlp, w_1[i], w_2[i], Stk("col", d, d_ff), Stk("row", d_ff, d),
            lambda x_, new: reduce_after(x_, new, early=i > 0), (sv["xh"], sv["rstd"], sv["g"]))
        if i == 0:
            dr, drb = reduce_layer("0_mlp", (dr, drb))

        def to_input(name, a, w, tk, in_parts=False):
            side_by_side = {}
            if in_parts:
                nparts = a.shape[0]
                side_by_side = dict(
                    a_spec_fn=lambda tm, tk_: pl.BlockSpec((nparts, tm, d), lambda i_, j_, k_: (0, i_, 0)),
                    a_fn=lambda blk: jnp.concatenate([blk[p] for p in range(nparts)], axis=1))
                tk = w.n
            if i == 0:
                if side_by_side:
                    side_by_side["a_spec_fn"] = (s, side_by_side["a_spec_fn"])
                return mm_plain_nt(name, a, w, F32, tm=512, tn=1024, tk=tk, add=dr, add_scale=ALPHA, **side_by_side), None
            prev = tape[i - 1][1]
            out = mm_nt_ln_backward(name, a, w, dr, prev["xh"], prev["rstd"], prev["g"], tk=tk, **side_by_side)
            g_ln["ln_ff_g"][i - 1], g_ln["ln_ff_b"][i - 1] = out[2], out[3]
            return out[0], out[1]

        if mixer == 0:
            dgate = mm_plain_nt(f"sc{j}_out_bwd", drb, w_out[j], F32)
            dw_out = mm_tn(f"sc{j}_dw_out", sv["gb"], drb, Stk("row", d, d), s, 512, 1024)
            du, conv_grads[j] = short_conv_gate_bwd(sv["u"], conv_w[j], dgate)
            nb = d // 256
            dw_in = mm_tn(
                f"sc{j}_dw_in", sv["xb"], du, Stk("col", d, 3 * d), s, 1024, 256,
                b_spec=pl.BlockSpec((None, s, 256), lambda i_, j_, k_: (j_ // nb, k_, j_ % nb)))
            du = reduce_after(du, {f"in_{j}": dw_in, f"out_{j}": dw_out})
            dr, drb = to_input(f"sc{j}_in_bwd", du, w_in[j], d, in_parts=True)
        elif mixer == 1:
            do = mm_plain_nt("mla_out_bwd", drb, w_o, BF16)
            g_o = mm_tn("mla_dw_o", sv["oh"], drb, Stk("row", d, d), s, 512, 1024)
            dqh, dkh, dvh = attention_bwd(sv["qh"], sv["kh"], sv["vh"], do)
            dql, dkn, dkpe = mla_unrope_grads(dqh, dkh, cf, sf)
            g_uq = mm_tn("mla_dw_uq", sv["cq"], dql, Stk("col", Q_LORA, N_HEADS * HEAD_PAD), s, Q_LORA, 512)
            dcq = mm_plain_nt("mla_uq_bwd", dql, w_uq, F32, tn=Q_LORA)
            g_uk = mm_tn("mla_dw_uk", sv["ckv"], dkn, Stk("row", KV_LORA, N_HEADS * QK_NOPE), s, KV_LORA, 1024)
            g_uv = mm_tn("mla_dw_uv", sv["ckv"], dvh, Stk("row", KV_LORA, N_HEADS * V_HEAD), s, KV_LORA, 1024)
            dckv = mm_plain_nt("mla_uk_bwd", dkn, w_uk, F32, tn=KV_LORA)
            dckv = mm_plain_nt("mla_uv_bwd", dvh, w_uv, F32, tn=KV_LORA, add=dckv)
            dt, smalls["g_q"], smalls["g_kv"] = mla_latents_bwd(sv["t"], mla_g_q, mla_g_kv, cf, sf, dcq, dckv, dkpe)
            g_dqkv = mm_tn("mla_dw_down", sv["xb"], dt, Stk("row", d, wd_t), s, 512, wd_t)
            dt = reduce_after(dt, {"dqkv": g_dqkv, "uq": g_uq, "uk": g_uk, "uv": g_uv, "o": g_o})
            dr, drb = to_input("mla_down_bwd", dt, w_dqkv, wd_t)
        else:
            dsw = mm_plain_nt("cf_pw2_bwd", drb, w_pw2, F32)
            g_pw2 = mm_tn("cf_dw_pw2", sv["sb"], drb, Stk("row", d, d), s, 512, 1024)
            smalls["b_pw2"] = dr_sum
            dhc, smalls["norm_g"], smalls["norm_b"] = conformer_norm_swish_bwd(sv["hc"], norm_g, norm_b, dsw)
            du, smalls["b_pw1"], smalls["dw_w"], smalls["dw_b"] = conformer_glu_conv_bwd(sv["u"], dw_w, dhc)
            nb = d // 512
            g_pw1 = mm_tn(
                "cf_dw_pw1", sv["xb"], du, Stk("col", d, 2 * d), s, 1024, 512,
                b_spec=pl.BlockSpec((None, s, 512), lambda i_, j_, k_: (j_ // nb, k_, j_ % nb)))
            du = reduce_after(du, {"pw1": g_pw1, "pw2": g_pw2})
            dr, drb = to_input("cf_pw1_bwd", du, w_pw1, d, in_parts=True)
        if i < DEPTH - 1:
            dr, drb = sum_layer((dr, drb))
        dr, drb = reduce_layer(i, (dr, drb))
    grad_x = sum_layer(sum_layer((dr, None), last=True), last=True)[0].reshape(1, s, d)

    mine = stacks
    other = (pair_share("pair_share_mixers", mine[:9]) + pair_share("pair_share_up", mine[9:10])
             + pair_share("pair_share_down", mine[10:]))

    def padded(get):
        dqkv = jnp.concatenate([get("mla_w_dq")[0], jnp.pad(get("mla_w_dkv")[0], ((0, 0), (0, 128 - QK_ROPE)))], axis=1)
        uq = jnp.pad(get("mla_w_uq")[0].reshape(Q_LORA, 2, QK_NOPE + QK_ROPE),
                     ((0, 0), (0, 0), (0, HEAD_PAD - QK_NOPE - QK_ROPE))).reshape(Q_LORA, 2 * HEAD_PAD)
        return [get("sc_w_in"), get("sc_w_out"), dqkv[None], uq[None],
                get("mla_w_uk").reshape(1, KV_LORA // N_CHIPS, d), get("mla_w_uv").reshape(1, KV_LORA // N_CHIPS, d),
                get("mla_w_o"), get("cf_w_pw1"), get("cf_w_pw2"), get("ff_w1"), get("ff_w2")]

    w_l, m_l, v_l = (padded(lambda n, p=p: given[p + n]) for p in ("", "m_", "v_"))
    res = [adamw_joined(w_l[k], m_l[k], v_l[k], mine[k], other[k], core) for k in range(len(groups))]

    def unpadded(k):
        r_in, r_out, r_dqkv, r_uq, r_uk, r_uv, r_o, r_pw1, r_pw2, r_w1, r_w2 = (r[k] for r in res)
        return {
            "sc_w_in": r_in, "sc_w_out": r_out, "mla_w_dq": r_dqkv[:, :, 0:Q_LORA],
            "mla_w_dkv": r_dqkv[:, :, Q_LORA:Q_LORA + KV_LORA + QK_ROPE],
            "mla_w_uq": r_uq.reshape(1, Q_LORA, 2, HEAD_PAD)[:, :, :, 0:QK_NOPE + QK_ROPE].reshape(mla_w_uq.shape),
            "mla_w_uk": r_uk.reshape(mla_w_uk.shape), "mla_w_uv": r_uv.reshape(mla_w_uv.shape),
            "mla_w_o": r_o, "cf_w_pw1": r_pw1, "cf_w_pw2": r_pw2, "ff_w1": r_w1, "ff_w2": r_w2}

    big_g, big_d, big_m, big_v = (unpadded(k) for k in range(4))

    pad_row = lambda a: jnp.pad(a, ((0, 0), (0, d - a.shape[1])))
    small_parts = ([g for n in ("ln_mix_g", "ln_mix_b", "ln_ff_g", "ln_ff_b") for g in g_ln[n]]
                   + [pad_row(smalls["g_q"]), pad_row(smalls["g_kv"]), conv_grads[0], conv_grads[1],
                      smalls["b_pw1"].reshape(2, d), smalls["dw_w"], smalls["dw_b"], smalls["norm_g"], smalls["norm_b"],
                      smalls["b_pw2"], loss_part])
    red = all_reduce_small(small_parts, 64)
    loss = red[61, 0]

    where = {
        "ln_mix_g": [((), 0, DEPTH, "all")], "ln_mix_b": [((), 4, DEPTH, "all")],
        "ln_ff_g": [((), 8, DEPTH, "all")], "ln_ff_b": [((), 12, DEPTH, "all")],
        "mla_g_q": [((), 16, 1, Q_LORA)], "mla_g_kv": [((), 17, 1, KV_LORA)],
        "sc_conv_w": [((0,), 18, SC_WIDTH, "chip"), ((1,), 21, SC_WIDTH, "chip")],
        "cf_b_pw1": [((), 24, 2, "chip")], "cf_dw_w": [((0,), 26, CONF_WIDTH, "chip")],
        "cf_dw_b": [((), 57, 1, "chip")], "cf_norm_g": [((), 58, 1, "chip")], "cf_norm_b": [((), 59, 1, "chip")],
        "cf_b_pw2": [((), 60, 1, "chip")]}
    vec = list(where)
    vec_res = vector_update(red, chip, [given[n] for n in vec], [given["m_" + n] for n in vec],
                            [given["v_" + n] for n in vec], [where[n] for n in vec])
    gw = dict(big_g)
    upd = {n: [big_d[n], big_m[n], big_v[n]] for n in big_g}
    for k, n in enumerate(vec):
        gw[n] = vec_res[0][k]
        upd[n] = [vec_res[1][k], vec_res[2][k], vec_res[3][k]]

    return (loss, grad_x, *[gw[n] for n in WEIGHTS], *[upd[n][0] for n in WEIGHTS],
            *[upd[n][1] for n in WEIGHTS], *[upd[n][2] for n in WEIGHTS])
```

```python
import jax
import jax.numpy as jnp
from jax import lax
from jax.experimental import pallas as pl
from jax.experimental.pallas import tpu as pltpu
from jax.experimental.pallas import tpu_sc as plsc

F32 = jnp.float32
BF16 = jnp.bfloat16
MESH = pl.DeviceIdType.MESH

DEPTH = 4
ALPHA = (2.0 * DEPTH) ** 0.25
LN_EPS = 1e-5
RMS_EPS = 1e-6
CHUNK_SHIFT = 6
N_HEADS = 8
QK_NOPE = 128
QK_ROPE = 64
V_HEAD = 128
HEAD_PAD = 256
Q_LORA = 384
KV_LORA = 256
ROPE_THETA = 10000.0
SC_WIDTH = 3
CONF_WIDTH = 31
CONV_PAD = 32
CONV_CHUNK = 128
N_CHIPS = 4
ATTN_SCALE = (QK_NOPE + QK_ROPE) ** -0.5

ADAM_LR = 0.001
ADAM_B1 = 0.9
ADAM_B2 = 0.999
ADAM_EPS = 1e-08
ADAM_WD = 0.01
ADAM_STEP = 10

VMEM_LIMIT = 56 * 2**20

NN = (((1,), (0,)), ((), ()))
NT = (((1,), (1,)), ((), ()))
TN = (((0,), (0,)), ((), ()))


def _params(sem=None):
    return pltpu.CompilerParams(dimension_semantics=sem, vmem_limit_bytes=VMEM_LIMIT)


class Stk:
    def __init__(self, kind, k, n, arr=None):
        self.kind, self.k, self.n = kind, k, n
        self.plain = kind != "col"
        self.nloc = n // N_CHIPS if kind == "col" else n
        self.arr = arr.reshape(k, n) if arr is not None and self.plain else arr

    @property
    def shape(self):
        return (self.k, self.n) if self.plain else (N_CHIPS, self.k, self.nloc)

    def spec(self, bk, bn, f, resident=False):
        if self.plain:
            return pl.BlockSpec((bk, bn), f, pipeline_mode=pl.Buffered(1)) if resident else pl.BlockSpec((bk, bn), f)
        assert self.k % bk == 0 and self.nloc % bn == 0, (self.k, bk, self.nloc, bn)
        pn = self.nloc // bn

        def imap(*g):
            kb, nb = f(*g)
            return nb // pn, kb, nb % pn

        return pl.BlockSpec((None, bk, bn), imap)


def _mm(name, mode, a, b, grid, a_spec, b_spec, acc_shape, extras, extra_specs, out_shapes, out_specs, epi, a_fn=None,
        rows_in_order=False):
    nk = grid[2]
    ne = len(extras)

    def body(*refs):
        a_ref, b_ref = refs[0], refs[1]
        e_refs = refs[2:2 + ne]
        av = a_ref[...] if a_fn is None else a_fn(a_ref[...])
        part = lax.dot_general(av, b_ref[...], mode, preferred_element_type=F32)
        if nk == 1:
            epi(part, e_refs, refs[2 + ne:])
            return
        o_refs = refs[2 + ne:-1]
        acc = refs[-1]
        k = pl.program_id(2)

        @pl.when(k == 0)
        def _():
            acc[...] = part

        @pl.when(k > 0)
        def _():
            acc[...] += part

        @pl.when(k == nk - 1)
        def _():
            epi(acc[...], e_refs, o_refs)

    return pl.pallas_call(
        body, grid=grid, in_specs=[a_spec, b_spec, *extra_specs], out_specs=out_specs, out_shape=out_shapes,
        scratch_shapes=[pltpu.VMEM(acc_shape, F32)] if nk > 1 else [],
        compiler_params=_params(("arbitrary",) * 3 if rows_in_order else ("parallel", "parallel", "arbitrary")),
        name=name)(a, b, *extras)


def _tile(n, t):
    t = min(n, t)
    while n % t:
        t -= 8
    assert t > 0, (n, t)
    return t


def mm_nn(name, a, w, tm, tn, tk, epi, out_shapes, out_specs, extras=(), extra_specs=(), a_spec=None, a_fn=None):
    m = a.shape[0]
    tm, tn, tk = _tile(m, tm), _tile(w.n, tn), _tile(w.k, tk)
    grid = (m // tm, w.n // tn, w.k // tk)
    a_spec = a_spec or pl.BlockSpec((tm, tk), lambda i, j, k: (i, k))
    b_spec = w.spec(tk, tn, lambda i, j, k: (k, j))
    return _mm(name, NN, a, w.arr, grid, a_spec, b_spec, (tm, tn), extras, extra_specs, out_shapes, out_specs, epi, a_fn)


def mm_nt(name, a, w, m, tm, tn, tk, epi, out_shapes, out_specs, extras=(), extra_specs=(), a_spec=None,
          rows_in_order=False, a_fn=None):
    tm, tn, tk = _tile(m, tm), _tile(w.k, tn), _tile(w.n, tk)
    grid = (m // tm, w.k // tn, w.n // tk)
    a_spec = a_spec or pl.BlockSpec((tm, tk), lambda i, j, k: (i, k))
    b_spec = w.spec(tn, tk, lambda i, j, k: (j, k), resident=grid[1] == 1 and grid[2] == 1)
    return _mm(name, NT, a, w.arr, grid, a_spec, b_spec, (tm, tn), extras, extra_specs, out_shapes, out_specs, epi,
               a_fn=a_fn, rows_in_order=rows_in_order)


def mm_tn(name, a, b, dw, s, tm=512, tn=512, tk=4096, a_spec=None, b_spec=None, a_fn=None):
    tm, tn, tk = _tile(dw.k, tm), _tile(dw.n, tn), _tile(s, tk)
    grid = (dw.k // tm, dw.n // tn, s // tk)
    a_spec = a_spec or pl.BlockSpec((tk, tm), lambda i, j, k: (k, i))
    b_spec = b_spec or pl.BlockSpec((tk, tn), lambda i, j, k: (k, j))

    def epi(acc, e, o):
        o[0][...] = acc.astype(BF16)

    out = _mm(name, TN, a, b, grid, a_spec, b_spec, (tm, tn), (), (), [jax.ShapeDtypeStruct(dw.shape, BF16)],
              [dw.spec(tm, tn, lambda i, j, k: (i, j))], epi, a_fn)[0]
    return out.reshape(N_CHIPS, dw.k // N_CHIPS, dw.n) if dw.plain else out


def _sds(shape, dtype):
    return jax.ShapeDtypeStruct(shape, dtype)


def _ij(tm, tn):
    return pl.BlockSpec((tm, tn), lambda i, j, k: (i, j))


def _i0(tm, c):
    return pl.BlockSpec((tm, c), lambda i, j, k: (i, 0))


def _0j(r, tn):
    return pl.BlockSpec((r, tn), lambda i, j, k: (0, j))


def _layer_norm_rows(r, g, b):
    mu = jnp.mean(r, axis=-1, keepdims=True)
    d = r - mu
    var = jnp.mean(d * d, axis=-1, keepdims=True)
    rstd = lax.rsqrt(var + LN_EPS)
    xh = d * rstd
    return xh * g + b, xh, rstd


def mm_residual_ln(name, a, w, x, g, b, bias=None, tm=512, tk=1024, a_fn=None):
    s, d = x.shape
    tm = _tile(s, tm)
    extras = [x, g, b] + ([bias] if bias is not None else [])
    especs = [_i0(tm, d), _0j(1, d), _0j(1, d)] + ([_0j(1, d)] if bias is not None else [])

    def epi(acc, e, o):
        r = ALPHA * e[0][...] + acc
        if bias is not None:
            r = r + e[3][...]
        y, xh, rstd = _layer_norm_rows(r, e[1][...], e[2][...])
        o[0][...] = y
        o[1][...] = y.astype(BF16)
        o[2][...] = xh
        o[3][...] = rstd

    return mm_nn(name, a, w, tm, d, tk, epi,
                 [_sds((s, d), F32), _sds((s, d), BF16), _sds((s, d), F32), _sds((s, 1), F32)],
                 [_i0(tm, d), _i0(tm, d), _i0(tm, d), _i0(tm, 1)], extras, especs, a_fn=a_fn)


def mm_plain_nn(name, a, w, out_dtype, tm=1024, tn=512, tk=1024, bias=None):
    m = a.shape[0]
    tm, tn = _tile(m, tm), _tile(w.n, tn)

    def epi(acc, e, o):
        if bias is not None:
            acc = acc + e[0][...]
        o[0][...] = acc.astype(out_dtype)

    extras, especs = ([bias], [_0j(1, tn)]) if bias is not None else ((), ())
    return mm_nn(name, a, w, tm, tn, tk, epi, [_sds((m, w.n), out_dtype)], [_ij(tm, tn)], extras, especs)[0]


def mm_plain_nt(name, a, w, out_dtype, tm=1024, tn=512, tk=1024, add=None, add_scale=1.0, a_spec_fn=None, a_fn=None):
    m = a.shape[0] if a_spec_fn is None else a_spec_fn[0]
    tm, tn = _tile(m, tm), _tile(w.k, tn)
    tk = _tile(w.n, tk)

    def epi(acc, e, o):
        if add is not None:
            acc = acc + add_scale * e[0][...].astype(F32)
        o[0][...] = acc.astype(out_dtype)

    extras, especs = ([add], [_ij(tm, tn)]) if add is not None else ((), ())
    a_spec = None if a_spec_fn is None else a_spec_fn[1](tm, tk)
    return mm_nt(name, a, w, m, tm, tn, tk, epi, [_sds((m, w.k), out_dtype)], [_ij(tm, tn)], extras, especs,
                 a_spec=a_spec, a_fn=a_fn)[0]


def _rows(tm, c):
    return pl.BlockSpec((tm, c), lambda i: (i, 0))


def _fix(shape):
    nd = len(shape)
    return pl.BlockSpec(shape, lambda i: (0,) * nd)


def _accumulate(ref, val):
    @pl.when(pl.program_id(0) == 0)
    def _():
        ref[...] = jnp.zeros_like(ref)

    ref[...] += val


def _ln_backward_rows(dyv, xh, rstd, g, dr_ref, drb_ref, dg_ref, db_ref, ds_ref):
    dxh = dyv * g
    m1 = jnp.mean(dxh, axis=-1, keepdims=True)
    m2 = jnp.mean(dxh * xh, axis=-1, keepdims=True)
    dr = rstd * (dxh - m1 - xh * m2)
    dr_ref[...] = dr
    drb_ref[...] = dr.astype(BF16)
    _accumulate(dg_ref, jnp.sum(dyv * xh, axis=0, keepdims=True))
    _accumulate(db_ref, jnp.sum(dyv, axis=0, keepdims=True))
    _accumulate(ds_ref, jnp.sum(dr, axis=0, keepdims=True))


def mm_nt_ln_backward(name, a, w, add, xhat, rstd, g, tm=512, tk=1024, a_spec_fn=None, a_fn=None):
    m, d = add.shape
    tm, tk = _tile(m, tm), _tile(w.n, tk)

    def epi(acc, e, o):
        _ln_backward_rows(acc + ALPHA * e[0][...], e[1][...], e[2][...], e[3][...], *o)

    vec = pl.BlockSpec((1, d), lambda i, j, k: (0, 0))
    a_spec = None if a_spec_fn is None else a_spec_fn(tm, tk)
    return mm_nt(name, a, w, m, tm, d, tk, epi,
                 [_sds((m, d), F32), _sds((m, d), BF16), _sds((1, d), F32), _sds((1, d), F32), _sds((1, d), F32)],
                 [_i0(tm, d), _i0(tm, d), vec, vec, vec], [add, xhat, rstd, g],
                 [_i0(tm, d), _i0(tm, d), _i0(tm, 1), vec], a_spec=a_spec, rows_in_order=True, a_fn=a_fn)


def loss_ln_backward(y, target, xhat, rstd, g, tm=512):
    s, d = y.shape
    tm = _tile(s, tm)

    def body(y_ref, t_ref, xh_ref, rstd_ref, g_ref, dr_ref, drb_ref, dg_ref, db_ref, ds_ref, loss_ref):
        e = y_ref[...] - t_ref[...]
        part = 0.5 * jnp.sum(jnp.mean(e * e, axis=-1, keepdims=True), axis=0, keepdims=True)
        _accumulate(loss_ref, jnp.broadcast_to(part, (1, d)))
        _ln_backward_rows(e * (1.0 / d), xh_ref[...], rstd_ref[...], g_ref[...], dr_ref, drb_ref, dg_ref, db_ref, ds_ref)

    return pl.pallas_call(
        body, grid=(s // tm,),
        in_specs=[_rows(tm, d), _rows(tm, d), _rows(tm, d), _rows(tm, 1), _fix((1, d))],
        out_specs=[_rows(tm, d), _rows(tm, d)] + [_fix((1, d))] * 4,
        out_shape=[_sds((s, d), F32), _sds((s, d), BF16)] + [_sds((1, d), F32)] * 4,
        compiler_params=_params(("arbitrary",)), name="loss_ln_backward")(y, target, xhat, rstd, g)


def _cols(s, tc, off=0):
    return pl.BlockSpec((s, tc), lambda i: (0, i + off))


def _shift_down(z, sft, rows):
    return jnp.where(rows >= sft, pltpu.roll(z, sft, 0), 0.0)


def _shift_up(z, sft, rows, s):
    return jnp.where(rows < s - sft, pltpu.roll(z, (s - sft) % s, 0), 0.0)


def short_conv_gate(u, conv_w, tc=256):
    s, d3 = u.shape
    d = d3 // 3
    nb = d // tc

    def body(b_ref, c_ref, h_ref, w_ref, o_ref):
        rows = lax.broadcasted_iota(jnp.int32, (s, tc), 0)
        z = c_ref[...] * h_ref[...]
        cz = jnp.zeros((s, tc), F32)
        for k in range(SC_WIDTH):
            sft = SC_WIDTH - 1 - k
            cz = cz + w_ref[pl.ds(k, 1), :] * (_shift_down(z, sft, rows) if sft else z)
        o_ref[...] = (b_ref[...] * cz).astype(BF16)

    return pl.pallas_call(
        body, grid=(nb,),
        in_specs=[_cols(s, tc), _cols(s, tc, nb), _cols(s, tc, 2 * nb), _cols(SC_WIDTH, tc)],
        out_specs=_cols(s, tc), out_shape=_sds((s, d), BF16),
        compiler_params=_params(("parallel",)), name="short_conv_gate")(u, u, u, conv_w)


def short_conv_gate_bwd(u, conv_w, dg, tc=256):
    s, d3 = u.shape
    d = d3 // 3
    nb = d // tc

    def body(b_ref, c_ref, h_ref, w_ref, dg_ref, du_ref, dw_ref):
        rows = lax.broadcasted_iota(jnp.int32, (s, tc), 0)
        c, h, dgv = c_ref[...], h_ref[...], dg_ref[...]
        z = c * h
        dcz = dgv * b_ref[...]
        cz = jnp.zeros((s, tc), F32)
        dz = jnp.zeros((s, tc), F32)
        for k in range(SC_WIDTH):
            sft = SC_WIDTH - 1 - k
            zs = _shift_down(z, sft, rows) if sft else z
            wk = w_ref[pl.ds(k, 1), :]
            cz = cz + wk * zs
            dz = dz + wk * (_shift_up(dcz, sft, rows, s) if sft else dcz)
            dw_ref[pl.ds(k, 1), :] = jnp.sum(dcz * zs, axis=0, keepdims=True)
        du_ref[0] = (dgv * cz).astype(BF16)
        du_ref[1] = (dz * h).astype(BF16)
        du_ref[2] = (dz * c).astype(BF16)

    return pl.pallas_call(
        body, grid=(nb,),
        in_specs=[_cols(s, tc), _cols(s, tc, nb), _cols(s, tc, 2 * nb), _cols(SC_WIDTH, tc), _cols(s, tc)],
        out_specs=[pl.BlockSpec((3, s, tc), lambda i: (0, 0, i)), _cols(SC_WIDTH, tc)],
        out_shape=[_sds((3, s, d), BF16), _sds((SC_WIDTH, d), F32)],
        compiler_params=_params(("parallel",)), name="short_conv_gate_bwd")(u, u, u, conv_w, dg)


def _store_shifted_down(ref, z, rows):
    s, tc = z.shape
    for b in range(8):
        ref[b, pl.ds(0, CONV_PAD), :] = jnp.zeros((CONV_PAD, tc), F32)
        ref[b, pl.ds(CONV_PAD, s), :] = z if b == 0 else _shift_down(z, b, rows)


def _store_shifted_up(ref, z, rows):
    s, tc = z.shape
    for b in range(8):
        ref[b, pl.ds(0, s), :] = z if b == 0 else _shift_up(z, b, rows, s)
        ref[b, pl.ds(s, CONV_PAD), :] = jnp.zeros((CONV_PAD, tc), F32)


def conformer_glu_conv(u, dw_w, dw_b, tc=128):
    s, d2 = u.shape
    d = d2 // 2
    nb = d // tc

    ch = min(CONV_CHUNK, s)

    def body(a_ref, g_ref, w_ref, b_ref, o_ref, down):
        rows = lax.broadcasted_iota(jnp.int32, (s, tc), 0)
        _store_shifted_down(down, a_ref[...] * jax.nn.sigmoid(g_ref[...]), rows)

        def chunk(ci, carry):
            r0 = pl.multiple_of(ci * ch, ch)
            acc = jnp.broadcast_to(b_ref[...], (ch, tc))
            for k in range(CONF_WIDTH):
                sft = CONF_WIDTH - 1 - k
                acc = acc + w_ref[pl.ds(k, 1), :] * down[sft % 8, pl.ds(CONV_PAD + r0 - (sft // 8) * 8, ch), :]
            o_ref[pl.ds(r0, ch), :] = acc
            return carry

        lax.fori_loop(0, s // ch, chunk, 0)

    return pl.pallas_call(
        body, grid=(nb,),
        in_specs=[_cols(s, tc), _cols(s, tc, nb), _cols(CONF_WIDTH, tc), _cols(1, tc)],
        out_specs=_cols(s, tc), out_shape=_sds((s, d), F32),
        scratch_shapes=[pltpu.VMEM((8, CONV_PAD + s, tc), F32)],
        compiler_params=_params(("parallel",)), name="conformer_glu_conv")(u, u, dw_w, dw_b)


def conformer_glu_conv_bwd(u, dw_w, dhc, tc=128):
    s, d2 = u.shape
    d = d2 // 2
    nb = d // tc
    ch = min(CONV_CHUNK, s)

    def body(a_ref, g_ref, w_ref, dhc_ref, du_ref, dbias_ref, dw_ref, db_ref, down, up, dw_acc, dh_buf):
        rows = lax.broadcasted_iota(jnp.int32, (s, tc), 0)
        a = a_ref[...]
        sg = jax.nn.sigmoid(g_ref[...])
        dhcv = dhc_ref[...]
        _store_shifted_down(down, a * sg, rows)
        _store_shifted_up(up, dhcv, rows)
        dw_acc[...] = jnp.zeros_like(dw_acc)

        def chunk(ci, carry):
            r0 = pl.multiple_of(ci * ch, ch)
            dc = dhc_ref[pl.ds(r0, ch), :]
            dh = jnp.zeros((ch, tc), F32)
            for k in range(CONF_WIDTH):
                sft = CONF_WIDTH - 1 - k
                a8, b = (sft // 8) * 8, sft % 8
                dh = dh + w_ref[pl.ds(k, 1), :] * up[b, pl.ds(r0 + a8, ch), :]
                prod = dc * down[b, pl.ds(CONV_PAD + r0 - a8, ch), :]
                dw_acc[k] += jnp.sum(prod.reshape(ch // 8, 8, tc), axis=0)
            dh_buf[pl.ds(r0, ch), :] = dh
            return carry

        lax.fori_loop(0, s // ch, chunk, 0)
        dh = dh_buf[...]
        da = dh * sg
        dgate = dh * a * sg * (1.0 - sg)
        du_ref[0] = da.astype(BF16)
        du_ref[1] = dgate.astype(BF16)
        dbias_ref[pl.ds(0, 1), :] = jnp.sum(da, axis=0, keepdims=True)
        dbias_ref[pl.ds(1, 1), :] = jnp.sum(dgate, axis=0, keepdims=True)
        db_ref[...] = jnp.sum(dhcv, axis=0, keepdims=True)
        for k in range(CONF_WIDTH):
            dw_ref[pl.ds(k, 1), :] = jnp.sum(dw_acc[k], axis=0, keepdims=True)

    return pl.pallas_call(
        body, grid=(nb,),
        in_specs=[_cols(s, tc), _cols(s, tc, nb), _cols(CONF_WIDTH, tc), _cols(s, tc)],
        out_specs=[pl.BlockSpec((2, s, tc), lambda i: (0, 0, i)), _cols(2, tc), _cols(CONF_WIDTH, tc), _cols(1, tc)],
        out_shape=[_sds((2, s, d), BF16), _sds((2, d), F32), _sds((CONF_WIDTH, d), F32), _sds((1, d), F32)],
        scratch_shapes=[pltpu.VMEM((8, CONV_PAD + s, tc), F32), pltpu.VMEM((8, CONV_PAD + s, tc), F32),
                        pltpu.VMEM((CONF_WIDTH + 1, 8, tc), F32), pltpu.VMEM((s, tc), F32)],
        compiler_params=_params(("parallel",)), name="conformer_glu_conv_bwd")(u, u, dw_w, dhc)


def conformer_norm_swish(hc, g, b, tm=512):
    s, d = hc.shape
    tm = _tile(s, tm)

    def body(h_ref, g_ref, b_ref, o_ref):
        n, _, _ = _layer_norm_rows(h_ref[...], g_ref[...], b_ref[...])
        o_ref[...] = (n * jax.nn.sigmoid(n)).astype(BF16)

    return pl.pallas_call(
        body, grid=(s // tm,), in_specs=[_rows(tm, d), _fix((1, d)), _fix((1, d))], out_specs=_rows(tm, d),
        out_shape=_sds((s, d), BF16), compiler_params=_params(("parallel",)), name="conformer_norm_swish")(hc, g, b)


def conformer_norm_swish_bwd(hc, g, b, ds, tm=512):
    s, d = hc.shape
    tm = _tile(s, tm)

    def body(h_ref, g_ref, b_ref, ds_ref, dh_ref, dg_ref, db_ref):
        n, nh, rstd = _layer_norm_rows(h_ref[...], g_ref[...], b_ref[...])
        sg = jax.nn.sigmoid(n)
        dn = ds_ref[...] * (sg * (1.0 + n * (1.0 - sg)))
        dnh = dn * g_ref[...]
        m1 = jnp.mean(dnh, axis=-1, keepdims=True)
        m2 = jnp.mean(dnh * nh, axis=-1, keepdims=True)
        dh_ref[...] = rstd * (dnh - m1 - nh * m2)
        _accumulate(dg_ref, jnp.sum(dn * nh, axis=0, keepdims=True))
        _accumulate(db_ref, jnp.sum(dn, axis=0, keepdims=True))

    return pl.pallas_call(
        body, grid=(s // tm,), in_specs=[_rows(tm, d), _fix((1, d)), _fix((1, d)), _rows(tm, d)],
        out_specs=[_rows(tm, d), _fix((1, d)), _fix((1, d))],
        out_shape=[_sds((s, d), F32), _sds((1, d), F32), _sds((1, d), F32)],
        compiler_params=_params(("arbitrary",)), name="conformer_norm_swish_bwd")(hc, g, b, ds)


def _swap_halves(x):
    lane = lax.broadcasted_iota(jnp.int32, x.shape, 1)
    return jnp.where(lane < QK_ROPE // 2, pltpu.roll(x, 128 - QK_ROPE // 2, 1), pltpu.roll(x, QK_ROPE // 2, 1))


def _rope(x, cf, sf):
    return x * cf + _swap_halves(x) * sf


def _unrope(dx, cf, sf):
    return dx * cf - _swap_halves(dx) * sf


def _rms_rows(x, g):
    r = lax.rsqrt(jnp.mean(x * x, axis=-1, keepdims=True) + RMS_EPS)
    return x * r, r


def mla_latents(t, g_q, g_kv, cf, sf, tm=512):
    s = t.shape[0]
    tm = _tile(s, tm)

    def body(t_ref, gq_ref, gkv_ref, cf_ref, sf_ref, cq_ref, ckv_ref, kpe_ref):
        xq, _ = _rms_rows(t_ref[:, 0:Q_LORA], gq_ref[...])
        cq_ref[...] = (xq * gq_ref[...]).astype(BF16)
        xkv, _ = _rms_rows(t_ref[:, Q_LORA:Q_LORA + KV_LORA], gkv_ref[...])
        ckv_ref[...] = (xkv * gkv_ref[...]).astype(BF16)
        kpe_ref[...] = _rope(t_ref[:, Q_LORA + KV_LORA:], cf_ref[...], sf_ref[...]).astype(BF16)

    w = Q_LORA + KV_LORA + 128
    return pl.pallas_call(
        body, grid=(s // tm,),
        in_specs=[_rows(tm, w), _fix((1, Q_LORA)), _fix((1, KV_LORA)), _rows(tm, 128), _rows(tm, 128)],
        out_specs=[_rows(tm, Q_LORA), _rows(tm, KV_LORA), _rows(tm, 128)],
        out_shape=[_sds((s, Q_LORA), BF16), _sds((s, KV_LORA), BF16), _sds((s, 128), BF16)],
        compiler_params=_params(("parallel",)), name="mla_latents")(t, g_q, g_kv, cf, sf)


def mla_latents_bwd(t, g_q, g_kv, cf, sf, dcq, dckv, dkpe, tm=512):
    s = t.shape[0]
    tm = _tile(s, tm)
    w = Q_LORA + KV_LORA + 128

    def rms_bwd(x, g, dy):
        xh, r = _rms_rows(x, g)
        dxh = dy * g
        return r * (dxh - xh * jnp.mean(dxh * xh, axis=-1, keepdims=True)), jnp.sum(dy * xh, axis=0, keepdims=True)

    def body(t_ref, gq_ref, gkv_ref, cf_ref, sf_ref, dcq_ref, dckv_ref, dkpe_ref, dt_ref, dgq_ref, dgkv_ref):
        dxq, dgq = rms_bwd(t_ref[:, 0:Q_LORA], gq_ref[...], dcq_ref[...])
        dxkv, dgkv = rms_bwd(t_ref[:, Q_LORA:Q_LORA + KV_LORA], gkv_ref[...], dckv_ref[...])
        dt_ref[:, 0:Q_LORA] = dxq.astype(BF16)
        dt_ref[:, Q_LORA:Q_LORA + KV_LORA] = dxkv.astype(BF16)
        dt_ref[:, Q_LORA + KV_LORA:] = _unrope(dkpe_ref[...], cf_ref[...], sf_ref[...]).astype(BF16)
        _accumulate(dgq_ref, dgq)
        _accumulate(dgkv_ref, dgkv)

    return pl.pallas_call(
        body, grid=(s // tm,),
        in_specs=[_rows(tm, w), _fix((1, Q_LORA)), _fix((1, KV_LORA)), _rows(tm, 128), _rows(tm, 128),
                  _rows(tm, Q_LORA), _rows(tm, KV_LORA), _rows(tm, 128)],
        out_specs=[_rows(tm, w), _fix((1, Q_LORA)), _fix((1, KV_LORA))],
        out_shape=[_sds((s, w), BF16), _sds((1, Q_LORA), F32), _sds((1, KV_LORA), F32)],
        compiler_params=_params(("arbitrary",)), name="mla_latents_bwd")(t, g_q, g_kv, cf, sf, dcq, dckv, dkpe)


def mla_queries(cq, w_uq, cf, sf, tm=2048):
    s = cq.shape[0]
    tm = _tile(s, tm)

    def epi(acc, e, o):
        o[0][:, 0:QK_NOPE] = acc[:, 0:QK_NOPE].astype(BF16)
        o[0][:, QK_NOPE:] = _rope(acc[:, QK_NOPE:], e[0][...], e[1][...]).astype(BF16)

    return mm_nn("mla_queries", cq, w_uq, tm, HEAD_PAD, Q_LORA, epi, [_sds((s, N_HEADS * HEAD_PAD), BF16)],
                 [_ij(tm, HEAD_PAD)], [cf, sf], [_i0(tm, 128), _i0(tm, 128)])[0]


def mla_keys(ckv, w_uk, kpe, tm=2048):
    s = ckv.shape[0]
    tm = _tile(s, tm)

    def epi(acc, e, o):
        o[0][:, 0:QK_NOPE] = acc.astype(BF16)
        o[0][:, QK_NOPE:] = e[0][...]

    return mm_nn("mla_keys", ckv, w_uk, tm, QK_NOPE, KV_LORA, epi, [_sds((s, N_HEADS * HEAD_PAD), BF16)],
                 [_ij(tm, HEAD_PAD)], [kpe], [_i0(tm, 128)])[0]


def _masked_scores(q, k, tq, kv):
    sc = lax.dot_general(q, k, NT, preferred_element_type=F32) * ATTN_SCALE
    row = lax.broadcasted_iota(jnp.int32, (tq, tq), 0)
    col = lax.broadcasted_iota(jnp.int32, (tq, tq), 1)
    ok = lax.shift_right_logical(col, CHUNK_SHIFT) <= lax.shift_right_logical(row, CHUNK_SHIFT)
    own = jnp.where(ok, sc[:, kv - tq:], -1e30)
    return own if kv == tq else jnp.concatenate([sc[:, :kv - tq], own], axis=1)


def attention(q, k, v, tq=512):
    s = q.shape[0]
    tq = _tile(s, tq)
    nq = s // tq

    def body(q_ref, k_ref, v_ref, o_ref):
        for qi in range(nq):
            kv = (qi + 1) * tq
            sc = _masked_scores(q_ref[pl.ds(qi * tq, tq), :], k_ref[pl.ds(0, kv), :], tq, kv)
            p = jnp.exp(sc - jnp.max(sc, axis=-1, keepdims=True))
            o = lax.dot_general(p.astype(BF16), v_ref[pl.ds(0, kv), :], NN, preferred_element_type=F32)
            o_ref[pl.ds(qi * tq, tq), :] = (o / jnp.sum(p, axis=-1, keepdims=True)).astype(BF16)

    hq = pl.BlockSpec((s, HEAD_PAD), lambda h: (0, h))
    hv = pl.BlockSpec((s, V_HEAD), lambda h: (0, h))
    return pl.pallas_call(
        body, grid=(N_HEADS,), in_specs=[hq, hq, hv], out_specs=hv, out_shape=_sds((s, N_HEADS * V_HEAD), BF16),
        compiler_params=_params(("parallel",)), name="attention")(q, k, v)


def attention_bwd(q, k, v, do, tq=512):
    s = q.shape[0]
    tq = _tile(s, tq)
    nq = s // tq

    def body(q_ref, k_ref, v_ref, do_ref, dq_ref, dk_ref, dv_ref, dk_acc, dv_acc):
        dk_acc[...] = jnp.zeros_like(dk_acc)
        dv_acc[...] = jnp.zeros_like(dv_acc)
        for qi in range(nq):
            kv = (qi + 1) * tq
            qt = q_ref[pl.ds(qi * tq, tq), :]
            kt = k_ref[pl.ds(0, kv), :]
            dot = do_ref[pl.ds(qi * tq, tq), :]
            sc = _masked_scores(qt, kt, tq, kv)
            p = jnp.exp(sc - jnp.max(sc, axis=-1, keepdims=True))
            p = p / jnp.sum(p, axis=-1, keepdims=True)
            dp = lax.dot_general(dot, v_ref[pl.ds(0, kv), :], NT, preferred_element_type=F32)
            delta = jnp.sum(p * dp, axis=-1, keepdims=True)
            ds = (p * (dp - delta) * ATTN_SCALE).astype(BF16)
            dq_ref[pl.ds(qi * tq, tq), :] = lax.dot_general(ds, kt, NN, preferred_element_type=F32).astype(BF16)
            dk_acc[pl.ds(0, kv), :] += lax.dot_general(ds, qt, TN, preferred_element_type=F32)
            dv_acc[pl.ds(0, kv), :] += lax.dot_general(p.astype(BF16), dot, TN, preferred_element_type=F32)
        dk_ref[...] = dk_acc[...].astype(BF16)
        dv_ref[...] = dv_acc[...].astype(BF16)

    hq = pl.BlockSpec((s, HEAD_PAD), lambda h: (0, h))
    hv = pl.BlockSpec((s, V_HEAD), lambda h: (0, h))
    return pl.pallas_call(
        body, grid=(N_HEADS,), in_specs=[hq, hq, hv, hv], out_specs=[hq, hq, hv],
        out_shape=[_sds((s, N_HEADS * HEAD_PAD), BF16), _sds((s, N_HEADS * HEAD_PAD), BF16),
                   _sds((s, N_HEADS * V_HEAD), BF16)],
        scratch_shapes=[pltpu.VMEM((s, HEAD_PAD), F32), pltpu.VMEM((s, V_HEAD), F32)],
        compiler_params=_params(("parallel",)), name="attention_bwd")(q, k, v, do)


def mla_unrope_grads(dq, dk, cf, sf, tm=512):
    s = dq.shape[0]
    tm = _tile(s, tm)

    def body(dq_ref, dk_ref, cf_ref, sf_ref, dql_ref, dkn_ref, dkpe_ref):
        cfv, sfv = cf_ref[...], sf_ref[...]
        dkpe = jnp.zeros((tm, 128), F32)
        for h in range(N_HEADS):
            lo = h * HEAD_PAD
            dql_ref[:, lo:lo + QK_NOPE] = dq_ref[:, lo:lo + QK_NOPE]
            dql_ref[:, lo + QK_NOPE:lo + HEAD_PAD] = _unrope(
                dq_ref[:, lo + QK_NOPE:lo + HEAD_PAD].astype(F32), cfv, sfv).astype(BF16)
            dkn_ref[:, h * QK_NOPE:(h + 1) * QK_NOPE] = dk_ref[:, lo:lo + QK_NOPE]
            dkpe = dkpe + dk_ref[:, lo + QK_NOPE:lo + HEAD_PAD].astype(F32)
        dkpe_ref[...] = dkpe

    wq = N_HEADS * HEAD_PAD
    return pl.pallas_call(
        body, grid=(s // tm,), in_specs=[_rows(tm, wq), _rows(tm, wq), _rows(tm, 128), _rows(tm, 128)],
        out_specs=[_rows(tm, wq), _rows(tm, N_HEADS * QK_NOPE), _rows(tm, 128)],
        out_shape=[_sds((s, wq), BF16), _sds((s, N_HEADS * QK_NOPE), BF16), _sds((s, 128), F32)],
        compiler_params=_params(("parallel",)), name="mla_unrope_grads")(dq, dk, cf, sf)


ANY = pl.BlockSpec(memory_space=pl.ANY)
GATHER_ID = 1
CHIP_EXCHANGE_ID = 2
PAIR_ID = 3
ALL_ID = 4


def _nbytes(a):
    return a.size * a.dtype.itemsize


def _copy_cost(operand_bytes, sent_fraction):
    sent = int(operand_bytes * sent_fraction)
    return pl.CostEstimate(flops=0, transcendentals=0, bytes_accessed=2 * sent, remote_bytes_transferred=sent)


def _handshake(peers):
    barrier = pltpu.get_barrier_semaphore()
    for peer in peers:
        pl.semaphore_signal(barrier, inc=1, device_id=peer, device_id_type=MESH)
    pl.semaphore_wait(barrier, len(peers))


def _place():
    x, y, c = lax.axis_index("x"), lax.axis_index("y"), lax.axis_index("c")
    chips = [(1 - x, y), (x, 1 - y), (1 - x, 1 - y)]
    return x, y, c, chips


def _half(ref, hc, axis=0):
    n = ref.shape[axis] // 2
    idx = (slice(None),) * axis + (pl.ds(hc * n, n),)
    return ref.at[idx]


def gather_shards(name, tensors, by_columns=()):
    nt = len(tensors)

    def body(*refs):
        a, g = refs[:nt], refs[nt:2 * nt]
        send, recv = refs[2 * nt:]
        x, y, c, _ = _place()
        q = 2 * x + y
        sib, xn, yn = (x, y, 1 - c), (1 - x, y, c), (x, 1 - y, c)
        q_xn, q_yn, q_diag = 2 * (1 - x) + y, 2 * x + 1 - y, 2 * (1 - x) + 1 - y
        _handshake([sib, xn, yn])

        def whole(t, p):
            if t in by_columns:
                n = a[t].shape[1]
                return g[t].at[:, pl.ds(p * n, n)]
            return g[t].at[p]

        def part(t, p, hc, quarter=None):
            rows = a[t].shape[0]
            if quarter is None:
                return whole(t, p).at[pl.ds(hc * (rows // 2), rows // 2)]
            return whole(t, p).at[pl.ds(hc * (rows // 2) + quarter * (rows // 4), rows // 4)]

        def rc(t, k, src, dst, to):
            return pltpu.make_async_remote_copy(src_ref=src, dst_ref=dst, send_sem=send.at[t, k], recv_sem=recv.at[t, k],
                                                device_id=to, device_id_type=MESH)

        sent = []

        def go(cp):
            cp.start()
            sent.append(cp)

        def landed(t, k, piece, frm):
            rc(t, k, piece, piece, frm).wait_recv()
            return piece

        for t in range(nt):
            go(rc(t, 8, a[t], whole(t, q), sib))
            mine = _half(a[t], c)
            go(rc(t, 0, mine, part(t, q, c), xn))
            go(rc(t, 1, mine, part(t, q, c), yn))
        for t in range(nt):
            from_y = landed(t, 1, part(t, q_yn, c), yn)
            go(rc(t, 2, part(t, q_yn, c, 0), part(t, q_yn, c, 0), xn))
            go(rc(t, 5, from_y, from_y, sib))
            from_x = landed(t, 0, part(t, q_xn, c), xn)
            go(rc(t, 3, part(t, q_xn, c, 1), part(t, q_xn, c, 1), yn))
            go(rc(t, 4, from_x, from_x, sib))
        for t in range(nt):
            for k, frm in ((2, xn), (3, yn)):
                piece = landed(t, k, part(t, q_diag, c, k - 2), frm)
                go(rc(t, 4 + k, piece, piece, sib))
        for t in range(nt):
            landed(t, 4, part(t, q_xn, 1 - c), sib)
            landed(t, 5, part(t, q_yn, 1 - c), sib)
            landed(t, 6, part(t, q_diag, 1 - c, 0), sib)
            landed(t, 7, part(t, q_diag, 1 - c, 1), sib)
            landed(t, 8, whole(t, q), sib)
        for cp in sent:
            cp.wait_send()

    return pl.kernel(
        body, name=name,
        out_type=[_sds((a.shape[0], N_CHIPS * a.shape[1]) if t in by_columns else (N_CHIPS,) + a.shape, a.dtype)
                  for t, a in enumerate(tensors)],
        mesh=plsc.ScalarSubcoreMesh(axis_name="sequencer", num_cores=1),
        scratch_types=[pltpu.SemaphoreType.DMA((nt, 9)), pltpu.SemaphoreType.DMA((nt, 9))],
        cost_estimate=_copy_cost(sum(_nbytes(a) for a in tensors), 4),
        compiler_params=pltpu.CompilerParams(collective_id=GATHER_ID))(*tensors)


def pair_exchange(name, grads, on_sequencer):
    nt = len(grads)

    def body(*refs):
        g, theirs = refs[:nt], refs[nt:2 * nt]
        send, recv = refs[2 * nt:]
        x, y, c, _ = _place()
        if on_sequencer:
            _handshake([(x, y, 1 - c)])
        cps = []
        for t in range(nt):
            cp = pltpu.make_async_remote_copy(src_ref=_half(g[t], 1 - c, 1), dst_ref=theirs[t], send_sem=send.at[t],
                                              recv_sem=recv.at[t], device_id=(x, y, 1 - c), device_id_type=MESH)
            cp.start()
            cps.append(cp)
        for cp in cps:
            cp.wait()

    if not on_sequencer:
        return pl.pallas_call(
            body, in_specs=[ANY] * nt, out_specs=[ANY] * nt,
            out_shape=[_sds((N_CHIPS, a.shape[1] // 2, a.shape[2]), a.dtype) for a in grads],
            scratch_shapes=[pltpu.SemaphoreType.DMA((nt,)), pltpu.SemaphoreType.DMA((nt,))],
            name=name)(*grads)
    return pl.kernel(
        body, name=name, out_type=[_sds((N_CHIPS, a.shape[1] // 2, a.shape[2]), a.dtype) for a in grads],
        mesh=plsc.ScalarSubcoreMesh(axis_name="sequencer", num_cores=1),
        scratch_types=[pltpu.SemaphoreType.DMA((nt,)), pltpu.SemaphoreType.DMA((nt,))],
        cost_estimate=_copy_cost(sum(_nbytes(a) for a in grads), 0.5),
        compiler_params=pltpu.CompilerParams(collective_id=PAIR_ID))(*grads)


def chip_exchange(name, parts):
    nt = len(parts)

    def body(*refs):
        a, r = refs[:nt], refs[nt:2 * nt]
        send, recv = refs[2 * nt:]
        x, y, c, chips = _place()
        _handshake([(*chip, c) for chip in chips])
        cps = []
        for t in range(nt):
            for j, chip in enumerate(chips):
                cp = pltpu.make_async_remote_copy(
                    src_ref=a[t].at[2 * chip[0] + chip[1]], dst_ref=r[t].at[j], send_sem=send.at[t, j],
                    recv_sem=recv.at[t, j], device_id=(*chip, c), device_id_type=MESH)
                cp.start()
                cps.append(cp)
        for cp in cps:
            cp.wait()

    return pl.kernel(
        body, name=name, out_type=[_sds((N_CHIPS - 1,) + a.shape[1:], a.dtype) for a in parts],
        mesh=plsc.ScalarSubcoreMesh(axis_name="sequencer", num_cores=1),
        scratch_types=[pltpu.SemaphoreType.DMA((nt, 3)), pltpu.SemaphoreType.DMA((nt, 3))],
        cost_estimate=_copy_cost(sum(_nbytes(a) for a in parts), 0.75),
        compiler_params=pltpu.CompilerParams(collective_id=CHIP_EXCHANGE_ID))(*parts)


def pair_share(name, halves):
    nt = len(halves)

    def body(*refs):
        h, other = refs[:nt], refs[nt:2 * nt]
        send, recv = refs[2 * nt:]
        x, y, c, _ = _place()
        _handshake([(x, y, 1 - c)])
        cps = []
        for t in range(nt):
            cp = pltpu.make_async_remote_copy(src_ref=h[t], dst_ref=other[t], send_sem=send.at[t], recv_sem=recv.at[t],
                                              device_id=(x, y, 1 - c), device_id_type=MESH)
            cp.start()
            cps.append(cp)
        for cp in cps:
            cp.wait()

    return pl.kernel(
        body, name=name, out_type=[_sds(a.shape, a.dtype) for a in halves],
        mesh=plsc.ScalarSubcoreMesh(axis_name="sequencer", num_cores=1),
        scratch_types=[pltpu.SemaphoreType.DMA((nt,)), pltpu.SemaphoreType.DMA((nt,))],
        cost_estimate=_copy_cost(sum(_nbytes(a) for a in halves), 1),
        compiler_params=pltpu.CompilerParams(collective_id=PAIR_ID))(*halves)


def pack_rows(name, parts, rows):
    cdim = parts[0].shape[1]
    n = len(parts)
    vm = pl.BlockSpec(memory_space=pltpu.VMEM)

    def pack(*refs):
        p, o_ref = refs[:n], refs[n]
        at = 0
        for ref in p:
            o_ref[pl.ds(at, ref.shape[0]), :] = ref[...]
            at += ref.shape[0]
        o_ref[pl.ds(at, rows - at), :] = jnp.zeros((rows - at, cdim), F32)

    return pl.pallas_call(pack, in_specs=[vm] * n, out_specs=vm, out_shape=_sds((rows, cdim), F32), name=name)(*parts)


def all_reduce_small(parts, rows):
    cdim = parts[0].shape[1]
    vm = pl.BlockSpec(memory_space=pltpu.VMEM)
    mine = pack_rows("small_pack", parts, rows)

    def exchange(mine_ref, buf, send, recv, lsem):
        x, y, c, _ = _place()
        me = 4 * x + 2 * y + c
        peers = [(x ^ (k >> 2), y ^ ((k >> 1) & 1), c ^ (k & 1)) for k in range(1, 8)]
        _handshake(peers)
        own = pltpu.make_async_copy(mine_ref, buf.at[me], lsem)
        own.start()
        cps = []
        for k, to in enumerate(peers):
            cp = pltpu.make_async_remote_copy(src_ref=mine_ref, dst_ref=buf.at[me], send_sem=send.at[k], recv_sem=recv.at[k],
                                              device_id=to, device_id_type=MESH)
            cp.start()
            cps.append(cp)
        for k, (px, py, pc) in enumerate(peers):
            pltpu.make_async_remote_copy(src_ref=mine_ref, dst_ref=buf.at[4 * px + 2 * py + pc], send_sem=send.at[k],
                                         recv_sem=recv.at[k], device_id=(x, y, c), device_id_type=MESH).wait_recv()
        for cp in cps:
            cp.wait_send()
        own.wait()

    landed = pl.kernel(
        exchange, name="small_exchange", out_type=_sds((8, rows, cdim), F32),
        mesh=plsc.ScalarSubcoreMesh(axis_name="sequencer", num_cores=1),
        scratch_types=[pltpu.SemaphoreType.DMA((7,)), pltpu.SemaphoreType.DMA((7,)), pltpu.SemaphoreType.DMA],
        cost_estimate=_copy_cost(rows * cdim * 4, 7),
        compiler_params=pltpu.CompilerParams(collective_id=ALL_ID))(mine)

    def total(buf, o_ref):
        acc = buf[0]
        for d in range(1, 8):
            acc = acc + buf[d]
        o_ref[...] = acc

    return pl.pallas_call(total, in_specs=[vm], out_specs=vm, out_shape=_sds((rows, cdim), F32), name="small_sum")(landed)


def pair_sum(gs, theirs, core, tm=256):
    n = len(gs)
    _, r, c = gs[0].shape
    tm = _tile(r // 2, tm)
    nh = r // 2 // tm

    def body(core_ref, *refs):
        for a_ref, b_ref, o_ref in zip(refs[:n], refs[n:2 * n], refs[2 * n:]):
            o_ref[...] = (a_ref[...].astype(F32) + b_ref[...].astype(F32)).astype(BF16)

    blk = (N_CHIPS, tm, c)
    own = pl.BlockSpec(blk, lambda i, cr: (0, cr[0] * nh + i, 0))
    half = pl.BlockSpec(blk, lambda i, cr: (0, i, 0))
    return pl.pallas_call(
        body, grid_spec=pltpu.PrefetchScalarGridSpec(
            num_scalar_prefetch=1, grid=(nh,), in_specs=[own] * n + [half] * n, out_specs=[half] * n),
        out_shape=[_sds(t.shape, BF16) for t in theirs], compiler_params=_params(("parallel",)),
        name="pair_sum")(core, *gs, *theirs)


def chip_sum(own, landed, chip, stack, layer, layers, tm=256):
    _, r, c = own.shape
    tm = _tile(r, tm)

    def body(chip_ref, own_ref, l_ref, *rest):
        acc = own_ref[...].astype(F32)
        for j in range(N_CHIPS - 1):
            acc = acc + l_ref[j].astype(F32)
        rest[-1][...] = acc

    in_specs = [pl.BlockSpec((None, tm, c), lambda i, qr: (qr[0], i, 0)),
                pl.BlockSpec((N_CHIPS - 1, tm, c), lambda i, qr: (0, i, 0))]
    args = [chip, own, landed]
    if stack is not None:
        in_specs.append(ANY)
        args.append(stack)
    return pl.pallas_call(
        body, grid_spec=pltpu.PrefetchScalarGridSpec(
            num_scalar_prefetch=1, grid=(r // tm,), in_specs=in_specs,
            out_specs=pl.BlockSpec((None, tm, c), lambda i, qr: (layer, i, 0))),
        out_shape=_sds((layers, r, c), F32), input_output_aliases={3: 0} if stack is not None else {},
        compiler_params=_params(("parallel",)), name="chip_sum")(*args)


def _adamw_math(w, g, m, v):
    bc1 = 1.0 - ADAM_B1 ** ADAM_STEP
    bc2 = 1.0 - ADAM_B2 ** ADAM_STEP
    nm = ADAM_B1 * m + (1.0 - ADAM_B1) * g
    nv = ADAM_B2 * v + (1.0 - ADAM_B2) * (g * g)
    return -ADAM_LR * ((nm / bc1) / (jnp.sqrt(nv / bc2) + ADAM_EPS) + ADAM_WD * w), nm, nv


def vector_update(red, chip, ws, ms, vs, where):
    n = len(ws)
    dd = red.shape[1]

    def body(chip_ref, red_ref, *refs):
        w_r, m_r, v_r = refs[0:n], refs[n:2 * n], refs[2 * n:3 * n]
        g_o, d_o, m_o, v_o = (refs[(3 + k) * n:(4 + k) * n] for k in range(4))
        q = chip_ref[0]

        def chip_block(val, width):
            out = val[:, 0:width]
            for p in range(1, val.shape[1] // width):
                out = jnp.where(q == p, val[:, p * width:(p + 1) * width], out)
            return out

        for k in range(n):
            for idx, r0, nr, cols in where[k]:
                width = w_r[k].shape[-1]
                if cols == "chip" and width * N_CHIPS != dd:
                    g = chip_block(jnp.concatenate([red_ref[pl.ds(r0 + j, 1), :] for j in range(nr)], axis=1), width)
                else:
                    g = red_ref[pl.ds(r0, nr), :]
                    g = chip_block(g, width) if cols == "chip" else g if cols == "all" else g[:, 0:cols]
                delta, nm, nv = _adamw_math(w_r[k][idx], g, m_r[k][idx], v_r[k][idx])
                g_o[k][idx] = g
                d_o[k][idx] = delta
                m_o[k][idx] = nm
                v_o[k][idx] = nv

    vm = pl.BlockSpec(memory_space=pltpu.VMEM)
    outs = pl.pallas_call(
        body, in_specs=[pl.BlockSpec(memory_space=pltpu.SMEM), vm] + [vm] * (3 * n), out_specs=[vm] * (4 * n),
        out_shape=[_sds(w.shape, F32) for w in ws] * 4, name="vector_update")(chip, red, *ws, *ms, *vs)
    return [outs[k * n:(k + 1) * n] for k in range(4)]


def adamw_joined(w, m, v, g_mine, g_theirs, core, tm=512):
    nl, r, c = w.shape
    tm = _tile(r // 2, tm)
    nh = r // 2 // tm

    def body(core_ref, w_ref, m_ref, v_ref, gm_ref, gt_ref, g_ref, d_ref, nm_ref, nv_ref):
        mine = (pl.program_id(1) // nh) == core_ref[0]
        gv = jnp.where(mine, gm_ref[...], gt_ref[...])
        g_ref[...] = gv
        d_ref[...], nm_ref[...], nv_ref[...] = _adamw_math(w_ref[...], gv, m_ref[...], v_ref[...])

    full = pl.BlockSpec((None, tm, c), lambda l, i, cr: (l, i, 0))
    mine = pl.BlockSpec((None, tm, c), lambda l, i, cr: (l, jnp.where(i // nh == cr[0], i % nh, 0), 0))
    theirs = pl.BlockSpec((None, tm, c), lambda l, i, cr: (l, jnp.where(i // nh == cr[0], 0, i % nh), 0))
    return pl.pallas_call(
        body, grid_spec=pltpu.PrefetchScalarGridSpec(
            num_scalar_prefetch=1, grid=(nl, r // tm), in_specs=[full, full, full, mine, theirs], out_specs=[full] * 4),
        out_shape=[_sds((nl, r, c), F32)] * 4, compiler_params=_params(("parallel", "parallel")),
        name="adamw_joined")(core, w, m, v, g_mine, g_theirs)


WEIGHTS = ['sc_w_in', 'sc_conv_w', 'sc_w_out', 'mla_w_dq', 'mla_g_q', 'mla_w_uq', 'mla_w_dkv', 'mla_g_kv', 'mla_w_uk',
           'mla_w_uv', 'mla_w_o', 'cf_w_pw1', 'cf_b_pw1', 'cf_dw_w', 'cf_dw_b', 'cf_norm_g', 'cf_norm_b', 'cf_w_pw2',
           'cf_b_pw2', 'ff_w1', 'ff_w2', 'ln_mix_g', 'ln_mix_b', 'ln_ff_g', 'ln_ff_b']
ARGS = ['x'] + WEIGHTS + ['loss_target'] + ['m_' + n for n in WEIGHTS] + ['v_' + n for n in WEIGHTS]


def _sq_relu(h):
    r = jnp.maximum(h, jnp.zeros_like(h))
    return r * r


def _mlp_forward(i, x, xb, w1, w2, g, b):
    hb = mm_plain_nn(f"mlp{i}_up", xb, w1, BF16, tm=2048, tn=1024)
    y, yb, xh, rstd = mm_residual_ln(f"mlp{i}_down_ln", hb, w2, x, g, b, tk=4096, a_fn=_sq_relu)
    return (y, yb), dict(xb=xb, hb=hb, xh=xh, rstd=rstd, g=g)


def _mlp_backward(i, dr, drb, sv, w1, w2, dw1, dw2, reduce_after, mixer_ln):
    s = dr.shape[0]
    tm, tn = _tile(s, 1024), 1024

    def epi(acc, e, o):
        o[0][...] = (acc * (2.0 * jnp.maximum(e[0][...].astype(F32), 0.0))).astype(BF16)

    dhb = mm_nt(f"mlp{i}_down_bwd", drb, w2, s, tm, tn, 1024, epi, [_sds((s, w2.k), BF16)], [_ij(tm, tn)],
                [sv["hb"]], [_ij(tm, tn)])[0]
    g_w2 = mm_tn(f"mlp{i}_dw2", sv["hb"], drb, dw2, s, 1024, 1024, a_fn=_sq_relu)
    g_w1 = mm_tn(f"mlp{i}_dw1", sv["xb"], dhb, dw1, s, 1024, 1024)
    dhb = reduce_after(dhb, {f"w1_{i}": g_w1, f"w2_{i}": g_w2})
    return mm_nt_ln_backward(f"mlp{i}_up_bwd", dhb, w1, dr, *mixer_ln, tk=4096)


def kernel(x, sc_w_in, sc_conv_w, sc_w_out, mla_w_dq, mla_g_q, mla_w_uq, mla_w_dkv, mla_g_kv, mla_w_uk, mla_w_uv, mla_w_o, cf_w_pw1, cf_b_pw1, cf_dw_w, cf_dw_b, cf_norm_g, cf_norm_b, cf_w_pw2, cf_b_pw2, ff_w1, ff_w2, ln_mix_g, ln_mix_b, ln_ff_g, ln_ff_b, loss_target, m_sc_w_in, m_sc_conv_w, m_sc_w_out, m_mla_w_dq, m_mla_g_q, m_mla_w_uq, m_mla_w_dkv, m_mla_g_kv, m_mla_w_uk, m_mla_w_uv, m_mla_w_o, m_cf_w_pw1, m_cf_b_pw1, m_cf_dw_w, m_cf_dw_b, m_cf_norm_g, m_cf_norm_b, m_cf_w_pw2, m_cf_b_pw2, m_ff_w1, m_ff_w2, m_ln_mix_g, m_ln_mix_b, m_ln_ff_g, m_ln_ff_b, v_sc_w_in, v_sc_conv_w, v_sc_w_out, v_mla_w_dq, v_mla_g_q, v_mla_w_uq, v_mla_w_dkv, v_mla_g_kv, v_mla_w_uk, v_mla_w_uv, v_mla_w_o, v_cf_w_pw1, v_cf_b_pw1, v_cf_dw_w, v_cf_dw_b, v_cf_norm_g, v_cf_norm_b, v_cf_w_pw2, v_cf_b_pw2, v_ff_w1, v_ff_w2, v_ln_mix_g, v_ln_mix_b, v_ln_ff_g, v_ln_ff_b):
    given = dict(zip(ARGS, (x, sc_w_in, sc_conv_w, sc_w_out, mla_w_dq, mla_g_q, mla_w_uq, mla_w_dkv, mla_g_kv, mla_w_uk, mla_w_uv, mla_w_o, cf_w_pw1, cf_b_pw1, cf_dw_w, cf_dw_b, cf_norm_g, cf_norm_b, cf_w_pw2, cf_b_pw2, ff_w1, ff_w2, ln_mix_g, ln_mix_b, ln_ff_g, ln_ff_b, loss_target, m_sc_w_in, m_sc_conv_w, m_sc_w_out, m_mla_w_dq, m_mla_g_q, m_mla_w_uq, m_mla_w_dkv, m_mla_g_kv, m_mla_w_uk, m_mla_w_uv, m_mla_w_o, m_cf_w_pw1, m_cf_b_pw1, m_cf_dw_w, m_cf_dw_b, m_cf_norm_g, m_cf_norm_b, m_cf_w_pw2, m_cf_b_pw2, m_ff_w1, m_ff_w2, m_ln_mix_g, m_ln_mix_b, m_ln_ff_g, m_ln_ff_b, v_sc_w_in, v_sc_conv_w, v_sc_w_out, v_mla_w_dq, v_mla_g_q, v_mla_w_uq, v_mla_w_dkv, v_mla_g_kv, v_mla_w_uk, v_mla_w_uv, v_mla_w_o, v_cf_w_pw1, v_cf_b_pw1, v_cf_dw_w, v_cf_dw_b, v_cf_norm_g, v_cf_norm_b, v_cf_w_pw2, v_cf_b_pw2, v_ff_w1, v_ff_w2, v_ln_mix_g, v_ln_mix_b, v_ln_ff_g, v_ln_ff_b)))
    s, d = x.shape[1], x.shape[2]
    d_ff = 4 * d
    dq4 = d // N_CHIPS
    xq = lax.axis_index("x") * 2 + lax.axis_index("y")

    w_dkv_pad = jnp.pad(mla_w_dkv[0], ((0, 0), (0, 128 - QK_ROPE)))
    w_uq_pad = jnp.pad(mla_w_uq[0].reshape(Q_LORA, 2, QK_NOPE + QK_ROPE), ((0, 0), (0, 0), (0, HEAD_PAD - QK_NOPE - QK_ROPE)))
    small = pack_rows("vector_weights_pack", [
        sc_conv_w.reshape(2 * SC_WIDTH, dq4), cf_b_pw1.reshape(2, dq4), cf_dw_w[0], cf_dw_b, cf_norm_g, cf_norm_b,
        cf_b_pw2], 64)
    mlp_w = lambda i: [ff_w1[i].astype(BF16), ff_w2[i].astype(BF16)]
    g_in, g_out, g_w1, g_w2 = [None] * 2, [None] * 2, [None] * DEPTH, [None] * DEPTH
    g_in[0], g_out[0], g_small = gather_shards(
        "gather_mixer0", [sc_w_in[0].astype(BF16), sc_w_out[0].astype(BF16), small], by_columns=(0,))
    (g_w1[0],) = gather_shards("gather_up0", [ff_w1[0].astype(BF16)], by_columns=(0,))
    (g_w2[0],) = gather_shards("gather_down0", [ff_w2[0].astype(BF16)])
    g_dqkv, g_uq, g_uk, g_uv, g_o = gather_shards("gather_mixer1", [
        jnp.concatenate([mla_w_dq[0], w_dkv_pad], axis=1).astype(BF16),
        w_uq_pad.reshape(Q_LORA, 2 * HEAD_PAD).astype(BF16),
        mla_w_uk.reshape(KV_LORA // N_CHIPS, N_HEADS * QK_NOPE).astype(BF16),
        mla_w_uv.reshape(KV_LORA // N_CHIPS, N_HEADS * V_HEAD).astype(BF16), mla_w_o[0].astype(BF16)], by_columns=(1,))
    g_w1[1], g_w2[1] = gather_shards("gather_mlp1", mlp_w(1), by_columns=(0,))
    g_pw1, g_pw2, g_w1[2], g_w2[2] = gather_shards(
        "gather_layer2", [cf_w_pw1[0].astype(BF16), cf_w_pw2[0].astype(BF16)] + mlp_w(2), by_columns=(0, 2))
    g_in[1], g_out[1], g_w1[3], g_w2[3] = gather_shards(
        "gather_layer3", [sc_w_in[1].astype(BF16), sc_w_out[1].astype(BF16)] + mlp_w(3), by_columns=(0, 2))

    wd_t = Q_LORA + KV_LORA + 128
    w_in = [Stk("full", d, 3 * d, g_in[j]) for j in range(2)]
    w_out = [Stk("row", d, d, g_out[j]) for j in range(2)]
    w_dqkv = Stk("row", d, wd_t, g_dqkv)
    w_uq = Stk("full", Q_LORA, N_HEADS * HEAD_PAD, g_uq)
    w_uk = Stk("row", KV_LORA, N_HEADS * QK_NOPE, g_uk)
    w_uv = Stk("row", KV_LORA, N_HEADS * V_HEAD, g_uv)
    w_o = Stk("row", d, d, g_o)
    w_pw1 = Stk("full", d, 2 * d, g_pw1)
    w_pw2 = Stk("row", d, d, g_pw2)
    w_1 = [Stk("full", d, d_ff, g_w1[i]) for i in range(DEPTH)]
    w_2 = [Stk("row", d_ff, d, g_w2[i]) for i in range(DEPTH)]

    def wide(rows):
        return jnp.swapaxes(rows, 0, 1).reshape(rows.shape[1], d)

    conv_w = wide(g_small[:, 0:6]).reshape(2, SC_WIDTH, d)
    b_pw1 = g_small[:, 6:8].reshape(1, 2 * d)
    dw_w = wide(g_small[:, 8:39])
    dw_b, norm_g, norm_b, b_pw2 = (wide(g_small[:, 39 + k:40 + k]) for k in range(4))

    pos = jnp.arange(s, dtype=F32)
    inv_freq = ROPE_THETA ** (-jnp.arange(0, QK_ROPE, 2, dtype=F32) / QK_ROPE)
    ang = pos[:, None] * inv_freq[None, :]
    cos, sin, zero = jnp.cos(ang), jnp.sin(ang), jnp.zeros((s, 128 - QK_ROPE), F32)
    cf = jnp.concatenate([cos, cos, zero], axis=1)
    sf = jnp.concatenate([-sin, sin, zero], axis=1)

    def row(a, i):
        return a[i:i + 1]

    xs = x.reshape(s, d)
    cur = (xs, xs.astype(BF16))
    tape = []
    for i in range(DEPTH):
        mixer, j = i % 3, i // 3
        xf, xb = cur
        lg, lb = row(ln_mix_g, i), row(ln_mix_b, i)
        if mixer == 0:
            u = mm_plain_nn(f"sc{j}_in", xb, w_in[j], F32, tn=3 * dq4)
            gb = short_conv_gate(u, conv_w[j])
            y, yb, xh, rstd = mm_residual_ln(f"sc{j}_out_ln", gb, w_out[j], xf, lg, lb)
            sv = dict(xb=xb, u=u, gb=gb)
        elif mixer == 1:
            t = mm_plain_nn("mla_down", xb, w_dqkv, F32, tn=wd_t // 2)
            cq, ckv, kpe = mla_latents(t, mla_g_q, mla_g_kv, cf, sf)
            qh = mla_queries(cq, w_uq, cf, sf)
            kh = mla_keys(ckv, w_uk, kpe)
            vh = mm_plain_nn("mla_values", ckv, w_uv, BF16, tk=KV_LORA)
            oh = attention(qh, kh, vh)
            y, yb, xh, rstd = mm_residual_ln("mla_out_ln", oh, w_o, xf, lg, lb)
            sv = dict(xb=xb, t=t, cq=cq, ckv=ckv, qh=qh, kh=kh, vh=vh, oh=oh)
        else:
            u = mm_plain_nn("cf_pw1", xb, w_pw1, F32, bias=b_pw1)
            hc = conformer_glu_conv(u, dw_w, dw_b)
            sb = conformer_norm_swish(hc, norm_g, norm_b)
            y, yb, xh, rstd = mm_residual_ln("cf_pw2_ln", sb, w_pw2, xf, lg, lb, bias=b_pw2)
            sv = dict(xb=xb, u=u, hc=hc, sb=sb)
        sv.update(xh=xh, rstd=rstd, g=lg)
        cur, sv_mlp = _mlp_forward(i, y, yb, w_1[i], w_2[i], row(ln_ff_g, i), row(ln_ff_b, i))
        tape.append((sv, sv_mlp))

    g_ln = {n: [None] * DEPTH for n in ("ln_mix_g", "ln_mix_b", "ln_ff_g", "ln_ff_b")}
    last = tape[DEPTH - 1][1]
    dr, drb, g_ln["ln_ff_g"][DEPTH - 1], g_ln["ln_ff_b"][DEPTH - 1], _, loss_part = loss_ln_backward(
        cur[0], loss_target.reshape(s, d), last["xh"], last["rstd"], last["g"])

    grads = {}
    smalls = {}
    conv_grads = [None, None]
    core = lax.axis_index("c").astype(jnp.int32).reshape(1)
    chip = xq.astype(jnp.int32).reshape(1)
    pairs, landed = {}, {}
    ready, theirs = [], {}

    def hold(xs, others):
        live = [x for x in xs if x is not None]
        out = lax.optimization_barrier((*live, *others))
        rest = iter(out[:len(live)])
        return tuple(None if x is None else next(rest) for x in xs), list(out[len(live):])

    def reduce_after(x, new, early=False):
        out = lax.optimization_barrier((x, *new.values()))
        grads.update(zip(new, out[1:]))
        if early:
            theirs.update(zip(new, pair_exchange(f"pair_exchange_{len(theirs)}", list(out[1:]), True)))
        ready.extend(new)
        return out[0]

    def reduce_layer(i, x):
        late = [n for n in ready if n not in theirs]
        if late:
            theirs.update(zip(late, pair_exchange(f"pair_exchange_layer{i}", [grads[n] for n in late], False)))
        by_shape = {}
        for n in ready:
            by_shape.setdefault(grads[n].shape, []).append(n)
        for names in by_shape.values():
            pairs.update(zip(names, pair_sum([grads[n] for n in names], [theirs[n] for n in names], core)))
        sums = [pairs[n] for n in ready]
        landed.update(zip(ready, chip_exchange(f"chip_exchange_layer{i}", sums)))
        exchanged.append(list(ready))
        ready.clear()
        return hold(x, sums)[0]

    groups = [["in_0", "in_1"], ["out_0", "out_1"], ["dqkv"], ["uq"], ["uk"], ["uv"], ["o"], ["pw1"], ["pw2"],
              [f"w1_{i}" for i in range(DEPTH)], [f"w2_{i}" for i in range(DEPTH)]]
    stacks = [None] * len(groups)
    exchanged = []

    def sum_layer(x, last=False):
        names = exchanged.pop(0)
        if last:
            x, held = hold(x, [landed[n] for n in names])
            landed.update(zip(names, held))
        new = []
        for n in names:
            k = next(k for k, members in enumerate(groups) if n in members)
            stacks[k] = chip_sum(pairs[n], landed[n], chip, stacks[k], groups[k].index(n), len(groups[k]))
            new.append(stacks[k])
        return x if last else hold(x, new)[0]

    for i in reversed(range(DEPTH)):
        mixer, j = i % 3, i // 3
        sv, sv_mlp = tape[i]
        dr, drb, g_ln["ln_mix_g"][i], g_ln["ln_mix_b"][i], dr_sum = _mlp_backward(
            i, dr, drb, sv_mlp, w_1[i], w_2[i], Stk("col", d, d_ff), Stk("row", d_ff, d),
            lambda x_, new: reduce_after(x_, new, early=i > 0), (sv["xh"], sv["rstd"], sv["g"]))
        if i == 0:
            dr, drb = reduce_layer("0_mlp", (dr, drb))

        def to_input(name, a, w, tk, in_parts=False):
            side_by_side = {}
            if in_parts:
                nparts = a.shape[0]
                side_by_side = dict(
                    a_spec_fn=lambda tm, tk_: pl.BlockSpec((nparts, tm, d), lambda i_, j_, k_: (0, i_, 0)),
                    a_fn=lambda blk: jnp.concatenate([blk[p] for p in range(nparts)], axis=1))
                tk = w.n
            if i == 0:
                if side_by_side:
                    side_by_side["a_spec_fn"] = (s, side_by_side["a_spec_fn"])
                return mm_plain_nt(name, a, w, F32, tm=512, tn=1024, tk=tk, add=dr, add_scale=ALPHA, **side_by_side), None
            prev = tape[i - 1][1]
            out = mm_nt_ln_backward(name, a, w, dr, prev["xh"], prev["rstd"], prev["g"], tk=tk, **side_by_side)
            g_ln["ln_ff_g"][i - 1], g_ln["ln_ff_b"][i - 1] = out[2], out[3]
            return out[0], out[1]

        if mixer == 0:
            dgate = mm_plain_nt(f"sc{j}_out_bwd", drb, w_out[j], F32)
            dw_out = mm_tn(f"sc{j}_dw_out", sv["gb"], drb, Stk("row", d, d), s, 512, 1024)
            du, conv_grads[j] = short_conv_gate_bwd(sv["u"], conv_w[j], dgate)
            nb = d // 256
            dw_in = mm_tn(
                f"sc{j}_dw_in", sv["xb"], du, Stk("col", d, 3 * d), s, 1024, 256,
                b_spec=pl.BlockSpec((None, s, 256), lambda i_, j_, k_: (j_ // nb, k_, j_ % nb)))
            du = reduce_after(du, {f"in_{j}": dw_in, f"out_{j}": dw_out})
            dr, drb = to_input(f"sc{j}_in_bwd", du, w_in[j], d, in_parts=True)
        elif mixer == 1:
            do = mm_plain_nt("mla_out_bwd", drb, w_o, BF16)
            g_o = mm_tn("mla_dw_o", sv["oh"], drb, Stk("row", d, d), s, 512, 1024)
            dqh, dkh, dvh = attention_bwd(sv["qh"], sv["kh"], sv["vh"], do)
            dql, dkn, dkpe = mla_unrope_grads(dqh, dkh, cf, sf)
            g_uq = mm_tn("mla_dw_uq", sv["cq"], dql, Stk("col", Q_LORA, N_HEADS * HEAD_PAD), s, Q_LORA, 512)
            dcq = mm_plain_nt("mla_uq_bwd", dql, w_uq, F32, tn=Q_LORA)
            g_uk = mm_tn("mla_dw_uk", sv["ckv"], dkn, Stk("row", KV_LORA, N_HEADS * QK_NOPE), s, KV_LORA, 1024)
            g_uv = mm_tn("mla_dw_uv", sv["ckv"], dvh, Stk("row", KV_LORA, N_HEADS * V_HEAD), s, KV_LORA, 1024)
            dckv = mm_plain_nt("mla_uk_bwd", dkn, w_uk, F32, tn=KV_LORA)
            dckv = mm_plain_nt("mla_uv_bwd", dvh, w_uv, F32, tn=KV_LORA, add=dckv)
            dt, smalls["g_q"], smalls["g_kv"] = mla_latents_bwd(sv["t"], mla_g_q, mla_g_kv, cf, sf, dcq, dckv, dkpe)
            g_dqkv = mm_tn("mla_dw_down", sv["xb"], dt, Stk("row", d, wd_t), s, 512, wd_t)
            dt = reduce_after(dt, {"dqkv": g_dqkv, "uq": g_uq, "uk": g_uk, "uv": g_uv, "o": g_o})
            dr, drb = to_input("mla_down_bwd", dt, w_dqkv, wd_t)
        else:
            dsw = mm_plain_nt("cf_pw2_bwd", drb, w_pw2, F32)
            g_pw2 = mm_tn("cf_dw_pw2", sv["sb"], drb, Stk("row", d, d), s, 512, 1024)
            smalls["b_pw2"] = dr_sum
            dhc, smalls["norm_g"], smalls["norm_b"] = conformer_norm_swish_bwd(sv["hc"], norm_g, norm_b, dsw)
            du, smalls["b_pw1"], smalls["dw_w"], smalls["dw_b"] = conformer_glu_conv_bwd(sv["u"], dw_w, dhc)
            nb = d // 512
            g_pw1 = mm_tn(
                "cf_dw_pw1", sv["xb"], du, Stk("col", d, 2 * d), s, 1024, 512,
                b_spec=pl.BlockSpec((None, s, 512), lambda i_, j_, k_: (j_ // nb, k_, j_ % nb)))
            du = reduce_after(du, {"pw1": g_pw1, "pw2": g_pw2})
            dr, drb = to_input("cf_pw1_bwd", du, w_pw1, d, in_parts=True)
        if i < DEPTH - 1:
            dr, drb = sum_layer((dr, drb))
        dr, drb = reduce_layer(i, (dr, drb))
    grad_x = sum_layer(sum_layer((dr, None), last=True), last=True)[0].reshape(1, s, d)

    mine = stacks
    other = (pair_share("pair_share_mixers", mine[:9]) + pair_share("pair_share_up", mine[9:10])
             + pair_share("pair_share_down", mine[10:]))

    def padded(get):
        dqkv = jnp.concatenate([get("mla_w_dq")[0], jnp.pad(get("mla_w_dkv")[0], ((0, 0), (0, 128 - QK_ROPE)))], axis=1)
        uq = jnp.pad(get("mla_w_uq")[0].reshape(Q_LORA, 2, QK_NOPE + QK_ROPE),
                     ((0, 0), (0, 0), (0, HEAD_PAD - QK_NOPE - QK_ROPE))).reshape(Q_LORA, 2 * HEAD_PAD)
        return [get("sc_w_in"), get("sc_w_out"), dqkv[None], uq[None],
                get("mla_w_uk").reshape(1, KV_LORA // N_CHIPS, d), get("mla_w_uv").reshape(1, KV_LORA // N_CHIPS, d),
                get("mla_w_o"), get("cf_w_pw1"), get("cf_w_pw2"), get("ff_w1"), get("ff_w2")]

    w_l, m_l, v_l = (padded(lambda n, p=p: given[p + n]) for p in ("", "m_", "v_"))
    res = [adamw_joined(w_l[k], m_l[k], v_l[k], mine[k], other[k], core) for k in range(len(groups))]

    def unpadded(k):
        r_in, r_out, r_dqkv, r_uq, r_uk, r_uv, r_o, r_pw1, r_pw2, r_w1, r_w2 = (r[k] for r in res)
        return {
            "sc_w_in": r_in, "sc_w_out": r_out, "mla_w_dq": r_dqkv[:, :, 0:Q_LORA],
            "mla_w_dkv": r_dqkv[:, :, Q_LORA:Q_LORA + KV_LORA + QK_ROPE],
            "mla_w_uq": r_uq.reshape(1, Q_LORA, 2, HEAD_PAD)[:, :, :, 0:QK_NOPE + QK_ROPE].reshape(mla_w_uq.shape),
            "mla_w_uk": r_uk.reshape(mla_w_uk.shape), "mla_w_uv": r_uv.reshape(mla_w_uv.shape),
            "mla_w_o": r_o, "cf_w_pw1": r_pw1, "cf_w_pw2": r_pw2, "ff_w1": r_w1, "ff_w2": r_w2}

    big_g, big_d, big_m, big_v = (unpadded(k) for k in range(4))

    pad_row = lambda a: jnp.pad(a, ((0, 0), (0, d - a.shape[1])))
    small_parts = ([g for n in ("ln_mix_g", "ln_mix_b", "ln_ff_g", "ln_ff_b") for g in g_ln[n]]
                   + [pad_row(smalls["g_q"]), pad_row(smalls["g_kv"]), conv_grads[0], conv_grads[1],
                      smalls["b_pw1"].reshape(2, d), smalls["dw_w"], smalls["dw_b"], smalls["norm_g"], smalls["norm_b"],
                      smalls["b_pw2"], loss_part])
    red = all_reduce_small(small_parts, 64)
    loss = red[61, 0]

    where = {
        "ln_mix_g": [((), 0, DEPTH, "all")], "ln_mix_b": [((), 4, DEPTH, "all")],
        "ln_ff_g": [((), 8, DEPTH, "all")], "ln_ff_b": [((), 12, DEPTH, "all")],
        "mla_g_q": [((), 16, 1, Q_LORA)], "mla_g_kv": [((), 17, 1, KV_LORA)],
        "sc_conv_w": [((0,), 18, SC_WIDTH, "chip"), ((1,), 21, SC_WIDTH, "chip")],
        "cf_b_pw1": [((), 24, 2, "chip")], "cf_dw_w": [((0,), 26, CONF_WIDTH, "chip")],
        "cf_dw_b": [((), 57, 1, "chip")], "cf_norm_g": [((), 58, 1, "chip")], "cf_norm_b": [((), 59, 1, "chip")],
        "cf_b_pw2": [((), 60, 1, "chip")]}
    vec = list(where)
    vec_res = vector_update(red, chip, [given[n] for n in vec], [given["m_" + n] for n in vec],
                            [given["v_" + n] for n in vec], [where[n] for n in vec])
    gw = dict(big_g)
    upd = {n: [big_d[n], big_m[n], big_v[n]] for n in big_g}
    for k, n in enumerate(vec):
        gw[n] = vec_res[0][k]
        upd[n] = [vec_res[1][k], vec_res[2][k], vec_res[3][k]]

    return (loss, grad_x, *[gw[n] for n in WEIGHTS], *[upd[n][0] for n in WEIGHTS],
            *[upd[n][1] for n in WEIGHTS], *[upd[n][2] for n in WEIGHTS])
```
